```python
import jax, jax.numpy as jnp
from jax import lax
import numpy as np

D_MODEL = 1024
BATCH = 16
SEQ = 2048
DEPTH = 1

HEAD_DIM = D_MODEL // 16
N_ATTN_HEADS = 8
N_KV_HEADS = 2
GQA_GROUP = N_ATTN_HEADS // N_KV_HEADS
WINDOW = 128
BLOCK = 128
N_CONV_GROUPS = 4
CONV_WIDTH = N_CONV_GROUPS * HEAD_DIM
CONV_K = 3
N_MEM_HEADS = 4
N_MEM = 256
ATTN_WIDTH = N_ATTN_HEADS * HEAD_DIM
KV_WIDTH = N_KV_HEADS * HEAD_DIM
MEM_WIDTH = N_MEM_HEADS * HEAD_DIM
MIX_WIDTH = ATTN_WIDTH + CONV_WIDTH + MEM_WIDTH
IN_PROJ_WIDTH = ATTN_WIDTH + 2 * KV_WIDTH + 3 * CONV_WIDTH + MEM_WIDTH
SPLIT_POINTS = (
    ATTN_WIDTH,
    ATTN_WIDTH + KV_WIDTH,
    ATTN_WIDTH + 2 * KV_WIDTH,
    ATTN_WIDTH + 2 * KV_WIDTH + CONV_WIDTH,
    ATTN_WIDTH + 2 * KV_WIDTH + 2 * CONV_WIDTH,
    ATTN_WIDTH + 2 * KV_WIDTH + 3 * CONV_WIDTH,
)
D_FF = ((8 * D_MODEL // 3 + 255) // 256) * 256
EPS = 1e-6
NEG_INF = -1e30

kernel_name = "hymba_conv_swa_memory_hybrid"


def rms_norm(x, g):
    xf = x.astype(jnp.float32)
    y = xf * lax.rsqrt(jnp.mean(xf * xf, axis=-1, keepdims=True) + EPS)
    return (y * g.astype(jnp.float32)).astype(x.dtype)


def alibi_slopes():
    return jnp.asarray(2.0 ** (-8.0 * np.arange(1, N_ATTN_HEADS + 1) / N_ATTN_HEADS), dtype=jnp.float32)


def sliding_window_attention(q, k, v, sinks):
    B, S = q.shape[0], q.shape[1]
    nb = S // BLOCK
    qb = q.reshape(B, nb, BLOCK, N_KV_HEADS, GQA_GROUP, HEAD_DIM)

    def band(t):
        tb = t.reshape(B, nb, BLOCK, N_KV_HEADS, HEAD_DIM)
        prev = jnp.pad(tb[:, :-1], ((0, 0), (1, 0), (0, 0), (0, 0), (0, 0)))
        return jnp.concatenate([prev, tb], axis=2)

    kb, vb = band(k), band(v)
    scores = jnp.einsum('bnqhgd,bnkhd->bnhgqk', qb, kb).astype(jnp.float32) * (HEAD_DIM ** -0.5)
    q_idx = jnp.arange(BLOCK)[:, None]
    k_idx = jnp.arange(2 * BLOCK)[None, :]
    dist = q_idx + BLOCK - k_idx
    key_pos = jnp.arange(nb)[:, None] * BLOCK - BLOCK + jnp.arange(2 * BLOCK)[None, :]
    valid = ((dist >= 0) & (dist < WINDOW))[None] & (key_pos >= 0)[:, None, :]
    slopes = alibi_slopes().reshape(N_KV_HEADS, GQA_GROUP)
    bias = -slopes[:, :, None, None] * dist.astype(jnp.float32)
    scores = jnp.where(valid[None, :, None, None], scores + bias[None, None], NEG_INF)
    sink = jnp.broadcast_to(sinks.astype(jnp.float32).reshape(1, 1, N_KV_HEADS, GQA_GROUP, 1, 1),
                            scores.shape[:-1] + (1,))
    probs = jax.nn.softmax(jnp.concatenate([scores, sink], axis=-1), axis=-1)[..., :-1]
    out = jnp.einsum('bnhgqk,bnkhd->bnqhgd', probs.astype(v.dtype), vb)
    return out.reshape(B, S, ATTN_WIDTH)


def short_gated_conv(h, b_gate, c_gate, conv_w, conv_b):
    S = h.shape[1]
    u = c_gate * h
    u_pad = jnp.pad(u, ((0, 0), (CONV_K - 1, 0), (0, 0)))
    conv = sum(conv_w[j] * u_pad[:, j:j + S] for j in range(CONV_K)) + conv_b
    return b_gate * conv


def memory_cross_attention(q, mem_n, w_mem_kv, mem_q_norm, mem_k_norm):
    B, S = q.shape[0], q.shape[1]
    q = rms_norm(q.reshape(B, S, N_MEM_HEADS, HEAD_DIM), mem_q_norm)
    kv = mem_n @ w_mem_kv
    k, v = jnp.split(kv, 2, axis=-1)
    k = rms_norm(k.reshape(B, -1, N_MEM_HEADS, HEAD_DIM), mem_k_norm)
    v = v.reshape(B, -1, N_MEM_HEADS, HEAD_DIM)
    scores = jnp.einsum('bshd,bmhd->bhsm', q, k).astype(jnp.float32) * (HEAD_DIM ** -0.5)
    probs = jax.nn.softmax(scores, axis=-1)
    out = jnp.einsum('bhsm,bmhd->bshd', probs.astype(v.dtype), v)
    return out.reshape(B, S, MEM_WIDTH)


def hybrid_mixer(xn, mem_n, w_in, q_norm, k_norm, attn_sinks, conv_w, conv_b, w_mem_kv,
                 mem_q_norm, mem_k_norm, out_norm_attn, out_norm_conv, out_norm_mem, w_out):
    B, S = xn.shape[0], xn.shape[1]
    proj = xn @ w_in
    q_a, k_a, v_a, c_h, c_b, c_c, q_m = jnp.split(proj, SPLIT_POINTS, axis=-1)
    q_a = rms_norm(q_a.reshape(B, S, N_ATTN_HEADS, HEAD_DIM), q_norm)
    k_a = rms_norm(k_a.reshape(B, S, N_KV_HEADS, HEAD_DIM), k_norm)
    v_a = v_a.reshape(B, S, N_KV_HEADS, HEAD_DIM)
    attn_out = sliding_window_attention(q_a, k_a, v_a, attn_sinks)
    conv_out = short_gated_conv(c_h, c_b, c_c, conv_w, conv_b)
    mem_out = memory_cross_attention(q_m, mem_n, w_mem_kv, mem_q_norm, mem_k_norm)
    merged = jnp.concatenate([rms_norm(attn_out, out_norm_attn),
                              rms_norm(conv_out, out_norm_conv),
                              rms_norm(mem_out, out_norm_mem)], axis=-1)
    return merged @ w_out


def swiglu_ffn(xn, w_gate, w_up, w_down):
    return (jax.nn.silu(xn @ w_gate) * (xn @ w_up)) @ w_down


def _fwd_setup_inputs(seed: int = 0) -> dict:
    key = jax.random.key(seed)
    ks = jax.random.split(key, 24)
    f32 = jnp.float32

    def normal(k, shape, scale):
        return jax.random.normal(k, shape, f32) * scale

    def gain(k, shape):
        return 1.0 + 0.05 * jax.random.normal(k, shape, f32)

    L = DEPTH
    return {
        "x": jax.random.normal(ks[0], (BATCH, SEQ, D_MODEL), f32),
        "mem": jax.random.normal(ks[1], (BATCH, N_MEM, D_MODEL), f32),
        "norm_mix": gain(ks[2], (L, D_MODEL)),
        "w_in": normal(ks[3], (L, D_MODEL, IN_PROJ_WIDTH), D_MODEL ** -0.5),
        "q_norm": gain(ks[4], (L, HEAD_DIM)),
        "k_norm": gain(ks[5], (L, HEAD_DIM)),
        "attn_sinks": normal(ks[6], (L, N_ATTN_HEADS), 0.5),
        "conv_w": normal(ks[7], (L, CONV_K, CONV_WIDTH), CONV_K ** -0.5),
        "conv_b": normal(ks[8], (L, CONV_WIDTH), 0.01),
        "norm_mem": gain(ks[9], (L, D_MODEL)),
        "w_mem_kv": normal(ks[10], (L, D_MODEL, 2 * MEM_WIDTH), D_MODEL ** -0.5),
        "mem_q_norm": gain(ks[11], (L, HEAD_DIM)),
        "mem_k_norm": gain(ks[12], (L, HEAD_DIM)),
        "out_norm_attn": gain(ks[13], (L, ATTN_WIDTH)),
        "out_norm_conv": gain(ks[14], (L, CONV_WIDTH)),
        "out_norm_mem": gain(ks[15], (L, MEM_WIDTH)),
        "w_out": normal(ks[16], (L, MIX_WIDTH, D_MODEL), MIX_WIDTH ** -0.5),
        "norm_ffn": gain(ks[17], (L, D_MODEL)),
        "w_gate": normal(ks[18], (L, D_MODEL, D_FF), D_MODEL ** -0.5),
        "w_up": normal(ks[19], (L, D_MODEL, D_FF), D_MODEL ** -0.5),
        "w_down": normal(ks[20], (L, D_FF, D_MODEL), D_FF ** -0.5),
    }


def _fwd_reference(x, mem, norm_mix, w_in, q_norm, k_norm, attn_sinks, conv_w, conv_b, norm_mem,
              w_mem_kv, mem_q_norm, mem_k_norm, out_norm_attn, out_norm_conv, out_norm_mem,
              w_out, norm_ffn, w_gate, w_up, w_down):
    for l in range(DEPTH):
        xn = rms_norm(x, norm_mix[l])
        mem_n = rms_norm(mem, norm_mem[l])
        x = x + hybrid_mixer(xn, mem_n, w_in[l], q_norm[l], k_norm[l], attn_sinks[l], conv_w[l],
                             conv_b[l], w_mem_kv[l], mem_q_norm[l], mem_k_norm[l],
                             out_norm_attn[l], out_norm_conv[l], out_norm_mem[l], w_out[l])
        x = x + swiglu_ffn(rms_norm(x, norm_ffn[l]), w_gate[l], w_up[l], w_down[l])
    return x


import jax as _jax
import jax.numpy as _jnp

TWIN_FORMAT = 'train_step'
FWD_PARAMS = ['x', 'mem', 'norm_mix', 'w_in', 'q_norm', 'k_norm', 'attn_sinks', 'conv_w', 'conv_b', 'norm_mem', 'w_mem_kv', 'mem_q_norm', 'mem_k_norm', 'out_norm_attn', 'out_norm_conv', 'out_norm_mem', 'w_out', 'norm_ffn', 'w_gate', 'w_up', 'w_down']
TWIN_WEIGHTS = ['norm_mix', 'w_in', 'q_norm', 'k_norm', 'attn_sinks', 'conv_w', 'conv_b', 'norm_mem', 'w_mem_kv', 'mem_q_norm', 'mem_k_norm', 'out_norm_attn', 'out_norm_conv', 'out_norm_mem', 'w_out', 'norm_ffn', 'w_gate', 'w_up', 'w_down']
TWIN_DIFF_INPUT = 'x'
TWIN_INPUTS = ['x', 'mem', 'norm_mix', 'w_in', 'q_norm', 'k_norm', 'attn_sinks', 'conv_w', 'conv_b', 'norm_mem', 'w_mem_kv', 'mem_q_norm', 'mem_k_norm', 'out_norm_attn', 'out_norm_conv', 'out_norm_mem', 'w_out', 'norm_ffn', 'w_gate', 'w_up', 'w_down', 'loss_target', 'm_norm_mix', 'm_w_in', 'm_q_norm', 'm_k_norm', 'm_attn_sinks', 'm_conv_w', 'm_conv_b', 'm_norm_mem', 'm_w_mem_kv', 'm_mem_q_norm', 'm_mem_k_norm', 'm_out_norm_attn', 'm_out_norm_conv', 'm_out_norm_mem', 'm_w_out', 'm_norm_ffn', 'm_w_gate', 'm_w_up', 'm_w_down', 'v_norm_mix', 'v_w_in', 'v_q_norm', 'v_k_norm', 'v_attn_sinks', 'v_conv_w', 'v_conv_b', 'v_norm_mem', 'v_w_mem_kv', 'v_mem_q_norm', 'v_mem_k_norm', 'v_out_norm_attn', 'v_out_norm_conv', 'v_out_norm_mem', 'v_w_out', 'v_norm_ffn', 'v_w_gate', 'v_w_up', 'v_w_down']
TWIN_OUTPUTS = ['loss', 'grad_x', 'grad_norm_mix', 'grad_w_in', 'grad_q_norm', 'grad_k_norm', 'grad_attn_sinks', 'grad_conv_w', 'grad_conv_b', 'grad_norm_mem', 'grad_w_mem_kv', 'grad_mem_q_norm', 'grad_mem_k_norm', 'grad_out_norm_attn', 'grad_out_norm_conv', 'grad_out_norm_mem', 'grad_w_out', 'grad_norm_ffn', 'grad_w_gate', 'grad_w_up', 'grad_w_down', 'delta_norm_mix', 'delta_w_in', 'delta_q_norm', 'delta_k_norm', 'delta_attn_sinks', 'delta_conv_w', 'delta_conv_b', 'delta_norm_mem', 'delta_w_mem_kv', 'delta_mem_q_norm', 'delta_mem_k_norm', 'delta_out_norm_attn', 'delta_out_norm_conv', 'delta_out_norm_mem', 'delta_w_out', 'delta_norm_ffn', 'delta_w_gate', 'delta_w_up', 'delta_w_down', 'new_m_norm_mix', 'new_m_w_in', 'new_m_q_norm', 'new_m_k_norm', 'new_m_attn_sinks', 'new_m_conv_w', 'new_m_conv_b', 'new_m_norm_mem', 'new_m_w_mem_kv', 'new_m_mem_q_norm', 'new_m_mem_k_norm', 'new_m_out_norm_attn', 'new_m_out_norm_conv', 'new_m_out_norm_mem', 'new_m_w_out', 'new_m_norm_ffn', 'new_m_w_gate', 'new_m_w_up', 'new_m_w_down', 'new_v_norm_mix', 'new_v_w_in', 'new_v_q_norm', 'new_v_k_norm', 'new_v_attn_sinks', 'new_v_conv_w', 'new_v_conv_b', 'new_v_norm_mem', 'new_v_w_mem_kv', 'new_v_mem_q_norm', 'new_v_mem_k_norm', 'new_v_out_norm_attn', 'new_v_out_norm_conv', 'new_v_out_norm_mem', 'new_v_w_out', 'new_v_norm_ffn', 'new_v_w_gate', 'new_v_w_up', 'new_v_w_down']
TWIN_LEAF_KINDS = {'loss': 'loss', 'grad_x': 'grad_x', 'grad_norm_mix': 'grad_w', 'grad_w_in': 'grad_w', 'grad_q_norm': 'grad_w', 'grad_k_norm': 'grad_w', 'grad_attn_sinks': 'grad_w', 'grad_conv_w': 'grad_w', 'grad_conv_b': 'grad_w', 'grad_norm_mem': 'grad_w', 'grad_w_mem_kv': 'grad_w', 'grad_mem_q_norm': 'grad_w', 'grad_mem_k_norm': 'grad_w', 'grad_out_norm_attn': 'grad_w', 'grad_out_norm_conv': 'grad_w', 'grad_out_norm_mem': 'grad_w', 'grad_w_out': 'grad_w', 'grad_norm_ffn': 'grad_w', 'grad_w_gate': 'grad_w', 'grad_w_up': 'grad_w', 'grad_w_down': 'grad_w', 'delta_norm_mix': 'delta_w', 'delta_w_in': 'delta_w', 'delta_q_norm': 'delta_w', 'delta_k_norm': 'delta_w', 'delta_attn_sinks': 'delta_w', 'delta_conv_w': 'delta_w', 'delta_conv_b': 'delta_w', 'delta_norm_mem': 'delta_w', 'delta_w_mem_kv': 'delta_w', 'delta_mem_q_norm': 'delta_w', 'delta_mem_k_norm': 'delta_w', 'delta_out_norm_attn': 'delta_w', 'delta_out_norm_conv': 'delta_w', 'delta_out_norm_mem': 'delta_w', 'delta_w_out': 'delta_w', 'delta_norm_ffn': 'delta_w', 'delta_w_gate': 'delta_w', 'delta_w_up': 'delta_w', 'delta_w_down': 'delta_w', 'new_m_norm_mix': 'new_m', 'new_m_w_in': 'new_m', 'new_m_q_norm': 'new_m', 'new_m_k_norm': 'new_m', 'new_m_attn_sinks': 'new_m', 'new_m_conv_w': 'new_m', 'new_m_conv_b': 'new_m', 'new_m_norm_mem': 'new_m', 'new_m_w_mem_kv': 'new_m', 'new_m_mem_q_norm': 'new_m', 'new_m_mem_k_norm': 'new_m', 'new_m_out_norm_attn': 'new_m', 'new_m_out_norm_conv': 'new_m', 'new_m_out_norm_mem': 'new_m', 'new_m_w_out': 'new_m', 'new_m_norm_ffn': 'new_m', 'new_m_w_gate': 'new_m', 'new_m_w_up': 'new_m', 'new_m_w_down': 'new_m', 'new_v_norm_mix': 'new_v', 'new_v_w_in': 'new_v', 'new_v_q_norm': 'new_v', 'new_v_k_norm': 'new_v', 'new_v_attn_sinks': 'new_v', 'new_v_conv_w': 'new_v', 'new_v_conv_b': 'new_v', 'new_v_norm_mem': 'new_v', 'new_v_w_mem_kv': 'new_v', 'new_v_mem_q_norm': 'new_v', 'new_v_mem_k_norm': 'new_v', 'new_v_out_norm_attn': 'new_v', 'new_v_out_norm_conv': 'new_v', 'new_v_out_norm_mem': 'new_v', 'new_v_w_out': 'new_v', 'new_v_norm_ffn': 'new_v', 'new_v_w_gate': 'new_v', 'new_v_w_up': 'new_v', 'new_v_w_down': 'new_v'}


def _forward(args):
    return _fwd_reference(*[args[k] for k in FWD_PARAMS])


def _output_shape():
    out = _jax.eval_shape(lambda: _forward(_fwd_setup_inputs(0)))
    return out.shape, out.dtype

N_MICROBATCH = 1
ADAM_LR = 0.001
ADAM_B1 = 0.9
ADAM_B2 = 0.999
ADAM_EPS = 1e-08
ADAM_WD = 0.01
ADAM_STEP = 10
PER_EXAMPLE_BATCH_AXIS = {'x': 0, 'mem': 0, 'loss_target': 0}
SHARED_INPUTS = []
_WEIGHT_DTYPES = {'norm_mix': _jnp.float32, 'w_in': _jnp.float32, 'q_norm': _jnp.float32, 'k_norm': _jnp.float32, 'attn_sinks': _jnp.float32, 'conv_w': _jnp.float32, 'conv_b': _jnp.float32, 'norm_mem': _jnp.float32, 'w_mem_kv': _jnp.float32, 'mem_q_norm': _jnp.float32, 'mem_k_norm': _jnp.float32, 'out_norm_attn': _jnp.float32, 'out_norm_conv': _jnp.float32, 'out_norm_mem': _jnp.float32, 'w_out': _jnp.float32, 'norm_ffn': _jnp.float32, 'w_gate': _jnp.float32, 'w_up': _jnp.float32, 'w_down': _jnp.float32}
MOMENT_SCALE = {'norm_mix': 7.343441e-01, 'w_in': 5.128627e-01, 'q_norm': 1.053207e+00, 'k_norm': 1.028067e+00, 'attn_sinks': 6.821783e+00, 'conv_w': 1.371698e+00, 'conv_b': 7.825591e-01, 'norm_mem': 1.255095e+00, 'w_mem_kv': 1.767664e+00, 'mem_q_norm': 1.448927e+00, 'mem_k_norm': 1.410839e+00, 'out_norm_attn': 3.805708e+01, 'out_norm_conv': 4.150249e+01, 'out_norm_mem': 3.296713e+01, 'w_out': 1.625734e+00, 'norm_ffn': 2.466903e+01, 'w_gate': 2.888714e-01, 'w_up': 2.464163e-01, 'w_down': 3.716223e-01}


def _to_microbatches(a, axis):
    t = _jnp.moveaxis(a, axis, 0)
    t = t.reshape((N_MICROBATCH, t.shape[0] // N_MICROBATCH) + t.shape[1:])
    return _jnp.moveaxis(t, 1, axis + 1)


def setup_inputs(seed: int = 0) -> dict:
    inp = _fwd_setup_inputs(seed)
    key = _jax.random.fold_in(_jax.random.key(seed), 7919)
    shape, _ = _output_shape()
    out = dict(inp)
    out["loss_target"] = _jax.random.normal(_jax.random.fold_in(key, 0), shape, _jnp.float32)
    for i, name in enumerate(TWIN_WEIGHTS):
        w = inp[name].astype(_jnp.float32)
        if MOMENT_SCALE is None:
            s = _jnp.sqrt(_jnp.mean(_jnp.square(w)) + 1e-30)
        else:
            s = MOMENT_SCALE[name]
        km, kv = _jax.random.split(_jax.random.fold_in(key, i + 1))
        out[name] = w
        out["m_" + name] = s * _jax.random.normal(km, w.shape, _jnp.float32)
        out["v_" + name] = (s * s) * _jax.random.uniform(kv, w.shape, _jnp.float32, 0.5, 1.5)
    if N_MICROBATCH > 1:
        for name, axis in PER_EXAMPLE_BATCH_AXIS.items():
            out[name] = _to_microbatches(out[name], axis)
    return {'x': out['x'], 'mem': out['mem'], 'norm_mix': out['norm_mix'], 'w_in': out['w_in'], 'q_norm': out['q_norm'], 'k_norm': out['k_norm'], 'attn_sinks': out['attn_sinks'], 'conv_w': out['conv_w'], 'conv_b': out['conv_b'], 'norm_mem': out['norm_mem'], 'w_mem_kv': out['w_mem_kv'], 'mem_q_norm': out['mem_q_norm'], 'mem_k_norm': out['mem_k_norm'], 'out_norm_attn': out['out_norm_attn'], 'out_norm_conv': out['out_norm_conv'], 'out_norm_mem': out['out_norm_mem'], 'w_out': out['w_out'], 'norm_ffn': out['norm_ffn'], 'w_gate': out['w_gate'], 'w_up': out['w_up'], 'w_down': out['w_down'], 'loss_target': out['loss_target'], 'm_norm_mix': out['m_norm_mix'], 'm_w_in': out['m_w_in'], 'm_q_norm': out['m_q_norm'], 'm_k_norm': out['m_k_norm'], 'm_attn_sinks': out['m_attn_sinks'], 'm_conv_w': out['m_conv_w'], 'm_conv_b': out['m_conv_b'], 'm_norm_mem': out['m_norm_mem'], 'm_w_mem_kv': out['m_w_mem_kv'], 'm_mem_q_norm': out['m_mem_q_norm'], 'm_mem_k_norm': out['m_mem_k_norm'], 'm_out_norm_attn': out['m_out_norm_attn'], 'm_out_norm_conv': out['m_out_norm_conv'], 'm_out_norm_mem': out['m_out_norm_mem'], 'm_w_out': out['m_w_out'], 'm_norm_ffn': out['m_norm_ffn'], 'm_w_gate': out['m_w_gate'], 'm_w_up': out['m_w_up'], 'm_w_down': out['m_w_down'], 'v_norm_mix': out['v_norm_mix'], 'v_w_in': out['v_w_in'], 'v_q_norm': out['v_q_norm'], 'v_k_norm': out['v_k_norm'], 'v_attn_sinks': out['v_attn_sinks'], 'v_conv_w': out['v_conv_w'], 'v_conv_b': out['v_conv_b'], 'v_norm_mem': out['v_norm_mem'], 'v_w_mem_kv': out['v_w_mem_kv'], 'v_mem_q_norm': out['v_mem_q_norm'], 'v_mem_k_norm': out['v_mem_k_norm'], 'v_out_norm_attn': out['v_out_norm_attn'], 'v_out_norm_conv': out['v_out_norm_conv'], 'v_out_norm_mem': out['v_out_norm_mem'], 'v_w_out': out['v_w_out'], 'v_norm_ffn': out['v_norm_ffn'], 'v_w_gate': out['v_w_gate'], 'v_w_up': out['v_w_up'], 'v_w_down': out['v_w_down']}


def _loss(weights, diff, rest, loss_target):
    with _jax.named_scope("forward"):
        args = {**rest, TWIN_DIFF_INPUT: diff, **{k: w.astype(_WEIGHT_DTYPES[k]) for k, w in weights.items()}}
        y = _forward(args)
    with _jax.named_scope("loss_head"):
        err = _jnp.square(y.astype(_jnp.float32) - loss_target)
        return 0.5 * _jnp.sum(_jnp.mean(err, axis=-1)) if err.ndim else 0.5 * err


def _adamw(w, g, m, v):
    m = ADAM_B1 * m + (1.0 - ADAM_B1) * g
    v = ADAM_B2 * v + (1.0 - ADAM_B2) * _jnp.square(g)
    m_hat = m / (1.0 - ADAM_B1 ** ADAM_STEP)
    v_hat = v / (1.0 - ADAM_B2 ** ADAM_STEP)
    delta = -ADAM_LR * (m_hat / (_jnp.sqrt(v_hat) + ADAM_EPS) + ADAM_WD * w)
    return delta, m, v


def reference(x, mem, norm_mix, w_in, q_norm, k_norm, attn_sinks, conv_w, conv_b, norm_mem, w_mem_kv, mem_q_norm, mem_k_norm, out_norm_attn, out_norm_conv, out_norm_mem, w_out, norm_ffn, w_gate, w_up, w_down, loss_target, m_norm_mix, m_w_in, m_q_norm, m_k_norm, m_attn_sinks, m_conv_w, m_conv_b, m_norm_mem, m_w_mem_kv, m_mem_q_norm, m_mem_k_norm, m_out_norm_attn, m_out_norm_conv, m_out_norm_mem, m_w_out, m_norm_ffn, m_w_gate, m_w_up, m_w_down, v_norm_mix, v_w_in, v_q_norm, v_k_norm, v_attn_sinks, v_conv_w, v_conv_b, v_norm_mem, v_w_mem_kv, v_mem_q_norm, v_mem_k_norm, v_out_norm_attn, v_out_norm_conv, v_out_norm_mem, v_w_out, v_norm_ffn, v_w_gate, v_w_up, v_w_down):
    given = dict(x=x, mem=mem, norm_mix=norm_mix, w_in=w_in, q_norm=q_norm, k_norm=k_norm, attn_sinks=attn_sinks, conv_w=conv_w, conv_b=conv_b, norm_mem=norm_mem, w_mem_kv=w_mem_kv, mem_q_norm=mem_q_norm, mem_k_norm=mem_k_norm, out_norm_attn=out_norm_attn, out_norm_conv=out_norm_conv, out_norm_mem=out_norm_mem, w_out=w_out, norm_ffn=norm_ffn, w_gate=w_gate, w_up=w_up, w_down=w_down, loss_target=loss_target, m_norm_mix=m_norm_mix, m_w_in=m_w_in, m_q_norm=m_q_norm, m_k_norm=m_k_norm, m_attn_sinks=m_attn_sinks, m_conv_w=m_conv_w, m_conv_b=m_conv_b, m_norm_mem=m_norm_mem, m_w_mem_kv=m_w_mem_kv, m_mem_q_norm=m_mem_q_norm, m_mem_k_norm=m_mem_k_norm, m_out_norm_attn=m_out_norm_attn, m_out_norm_conv=m_out_norm_conv, m_out_norm_mem=m_out_norm_mem, m_w_out=m_w_out, m_norm_ffn=m_norm_ffn, m_w_gate=m_w_gate, m_w_up=m_w_up, m_w_down=m_w_down, v_norm_mix=v_norm_mix, v_w_in=v_w_in, v_q_norm=v_q_norm, v_k_norm=v_k_norm, v_attn_sinks=v_attn_sinks, v_conv_w=v_conv_w, v_conv_b=v_conv_b, v_norm_mem=v_norm_mem, v_w_mem_kv=v_w_mem_kv, v_mem_q_norm=v_mem_q_norm, v_mem_k_norm=v_mem_k_norm, v_out_norm_attn=v_out_norm_attn, v_out_norm_conv=v_out_norm_conv, v_out_norm_mem=v_out_norm_mem, v_w_out=v_w_out, v_norm_ffn=v_norm_ffn, v_w_gate=v_w_gate, v_w_up=v_w_up, v_w_down=v_w_down)
    weights = {n: given[n] for n in TWIN_WEIGHTS}
    shared = {n: given[n] for n in SHARED_INPUTS}
    per_example = {n: given[n] for n in ['x', 'mem']}
    grad_fn = _jax.value_and_grad(_loss, argnums=(0, 1))

    def one_microbatch(ex, loss_target):
        ex = dict(ex)
        diff = ex.pop(TWIN_DIFF_INPUT)
        return grad_fn(weights, diff, {**shared, **ex}, loss_target)

    if N_MICROBATCH == 1:
        loss, (grad_w, grad_x) = one_microbatch(per_example, given["loss_target"])
    else:
        def body(carry, xs):
            loss_sum, grad_sum = carry
            l_k, (gw_k, gx_k) = one_microbatch(xs[0], xs[1])
            with _jax.named_scope("update"):
                return (loss_sum + l_k, _jax.tree.map(_jnp.add, grad_sum, gw_k)), gx_k

        init = (_jnp.zeros((), _jnp.float32), _jax.tree.map(_jnp.zeros_like, weights))
        (loss, grad_w), grad_x = _jax.lax.scan(body, init, (per_example, given["loss_target"]))
    with _jax.named_scope("update"):
        delta_w, new_m, new_v = {}, {}, {}
        for n in TWIN_WEIGHTS:
            delta_w[n], new_m[n], new_v[n] = _adamw(weights[n], grad_w[n], given["m_" + n], given["v_" + n])
    return (loss, grad_x, *[grad_w[n] for n in TWIN_WEIGHTS], *[delta_w[n] for n in TWIN_WEIGHTS],
            *[new_m[n] for n in TWIN_WEIGHTS], *[new_v[n] for n in TWIN_WEIGHTS])
```

```python
import functools

import jax
import jax.numpy as jnp
from jax import lax
from jax.experimental import pallas as pl
from jax.experimental.pallas import tpu as pltpu

f32 = jnp.float32
MXU = jnp.bfloat16
EPS = 1e-6
NEG = -1e30
HD = 64
BLK = 128
N_Q, N_KV, N_MEMH = 8, 2, 4
ATT_W, KV_W, CONV_W, MEM_W = 512, 128, 256, 256
VMEM_MIB = 1024 * 1024
ADAM_LR, ADAM_B1, ADAM_B2, ADAM_EPS, ADAM_WD, ADAM_STEP = 0.001, 0.9, 0.999, 1e-08, 0.01, 10

MESH = pl.DeviceIdType.MESH
VM = pl.BlockSpec(memory_space=pltpu.VMEM)
ANY = pl.BlockSpec(memory_space=pl.ANY)
SDS = jax.ShapeDtypeStruct


def _cp(sem=None, vmem_mib=32):
    return pltpu.CompilerParams(dimension_semantics=sem, vmem_limit_bytes=vmem_mib * VMEM_MIB)


def _c(v):
    return v.astype(MXU)


def _nn(a, b):
    return lax.dot_general(a, b, (((1,), (0,)), ((), ())), preferred_element_type=f32)


def _nt(a, b):
    return lax.dot_general(a, b, (((1,), (1,)), ((), ())), preferred_element_type=f32)


def _tn(a, b):
    return lax.dot_general(a, b, (((0,), (0,)), ((), ())), preferred_element_type=f32)


def _rstd(v):
    return lax.rsqrt(jnp.mean(v * v, axis=-1, keepdims=True) + EPS)


def _norm_bwd(dy, v, r, g):
    dyg = dy * g
    dv = r * dyg - v * (r * r * r) * jnp.mean(dyg * v, axis=-1, keepdims=True)
    return dv, jnp.sum(dy * v * r, axis=0, keepdims=True)


def _softmax_rows(s, extra=None):
    m = jnp.max(s, axis=-1, keepdims=True)
    if extra is not None:
        m = jnp.maximum(m, extra)
    p = jnp.exp(s - m)
    den = jnp.sum(p, axis=-1, keepdims=True)
    if extra is None:
        return p / den, None
    pe = jnp.exp(extra - m)
    den = den + pe
    return p / den, pe / den


def _place():
    return lax.axis_index("x"), lax.axis_index("y"), lax.axis_index("c")


def _remote(src, dst, ssem, rsem, dev):
    return pltpu.make_async_remote_copy(src_ref=src, dst_ref=dst, send_sem=ssem, recv_sem=rsem,
                                        device_id=dev, device_id_type=MESH)


def allgather_shards(shards, split):
    n = len(shards)

    def body(*refs):
        src, dst = refs[:n], refs[n:2 * n]
        ssem, rsem, lsem = refs[2 * n:]
        x, y, c = _place()
        k = 2 * x + y
        chips = [(1 - x, y), (x, 1 - y), (1 - x, 1 - y)]
        started = []
        for e in range(n):
            R = src[e].shape[0]
            cp = pltpu.make_async_copy(src[e], dst[e].at[pl.ds(pl.multiple_of(k * R, 8), R)], lsem.at[e])
            cp.start()
            started.append(cp)
        sends = []
        for e in range(n):
            R = src[e].shape[0]
            hr = R // 2
            for j, (cx, cy) in enumerate(chips):
                if split[e]:
                    s_rows = pl.ds(pl.multiple_of(c * hr, 8), hr)
                    d_rows = pl.ds(pl.multiple_of(k * R + c * hr, 8), hr)
                else:
                    s_rows = pl.ds(0, R)
                    d_rows = pl.ds(pl.multiple_of(k * R, 8), R)
                cp = _remote(src[e].at[s_rows], dst[e].at[d_rows], ssem.at[6 * e + j], rsem.at[6 * e + j], (cx, cy, c))
                cp.start()
                sends.append(cp)
        for e in range(n):
            R = src[e].shape[0]
            hr = R // 2
            for j, (cx, cy) in enumerate(chips):
                kj = 2 * cx + cy
                if split[e]:
                    rows = dst[e].at[pl.ds(pl.multiple_of(kj * R + c * hr, 8), hr)]
                else:
                    rows = dst[e].at[pl.ds(pl.multiple_of(kj * R, 8), R)]
                _remote(rows, rows, ssem.at[6 * e + j], rsem.at[6 * e + j], (cx, cy, c)).wait_recv()
                if split[e]:
                    fw = _remote(rows, rows, ssem.at[6 * e + 3 + j], rsem.at[6 * e + 3 + j], (x, y, 1 - c))
                    fw.start()
                    sends.append(fw)
        for e in range(n):
            if not split[e]:
                continue
            R = src[e].shape[0]
            hr = R // 2
            for j, (cx, cy) in enumerate(chips):
                kj = 2 * cx + cy
                rows = dst[e].at[pl.ds(pl.multiple_of(kj * R + (1 - c) * hr, 8), hr)]
                _remote(rows, rows, ssem.at[6 * e + 3 + j], rsem.at[6 * e + 3 + j], (x, y, 1 - c)).wait_recv()
        for cp in sends:
            cp.wait_send()
        for cp in started:
            cp.wait()

    return pl.pallas_call(
        body, name="allgather_weights",
        out_shape=[SDS((4 * s.shape[0], s.shape[1]), s.dtype) for s in shards],
        in_specs=[ANY] * n, out_specs=[ANY] * n,
        scratch_shapes=[pltpu.SemaphoreType.DMA((6 * n,)), pltpu.SemaphoreType.DMA((6 * n,)),
                        pltpu.SemaphoreType.DMA((n,))],
    )(*shards)


def exchange_halves(grads):
    n = len(grads)

    def body(*refs):
        g, st = refs[:n], refs[n:2 * n]
        ssem, rsem = refs[2 * n:]
        x, y, c = _place()
        cps = []
        for e in range(n):
            cp = _remote(g[e].at[:, 1 - c], st[e], ssem.at[e], rsem.at[e], (x, y, 1 - c))
            cp.start()
            cps.append(cp)
        for cp in cps:
            cp.wait()

    return pl.pallas_call(
        body, name="grad_exchange_halves",
        out_shape=[SDS((4,) + a.shape[2:], a.dtype) for a in grads],
        in_specs=[ANY] * n, out_specs=[ANY] * n,
        scratch_shapes=[pltpu.SemaphoreType.DMA((n,)), pltpu.SemaphoreType.DMA((n,))],
    )(*grads)


def scatter_partials(parts):
    n = len(parts)

    def body(*refs):
        p, st = refs[:n], refs[n:2 * n]
        ssem, rsem, lsem = refs[2 * n:]
        x, y, c = _place()
        k = 2 * x + y
        chips = [(1 - x, y), (x, 1 - y), (1 - x, 1 - y)]
        cps, loc = [], []
        for e in range(n):
            cp = pltpu.make_async_copy(p[e].at[k], st[e].at[k], lsem.at[e])
            cp.start()
            loc.append(cp)
            for j, (cx, cy) in enumerate(chips):
                kj = 2 * cx + cy
                cp = _remote(p[e].at[kj], st[e].at[k], ssem.at[3 * e + j], rsem.at[3 * e + j], (cx, cy, c))
                cp.start()
                cps.append(cp)
        for e in range(n):
            for j, (cx, cy) in enumerate(chips):
                kj = 2 * cx + cy
                _remote(p[e].at[kj], st[e].at[kj], ssem.at[3 * e + j], rsem.at[3 * e + j], (cx, cy, c)).wait_recv()
        for cp in cps:
            cp.wait_send()
        for cp in loc:
            cp.wait()

    return pl.pallas_call(
        body, name="grad_scatter_partials",
        out_shape=[SDS(a.shape, a.dtype) for a in parts],
        in_specs=[ANY] * n, out_specs=[ANY] * n,
        scratch_shapes=[pltpu.SemaphoreType.DMA((3 * n,)), pltpu.SemaphoreType.DMA((3 * n,)),
                        pltpu.SemaphoreType.DMA((n,))],
    )(*parts)


def join_halves(halves):
    n = len(halves)

    def body(*refs):
        h, full = refs[:n], refs[n:2 * n]
        ssem, rsem, lsem = refs[2 * n:]
        x, y, c = _place()
        cps = []
        for e in range(n):
            lc = pltpu.make_async_copy(h[e], full[e].at[c], lsem.at[e])
            lc.start()
            cp = _remote(h[e], full[e].at[c], ssem.at[e], rsem.at[e], (x, y, 1 - c))
            cp.start()
            cps.append((lc, cp))
        for e, (lc, cp) in enumerate(cps):
            _remote(h[e], full[e].at[1 - c], ssem.at[e], rsem.at[e], (x, y, 1 - c)).wait_recv()
            cp.wait_send()
            lc.wait()

    return pl.pallas_call(
        body, name="grad_join_halves",
        out_shape=[SDS((2,) + a.shape, a.dtype) for a in halves],
        in_specs=[ANY] * n, out_specs=[ANY] * n,
        scratch_shapes=[pltpu.SemaphoreType.DMA((n,)), pltpu.SemaphoreType.DMA((n,)), pltpu.SemaphoreType.DMA((n,))],
    )(*halves)


def allreduce_small(vec):
    def body(v_ref, o_ref, buf, ssem, rsem):
        x, y, c = _place()
        me = 4 * x + 2 * y + c
        buf[me] = v_ref[...]
        cps = []
        for m in range(1, 8):
            peer = (1 - x if m & 4 else x, 1 - y if m & 2 else y, 1 - c if m & 1 else c)
            cp = _remote(v_ref, buf.at[me], ssem.at[m - 1], rsem.at[m - 1], peer)
            cp.start()
            cps.append(cp)
        for m in range(1, 8):
            peer = (1 - x if m & 4 else x, 1 - y if m & 2 else y, 1 - c if m & 1 else c)
            src = 4 * peer[0] + 2 * peer[1] + peer[2]
            _remote(v_ref, buf.at[src], ssem.at[m - 1], rsem.at[m - 1], peer).wait_recv()
        for cp in cps:
            cp.wait_send()
        acc = buf[0]
        for d in range(1, 8):
            acc = acc + buf[d]
        o_ref[...] = acc

    return pl.pallas_call(
        body, name="allreduce_small", out_shape=SDS(vec.shape, f32), in_specs=[VM], out_specs=VM,
        scratch_shapes=[pltpu.VMEM((8,) + vec.shape, f32), pltpu.SemaphoreType.DMA((7,)), pltpu.SemaphoreType.DMA((7,))],
    )(vec)


def add_halves(cidx, grads, stages, nch=4):
    n = len(grads)

    def body(c_ref, *refs):
        g, st, o = refs[:n], refs[n:2 * n], refs[2 * n:]
        for e in range(n):
            o[e][...] = g[e][...] + st[e][...]

    in_specs, out_specs, out_shape = [], [], []
    for a in grads:
        hr, C = a.shape[2], a.shape[3]
        in_specs.append(pl.BlockSpec((None, None, hr // nch, C), lambda s, q, c_ref: (s, c_ref[0], q, 0)))
    for a in stages:
        hr, C = a.shape[1], a.shape[2]
        in_specs.append(pl.BlockSpec((None, hr // nch, C), lambda s, q, c_ref: (s, q, 0)))
        out_specs.append(pl.BlockSpec((None, hr // nch, C), lambda s, q, c_ref: (s, q, 0)))
        out_shape.append(SDS(a.shape, f32))
    return pl.pallas_call(
        body, name="grad_add_halves", out_shape=out_shape,
        grid_spec=pltpu.PrefetchScalarGridSpec(num_scalar_prefetch=1, grid=(4, nch), in_specs=in_specs, out_specs=out_specs),
        compiler_params=_cp(("arbitrary", "arbitrary")),
    )(cidx, *grads, *stages)


def add_chips(stages, nch=4):
    n = len(stages)

    def body(*refs):
        st, o = refs[:n], refs[n:]
        for e in range(n):
            o[e][...] = ((st[e][0] + st[e][1]) + st[e][2]) + st[e][3]

    in_specs, out_specs, out_shape = [], [], []
    for a in stages:
        hr, C = a.shape[1], a.shape[2]
        in_specs.append(pl.BlockSpec((4, hr // nch, C), lambda q: (0, q, 0)))
        out_specs.append(pl.BlockSpec((hr // nch, C), lambda q: (q, 0)))
        out_shape.append(SDS((hr, C), f32))
    return pl.pallas_call(body, name="grad_add_chips", out_shape=out_shape, grid=(nch,), in_specs=in_specs,
                          out_specs=out_specs, compiler_params=_cp(("arbitrary",)))(*stages)


def _adamw_math(w, g, m, v):
    m = ADAM_B1 * m + (1.0 - ADAM_B1) * g
    v = ADAM_B2 * v + (1.0 - ADAM_B2) * (g * g)
    m_hat = m / (1.0 - ADAM_B1 ** ADAM_STEP)
    v_hat = v / (1.0 - ADAM_B2 ** ADAM_STEP)
    delta = -ADAM_LR * (m_hat / (jnp.sqrt(v_hat) + ADAM_EPS) + ADAM_WD * w)
    return delta, m, v


def adamw(w, g, m, v, name, nrow=4):
    R, C = w.shape

    def body(w_ref, g_ref, m_ref, v_ref, d_ref, mo_ref, vo_ref):
        d, mm, vv = _adamw_math(w_ref[...], g_ref[...], m_ref[...], v_ref[...])
        d_ref[...] = d
        mo_ref[...] = mm
        vo_ref[...] = vv

    spec = pl.BlockSpec((R // nrow, C), lambda i: (i, 0))
    return pl.pallas_call(body, name=name, out_shape=[SDS((R, C), f32)] * 3, grid=(nrow,), in_specs=[spec] * 4,
                          out_specs=[spec] * 3, compiler_params=_cp(("arbitrary",)))(w, g, m, v)


def mem_kv_fwd(mem2d, g_mem, wmkv):
    M, D = mem2d.shape

    def body(m_ref, g_ref, w_ref, mn_ref, kv_ref):
        m = m_ref[...]
        mn = _c(m * _rstd(m) * g_ref[...])
        mn_ref[...] = mn
        kv_ref[...] = _nn(mn, w_ref[...])

    return pl.pallas_call(body, name="mem_kv_fwd", out_shape=[SDS((M, D), MXU), SDS((M, 2 * MEM_W), f32)],
                          in_specs=[VM] * 3, out_specs=[VM] * 2, compiler_params=_cp())(mem2d, g_mem, wmkv)


def in_proj_fwd(x2d, g1, winT, tm):
    T, D = x2d.shape
    P = winT.shape[0]

    def body(x_ref, g_ref, w_ref, xn_ref, proj_ref):
        xv = x_ref[...]
        xn = _c(xv * _rstd(xv) * g_ref[...])
        xn_ref[...] = xn
        proj_ref[...] = _nt(xn, w_ref[...])

    return pl.pallas_call(
        body, name="in_proj_fwd", grid=(T // tm,),
        in_specs=[pl.BlockSpec((tm, D), lambda i: (i, 0)), pl.BlockSpec((1, D), lambda i: (0, 0)), VM],
        out_specs=[pl.BlockSpec((tm, D), lambda i: (i, 0)), pl.BlockSpec((tm, P), lambda i: (i, 0))],
        out_shape=[SDS((T, D), MXU), SDS((T, P), f32)], compiler_params=_cp(("arbitrary",), 40),
    )(x2d, g1, winT)


def _swa_geometry(j):
    qi = lax.broadcasted_iota(jnp.int32, (BLK, 2 * BLK), 0)
    ki = lax.broadcasted_iota(jnp.int32, (BLK, 2 * BLK), 1)
    dist = qi + BLK - ki
    valid = (dist >= 0) & (dist < BLK) & ((ki >= BLK) | (j > 0))
    return valid, dist.astype(f32)


def _swa_probs(qn, kn, valid, distf, h, sink):
    s = _nt(qn, kn) * (HD ** -0.5) - (2.0 ** -(h + 1)) * distf
    s = jnp.where(valid, s, NEG)
    return _softmax_rows(s, sink)


def attn_fwd(proj, qg, kg, sink_rows, BL, S):
    NB = S // BLK
    T = BL * S

    def body(q_ref, kc_ref, kp_ref, vc_ref, vp_ref, qg_ref, kg_ref, sk_ref, o_ref):
        valid, distf = _swa_geometry(pl.program_id(1))
        q = q_ref[...]
        k2 = jnp.concatenate([kp_ref[...], kc_ref[...]], axis=0)
        v2 = jnp.concatenate([vp_ref[...], vc_ref[...]], axis=0)
        for g in range(N_KV):
            kh = k2[:, g * HD:(g + 1) * HD]
            kn = _c(kh * _rstd(kh) * kg_ref[...])
            vh = _c(v2[:, g * HD:(g + 1) * HD])
            for hh in range(N_Q // N_KV):
                h = g * (N_Q // N_KV) + hh
                qh = q[:, h * HD:(h + 1) * HD]
                qn = _c(qh * _rstd(qh) * qg_ref[...])
                p, _ = _swa_probs(qn, kn, valid, distf, h, sk_ref[h:h + 1, 0:1])
                o_ref[:, pl.ds(h * HD, HD)] = _nn(_c(p), vh)

    cur = lambda col: (lambda b, j: (b * NB + j, col))
    prev = lambda col: (lambda b, j: (b * NB + jnp.maximum(j - 1, 0), col))
    small = lambda shape: pl.BlockSpec(shape, lambda b, j: (0, 0))
    return pl.pallas_call(
        body, name="attn_fwd", grid=(BL, NB),
        in_specs=[pl.BlockSpec((BLK, ATT_W), cur(0)),
                  pl.BlockSpec((BLK, KV_W), cur(4)), pl.BlockSpec((BLK, KV_W), prev(4)),
                  pl.BlockSpec((BLK, KV_W), cur(5)), pl.BlockSpec((BLK, KV_W), prev(5)),
                  small((1, HD)), small((1, HD)), small((8, 128))],
        out_specs=pl.BlockSpec((BLK, ATT_W), cur(0)),
        out_shape=SDS((T, ATT_W), f32), compiler_params=_cp(("arbitrary", "arbitrary")),
    )(proj, proj, proj, proj, proj, qg, kg, sink_rows)


def _conv_taps(u, uh, tm):
    row = lax.broadcasted_iota(jnp.int32, u.shape, 0)
    u1 = jnp.where(row == 0, uh[7:8, :], pltpu.roll(u, 1, 0))
    u2 = jnp.where(row == 0, uh[6:7, :], jnp.where(row == 1, uh[7:8, :], pltpu.roll(u, 2, 0)))
    return u1, u2


def _mem_heads(qm, km, vm, qg, h):
    qh = qm[:, h * HD:(h + 1) * HD]
    r = _rstd(qh)
    qn = qh * r * qg
    kh = _c(km[:, h * HD:(h + 1) * HD])
    vh = _c(vm[:, h * HD:(h + 1) * HD])
    p, _ = _softmax_rows(_nt(_c(qn), kh) * (HD ** -0.5))
    return qh, r, qn, kh, vh, p


def mixer_tail_fwd(x2d, attn_out, proj, km, vm, conv_w8, conv_b, g_a, g_c, g_m, mqg, wout, g_f, S, tm):
    T, D = x2d.shape
    NM = km.shape[0] // (T // S)

    def body(x_ref, ao_ref, ch_ref, cb_ref, cc_ref, chh_ref, cch_ref, qm_ref, km_ref, vm_ref, cw_ref, cbias_ref,
             ga_ref, gc_ref, gm_ref, mqg_ref, wout_ref, gf_ref, co_ref, mo_ref, mg_ref, x1_ref, h_ref):
        first = (pl.program_id(0) * tm) % S == 0
        u = cc_ref[...] * ch_ref[...]
        uh = jnp.where(first, 0.0, cch_ref[...] * chh_ref[...])
        u1, u2 = _conv_taps(u, uh, tm)
        conv = cw_ref[0:1, :] * u2 + cw_ref[1:2, :] * u1 + cw_ref[2:3, :] * u + cbias_ref[...]
        conv_out = cb_ref[...] * conv
        co_ref[...] = conv_out
        qm, kmv, vmv = qm_ref[...], km_ref[...], vm_ref[...]
        for h in range(N_MEMH):
            _, _, _, _, vh, p = _mem_heads(qm, kmv, vmv, mqg_ref[...], h)
            mo_ref[:, pl.ds(h * HD, HD)] = _nn(_c(p), vh)
        mem_out = mo_ref[...]
        ao = ao_ref[...]
        merged = _c(jnp.concatenate([ao * _rstd(ao) * ga_ref[...], conv_out * _rstd(conv_out) * gc_ref[...],
                                     mem_out * _rstd(mem_out) * gm_ref[...]], axis=1))
        mg_ref[...] = merged
        x1 = x_ref[...] + _nn(merged, wout_ref[...])
        x1_ref[...] = x1
        h_ref[...] = _c(x1 * _rstd(x1) * gf_ref[...])

    tile = lambda w, col: pl.BlockSpec((tm, w), lambda i: (i, col))
    halo = lambda col: pl.BlockSpec((8, CONV_W), lambda i: (jnp.maximum(i * (tm // 8) - 1, 0), col))
    seq = pl.BlockSpec((NM, MEM_W), lambda i: ((i * tm) // S, 0))
    small = lambda a: pl.BlockSpec(a.shape, lambda i: (0, 0))
    return pl.pallas_call(
        body, name="mixer_tail_fwd", grid=(T // tm,),
        in_specs=[tile(D, 0), tile(ATT_W, 0), tile(CONV_W, 3), tile(CONV_W, 4), tile(CONV_W, 5), halo(3), halo(5),
                  tile(MEM_W, 6), seq, seq, small(conv_w8), small(conv_b), small(g_a), small(g_c), small(g_m),
                  small(mqg), VM, small(g_f)],
        out_specs=[tile(CONV_W, 0), tile(MEM_W, 0), tile(D, 0), tile(D, 0), tile(D, 0)],
        out_shape=[SDS((T, CONV_W), f32), SDS((T, MEM_W), f32), SDS((T, D), MXU), SDS((T, D), f32), SDS((T, D), MXU)],
        compiler_params=_cp(("arbitrary",), 40),
    )(x2d, attn_out, proj, proj, proj, proj, proj, proj, km, vm, conv_w8, conv_b, g_a, g_c, g_m, mqg, wout, g_f)


def ffn_fwd_bwd(h, x1, tgt, wgT, wuT, wd, g_f, tm):
    T, D = x1.shape
    F = wd.shape[0]

    def body(h_ref, x1_ref, t_ref, wg_ref, wu_ref, wd_ref, gf_ref,
             dx1_ref, dx2_ref, act_ref, dg_ref, du_ref, loss_ref, dgf_ref):
        @pl.when(pl.program_id(0) == 0)
        def _():
            loss_ref[...] = jnp.zeros_like(loss_ref)
            dgf_ref[...] = jnp.zeros_like(dgf_ref)

        hv = h_ref[...]
        gate = _nt(hv, wg_ref[...])
        up = _nt(hv, wu_ref[...])
        sg = jax.nn.sigmoid(gate)
        sl = gate * sg
        act = _c(sl * up)
        act_ref[...] = act
        x1v = x1_ref[...]
        diff = (x1v + _nn(act, wd_ref[...])) - t_ref[...]
        loss_ref[...] += 0.5 * jnp.sum(jnp.sum(diff * diff, axis=-1, keepdims=True) / D, axis=0, keepdims=True)
        dx2 = diff / D
        dx2b = _c(dx2)
        dx2_ref[...] = dx2b
        d_act = _nt(dx2b, wd_ref[...])
        d_up = _c(d_act * sl)
        d_gate = _c(d_act * up * (sg * (1.0 + gate * (1.0 - sg))))
        du_ref[...] = d_up
        dg_ref[...] = d_gate
        dh = _nn(d_gate, wg_ref[...]) + _nn(d_up, wu_ref[...])
        dv, dgf = _norm_bwd(dh, x1v, _rstd(x1v), gf_ref[...])
        dx1_ref[...] = dx2 + dv
        dgf_ref[...] += dgf

    tile = lambda w: pl.BlockSpec((tm, w), lambda i: (i, 0))
    return pl.pallas_call(
        body, name="ffn_fwd_bwd", grid=(T // tm,),
        in_specs=[tile(D), tile(D), tile(D), VM, VM, VM, pl.BlockSpec((1, D), lambda i: (0, 0))],
        out_specs=[tile(D), tile(D), tile(F), tile(F), tile(F), pl.BlockSpec((8, 128), lambda i: (0, 0)),
                   pl.BlockSpec((1, D), lambda i: (0, 0))],
        out_shape=[SDS((T, D), f32), SDS((T, D), MXU), SDS((T, F), MXU), SDS((T, F), MXU), SDS((T, F), MXU),
                   SDS((8, 128), f32), SDS((1, D), f32)],
        compiler_params=_cp(("arbitrary",), 56),
    )(h, x1, tgt, wgT, wuT, wd, g_f)


def matmul_tn(a, b, name, tmo, tk):
    T, M = a.shape
    N = b.shape[1]

    def body(a_ref, b_ref, o_ref):
        @pl.when(pl.program_id(1) == 0)
        def _():
            o_ref[...] = jnp.zeros_like(o_ref)

        o_ref[...] += _tn(a_ref[...], b_ref[...])

    return pl.pallas_call(
        body, name=name, grid=(M // tmo, T // tk),
        in_specs=[pl.BlockSpec((tk, tmo), lambda m, k: (k, m)), pl.BlockSpec((tk, N), lambda m, k: (k, 0))],
        out_specs=pl.BlockSpec((tmo, N), lambda m, k: (m, 0)), out_shape=SDS((M, N), f32),
        compiler_params=_cp(("arbitrary", "arbitrary"), 48),
    )(a, b)


def out_proj_bwd(dx1, merged, attn_out, conv_out, mem_out, g_a, g_c, g_m, wout, tm):
    T, D = dx1.shape

    def body(dx1_ref, mg_ref, ao_ref, co_ref, mo_ref, ga_ref, gc_ref, gm_ref, w_ref,
             dao_ref, dco_ref, dmo_ref, dw_ref, dgain_ref):
        @pl.when(pl.program_id(0) == 0)
        def _():
            dw_ref[...] = jnp.zeros_like(dw_ref)
            dgain_ref[...] = jnp.zeros_like(dgain_ref)

        dxb = _c(dx1_ref[...])
        dw_ref[...] += _tn(mg_ref[...], dxb)
        dmg = _nt(dxb, w_ref[...])
        ao, co, mo = ao_ref[...], co_ref[...], mo_ref[...]
        da, ga = _norm_bwd(dmg[:, :ATT_W], ao, _rstd(ao), ga_ref[...])
        dc, gc = _norm_bwd(dmg[:, ATT_W:ATT_W + CONV_W], co, _rstd(co), gc_ref[...])
        dm, gm = _norm_bwd(dmg[:, ATT_W + CONV_W:], mo, _rstd(mo), gm_ref[...])
        dao_ref[...] = da
        dco_ref[...] = dc
        dmo_ref[...] = dm
        dgain_ref[...] += jnp.concatenate([ga, gc, gm], axis=1)

    tile = lambda w: pl.BlockSpec((tm, w), lambda i: (i, 0))
    small = lambda a: pl.BlockSpec(a.shape, lambda i: (0, 0))
    return pl.pallas_call(
        body, name="out_proj_bwd", grid=(T // tm,),
        in_specs=[tile(D), tile(D), tile(ATT_W), tile(CONV_W), tile(MEM_W), small(g_a), small(g_c), small(g_m), VM],
        out_specs=[tile(ATT_W), tile(CONV_W), tile(MEM_W), pl.BlockSpec((D, D), lambda i: (0, 0)),
                   pl.BlockSpec((1, D), lambda i: (0, 0))],
        out_shape=[SDS((T, ATT_W), f32), SDS((T, CONV_W), f32), SDS((T, MEM_W), f32), SDS((D, D), f32), SDS((1, D), f32)],
        compiler_params=_cp(("arbitrary",), 40),
    )(dx1, merged, attn_out, conv_out, mem_out, g_a, g_c, g_m, wout)


def attn_bwd(proj, d_attn, qg, kg, sink_rows, BL, S):
    NB = S // BLK
    T = BL * S
    G = N_Q // N_KV

    def body(q_ref, kc_ref, kp_ref, vc_ref, vp_ref, do_ref, qg_ref, kg_ref, sk_ref,
             dq_ref, dk_ref, dv_ref, dqg_ref, dkg_ref, dsk_ref, pend_k, pend_v, fin_k, fin_v):
        b, j = pl.program_id(0), pl.program_id(1)

        @pl.when((b == 0) & (j == 0))
        def _():
            dqg_ref[...] = jnp.zeros_like(dqg_ref)
            dkg_ref[...] = jnp.zeros_like(dkg_ref)
            dsk_ref[...] = jnp.zeros_like(dsk_ref)

        @pl.when(j == 0)
        def _():
            pend_k[...] = jnp.zeros_like(pend_k)
            pend_v[...] = jnp.zeros_like(pend_v)

        @pl.when(j < NB)
        def _():
            valid, distf = _swa_geometry(j)
            q, do = q_ref[...], do_ref[...]
            k2 = jnp.concatenate([kp_ref[...], kc_ref[...]], axis=0)
            v2 = jnp.concatenate([vp_ref[...], vc_ref[...]], axis=0)
            lane = lax.broadcasted_iota(jnp.int32, (8, 128), 1)
            dqg = jnp.zeros((1, HD), f32)
            dsk = jnp.zeros((8, 128), f32)
            dks, dvs = [], []
            for g in range(N_KV):
                kh = k2[:, g * HD:(g + 1) * HD]
                kn = _c(kh * _rstd(kh) * kg_ref[...])
                vh = _c(v2[:, g * HD:(g + 1) * HD])
                dk_g = jnp.zeros((2 * BLK, HD), f32)
                dv_g = jnp.zeros((2 * BLK, HD), f32)
                for hh in range(G):
                    h = g * G + hh
                    qh = q[:, h * HD:(h + 1) * HD]
                    r = _rstd(qh)
                    qn = _c(qh * r * qg_ref[...])
                    p, ps = _swa_probs(qn, kn, valid, distf, h, sk_ref[h:h + 1, 0:1])
                    doh = _c(do[:, h * HD:(h + 1) * HD])
                    dp = _nt(doh, vh)
                    delta = jnp.sum(p * dp, axis=-1, keepdims=True)
                    ds = _c(p * (dp - delta) * (HD ** -0.5))
                    dsk = dsk + jnp.where(lane == h, -jnp.sum(ps * delta), 0.0)
                    dv_g = dv_g + _tn(_c(p), doh)
                    dk_g = dk_g + _tn(ds, qn)
                    dqh, gq = _norm_bwd(_nn(ds, kn), qh, r, qg_ref[...])
                    dq_ref[:, pl.ds(h * HD, HD)] = dqh
                    dqg = dqg + gq
                dks.append(dk_g)
                dvs.append(dv_g)
            dk2 = jnp.concatenate(dks, axis=1)
            dv2 = jnp.concatenate(dvs, axis=1)
            fin_k[...] = pend_k[...] + dk2[:BLK]
            fin_v[...] = pend_v[...] + dv2[:BLK]
            pend_k[...] = dk2[BLK:]
            pend_v[...] = dv2[BLK:]
            dqg_ref[...] += dqg
            dsk_ref[...] += dsk

        @pl.when(j == NB)
        def _():
            fin_k[...] = pend_k[...]
            fin_v[...] = pend_v[...]

        dv_ref[...] = fin_v[...]
        kp = kp_ref[...]
        dkn = fin_k[...]
        dkg = jnp.zeros((1, HD), f32)
        for g in range(N_KV):
            kh = kp[:, g * HD:(g + 1) * HD]
            dkh, gk = _norm_bwd(dkn[:, g * HD:(g + 1) * HD], kh, _rstd(kh), kg_ref[...])
            dk_ref[:, pl.ds(g * HD, HD)] = dkh
            dkg = dkg + gk
        dkg_ref[...] += dkg

    cur = lambda col: (lambda b, j: (b * NB + jnp.minimum(j, NB - 1), col))
    prev = lambda col: (lambda b, j: (b * NB + jnp.maximum(j - 1, 0), col))
    small = lambda shape: pl.BlockSpec(shape, lambda b, j: (0, 0))
    return pl.pallas_call(
        body, name="attn_bwd", grid=(BL, NB + 1),
        in_specs=[pl.BlockSpec((BLK, ATT_W), cur(0)),
                  pl.BlockSpec((BLK, KV_W), cur(4)), pl.BlockSpec((BLK, KV_W), prev(4)),
                  pl.BlockSpec((BLK, KV_W), cur(5)), pl.BlockSpec((BLK, KV_W), prev(5)),
                  pl.BlockSpec((BLK, ATT_W), cur(0)), small((1, HD)), small((1, HD)), small((8, 128))],
        out_specs=[pl.BlockSpec((BLK, ATT_W), cur(0)), pl.BlockSpec((BLK, KV_W), prev(0)),
                   pl.BlockSpec((BLK, KV_W), prev(0)), small((1, HD)), small((1, HD)), small((8, 128))],
        out_shape=[SDS((T, ATT_W), f32), SDS((T, KV_W), f32), SDS((T, KV_W), f32), SDS((1, HD), f32),
                   SDS((1, HD), f32), SDS((8, 128), f32)],
        scratch_shapes=[pltpu.VMEM((BLK, KV_W), f32)] * 4,
        compiler_params=_cp(("arbitrary", "arbitrary")),
    )(proj, proj, proj, proj, proj, d_attn, qg, kg, sink_rows)


def mem_conv_bwd(d_mem_out, d_conv_out, proj, km, vm, conv_w8, conv_b, mqg, S, tm):
    T = d_mem_out.shape[0]
    NM = km.shape[0] // (T // S)

    def body(dmo_ref, dco_ref, ch_ref, cb_ref, cc_ref, chh_ref, cch_ref, qm_ref, km_ref, vm_ref, cw_ref, cbias_ref,
             mqg_ref, dqm_ref, dkm_ref, dvm_ref, dmqg_ref, dcb_ref, dcv_ref, dcw_ref, dcbias_ref):
        i = pl.program_id(0)
        first = (i * tm) % S == 0

        @pl.when(i == 0)
        def _():
            dmqg_ref[...] = jnp.zeros_like(dmqg_ref)
            dcw_ref[...] = jnp.zeros_like(dcw_ref)
            dcbias_ref[...] = jnp.zeros_like(dcbias_ref)

        @pl.when(first)
        def _():
            dkm_ref[...] = jnp.zeros_like(dkm_ref)
            dvm_ref[...] = jnp.zeros_like(dvm_ref)

        qm, kmv, vmv, dmo = qm_ref[...], km_ref[...], vm_ref[...], dmo_ref[...]
        dmqg = jnp.zeros((1, HD), f32)
        for h in range(N_MEMH):
            qh, r, qn, kh, vh, p = _mem_heads(qm, kmv, vmv, mqg_ref[...], h)
            doh = _c(dmo[:, h * HD:(h + 1) * HD])
            dp = _nt(doh, vh)
            ds = _c(p * (dp - jnp.sum(p * dp, axis=-1, keepdims=True)) * (HD ** -0.5))
            dvm_ref[:, pl.ds(h * HD, HD)] += _tn(_c(p), doh)
            dkm_ref[:, pl.ds(h * HD, HD)] += _tn(ds, _c(qn))
            dqh, gq = _norm_bwd(_nn(ds, kh), qh, r, mqg_ref[...])
            dqm_ref[:, pl.ds(h * HD, HD)] = dqh
            dmqg = dmqg + gq
        dmqg_ref[...] += dmqg

        u = cc_ref[...] * ch_ref[...]
        uh = jnp.where(first, 0.0, cch_ref[...] * chh_ref[...])
        u1, u2 = _conv_taps(u, uh, tm)
        conv = cw_ref[0:1, :] * u2 + cw_ref[1:2, :] * u1 + cw_ref[2:3, :] * u + cbias_ref[...]
        dy = dco_ref[...]
        dcb_ref[...] = dy * conv
        dcv = dy * cb_ref[...]
        dcv_ref[...] = dcv
        dcbias_ref[...] += jnp.sum(dcv, axis=0, keepdims=True)
        dcw_ref[0:1, :] += jnp.sum(dcv * u2, axis=0, keepdims=True)
        dcw_ref[1:2, :] += jnp.sum(dcv * u1, axis=0, keepdims=True)
        dcw_ref[2:3, :] += jnp.sum(dcv * u, axis=0, keepdims=True)

    tile = lambda w, col: pl.BlockSpec((tm, w), lambda i: (i, col))
    halo = lambda col: pl.BlockSpec((8, CONV_W), lambda i: (jnp.maximum(i * (tm // 8) - 1, 0), col))
    seq = pl.BlockSpec((NM, MEM_W), lambda i: ((i * tm) // S, 0))
    small = lambda a: pl.BlockSpec(a.shape, lambda i: (0, 0))
    return pl.pallas_call(
        body, name="mem_conv_bwd", grid=(T // tm,),
        in_specs=[tile(MEM_W, 0), tile(CONV_W, 0), tile(CONV_W, 3), tile(CONV_W, 4), tile(CONV_W, 5), halo(3), halo(5),
                  tile(MEM_W, 6), seq, seq, small(conv_w8), small(conv_b), small(mqg)],
        out_specs=[tile(MEM_W, 0), seq, seq, small(mqg), tile(CONV_W, 0), tile(CONV_W, 0), small(conv_w8), small(conv_b)],
        out_shape=[SDS((T, MEM_W), f32), SDS(km.shape, f32), SDS(km.shape, f32), SDS(mqg.shape, f32),
                   SDS((T, CONV_W), f32), SDS((T, CONV_W), f32), SDS(conv_w8.shape, f32), SDS(conv_b.shape, f32)],
        compiler_params=_cp(("arbitrary",)),
    )(d_mem_out, d_conv_out, proj, proj, proj, proj, proj, proj, km, vm, conv_w8, conv_b, mqg)


def in_proj_bwd(dq, dk, dv, dcb, dcv, dqm, proj, conv_w8, xn, x2d, dx1, g1, winT, S, tm):
    T, D = x2d.shape
    P = winT.shape[0]
    last_blk = T // 8 - 1

    def body(dq_ref, dk_ref, dv_ref, dcb_ref, dcv_ref, dcvn_ref, dqm_ref, ch_ref, cc_ref, cw_ref, xn_ref, x_ref,
             dx1_ref, g_ref, w_ref, dx_ref, dw_ref, dg_ref):
        i = pl.program_id(0)

        @pl.when(i == 0)
        def _():
            dw_ref[...] = jnp.zeros_like(dw_ref)
            dg_ref[...] = jnp.zeros_like(dg_ref)

        last = ((i + 1) * tm) % S == 0
        dcv = dcv_ref[...]
        nxt = jnp.where(last, 0.0, dcvn_ref[...])
        row = lax.broadcasted_iota(jnp.int32, dcv.shape, 0)
        n1 = jnp.where(row == tm - 1, nxt[0:1, :], pltpu.roll(dcv, tm - 1, 0))
        n2 = jnp.where(row == tm - 2, nxt[0:1, :], jnp.where(row == tm - 1, nxt[1:2, :], pltpu.roll(dcv, tm - 2, 0)))
        du = cw_ref[2:3, :] * dcv + cw_ref[1:2, :] * n1 + cw_ref[0:1, :] * n2
        d_proj = jnp.concatenate([_c(dq_ref[...]), _c(dk_ref[...]), _c(dv_ref[...]), _c(du * cc_ref[...]),
                                  _c(dcb_ref[...]), _c(du * ch_ref[...]), _c(dqm_ref[...])], axis=1)
        dw_ref[...] += _tn(d_proj, xn_ref[...])
        xv = x_ref[...]
        dv_, dg = _norm_bwd(_nn(d_proj, w_ref[...]), xv, _rstd(xv), g_ref[...])
        dx_ref[...] = dx1_ref[...] + dv_
        dg_ref[...] += dg

    tile = lambda w, col=0: pl.BlockSpec((tm, w), lambda i: (i, col))
    nhalo = pl.BlockSpec((8, CONV_W), lambda i: (jnp.minimum((i + 1) * (tm // 8), last_blk), 0))
    small = lambda a: pl.BlockSpec(a.shape, lambda i: (0, 0))
    return pl.pallas_call(
        body, name="in_proj_bwd", grid=(T // tm,),
        in_specs=[tile(ATT_W), tile(KV_W), tile(KV_W), tile(CONV_W), tile(CONV_W), nhalo, tile(MEM_W),
                  tile(CONV_W, 3), tile(CONV_W, 5), small(conv_w8), tile(D), tile(D), tile(D), small(g1), VM],
        out_specs=[tile(D), pl.BlockSpec((P, D), lambda i: (0, 0)), small(g1)],
        out_shape=[SDS((T, D), f32), SDS((P, D), f32), SDS(g1.shape, f32)],
        compiler_params=_cp(("arbitrary",), 48),
    )(dq, dk, dv, dcb, dcv, dcv, dqm, proj, proj, conv_w8, xn, x2d, dx1, g1, winT)


def mem_kv_bwd(dkm, dvm, kv, memn, mem2d, g_mem, mkg, wmkv):
    M, D = mem2d.shape

    def body(dkm_ref, dvm_ref, kv_ref, mn_ref, m_ref, g_ref, kg_ref, w_ref, dw_ref, dg_ref, dkg_ref):
        kv_ = kv_ref[...]
        dkn = dkm_ref[...]
        dkg = jnp.zeros((1, HD), f32)
        parts = []
        for h in range(N_MEMH):
            kh = kv_[:, h * HD:(h + 1) * HD]
            dkh, gk = _norm_bwd(dkn[:, h * HD:(h + 1) * HD], kh, _rstd(kh), kg_ref[...])
            parts.append(dkh)
            dkg = dkg + gk
        dkg_ref[...] = dkg
        dkv = _c(jnp.concatenate(parts + [dvm_ref[...]], axis=1))
        dw_ref[...] = _tn(mn_ref[...], dkv)
        mv = m_ref[...]
        dg_ref[...] = jnp.sum(_nt(dkv, w_ref[...]) * mv * _rstd(mv), axis=0, keepdims=True)

    return pl.pallas_call(
        body, name="mem_kv_bwd", out_shape=[SDS(wmkv.shape, f32), SDS(g_mem.shape, f32), SDS(mkg.shape, f32)],
        in_specs=[VM] * 8, out_specs=[VM] * 3, compiler_params=_cp(None, 40),
    )(dkm, dvm, kv, memn, mem2d, g_mem, mkg, wmkv)


def mem_k_norm_fwd(kv, mkg):
    M = kv.shape[0]

    def body(kv_ref, g_ref, km_ref, vm_ref):
        kv_ = kv_ref[...]
        for h in range(N_MEMH):
            kh = kv_[:, h * HD:(h + 1) * HD]
            km_ref[:, pl.ds(h * HD, HD)] = kh * _rstd(kh) * g_ref[...]
        vm_ref[...] = kv_[:, MEM_W:]

    return pl.pallas_call(body, name="mem_k_norm_fwd", out_shape=[SDS((M, MEM_W), f32)] * 2, in_specs=[VM] * 2,
                          out_specs=[VM] * 2, compiler_params=_cp())(kv, mkg)


SMALL_ROWS = 8
SMALL_COLS = 1024


def _pack_small(norm_mix, norm_mem, norm_ffn, gains, conv_w_flat, conv_b, qg, kg, mqg, mkg, sinks, loss):
    z = lambda n: jnp.zeros((1, n), f32)
    row4 = jnp.concatenate([conv_w_flat, z(768 - conv_w_flat.shape[1]), conv_b], axis=1)
    row5 = jnp.concatenate([qg, kg, mqg, mkg, sinks, loss, z(SMALL_COLS - 4 * HD - 9)], axis=1)
    return jnp.concatenate([norm_mix, norm_mem, norm_ffn, gains, row4, row5, z(SMALL_COLS), z(SMALL_COLS)], axis=0)


def kernel(x, mem, norm_mix, w_in, q_norm, k_norm, attn_sinks, conv_w, conv_b, norm_mem, w_mem_kv, mem_q_norm, mem_k_norm, out_norm_attn, out_norm_conv, out_norm_mem, w_out, norm_ffn, w_gate, w_up, w_down, loss_target, m_norm_mix, m_w_in, m_q_norm, m_k_norm, m_attn_sinks, m_conv_w, m_conv_b, m_norm_mem, m_w_mem_kv, m_mem_q_norm, m_mem_k_norm, m_out_norm_attn, m_out_norm_conv, m_out_norm_mem, m_w_out, m_norm_ffn, m_w_gate, m_w_up, m_w_down, v_norm_mix, v_w_in, v_q_norm, v_k_norm, v_attn_sinks, v_conv_w, v_conv_b, v_norm_mem, v_w_mem_kv, v_mem_q_norm, v_mem_k_norm, v_out_norm_attn, v_out_norm_conv, v_out_norm_mem, v_w_out, v_norm_ffn, v_w_gate, v_w_up, v_w_down):
    BL, S, D = x.shape
    T = BL * S
    TM = 256
    xi, yi, ci = _place()
    chip = 2 * xi + yi

    cw_pad = jnp.zeros((8, 128), f32).at[:3, :HD].set(conv_w[0])
    winT, wgT, wuT, wd, wout, wmkv, cw_all = allgather_shards(
        [_c(w_in[0].T), _c(w_gate[0].T), _c(w_up[0].T), _c(w_down[0]), _c(w_out[0]), _c(w_mem_kv[0]), cw_pad],
        [True, True, True, True, True, True, False])
    conv_w_full = jnp.transpose(cw_all.reshape(4, 8, 128)[:, :3, :HD], (1, 0, 2)).reshape(3, CONV_W)
    conv_w8 = jnp.zeros((8, CONV_W), f32).at[:3].set(conv_w_full)
    sink_rows = jnp.broadcast_to(attn_sinks.reshape(N_Q, 1), (N_Q, 128))

    x2d = x.reshape(T, D)
    mem2d = mem.reshape(-1, D)
    memn, kv = mem_kv_fwd(mem2d, norm_mem, wmkv)
    km, vm = mem_k_norm_fwd(kv, mem_k_norm)
    xn, proj = in_proj_fwd(x2d, norm_mix, winT, TM)
    attn_out = attn_fwd(proj, q_norm, k_norm, sink_rows, BL, S)
    conv_out, mem_out, merged, x1, h = mixer_tail_fwd(
        x2d, attn_out, proj, km, vm, conv_w8, conv_b, out_norm_attn, out_norm_conv, out_norm_mem, mem_q_norm, wout,
        norm_ffn, S, TM)

    dx1, dx2b, act, d_gate, d_up, loss8, d_norm_ffn = ffn_fwd_bwd(h, x1, loss_target.reshape(T, D), wgT, wuT, wd, norm_ffn, TM)
    F = wd.shape[0]
    g_wd = matmul_tn(act, dx2b, "dw_down", F // 2, 512)
    g_wgT = matmul_tn(d_gate, h, "dw_gate", F // 2, 512)
    g_wuT = matmul_tn(d_up, h, "dw_up", F // 2, 512)

    d_attn, d_conv_out, d_mem_out, g_wout, d_gains = out_proj_bwd(
        dx1, merged, attn_out, conv_out, mem_out, out_norm_attn, out_norm_conv, out_norm_mem, wout, TM)
    dq, dk, dv, d_qg, d_kg, d_sink8 = attn_bwd(proj, d_attn, q_norm, k_norm, sink_rows, BL, S)
    dqm, dkm, dvm, d_mqg, dcb, dcv, d_cw8, d_cbias = mem_conv_bwd(
        d_mem_out, d_conv_out, proj, km, vm, conv_w8, conv_b, mem_q_norm, S, TM)
    g_x, g_winT, d_norm_mix = in_proj_bwd(dq, dk, dv, dcb, dcv, dqm, proj, conv_w8, xn, x2d, dx1, norm_mix, winT, S, TM)
    g_wmkv, d_norm_mem, d_mkg = mem_kv_bwd(dkm, dvm, kv, memn, mem2d, norm_mem, mem_k_norm, wmkv)

    big = [g_winT, g_wgT, g_wuT, g_wd, g_wout, g_wmkv]
    halves = [g.reshape(4, 2, g.shape[0] // 8, g.shape[1]) for g in big]
    stage1 = exchange_halves(halves)
    partial = add_halves(ci.reshape(1).astype(jnp.int32), halves, stage1)
    stage2 = scatter_partials(partial)
    mine = add_chips(stage2)
    full = join_halves(mine)
    r_winT, r_wgT, r_wuT, r_wd, r_wout, r_wmkv = [f.reshape(2 * f.shape[1], f.shape[2]) for f in full]
    big_grads = {"w_in": r_winT.T, "w_gate": r_wgT.T, "w_up": r_wuT.T, "w_down": r_wd, "w_out": r_wout, "w_mem_kv": r_wmkv}

    small = _pack_small(d_norm_mix, d_norm_mem, d_norm_ffn, d_gains, d_cw8[:3].reshape(1, 3 * CONV_W), d_cbias,
                        d_qg, d_kg, d_mqg, d_mkg, d_sink8[0:1, :N_Q], loss8[0:1, 0:1])
    tot = allreduce_small(small)
    loss = tot[5, 4 * HD + 8]
    g_cw = lax.dynamic_slice(tot[4, :3 * CONV_W].reshape(3, CONV_W), (0, chip * HD), (3, HD))
    small_g = {"norm_mix": tot[0:1], "norm_mem": tot[1:2], "norm_ffn": tot[2:3],
               "out_norm_attn": tot[3:4, :ATT_W], "out_norm_conv": tot[3:4, ATT_W:ATT_W + CONV_W],
               "out_norm_mem": tot[3:4, ATT_W + CONV_W:], "conv_w": g_cw[None], "conv_b": tot[4:5, 768:],
               "q_norm": tot[5:6, 0:HD], "k_norm": tot[5:6, HD:2 * HD], "mem_q_norm": tot[5:6, 2 * HD:3 * HD],
               "mem_k_norm": tot[5:6, 3 * HD:4 * HD], "attn_sinks": tot[5:6, 4 * HD:4 * HD + N_Q]}

    def pack(d):
        return _pack_small(d["norm_mix"], d["norm_mem"], d["norm_ffn"],
                           jnp.concatenate([d["out_norm_attn"], d["out_norm_conv"], d["out_norm_mem"]], axis=1),
                           d["conv_w"].reshape(1, 3 * HD), d["conv_b"], d["q_norm"], d["k_norm"], d["mem_q_norm"],
                           d["mem_k_norm"], d["attn_sinks"], jnp.zeros((1, 1), f32))

    def unpack(p):
        return {"norm_mix": p[0:1], "norm_mem": p[1:2], "norm_ffn": p[2:3], "out_norm_attn": p[3:4, :ATT_W],
                "out_norm_conv": p[3:4, ATT_W:ATT_W + CONV_W], "out_norm_mem": p[3:4, ATT_W + CONV_W:],
                "conv_w": p[4, :3 * HD].reshape(1, 3, HD), "conv_b": p[4:5, 768:], "q_norm": p[5:6, 0:HD],
                "k_norm": p[5:6, HD:2 * HD], "mem_q_norm": p[5:6, 2 * HD:3 * HD], "mem_k_norm": p[5:6, 3 * HD:4 * HD],
                "attn_sinks": p[5:6, 4 * HD:4 * HD + N_Q]}

    w_small = dict(norm_mix=norm_mix, norm_mem=norm_mem, norm_ffn=norm_ffn, out_norm_attn=out_norm_attn,
                   out_norm_conv=out_norm_conv, out_norm_mem=out_norm_mem, conv_w=conv_w, conv_b=conv_b, q_norm=q_norm,
                   k_norm=k_norm, mem_q_norm=mem_q_norm, mem_k_norm=mem_k_norm, attn_sinks=attn_sinks)
    m_small = dict(norm_mix=m_norm_mix, norm_mem=m_norm_mem, norm_ffn=m_norm_ffn, out_norm_attn=m_out_norm_attn,
                   out_norm_conv=m_out_norm_conv, out_norm_mem=m_out_norm_mem, conv_w=m_conv_w, conv_b=m_conv_b,
                   q_norm=m_q_norm, k_norm=m_k_norm, mem_q_norm=m_mem_q_norm, mem_k_norm=m_mem_k_norm,
                   attn_sinks=m_attn_sinks)
    v_small = dict(norm_mix=v_norm_mix, norm_mem=v_norm_mem, norm_ffn=v_norm_ffn, out_norm_attn=v_out_norm_attn,
                   out_norm_conv=v_out_norm_conv, out_norm_mem=v_out_norm_mem, conv_w=v_conv_w, conv_b=v_conv_b,
                   q_norm=v_q_norm, k_norm=v_k_norm, mem_q_norm=v_mem_q_norm, mem_k_norm=v_mem_k_norm,
                   attn_sinks=v_attn_sinks)
    sd, sm, sv = adamw(pack(w_small), pack(small_g), pack(m_small), pack(v_small), "adamw_small", nrow=1)
    sd, sm, sv = unpack(sd), unpack(sm), unpack(sv)

    w_big = dict(w_in=(w_in, m_w_in, v_w_in), w_gate=(w_gate, m_w_gate, v_w_gate), w_up=(w_up, m_w_up, v_w_up),
                 w_down=(w_down, m_w_down, v_w_down), w_out=(w_out, m_w_out, v_w_out),
                 w_mem_kv=(w_mem_kv, m_w_mem_kv, v_w_mem_kv))
    grads, deltas, new_m, new_v = {}, {}, {}, {}
    for name, (w, m, v) in w_big.items():
        g = big_grads[name]
        d, mo, vo = adamw(w[0], g, m[0], v[0], "adamw_" + name)
        grads[name], deltas[name], new_m[name], new_v[name] = g[None], d[None], mo[None], vo[None]
    for name in w_small:
        grads[name], deltas[name], new_m[name], new_v[name] = small_g[name], sd[name], sm[name], sv[name]

    order = ["norm_mix", "w_in", "q_norm", "k_norm", "attn_sinks", "conv_w", "conv_b", "norm_mem", "w_mem_kv",
             "mem_q_norm", "mem_k_norm", "out_norm_attn", "out_norm_conv", "out_norm_mem", "w_out", "norm_ffn",
             "w_gate", "w_up", "w_down"]
    return (loss, g_x.reshape(BL, S, D), *[grads[n] for n in order], *[deltas[n] for n in order],
            *[new_m[n] for n in order], *[new_v[n] for n in order])
```

```python
import collections
import functools

import jax
import jax.numpy as jnp
from jax import lax
from jax.experimental import pallas as pl
from jax.experimental.pallas import tpu as pltpu

f32 = jnp.float32
MXU = jnp.bfloat16
WIRE = jnp.bfloat16
EPS = 1e-6
NEG = -1e30
HD = 64
BLK = 128
N_Q, N_KV, N_MEMH = 8, 2, 4
GQA = N_Q // N_KV
ATT_W, KV_W, CONV_W, MEM_W = 512, 128, 256, 256
VMEM_MIB = 1024 * 1024
ADAM_LR, ADAM_B1, ADAM_B2, ADAM_EPS, ADAM_WD, ADAM_STEP = 0.001, 0.9, 0.999, 1e-08, 0.01, 10

MESH = pl.DeviceIdType.MESH
VM = pl.BlockSpec(memory_space=pltpu.VMEM)
ANY = pl.BlockSpec(memory_space=pl.ANY)
SDS = jax.ShapeDtypeStruct
DMA = pltpu.SemaphoreType.DMA


def _c(v):
    return v.astype(MXU)


def _nn(a, b):
    return lax.dot_general(a, b, (((1,), (0,)), ((), ())), preferred_element_type=f32)


def _nt(a, b):
    return lax.dot_general(a, b, (((1,), (1,)), ((), ())), preferred_element_type=f32)


def _tn(a, b):
    return lax.dot_general(a, b, (((0,), (0,)), ((), ())), preferred_element_type=f32)


def _rstd(v):
    return lax.rsqrt(jnp.mean(v * v, axis=-1, keepdims=True) + EPS)


def _norm_bwd(dy, v, r, g):
    dyg = dy * g
    dv = r * dyg - v * (r * r * r) * jnp.mean(dyg * v, axis=-1, keepdims=True)
    return dv, jnp.sum(dy * v * r, axis=0, keepdims=True)


def _softmax_rows(s, extra=None):
    m = jnp.max(s, axis=-1, keepdims=True)
    if extra is not None:
        m = jnp.maximum(m, extra)
    p = jnp.exp(s - m)
    den = jnp.sum(p, axis=-1, keepdims=True)
    if extra is None:
        return p * (1.0 / den), None
    pe = jnp.exp(extra - m)
    inv = 1.0 / (den + pe)
    return p * inv, pe * inv


def _place():
    return lax.axis_index("x"), lax.axis_index("y"), lax.axis_index("c")


def _other_chips(x, y):
    return [(1 - x, y), (x, 1 - y), (1 - x, 1 - y)]


Exchange = collections.namedtuple("Exchange", "ins outs sems start finish")


def _run(name, body, grid, ins, in_specs, out_shape, out_specs, scratch=(), vmem_mib=32, exchange=None):
    ins, in_specs, out_shape, out_specs, scratch = list(ins), list(in_specs), list(out_shape), list(out_specs), list(scratch)
    ni, no, ns = len(ins), len(out_shape), len(scratch)
    ex = exchange
    if ex is not None:
        nxi, nxo = len(ex.ins), len(ex.outs)

    def call_body(*refs):
        if ex is None:
            body(*refs)
            return
        a, xa = refs[:ni], refs[ni:ni + nxi]
        o, xo = refs[ni + nxi:ni + nxi + no], refs[ni + nxi + no:ni + nxi + no + nxo]
        s, xs = refs[ni + nxi + no + nxo:ni + nxi + no + nxo + ns], refs[ni + nxi + no + nxo + ns:]
        if grid:
            first = functools.reduce(jnp.logical_and, [pl.program_id(d) == 0 for d in range(len(grid))])
            last = functools.reduce(jnp.logical_and, [pl.program_id(d) == grid[d] - 1 for d in range(len(grid))])
            pl.when(first)(lambda: ex.start(xa, xo, xs))
            body(*a, *o, *s)
            pl.when(last)(lambda: ex.finish(xa, xo, xs))
        else:
            ex.start(xa, xo, xs)
            if body is not None:
                body(*a, *o, *s)
            ex.finish(xa, xo, xs)

    if ex is not None:
        ins, in_specs = ins + list(ex.ins), in_specs + [ANY] * nxi
        out_shape, out_specs = out_shape + list(ex.outs), out_specs + [ANY] * nxo
        scratch = scratch + list(ex.sems)
    kw = dict(grid=grid) if grid else {}
    res = pl.pallas_call(
        call_body, name=name, out_shape=out_shape, in_specs=in_specs, out_specs=out_specs, scratch_shapes=scratch,
        compiler_params=pltpu.CompilerParams(dimension_semantics=("arbitrary",) * len(grid) if grid else None,
                                             vmem_limit_bytes=vmem_mib * VMEM_MIB), **kw)(*ins)
    res = list(res)
    return (res[:no], res[no:]) if ex is not None else res


def _remote(src, dst, ssem, rsem, dev):
    return pltpu.make_async_remote_copy(src_ref=src, dst_ref=dst, send_sem=ssem, recv_sem=rsem,
                                        device_id=dev, device_id_type=MESH)


def gather_exchange(shards, split):
    n = len(shards)

    def rows(ref, e, kk, half=None):
        R = shards[e].shape[0]
        if half is None:
            return ref.at[pl.ds(pl.multiple_of(kk * R, 8), R)]
        return ref.at[pl.ds(pl.multiple_of(kk * R + half * (R // 2), 8), R // 2)]

    def ici(src, dst, sm, e, j, chip_j, x, y, c):
        k = 2 * x + y
        if split[e]:
            s = src[e].at[pl.ds(pl.multiple_of(c * (shards[e].shape[0] // 2), 8), shards[e].shape[0] // 2)]
            return _remote(s, rows(dst[e], e, k, c), sm[0].at[6 * e + j], sm[1].at[6 * e + j], (*chip_j, c))
        return _remote(src[e], rows(dst[e], e, k), sm[0].at[6 * e + j], sm[1].at[6 * e + j], (*chip_j, c))

    def landed(dst, e, chip_j, c):
        kj = 2 * chip_j[0] + chip_j[1]
        return rows(dst[e], e, kj, c) if split[e] else rows(dst[e], e, kj)

    def forward(dst, sm, e, j, chip_j, x, y, c, sender_c):
        kj = 2 * chip_j[0] + chip_j[1]
        r = rows(dst[e], e, kj, sender_c)
        return _remote(r, r, sm[0].at[6 * e + 3 + j], sm[1].at[6 * e + 3 + j], (x, y, 1 - c))

    def local(src, dst, sm, e, x, y):
        return pltpu.make_async_copy(src[e], rows(dst[e], e, 2 * x + y), sm[2].at[e])

    def start(src, dst, sm):
        x, y, c = _place()
        for e in range(n):
            local(src, dst, sm, e, x, y).start()
            for j, chip_j in enumerate(_other_chips(x, y)):
                ici(src, dst, sm, e, j, chip_j, x, y, c).start()

    def finish(src, dst, sm):
        x, y, c = _place()
        chips = _other_chips(x, y)
        for e in range(n):
            for j, chip_j in enumerate(chips):
                r = landed(dst, e, chip_j, c)
                _remote(r, r, sm[0].at[6 * e + j], sm[1].at[6 * e + j], (*chip_j, c)).wait_recv()
                if split[e]:
                    forward(dst, sm, e, j, chip_j, x, y, c, c).start()
        for e in range(n):
            for j, chip_j in enumerate(chips):
                if split[e]:
                    forward(dst, sm, e, j, chip_j, x, y, c, 1 - c).wait_recv()
        for e in range(n):
            for j, chip_j in enumerate(chips):
                ici(src, dst, sm, e, j, chip_j, x, y, c).wait_send()
                if split[e]:
                    forward(dst, sm, e, j, chip_j, x, y, c, c).wait_send()
            local(src, dst, sm, e, x, y).wait()

    outs = [SDS((4 * s.shape[0], s.shape[1]), s.dtype) for s in shards]
    return Exchange(list(shards), outs, [DMA((6 * n,)), DMA((6 * n,)), DMA((n,))], start, finish)


def halves_exchange(grads):
    n = len(grads)

    def copy(g, st, sm, e, x, y, c):
        return _remote(g[e].at[:, 1 - c], st[e], sm[0].at[e], sm[1].at[e], (x, y, 1 - c))

    def start(g, st, sm):
        x, y, c = _place()
        for e in range(n):
            copy(g, st, sm, e, x, y, c).start()

    def finish(g, st, sm):
        x, y, c = _place()
        for e in range(n):
            copy(g, st, sm, e, x, y, c).wait()

    outs = [SDS((4,) + a.shape[2:], a.dtype) for a in grads]
    return Exchange(list(grads), outs, [DMA((n,)), DMA((n,))], start, finish)


def scatter_exchange(parts):
    n = len(parts)

    def ici(p, st, sm, e, j, chip_j, x, y, c):
        k, kj = 2 * x + y, 2 * chip_j[0] + chip_j[1]
        return _remote(p[e].at[kj], st[e].at[c, k], sm[0].at[8 * e + j], sm[1].at[8 * e + j], (*chip_j, c))

    def own(p, st, sm, e, x, y, c):
        k = 2 * x + y
        return _remote(p[e].at[k], st[e].at[c, k], sm[0].at[8 * e + 3], sm[1].at[8 * e + 3], (x, y, 1 - c))

    def forward(st, sm, e, j, chip_j, x, y, c, sender_c):
        kj = 2 * chip_j[0] + chip_j[1]
        r = st[e].at[sender_c, kj]
        return _remote(r, r, sm[0].at[8 * e + 4 + j], sm[1].at[8 * e + 4 + j], (x, y, 1 - c))

    def local(p, st, sm, e, x, y, c):
        k = 2 * x + y
        return pltpu.make_async_copy(p[e].at[k], st[e].at[c, k], sm[2].at[e])

    def start(p, st, sm):
        x, y, c = _place()
        for e in range(n):
            local(p, st, sm, e, x, y, c).start()
            own(p, st, sm, e, x, y, c).start()
            for j, chip_j in enumerate(_other_chips(x, y)):
                ici(p, st, sm, e, j, chip_j, x, y, c).start()

    def finish(p, st, sm):
        x, y, c = _place()
        k = 2 * x + y
        chips = _other_chips(x, y)
        for e in range(n):
            for j, chip_j in enumerate(chips):
                kj = 2 * chip_j[0] + chip_j[1]
                r = st[e].at[c, kj]
                _remote(r, r, sm[0].at[8 * e + j], sm[1].at[8 * e + j], (*chip_j, c)).wait_recv()
                forward(st, sm, e, j, chip_j, x, y, c, c).start()
        for e in range(n):
            r = st[e].at[1 - c, k]
            _remote(r, r, sm[0].at[8 * e + 3], sm[1].at[8 * e + 3], (x, y, 1 - c)).wait_recv()
            for j, chip_j in enumerate(chips):
                forward(st, sm, e, j, chip_j, x, y, c, 1 - c).wait_recv()
        for e in range(n):
            own(p, st, sm, e, x, y, c).wait_send()
            for j, chip_j in enumerate(chips):
                ici(p, st, sm, e, j, chip_j, x, y, c).wait_send()
                forward(st, sm, e, j, chip_j, x, y, c, c).wait_send()
            local(p, st, sm, e, x, y, c).wait()

    outs = [SDS((2,) + a.shape, a.dtype) for a in parts]
    return Exchange(list(parts), outs, [DMA((8 * n,)), DMA((8 * n,)), DMA((n,))], start, finish)


def allreduce_small(d_norm_mix, d_norm_mem, d_norm_ffn, d_gains, d_cw8, d_cbias, d_qg, d_kg, d_mqg, d_mkg, d_sink8, loss8):
    def body(nm_ref, nmem_ref, nf_ref, gn_ref, cw_ref, cb_ref, qg_ref, kg_ref, mqg_ref, mkg_ref, sk_ref, ls_ref,
             o_ref, buf, ssem, rsem):
        x, y, c = _place()
        me = 4 * x + 2 * y + c
        mine = buf.at[me]
        mine[...] = jnp.zeros((8, 1024), f32)
        mine[0:1, :] = nm_ref[...]
        mine[1:2, :] = nmem_ref[...]
        mine[2:3, :] = nf_ref[...]
        mine[3:4, :] = gn_ref[...]
        for j in range(3):
            mine[4:5, pl.ds(j * CONV_W, CONV_W)] = cw_ref[j:j + 1, :]
        mine[4:5, pl.ds(3 * CONV_W, CONV_W)] = cb_ref[...]
        for j, r in enumerate((qg_ref, kg_ref, mqg_ref, mkg_ref)):
            mine[5:6, pl.ds(j * HD, HD)] = r[...]
        mine[5:6, pl.ds(256, 128)] = sk_ref[0:1, :]
        mine[5:6, pl.ds(384, 128)] = ls_ref[0:1, :]

        def peer_of(m):
            return (1 - x if m & 4 else x, 1 - y if m & 2 else y, 1 - c if m & 1 else c)

        for m in range(1, 8):
            _remote(mine, mine, ssem.at[m - 1], rsem.at[m - 1], peer_of(m)).start()
        for m in range(1, 8):
            p = peer_of(m)
            got = buf.at[4 * p[0] + 2 * p[1] + p[2]]
            _remote(got, got, ssem.at[m - 1], rsem.at[m - 1], p).wait_recv()
        for m in range(1, 8):
            _remote(mine, mine, ssem.at[m - 1], rsem.at[m - 1], peer_of(m)).wait_send()
        acc = buf[0]
        for d in range(1, 8):
            acc = acc + buf[d]
        o_ref[...] = acc

    ins = [d_norm_mix, d_norm_mem, d_norm_ffn, d_gains, d_cw8, d_cbias, d_qg, d_kg, d_mqg, d_mkg, d_sink8, loss8]
    return _run("allreduce_small", body, (), ins, [VM] * len(ins), [SDS((8, 1024), f32)], [VM],
                scratch=[pltpu.VMEM((8, 8, 1024), f32), DMA((7,)), DMA((7,))])[0]


def add_halves(cidx, grads, stages, name, nch=2):
    n = len(grads)

    def body(c_ref, *refs):
        g, st, o = refs[:n], refs[n:2 * n], refs[2 * n:]
        for e in range(n):
            o[e][...] = (g[e][...] + st[e][...]).astype(WIRE)

    in_specs, out_specs, out_shape = [], [], []
    for a in grads:
        hr, C = a.shape[2], a.shape[3]
        in_specs.append(pl.BlockSpec((None, None, hr // nch, C), lambda s, q, c_ref: (s, c_ref[0], q, 0)))
    for a in stages:
        hr, C = a.shape[1], a.shape[2]
        in_specs.append(pl.BlockSpec((None, hr // nch, C), lambda s, q, c_ref: (s, q, 0)))
        out_specs.append(pl.BlockSpec((None, hr // nch, C), lambda s, q, c_ref: (s, q, 0)))
        out_shape.append(SDS(a.shape, WIRE))
    return pl.pallas_call(
        body, name=name, out_shape=out_shape,
        grid_spec=pltpu.PrefetchScalarGridSpec(num_scalar_prefetch=1, grid=(4, nch), in_specs=in_specs, out_specs=out_specs),
        compiler_params=pltpu.CompilerParams(dimension_semantics=("arbitrary", "arbitrary")),
    )(cidx, *grads, *stages)


def _adamw_math(w, g, m, v):
    m = ADAM_B1 * m + (1.0 - ADAM_B1) * g
    v = ADAM_B2 * v + (1.0 - ADAM_B2) * (g * g)
    m_hat = m / (1.0 - ADAM_B1 ** ADAM_STEP)
    v_hat = v / (1.0 - ADAM_B2 ** ADAM_STEP)
    delta = -ADAM_LR * (m_hat / (jnp.sqrt(v_hat) + ADAM_EPS) + ADAM_WD * w)
    return delta, m, v


def _sum_chips(st):
    return ((st[0].astype(f32) + st[1].astype(f32)) + st[2].astype(f32)) + st[3].astype(f32)


def adamw_big(name, stages, ws, ms, vs, transposed, nstep):
    n = len(stages)

    def body(*refs):
        st, w, m, v = refs[:n], refs[n:2 * n], refs[2 * n:3 * n], refs[3 * n:4 * n]
        outs = refs[4 * n:]
        for e in range(n):
            if transposed:
                g = jnp.concatenate([_sum_chips(st[e].at[0]), _sum_chips(st[e].at[1])], axis=0).T
            else:
                g = _sum_chips(st[e])
            d, mm, vv = _adamw_math(w[e][...], g, m[e][...], v[e][...])
            outs[4 * e][...] = g
            outs[4 * e + 1][...] = d
            outs[4 * e + 2][...] = mm
            outs[4 * e + 3][...] = vv

    st_specs, w_specs = [], []
    half = nstep // 2
    for e in range(n):
        _, _, hr, C = stages[e].shape
        if transposed:
            st_specs.append(pl.BlockSpec((2, 4, hr, C // nstep), lambda i: (0, 0, 0, i)))
            w_specs.append(pl.BlockSpec((C // nstep, 2 * hr), lambda i: (i, 0)))
        else:
            st_specs.append(pl.BlockSpec((None, 4, hr // half, C), lambda i: (i // half, 0, i % half, 0)))
            w_specs.append(pl.BlockSpec((hr // half, C), lambda i: (i, 0)))
    out_specs = [s for s in w_specs for _ in range(4)]
    out_shape = [SDS(w.shape, f32) for w in ws for _ in range(4)]
    res = _run(name, body, (nstep,), list(stages) + list(ws) + list(ms) + list(vs), st_specs + w_specs * 3,
               out_shape, out_specs, vmem_mib=48)
    return [res[4 * e:4 * e + 4] for e in range(n)]


SMALL = ("norm_mix", "norm_mem", "norm_ffn", "out_norm_attn", "out_norm_conv", "out_norm_mem", "conv_w", "conv_b",
         "q_norm", "k_norm", "mem_q_norm", "mem_k_norm", "attn_sinks")


def adamw_small(tot, ws, ms, vs):
    ns = len(SMALL)

    def grad_of(name, tot_ref, chip):
        where = {"norm_mix": (0, 0, 1024), "norm_mem": (1, 0, 1024), "norm_ffn": (2, 0, 1024),
                 "out_norm_attn": (3, 0, ATT_W), "out_norm_conv": (3, ATT_W, CONV_W),
                 "out_norm_mem": (3, ATT_W + CONV_W, MEM_W), "conv_b": (4, 3 * CONV_W, CONV_W), "q_norm": (5, 0, HD),
                 "k_norm": (5, HD, HD), "mem_q_norm": (5, 2 * HD, HD), "mem_k_norm": (5, 3 * HD, HD),
                 "attn_sinks": (5, 256, N_Q)}
        if name != "conv_w":
            r, c0, w = where[name]
            return tot_ref[r:r + 1, c0:c0 + w]
        taps = []
        for j in range(3):
            mine = tot_ref[4:5, j * CONV_W:j * CONV_W + HD]
            for s in range(1, 4):
                mine = jnp.where(chip == s, tot_ref[4:5, j * CONV_W + s * HD:j * CONV_W + (s + 1) * HD], mine)
            taps.append(mine)
        return jnp.concatenate(taps, axis=0)[None]

    def body(tot_ref, *refs):
        w, m, v, outs = refs[:ns], refs[ns:2 * ns], refs[2 * ns:3 * ns], refs[3 * ns:]
        x, y, _ = _place()
        for i, name in enumerate(SMALL):
            g = grad_of(name, tot_ref, 2 * x + y)
            d, mm, vv = _adamw_math(w[i][...], g, m[i][...], v[i][...])
            outs[4 * i][...] = g
            outs[4 * i + 1][...] = d
            outs[4 * i + 2][...] = mm
            outs[4 * i + 3][...] = vv

    ins = [tot] + [d[k] for d in (ws, ms, vs) for k in SMALL]
    out_shape = [SDS(ws[k].shape, f32) for k in SMALL for _ in range(4)]
    res = _run("adamw_small", body, (), ins, [VM] * len(ins), out_shape, [VM] * len(out_shape))
    return {k: res[4 * i:4 * i + 4] for i, k in enumerate(SMALL)}


def prep_weights(w_in, w_gate, w_up, w_down, w_out, w_mem_kv):
    def body(win_ref, wg_ref, wu_ref, wd_ref, wo_ref, wm_ref, winT_ref, wgT_ref, wuT_ref, wdb_ref, wob_ref, wmb_ref):
        winT_ref[...] = _c(win_ref[...].T)
        wgT_ref[...] = _c(wg_ref[...].T)
        wuT_ref[...] = _c(wu_ref[...].T)
        wdb_ref[...] = _c(wd_ref[...])
        wob_ref[...] = _c(wo_ref[...])
        wmb_ref[...] = _c(wm_ref[...])

    ins = [w_in, w_gate, w_up, w_down, w_out, w_mem_kv]
    out_shape = [SDS(a.shape[::-1], MXU) for a in ins[:3]] + [SDS(a.shape, MXU) for a in ins[3:]]
    return _run("prep_weights", body, (), ins, [VM] * 6, out_shape, [VM] * 6, vmem_mib=48)


def mem_kv_fwd(mem2d, g_mem, mkg, wmkv):
    M, D = mem2d.shape

    def body(m_ref, g_ref, kg_ref, w_ref, mn_ref, kv_ref, km_ref, vm_ref):
        m = m_ref[...]
        mn = _c(m * _rstd(m) * g_ref[...])
        mn_ref[...] = mn
        kv = _nn(mn, w_ref[...])
        kv_ref[...] = kv
        for h in range(N_MEMH):
            kh = kv[:, h * HD:(h + 1) * HD]
            km_ref[:, pl.ds(h * HD, HD)] = kh * _rstd(kh) * kg_ref[...]
        vm_ref[...] = kv[:, MEM_W:]

    return _run("mem_kv_fwd", body, (), [mem2d, g_mem, mkg, wmkv], [VM] * 4,
                [SDS((M, D), MXU), SDS((M, 2 * MEM_W), f32), SDS((M, MEM_W), f32), SDS((M, MEM_W), f32)], [VM] * 4)


def in_proj_fwd(x2d, g1, winT, tm, exchange):
    T, D = x2d.shape
    P = winT.shape[0]

    def body(x_ref, g_ref, w_ref, xn_ref, proj_ref):
        xv = x_ref[...]
        xn = _c(xv * _rstd(xv) * g_ref[...])
        xn_ref[...] = xn
        proj_ref[...] = _nt(xn, w_ref[...])

    return _run("in_proj_fwd", body, (T // tm,), [x2d, g1, winT],
                [pl.BlockSpec((tm, D), lambda i: (i, 0)), pl.BlockSpec((1, D), lambda i: (0, 0)), VM],
                [SDS((T, D), MXU), SDS((T, P), f32)],
                [pl.BlockSpec((tm, D), lambda i: (i, 0)), pl.BlockSpec((tm, P), lambda i: (i, 0))],
                vmem_mib=40, exchange=exchange)


def _swa_setup(j, g, sk_ref):
    rows = GQA * BLK
    ri = lax.broadcasted_iota(jnp.int32, (rows, 2 * BLK), 0)
    ki = lax.broadcasted_iota(jnp.int32, (rows, 2 * BLK), 1)
    dist = (ri & (BLK - 1)) + BLK - ki
    valid = (dist >= 0) & (dist < BLK) & ((ki >= BLK) | (j > 0))
    hrow = lax.broadcasted_iota(jnp.int32, (rows, 1), 0) // BLK
    slope = jnp.zeros((rows, 1), f32)
    sink = jnp.zeros((rows, 1), f32)
    for hh in range(GQA):
        h = g * GQA + hh
        slope = jnp.where(hrow == hh, 2.0 ** -(h + 1), slope)
        sink = jnp.where(hrow == hh, sk_ref[h:h + 1, 0:1], sink)
    return valid, slope * dist.astype(f32), sink


def _stack_heads(v, g):
    return jnp.concatenate([v[:, (g * GQA + hh) * HD:(g * GQA + hh + 1) * HD] for hh in range(GQA)], axis=0)


def attn_fwd(proj, qg, kg, sink_rows, BL, S, exchange):
    NB = S // BLK
    T = BL * S

    def body(q_ref, kc_ref, kp_ref, vc_ref, vp_ref, qg_ref, kg_ref, sk_ref, o_ref):
        j = pl.program_id(1)
        q = q_ref[...]
        k2 = jnp.concatenate([kp_ref[...], kc_ref[...]], axis=0)
        v2 = jnp.concatenate([vp_ref[...], vc_ref[...]], axis=0)
        for g in range(N_KV):
            valid, bias, sink = _swa_setup(j, g, sk_ref)
            kh = k2[:, g * HD:(g + 1) * HD]
            kn = _c(kh * _rstd(kh) * kg_ref[...])
            vh = _c(v2[:, g * HD:(g + 1) * HD])
            qs = _stack_heads(q, g)
            qn = _c(qs * _rstd(qs) * qg_ref[...])
            s = jnp.where(valid, _nt(qn, kn) * (HD ** -0.5) - bias, NEG)
            p, _ = _softmax_rows(s, sink)
            o = _nn(_c(p), vh)
            for hh in range(GQA):
                o_ref[:, pl.ds((g * GQA + hh) * HD, HD)] = o[hh * BLK:(hh + 1) * BLK]

    cur = lambda col: (lambda b, j: (b * NB + j, col))
    prev = lambda col: (lambda b, j: (b * NB + jnp.maximum(j - 1, 0), col))
    small = lambda shape: pl.BlockSpec(shape, lambda b, j: (0, 0))
    return _run("attn_fwd", body, (BL, NB), [proj, proj, proj, proj, proj, qg, kg, sink_rows],
                [pl.BlockSpec((BLK, ATT_W), cur(0)),
                 pl.BlockSpec((BLK, KV_W), cur(4)), pl.BlockSpec((BLK, KV_W), prev(4)),
                 pl.BlockSpec((BLK, KV_W), cur(5)), pl.BlockSpec((BLK, KV_W), prev(5)),
                 small((1, HD)), small((1, HD)), small((8, 128))],
                [SDS((T, ATT_W), f32)], [pl.BlockSpec((BLK, ATT_W), cur(0))], exchange=exchange)


def _conv_taps(u, uh):
    row = lax.broadcasted_iota(jnp.int32, u.shape, 0)
    u1 = jnp.where(row == 0, uh[7:8, :], pltpu.roll(u, 1, 0))
    u2 = jnp.where(row == 0, uh[6:7, :], jnp.where(row == 1, uh[7:8, :], pltpu.roll(u, 2, 0)))
    return u1, u2


def _mem_heads(qm, km, vm, qg, h):
    qh = qm[:, h * HD:(h + 1) * HD]
    r = _rstd(qh)
    qn = qh * r * qg
    kh = _c(km[:, h * HD:(h + 1) * HD])
    vh = _c(vm[:, h * HD:(h + 1) * HD])
    p, _ = _softmax_rows(_nt(_c(qn), kh) * (HD ** -0.5))
    return qh, r, qn, kh, vh, p


def mixer_tail_fwd(x2d, attn_out, proj, km, vm, conv_w8, conv_b, g_a, g_c, g_m, mqg, wout, g_f, S, tm, exchange):
    T, D = x2d.shape
    NM = km.shape[0] // (T // S)

    def body(x_ref, ao_ref, ch_ref, cb_ref, cc_ref, chh_ref, cch_ref, qm_ref, km_ref, vm_ref, cw_ref, cbias_ref,
             ga_ref, gc_ref, gm_ref, mqg_ref, wout_ref, gf_ref, co_ref, mo_ref, mg_ref, x1_ref, h_ref):
        first = (pl.program_id(0) * tm) % S == 0
        u = cc_ref[...] * ch_ref[...]
        uh = jnp.where(first, 0.0, cch_ref[...] * chh_ref[...])
        u1, u2 = _conv_taps(u, uh)
        conv = cw_ref[0:1, :] * u2 + cw_ref[1:2, :] * u1 + cw_ref[2:3, :] * u + cbias_ref[...]
        conv_out = cb_ref[...] * conv
        co_ref[...] = conv_out
        qm, kmv, vmv = qm_ref[...], km_ref[...], vm_ref[...]
        for h in range(N_MEMH):
            _, _, _, _, vh, p = _mem_heads(qm, kmv, vmv, mqg_ref[...], h)
            mo_ref[:, pl.ds(h * HD, HD)] = _nn(_c(p), vh)
        mem_out = mo_ref[...]
        ao = ao_ref[...]
        merged = _c(jnp.concatenate([ao * _rstd(ao) * ga_ref[...], conv_out * _rstd(conv_out) * gc_ref[...],
                                     mem_out * _rstd(mem_out) * gm_ref[...]], axis=1))
        mg_ref[...] = merged
        x1 = x_ref[...] + _nn(merged, wout_ref[...])
        x1_ref[...] = x1
        h_ref[...] = _c(x1 * _rstd(x1) * gf_ref[...])

    tile = lambda w, col: pl.BlockSpec((tm, w), lambda i: (i, col))
    halo = lambda col: pl.BlockSpec((8, CONV_W), lambda i: (jnp.maximum(i * (tm // 8) - 1, 0), col))
    seq = pl.BlockSpec((NM, MEM_W), lambda i: ((i * tm) // S, 0))
    small = lambda a: pl.BlockSpec(a.shape, lambda i: (0, 0))
    return _run("mixer_tail_fwd", body, (T // tm,),
                [x2d, attn_out, proj, proj, proj, proj, proj, proj, km, vm, conv_w8, conv_b, g_a, g_c, g_m, mqg, wout, g_f],
                [tile(D, 0), tile(ATT_W, 0), tile(CONV_W, 3), tile(CONV_W, 4), tile(CONV_W, 5), halo(3), halo(5),
                 tile(MEM_W, 6), seq, seq, small(conv_w8), small(conv_b), small(g_a), small(g_c), small(g_m),
                 small(mqg), VM, small(g_f)],
                [SDS((T, CONV_W), f32), SDS((T, MEM_W), f32), SDS((T, D), MXU), SDS((T, D), f32), SDS((T, D), MXU)],
                [tile(CONV_W, 0), tile(MEM_W, 0), tile(D, 0), tile(D, 0), tile(D, 0)], vmem_mib=40, exchange=exchange)


def ffn_fwd_bwd(h, x1, tgt, wgT, wuT, wd, g_f, tm):
    T, D = x1.shape
    F = wd.shape[0]

    def body(h_ref, x1_ref, t_ref, wg_ref, wu_ref, wd_ref, gf_ref,
             dx1_ref, dx2_ref, act_ref, dg_ref, du_ref, loss_ref, dgf_ref):
        @pl.when(pl.program_id(0) == 0)
        def _():
            loss_ref[...] = jnp.zeros_like(loss_ref)
            dgf_ref[...] = jnp.zeros_like(dgf_ref)

        hv = h_ref[...]
        gate = _nt(hv, wg_ref[...])
        up = _nt(hv, wu_ref[...])
        sg = jax.nn.sigmoid(gate)
        sl = gate * sg
        act = _c(sl * up)
        act_ref[...] = act
        x1v = x1_ref[...]
        diff = (x1v + _nn(act, wd_ref[...])) - t_ref[...]
        loss_ref[...] += 0.5 * jnp.sum(jnp.sum(diff * diff, axis=-1, keepdims=True) / D, axis=0, keepdims=True)
        dx2 = diff / D
        dx2b = _c(dx2)
        dx2_ref[...] = dx2b
        d_act = _nt(dx2b, wd_ref[...])
        d_up = _c(d_act * sl)
        d_gate = _c(d_act * up * (sg * (1.0 + gate * (1.0 - sg))))
        du_ref[...] = d_up
        dg_ref[...] = d_gate
        dh = _nn(d_gate, wg_ref[...]) + _nn(d_up, wu_ref[...])
        dv, dgf = _norm_bwd(dh, x1v, _rstd(x1v), gf_ref[...])
        dx1_ref[...] = dx2 + dv
        dgf_ref[...] += dgf

    tile = lambda w: pl.BlockSpec((tm, w), lambda i: (i, 0))
    return _run("ffn_fwd_bwd", body, (T // tm,), [h, x1, tgt, wgT, wuT, wd, g_f],
                [tile(D), tile(D), tile(D), VM, VM, VM, pl.BlockSpec((1, D), lambda i: (0, 0))],
                [SDS((T, D), f32), SDS((T, D), MXU), SDS((T, F), MXU), SDS((T, F), MXU), SDS((T, F), MXU),
                 SDS((8, 128), f32), SDS((1, D), f32)],
                [tile(D), tile(D), tile(F), tile(F), tile(F), pl.BlockSpec((8, 128), lambda i: (0, 0)),
                 pl.BlockSpec((1, D), lambda i: (0, 0))], vmem_mib=56)


def matmul_tn(a, b, name, tmo, tk):
    T, M = a.shape
    N = b.shape[1]

    def body(a_ref, b_ref, o_ref):
        @pl.when(pl.program_id(1) == 0)
        def _():
            o_ref[...] = jnp.zeros_like(o_ref)

        o_ref[...] += _tn(a_ref[...], b_ref[...])

    return _run(name, body, (M // tmo, T // tk), [a, b],
                [pl.BlockSpec((tk, tmo), lambda m, k: (k, m)), pl.BlockSpec((tk, N), lambda m, k: (k, 0))],
                [SDS((M, N), f32)], [pl.BlockSpec((tmo, N), lambda m, k: (m, 0))], vmem_mib=48)[0]


def out_proj_bwd(dx1, merged, attn_out, conv_out, mem_out, g_a, g_c, g_m, wout, tm):
    T, D = dx1.shape

    def body(dx1_ref, mg_ref, ao_ref, co_ref, mo_ref, ga_ref, gc_ref, gm_ref, w_ref,
             dao_ref, dco_ref, dmo_ref, dw_ref, dgain_ref):
        @pl.when(pl.program_id(0) == 0)
        def _():
            dw_ref[...] = jnp.zeros_like(dw_ref)
            dgain_ref[...] = jnp.zeros_like(dgain_ref)

        dxb = _c(dx1_ref[...])
        dw_ref[...] += _tn(mg_ref[...], dxb)
        dmg = _nt(dxb, w_ref[...])
        ao, co, mo = ao_ref[...], co_ref[...], mo_ref[...]
        da, ga = _norm_bwd(dmg[:, :ATT_W], ao, _rstd(ao), ga_ref[...])
        dc, gc = _norm_bwd(dmg[:, ATT_W:ATT_W + CONV_W], co, _rstd(co), gc_ref[...])
        dm, gm = _norm_bwd(dmg[:, ATT_W + CONV_W:], mo, _rstd(mo), gm_ref[...])
        dao_ref[...] = da
        dco_ref[...] = dc
        dmo_ref[...] = dm
        dgain_ref[...] += jnp.concatenate([ga, gc, gm], axis=1)

    tile = lambda w: pl.BlockSpec((tm, w), lambda i: (i, 0))
    small = lambda a: pl.BlockSpec(a.shape, lambda i: (0, 0))
    return _run("out_proj_bwd", body, (T // tm,), [dx1, merged, attn_out, conv_out, mem_out, g_a, g_c, g_m, wout],
                [tile(D), tile(D), tile(ATT_W), tile(CONV_W), tile(MEM_W), small(g_a), small(g_c), small(g_m), VM],
                [SDS((T, ATT_W), f32), SDS((T, CONV_W), f32), SDS((T, MEM_W), f32), SDS((D, D), f32), SDS((1, D), f32)],
                [tile(ATT_W), tile(CONV_W), tile(MEM_W), pl.BlockSpec((D, D), lambda i: (0, 0)),
                 pl.BlockSpec((1, D), lambda i: (0, 0))], vmem_mib=40)


def attn_bwd(proj, d_attn, qg, kg, sink_rows, BL, S, exchange):
    NB = S // BLK
    T = BL * S

    def body(q_ref, kc_ref, kp_ref, vc_ref, vp_ref, do_ref, qg_ref, kg_ref, sk_ref,
             dq_ref, dk_ref, dv_ref, dqg_ref, dkg_ref, dsk_ref, pend_k, pend_v, fin_k, fin_v):
        b, j = pl.program_id(0), pl.program_id(1)

        @pl.when((b == 0) & (j == 0))
        def _():
            dqg_ref[...] = jnp.zeros_like(dqg_ref)
            dkg_ref[...] = jnp.zeros_like(dkg_ref)
            dsk_ref[...] = jnp.zeros_like(dsk_ref)

        @pl.when(j == 0)
        def _():
            pend_k[...] = jnp.zeros_like(pend_k)
            pend_v[...] = jnp.zeros_like(pend_v)

        @pl.when(j < NB)
        def _():
            q, do = q_ref[...], do_ref[...]
            k2 = jnp.concatenate([kp_ref[...], kc_ref[...]], axis=0)
            v2 = jnp.concatenate([vp_ref[...], vc_ref[...]], axis=0)
            lane = lax.broadcasted_iota(jnp.int32, (8, 128), 1)
            dqg = jnp.zeros((1, HD), f32)
            dsk = jnp.zeros((8, 128), f32)
            dks, dvs = [], []
            for g in range(N_KV):
                valid, bias, sink = _swa_setup(j, g, sk_ref)
                kh = k2[:, g * HD:(g + 1) * HD]
                kn = _c(kh * _rstd(kh) * kg_ref[...])
                vh = _c(v2[:, g * HD:(g + 1) * HD])
                qs = _stack_heads(q, g)
                r = _rstd(qs)
                qn = _c(qs * r * qg_ref[...])
                s = jnp.where(valid, _nt(qn, kn) * (HD ** -0.5) - bias, NEG)
                p, ps = _softmax_rows(s, sink)
                dos = _c(_stack_heads(do, g))
                dp = _nt(dos, vh)
                delta = jnp.sum(p * dp, axis=-1, keepdims=True)
                ds = _c(p * (dp - delta) * (HD ** -0.5))
                t = ps * delta
                for hh in range(GQA):
                    dsk = dsk + jnp.where(lane == g * GQA + hh, -jnp.sum(t[hh * BLK:(hh + 1) * BLK]), 0.0)
                dvs.append(_tn(_c(p), dos))
                dks.append(_tn(ds, qn))
                dqs, gq = _norm_bwd(_nn(ds, kn), qs, r, qg_ref[...])
                for hh in range(GQA):
                    dq_ref[:, pl.ds((g * GQA + hh) * HD, HD)] = dqs[hh * BLK:(hh + 1) * BLK]
                dqg = dqg + gq
            dk2 = jnp.concatenate(dks, axis=1)
            dv2 = jnp.concatenate(dvs, axis=1)
            fin_k[...] = pend_k[...] + dk2[:BLK]
            fin_v[...] = pend_v[...] + dv2[:BLK]
            pend_k[...] = dk2[BLK:]
            pend_v[...] = dv2[BLK:]
            dqg_ref[...] += dqg
            dsk_ref[...] += dsk

        @pl.when(j == NB)
        def _():
            fin_k[...] = pend_k[...]
            fin_v[...] = pend_v[...]

        dv_ref[...] = fin_v[...]
        kp = kp_ref[...]
        dkn = fin_k[...]
        dkg = jnp.zeros((1, HD), f32)
        for g in range(N_KV):
            kh = kp[:, g * HD:(g + 1) * HD]
            dkh, gk = _norm_bwd(dkn[:, g * HD:(g + 1) * HD], kh, _rstd(kh), kg_ref[...])
            dk_ref[:, pl.ds(g * HD, HD)] = dkh
            dkg = dkg + gk
        dkg_ref[...] += dkg

    cur = lambda col: (lambda b, j: (b * NB + jnp.minimum(j, NB - 1), col))
    prev = lambda col: (lambda b, j: (b * NB + jnp.maximum(j - 1, 0), col))
    small = lambda shape: pl.BlockSpec(shape, lambda b, j: (0, 0))
    return _run("attn_bwd", body, (BL, NB + 1), [proj, proj, proj, proj, proj, d_attn, qg, kg, sink_rows],
                [pl.BlockSpec((BLK, ATT_W), cur(0)),
                 pl.BlockSpec((BLK, KV_W), cur(4)), pl.BlockSpec((BLK, KV_W), prev(4)),
                 pl.BlockSpec((BLK, KV_W), cur(5)), pl.BlockSpec((BLK, KV_W), prev(5)),
                 pl.BlockSpec((BLK, ATT_W), cur(0)), small((1, HD)), small((1, HD)), small((8, 128))],
                [SDS((T, ATT_W), f32), SDS((T, KV_W), f32), SDS((T, KV_W), f32), SDS((1, HD), f32),
                 SDS((1, HD), f32), SDS((8, 128), f32)],
                [pl.BlockSpec((BLK, ATT_W), cur(0)), pl.BlockSpec((BLK, KV_W), prev(0)),
                 pl.BlockSpec((BLK, KV_W), prev(0)), small((1, HD)), small((1, HD)), small((8, 128))],
                scratch=[pltpu.VMEM((BLK, KV_W), f32)] * 4, exchange=exchange)


def mem_conv_bwd(d_mem_out, d_conv_out, proj, km, vm, conv_w8, conv_b, mqg, S, tm, exchange):
    T = d_mem_out.shape[0]
    NM = km.shape[0] // (T // S)

    def body(dmo_ref, dco_ref, ch_ref, cb_ref, cc_ref, chh_ref, cch_ref, qm_ref, km_ref, vm_ref, cw_ref, cbias_ref,
             mqg_ref, dqm_ref, dkm_ref, dvm_ref, dmqg_ref, dcb_ref, dcv_ref, dcw_ref, dcbias_ref):
        i = pl.program_id(0)
        first = (i * tm) % S == 0

        @pl.when(i == 0)
        def _():
            dmqg_ref[...] = jnp.zeros_like(dmqg_ref)
            dcw_ref[...] = jnp.zeros_like(dcw_ref)
            dcbias_ref[...] = jnp.zeros_like(dcbias_ref)

        @pl.when(first)
        def _():
            dkm_ref[...] = jnp.zeros_like(dkm_ref)
            dvm_ref[...] = jnp.zeros_like(dvm_ref)

        qm, kmv, vmv, dmo = qm_ref[...], km_ref[...], vm_ref[...], dmo_ref[...]
        dmqg = jnp.zeros((1, HD), f32)
        for h in range(N_MEMH):
            qh, r, qn, kh, vh, p = _mem_heads(qm, kmv, vmv, mqg_ref[...], h)
            doh = _c(dmo[:, h * HD:(h + 1) * HD])
            dp = _nt(doh, vh)
            ds = _c(p * (dp - jnp.sum(p * dp, axis=-1, keepdims=True)) * (HD ** -0.5))
            dvm_ref[:, pl.ds(h * HD, HD)] += _tn(_c(p), doh)
            dkm_ref[:, pl.ds(h * HD, HD)] += _tn(ds, _c(qn))
            dqh, gq = _norm_bwd(_nn(ds, kh), qh, r, mqg_ref[...])
            dqm_ref[:, pl.ds(h * HD, HD)] = dqh
            dmqg = dmqg + gq
        dmqg_ref[...] += dmqg

        u = cc_ref[...] * ch_ref[...]
        uh = jnp.where(first, 0.0, cch_ref[...] * chh_ref[...])
        u1, u2 = _conv_taps(u, uh)
        conv = cw_ref[0:1, :] * u2 + cw_ref[1:2, :] * u1 + cw_ref[2:3, :] * u + cbias_ref[...]
        dy = dco_ref[...]
        dcb_ref[...] = dy * conv
        dcv = dy * cb_ref[...]
        dcv_ref[...] = dcv
        dcbias_ref[...] += jnp.sum(dcv, axis=0, keepdims=True)
        dcw_ref[0:1, :] += jnp.sum(dcv * u2, axis=0, keepdims=True)
        dcw_ref[1:2, :] += jnp.sum(dcv * u1, axis=0, keepdims=True)
        dcw_ref[2:3, :] += jnp.sum(dcv * u, axis=0, keepdims=True)

    tile = lambda w, col: pl.BlockSpec((tm, w), lambda i: (i, col))
    halo = lambda col: pl.BlockSpec((8, CONV_W), lambda i: (jnp.maximum(i * (tm // 8) - 1, 0), col))
    seq = pl.BlockSpec((NM, MEM_W), lambda i: ((i * tm) // S, 0))
    small = lambda a: pl.BlockSpec(a.shape, lambda i: (0, 0))
    return _run("mem_conv_bwd", body, (T // tm,),
                [d_mem_out, d_conv_out, proj, proj, proj, proj, proj, proj, km, vm, conv_w8, conv_b, mqg],
                [tile(MEM_W, 0), tile(CONV_W, 0), tile(CONV_W, 3), tile(CONV_W, 4), tile(CONV_W, 5), halo(3), halo(5),
                 tile(MEM_W, 6), seq, seq, small(conv_w8), small(conv_b), small(mqg)],
                [SDS((T, MEM_W), f32), SDS(km.shape, f32), SDS(km.shape, f32), SDS(mqg.shape, f32),
                 SDS((T, CONV_W), f32), SDS((T, CONV_W), f32), SDS(conv_w8.shape, f32), SDS(conv_b.shape, f32)],
                [tile(MEM_W, 0), seq, seq, small(mqg), tile(CONV_W, 0), tile(CONV_W, 0), small(conv_w8), small(conv_b)],
                exchange=exchange)


def in_proj_bwd(dq, dk, dv, dcb, dcv, dqm, proj, conv_w8, xn, x2d, dx1, g1, winT, S, tm):
    T, D = x2d.shape
    P = winT.shape[0]
    last_blk = T // 8 - 1

    def body(dq_ref, dk_ref, dv_ref, dcb_ref, dcv_ref, dcvn_ref, dqm_ref, ch_ref, cc_ref, cw_ref, xn_ref, x_ref,
             dx1_ref, g_ref, w_ref, dx_ref, dw_ref, dg_ref):
        i = pl.program_id(0)

        @pl.when(i == 0)
        def _():
            dw_ref[...] = jnp.zeros_like(dw_ref)
            dg_ref[...] = jnp.zeros_like(dg_ref)

        last = ((i + 1) * tm) % S == 0
        dcv = dcv_ref[...]
        nxt = jnp.where(last, 0.0, dcvn_ref[...])
        row = lax.broadcasted_iota(jnp.int32, dcv.shape, 0)
        n1 = jnp.where(row == tm - 1, nxt[0:1, :], pltpu.roll(dcv, tm - 1, 0))
        n2 = jnp.where(row == tm - 2, nxt[0:1, :], jnp.where(row == tm - 1, nxt[1:2, :], pltpu.roll(dcv, tm - 2, 0)))
        du = cw_ref[2:3, :] * dcv + cw_ref[1:2, :] * n1 + cw_ref[0:1, :] * n2
        d_proj = jnp.concatenate([_c(dq_ref[...]), _c(dk_ref[...]), _c(dv_ref[...]), _c(du * cc_ref[...]),
                                  _c(dcb_ref[...]), _c(du * ch_ref[...]), _c(dqm_ref[...])], axis=1)
        dw_ref[...] += _tn(d_proj, xn_ref[...])
        xv = x_ref[...]
        dv_, dg = _norm_bwd(_nn(d_proj, w_ref[...]), xv, _rstd(xv), g_ref[...])
        dx_ref[...] = dx1_ref[...] + dv_
        dg_ref[...] += dg

    tile = lambda w, col=0: pl.BlockSpec((tm, w), lambda i: (i, col))
    nhalo = pl.BlockSpec((8, CONV_W), lambda i: (jnp.minimum((i + 1) * (tm // 8), last_blk), 0))
    small = lambda a: pl.BlockSpec(a.shape, lambda i: (0, 0))
    return _run("in_proj_bwd", body, (T // tm,),
                [dq, dk, dv, dcb, dcv, dcv, dqm, proj, proj, conv_w8, xn, x2d, dx1, g1, winT],
                [tile(ATT_W), tile(KV_W), tile(KV_W), tile(CONV_W), tile(CONV_W), nhalo, tile(MEM_W),
                 tile(CONV_W, 3), tile(CONV_W, 5), small(conv_w8), tile(D), tile(D), tile(D), small(g1), VM],
                [SDS((T, D), f32), SDS((P, D), f32), SDS(g1.shape, f32)],
                [tile(D), pl.BlockSpec((P, D), lambda i: (0, 0)), small(g1)], vmem_mib=48)


def mem_kv_bwd(dkm, dvm, kv, memn, mem2d, g_mem, mkg, wmkv):
    def body(dkm_ref, dvm_ref, kv_ref, mn_ref, m_ref, g_ref, kg_ref, w_ref, dw_ref, dg_ref, dkg_ref):
        kv_ = kv_ref[...]
        dkn = dkm_ref[...]
        dkg = jnp.zeros((1, HD), f32)
        parts = []
        for h in range(N_MEMH):
            kh = kv_[:, h * HD:(h + 1) * HD]
            dkh, gk = _norm_bwd(dkn[:, h * HD:(h + 1) * HD], kh, _rstd(kh), kg_ref[...])
            parts.append(dkh)
            dkg = dkg + gk
        dkg_ref[...] = dkg
        dkv = _c(jnp.concatenate(parts + [dvm_ref[...]], axis=1))
        dw_ref[...] = _tn(mn_ref[...], dkv)
        mv = m_ref[...]
        dg_ref[...] = jnp.sum(_nt(dkv, w_ref[...]) * mv * _rstd(mv), axis=0, keepdims=True)

    return _run("mem_kv_bwd", body, (), [dkm, dvm, kv, memn, mem2d, g_mem, mkg, wmkv], [VM] * 8,
                [SDS(wmkv.shape, f32), SDS(g_mem.shape, f32), SDS(mkg.shape, f32)], [VM] * 3, vmem_mib=40)


def _halves_view(g):
    return g.reshape(4, 2, g.shape[0] // 8, g.shape[1])


def kernel(x, mem, norm_mix, w_in, q_norm, k_norm, attn_sinks, conv_w, conv_b, norm_mem, w_mem_kv, mem_q_norm, mem_k_norm, out_norm_attn, out_norm_conv, out_norm_mem, w_out, norm_ffn, w_gate, w_up, w_down, loss_target, m_norm_mix, m_w_in, m_q_norm, m_k_norm, m_attn_sinks, m_conv_w, m_conv_b, m_norm_mem, m_w_mem_kv, m_mem_q_norm, m_mem_k_norm, m_out_norm_attn, m_out_norm_conv, m_out_norm_mem, m_w_out, m_norm_ffn, m_w_gate, m_w_up, m_w_down, v_norm_mix, v_w_in, v_q_norm, v_k_norm, v_attn_sinks, v_conv_w, v_conv_b, v_norm_mem, v_w_mem_kv, v_mem_q_norm, v_mem_k_norm, v_out_norm_attn, v_out_norm_conv, v_out_norm_mem, v_w_out, v_norm_ffn, v_w_gate, v_w_up, v_w_down):
    BL, S, D = x.shape
    T = BL * S
    TM = 256
    _, _, ci = _place()
    cidx = ci.reshape(1).astype(jnp.int32)

    winT_s, wgT_s, wuT_s, wd_s, wout_s, wmkv_s = prep_weights(w_in[0], w_gate[0], w_up[0], w_down[0], w_out[0], w_mem_kv[0])
    cw_pad = jnp.zeros((8, 128), f32).at[:3, :HD].set(conv_w[0])
    _, (winT, cw_all) = _run("gather_w_in", None, (), [], [], [], [], exchange=gather_exchange([winT_s, cw_pad], [True, False]))
    conv_w_full = jnp.transpose(cw_all.reshape(4, 8, 128)[:, :3, :HD], (1, 0, 2)).reshape(3, CONV_W)
    conv_w8 = jnp.zeros((8, CONV_W), f32).at[:3].set(conv_w_full)
    sink_rows = jnp.broadcast_to(attn_sinks.reshape(N_Q, 1), (N_Q, 128))

    x2d = x.reshape(T, D)
    mem2d = mem.reshape(-1, D)
    (xn, proj), (wmkv, wout) = in_proj_fwd(x2d, norm_mix, winT, TM, gather_exchange([wmkv_s, wout_s], [True, True]))
    memn, kv, km, vm = mem_kv_fwd(mem2d, norm_mem, mem_k_norm, wmkv)
    (attn_out,), (wgT, wuT) = attn_fwd(proj, q_norm, k_norm, sink_rows, BL, S, gather_exchange([wgT_s, wuT_s], [True, True]))
    (conv_out, mem_out, merged, x1, h), (wd,) = mixer_tail_fwd(
        x2d, attn_out, proj, km, vm, conv_w8, conv_b, out_norm_attn, out_norm_conv, out_norm_mem, mem_q_norm, wout,
        norm_ffn, S, TM, gather_exchange([wd_s], [True]))

    dx1, dx2b, act, d_gate, d_up, loss8, d_norm_ffn = ffn_fwd_bwd(h, x1, loss_target.reshape(T, D), wgT, wuT, wd, norm_ffn, TM)
    F = wd.shape[0]
    g_wd = matmul_tn(act, dx2b, "dw_down", F // 2, 512)
    g_wgT = matmul_tn(d_gate, h, "dw_gate", F // 2, 512)
    g_wuT = matmul_tn(d_up, h, "dw_up", F // 2, 512)

    d_attn, d_conv_out, d_mem_out, g_wout, d_gains = out_proj_bwd(
        dx1, merged, attn_out, conv_out, mem_out, out_norm_attn, out_norm_conv, out_norm_mem, wout, TM)
    late = [_halves_view(g) for g in (g_wgT, g_wuT, g_wd, g_wout)]
    (dqm, dkm, dvm, d_mqg, dcb, dcv, d_cw8, d_cbias), late_sib = mem_conv_bwd(
        d_mem_out, d_conv_out, proj, km, vm, conv_w8, conv_b, mem_q_norm, S, TM, halves_exchange(late))
    late_part = add_halves(cidx, late, late_sib, "grad_add_halves_ffn")
    (dq, dk, dv, d_qg, d_kg, d_sink8), late_stage = attn_bwd(proj, d_attn, q_norm, k_norm, sink_rows, BL, S,
                                                             scatter_exchange(late_part))
    g_x, g_winT, d_norm_mix = in_proj_bwd(dq, dk, dv, dcb, dcv, dqm, proj, conv_w8, xn, x2d, dx1, norm_mix, winT, S, TM)
    g_wmkv, d_norm_mem, d_mkg = mem_kv_bwd(dkm, dvm, kv, memn, mem2d, norm_mem, mem_k_norm, wmkv)

    tail = [_halves_view(g) for g in (g_winT, g_wmkv)]
    _, tail_sib = _run("grad_halves_tail", None, (), [], [], [], [], exchange=halves_exchange(tail))
    tail_part = add_halves(cidx, tail, tail_sib, "grad_add_halves_tail")
    _, tail_stage = _run("grad_scatter_tail", None, (), [], [], [], [], exchange=scatter_exchange(tail_part))

    st_wgT, st_wuT, st_wd, st_wout = late_stage
    st_winT, st_wmkv = tail_stage
    big = adamw_big("adamw_col_sharded", [st_winT, st_wgT, st_wuT], [w_in[0], w_gate[0], w_up[0]],
                    [m_w_in[0], m_w_gate[0], m_w_up[0]], [v_w_in[0], v_w_gate[0], v_w_up[0]], True, 8)
    big += adamw_big("adamw_row_sharded", [st_wd, st_wout, st_wmkv], [w_down[0], w_out[0], w_mem_kv[0]],
                     [m_w_down[0], m_w_out[0], m_w_mem_kv[0]], [v_w_down[0], v_w_out[0], v_w_mem_kv[0]], False, 4)
    res = {n: [a[None] for a in big[i]] for i, n in enumerate(("w_in", "w_gate", "w_up", "w_down", "w_out", "w_mem_kv"))}

    tot = allreduce_small(d_norm_mix, d_norm_mem, d_norm_ffn, d_gains, d_cw8, d_cbias, d_qg, d_kg, d_mqg, d_mkg, d_sink8, loss8)
    loss = tot[5, 384]
    w_small = dict(norm_mix=norm_mix, norm_mem=norm_mem, norm_ffn=norm_ffn, out_norm_attn=out_norm_attn,
                   out_norm_conv=out_norm_conv, out_norm_mem=out_norm_mem, conv_w=conv_w, conv_b=conv_b, q_norm=q_norm,
                   k_norm=k_norm, mem_q_norm=mem_q_norm, mem_k_norm=mem_k_norm, attn_sinks=attn_sinks)
    m_small = dict(norm_mix=m_norm_mix, norm_mem=m_norm_mem, norm_ffn=m_norm_ffn, out_norm_attn=m_out_norm_attn,
                   out_norm_conv=m_out_norm_conv, out_norm_mem=m_out_norm_mem, conv_w=m_conv_w, conv_b=m_conv_b,
                   q_norm=m_q_norm, k_norm=m_k_norm, mem_q_norm=m_mem_q_norm, mem_k_norm=m_mem_k_norm,
                   attn_sinks=m_attn_sinks)
    v_small = dict(norm_mix=v_norm_mix, norm_mem=v_norm_mem, norm_ffn=v_norm_ffn, out_norm_attn=v_out_norm_attn,
                   out_norm_conv=v_out_norm_conv, out_norm_mem=v_out_norm_mem, conv_w=v_conv_w, conv_b=v_conv_b,
                   q_norm=v_q_norm, k_norm=v_k_norm, mem_q_norm=v_mem_q_norm, mem_k_norm=v_mem_k_norm,
                   attn_sinks=v_attn_sinks)
    res.update(adamw_small(tot, w_small, m_small, v_small))

    order = ["norm_mix", "w_in", "q_norm", "k_norm", "attn_sinks", "conv_w", "conv_b", "norm_mem", "w_mem_kv",
             "mem_q_norm", "mem_k_norm", "out_norm_attn", "out_norm_conv", "out_norm_mem", "w_out", "norm_ffn",
             "w_gate", "w_up", "w_down"]
    return (loss, g_x.reshape(BL, S, D), *[res[n][0] for n in order], *[res[n][1] for n in order],
            *[res[n][2] for n in order], *[res[n][3] for n in order])
```

```python
import collections
import functools

import jax
import jax.numpy as jnp
from jax import lax
from jax.experimental import pallas as pl
from jax.experimental.pallas import tpu as pltpu

f32 = jnp.float32
MXU = jnp.bfloat16
WIRE = jnp.bfloat16
EPS = 1e-6
NEG = -1e30
HD = 64
BLK = 128
N_Q, N_KV, N_MEMH = 8, 2, 4
GQA = N_Q // N_KV
ATT_W, KV_W, CONV_W, MEM_W = 512, 128, 256, 256
VMEM_MIB = 1024 * 1024
ADAM_LR, ADAM_B1, ADAM_B2, ADAM_EPS, ADAM_WD, ADAM_STEP = 0.001, 0.9, 0.999, 1e-08, 0.01, 10

MESH = pl.DeviceIdType.MESH
VM = pl.BlockSpec(memory_space=pltpu.VMEM)
ANY = pl.BlockSpec(memory_space=pl.ANY)
SDS = jax.ShapeDtypeStruct
DMA = pltpu.SemaphoreType.DMA


def _c(v):
    return v.astype(MXU)


def _nn(a, b):
    return lax.dot_general(a, b, (((1,), (0,)), ((), ())), preferred_element_type=f32)


def _nt(a, b):
    return lax.dot_general(a, b, (((1,), (1,)), ((), ())), preferred_element_type=f32)


def _tn(a, b):
    return lax.dot_general(a, b, (((0,), (0,)), ((), ())), preferred_element_type=f32)


def _rstd(v):
    return lax.rsqrt(jnp.mean(v * v, axis=-1, keepdims=True) + EPS)


def _norm_bwd(dy, v, r, g):
    dyg = dy * g
    dv = r * dyg - v * (r * r * r) * jnp.mean(dyg * v, axis=-1, keepdims=True)
    return dv, jnp.sum(dy * v * r, axis=0, keepdims=True)


def _softmax_rows(s, extra=None):
    m = jnp.max(s, axis=-1, keepdims=True)
    if extra is not None:
        m = jnp.maximum(m, extra)
    p = jnp.exp(s - m)
    den = jnp.sum(p, axis=-1, keepdims=True)
    if extra is None:
        return p * (1.0 / den), None
    pe = jnp.exp(extra - m)
    inv = 1.0 / (den + pe)
    return p * inv, pe * inv


def _place():
    return lax.axis_index("x"), lax.axis_index("y"), lax.axis_index("c")


def _other_chips(x, y):
    return [(1 - x, y), (x, 1 - y), (1 - x, 1 - y)]


Exchange = collections.namedtuple("Exchange", "ins outs sems start finish")


def _run(name, body, grid, ins, in_specs, out_shape, out_specs, scratch=(), vmem_mib=32, exchange=None):
    ins, in_specs, out_shape, out_specs, scratch = list(ins), list(in_specs), list(out_shape), list(out_specs), list(scratch)
    ni, no, ns = len(ins), len(out_shape), len(scratch)
    ex = exchange
    if ex is not None:
        nxi, nxo = len(ex.ins), len(ex.outs)

    def call_body(*refs):
        if ex is None:
            body(*refs)
            return
        a, xa = refs[:ni], refs[ni:ni + nxi]
        o, xo = refs[ni + nxi:ni + nxi + no], refs[ni + nxi + no:ni + nxi + no + nxo]
        s, xs = refs[ni + nxi + no + nxo:ni + nxi + no + nxo + ns], refs[ni + nxi + no + nxo + ns:]
        if grid:
            first = functools.reduce(jnp.logical_and, [pl.program_id(d) == 0 for d in range(len(grid))])
            last = functools.reduce(jnp.logical_and, [pl.program_id(d) == grid[d] - 1 for d in range(len(grid))])
            pl.when(first)(lambda: ex.start(xa, xo, xs))
            body(*a, *o, *s)
            pl.when(last)(lambda: ex.finish(xa, xo, xs))
        else:
            ex.start(xa, xo, xs)
            if body is not None:
                body(*a, *o, *s)
            ex.finish(xa, xo, xs)

    if ex is not None:
        ins, in_specs = ins + list(ex.ins), in_specs + [ANY] * nxi
        out_shape, out_specs = out_shape + list(ex.outs), out_specs + [ANY] * nxo
        scratch = scratch + list(ex.sems)
    kw = dict(grid=grid) if grid else {}
    res = pl.pallas_call(
        call_body, name=name, out_shape=out_shape, in_specs=in_specs, out_specs=out_specs, scratch_shapes=scratch,
        compiler_params=pltpu.CompilerParams(dimension_semantics=("arbitrary",) * len(grid) if grid else None,
                                             vmem_limit_bytes=vmem_mib * VMEM_MIB), **kw)(*ins)
    res = list(res)
    return (res[:no], res[no:]) if ex is not None else res


def _remote(src, dst, ssem, rsem, dev):
    return pltpu.make_async_remote_copy(src_ref=src, dst_ref=dst, send_sem=ssem, recv_sem=rsem,
                                        device_id=dev, device_id_type=MESH)


def gather_exchange(shards, split):
    n = len(shards)

    def rows(ref, e, kk, half=None):
        R = shards[e].shape[0]
        if half is None:
            return ref.at[pl.ds(pl.multiple_of(kk * R, 8), R)]
        return ref.at[pl.ds(pl.multiple_of(kk * R + half * (R // 2), 8), R // 2)]

    def ici(src, dst, sm, e, j, chip_j, x, y, c):
        k = 2 * x + y
        if split[e]:
            s = src[e].at[pl.ds(pl.multiple_of(c * (shards[e].shape[0] // 2), 8), shards[e].shape[0] // 2)]
            return _remote(s, rows(dst[e], e, k, c), sm[0].at[6 * e + j], sm[1].at[6 * e + j], (*chip_j, c))
        return _remote(src[e], rows(dst[e], e, k), sm[0].at[6 * e + j], sm[1].at[6 * e + j], (*chip_j, c))

    def landed(dst, e, chip_j, c):
        kj = 2 * chip_j[0] + chip_j[1]
        return rows(dst[e], e, kj, c) if split[e] else rows(dst[e], e, kj)

    def forward(dst, sm, e, j, chip_j, x, y, c, sender_c):
        kj = 2 * chip_j[0] + chip_j[1]
        r = rows(dst[e], e, kj, sender_c)
        return _remote(r, r, sm[0].at[6 * e + 3 + j], sm[1].at[6 * e + 3 + j], (x, y, 1 - c))

    def local(src, dst, sm, e, x, y):
        return pltpu.make_async_copy(src[e], rows(dst[e], e, 2 * x + y), sm[2].at[e])

    def start(src, dst, sm):
        x, y, c = _place()
        for e in range(n):
            local(src, dst, sm, e, x, y).start()
            for j, chip_j in enumerate(_other_chips(x, y)):
                ici(src, dst, sm, e, j, chip_j, x, y, c).start()

    def finish(src, dst, sm):
        x, y, c = _place()
        chips = _other_chips(x, y)
        for e in range(n):
            for j, chip_j in enumerate(chips):
                r = landed(dst, e, chip_j, c)
                _remote(r, r, sm[0].at[6 * e + j], sm[1].at[6 * e + j], (*chip_j, c)).wait_recv()
                if split[e]:
                    forward(dst, sm, e, j, chip_j, x, y, c, c).start()
        for e in range(n):
            for j, chip_j in enumerate(chips):
                if split[e]:
                    forward(dst, sm, e, j, chip_j, x, y, c, 1 - c).wait_recv()
        for e in range(n):
            for j, chip_j in enumerate(chips):
                ici(src, dst, sm, e, j, chip_j, x, y, c).wait_send()
                if split[e]:
                    forward(dst, sm, e, j, chip_j, x, y, c, c).wait_send()
            local(src, dst, sm, e, x, y).wait()

    outs = [SDS((4 * s.shape[0], s.shape[1]), s.dtype) for s in shards]
    return Exchange(list(shards), outs, [DMA((6 * n,)), DMA((6 * n,)), DMA((n,))], start, finish)


def halves_exchange(grads):
    n = len(grads)

    def copy(g, st, sm, e, x, y, c):
        return _remote(g[e].at[:, 1 - c], st[e], sm[0].at[e], sm[1].at[e], (x, y, 1 - c))

    def start(g, st, sm):
        x, y, c = _place()
        for e in range(n):
            copy(g, st, sm, e, x, y, c).start()

    def finish(g, st, sm):
        x, y, c = _place()
        for e in range(n):
            copy(g, st, sm, e, x, y, c).wait()

    outs = [SDS((4,) + a.shape[2:], a.dtype) for a in grads]
    return Exchange(list(grads), outs, [DMA((n,)), DMA((n,))], start, finish)


def scatter_exchange(parts):
    n = len(parts)

    def ici(p, st, sm, e, j, chip_j, x, y, c):
        k, kj = 2 * x + y, 2 * chip_j[0] + chip_j[1]
        return _remote(p[e].at[kj], st[e].at[c, k], sm[0].at[8 * e + j], sm[1].at[8 * e + j], (*chip_j, c))

    def own(p, st, sm, e, x, y, c):
        k = 2 * x + y
        return _remote(p[e].at[k], st[e].at[c, k], sm[0].at[8 * e + 3], sm[1].at[8 * e + 3], (x, y, 1 - c))

    def forward(st, sm, e, j, chip_j, x, y, c, sender_c):
        kj = 2 * chip_j[0] + chip_j[1]
        r = st[e].at[sender_c, kj]
        return _remote(r, r, sm[0].at[8 * e + 4 + j], sm[1].at[8 * e + 4 + j], (x, y, 1 - c))

    def local(p, st, sm, e, x, y, c):
        k = 2 * x + y
        return pltpu.make_async_copy(p[e].at[k], st[e].at[c, k], sm[2].at[e])

    def start(p, st, sm):
        x, y, c = _place()
        for e in range(n):
            local(p, st, sm, e, x, y, c).start()
            own(p, st, sm, e, x, y, c).start()
            for j, chip_j in enumerate(_other_chips(x, y)):
                ici(p, st, sm, e, j, chip_j, x, y, c).start()

    def finish(p, st, sm):
        x, y, c = _place()
        k = 2 * x + y
        chips = _other_chips(x, y)
        for e in range(n):
            for j, chip_j in enumerate(chips):
                kj = 2 * chip_j[0] + chip_j[1]
                r = st[e].at[c, kj]
                _remote(r, r, sm[0].at[8 * e + j], sm[1].at[8 * e + j], (*chip_j, c)).wait_recv()
                forward(st, sm, e, j, chip_j, x, y, c, c).start()
        for e in range(n):
            r = st[e].at[1 - c, k]
            _remote(r, r, sm[0].at[8 * e + 3], sm[1].at[8 * e + 3], (x, y, 1 - c)).wait_recv()
            for j, chip_j in enumerate(chips):
                forward(st, sm, e, j, chip_j, x, y, c, 1 - c).wait_recv()
        for e in range(n):
            own(p, st, sm, e, x, y, c).wait_send()
            for j, chip_j in enumerate(chips):
                ici(p, st, sm, e, j, chip_j, x, y, c).wait_send()
                forward(st, sm, e, j, chip_j, x, y, c, c).wait_send()
            local(p, st, sm, e, x, y, c).wait()

    outs = [SDS((2,) + a.shape, a.dtype) for a in parts]
    return Exchange(list(parts), outs, [DMA((8 * n,)), DMA((8 * n,)), DMA((n,))], start, finish)


def allreduce_small(d_norm_mix, d_norm_mem, d_norm_ffn, d_gains, d_cw8, d_cbias, d_qg, d_kg, d_mqg, d_mkg, d_sink8, loss8):
    def body(nm_ref, nmem_ref, nf_ref, gn_ref, cw_ref, cb_ref, qg_ref, kg_ref, mqg_ref, mkg_ref, sk_ref, ls_ref,
             o_ref, buf, ssem, rsem):
        x, y, c = _place()
        me = 4 * x + 2 * y + c
        mine = buf.at[me]
        mine[...] = jnp.zeros((8, 1024), f32)
        mine[0:1, :] = nm_ref[...]
        mine[1:2, :] = nmem_ref[...]
        mine[2:3, :] = nf_ref[...]
        mine[3:4, :] = gn_ref[...]
        for j in range(3):
            mine[4:5, pl.ds(j * CONV_W, CONV_W)] = cw_ref[j:j + 1, :]
        mine[4:5, pl.ds(3 * CONV_W, CONV_W)] = cb_ref[...]
        for j, r in enumerate((qg_ref, kg_ref, mqg_ref, mkg_ref)):
            mine[5:6, pl.ds(j * HD, HD)] = r[...]
        mine[5:6, pl.ds(256, 128)] = sk_ref[0:1, :]
        mine[5:6, pl.ds(384, 128)] = ls_ref[0:1, :]

        def peer_of(m):
            return (1 - x if m & 4 else x, 1 - y if m & 2 else y, 1 - c if m & 1 else c)

        for m in range(1, 8):
            _remote(mine, mine, ssem.at[m - 1], rsem.at[m - 1], peer_of(m)).start()
        for m in range(1, 8):
            p = peer_of(m)
            got = buf.at[4 * p[0] + 2 * p[1] + p[2]]
            _remote(got, got, ssem.at[m - 1], rsem.at[m - 1], p).wait_recv()
        for m in range(1, 8):
            _remote(mine, mine, ssem.at[m - 1], rsem.at[m - 1], peer_of(m)).wait_send()
        acc = buf[0]
        for d in range(1, 8):
            acc = acc + buf[d]
        o_ref[...] = acc

    ins = [d_norm_mix, d_norm_mem, d_norm_ffn, d_gains, d_cw8, d_cbias, d_qg, d_kg, d_mqg, d_mkg, d_sink8, loss8]
    return _run("allreduce_small", body, (), ins, [VM] * len(ins), [SDS((8, 1024), f32)], [VM],
                scratch=[pltpu.VMEM((8, 8, 1024), f32), DMA((7,)), DMA((7,))])[0]


def add_halves(cidx, grads, stages, name, nch=2):
    n = len(grads)

    def body(c_ref, *refs):
        g, st, o = refs[:n], refs[n:2 * n], refs[2 * n:]
        for e in range(n):
            o[e][...] = (g[e][...] + st[e][...]).astype(WIRE)

    in_specs, out_specs, out_shape = [], [], []
    for a in grads:
        hr, C = a.shape[2], a.shape[3]
        in_specs.append(pl.BlockSpec((None, None, hr // nch, C), lambda s, q, c_ref: (s, c_ref[0], q, 0)))
    for a in stages:
        hr, C = a.shape[1], a.shape[2]
        in_specs.append(pl.BlockSpec((None, hr // nch, C), lambda s, q, c_ref: (s, q, 0)))
        out_specs.append(pl.BlockSpec((None, hr // nch, C), lambda s, q, c_ref: (s, q, 0)))
        out_shape.append(SDS(a.shape, WIRE))
    return pl.pallas_call(
        body, name=name, out_shape=out_shape,
        grid_spec=pltpu.PrefetchScalarGridSpec(num_scalar_prefetch=1, grid=(4, nch), in_specs=in_specs, out_specs=out_specs),
        compiler_params=pltpu.CompilerParams(dimension_semantics=("arbitrary", "arbitrary")),
    )(cidx, *grads, *stages)


def _adamw_math(w, g, m, v):
    m = ADAM_B1 * m + (1.0 - ADAM_B1) * g
    v = ADAM_B2 * v + (1.0 - ADAM_B2) * (g * g)
    m_hat = m / (1.0 - ADAM_B1 ** ADAM_STEP)
    v_hat = v / (1.0 - ADAM_B2 ** ADAM_STEP)
    delta = -ADAM_LR * (m_hat / (jnp.sqrt(v_hat) + ADAM_EPS) + ADAM_WD * w)
    return delta, m, v


def _sum_chips(st):
    return ((st[0].astype(f32) + st[1].astype(f32)) + st[2].astype(f32)) + st[3].astype(f32)


def adamw_big(name, stages, ws, ms, vs, nstep, exchange=None):
    n = len(stages)

    def body(*refs):
        st, w, m, v = refs[:n], refs[n:2 * n], refs[2 * n:3 * n], refs[3 * n:4 * n]
        outs = refs[4 * n:]
        for e in range(n):
            g = jnp.concatenate([_sum_chips(st[e].at[0]), _sum_chips(st[e].at[1])], axis=0)
            d, mm, vv = _adamw_math(w[e][...], g, m[e][...], v[e][...])
            outs[4 * e][...] = g
            outs[4 * e + 1][...] = d
            outs[4 * e + 2][...] = mm
            outs[4 * e + 3][...] = vv

    st_specs, w_specs = [], []
    for e in range(n):
        _, _, hr, C = stages[e].shape
        st_specs.append(pl.BlockSpec((2, 4, hr, C // nstep), lambda i: (0, 0, 0, i)))
        w_specs.append(pl.BlockSpec((2 * hr, C // nstep), lambda i: (0, i)))
    out_specs = [s for s in w_specs for _ in range(4)]
    out_shape = [SDS(w.shape, f32) for w in ws for _ in range(4)]
    res = _run(name, body, (nstep,), list(stages) + list(ws) + list(ms) + list(vs), st_specs + w_specs * 3,
               out_shape, out_specs, vmem_mib=48, exchange=exchange)
    res, sent = res if exchange is not None else (res, None)
    return [res[4 * e:4 * e + 4] for e in range(n)], sent


SMALL = ("norm_mix", "norm_mem", "norm_ffn", "out_norm_attn", "out_norm_conv", "out_norm_mem", "conv_w", "conv_b",
         "q_norm", "k_norm", "mem_q_norm", "mem_k_norm", "attn_sinks")


def adamw_small(tot, ws, ms, vs):
    ns = len(SMALL)

    def grad_of(name, tot_ref, chip):
        where = {"norm_mix": (0, 0, 1024), "norm_mem": (1, 0, 1024), "norm_ffn": (2, 0, 1024),
                 "out_norm_attn": (3, 0, ATT_W), "out_norm_conv": (3, ATT_W, CONV_W),
                 "out_norm_mem": (3, ATT_W + CONV_W, MEM_W), "conv_b": (4, 3 * CONV_W, CONV_W), "q_norm": (5, 0, HD),
                 "k_norm": (5, HD, HD), "mem_q_norm": (5, 2 * HD, HD), "mem_k_norm": (5, 3 * HD, HD),
                 "attn_sinks": (5, 256, N_Q)}
        if name != "conv_w":
            r, c0, w = where[name]
            return tot_ref[r:r + 1, c0:c0 + w]
        taps = []
        for j in range(3):
            mine = tot_ref[4:5, j * CONV_W:j * CONV_W + HD]
            for s in range(1, 4):
                mine = jnp.where(chip == s, tot_ref[4:5, j * CONV_W + s * HD:j * CONV_W + (s + 1) * HD], mine)
            taps.append(mine)
        return jnp.concatenate(taps, axis=0)[None]

    def body(tot_ref, *refs):
        w, m, v, outs = refs[:ns], refs[ns:2 * ns], refs[2 * ns:3 * ns], refs[3 * ns:]
        x, y, _ = _place()
        for i, name in enumerate(SMALL):
            g = grad_of(name, tot_ref, 2 * x + y)
            d, mm, vv = _adamw_math(w[i][...], g, m[i][...], v[i][...])
            outs[4 * i][...] = g
            outs[4 * i + 1][...] = d
            outs[4 * i + 2][...] = mm
            outs[4 * i + 3][...] = vv

    ins = [tot] + [d[k] for d in (ws, ms, vs) for k in SMALL]
    out_shape = [SDS(ws[k].shape, f32) for k in SMALL for _ in range(4)]
    res = _run("adamw_small", body, (), ins, [VM] * len(ins), out_shape, [VM] * len(out_shape))
    return {k: res[4 * i:4 * i + 4] for i, k in enumerate(SMALL)}


def prep_weights(shards):
    n = len(shards)

    def body(*refs):
        for e in range(n):
            refs[n + e][...] = _c(refs[e][...])

    return _run("prep_weights", body, (), shards, [VM] * n, [SDS(a.shape, MXU) for a in shards], [VM] * n, vmem_mib=48)


def mem_kv_fwd(mem2d, g_mem, mkg, wmkv):
    M, D = mem2d.shape

    def body(m_ref, g_ref, kg_ref, w_ref, mn_ref, kv_ref, km_ref, vm_ref):
        m = m_ref[...]
        mn = _c(m * _rstd(m) * g_ref[...])
        mn_ref[...] = mn
        kv = _nn(mn, w_ref[...])
        kv_ref[...] = kv
        for h in range(N_MEMH):
            kh = kv[:, h * HD:(h + 1) * HD]
            km_ref[:, pl.ds(h * HD, HD)] = kh * _rstd(kh) * kg_ref[...]
        vm_ref[...] = kv[:, MEM_W:]

    return _run("mem_kv_fwd", body, (), [mem2d, g_mem, mkg, wmkv], [VM] * 4,
                [SDS((M, D), MXU), SDS((M, 2 * MEM_W), f32), SDS((M, MEM_W), f32), SDS((M, MEM_W), f32)], [VM] * 4)


def in_proj_fwd(x2d, g1, winT, tm, exchange):
    T, D = x2d.shape
    P = winT.shape[0]

    def body(x_ref, g_ref, w_ref, xn_ref, proj_ref):
        xv = x_ref[...]
        xn = _c(xv * _rstd(xv) * g_ref[...])
        xn_ref[...] = xn
        proj_ref[...] = _nt(xn, w_ref[...])

    return _run("in_proj_fwd", body, (T // tm,), [x2d, g1, winT],
                [pl.BlockSpec((tm, D), lambda i: (i, 0)), pl.BlockSpec((1, D), lambda i: (0, 0)), VM],
                [SDS((T, D), MXU), SDS((T, P), f32)],
                [pl.BlockSpec((tm, D), lambda i: (i, 0)), pl.BlockSpec((tm, P), lambda i: (i, 0))],
                vmem_mib=40, exchange=exchange)


def _swa_setup(j, g, sk_ref):
    rows = GQA * BLK
    ri = lax.broadcasted_iota(jnp.int32, (rows, 2 * BLK), 0)
    ki = lax.broadcasted_iota(jnp.int32, (rows, 2 * BLK), 1)
    dist = (ri & (BLK - 1)) + BLK - ki
    valid = (dist >= 0) & (dist < BLK) & ((ki >= BLK) | (j > 0))
    hrow = lax.broadcasted_iota(jnp.int32, (rows, 1), 0) // BLK
    slope = jnp.zeros((rows, 1), f32)
    sink = jnp.zeros((rows, 1), f32)
    for hh in range(GQA):
        h = g * GQA + hh
        slope = jnp.where(hrow == hh, 2.0 ** -(h + 1), slope)
        sink = jnp.where(hrow == hh, sk_ref[h:h + 1, 0:1], sink)
    return valid, slope * dist.astype(f32), sink


def _stack_heads(v, g):
    return jnp.concatenate([v[:, (g * GQA + hh) * HD:(g * GQA + hh + 1) * HD] for hh in range(GQA)], axis=0)


def attn_fwd(proj, qg, kg, sink_rows, BL, S, exchange):
    NB = S // BLK
    T = BL * S

    def body(q_ref, kc_ref, kp_ref, vc_ref, vp_ref, qg_ref, kg_ref, sk_ref, o_ref):
        j = pl.program_id(1)
        q = q_ref[...]
        k2 = jnp.concatenate([kp_ref[...], kc_ref[...]], axis=0)
        v2 = jnp.concatenate([vp_ref[...], vc_ref[...]], axis=0)
        for g in range(N_KV):
            valid, bias, sink = _swa_setup(j, g, sk_ref)
            kh = k2[:, g * HD:(g + 1) * HD]
            kn = _c(kh * _rstd(kh) * kg_ref[...])
            vh = _c(v2[:, g * HD:(g + 1) * HD])
            qs = _stack_heads(q, g)
            qn = _c(qs * _rstd(qs) * qg_ref[...])
            s = jnp.where(valid, _nt(qn, kn) * (HD ** -0.5) - bias, NEG)
            p, _ = _softmax_rows(s, sink)
            o = _nn(_c(p), vh)
            for hh in range(GQA):
                o_ref[:, pl.ds((g * GQA + hh) * HD, HD)] = o[hh * BLK:(hh + 1) * BLK]

    cur = lambda col: (lambda b, j: (b * NB + j, col))
    prev = lambda col: (lambda b, j: (b * NB + jnp.maximum(j - 1, 0), col))
    small = lambda shape: pl.BlockSpec(shape, lambda b, j: (0, 0))
    return _run("attn_fwd", body, (BL, NB), [proj, proj, proj, proj, proj, qg, kg, sink_rows],
                [pl.BlockSpec((BLK, ATT_W), cur(0)),
                 pl.BlockSpec((BLK, KV_W), cur(4)), pl.BlockSpec((BLK, KV_W), prev(4)),
                 pl.BlockSpec((BLK, KV_W), cur(5)), pl.BlockSpec((BLK, KV_W), prev(5)),
                 small((1, HD)), small((1, HD)), small((8, 128))],
                [SDS((T, ATT_W), f32)], [pl.BlockSpec((BLK, ATT_W), cur(0))], exchange=exchange)


def _conv_taps(u, uh):
    row = lax.broadcasted_iota(jnp.int32, u.shape, 0)
    u1 = jnp.where(row == 0, uh[7:8, :], pltpu.roll(u, 1, 0))
    u2 = jnp.where(row == 0, uh[6:7, :], jnp.where(row == 1, uh[7:8, :], pltpu.roll(u, 2, 0)))
    return u1, u2


def _mem_heads(qm, km, vm, qg, h):
    qh = qm[:, h * HD:(h + 1) * HD]
    r = _rstd(qh)
    qn = qh * r * qg
    kh = _c(km[:, h * HD:(h + 1) * HD])
    vh = _c(vm[:, h * HD:(h + 1) * HD])
    p, _ = _softmax_rows(_nt(_c(qn), kh) * (HD ** -0.5))
    return qh, r, qn, kh, vh, p


def mixer_tail_fwd(x2d, attn_out, proj, km, vm, conv_w8, conv_b, g_a, g_c, g_m, mqg, wout, g_f, S, tm, exchange):
    T, D = x2d.shape
    NM = km.shape[0] // (T // S)

    def body(x_ref, ao_ref, ch_ref, cb_ref, cc_ref, chh_ref, cch_ref, qm_ref, km_ref, vm_ref, cw_ref, cbias_ref,
             ga_ref, gc_ref, gm_ref, mqg_ref, wout_ref, gf_ref, co_ref, mo_ref, mg_ref, x1_ref, h_ref):
        first = (pl.program_id(0) * tm) % S == 0
        u = cc_ref[...] * ch_ref[...]
        uh = jnp.where(first, 0.0, cch_ref[...] * chh_ref[...])
        u1, u2 = _conv_taps(u, uh)
        conv = cw_ref[0:1, :] * u2 + cw_ref[1:2, :] * u1 + cw_ref[2:3, :] * u + cbias_ref[...]
        conv_out = cb_ref[...] * conv
        co_ref[...] = conv_out
        qm, kmv, vmv = qm_ref[...], km_ref[...], vm_ref[...]
        for h in range(N_MEMH):
            _, _, _, _, vh, p = _mem_heads(qm, kmv, vmv, mqg_ref[...], h)
            mo_ref[:, pl.ds(h * HD, HD)] = _nn(_c(p), vh)
        mem_out = mo_ref[...]
        ao = ao_ref[...]
        merged = _c(jnp.concatenate([ao * _rstd(ao) * ga_ref[...], conv_out * _rstd(conv_out) * gc_ref[...],
                                     mem_out * _rstd(mem_out) * gm_ref[...]], axis=1))
        mg_ref[...] = merged
        x1 = x_ref[...] + _nn(merged, wout_ref[...])
        x1_ref[...] = x1
        h_ref[...] = _c(x1 * _rstd(x1) * gf_ref[...])

    tile = lambda w, col: pl.BlockSpec((tm, w), lambda i: (i, col))
    halo = lambda col: pl.BlockSpec((8, CONV_W), lambda i: (jnp.maximum(i * (tm // 8) - 1, 0), col))
    seq = pl.BlockSpec((NM, MEM_W), lambda i: ((i * tm) // S, 0))
    small = lambda a: pl.BlockSpec(a.shape, lambda i: (0, 0))
    return _run("mixer_tail_fwd", body, (T // tm,),
                [x2d, attn_out, proj, proj, proj, proj, proj, proj, km, vm, conv_w8, conv_b, g_a, g_c, g_m, mqg, wout, g_f],
                [tile(D, 0), tile(ATT_W, 0), tile(CONV_W, 3), tile(CONV_W, 4), tile(CONV_W, 5), halo(3), halo(5),
                 tile(MEM_W, 6), seq, seq, small(conv_w8), small(conv_b), small(g_a), small(g_c), small(g_m),
                 small(mqg), VM, small(g_f)],
                [SDS((T, CONV_W), f32), SDS((T, MEM_W), f32), SDS((T, D), MXU), SDS((T, D), f32), SDS((T, D), MXU)],
                [tile(CONV_W, 0), tile(MEM_W, 0), tile(D, 0), tile(D, 0), tile(D, 0)], vmem_mib=40, exchange=exchange)


def ffn_fwd_bwd(h, x1, tgt, wgT, wuT, wd, g_f, tm):
    T, D = x1.shape
    F = wd.shape[0]

    def body(h_ref, x1_ref, t_ref, wg_ref, wu_ref, wd_ref, gf_ref,
             dx1_ref, dx2_ref, act_ref, dg_ref, du_ref, loss_ref, dgf_ref):
        @pl.when(pl.program_id(0) == 0)
        def _():
            loss_ref[...] = jnp.zeros_like(loss_ref)
            dgf_ref[...] = jnp.zeros_like(dgf_ref)

        hv = h_ref[...]
        gate = _nt(hv, wg_ref[...])
        up = _nt(hv, wu_ref[...])
        sg = jax.nn.sigmoid(gate)
        sl = gate * sg
        act = _c(sl * up)
        act_ref[...] = act
        x1v = x1_ref[...]
        diff = (x1v + _nn(act, wd_ref[...])) - t_ref[...]
        loss_ref[...] += 0.5 * jnp.sum(jnp.sum(diff * diff, axis=-1, keepdims=True) / D, axis=0, keepdims=True)
        dx2 = diff / D
        dx2b = _c(dx2)
        dx2_ref[...] = dx2b
        d_act = _nt(dx2b, wd_ref[...])
        d_up = _c(d_act * sl)
        d_gate = _c(d_act * up * (sg * (1.0 + gate * (1.0 - sg))))
        du_ref[...] = d_up
        dg_ref[...] = d_gate
        dh = _nn(d_gate, wg_ref[...]) + _nn(d_up, wu_ref[...])
        dv, dgf = _norm_bwd(dh, x1v, _rstd(x1v), gf_ref[...])
        dx1_ref[...] = dx2 + dv
        dgf_ref[...] += dgf

    tile = lambda w: pl.BlockSpec((tm, w), lambda i: (i, 0))
    return _run("ffn_fwd_bwd", body, (T // tm,), [h, x1, tgt, wgT, wuT, wd, g_f],
                [tile(D), tile(D), tile(D), VM, VM, VM, pl.BlockSpec((1, D), lambda i: (0, 0))],
                [SDS((T, D), f32), SDS((T, D), MXU), SDS((T, F), MXU), SDS((T, F), MXU), SDS((T, F), MXU),
                 SDS((8, 128), f32), SDS((1, D), f32)],
                [tile(D), tile(D), tile(F), tile(F), tile(F), pl.BlockSpec((8, 128), lambda i: (0, 0)),
                 pl.BlockSpec((1, D), lambda i: (0, 0))], vmem_mib=56)


def matmul_tn(a, b, name, tmo, tk):
    T, M = a.shape
    N = b.shape[1]

    def body(a_ref, b_ref, o_ref):
        @pl.when(pl.program_id(1) == 0)
        def _():
            o_ref[...] = jnp.zeros_like(o_ref)

        o_ref[...] += _tn(a_ref[...], b_ref[...])

    return _run(name, body, (M // tmo, T // tk), [a, b],
                [pl.BlockSpec((tk, tmo), lambda m, k: (k, m)), pl.BlockSpec((tk, N), lambda m, k: (k, 0))],
                [SDS((M, N), f32)], [pl.BlockSpec((tmo, N), lambda m, k: (m, 0))], vmem_mib=48)[0]


def out_proj_bwd(dx1, merged, attn_out, conv_out, mem_out, g_a, g_c, g_m, wout, tm):
    T, D = dx1.shape

    def body(dx1_ref, mg_ref, ao_ref, co_ref, mo_ref, ga_ref, gc_ref, gm_ref, w_ref,
             dao_ref, dco_ref, dmo_ref, dw_ref, dgain_ref):
        @pl.when(pl.program_id(0) == 0)
        def _():
            dw_ref[...] = jnp.zeros_like(dw_ref)
            dgain_ref[...] = jnp.zeros_like(dgain_ref)

        dxb = _c(dx1_ref[...])
        dw_ref[...] += _tn(mg_ref[...], dxb)
        dmg = _nt(dxb, w_ref[...])
        ao, co, mo = ao_ref[...], co_ref[...], mo_ref[...]
        da, ga = _norm_bwd(dmg[:, :ATT_W], ao, _rstd(ao), ga_ref[...])
        dc, gc = _norm_bwd(dmg[:, ATT_W:ATT_W + CONV_W], co, _rstd(co), gc_ref[...])
        dm, gm = _norm_bwd(dmg[:, ATT_W + CONV_W:], mo, _rstd(mo), gm_ref[...])
        dao_ref[...] = da
        dco_ref[...] = dc
        dmo_ref[...] = dm
        dgain_ref[...] += jnp.concatenate([ga, gc, gm], axis=1)

    tile = lambda w: pl.BlockSpec((tm, w), lambda i: (i, 0))
    small = lambda a: pl.BlockSpec(a.shape, lambda i: (0, 0))
    return _run("out_proj_bwd", body, (T // tm,), [dx1, merged, attn_out, conv_out, mem_out, g_a, g_c, g_m, wout],
                [tile(D), tile(D), tile(ATT_W), tile(CONV_W), tile(MEM_W), small(g_a), small(g_c), small(g_m), VM],
                [SDS((T, ATT_W), f32), SDS((T, CONV_W), f32), SDS((T, MEM_W), f32), SDS((D, D), f32), SDS((1, D), f32)],
                [tile(ATT_W), tile(CONV_W), tile(MEM_W), pl.BlockSpec((D, D), lambda i: (0, 0)),
                 pl.BlockSpec((1, D), lambda i: (0, 0))], vmem_mib=40)


def attn_bwd(proj, d_attn, qg, kg, sink_rows, BL, S, exchange):
    NB = S // BLK
    T = BL * S

    def body(q_ref, kc_ref, kp_ref, vc_ref, vp_ref, do_ref, qg_ref, kg_ref, sk_ref,
             dq_ref, dk_ref, dv_ref, dqg_ref, dkg_ref, dsk_ref, pend_k, pend_v, fin_k, fin_v):
        b, j = pl.program_id(0), pl.program_id(1)

        @pl.when((b == 0) & (j == 0))
        def _():
            dqg_ref[...] = jnp.zeros_like(dqg_ref)
            dkg_ref[...] = jnp.zeros_like(dkg_ref)
            dsk_ref[...] = jnp.zeros_like(dsk_ref)

        @pl.when(j == 0)
        def _():
            pend_k[...] = jnp.zeros_like(pend_k)
            pend_v[...] = jnp.zeros_like(pend_v)

        @pl.when(j < NB)
        def _():
            q, do = q_ref[...], do_ref[...]
            k2 = jnp.concatenate([kp_ref[...], kc_ref[...]], axis=0)
            v2 = jnp.concatenate([vp_ref[...], vc_ref[...]], axis=0)
            lane = lax.broadcasted_iota(jnp.int32, (8, 128), 1)
            dqg = jnp.zeros((1, HD), f32)
            dsk = jnp.zeros((8, 128), f32)
            dks, dvs = [], []
            for g in range(N_KV):
                valid, bias, sink = _swa_setup(j, g, sk_ref)
                kh = k2[:, g * HD:(g + 1) * HD]
                kn = _c(kh * _rstd(kh) * kg_ref[...])
                vh = _c(v2[:, g * HD:(g + 1) * HD])
                qs = _stack_heads(q, g)
                r = _rstd(qs)
                qn = _c(qs * r * qg_ref[...])
                s = jnp.where(valid, _nt(qn, kn) * (HD ** -0.5) - bias, NEG)
                p, ps = _softmax_rows(s, sink)
                dos = _c(_stack_heads(do, g))
                dp = _nt(dos, vh)
                delta = jnp.sum(p * dp, axis=-1, keepdims=True)
                ds = _c(p * (dp - delta) * (HD ** -0.5))
                t = ps * delta
                for hh in range(GQA):
                    dsk = dsk + jnp.where(lane == g * GQA + hh, -jnp.sum(t[hh * BLK:(hh + 1) * BLK]), 0.0)
                dvs.append(_tn(_c(p), dos))
                dks.append(_tn(ds, qn))
                dqs, gq = _norm_bwd(_nn(ds, kn), qs, r, qg_ref[...])
                for hh in range(GQA):
                    dq_ref[:, pl.ds((g * GQA + hh) * HD, HD)] = dqs[hh * BLK:(hh + 1) * BLK]
                dqg = dqg + gq
            dk2 = jnp.concatenate(dks, axis=1)
            dv2 = jnp.concatenate(dvs, axis=1)
            fin_k[...] = pend_k[...] + dk2[:BLK]
            fin_v[...] = pend_v[...] + dv2[:BLK]
            pend_k[...] = dk2[BLK:]
            pend_v[...] = dv2[BLK:]
            dqg_ref[...] += dqg
            dsk_ref[...] += dsk

        @pl.when(j == NB)
        def _():
            fin_k[...] = pend_k[...]
            fin_v[...] = pend_v[...]

        dv_ref[...] = fin_v[...]
        kp = kp_ref[...]
        dkn = fin_k[...]
        dkg = jnp.zeros((1, HD), f32)
        for g in range(N_KV):
            kh = kp[:, g * HD:(g + 1) * HD]
            dkh, gk = _norm_bwd(dkn[:, g * HD:(g + 1) * HD], kh, _rstd(kh), kg_ref[...])
            dk_ref[:, pl.ds(g * HD, HD)] = dkh
            dkg = dkg + gk
        dkg_ref[...] += dkg

    cur = lambda col: (lambda b, j: (b * NB + jnp.minimum(j, NB - 1), col))
    prev = lambda col: (lambda b, j: (b * NB + jnp.maximum(j - 1, 0), col))
    small = lambda shape: pl.BlockSpec(shape, lambda b, j: (0, 0))
    return _run("attn_bwd", body, (BL, NB + 1), [proj, proj, proj, proj, proj, d_attn, qg, kg, sink_rows],
                [pl.BlockSpec((BLK, ATT_W), cur(0)),
                 pl.BlockSpec((BLK, KV_W), cur(4)), pl.BlockSpec((BLK, KV_W), prev(4)),
                 pl.BlockSpec((BLK, KV_W), cur(5)), pl.BlockSpec((BLK, KV_W), prev(5)),
                 pl.BlockSpec((BLK, ATT_W), cur(0)), small((1, HD)), small((1, HD)), small((8, 128))],
                [SDS((T, ATT_W), f32), SDS((T, KV_W), f32), SDS((T, KV_W), f32), SDS((1, HD), f32),
                 SDS((1, HD), f32), SDS((8, 128), f32)],
                [pl.BlockSpec((BLK, ATT_W), cur(0)), pl.BlockSpec((BLK, KV_W), prev(0)),
                 pl.BlockSpec((BLK, KV_W), prev(0)), small((1, HD)), small((1, HD)), small((8, 128))],
                scratch=[pltpu.VMEM((BLK, KV_W), f32)] * 4, exchange=exchange)


def mem_conv_bwd(d_mem_out, d_conv_out, proj, km, vm, conv_w8, conv_b, mqg, S, tm, exchange):
    T = d_mem_out.shape[0]
    NM = km.shape[0] // (T // S)

    def body(dmo_ref, dco_ref, ch_ref, cb_ref, cc_ref, chh_ref, cch_ref, qm_ref, km_ref, vm_ref, cw_ref, cbias_ref,
             mqg_ref, dqm_ref, dkm_ref, dvm_ref, dmqg_ref, dcb_ref, dcv_ref, dcw_ref, dcbias_ref):
        i = pl.program_id(0)
        first = (i * tm) % S == 0

        @pl.when(i == 0)
        def _():
            dmqg_ref[...] = jnp.zeros_like(dmqg_ref)
            dcw_ref[...] = jnp.zeros_like(dcw_ref)
            dcbias_ref[...] = jnp.zeros_like(dcbias_ref)

        @pl.when(first)
        def _():
            dkm_ref[...] = jnp.zeros_like(dkm_ref)
            dvm_ref[...] = jnp.zeros_like(dvm_ref)

        qm, kmv, vmv, dmo = qm_ref[...], km_ref[...], vm_ref[...], dmo_ref[...]
        dmqg = jnp.zeros((1, HD), f32)
        for h in range(N_MEMH):
            qh, r, qn, kh, vh, p = _mem_heads(qm, kmv, vmv, mqg_ref[...], h)
            doh = _c(dmo[:, h * HD:(h + 1) * HD])
            dp = _nt(doh, vh)
            ds = _c(p * (dp - jnp.sum(p * dp, axis=-1, keepdims=True)) * (HD ** -0.5))
            dvm_ref[:, pl.ds(h * HD, HD)] += _tn(_c(p), doh)
            dkm_ref[:, pl.ds(h * HD, HD)] += _tn(ds, _c(qn))
            dqh, gq = _norm_bwd(_nn(ds, kh), qh, r, mqg_ref[...])
            dqm_ref[:, pl.ds(h * HD, HD)] = dqh
            dmqg = dmqg + gq
        dmqg_ref[...] += dmqg

        u = cc_ref[...] * ch_ref[...]
        uh = jnp.where(first, 0.0, cch_ref[...] * chh_ref[...])
        u1, u2 = _conv_taps(u, uh)
        conv = cw_ref[0:1, :] * u2 + cw_ref[1:2, :] * u1 + cw_ref[2:3, :] * u + cbias_ref[...]
        dy = dco_ref[...]
        dcb_ref[...] = dy * conv
        dcv = dy * cb_ref[...]
        dcv_ref[...] = dcv
        dcbias_ref[...] += jnp.sum(dcv, axis=0, keepdims=True)
        dcw_ref[0:1, :] += jnp.sum(dcv * u2, axis=0, keepdims=True)
        dcw_ref[1:2, :] += jnp.sum(dcv * u1, axis=0, keepdims=True)
        dcw_ref[2:3, :] += jnp.sum(dcv * u, axis=0, keepdims=True)

    tile = lambda w, col: pl.BlockSpec((tm, w), lambda i: (i, col))
    halo = lambda col: pl.BlockSpec((8, CONV_W), lambda i: (jnp.maximum(i * (tm // 8) - 1, 0), col))
    seq = pl.BlockSpec((NM, MEM_W), lambda i: ((i * tm) // S, 0))
    small = lambda a: pl.BlockSpec(a.shape, lambda i: (0, 0))
    return _run("mem_conv_bwd", body, (T // tm,),
                [d_mem_out, d_conv_out, proj, proj, proj, proj, proj, proj, km, vm, conv_w8, conv_b, mqg],
                [tile(MEM_W, 0), tile(CONV_W, 0), tile(CONV_W, 3), tile(CONV_W, 4), tile(CONV_W, 5), halo(3), halo(5),
                 tile(MEM_W, 6), seq, seq, small(conv_w8), small(conv_b), small(mqg)],
                [SDS((T, MEM_W), f32), SDS(km.shape, f32), SDS(km.shape, f32), SDS(mqg.shape, f32),
                 SDS((T, CONV_W), f32), SDS((T, CONV_W), f32), SDS(conv_w8.shape, f32), SDS(conv_b.shape, f32)],
                [tile(MEM_W, 0), seq, seq, small(mqg), tile(CONV_W, 0), tile(CONV_W, 0), small(conv_w8), small(conv_b)],
                exchange=exchange)


def in_proj_bwd(dq, dk, dv, dcb, dcv, dqm, proj, conv_w8, xn, x2d, dx1, g1, winT, S, tm):
    T, D = x2d.shape
    P = winT.shape[0]
    last_blk = T // 8 - 1

    def body(dq_ref, dk_ref, dv_ref, dcb_ref, dcv_ref, dcvn_ref, dqm_ref, ch_ref, cc_ref, cw_ref, xn_ref, x_ref,
             dx1_ref, g_ref, w_ref, dx_ref, dw_ref, dg_ref):
        i = pl.program_id(0)

        @pl.when(i == 0)
        def _():
            dw_ref[...] = jnp.zeros_like(dw_ref)
            dg_ref[...] = jnp.zeros_like(dg_ref)

        last = ((i + 1) * tm) % S == 0
        dcv = dcv_ref[...]
        nxt = jnp.where(last, 0.0, dcvn_ref[...])
        row = lax.broadcasted_iota(jnp.int32, dcv.shape, 0)
        n1 = jnp.where(row == tm - 1, nxt[0:1, :], pltpu.roll(dcv, tm - 1, 0))
        n2 = jnp.where(row == tm - 2, nxt[0:1, :], jnp.where(row == tm - 1, nxt[1:2, :], pltpu.roll(dcv, tm - 2, 0)))
        du = cw_ref[2:3, :] * dcv + cw_ref[1:2, :] * n1 + cw_ref[0:1, :] * n2
        d_proj = jnp.concatenate([_c(dq_ref[...]), _c(dk_ref[...]), _c(dv_ref[...]), _c(du * cc_ref[...]),
                                  _c(dcb_ref[...]), _c(du * ch_ref[...]), _c(dqm_ref[...])], axis=1)
        dw_ref[...] += _tn(d_proj, xn_ref[...])
        xv = x_ref[...]
        dv_, dg = _norm_bwd(_nn(d_proj, w_ref[...]), xv, _rstd(xv), g_ref[...])
        dx_ref[...] = dx1_ref[...] + dv_
        dg_ref[...] += dg

    tile = lambda w, col=0: pl.BlockSpec((tm, w), lambda i: (i, col))
    nhalo = pl.BlockSpec((8, CONV_W), lambda i: (jnp.minimum((i + 1) * (tm // 8), last_blk), 0))
    small = lambda a: pl.BlockSpec(a.shape, lambda i: (0, 0))
    return _run("in_proj_bwd", body, (T // tm,),
                [dq, dk, dv, dcb, dcv, dcv, dqm, proj, proj, conv_w8, xn, x2d, dx1, g1, winT],
                [tile(ATT_W), tile(KV_W), tile(KV_W), tile(CONV_W), tile(CONV_W), nhalo, tile(MEM_W),
                 tile(CONV_W, 3), tile(CONV_W, 5), small(conv_w8), tile(D), tile(D), tile(D), small(g1), VM],
                [SDS((T, D), f32), SDS((P, D), f32), SDS(g1.shape, f32)],
                [tile(D), pl.BlockSpec((P, D), lambda i: (0, 0)), small(g1)], vmem_mib=48)


def mem_kv_bwd(dkm, dvm, kv, memn, mem2d, g_mem, mkg, wmkv):
    def body(dkm_ref, dvm_ref, kv_ref, mn_ref, m_ref, g_ref, kg_ref, w_ref, dw_ref, dg_ref, dkg_ref):
        kv_ = kv_ref[...]
        dkn = dkm_ref[...]
        dkg = jnp.zeros((1, HD), f32)
        parts = []
        for h in range(N_MEMH):
            kh = kv_[:, h * HD:(h + 1) * HD]
            dkh, gk = _norm_bwd(dkn[:, h * HD:(h + 1) * HD], kh, _rstd(kh), kg_ref[...])
            parts.append(dkh)
            dkg = dkg + gk
        dkg_ref[...] = dkg
        dkv = _c(jnp.concatenate(parts + [dvm_ref[...]], axis=1))
        dw_ref[...] = _tn(mn_ref[...], dkv)
        mv = m_ref[...]
        dg_ref[...] = jnp.sum(_nt(dkv, w_ref[...]) * mv * _rstd(mv), axis=0, keepdims=True)

    return _run("mem_kv_bwd", body, (), [dkm, dvm, kv, memn, mem2d, g_mem, mkg, wmkv], [VM] * 8,
                [SDS(wmkv.shape, f32), SDS(g_mem.shape, f32), SDS(mkg.shape, f32)], [VM] * 3, vmem_mib=40)


def _halves_view(g):
    return g.reshape(4, 2, g.shape[0] // 8, g.shape[1])


def kernel(x, mem, norm_mix, w_in, q_norm, k_norm, attn_sinks, conv_w, conv_b, norm_mem, w_mem_kv, mem_q_norm, mem_k_norm, out_norm_attn, out_norm_conv, out_norm_mem, w_out, norm_ffn, w_gate, w_up, w_down, loss_target, m_norm_mix, m_w_in, m_q_norm, m_k_norm, m_attn_sinks, m_conv_w, m_conv_b, m_norm_mem, m_w_mem_kv, m_mem_q_norm, m_mem_k_norm, m_out_norm_attn, m_out_norm_conv, m_out_norm_mem, m_w_out, m_norm_ffn, m_w_gate, m_w_up, m_w_down, v_norm_mix, v_w_in, v_q_norm, v_k_norm, v_attn_sinks, v_conv_w, v_conv_b, v_norm_mem, v_w_mem_kv, v_mem_q_norm, v_mem_k_norm, v_out_norm_attn, v_out_norm_conv, v_out_norm_mem, v_w_out, v_norm_ffn, v_w_gate, v_w_up, v_w_down):
    BL, S, D = x.shape
    T = BL * S
    TM = 256
    _, _, ci = _place()
    cidx = ci.reshape(1).astype(jnp.int32)

    rowblocks = lambda a, b, c, d, e, f: [a[0].T, b[0].T, c[0].T, d[0], e[0], f[0]]
    w_rb = rowblocks(w_in, w_gate, w_up, w_down, w_out, w_mem_kv)
    m_rb = rowblocks(m_w_in, m_w_gate, m_w_up, m_w_down, m_w_out, m_w_mem_kv)
    v_rb = rowblocks(v_w_in, v_w_gate, v_w_up, v_w_down, v_w_out, v_w_mem_kv)
    winT_s, wgT_s, wuT_s, wd_s, wout_s, wmkv_s = prep_weights(w_rb)
    cw_pad = jnp.zeros((8, 128), f32).at[:3, :HD].set(conv_w[0])
    _, (winT, cw_all) = _run("gather_w_in", None, (), [], [], [], [], exchange=gather_exchange([winT_s, cw_pad], [True, False]))
    conv_w_full = jnp.transpose(cw_all.reshape(4, 8, 128)[:, :3, :HD], (1, 0, 2)).reshape(3, CONV_W)
    conv_w8 = jnp.zeros((8, CONV_W), f32).at[:3].set(conv_w_full)
    sink_rows = jnp.broadcast_to(attn_sinks.reshape(N_Q, 1), (N_Q, 128))

    x2d = x.reshape(T, D)
    mem2d = mem.reshape(-1, D)
    (xn, proj), (wmkv, wout) = in_proj_fwd(x2d, norm_mix, winT, TM, gather_exchange([wmkv_s, wout_s], [True, True]))
    memn, kv, km, vm = mem_kv_fwd(mem2d, norm_mem, mem_k_norm, wmkv)
    (attn_out,), (wgT, wuT) = attn_fwd(proj, q_norm, k_norm, sink_rows, BL, S, gather_exchange([wgT_s, wuT_s], [True, True]))
    (conv_out, mem_out, merged, x1, h), (wd,) = mixer_tail_fwd(
        x2d, attn_out, proj, km, vm, conv_w8, conv_b, out_norm_attn, out_norm_conv, out_norm_mem, mem_q_norm, wout,
        norm_ffn, S, TM, gather_exchange([wd_s], [True]))

    dx1, dx2b, act, d_gate, d_up, loss8, d_norm_ffn = ffn_fwd_bwd(h, x1, loss_target.reshape(T, D), wgT, wuT, wd, norm_ffn, TM)
    F = wd.shape[0]
    g_wd = matmul_tn(act, dx2b, "dw_down", F // 2, 512)
    g_wgT = matmul_tn(d_gate, h, "dw_gate", F // 2, 512)
    g_wuT = matmul_tn(d_up, h, "dw_up", F // 2, 512)

    d_attn, d_conv_out, d_mem_out, g_wout, d_gains = out_proj_bwd(
        dx1, merged, attn_out, conv_out, mem_out, out_norm_attn, out_norm_conv, out_norm_mem, wout, TM)
    late = [_halves_view(g) for g in (g_wgT, g_wuT, g_wd, g_wout)]
    (dqm, dkm, dvm, d_mqg, dcb, dcv, d_cw8, d_cbias), late_sib = mem_conv_bwd(
        d_mem_out, d_conv_out, proj, km, vm, conv_w8, conv_b, mem_q_norm, S, TM, halves_exchange(late))
    late_part = add_halves(cidx, late, late_sib, "grad_add_halves_ffn")
    (dq, dk, dv, d_qg, d_kg, d_sink8), late_stage = attn_bwd(proj, d_attn, q_norm, k_norm, sink_rows, BL, S,
                                                             scatter_exchange(late_part))
    g_x, g_winT, d_norm_mix = in_proj_bwd(dq, dk, dv, dcb, dcv, dqm, proj, conv_w8, xn, x2d, dx1, norm_mix, winT, S, TM)
    g_wmkv, d_norm_mem, d_mkg = mem_kv_bwd(dkm, dvm, kv, memn, mem2d, norm_mem, mem_k_norm, wmkv)

    tail = [_halves_view(g) for g in (g_winT, g_wmkv)]
    _, tail_sib = _run("grad_halves_tail", None, (), [], [], [], [], exchange=halves_exchange(tail))
    tail_part = add_halves(cidx, tail, tail_sib, "grad_add_halves_tail")
    late_res, tail_stage = adamw_big("adamw_late", late_stage, w_rb[1:5], m_rb[1:5], v_rb[1:5], 8,
                                     exchange=scatter_exchange(tail_part))
    tail_res, _ = adamw_big("adamw_tail", tail_stage, [w_rb[0], w_rb[5]], [m_rb[0], m_rb[5]], [v_rb[0], v_rb[5]], 4)
    res = {"w_in": [a.T[None] for a in tail_res[0]], "w_gate": [a.T[None] for a in late_res[0]],
           "w_up": [a.T[None] for a in late_res[1]], "w_down": [a[None] for a in late_res[2]],
           "w_out": [a[None] for a in late_res[3]], "w_mem_kv": [a[None] for a in tail_res[1]]}

    tot = allreduce_small(d_norm_mix, d_norm_mem, d_norm_ffn, d_gains, d_cw8, d_cbias, d_qg, d_kg, d_mqg, d_mkg, d_sink8, loss8)
    loss = tot[5, 384]
    w_small = dict(norm_mix=norm_mix, norm_mem=norm_mem, norm_ffn=norm_ffn, out_norm_attn=out_norm_attn,
                   out_norm_conv=out_norm_conv, out_norm_mem=out_norm_mem, conv_w=conv_w, conv_b=conv_b, q_norm=q_norm,
                   k_norm=k_norm, mem_q_norm=mem_q_norm, mem_k_norm=mem_k_norm, attn_sinks=attn_sinks)
    m_small = dict(norm_mix=m_norm_mix, norm_mem=m_norm_mem, norm_ffn=m_norm_ffn, out_norm_attn=m_out_norm_attn,
                   out_norm_conv=m_out_norm_conv, out_norm_mem=m_out_norm_mem, conv_w=m_conv_w, conv_b=m_conv_b,
                   q_norm=m_q_norm, k_norm=m_k_norm, mem_q_norm=m_mem_q_norm, mem_k_norm=m_mem_k_norm,
                   attn_sinks=m_attn_sinks)
    v_small = dict(norm_mix=v_norm_mix, norm_mem=v_norm_mem, norm_ffn=v_norm_ffn, out_norm_attn=v_out_norm_attn,
                   out_norm_conv=v_out_norm_conv, out_norm_mem=v_out_norm_mem, conv_w=v_conv_w, conv_b=v_conv_b,
                   q_norm=v_q_norm, k_norm=v_k_norm, mem_q_norm=v_mem_q_norm, mem_k_norm=v_mem_k_norm,
                   attn_sinks=v_attn_sinks)
    res.update(adamw_small(tot, w_small, m_small, v_small))

    order = ["norm_mix", "w_in", "q_norm", "k_norm", "attn_sinks", "conv_w", "conv_b", "norm_mem", "w_mem_kv",
             "mem_q_norm", "mem_k_norm", "out_norm_attn", "out_norm_conv", "out_norm_mem", "w_out", "norm_ffn",
             "w_gate", "w_up", "w_down"]
    return (loss, g_x.reshape(BL, S, D), *[res[n][0] for n in order], *[res[n][1] for n in order],
            *[res[n][2] for n in order], *[res[n][3] for n in order])
```

```python
import collections
import functools

import jax
import jax.numpy as jnp
from jax import lax
from jax.experimental import pallas as pl
from jax.experimental.pallas import tpu as pltpu

f32 = jnp.float32
MXU = jnp.bfloat16
WIRE = jnp.bfloat16
EPS = 1e-6
NEG = -1e30
HD = 64
BLK = 128
N_Q, N_KV, N_MEMH = 8, 2, 4
GQA = N_Q // N_KV
ATT_W, KV_W, CONV_W, MEM_W = 512, 128, 256, 256
VMEM_MIB = 1024 * 1024
ADAM_LR, ADAM_B1, ADAM_B2, ADAM_EPS, ADAM_WD, ADAM_STEP = 0.001, 0.9, 0.999, 1e-08, 0.01, 10

MESH = pl.DeviceIdType.MESH
VM = pl.BlockSpec(memory_space=pltpu.VMEM)
ANY = pl.BlockSpec(memory_space=pl.ANY)
SDS = jax.ShapeDtypeStruct
DMA = pltpu.SemaphoreType.DMA


def _c(v):
    return v.astype(MXU)


def _nn(a, b):
    return lax.dot_general(a, b, (((1,), (0,)), ((), ())), preferred_element_type=f32)


def _nt(a, b):
    return lax.dot_general(a, b, (((1,), (1,)), ((), ())), preferred_element_type=f32)


def _tn(a, b):
    return lax.dot_general(a, b, (((0,), (0,)), ((), ())), preferred_element_type=f32)


def _rstd(v):
    return lax.rsqrt(jnp.mean(v * v, axis=-1, keepdims=True) + EPS)


def _norm_bwd(dy, v, r, g):
    dyg = dy * g
    dv = r * dyg - v * (r * r * r) * jnp.mean(dyg * v, axis=-1, keepdims=True)
    return dv, jnp.sum(dy * v * r, axis=0, keepdims=True)


def _split3(v):
    hi = _c(v)
    r1 = v - hi.astype(f32)
    mid = _c(r1)
    return hi, mid, _c(r1 - mid.astype(f32))


def _rowsum_mxu(v, width):
    ones = jnp.ones((v.shape[1], width), MXU)
    return sum(_nn(a, ones) for a in _split3(v))


def _seg_sums(v):
    r = lax.broadcasted_iota(jnp.int32, (2 * HD, 2 * HD), 0) // HD
    c = lax.broadcasted_iota(jnp.int32, (2 * HD, 2 * HD), 1) // HD
    bd = (r == c).astype(MXU)
    outs = []
    for b in range(v.shape[1] // (2 * HD)):
        outs.append(sum(_nn(a, bd) for a in _split3(v[:, b * 2 * HD:(b + 1) * 2 * HD])))
    return outs[0] if len(outs) == 1 else jnp.concatenate(outs, axis=1)


def _lanes(g, width):
    return jnp.concatenate([g] * (width // HD), axis=1)


def _heads_rstd(v):
    return lax.rsqrt(_seg_sums(v * v) * (1.0 / HD) + EPS)


def _heads_norm_bwd(dy, v, g):
    r = _heads_rstd(v)
    gl = _lanes(g, v.shape[1])
    dyg = dy * gl
    dv = r * dyg - v * (r * r * r) * (_seg_sums(dyg * v) * (1.0 / HD))
    dgl = jnp.sum(dy * v * r, axis=0, keepdims=True)
    return dv, sum(dgl[:, s * HD:(s + 1) * HD] for s in range(v.shape[1] // HD))


def _exp_scores(s, extra=None):
    m = jnp.max(s, axis=-1, keepdims=True)
    if extra is None:
        return jnp.exp(s - m), None
    m = jnp.maximum(m, extra)
    return jnp.exp(s - m), jnp.exp(extra - m)


def _place():
    return lax.axis_index("x"), lax.axis_index("y"), lax.axis_index("c")


def _other_chips(x, y):
    return [(1 - x, y), (x, 1 - y), (1 - x, 1 - y)]


Exchange = collections.namedtuple("Exchange", "ins outs sems start finish")


def _run(name, body, grid, ins, in_specs, out_shape, out_specs, scratch=(), vmem_mib=32, exchange=None):
    ins, in_specs, out_shape, out_specs, scratch = list(ins), list(in_specs), list(out_shape), list(out_specs), list(scratch)
    ni, no, ns = len(ins), len(out_shape), len(scratch)
    ex = exchange
    if ex is not None:
        nxi, nxo = len(ex.ins), len(ex.outs)

    def call_body(*refs):
        if ex is None:
            body(*refs)
            return
        a, xa = refs[:ni], refs[ni:ni + nxi]
        o, xo = refs[ni + nxi:ni + nxi + no], refs[ni + nxi + no:ni + nxi + no + nxo]
        s, xs = refs[ni + nxi + no + nxo:ni + nxi + no + nxo + ns], refs[ni + nxi + no + nxo + ns:]
        if grid:
            first = functools.reduce(jnp.logical_and, [pl.program_id(d) == 0 for d in range(len(grid))])
            last = functools.reduce(jnp.logical_and, [pl.program_id(d) == grid[d] - 1 for d in range(len(grid))])
            pl.when(first)(lambda: ex.start(xa, xo, xs))
            body(*a, *o, *s)
            pl.when(last)(lambda: ex.finish(xa, xo, xs))
        else:
            ex.start(xa, xo, xs)
            if body is not None:
                body(*a, *o, *s)
            ex.finish(xa, xo, xs)

    if ex is not None:
        ins, in_specs = ins + list(ex.ins), in_specs + [ANY] * nxi
        out_shape, out_specs = out_shape + list(ex.outs), out_specs + [ANY] * nxo
        scratch = scratch + list(ex.sems)
    kw = dict(grid=grid) if grid else {}
    res = pl.pallas_call(
        call_body, name=name, out_shape=out_shape, in_specs=in_specs, out_specs=out_specs, scratch_shapes=scratch,
        compiler_params=pltpu.CompilerParams(dimension_semantics=("arbitrary",) * len(grid) if grid else None,
                                             vmem_limit_bytes=vmem_mib * VMEM_MIB), **kw)(*ins)
    res = list(res)
    return (res[:no], res[no:]) if ex is not None else res


def _remote(src, dst, ssem, rsem, dev):
    return pltpu.make_async_remote_copy(src_ref=src, dst_ref=dst, send_sem=ssem, recv_sem=rsem,
                                        device_id=dev, device_id_type=MESH)


def gather_exchange(shards, split):
    n = len(shards)

    def rows(ref, e, kk, half=None):
        R = shards[e].shape[0]
        if half is None:
            return ref.at[pl.ds(pl.multiple_of(kk * R, 8), R)]
        return ref.at[pl.ds(pl.multiple_of(kk * R + half * (R // 2), 8), R // 2)]

    def ici(src, dst, sm, e, j, chip_j, x, y, c):
        k = 2 * x + y
        if split[e]:
            s = src[e].at[pl.ds(pl.multiple_of(c * (shards[e].shape[0] // 2), 8), shards[e].shape[0] // 2)]
            return _remote(s, rows(dst[e], e, k, c), sm[0].at[6 * e + j], sm[1].at[6 * e + j], (*chip_j, c))
        return _remote(src[e], rows(dst[e], e, k), sm[0].at[6 * e + j], sm[1].at[6 * e + j], (*chip_j, c))

    def landed(dst, e, chip_j, c):
        kj = 2 * chip_j[0] + chip_j[1]
        return rows(dst[e], e, kj, c) if split[e] else rows(dst[e], e, kj)

    def forward(dst, sm, e, j, chip_j, x, y, c, sender_c):
        kj = 2 * chip_j[0] + chip_j[1]
        r = rows(dst[e], e, kj, sender_c)
        return _remote(r, r, sm[0].at[6 * e + 3 + j], sm[1].at[6 * e + 3 + j], (x, y, 1 - c))

    def local(src, dst, sm, e, x, y):
        return pltpu.make_async_copy(src[e], rows(dst[e], e, 2 * x + y), sm[2].at[e])

    def start(src, dst, sm):
        x, y, c = _place()
        for e in range(n):
            local(src, dst, sm, e, x, y).start()
            for j, chip_j in enumerate(_other_chips(x, y)):
                ici(src, dst, sm, e, j, chip_j, x, y, c).start()

    def finish(src, dst, sm):
        x, y, c = _place()
        chips = _other_chips(x, y)
        for e in range(n):
            for j, chip_j in enumerate(chips):
                r = landed(dst, e, chip_j, c)
                _remote(r, r, sm[0].at[6 * e + j], sm[1].at[6 * e + j], (*chip_j, c)).wait_recv()
                if split[e]:
                    forward(dst, sm, e, j, chip_j, x, y, c, c).start()
        for e in range(n):
            for j, chip_j in enumerate(chips):
                if split[e]:
                    forward(dst, sm, e, j, chip_j, x, y, c, 1 - c).wait_recv()
        for e in range(n):
            for j, chip_j in enumerate(chips):
                ici(src, dst, sm, e, j, chip_j, x, y, c).wait_send()
                if split[e]:
                    forward(dst, sm, e, j, chip_j, x, y, c, c).wait_send()
            local(src, dst, sm, e, x, y).wait()

    outs = [SDS((4 * s.shape[0], s.shape[1]), s.dtype) for s in shards]
    return Exchange(list(shards), outs, [DMA((6 * n,)), DMA((6 * n,)), DMA((n,))], start, finish)


def halves_exchange(grads):
    n = len(grads)

    def copy(g, st, sm, e, x, y, c):
        return _remote(g[e].at[:, 1 - c], st[e], sm[0].at[e], sm[1].at[e], (x, y, 1 - c))

    def start(g, st, sm):
        x, y, c = _place()
        for e in range(n):
            copy(g, st, sm, e, x, y, c).start()

    def finish(g, st, sm):
        x, y, c = _place()
        for e in range(n):
            copy(g, st, sm, e, x, y, c).wait()

    outs = [SDS((4,) + a.shape[2:], a.dtype) for a in grads]
    return Exchange(list(grads), outs, [DMA((n,)), DMA((n,))], start, finish)


def scatter_exchange(parts):
    n = len(parts)

    def ici(p, st, sm, e, j, chip_j, x, y, c):
        k, kj = 2 * x + y, 2 * chip_j[0] + chip_j[1]
        return _remote(p[e].at[kj], st[e].at[c, k], sm[0].at[8 * e + j], sm[1].at[8 * e + j], (*chip_j, c))

    def own(p, st, sm, e, x, y, c):
        k = 2 * x + y
        return _remote(p[e].at[k], st[e].at[c, k], sm[0].at[8 * e + 3], sm[1].at[8 * e + 3], (x, y, 1 - c))

    def forward(st, sm, e, j, chip_j, x, y, c, sender_c):
        kj = 2 * chip_j[0] + chip_j[1]
        r = st[e].at[sender_c, kj]
        return _remote(r, r, sm[0].at[8 * e + 4 + j], sm[1].at[8 * e + 4 + j], (x, y, 1 - c))

    def local(p, st, sm, e, x, y, c):
        k = 2 * x + y
        return pltpu.make_async_copy(p[e].at[k], st[e].at[c, k], sm[2].at[e])

    def start(p, st, sm):
        x, y, c = _place()
        for e in range(n):
            local(p, st, sm, e, x, y, c).start()
            own(p, st, sm, e, x, y, c).start()
            for j, chip_j in enumerate(_other_chips(x, y)):
                ici(p, st, sm, e, j, chip_j, x, y, c).start()

    def finish(p, st, sm):
        x, y, c = _place()
        k = 2 * x + y
        chips = _other_chips(x, y)
        for e in range(n):
            for j, chip_j in enumerate(chips):
                kj = 2 * chip_j[0] + chip_j[1]
                r = st[e].at[c, kj]
                _remote(r, r, sm[0].at[8 * e + j], sm[1].at[8 * e + j], (*chip_j, c)).wait_recv()
                forward(st, sm, e, j, chip_j, x, y, c, c).start()
        for e in range(n):
            r = st[e].at[1 - c, k]
            _remote(r, r, sm[0].at[8 * e + 3], sm[1].at[8 * e + 3], (x, y, 1 - c)).wait_recv()
            for j, chip_j in enumerate(chips):
                forward(st, sm, e, j, chip_j, x, y, c, 1 - c).wait_recv()
        for e in range(n):
            own(p, st, sm, e, x, y, c).wait_send()
            for j, chip_j in enumerate(chips):
                ici(p, st, sm, e, j, chip_j, x, y, c).wait_send()
                forward(st, sm, e, j, chip_j, x, y, c, c).wait_send()
            local(p, st, sm, e, x, y, c).wait()

    outs = [SDS((2,) + a.shape, a.dtype) for a in parts]
    return Exchange(list(parts), outs, [DMA((8 * n,)), DMA((8 * n,)), DMA((n,))], start, finish)


def allreduce_small(d_norm_mix, d_norm_mem, d_norm_ffn, d_gains, d_cw8, d_cbias, d_qg, d_kg, d_mqg, d_mkg, d_sink8, loss8):
    def body(nm_ref, nmem_ref, nf_ref, gn_ref, cw_ref, cb_ref, qg_ref, kg_ref, mqg_ref, mkg_ref, sk_ref, ls_ref,
             o_ref, buf, ssem, rsem):
        x, y, c = _place()
        me = 4 * x + 2 * y + c
        mine = buf.at[me]
        mine[...] = jnp.zeros((8, 1024), f32)
        mine[0:1, :] = nm_ref[...]
        mine[1:2, :] = nmem_ref[...]
        mine[2:3, :] = nf_ref[...]
        mine[3:4, :] = gn_ref[...]
        for j in range(3):
            mine[4:5, pl.ds(j * CONV_W, CONV_W)] = cw_ref[j:j + 1, :]
        mine[4:5, pl.ds(3 * CONV_W, CONV_W)] = cb_ref[...]
        for j, r in enumerate((qg_ref, kg_ref, mqg_ref, mkg_ref)):
            mine[5:6, pl.ds(j * HD, HD)] = r[...]
        mine[5:6, pl.ds(256, 128)] = sk_ref[0:1, :]
        mine[5:6, pl.ds(384, 128)] = ls_ref[0:1, :]

        def peer_of(m):
            return (1 - x if m & 4 else x, 1 - y if m & 2 else y, 1 - c if m & 1 else c)

        for m in range(1, 8):
            _remote(mine, mine, ssem.at[m - 1], rsem.at[m - 1], peer_of(m)).start()
        for m in range(1, 8):
            p = peer_of(m)
            got = buf.at[4 * p[0] + 2 * p[1] + p[2]]
            _remote(got, got, ssem.at[m - 1], rsem.at[m - 1], p).wait_recv()
        for m in range(1, 8):
            _remote(mine, mine, ssem.at[m - 1], rsem.at[m - 1], peer_of(m)).wait_send()
        acc = buf[0]
        for d in range(1, 8):
            acc = acc + buf[d]
        o_ref[...] = acc

    ins = [d_norm_mix, d_norm_mem, d_norm_ffn, d_gains, d_cw8, d_cbias, d_qg, d_kg, d_mqg, d_mkg, d_sink8, loss8]
    return _run("allreduce_small", body, (), ins, [VM] * len(ins), [SDS((8, 1024), f32)], [VM],
                scratch=[pltpu.VMEM((8, 8, 1024), f32), DMA((7,)), DMA((7,))])[0]


def add_halves(cidx, grads, stages, name, nch=2):
    n = len(grads)

    def body(c_ref, *refs):
        g, st, o = refs[:n], refs[n:2 * n], refs[2 * n:]
        for e in range(n):
            o[e][...] = (g[e][...] + st[e][...]).astype(WIRE)

    in_specs, out_specs, out_shape = [], [], []
    for a in grads:
        hr, C = a.shape[2], a.shape[3]
        in_specs.append(pl.BlockSpec((None, None, hr // nch, C), lambda s, q, c_ref: (s, c_ref[0], q, 0)))
    for a in stages:
        hr, C = a.shape[1], a.shape[2]
        in_specs.append(pl.BlockSpec((None, hr // nch, C), lambda s, q, c_ref: (s, q, 0)))
        out_specs.append(pl.BlockSpec((None, hr // nch, C), lambda s, q, c_ref: (s, q, 0)))
        out_shape.append(SDS(a.shape, WIRE))
    return pl.pallas_call(
        body, name=name, out_shape=out_shape,
        grid_spec=pltpu.PrefetchScalarGridSpec(num_scalar_prefetch=1, grid=(4, nch), in_specs=in_specs, out_specs=out_specs),
        compiler_params=pltpu.CompilerParams(dimension_semantics=("arbitrary", "arbitrary")),
    )(cidx, *grads, *stages)


def _adamw_math(w, g, m, v):
    m = ADAM_B1 * m + (1.0 - ADAM_B1) * g
    v = ADAM_B2 * v + (1.0 - ADAM_B2) * (g * g)
    m_hat = m / (1.0 - ADAM_B1 ** ADAM_STEP)
    v_hat = v / (1.0 - ADAM_B2 ** ADAM_STEP)
    delta = -ADAM_LR * (m_hat / (jnp.sqrt(v_hat) + ADAM_EPS) + ADAM_WD * w)
    return delta, m, v


def _sum_chips(st):
    return ((st[0].astype(f32) + st[1].astype(f32)) + st[2].astype(f32)) + st[3].astype(f32)


def adamw_big(name, stages, ws, ms, vs, nstep, exchange=None):
    n = len(stages)

    def body(*refs):
        st, w, m, v = refs[:n], refs[n:2 * n], refs[2 * n:3 * n], refs[3 * n:4 * n]
        outs = refs[4 * n:]
        for e in range(n):
            g = jnp.concatenate([_sum_chips(st[e].at[0]), _sum_chips(st[e].at[1])], axis=0)
            d, mm, vv = _adamw_math(w[e][...], g, m[e][...], v[e][...])
            outs[4 * e][...] = g
            outs[4 * e + 1][...] = d
            outs[4 * e + 2][...] = mm
            outs[4 * e + 3][...] = vv

    st_specs, w_specs = [], []
    for e in range(n):
        _, _, hr, C = stages[e].shape
        st_specs.append(pl.BlockSpec((2, 4, hr, C // nstep), lambda i: (0, 0, 0, i)))
        w_specs.append(pl.BlockSpec((2 * hr, C // nstep), lambda i: (0, i)))
    out_specs = [s for s in w_specs for _ in range(4)]
    out_shape = [SDS(w.shape, f32) for w in ws for _ in range(4)]
    res = _run(name, body, (nstep,), list(stages) + list(ws) + list(ms) + list(vs), st_specs + w_specs * 3,
               out_shape, out_specs, vmem_mib=48, exchange=exchange)
    res, sent = res if exchange is not None else (res, None)
    return [res[4 * e:4 * e + 4] for e in range(n)], sent


SMALL = ("norm_mix", "norm_mem", "norm_ffn", "out_norm_attn", "out_norm_conv", "out_norm_mem", "conv_w", "conv_b",
         "q_norm", "k_norm", "mem_q_norm", "mem_k_norm", "attn_sinks")


def adamw_small(tot, ws, ms, vs):
    ns = len(SMALL)

    def grad_of(name, tot_ref, chip):
        where = {"norm_mix": (0, 0, 1024), "norm_mem": (1, 0, 1024), "norm_ffn": (2, 0, 1024),
                 "out_norm_attn": (3, 0, ATT_W), "out_norm_conv": (3, ATT_W, CONV_W),
                 "out_norm_mem": (3, ATT_W + CONV_W, MEM_W), "conv_b": (4, 3 * CONV_W, CONV_W), "q_norm": (5, 0, HD),
                 "k_norm": (5, HD, HD), "mem_q_norm": (5, 2 * HD, HD), "mem_k_norm": (5, 3 * HD, HD),
                 "attn_sinks": (5, 256, N_Q)}
        if name != "conv_w":
            r, c0, w = where[name]
            return tot_ref[r:r + 1, c0:c0 + w]
        taps = []
        for j in range(3):
            mine = tot_ref[4:5, j * CONV_W:j * CONV_W + HD]
            for s in range(1, 4):
                mine = jnp.where(chip == s, tot_ref[4:5, j * CONV_W + s * HD:j * CONV_W + (s + 1) * HD], mine)
            taps.append(mine)
        return jnp.concatenate(taps, axis=0)[None]

    def body(tot_ref, *refs):
        w, m, v, outs = refs[:ns], refs[ns:2 * ns], refs[2 * ns:3 * ns], refs[3 * ns:]
        x, y, _ = _place()
        for i, name in enumerate(SMALL):
            g = grad_of(name, tot_ref, 2 * x + y)
            d, mm, vv = _adamw_math(w[i][...], g, m[i][...], v[i][...])
            outs[4 * i][...] = g
            outs[4 * i + 1][...] = d
            outs[4 * i + 2][...] = mm
            outs[4 * i + 3][...] = vv

    ins = [tot] + [d[k] for d in (ws, ms, vs) for k in SMALL]
    out_shape = [SDS(ws[k].shape, f32) for k in SMALL for _ in range(4)]
    res = _run("adamw_small", body, (), ins, [VM] * len(ins), out_shape, [VM] * len(out_shape))
    return {k: res[4 * i:4 * i + 4] for i, k in enumerate(SMALL)}


def prep_weights(shards):
    n = len(shards)

    def body(*refs):
        for e in range(n):
            refs[n + e][...] = _c(refs[e][...])

    return _run("prep_weights", body, (), shards, [VM] * n, [SDS(a.shape, MXU) for a in shards], [VM] * n, vmem_mib=48)


def mem_kv_fwd(mem2d, g_mem, mkg, wmkv):
    M, D = mem2d.shape

    def body(m_ref, g_ref, kg_ref, w_ref, mn_ref, kv_ref, km_ref, vm_ref):
        m = m_ref[...]
        mn = _c(m * _rstd(m) * g_ref[...])
        mn_ref[...] = mn
        kv = _nn(mn, w_ref[...])
        kv_ref[...] = kv
        kk = kv[:, :MEM_W]
        km_ref[...] = _c(kk * _heads_rstd(kk) * _lanes(kg_ref[...], MEM_W))
        vm_ref[...] = _c(kv[:, MEM_W:])

    return _run("mem_kv_fwd", body, (), [mem2d, g_mem, mkg, wmkv], [VM] * 4,
                [SDS((M, D), MXU), SDS((M, 2 * MEM_W), f32), SDS((M, MEM_W), MXU), SDS((M, MEM_W), MXU)], [VM] * 4)


QKV_W = ATT_W + 2 * KV_W + MEM_W


def in_proj_fwd(x2d, g1, winT, qg, kg, mqg, tm, exchange):
    T, D = x2d.shape
    P = winT.shape[0]

    def body(x_ref, g_ref, w_ref, qg_ref, kg_ref, mqg_ref, xn_ref, proj_ref, qkv_ref):
        xv = x_ref[...]
        xn = _c(xv * _rstd(xv) * g_ref[...])
        xn_ref[...] = xn
        proj = _nt(xn, w_ref[...])
        proj_ref[...] = proj
        q, k = proj[:, :ATT_W], proj[:, ATT_W:ATT_W + KV_W]
        qm = proj[:, P - MEM_W:]
        qkv_ref[...] = jnp.concatenate(
            [_c(q * _heads_rstd(q) * _lanes(qg_ref[...], ATT_W)), _c(k * _heads_rstd(k) * _lanes(kg_ref[...], KV_W)),
             _c(proj[:, ATT_W + KV_W:ATT_W + 2 * KV_W]), _c(qm * _heads_rstd(qm) * _lanes(mqg_ref[...], MEM_W))], axis=1)

    small = lambda a: pl.BlockSpec(a.shape, lambda i: (0, 0))
    return _run("in_proj_fwd", body, (T // tm,), [x2d, g1, winT, qg, kg, mqg],
                [pl.BlockSpec((tm, D), lambda i: (i, 0)), small(g1), VM, small(qg), small(kg), small(mqg)],
                [SDS((T, D), MXU), SDS((T, P), f32), SDS((T, QKV_W), MXU)],
                [pl.BlockSpec((tm, D), lambda i: (i, 0)), pl.BlockSpec((tm, P), lambda i: (i, 0)),
                 pl.BlockSpec((tm, QKV_W), lambda i: (i, 0))],
                vmem_mib=40, exchange=exchange)


def _swa_setup(j, g, sk_ref):
    rows = GQA * BLK
    ri = lax.broadcasted_iota(jnp.int32, (rows, 2 * BLK), 0)
    ki = lax.broadcasted_iota(jnp.int32, (rows, 2 * BLK), 1)
    dist = (ri & (BLK - 1)) + BLK - ki
    valid = (dist >= 0) & (dist < BLK) & ((ki >= BLK) | (j > 0))
    hrow = lax.broadcasted_iota(jnp.int32, (rows, 1), 0) // BLK
    slope = jnp.zeros((rows, 1), f32)
    sink = jnp.zeros((rows, 1), f32)
    for hh in range(GQA):
        h = g * GQA + hh
        slope = jnp.where(hrow == hh, 2.0 ** -(h + 1), slope)
        sink = jnp.where(hrow == hh, sk_ref[h:h + 1, 0:1], sink)
    return valid, slope * dist.astype(f32), sink


def _stack_heads(v, g):
    return jnp.concatenate([v[:, (g * GQA + hh) * HD:(g * GQA + hh + 1) * HD] for hh in range(GQA)], axis=0)


def attn_fwd(qkv, sink_rows, BL, S, exchange):
    NB = S // BLK
    T = BL * S

    def body(q_ref, kc_ref, kp_ref, vc_ref, vp_ref, sk_ref, o_ref):
        j = pl.program_id(1)
        q = q_ref[...]
        k2 = jnp.concatenate([kp_ref[...], kc_ref[...]], axis=0)
        v2 = jnp.concatenate([vp_ref[...], vc_ref[...]], axis=0)
        ones = jnp.ones((2 * BLK, HD), MXU)
        for g in range(N_KV):
            valid, bias, sink = _swa_setup(j, g, sk_ref)
            kn, vh = k2[:, g * HD:(g + 1) * HD], v2[:, g * HD:(g + 1) * HD]
            s = jnp.where(valid, _nt(_stack_heads(q, g), kn) * (HD ** -0.5) - bias, NEG)
            e, es = _exp_scores(s, sink)
            eb = _c(e)
            o = _nn(eb, vh) * (1.0 / (_nn(eb, ones) + es))
            for hh in range(GQA):
                o_ref[:, pl.ds((g * GQA + hh) * HD, HD)] = o[hh * BLK:(hh + 1) * BLK]

    cur = lambda col: (lambda b, j: (b * NB + j, col))
    prev = lambda col: (lambda b, j: (b * NB + jnp.maximum(j - 1, 0), col))
    return _run("attn_fwd", body, (BL, NB), [qkv, qkv, qkv, qkv, qkv, sink_rows],
                [pl.BlockSpec((BLK, ATT_W), cur(0)),
                 pl.BlockSpec((BLK, KV_W), cur(4)), pl.BlockSpec((BLK, KV_W), prev(4)),
                 pl.BlockSpec((BLK, KV_W), cur(5)), pl.BlockSpec((BLK, KV_W), prev(5)),
                 pl.BlockSpec((8, 128), lambda b, j: (0, 0))],
                [SDS((T, ATT_W), f32)], [pl.BlockSpec((BLK, ATT_W), cur(0))], exchange=exchange)


def _conv_taps(u, uh):
    row = lax.broadcasted_iota(jnp.int32, u.shape, 0)
    u1 = jnp.where(row == 0, uh[7:8, :], pltpu.roll(u, 1, 0))
    u2 = jnp.where(row == 0, uh[6:7, :], jnp.where(row == 1, uh[7:8, :], pltpu.roll(u, 2, 0)))
    return u1, u2


def _mem_head(qm, km, vm, h):
    qh, kh, vh = (a[:, h * HD:(h + 1) * HD] for a in (qm, km, vm))
    e, _ = _exp_scores(_nt(qh, kh) * (HD ** -0.5))
    return qh, kh, vh, e


def mixer_tail_fwd(x2d, attn_out, proj, qkv, km, vm, conv_w8, conv_b, g_a, g_c, g_m, wout, g_f, S, tm, exchange):
    T, D = x2d.shape
    NM = km.shape[0] // (T // S)

    def body(x_ref, ao_ref, ch_ref, cb_ref, cc_ref, chh_ref, cch_ref, qm_ref, km_ref, vm_ref, cw_ref, cbias_ref,
             ga_ref, gc_ref, gm_ref, wout_ref, gf_ref, co_ref, mo_ref, mg_ref, x1_ref, h_ref):
        first = (pl.program_id(0) * tm) % S == 0
        u = cc_ref[...] * ch_ref[...]
        uh = jnp.where(first, 0.0, cch_ref[...] * chh_ref[...])
        u1, u2 = _conv_taps(u, uh)
        conv = cw_ref[0:1, :] * u2 + cw_ref[1:2, :] * u1 + cw_ref[2:3, :] * u + cbias_ref[...]
        conv_out = cb_ref[...] * conv
        co_ref[...] = conv_out
        qm, kmv, vmv = qm_ref[...], km_ref[...], vm_ref[...]
        ones = jnp.ones((NM, HD), MXU)
        for h in range(N_MEMH):
            _, _, vh, e = _mem_head(qm, kmv, vmv, h)
            eb = _c(e)
            mo_ref[:, pl.ds(h * HD, HD)] = _nn(eb, vh) * (1.0 / _nn(eb, ones))
        mem_out = mo_ref[...]
        ao = ao_ref[...]
        merged = _c(jnp.concatenate([ao * _rstd(ao) * ga_ref[...], conv_out * _rstd(conv_out) * gc_ref[...],
                                     mem_out * _rstd(mem_out) * gm_ref[...]], axis=1))
        mg_ref[...] = merged
        x1 = x_ref[...] + _nn(merged, wout_ref[...])
        x1_ref[...] = x1
        h_ref[...] = _c(x1 * _rstd(x1) * gf_ref[...])

    tile = lambda w, col: pl.BlockSpec((tm, w), lambda i: (i, col))
    halo = lambda col: pl.BlockSpec((8, CONV_W), lambda i: (jnp.maximum(i * (tm // 8) - 1, 0), col))
    seq = pl.BlockSpec((NM, MEM_W), lambda i: ((i * tm) // S, 0))
    small = lambda a: pl.BlockSpec(a.shape, lambda i: (0, 0))
    return _run("mixer_tail_fwd", body, (T // tm,),
                [x2d, attn_out, proj, proj, proj, proj, proj, qkv, km, vm, conv_w8, conv_b, g_a, g_c, g_m, wout, g_f],
                [tile(D, 0), tile(ATT_W, 0), tile(CONV_W, 3), tile(CONV_W, 4), tile(CONV_W, 5), halo(3), halo(5),
                 tile(MEM_W, 3), seq, seq, small(conv_w8), small(conv_b), small(g_a), small(g_c), small(g_m),
                 VM, small(g_f)],
                [SDS((T, CONV_W), f32), SDS((T, MEM_W), f32), SDS((T, D), MXU), SDS((T, D), f32), SDS((T, D), MXU)],
                [tile(CONV_W, 0), tile(MEM_W, 0), tile(D, 0), tile(D, 0), tile(D, 0)], vmem_mib=40, exchange=exchange)


def ffn_fwd_bwd(h, x1, tgt, wgT, wuT, wd, g_f, tm):
    T, D = x1.shape
    F = wd.shape[0]

    def body(h_ref, x1_ref, t_ref, wg_ref, wu_ref, wd_ref, gf_ref,
             dx1_ref, dx2_ref, act_ref, dg_ref, du_ref, loss_ref, dgf_ref):
        @pl.when(pl.program_id(0) == 0)
        def _():
            loss_ref[...] = jnp.zeros_like(loss_ref)
            dgf_ref[...] = jnp.zeros_like(dgf_ref)

        hv = h_ref[...]
        gate = _nt(hv, wg_ref[...])
        up = _nt(hv, wu_ref[...])
        sg = jax.nn.sigmoid(gate)
        sl = gate * sg
        act = _c(sl * up)
        act_ref[...] = act
        x1v = x1_ref[...]
        diff = (x1v + _nn(act, wd_ref[...])) - t_ref[...]
        loss_ref[...] += 0.5 * jnp.sum(jnp.sum(diff * diff, axis=-1, keepdims=True) / D, axis=0, keepdims=True)
        dx2 = diff / D
        dx2b = _c(dx2)
        dx2_ref[...] = dx2b
        d_act = _nt(dx2b, wd_ref[...])
        d_up = _c(d_act * sl)
        d_gate = _c(d_act * up * (sg * (1.0 + gate * (1.0 - sg))))
        du_ref[...] = d_up
        dg_ref[...] = d_gate
        dh = _nn(d_gate, wg_ref[...]) + _nn(d_up, wu_ref[...])
        dv, dgf = _norm_bwd(dh, x1v, _rstd(x1v), gf_ref[...])
        dx1_ref[...] = dx2 + dv
        dgf_ref[...] += dgf

    tile = lambda w: pl.BlockSpec((tm, w), lambda i: (i, 0))
    return _run("ffn_fwd_bwd", body, (T // tm,), [h, x1, tgt, wgT, wuT, wd, g_f],
                [tile(D), tile(D), tile(D), VM, VM, VM, pl.BlockSpec((1, D), lambda i: (0, 0))],
                [SDS((T, D), f32), SDS((T, D), MXU), SDS((T, F), MXU), SDS((T, F), MXU), SDS((T, F), MXU),
                 SDS((8, 128), f32), SDS((1, D), f32)],
                [tile(D), tile(D), tile(F), tile(F), tile(F), pl.BlockSpec((8, 128), lambda i: (0, 0)),
                 pl.BlockSpec((1, D), lambda i: (0, 0))], vmem_mib=56)


def matmul_tn(a, b, name, tmo, tk):
    T, M = a.shape
    N = b.shape[1]

    def body(a_ref, b_ref, o_ref):
        @pl.when(pl.program_id(1) == 0)
        def _():
            o_ref[...] = jnp.zeros_like(o_ref)

        o_ref[...] += _tn(a_ref[...], b_ref[...])

    return _run(name, body, (M // tmo, T // tk), [a, b],
                [pl.BlockSpec((tk, tmo), lambda m, k: (k, m)), pl.BlockSpec((tk, N), lambda m, k: (k, 0))],
                [SDS((M, N), f32)], [pl.BlockSpec((tmo, N), lambda m, k: (m, 0))], vmem_mib=48)[0]


def out_proj_bwd(dx1, merged, attn_out, conv_out, mem_out, g_a, g_c, g_m, wout, tm):
    T, D = dx1.shape

    def body(dx1_ref, mg_ref, ao_ref, co_ref, mo_ref, ga_ref, gc_ref, gm_ref, w_ref,
             dao_ref, dco_ref, dmo_ref, dw_ref, dgain_ref):
        @pl.when(pl.program_id(0) == 0)
        def _():
            dw_ref[...] = jnp.zeros_like(dw_ref)
            dgain_ref[...] = jnp.zeros_like(dgain_ref)

        dxb = _c(dx1_ref[...])
        dw_ref[...] += _tn(mg_ref[...], dxb)
        dmg = _nt(dxb, w_ref[...])
        ao, co, mo = ao_ref[...], co_ref[...], mo_ref[...]
        da, ga = _norm_bwd(dmg[:, :ATT_W], ao, _rstd(ao), ga_ref[...])
        dc, gc = _norm_bwd(dmg[:, ATT_W:ATT_W + CONV_W], co, _rstd(co), gc_ref[...])
        dm, gm = _norm_bwd(dmg[:, ATT_W + CONV_W:], mo, _rstd(mo), gm_ref[...])
        dao_ref[...] = da
        dco_ref[...] = dc
        dmo_ref[...] = dm
        dgain_ref[...] += jnp.concatenate([ga, gc, gm], axis=1)

    tile = lambda w: pl.BlockSpec((tm, w), lambda i: (i, 0))
    small = lambda a: pl.BlockSpec(a.shape, lambda i: (0, 0))
    return _run("out_proj_bwd", body, (T // tm,), [dx1, merged, attn_out, conv_out, mem_out, g_a, g_c, g_m, wout],
                [tile(D), tile(D), tile(ATT_W), tile(CONV_W), tile(MEM_W), small(g_a), small(g_c), small(g_m), VM],
                [SDS((T, ATT_W), f32), SDS((T, CONV_W), f32), SDS((T, MEM_W), f32), SDS((D, D), f32), SDS((1, D), f32)],
                [tile(ATT_W), tile(CONV_W), tile(MEM_W), pl.BlockSpec((D, D), lambda i: (0, 0)),
                 pl.BlockSpec((1, D), lambda i: (0, 0))], vmem_mib=40)


def attn_bwd(qkv, d_attn, attn_out, sink_rows, BL, S, exchange):
    NB = S // BLK
    T = BL * S

    def body(q_ref, kc_ref, kp_ref, vc_ref, vp_ref, do_ref, ao_ref, sk_ref,
             dq_ref, dk_ref, dv_ref, dsk_ref, pend_k, pend_v):
        b, j = pl.program_id(0), pl.program_id(1)

        @pl.when((b == 0) & (j == 0))
        def _():
            dsk_ref[...] = jnp.zeros_like(dsk_ref)

        @pl.when(j == 0)
        def _():
            pend_k[...] = jnp.zeros_like(pend_k)
            pend_v[...] = jnp.zeros_like(pend_v)

        @pl.when(j < NB)
        def _():
            q, do, ao = q_ref[...], do_ref[...], ao_ref[...]
            k2 = jnp.concatenate([kp_ref[...], kc_ref[...]], axis=0)
            v2 = jnp.concatenate([vp_ref[...], vc_ref[...]], axis=0)
            lane = lax.broadcasted_iota(jnp.int32, (8, 128), 1)
            ones_n = jnp.ones((2 * BLK, HD), MXU)
            ones_w = jnp.ones((2 * BLK, 2 * BLK), MXU)
            dsk = jnp.zeros((8, 128), f32)
            dks, dvs = [], []
            for g in range(N_KV):
                valid, bias, sink = _swa_setup(j, g, sk_ref)
                kn, vh = k2[:, g * HD:(g + 1) * HD], v2[:, g * HD:(g + 1) * HD]
                qs = _stack_heads(q, g)
                s = jnp.where(valid, _nt(qs, kn) * (HD ** -0.5) - bias, NEG)
                e, es = _exp_scores(s, sink)
                eb = _c(e)
                inv_n = 1.0 / (_nn(eb, ones_n) + es)
                inv_w = 1.0 / (_nn(eb, ones_w) + es)
                dos = _stack_heads(do, g)
                delta = _rowsum_mxu(dos * _stack_heads(ao, g), 2 * BLK)
                dp = _nt(_c(dos), vh)
                ds = _c(e * inv_w * (dp - delta) * (HD ** -0.5))
                t = es * inv_n[:, 0:1] * delta[:, 0:1]
                for hh in range(GQA):
                    dsk = dsk + jnp.where(lane == g * GQA + hh, -jnp.sum(t[hh * BLK:(hh + 1) * BLK]), 0.0)
                dvs.append(_tn(eb, _c(dos * inv_n)))
                dks.append(_tn(ds, qs))
                dqs = _nn(ds, kn)
                for hh in range(GQA):
                    dq_ref[:, pl.ds((g * GQA + hh) * HD, HD)] = dqs[hh * BLK:(hh + 1) * BLK]
            dk2 = jnp.concatenate(dks, axis=1)
            dv2 = jnp.concatenate(dvs, axis=1)
            dk_ref[...] = pend_k[...] + dk2[:BLK]
            dv_ref[...] = pend_v[...] + dv2[:BLK]
            pend_k[...] = dk2[BLK:]
            pend_v[...] = dv2[BLK:]
            dsk_ref[...] += dsk

        @pl.when(j == NB)
        def _():
            dk_ref[...] = pend_k[...]
            dv_ref[...] = pend_v[...]

    cur = lambda col: (lambda b, j: (b * NB + jnp.minimum(j, NB - 1), col))
    prev = lambda col: (lambda b, j: (b * NB + jnp.maximum(j - 1, 0), col))
    small = lambda shape: pl.BlockSpec(shape, lambda b, j: (0, 0))
    return _run("attn_bwd", body, (BL, NB + 1), [qkv, qkv, qkv, qkv, qkv, d_attn, attn_out, sink_rows],
                [pl.BlockSpec((BLK, ATT_W), cur(0)),
                 pl.BlockSpec((BLK, KV_W), cur(4)), pl.BlockSpec((BLK, KV_W), prev(4)),
                 pl.BlockSpec((BLK, KV_W), cur(5)), pl.BlockSpec((BLK, KV_W), prev(5)),
                 pl.BlockSpec((BLK, ATT_W), cur(0)), pl.BlockSpec((BLK, ATT_W), cur(0)), small((8, 128))],
                [SDS((T, ATT_W), f32), SDS((T, KV_W), f32), SDS((T, KV_W), f32), SDS((8, 128), f32)],
                [pl.BlockSpec((BLK, ATT_W), cur(0)), pl.BlockSpec((BLK, KV_W), prev(0)),
                 pl.BlockSpec((BLK, KV_W), prev(0)), small((8, 128))],
                scratch=[pltpu.VMEM((BLK, KV_W), f32)] * 2, exchange=exchange)


def mem_conv_bwd(d_mem_out, mem_out, d_conv_out, proj, qkv, km, vm, conv_w8, conv_b, S, tm, exchange):
    T = d_mem_out.shape[0]
    NM = km.shape[0] // (T // S)

    def body(dmo_ref, mo_ref, dco_ref, ch_ref, cb_ref, cc_ref, chh_ref, cch_ref, qm_ref, km_ref, vm_ref, cw_ref,
             cbias_ref, dqm_ref, dkm_ref, dvm_ref, dcb_ref, dcv_ref, dcw_ref, dcbias_ref):
        i = pl.program_id(0)
        first = (i * tm) % S == 0

        @pl.when(i == 0)
        def _():
            dcw_ref[...] = jnp.zeros_like(dcw_ref)
            dcbias_ref[...] = jnp.zeros_like(dcbias_ref)

        @pl.when(first)
        def _():
            dkm_ref[...] = jnp.zeros_like(dkm_ref)
            dvm_ref[...] = jnp.zeros_like(dvm_ref)

        qm, kmv, vmv, dmo, mo = qm_ref[...], km_ref[...], vm_ref[...], dmo_ref[...], mo_ref[...]
        ones_n = jnp.ones((NM, HD), MXU)
        ones_w = jnp.ones((NM, NM), MXU)
        for h in range(N_MEMH):
            qh, kh, vh, e = _mem_head(qm, kmv, vmv, h)
            eb = _c(e)
            doh = dmo[:, h * HD:(h + 1) * HD]
            delta = _rowsum_mxu(doh * mo[:, h * HD:(h + 1) * HD], NM)
            dp = _nt(_c(doh), vh)
            ds = _c(e * (1.0 / _nn(eb, ones_w)) * (dp - delta) * (HD ** -0.5))
            dvm_ref[:, pl.ds(h * HD, HD)] += _tn(eb, _c(doh * (1.0 / _nn(eb, ones_n))))
            dkm_ref[:, pl.ds(h * HD, HD)] += _tn(ds, qh)
            dqm_ref[:, pl.ds(h * HD, HD)] = _nn(ds, kh)

        u = cc_ref[...] * ch_ref[...]
        uh = jnp.where(first, 0.0, cch_ref[...] * chh_ref[...])
        u1, u2 = _conv_taps(u, uh)
        conv = cw_ref[0:1, :] * u2 + cw_ref[1:2, :] * u1 + cw_ref[2:3, :] * u + cbias_ref[...]
        dy = dco_ref[...]
        dcb_ref[...] = dy * conv
        dcv = dy * cb_ref[...]
        dcv_ref[...] = dcv
        dcbias_ref[...] += jnp.sum(dcv, axis=0, keepdims=True)
        dcw_ref[0:1, :] += jnp.sum(dcv * u2, axis=0, keepdims=True)
        dcw_ref[1:2, :] += jnp.sum(dcv * u1, axis=0, keepdims=True)
        dcw_ref[2:3, :] += jnp.sum(dcv * u, axis=0, keepdims=True)

    tile = lambda w, col: pl.BlockSpec((tm, w), lambda i: (i, col))
    halo = lambda col: pl.BlockSpec((8, CONV_W), lambda i: (jnp.maximum(i * (tm // 8) - 1, 0), col))
    seq = pl.BlockSpec((NM, MEM_W), lambda i: ((i * tm) // S, 0))
    small = lambda a: pl.BlockSpec(a.shape, lambda i: (0, 0))
    return _run("mem_conv_bwd", body, (T // tm,),
                [d_mem_out, mem_out, d_conv_out, proj, proj, proj, proj, proj, qkv, km, vm, conv_w8, conv_b],
                [tile(MEM_W, 0), tile(MEM_W, 0), tile(CONV_W, 0), tile(CONV_W, 3), tile(CONV_W, 4), tile(CONV_W, 5),
                 halo(3), halo(5), tile(MEM_W, 3), seq, seq, small(conv_w8), small(conv_b)],
                [SDS((T, MEM_W), f32), SDS(km.shape, f32), SDS(km.shape, f32),
                 SDS((T, CONV_W), f32), SDS((T, CONV_W), f32), SDS(conv_w8.shape, f32), SDS(conv_b.shape, f32)],
                [tile(MEM_W, 0), seq, seq, tile(CONV_W, 0), tile(CONV_W, 0), small(conv_w8), small(conv_b)],
                exchange=exchange)


def in_proj_bwd(dqn, dkn, dv, dcb, dcv, dqmn, proj, conv_w8, xn, x2d, dx1, g1, qg, kg, mqg, winT, S, tm):
    T, D = x2d.shape
    P = winT.shape[0]
    last_blk = T // 8 - 1

    def body(dq_ref, dk_ref, dv_ref, dcb_ref, dcv_ref, dcvn_ref, dqm_ref, qa_ref, ka_ref, ch_ref, cc_ref, qma_ref,
             cw_ref, xn_ref, x_ref, dx1_ref, g_ref, qg_ref, kg_ref, mqg_ref, w_ref,
             dx_ref, dw_ref, dg_ref, dqg_ref, dkg_ref, dmqg_ref):
        i = pl.program_id(0)

        @pl.when(i == 0)
        def _():
            dw_ref[...] = jnp.zeros_like(dw_ref)
            dg_ref[...] = jnp.zeros_like(dg_ref)
            dqg_ref[...] = jnp.zeros_like(dqg_ref)
            dkg_ref[...] = jnp.zeros_like(dkg_ref)
            dmqg_ref[...] = jnp.zeros_like(dmqg_ref)

        dqa, gq = _heads_norm_bwd(dq_ref[...], qa_ref[...], qg_ref[...])
        dka, gk = _heads_norm_bwd(dk_ref[...], ka_ref[...], kg_ref[...])
        dqma, gmq = _heads_norm_bwd(dqm_ref[...], qma_ref[...], mqg_ref[...])
        dqg_ref[...] += gq
        dkg_ref[...] += gk
        dmqg_ref[...] += gmq

        last = ((i + 1) * tm) % S == 0
        dcv = dcv_ref[...]
        nxt = jnp.where(last, 0.0, dcvn_ref[...])
        row = lax.broadcasted_iota(jnp.int32, dcv.shape, 0)
        n1 = jnp.where(row == tm - 1, nxt[0:1, :], pltpu.roll(dcv, tm - 1, 0))
        n2 = jnp.where(row == tm - 2, nxt[0:1, :], jnp.where(row == tm - 1, nxt[1:2, :], pltpu.roll(dcv, tm - 2, 0)))
        du = cw_ref[2:3, :] * dcv + cw_ref[1:2, :] * n1 + cw_ref[0:1, :] * n2
        d_proj = jnp.concatenate([_c(dqa), _c(dka), _c(dv_ref[...]), _c(du * cc_ref[...]),
                                  _c(dcb_ref[...]), _c(du * ch_ref[...]), _c(dqma)], axis=1)
        dw_ref[...] += _tn(d_proj, xn_ref[...])
        xv = x_ref[...]
        dv_, dg = _norm_bwd(_nn(d_proj, w_ref[...]), xv, _rstd(xv), g_ref[...])
        dx_ref[...] = dx1_ref[...] + dv_
        dg_ref[...] += dg

    tile = lambda w, col=0: pl.BlockSpec((tm, w), lambda i: (i, col))
    nhalo = pl.BlockSpec((8, CONV_W), lambda i: (jnp.minimum((i + 1) * (tm // 8), last_blk), 0))
    small = lambda a: pl.BlockSpec(a.shape, lambda i: (0, 0))
    return _run("in_proj_bwd", body, (T // tm,),
                [dqn, dkn, dv, dcb, dcv, dcv, dqmn, proj, proj, proj, proj, proj, conv_w8, xn, x2d, dx1, g1, qg, kg, mqg, winT],
                [tile(ATT_W), tile(KV_W), tile(KV_W), tile(CONV_W), tile(CONV_W), nhalo, tile(MEM_W),
                 tile(ATT_W, 0), tile(KV_W, 4), tile(CONV_W, 3), tile(CONV_W, 5), tile(MEM_W, 6), small(conv_w8),
                 tile(D), tile(D), tile(D), small(g1), small(qg), small(kg), small(mqg), VM],
                [SDS((T, D), f32), SDS((P, D), f32), SDS(g1.shape, f32), SDS(qg.shape, f32), SDS(kg.shape, f32),
                 SDS(mqg.shape, f32)],
                [tile(D), pl.BlockSpec((P, D), lambda i: (0, 0)), small(g1), small(qg), small(kg), small(mqg)],
                vmem_mib=48)


def mem_kv_bwd(dkm, dvm, kv, memn, mem2d, g_mem, mkg, wmkv):
    def body(dkm_ref, dvm_ref, kv_ref, mn_ref, m_ref, g_ref, kg_ref, w_ref, dw_ref, dg_ref, dkg_ref):
        kv_ = kv_ref[...]
        dkn = dkm_ref[...]
        dkg = jnp.zeros((1, HD), f32)
        parts = []
        for h in range(N_MEMH):
            kh = kv_[:, h * HD:(h + 1) * HD]
            dkh, gk = _norm_bwd(dkn[:, h * HD:(h + 1) * HD], kh, _rstd(kh), kg_ref[...])
            parts.append(dkh)
            dkg = dkg + gk
        dkg_ref[...] = dkg
        dkv = _c(jnp.concatenate(parts + [dvm_ref[...]], axis=1))
        dw_ref[...] = _tn(mn_ref[...], dkv)
        mv = m_ref[...]
        dg_ref[...] = jnp.sum(_nt(dkv, w_ref[...]) * mv * _rstd(mv), axis=0, keepdims=True)

    return _run("mem_kv_bwd", body, (), [dkm, dvm, kv, memn, mem2d, g_mem, mkg, wmkv], [VM] * 8,
                [SDS(wmkv.shape, f32), SDS(g_mem.shape, f32), SDS(mkg.shape, f32)], [VM] * 3, vmem_mib=40)


def _halves_view(g):
    return g.reshape(4, 2, g.shape[0] // 8, g.shape[1])


def kernel(x, mem, norm_mix, w_in, q_norm, k_norm, attn_sinks, conv_w, conv_b, norm_mem, w_mem_kv, mem_q_norm, mem_k_norm, out_norm_attn, out_norm_conv, out_norm_mem, w_out, norm_ffn, w_gate, w_up, w_down, loss_target, m_norm_mix, m_w_in, m_q_norm, m_k_norm, m_attn_sinks, m_conv_w, m_conv_b, m_norm_mem, m_w_mem_kv, m_mem_q_norm, m_mem_k_norm, m_out_norm_attn, m_out_norm_conv, m_out_norm_mem, m_w_out, m_norm_ffn, m_w_gate, m_w_up, m_w_down, v_norm_mix, v_w_in, v_q_norm, v_k_norm, v_attn_sinks, v_conv_w, v_conv_b, v_norm_mem, v_w_mem_kv, v_mem_q_norm, v_mem_k_norm, v_out_norm_attn, v_out_norm_conv, v_out_norm_mem, v_w_out, v_norm_ffn, v_w_gate, v_w_up, v_w_down):
    BL, S, D = x.shape
    T = BL * S
    TM = 256
    _, _, ci = _place()
    cidx = ci.reshape(1).astype(jnp.int32)

    rowblocks = lambda a, b, c, d, e, f: [a[0].T, b[0].T, c[0].T, d[0], e[0], f[0]]
    w_rb = rowblocks(w_in, w_gate, w_up, w_down, w_out, w_mem_kv)
    m_rb = rowblocks(m_w_in, m_w_gate, m_w_up, m_w_down, m_w_out, m_w_mem_kv)
    v_rb = rowblocks(v_w_in, v_w_gate, v_w_up, v_w_down, v_w_out, v_w_mem_kv)
    winT_s, wgT_s, wuT_s, wd_s, wout_s, wmkv_s = prep_weights(w_rb)
    cw_pad = jnp.zeros((8, 128), f32).at[:3, :HD].set(conv_w[0])
    _, (winT, cw_all) = _run("gather_w_in", None, (), [], [], [], [], exchange=gather_exchange([winT_s, cw_pad], [True, False]))
    conv_w_full = jnp.transpose(cw_all.reshape(4, 8, 128)[:, :3, :HD], (1, 0, 2)).reshape(3, CONV_W)
    conv_w8 = jnp.zeros((8, CONV_W), f32).at[:3].set(conv_w_full)
    sink_rows = jnp.broadcast_to(attn_sinks.reshape(N_Q, 1), (N_Q, 128))

    x2d = x.reshape(T, D)
    mem2d = mem.reshape(-1, D)
    (xn, proj, qkv), (wgT,) = in_proj_fwd(x2d, norm_mix, winT, q_norm, k_norm, mem_q_norm, TM,
                                          gather_exchange([wgT_s], [True]))
    (attn_out,), (wuT, wout, wmkv) = attn_fwd(qkv, sink_rows, BL, S,
                                              gather_exchange([wuT_s, wout_s, wmkv_s], [True, True, True]))
    memn, kv, km, vm = mem_kv_fwd(mem2d, norm_mem, mem_k_norm, wmkv)
    (conv_out, mem_out, merged, x1, h), (wd,) = mixer_tail_fwd(
        x2d, attn_out, proj, qkv, km, vm, conv_w8, conv_b, out_norm_attn, out_norm_conv, out_norm_mem, wout,
        norm_ffn, S, TM, gather_exchange([wd_s], [True]))

    dx1, dx2b, act, d_gate, d_up, loss8, d_norm_ffn = ffn_fwd_bwd(h, x1, loss_target.reshape(T, D), wgT, wuT, wd, norm_ffn, TM)
    F = wd.shape[0]
    g_wd = matmul_tn(act, dx2b, "dw_down", F // 2, 512)
    g_wgT = matmul_tn(d_gate, h, "dw_gate", F // 2, 512)
    g_wuT = matmul_tn(d_up, h, "dw_up", F // 2, 512)

    d_attn, d_conv_out, d_mem_out, g_wout, d_gains = out_proj_bwd(
        dx1, merged, attn_out, conv_out, mem_out, out_norm_attn, out_norm_conv, out_norm_mem, wout, TM)
    late = [_halves_view(g) for g in (g_wgT, g_wuT, g_wd, g_wout)]
    (dqmn, dkm, dvm, dcb, dcv, d_cw8, d_cbias), late_sib = mem_conv_bwd(
        d_mem_out, mem_out, d_conv_out, proj, qkv, km, vm, conv_w8, conv_b, S, TM, halves_exchange(late))
    late_part = add_halves(cidx, late, late_sib, "grad_add_halves_ffn")
    (dqn, dkn, dv, d_sink8), late_stage = attn_bwd(qkv, d_attn, attn_out, sink_rows, BL, S, scatter_exchange(late_part))
    g_x, g_winT, d_norm_mix, d_qg, d_kg, d_mqg = in_proj_bwd(
        dqn, dkn, dv, dcb, dcv, dqmn, proj, conv_w8, xn, x2d, dx1, norm_mix, q_norm, k_norm, mem_q_norm, winT, S, TM)
    g_wmkv, d_norm_mem, d_mkg = mem_kv_bwd(dkm, dvm, kv, memn, mem2d, norm_mem, mem_k_norm, wmkv)

    tail = [_halves_view(g) for g in (g_winT, g_wmkv)]
    _, tail_sib = _run("grad_halves_tail", None, (), [], [], [], [], exchange=halves_exchange(tail))
    tail_part = add_halves(cidx, tail, tail_sib, "grad_add_halves_tail")
    late_res, tail_stage = adamw_big("adamw_late", late_stage, w_rb[1:5], m_rb[1:5], v_rb[1:5], 8,
                                     exchange=scatter_exchange(tail_part))
    tail_res, _ = adamw_big("adamw_tail", tail_stage, [w_rb[0], w_rb[5]], [m_rb[0], m_rb[5]], [v_rb[0], v_rb[5]], 4)
    res = {"w_in": [a.T[None] for a in tail_res[0]], "w_gate": [a.T[None] for a in late_res[0]],
           "w_up": [a.T[None] for a in late_res[1]], "w_down": [a[None] for a in late_res[2]],
           "w_out": [a[None] for a in late_res[3]], "w_mem_kv": [a[None] for a in tail_res[1]]}

    tot = allreduce_small(d_norm_mix, d_norm_mem, d_norm_ffn, d_gains, d_cw8, d_cbias, d_qg, d_kg, d_mqg, d_mkg, d_sink8, loss8)
    loss = tot[5, 384]
    w_small = dict(norm_mix=norm_mix, norm_mem=norm_mem, norm_ffn=norm_ffn, out_norm_attn=out_norm_attn,
                   out_norm_conv=out_norm_conv, out_norm_mem=out_norm_mem, conv_w=conv_w, conv_b=conv_b, q_norm=q_norm,
                   k_norm=k_norm, mem_q_norm=mem_q_norm, mem_k_norm=mem_k_norm, attn_sinks=attn_sinks)
    m_small = dict(norm_mix=m_norm_mix, norm_mem=m_norm_mem, norm_ffn=m_norm_ffn, out_norm_attn=m_out_norm_attn,
                   out_norm_conv=m_out_norm_conv, out_norm_mem=m_out_norm_mem, conv_w=m_conv_w, conv_b=m_conv_b,
                   q_norm=m_q_norm, k_norm=m_k_norm, mem_q_norm=m_mem_q_norm, mem_k_norm=m_mem_k_norm,
                   attn_sinks=m_attn_sinks)
    v_small = dict(norm_mix=v_norm_mix, norm_mem=v_norm_mem, norm_ffn=v_norm_ffn, out_norm_attn=v_out_norm_attn,
                   out_norm_conv=v_out_norm_conv, out_norm_mem=v_out_norm_mem, conv_w=v_conv_w, conv_b=v_conv_b,
                   q_norm=v_q_norm, k_norm=v_k_norm, mem_q_norm=v_mem_q_norm, mem_k_norm=v_mem_k_norm,
                   attn_sinks=v_attn_sinks)
    res.update(adamw_small(tot, w_small, m_small, v_small))

    order = ["norm_mix", "w_in", "q_norm", "k_norm", "attn_sinks", "conv_w", "conv_b", "norm_mem", "w_mem_kv",
             "mem_q_norm", "mem_k_norm", "out_norm_attn", "out_norm_conv", "out_norm_mem", "w_out", "norm_ffn",
             "w_gate", "w_up", "w_down"]
    return (loss, g_x.reshape(BL, S, D), *[res[n][0] for n in order], *[res[n][1] for n in order],
            *[res[n][2] for n in order], *[res[n][3] for n in order])
```

```python
import collections
import functools

import jax
import jax.numpy as jnp
from jax import lax
from jax.experimental import pallas as pl
from jax.experimental.pallas import tpu as pltpu

f32 = jnp.float32
MXU = jnp.bfloat16
WIRE = jnp.bfloat16
EPS = 1e-6
NEG = -1e30
HD = 64
BLK = 128
N_Q, N_KV, N_MEMH = 8, 2, 4
GQA = N_Q // N_KV
ATT_W, KV_W, CONV_W, MEM_W = 512, 128, 256, 256
VMEM_MIB = 1024 * 1024
ADAM_LR, ADAM_B1, ADAM_B2, ADAM_EPS, ADAM_WD, ADAM_STEP = 0.001, 0.9, 0.999, 1e-08, 0.01, 10

MESH = pl.DeviceIdType.MESH
VM = pl.BlockSpec(memory_space=pltpu.VMEM)
ANY = pl.BlockSpec(memory_space=pl.ANY)
SDS = jax.ShapeDtypeStruct
DMA = pltpu.SemaphoreType.DMA


def _c(v):
    return v.astype(MXU)


def _nn(a, b):
    return lax.dot_general(a, b, (((1,), (0,)), ((), ())), preferred_element_type=f32)


def _nt(a, b):
    return lax.dot_general(a, b, (((1,), (1,)), ((), ())), preferred_element_type=f32)


def _tn(a, b):
    return lax.dot_general(a, b, (((0,), (0,)), ((), ())), preferred_element_type=f32)


def _rstd(v):
    return lax.rsqrt(jnp.mean(v * v, axis=-1, keepdims=True) + EPS)


def _norm_bwd(dy, v, r, g):
    dyg = dy * g
    dv = r * dyg - v * (r * r * r) * jnp.mean(dyg * v, axis=-1, keepdims=True)
    return dv, jnp.sum(dy * v * r, axis=0, keepdims=True)


def _split3(v):
    hi = _c(v)
    r1 = v - hi.astype(f32)
    mid = _c(r1)
    return hi, mid, _c(r1 - mid.astype(f32))


def _rowsum_mxu(v, width):
    ones = jnp.ones((v.shape[1], width), MXU)
    return sum(_nn(a, ones) for a in _split3(v))


def _seg_sums(v):
    r = lax.broadcasted_iota(jnp.int32, (2 * HD, 2 * HD), 0) // HD
    c = lax.broadcasted_iota(jnp.int32, (2 * HD, 2 * HD), 1) // HD
    bd = (r == c).astype(MXU)
    outs = []
    for b in range(v.shape[1] // (2 * HD)):
        outs.append(sum(_nn(a, bd) for a in _split3(v[:, b * 2 * HD:(b + 1) * 2 * HD])))
    return outs[0] if len(outs) == 1 else jnp.concatenate(outs, axis=1)


def _lanes(g, width):
    return jnp.concatenate([g] * (width // HD), axis=1)


def _heads_rstd(v):
    return lax.rsqrt(_seg_sums(v * v) * (1.0 / HD) + EPS)


def _heads_norm_bwd(dy, v, g):
    r = _heads_rstd(v)
    gl = _lanes(g, v.shape[1])
    dyg = dy * gl
    dv = r * dyg - v * (r * r * r) * (_seg_sums(dyg * v) * (1.0 / HD))
    dgl = jnp.sum(dy * v * r, axis=0, keepdims=True)
    return dv, sum(dgl[:, s * HD:(s + 1) * HD] for s in range(v.shape[1] // HD))


def _exp_scores(s, extra=None):
    m = jnp.max(s, axis=-1, keepdims=True)
    if extra is None:
        return jnp.exp(s - m), None
    m = jnp.maximum(m, extra)
    return jnp.exp(s - m), jnp.exp(extra - m)


def _place():
    return lax.axis_index("x"), lax.axis_index("y"), lax.axis_index("c")


SMALL_AT = {"norm_mix": (0, 0, 1024), "norm_mem": (1, 0, 1024), "norm_ffn": (2, 0, 1024),
            "out_norm_attn": (3, 0, ATT_W), "out_norm_conv": (3, ATT_W, CONV_W), "out_norm_mem": (3, ATT_W + CONV_W, MEM_W),
            "conv_b": (4, 3 * CONV_W, CONV_W), "q_norm": (5, 0, HD), "k_norm": (5, HD, HD), "mem_q_norm": (5, 2 * HD, HD),
            "mem_k_norm": (5, 3 * HD, HD), "attn_sinks": (5, 256, N_Q)}
SMALL = ("norm_mix", "norm_mem", "norm_ffn", "out_norm_attn", "out_norm_conv", "out_norm_mem", "conv_w", "conv_b",
         "q_norm", "k_norm", "mem_q_norm", "mem_k_norm", "attn_sinks")


def _small(pk_ref, name):
    r, c0, w = SMALL_AT[name]
    return pk_ref[r:r + 1, c0:c0 + w]


def _pack_small(d):
    z = lambda n: jnp.zeros((1, n), f32)
    row3 = jnp.concatenate([d["out_norm_attn"], d["out_norm_conv"], d["out_norm_mem"]], axis=1)
    row4 = jnp.concatenate([d["conv_w"].reshape(1, 3 * HD), z(3 * CONV_W - 3 * HD), d["conv_b"]], axis=1)
    row5 = jnp.concatenate([d["q_norm"], d["k_norm"], d["mem_q_norm"], d["mem_k_norm"], d["attn_sinks"],
                            z(1024 - 4 * HD - N_Q)], axis=1)
    return jnp.concatenate([d["norm_mix"], d["norm_mem"], d["norm_ffn"], row3, row4, row5, z(1024), z(1024)], axis=0)


def _other_chips(x, y):
    return [(1 - x, y), (x, 1 - y), (1 - x, 1 - y)]


Exchange = collections.namedtuple("Exchange", "ins outs sems start finish")


def _run(name, body, grid, ins, in_specs, out_shape, out_specs, scratch=(), vmem_mib=32, exchange=None):
    ins, in_specs, out_shape, out_specs, scratch = list(ins), list(in_specs), list(out_shape), list(out_specs), list(scratch)
    ni, no, ns = len(ins), len(out_shape), len(scratch)
    ex = exchange
    if ex is not None:
        nxi, nxo = len(ex.ins), len(ex.outs)

    def call_body(*refs):
        if ex is None:
            body(*refs)
            return
        a, xa = refs[:ni], refs[ni:ni + nxi]
        o, xo = refs[ni + nxi:ni + nxi + no], refs[ni + nxi + no:ni + nxi + no + nxo]
        s, xs = refs[ni + nxi + no + nxo:ni + nxi + no + nxo + ns], refs[ni + nxi + no + nxo + ns:]
        if grid:
            first = functools.reduce(jnp.logical_and, [pl.program_id(d) == 0 for d in range(len(grid))])
            last = functools.reduce(jnp.logical_and, [pl.program_id(d) == grid[d] - 1 for d in range(len(grid))])
            pl.when(first)(lambda: ex.start(xa, xo, xs))
            body(*a, *o, *s)
            pl.when(last)(lambda: ex.finish(xa, xo, xs))
        else:
            ex.start(xa, xo, xs)
            if body is not None:
                body(*a, *o, *s)
            ex.finish(xa, xo, xs)

    if ex is not None:
        ins, in_specs = ins + list(ex.ins), in_specs + [ANY] * nxi
        out_shape, out_specs = out_shape + list(ex.outs), out_specs + [ANY] * nxo
        scratch = scratch + list(ex.sems)
    kw = dict(grid=grid) if grid else {}
    res = pl.pallas_call(
        call_body, name=name, out_shape=out_shape, in_specs=in_specs, out_specs=out_specs, scratch_shapes=scratch,
        compiler_params=pltpu.CompilerParams(dimension_semantics=("arbitrary",) * len(grid) if grid else None,
                                             vmem_limit_bytes=vmem_mib * VMEM_MIB), **kw)(*ins)
    res = list(res)
    return (res[:no], res[no:]) if ex is not None else res


def _remote(src, dst, ssem, rsem, dev):
    return pltpu.make_async_remote_copy(src_ref=src, dst_ref=dst, send_sem=ssem, recv_sem=rsem,
                                        device_id=dev, device_id_type=MESH)


def gather_exchange(shards, split):
    n = len(shards)

    def rows(ref, e, kk, half=None):
        R = shards[e].shape[0]
        if half is None:
            return ref.at[pl.ds(pl.multiple_of(kk * R, 8), R)]
        return ref.at[pl.ds(pl.multiple_of(kk * R + half * (R // 2), 8), R // 2)]

    def ici(src, dst, sm, e, j, chip_j, x, y, c):
        k = 2 * x + y
        if split[e]:
            s = src[e].at[pl.ds(pl.multiple_of(c * (shards[e].shape[0] // 2), 8), shards[e].shape[0] // 2)]
            return _remote(s, rows(dst[e], e, k, c), sm[0].at[6 * e + j], sm[1].at[6 * e + j], (*chip_j, c))
        return _remote(src[e], rows(dst[e], e, k), sm[0].at[6 * e + j], sm[1].at[6 * e + j], (*chip_j, c))

    def landed(dst, e, chip_j, c):
        kj = 2 * chip_j[0] + chip_j[1]
        return rows(dst[e], e, kj, c) if split[e] else rows(dst[e], e, kj)

    def forward(dst, sm, e, j, chip_j, x, y, c, sender_c):
        kj = 2 * chip_j[0] + chip_j[1]
        r = rows(dst[e], e, kj, sender_c)
        return _remote(r, r, sm[0].at[6 * e + 3 + j], sm[1].at[6 * e + 3 + j], (x, y, 1 - c))

    def local(src, dst, sm, e, x, y):
        return pltpu.make_async_copy(src[e], rows(dst[e], e, 2 * x + y), sm[2].at[e])

    def start(src, dst, sm):
        x, y, c = _place()
        for e in range(n):
            local(src, dst, sm, e, x, y).start()
            for j, chip_j in enumerate(_other_chips(x, y)):
                ici(src, dst, sm, e, j, chip_j, x, y, c).start()

    def finish(src, dst, sm):
        x, y, c = _place()
        chips = _other_chips(x, y)
        for e in range(n):
            for j, chip_j in enumerate(chips):
                r = landed(dst, e, chip_j, c)
                _remote(r, r, sm[0].at[6 * e + j], sm[1].at[6 * e + j], (*chip_j, c)).wait_recv()
                if split[e]:
                    forward(dst, sm, e, j, chip_j, x, y, c, c).start()
        for e in range(n):
            for j, chip_j in enumerate(chips):
                if split[e]:
                    forward(dst, sm, e, j, chip_j, x, y, c, 1 - c).wait_recv()
        for e in range(n):
            for j, chip_j in enumerate(chips):
                ici(src, dst, sm, e, j, chip_j, x, y, c).wait_send()
                if split[e]:
                    forward(dst, sm, e, j, chip_j, x, y, c, c).wait_send()
            local(src, dst, sm, e, x, y).wait()

    outs = [SDS((4 * s.shape[0], s.shape[1]), s.dtype) for s in shards]
    return Exchange(list(shards), outs, [DMA((6 * n,)), DMA((6 * n,)), DMA((n,))], start, finish)


def halves_exchange(grads):
    n = len(grads)

    def copy(g, st, sm, e, x, y, c):
        return _remote(g[e].at[:, 1 - c], st[e], sm[0].at[e], sm[1].at[e], (x, y, 1 - c))

    def start(g, st, sm):
        x, y, c = _place()
        for e in range(n):
            copy(g, st, sm, e, x, y, c).start()

    def finish(g, st, sm):
        x, y, c = _place()
        for e in range(n):
            copy(g, st, sm, e, x, y, c).wait()

    outs = [SDS((4,) + a.shape[2:], a.dtype) for a in grads]
    return Exchange(list(grads), outs, [DMA((n,)), DMA((n,))], start, finish)


def scatter_exchange(parts):
    n = len(parts)

    def ici(p, st, sm, e, j, chip_j, x, y, c):
        k, kj = 2 * x + y, 2 * chip_j[0] + chip_j[1]
        return _remote(p[e].at[kj], st[e].at[c, k], sm[0].at[8 * e + j], sm[1].at[8 * e + j], (*chip_j, c))

    def own(p, st, sm, e, x, y, c):
        k = 2 * x + y
        return _remote(p[e].at[k], st[e].at[c, k], sm[0].at[8 * e + 3], sm[1].at[8 * e + 3], (x, y, 1 - c))

    def forward(st, sm, e, j, chip_j, x, y, c, sender_c):
        kj = 2 * chip_j[0] + chip_j[1]
        r = st[e].at[sender_c, kj]
        return _remote(r, r, sm[0].at[8 * e + 4 + j], sm[1].at[8 * e + 4 + j], (x, y, 1 - c))

    def local(p, st, sm, e, x, y, c):
        k = 2 * x + y
        return pltpu.make_async_copy(p[e].at[k], st[e].at[c, k], sm[2].at[e])

    def start(p, st, sm):
        x, y, c = _place()
        for e in range(n):
            local(p, st, sm, e, x, y, c).start()
            own(p, st, sm, e, x, y, c).start()
            for j, chip_j in enumerate(_other_chips(x, y)):
                ici(p, st, sm, e, j, chip_j, x, y, c).start()

    def finish(p, st, sm):
        x, y, c = _place()
        k = 2 * x + y
        chips = _other_chips(x, y)
        for e in range(n):
            for j, chip_j in enumerate(chips):
                kj = 2 * chip_j[0] + chip_j[1]
                r = st[e].at[c, kj]
                _remote(r, r, sm[0].at[8 * e + j], sm[1].at[8 * e + j], (*chip_j, c)).wait_recv()
                forward(st, sm, e, j, chip_j, x, y, c, c).start()
        for e in range(n):
            r = st[e].at[1 - c, k]
            _remote(r, r, sm[0].at[8 * e + 3], sm[1].at[8 * e + 3], (x, y, 1 - c)).wait_recv()
            for j, chip_j in enumerate(chips):
                forward(st, sm, e, j, chip_j, x, y, c, 1 - c).wait_recv()
        for e in range(n):
            own(p, st, sm, e, x, y, c).wait_send()
            for j, chip_j in enumerate(chips):
                ici(p, st, sm, e, j, chip_j, x, y, c).wait_send()
                forward(st, sm, e, j, chip_j, x, y, c, c).wait_send()
            local(p, st, sm, e, x, y, c).wait()

    outs = [SDS((2,) + a.shape, a.dtype) for a in parts]
    return Exchange(list(parts), outs, [DMA((8 * n,)), DMA((8 * n,)), DMA((n,))], start, finish)


def allreduce_small(d_norm_mix, d_norm_mem, d_norm_ffn, d_gains, d_cw8, d_cbias, d_qg, d_kg, d_mqg, d_mkg, d_sink8, loss8,
                    exchange):
    def body(nm_ref, nmem_ref, nf_ref, gn_ref, cw_ref, cb_ref, qg_ref, kg_ref, mqg_ref, mkg_ref, sk_ref, ls_ref,
             o_ref, buf, ssem, rsem):
        x, y, c = _place()
        me = 4 * x + 2 * y + c
        mine = buf.at[me]
        mine[...] = jnp.zeros((8, 1024), f32)
        mine[0:1, :] = nm_ref[...]
        mine[1:2, :] = nmem_ref[...]
        mine[2:3, :] = nf_ref[...]
        mine[3:4, :] = gn_ref[...]
        for j in range(3):
            mine[4:5, pl.ds(j * CONV_W, CONV_W)] = cw_ref[j:j + 1, :]
        mine[4:5, pl.ds(3 * CONV_W, CONV_W)] = cb_ref[...]
        for j, r in enumerate((qg_ref, kg_ref, mqg_ref, mkg_ref)):
            mine[5:6, pl.ds(j * HD, HD)] = r[...]
        mine[5:6, pl.ds(256, 128)] = sk_ref[0:1, :]
        mine[5:6, pl.ds(384, 128)] = ls_ref[0:1, :]

        def peer_of(m):
            return (1 - x if m & 4 else x, 1 - y if m & 2 else y, 1 - c if m & 1 else c)

        for m in range(1, 8):
            _remote(mine, mine, ssem.at[m - 1], rsem.at[m - 1], peer_of(m)).start()
        for m in range(1, 8):
            p = peer_of(m)
            got = buf.at[4 * p[0] + 2 * p[1] + p[2]]
            _remote(got, got, ssem.at[m - 1], rsem.at[m - 1], p).wait_recv()
        for m in range(1, 8):
            _remote(mine, mine, ssem.at[m - 1], rsem.at[m - 1], peer_of(m)).wait_send()
        acc = buf[0]
        for d in range(1, 8):
            acc = acc + buf[d]
        o_ref[...] = acc

    ins = [d_norm_mix, d_norm_mem, d_norm_ffn, d_gains, d_cw8, d_cbias, d_qg, d_kg, d_mqg, d_mkg, d_sink8, loss8]
    (tot,), sent = _run("allreduce_small", body, (), ins, [VM] * len(ins), [SDS((8, 1024), f32)], [VM],
                        scratch=[pltpu.VMEM((8, 8, 1024), f32), DMA((7,)), DMA((7,))], exchange=exchange)
    return tot, sent


def add_halves(cidx, grads, stages, name, nch=2):
    n = len(grads)

    def body(c_ref, *refs):
        g, st, o = refs[:n], refs[n:2 * n], refs[2 * n:]
        for e in range(n):
            o[e][...] = (g[e][...] + st[e][...]).astype(WIRE)

    in_specs, out_specs, out_shape = [], [], []
    for a in grads:
        hr, C = a.shape[2], a.shape[3]
        in_specs.append(pl.BlockSpec((None, None, hr // nch, C), lambda s, q, c_ref: (s, c_ref[0], q, 0)))
    for a in stages:
        hr, C = a.shape[1], a.shape[2]
        in_specs.append(pl.BlockSpec((None, hr // nch, C), lambda s, q, c_ref: (s, q, 0)))
        out_specs.append(pl.BlockSpec((None, hr // nch, C), lambda s, q, c_ref: (s, q, 0)))
        out_shape.append(SDS(a.shape, WIRE))
    return pl.pallas_call(
        body, name=name, out_shape=out_shape,
        grid_spec=pltpu.PrefetchScalarGridSpec(num_scalar_prefetch=1, grid=(4, nch), in_specs=in_specs, out_specs=out_specs),
        compiler_params=pltpu.CompilerParams(dimension_semantics=("arbitrary", "arbitrary")),
    )(cidx, *grads, *stages)


def _adamw_math(w, g, m, v):
    m = ADAM_B1 * m + (1.0 - ADAM_B1) * g
    v = ADAM_B2 * v + (1.0 - ADAM_B2) * (g * g)
    m_hat = m / (1.0 - ADAM_B1 ** ADAM_STEP)
    v_hat = v / (1.0 - ADAM_B2 ** ADAM_STEP)
    delta = -ADAM_LR * (m_hat / (jnp.sqrt(v_hat) + ADAM_EPS) + ADAM_WD * w)
    return delta, m, v


def _sum_chips(st):
    return ((st[0].astype(f32) + st[1].astype(f32)) + st[2].astype(f32)) + st[3].astype(f32)


def adamw_big(name, stages, ws, ms, vs, nstep, exchange=None):
    n = len(stages)

    def body(*refs):
        st, w, m, v = refs[:n], refs[n:2 * n], refs[2 * n:3 * n], refs[3 * n:4 * n]
        outs = refs[4 * n:]
        for e in range(n):
            g = jnp.concatenate([_sum_chips(st[e].at[0]), _sum_chips(st[e].at[1])], axis=0)
            d, mm, vv = _adamw_math(w[e][...], g, m[e][...], v[e][...])
            outs[4 * e][...] = g
            outs[4 * e + 1][...] = d
            outs[4 * e + 2][...] = mm
            outs[4 * e + 3][...] = vv

    st_specs, w_specs = [], []
    for e in range(n):
        _, _, hr, C = stages[e].shape
        st_specs.append(pl.BlockSpec((2, 4, hr, C // nstep), lambda i: (0, 0, 0, i)))
        w_specs.append(pl.BlockSpec((2 * hr, C // nstep), lambda i: (0, i)))
    out_specs = [s for s in w_specs for _ in range(4)]
    out_shape = [SDS(w.shape, f32) for w in ws for _ in range(4)]
    res = _run(name, body, (nstep,), list(stages) + list(ws) + list(ms) + list(vs), st_specs + w_specs * 3,
               out_shape, out_specs, vmem_mib=48, exchange=exchange)
    res, sent = res if exchange is not None else (res, None)
    return [res[4 * e:4 * e + 4] for e in range(n)], sent


def adamw_small(tot, pk_w, pk_m, pk_v, shapes):
    def body(tot_ref, w_ref, m_ref, v_ref, *outs):
        x, y, _ = _place()
        chip = 2 * x + y
        taps = []
        for j in range(3):
            mine = tot_ref[4:5, j * CONV_W:j * CONV_W + HD]
            for s in range(1, 4):
                mine = jnp.where(chip == s, tot_ref[4:5, j * CONV_W + s * HD:j * CONV_W + (s + 1) * HD], mine)
            taps.append(mine)
        row4 = jnp.concatenate(taps + [jnp.zeros((1, 3 * CONV_W - 3 * HD), f32), tot_ref[4:5, 3 * CONV_W:]], axis=1)
        tot_v = tot_ref[...]
        row = lax.broadcasted_iota(jnp.int32, tot_v.shape, 0)
        g = jnp.where(row == 4, jnp.broadcast_to(row4, tot_v.shape), tot_v)
        d, mm, vv = _adamw_math(w_ref[...], g, m_ref[...], v_ref[...])
        for i, name in enumerate(SMALL):
            for k, val in enumerate((g, d, mm, vv)):
                if name == "conv_w":
                    outs[4 * i + k][...] = jnp.concatenate([val[4:5, j * HD:(j + 1) * HD] for j in range(3)], axis=0)[None]
                else:
                    r, c0, w = SMALL_AT[name]
                    outs[4 * i + k][...] = val[r:r + 1, c0:c0 + w]

    out_shape = [SDS(shapes[k], f32) for k in SMALL for _ in range(4)]
    res = _run("adamw_small", body, (), [tot, pk_w, pk_m, pk_v], [VM] * 4, out_shape, [VM] * len(out_shape))
    return {k: res[4 * i:4 * i + 4] for i, k in enumerate(SMALL)}


def prep_weights(shards):
    n = len(shards)

    def body(*refs):
        for e in range(n):
            refs[n + e][...] = _c(refs[e][...])

    return _run("prep_weights", body, (), shards, [VM] * n, [SDS(a.shape, MXU) for a in shards], [VM] * n, vmem_mib=48)


def mem_kv_fwd(mem2d, pk, wmkv):
    M, D = mem2d.shape

    def body(m_ref, pk_ref, w_ref, mn_ref, kv_ref, km_ref, vm_ref):
        m = m_ref[...]
        mn = _c(m * _rstd(m) * _small(pk_ref, "norm_mem"))
        mn_ref[...] = mn
        kv = _nn(mn, w_ref[...])
        kv_ref[...] = kv
        kk = kv[:, :MEM_W]
        km_ref[...] = _c(kk * _heads_rstd(kk) * _lanes(_small(pk_ref, "mem_k_norm"), MEM_W))
        vm_ref[...] = _c(kv[:, MEM_W:])

    return _run("mem_kv_fwd", body, (), [mem2d, pk, wmkv], [VM] * 3,
                [SDS((M, D), MXU), SDS((M, 2 * MEM_W), f32), SDS((M, MEM_W), MXU), SDS((M, MEM_W), MXU)], [VM] * 4)


QKV_W = ATT_W + 2 * KV_W + MEM_W


def in_proj_fwd(x2d, pk, winT, tm, exchange):
    T, D = x2d.shape
    P = winT.shape[0]

    def body(x_ref, pk_ref, w_ref, xn_ref, proj_ref, qkv_ref):
        xv = x_ref[...]
        xn = _c(xv * _rstd(xv) * _small(pk_ref, "norm_mix"))
        xn_ref[...] = xn
        proj = _nt(xn, w_ref[...])
        proj_ref[...] = proj
        q, k = proj[:, :ATT_W], proj[:, ATT_W:ATT_W + KV_W]
        qm = proj[:, P - MEM_W:]
        qkv_ref[...] = jnp.concatenate(
            [_c(q * _heads_rstd(q) * _lanes(_small(pk_ref, "q_norm"), ATT_W)),
             _c(k * _heads_rstd(k) * _lanes(_small(pk_ref, "k_norm"), KV_W)),
             _c(proj[:, ATT_W + KV_W:ATT_W + 2 * KV_W]),
             _c(qm * _heads_rstd(qm) * _lanes(_small(pk_ref, "mem_q_norm"), MEM_W))], axis=1)

    return _run("in_proj_fwd", body, (T // tm,), [x2d, pk, winT],
                [pl.BlockSpec((tm, D), lambda i: (i, 0)), VM, VM],
                [SDS((T, D), MXU), SDS((T, P), f32), SDS((T, QKV_W), MXU)],
                [pl.BlockSpec((tm, D), lambda i: (i, 0)), pl.BlockSpec((tm, P), lambda i: (i, 0)),
                 pl.BlockSpec((tm, QKV_W), lambda i: (i, 0))],
                vmem_mib=40, exchange=exchange)


def _swa_setup(j, g, sk_ref):
    rows = GQA * BLK
    ri = lax.broadcasted_iota(jnp.int32, (rows, 2 * BLK), 0)
    ki = lax.broadcasted_iota(jnp.int32, (rows, 2 * BLK), 1)
    dist = (ri & (BLK - 1)) + BLK - ki
    valid = (dist >= 0) & (dist < BLK) & ((ki >= BLK) | (j > 0))
    hrow = lax.broadcasted_iota(jnp.int32, (rows, 1), 0) // BLK
    slope = jnp.zeros((rows, 1), f32)
    sink = jnp.zeros((rows, 1), f32)
    for hh in range(GQA):
        h = g * GQA + hh
        slope = jnp.where(hrow == hh, 2.0 ** -(h + 1), slope)
        sink = jnp.where(hrow == hh, sk_ref[h:h + 1, 0:1], sink)
    return valid, slope * dist.astype(f32), sink


def _stack_heads(v, g):
    return jnp.concatenate([v[:, (g * GQA + hh) * HD:(g * GQA + hh + 1) * HD] for hh in range(GQA)], axis=0)


def attn_fwd(qkv, sink_rows, BL, S, exchange):
    NB = S // BLK
    T = BL * S

    def body(q_ref, kc_ref, kp_ref, vc_ref, vp_ref, sk_ref, o_ref):
        j = pl.program_id(1)
        q = q_ref[...]
        k2 = jnp.concatenate([kp_ref[...], kc_ref[...]], axis=0)
        v2 = jnp.concatenate([vp_ref[...], vc_ref[...]], axis=0)
        ones = jnp.ones((2 * BLK, HD), MXU)
        for g in range(N_KV):
            valid, bias, sink = _swa_setup(j, g, sk_ref)
            kn, vh = k2[:, g * HD:(g + 1) * HD], v2[:, g * HD:(g + 1) * HD]
            s = jnp.where(valid, _nt(_stack_heads(q, g), kn) * (HD ** -0.5) - bias, NEG)
            e, es = _exp_scores(s, sink)
            eb = _c(e)
            o = _nn(eb, vh) * (1.0 / (_nn(eb, ones) + es))
            for hh in range(GQA):
                o_ref[:, pl.ds((g * GQA + hh) * HD, HD)] = o[hh * BLK:(hh + 1) * BLK]

    cur = lambda col: (lambda b, j: (b * NB + j, col))
    prev = lambda col: (lambda b, j: (b * NB + jnp.maximum(j - 1, 0), col))
    return _run("attn_fwd", body, (BL, NB), [qkv, qkv, qkv, qkv, qkv, sink_rows],
                [pl.BlockSpec((BLK, ATT_W), cur(0)),
                 pl.BlockSpec((BLK, KV_W), cur(4)), pl.BlockSpec((BLK, KV_W), prev(4)),
                 pl.BlockSpec((BLK, KV_W), cur(5)), pl.BlockSpec((BLK, KV_W), prev(5)),
                 pl.BlockSpec((8, 128), lambda b, j: (0, 0))],
                [SDS((T, ATT_W), f32)], [pl.BlockSpec((BLK, ATT_W), cur(0))], exchange=exchange)


def _conv_taps(u, uh):
    row = lax.broadcasted_iota(jnp.int32, u.shape, 0)
    u1 = jnp.where(row == 0, uh[7:8, :], pltpu.roll(u, 1, 0))
    u2 = jnp.where(row == 0, uh[6:7, :], jnp.where(row == 1, uh[7:8, :], pltpu.roll(u, 2, 0)))
    return u1, u2


def _mem_head(qm, km, vm, h):
    qh, kh, vh = (a[:, h * HD:(h + 1) * HD] for a in (qm, km, vm))
    e, _ = _exp_scores(_nt(qh, kh) * (HD ** -0.5))
    return qh, kh, vh, e


def mixer_tail_fwd(x2d, attn_out, proj, qkv, km, vm, conv_w8, pk, wout, S, tm, exchange):
    T, D = x2d.shape
    NM = km.shape[0] // (T // S)

    def body(x_ref, ao_ref, ch_ref, cb_ref, cc_ref, chh_ref, cch_ref, qm_ref, km_ref, vm_ref, cw_ref, pk_ref,
             wout_ref, co_ref, mo_ref, mg_ref, x1_ref, h_ref):
        first = (pl.program_id(0) * tm) % S == 0
        u = cc_ref[...] * ch_ref[...]
        uh = jnp.where(first, 0.0, cch_ref[...] * chh_ref[...])
        u1, u2 = _conv_taps(u, uh)
        conv = cw_ref[0:1, :] * u2 + cw_ref[1:2, :] * u1 + cw_ref[2:3, :] * u + _small(pk_ref, "conv_b")
        conv_out = cb_ref[...] * conv
        co_ref[...] = conv_out
        qm, kmv, vmv = qm_ref[...], km_ref[...], vm_ref[...]
        ones = jnp.ones((NM, HD), MXU)
        for h in range(N_MEMH):
            _, _, vh, e = _mem_head(qm, kmv, vmv, h)
            eb = _c(e)
            mo_ref[:, pl.ds(h * HD, HD)] = _nn(eb, vh) * (1.0 / _nn(eb, ones))
        mem_out = mo_ref[...]
        ao = ao_ref[...]
        merged = _c(jnp.concatenate([ao * _rstd(ao) * _small(pk_ref, "out_norm_attn"),
                                     conv_out * _rstd(conv_out) * _small(pk_ref, "out_norm_conv"),
                                     mem_out * _rstd(mem_out) * _small(pk_ref, "out_norm_mem")], axis=1))
        mg_ref[...] = merged
        x1 = x_ref[...] + _nn(merged, wout_ref[...])
        x1_ref[...] = x1
        h_ref[...] = _c(x1 * _rstd(x1) * _small(pk_ref, "norm_ffn"))

    tile = lambda w, col: pl.BlockSpec((tm, w), lambda i: (i, col))
    halo = lambda col: pl.BlockSpec((8, CONV_W), lambda i: (jnp.maximum(i * (tm // 8) - 1, 0), col))
    seq = pl.BlockSpec((NM, MEM_W), lambda i: ((i * tm) // S, 0))
    small = lambda a: pl.BlockSpec(a.shape, lambda i: (0, 0))
    return _run("mixer_tail_fwd", body, (T // tm,),
                [x2d, attn_out, proj, proj, proj, proj, proj, qkv, km, vm, conv_w8, pk, wout],
                [tile(D, 0), tile(ATT_W, 0), tile(CONV_W, 3), tile(CONV_W, 4), tile(CONV_W, 5), halo(3), halo(5),
                 tile(MEM_W, 3), seq, seq, VM, VM, VM],
                [SDS((T, CONV_W), f32), SDS((T, MEM_W), f32), SDS((T, D), MXU), SDS((T, D), f32), SDS((T, D), MXU)],
                [tile(CONV_W, 0), tile(MEM_W, 0), tile(D, 0), tile(D, 0), tile(D, 0)], vmem_mib=40, exchange=exchange)


def ffn_fwd_bwd(h, x1, tgt, wgT, wuT, wd, pk, tm):
    T, D = x1.shape
    F = wd.shape[0]

    def body(h_ref, x1_ref, t_ref, wg_ref, wu_ref, wd_ref, pk_ref,
             dx1_ref, dx2_ref, act_ref, dg_ref, du_ref, loss_ref, dgf_ref):
        @pl.when(pl.program_id(0) == 0)
        def _():
            loss_ref[...] = jnp.zeros_like(loss_ref)
            dgf_ref[...] = jnp.zeros_like(dgf_ref)

        hv = h_ref[...]
        gate = _nt(hv, wg_ref[...])
        up = _nt(hv, wu_ref[...])
        sg = jax.nn.sigmoid(gate)
        sl = gate * sg
        act = _c(sl * up)
        act_ref[...] = act
        x1v = x1_ref[...]
        diff = (x1v + _nn(act, wd_ref[...])) - t_ref[...]
        loss_ref[...] += 0.5 * jnp.sum(jnp.sum(diff * diff, axis=-1, keepdims=True) / D, axis=0, keepdims=True)
        dx2 = diff / D
        dx2b = _c(dx2)
        dx2_ref[...] = dx2b
        d_act = _nt(dx2b, wd_ref[...])
        d_up = _c(d_act * sl)
        d_gate = _c(d_act * up * (sg * (1.0 + gate * (1.0 - sg))))
        du_ref[...] = d_up
        dg_ref[...] = d_gate
        dh = _nn(d_gate, wg_ref[...]) + _nn(d_up, wu_ref[...])
        dv, dgf = _norm_bwd(dh, x1v, _rstd(x1v), _small(pk_ref, "norm_ffn"))
        dx1_ref[...] = dx2 + dv
        dgf_ref[...] += dgf

    tile = lambda w: pl.BlockSpec((tm, w), lambda i: (i, 0))
    return _run("ffn_fwd_bwd", body, (T // tm,), [h, x1, tgt, wgT, wuT, wd, pk],
                [tile(D), tile(D), tile(D), VM, VM, VM, VM],
                [SDS((T, D), f32), SDS((T, D), MXU), SDS((T, F), MXU), SDS((T, F), MXU), SDS((T, F), MXU),
                 SDS((8, 128), f32), SDS((1, D), f32)],
                [tile(D), tile(D), tile(F), tile(F), tile(F), pl.BlockSpec((8, 128), lambda i: (0, 0)),
                 pl.BlockSpec((1, D), lambda i: (0, 0))], vmem_mib=56)


def matmul_tn(a, b, name, tmo, tk):
    T, M = a.shape
    N = b.shape[1]

    def body(a_ref, b_ref, o_ref):
        @pl.when(pl.program_id(1) == 0)
        def _():
            o_ref[...] = jnp.zeros_like(o_ref)

        o_ref[...] += _tn(a_ref[...], b_ref[...])

    return _run(name, body, (M // tmo, T // tk), [a, b],
                [pl.BlockSpec((tk, tmo), lambda m, k: (k, m)), pl.BlockSpec((tk, N), lambda m, k: (k, 0))],
                [SDS((M, N), f32)], [pl.BlockSpec((tmo, N), lambda m, k: (m, 0))], vmem_mib=48)[0]


def out_proj_bwd(dx1, merged, attn_out, conv_out, mem_out, pk, wout, tm, exchange):
    T, D = dx1.shape

    def body(dx1_ref, mg_ref, ao_ref, co_ref, mo_ref, pk_ref, w_ref,
             dao_ref, dco_ref, dmo_ref, dw_ref, dgain_ref):
        @pl.when(pl.program_id(0) == 0)
        def _():
            dw_ref[...] = jnp.zeros_like(dw_ref)
            dgain_ref[...] = jnp.zeros_like(dgain_ref)

        dxb = _c(dx1_ref[...])
        dw_ref[...] += _tn(mg_ref[...], dxb)
        dmg = _nt(dxb, w_ref[...])
        ao, co, mo = ao_ref[...], co_ref[...], mo_ref[...]
        da, ga = _norm_bwd(dmg[:, :ATT_W], ao, _rstd(ao), _small(pk_ref, "out_norm_attn"))
        dc, gc = _norm_bwd(dmg[:, ATT_W:ATT_W + CONV_W], co, _rstd(co), _small(pk_ref, "out_norm_conv"))
        dm, gm = _norm_bwd(dmg[:, ATT_W + CONV_W:], mo, _rstd(mo), _small(pk_ref, "out_norm_mem"))
        dao_ref[...] = da
        dco_ref[...] = dc
        dmo_ref[...] = dm
        dgain_ref[...] += jnp.concatenate([ga, gc, gm], axis=1)

    tile = lambda w: pl.BlockSpec((tm, w), lambda i: (i, 0))
    return _run("out_proj_bwd", body, (T // tm,), [dx1, merged, attn_out, conv_out, mem_out, pk, wout],
                [tile(D), tile(D), tile(ATT_W), tile(CONV_W), tile(MEM_W), VM, VM],
                [SDS((T, ATT_W), f32), SDS((T, CONV_W), f32), SDS((T, MEM_W), f32), SDS((D, D), f32), SDS((1, D), f32)],
                [tile(ATT_W), tile(CONV_W), tile(MEM_W), pl.BlockSpec((D, D), lambda i: (0, 0)),
                 pl.BlockSpec((1, D), lambda i: (0, 0))], vmem_mib=40, exchange=exchange)


def attn_bwd(qkv, d_attn, attn_out, sink_rows, BL, S, exchange):
    NB = S // BLK
    T = BL * S

    def body(q_ref, kc_ref, kp_ref, vc_ref, vp_ref, do_ref, ao_ref, sk_ref,
             dq_ref, dk_ref, dv_ref, dsk_ref, pend_k, pend_v):
        b, j = pl.program_id(0), pl.program_id(1)

        @pl.when((b == 0) & (j == 0))
        def _():
            dsk_ref[...] = jnp.zeros_like(dsk_ref)

        @pl.when(j == 0)
        def _():
            pend_k[...] = jnp.zeros_like(pend_k)
            pend_v[...] = jnp.zeros_like(pend_v)

        @pl.when(j < NB)
        def _():
            q, do, ao = q_ref[...], do_ref[...], ao_ref[...]
            k2 = jnp.concatenate([kp_ref[...], kc_ref[...]], axis=0)
            v2 = jnp.concatenate([vp_ref[...], vc_ref[...]], axis=0)
            lane = lax.broadcasted_iota(jnp.int32, (8, 128), 1)
            ones_w = jnp.ones((2 * BLK, 2 * BLK), MXU)
            dsk = jnp.zeros((8, 128), f32)
            dks, dvs = [], []
            for g in range(N_KV):
                valid, bias, sink = _swa_setup(j, g, sk_ref)
                kn, vh = k2[:, g * HD:(g + 1) * HD], v2[:, g * HD:(g + 1) * HD]
                qs = _stack_heads(q, g)
                s = jnp.where(valid, _nt(qs, kn) * (HD ** -0.5) - bias, NEG)
                e, es = _exp_scores(s, sink)
                eb = _c(e)
                inv_w = 1.0 / (_nn(eb, ones_w) + es)
                inv_n = inv_w[:, :HD]
                dos = _stack_heads(do, g)
                delta = _rowsum_mxu(dos * _stack_heads(ao, g), 2 * BLK)
                dp = _nt(_c(dos), vh)
                ds = _c(e * inv_w * (dp - delta) * (HD ** -0.5))
                t = es * inv_n[:, 0:1] * delta[:, 0:1]
                for hh in range(GQA):
                    dsk = dsk + jnp.where(lane == g * GQA + hh, -jnp.sum(t[hh * BLK:(hh + 1) * BLK]), 0.0)
                dvs.append(_tn(eb, _c(dos * inv_n)))
                dks.append(_tn(ds, qs))
                dqs = _nn(ds, kn)
                for hh in range(GQA):
                    dq_ref[:, pl.ds((g * GQA + hh) * HD, HD)] = dqs[hh * BLK:(hh + 1) * BLK]
            dk2 = jnp.concatenate(dks, axis=1)
            dv2 = jnp.concatenate(dvs, axis=1)
            dk_ref[...] = pend_k[...] + dk2[:BLK]
            dv_ref[...] = pend_v[...] + dv2[:BLK]
            pend_k[...] = dk2[BLK:]
            pend_v[...] = dv2[BLK:]
            dsk_ref[...] += dsk

        @pl.when(j == NB)
        def _():
            dk_ref[...] = pend_k[...]
            dv_ref[...] = pend_v[...]

    cur = lambda col: (lambda b, j: (b * NB + jnp.minimum(j, NB - 1), col))
    prev = lambda col: (lambda b, j: (b * NB + jnp.maximum(j - 1, 0), col))
    small = lambda shape: pl.BlockSpec(shape, lambda b, j: (0, 0))
    return _run("attn_bwd", body, (BL, NB + 1), [qkv, qkv, qkv, qkv, qkv, d_attn, attn_out, sink_rows],
                [pl.BlockSpec((BLK, ATT_W), cur(0)),
                 pl.BlockSpec((BLK, KV_W), cur(4)), pl.BlockSpec((BLK, KV_W), prev(4)),
                 pl.BlockSpec((BLK, KV_W), cur(5)), pl.BlockSpec((BLK, KV_W), prev(5)),
                 pl.BlockSpec((BLK, ATT_W), cur(0)), pl.BlockSpec((BLK, ATT_W), cur(0)), small((8, 128))],
                [SDS((T, ATT_W), f32), SDS((T, KV_W), f32), SDS((T, KV_W), f32), SDS((8, 128), f32)],
                [pl.BlockSpec((BLK, ATT_W), cur(0)), pl.BlockSpec((BLK, KV_W), prev(0)),
                 pl.BlockSpec((BLK, KV_W), prev(0)), small((8, 128))],
                scratch=[pltpu.VMEM((BLK, KV_W), f32)] * 2, exchange=exchange)


def mem_conv_bwd(d_mem_out, mem_out, d_conv_out, proj, qkv, km, vm, conv_w8, pk, S, tm, exchange):
    T = d_mem_out.shape[0]
    NM = km.shape[0] // (T // S)

    def body(dmo_ref, mo_ref, dco_ref, ch_ref, cb_ref, cc_ref, chh_ref, cch_ref, qm_ref, km_ref, vm_ref, cw_ref,
             pk_ref, dqm_ref, dkm_ref, dvm_ref, dcb_ref, dcv_ref, dcw_ref, dcbias_ref):
        i = pl.program_id(0)
        first = (i * tm) % S == 0

        @pl.when(i == 0)
        def _():
            dcw_ref[...] = jnp.zeros_like(dcw_ref)
            dcbias_ref[...] = jnp.zeros_like(dcbias_ref)

        @pl.when(first)
        def _():
            dkm_ref[...] = jnp.zeros_like(dkm_ref)
            dvm_ref[...] = jnp.zeros_like(dvm_ref)

        qm, kmv, vmv, dmo, mo = qm_ref[...], km_ref[...], vm_ref[...], dmo_ref[...], mo_ref[...]
        ones_w = jnp.ones((NM, NM), MXU)
        for h in range(N_MEMH):
            qh, kh, vh, e = _mem_head(qm, kmv, vmv, h)
            eb = _c(e)
            doh = dmo[:, h * HD:(h + 1) * HD]
            delta = _rowsum_mxu(doh * mo[:, h * HD:(h + 1) * HD], NM)
            dp = _nt(_c(doh), vh)
            inv_w = 1.0 / _nn(eb, ones_w)
            ds = _c(e * inv_w * (dp - delta) * (HD ** -0.5))
            dvm_ref[:, pl.ds(h * HD, HD)] += _tn(eb, _c(doh * inv_w[:, :HD]))
            dkm_ref[:, pl.ds(h * HD, HD)] += _tn(ds, qh)
            dqm_ref[:, pl.ds(h * HD, HD)] = _nn(ds, kh)

        u = cc_ref[...] * ch_ref[...]
        uh = jnp.where(first, 0.0, cch_ref[...] * chh_ref[...])
        u1, u2 = _conv_taps(u, uh)
        conv = cw_ref[0:1, :] * u2 + cw_ref[1:2, :] * u1 + cw_ref[2:3, :] * u + _small(pk_ref, "conv_b")
        dy = dco_ref[...]
        dcb_ref[...] = dy * conv
        dcv = dy * cb_ref[...]
        dcv_ref[...] = dcv
        dcbias_ref[...] += jnp.sum(dcv, axis=0, keepdims=True)
        dcw_ref[0:1, :] += jnp.sum(dcv * u2, axis=0, keepdims=True)
        dcw_ref[1:2, :] += jnp.sum(dcv * u1, axis=0, keepdims=True)
        dcw_ref[2:3, :] += jnp.sum(dcv * u, axis=0, keepdims=True)

    tile = lambda w, col: pl.BlockSpec((tm, w), lambda i: (i, col))
    halo = lambda col: pl.BlockSpec((8, CONV_W), lambda i: (jnp.maximum(i * (tm // 8) - 1, 0), col))
    seq = pl.BlockSpec((NM, MEM_W), lambda i: ((i * tm) // S, 0))
    const = lambda shape: pl.BlockSpec(shape, lambda i: (0, 0))
    return _run("mem_conv_bwd", body, (T // tm,),
                [d_mem_out, mem_out, d_conv_out, proj, proj, proj, proj, proj, qkv, km, vm, conv_w8, pk],
                [tile(MEM_W, 0), tile(MEM_W, 0), tile(CONV_W, 0), tile(CONV_W, 3), tile(CONV_W, 4), tile(CONV_W, 5),
                 halo(3), halo(5), tile(MEM_W, 3), seq, seq, VM, VM],
                [SDS((T, MEM_W), f32), SDS(km.shape, f32), SDS(km.shape, f32),
                 SDS((T, CONV_W), f32), SDS((T, CONV_W), f32), SDS((8, CONV_W), f32), SDS((1, CONV_W), f32)],
                [tile(MEM_W, 0), seq, seq, tile(CONV_W, 0), tile(CONV_W, 0), const((8, CONV_W)), const((1, CONV_W))],
                exchange=exchange)


def in_proj_bwd(dqn, dkn, dv, dcb, dcv, dqmn, proj, conv_w8, xn, x2d, dx1, pk, winT, S, tm, exchange):
    T, D = x2d.shape
    P = winT.shape[0]
    last_blk = T // 8 - 1

    def body(dq_ref, dk_ref, dv_ref, dcb_ref, dcv_ref, dcvn_ref, dqm_ref, qa_ref, ka_ref, ch_ref, cc_ref, qma_ref,
             cw_ref, xn_ref, x_ref, dx1_ref, pk_ref, w_ref,
             dx_ref, dw_ref, dg_ref, dqg_ref, dkg_ref, dmqg_ref):
        i = pl.program_id(0)

        @pl.when(i == 0)
        def _():
            dw_ref[...] = jnp.zeros_like(dw_ref)
            dg_ref[...] = jnp.zeros_like(dg_ref)
            dqg_ref[...] = jnp.zeros_like(dqg_ref)
            dkg_ref[...] = jnp.zeros_like(dkg_ref)
            dmqg_ref[...] = jnp.zeros_like(dmqg_ref)

        dqa, gq = _heads_norm_bwd(dq_ref[...], qa_ref[...], _small(pk_ref, "q_norm"))
        dka, gk = _heads_norm_bwd(dk_ref[...], ka_ref[...], _small(pk_ref, "k_norm"))
        dqma, gmq = _heads_norm_bwd(dqm_ref[...], qma_ref[...], _small(pk_ref, "mem_q_norm"))
        dqg_ref[...] += gq
        dkg_ref[...] += gk
        dmqg_ref[...] += gmq

        last = ((i + 1) * tm) % S == 0
        dcv = dcv_ref[...]
        nxt = jnp.where(last, 0.0, dcvn_ref[...])
        row = lax.broadcasted_iota(jnp.int32, dcv.shape, 0)
        n1 = jnp.where(row == tm - 1, nxt[0:1, :], pltpu.roll(dcv, tm - 1, 0))
        n2 = jnp.where(row == tm - 2, nxt[0:1, :], jnp.where(row == tm - 1, nxt[1:2, :], pltpu.roll(dcv, tm - 2, 0)))
        du = cw_ref[2:3, :] * dcv + cw_ref[1:2, :] * n1 + cw_ref[0:1, :] * n2
        d_proj = jnp.concatenate([_c(dqa), _c(dka), _c(dv_ref[...]), _c(du * cc_ref[...]),
                                  _c(dcb_ref[...]), _c(du * ch_ref[...]), _c(dqma)], axis=1)
        dw_ref[...] += _tn(d_proj, xn_ref[...])
        xv = x_ref[...]
        dv_, dg = _norm_bwd(_nn(d_proj, w_ref[...]), xv, _rstd(xv), _small(pk_ref, "norm_mix"))
        dx_ref[...] = dx1_ref[...] + dv_
        dg_ref[...] += dg

    tile = lambda w, col=0: pl.BlockSpec((tm, w), lambda i: (i, col))
    nhalo = pl.BlockSpec((8, CONV_W), lambda i: (jnp.minimum((i + 1) * (tm // 8), last_blk), 0))
    const = lambda shape: pl.BlockSpec(shape, lambda i: (0, 0))
    return _run("in_proj_bwd", body, (T // tm,),
                [dqn, dkn, dv, dcb, dcv, dcv, dqmn, proj, proj, proj, proj, proj, conv_w8, xn, x2d, dx1, pk, winT],
                [tile(ATT_W), tile(KV_W), tile(KV_W), tile(CONV_W), tile(CONV_W), nhalo, tile(MEM_W),
                 tile(ATT_W, 0), tile(KV_W, 4), tile(CONV_W, 3), tile(CONV_W, 5), tile(MEM_W, 6), VM,
                 tile(D), tile(D), tile(D), VM, VM],
                [SDS((T, D), f32), SDS((P, D), f32), SDS((1, D), f32), SDS((1, HD), f32), SDS((1, HD), f32),
                 SDS((1, HD), f32)],
                [tile(D), pl.BlockSpec((P, D), lambda i: (0, 0)), const((1, D)), const((1, HD)), const((1, HD)),
                 const((1, HD))],
                vmem_mib=48, exchange=exchange)


def mem_kv_bwd(dkm, dvm, kv, memn, mem2d, pk, wmkv):
    def body(dkm_ref, dvm_ref, kv_ref, mn_ref, m_ref, pk_ref, w_ref, dw_ref, dg_ref, dkg_ref):
        dkk, dkg = _heads_norm_bwd(dkm_ref[...], kv_ref[:, :MEM_W], _small(pk_ref, "mem_k_norm"))
        dkg_ref[...] = dkg
        dkv = _c(jnp.concatenate([dkk, dvm_ref[...]], axis=1))
        dw_ref[...] = _tn(mn_ref[...], dkv)
        mv = m_ref[...]
        dg_ref[...] = jnp.sum(_nt(dkv, w_ref[...]) * mv * _rstd(mv), axis=0, keepdims=True)

    return _run("mem_kv_bwd", body, (), [dkm, dvm, kv, memn, mem2d, pk, wmkv], [VM] * 7,
                [SDS(wmkv.shape, f32), SDS((1, mem2d.shape[1]), f32), SDS((1, HD), f32)], [VM] * 3, vmem_mib=40)


def _halves_view(g):
    return g.reshape(4, 2, g.shape[0] // 8, g.shape[1])


def kernel(x, mem, norm_mix, w_in, q_norm, k_norm, attn_sinks, conv_w, conv_b, norm_mem, w_mem_kv, mem_q_norm, mem_k_norm, out_norm_attn, out_norm_conv, out_norm_mem, w_out, norm_ffn, w_gate, w_up, w_down, loss_target, m_norm_mix, m_w_in, m_q_norm, m_k_norm, m_attn_sinks, m_conv_w, m_conv_b, m_norm_mem, m_w_mem_kv, m_mem_q_norm, m_mem_k_norm, m_out_norm_attn, m_out_norm_conv, m_out_norm_mem, m_w_out, m_norm_ffn, m_w_gate, m_w_up, m_w_down, v_norm_mix, v_w_in, v_q_norm, v_k_norm, v_attn_sinks, v_conv_w, v_conv_b, v_norm_mem, v_w_mem_kv, v_mem_q_norm, v_mem_k_norm, v_out_norm_attn, v_out_norm_conv, v_out_norm_mem, v_w_out, v_norm_ffn, v_w_gate, v_w_up, v_w_down):
    BL, S, D = x.shape
    T = BL * S
    TM = 256
    _, _, ci = _place()
    cidx = ci.reshape(1).astype(jnp.int32)
    w_small = dict(norm_mix=norm_mix, norm_mem=norm_mem, norm_ffn=norm_ffn, out_norm_attn=out_norm_attn,
                   out_norm_conv=out_norm_conv, out_norm_mem=out_norm_mem, conv_w=conv_w, conv_b=conv_b, q_norm=q_norm,
                   k_norm=k_norm, mem_q_norm=mem_q_norm, mem_k_norm=mem_k_norm, attn_sinks=attn_sinks)
    m_small = dict(norm_mix=m_norm_mix, norm_mem=m_norm_mem, norm_ffn=m_norm_ffn, out_norm_attn=m_out_norm_attn,
                   out_norm_conv=m_out_norm_conv, out_norm_mem=m_out_norm_mem, conv_w=m_conv_w, conv_b=m_conv_b,
                   q_norm=m_q_norm, k_norm=m_k_norm, mem_q_norm=m_mem_q_norm, mem_k_norm=m_mem_k_norm,
                   attn_sinks=m_attn_sinks)
    v_small = dict(norm_mix=v_norm_mix, norm_mem=v_norm_mem, norm_ffn=v_norm_ffn, out_norm_attn=v_out_norm_attn,
                   out_norm_conv=v_out_norm_conv, out_norm_mem=v_out_norm_mem, conv_w=v_conv_w, conv_b=v_conv_b,
                   q_norm=v_q_norm, k_norm=v_k_norm, mem_q_norm=v_mem_q_norm, mem_k_norm=v_mem_k_norm,
                   attn_sinks=v_attn_sinks)
    pk = _pack_small(w_small)

    rowblocks = lambda a, b, c, d, e, f: [a[0].T, b[0].T, c[0].T, d[0], e[0], f[0]]
    w_rb = rowblocks(w_in, w_gate, w_up, w_down, w_out, w_mem_kv)
    m_rb = rowblocks(m_w_in, m_w_gate, m_w_up, m_w_down, m_w_out, m_w_mem_kv)
    v_rb = rowblocks(v_w_in, v_w_gate, v_w_up, v_w_down, v_w_out, v_w_mem_kv)
    winT_s, wgT_s, wuT_s, wd_s, wout_s, wmkv_s = prep_weights(w_rb)
    cw_pad = jnp.zeros((8, 128), f32).at[:3, :HD].set(conv_w[0])
    _, (winT, cw_all) = _run("gather_w_in", None, (), [], [], [], [], exchange=gather_exchange([winT_s, cw_pad], [True, False]))
    conv_w_full = jnp.transpose(cw_all.reshape(4, 8, 128)[:, :3, :HD], (1, 0, 2)).reshape(3, CONV_W)
    conv_w8 = jnp.zeros((8, CONV_W), f32).at[:3].set(conv_w_full)
    sink_rows = jnp.broadcast_to(attn_sinks.reshape(N_Q, 1), (N_Q, 128))

    x2d = x.reshape(T, D)
    mem2d = mem.reshape(-1, D)
    (xn, proj, qkv), (wgT,) = in_proj_fwd(x2d, pk, winT, TM, gather_exchange([wgT_s], [True]))
    (attn_out,), (wuT, wout, wmkv) = attn_fwd(qkv, sink_rows, BL, S,
                                              gather_exchange([wuT_s, wout_s, wmkv_s], [True, True, True]))
    memn, kv, km, vm = mem_kv_fwd(mem2d, pk, wmkv)
    (conv_out, mem_out, merged, x1, h), (wd,) = mixer_tail_fwd(
        x2d, attn_out, proj, qkv, km, vm, conv_w8, pk, wout, S, TM, gather_exchange([wd_s], [True]))

    dx1, dx2b, act, d_gate, d_up, loss8, d_norm_ffn = ffn_fwd_bwd(h, x1, loss_target.reshape(T, D), wgT, wuT, wd, pk, TM)
    F = wd.shape[0]
    g_wd = matmul_tn(act, dx2b, "dw_down", F // 2, 512)
    g_wgT = matmul_tn(d_gate, h, "dw_gate", F // 2, 512)
    g_wuT = matmul_tn(d_up, h, "dw_up", F // 2, 512)

    ffn_halves = [_halves_view(g) for g in (g_wgT, g_wuT, g_wd)]
    (d_attn, d_conv_out, d_mem_out, g_wout, d_gains), ffn_sib = out_proj_bwd(
        dx1, merged, attn_out, conv_out, mem_out, pk, wout, TM, halves_exchange(ffn_halves))
    out_halves = [_halves_view(g_wout)]
    (dqmn, dkm, dvm, dcb, dcv, d_cw8, d_cbias), out_sib = mem_conv_bwd(
        d_mem_out, mem_out, d_conv_out, proj, qkv, km, vm, conv_w8, pk, S, TM, halves_exchange(out_halves))
    p_wgT, p_wuT, p_wd, p_wout = add_halves(cidx, ffn_halves + out_halves, ffn_sib + out_sib, "grad_add_halves_ffn")
    (dqn, dkn, dv, d_sink8), (st_wgT, st_wuT, st_wout) = attn_bwd(
        qkv, d_attn, attn_out, sink_rows, BL, S, scatter_exchange([p_wgT, p_wuT, p_wout]))
    (g_x, g_winT, d_norm_mix, d_qg, d_kg, d_mqg), (st_wd,) = in_proj_bwd(
        dqn, dkn, dv, dcb, dcv, dqmn, proj, conv_w8, xn, x2d, dx1, pk, winT, S, TM, scatter_exchange([p_wd]))
    g_wmkv, d_norm_mem, d_mkg = mem_kv_bwd(dkm, dvm, kv, memn, mem2d, pk, wmkv)

    tail = [_halves_view(g) for g in (g_winT, g_wmkv)]
    _, tail_sib = _run("grad_halves_tail", None, (), [], [], [], [], exchange=halves_exchange(tail))
    tail_part = add_halves(cidx, tail, tail_sib, "grad_add_halves_tail")
    tot, tail_stage = allreduce_small(d_norm_mix, d_norm_mem, d_norm_ffn, d_gains, d_cw8, d_cbias, d_qg, d_kg, d_mqg,
                                      d_mkg, d_sink8, loss8, scatter_exchange(tail_part))
    loss = tot[5, 384]
    late_res, _ = adamw_big("adamw_late", [st_wgT, st_wuT, st_wd, st_wout], w_rb[1:5], m_rb[1:5], v_rb[1:5], 8)
    tail_res, _ = adamw_big("adamw_tail", tail_stage, [w_rb[0], w_rb[5]], [m_rb[0], m_rb[5]], [v_rb[0], v_rb[5]], 4)
    res = {"w_in": [a.T[None] for a in tail_res[0]], "w_gate": [a.T[None] for a in late_res[0]],
           "w_up": [a.T[None] for a in late_res[1]], "w_down": [a[None] for a in late_res[2]],
           "w_out": [a[None] for a in late_res[3]], "w_mem_kv": [a[None] for a in tail_res[1]]}
    res.update(adamw_small(tot, pk, _pack_small(m_small), _pack_small(v_small), {k: w_small[k].shape for k in SMALL}))

    order = ["norm_mix", "w_in", "q_norm", "k_norm", "attn_sinks", "conv_w", "conv_b", "norm_mem", "w_mem_kv",
             "mem_q_norm", "mem_k_norm", "out_norm_attn", "out_norm_conv", "out_norm_mem", "w_out", "norm_ffn",
             "w_gate", "w_up", "w_down"]
    return (loss, g_x.reshape(BL, S, D), *[res[n][0] for n in order], *[res[n][1] for n in order],
            *[res[n][2] for n in order], *[res[n][3] for n in order])
```

```python
import collections
import functools

import jax
import jax.numpy as jnp
import numpy as np
from jax import lax
from jax.experimental import pallas as pl
from jax.experimental.pallas import tpu as pltpu

f32 = jnp.float32
MXU = jnp.bfloat16
WIRE = jnp.bfloat16
EPS = 1e-6
NEG = -1e30
HD = 64
BLK = 128
N_Q, N_KV, N_MEMH = 8, 2, 4
GQA = N_Q // N_KV
ATT_W, KV_W, CONV_W, MEM_W = 512, 128, 256, 256
VMEM_MIB = 1024 * 1024
ADAM_LR, ADAM_B1, ADAM_B2, ADAM_EPS, ADAM_WD, ADAM_STEP = 0.001, 0.9, 0.999, 1e-08, 0.01, 10

MESH = pl.DeviceIdType.MESH
VM = pl.BlockSpec(memory_space=pltpu.VMEM)
ANY = pl.BlockSpec(memory_space=pl.ANY)
SDS = jax.ShapeDtypeStruct
DMA = pltpu.SemaphoreType.DMA


def _c(v):
    return v.astype(MXU)


def _nn(a, b):
    return lax.dot_general(a, b, (((1,), (0,)), ((), ())), preferred_element_type=f32)


def _nt(a, b):
    return lax.dot_general(a, b, (((1,), (1,)), ((), ())), preferred_element_type=f32)


def _tn(a, b):
    return lax.dot_general(a, b, (((0,), (0,)), ((), ())), preferred_element_type=f32)


def _rstd(v):
    return lax.rsqrt(jnp.mean(v * v, axis=-1, keepdims=True) + EPS)


def _norm_bwd(dy, v, r, g):
    dyg = dy * g
    dv = r * dyg - v * (r * r * r) * jnp.mean(dyg * v, axis=-1, keepdims=True)
    return dv, jnp.sum(dy * v * r, axis=0, keepdims=True)


def _split3(v):
    hi = _c(v)
    r1 = v - hi.astype(f32)
    mid = _c(r1)
    return hi, mid, _c(r1 - mid.astype(f32))


def _rowsum_mxu(v, width):
    ones = jnp.ones((v.shape[1], width), MXU)
    return sum(_nn(a, ones) for a in _split3(v))


def _seg_sums(v):
    r = lax.broadcasted_iota(jnp.int32, (2 * HD, 2 * HD), 0) // HD
    c = lax.broadcasted_iota(jnp.int32, (2 * HD, 2 * HD), 1) // HD
    bd = (r == c).astype(MXU)
    outs = []
    for b in range(v.shape[1] // (2 * HD)):
        outs.append(sum(_nn(a, bd) for a in _split3(v[:, b * 2 * HD:(b + 1) * 2 * HD])))
    return outs[0] if len(outs) == 1 else jnp.concatenate(outs, axis=1)


def _lanes(g, width):
    return jnp.concatenate([g] * (width // HD), axis=1)


def _heads_rstd(v):
    return lax.rsqrt(_seg_sums(v * v) * (1.0 / HD) + EPS)


def _heads_norm_bwd(dy, v, g):
    r = _heads_rstd(v)
    gl = _lanes(g, v.shape[1])
    dyg = dy * gl
    dv = r * dyg - v * (r * r * r) * (_seg_sums(dyg * v) * (1.0 / HD))
    dgl = jnp.sum(dy * v * r, axis=0, keepdims=True)
    return dv, sum(dgl[:, s * HD:(s + 1) * HD] for s in range(v.shape[1] // HD))


def _exp_scores(s, extra=None):
    m = jnp.max(s, axis=-1, keepdims=True)
    if extra is None:
        return jnp.exp(s - m), None
    m = jnp.maximum(m, extra)
    return jnp.exp(s - m), jnp.exp(extra - m)


def _place():
    return lax.axis_index("x"), lax.axis_index("y"), lax.axis_index("c")


SMALL_AT = {"norm_mix": (0, 0, 1024), "norm_mem": (1, 0, 1024), "norm_ffn": (2, 0, 1024),
            "out_norm_attn": (3, 0, ATT_W), "out_norm_conv": (3, ATT_W, CONV_W), "out_norm_mem": (3, ATT_W + CONV_W, MEM_W),
            "conv_b": (4, 3 * CONV_W, CONV_W), "q_norm": (5, 0, HD), "k_norm": (5, HD, HD), "mem_q_norm": (5, 2 * HD, HD),
            "mem_k_norm": (5, 3 * HD, HD), "attn_sinks": (5, 256, N_Q)}
SMALL = ("norm_mix", "norm_mem", "norm_ffn", "out_norm_attn", "out_norm_conv", "out_norm_mem", "conv_w", "conv_b",
         "q_norm", "k_norm", "mem_q_norm", "mem_k_norm", "attn_sinks")


def _small(pk_ref, name):
    r, c0, w = SMALL_AT[name]
    return pk_ref[r:r + 1, c0:c0 + w]


def _pack_small(d):
    z = lambda n: jnp.zeros((1, n), f32)
    row3 = jnp.concatenate([d["out_norm_attn"], d["out_norm_conv"], d["out_norm_mem"]], axis=1)
    row4 = jnp.concatenate([d["conv_w"].reshape(1, 3 * HD), z(3 * CONV_W - 3 * HD), d["conv_b"]], axis=1)
    row5 = jnp.concatenate([d["q_norm"], d["k_norm"], d["mem_q_norm"], d["mem_k_norm"], d["attn_sinks"],
                            z(1024 - 4 * HD - N_Q)], axis=1)
    return jnp.concatenate([d["norm_mix"], d["norm_mem"], d["norm_ffn"], row3, row4, row5, z(1024), z(1024)], axis=0)


def _other_chips(x, y):
    return [(1 - x, y), (x, 1 - y), (1 - x, 1 - y)]


Exchange = collections.namedtuple("Exchange", "ins outs sems start finish")


def _run(name, body, grid, ins, in_specs, out_shape, out_specs, scratch=(), vmem_mib=32, exchange=None):
    ins, in_specs, out_shape, out_specs, scratch = list(ins), list(in_specs), list(out_shape), list(out_specs), list(scratch)
    ni, no, ns = len(ins), len(out_shape), len(scratch)
    ex = exchange
    if ex is not None:
        nxi, nxo = len(ex.ins), len(ex.outs)

    def call_body(*refs):
        if ex is None:
            body(*refs)
            return
        a, xa = refs[:ni], refs[ni:ni + nxi]
        o, xo = refs[ni + nxi:ni + nxi + no], refs[ni + nxi + no:ni + nxi + no + nxo]
        s, xs = refs[ni + nxi + no + nxo:ni + nxi + no + nxo + ns], refs[ni + nxi + no + nxo + ns:]
        if grid:
            first = functools.reduce(jnp.logical_and, [pl.program_id(d) == 0 for d in range(len(grid))])
            last = functools.reduce(jnp.logical_and, [pl.program_id(d) == grid[d] - 1 for d in range(len(grid))])
            pl.when(first)(lambda: ex.start(xa, xo, xs))
            body(*a, *o, *s)
            pl.when(last)(lambda: ex.finish(xa, xo, xs))
        else:
            ex.start(xa, xo, xs)
            if body is not None:
                body(*a, *o, *s)
            ex.finish(xa, xo, xs)

    if ex is not None:
        ins, in_specs = ins + list(ex.ins), in_specs + [ANY] * nxi
        out_shape, out_specs = out_shape + list(ex.outs), out_specs + [ANY] * nxo
        scratch = scratch + list(ex.sems)
    kw = dict(grid=grid) if grid else {}
    res = pl.pallas_call(
        call_body, name=name, out_shape=out_shape, in_specs=in_specs, out_specs=out_specs, scratch_shapes=scratch,
        compiler_params=pltpu.CompilerParams(dimension_semantics=("arbitrary",) * len(grid) if grid else None,
                                             vmem_limit_bytes=vmem_mib * VMEM_MIB), **kw)(*ins)
    res = list(res)
    return (res[:no], res[no:]) if ex is not None else res


def _remote(src, dst, ssem, rsem, dev):
    return pltpu.make_async_remote_copy(src_ref=src, dst_ref=dst, send_sem=ssem, recv_sem=rsem,
                                        device_id=dev, device_id_type=MESH)


def gather_exchange(shards, split):
    n = len(shards)

    def rows(ref, e, kk, half=None):
        R = shards[e].shape[0]
        if half is None:
            return ref.at[pl.ds(pl.multiple_of(kk * R, 8), R)]
        return ref.at[pl.ds(pl.multiple_of(kk * R + half * (R // 2), 8), R // 2)]

    def ici(src, dst, sm, e, j, chip_j, x, y, c):
        k = 2 * x + y
        if split[e]:
            s = src[e].at[pl.ds(pl.multiple_of(c * (shards[e].shape[0] // 2), 8), shards[e].shape[0] // 2)]
            return _remote(s, rows(dst[e], e, k, c), sm[0].at[6 * e + j], sm[1].at[6 * e + j], (*chip_j, c))
        return _remote(src[e], rows(dst[e], e, k), sm[0].at[6 * e + j], sm[1].at[6 * e + j], (*chip_j, c))

    def landed(dst, e, chip_j, c):
        kj = 2 * chip_j[0] + chip_j[1]
        return rows(dst[e], e, kj, c) if split[e] else rows(dst[e], e, kj)

    def forward(dst, sm, e, j, chip_j, x, y, c, sender_c):
        kj = 2 * chip_j[0] + chip_j[1]
        r = rows(dst[e], e, kj, sender_c)
        return _remote(r, r, sm[0].at[6 * e + 3 + j], sm[1].at[6 * e + 3 + j], (x, y, 1 - c))

    def local(src, dst, sm, e, x, y):
        return pltpu.make_async_copy(src[e], rows(dst[e], e, 2 * x + y), sm[2].at[e])

    def start(src, dst, sm):
        x, y, c = _place()
        for e in range(n):
            local(src, dst, sm, e, x, y).start()
            for j, chip_j in enumerate(_other_chips(x, y)):
                ici(src, dst, sm, e, j, chip_j, x, y, c).start()

    def finish(src, dst, sm):
        x, y, c = _place()
        chips = _other_chips(x, y)
        for e in range(n):
            for j, chip_j in enumerate(chips):
                r = landed(dst, e, chip_j, c)
                _remote(r, r, sm[0].at[6 * e + j], sm[1].at[6 * e + j], (*chip_j, c)).wait_recv()
                if split[e]:
                    forward(dst, sm, e, j, chip_j, x, y, c, c).start()
        for e in range(n):
            for j, chip_j in enumerate(chips):
                if split[e]:
                    forward(dst, sm, e, j, chip_j, x, y, c, 1 - c).wait_recv()
        for e in range(n):
            for j, chip_j in enumerate(chips):
                ici(src, dst, sm, e, j, chip_j, x, y, c).wait_send()
                if split[e]:
                    forward(dst, sm, e, j, chip_j, x, y, c, c).wait_send()
            local(src, dst, sm, e, x, y).wait()

    outs = [SDS((4 * s.shape[0], s.shape[1]), s.dtype) for s in shards]
    return Exchange(list(shards), outs, [DMA((6 * n,)), DMA((6 * n,)), DMA((n,))], start, finish)


def halves_exchange(grads):
    n = len(grads)

    def copy(g, st, sm, e, x, y, c):
        return _remote(g[e].at[:, 1 - c], st[e], sm[0].at[e], sm[1].at[e], (x, y, 1 - c))

    def start(g, st, sm):
        x, y, c = _place()
        for e in range(n):
            copy(g, st, sm, e, x, y, c).start()

    def finish(g, st, sm):
        x, y, c = _place()
        for e in range(n):
            copy(g, st, sm, e, x, y, c).wait()

    outs = [SDS((4,) + a.shape[2:], a.dtype) for a in grads]
    return Exchange(list(grads), outs, [DMA((n,)), DMA((n,))], start, finish)


def scatter_exchange(parts):
    n = len(parts)

    def ici(p, st, sm, e, j, chip_j, x, y, c):
        k, kj = 2 * x + y, 2 * chip_j[0] + chip_j[1]
        return _remote(p[e].at[kj], st[e].at[c, k], sm[0].at[8 * e + j], sm[1].at[8 * e + j], (*chip_j, c))

    def own(p, st, sm, e, x, y, c):
        k = 2 * x + y
        return _remote(p[e].at[k], st[e].at[c, k], sm[0].at[8 * e + 3], sm[1].at[8 * e + 3], (x, y, 1 - c))

    def forward(st, sm, e, j, chip_j, x, y, c, sender_c):
        kj = 2 * chip_j[0] + chip_j[1]
        r = st[e].at[sender_c, kj]
        return _remote(r, r, sm[0].at[8 * e + 4 + j], sm[1].at[8 * e + 4 + j], (x, y, 1 - c))

    def local(p, st, sm, e, x, y, c):
        k = 2 * x + y
        return pltpu.make_async_copy(p[e].at[k], st[e].at[c, k], sm[2].at[e])

    def start(p, st, sm):
        x, y, c = _place()
        for e in range(n):
            local(p, st, sm, e, x, y, c).start()
            own(p, st, sm, e, x, y, c).start()
            for j, chip_j in enumerate(_other_chips(x, y)):
                ici(p, st, sm, e, j, chip_j, x, y, c).start()

    def finish(p, st, sm):
        x, y, c = _place()
        k = 2 * x + y
        chips = _other_chips(x, y)
        for e in range(n):
            for j, chip_j in enumerate(chips):
                kj = 2 * chip_j[0] + chip_j[1]
                r = st[e].at[c, kj]
                _remote(r, r, sm[0].at[8 * e + j], sm[1].at[8 * e + j], (*chip_j, c)).wait_recv()
                forward(st, sm, e, j, chip_j, x, y, c, c).start()
        for e in range(n):
            r = st[e].at[1 - c, k]
            _remote(r, r, sm[0].at[8 * e + 3], sm[1].at[8 * e + 3], (x, y, 1 - c)).wait_recv()
            for j, chip_j in enumerate(chips):
                forward(st, sm, e, j, chip_j, x, y, c, 1 - c).wait_recv()
        for e in range(n):
            own(p, st, sm, e, x, y, c).wait_send()
            for j, chip_j in enumerate(chips):
                ici(p, st, sm, e, j, chip_j, x, y, c).wait_send()
                forward(st, sm, e, j, chip_j, x, y, c, c).wait_send()
            local(p, st, sm, e, x, y, c).wait()

    outs = [SDS((2,) + a.shape, a.dtype) for a in parts]
    return Exchange(list(parts), outs, [DMA((8 * n,)), DMA((8 * n,)), DMA((n,))], start, finish)


def allreduce_small(d_norm_mix, d_norm_mem, d_norm_ffn, d_gains, d_cw8, d_cbias, d_qg, d_kg, d_mqg, d_mkg, d_sink8, loss8,
                    exchange):
    def body(nm_ref, nmem_ref, nf_ref, gn_ref, cw_ref, cb_ref, qg_ref, kg_ref, mqg_ref, mkg_ref, sk_ref, ls_ref,
             o_ref, buf, ssem, rsem):
        x, y, c = _place()
        me = 4 * x + 2 * y + c
        mine = buf.at[me]
        mine[...] = jnp.zeros((8, 1024), f32)
        mine[0:1, :] = nm_ref[...]
        mine[1:2, :] = nmem_ref[...]
        mine[2:3, :] = nf_ref[...]
        mine[3:4, :] = gn_ref[...]
        for j in range(3):
            mine[4:5, pl.ds(j * CONV_W, CONV_W)] = cw_ref[j:j + 1, :]
        mine[4:5, pl.ds(3 * CONV_W, CONV_W)] = cb_ref[...]
        for j, r in enumerate((qg_ref, kg_ref, mqg_ref, mkg_ref)):
            mine[5:6, pl.ds(j * HD, HD)] = r[...]
        mine[5:6, pl.ds(256, 128)] = sk_ref[0:1, :]
        mine[5:6, pl.ds(384, 128)] = ls_ref[0:1, :]

        def peer_of(m):
            return (1 - x if m & 4 else x, 1 - y if m & 2 else y, 1 - c if m & 1 else c)

        for m in range(1, 8):
            _remote(mine, mine, ssem.at[m - 1], rsem.at[m - 1], peer_of(m)).start()
        for m in range(1, 8):
            p = peer_of(m)
            got = buf.at[4 * p[0] + 2 * p[1] + p[2]]
            _remote(got, got, ssem.at[m - 1], rsem.at[m - 1], p).wait_recv()
        for m in range(1, 8):
            _remote(mine, mine, ssem.at[m - 1], rsem.at[m - 1], peer_of(m)).wait_send()
        acc = buf[0]
        for d in range(1, 8):
            acc = acc + buf[d]
        o_ref[...] = acc

    ins = [d_norm_mix, d_norm_mem, d_norm_ffn, d_gains, d_cw8, d_cbias, d_qg, d_kg, d_mqg, d_mkg, d_sink8, loss8]
    (tot,), sent = _run("allreduce_small", body, (), ins, [VM] * len(ins), [SDS((8, 1024), f32)], [VM],
                        scratch=[pltpu.VMEM((8, 8, 1024), f32), DMA((7,)), DMA((7,))], exchange=exchange)
    return tot, sent


def add_halves(cidx, grads, stages, name, nch=2):
    n = len(grads)

    def body(c_ref, *refs):
        g, st, o = refs[:n], refs[n:2 * n], refs[2 * n:]
        for e in range(n):
            o[e][...] = (g[e][...] + st[e][...]).astype(WIRE)

    in_specs, out_specs, out_shape = [], [], []
    for a in grads:
        hr, C = a.shape[2], a.shape[3]
        in_specs.append(pl.BlockSpec((None, None, hr // nch, C), lambda s, q, c_ref: (s, c_ref[0], q, 0)))
    for a in stages:
        hr, C = a.shape[1], a.shape[2]
        in_specs.append(pl.BlockSpec((None, hr // nch, C), lambda s, q, c_ref: (s, q, 0)))
        out_specs.append(pl.BlockSpec((None, hr // nch, C), lambda s, q, c_ref: (s, q, 0)))
        out_shape.append(SDS(a.shape, WIRE))
    return pl.pallas_call(
        body, name=name, out_shape=out_shape,
        grid_spec=pltpu.PrefetchScalarGridSpec(num_scalar_prefetch=1, grid=(4, nch), in_specs=in_specs, out_specs=out_specs),
        compiler_params=pltpu.CompilerParams(dimension_semantics=("arbitrary", "arbitrary")),
    )(cidx, *grads, *stages)


def _adamw_math(w, g, m, v):
    m = ADAM_B1 * m + (1.0 - ADAM_B1) * g
    v = ADAM_B2 * v + (1.0 - ADAM_B2) * (g * g)
    m_hat = m / (1.0 - ADAM_B1 ** ADAM_STEP)
    v_hat = v / (1.0 - ADAM_B2 ** ADAM_STEP)
    delta = -ADAM_LR * (m_hat / (jnp.sqrt(v_hat) + ADAM_EPS) + ADAM_WD * w)
    return delta, m, v


def _sum_chips(st):
    return ((st[0].astype(f32) + st[1].astype(f32)) + st[2].astype(f32)) + st[3].astype(f32)


def adamw_big(name, stages, ws, ms, vs, nstep, exchange=None):
    n = len(stages)

    def body(*refs):
        st, w, m, v = refs[:n], refs[n:2 * n], refs[2 * n:3 * n], refs[3 * n:4 * n]
        outs = refs[4 * n:]
        for e in range(n):
            g = jnp.concatenate([_sum_chips(st[e].at[0]), _sum_chips(st[e].at[1])], axis=0)
            d, mm, vv = _adamw_math(w[e][...], g, m[e][...], v[e][...])
            outs[4 * e][...] = g
            outs[4 * e + 1][...] = d
            outs[4 * e + 2][...] = mm
            outs[4 * e + 3][...] = vv

    st_specs, w_specs = [], []
    for e in range(n):
        _, _, hr, C = stages[e].shape
        st_specs.append(pl.BlockSpec((2, 4, hr, C // nstep), lambda i: (0, 0, 0, i)))
        w_specs.append(pl.BlockSpec((2 * hr, C // nstep), lambda i: (0, i)))
    out_specs = [s for s in w_specs for _ in range(4)]
    out_shape = [SDS(w.shape, f32) for w in ws for _ in range(4)]
    res = _run(name, body, (nstep,), list(stages) + list(ws) + list(ms) + list(vs), st_specs + w_specs * 3,
               out_shape, out_specs, vmem_mib=48, exchange=exchange)
    res, sent = res if exchange is not None else (res, None)
    return [res[4 * e:4 * e + 4] for e in range(n)], sent


def adamw_small(tot, pk_w, pk_m, pk_v, shapes):
    def body(tot_ref, w_ref, m_ref, v_ref, *outs):
        x, y, _ = _place()
        chip = 2 * x + y
        taps = []
        for j in range(3):
            mine = tot_ref[4:5, j * CONV_W:j * CONV_W + HD]
            for s in range(1, 4):
                mine = jnp.where(chip == s, tot_ref[4:5, j * CONV_W + s * HD:j * CONV_W + (s + 1) * HD], mine)
            taps.append(mine)
        row4 = jnp.concatenate(taps + [jnp.zeros((1, 3 * CONV_W - 3 * HD), f32), tot_ref[4:5, 3 * CONV_W:]], axis=1)
        tot_v = tot_ref[...]
        row = lax.broadcasted_iota(jnp.int32, tot_v.shape, 0)
        g = jnp.where(row == 4, jnp.broadcast_to(row4, tot_v.shape), tot_v)
        d, mm, vv = _adamw_math(w_ref[...], g, m_ref[...], v_ref[...])
        for i, name in enumerate(SMALL):
            for k, val in enumerate((g, d, mm, vv)):
                if name == "conv_w":
                    outs[4 * i + k][...] = jnp.concatenate([val[4:5, j * HD:(j + 1) * HD] for j in range(3)], axis=0)[None]
                else:
                    r, c0, w = SMALL_AT[name]
                    outs[4 * i + k][...] = val[r:r + 1, c0:c0 + w]

    out_shape = [SDS(shapes[k], f32) for k in SMALL for _ in range(4)]
    res = _run("adamw_small", body, (), [tot, pk_w, pk_m, pk_v], [VM] * 4, out_shape, [VM] * len(out_shape))
    return {k: res[4 * i:4 * i + 4] for i, k in enumerate(SMALL)}


def prep_weights(shards):
    n = len(shards)

    def body(*refs):
        for e in range(n):
            refs[n + e][...] = _c(refs[e][...])

    return _run("prep_weights", body, (), shards, [VM] * n, [SDS(a.shape, MXU) for a in shards], [VM] * n, vmem_mib=48)


def mem_kv_fwd(mem2d, pk, wmkv):
    M, D = mem2d.shape

    def body(m_ref, pk_ref, w_ref, mn_ref, kv_ref, km_ref, vm_ref):
        m = m_ref[...]
        mn = _c(m * _rstd(m) * _small(pk_ref, "norm_mem"))
        mn_ref[...] = mn
        kv = _nn(mn, w_ref[...])
        kv_ref[...] = kv
        kk = kv[:, :MEM_W]
        km_ref[...] = _c(kk * _heads_rstd(kk) * _lanes(_small(pk_ref, "mem_k_norm"), MEM_W))
        vm_ref[...] = _c(kv[:, MEM_W:])

    return _run("mem_kv_fwd", body, (), [mem2d, pk, wmkv], [VM] * 3,
                [SDS((M, D), MXU), SDS((M, 2 * MEM_W), f32), SDS((M, MEM_W), MXU), SDS((M, MEM_W), MXU)], [VM] * 4)


QKV_W = ATT_W + 2 * KV_W + MEM_W


def in_proj_fwd(x2d, pk, winT, tm, exchange):
    T, D = x2d.shape
    P = winT.shape[0]

    def body(x_ref, pk_ref, w_ref, xn_ref, proj_ref, qkv_ref):
        xv = x_ref[...]
        xn = _c(xv * _rstd(xv) * _small(pk_ref, "norm_mix"))
        xn_ref[...] = xn
        proj = _nt(xn, w_ref[...])
        proj_ref[...] = proj
        q, k = proj[:, :ATT_W], proj[:, ATT_W:ATT_W + KV_W]
        qm = proj[:, P - MEM_W:]
        qkv_ref[...] = jnp.concatenate(
            [_c(q * _heads_rstd(q) * _lanes(_small(pk_ref, "q_norm"), ATT_W)),
             _c(k * _heads_rstd(k) * _lanes(_small(pk_ref, "k_norm"), KV_W)),
             _c(proj[:, ATT_W + KV_W:ATT_W + 2 * KV_W]),
             _c(qm * _heads_rstd(qm) * _lanes(_small(pk_ref, "mem_q_norm"), MEM_W))], axis=1)

    return _run("in_proj_fwd", body, (T // tm,), [x2d, pk, winT],
                [pl.BlockSpec((tm, D), lambda i: (i, 0)), VM, VM],
                [SDS((T, D), MXU), SDS((T, P), f32), SDS((T, QKV_W), MXU)],
                [pl.BlockSpec((tm, D), lambda i: (i, 0)), pl.BlockSpec((tm, P), lambda i: (i, 0)),
                 pl.BlockSpec((tm, QKV_W), lambda i: (i, 0))],
                vmem_mib=40, exchange=exchange)


def _swa_bias_table():
    r = np.arange(GQA * BLK)[:, None]
    k = np.arange(2 * BLK)[None, :]
    dist = (r % BLK) + BLK - k
    band = (dist >= 0) & (dist < BLK)
    tab = np.empty((2, N_KV, GQA * BLK, 2 * BLK), np.float32)
    for later in range(2):
        valid = band & ((k >= BLK) | (later == 1))
        for g in range(N_KV):
            slope = 2.0 ** -(g * GQA + r // BLK + 1.0)
            tab[later, g] = np.where(valid, -slope * dist, NEG)
    return jnp.asarray(tab)


def _sink_column(g, sk_ref):
    hrow = lax.broadcasted_iota(jnp.int32, (GQA * BLK, 1), 0) // BLK
    sink = jnp.zeros((GQA * BLK, 1), f32)
    for hh in range(GQA):
        sink = jnp.where(hrow == hh, sk_ref[g * GQA + hh:g * GQA + hh + 1, 0:1], sink)
    return sink


def _stack_heads(v, g):
    return jnp.concatenate([v[:, (g * GQA + hh) * HD:(g * GQA + hh + 1) * HD] for hh in range(GQA)], axis=0)


def attn_fwd(qkv, sink_rows, BL, S, exchange):
    NB = S // BLK
    T = BL * S

    def body(q_ref, kc_ref, kp_ref, vc_ref, vp_ref, sk_ref, tab_ref, o_ref):
        q = q_ref[...]
        k2 = jnp.concatenate([kp_ref[...], kc_ref[...]], axis=0)
        v2 = jnp.concatenate([vp_ref[...], vc_ref[...]], axis=0)
        ones = jnp.ones((2 * BLK, HD), MXU)
        for g in range(N_KV):
            kn, vh = k2[:, g * HD:(g + 1) * HD], v2[:, g * HD:(g + 1) * HD]
            s = _nt(_stack_heads(q, g), kn) * (HD ** -0.5) + tab_ref[g]
            e, es = _exp_scores(s, _sink_column(g, sk_ref))
            eb = _c(e)
            o = _nn(eb, vh) * (1.0 / (_nn(eb, ones) + es))
            for hh in range(GQA):
                o_ref[:, pl.ds((g * GQA + hh) * HD, HD)] = o[hh * BLK:(hh + 1) * BLK]

    cur = lambda col: (lambda b, j: (b * NB + j, col))
    prev = lambda col: (lambda b, j: (b * NB + jnp.maximum(j - 1, 0), col))
    return _run("attn_fwd", body, (BL, NB), [qkv, qkv, qkv, qkv, qkv, sink_rows, _swa_bias_table()],
                [pl.BlockSpec((BLK, ATT_W), cur(0)),
                 pl.BlockSpec((BLK, KV_W), cur(4)), pl.BlockSpec((BLK, KV_W), prev(4)),
                 pl.BlockSpec((BLK, KV_W), cur(5)), pl.BlockSpec((BLK, KV_W), prev(5)),
                 pl.BlockSpec((8, 128), lambda b, j: (0, 0)),
                 pl.BlockSpec((None, N_KV, GQA * BLK, 2 * BLK), lambda b, j: (jnp.minimum(j, 1), 0, 0, 0))],
                [SDS((T, ATT_W), f32)], [pl.BlockSpec((BLK, ATT_W), cur(0))], exchange=exchange)


def _conv_taps(u, uh):
    row = lax.broadcasted_iota(jnp.int32, u.shape, 0)
    u1 = jnp.where(row == 0, uh[7:8, :], pltpu.roll(u, 1, 0))
    u2 = jnp.where(row == 0, uh[6:7, :], jnp.where(row == 1, uh[7:8, :], pltpu.roll(u, 2, 0)))
    return u1, u2


def _mem_head(qm, km, vm, h):
    qh, kh, vh = (a[:, h * HD:(h + 1) * HD] for a in (qm, km, vm))
    e, _ = _exp_scores(_nt(qh, kh) * (HD ** -0.5))
    return qh, kh, vh, e


def mixer_tail_fwd(x2d, attn_out, proj, qkv, km, vm, conv_w8, pk, wout, S, tm, exchange):
    T, D = x2d.shape
    NM = km.shape[0] // (T // S)

    def body(x_ref, ao_ref, ch_ref, cb_ref, cc_ref, chh_ref, cch_ref, qm_ref, km_ref, vm_ref, cw_ref, pk_ref,
             wout_ref, co_ref, mo_ref, mg_ref, x1_ref, h_ref):
        first = (pl.program_id(0) * tm) % S == 0
        u = cc_ref[...] * ch_ref[...]
        uh = jnp.where(first, 0.0, cch_ref[...] * chh_ref[...])
        u1, u2 = _conv_taps(u, uh)
        conv = cw_ref[0:1, :] * u2 + cw_ref[1:2, :] * u1 + cw_ref[2:3, :] * u + _small(pk_ref, "conv_b")
        conv_out = cb_ref[...] * conv
        co_ref[...] = conv_out
        qm, kmv, vmv = qm_ref[...], km_ref[...], vm_ref[...]
        ones = jnp.ones((NM, HD), MXU)
        for h in range(N_MEMH):
            _, _, vh, e = _mem_head(qm, kmv, vmv, h)
            eb = _c(e)
            mo_ref[:, pl.ds(h * HD, HD)] = _nn(eb, vh) * (1.0 / _nn(eb, ones))
        mem_out = mo_ref[...]
        ao = ao_ref[...]
        merged = _c(jnp.concatenate([ao * _rstd(ao) * _small(pk_ref, "out_norm_attn"),
                                     conv_out * _rstd(conv_out) * _small(pk_ref, "out_norm_conv"),
                                     mem_out * _rstd(mem_out) * _small(pk_ref, "out_norm_mem")], axis=1))
        mg_ref[...] = merged
        x1 = x_ref[...] + _nn(merged, wout_ref[...])
        x1_ref[...] = x1
        h_ref[...] = _c(x1 * _rstd(x1) * _small(pk_ref, "norm_ffn"))

    tile = lambda w, col: pl.BlockSpec((tm, w), lambda i: (i, col))
    halo = lambda col: pl.BlockSpec((8, CONV_W), lambda i: (jnp.maximum(i * (tm // 8) - 1, 0), col))
    seq = pl.BlockSpec((NM, MEM_W), lambda i: ((i * tm) // S, 0))
    small = lambda a: pl.BlockSpec(a.shape, lambda i: (0, 0))
    return _run("mixer_tail_fwd", body, (T // tm,),
                [x2d, attn_out, proj, proj, proj, proj, proj, qkv, km, vm, conv_w8, pk, wout],
                [tile(D, 0), tile(ATT_W, 0), tile(CONV_W, 3), tile(CONV_W, 4), tile(CONV_W, 5), halo(3), halo(5),
                 tile(MEM_W, 3), seq, seq, VM, VM, VM],
                [SDS((T, CONV_W), f32), SDS((T, MEM_W), f32), SDS((T, D), MXU), SDS((T, D), f32), SDS((T, D), MXU)],
                [tile(CONV_W, 0), tile(MEM_W, 0), tile(D, 0), tile(D, 0), tile(D, 0)], vmem_mib=40, exchange=exchange)


def ffn_fwd_bwd(h, x1, tgt, wgT, wuT, wd, pk, tm):
    T, D = x1.shape
    F = wd.shape[0]

    def body(h_ref, x1_ref, t_ref, wg_ref, wu_ref, wd_ref, pk_ref,
             dx1_ref, dx2_ref, act_ref, dg_ref, du_ref, loss_ref, dgf_ref):
        @pl.when(pl.program_id(0) == 0)
        def _():
            loss_ref[...] = jnp.zeros_like(loss_ref)
            dgf_ref[...] = jnp.zeros_like(dgf_ref)

        hv = h_ref[...]
        gate = _nt(hv, wg_ref[...])
        up = _nt(hv, wu_ref[...])
        sg = jax.nn.sigmoid(gate)
        sl = gate * sg
        act = _c(sl * up)
        act_ref[...] = act
        x1v = x1_ref[...]
        diff = (x1v + _nn(act, wd_ref[...])) - t_ref[...]
        loss_ref[...] += 0.5 * jnp.sum(jnp.sum(diff * diff, axis=-1, keepdims=True) / D, axis=0, keepdims=True)
        dx2 = diff / D
        dx2b = _c(dx2)
        dx2_ref[...] = dx2b
        d_act = _nt(dx2b, wd_ref[...])
        d_up = _c(d_act * sl)
        d_gate = _c(d_act * up * (sg * (1.0 + gate * (1.0 - sg))))
        du_ref[...] = d_up
        dg_ref[...] = d_gate
        dh = _nn(d_gate, wg_ref[...]) + _nn(d_up, wu_ref[...])
        dv, dgf = _norm_bwd(dh, x1v, _rstd(x1v), _small(pk_ref, "norm_ffn"))
        dx1_ref[...] = dx2 + dv
        dgf_ref[...] += dgf

    tile = lambda w: pl.BlockSpec((tm, w), lambda i: (i, 0))
    return _run("ffn_fwd_bwd", body, (T // tm,), [h, x1, tgt, wgT, wuT, wd, pk],
                [tile(D), tile(D), tile(D), VM, VM, VM, VM],
                [SDS((T, D), f32), SDS((T, D), MXU), SDS((T, F), MXU), SDS((T, F), MXU), SDS((T, F), MXU),
                 SDS((8, 128), f32), SDS((1, D), f32)],
                [tile(D), tile(D), tile(F), tile(F), tile(F), pl.BlockSpec((8, 128), lambda i: (0, 0)),
                 pl.BlockSpec((1, D), lambda i: (0, 0))], vmem_mib=56)


def matmul_tn(a, b, name, tmo, tk):
    T, M = a.shape
    N = b.shape[1]

    def body(a_ref, b_ref, o_ref):
        @pl.when(pl.program_id(1) == 0)
        def _():
            o_ref[...] = jnp.zeros_like(o_ref)

        o_ref[...] += _tn(a_ref[...], b_ref[...])

    return _run(name, body, (M // tmo, T // tk), [a, b],
                [pl.BlockSpec((tk, tmo), lambda m, k: (k, m)), pl.BlockSpec((tk, N), lambda m, k: (k, 0))],
                [SDS((M, N), f32)], [pl.BlockSpec((tmo, N), lambda m, k: (m, 0))], vmem_mib=48)[0]


def out_proj_bwd(dx1, merged, attn_out, conv_out, mem_out, pk, wout, tm):
    T, D = dx1.shape

    def body(dx1_ref, mg_ref, ao_ref, co_ref, mo_ref, pk_ref, w_ref,
             dao_ref, dco_ref, dmo_ref, dw_ref, dgain_ref):
        @pl.when(pl.program_id(0) == 0)
        def _():
            dw_ref[...] = jnp.zeros_like(dw_ref)
            dgain_ref[...] = jnp.zeros_like(dgain_ref)

        dxb = _c(dx1_ref[...])
        dw_ref[...] += _tn(mg_ref[...], dxb)
        dmg = _nt(dxb, w_ref[...])
        ao, co, mo = ao_ref[...], co_ref[...], mo_ref[...]
        da, ga = _norm_bwd(dmg[:, :ATT_W], ao, _rstd(ao), _small(pk_ref, "out_norm_attn"))
        dc, gc = _norm_bwd(dmg[:, ATT_W:ATT_W + CONV_W], co, _rstd(co), _small(pk_ref, "out_norm_conv"))
        dm, gm = _norm_bwd(dmg[:, ATT_W + CONV_W:], mo, _rstd(mo), _small(pk_ref, "out_norm_mem"))
        dao_ref[...] = da
        dco_ref[...] = dc
        dmo_ref[...] = dm
        dgain_ref[...] += jnp.concatenate([ga, gc, gm], axis=1)

    tile = lambda w: pl.BlockSpec((tm, w), lambda i: (i, 0))
    return _run("out_proj_bwd", body, (T // tm,), [dx1, merged, attn_out, conv_out, mem_out, pk, wout],
                [tile(D), tile(D), tile(ATT_W), tile(CONV_W), tile(MEM_W), VM, VM],
                [SDS((T, ATT_W), f32), SDS((T, CONV_W), f32), SDS((T, MEM_W), f32), SDS((D, D), f32), SDS((1, D), f32)],
                [tile(ATT_W), tile(CONV_W), tile(MEM_W), pl.BlockSpec((D, D), lambda i: (0, 0)),
                 pl.BlockSpec((1, D), lambda i: (0, 0))], vmem_mib=40)


def attn_bwd(qkv, d_attn, attn_out, sink_rows, BL, S, exchange):
    NB = S // BLK
    T = BL * S

    def body(q_ref, kc_ref, kp_ref, vc_ref, vp_ref, do_ref, ao_ref, sk_ref, tab_ref,
             dq_ref, dk_ref, dv_ref, dsk_ref, pend_k, pend_v):
        b, j = pl.program_id(0), pl.program_id(1)

        @pl.when((b == 0) & (j == 0))
        def _():
            dsk_ref[...] = jnp.zeros_like(dsk_ref)

        @pl.when(j == 0)
        def _():
            pend_k[...] = jnp.zeros_like(pend_k)
            pend_v[...] = jnp.zeros_like(pend_v)

        @pl.when(j < NB)
        def _():
            q, do, ao = q_ref[...], do_ref[...], ao_ref[...]
            k2 = jnp.concatenate([kp_ref[...], kc_ref[...]], axis=0)
            v2 = jnp.concatenate([vp_ref[...], vc_ref[...]], axis=0)
            lane = lax.broadcasted_iota(jnp.int32, (8, 128), 1)
            ones_w = jnp.ones((2 * BLK, 2 * BLK), MXU)
            dsk = jnp.zeros((8, 128), f32)
            dks, dvs = [], []
            for g in range(N_KV):
                kn, vh = k2[:, g * HD:(g + 1) * HD], v2[:, g * HD:(g + 1) * HD]
                qs = _stack_heads(q, g)
                s = _nt(qs, kn) * (HD ** -0.5) + tab_ref[g]
                e, es = _exp_scores(s, _sink_column(g, sk_ref))
                eb = _c(e)
                inv_w = 1.0 / (_nn(eb, ones_w) + es)
                inv_n = inv_w[:, :HD]
                dos = _stack_heads(do, g)
                delta = _rowsum_mxu(dos * _stack_heads(ao, g), 2 * BLK)
                dp = _nt(_c(dos), vh)
                ds = _c(e * inv_w * (dp - delta) * (HD ** -0.5))
                t = es * inv_n[:, 0:1] * delta[:, 0:1]
                for hh in range(GQA):
                    dsk = dsk + jnp.where(lane == g * GQA + hh, -jnp.sum(t[hh * BLK:(hh + 1) * BLK]), 0.0)
                dvs.append(_tn(eb, _c(dos * inv_n)))
                dks.append(_tn(ds, qs))
                dqs = _nn(ds, kn)
                for hh in range(GQA):
                    dq_ref[:, pl.ds((g * GQA + hh) * HD, HD)] = dqs[hh * BLK:(hh + 1) * BLK]
            dk2 = jnp.concatenate(dks, axis=1)
            dv2 = jnp.concatenate(dvs, axis=1)
            dk_ref[...] = pend_k[...] + dk2[:BLK]
            dv_ref[...] = pend_v[...] + dv2[:BLK]
            pend_k[...] = dk2[BLK:]
            pend_v[...] = dv2[BLK:]
            dsk_ref[...] += dsk

        @pl.when(j == NB)
        def _():
            dk_ref[...] = pend_k[...]
            dv_ref[...] = pend_v[...]

    cur = lambda col: (lambda b, j: (b * NB + jnp.minimum(j, NB - 1), col))
    prev = lambda col: (lambda b, j: (b * NB + jnp.maximum(j - 1, 0), col))
    small = lambda shape: pl.BlockSpec(shape, lambda b, j: (0, 0))
    return _run("attn_bwd", body, (BL, NB + 1), [qkv, qkv, qkv, qkv, qkv, d_attn, attn_out, sink_rows, _swa_bias_table()],
                [pl.BlockSpec((BLK, ATT_W), cur(0)),
                 pl.BlockSpec((BLK, KV_W), cur(4)), pl.BlockSpec((BLK, KV_W), prev(4)),
                 pl.BlockSpec((BLK, KV_W), cur(5)), pl.BlockSpec((BLK, KV_W), prev(5)),
                 pl.BlockSpec((BLK, ATT_W), cur(0)), pl.BlockSpec((BLK, ATT_W), cur(0)), small((8, 128)),
                 pl.BlockSpec((None, N_KV, GQA * BLK, 2 * BLK), lambda b, j: (jnp.minimum(j, 1), 0, 0, 0))],
                [SDS((T, ATT_W), f32), SDS((T, KV_W), f32), SDS((T, KV_W), f32), SDS((8, 128), f32)],
                [pl.BlockSpec((BLK, ATT_W), cur(0)), pl.BlockSpec((BLK, KV_W), prev(0)),
                 pl.BlockSpec((BLK, KV_W), prev(0)), small((8, 128))],
                scratch=[pltpu.VMEM((BLK, KV_W), f32)] * 2, exchange=exchange)


def mem_conv_bwd(d_mem_out, mem_out, d_conv_out, proj, qkv, km, vm, conv_w8, pk, S, tm, exchange):
    T = d_mem_out.shape[0]
    NM = km.shape[0] // (T // S)

    def body(dmo_ref, mo_ref, dco_ref, ch_ref, cb_ref, cc_ref, chh_ref, cch_ref, qm_ref, km_ref, vm_ref, cw_ref,
             pk_ref, dqm_ref, dkm_ref, dvm_ref, dcb_ref, dcv_ref, dcw_ref, dcbias_ref):
        i = pl.program_id(0)
        first = (i * tm) % S == 0

        @pl.when(i == 0)
        def _():
            dcw_ref[...] = jnp.zeros_like(dcw_ref)
            dcbias_ref[...] = jnp.zeros_like(dcbias_ref)

        @pl.when(first)
        def _():
            dkm_ref[...] = jnp.zeros_like(dkm_ref)
            dvm_ref[...] = jnp.zeros_like(dvm_ref)

        qm, kmv, vmv, dmo, mo = qm_ref[...], km_ref[...], vm_ref[...], dmo_ref[...], mo_ref[...]
        ones_w = jnp.ones((NM, NM), MXU)
        for h in range(N_MEMH):
            qh, kh, vh, e = _mem_head(qm, kmv, vmv, h)
            eb = _c(e)
            doh = dmo[:, h * HD:(h + 1) * HD]
            delta = _rowsum_mxu(doh * mo[:, h * HD:(h + 1) * HD], NM)
            dp = _nt(_c(doh), vh)
            inv_w = 1.0 / _nn(eb, ones_w)
            ds = _c(e * inv_w * (dp - delta) * (HD ** -0.5))
            dvm_ref[:, pl.ds(h * HD, HD)] += _tn(eb, _c(doh * inv_w[:, :HD]))
            dkm_ref[:, pl.ds(h * HD, HD)] += _tn(ds, qh)
            dqm_ref[:, pl.ds(h * HD, HD)] = _nn(ds, kh)

        u = cc_ref[...] * ch_ref[...]
        uh = jnp.where(first, 0.0, cch_ref[...] * chh_ref[...])
        u1, u2 = _conv_taps(u, uh)
        conv = cw_ref[0:1, :] * u2 + cw_ref[1:2, :] * u1 + cw_ref[2:3, :] * u + _small(pk_ref, "conv_b")
        dy = dco_ref[...]
        dcb_ref[...] = dy * conv
        dcv = dy * cb_ref[...]
        dcv_ref[...] = dcv
        dcbias_ref[...] += jnp.sum(dcv, axis=0, keepdims=True)
        dcw_ref[0:1, :] += jnp.sum(dcv * u2, axis=0, keepdims=True)
        dcw_ref[1:2, :] += jnp.sum(dcv * u1, axis=0, keepdims=True)
        dcw_ref[2:3, :] += jnp.sum(dcv * u, axis=0, keepdims=True)

    tile = lambda w, col: pl.BlockSpec((tm, w), lambda i: (i, col))
    halo = lambda col: pl.BlockSpec((8, CONV_W), lambda i: (jnp.maximum(i * (tm // 8) - 1, 0), col))
    seq = pl.BlockSpec((NM, MEM_W), lambda i: ((i * tm) // S, 0))
    const = lambda shape: pl.BlockSpec(shape, lambda i: (0, 0))
    return _run("mem_conv_bwd", body, (T // tm,),
                [d_mem_out, mem_out, d_conv_out, proj, proj, proj, proj, proj, qkv, km, vm, conv_w8, pk],
                [tile(MEM_W, 0), tile(MEM_W, 0), tile(CONV_W, 0), tile(CONV_W, 3), tile(CONV_W, 4), tile(CONV_W, 5),
                 halo(3), halo(5), tile(MEM_W, 3), seq, seq, VM, VM],
                [SDS((T, MEM_W), f32), SDS(km.shape, f32), SDS(km.shape, f32),
                 SDS((T, CONV_W), f32), SDS((T, CONV_W), f32), SDS((8, CONV_W), f32), SDS((1, CONV_W), f32)],
                [tile(MEM_W, 0), seq, seq, tile(CONV_W, 0), tile(CONV_W, 0), const((8, CONV_W)), const((1, CONV_W))],
                exchange=exchange)


def in_proj_bwd(dqn, dkn, dv, dcb, dcv, dqmn, proj, conv_w8, xn, x2d, dx1, pk, winT, S, tm):
    T, D = x2d.shape
    P = winT.shape[0]
    last_blk = T // 8 - 1

    def body(dq_ref, dk_ref, dv_ref, dcb_ref, dcv_ref, dcvn_ref, dqm_ref, qa_ref, ka_ref, ch_ref, cc_ref, qma_ref,
             cw_ref, xn_ref, x_ref, dx1_ref, pk_ref, w_ref,
             dx_ref, dw_ref, dg_ref, dqg_ref, dkg_ref, dmqg_ref):
        i = pl.program_id(0)

        @pl.when(i == 0)
        def _():
            dw_ref[...] = jnp.zeros_like(dw_ref)
            dg_ref[...] = jnp.zeros_like(dg_ref)
            dqg_ref[...] = jnp.zeros_like(dqg_ref)
            dkg_ref[...] = jnp.zeros_like(dkg_ref)
            dmqg_ref[...] = jnp.zeros_like(dmqg_ref)

        dqa, gq = _heads_norm_bwd(dq_ref[...], qa_ref[...], _small(pk_ref, "q_norm"))
        dka, gk = _heads_norm_bwd(dk_ref[...], ka_ref[...], _small(pk_ref, "k_norm"))
        dqma, gmq = _heads_norm_bwd(dqm_ref[...], qma_ref[...], _small(pk_ref, "mem_q_norm"))
        dqg_ref[...] += gq
        dkg_ref[...] += gk
        dmqg_ref[...] += gmq

        last = ((i + 1) * tm) % S == 0
        dcv = dcv_ref[...]
        nxt = jnp.where(last, 0.0, dcvn_ref[...])
        row = lax.broadcasted_iota(jnp.int32, dcv.shape, 0)
        n1 = jnp.where(row == tm - 1, nxt[0:1, :], pltpu.roll(dcv, tm - 1, 0))
        n2 = jnp.where(row == tm - 2, nxt[0:1, :], jnp.where(row == tm - 1, nxt[1:2, :], pltpu.roll(dcv, tm - 2, 0)))
        du = cw_ref[2:3, :] * dcv + cw_ref[1:2, :] * n1 + cw_ref[0:1, :] * n2
        d_proj = jnp.concatenate([_c(dqa), _c(dka), _c(dv_ref[...]), _c(du * cc_ref[...]),
                                  _c(dcb_ref[...]), _c(du * ch_ref[...]), _c(dqma)], axis=1)
        dw_ref[...] += _tn(d_proj, xn_ref[...])
        xv = x_ref[...]
        dv_, dg = _norm_bwd(_nn(d_proj, w_ref[...]), xv, _rstd(xv), _small(pk_ref, "norm_mix"))
        dx_ref[...] = dx1_ref[...] + dv_
        dg_ref[...] += dg

    tile = lambda w, col=0: pl.BlockSpec((tm, w), lambda i: (i, col))
    nhalo = pl.BlockSpec((8, CONV_W), lambda i: (jnp.minimum((i + 1) * (tm // 8), last_blk), 0))
    const = lambda shape: pl.BlockSpec(shape, lambda i: (0, 0))
    return _run("in_proj_bwd", body, (T // tm,),
                [dqn, dkn, dv, dcb, dcv, dcv, dqmn, proj, proj, proj, proj, proj, conv_w8, xn, x2d, dx1, pk, winT],
                [tile(ATT_W), tile(KV_W), tile(KV_W), tile(CONV_W), tile(CONV_W), nhalo, tile(MEM_W),
                 tile(ATT_W, 0), tile(KV_W, 4), tile(CONV_W, 3), tile(CONV_W, 5), tile(MEM_W, 6), VM,
                 tile(D), tile(D), tile(D), VM, VM],
                [SDS((T, D), f32), SDS((P, D), f32), SDS((1, D), f32), SDS((1, HD), f32), SDS((1, HD), f32),
                 SDS((1, HD), f32)],
                [tile(D), pl.BlockSpec((P, D), lambda i: (0, 0)), const((1, D)), const((1, HD)), const((1, HD)),
                 const((1, HD))],
                vmem_mib=48)


def mem_kv_bwd(dkm, dvm, kv, memn, mem2d, pk, wmkv):
    def body(dkm_ref, dvm_ref, kv_ref, mn_ref, m_ref, pk_ref, w_ref, dw_ref, dg_ref, dkg_ref):
        dkk, dkg = _heads_norm_bwd(dkm_ref[...], kv_ref[:, :MEM_W], _small(pk_ref, "mem_k_norm"))
        dkg_ref[...] = dkg
        dkv = _c(jnp.concatenate([dkk, dvm_ref[...]], axis=1))
        dw_ref[...] = _tn(mn_ref[...], dkv)
        mv = m_ref[...]
        dg_ref[...] = jnp.sum(_nt(dkv, w_ref[...]) * mv * _rstd(mv), axis=0, keepdims=True)

    return _run("mem_kv_bwd", body, (), [dkm, dvm, kv, memn, mem2d, pk, wmkv], [VM] * 7,
                [SDS(wmkv.shape, f32), SDS((1, mem2d.shape[1]), f32), SDS((1, HD), f32)], [VM] * 3, vmem_mib=40)


def _halves_view(g):
    return g.reshape(4, 2, g.shape[0] // 8, g.shape[1])


def kernel(x, mem, norm_mix, w_in, q_norm, k_norm, attn_sinks, conv_w, conv_b, norm_mem, w_mem_kv, mem_q_norm, mem_k_norm, out_norm_attn, out_norm_conv, out_norm_mem, w_out, norm_ffn, w_gate, w_up, w_down, loss_target, m_norm_mix, m_w_in, m_q_norm, m_k_norm, m_attn_sinks, m_conv_w, m_conv_b, m_norm_mem, m_w_mem_kv, m_mem_q_norm, m_mem_k_norm, m_out_norm_attn, m_out_norm_conv, m_out_norm_mem, m_w_out, m_norm_ffn, m_w_gate, m_w_up, m_w_down, v_norm_mix, v_w_in, v_q_norm, v_k_norm, v_attn_sinks, v_conv_w, v_conv_b, v_norm_mem, v_w_mem_kv, v_mem_q_norm, v_mem_k_norm, v_out_norm_attn, v_out_norm_conv, v_out_norm_mem, v_w_out, v_norm_ffn, v_w_gate, v_w_up, v_w_down):
    BL, S, D = x.shape
    T = BL * S
    TM = 256
    _, _, ci = _place()
    cidx = ci.reshape(1).astype(jnp.int32)
    w_small = dict(norm_mix=norm_mix, norm_mem=norm_mem, norm_ffn=norm_ffn, out_norm_attn=out_norm_attn,
                   out_norm_conv=out_norm_conv, out_norm_mem=out_norm_mem, conv_w=conv_w, conv_b=conv_b, q_norm=q_norm,
                   k_norm=k_norm, mem_q_norm=mem_q_norm, mem_k_norm=mem_k_norm, attn_sinks=attn_sinks)
    m_small = dict(norm_mix=m_norm_mix, norm_mem=m_norm_mem, norm_ffn=m_norm_ffn, out_norm_attn=m_out_norm_attn,
                   out_norm_conv=m_out_norm_conv, out_norm_mem=m_out_norm_mem, conv_w=m_conv_w, conv_b=m_conv_b,
                   q_norm=m_q_norm, k_norm=m_k_norm, mem_q_norm=m_mem_q_norm, mem_k_norm=m_mem_k_norm,
                   attn_sinks=m_attn_sinks)
    v_small = dict(norm_mix=v_norm_mix, norm_mem=v_norm_mem, norm_ffn=v_norm_ffn, out_norm_attn=v_out_norm_attn,
                   out_norm_conv=v_out_norm_conv, out_norm_mem=v_out_norm_mem, conv_w=v_conv_w, conv_b=v_conv_b,
                   q_norm=v_q_norm, k_norm=v_k_norm, mem_q_norm=v_mem_q_norm, mem_k_norm=v_mem_k_norm,
                   attn_sinks=v_attn_sinks)
    pk = _pack_small(w_small)

    rowblocks = lambda a, b, c, d, e, f: [a[0].T, b[0].T, c[0].T, d[0], e[0], f[0]]
    w_rb = rowblocks(w_in, w_gate, w_up, w_down, w_out, w_mem_kv)
    m_rb = rowblocks(m_w_in, m_w_gate, m_w_up, m_w_down, m_w_out, m_w_mem_kv)
    v_rb = rowblocks(v_w_in, v_w_gate, v_w_up, v_w_down, v_w_out, v_w_mem_kv)
    winT_s, wgT_s, wuT_s, wd_s, wout_s, wmkv_s = prep_weights(w_rb)
    cw_pad = jnp.zeros((8, 128), f32).at[:3, :HD].set(conv_w[0])
    _, (winT, cw_all) = _run("gather_w_in", None, (), [], [], [], [], exchange=gather_exchange([winT_s, cw_pad], [True, False]))
    conv_w_full = jnp.transpose(cw_all.reshape(4, 8, 128)[:, :3, :HD], (1, 0, 2)).reshape(3, CONV_W)
    conv_w8 = jnp.zeros((8, CONV_W), f32).at[:3].set(conv_w_full)
    sink_rows = jnp.broadcast_to(attn_sinks.reshape(N_Q, 1), (N_Q, 128))

    x2d = x.reshape(T, D)
    mem2d = mem.reshape(-1, D)
    (xn, proj, qkv), (wgT,) = in_proj_fwd(x2d, pk, winT, TM, gather_exchange([wgT_s], [True]))
    (attn_out,), (wuT, wout, wmkv) = attn_fwd(qkv, sink_rows, BL, S,
                                              gather_exchange([wuT_s, wout_s, wmkv_s], [True, True, True]))
    memn, kv, km, vm = mem_kv_fwd(mem2d, pk, wmkv)
    (conv_out, mem_out, merged, x1, h), (wd,) = mixer_tail_fwd(
        x2d, attn_out, proj, qkv, km, vm, conv_w8, pk, wout, S, TM, gather_exchange([wd_s], [True]))

    dx1, dx2b, act, d_gate, d_up, loss8, d_norm_ffn = ffn_fwd_bwd(h, x1, loss_target.reshape(T, D), wgT, wuT, wd, pk, TM)
    F = wd.shape[0]
    g_wd = matmul_tn(act, dx2b, "dw_down", F // 2, min(T, 1024))
    g_wgT = matmul_tn(d_gate, h, "dw_gate", F // 2, min(T, 1024))
    g_wuT = matmul_tn(d_up, h, "dw_up", F // 2, min(T, 1024))

    d_attn, d_conv_out, d_mem_out, g_wout, d_gains = out_proj_bwd(dx1, merged, attn_out, conv_out, mem_out, pk, wout, TM)
    late = [_halves_view(g) for g in (g_wgT, g_wuT, g_wd, g_wout)]
    (dqmn, dkm, dvm, dcb, dcv, d_cw8, d_cbias), late_sib = mem_conv_bwd(
        d_mem_out, mem_out, d_conv_out, proj, qkv, km, vm, conv_w8, pk, S, TM, halves_exchange(late))
    late_part = add_halves(cidx, late, late_sib, "grad_add_halves_ffn")
    (dqn, dkn, dv, d_sink8), late_stage = attn_bwd(qkv, d_attn, attn_out, sink_rows, BL, S, scatter_exchange(late_part))
    g_x, g_winT, d_norm_mix, d_qg, d_kg, d_mqg = in_proj_bwd(
        dqn, dkn, dv, dcb, dcv, dqmn, proj, conv_w8, xn, x2d, dx1, pk, winT, S, TM)
    g_wmkv, d_norm_mem, d_mkg = mem_kv_bwd(dkm, dvm, kv, memn, mem2d, pk, wmkv)

    tail = [_halves_view(g) for g in (g_winT, g_wmkv)]
    _, tail_sib = _run("grad_halves_tail", None, (), [], [], [], [], exchange=halves_exchange(tail))
    tail_part = add_halves(cidx, tail, tail_sib, "grad_add_halves_tail")
    tot, tail_stage = allreduce_small(d_norm_mix, d_norm_mem, d_norm_ffn, d_gains, d_cw8, d_cbias, d_qg, d_kg, d_mqg,
                                      d_mkg, d_sink8, loss8, scatter_exchange(tail_part))
    loss = tot[5, 384]
    late_res, _ = adamw_big("adamw_late", late_stage, w_rb[1:5], m_rb[1:5], v_rb[1:5], 8)
    tail_res, _ = adamw_big("adamw_tail", tail_stage, [w_rb[0], w_rb[5]], [m_rb[0], m_rb[5]], [v_rb[0], v_rb[5]], 4)
    res = {"w_in": [a.T[None] for a in tail_res[0]], "w_gate": [a.T[None] for a in late_res[0]],
           "w_up": [a.T[None] for a in late_res[1]], "w_down": [a[None] for a in late_res[2]],
           "w_out": [a[None] for a in late_res[3]], "w_mem_kv": [a[None] for a in tail_res[1]]}
    res.update(adamw_small(tot, pk, _pack_small(m_small), _pack_small(v_small), {k: w_small[k].shape for k in SMALL}))

    order = ["norm_mix", "w_in", "q_norm", "k_norm", "attn_sinks", "conv_w", "conv_b", "norm_mem", "w_mem_kv",
             "mem_q_norm", "mem_k_norm", "out_norm_attn", "out_norm_conv", "out_norm_mem", "w_out", "norm_ffn",
             "w_gate", "w_up", "w_down"]
    return (loss, g_x.reshape(BL, S, D), *[res[n][0] for n in order], *[res[n][1] for n in order],
            *[res[n][2] for n in order], *[res[n][3] for n in order])
```

```python
import collections
import functools

import jax
import jax.numpy as jnp
import numpy as np
from jax import lax
from jax.experimental import pallas as pl
from jax.experimental.pallas import tpu as pltpu

f32 = jnp.float32
MXU = jnp.bfloat16
WIRE = jnp.bfloat16
EPS = 1e-6
NEG = -1e30
HD = 64
BLK = 128
N_Q, N_KV, N_MEMH = 8, 2, 4
GQA = N_Q // N_KV
ATT_W, KV_W, CONV_W, MEM_W = 512, 128, 256, 256
VMEM_MIB = 1024 * 1024
ADAM_LR, ADAM_B1, ADAM_B2, ADAM_EPS, ADAM_WD, ADAM_STEP = 0.001, 0.9, 0.999, 1e-08, 0.01, 10

MESH = pl.DeviceIdType.MESH
VM = pl.BlockSpec(memory_space=pltpu.VMEM)
ANY = pl.BlockSpec(memory_space=pl.ANY)
SDS = jax.ShapeDtypeStruct
DMA = pltpu.SemaphoreType.DMA


def _c(v):
    return v.astype(MXU)


def _nn(a, b):
    return lax.dot_general(a, b, (((1,), (0,)), ((), ())), preferred_element_type=f32)


def _nt(a, b):
    return lax.dot_general(a, b, (((1,), (1,)), ((), ())), preferred_element_type=f32)


def _tn(a, b):
    return lax.dot_general(a, b, (((0,), (0,)), ((), ())), preferred_element_type=f32)


def _rstd(v):
    return lax.rsqrt(jnp.mean(v * v, axis=-1, keepdims=True) + EPS)


def _norm_bwd(dy, v, r, g):
    dyg = dy * g
    dv = r * dyg - v * (r * r * r) * jnp.mean(dyg * v, axis=-1, keepdims=True)
    return dv, jnp.sum(dy * v * r, axis=0, keepdims=True)


def _split3(v):
    hi = _c(v)
    r1 = v - hi.astype(f32)
    mid = _c(r1)
    return hi, mid, _c(r1 - mid.astype(f32))


def _rowsum_mxu(v, width):
    ones = jnp.ones((v.shape[1], width), MXU)
    return sum(_nn(a, ones) for a in _split3(v))


def _seg_sums(v):
    r = lax.broadcasted_iota(jnp.int32, (2 * HD, 2 * HD), 0) // HD
    c = lax.broadcasted_iota(jnp.int32, (2 * HD, 2 * HD), 1) // HD
    bd = (r == c).astype(MXU)
    outs = []
    for b in range(v.shape[1] // (2 * HD)):
        outs.append(sum(_nn(a, bd) for a in _split3(v[:, b * 2 * HD:(b + 1) * 2 * HD])))
    return outs[0] if len(outs) == 1 else jnp.concatenate(outs, axis=1)


def _lanes(g, width):
    return jnp.concatenate([g] * (width // HD), axis=1)


def _heads_rstd(v):
    return lax.rsqrt(_seg_sums(v * v) * (1.0 / HD) + EPS)


def _heads_norm_bwd(dy, v, g):
    r = _heads_rstd(v)
    gl = _lanes(g, v.shape[1])
    dyg = dy * gl
    dv = r * dyg - v * (r * r * r) * (_seg_sums(dyg * v) * (1.0 / HD))
    dgl = jnp.sum(dy * v * r, axis=0, keepdims=True)
    return dv, sum(dgl[:, s * HD:(s + 1) * HD] for s in range(v.shape[1] // HD))


def _exp_scores(s, extra=None):
    m = jnp.max(s, axis=-1, keepdims=True)
    if extra is None:
        return jnp.exp(s - m), None
    m = jnp.maximum(m, extra)
    return jnp.exp(s - m), jnp.exp(extra - m)


def _place():
    return lax.axis_index("x"), lax.axis_index("y"), lax.axis_index("c")


SMALL_AT = {"norm_mix": (0, 0, 1024), "norm_mem": (1, 0, 1024), "norm_ffn": (2, 0, 1024),
            "out_norm_attn": (3, 0, ATT_W), "out_norm_conv": (3, ATT_W, CONV_W), "out_norm_mem": (3, ATT_W + CONV_W, MEM_W),
            "conv_b": (4, 3 * CONV_W, CONV_W), "q_norm": (5, 0, HD), "k_norm": (5, HD, HD), "mem_q_norm": (5, 2 * HD, HD),
            "mem_k_norm": (5, 3 * HD, HD), "attn_sinks": (5, 256, N_Q)}
SMALL = ("norm_mix", "norm_mem", "norm_ffn", "out_norm_attn", "out_norm_conv", "out_norm_mem", "conv_w", "conv_b",
         "q_norm", "k_norm", "mem_q_norm", "mem_k_norm", "attn_sinks")


def _small(pk_ref, name):
    r, c0, w = SMALL_AT[name]
    return pk_ref[r:r + 1, c0:c0 + w]


def _pack_small(d):
    z = lambda n: jnp.zeros((1, n), f32)
    row3 = jnp.concatenate([d["out_norm_attn"], d["out_norm_conv"], d["out_norm_mem"]], axis=1)
    row4 = jnp.concatenate([d["conv_w"].reshape(1, 3 * HD), z(3 * CONV_W - 3 * HD), d["conv_b"]], axis=1)
    row5 = jnp.concatenate([d["q_norm"], d["k_norm"], d["mem_q_norm"], d["mem_k_norm"], d["attn_sinks"],
                            z(1024 - 4 * HD - N_Q)], axis=1)
    return jnp.concatenate([d["norm_mix"], d["norm_mem"], d["norm_ffn"], row3, row4, row5, z(1024), z(1024)], axis=0)


def _other_chips(x, y):
    return [(1 - x, y), (x, 1 - y), (1 - x, 1 - y)]


Exchange = collections.namedtuple("Exchange", "ins outs sems start finish")


def _run(name, body, grid, ins, in_specs, out_shape, out_specs, scratch=(), vmem_mib=32, exchange=None):
    ins, in_specs, out_shape, out_specs, scratch = list(ins), list(in_specs), list(out_shape), list(out_specs), list(scratch)
    ni, no, ns = len(ins), len(out_shape), len(scratch)
    ex = exchange
    if ex is not None:
        nxi, nxo = len(ex.ins), len(ex.outs)

    def call_body(*refs):
        if ex is None:
            body(*refs)
            return
        a, xa = refs[:ni], refs[ni:ni + nxi]
        o, xo = refs[ni + nxi:ni + nxi + no], refs[ni + nxi + no:ni + nxi + no + nxo]
        s, xs = refs[ni + nxi + no + nxo:ni + nxi + no + nxo + ns], refs[ni + nxi + no + nxo + ns:]
        if grid:
            first = functools.reduce(jnp.logical_and, [pl.program_id(d) == 0 for d in range(len(grid))])
            last = functools.reduce(jnp.logical_and, [pl.program_id(d) == grid[d] - 1 for d in range(len(grid))])
            pl.when(first)(lambda: ex.start(xa, xo, xs))
            body(*a, *o, *s)
            pl.when(last)(lambda: ex.finish(xa, xo, xs))
        else:
            ex.start(xa, xo, xs)
            if body is not None:
                body(*a, *o, *s)
            ex.finish(xa, xo, xs)

    if ex is not None:
        ins, in_specs = ins + list(ex.ins), in_specs + [ANY] * nxi
        out_shape, out_specs = out_shape + list(ex.outs), out_specs + [ANY] * nxo
        scratch = scratch + list(ex.sems)
    kw = dict(grid=grid) if grid else {}
    res = pl.pallas_call(
        call_body, name=name, out_shape=out_shape, in_specs=in_specs, out_specs=out_specs, scratch_shapes=scratch,
        compiler_params=pltpu.CompilerParams(dimension_semantics=("arbitrary",) * len(grid) if grid else None,
                                             vmem_limit_bytes=vmem_mib * VMEM_MIB), **kw)(*ins)
    res = list(res)
    return (res[:no], res[no:]) if ex is not None else res


def _remote(src, dst, ssem, rsem, dev):
    return pltpu.make_async_remote_copy(src_ref=src, dst_ref=dst, send_sem=ssem, recv_sem=rsem,
                                        device_id=dev, device_id_type=MESH)


def gather_exchange(shards, split):
    n = len(shards)

    def rows(ref, e, kk, half=None):
        R = shards[e].shape[0]
        if half is None:
            return ref.at[pl.ds(pl.multiple_of(kk * R, 8), R)]
        return ref.at[pl.ds(pl.multiple_of(kk * R + half * (R // 2), 8), R // 2)]

    def ici(src, dst, sm, e, j, chip_j, x, y, c):
        k = 2 * x + y
        if split[e]:
            s = src[e].at[pl.ds(pl.multiple_of(c * (shards[e].shape[0] // 2), 8), shards[e].shape[0] // 2)]
            return _remote(s, rows(dst[e], e, k, c), sm[0].at[6 * e + j], sm[1].at[6 * e + j], (*chip_j, c))
        return _remote(src[e], rows(dst[e], e, k), sm[0].at[6 * e + j], sm[1].at[6 * e + j], (*chip_j, c))

    def landed(dst, e, chip_j, c):
        kj = 2 * chip_j[0] + chip_j[1]
        return rows(dst[e], e, kj, c) if split[e] else rows(dst[e], e, kj)

    def forward(dst, sm, e, j, chip_j, x, y, c, sender_c):
        kj = 2 * chip_j[0] + chip_j[1]
        r = rows(dst[e], e, kj, sender_c)
        return _remote(r, r, sm[0].at[6 * e + 3 + j], sm[1].at[6 * e + 3 + j], (x, y, 1 - c))

    def local(src, dst, sm, e, x, y):
        return pltpu.make_async_copy(src[e], rows(dst[e], e, 2 * x + y), sm[2].at[e])

    def start(src, dst, sm):
        x, y, c = _place()
        for e in range(n):
            local(src, dst, sm, e, x, y).start()
            for j, chip_j in enumerate(_other_chips(x, y)):
                ici(src, dst, sm, e, j, chip_j, x, y, c).start()

    def finish(src, dst, sm):
        x, y, c = _place()
        chips = _other_chips(x, y)
        for e in range(n):
            for j, chip_j in enumerate(chips):
                r = landed(dst, e, chip_j, c)
                _remote(r, r, sm[0].at[6 * e + j], sm[1].at[6 * e + j], (*chip_j, c)).wait_recv()
                if split[e]:
                    forward(dst, sm, e, j, chip_j, x, y, c, c).start()
        for e in range(n):
            for j, chip_j in enumerate(chips):
                if split[e]:
                    forward(dst, sm, e, j, chip_j, x, y, c, 1 - c).wait_recv()
        for e in range(n):
            for j, chip_j in enumerate(chips):
                ici(src, dst, sm, e, j, chip_j, x, y, c).wait_send()
                if split[e]:
                    forward(dst, sm, e, j, chip_j, x, y, c, c).wait_send()
            local(src, dst, sm, e, x, y).wait()

    outs = [SDS((4 * s.shape[0], s.shape[1]), s.dtype) for s in shards]
    return Exchange(list(shards), outs, [DMA((6 * n,)), DMA((6 * n,)), DMA((n,))], start, finish)


def halves_exchange(grads):
    n = len(grads)

    def copy(g, st, sm, e, x, y, c):
        return _remote(g[e].at[:, 1 - c], st[e], sm[0].at[e], sm[1].at[e], (x, y, 1 - c))

    def start(g, st, sm):
        x, y, c = _place()
        for e in range(n):
            copy(g, st, sm, e, x, y, c).start()

    def finish(g, st, sm):
        x, y, c = _place()
        for e in range(n):
            copy(g, st, sm, e, x, y, c).wait()

    outs = [SDS((4,) + a.shape[2:], a.dtype) for a in grads]
    return Exchange(list(grads), outs, [DMA((n,)), DMA((n,))], start, finish)


def scatter_exchange(parts):
    n = len(parts)

    def ici(p, st, sm, e, j, chip_j, x, y, c):
        k, kj = 2 * x + y, 2 * chip_j[0] + chip_j[1]
        return _remote(p[e].at[kj], st[e].at[c, k], sm[0].at[8 * e + j], sm[1].at[8 * e + j], (*chip_j, c))

    def own(p, st, sm, e, x, y, c):
        k = 2 * x + y
        return _remote(p[e].at[k], st[e].at[c, k], sm[0].at[8 * e + 3], sm[1].at[8 * e + 3], (x, y, 1 - c))

    def forward(st, sm, e, j, chip_j, x, y, c, sender_c):
        kj = 2 * chip_j[0] + chip_j[1]
        r = st[e].at[sender_c, kj]
        return _remote(r, r, sm[0].at[8 * e + 4 + j], sm[1].at[8 * e + 4 + j], (x, y, 1 - c))

    def local(p, st, sm, e, x, y, c):
        k = 2 * x + y
        return pltpu.make_async_copy(p[e].at[k], st[e].at[c, k], sm[2].at[e])

    def start(p, st, sm):
        x, y, c = _place()
        for e in range(n):
            local(p, st, sm, e, x, y, c).start()
            own(p, st, sm, e, x, y, c).start()
            for j, chip_j in enumerate(_other_chips(x, y)):
                ici(p, st, sm, e, j, chip_j, x, y, c).start()

    def finish(p, st, sm):
        x, y, c = _place()
        k = 2 * x + y
        chips = _other_chips(x, y)
        for e in range(n):
            for j, chip_j in enumerate(chips):
                kj = 2 * chip_j[0] + chip_j[1]
                r = st[e].at[c, kj]
                _remote(r, r, sm[0].at[8 * e + j], sm[1].at[8 * e + j], (*chip_j, c)).wait_recv()
                forward(st, sm, e, j, chip_j, x, y, c, c).start()
        for e in range(n):
            r = st[e].at[1 - c, k]
            _remote(r, r, sm[0].at[8 * e + 3], sm[1].at[8 * e + 3], (x, y, 1 - c)).wait_recv()
            for j, chip_j in enumerate(chips):
                forward(st, sm, e, j, chip_j, x, y, c, 1 - c).wait_recv()
        for e in range(n):
            own(p, st, sm, e, x, y, c).wait_send()
            for j, chip_j in enumerate(chips):
                ici(p, st, sm, e, j, chip_j, x, y, c).wait_send()
                forward(st, sm, e, j, chip_j, x, y, c, c).wait_send()
            local(p, st, sm, e, x, y, c).wait()

    outs = [SDS((2,) + a.shape, a.dtype) for a in parts]
    return Exchange(list(parts), outs, [DMA((8 * n,)), DMA((8 * n,)), DMA((n,))], start, finish)


def tail_reduce(d_norm_mix, d_norm_mem, d_norm_ffn, d_gains, d_cw8, d_cbias, d_qg, d_kg, d_mqg, d_mkg, d_sink8, loss8, tail):
    n = len(tail)
    halves = halves_exchange(tail)
    scatter = scatter_exchange([SDS((4,) + a.shape[2:], WIRE) for a in tail])

    def body(nm_ref, nmem_ref, nf_ref, gn_ref, cw_ref, cb_ref, qg_ref, kg_ref, mqg_ref, mkg_ref, sk_ref, ls_ref, *rest):
        g, o_ref, st = rest[:n], rest[n], rest[n + 1:2 * n + 1]
        buf, ssem, rsem = rest[2 * n + 1:2 * n + 4]
        own, sib, part = (rest[2 * n + 4 + i * n:2 * n + 4 + (i + 1) * n] for i in range(3))
        lsem = rest[5 * n + 4]
        hsem, xsem = rest[5 * n + 5:5 * n + 7], rest[5 * n + 7:]
        x, y, c = _place()
        loads = [pltpu.make_async_copy(g[e].at[:, c], own[e], lsem.at[e]) for e in range(n)]
        for ld in loads:
            ld.start()
        halves.start(g, sib, hsem)
        me = 4 * x + 2 * y + c
        mine = buf.at[me]
        mine[...] = jnp.zeros((8, 1024), f32)
        mine[0:1, :] = nm_ref[...]
        mine[1:2, :] = nmem_ref[...]
        mine[2:3, :] = nf_ref[...]
        mine[3:4, :] = gn_ref[...]
        for j in range(3):
            mine[4:5, pl.ds(j * CONV_W, CONV_W)] = cw_ref[j:j + 1, :]
        mine[4:5, pl.ds(3 * CONV_W, CONV_W)] = cb_ref[...]
        for j, r in enumerate((qg_ref, kg_ref, mqg_ref, mkg_ref)):
            mine[5:6, pl.ds(j * HD, HD)] = r[...]
        mine[5:6, pl.ds(256, 128)] = sk_ref[0:1, :]
        mine[5:6, pl.ds(384, 128)] = ls_ref[0:1, :]

        def peer_of(m):
            return (1 - x if m & 4 else x, 1 - y if m & 2 else y, 1 - c if m & 1 else c)

        for m in range(1, 8):
            _remote(mine, mine, ssem.at[m - 1], rsem.at[m - 1], peer_of(m)).start()
        for ld in loads:
            ld.wait()
        halves.finish(g, sib, hsem)
        for e in range(n):
            part[e][...] = (own[e][...] + sib[e][...]).astype(WIRE)
        scatter.start(part, st, xsem)
        scatter.finish(part, st, xsem)
        for m in range(1, 8):
            p = peer_of(m)
            got = buf.at[4 * p[0] + 2 * p[1] + p[2]]
            _remote(got, got, ssem.at[m - 1], rsem.at[m - 1], p).wait_recv()
        for m in range(1, 8):
            _remote(mine, mine, ssem.at[m - 1], rsem.at[m - 1], peer_of(m)).wait_send()
        acc = buf[0]
        for d in range(1, 8):
            acc = acc + buf[d]
        o_ref[...] = acc

    ins = [d_norm_mix, d_norm_mem, d_norm_ffn, d_gains, d_cw8, d_cbias, d_qg, d_kg, d_mqg, d_mkg, d_sink8, loss8]
    half_shape = [(4,) + a.shape[2:] for a in tail]
    scratch = ([pltpu.VMEM((8, 8, 1024), f32), DMA((7,)), DMA((7,))]
               + [pltpu.VMEM(s, f32) for s in half_shape] * 2 + [pltpu.VMEM(s, WIRE) for s in half_shape]
               + [DMA((n,))] + list(halves.sems) + list(scatter.sems))
    res = _run("tail_reduce", body, (), ins + list(tail), [VM] * len(ins) + [ANY] * n,
               [SDS((8, 1024), f32)] + list(scatter.outs), [VM] + [ANY] * n, scratch=scratch, vmem_mib=40)
    return res[0], res[1:]


def add_halves(cidx, grads, stages, name, nch=2):
    n = len(grads)

    def body(c_ref, *refs):
        g, st, o = refs[:n], refs[n:2 * n], refs[2 * n:]
        for e in range(n):
            o[e][...] = (g[e][...] + st[e][...]).astype(WIRE)

    in_specs, out_specs, out_shape = [], [], []
    for a in grads:
        hr, C = a.shape[2], a.shape[3]
        in_specs.append(pl.BlockSpec((None, None, hr // nch, C), lambda s, q, c_ref: (s, c_ref[0], q, 0)))
    for a in stages:
        hr, C = a.shape[1], a.shape[2]
        in_specs.append(pl.BlockSpec((None, hr // nch, C), lambda s, q, c_ref: (s, q, 0)))
        out_specs.append(pl.BlockSpec((None, hr // nch, C), lambda s, q, c_ref: (s, q, 0)))
        out_shape.append(SDS(a.shape, WIRE))
    return pl.pallas_call(
        body, name=name, out_shape=out_shape,
        grid_spec=pltpu.PrefetchScalarGridSpec(num_scalar_prefetch=1, grid=(4, nch), in_specs=in_specs, out_specs=out_specs),
        compiler_params=pltpu.CompilerParams(dimension_semantics=("arbitrary", "arbitrary")),
    )(cidx, *grads, *stages)


def _adamw_math(w, g, m, v):
    m = ADAM_B1 * m + (1.0 - ADAM_B1) * g
    v = ADAM_B2 * v + (1.0 - ADAM_B2) * (g * g)
    m_hat = m / (1.0 - ADAM_B1 ** ADAM_STEP)
    v_hat = v / (1.0 - ADAM_B2 ** ADAM_STEP)
    delta = -ADAM_LR * (m_hat / (jnp.sqrt(v_hat) + ADAM_EPS) + ADAM_WD * w)
    return delta, m, v


def _sum_chips(st):
    return ((st[0].astype(f32) + st[1].astype(f32)) + st[2].astype(f32)) + st[3].astype(f32)


def adamw_big(name, stages, ws, ms, vs, nstep, exchange=None):
    n = len(stages)

    def body(*refs):
        st, w, m, v = refs[:n], refs[n:2 * n], refs[2 * n:3 * n], refs[3 * n:4 * n]
        outs = refs[4 * n:]
        for e in range(n):
            g = jnp.concatenate([_sum_chips(st[e].at[0]), _sum_chips(st[e].at[1])], axis=0)
            d, mm, vv = _adamw_math(w[e][...], g, m[e][...], v[e][...])
            outs[4 * e][...] = g
            outs[4 * e + 1][...] = d
            outs[4 * e + 2][...] = mm
            outs[4 * e + 3][...] = vv

    st_specs, w_specs = [], []
    for e in range(n):
        _, _, hr, C = stages[e].shape
        st_specs.append(pl.BlockSpec((2, 4, hr, C // nstep), lambda i: (0, 0, 0, i)))
        w_specs.append(pl.BlockSpec((2 * hr, C // nstep), lambda i: (0, i)))
    out_specs = [s for s in w_specs for _ in range(4)]
    out_shape = [SDS(w.shape, f32) for w in ws for _ in range(4)]
    res = _run(name, body, (nstep,), list(stages) + list(ws) + list(ms) + list(vs), st_specs + w_specs * 3,
               out_shape, out_specs, vmem_mib=48, exchange=exchange)
    res, sent = res if exchange is not None else (res, None)
    return [res[4 * e:4 * e + 4] for e in range(n)], sent


def adamw_small(tot, pk_w, pk_m, pk_v, shapes):
    def body(tot_ref, w_ref, m_ref, v_ref, *outs):
        x, y, _ = _place()
        chip = 2 * x + y
        taps = []
        for j in range(3):
            mine = tot_ref[4:5, j * CONV_W:j * CONV_W + HD]
            for s in range(1, 4):
                mine = jnp.where(chip == s, tot_ref[4:5, j * CONV_W + s * HD:j * CONV_W + (s + 1) * HD], mine)
            taps.append(mine)
        row4 = jnp.concatenate(taps + [jnp.zeros((1, 3 * CONV_W - 3 * HD), f32), tot_ref[4:5, 3 * CONV_W:]], axis=1)
        tot_v = tot_ref[...]
        row = lax.broadcasted_iota(jnp.int32, tot_v.shape, 0)
        g = jnp.where(row == 4, jnp.broadcast_to(row4, tot_v.shape), tot_v)
        d, mm, vv = _adamw_math(w_ref[...], g, m_ref[...], v_ref[...])
        for i, name in enumerate(SMALL):
            for k, val in enumerate((g, d, mm, vv)):
                if name == "conv_w":
                    outs[4 * i + k][...] = jnp.concatenate([val[4:5, j * HD:(j + 1) * HD] for j in range(3)], axis=0)[None]
                else:
                    r, c0, w = SMALL_AT[name]
                    outs[4 * i + k][...] = val[r:r + 1, c0:c0 + w]

    out_shape = [SDS(shapes[k], f32) for k in SMALL for _ in range(4)]
    res = _run("adamw_small", body, (), [tot, pk_w, pk_m, pk_v], [VM] * 4, out_shape, [VM] * len(out_shape))
    return {k: res[4 * i:4 * i + 4] for i, k in enumerate(SMALL)}


def prep_weights(shards):
    n = len(shards)

    def body(*refs):
        for e in range(n):
            refs[n + e][...] = _c(refs[e][...])

    return _run("prep_weights", body, (), shards, [VM] * n, [SDS(a.shape, MXU) for a in shards], [VM] * n, vmem_mib=48)


def mem_kv_fwd(mem2d, pk, wmkv):
    M, D = mem2d.shape

    def body(m_ref, pk_ref, w_ref, mn_ref, kv_ref, km_ref, vm_ref):
        m = m_ref[...]
        mn = _c(m * _rstd(m) * _small(pk_ref, "norm_mem"))
        mn_ref[...] = mn
        kv = _nn(mn, w_ref[...])
        kv_ref[...] = kv
        kk = kv[:, :MEM_W]
        km_ref[...] = _c(kk * _heads_rstd(kk) * _lanes(_small(pk_ref, "mem_k_norm"), MEM_W))
        vm_ref[...] = _c(kv[:, MEM_W:])

    return _run("mem_kv_fwd", body, (), [mem2d, pk, wmkv], [VM] * 3,
                [SDS((M, D), MXU), SDS((M, 2 * MEM_W), f32), SDS((M, MEM_W), MXU), SDS((M, MEM_W), MXU)], [VM] * 4)


QKV_W = ATT_W + 2 * KV_W + MEM_W


def in_proj_fwd(x2d, pk, winT, tm, exchange):
    T, D = x2d.shape
    P = winT.shape[0]

    def body(x_ref, pk_ref, w_ref, xn_ref, proj_ref, qkv_ref):
        xv = x_ref[...]
        xn = _c(xv * _rstd(xv) * _small(pk_ref, "norm_mix"))
        xn_ref[...] = xn
        proj = _nt(xn, w_ref[...])
        proj_ref[...] = proj
        q, k = proj[:, :ATT_W], proj[:, ATT_W:ATT_W + KV_W]
        qm = proj[:, P - MEM_W:]
        qkv_ref[...] = jnp.concatenate(
            [_c(q * _heads_rstd(q) * _lanes(_small(pk_ref, "q_norm"), ATT_W)),
             _c(k * _heads_rstd(k) * _lanes(_small(pk_ref, "k_norm"), KV_W)),
             _c(proj[:, ATT_W + KV_W:ATT_W + 2 * KV_W]),
             _c(qm * _heads_rstd(qm) * _lanes(_small(pk_ref, "mem_q_norm"), MEM_W))], axis=1)

    return _run("in_proj_fwd", body, (T // tm,), [x2d, pk, winT],
                [pl.BlockSpec((tm, D), lambda i: (i, 0)), VM, VM],
                [SDS((T, D), MXU), SDS((T, P), f32), SDS((T, QKV_W), MXU)],
                [pl.BlockSpec((tm, D), lambda i: (i, 0)), pl.BlockSpec((tm, P), lambda i: (i, 0)),
                 pl.BlockSpec((tm, QKV_W), lambda i: (i, 0))],
                vmem_mib=40, exchange=exchange)


def _swa_bias_table():
    r = np.arange(GQA * BLK)[:, None]
    k = np.arange(2 * BLK)[None, :]
    dist = (r % BLK) + BLK - k
    band = (dist >= 0) & (dist < BLK)
    tab = np.empty((2, N_KV, GQA * BLK, 2 * BLK), np.float32)
    for later in range(2):
        valid = band & ((k >= BLK) | (later == 1))
        for g in range(N_KV):
            slope = 2.0 ** -(g * GQA + r // BLK + 1.0)
            tab[later, g] = np.where(valid, -slope * dist, NEG)
    return jnp.asarray(tab)


def _sink_column(g, sk_ref):
    hrow = lax.broadcasted_iota(jnp.int32, (GQA * BLK, 1), 0) // BLK
    sink = jnp.zeros((GQA * BLK, 1), f32)
    for hh in range(GQA):
        sink = jnp.where(hrow == hh, sk_ref[g * GQA + hh:g * GQA + hh + 1, 0:1], sink)
    return sink


def _stack_heads(v, g):
    return jnp.concatenate([v[:, (g * GQA + hh) * HD:(g * GQA + hh + 1) * HD] for hh in range(GQA)], axis=0)


def attn_fwd(qkv, sink_rows, BL, S, exchange):
    NB = S // BLK
    T = BL * S

    def body(q_ref, kc_ref, kp_ref, vc_ref, vp_ref, sk_ref, tab_ref, o_ref):
        q = q_ref[...]
        k2 = jnp.concatenate([kp_ref[...], kc_ref[...]], axis=0)
        v2 = jnp.concatenate([vp_ref[...], vc_ref[...]], axis=0)
        ones = jnp.ones((2 * BLK, HD), MXU)
        for g in range(N_KV):
            kn, vh = k2[:, g * HD:(g + 1) * HD], v2[:, g * HD:(g + 1) * HD]
            s = _nt(_stack_heads(q, g), kn) * (HD ** -0.5) + tab_ref[g]
            e, es = _exp_scores(s, _sink_column(g, sk_ref))
            eb = _c(e)
            o = _nn(eb, vh) * (1.0 / (_nn(eb, ones) + es))
            for hh in range(GQA):
                o_ref[:, pl.ds((g * GQA + hh) * HD, HD)] = o[hh * BLK:(hh + 1) * BLK]

    cur = lambda col: (lambda b, j: (b * NB + j, col))
    prev = lambda col: (lambda b, j: (b * NB + jnp.maximum(j - 1, 0), col))
    return _run("attn_fwd", body, (BL, NB), [qkv, qkv, qkv, qkv, qkv, sink_rows, _swa_bias_table()],
                [pl.BlockSpec((BLK, ATT_W), cur(0)),
                 pl.BlockSpec((BLK, KV_W), cur(4)), pl.BlockSpec((BLK, KV_W), prev(4)),
                 pl.BlockSpec((BLK, KV_W), cur(5)), pl.BlockSpec((BLK, KV_W), prev(5)),
                 pl.BlockSpec((8, 128), lambda b, j: (0, 0)),
                 pl.BlockSpec((None, N_KV, GQA * BLK, 2 * BLK), lambda b, j: (jnp.minimum(j, 1), 0, 0, 0))],
                [SDS((T, ATT_W), f32)], [pl.BlockSpec((BLK, ATT_W), cur(0))], exchange=exchange)


def _conv_taps(u, uh):
    row = lax.broadcasted_iota(jnp.int32, u.shape, 0)
    u1 = jnp.where(row == 0, uh[7:8, :], pltpu.roll(u, 1, 0))
    u2 = jnp.where(row == 0, uh[6:7, :], jnp.where(row == 1, uh[7:8, :], pltpu.roll(u, 2, 0)))
    return u1, u2


def _mem_head(qm, km, vm, h):
    qh, kh, vh = (a[:, h * HD:(h + 1) * HD] for a in (qm, km, vm))
    e, _ = _exp_scores(_nt(qh, kh) * (HD ** -0.5))
    return qh, kh, vh, e


def mixer_tail_fwd(x2d, attn_out, proj, qkv, km, vm, conv_w8, pk, wout, S, tm, exchange):
    T, D = x2d.shape
    NM = km.shape[0] // (T // S)

    def body(x_ref, ao_ref, ch_ref, cb_ref, cc_ref, chh_ref, cch_ref, qm_ref, km_ref, vm_ref, cw_ref, pk_ref,
             wout_ref, co_ref, mo_ref, mg_ref, x1_ref, h_ref):
        first = (pl.program_id(0) * tm) % S == 0
        u = cc_ref[...] * ch_ref[...]
        uh = jnp.where(first, 0.0, cch_ref[...] * chh_ref[...])
        u1, u2 = _conv_taps(u, uh)
        conv = cw_ref[0:1, :] * u2 + cw_ref[1:2, :] * u1 + cw_ref[2:3, :] * u + _small(pk_ref, "conv_b")
        conv_out = cb_ref[...] * conv
        co_ref[...] = conv_out
        qm, kmv, vmv = qm_ref[...], km_ref[...], vm_ref[...]
        ones = jnp.ones((NM, HD), MXU)
        for h in range(N_MEMH):
            _, _, vh, e = _mem_head(qm, kmv, vmv, h)
            eb = _c(e)
            mo_ref[:, pl.ds(h * HD, HD)] = _nn(eb, vh) * (1.0 / _nn(eb, ones))
        mem_out = mo_ref[...]
        ao = ao_ref[...]
        merged = _c(jnp.concatenate([ao * _rstd(ao) * _small(pk_ref, "out_norm_attn"),
                                     conv_out * _rstd(conv_out) * _small(pk_ref, "out_norm_conv"),
                                     mem_out * _rstd(mem_out) * _small(pk_ref, "out_norm_mem")], axis=1))
        mg_ref[...] = merged
        x1 = x_ref[...] + _nn(merged, wout_ref[...])
        x1_ref[...] = x1
        h_ref[...] = _c(x1 * _rstd(x1) * _small(pk_ref, "norm_ffn"))

    tile = lambda w, col: pl.BlockSpec((tm, w), lambda i: (i, col))
    halo = lambda col: pl.BlockSpec((8, CONV_W), lambda i: (jnp.maximum(i * (tm // 8) - 1, 0), col))
    seq = pl.BlockSpec((NM, MEM_W), lambda i: ((i * tm) // S, 0))
    small = lambda a: pl.BlockSpec(a.shape, lambda i: (0, 0))
    return _run("mixer_tail_fwd", body, (T // tm,),
                [x2d, attn_out, proj, proj, proj, proj, proj, qkv, km, vm, conv_w8, pk, wout],
                [tile(D, 0), tile(ATT_W, 0), tile(CONV_W, 3), tile(CONV_W, 4), tile(CONV_W, 5), halo(3), halo(5),
                 tile(MEM_W, 3), seq, seq, VM, VM, VM],
                [SDS((T, CONV_W), f32), SDS((T, MEM_W), f32), SDS((T, D), MXU), SDS((T, D), f32), SDS((T, D), MXU)],
                [tile(CONV_W, 0), tile(MEM_W, 0), tile(D, 0), tile(D, 0), tile(D, 0)], vmem_mib=40, exchange=exchange)


def ffn_fwd_bwd(h, x1, tgt, wgT, wuT, wd, pk, tm):
    T, D = x1.shape
    F = wd.shape[0]

    def body(h_ref, x1_ref, t_ref, wg_ref, wu_ref, wd_ref, pk_ref,
             dx1_ref, dx2_ref, act_ref, dg_ref, du_ref, loss_ref, dgf_ref):
        @pl.when(pl.program_id(0) == 0)
        def _():
            loss_ref[...] = jnp.zeros_like(loss_ref)
            dgf_ref[...] = jnp.zeros_like(dgf_ref)

        hv = h_ref[...]
        gate = _nt(hv, wg_ref[...])
        up = _nt(hv, wu_ref[...])
        sg = jax.nn.sigmoid(gate)
        sl = gate * sg
        act = _c(sl * up)
        act_ref[...] = act
        x1v = x1_ref[...]
        diff = (x1v + _nn(act, wd_ref[...])) - t_ref[...]
        loss_ref[...] += 0.5 * jnp.sum(jnp.sum(diff * diff, axis=-1, keepdims=True) / D, axis=0, keepdims=True)
        dx2 = diff / D
        dx2b = _c(dx2)
        dx2_ref[...] = dx2b
        d_act = _nt(dx2b, wd_ref[...])
        d_up = _c(d_act * sl)
        d_gate = _c(d_act * up * (sg * (1.0 + gate * (1.0 - sg))))
        du_ref[...] = d_up
        dg_ref[...] = d_gate
        dh = _nn(d_gate, wg_ref[...]) + _nn(d_up, wu_ref[...])
        dv, dgf = _norm_bwd(dh, x1v, _rstd(x1v), _small(pk_ref, "norm_ffn"))
        dx1_ref[...] = dx2 + dv
        dgf_ref[...] += dgf

    tile = lambda w: pl.BlockSpec((tm, w), lambda i: (i, 0))
    return _run("ffn_fwd_bwd", body, (T // tm,), [h, x1, tgt, wgT, wuT, wd, pk],
                [tile(D), tile(D), tile(D), VM, VM, VM, VM],
                [SDS((T, D), f32), SDS((T, D), MXU), SDS((T, F), MXU), SDS((T, F), MXU), SDS((T, F), MXU),
                 SDS((8, 128), f32), SDS((1, D), f32)],
                [tile(D), tile(D), tile(F), tile(F), tile(F), pl.BlockSpec((8, 128), lambda i: (0, 0)),
                 pl.BlockSpec((1, D), lambda i: (0, 0))], vmem_mib=56)


def matmul_tn(a, b, name, tmo, tk):
    T, M = a.shape
    N = b.shape[1]

    def body(a_ref, b_ref, o_ref):
        @pl.when(pl.program_id(1) == 0)
        def _():
            o_ref[...] = jnp.zeros_like(o_ref)

        o_ref[...] += _tn(a_ref[...], b_ref[...])

    return _run(name, body, (M // tmo, T // tk), [a, b],
                [pl.BlockSpec((tk, tmo), lambda m, k: (k, m)), pl.BlockSpec((tk, N), lambda m, k: (k, 0))],
                [SDS((M, N), f32)], [pl.BlockSpec((tmo, N), lambda m, k: (m, 0))], vmem_mib=48)[0]


def out_proj_bwd(dx1, merged, attn_out, conv_out, mem_out, pk, wout, tm):
    T, D = dx1.shape

    def body(dx1_ref, mg_ref, ao_ref, co_ref, mo_ref, pk_ref, w_ref,
             dao_ref, dco_ref, dmo_ref, dw_ref, dgain_ref):
        @pl.when(pl.program_id(0) == 0)
        def _():
            dw_ref[...] = jnp.zeros_like(dw_ref)
            dgain_ref[...] = jnp.zeros_like(dgain_ref)

        dxb = _c(dx1_ref[...])
        dw_ref[...] += _tn(mg_ref[...], dxb)
        dmg = _nt(dxb, w_ref[...])
        ao, co, mo = ao_ref[...], co_ref[...], mo_ref[...]
        da, ga = _norm_bwd(dmg[:, :ATT_W], ao, _rstd(ao), _small(pk_ref, "out_norm_attn"))
        dc, gc = _norm_bwd(dmg[:, ATT_W:ATT_W + CONV_W], co, _rstd(co), _small(pk_ref, "out_norm_conv"))
        dm, gm = _norm_bwd(dmg[:, ATT_W + CONV_W:], mo, _rstd(mo), _small(pk_ref, "out_norm_mem"))
        dao_ref[...] = da
        dco_ref[...] = dc
        dmo_ref[...] = dm
        dgain_ref[...] += jnp.concatenate([ga, gc, gm], axis=1)

    tile = lambda w: pl.BlockSpec((tm, w), lambda i: (i, 0))
    return _run("out_proj_bwd", body, (T // tm,), [dx1, merged, attn_out, conv_out, mem_out, pk, wout],
                [tile(D), tile(D), tile(ATT_W), tile(CONV_W), tile(MEM_W), VM, VM],
                [SDS((T, ATT_W), f32), SDS((T, CONV_W), f32), SDS((T, MEM_W), f32), SDS((D, D), f32), SDS((1, D), f32)],
                [tile(ATT_W), tile(CONV_W), tile(MEM_W), pl.BlockSpec((D, D), lambda i: (0, 0)),
                 pl.BlockSpec((1, D), lambda i: (0, 0))], vmem_mib=40)


def attn_bwd(qkv, d_attn, attn_out, sink_rows, BL, S, exchange):
    NB = S // BLK
    T = BL * S

    def body(q_ref, kc_ref, kp_ref, vc_ref, vp_ref, do_ref, ao_ref, sk_ref, tab_ref,
             dq_ref, dk_ref, dv_ref, dsk_ref, pend_k, pend_v):
        b, j = pl.program_id(0), pl.program_id(1)

        @pl.when((b == 0) & (j == 0))
        def _():
            dsk_ref[...] = jnp.zeros_like(dsk_ref)

        @pl.when(j == 0)
        def _():
            pend_k[...] = jnp.zeros_like(pend_k)
            pend_v[...] = jnp.zeros_like(pend_v)

        @pl.when(j < NB)
        def _():
            q, do, ao = q_ref[...], do_ref[...], ao_ref[...]
            k2 = jnp.concatenate([kp_ref[...], kc_ref[...]], axis=0)
            v2 = jnp.concatenate([vp_ref[...], vc_ref[...]], axis=0)
            lane = lax.broadcasted_iota(jnp.int32, (8, 128), 1)
            ones_w = jnp.ones((2 * BLK, 2 * BLK), MXU)
            dsk = jnp.zeros((8, 128), f32)
            dks, dvs = [], []
            for g in range(N_KV):
                kn, vh = k2[:, g * HD:(g + 1) * HD], v2[:, g * HD:(g + 1) * HD]
                qs = _stack_heads(q, g)
                s = _nt(qs, kn) * (HD ** -0.5) + tab_ref[g]
                e, es = _exp_scores(s, _sink_column(g, sk_ref))
                eb = _c(e)
                inv_w = 1.0 / (_nn(eb, ones_w) + es)
                inv_n = inv_w[:, :HD]
                dos = _stack_heads(do, g)
                delta = _rowsum_mxu(dos * _stack_heads(ao, g), 2 * BLK)
                dp = _nt(_c(dos), vh)
                ds = _c(e * inv_w * (dp - delta) * (HD ** -0.5))
                t = es * inv_n[:, 0:1] * delta[:, 0:1]
                for hh in range(GQA):
                    dsk = dsk + jnp.where(lane == g * GQA + hh, -jnp.sum(t[hh * BLK:(hh + 1) * BLK]), 0.0)
                dvs.append(_tn(eb, _c(dos * inv_n)))
                dks.append(_tn(ds, qs))
                dqs = _nn(ds, kn)
                for hh in range(GQA):
                    dq_ref[:, pl.ds((g * GQA + hh) * HD, HD)] = dqs[hh * BLK:(hh + 1) * BLK]
            dk2 = jnp.concatenate(dks, axis=1)
            dv2 = jnp.concatenate(dvs, axis=1)
            dk_ref[...] = pend_k[...] + dk2[:BLK]
            dv_ref[...] = pend_v[...] + dv2[:BLK]
            pend_k[...] = dk2[BLK:]
            pend_v[...] = dv2[BLK:]
            dsk_ref[...] += dsk

        @pl.when(j == NB)
        def _():
            dk_ref[...] = pend_k[...]
            dv_ref[...] = pend_v[...]

    cur = lambda col: (lambda b, j: (b * NB + jnp.minimum(j, NB - 1), col))
    prev = lambda col: (lambda b, j: (b * NB + jnp.maximum(j - 1, 0), col))
    small = lambda shape: pl.BlockSpec(shape, lambda b, j: (0, 0))
    return _run("attn_bwd", body, (BL, NB + 1), [qkv, qkv, qkv, qkv, qkv, d_attn, attn_out, sink_rows, _swa_bias_table()],
                [pl.BlockSpec((BLK, ATT_W), cur(0)),
                 pl.BlockSpec((BLK, KV_W), cur(4)), pl.BlockSpec((BLK, KV_W), prev(4)),
                 pl.BlockSpec((BLK, KV_W), cur(5)), pl.BlockSpec((BLK, KV_W), prev(5)),
                 pl.BlockSpec((BLK, ATT_W), cur(0)), pl.BlockSpec((BLK, ATT_W), cur(0)), small((8, 128)),
                 pl.BlockSpec((None, N_KV, GQA * BLK, 2 * BLK), lambda b, j: (jnp.minimum(j, 1), 0, 0, 0))],
                [SDS((T, ATT_W), f32), SDS((T, KV_W), f32), SDS((T, KV_W), f32), SDS((8, 128), f32)],
                [pl.BlockSpec((BLK, ATT_W), cur(0)), pl.BlockSpec((BLK, KV_W), prev(0)),
                 pl.BlockSpec((BLK, KV_W), prev(0)), small((8, 128))],
                scratch=[pltpu.VMEM((BLK, KV_W), f32)] * 2, exchange=exchange)


def mem_conv_bwd(d_mem_out, mem_out, d_conv_out, proj, qkv, km, vm, conv_w8, pk, S, tm, exchange):
    T = d_mem_out.shape[0]
    NM = km.shape[0] // (T // S)

    def body(dmo_ref, mo_ref, dco_ref, ch_ref, cb_ref, cc_ref, chh_ref, cch_ref, qm_ref, km_ref, vm_ref, cw_ref,
             pk_ref, dqm_ref, dkm_ref, dvm_ref, dcb_ref, dcv_ref, dcw_ref, dcbias_ref):
        i = pl.program_id(0)
        first = (i * tm) % S == 0

        @pl.when(i == 0)
        def _():
            dcw_ref[...] = jnp.zeros_like(dcw_ref)
            dcbias_ref[...] = jnp.zeros_like(dcbias_ref)

        @pl.when(first)
        def _():
            dkm_ref[...] = jnp.zeros_like(dkm_ref)
            dvm_ref[...] = jnp.zeros_like(dvm_ref)

        qm, kmv, vmv, dmo, mo = qm_ref[...], km_ref[...], vm_ref[...], dmo_ref[...], mo_ref[...]
        ones_w = jnp.ones((NM, NM), MXU)
        for h in range(N_MEMH):
            qh, kh, vh, e = _mem_head(qm, kmv, vmv, h)
            eb = _c(e)
            doh = dmo[:, h * HD:(h + 1) * HD]
            delta = _rowsum_mxu(doh * mo[:, h * HD:(h + 1) * HD], NM)
            dp = _nt(_c(doh), vh)
            inv_w = 1.0 / _nn(eb, ones_w)
            ds = _c(e * inv_w * (dp - delta) * (HD ** -0.5))
            dvm_ref[:, pl.ds(h * HD, HD)] += _tn(eb, _c(doh * inv_w[:, :HD]))
            dkm_ref[:, pl.ds(h * HD, HD)] += _tn(ds, qh)
            dqm_ref[:, pl.ds(h * HD, HD)] = _nn(ds, kh)

        u = cc_ref[...] * ch_ref[...]
        uh = jnp.where(first, 0.0, cch_ref[...] * chh_ref[...])
        u1, u2 = _conv_taps(u, uh)
        conv = cw_ref[0:1, :] * u2 + cw_ref[1:2, :] * u1 + cw_ref[2:3, :] * u + _small(pk_ref, "conv_b")
        dy = dco_ref[...]
        dcb_ref[...] = dy * conv
        dcv = dy * cb_ref[...]
        dcv_ref[...] = dcv
        dcbias_ref[...] += jnp.sum(dcv, axis=0, keepdims=True)
        dcw_ref[0:1, :] += jnp.sum(dcv * u2, axis=0, keepdims=True)
        dcw_ref[1:2, :] += jnp.sum(dcv * u1, axis=0, keepdims=True)
        dcw_ref[2:3, :] += jnp.sum(dcv * u, axis=0, keepdims=True)

    tile = lambda w, col: pl.BlockSpec((tm, w), lambda i: (i, col))
    halo = lambda col: pl.BlockSpec((8, CONV_W), lambda i: (jnp.maximum(i * (tm // 8) - 1, 0), col))
    seq = pl.BlockSpec((NM, MEM_W), lambda i: ((i * tm) // S, 0))
    const = lambda shape: pl.BlockSpec(shape, lambda i: (0, 0))
    return _run("mem_conv_bwd", body, (T // tm,),
                [d_mem_out, mem_out, d_conv_out, proj, proj, proj, proj, proj, qkv, km, vm, conv_w8, pk],
                [tile(MEM_W, 0), tile(MEM_W, 0), tile(CONV_W, 0), tile(CONV_W, 3), tile(CONV_W, 4), tile(CONV_W, 5),
                 halo(3), halo(5), tile(MEM_W, 3), seq, seq, VM, VM],
                [SDS((T, MEM_W), f32), SDS(km.shape, f32), SDS(km.shape, f32),
                 SDS((T, CONV_W), f32), SDS((T, CONV_W), f32), SDS((8, CONV_W), f32), SDS((1, CONV_W), f32)],
                [tile(MEM_W, 0), seq, seq, tile(CONV_W, 0), tile(CONV_W, 0), const((8, CONV_W)), const((1, CONV_W))],
                exchange=exchange)


def in_proj_bwd(dqn, dkn, dv, dcb, dcv, dqmn, proj, conv_w8, xn, x2d, dx1, pk, winT, S, tm):
    T, D = x2d.shape
    P = winT.shape[0]
    last_blk = T // 8 - 1

    def body(dq_ref, dk_ref, dv_ref, dcb_ref, dcv_ref, dcvn_ref, dqm_ref, qa_ref, ka_ref, ch_ref, cc_ref, qma_ref,
             cw_ref, xn_ref, x_ref, dx1_ref, pk_ref, w_ref,
             dx_ref, dw_ref, dg_ref, dqg_ref, dkg_ref, dmqg_ref):
        i = pl.program_id(0)

        @pl.when(i == 0)
        def _():
            dw_ref[...] = jnp.zeros_like(dw_ref)
            dg_ref[...] = jnp.zeros_like(dg_ref)
            dqg_ref[...] = jnp.zeros_like(dqg_ref)
            dkg_ref[...] = jnp.zeros_like(dkg_ref)
            dmqg_ref[...] = jnp.zeros_like(dmqg_ref)

        dqa, gq = _heads_norm_bwd(dq_ref[...], qa_ref[...], _small(pk_ref, "q_norm"))
        dka, gk = _heads_norm_bwd(dk_ref[...], ka_ref[...], _small(pk_ref, "k_norm"))
        dqma, gmq = _heads_norm_bwd(dqm_ref[...], qma_ref[...], _small(pk_ref, "mem_q_norm"))
        dqg_ref[...] += gq
        dkg_ref[...] += gk
        dmqg_ref[...] += gmq

        last = ((i + 1) * tm) % S == 0
        dcv = dcv_ref[...]
        nxt = jnp.where(last, 0.0, dcvn_ref[...])
        row = lax.broadcasted_iota(jnp.int32, dcv.shape, 0)
        n1 = jnp.where(row == tm - 1, nxt[0:1, :], pltpu.roll(dcv, tm - 1, 0))
        n2 = jnp.where(row == tm - 2, nxt[0:1, :], jnp.where(row == tm - 1, nxt[1:2, :], pltpu.roll(dcv, tm - 2, 0)))
        du = cw_ref[2:3, :] * dcv + cw_ref[1:2, :] * n1 + cw_ref[0:1, :] * n2
        d_proj = jnp.concatenate([_c(dqa), _c(dka), _c(dv_ref[...]), _c(du * cc_ref[...]),
                                  _c(dcb_ref[...]), _c(du * ch_ref[...]), _c(dqma)], axis=1)
        dw_ref[...] += _tn(d_proj, xn_ref[...])
        xv = x_ref[...]
        dv_, dg = _norm_bwd(_nn(d_proj, w_ref[...]), xv, _rstd(xv), _small(pk_ref, "norm_mix"))
        dx_ref[...] = dx1_ref[...] + dv_
        dg_ref[...] += dg

    tile = lambda w, col=0: pl.BlockSpec((tm, w), lambda i: (i, col))
    nhalo = pl.BlockSpec((8, CONV_W), lambda i: (jnp.minimum((i + 1) * (tm // 8), last_blk), 0))
    const = lambda shape: pl.BlockSpec(shape, lambda i: (0, 0))
    return _run("in_proj_bwd", body, (T // tm,),
                [dqn, dkn, dv, dcb, dcv, dcv, dqmn, proj, proj, proj, proj, proj, conv_w8, xn, x2d, dx1, pk, winT],
                [tile(ATT_W), tile(KV_W), tile(KV_W), tile(CONV_W), tile(CONV_W), nhalo, tile(MEM_W),
                 tile(ATT_W, 0), tile(KV_W, 4), tile(CONV_W, 3), tile(CONV_W, 5), tile(MEM_W, 6), VM,
                 tile(D), tile(D), tile(D), VM, VM],
                [SDS((T, D), f32), SDS((P, D), f32), SDS((1, D), f32), SDS((1, HD), f32), SDS((1, HD), f32),
                 SDS((1, HD), f32)],
                [tile(D), pl.BlockSpec((P, D), lambda i: (0, 0)), const((1, D)), const((1, HD)), const((1, HD)),
                 const((1, HD))],
                vmem_mib=48)


def mem_kv_bwd(dkm, dvm, kv, memn, mem2d, pk, wmkv):
    def body(dkm_ref, dvm_ref, kv_ref, mn_ref, m_ref, pk_ref, w_ref, dw_ref, dg_ref, dkg_ref):
        dkk, dkg = _heads_norm_bwd(dkm_ref[...], kv_ref[:, :MEM_W], _small(pk_ref, "mem_k_norm"))
        dkg_ref[...] = dkg
        dkv = _c(jnp.concatenate([dkk, dvm_ref[...]], axis=1))
        dw_ref[...] = _tn(mn_ref[...], dkv)
        mv = m_ref[...]
        dg_ref[...] = jnp.sum(_nt(dkv, w_ref[...]) * mv * _rstd(mv), axis=0, keepdims=True)

    return _run("mem_kv_bwd", body, (), [dkm, dvm, kv, memn, mem2d, pk, wmkv], [VM] * 7,
                [SDS(wmkv.shape, f32), SDS((1, mem2d.shape[1]), f32), SDS((1, HD), f32)], [VM] * 3, vmem_mib=40)


def _halves_view(g):
    return g.reshape(4, 2, g.shape[0] // 8, g.shape[1])


def kernel(x, mem, norm_mix, w_in, q_norm, k_norm, attn_sinks, conv_w, conv_b, norm_mem, w_mem_kv, mem_q_norm, mem_k_norm, out_norm_attn, out_norm_conv, out_norm_mem, w_out, norm_ffn, w_gate, w_up, w_down, loss_target, m_norm_mix, m_w_in, m_q_norm, m_k_norm, m_attn_sinks, m_conv_w, m_conv_b, m_norm_mem, m_w_mem_kv, m_mem_q_norm, m_mem_k_norm, m_out_norm_attn, m_out_norm_conv, m_out_norm_mem, m_w_out, m_norm_ffn, m_w_gate, m_w_up, m_w_down, v_norm_mix, v_w_in, v_q_norm, v_k_norm, v_attn_sinks, v_conv_w, v_conv_b, v_norm_mem, v_w_mem_kv, v_mem_q_norm, v_mem_k_norm, v_out_norm_attn, v_out_norm_conv, v_out_norm_mem, v_w_out, v_norm_ffn, v_w_gate, v_w_up, v_w_down):
    BL, S, D = x.shape
    T = BL * S
    TM = 256
    _, _, ci = _place()
    cidx = ci.reshape(1).astype(jnp.int32)
    w_small = dict(norm_mix=norm_mix, norm_mem=norm_mem, norm_ffn=norm_ffn, out_norm_attn=out_norm_attn,
                   out_norm_conv=out_norm_conv, out_norm_mem=out_norm_mem, conv_w=conv_w, conv_b=conv_b, q_norm=q_norm,
                   k_norm=k_norm, mem_q_norm=mem_q_norm, mem_k_norm=mem_k_norm, attn_sinks=attn_sinks)
    m_small = dict(norm_mix=m_norm_mix, norm_mem=m_norm_mem, norm_ffn=m_norm_ffn, out_norm_attn=m_out_norm_attn,
                   out_norm_conv=m_out_norm_conv, out_norm_mem=m_out_norm_mem, conv_w=m_conv_w, conv_b=m_conv_b,
                   q_norm=m_q_norm, k_norm=m_k_norm, mem_q_norm=m_mem_q_norm, mem_k_norm=m_mem_k_norm,
                   attn_sinks=m_attn_sinks)
    v_small = dict(norm_mix=v_norm_mix, norm_mem=v_norm_mem, norm_ffn=v_norm_ffn, out_norm_attn=v_out_norm_attn,
                   out_norm_conv=v_out_norm_conv, out_norm_mem=v_out_norm_mem, conv_w=v_conv_w, conv_b=v_conv_b,
                   q_norm=v_q_norm, k_norm=v_k_norm, mem_q_norm=v_mem_q_norm, mem_k_norm=v_mem_k_norm,
                   attn_sinks=v_attn_sinks)
    pk = _pack_small(w_small)

    rowblocks = lambda a, b, c, d, e, f: [a[0].T, b[0].T, c[0].T, d[0], e[0], f[0]]
    w_rb = rowblocks(w_in, w_gate, w_up, w_down, w_out, w_mem_kv)
    m_rb = rowblocks(m_w_in, m_w_gate, m_w_up, m_w_down, m_w_out, m_w_mem_kv)
    v_rb = rowblocks(v_w_in, v_w_gate, v_w_up, v_w_down, v_w_out, v_w_mem_kv)
    winT_s, wgT_s, wuT_s, wd_s, wout_s, wmkv_s = prep_weights(w_rb)
    cw_pad = jnp.zeros((8, 128), f32).at[:3, :HD].set(conv_w[0])
    _, (winT, cw_all) = _run("gather_w_in", None, (), [], [], [], [], exchange=gather_exchange([winT_s, cw_pad], [True, False]))
    conv_w_full = jnp.transpose(cw_all.reshape(4, 8, 128)[:, :3, :HD], (1, 0, 2)).reshape(3, CONV_W)
    conv_w8 = jnp.zeros((8, CONV_W), f32).at[:3].set(conv_w_full)
    sink_rows = jnp.broadcast_to(attn_sinks.reshape(N_Q, 1), (N_Q, 128))

    x2d = x.reshape(T, D)
    mem2d = mem.reshape(-1, D)
    (xn, proj, qkv), (wgT,) = in_proj_fwd(x2d, pk, winT, TM, gather_exchange([wgT_s], [True]))
    (attn_out,), (wuT, wout, wmkv) = attn_fwd(qkv, sink_rows, BL, S,
                                              gather_exchange([wuT_s, wout_s, wmkv_s], [True, True, True]))
    memn, kv, km, vm = mem_kv_fwd(mem2d, pk, wmkv)
    (conv_out, mem_out, merged, x1, h), (wd,) = mixer_tail_fwd(
        x2d, attn_out, proj, qkv, km, vm, conv_w8, pk, wout, S, TM, gather_exchange([wd_s], [True]))

    dx1, dx2b, act, d_gate, d_up, loss8, d_norm_ffn = ffn_fwd_bwd(h, x1, loss_target.reshape(T, D), wgT, wuT, wd, pk, TM)
    F = wd.shape[0]
    g_wd = matmul_tn(act, dx2b, "dw_down", F // 2, min(T, 1024))
    g_wgT = matmul_tn(d_gate, h, "dw_gate", F // 2, min(T, 1024))
    g_wuT = matmul_tn(d_up, h, "dw_up", F // 2, min(T, 1024))

    d_attn, d_conv_out, d_mem_out, g_wout, d_gains = out_proj_bwd(dx1, merged, attn_out, conv_out, mem_out, pk, wout, TM)
    late = [_halves_view(g) for g in (g_wgT, g_wuT, g_wd, g_wout)]
    (dqmn, dkm, dvm, dcb, dcv, d_cw8, d_cbias), late_sib = mem_conv_bwd(
        d_mem_out, mem_out, d_conv_out, proj, qkv, km, vm, conv_w8, pk, S, TM, halves_exchange(late))
    late_part = add_halves(cidx, late, late_sib, "grad_add_halves_ffn")
    (dqn, dkn, dv, d_sink8), late_stage = attn_bwd(qkv, d_attn, attn_out, sink_rows, BL, S, scatter_exchange(late_part))
    g_x, g_winT, d_norm_mix, d_qg, d_kg, d_mqg = in_proj_bwd(
        dqn, dkn, dv, dcb, dcv, dqmn, proj, conv_w8, xn, x2d, dx1, pk, winT, S, TM)
    g_wmkv, d_norm_mem, d_mkg = mem_kv_bwd(dkm, dvm, kv, memn, mem2d, pk, wmkv)

    tot, tail_stage = tail_reduce(d_norm_mix, d_norm_mem, d_norm_ffn, d_gains, d_cw8, d_cbias, d_qg, d_kg, d_mqg, d_mkg,
                                  d_sink8, loss8, [_halves_view(g) for g in (g_winT, g_wmkv)])
    loss = tot[5, 384]
    late_res, _ = adamw_big("adamw_late", late_stage, w_rb[1:5], m_rb[1:5], v_rb[1:5], 8)
    tail_res, _ = adamw_big("adamw_tail", tail_stage, [w_rb[0], w_rb[5]], [m_rb[0], m_rb[5]], [v_rb[0], v_rb[5]], 4)
    res = {"w_in": [a.T[None] for a in tail_res[0]], "w_gate": [a.T[None] for a in late_res[0]],
           "w_up": [a.T[None] for a in late_res[1]], "w_down": [a[None] for a in late_res[2]],
           "w_out": [a[None] for a in late_res[3]], "w_mem_kv": [a[None] for a in tail_res[1]]}
    res.update(adamw_small(tot, pk, _pack_small(m_small), _pack_small(v_small), {k: w_small[k].shape for k in SMALL}))

    order = ["norm_mix", "w_in", "q_norm", "k_norm", "attn_sinks", "conv_w", "conv_b", "norm_mem", "w_mem_kv",
             "mem_q_norm", "mem_k_norm", "out_norm_attn", "out_norm_conv", "out_norm_mem", "w_out", "norm_ffn",
             "w_gate", "w_up", "w_down"]
    return (loss, g_x.reshape(BL, S, D), *[res[n][0] for n in order], *[res[n][1] for n in order],
            *[res[n][2] for n in order], *[res[n][3] for n in order])
```

```python
import collections
import functools

import jax
import jax.numpy as jnp
import numpy as np
from jax import lax
from jax.experimental import pallas as pl
from jax.experimental.pallas import tpu as pltpu

f32 = jnp.float32
MXU = jnp.bfloat16
WIRE = jnp.bfloat16
EPS = 1e-6
NEG = -1e30
HD = 64
BLK = 128
N_Q, N_KV, N_MEMH = 8, 2, 4
GQA = N_Q // N_KV
ATT_W, KV_W, CONV_W, MEM_W = 512, 128, 256, 256
VMEM_MIB = 1024 * 1024
ADAM_LR, ADAM_B1, ADAM_B2, ADAM_EPS, ADAM_WD, ADAM_STEP = 0.001, 0.9, 0.999, 1e-08, 0.01, 10

MESH = pl.DeviceIdType.MESH
VM = pl.BlockSpec(memory_space=pltpu.VMEM)
ANY = pl.BlockSpec(memory_space=pl.ANY)
SDS = jax.ShapeDtypeStruct
DMA = pltpu.SemaphoreType.DMA


def _c(v):
    return v.astype(MXU)


def _nn(a, b):
    return lax.dot_general(a, b, (((1,), (0,)), ((), ())), preferred_element_type=f32)


def _nt(a, b):
    return lax.dot_general(a, b, (((1,), (1,)), ((), ())), preferred_element_type=f32)


def _tn(a, b):
    return lax.dot_general(a, b, (((0,), (0,)), ((), ())), preferred_element_type=f32)


def _rstd(v):
    return lax.rsqrt(jnp.mean(v * v, axis=-1, keepdims=True) + EPS)


def _norm_bwd(dy, v, r, g):
    dyg = dy * g
    dv = r * dyg - v * (r * r * r) * jnp.mean(dyg * v, axis=-1, keepdims=True)
    return dv, jnp.sum(dy * v * r, axis=0, keepdims=True)


def _split3(v):
    hi = _c(v)
    r1 = v - hi.astype(f32)
    mid = _c(r1)
    return hi, mid, _c(r1 - mid.astype(f32))


def _rowsum_mxu(v, width):
    ones = jnp.ones((v.shape[1], width), MXU)
    return sum(_nn(a, ones) for a in _split3(v))


def _seg_sums(v):
    r = lax.broadcasted_iota(jnp.int32, (2 * HD, 2 * HD), 0) // HD
    c = lax.broadcasted_iota(jnp.int32, (2 * HD, 2 * HD), 1) // HD
    bd = (r == c).astype(MXU)
    outs = []
    for b in range(v.shape[1] // (2 * HD)):
        outs.append(sum(_nn(a, bd) for a in _split3(v[:, b * 2 * HD:(b + 1) * 2 * HD])))
    return outs[0] if len(outs) == 1 else jnp.concatenate(outs, axis=1)


def _lanes(g, width):
    return jnp.concatenate([g] * (width // HD), axis=1)


def _heads_rstd(v):
    return lax.rsqrt(_seg_sums(v * v) * (1.0 / HD) + EPS)


def _heads_norm_bwd(dy, v, g):
    r = _heads_rstd(v)
    gl = _lanes(g, v.shape[1])
    dyg = dy * gl
    dv = r * dyg - v * (r * r * r) * (_seg_sums(dyg * v) * (1.0 / HD))
    dgl = jnp.sum(dy * v * r, axis=0, keepdims=True)
    return dv, sum(dgl[:, s * HD:(s + 1) * HD] for s in range(v.shape[1] // HD))


def _exp_scores(s, extra=None):
    m = jnp.max(s, axis=-1, keepdims=True)
    if extra is None:
        return jnp.exp(s - m), None
    m = jnp.maximum(m, extra)
    return jnp.exp(s - m), jnp.exp(extra - m)


def _place():
    return lax.axis_index("x"), lax.axis_index("y"), lax.axis_index("c")


SMALL_AT = {"norm_mix": (0, 0, 1024), "norm_mem": (1, 0, 1024), "norm_ffn": (2, 0, 1024),
            "out_norm_attn": (3, 0, ATT_W), "out_norm_conv": (3, ATT_W, CONV_W), "out_norm_mem": (3, ATT_W + CONV_W, MEM_W),
            "conv_b": (4, 3 * CONV_W, CONV_W), "q_norm": (5, 0, HD), "k_norm": (5, HD, HD), "mem_q_norm": (5, 2 * HD, HD),
            "mem_k_norm": (5, 3 * HD, HD), "attn_sinks": (5, 256, N_Q)}
SMALL = ("norm_mix", "norm_mem", "norm_ffn", "out_norm_attn", "out_norm_conv", "out_norm_mem", "conv_w", "conv_b",
         "q_norm", "k_norm", "mem_q_norm", "mem_k_norm", "attn_sinks")


def _small(pk_ref, name):
    r, c0, w = SMALL_AT[name]
    return pk_ref[r:r + 1, c0:c0 + w]


def _pack_small(d):
    z = lambda n: jnp.zeros((1, n), f32)
    row3 = jnp.concatenate([d["out_norm_attn"], d["out_norm_conv"], d["out_norm_mem"]], axis=1)
    row4 = jnp.concatenate([d["conv_w"].reshape(1, 3 * HD), z(3 * CONV_W - 3 * HD), d["conv_b"]], axis=1)
    row5 = jnp.concatenate([d["q_norm"], d["k_norm"], d["mem_q_norm"], d["mem_k_norm"], d["attn_sinks"],
                            z(1024 - 4 * HD - N_Q)], axis=1)
    return jnp.concatenate([d["norm_mix"], d["norm_mem"], d["norm_ffn"], row3, row4, row5, z(1024), z(1024)], axis=0)


def _other_chips(x, y):
    return [(1 - x, y), (x, 1 - y), (1 - x, 1 - y)]


Exchange = collections.namedtuple("Exchange", "ins outs sems start finish aliases", defaults=({},))


def _together(exchanges):
    def parts(refs, key):
        out, at = [], 0
        for ex in exchanges:
            out.append(refs[at:at + len(getattr(ex, key))])
            at += len(getattr(ex, key))
        return out

    def start(xa, xo, xs):
        for ex, a, o, s in zip(exchanges, parts(xa, "ins"), parts(xo, "outs"), parts(xs, "sems")):
            ex.start(a, o, s)

    def finish(xa, xo, xs):
        for ex, a, o, s in zip(exchanges, parts(xa, "ins"), parts(xo, "outs"), parts(xs, "sems")):
            ex.finish(a, o, s)

    aliases, ai, ao = {}, 0, 0
    for ex in exchanges:
        aliases.update({ai + i: ao + o for i, o in ex.aliases.items()})
        ai, ao = ai + len(ex.ins), ao + len(ex.outs)
    return Exchange([a for ex in exchanges for a in ex.ins], [o for ex in exchanges for o in ex.outs],
                    [s for ex in exchanges for s in ex.sems], start, finish, aliases)


def _run(name, body, grid, ins, in_specs, out_shape, out_specs, scratch=(), vmem_mib=32, exchange=None):
    ins, in_specs, out_shape, out_specs, scratch = list(ins), list(in_specs), list(out_shape), list(out_specs), list(scratch)
    ni, no, ns = len(ins), len(out_shape), len(scratch)
    ex = exchange
    if ex is not None:
        nxi, nxo = len(ex.ins), len(ex.outs)

    def call_body(*refs):
        if ex is None:
            body(*refs)
            return
        a, xa = refs[:ni], refs[ni:ni + nxi]
        o, xo = refs[ni + nxi:ni + nxi + no], refs[ni + nxi + no:ni + nxi + no + nxo]
        s, xs = refs[ni + nxi + no + nxo:ni + nxi + no + nxo + ns], refs[ni + nxi + no + nxo + ns:]
        if grid:
            first = functools.reduce(jnp.logical_and, [pl.program_id(d) == 0 for d in range(len(grid))])
            last = functools.reduce(jnp.logical_and, [pl.program_id(d) == grid[d] - 1 for d in range(len(grid))])
            pl.when(first)(lambda: ex.start(xa, xo, xs))
            body(*a, *o, *s)
            pl.when(last)(lambda: ex.finish(xa, xo, xs))
        else:
            ex.start(xa, xo, xs)
            if body is not None:
                body(*a, *o, *s)
            ex.finish(xa, xo, xs)

    kw = dict(grid=grid) if grid else {}
    if ex is not None:
        if ex.aliases:
            kw["input_output_aliases"] = {ni + i: no + o for i, o in ex.aliases.items()}
        ins, in_specs = ins + list(ex.ins), in_specs + [ANY] * nxi
        out_shape, out_specs = out_shape + list(ex.outs), out_specs + [ANY] * nxo
        scratch = scratch + list(ex.sems)
    res = pl.pallas_call(
        call_body, name=name, out_shape=out_shape, in_specs=in_specs, out_specs=out_specs, scratch_shapes=scratch,
        compiler_params=pltpu.CompilerParams(dimension_semantics=("arbitrary",) * len(grid) if grid else None,
                                             vmem_limit_bytes=vmem_mib * VMEM_MIB), **kw)(*ins)
    res = list(res)
    return (res[:no], res[no:]) if ex is not None else res


def _remote(src, dst, ssem, rsem, dev):
    return pltpu.make_async_remote_copy(src_ref=src, dst_ref=dst, send_sem=ssem, recv_sem=rsem,
                                        device_id=dev, device_id_type=MESH)


def gather_exchange(shards, split):
    n = len(shards)

    def rows(ref, e, kk, half=None):
        R = shards[e].shape[0]
        if half is None:
            return ref.at[pl.ds(pl.multiple_of(kk * R, 8), R)]
        return ref.at[pl.ds(pl.multiple_of(kk * R + half * (R // 2), 8), R // 2)]

    def ici(src, dst, sm, e, j, chip_j, x, y, c):
        k = 2 * x + y
        if split[e]:
            s = src[e].at[pl.ds(pl.multiple_of(c * (shards[e].shape[0] // 2), 8), shards[e].shape[0] // 2)]
            return _remote(s, rows(dst[e], e, k, c), sm[0].at[6 * e + j], sm[1].at[6 * e + j], (*chip_j, c))
        return _remote(src[e], rows(dst[e], e, k), sm[0].at[6 * e + j], sm[1].at[6 * e + j], (*chip_j, c))

    def landed(dst, e, chip_j, c):
        kj = 2 * chip_j[0] + chip_j[1]
        return rows(dst[e], e, kj, c) if split[e] else rows(dst[e], e, kj)

    def forward(dst, sm, e, j, chip_j, x, y, c, sender_c):
        kj = 2 * chip_j[0] + chip_j[1]
        r = rows(dst[e], e, kj, sender_c)
        return _remote(r, r, sm[0].at[6 * e + 3 + j], sm[1].at[6 * e + 3 + j], (x, y, 1 - c))

    def local(src, dst, sm, e, x, y):
        return pltpu.make_async_copy(src[e], rows(dst[e], e, 2 * x + y), sm[2].at[e])

    def start(src, dst, sm):
        x, y, c = _place()
        for e in range(n):
            local(src, dst, sm, e, x, y).start()
            for j, chip_j in enumerate(_other_chips(x, y)):
                ici(src, dst, sm, e, j, chip_j, x, y, c).start()

    def finish(src, dst, sm):
        x, y, c = _place()
        chips = _other_chips(x, y)
        for e in range(n):
            for j, chip_j in enumerate(chips):
                r = landed(dst, e, chip_j, c)
                _remote(r, r, sm[0].at[6 * e + j], sm[1].at[6 * e + j], (*chip_j, c)).wait_recv()
                if split[e]:
                    forward(dst, sm, e, j, chip_j, x, y, c, c).start()
        for e in range(n):
            for j, chip_j in enumerate(chips):
                if split[e]:
                    forward(dst, sm, e, j, chip_j, x, y, c, 1 - c).wait_recv()
        for e in range(n):
            for j, chip_j in enumerate(chips):
                ici(src, dst, sm, e, j, chip_j, x, y, c).wait_send()
                if split[e]:
                    forward(dst, sm, e, j, chip_j, x, y, c, c).wait_send()
            local(src, dst, sm, e, x, y).wait()

    outs = [SDS((4 * s.shape[0], s.shape[1]), s.dtype) for s in shards]
    return Exchange(list(shards), outs, [DMA((6 * n,)), DMA((6 * n,)), DMA((n,))], start, finish)


def _block_rows(ref, R, kk, half, quarter=None):
    hr = R // 2
    if quarter is None:
        return ref.at[pl.ds(pl.multiple_of(kk * R + half * hr, 8), hr)]
    return ref.at[pl.ds(pl.multiple_of(kk * R + half * hr + quarter * (hr // 2), 8), hr // 2)]


def gather_near_exchange(shards):
    n = len(shards)
    R = [s.shape[0] for s in shards]

    def ici(src, dst, sm, e, j, chip_j, x, y, c):
        half = src[e].at[pl.ds(pl.multiple_of(c * (R[e] // 2), 8), R[e] // 2)]
        return _remote(half, _block_rows(dst[e], R[e], 2 * x + y, c), sm[0].at[4 * e + j], sm[1].at[4 * e + j], (*chip_j, c))

    def forward(dst, sm, e, j, chip_j, x, y, c, sender_c):
        r = _block_rows(dst[e], R[e], 2 * chip_j[0] + chip_j[1], sender_c)
        return _remote(r, r, sm[0].at[4 * e + 2 + j], sm[1].at[4 * e + 2 + j], (x, y, 1 - c))

    def local(src, dst, sm, e, x, y):
        return pltpu.make_async_copy(src[e], dst[e].at[pl.ds(pl.multiple_of((2 * x + y) * R[e], 8), R[e])], sm[2].at[e])

    def start(src, dst, sm):
        x, y, c = _place()
        for e in range(n):
            local(src, dst, sm, e, x, y).start()
            for j, chip_j in enumerate(_other_chips(x, y)[:2]):
                ici(src, dst, sm, e, j, chip_j, x, y, c).start()

    def finish(src, dst, sm):
        x, y, c = _place()
        near = _other_chips(x, y)[:2]
        for e in range(n):
            for j, chip_j in enumerate(near):
                r = _block_rows(dst[e], R[e], 2 * chip_j[0] + chip_j[1], c)
                _remote(r, r, sm[0].at[4 * e + j], sm[1].at[4 * e + j], (*chip_j, c)).wait_recv()
                forward(dst, sm, e, j, chip_j, x, y, c, c).start()
        for e in range(n):
            for j, chip_j in enumerate(near):
                forward(dst, sm, e, j, chip_j, x, y, c, 1 - c).wait_recv()
        for e in range(n):
            for j, chip_j in enumerate(near):
                ici(src, dst, sm, e, j, chip_j, x, y, c).wait_send()
                forward(dst, sm, e, j, chip_j, x, y, c, c).wait_send()
            local(src, dst, sm, e, x, y).wait()

    outs = [SDS((4 * s.shape[0], s.shape[1]), s.dtype) for s in shards]
    return Exchange(list(shards), outs, [DMA((4 * n,)), DMA((4 * n,)), DMA((n,))], start, finish)


def gather_far_exchange(bufs):
    n = len(bufs)
    R = [b.shape[0] // 4 for b in bufs]

    def relay(src, dst, sm, e, j, x, y, c):
        to, of = _other_chips(x, y)[j], _other_chips(x, y)[1 - j]
        kk = 2 * of[0] + of[1]
        return _remote(_block_rows(src[e], R[e], kk, c, j), _block_rows(dst[e], R[e], kk, c, j),
                       sm[0].at[4 * e + j], sm[1].at[4 * e + j], (*to, c))

    def landed(dst, e, j, x, y, half):
        return _block_rows(dst[e], R[e], 2 * (1 - x) + (1 - y), half, j)

    def forward(dst, sm, e, j, x, y, c, sender_c):
        r = landed(dst, e, j, x, y, sender_c)
        return _remote(r, r, sm[0].at[4 * e + 2 + j], sm[1].at[4 * e + 2 + j], (x, y, 1 - c))

    def start(src, dst, sm):
        x, y, c = _place()
        for e in range(n):
            for j in range(2):
                relay(src, dst, sm, e, j, x, y, c).start()

    def finish(src, dst, sm):
        x, y, c = _place()
        for e in range(n):
            for j in range(2):
                r = landed(dst, e, j, x, y, c)
                _remote(r, r, sm[0].at[4 * e + j], sm[1].at[4 * e + j], (*_other_chips(x, y)[j], c)).wait_recv()
                forward(dst, sm, e, j, x, y, c, c).start()
        for e in range(n):
            for j in range(2):
                forward(dst, sm, e, j, x, y, c, 1 - c).wait_recv()
        for e in range(n):
            for j in range(2):
                relay(src, dst, sm, e, j, x, y, c).wait_send()
                forward(dst, sm, e, j, x, y, c, c).wait_send()

    outs = [SDS(b.shape, b.dtype) for b in bufs]
    return Exchange(list(bufs), outs, [DMA((4 * n,)), DMA((4 * n,))], start, finish, {i: i for i in range(n)})


def gather_two_legs(shards):
    near = gather_near_exchange(shards)
    far = gather_far_exchange(near.outs)

    def finish(src, dst, sm):
        near.finish(src, dst, sm[:3])
        far.start(dst, dst, sm[3:])
        far.finish(dst, dst, sm[3:])

    return Exchange(near.ins, near.outs, list(near.sems) + list(far.sems), lambda src, dst, sm: near.start(src, dst, sm[:3]), finish)


def halves_exchange(grads):
    n = len(grads)

    def copy(g, st, sm, e, x, y, c):
        return _remote(g[e].at[:, 1 - c], st[e], sm[0].at[e], sm[1].at[e], (x, y, 1 - c))

    def start(g, st, sm):
        x, y, c = _place()
        for e in range(n):
            copy(g, st, sm, e, x, y, c).start()

    def finish(g, st, sm):
        x, y, c = _place()
        for e in range(n):
            copy(g, st, sm, e, x, y, c).wait()

    outs = [SDS((4,) + a.shape[2:], a.dtype) for a in grads]
    return Exchange(list(grads), outs, [DMA((n,)), DMA((n,))], start, finish)


def scatter_exchange(parts):
    n = len(parts)

    def ici(p, st, sm, e, j, chip_j, x, y, c):
        k, kj = 2 * x + y, 2 * chip_j[0] + chip_j[1]
        return _remote(p[e].at[kj], st[e].at[c, k], sm[0].at[8 * e + j], sm[1].at[8 * e + j], (*chip_j, c))

    def own(p, st, sm, e, x, y, c):
        k = 2 * x + y
        return _remote(p[e].at[k], st[e].at[c, k], sm[0].at[8 * e + 3], sm[1].at[8 * e + 3], (x, y, 1 - c))

    def forward(st, sm, e, j, chip_j, x, y, c, sender_c):
        kj = 2 * chip_j[0] + chip_j[1]
        r = st[e].at[sender_c, kj]
        return _remote(r, r, sm[0].at[8 * e + 4 + j], sm[1].at[8 * e + 4 + j], (x, y, 1 - c))

    def local(p, st, sm, e, x, y, c):
        k = 2 * x + y
        return pltpu.make_async_copy(p[e].at[k], st[e].at[c, k], sm[2].at[e])

    def start(p, st, sm):
        x, y, c = _place()
        for e in range(n):
            local(p, st, sm, e, x, y, c).start()
            own(p, st, sm, e, x, y, c).start()
            for j, chip_j in enumerate(_other_chips(x, y)):
                ici(p, st, sm, e, j, chip_j, x, y, c).start()

    def finish(p, st, sm):
        x, y, c = _place()
        k = 2 * x + y
        chips = _other_chips(x, y)
        for e in range(n):
            for j, chip_j in enumerate(chips):
                kj = 2 * chip_j[0] + chip_j[1]
                r = st[e].at[c, kj]
                _remote(r, r, sm[0].at[8 * e + j], sm[1].at[8 * e + j], (*chip_j, c)).wait_recv()
                forward(st, sm, e, j, chip_j, x, y, c, c).start()
        for e in range(n):
            r = st[e].at[1 - c, k]
            _remote(r, r, sm[0].at[8 * e + 3], sm[1].at[8 * e + 3], (x, y, 1 - c)).wait_recv()
            for j, chip_j in enumerate(chips):
                forward(st, sm, e, j, chip_j, x, y, c, 1 - c).wait_recv()
        for e in range(n):
            own(p, st, sm, e, x, y, c).wait_send()
            for j, chip_j in enumerate(chips):
                ici(p, st, sm, e, j, chip_j, x, y, c).wait_send()
                forward(st, sm, e, j, chip_j, x, y, c, c).wait_send()
            local(p, st, sm, e, x, y, c).wait()

    outs = [SDS((2,) + a.shape, a.dtype) for a in parts]
    return Exchange(list(parts), outs, [DMA((8 * n,)), DMA((8 * n,)), DMA((n,))], start, finish)


def tail_reduce(d_norm_mix, d_norm_mem, d_norm_ffn, d_gains, d_cw8, d_cbias, d_qg, d_kg, d_mqg, d_mkg, d_sink8, loss8, tail):
    n = len(tail)
    halves = halves_exchange(tail)
    scatter = scatter_exchange([SDS((4,) + a.shape[2:], WIRE) for a in tail])

    def body(nm_ref, nmem_ref, nf_ref, gn_ref, cw_ref, cb_ref, qg_ref, kg_ref, mqg_ref, mkg_ref, sk_ref, ls_ref, *rest):
        g, o_ref, st = rest[:n], rest[n], rest[n + 1:2 * n + 1]
        buf, ssem, rsem = rest[2 * n + 1:2 * n + 4]
        own, sib, part = (rest[2 * n + 4 + i * n:2 * n + 4 + (i + 1) * n] for i in range(3))
        lsem = rest[5 * n + 4]
        hsem, xsem = rest[5 * n + 5:5 * n + 7], rest[5 * n + 7:]
        x, y, c = _place()
        loads = [pltpu.make_async_copy(g[e].at[:, c], own[e], lsem.at[e]) for e in range(n)]
        for ld in loads:
            ld.start()
        halves.start(g, sib, hsem)
        me = 4 * x + 2 * y + c
        mine = buf.at[me]
        mine[...] = jnp.zeros((8, 1024), f32)
        mine[0:1, :] = nm_ref[...]
        mine[1:2, :] = nmem_ref[...]
        mine[2:3, :] = nf_ref[...]
        mine[3:4, :] = gn_ref[...]
        for j in range(3):
            mine[4:5, pl.ds(j * CONV_W, CONV_W)] = cw_ref[j:j + 1, :]
        mine[4:5, pl.ds(3 * CONV_W, CONV_W)] = cb_ref[...]
        for j, r in enumerate((qg_ref, kg_ref, mqg_ref, mkg_ref)):
            mine[5:6, pl.ds(j * HD, HD)] = r[...]
        mine[5:6, pl.ds(256, 128)] = sk_ref[0:1, :]
        mine[5:6, pl.ds(384, 128)] = ls_ref[0:1, :]

        def peer_of(m):
            return (1 - x if m & 4 else x, 1 - y if m & 2 else y, 1 - c if m & 1 else c)

        for m in range(1, 8):
            _remote(mine, mine, ssem.at[m - 1], rsem.at[m - 1], peer_of(m)).start()
        for ld in loads:
            ld.wait()
        halves.finish(g, sib, hsem)
        for e in range(n):
            part[e][...] = (own[e][...] + sib[e][...]).astype(WIRE)
        scatter.start(part, st, xsem)
        scatter.finish(part, st, xsem)
        for m in range(1, 8):
            p = peer_of(m)
            got = buf.at[4 * p[0] + 2 * p[1] + p[2]]
            _remote(got, got, ssem.at[m - 1], rsem.at[m - 1], p).wait_recv()
        for m in range(1, 8):
            _remote(mine, mine, ssem.at[m - 1], rsem.at[m - 1], peer_of(m)).wait_send()
        acc = buf[0]
        for d in range(1, 8):
            acc = acc + buf[d]
        o_ref[...] = acc

    ins = [d_norm_mix, d_norm_mem, d_norm_ffn, d_gains, d_cw8, d_cbias, d_qg, d_kg, d_mqg, d_mkg, d_sink8, loss8]
    half_shape = [(4,) + a.shape[2:] for a in tail]
    scratch = ([pltpu.VMEM((8, 8, 1024), f32), DMA((7,)), DMA((7,))]
               + [pltpu.VMEM(s, f32) for s in half_shape] * 2 + [pltpu.VMEM(s, WIRE) for s in half_shape]
               + [DMA((n,))] + list(halves.sems) + list(scatter.sems))
    res = _run("tail_reduce", body, (), ins + list(tail), [VM] * len(ins) + [ANY] * n,
               [SDS((8, 1024), f32)] + list(scatter.outs), [VM] + [ANY] * n, scratch=scratch, vmem_mib=40)
    return res[0], res[1:]


def add_halves(cidx, grads, stages, name, nch=2):
    n = len(grads)

    def body(c_ref, *refs):
        g, st, o = refs[:n], refs[n:2 * n], refs[2 * n:]
        for e in range(n):
            o[e][...] = (g[e][...] + st[e][...]).astype(WIRE)

    in_specs, out_specs, out_shape = [], [], []
    for a in grads:
        hr, C = a.shape[2], a.shape[3]
        in_specs.append(pl.BlockSpec((None, None, hr // nch, C), lambda s, q, c_ref: (s, c_ref[0], q, 0)))
    for a in stages:
        hr, C = a.shape[1], a.shape[2]
        in_specs.append(pl.BlockSpec((None, hr // nch, C), lambda s, q, c_ref: (s, q, 0)))
        out_specs.append(pl.BlockSpec((None, hr // nch, C), lambda s, q, c_ref: (s, q, 0)))
        out_shape.append(SDS(a.shape, WIRE))
    return pl.pallas_call(
        body, name=name, out_shape=out_shape,
        grid_spec=pltpu.PrefetchScalarGridSpec(num_scalar_prefetch=1, grid=(4, nch), in_specs=in_specs, out_specs=out_specs),
        compiler_params=pltpu.CompilerParams(dimension_semantics=("arbitrary", "arbitrary")),
    )(cidx, *grads, *stages)


def _adamw_math(w, g, m, v):
    m = ADAM_B1 * m + (1.0 - ADAM_B1) * g
    v = ADAM_B2 * v + (1.0 - ADAM_B2) * (g * g)
    m_hat = m / (1.0 - ADAM_B1 ** ADAM_STEP)
    v_hat = v / (1.0 - ADAM_B2 ** ADAM_STEP)
    delta = -ADAM_LR * (m_hat / (jnp.sqrt(v_hat) + ADAM_EPS) + ADAM_WD * w)
    return delta, m, v


def _sum_chips(st):
    return ((st[0].astype(f32) + st[1].astype(f32)) + st[2].astype(f32)) + st[3].astype(f32)


def adamw_big(name, stages, ws, ms, vs, nstep, exchange=None):
    n = len(stages)

    def body(*refs):
        st, w, m, v = refs[:n], refs[n:2 * n], refs[2 * n:3 * n], refs[3 * n:4 * n]
        outs = refs[4 * n:]
        for e in range(n):
            g = jnp.concatenate([_sum_chips(st[e].at[0]), _sum_chips(st[e].at[1])], axis=0)
            d, mm, vv = _adamw_math(w[e][...], g, m[e][...], v[e][...])
            outs[4 * e][...] = g
            outs[4 * e + 1][...] = d
            outs[4 * e + 2][...] = mm
            outs[4 * e + 3][...] = vv

    st_specs, w_specs = [], []
    for e in range(n):
        _, _, hr, C = stages[e].shape
        st_specs.append(pl.BlockSpec((2, 4, hr, C // nstep), lambda i: (0, 0, 0, i)))
        w_specs.append(pl.BlockSpec((2 * hr, C // nstep), lambda i: (0, i)))
    out_specs = [s for s in w_specs for _ in range(4)]
    out_shape = [SDS(w.shape, f32) for w in ws for _ in range(4)]
    res = _run(name, body, (nstep,), list(stages) + list(ws) + list(ms) + list(vs), st_specs + w_specs * 3,
               out_shape, out_specs, vmem_mib=48, exchange=exchange)
    res, sent = res if exchange is not None else (res, None)
    return [res[4 * e:4 * e + 4] for e in range(n)], sent


def adamw_small(tot, pk_w, pk_m, pk_v, shapes):
    def body(tot_ref, w_ref, m_ref, v_ref, *outs):
        x, y, _ = _place()
        chip = 2 * x + y
        taps = []
        for j in range(3):
            mine = tot_ref[4:5, j * CONV_W:j * CONV_W + HD]
            for s in range(1, 4):
                mine = jnp.where(chip == s, tot_ref[4:5, j * CONV_W + s * HD:j * CONV_W + (s + 1) * HD], mine)
            taps.append(mine)
        row4 = jnp.concatenate(taps + [jnp.zeros((1, 3 * CONV_W - 3 * HD), f32), tot_ref[4:5, 3 * CONV_W:]], axis=1)
        tot_v = tot_ref[...]
        row = lax.broadcasted_iota(jnp.int32, tot_v.shape, 0)
        g = jnp.where(row == 4, jnp.broadcast_to(row4, tot_v.shape), tot_v)
        d, mm, vv = _adamw_math(w_ref[...], g, m_ref[...], v_ref[...])
        for i, name in enumerate(SMALL):
            for k, val in enumerate((g, d, mm, vv)):
                if name == "conv_w":
                    outs[4 * i + k][...] = jnp.concatenate([val[4:5, j * HD:(j + 1) * HD] for j in range(3)], axis=0)[None]
                else:
                    r, c0, w = SMALL_AT[name]
                    outs[4 * i + k][...] = val[r:r + 1, c0:c0 + w]

    out_shape = [SDS(shapes[k], f32) for k in SMALL for _ in range(4)]
    res = _run("adamw_small", body, (), [tot, pk_w, pk_m, pk_v], [VM] * 4, out_shape, [VM] * len(out_shape))
    return {k: res[4 * i:4 * i + 4] for i, k in enumerate(SMALL)}


def prep_weights(shards):
    n = len(shards)

    def body(*refs):
        for e in range(n):
            refs[n + e][...] = _c(refs[e][...])

    return _run("prep_weights", body, (), shards, [VM] * n, [SDS(a.shape, MXU) for a in shards], [VM] * n, vmem_mib=48)


def mem_kv_fwd(mem2d, pk, wmkv):
    M, D = mem2d.shape

    def body(m_ref, pk_ref, w_ref, mn_ref, kv_ref, km_ref, vm_ref):
        m = m_ref[...]
        mn = _c(m * _rstd(m) * _small(pk_ref, "norm_mem"))
        mn_ref[...] = mn
        kv = _nn(mn, w_ref[...])
        kv_ref[...] = kv
        kk = kv[:, :MEM_W]
        km_ref[...] = _c(kk * _heads_rstd(kk) * _lanes(_small(pk_ref, "mem_k_norm"), MEM_W))
        vm_ref[...] = _c(kv[:, MEM_W:])

    return _run("mem_kv_fwd", body, (), [mem2d, pk, wmkv], [VM] * 3,
                [SDS((M, D), MXU), SDS((M, 2 * MEM_W), f32), SDS((M, MEM_W), MXU), SDS((M, MEM_W), MXU)], [VM] * 4)


QKV_W = ATT_W + 2 * KV_W + MEM_W


def in_proj_fwd(x2d, pk, winT, tm, exchange):
    T, D = x2d.shape
    P = winT.shape[0]

    def body(x_ref, pk_ref, w_ref, xn_ref, proj_ref, qkv_ref):
        xv = x_ref[...]
        xn = _c(xv * _rstd(xv) * _small(pk_ref, "norm_mix"))
        xn_ref[...] = xn
        proj = _nt(xn, w_ref[...])
        proj_ref[...] = proj
        q, k = proj[:, :ATT_W], proj[:, ATT_W:ATT_W + KV_W]
        qm = proj[:, P - MEM_W:]
        qkv_ref[...] = jnp.concatenate(
            [_c(q * _heads_rstd(q) * _lanes(_small(pk_ref, "q_norm"), ATT_W)),
             _c(k * _heads_rstd(k) * _lanes(_small(pk_ref, "k_norm"), KV_W)),
             _c(proj[:, ATT_W + KV_W:ATT_W + 2 * KV_W]),
             _c(qm * _heads_rstd(qm) * _lanes(_small(pk_ref, "mem_q_norm"), MEM_W))], axis=1)

    return _run("in_proj_fwd", body, (T // tm,), [x2d, pk, winT],
                [pl.BlockSpec((tm, D), lambda i: (i, 0)), VM, VM],
                [SDS((T, D), MXU), SDS((T, P), f32), SDS((T, QKV_W), MXU)],
                [pl.BlockSpec((tm, D), lambda i: (i, 0)), pl.BlockSpec((tm, P), lambda i: (i, 0)),
                 pl.BlockSpec((tm, QKV_W), lambda i: (i, 0))],
                vmem_mib=40, exchange=exchange)


def _swa_bias_table():
    r = np.arange(GQA * BLK)[:, None]
    k = np.arange(2 * BLK)[None, :]
    dist = (r % BLK) + BLK - k
    band = (dist >= 0) & (dist < BLK)
    tab = np.empty((2, N_KV, GQA * BLK, 2 * BLK), np.float32)
    for later in range(2):
        valid = band & ((k >= BLK) | (later == 1))
        for g in range(N_KV):
            slope = 2.0 ** -(g * GQA + r // BLK + 1.0)
            tab[later, g] = np.where(valid, -slope * dist, NEG)
    return jnp.asarray(tab)


def _sink_column(g, sk_ref):
    hrow = lax.broadcasted_iota(jnp.int32, (GQA * BLK, 1), 0) // BLK
    sink = jnp.zeros((GQA * BLK, 1), f32)
    for hh in range(GQA):
        sink = jnp.where(hrow == hh, sk_ref[g * GQA + hh:g * GQA + hh + 1, 0:1], sink)
    return sink


def _stack_heads(v, g):
    return jnp.concatenate([v[:, (g * GQA + hh) * HD:(g * GQA + hh + 1) * HD] for hh in range(GQA)], axis=0)


def attn_fwd(qkv, sink_rows, BL, S, exchange):
    NB = S // BLK
    T = BL * S

    def body(q_ref, kc_ref, kp_ref, vc_ref, vp_ref, sk_ref, tab_ref, o_ref):
        q = q_ref[...]
        k2 = jnp.concatenate([kp_ref[...], kc_ref[...]], axis=0)
        v2 = jnp.concatenate([vp_ref[...], vc_ref[...]], axis=0)
        ones = jnp.ones((2 * BLK, HD), MXU)
        for g in range(N_KV):
            kn, vh = k2[:, g * HD:(g + 1) * HD], v2[:, g * HD:(g + 1) * HD]
            s = _nt(_stack_heads(q, g), kn) * (HD ** -0.5) + tab_ref[g]
            e, es = _exp_scores(s, _sink_column(g, sk_ref))
            eb = _c(e)
            o = _nn(eb, vh) * (1.0 / (_nn(eb, ones) + es))
            for hh in range(GQA):
                o_ref[:, pl.ds((g * GQA + hh) * HD, HD)] = o[hh * BLK:(hh + 1) * BLK]

    cur = lambda col: (lambda b, j: (b * NB + j, col))
    prev = lambda col: (lambda b, j: (b * NB + jnp.maximum(j - 1, 0), col))
    return _run("attn_fwd", body, (BL, NB), [qkv, qkv, qkv, qkv, qkv, sink_rows, _swa_bias_table()],
                [pl.BlockSpec((BLK, ATT_W), cur(0)),
                 pl.BlockSpec((BLK, KV_W), cur(4)), pl.BlockSpec((BLK, KV_W), prev(4)),
                 pl.BlockSpec((BLK, KV_W), cur(5)), pl.BlockSpec((BLK, KV_W), prev(5)),
                 pl.BlockSpec((8, 128), lambda b, j: (0, 0)),
                 pl.BlockSpec((None, N_KV, GQA * BLK, 2 * BLK), lambda b, j: (jnp.minimum(j, 1), 0, 0, 0))],
                [SDS((T, ATT_W), f32)], [pl.BlockSpec((BLK, ATT_W), cur(0))], exchange=exchange)


def _conv_taps(u, uh):
    row = lax.broadcasted_iota(jnp.int32, u.shape, 0)
    u1 = jnp.where(row == 0, uh[7:8, :], pltpu.roll(u, 1, 0))
    u2 = jnp.where(row == 0, uh[6:7, :], jnp.where(row == 1, uh[7:8, :], pltpu.roll(u, 2, 0)))
    return u1, u2


def _mem_head(qm, km, vm, h):
    qh, kh, vh = (a[:, h * HD:(h + 1) * HD] for a in (qm, km, vm))
    e, _ = _exp_scores(_nt(qh, kh) * (HD ** -0.5))
    return qh, kh, vh, e


def mixer_tail_fwd(x2d, attn_out, proj, qkv, km, vm, conv_w8, pk, wout, S, tm, exchange):
    T, D = x2d.shape
    NM = km.shape[0] // (T // S)

    def body(x_ref, ao_ref, ch_ref, cb_ref, cc_ref, chh_ref, cch_ref, qm_ref, km_ref, vm_ref, cw_ref, pk_ref,
             wout_ref, co_ref, mo_ref, mg_ref, x1_ref, h_ref):
        first = (pl.program_id(0) * tm) % S == 0
        u = cc_ref[...] * ch_ref[...]
        uh = jnp.where(first, 0.0, cch_ref[...] * chh_ref[...])
        u1, u2 = _conv_taps(u, uh)
        conv = cw_ref[0:1, :] * u2 + cw_ref[1:2, :] * u1 + cw_ref[2:3, :] * u + _small(pk_ref, "conv_b")
        conv_out = cb_ref[...] * conv
        co_ref[...] = conv_out
        qm, kmv, vmv = qm_ref[...], km_ref[...], vm_ref[...]
        ones = jnp.ones((NM, HD), MXU)
        for h in range(N_MEMH):
            _, _, vh, e = _mem_head(qm, kmv, vmv, h)
            eb = _c(e)
            mo_ref[:, pl.ds(h * HD, HD)] = _nn(eb, vh) * (1.0 / _nn(eb, ones))
        mem_out = mo_ref[...]
        ao = ao_ref[...]
        merged = _c(jnp.concatenate([ao * _rstd(ao) * _small(pk_ref, "out_norm_attn"),
                                     conv_out * _rstd(conv_out) * _small(pk_ref, "out_norm_conv"),
                                     mem_out * _rstd(mem_out) * _small(pk_ref, "out_norm_mem")], axis=1))
        mg_ref[...] = merged
        x1 = x_ref[...] + _nn(merged, wout_ref[...])
        x1_ref[...] = x1
        h_ref[...] = _c(x1 * _rstd(x1) * _small(pk_ref, "norm_ffn"))

    tile = lambda w, col: pl.BlockSpec((tm, w), lambda i: (i, col))
    halo = lambda col: pl.BlockSpec((8, CONV_W), lambda i: (jnp.maximum(i * (tm // 8) - 1, 0), col))
    seq = pl.BlockSpec((NM, MEM_W), lambda i: ((i * tm) // S, 0))
    small = lambda a: pl.BlockSpec(a.shape, lambda i: (0, 0))
    return _run("mixer_tail_fwd", body, (T // tm,),
                [x2d, attn_out, proj, proj, proj, proj, proj, qkv, km, vm, conv_w8, pk, wout],
                [tile(D, 0), tile(ATT_W, 0), tile(CONV_W, 3), tile(CONV_W, 4), tile(CONV_W, 5), halo(3), halo(5),
                 tile(MEM_W, 3), seq, seq, VM, VM, VM],
                [SDS((T, CONV_W), f32), SDS((T, MEM_W), f32), SDS((T, D), MXU), SDS((T, D), f32), SDS((T, D), MXU)],
                [tile(CONV_W, 0), tile(MEM_W, 0), tile(D, 0), tile(D, 0), tile(D, 0)], vmem_mib=40, exchange=exchange)


def ffn_fwd_bwd(h, x1, tgt, wgT, wuT, wd, pk, tm):
    T, D = x1.shape
    F = wd.shape[0]

    def body(h_ref, x1_ref, t_ref, wg_ref, wu_ref, wd_ref, pk_ref,
             dx1_ref, dx2_ref, act_ref, dg_ref, du_ref, loss_ref, dgf_ref):
        @pl.when(pl.program_id(0) == 0)
        def _():
            loss_ref[...] = jnp.zeros_like(loss_ref)
            dgf_ref[...] = jnp.zeros_like(dgf_ref)

        hv = h_ref[...]
        gate = _nt(hv, wg_ref[...])
        up = _nt(hv, wu_ref[...])
        sg = jax.nn.sigmoid(gate)
        sl = gate * sg
        act = _c(sl * up)
        act_ref[...] = act
        x1v = x1_ref[...]
        diff = (x1v + _nn(act, wd_ref[...])) - t_ref[...]
        loss_ref[...] += 0.5 * jnp.sum(jnp.sum(diff * diff, axis=-1, keepdims=True) / D, axis=0, keepdims=True)
        dx2 = diff / D
        dx2b = _c(dx2)
        dx2_ref[...] = dx2b
        d_act = _nt(dx2b, wd_ref[...])
        d_up = _c(d_act * sl)
        d_gate = _c(d_act * up * (sg * (1.0 + gate * (1.0 - sg))))
        du_ref[...] = d_up
        dg_ref[...] = d_gate
        dh = _nn(d_gate, wg_ref[...]) + _nn(d_up, wu_ref[...])
        dv, dgf = _norm_bwd(dh, x1v, _rstd(x1v), _small(pk_ref, "norm_ffn"))
        dx1_ref[...] = dx2 + dv
        dgf_ref[...] += dgf

    tile = lambda w: pl.BlockSpec((tm, w), lambda i: (i, 0))
    return _run("ffn_fwd_bwd", body, (T // tm,), [h, x1, tgt, wgT, wuT, wd, pk],
                [tile(D), tile(D), tile(D), VM, VM, VM, VM],
                [SDS((T, D), f32), SDS((T, D), MXU), SDS((T, F), MXU), SDS((T, F), MXU), SDS((T, F), MXU),
                 SDS((8, 128), f32), SDS((1, D), f32)],
                [tile(D), tile(D), tile(F), tile(F), tile(F), pl.BlockSpec((8, 128), lambda i: (0, 0)),
                 pl.BlockSpec((1, D), lambda i: (0, 0))], vmem_mib=56)


def matmul_tn(a, b, name, tmo, tk):
    T, M = a.shape
    N = b.shape[1]

    def body(a_ref, b_ref, o_ref):
        @pl.when(pl.program_id(1) == 0)
        def _():
            o_ref[...] = jnp.zeros_like(o_ref)

        o_ref[...] += _tn(a_ref[...], b_ref[...])

    return _run(name, body, (M // tmo, T // tk), [a, b],
                [pl.BlockSpec((tk, tmo), lambda m, k: (k, m)), pl.BlockSpec((tk, N), lambda m, k: (k, 0))],
                [SDS((M, N), f32)], [pl.BlockSpec((tmo, N), lambda m, k: (m, 0))], vmem_mib=48)[0]


def out_proj_bwd(dx1, merged, attn_out, conv_out, mem_out, pk, wout, tm):
    T, D = dx1.shape

    def body(dx1_ref, mg_ref, ao_ref, co_ref, mo_ref, pk_ref, w_ref,
             dao_ref, dco_ref, dmo_ref, dw_ref, dgain_ref):
        @pl.when(pl.program_id(0) == 0)
        def _():
            dw_ref[...] = jnp.zeros_like(dw_ref)
            dgain_ref[...] = jnp.zeros_like(dgain_ref)

        dxb = _c(dx1_ref[...])
        dw_ref[...] += _tn(mg_ref[...], dxb)
        dmg = _nt(dxb, w_ref[...])
        ao, co, mo = ao_ref[...], co_ref[...], mo_ref[...]
        da, ga = _norm_bwd(dmg[:, :ATT_W], ao, _rstd(ao), _small(pk_ref, "out_norm_attn"))
        dc, gc = _norm_bwd(dmg[:, ATT_W:ATT_W + CONV_W], co, _rstd(co), _small(pk_ref, "out_norm_conv"))
        dm, gm = _norm_bwd(dmg[:, ATT_W + CONV_W:], mo, _rstd(mo), _small(pk_ref, "out_norm_mem"))
        dao_ref[...] = da
        dco_ref[...] = dc
        dmo_ref[...] = dm
        dgain_ref[...] += jnp.concatenate([ga, gc, gm], axis=1)

    tile = lambda w: pl.BlockSpec((tm, w), lambda i: (i, 0))
    return _run("out_proj_bwd", body, (T // tm,), [dx1, merged, attn_out, conv_out, mem_out, pk, wout],
                [tile(D), tile(D), tile(ATT_W), tile(CONV_W), tile(MEM_W), VM, VM],
                [SDS((T, ATT_W), f32), SDS((T, CONV_W), f32), SDS((T, MEM_W), f32), SDS((D, D), f32), SDS((1, D), f32)],
                [tile(ATT_W), tile(CONV_W), tile(MEM_W), pl.BlockSpec((D, D), lambda i: (0, 0)),
                 pl.BlockSpec((1, D), lambda i: (0, 0))], vmem_mib=40)


def attn_bwd(qkv, d_attn, attn_out, sink_rows, BL, S, exchange):
    NB = S // BLK
    T = BL * S

    def body(q_ref, kc_ref, kp_ref, vc_ref, vp_ref, do_ref, ao_ref, sk_ref, tab_ref,
             dq_ref, dk_ref, dv_ref, dsk_ref, pend_k, pend_v):
        b, j = pl.program_id(0), pl.program_id(1)

        @pl.when((b == 0) & (j == 0))
        def _():
            dsk_ref[...] = jnp.zeros_like(dsk_ref)

        @pl.when(j == 0)
        def _():
            pend_k[...] = jnp.zeros_like(pend_k)
            pend_v[...] = jnp.zeros_like(pend_v)

        @pl.when(j < NB)
        def _():
            q, do, ao = q_ref[...], do_ref[...], ao_ref[...]
            k2 = jnp.concatenate([kp_ref[...], kc_ref[...]], axis=0)
            v2 = jnp.concatenate([vp_ref[...], vc_ref[...]], axis=0)
            lane = lax.broadcasted_iota(jnp.int32, (8, 128), 1)
            ones_w = jnp.ones((2 * BLK, 2 * BLK), MXU)
            dsk = jnp.zeros((8, 128), f32)
            dks, dvs = [], []
            for g in range(N_KV):
                kn, vh = k2[:, g * HD:(g + 1) * HD], v2[:, g * HD:(g + 1) * HD]
                qs = _stack_heads(q, g)
                s = _nt(qs, kn) * (HD ** -0.5) + tab_ref[g]
                e, es = _exp_scores(s, _sink_column(g, sk_ref))
                eb = _c(e)
                inv_w = 1.0 / (_nn(eb, ones_w) + es)
                inv_n = inv_w[:, :HD]
                dos = _stack_heads(do, g)
                delta = _rowsum_mxu(dos * _stack_heads(ao, g), 2 * BLK)
                dp = _nt(_c(dos), vh)
                ds = _c(e * inv_w * (dp - delta) * (HD ** -0.5))
                t = es * inv_n[:, 0:1] * delta[:, 0:1]
                for hh in range(GQA):
                    dsk = dsk + jnp.where(lane == g * GQA + hh, -jnp.sum(t[hh * BLK:(hh + 1) * BLK]), 0.0)
                dvs.append(_tn(eb, _c(dos * inv_n)))
                dks.append(_tn(ds, qs))
                dqs = _nn(ds, kn)
                for hh in range(GQA):
                    dq_ref[:, pl.ds((g * GQA + hh) * HD, HD)] = dqs[hh * BLK:(hh + 1) * BLK]
            dk2 = jnp.concatenate(dks, axis=1)
            dv2 = jnp.concatenate(dvs, axis=1)
            dk_ref[...] = pend_k[...] + dk2[:BLK]
            dv_ref[...] = pend_v[...] + dv2[:BLK]
            pend_k[...] = dk2[BLK:]
            pend_v[...] = dv2[BLK:]
            dsk_ref[...] += dsk

        @pl.when(j == NB)
        def _():
            dk_ref[...] = pend_k[...]
            dv_ref[...] = pend_v[...]

    cur = lambda col: (lambda b, j: (b * NB + jnp.minimum(j, NB - 1), col))
    prev = lambda col: (lambda b, j: (b * NB + jnp.maximum(j - 1, 0), col))
    small = lambda shape: pl.BlockSpec(shape, lambda b, j: (0, 0))
    return _run("attn_bwd", body, (BL, NB + 1), [qkv, qkv, qkv, qkv, qkv, d_attn, attn_out, sink_rows, _swa_bias_table()],
                [pl.BlockSpec((BLK, ATT_W), cur(0)),
                 pl.BlockSpec((BLK, KV_W), cur(4)), pl.BlockSpec((BLK, KV_W), prev(4)),
                 pl.BlockSpec((BLK, KV_W), cur(5)), pl.BlockSpec((BLK, KV_W), prev(5)),
                 pl.BlockSpec((BLK, ATT_W), cur(0)), pl.BlockSpec((BLK, ATT_W), cur(0)), small((8, 128)),
                 pl.BlockSpec((None, N_KV, GQA * BLK, 2 * BLK), lambda b, j: (jnp.minimum(j, 1), 0, 0, 0))],
                [SDS((T, ATT_W), f32), SDS((T, KV_W), f32), SDS((T, KV_W), f32), SDS((8, 128), f32)],
                [pl.BlockSpec((BLK, ATT_W), cur(0)), pl.BlockSpec((BLK, KV_W), prev(0)),
                 pl.BlockSpec((BLK, KV_W), prev(0)), small((8, 128))],
                scratch=[pltpu.VMEM((BLK, KV_W), f32)] * 2, exchange=exchange)


def mem_conv_bwd(d_mem_out, mem_out, d_conv_out, proj, qkv, km, vm, conv_w8, pk, S, tm, exchange):
    T = d_mem_out.shape[0]
    NM = km.shape[0] // (T // S)

    def body(dmo_ref, mo_ref, dco_ref, ch_ref, cb_ref, cc_ref, chh_ref, cch_ref, qm_ref, km_ref, vm_ref, cw_ref,
             pk_ref, dqm_ref, dkm_ref, dvm_ref, dcb_ref, dcv_ref, dcw_ref, dcbias_ref):
        i = pl.program_id(0)
        first = (i * tm) % S == 0

        @pl.when(i == 0)
        def _():
            dcw_ref[...] = jnp.zeros_like(dcw_ref)
            dcbias_ref[...] = jnp.zeros_like(dcbias_ref)

        @pl.when(first)
        def _():
            dkm_ref[...] = jnp.zeros_like(dkm_ref)
            dvm_ref[...] = jnp.zeros_like(dvm_ref)

        qm, kmv, vmv, dmo, mo = qm_ref[...], km_ref[...], vm_ref[...], dmo_ref[...], mo_ref[...]
        ones_w = jnp.ones((NM, NM), MXU)
        for h in range(N_MEMH):
            qh, kh, vh, e = _mem_head(qm, kmv, vmv, h)
            eb = _c(e)
            doh = dmo[:, h * HD:(h + 1) * HD]
            delta = _rowsum_mxu(doh * mo[:, h * HD:(h + 1) * HD], NM)
            dp = _nt(_c(doh), vh)
            inv_w = 1.0 / _nn(eb, ones_w)
            ds = _c(e * inv_w * (dp - delta) * (HD ** -0.5))
            dvm_ref[:, pl.ds(h * HD, HD)] += _tn(eb, _c(doh * inv_w[:, :HD]))
            dkm_ref[:, pl.ds(h * HD, HD)] += _tn(ds, qh)
            dqm_ref[:, pl.ds(h * HD, HD)] = _nn(ds, kh)

        u = cc_ref[...] * ch_ref[...]
        uh = jnp.where(first, 0.0, cch_ref[...] * chh_ref[...])
        u1, u2 = _conv_taps(u, uh)
        conv = cw_ref[0:1, :] * u2 + cw_ref[1:2, :] * u1 + cw_ref[2:3, :] * u + _small(pk_ref, "conv_b")
        dy = dco_ref[...]
        dcb_ref[...] = dy * conv
        dcv = dy * cb_ref[...]
        dcv_ref[...] = dcv
        dcbias_ref[...] += jnp.sum(dcv, axis=0, keepdims=True)
        dcw_ref[0:1, :] += jnp.sum(dcv * u2, axis=0, keepdims=True)
        dcw_ref[1:2, :] += jnp.sum(dcv * u1, axis=0, keepdims=True)
        dcw_ref[2:3, :] += jnp.sum(dcv * u, axis=0, keepdims=True)

    tile = lambda w, col: pl.BlockSpec((tm, w), lambda i: (i, col))
    halo = lambda col: pl.BlockSpec((8, CONV_W), lambda i: (jnp.maximum(i * (tm // 8) - 1, 0), col))
    seq = pl.BlockSpec((NM, MEM_W), lambda i: ((i * tm) // S, 0))
    const = lambda shape: pl.BlockSpec(shape, lambda i: (0, 0))
    return _run("mem_conv_bwd", body, (T // tm,),
                [d_mem_out, mem_out, d_conv_out, proj, proj, proj, proj, proj, qkv, km, vm, conv_w8, pk],
                [tile(MEM_W, 0), tile(MEM_W, 0), tile(CONV_W, 0), tile(CONV_W, 3), tile(CONV_W, 4), tile(CONV_W, 5),
                 halo(3), halo(5), tile(MEM_W, 3), seq, seq, VM, VM],
                [SDS((T, MEM_W), f32), SDS(km.shape, f32), SDS(km.shape, f32),
                 SDS((T, CONV_W), f32), SDS((T, CONV_W), f32), SDS((8, CONV_W), f32), SDS((1, CONV_W), f32)],
                [tile(MEM_W, 0), seq, seq, tile(CONV_W, 0), tile(CONV_W, 0), const((8, CONV_W)), const((1, CONV_W))],
                exchange=exchange)


def in_proj_bwd(dqn, dkn, dv, dcb, dcv, dqmn, proj, conv_w8, xn, x2d, dx1, pk, winT, S, tm):
    T, D = x2d.shape
    P = winT.shape[0]
    last_blk = T // 8 - 1

    def body(dq_ref, dk_ref, dv_ref, dcb_ref, dcv_ref, dcvn_ref, dqm_ref, qa_ref, ka_ref, ch_ref, cc_ref, qma_ref,
             cw_ref, xn_ref, x_ref, dx1_ref, pk_ref, w_ref,
             dx_ref, dw_ref, dg_ref, dqg_ref, dkg_ref, dmqg_ref):
        i = pl.program_id(0)

        @pl.when(i == 0)
        def _():
            dw_ref[...] = jnp.zeros_like(dw_ref)
            dg_ref[...] = jnp.zeros_like(dg_ref)
            dqg_ref[...] = jnp.zeros_like(dqg_ref)
            dkg_ref[...] = jnp.zeros_like(dkg_ref)
            dmqg_ref[...] = jnp.zeros_like(dmqg_ref)

        dqa, gq = _heads_norm_bwd(dq_ref[...], qa_ref[...], _small(pk_ref, "q_norm"))
        dka, gk = _heads_norm_bwd(dk_ref[...], ka_ref[...], _small(pk_ref, "k_norm"))
        dqma, gmq = _heads_norm_bwd(dqm_ref[...], qma_ref[...], _small(pk_ref, "mem_q_norm"))
        dqg_ref[...] += gq
        dkg_ref[...] += gk
        dmqg_ref[...] += gmq

        last = ((i + 1) * tm) % S == 0
        dcv = dcv_ref[...]
        nxt = jnp.where(last, 0.0, dcvn_ref[...])
        row = lax.broadcasted_iota(jnp.int32, dcv.shape, 0)
        n1 = jnp.where(row == tm - 1, nxt[0:1, :], pltpu.roll(dcv, tm - 1, 0))
        n2 = jnp.where(row == tm - 2, nxt[0:1, :], jnp.where(row == tm - 1, nxt[1:2, :], pltpu.roll(dcv, tm - 2, 0)))
        du = cw_ref[2:3, :] * dcv + cw_ref[1:2, :] * n1 + cw_ref[0:1, :] * n2
        d_proj = jnp.concatenate([_c(dqa), _c(dka), _c(dv_ref[...]), _c(du * cc_ref[...]),
                                  _c(dcb_ref[...]), _c(du * ch_ref[...]), _c(dqma)], axis=1)
        dw_ref[...] += _tn(d_proj, xn_ref[...])
        xv = x_ref[...]
        dv_, dg = _norm_bwd(_nn(d_proj, w_ref[...]), xv, _rstd(xv), _small(pk_ref, "norm_mix"))
        dx_ref[...] = dx1_ref[...] + dv_
        dg_ref[...] += dg

    tile = lambda w, col=0: pl.BlockSpec((tm, w), lambda i: (i, col))
    nhalo = pl.BlockSpec((8, CONV_W), lambda i: (jnp.minimum((i + 1) * (tm // 8), last_blk), 0))
    const = lambda shape: pl.BlockSpec(shape, lambda i: (0, 0))
    return _run("in_proj_bwd", body, (T // tm,),
                [dqn, dkn, dv, dcb, dcv, dcv, dqmn, proj, proj, proj, proj, proj, conv_w8, xn, x2d, dx1, pk, winT],
                [tile(ATT_W), tile(KV_W), tile(KV_W), tile(CONV_W), tile(CONV_W), nhalo, tile(MEM_W),
                 tile(ATT_W, 0), tile(KV_W, 4), tile(CONV_W, 3), tile(CONV_W, 5), tile(MEM_W, 6), VM,
                 tile(D), tile(D), tile(D), VM, VM],
                [SDS((T, D), f32), SDS((P, D), f32), SDS((1, D), f32), SDS((1, HD), f32), SDS((1, HD), f32),
                 SDS((1, HD), f32)],
                [tile(D), pl.BlockSpec((P, D), lambda i: (0, 0)), const((1, D)), const((1, HD)), const((1, HD)),
                 const((1, HD))],
                vmem_mib=48)


def mem_kv_bwd(dkm, dvm, kv, memn, mem2d, pk, wmkv):
    def body(dkm_ref, dvm_ref, kv_ref, mn_ref, m_ref, pk_ref, w_ref, dw_ref, dg_ref, dkg_ref):
        dkk, dkg = _heads_norm_bwd(dkm_ref[...], kv_ref[:, :MEM_W], _small(pk_ref, "mem_k_norm"))
        dkg_ref[...] = dkg
        dkv = _c(jnp.concatenate([dkk, dvm_ref[...]], axis=1))
        dw_ref[...] = _tn(mn_ref[...], dkv)
        mv = m_ref[...]
        dg_ref[...] = jnp.sum(_nt(dkv, w_ref[...]) * mv * _rstd(mv), axis=0, keepdims=True)

    return _run("mem_kv_bwd", body, (), [dkm, dvm, kv, memn, mem2d, pk, wmkv], [VM] * 7,
                [SDS(wmkv.shape, f32), SDS((1, mem2d.shape[1]), f32), SDS((1, HD), f32)], [VM] * 3, vmem_mib=40)


def _halves_view(g):
    return g.reshape(4, 2, g.shape[0] // 8, g.shape[1])


def kernel(x, mem, norm_mix, w_in, q_norm, k_norm, attn_sinks, conv_w, conv_b, norm_mem, w_mem_kv, mem_q_norm, mem_k_norm, out_norm_attn, out_norm_conv, out_norm_mem, w_out, norm_ffn, w_gate, w_up, w_down, loss_target, m_norm_mix, m_w_in, m_q_norm, m_k_norm, m_attn_sinks, m_conv_w, m_conv_b, m_norm_mem, m_w_mem_kv, m_mem_q_norm, m_mem_k_norm, m_out_norm_attn, m_out_norm_conv, m_out_norm_mem, m_w_out, m_norm_ffn, m_w_gate, m_w_up, m_w_down, v_norm_mix, v_w_in, v_q_norm, v_k_norm, v_attn_sinks, v_conv_w, v_conv_b, v_norm_mem, v_w_mem_kv, v_mem_q_norm, v_mem_k_norm, v_out_norm_attn, v_out_norm_conv, v_out_norm_mem, v_w_out, v_norm_ffn, v_w_gate, v_w_up, v_w_down):
    BL, S, D = x.shape
    T = BL * S
    TM = 256
    _, _, ci = _place()
    cidx = ci.reshape(1).astype(jnp.int32)
    w_small = dict(norm_mix=norm_mix, norm_mem=norm_mem, norm_ffn=norm_ffn, out_norm_attn=out_norm_attn,
                   out_norm_conv=out_norm_conv, out_norm_mem=out_norm_mem, conv_w=conv_w, conv_b=conv_b, q_norm=q_norm,
                   k_norm=k_norm, mem_q_norm=mem_q_norm, mem_k_norm=mem_k_norm, attn_sinks=attn_sinks)
    m_small = dict(norm_mix=m_norm_mix, norm_mem=m_norm_mem, norm_ffn=m_norm_ffn, out_norm_attn=m_out_norm_attn,
                   out_norm_conv=m_out_norm_conv, out_norm_mem=m_out_norm_mem, conv_w=m_conv_w, conv_b=m_conv_b,
                   q_norm=m_q_norm, k_norm=m_k_norm, mem_q_norm=m_mem_q_norm, mem_k_norm=m_mem_k_norm,
                   attn_sinks=m_attn_sinks)
    v_small = dict(norm_mix=v_norm_mix, norm_mem=v_norm_mem, norm_ffn=v_norm_ffn, out_norm_attn=v_out_norm_attn,
                   out_norm_conv=v_out_norm_conv, out_norm_mem=v_out_norm_mem, conv_w=v_conv_w, conv_b=v_conv_b,
                   q_norm=v_q_norm, k_norm=v_k_norm, mem_q_norm=v_mem_q_norm, mem_k_norm=v_mem_k_norm,
                   attn_sinks=v_attn_sinks)
    pk = _pack_small(w_small)

    rowblocks = lambda a, b, c, d, e, f: [a[0].T, b[0].T, c[0].T, d[0], e[0], f[0]]
    w_rb = rowblocks(w_in, w_gate, w_up, w_down, w_out, w_mem_kv)
    m_rb = rowblocks(m_w_in, m_w_gate, m_w_up, m_w_down, m_w_out, m_w_mem_kv)
    v_rb = rowblocks(v_w_in, v_w_gate, v_w_up, v_w_down, v_w_out, v_w_mem_kv)
    winT_s, wgT_s, wuT_s, wd_s, wout_s, wmkv_s = prep_weights(w_rb)
    cw_pad = jnp.zeros((8, 128), f32).at[:3, :HD].set(conv_w[0])
    _, (winT, cw_all) = _run("gather_w_in", None, (), [], [], [], [],
                             exchange=_together([gather_two_legs([winT_s]), gather_exchange([cw_pad], [False])]))
    conv_w_full = jnp.transpose(cw_all.reshape(4, 8, 128)[:, :3, :HD], (1, 0, 2)).reshape(3, CONV_W)
    conv_w8 = jnp.zeros((8, CONV_W), f32).at[:3].set(conv_w_full)
    sink_rows = jnp.broadcast_to(attn_sinks.reshape(N_Q, 1), (N_Q, 128))

    x2d = x.reshape(T, D)
    mem2d = mem.reshape(-1, D)
    (xn, proj, qkv), near1 = in_proj_fwd(x2d, pk, winT, TM, gather_near_exchange([wgT_s, wout_s, wmkv_s]))
    (attn_out,), (wgT, wout, wmkv, *near2) = attn_fwd(
        qkv, sink_rows, BL, S, _together([gather_far_exchange(near1), gather_near_exchange([wuT_s, wd_s])]))
    memn, kv, km, vm = mem_kv_fwd(mem2d, pk, wmkv)
    (conv_out, mem_out, merged, x1, h), (wuT, wd) = mixer_tail_fwd(
        x2d, attn_out, proj, qkv, km, vm, conv_w8, pk, wout, S, TM, gather_far_exchange(near2))

    dx1, dx2b, act, d_gate, d_up, loss8, d_norm_ffn = ffn_fwd_bwd(h, x1, loss_target.reshape(T, D), wgT, wuT, wd, pk, TM)
    F = wd.shape[0]
    g_wd = matmul_tn(act, dx2b, "dw_down", F // 2, min(T, 1024))
    g_wgT = matmul_tn(d_gate, h, "dw_gate", F // 2, min(T, 1024))
    g_wuT = matmul_tn(d_up, h, "dw_up", F // 2, min(T, 1024))

    d_attn, d_conv_out, d_mem_out, g_wout, d_gains = out_proj_bwd(dx1, merged, attn_out, conv_out, mem_out, pk, wout, TM)
    late = [_halves_view(g) for g in (g_wgT, g_wuT, g_wd, g_wout)]
    (dqmn, dkm, dvm, dcb, dcv, d_cw8, d_cbias), late_sib = mem_conv_bwd(
        d_mem_out, mem_out, d_conv_out, proj, qkv, km, vm, conv_w8, pk, S, TM, halves_exchange(late))
    late_part = add_halves(cidx, late, late_sib, "grad_add_halves_ffn")
    (dqn, dkn, dv, d_sink8), late_stage = attn_bwd(qkv, d_attn, attn_out, sink_rows, BL, S, scatter_exchange(late_part))
    g_x, g_winT, d_norm_mix, d_qg, d_kg, d_mqg = in_proj_bwd(
        dqn, dkn, dv, dcb, dcv, dqmn, proj, conv_w8, xn, x2d, dx1, pk, winT, S, TM)
    g_wmkv, d_norm_mem, d_mkg = mem_kv_bwd(dkm, dvm, kv, memn, mem2d, pk, wmkv)

    tot, tail_stage = tail_reduce(d_norm_mix, d_norm_mem, d_norm_ffn, d_gains, d_cw8, d_cbias, d_qg, d_kg, d_mqg, d_mkg,
                                  d_sink8, loss8, [_halves_view(g) for g in (g_winT, g_wmkv)])
    loss = tot[5, 384]
    late_res, _ = adamw_big("adamw_late", late_stage, w_rb[1:5], m_rb[1:5], v_rb[1:5], 8)
    tail_res, _ = adamw_big("adamw_tail", tail_stage, [w_rb[0], w_rb[5]], [m_rb[0], m_rb[5]], [v_rb[0], v_rb[5]], 4)
    res = {"w_in": [a.T[None] for a in tail_res[0]], "w_gate": [a.T[None] for a in late_res[0]],
           "w_up": [a.T[None] for a in late_res[1]], "w_down": [a[None] for a in late_res[2]],
           "w_out": [a[None] for a in late_res[3]], "w_mem_kv": [a[None] for a in tail_res[1]]}
    res.update(adamw_small(tot, pk, _pack_small(m_small), _pack_small(v_small), {k: w_small[k].shape for k in SMALL}))

    order = ["norm_mix", "w_in", "q_norm", "k_norm", "attn_sinks", "conv_w", "conv_b", "norm_mem", "w_mem_kv",
             "mem_q_norm", "mem_k_norm", "out_norm_attn", "out_norm_conv", "out_norm_mem", "w_out", "norm_ffn",
             "w_gate", "w_up", "w_down"]
    return (loss, g_x.reshape(BL, S, D), *[res[n][0] for n in order], *[res[n][1] for n in order],
            *[res[n][2] for n in order], *[res[n][3] for n in order])
```

```python
import collections
import functools

import jax
import jax.numpy as jnp
import numpy as np
from jax import lax
from jax.experimental import pallas as pl
from jax.experimental.pallas import tpu as pltpu

f32 = jnp.float32
MXU = jnp.bfloat16
WIRE = jnp.bfloat16
EPS = 1e-6
NEG = -1e30
HD = 64
BLK = 128
N_Q, N_KV, N_MEMH = 8, 2, 4
GQA = N_Q // N_KV
ATT_W, KV_W, CONV_W, MEM_W = 512, 128, 256, 256
VMEM_MIB = 1024 * 1024
ADAM_LR, ADAM_B1, ADAM_B2, ADAM_EPS, ADAM_WD, ADAM_STEP = 0.001, 0.9, 0.999, 1e-08, 0.01, 10

MESH = pl.DeviceIdType.MESH
VM = pl.BlockSpec(memory_space=pltpu.VMEM)
ANY = pl.BlockSpec(memory_space=pl.ANY)
SDS = jax.ShapeDtypeStruct
DMA = pltpu.SemaphoreType.DMA


def _c(v):
    return v.astype(MXU)


def _nn(a, b):
    return lax.dot_general(a, b, (((1,), (0,)), ((), ())), preferred_element_type=f32)


def _nt(a, b):
    return lax.dot_general(a, b, (((1,), (1,)), ((), ())), preferred_element_type=f32)


def _tn(a, b):
    return lax.dot_general(a, b, (((0,), (0,)), ((), ())), preferred_element_type=f32)


def _rstd(v):
    return lax.rsqrt(jnp.mean(v * v, axis=-1, keepdims=True) + EPS)


def _norm_bwd(dy, v, r, g):
    dyg = dy * g
    dv = r * dyg - v * (r * r * r) * jnp.mean(dyg * v, axis=-1, keepdims=True)
    return dv, jnp.sum(dy * v * r, axis=0, keepdims=True)


def _split3(v):
    hi = _c(v)
    r1 = v - hi.astype(f32)
    mid = _c(r1)
    return hi, mid, _c(r1 - mid.astype(f32))


def _rowsum_mxu(v, width):
    ones = jnp.ones((v.shape[1], width), MXU)
    return sum(_nn(a, ones) for a in _split3(v))


def _seg_sums(v):
    r = lax.broadcasted_iota(jnp.int32, (2 * HD, 2 * HD), 0) // HD
    c = lax.broadcasted_iota(jnp.int32, (2 * HD, 2 * HD), 1) // HD
    bd = (r == c).astype(MXU)
    outs = []
    for b in range(v.shape[1] // (2 * HD)):
        outs.append(sum(_nn(a, bd) for a in _split3(v[:, b * 2 * HD:(b + 1) * 2 * HD])))
    return outs[0] if len(outs) == 1 else jnp.concatenate(outs, axis=1)


def _lanes(g, width):
    return jnp.concatenate([g] * (width // HD), axis=1)


def _heads_rstd(v):
    return lax.rsqrt(_seg_sums(v * v) * (1.0 / HD) + EPS)


def _heads_norm_bwd(dy, v, g):
    r = _heads_rstd(v)
    gl = _lanes(g, v.shape[1])
    dyg = dy * gl
    dv = r * dyg - v * (r * r * r) * (_seg_sums(dyg * v) * (1.0 / HD))
    dgl = jnp.sum(dy * v * r, axis=0, keepdims=True)
    return dv, sum(dgl[:, s * HD:(s + 1) * HD] for s in range(v.shape[1] // HD))


def _exp_scores(s, extra=None):
    m = jnp.max(s, axis=-1, keepdims=True)
    if extra is None:
        return jnp.exp(s - m), None
    m = jnp.maximum(m, extra)
    return jnp.exp(s - m), jnp.exp(extra - m)


def _place():
    return lax.axis_index("x"), lax.axis_index("y"), lax.axis_index("c")


SMALL_AT = {"norm_mix": (0, 0, 1024), "norm_mem": (1, 0, 1024), "norm_ffn": (2, 0, 1024),
            "out_norm_attn": (3, 0, ATT_W), "out_norm_conv": (3, ATT_W, CONV_W), "out_norm_mem": (3, ATT_W + CONV_W, MEM_W),
            "conv_b": (4, 3 * CONV_W, CONV_W), "q_norm": (5, 0, HD), "k_norm": (5, HD, HD), "mem_q_norm": (5, 2 * HD, HD),
            "mem_k_norm": (5, 3 * HD, HD), "attn_sinks": (5, 256, N_Q)}
SMALL = ("norm_mix", "norm_mem", "norm_ffn", "out_norm_attn", "out_norm_conv", "out_norm_mem", "conv_w", "conv_b",
         "q_norm", "k_norm", "mem_q_norm", "mem_k_norm", "attn_sinks")


def _small(pk_ref, name):
    r, c0, w = SMALL_AT[name]
    return pk_ref[r:r + 1, c0:c0 + w]


def _pack_small(d):
    z = lambda n: jnp.zeros((1, n), f32)
    row3 = jnp.concatenate([d["out_norm_attn"], d["out_norm_conv"], d["out_norm_mem"]], axis=1)
    row4 = jnp.concatenate([d["conv_w"].reshape(1, 3 * HD), z(3 * CONV_W - 3 * HD), d["conv_b"]], axis=1)
    row5 = jnp.concatenate([d["q_norm"], d["k_norm"], d["mem_q_norm"], d["mem_k_norm"], d["attn_sinks"],
                            z(1024 - 4 * HD - N_Q)], axis=1)
    return jnp.concatenate([d["norm_mix"], d["norm_mem"], d["norm_ffn"], row3, row4, row5, z(1024), z(1024)], axis=0)


def _other_chips(x, y):
    return [(1 - x, y), (x, 1 - y), (1 - x, 1 - y)]


Exchange = collections.namedtuple("Exchange", "ins outs sems start finish relay", defaults=(None,))
RELAY_STEPS_BEFORE_END = 3


def _run(name, body, grid, ins, in_specs, out_shape, out_specs, scratch=(), vmem_mib=32, exchange=None):
    ins, in_specs, out_shape, out_specs, scratch = list(ins), list(in_specs), list(out_shape), list(out_specs), list(scratch)
    ni, no, ns = len(ins), len(out_shape), len(scratch)
    ex = exchange
    if ex is not None:
        nxi, nxo = len(ex.ins), len(ex.outs)

    def call_body(*refs):
        if ex is None:
            body(*refs)
            return
        a, xa = refs[:ni], refs[ni:ni + nxi]
        o, xo = refs[ni + nxi:ni + nxi + no], refs[ni + nxi + no:ni + nxi + no + nxo]
        s, xs = refs[ni + nxi + no + nxo:ni + nxi + no + nxo + ns], refs[ni + nxi + no + nxo + ns:]
        if grid:
            first = functools.reduce(jnp.logical_and, [pl.program_id(d) == 0 for d in range(len(grid))])
            last = functools.reduce(jnp.logical_and, [pl.program_id(d) == grid[d] - 1 for d in range(len(grid))])
            pl.when(first)(lambda: ex.start(xa, xo, xs))
            body(*a, *o, *s)
            if ex.relay is not None:
                early = functools.reduce(jnp.logical_and, [pl.program_id(d) == grid[d] - 1 for d in range(len(grid) - 1)],
                                         pl.program_id(len(grid) - 1) == grid[-1] - 1 - RELAY_STEPS_BEFORE_END)
                pl.when(early)(lambda: ex.relay(xa, xo, xs))
            pl.when(last)(lambda: ex.finish(xa, xo, xs))
        else:
            ex.start(xa, xo, xs)
            if body is not None:
                body(*a, *o, *s)
            if ex.relay is not None:
                ex.relay(xa, xo, xs)
            ex.finish(xa, xo, xs)

    if ex is not None:
        ins, in_specs = ins + list(ex.ins), in_specs + [ANY] * nxi
        out_shape, out_specs = out_shape + list(ex.outs), out_specs + [ANY] * nxo
        scratch = scratch + list(ex.sems)
    kw = dict(grid=grid) if grid else {}
    res = pl.pallas_call(
        call_body, name=name, out_shape=out_shape, in_specs=in_specs, out_specs=out_specs, scratch_shapes=scratch,
        compiler_params=pltpu.CompilerParams(dimension_semantics=("arbitrary",) * len(grid) if grid else None,
                                             vmem_limit_bytes=vmem_mib * VMEM_MIB), **kw)(*ins)
    res = list(res)
    return (res[:no], res[no:]) if ex is not None else res


def _remote(src, dst, ssem, rsem, dev):
    return pltpu.make_async_remote_copy(src_ref=src, dst_ref=dst, send_sem=ssem, recv_sem=rsem,
                                        device_id=dev, device_id_type=MESH)


def gather_exchange(shards, split):
    n = len(shards)

    def rows(ref, e, kk, half=None):
        R = shards[e].shape[0]
        if half is None:
            return ref.at[pl.ds(pl.multiple_of(kk * R, 8), R)]
        return ref.at[pl.ds(pl.multiple_of(kk * R + half * (R // 2), 8), R // 2)]

    def ici(src, dst, sm, e, j, chip_j, x, y, c):
        k = 2 * x + y
        if split[e]:
            s = src[e].at[pl.ds(pl.multiple_of(c * (shards[e].shape[0] // 2), 8), shards[e].shape[0] // 2)]
            return _remote(s, rows(dst[e], e, k, c), sm[0].at[6 * e + j], sm[1].at[6 * e + j], (*chip_j, c))
        return _remote(src[e], rows(dst[e], e, k), sm[0].at[6 * e + j], sm[1].at[6 * e + j], (*chip_j, c))

    def landed(dst, e, chip_j, c):
        kj = 2 * chip_j[0] + chip_j[1]
        return rows(dst[e], e, kj, c) if split[e] else rows(dst[e], e, kj)

    def forward(dst, sm, e, j, chip_j, x, y, c, sender_c):
        kj = 2 * chip_j[0] + chip_j[1]
        r = rows(dst[e], e, kj, sender_c)
        return _remote(r, r, sm[0].at[6 * e + 3 + j], sm[1].at[6 * e + 3 + j], (x, y, 1 - c))

    def local(src, dst, sm, e, x, y):
        return pltpu.make_async_copy(src[e], rows(dst[e], e, 2 * x + y), sm[2].at[e])

    def start(src, dst, sm):
        x, y, c = _place()
        for e in range(n):
            local(src, dst, sm, e, x, y).start()
            for j, chip_j in enumerate(_other_chips(x, y)):
                ici(src, dst, sm, e, j, chip_j, x, y, c).start()

    def relay(src, dst, sm):
        x, y, c = _place()
        for e in range(n):
            for j, chip_j in enumerate(_other_chips(x, y)):
                r = landed(dst, e, chip_j, c)
                _remote(r, r, sm[0].at[6 * e + j], sm[1].at[6 * e + j], (*chip_j, c)).wait_recv()
                if split[e]:
                    forward(dst, sm, e, j, chip_j, x, y, c, c).start()

    def finish(src, dst, sm):
        x, y, c = _place()
        chips = _other_chips(x, y)
        for e in range(n):
            for j, chip_j in enumerate(chips):
                if split[e]:
                    forward(dst, sm, e, j, chip_j, x, y, c, 1 - c).wait_recv()
        for e in range(n):
            for j, chip_j in enumerate(chips):
                ici(src, dst, sm, e, j, chip_j, x, y, c).wait_send()
                if split[e]:
                    forward(dst, sm, e, j, chip_j, x, y, c, c).wait_send()
            local(src, dst, sm, e, x, y).wait()

    outs = [SDS((4 * s.shape[0], s.shape[1]), s.dtype) for s in shards]
    return Exchange(list(shards), outs, [DMA((6 * n,)), DMA((6 * n,)), DMA((n,))], start, finish, relay)


def halves_exchange(grads):
    n = len(grads)

    def copy(g, st, sm, e, x, y, c):
        return _remote(g[e].at[:, 1 - c], st[e], sm[0].at[e], sm[1].at[e], (x, y, 1 - c))

    def start(g, st, sm):
        x, y, c = _place()
        for e in range(n):
            copy(g, st, sm, e, x, y, c).start()

    def finish(g, st, sm):
        x, y, c = _place()
        for e in range(n):
            copy(g, st, sm, e, x, y, c).wait()

    outs = [SDS((4,) + a.shape[2:], a.dtype) for a in grads]
    return Exchange(list(grads), outs, [DMA((n,)), DMA((n,))], start, finish)


def scatter_exchange(parts):
    n = len(parts)

    def ici(p, st, sm, e, j, chip_j, x, y, c):
        k, kj = 2 * x + y, 2 * chip_j[0] + chip_j[1]
        return _remote(p[e].at[kj], st[e].at[c, k], sm[0].at[8 * e + j], sm[1].at[8 * e + j], (*chip_j, c))

    def own(p, st, sm, e, x, y, c):
        k = 2 * x + y
        return _remote(p[e].at[k], st[e].at[c, k], sm[0].at[8 * e + 3], sm[1].at[8 * e + 3], (x, y, 1 - c))

    def forward(st, sm, e, j, chip_j, x, y, c, sender_c):
        kj = 2 * chip_j[0] + chip_j[1]
        r = st[e].at[sender_c, kj]
        return _remote(r, r, sm[0].at[8 * e + 4 + j], sm[1].at[8 * e + 4 + j], (x, y, 1 - c))

    def local(p, st, sm, e, x, y, c):
        k = 2 * x + y
        return pltpu.make_async_copy(p[e].at[k], st[e].at[c, k], sm[2].at[e])

    def start(p, st, sm):
        x, y, c = _place()
        for e in range(n):
            local(p, st, sm, e, x, y, c).start()
            own(p, st, sm, e, x, y, c).start()
            for j, chip_j in enumerate(_other_chips(x, y)):
                ici(p, st, sm, e, j, chip_j, x, y, c).start()

    def relay(p, st, sm):
        x, y, c = _place()
        for e in range(n):
            for j, chip_j in enumerate(_other_chips(x, y)):
                kj = 2 * chip_j[0] + chip_j[1]
                r = st[e].at[c, kj]
                _remote(r, r, sm[0].at[8 * e + j], sm[1].at[8 * e + j], (*chip_j, c)).wait_recv()
                forward(st, sm, e, j, chip_j, x, y, c, c).start()

    def finish(p, st, sm):
        x, y, c = _place()
        k = 2 * x + y
        chips = _other_chips(x, y)
        for e in range(n):
            r = st[e].at[1 - c, k]
            _remote(r, r, sm[0].at[8 * e + 3], sm[1].at[8 * e + 3], (x, y, 1 - c)).wait_recv()
            for j, chip_j in enumerate(chips):
                forward(st, sm, e, j, chip_j, x, y, c, 1 - c).wait_recv()
        for e in range(n):
            own(p, st, sm, e, x, y, c).wait_send()
            for j, chip_j in enumerate(chips):
                ici(p, st, sm, e, j, chip_j, x, y, c).wait_send()
                forward(st, sm, e, j, chip_j, x, y, c, c).wait_send()
            local(p, st, sm, e, x, y, c).wait()

    outs = [SDS((2,) + a.shape, a.dtype) for a in parts]
    return Exchange(list(parts), outs, [DMA((8 * n,)), DMA((8 * n,)), DMA((n,))], start, finish, relay)


def tail_reduce(d_norm_mix, d_norm_mem, d_norm_ffn, d_gains, d_cw8, d_cbias, d_qg, d_kg, d_mqg, d_mkg, d_sink8, loss8, tail):
    n = len(tail)
    halves = halves_exchange(tail)
    scatter = scatter_exchange([SDS((4,) + a.shape[2:], WIRE) for a in tail])

    def body(nm_ref, nmem_ref, nf_ref, gn_ref, cw_ref, cb_ref, qg_ref, kg_ref, mqg_ref, mkg_ref, sk_ref, ls_ref, *rest):
        g, o_ref, st = rest[:n], rest[n], rest[n + 1:2 * n + 1]
        buf, ssem, rsem = rest[2 * n + 1:2 * n + 4]
        own, sib, part = (rest[2 * n + 4 + i * n:2 * n + 4 + (i + 1) * n] for i in range(3))
        lsem = rest[5 * n + 4]
        hsem, xsem = rest[5 * n + 5:5 * n + 7], rest[5 * n + 7:]
        x, y, c = _place()
        loads = [pltpu.make_async_copy(g[e].at[:, c], own[e], lsem.at[e]) for e in range(n)]
        for ld in loads:
            ld.start()
        halves.start(g, sib, hsem)
        me = 4 * x + 2 * y + c
        mine = buf.at[me]
        mine[...] = jnp.zeros((8, 1024), f32)
        mine[0:1, :] = nm_ref[...]
        mine[1:2, :] = nmem_ref[...]
        mine[2:3, :] = nf_ref[...]
        mine[3:4, :] = gn_ref[...]
        for j in range(3):
            mine[4:5, pl.ds(j * CONV_W, CONV_W)] = cw_ref[j:j + 1, :]
        mine[4:5, pl.ds(3 * CONV_W, CONV_W)] = cb_ref[...]
        for j, r in enumerate((qg_ref, kg_ref, mqg_ref, mkg_ref)):
            mine[5:6, pl.ds(j * HD, HD)] = r[...]
        mine[5:6, pl.ds(256, 128)] = sk_ref[0:1, :]
        mine[5:6, pl.ds(384, 128)] = ls_ref[0:1, :]

        def peer_of(m):
            return (1 - x if m & 4 else x, 1 - y if m & 2 else y, 1 - c if m & 1 else c)

        for m in range(1, 8):
            _remote(mine, mine, ssem.at[m - 1], rsem.at[m - 1], peer_of(m)).start()
        for ld in loads:
            ld.wait()
        halves.finish(g, sib, hsem)
        for e in range(n):
            part[e][...] = (own[e][...] + sib[e][...]).astype(WIRE)
        scatter.start(part, st, xsem)
        scatter.relay(part, st, xsem)
        scatter.finish(part, st, xsem)
        for m in range(1, 8):
            p = peer_of(m)
            got = buf.at[4 * p[0] + 2 * p[1] + p[2]]
            _remote(got, got, ssem.at[m - 1], rsem.at[m - 1], p).wait_recv()
        for m in range(1, 8):
            _remote(mine, mine, ssem.at[m - 1], rsem.at[m - 1], peer_of(m)).wait_send()
        acc = buf[0]
        for d in range(1, 8):
            acc = acc + buf[d]
        o_ref[...] = acc

    ins = [d_norm_mix, d_norm_mem, d_norm_ffn, d_gains, d_cw8, d_cbias, d_qg, d_kg, d_mqg, d_mkg, d_sink8, loss8]
    half_shape = [(4,) + a.shape[2:] for a in tail]
    scratch = ([pltpu.VMEM((8, 8, 1024), f32), DMA((7,)), DMA((7,))]
               + [pltpu.VMEM(s, f32) for s in half_shape] * 2 + [pltpu.VMEM(s, WIRE) for s in half_shape]
               + [DMA((n,))] + list(halves.sems) + list(scatter.sems))
    res = _run("tail_reduce", body, (), ins + list(tail), [VM] * len(ins) + [ANY] * n,
               [SDS((8, 1024), f32)] + list(scatter.outs), [VM] + [ANY] * n, scratch=scratch, vmem_mib=40)
    return res[0], res[1:]


def add_halves(cidx, grads, stages, name, nch=2):
    n = len(grads)

    def body(c_ref, *refs):
        g, st, o = refs[:n], refs[n:2 * n], refs[2 * n:]
        for e in range(n):
            o[e][...] = (g[e][...] + st[e][...]).astype(WIRE)

    in_specs, out_specs, out_shape = [], [], []
    for a in grads:
        hr, C = a.shape[2], a.shape[3]
        in_specs.append(pl.BlockSpec((None, None, hr // nch, C), lambda s, q, c_ref: (s, c_ref[0], q, 0)))
    for a in stages:
        hr, C = a.shape[1], a.shape[2]
        in_specs.append(pl.BlockSpec((None, hr // nch, C), lambda s, q, c_ref: (s, q, 0)))
        out_specs.append(pl.BlockSpec((None, hr // nch, C), lambda s, q, c_ref: (s, q, 0)))
        out_shape.append(SDS(a.shape, WIRE))
    return pl.pallas_call(
        body, name=name, out_shape=out_shape,
        grid_spec=pltpu.PrefetchScalarGridSpec(num_scalar_prefetch=1, grid=(4, nch), in_specs=in_specs, out_specs=out_specs),
        compiler_params=pltpu.CompilerParams(dimension_semantics=("arbitrary", "arbitrary")),
    )(cidx, *grads, *stages)


def _adamw_math(w, g, m, v):
    m = ADAM_B1 * m + (1.0 - ADAM_B1) * g
    v = ADAM_B2 * v + (1.0 - ADAM_B2) * (g * g)
    m_hat = m / (1.0 - ADAM_B1 ** ADAM_STEP)
    v_hat = v / (1.0 - ADAM_B2 ** ADAM_STEP)
    delta = -ADAM_LR * (m_hat / (jnp.sqrt(v_hat) + ADAM_EPS) + ADAM_WD * w)
    return delta, m, v


def _sum_chips(st):
    return ((st[0].astype(f32) + st[1].astype(f32)) + st[2].astype(f32)) + st[3].astype(f32)


def adamw_big(name, stages, ws, ms, vs, nstep, exchange=None):
    n = len(stages)

    def body(*refs):
        st, w, m, v = refs[:n], refs[n:2 * n], refs[2 * n:3 * n], refs[3 * n:4 * n]
        outs = refs[4 * n:]
        for e in range(n):
            g = jnp.concatenate([_sum_chips(st[e].at[0]), _sum_chips(st[e].at[1])], axis=0)
            d, mm, vv = _adamw_math(w[e][...], g, m[e][...], v[e][...])
            outs[4 * e][...] = g
            outs[4 * e + 1][...] = d
            outs[4 * e + 2][...] = mm
            outs[4 * e + 3][...] = vv

    st_specs, w_specs = [], []
    for e in range(n):
        _, _, hr, C = stages[e].shape
        st_specs.append(pl.BlockSpec((2, 4, hr, C // nstep), lambda i: (0, 0, 0, i)))
        w_specs.append(pl.BlockSpec((2 * hr, C // nstep), lambda i: (0, i)))
    out_specs = [s for s in w_specs for _ in range(4)]
    out_shape = [SDS(w.shape, f32) for w in ws for _ in range(4)]
    res = _run(name, body, (nstep,), list(stages) + list(ws) + list(ms) + list(vs), st_specs + w_specs * 3,
               out_shape, out_specs, vmem_mib=48, exchange=exchange)
    res, sent = res if exchange is not None else (res, None)
    return [res[4 * e:4 * e + 4] for e in range(n)], sent


def adamw_small(tot, pk_w, pk_m, pk_v, shapes):
    def body(tot_ref, w_ref, m_ref, v_ref, *outs):
        x, y, _ = _place()
        chip = 2 * x + y
        taps = []
        for j in range(3):
            mine = tot_ref[4:5, j * CONV_W:j * CONV_W + HD]
            for s in range(1, 4):
                mine = jnp.where(chip == s, tot_ref[4:5, j * CONV_W + s * HD:j * CONV_W + (s + 1) * HD], mine)
            taps.append(mine)
        row4 = jnp.concatenate(taps + [jnp.zeros((1, 3 * CONV_W - 3 * HD), f32), tot_ref[4:5, 3 * CONV_W:]], axis=1)
        tot_v = tot_ref[...]
        row = lax.broadcasted_iota(jnp.int32, tot_v.shape, 0)
        g = jnp.where(row == 4, jnp.broadcast_to(row4, tot_v.shape), tot_v)
        d, mm, vv = _adamw_math(w_ref[...], g, m_ref[...], v_ref[...])
        for i, name in enumerate(SMALL):
            for k, val in enumerate((g, d, mm, vv)):
                if name == "conv_w":
                    outs[4 * i + k][...] = jnp.concatenate([val[4:5, j * HD:(j + 1) * HD] for j in range(3)], axis=0)[None]
                else:
                    r, c0, w = SMALL_AT[name]
                    outs[4 * i + k][...] = val[r:r + 1, c0:c0 + w]

    out_shape = [SDS(shapes[k], f32) for k in SMALL for _ in range(4)]
    res = _run("adamw_small", body, (), [tot, pk_w, pk_m, pk_v], [VM] * 4, out_shape, [VM] * len(out_shape))
    return {k: res[4 * i:4 * i + 4] for i, k in enumerate(SMALL)}


def prep_weights(shards):
    n = len(shards)

    def body(*refs):
        for e in range(n):
            refs[n + e][...] = _c(refs[e][...])

    return _run("prep_weights", body, (), shards, [VM] * n, [SDS(a.shape, MXU) for a in shards], [VM] * n, vmem_mib=48)


def mem_kv_fwd(mem2d, pk, wmkv):
    M, D = mem2d.shape

    def body(m_ref, pk_ref, w_ref, mn_ref, kv_ref, km_ref, vm_ref):
        m = m_ref[...]
        mn = _c(m * _rstd(m) * _small(pk_ref, "norm_mem"))
        mn_ref[...] = mn
        kv = _nn(mn, w_ref[...])
        kv_ref[...] = kv
        kk = kv[:, :MEM_W]
        km_ref[...] = _c(kk * _heads_rstd(kk) * _lanes(_small(pk_ref, "mem_k_norm"), MEM_W))
        vm_ref[...] = _c(kv[:, MEM_W:])

    return _run("mem_kv_fwd", body, (), [mem2d, pk, wmkv], [VM] * 3,
                [SDS((M, D), MXU), SDS((M, 2 * MEM_W), f32), SDS((M, MEM_W), MXU), SDS((M, MEM_W), MXU)], [VM] * 4)


QKV_W = ATT_W + 2 * KV_W + MEM_W


def in_proj_fwd(x2d, pk, winT, tm, exchange):
    T, D = x2d.shape
    P = winT.shape[0]

    def body(x_ref, pk_ref, w_ref, xn_ref, proj_ref, qkv_ref):
        xv = x_ref[...]
        xn = _c(xv * _rstd(xv) * _small(pk_ref, "norm_mix"))
        xn_ref[...] = xn
        proj = _nt(xn, w_ref[...])
        proj_ref[...] = proj
        q, k = proj[:, :ATT_W], proj[:, ATT_W:ATT_W + KV_W]
        qm = proj[:, P - MEM_W:]
        qkv_ref[...] = jnp.concatenate(
            [_c(q * _heads_rstd(q) * _lanes(_small(pk_ref, "q_norm"), ATT_W)),
             _c(k * _heads_rstd(k) * _lanes(_small(pk_ref, "k_norm"), KV_W)),
             _c(proj[:, ATT_W + KV_W:ATT_W + 2 * KV_W]),
             _c(qm * _heads_rstd(qm) * _lanes(_small(pk_ref, "mem_q_norm"), MEM_W))], axis=1)

    return _run("in_proj_fwd", body, (T // tm,), [x2d, pk, winT],
                [pl.BlockSpec((tm, D), lambda i: (i, 0)), VM, VM],
                [SDS((T, D), MXU), SDS((T, P), f32), SDS((T, QKV_W), MXU)],
                [pl.BlockSpec((tm, D), lambda i: (i, 0)), pl.BlockSpec((tm, P), lambda i: (i, 0)),
                 pl.BlockSpec((tm, QKV_W), lambda i: (i, 0))],
                vmem_mib=40, exchange=exchange)


def _swa_bias_table():
    r = np.arange(GQA * BLK)[:, None]
    k = np.arange(2 * BLK)[None, :]
    dist = (r % BLK) + BLK - k
    band = (dist >= 0) & (dist < BLK)
    tab = np.empty((2, N_KV, GQA * BLK, 2 * BLK), np.float32)
    for later in range(2):
        valid = band & ((k >= BLK) | (later == 1))
        for g in range(N_KV):
            slope = 2.0 ** -(g * GQA + r // BLK + 1.0)
            tab[later, g] = np.where(valid, -slope * dist, NEG)
    return jnp.asarray(tab)


def _sink_column(g, sk_ref):
    hrow = lax.broadcasted_iota(jnp.int32, (GQA * BLK, 1), 0) // BLK
    sink = jnp.zeros((GQA * BLK, 1), f32)
    for hh in range(GQA):
        sink = jnp.where(hrow == hh, sk_ref[g * GQA + hh:g * GQA + hh + 1, 0:1], sink)
    return sink


def _stack_heads(v, g):
    return jnp.concatenate([v[:, (g * GQA + hh) * HD:(g * GQA + hh + 1) * HD] for hh in range(GQA)], axis=0)


def attn_fwd(qkv, sink_rows, BL, S, exchange):
    NB = S // BLK
    T = BL * S

    def body(q_ref, kc_ref, kp_ref, vc_ref, vp_ref, sk_ref, tab_ref, o_ref):
        q = q_ref[...]
        k2 = jnp.concatenate([kp_ref[...], kc_ref[...]], axis=0)
        v2 = jnp.concatenate([vp_ref[...], vc_ref[...]], axis=0)
        ones = jnp.ones((2 * BLK, HD), MXU)
        for g in range(N_KV):
            kn, vh = k2[:, g * HD:(g + 1) * HD], v2[:, g * HD:(g + 1) * HD]
            s = _nt(_stack_heads(q, g), kn) * (HD ** -0.5) + tab_ref[g]
            e, es = _exp_scores(s, _sink_column(g, sk_ref))
            eb = _c(e)
            o = _nn(eb, vh) * (1.0 / (_nn(eb, ones) + es))
            for hh in range(GQA):
                o_ref[:, pl.ds((g * GQA + hh) * HD, HD)] = o[hh * BLK:(hh + 1) * BLK]

    cur = lambda col: (lambda b, j: (b * NB + j, col))
    prev = lambda col: (lambda b, j: (b * NB + jnp.maximum(j - 1, 0), col))
    return _run("attn_fwd", body, (BL, NB), [qkv, qkv, qkv, qkv, qkv, sink_rows, _swa_bias_table()],
                [pl.BlockSpec((BLK, ATT_W), cur(0)),
                 pl.BlockSpec((BLK, KV_W), cur(4)), pl.BlockSpec((BLK, KV_W), prev(4)),
                 pl.BlockSpec((BLK, KV_W), cur(5)), pl.BlockSpec((BLK, KV_W), prev(5)),
                 pl.BlockSpec((8, 128), lambda b, j: (0, 0)),
                 pl.BlockSpec((None, N_KV, GQA * BLK, 2 * BLK), lambda b, j: (jnp.minimum(j, 1), 0, 0, 0))],
                [SDS((T, ATT_W), f32)], [pl.BlockSpec((BLK, ATT_W), cur(0))], exchange=exchange)


def _conv_taps(u, uh):
    row = lax.broadcasted_iota(jnp.int32, u.shape, 0)
    u1 = jnp.where(row == 0, uh[7:8, :], pltpu.roll(u, 1, 0))
    u2 = jnp.where(row == 0, uh[6:7, :], jnp.where(row == 1, uh[7:8, :], pltpu.roll(u, 2, 0)))
    return u1, u2


def _mem_head(qm, km, vm, h):
    qh, kh, vh = (a[:, h * HD:(h + 1) * HD] for a in (qm, km, vm))
    e, _ = _exp_scores(_nt(qh, kh) * (HD ** -0.5))
    return qh, kh, vh, e


def mixer_tail_fwd(x2d, attn_out, proj, qkv, km, vm, conv_w8, pk, wout, S, tm, exchange):
    T, D = x2d.shape
    NM = km.shape[0] // (T // S)

    def body(x_ref, ao_ref, ch_ref, cb_ref, cc_ref, chh_ref, cch_ref, qm_ref, km_ref, vm_ref, cw_ref, pk_ref,
             wout_ref, co_ref, mo_ref, mg_ref, x1_ref, h_ref):
        first = (pl.program_id(0) * tm) % S == 0
        u = cc_ref[...] * ch_ref[...]
        uh = jnp.where(first, 0.0, cch_ref[...] * chh_ref[...])
        u1, u2 = _conv_taps(u, uh)
        conv = cw_ref[0:1, :] * u2 + cw_ref[1:2, :] * u1 + cw_ref[2:3, :] * u + _small(pk_ref, "conv_b")
        conv_out = cb_ref[...] * conv
        co_ref[...] = conv_out
        qm, kmv, vmv = qm_ref[...], km_ref[...], vm_ref[...]
        ones = jnp.ones((NM, HD), MXU)
        for h in range(N_MEMH):
            _, _, vh, e = _mem_head(qm, kmv, vmv, h)
            eb = _c(e)
            mo_ref[:, pl.ds(h * HD, HD)] = _nn(eb, vh) * (1.0 / _nn(eb, ones))
        mem_out = mo_ref[...]
        ao = ao_ref[...]
        merged = _c(jnp.concatenate([ao * _rstd(ao) * _small(pk_ref, "out_norm_attn"),
                                     conv_out * _rstd(conv_out) * _small(pk_ref, "out_norm_conv"),
                                     mem_out * _rstd(mem_out) * _small(pk_ref, "out_norm_mem")], axis=1))
        mg_ref[...] = merged
        x1 = x_ref[...] + _nn(merged, wout_ref[...])
        x1_ref[...] = x1
        h_ref[...] = _c(x1 * _rstd(x1) * _small(pk_ref, "norm_ffn"))

    tile = lambda w, col: pl.BlockSpec((tm, w), lambda i: (i, col))
    halo = lambda col: pl.BlockSpec((8, CONV_W), lambda i: (jnp.maximum(i * (tm // 8) - 1, 0), col))
    seq = pl.BlockSpec((NM, MEM_W), lambda i: ((i * tm) // S, 0))
    small = lambda a: pl.BlockSpec(a.shape, lambda i: (0, 0))
    return _run("mixer_tail_fwd", body, (T // tm,),
                [x2d, attn_out, proj, proj, proj, proj, proj, qkv, km, vm, conv_w8, pk, wout],
                [tile(D, 0), tile(ATT_W, 0), tile(CONV_W, 3), tile(CONV_W, 4), tile(CONV_W, 5), halo(3), halo(5),
                 tile(MEM_W, 3), seq, seq, VM, VM, VM],
                [SDS((T, CONV_W), f32), SDS((T, MEM_W), f32), SDS((T, D), MXU), SDS((T, D), f32), SDS((T, D), MXU)],
                [tile(CONV_W, 0), tile(MEM_W, 0), tile(D, 0), tile(D, 0), tile(D, 0)], vmem_mib=40, exchange=exchange)


def ffn_fwd_bwd(h, x1, tgt, wgT, wuT, wd, pk, tm):
    T, D = x1.shape
    F = wd.shape[0]

    def body(h_ref, x1_ref, t_ref, wg_ref, wu_ref, wd_ref, pk_ref,
             dx1_ref, dx2_ref, act_ref, dg_ref, du_ref, loss_ref, dgf_ref):
        @pl.when(pl.program_id(0) == 0)
        def _():
            loss_ref[...] = jnp.zeros_like(loss_ref)
            dgf_ref[...] = jnp.zeros_like(dgf_ref)

        hv = h_ref[...]
        gate = _nt(hv, wg_ref[...])
        up = _nt(hv, wu_ref[...])
        sg = jax.nn.sigmoid(gate)
        sl = gate * sg
        act = _c(sl * up)
        act_ref[...] = act
        x1v = x1_ref[...]
        diff = (x1v + _nn(act, wd_ref[...])) - t_ref[...]
        loss_ref[...] += 0.5 * jnp.sum(jnp.sum(diff * diff, axis=-1, keepdims=True) / D, axis=0, keepdims=True)
        dx2 = diff / D
        dx2b = _c(dx2)
        dx2_ref[...] = dx2b
        d_act = _nt(dx2b, wd_ref[...])
        d_up = _c(d_act * sl)
        d_gate = _c(d_act * up * (sg * (1.0 + gate * (1.0 - sg))))
        du_ref[...] = d_up
        dg_ref[...] = d_gate
        dh = _nn(d_gate, wg_ref[...]) + _nn(d_up, wu_ref[...])
        dv, dgf = _norm_bwd(dh, x1v, _rstd(x1v), _small(pk_ref, "norm_ffn"))
        dx1_ref[...] = dx2 + dv
        dgf_ref[...] += dgf

    tile = lambda w: pl.BlockSpec((tm, w), lambda i: (i, 0))
    return _run("ffn_fwd_bwd", body, (T // tm,), [h, x1, tgt, wgT, wuT, wd, pk],
                [tile(D), tile(D), tile(D), VM, VM, VM, VM],
                [SDS((T, D), f32), SDS((T, D), MXU), SDS((T, F), MXU), SDS((T, F), MXU), SDS((T, F), MXU),
                 SDS((8, 128), f32), SDS((1, D), f32)],
                [tile(D), tile(D), tile(F), tile(F), tile(F), pl.BlockSpec((8, 128), lambda i: (0, 0)),
                 pl.BlockSpec((1, D), lambda i: (0, 0))], vmem_mib=56)


def matmul_tn(a, b, name, tmo, tk):
    T, M = a.shape
    N = b.shape[1]

    def body(a_ref, b_ref, o_ref):
        @pl.when(pl.program_id(1) == 0)
        def _():
            o_ref[...] = jnp.zeros_like(o_ref)

        o_ref[...] += _tn(a_ref[...], b_ref[...])

    return _run(name, body, (M // tmo, T // tk), [a, b],
                [pl.BlockSpec((tk, tmo), lambda m, k: (k, m)), pl.BlockSpec((tk, N), lambda m, k: (k, 0))],
                [SDS((M, N), f32)], [pl.BlockSpec((tmo, N), lambda m, k: (m, 0))], vmem_mib=48)[0]


def out_proj_bwd(dx1, merged, attn_out, conv_out, mem_out, pk, wout, tm):
    T, D = dx1.shape

    def body(dx1_ref, mg_ref, ao_ref, co_ref, mo_ref, pk_ref, w_ref,
             dao_ref, dco_ref, dmo_ref, dw_ref, dgain_ref):
        @pl.when(pl.program_id(0) == 0)
        def _():
            dw_ref[...] = jnp.zeros_like(dw_ref)
            dgain_ref[...] = jnp.zeros_like(dgain_ref)

        dxb = _c(dx1_ref[...])
        dw_ref[...] += _tn(mg_ref[...], dxb)
        dmg = _nt(dxb, w_ref[...])
        ao, co, mo = ao_ref[...], co_ref[...], mo_ref[...]
        da, ga = _norm_bwd(dmg[:, :ATT_W], ao, _rstd(ao), _small(pk_ref, "out_norm_attn"))
        dc, gc = _norm_bwd(dmg[:, ATT_W:ATT_W + CONV_W], co, _rstd(co), _small(pk_ref, "out_norm_conv"))
        dm, gm = _norm_bwd(dmg[:, ATT_W + CONV_W:], mo, _rstd(mo), _small(pk_ref, "out_norm_mem"))
        dao_ref[...] = da
        dco_ref[...] = dc
        dmo_ref[...] = dm
        dgain_ref[...] += jnp.concatenate([ga, gc, gm], axis=1)

    tile = lambda w: pl.BlockSpec((tm, w), lambda i: (i, 0))
    return _run("out_proj_bwd", body, (T // tm,), [dx1, merged, attn_out, conv_out, mem_out, pk, wout],
                [tile(D), tile(D), tile(ATT_W), tile(CONV_W), tile(MEM_W), VM, VM],
                [SDS((T, ATT_W), f32), SDS((T, CONV_W), f32), SDS((T, MEM_W), f32), SDS((D, D), f32), SDS((1, D), f32)],
                [tile(ATT_W), tile(CONV_W), tile(MEM_W), pl.BlockSpec((D, D), lambda i: (0, 0)),
                 pl.BlockSpec((1, D), lambda i: (0, 0))], vmem_mib=40)


def attn_bwd(qkv, d_attn, attn_out, sink_rows, BL, S, exchange):
    NB = S // BLK
    T = BL * S

    def body(q_ref, kc_ref, kp_ref, vc_ref, vp_ref, do_ref, ao_ref, sk_ref, tab_ref,
             dq_ref, dk_ref, dv_ref, dsk_ref, pend_k, pend_v):
        b, j = pl.program_id(0), pl.program_id(1)

        @pl.when((b == 0) & (j == 0))
        def _():
            dsk_ref[...] = jnp.zeros_like(dsk_ref)

        @pl.when(j == 0)
        def _():
            pend_k[...] = jnp.zeros_like(pend_k)
            pend_v[...] = jnp.zeros_like(pend_v)

        @pl.when(j < NB)
        def _():
            q, do, ao = q_ref[...], do_ref[...], ao_ref[...]
            k2 = jnp.concatenate([kp_ref[...], kc_ref[...]], axis=0)
            v2 = jnp.concatenate([vp_ref[...], vc_ref[...]], axis=0)
            lane = lax.broadcasted_iota(jnp.int32, (8, 128), 1)
            ones_w = jnp.ones((2 * BLK, 2 * BLK), MXU)
            dsk = jnp.zeros((8, 128), f32)
            dks, dvs = [], []
            for g in range(N_KV):
                kn, vh = k2[:, g * HD:(g + 1) * HD], v2[:, g * HD:(g + 1) * HD]
                qs = _stack_heads(q, g)
                s = _nt(qs, kn) * (HD ** -0.5) + tab_ref[g]
                e, es = _exp_scores(s, _sink_column(g, sk_ref))
                eb = _c(e)
                inv_w = 1.0 / (_nn(eb, ones_w) + es)
                inv_n = inv_w[:, :HD]
                dos = _stack_heads(do, g)
                delta = _rowsum_mxu(dos * _stack_heads(ao, g), 2 * BLK)
                dp = _nt(_c(dos), vh)
                ds = _c(e * inv_w * (dp - delta) * (HD ** -0.5))
                t = es * inv_n[:, 0:1] * delta[:, 0:1]
                for hh in range(GQA):
                    dsk = dsk + jnp.where(lane == g * GQA + hh, -jnp.sum(t[hh * BLK:(hh + 1) * BLK]), 0.0)
                dvs.append(_tn(eb, _c(dos * inv_n)))
                dks.append(_tn(ds, qs))
                dqs = _nn(ds, kn)
                for hh in range(GQA):
                    dq_ref[:, pl.ds((g * GQA + hh) * HD, HD)] = dqs[hh * BLK:(hh + 1) * BLK]
            dk2 = jnp.concatenate(dks, axis=1)
            dv2 = jnp.concatenate(dvs, axis=1)
            dk_ref[...] = pend_k[...] + dk2[:BLK]
            dv_ref[...] = pend_v[...] + dv2[:BLK]
            pend_k[...] = dk2[BLK:]
            pend_v[...] = dv2[BLK:]
            dsk_ref[...] += dsk

        @pl.when(j == NB)
        def _():
            dk_ref[...] = pend_k[...]
            dv_ref[...] = pend_v[...]

    cur = lambda col: (lambda b, j: (b * NB + jnp.minimum(j, NB - 1), col))
    prev = lambda col: (lambda b, j: (b * NB + jnp.maximum(j - 1, 0), col))
    small = lambda shape: pl.BlockSpec(shape, lambda b, j: (0, 0))
    return _run("attn_bwd", body, (BL, NB + 1), [qkv, qkv, qkv, qkv, qkv, d_attn, attn_out, sink_rows, _swa_bias_table()],
                [pl.BlockSpec((BLK, ATT_W), cur(0)),
                 pl.BlockSpec((BLK, KV_W), cur(4)), pl.BlockSpec((BLK, KV_W), prev(4)),
                 pl.BlockSpec((BLK, KV_W), cur(5)), pl.BlockSpec((BLK, KV_W), prev(5)),
                 pl.BlockSpec((BLK, ATT_W), cur(0)), pl.BlockSpec((BLK, ATT_W), cur(0)), small((8, 128)),
                 pl.BlockSpec((None, N_KV, GQA * BLK, 2 * BLK), lambda b, j: (jnp.minimum(j, 1), 0, 0, 0))],
                [SDS((T, ATT_W), f32), SDS((T, KV_W), f32), SDS((T, KV_W), f32), SDS((8, 128), f32)],
                [pl.BlockSpec((BLK, ATT_W), cur(0)), pl.BlockSpec((BLK, KV_W), prev(0)),
                 pl.BlockSpec((BLK, KV_W), prev(0)), small((8, 128))],
                scratch=[pltpu.VMEM((BLK, KV_W), f32)] * 2, exchange=exchange)


def mem_conv_bwd(d_mem_out, mem_out, d_conv_out, proj, qkv, km, vm, conv_w8, pk, S, tm, exchange):
    T = d_mem_out.shape[0]
    NM = km.shape[0] // (T // S)

    def body(dmo_ref, mo_ref, dco_ref, ch_ref, cb_ref, cc_ref, chh_ref, cch_ref, qm_ref, km_ref, vm_ref, cw_ref,
             pk_ref, dqm_ref, dkm_ref, dvm_ref, dcb_ref, dcv_ref, dcw_ref, dcbias_ref):
        i = pl.program_id(0)
        first = (i * tm) % S == 0

        @pl.when(i == 0)
        def _():
            dcw_ref[...] = jnp.zeros_like(dcw_ref)
            dcbias_ref[...] = jnp.zeros_like(dcbias_ref)

        @pl.when(first)
        def _():
            dkm_ref[...] = jnp.zeros_like(dkm_ref)
            dvm_ref[...] = jnp.zeros_like(dvm_ref)

        qm, kmv, vmv, dmo, mo = qm_ref[...], km_ref[...], vm_ref[...], dmo_ref[...], mo_ref[...]
        ones_w = jnp.ones((NM, NM), MXU)
        for h in range(N_MEMH):
            qh, kh, vh, e = _mem_head(qm, kmv, vmv, h)
            eb = _c(e)
            doh = dmo[:, h * HD:(h + 1) * HD]
            delta = _rowsum_mxu(doh * mo[:, h * HD:(h + 1) * HD], NM)
            dp = _nt(_c(doh), vh)
            inv_w = 1.0 / _nn(eb, ones_w)
            ds = _c(e * inv_w * (dp - delta) * (HD ** -0.5))
            dvm_ref[:, pl.ds(h * HD, HD)] += _tn(eb, _c(doh * inv_w[:, :HD]))
            dkm_ref[:, pl.ds(h * HD, HD)] += _tn(ds, qh)
            dqm_ref[:, pl.ds(h * HD, HD)] = _nn(ds, kh)

        u = cc_ref[...] * ch_ref[...]
        uh = jnp.where(first, 0.0, cch_ref[...] * chh_ref[...])
        u1, u2 = _conv_taps(u, uh)
        conv = cw_ref[0:1, :] * u2 + cw_ref[1:2, :] * u1 + cw_ref[2:3, :] * u + _small(pk_ref, "conv_b")
        dy = dco_ref[...]
        dcb_ref[...] = dy * conv
        dcv = dy * cb_ref[...]
        dcv_ref[...] = dcv
        dcbias_ref[...] += jnp.sum(dcv, axis=0, keepdims=True)
        dcw_ref[0:1, :] += jnp.sum(dcv * u2, axis=0, keepdims=True)
        dcw_ref[1:2, :] += jnp.sum(dcv * u1, axis=0, keepdims=True)
        dcw_ref[2:3, :] += jnp.sum(dcv * u, axis=0, keepdims=True)

    tile = lambda w, col: pl.BlockSpec((tm, w), lambda i: (i, col))
    halo = lambda col: pl.BlockSpec((8, CONV_W), lambda i: (jnp.maximum(i * (tm // 8) - 1, 0), col))
    seq = pl.BlockSpec((NM, MEM_W), lambda i: ((i * tm) // S, 0))
    const = lambda shape: pl.BlockSpec(shape, lambda i: (0, 0))
    return _run("mem_conv_bwd", body, (T // tm,),
                [d_mem_out, mem_out, d_conv_out, proj, proj, proj, proj, proj, qkv, km, vm, conv_w8, pk],
                [tile(MEM_W, 0), tile(MEM_W, 0), tile(CONV_W, 0), tile(CONV_W, 3), tile(CONV_W, 4), tile(CONV_W, 5),
                 halo(3), halo(5), tile(MEM_W, 3), seq, seq, VM, VM],
                [SDS((T, MEM_W), f32), SDS(km.shape, f32), SDS(km.shape, f32),
                 SDS((T, CONV_W), f32), SDS((T, CONV_W), f32), SDS((8, CONV_W), f32), SDS((1, CONV_W), f32)],
                [tile(MEM_W, 0), seq, seq, tile(CONV_W, 0), tile(CONV_W, 0), const((8, CONV_W)), const((1, CONV_W))],
                exchange=exchange)


def in_proj_bwd(dqn, dkn, dv, dcb, dcv, dqmn, proj, conv_w8, xn, x2d, dx1, pk, winT, S, tm):
    T, D = x2d.shape
    P = winT.shape[0]
    last_blk = T // 8 - 1

    def body(dq_ref, dk_ref, dv_ref, dcb_ref, dcv_ref, dcvn_ref, dqm_ref, qa_ref, ka_ref, ch_ref, cc_ref, qma_ref,
             cw_ref, xn_ref, x_ref, dx1_ref, pk_ref, w_ref,
             dx_ref, dw_ref, dg_ref, dqg_ref, dkg_ref, dmqg_ref):
        i = pl.program_id(0)

        @pl.when(i == 0)
        def _():
            dw_ref[...] = jnp.zeros_like(dw_ref)
            dg_ref[...] = jnp.zeros_like(dg_ref)
            dqg_ref[...] = jnp.zeros_like(dqg_ref)
            dkg_ref[...] = jnp.zeros_like(dkg_ref)
            dmqg_ref[...] = jnp.zeros_like(dmqg_ref)

        dqa, gq = _heads_norm_bwd(dq_ref[...], qa_ref[...], _small(pk_ref, "q_norm"))
        dka, gk = _heads_norm_bwd(dk_ref[...], ka_ref[...], _small(pk_ref, "k_norm"))
        dqma, gmq = _heads_norm_bwd(dqm_ref[...], qma_ref[...], _small(pk_ref, "mem_q_norm"))
        dqg_ref[...] += gq
        dkg_ref[...] += gk
        dmqg_ref[...] += gmq

        last = ((i + 1) * tm) % S == 0
        dcv = dcv_ref[...]
        nxt = jnp.where(last, 0.0, dcvn_ref[...])
        row = lax.broadcasted_iota(jnp.int32, dcv.shape, 0)
        n1 = jnp.where(row == tm - 1, nxt[0:1, :], pltpu.roll(dcv, tm - 1, 0))
        n2 = jnp.where(row == tm - 2, nxt[0:1, :], jnp.where(row == tm - 1, nxt[1:2, :], pltpu.roll(dcv, tm - 2, 0)))
        du = cw_ref[2:3, :] * dcv + cw_ref[1:2, :] * n1 + cw_ref[0:1, :] * n2
        d_proj = jnp.concatenate([_c(dqa), _c(dka), _c(dv_ref[...]), _c(du * cc_ref[...]),
                                  _c(dcb_ref[...]), _c(du * ch_ref[...]), _c(dqma)], axis=1)
        dw_ref[...] += _tn(d_proj, xn_ref[...])
        xv = x_ref[...]
        dv_, dg = _norm_bwd(_nn(d_proj, w_ref[...]), xv, _rstd(xv), _small(pk_ref, "norm_mix"))
        dx_ref[...] = dx1_ref[...] + dv_
        dg_ref[...] += dg

    tile = lambda w, col=0: pl.BlockSpec((tm, w), lambda i: (i, col))
    nhalo = pl.BlockSpec((8, CONV_W), lambda i: (jnp.minimum((i + 1) * (tm // 8), last_blk), 0))
    const = lambda shape: pl.BlockSpec(shape, lambda i: (0, 0))
    return _run("in_proj_bwd", body, (T // tm,),
                [dqn, dkn, dv, dcb, dcv, dcv, dqmn, proj, proj, proj, proj, proj, conv_w8, xn, x2d, dx1, pk, winT],
                [tile(ATT_W), tile(KV_W), tile(KV_W), tile(CONV_W), tile(CONV_W), nhalo, tile(MEM_W),
                 tile(ATT_W, 0), tile(KV_W, 4), tile(CONV_W, 3), tile(CONV_W, 5), tile(MEM_W, 6), VM,
                 tile(D), tile(D), tile(D), VM, VM],
                [SDS((T, D), f32), SDS((P, D), f32), SDS((1, D), f32), SDS((1, HD), f32), SDS((1, HD), f32),
                 SDS((1, HD), f32)],
                [tile(D), pl.BlockSpec((P, D), lambda i: (0, 0)), const((1, D)), const((1, HD)), const((1, HD)),
                 const((1, HD))],
                vmem_mib=48)


def mem_kv_bwd(dkm, dvm, kv, memn, mem2d, pk, wmkv):
    def body(dkm_ref, dvm_ref, kv_ref, mn_ref, m_ref, pk_ref, w_ref, dw_ref, dg_ref, dkg_ref):
        dkk, dkg = _heads_norm_bwd(dkm_ref[...], kv_ref[:, :MEM_W], _small(pk_ref, "mem_k_norm"))
        dkg_ref[...] = dkg
        dkv = _c(jnp.concatenate([dkk, dvm_ref[...]], axis=1))
        dw_ref[...] = _tn(mn_ref[...], dkv)
        mv = m_ref[...]
        dg_ref[...] = jnp.sum(_nt(dkv, w_ref[...]) * mv * _rstd(mv), axis=0, keepdims=True)

    return _run("mem_kv_bwd", body, (), [dkm, dvm, kv, memn, mem2d, pk, wmkv], [VM] * 7,
                [SDS(wmkv.shape, f32), SDS((1, mem2d.shape[1]), f32), SDS((1, HD), f32)], [VM] * 3, vmem_mib=40)


def _halves_view(g):
    return g.reshape(4, 2, g.shape[0] // 8, g.shape[1])


def kernel(x, mem, norm_mix, w_in, q_norm, k_norm, attn_sinks, conv_w, conv_b, norm_mem, w_mem_kv, mem_q_norm, mem_k_norm, out_norm_attn, out_norm_conv, out_norm_mem, w_out, norm_ffn, w_gate, w_up, w_down, loss_target, m_norm_mix, m_w_in, m_q_norm, m_k_norm, m_attn_sinks, m_conv_w, m_conv_b, m_norm_mem, m_w_mem_kv, m_mem_q_norm, m_mem_k_norm, m_out_norm_attn, m_out_norm_conv, m_out_norm_mem, m_w_out, m_norm_ffn, m_w_gate, m_w_up, m_w_down, v_norm_mix, v_w_in, v_q_norm, v_k_norm, v_attn_sinks, v_conv_w, v_conv_b, v_norm_mem, v_w_mem_kv, v_mem_q_norm, v_mem_k_norm, v_out_norm_attn, v_out_norm_conv, v_out_norm_mem, v_w_out, v_norm_ffn, v_w_gate, v_w_up, v_w_down):
    BL, S, D = x.shape
    T = BL * S
    TM = 256
    _, _, ci = _place()
    cidx = ci.reshape(1).astype(jnp.int32)
    w_small = dict(norm_mix=norm_mix, norm_mem=norm_mem, norm_ffn=norm_ffn, out_norm_attn=out_norm_attn,
                   out_norm_conv=out_norm_conv, out_norm_mem=out_norm_mem, conv_w=conv_w, conv_b=conv_b, q_norm=q_norm,
                   k_norm=k_norm, mem_q_norm=mem_q_norm, mem_k_norm=mem_k_norm, attn_sinks=attn_sinks)
    m_small = dict(norm_mix=m_norm_mix, norm_mem=m_norm_mem, norm_ffn=m_norm_ffn, out_norm_attn=m_out_norm_attn,
                   out_norm_conv=m_out_norm_conv, out_norm_mem=m_out_norm_mem, conv_w=m_conv_w, conv_b=m_conv_b,
                   q_norm=m_q_norm, k_norm=m_k_norm, mem_q_norm=m_mem_q_norm, mem_k_norm=m_mem_k_norm,
                   attn_sinks=m_attn_sinks)
    v_small = dict(norm_mix=v_norm_mix, norm_mem=v_norm_mem, norm_ffn=v_norm_ffn, out_norm_attn=v_out_norm_attn,
                   out_norm_conv=v_out_norm_conv, out_norm_mem=v_out_norm_mem, conv_w=v_conv_w, conv_b=v_conv_b,
                   q_norm=v_q_norm, k_norm=v_k_norm, mem_q_norm=v_mem_q_norm, mem_k_norm=v_mem_k_norm,
                   attn_sinks=v_attn_sinks)
    pk = _pack_small(w_small)

    rowblocks = lambda a, b, c, d, e, f: [a[0].T, b[0].T, c[0].T, d[0], e[0], f[0]]
    w_rb = rowblocks(w_in, w_gate, w_up, w_down, w_out, w_mem_kv)
    m_rb = rowblocks(m_w_in, m_w_gate, m_w_up, m_w_down, m_w_out, m_w_mem_kv)
    v_rb = rowblocks(v_w_in, v_w_gate, v_w_up, v_w_down, v_w_out, v_w_mem_kv)
    winT_s, wgT_s, wuT_s, wd_s, wout_s, wmkv_s = prep_weights(w_rb)
    cw_pad = jnp.zeros((8, 128), f32).at[:3, :HD].set(conv_w[0])
    _, (winT, cw_all) = _run("gather_w_in", None, (), [], [], [], [], exchange=gather_exchange([winT_s, cw_pad], [True, False]))
    conv_w_full = jnp.transpose(cw_all.reshape(4, 8, 128)[:, :3, :HD], (1, 0, 2)).reshape(3, CONV_W)
    conv_w8 = jnp.zeros((8, CONV_W), f32).at[:3].set(conv_w_full)
    sink_rows = jnp.broadcast_to(attn_sinks.reshape(N_Q, 1), (N_Q, 128))

    x2d = x.reshape(T, D)
    mem2d = mem.reshape(-1, D)
    (xn, proj, qkv), (wgT,) = in_proj_fwd(x2d, pk, winT, TM, gather_exchange([wgT_s], [True]))
    (attn_out,), (wuT, wout, wmkv) = attn_fwd(qkv, sink_rows, BL, S,
                                              gather_exchange([wuT_s, wout_s, wmkv_s], [True, True, True]))
    memn, kv, km, vm = mem_kv_fwd(mem2d, pk, wmkv)
    (conv_out, mem_out, merged, x1, h), (wd,) = mixer_tail_fwd(
        x2d, attn_out, proj, qkv, km, vm, conv_w8, pk, wout, S, TM, gather_exchange([wd_s], [True]))

    dx1, dx2b, act, d_gate, d_up, loss8, d_norm_ffn = ffn_fwd_bwd(h, x1, loss_target.reshape(T, D), wgT, wuT, wd, pk, TM)
    F = wd.shape[0]
    g_wd = matmul_tn(act, dx2b, "dw_down", F // 2, min(T, 1024))
    g_wgT = matmul_tn(d_gate, h, "dw_gate", F // 2, min(T, 1024))
    g_wuT = matmul_tn(d_up, h, "dw_up", F // 2, min(T, 1024))

    d_attn, d_conv_out, d_mem_out, g_wout, d_gains = out_proj_bwd(dx1, merged, attn_out, conv_out, mem_out, pk, wout, TM)
    late = [_halves_view(g) for g in (g_wgT, g_wuT, g_wd, g_wout)]
    (dqmn, dkm, dvm, dcb, dcv, d_cw8, d_cbias), late_sib = mem_conv_bwd(
        d_mem_out, mem_out, d_conv_out, proj, qkv, km, vm, conv_w8, pk, S, TM, halves_exchange(late))
    late_part = add_halves(cidx, late, late_sib, "grad_add_halves_ffn")
    (dqn, dkn, dv, d_sink8), late_stage = attn_bwd(qkv, d_attn, attn_out, sink_rows, BL, S, scatter_exchange(late_part))
    g_x, g_winT, d_norm_mix, d_qg, d_kg, d_mqg = in_proj_bwd(
        dqn, dkn, dv, dcb, dcv, dqmn, proj, conv_w8, xn, x2d, dx1, pk, winT, S, TM)
    g_wmkv, d_norm_mem, d_mkg = mem_kv_bwd(dkm, dvm, kv, memn, mem2d, pk, wmkv)

    tot, tail_stage = tail_reduce(d_norm_mix, d_norm_mem, d_norm_ffn, d_gains, d_cw8, d_cbias, d_qg, d_kg, d_mqg, d_mkg,
                                  d_sink8, loss8, [_halves_view(g) for g in (g_winT, g_wmkv)])
    loss = tot[5, 384]
    late_res, _ = adamw_big("adamw_late", late_stage, w_rb[1:5], m_rb[1:5], v_rb[1:5], 8)
    tail_res, _ = adamw_big("adamw_tail", tail_stage, [w_rb[0], w_rb[5]], [m_rb[0], m_rb[5]], [v_rb[0], v_rb[5]], 4)
    res = {"w_in": [a.T[None] for a in tail_res[0]], "w_gate": [a.T[None] for a in late_res[0]],
           "w_up": [a.T[None] for a in late_res[1]], "w_down": [a[None] for a in late_res[2]],
           "w_out": [a[None] for a in late_res[3]], "w_mem_kv": [a[None] for a in tail_res[1]]}
    res.update(adamw_small(tot, pk, _pack_small(m_small), _pack_small(v_small), {k: w_small[k].shape for k in SMALL}))

    order = ["norm_mix", "w_in", "q_norm", "k_norm", "attn_sinks", "conv_w", "conv_b", "norm_mem", "w_mem_kv",
             "mem_q_norm", "mem_k_norm", "out_norm_attn", "out_norm_conv", "out_norm_mem", "w_out", "norm_ffn",
             "w_gate", "w_up", "w_down"]
    return (loss, g_x.reshape(BL, S, D), *[res[n][0] for n in order], *[res[n][1] for n in order],
            *[res[n][2] for n in order], *[res[n][3] for n in order])
```

```python
import collections
import functools

import jax
import jax.numpy as jnp
import numpy as np
from jax import lax
from jax.experimental import pallas as pl
from jax.experimental.pallas import tpu as pltpu

f32 = jnp.float32
MXU = jnp.bfloat16
WIRE = jnp.bfloat16
EPS = 1e-6
NEG = -1e30
HD = 64
BLK = 128
N_Q, N_KV, N_MEMH = 8, 2, 4
GQA = N_Q // N_KV
ATT_W, KV_W, CONV_W, MEM_W = 512, 128, 256, 256
VMEM_MIB = 1024 * 1024
ADAM_LR, ADAM_B1, ADAM_B2, ADAM_EPS, ADAM_WD, ADAM_STEP = 0.001, 0.9, 0.999, 1e-08, 0.01, 10

MESH = pl.DeviceIdType.MESH
VM = pl.BlockSpec(memory_space=pltpu.VMEM)
ANY = pl.BlockSpec(memory_space=pl.ANY)
SDS = jax.ShapeDtypeStruct
DMA = pltpu.SemaphoreType.DMA


def _c(v):
    return v.astype(MXU)


def _nn(a, b):
    return lax.dot_general(a, b, (((1,), (0,)), ((), ())), preferred_element_type=f32)


def _nt(a, b):
    return lax.dot_general(a, b, (((1,), (1,)), ((), ())), preferred_element_type=f32)


def _tn(a, b):
    return lax.dot_general(a, b, (((0,), (0,)), ((), ())), preferred_element_type=f32)


def _rstd(v):
    return lax.rsqrt(jnp.mean(v * v, axis=-1, keepdims=True) + EPS)


def _norm_bwd(dy, v, r, g):
    dyg = dy * g
    dv = r * dyg - v * (r * r * r) * jnp.mean(dyg * v, axis=-1, keepdims=True)
    return dv, jnp.sum(dy * v * r, axis=0, keepdims=True)


def _split3(v):
    hi = _c(v)
    r1 = v - hi.astype(f32)
    mid = _c(r1)
    return hi, mid, _c(r1 - mid.astype(f32))


def _rowsum_mxu(v, width):
    ones = jnp.ones((v.shape[1], width), MXU)
    return sum(_nn(a, ones) for a in _split3(v))


def _seg_sums(v):
    r = lax.broadcasted_iota(jnp.int32, (2 * HD, 2 * HD), 0) // HD
    c = lax.broadcasted_iota(jnp.int32, (2 * HD, 2 * HD), 1) // HD
    bd = (r == c).astype(MXU)
    outs = []
    for b in range(v.shape[1] // (2 * HD)):
        outs.append(sum(_nn(a, bd) for a in _split3(v[:, b * 2 * HD:(b + 1) * 2 * HD])))
    return outs[0] if len(outs) == 1 else jnp.concatenate(outs, axis=1)


def _lanes(g, width):
    return jnp.concatenate([g] * (width // HD), axis=1)


def _heads_rstd(v):
    return lax.rsqrt(_seg_sums(v * v) * (1.0 / HD) + EPS)


def _heads_norm_bwd(dy, v, g):
    r = _heads_rstd(v)
    gl = _lanes(g, v.shape[1])
    dyg = dy * gl
    dv = r * dyg - v * (r * r * r) * (_seg_sums(dyg * v) * (1.0 / HD))
    dgl = jnp.sum(dy * v * r, axis=0, keepdims=True)
    return dv, sum(dgl[:, s * HD:(s + 1) * HD] for s in range(v.shape[1] // HD))


def _exp_scores(s, extra=None):
    m = jnp.max(s, axis=-1, keepdims=True)
    if extra is None:
        return jnp.exp(s - m), None
    m = jnp.maximum(m, extra)
    return jnp.exp(s - m), jnp.exp(extra - m)


def _place():
    return lax.axis_index("x"), lax.axis_index("y"), lax.axis_index("c")


SMALL_AT = {"norm_mix": (0, 0, 1024), "norm_mem": (1, 0, 1024), "norm_ffn": (2, 0, 1024),
            "out_norm_attn": (3, 0, ATT_W), "out_norm_conv": (3, ATT_W, CONV_W), "out_norm_mem": (3, ATT_W + CONV_W, MEM_W),
            "conv_b": (4, 3 * CONV_W, CONV_W), "q_norm": (5, 0, HD), "k_norm": (5, HD, HD), "mem_q_norm": (5, 2 * HD, HD),
            "mem_k_norm": (5, 3 * HD, HD), "attn_sinks": (5, 256, N_Q)}
SMALL = ("norm_mix", "norm_mem", "norm_ffn", "out_norm_attn", "out_norm_conv", "out_norm_mem", "conv_w", "conv_b",
         "q_norm", "k_norm", "mem_q_norm", "mem_k_norm", "attn_sinks")


def _small(pk_ref, name):
    r, c0, w = SMALL_AT[name]
    return pk_ref[r:r + 1, c0:c0 + w]


def _pack_small(d):
    z = lambda n: jnp.zeros((1, n), f32)
    row3 = jnp.concatenate([d["out_norm_attn"], d["out_norm_conv"], d["out_norm_mem"]], axis=1)
    row4 = jnp.concatenate([d["conv_w"].reshape(1, 3 * HD), z(3 * CONV_W - 3 * HD), d["conv_b"]], axis=1)
    row5 = jnp.concatenate([d["q_norm"], d["k_norm"], d["mem_q_norm"], d["mem_k_norm"], d["attn_sinks"],
                            z(1024 - 4 * HD - N_Q)], axis=1)
    return jnp.concatenate([d["norm_mix"], d["norm_mem"], d["norm_ffn"], row3, row4, row5, z(1024), z(1024)], axis=0)


def _other_chips(x, y):
    return [(1 - x, y), (x, 1 - y), (1 - x, 1 - y)]


Exchange = collections.namedtuple("Exchange", "ins outs sems start finish relay relay_steps_before_end", defaults=(None, 0))


def _run(name, body, grid, ins, in_specs, out_shape, out_specs, scratch=(), vmem_mib=32, exchange=None):
    ins, in_specs, out_shape, out_specs, scratch = list(ins), list(in_specs), list(out_shape), list(out_specs), list(scratch)
    ni, no, ns = len(ins), len(out_shape), len(scratch)
    ex = exchange
    if ex is not None:
        nxi, nxo = len(ex.ins), len(ex.outs)

    def call_body(*refs):
        if ex is None:
            body(*refs)
            return
        a, xa = refs[:ni], refs[ni:ni + nxi]
        o, xo = refs[ni + nxi:ni + nxi + no], refs[ni + nxi + no:ni + nxi + no + nxo]
        s, xs = refs[ni + nxi + no + nxo:ni + nxi + no + nxo + ns], refs[ni + nxi + no + nxo + ns:]
        if grid:
            first = functools.reduce(jnp.logical_and, [pl.program_id(d) == 0 for d in range(len(grid))])
            last = functools.reduce(jnp.logical_and, [pl.program_id(d) == grid[d] - 1 for d in range(len(grid))])
            pl.when(first)(lambda: ex.start(xa, xo, xs))
            body(*a, *o, *s)
            if ex.relay is not None:
                early = functools.reduce(jnp.logical_and, [pl.program_id(d) == grid[d] - 1 for d in range(len(grid) - 1)],
                                         pl.program_id(len(grid) - 1) == grid[-1] - 1 - ex.relay_steps_before_end)
                pl.when(early)(lambda: ex.relay(xa, xo, xs))
            pl.when(last)(lambda: ex.finish(xa, xo, xs))
        else:
            ex.start(xa, xo, xs)
            if body is not None:
                body(*a, *o, *s)
            if ex.relay is not None:
                ex.relay(xa, xo, xs)
            ex.finish(xa, xo, xs)

    if ex is not None:
        ins, in_specs = ins + list(ex.ins), in_specs + [ANY] * nxi
        out_shape, out_specs = out_shape + list(ex.outs), out_specs + [ANY] * nxo
        scratch = scratch + list(ex.sems)
    kw = dict(grid=grid) if grid else {}
    res = pl.pallas_call(
        call_body, name=name, out_shape=out_shape, in_specs=in_specs, out_specs=out_specs, scratch_shapes=scratch,
        compiler_params=pltpu.CompilerParams(dimension_semantics=("arbitrary",) * len(grid) if grid else None,
                                             vmem_limit_bytes=vmem_mib * VMEM_MIB), **kw)(*ins)
    res = list(res)
    return (res[:no], res[no:]) if ex is not None else res


def _remote(src, dst, ssem, rsem, dev):
    return pltpu.make_async_remote_copy(src_ref=src, dst_ref=dst, send_sem=ssem, recv_sem=rsem,
                                        device_id=dev, device_id_type=MESH)


def gather_exchange(shards, split, relay_early=0):
    n = len(shards)

    def rows(ref, e, kk, half=None):
        R = shards[e].shape[0]
        if half is None:
            return ref.at[pl.ds(pl.multiple_of(kk * R, 8), R)]
        return ref.at[pl.ds(pl.multiple_of(kk * R + half * (R // 2), 8), R // 2)]

    def ici(src, dst, sm, e, j, chip_j, x, y, c):
        k = 2 * x + y
        if split[e]:
            s = src[e].at[pl.ds(pl.multiple_of(c * (shards[e].shape[0] // 2), 8), shards[e].shape[0] // 2)]
            return _remote(s, rows(dst[e], e, k, c), sm[0].at[6 * e + j], sm[1].at[6 * e + j], (*chip_j, c))
        return _remote(src[e], rows(dst[e], e, k), sm[0].at[6 * e + j], sm[1].at[6 * e + j], (*chip_j, c))

    def landed(dst, e, chip_j, c):
        kj = 2 * chip_j[0] + chip_j[1]
        return rows(dst[e], e, kj, c) if split[e] else rows(dst[e], e, kj)

    def forward(dst, sm, e, j, chip_j, x, y, c, sender_c):
        kj = 2 * chip_j[0] + chip_j[1]
        r = rows(dst[e], e, kj, sender_c)
        return _remote(r, r, sm[0].at[6 * e + 3 + j], sm[1].at[6 * e + 3 + j], (x, y, 1 - c))

    def local(src, dst, sm, e, x, y):
        return pltpu.make_async_copy(src[e], rows(dst[e], e, 2 * x + y), sm[2].at[e])

    def start(src, dst, sm):
        x, y, c = _place()
        for e in range(n):
            local(src, dst, sm, e, x, y).start()
            for j, chip_j in enumerate(_other_chips(x, y)):
                ici(src, dst, sm, e, j, chip_j, x, y, c).start()

    def relay(src, dst, sm):
        x, y, c = _place()
        for e in range(n):
            for j, chip_j in enumerate(_other_chips(x, y)):
                r = landed(dst, e, chip_j, c)
                _remote(r, r, sm[0].at[6 * e + j], sm[1].at[6 * e + j], (*chip_j, c)).wait_recv()
                if split[e]:
                    forward(dst, sm, e, j, chip_j, x, y, c, c).start()

    def finish(src, dst, sm):
        x, y, c = _place()
        chips = _other_chips(x, y)
        for e in range(n):
            for j, chip_j in enumerate(chips):
                if split[e]:
                    forward(dst, sm, e, j, chip_j, x, y, c, 1 - c).wait_recv()
        for e in range(n):
            for j, chip_j in enumerate(chips):
                ici(src, dst, sm, e, j, chip_j, x, y, c).wait_send()
                if split[e]:
                    forward(dst, sm, e, j, chip_j, x, y, c, c).wait_send()
            local(src, dst, sm, e, x, y).wait()

    outs = [SDS((4 * s.shape[0], s.shape[1]), s.dtype) for s in shards]
    return Exchange(list(shards), outs, [DMA((6 * n,)), DMA((6 * n,)), DMA((n,))], start, finish, relay, relay_early)


def halves_exchange(grads):
    n = len(grads)

    def copy(g, st, sm, e, x, y, c):
        return _remote(g[e].at[:, 1 - c], st[e], sm[0].at[e], sm[1].at[e], (x, y, 1 - c))

    def start(g, st, sm):
        x, y, c = _place()
        for e in range(n):
            copy(g, st, sm, e, x, y, c).start()

    def finish(g, st, sm):
        x, y, c = _place()
        for e in range(n):
            copy(g, st, sm, e, x, y, c).wait()

    outs = [SDS((4,) + a.shape[2:], a.dtype) for a in grads]
    return Exchange(list(grads), outs, [DMA((n,)), DMA((n,))], start, finish)


def scatter_exchange(parts):
    n = len(parts)

    def ici(p, st, sm, e, j, chip_j, x, y, c):
        k, kj = 2 * x + y, 2 * chip_j[0] + chip_j[1]
        return _remote(p[e].at[kj], st[e].at[c, k], sm[0].at[8 * e + j], sm[1].at[8 * e + j], (*chip_j, c))

    def own(p, st, sm, e, x, y, c):
        k = 2 * x + y
        return _remote(p[e].at[k], st[e].at[c, k], sm[0].at[8 * e + 3], sm[1].at[8 * e + 3], (x, y, 1 - c))

    def forward(st, sm, e, j, chip_j, x, y, c, sender_c):
        kj = 2 * chip_j[0] + chip_j[1]
        r = st[e].at[sender_c, kj]
        return _remote(r, r, sm[0].at[8 * e + 4 + j], sm[1].at[8 * e + 4 + j], (x, y, 1 - c))

    def local(p, st, sm, e, x, y, c):
        k = 2 * x + y
        return pltpu.make_async_copy(p[e].at[k], st[e].at[c, k], sm[2].at[e])

    def start(p, st, sm):
        x, y, c = _place()
        for e in range(n):
            local(p, st, sm, e, x, y, c).start()
            own(p, st, sm, e, x, y, c).start()
            for j, chip_j in enumerate(_other_chips(x, y)):
                ici(p, st, sm, e, j, chip_j, x, y, c).start()

    def relay(p, st, sm):
        x, y, c = _place()
        for e in range(n):
            for j, chip_j in enumerate(_other_chips(x, y)):
                kj = 2 * chip_j[0] + chip_j[1]
                r = st[e].at[c, kj]
                _remote(r, r, sm[0].at[8 * e + j], sm[1].at[8 * e + j], (*chip_j, c)).wait_recv()
                forward(st, sm, e, j, chip_j, x, y, c, c).start()

    def finish(p, st, sm):
        x, y, c = _place()
        k = 2 * x + y
        chips = _other_chips(x, y)
        for e in range(n):
            r = st[e].at[1 - c, k]
            _remote(r, r, sm[0].at[8 * e + 3], sm[1].at[8 * e + 3], (x, y, 1 - c)).wait_recv()
            for j, chip_j in enumerate(chips):
                forward(st, sm, e, j, chip_j, x, y, c, 1 - c).wait_recv()
        for e in range(n):
            own(p, st, sm, e, x, y, c).wait_send()
            for j, chip_j in enumerate(chips):
                ici(p, st, sm, e, j, chip_j, x, y, c).wait_send()
                forward(st, sm, e, j, chip_j, x, y, c, c).wait_send()
            local(p, st, sm, e, x, y, c).wait()

    outs = [SDS((2,) + a.shape, a.dtype) for a in parts]
    return Exchange(list(parts), outs, [DMA((8 * n,)), DMA((8 * n,)), DMA((n,))], start, finish, relay)


def tail_reduce(d_norm_mix, d_norm_mem, d_norm_ffn, d_gains, d_cw8, d_cbias, d_qg, d_kg, d_mqg, d_mkg, d_sink8, loss8, tail):
    n = len(tail)
    halves = halves_exchange(tail)
    scatter = scatter_exchange([SDS((4,) + a.shape[2:], WIRE) for a in tail])

    def body(nm_ref, nmem_ref, nf_ref, gn_ref, cw_ref, cb_ref, qg_ref, kg_ref, mqg_ref, mkg_ref, sk_ref, ls_ref, *rest):
        g, o_ref, st = rest[:n], rest[n], rest[n + 1:2 * n + 1]
        buf, ssem, rsem = rest[2 * n + 1:2 * n + 4]
        own, sib, part = (rest[2 * n + 4 + i * n:2 * n + 4 + (i + 1) * n] for i in range(3))
        lsem = rest[5 * n + 4]
        hsem, xsem = rest[5 * n + 5:5 * n + 7], rest[5 * n + 7:]
        x, y, c = _place()
        loads = [pltpu.make_async_copy(g[e].at[:, c], own[e], lsem.at[e]) for e in range(n)]
        for ld in loads:
            ld.start()
        halves.start(g, sib, hsem)
        me = 4 * x + 2 * y + c
        mine = buf.at[me]
        mine[...] = jnp.zeros((8, 1024), f32)
        mine[0:1, :] = nm_ref[...]
        mine[1:2, :] = nmem_ref[...]
        mine[2:3, :] = nf_ref[...]
        mine[3:4, :] = gn_ref[...]
        for j in range(3):
            mine[4:5, pl.ds(j * CONV_W, CONV_W)] = cw_ref[j:j + 1, :]
        mine[4:5, pl.ds(3 * CONV_W, CONV_W)] = cb_ref[...]
        for j, r in enumerate((qg_ref, kg_ref, mqg_ref, mkg_ref)):
            mine[5:6, pl.ds(j * HD, HD)] = r[...]
        mine[5:6, pl.ds(256, 128)] = sk_ref[0:1, :]
        mine[5:6, pl.ds(384, 128)] = ls_ref[0:1, :]

        def peer_of(m):
            return (1 - x if m & 4 else x, 1 - y if m & 2 else y, 1 - c if m & 1 else c)

        for m in range(1, 8):
            _remote(mine, mine, ssem.at[m - 1], rsem.at[m - 1], peer_of(m)).start()
        for ld in loads:
            ld.wait()
        halves.finish(g, sib, hsem)
        for e in range(n):
            part[e][...] = (own[e][...] + sib[e][...]).astype(WIRE)
        scatter.start(part, st, xsem)
        scatter.relay(part, st, xsem)
        scatter.finish(part, st, xsem)
        for m in range(1, 8):
            p = peer_of(m)
            got = buf.at[4 * p[0] + 2 * p[1] + p[2]]
            _remote(got, got, ssem.at[m - 1], rsem.at[m - 1], p).wait_recv()
        for m in range(1, 8):
            _remote(mine, mine, ssem.at[m - 1], rsem.at[m - 1], peer_of(m)).wait_send()
        acc = buf[0]
        for d in range(1, 8):
            acc = acc + buf[d]
        o_ref[...] = acc

    ins = [d_norm_mix, d_norm_mem, d_norm_ffn, d_gains, d_cw8, d_cbias, d_qg, d_kg, d_mqg, d_mkg, d_sink8, loss8]
    half_shape = [(4,) + a.shape[2:] for a in tail]
    scratch = ([pltpu.VMEM((8, 8, 1024), f32), DMA((7,)), DMA((7,))]
               + [pltpu.VMEM(s, f32) for s in half_shape] * 2 + [pltpu.VMEM(s, WIRE) for s in half_shape]
               + [DMA((n,))] + list(halves.sems) + list(scatter.sems))
    res = _run("tail_reduce", body, (), ins + list(tail), [VM] * len(ins) + [ANY] * n,
               [SDS((8, 1024), f32)] + list(scatter.outs), [VM] + [ANY] * n, scratch=scratch, vmem_mib=40)
    return res[0], res[1:]


def add_halves(cidx, grads, stages, name, nch=2):
    n = len(grads)

    def body(c_ref, *refs):
        g, st, o = refs[:n], refs[n:2 * n], refs[2 * n:]
        for e in range(n):
            o[e][...] = (g[e][...] + st[e][...]).astype(WIRE)

    in_specs, out_specs, out_shape = [], [], []
    for a in grads:
        hr, C = a.shape[2], a.shape[3]
        in_specs.append(pl.BlockSpec((None, None, hr // nch, C), lambda s, q, c_ref: (s, c_ref[0], q, 0)))
    for a in stages:
        hr, C = a.shape[1], a.shape[2]
        in_specs.append(pl.BlockSpec((None, hr // nch, C), lambda s, q, c_ref: (s, q, 0)))
        out_specs.append(pl.BlockSpec((None, hr // nch, C), lambda s, q, c_ref: (s, q, 0)))
        out_shape.append(SDS(a.shape, WIRE))
    return pl.pallas_call(
        body, name=name, out_shape=out_shape,
        grid_spec=pltpu.PrefetchScalarGridSpec(num_scalar_prefetch=1, grid=(4, nch), in_specs=in_specs, out_specs=out_specs),
        compiler_params=pltpu.CompilerParams(dimension_semantics=("arbitrary", "arbitrary")),
    )(cidx, *grads, *stages)


def _adamw_math(w, g, m, v):
    m = ADAM_B1 * m + (1.0 - ADAM_B1) * g
    v = ADAM_B2 * v + (1.0 - ADAM_B2) * (g * g)
    m_hat = m / (1.0 - ADAM_B1 ** ADAM_STEP)
    v_hat = v / (1.0 - ADAM_B2 ** ADAM_STEP)
    delta = -ADAM_LR * (m_hat / (jnp.sqrt(v_hat) + ADAM_EPS) + ADAM_WD * w)
    return delta, m, v


def _sum_chips(st):
    return ((st[0].astype(f32) + st[1].astype(f32)) + st[2].astype(f32)) + st[3].astype(f32)


def adamw_big(name, stages, ws, ms, vs, nstep, exchange=None):
    n = len(stages)

    def body(*refs):
        st, w, m, v = refs[:n], refs[n:2 * n], refs[2 * n:3 * n], refs[3 * n:4 * n]
        outs = refs[4 * n:]
        for e in range(n):
            g = jnp.concatenate([_sum_chips(st[e].at[0]), _sum_chips(st[e].at[1])], axis=0)
            d, mm, vv = _adamw_math(w[e][...], g, m[e][...], v[e][...])
            outs[4 * e][...] = g
            outs[4 * e + 1][...] = d
            outs[4 * e + 2][...] = mm
            outs[4 * e + 3][...] = vv

    st_specs, w_specs = [], []
    for e in range(n):
        _, _, hr, C = stages[e].shape
        st_specs.append(pl.BlockSpec((2, 4, hr, C // nstep), lambda i: (0, 0, 0, i)))
        w_specs.append(pl.BlockSpec((2 * hr, C // nstep), lambda i: (0, i)))
    out_specs = [s for s in w_specs for _ in range(4)]
    out_shape = [SDS(w.shape, f32) for w in ws for _ in range(4)]
    res = _run(name, body, (nstep,), list(stages) + list(ws) + list(ms) + list(vs), st_specs + w_specs * 3,
               out_shape, out_specs, vmem_mib=48, exchange=exchange)
    res, sent = res if exchange is not None else (res, None)
    return [res[4 * e:4 * e + 4] for e in range(n)], sent


def adamw_small(tot, pk_w, pk_m, pk_v, shapes):
    def body(tot_ref, w_ref, m_ref, v_ref, *outs):
        x, y, _ = _place()
        chip = 2 * x + y
        taps = []
        for j in range(3):
            mine = tot_ref[4:5, j * CONV_W:j * CONV_W + HD]
            for s in range(1, 4):
                mine = jnp.where(chip == s, tot_ref[4:5, j * CONV_W + s * HD:j * CONV_W + (s + 1) * HD], mine)
            taps.append(mine)
        row4 = jnp.concatenate(taps + [jnp.zeros((1, 3 * CONV_W - 3 * HD), f32), tot_ref[4:5, 3 * CONV_W:]], axis=1)
        tot_v = tot_ref[...]
        row = lax.broadcasted_iota(jnp.int32, tot_v.shape, 0)
        g = jnp.where(row == 4, jnp.broadcast_to(row4, tot_v.shape), tot_v)
        d, mm, vv = _adamw_math(w_ref[...], g, m_ref[...], v_ref[...])
        for i, name in enumerate(SMALL):
            for k, val in enumerate((g, d, mm, vv)):
                if name == "conv_w":
                    outs[4 * i + k][...] = jnp.concatenate([val[4:5, j * HD:(j + 1) * HD] for j in range(3)], axis=0)[None]
                else:
                    r, c0, w = SMALL_AT[name]
                    outs[4 * i + k][...] = val[r:r + 1, c0:c0 + w]

    out_shape = [SDS(shapes[k], f32) for k in SMALL for _ in range(4)]
    res = _run("adamw_small", body, (), [tot, pk_w, pk_m, pk_v], [VM] * 4, out_shape, [VM] * len(out_shape))
    return {k: res[4 * i:4 * i + 4] for i, k in enumerate(SMALL)}


def prep_weights(name, shards, exchange=None):
    n = len(shards)

    def body(*refs):
        for e in range(n):
            refs[n + e][...] = _c(refs[e][...])

    return _run(name, body, (), shards, [VM] * n, [SDS(a.shape, MXU) for a in shards], [VM] * n, vmem_mib=48, exchange=exchange)


def mem_kv_fwd(mem2d, pk, wmkv):
    M, D = mem2d.shape

    def body(m_ref, pk_ref, w_ref, mn_ref, kv_ref, km_ref, vm_ref):
        m = m_ref[...]
        mn = _c(m * _rstd(m) * _small(pk_ref, "norm_mem"))
        mn_ref[...] = mn
        kv = _nn(mn, w_ref[...])
        kv_ref[...] = kv
        kk = kv[:, :MEM_W]
        km_ref[...] = _c(kk * _heads_rstd(kk) * _lanes(_small(pk_ref, "mem_k_norm"), MEM_W))
        vm_ref[...] = _c(kv[:, MEM_W:])

    return _run("mem_kv_fwd", body, (), [mem2d, pk, wmkv], [VM] * 3,
                [SDS((M, D), MXU), SDS((M, 2 * MEM_W), f32), SDS((M, MEM_W), MXU), SDS((M, MEM_W), MXU)], [VM] * 4)


QKV_W = ATT_W + 2 * KV_W + MEM_W


def in_proj_fwd(x2d, pk, winT, tm, exchange):
    T, D = x2d.shape
    P = winT.shape[0]

    def body(x_ref, pk_ref, w_ref, xn_ref, proj_ref, qkv_ref):
        xv = x_ref[...]
        xn = _c(xv * _rstd(xv) * _small(pk_ref, "norm_mix"))
        xn_ref[...] = xn
        proj = _nt(xn, w_ref[...])
        proj_ref[...] = proj
        q, k = proj[:, :ATT_W], proj[:, ATT_W:ATT_W + KV_W]
        qm = proj[:, P - MEM_W:]
        qkv_ref[...] = jnp.concatenate(
            [_c(q * _heads_rstd(q) * _lanes(_small(pk_ref, "q_norm"), ATT_W)),
             _c(k * _heads_rstd(k) * _lanes(_small(pk_ref, "k_norm"), KV_W)),
             _c(proj[:, ATT_W + KV_W:ATT_W + 2 * KV_W]),
             _c(qm * _heads_rstd(qm) * _lanes(_small(pk_ref, "mem_q_norm"), MEM_W))], axis=1)

    return _run("in_proj_fwd", body, (T // tm,), [x2d, pk, winT],
                [pl.BlockSpec((tm, D), lambda i: (i, 0)), VM, VM],
                [SDS((T, D), MXU), SDS((T, P), f32), SDS((T, QKV_W), MXU)],
                [pl.BlockSpec((tm, D), lambda i: (i, 0)), pl.BlockSpec((tm, P), lambda i: (i, 0)),
                 pl.BlockSpec((tm, QKV_W), lambda i: (i, 0))],
                vmem_mib=40, exchange=exchange)


def _swa_bias_table():
    r = np.arange(GQA * BLK)[:, None]
    k = np.arange(2 * BLK)[None, :]
    dist = (r % BLK) + BLK - k
    band = (dist >= 0) & (dist < BLK)
    tab = np.empty((2, N_KV, GQA * BLK, 2 * BLK), np.float32)
    for later in range(2):
        valid = band & ((k >= BLK) | (later == 1))
        for g in range(N_KV):
            slope = 2.0 ** -(g * GQA + r // BLK + 1.0)
            tab[later, g] = np.where(valid, -slope * dist, NEG)
    return jnp.asarray(tab)


def _sink_column(g, sk_ref):
    hrow = lax.broadcasted_iota(jnp.int32, (GQA * BLK, 1), 0) // BLK
    sink = jnp.zeros((GQA * BLK, 1), f32)
    for hh in range(GQA):
        sink = jnp.where(hrow == hh, sk_ref[g * GQA + hh:g * GQA + hh + 1, 0:1], sink)
    return sink


def _stack_heads(v, g):
    return jnp.concatenate([v[:, (g * GQA + hh) * HD:(g * GQA + hh + 1) * HD] for hh in range(GQA)], axis=0)


def attn_fwd(qkv, sink_rows, BL, S, exchange):
    NB = S // BLK
    T = BL * S

    def body(q_ref, kc_ref, kp_ref, vc_ref, vp_ref, sk_ref, tab_ref, o_ref):
        q = q_ref[...]
        k2 = jnp.concatenate([kp_ref[...], kc_ref[...]], axis=0)
        v2 = jnp.concatenate([vp_ref[...], vc_ref[...]], axis=0)
        ones = jnp.ones((2 * BLK, HD), MXU)
        for g in range(N_KV):
            kn, vh = k2[:, g * HD:(g + 1) * HD], v2[:, g * HD:(g + 1) * HD]
            s = _nt(_stack_heads(q, g), kn) * (HD ** -0.5) + tab_ref[g]
            e, es = _exp_scores(s, _sink_column(g, sk_ref))
            eb = _c(e)
            o = _nn(eb, vh) * (1.0 / (_nn(eb, ones) + es))
            for hh in range(GQA):
                o_ref[:, pl.ds((g * GQA + hh) * HD, HD)] = o[hh * BLK:(hh + 1) * BLK]

    cur = lambda col: (lambda b, j: (b * NB + j, col))
    prev = lambda col: (lambda b, j: (b * NB + jnp.maximum(j - 1, 0), col))
    return _run("attn_fwd", body, (BL, NB), [qkv, qkv, qkv, qkv, qkv, sink_rows, _swa_bias_table()],
                [pl.BlockSpec((BLK, ATT_W), cur(0)),
                 pl.BlockSpec((BLK, KV_W), cur(4)), pl.BlockSpec((BLK, KV_W), prev(4)),
                 pl.BlockSpec((BLK, KV_W), cur(5)), pl.BlockSpec((BLK, KV_W), prev(5)),
                 pl.BlockSpec((8, 128), lambda b, j: (0, 0)),
                 pl.BlockSpec((None, N_KV, GQA * BLK, 2 * BLK), lambda b, j: (jnp.minimum(j, 1), 0, 0, 0))],
                [SDS((T, ATT_W), f32)], [pl.BlockSpec((BLK, ATT_W), cur(0))], exchange=exchange)


def _conv_taps(u, uh):
    row = lax.broadcasted_iota(jnp.int32, u.shape, 0)
    u1 = jnp.where(row == 0, uh[7:8, :], pltpu.roll(u, 1, 0))
    u2 = jnp.where(row == 0, uh[6:7, :], jnp.where(row == 1, uh[7:8, :], pltpu.roll(u, 2, 0)))
    return u1, u2


def _mem_head(qm, km, vm, h):
    qh, kh, vh = (a[:, h * HD:(h + 1) * HD] for a in (qm, km, vm))
    e, _ = _exp_scores(_nt(qh, kh) * (HD ** -0.5))
    return qh, kh, vh, e


def mixer_tail_fwd(x2d, attn_out, proj, qkv, km, vm, conv_w8, pk, wout, S, tm, exchange):
    T, D = x2d.shape
    NM = km.shape[0] // (T // S)

    def body(x_ref, ao_ref, ch_ref, cb_ref, cc_ref, chh_ref, cch_ref, qm_ref, km_ref, vm_ref, cw_ref, pk_ref,
             wout_ref, co_ref, mo_ref, mg_ref, x1_ref, h_ref):
        first = (pl.program_id(0) * tm) % S == 0
        u = cc_ref[...] * ch_ref[...]
        uh = jnp.where(first, 0.0, cch_ref[...] * chh_ref[...])
        u1, u2 = _conv_taps(u, uh)
        conv = cw_ref[0:1, :] * u2 + cw_ref[1:2, :] * u1 + cw_ref[2:3, :] * u + _small(pk_ref, "conv_b")
        conv_out = cb_ref[...] * conv
        co_ref[...] = conv_out
        qm, kmv, vmv = qm_ref[...], km_ref[...], vm_ref[...]
        ones = jnp.ones((NM, HD), MXU)
        for h in range(N_MEMH):
            _, _, vh, e = _mem_head(qm, kmv, vmv, h)
            eb = _c(e)
            mo_ref[:, pl.ds(h * HD, HD)] = _nn(eb, vh) * (1.0 / _nn(eb, ones))
        mem_out = mo_ref[...]
        ao = ao_ref[...]
        merged = _c(jnp.concatenate([ao * _rstd(ao) * _small(pk_ref, "out_norm_attn"),
                                     conv_out * _rstd(conv_out) * _small(pk_ref, "out_norm_conv"),
                                     mem_out * _rstd(mem_out) * _small(pk_ref, "out_norm_mem")], axis=1))
        mg_ref[...] = merged
        x1 = x_ref[...] + _nn(merged, wout_ref[...])
        x1_ref[...] = x1
        h_ref[...] = _c(x1 * _rstd(x1) * _small(pk_ref, "norm_ffn"))

    tile = lambda w, col: pl.BlockSpec((tm, w), lambda i: (i, col))
    halo = lambda col: pl.BlockSpec((8, CONV_W), lambda i: (jnp.maximum(i * (tm // 8) - 1, 0), col))
    seq = pl.BlockSpec((NM, MEM_W), lambda i: ((i * tm) // S, 0))
    small = lambda a: pl.BlockSpec(a.shape, lambda i: (0, 0))
    return _run("mixer_tail_fwd", body, (T // tm,),
                [x2d, attn_out, proj, proj, proj, proj, proj, qkv, km, vm, conv_w8, pk, wout],
                [tile(D, 0), tile(ATT_W, 0), tile(CONV_W, 3), tile(CONV_W, 4), tile(CONV_W, 5), halo(3), halo(5),
                 tile(MEM_W, 3), seq, seq, VM, VM, VM],
                [SDS((T, CONV_W), f32), SDS((T, MEM_W), f32), SDS((T, D), MXU), SDS((T, D), f32), SDS((T, D), MXU)],
                [tile(CONV_W, 0), tile(MEM_W, 0), tile(D, 0), tile(D, 0), tile(D, 0)], vmem_mib=40, exchange=exchange)


def ffn_fwd_bwd(h, x1, tgt, wgT, wuT, wd, pk, tm):
    T, D = x1.shape
    F = wd.shape[0]

    def body(h_ref, x1_ref, t_ref, wg_ref, wu_ref, wd_ref, pk_ref,
             dx1_ref, dx2_ref, act_ref, dg_ref, du_ref, loss_ref, dgf_ref):
        @pl.when(pl.program_id(0) == 0)
        def _():
            loss_ref[...] = jnp.zeros_like(loss_ref)
            dgf_ref[...] = jnp.zeros_like(dgf_ref)

        hv = h_ref[...]
        gate = _nt(hv, wg_ref[...])
        up = _nt(hv, wu_ref[...])
        sg = jax.nn.sigmoid(gate)
        sl = gate * sg
        act = _c(sl * up)
        act_ref[...] = act
        x1v = x1_ref[...]
        diff = (x1v + _nn(act, wd_ref[...])) - t_ref[...]
        loss_ref[...] += 0.5 * jnp.sum(jnp.sum(diff * diff, axis=-1, keepdims=True) / D, axis=0, keepdims=True)
        dx2 = diff / D
        dx2b = _c(dx2)
        dx2_ref[...] = dx2b
        d_act = _nt(dx2b, wd_ref[...])
        d_up = _c(d_act * sl)
        d_gate = _c(d_act * up * (sg * (1.0 + gate * (1.0 - sg))))
        du_ref[...] = d_up
        dg_ref[...] = d_gate
        dh = _nn(d_gate, wg_ref[...]) + _nn(d_up, wu_ref[...])
        dv, dgf = _norm_bwd(dh, x1v, _rstd(x1v), _small(pk_ref, "norm_ffn"))
        dx1_ref[...] = dx2 + dv
        dgf_ref[...] += dgf

    tile = lambda w: pl.BlockSpec((tm, w), lambda i: (i, 0))
    return _run("ffn_fwd_bwd", body, (T // tm,), [h, x1, tgt, wgT, wuT, wd, pk],
                [tile(D), tile(D), tile(D), VM, VM, VM, VM],
                [SDS((T, D), f32), SDS((T, D), MXU), SDS((T, F), MXU), SDS((T, F), MXU), SDS((T, F), MXU),
                 SDS((8, 128), f32), SDS((1, D), f32)],
                [tile(D), tile(D), tile(F), tile(F), tile(F), pl.BlockSpec((8, 128), lambda i: (0, 0)),
                 pl.BlockSpec((1, D), lambda i: (0, 0))], vmem_mib=56)


def matmul_tn(a, b, name, tmo, tk):
    T, M = a.shape
    N = b.shape[1]

    def body(a_ref, b_ref, o_ref):
        @pl.when(pl.program_id(1) == 0)
        def _():
            o_ref[...] = jnp.zeros_like(o_ref)

        o_ref[...] += _tn(a_ref[...], b_ref[...])

    return _run(name, body, (M // tmo, T // tk), [a, b],
                [pl.BlockSpec((tk, tmo), lambda m, k: (k, m)), pl.BlockSpec((tk, N), lambda m, k: (k, 0))],
                [SDS((M, N), f32)], [pl.BlockSpec((tmo, N), lambda m, k: (m, 0))], vmem_mib=48)[0]


def out_proj_bwd(dx1, merged, attn_out, conv_out, mem_out, pk, wout, tm):
    T, D = dx1.shape

    def body(dx1_ref, mg_ref, ao_ref, co_ref, mo_ref, pk_ref, w_ref,
             dao_ref, dco_ref, dmo_ref, dw_ref, dgain_ref):
        @pl.when(pl.program_id(0) == 0)
        def _():
            dw_ref[...] = jnp.zeros_like(dw_ref)
            dgain_ref[...] = jnp.zeros_like(dgain_ref)

        dxb = _c(dx1_ref[...])
        dw_ref[...] += _tn(mg_ref[...], dxb)
        dmg = _nt(dxb, w_ref[...])
        ao, co, mo = ao_ref[...], co_ref[...], mo_ref[...]
        da, ga = _norm_bwd(dmg[:, :ATT_W], ao, _rstd(ao), _small(pk_ref, "out_norm_attn"))
        dc, gc = _norm_bwd(dmg[:, ATT_W:ATT_W + CONV_W], co, _rstd(co), _small(pk_ref, "out_norm_conv"))
        dm, gm = _norm_bwd(dmg[:, ATT_W + CONV_W:], mo, _rstd(mo), _small(pk_ref, "out_norm_mem"))
        dao_ref[...] = da
        dco_ref[...] = dc
        dmo_ref[...] = dm
        dgain_ref[...] += jnp.concatenate([ga, gc, gm], axis=1)

    tile = lambda w: pl.BlockSpec((tm, w), lambda i: (i, 0))
    return _run("out_proj_bwd", body, (T // tm,), [dx1, merged, attn_out, conv_out, mem_out, pk, wout],
                [tile(D), tile(D), tile(ATT_W), tile(CONV_W), tile(MEM_W), VM, VM],
                [SDS((T, ATT_W), f32), SDS((T, CONV_W), f32), SDS((T, MEM_W), f32), SDS((D, D), f32), SDS((1, D), f32)],
                [tile(ATT_W), tile(CONV_W), tile(MEM_W), pl.BlockSpec((D, D), lambda i: (0, 0)),
                 pl.BlockSpec((1, D), lambda i: (0, 0))], vmem_mib=40)


def attn_bwd(qkv, d_attn, attn_out, sink_rows, BL, S, exchange):
    NB = S // BLK
    T = BL * S

    def body(q_ref, kc_ref, kp_ref, vc_ref, vp_ref, do_ref, ao_ref, sk_ref, tab_ref,
             dq_ref, dk_ref, dv_ref, dsk_ref, pend_k, pend_v):
        b, j = pl.program_id(0), pl.program_id(1)

        @pl.when((b == 0) & (j == 0))
        def _():
            dsk_ref[...] = jnp.zeros_like(dsk_ref)

        @pl.when(j == 0)
        def _():
            pend_k[...] = jnp.zeros_like(pend_k)
            pend_v[...] = jnp.zeros_like(pend_v)

        @pl.when(j < NB)
        def _():
            q, do, ao = q_ref[...], do_ref[...], ao_ref[...]
            k2 = jnp.concatenate([kp_ref[...], kc_ref[...]], axis=0)
            v2 = jnp.concatenate([vp_ref[...], vc_ref[...]], axis=0)
            lane = lax.broadcasted_iota(jnp.int32, (8, 128), 1)
            ones_w = jnp.ones((2 * BLK, 2 * BLK), MXU)
            dsk = jnp.zeros((8, 128), f32)
            dks, dvs = [], []
            for g in range(N_KV):
                kn, vh = k2[:, g * HD:(g + 1) * HD], v2[:, g * HD:(g + 1) * HD]
                qs = _stack_heads(q, g)
                s = _nt(qs, kn) * (HD ** -0.5) + tab_ref[g]
                e, es = _exp_scores(s, _sink_column(g, sk_ref))
                eb = _c(e)
                inv_w = 1.0 / (_nn(eb, ones_w) + es)
                inv_n = inv_w[:, :HD]
                dos = _stack_heads(do, g)
                delta = _rowsum_mxu(dos * _stack_heads(ao, g), 2 * BLK)
                dp = _nt(_c(dos), vh)
                ds = _c(e * inv_w * (dp - delta) * (HD ** -0.5))
                t = es * inv_n[:, 0:1] * delta[:, 0:1]
                for hh in range(GQA):
                    dsk = dsk + jnp.where(lane == g * GQA + hh, -jnp.sum(t[hh * BLK:(hh + 1) * BLK]), 0.0)
                dvs.append(_tn(eb, _c(dos * inv_n)))
                dks.append(_tn(ds, qs))
                dqs = _nn(ds, kn)
                for hh in range(GQA):
                    dq_ref[:, pl.ds((g * GQA + hh) * HD, HD)] = dqs[hh * BLK:(hh + 1) * BLK]
            dk2 = jnp.concatenate(dks, axis=1)
            dv2 = jnp.concatenate(dvs, axis=1)
            dk_ref[...] = pend_k[...] + dk2[:BLK]
            dv_ref[...] = pend_v[...] + dv2[:BLK]
            pend_k[...] = dk2[BLK:]
            pend_v[...] = dv2[BLK:]
            dsk_ref[...] += dsk

        @pl.when(j == NB)
        def _():
            dk_ref[...] = pend_k[...]
            dv_ref[...] = pend_v[...]

    cur = lambda col: (lambda b, j: (b * NB + jnp.minimum(j, NB - 1), col))
    prev = lambda col: (lambda b, j: (b * NB + jnp.maximum(j - 1, 0), col))
    small = lambda shape: pl.BlockSpec(shape, lambda b, j: (0, 0))
    return _run("attn_bwd", body, (BL, NB + 1), [qkv, qkv, qkv, qkv, qkv, d_attn, attn_out, sink_rows, _swa_bias_table()],
                [pl.BlockSpec((BLK, ATT_W), cur(0)),
                 pl.BlockSpec((BLK, KV_W), cur(4)), pl.BlockSpec((BLK, KV_W), prev(4)),
                 pl.BlockSpec((BLK, KV_W), cur(5)), pl.BlockSpec((BLK, KV_W), prev(5)),
                 pl.BlockSpec((BLK, ATT_W), cur(0)), pl.BlockSpec((BLK, ATT_W), cur(0)), small((8, 128)),
                 pl.BlockSpec((None, N_KV, GQA * BLK, 2 * BLK), lambda b, j: (jnp.minimum(j, 1), 0, 0, 0))],
                [SDS((T, ATT_W), f32), SDS((T, KV_W), f32), SDS((T, KV_W), f32), SDS((8, 128), f32)],
                [pl.BlockSpec((BLK, ATT_W), cur(0)), pl.BlockSpec((BLK, KV_W), prev(0)),
                 pl.BlockSpec((BLK, KV_W), prev(0)), small((8, 128))],
                scratch=[pltpu.VMEM((BLK, KV_W), f32)] * 2, exchange=exchange)


def mem_conv_bwd(d_mem_out, mem_out, d_conv_out, proj, qkv, km, vm, conv_w8, pk, S, tm, exchange):
    T = d_mem_out.shape[0]
    NM = km.shape[0] // (T // S)

    def body(dmo_ref, mo_ref, dco_ref, ch_ref, cb_ref, cc_ref, chh_ref, cch_ref, qm_ref, km_ref, vm_ref, cw_ref,
             pk_ref, dqm_ref, dkm_ref, dvm_ref, dcb_ref, dcv_ref, dcw_ref, dcbias_ref):
        i = pl.program_id(0)
        first = (i * tm) % S == 0

        @pl.when(i == 0)
        def _():
            dcw_ref[...] = jnp.zeros_like(dcw_ref)
            dcbias_ref[...] = jnp.zeros_like(dcbias_ref)

        @pl.when(first)
        def _():
            dkm_ref[...] = jnp.zeros_like(dkm_ref)
            dvm_ref[...] = jnp.zeros_like(dvm_ref)

        qm, kmv, vmv, dmo, mo = qm_ref[...], km_ref[...], vm_ref[...], dmo_ref[...], mo_ref[...]
        ones_w = jnp.ones((NM, NM), MXU)
        for h in range(N_MEMH):
            qh, kh, vh, e = _mem_head(qm, kmv, vmv, h)
            eb = _c(e)
            doh = dmo[:, h * HD:(h + 1) * HD]
            delta = _rowsum_mxu(doh * mo[:, h * HD:(h + 1) * HD], NM)
            dp = _nt(_c(doh), vh)
            inv_w = 1.0 / _nn(eb, ones_w)
            ds = _c(e * inv_w * (dp - delta) * (HD ** -0.5))
            dvm_ref[:, pl.ds(h * HD, HD)] += _tn(eb, _c(doh * inv_w[:, :HD]))
            dkm_ref[:, pl.ds(h * HD, HD)] += _tn(ds, qh)
            dqm_ref[:, pl.ds(h * HD, HD)] = _nn(ds, kh)

        u = cc_ref[...] * ch_ref[...]
        uh = jnp.where(first, 0.0, cch_ref[...] * chh_ref[...])
        u1, u2 = _conv_taps(u, uh)
        conv = cw_ref[0:1, :] * u2 + cw_ref[1:2, :] * u1 + cw_ref[2:3, :] * u + _small(pk_ref, "conv_b")
        dy = dco_ref[...]
        dcb_ref[...] = dy * conv
        dcv = dy * cb_ref[...]
        dcv_ref[...] = dcv
        dcbias_ref[...] += jnp.sum(dcv, axis=0, keepdims=True)
        dcw_ref[0:1, :] += jnp.sum(dcv * u2, axis=0, keepdims=True)
        dcw_ref[1:2, :] += jnp.sum(dcv * u1, axis=0, keepdims=True)
        dcw_ref[2:3, :] += jnp.sum(dcv * u, axis=0, keepdims=True)

    tile = lambda w, col: pl.BlockSpec((tm, w), lambda i: (i, col))
    halo = lambda col: pl.BlockSpec((8, CONV_W), lambda i: (jnp.maximum(i * (tm // 8) - 1, 0), col))
    seq = pl.BlockSpec((NM, MEM_W), lambda i: ((i * tm) // S, 0))
    const = lambda shape: pl.BlockSpec(shape, lambda i: (0, 0))
    return _run("mem_conv_bwd", body, (T // tm,),
                [d_mem_out, mem_out, d_conv_out, proj, proj, proj, proj, proj, qkv, km, vm, conv_w8, pk],
                [tile(MEM_W, 0), tile(MEM_W, 0), tile(CONV_W, 0), tile(CONV_W, 3), tile(CONV_W, 4), tile(CONV_W, 5),
                 halo(3), halo(5), tile(MEM_W, 3), seq, seq, VM, VM],
                [SDS((T, MEM_W), f32), SDS(km.shape, f32), SDS(km.shape, f32),
                 SDS((T, CONV_W), f32), SDS((T, CONV_W), f32), SDS((8, CONV_W), f32), SDS((1, CONV_W), f32)],
                [tile(MEM_W, 0), seq, seq, tile(CONV_W, 0), tile(CONV_W, 0), const((8, CONV_W)), const((1, CONV_W))],
                exchange=exchange)


def in_proj_bwd(dqn, dkn, dv, dcb, dcv, dqmn, proj, conv_w8, xn, x2d, dx1, pk, winT, S, tm):
    T, D = x2d.shape
    P = winT.shape[0]
    last_blk = T // 8 - 1

    def body(dq_ref, dk_ref, dv_ref, dcb_ref, dcv_ref, dcvn_ref, dqm_ref, qa_ref, ka_ref, ch_ref, cc_ref, qma_ref,
             cw_ref, xn_ref, x_ref, dx1_ref, pk_ref, w_ref,
             dx_ref, dw_ref, dg_ref, dqg_ref, dkg_ref, dmqg_ref):
        i = pl.program_id(0)

        @pl.when(i == 0)
        def _():
            dw_ref[...] = jnp.zeros_like(dw_ref)
            dg_ref[...] = jnp.zeros_like(dg_ref)
            dqg_ref[...] = jnp.zeros_like(dqg_ref)
            dkg_ref[...] = jnp.zeros_like(dkg_ref)
            dmqg_ref[...] = jnp.zeros_like(dmqg_ref)

        dqa, gq = _heads_norm_bwd(dq_ref[...], qa_ref[...], _small(pk_ref, "q_norm"))
        dka, gk = _heads_norm_bwd(dk_ref[...], ka_ref[...], _small(pk_ref, "k_norm"))
        dqma, gmq = _heads_norm_bwd(dqm_ref[...], qma_ref[...], _small(pk_ref, "mem_q_norm"))
        dqg_ref[...] += gq
        dkg_ref[...] += gk
        dmqg_ref[...] += gmq

        last = ((i + 1) * tm) % S == 0
        dcv = dcv_ref[...]
        nxt = jnp.where(last, 0.0, dcvn_ref[...])
        row = lax.broadcasted_iota(jnp.int32, dcv.shape, 0)
        n1 = jnp.where(row == tm - 1, nxt[0:1, :], pltpu.roll(dcv, tm - 1, 0))
        n2 = jnp.where(row == tm - 2, nxt[0:1, :], jnp.where(row == tm - 1, nxt[1:2, :], pltpu.roll(dcv, tm - 2, 0)))
        du = cw_ref[2:3, :] * dcv + cw_ref[1:2, :] * n1 + cw_ref[0:1, :] * n2
        d_proj = jnp.concatenate([_c(dqa), _c(dka), _c(dv_ref[...]), _c(du * cc_ref[...]),
                                  _c(dcb_ref[...]), _c(du * ch_ref[...]), _c(dqma)], axis=1)
        dw_ref[...] += _tn(d_proj, xn_ref[...])
        xv = x_ref[...]
        dv_, dg = _norm_bwd(_nn(d_proj, w_ref[...]), xv, _rstd(xv), _small(pk_ref, "norm_mix"))
        dx_ref[...] = dx1_ref[...] + dv_
        dg_ref[...] += dg

    tile = lambda w, col=0: pl.BlockSpec((tm, w), lambda i: (i, col))
    nhalo = pl.BlockSpec((8, CONV_W), lambda i: (jnp.minimum((i + 1) * (tm // 8), last_blk), 0))
    const = lambda shape: pl.BlockSpec(shape, lambda i: (0, 0))
    return _run("in_proj_bwd", body, (T // tm,),
                [dqn, dkn, dv, dcb, dcv, dcv, dqmn, proj, proj, proj, proj, proj, conv_w8, xn, x2d, dx1, pk, winT],
                [tile(ATT_W), tile(KV_W), tile(KV_W), tile(CONV_W), tile(CONV_W), nhalo, tile(MEM_W),
                 tile(ATT_W, 0), tile(KV_W, 4), tile(CONV_W, 3), tile(CONV_W, 5), tile(MEM_W, 6), VM,
                 tile(D), tile(D), tile(D), VM, VM],
                [SDS((T, D), f32), SDS((P, D), f32), SDS((1, D), f32), SDS((1, HD), f32), SDS((1, HD), f32),
                 SDS((1, HD), f32)],
                [tile(D), pl.BlockSpec((P, D), lambda i: (0, 0)), const((1, D)), const((1, HD)), const((1, HD)),
                 const((1, HD))],
                vmem_mib=48)


def mem_kv_bwd(dkm, dvm, kv, memn, mem2d, pk, wmkv):
    def body(dkm_ref, dvm_ref, kv_ref, mn_ref, m_ref, pk_ref, w_ref, dw_ref, dg_ref, dkg_ref):
        dkk, dkg = _heads_norm_bwd(dkm_ref[...], kv_ref[:, :MEM_W], _small(pk_ref, "mem_k_norm"))
        dkg_ref[...] = dkg
        dkv = _c(jnp.concatenate([dkk, dvm_ref[...]], axis=1))
        dw_ref[...] = _tn(mn_ref[...], dkv)
        mv = m_ref[...]
        dg_ref[...] = jnp.sum(_nt(dkv, w_ref[...]) * mv * _rstd(mv), axis=0, keepdims=True)

    return _run("mem_kv_bwd", body, (), [dkm, dvm, kv, memn, mem2d, pk, wmkv], [VM] * 7,
                [SDS(wmkv.shape, f32), SDS((1, mem2d.shape[1]), f32), SDS((1, HD), f32)], [VM] * 3, vmem_mib=40)


def _halves_view(g):
    return g.reshape(4, 2, g.shape[0] // 8, g.shape[1])


def kernel(x, mem, norm_mix, w_in, q_norm, k_norm, attn_sinks, conv_w, conv_b, norm_mem, w_mem_kv, mem_q_norm, mem_k_norm, out_norm_attn, out_norm_conv, out_norm_mem, w_out, norm_ffn, w_gate, w_up, w_down, loss_target, m_norm_mix, m_w_in, m_q_norm, m_k_norm, m_attn_sinks, m_conv_w, m_conv_b, m_norm_mem, m_w_mem_kv, m_mem_q_norm, m_mem_k_norm, m_out_norm_attn, m_out_norm_conv, m_out_norm_mem, m_w_out, m_norm_ffn, m_w_gate, m_w_up, m_w_down, v_norm_mix, v_w_in, v_q_norm, v_k_norm, v_attn_sinks, v_conv_w, v_conv_b, v_norm_mem, v_w_mem_kv, v_mem_q_norm, v_mem_k_norm, v_out_norm_attn, v_out_norm_conv, v_out_norm_mem, v_w_out, v_norm_ffn, v_w_gate, v_w_up, v_w_down):
    BL, S, D = x.shape
    T = BL * S
    TM = 256
    _, _, ci = _place()
    cidx = ci.reshape(1).astype(jnp.int32)
    w_small = dict(norm_mix=norm_mix, norm_mem=norm_mem, norm_ffn=norm_ffn, out_norm_attn=out_norm_attn,
                   out_norm_conv=out_norm_conv, out_norm_mem=out_norm_mem, conv_w=conv_w, conv_b=conv_b, q_norm=q_norm,
                   k_norm=k_norm, mem_q_norm=mem_q_norm, mem_k_norm=mem_k_norm, attn_sinks=attn_sinks)
    m_small = dict(norm_mix=m_norm_mix, norm_mem=m_norm_mem, norm_ffn=m_norm_ffn, out_norm_attn=m_out_norm_attn,
                   out_norm_conv=m_out_norm_conv, out_norm_mem=m_out_norm_mem, conv_w=m_conv_w, conv_b=m_conv_b,
                   q_norm=m_q_norm, k_norm=m_k_norm, mem_q_norm=m_mem_q_norm, mem_k_norm=m_mem_k_norm,
                   attn_sinks=m_attn_sinks)
    v_small = dict(norm_mix=v_norm_mix, norm_mem=v_norm_mem, norm_ffn=v_norm_ffn, out_norm_attn=v_out_norm_attn,
                   out_norm_conv=v_out_norm_conv, out_norm_mem=v_out_norm_mem, conv_w=v_conv_w, conv_b=v_conv_b,
                   q_norm=v_q_norm, k_norm=v_k_norm, mem_q_norm=v_mem_q_norm, mem_k_norm=v_mem_k_norm,
                   attn_sinks=v_attn_sinks)
    pk = _pack_small(w_small)

    rowblocks = lambda a, b, c, d, e, f: [a[0].T, b[0].T, c[0].T, d[0], e[0], f[0]]
    w_rb = rowblocks(w_in, w_gate, w_up, w_down, w_out, w_mem_kv)
    m_rb = rowblocks(m_w_in, m_w_gate, m_w_up, m_w_down, m_w_out, m_w_mem_kv)
    v_rb = rowblocks(v_w_in, v_w_gate, v_w_up, v_w_down, v_w_out, v_w_mem_kv)
    (winT_s,) = prep_weights("prep_w_in", w_rb[:1])
    cw_pad = jnp.zeros((8, 128), f32).at[:3, :HD].set(conv_w[0])
    (wgT_s, wuT_s, wd_s, wout_s, wmkv_s), (winT, cw_all) = prep_weights(
        "gather_w_in", w_rb[1:], gather_exchange([winT_s, cw_pad], [True, False]))
    conv_w_full = jnp.transpose(cw_all.reshape(4, 8, 128)[:, :3, :HD], (1, 0, 2)).reshape(3, CONV_W)
    conv_w8 = jnp.zeros((8, CONV_W), f32).at[:3].set(conv_w_full)
    sink_rows = jnp.broadcast_to(attn_sinks.reshape(N_Q, 1), (N_Q, 128))

    x2d = x.reshape(T, D)
    mem2d = mem.reshape(-1, D)
    (xn, proj, qkv), (wgT,) = in_proj_fwd(x2d, pk, winT, TM, gather_exchange([wgT_s], [True]))
    (attn_out,), (wuT, wout, wmkv) = attn_fwd(qkv, sink_rows, BL, S,
                                              gather_exchange([wuT_s, wout_s, wmkv_s], [True, True, True], relay_early=3))
    memn, kv, km, vm = mem_kv_fwd(mem2d, pk, wmkv)
    (conv_out, mem_out, merged, x1, h), (wd,) = mixer_tail_fwd(
        x2d, attn_out, proj, qkv, km, vm, conv_w8, pk, wout, S, TM, gather_exchange([wd_s], [True]))

    dx1, dx2b, act, d_gate, d_up, loss8, d_norm_ffn = ffn_fwd_bwd(h, x1, loss_target.reshape(T, D), wgT, wuT, wd, pk, TM)
    F = wd.shape[0]
    g_wd = matmul_tn(act, dx2b, "dw_down", F // 2, min(T, 1024))
    g_wgT = matmul_tn(d_gate, h, "dw_gate", F // 2, min(T, 1024))
    g_wuT = matmul_tn(d_up, h, "dw_up", F // 2, min(T, 1024))

    d_attn, d_conv_out, d_mem_out, g_wout, d_gains = out_proj_bwd(dx1, merged, attn_out, conv_out, mem_out, pk, wout, TM)
    late = [_halves_view(g) for g in (g_wgT, g_wuT, g_wd, g_wout)]
    (dqmn, dkm, dvm, dcb, dcv, d_cw8, d_cbias), late_sib = mem_conv_bwd(
        d_mem_out, mem_out, d_conv_out, proj, qkv, km, vm, conv_w8, pk, S, TM, halves_exchange(late))
    late_part = add_halves(cidx, late, late_sib, "grad_add_halves_ffn")
    (dqn, dkn, dv, d_sink8), late_stage = attn_bwd(qkv, d_attn, attn_out, sink_rows, BL, S, scatter_exchange(late_part))
    g_x, g_winT, d_norm_mix, d_qg, d_kg, d_mqg = in_proj_bwd(
        dqn, dkn, dv, dcb, dcv, dqmn, proj, conv_w8, xn, x2d, dx1, pk, winT, S, TM)
    g_wmkv, d_norm_mem, d_mkg = mem_kv_bwd(dkm, dvm, kv, memn, mem2d, pk, wmkv)

    tot, tail_stage = tail_reduce(d_norm_mix, d_norm_mem, d_norm_ffn, d_gains, d_cw8, d_cbias, d_qg, d_kg, d_mqg, d_mkg,
                                  d_sink8, loss8, [_halves_view(g) for g in (g_winT, g_wmkv)])
    loss = tot[5, 384]
    late_res, _ = adamw_big("adamw_late", late_stage, w_rb[1:5], m_rb[1:5], v_rb[1:5], 8)
    tail_res, _ = adamw_big("adamw_tail", tail_stage, [w_rb[0], w_rb[5]], [m_rb[0], m_rb[5]], [v_rb[0], v_rb[5]], 4)
    res = {"w_in": [a.T[None] for a in tail_res[0]], "w_gate": [a.T[None] for a in late_res[0]],
           "w_up": [a.T[None] for a in late_res[1]], "w_down": [a[None] for a in late_res[2]],
           "w_out": [a[None] for a in late_res[3]], "w_mem_kv": [a[None] for a in tail_res[1]]}
    res.update(adamw_small(tot, pk, _pack_small(m_small), _pack_small(v_small), {k: w_small[k].shape for k in SMALL}))

    order = ["norm_mix", "w_in", "q_norm", "k_norm", "attn_sinks", "conv_w", "conv_b", "norm_mem", "w_mem_kv",
             "mem_q_norm", "mem_k_norm", "out_norm_attn", "out_norm_conv", "out_norm_mem", "w_out", "norm_ffn",
             "w_gate", "w_up", "w_down"]
    return (loss, g_x.reshape(BL, S, D), *[res[n][0] for n in order], *[res[n][1] for n in order],
            *[res[n][2] for n in order], *[res[n][3] for n in order])
```

```python
import collections
import functools

import jax
import jax.numpy as jnp
import numpy as np
from jax import lax
from jax.experimental import pallas as pl
from jax.experimental.pallas import tpu as pltpu

f32 = jnp.float32
MXU = jnp.bfloat16
WIRE = jnp.bfloat16
EPS = 1e-6
NEG = -1e30
HD = 64
BLK = 128
N_Q, N_KV, N_MEMH = 8, 2, 4
GQA = N_Q // N_KV
ATT_W, KV_W, CONV_W, MEM_W = 512, 128, 256, 256
VMEM_MIB = 1024 * 1024
ADAM_LR, ADAM_B1, ADAM_B2, ADAM_EPS, ADAM_WD, ADAM_STEP = 0.001, 0.9, 0.999, 1e-08, 0.01, 10

MESH = pl.DeviceIdType.MESH
VM = pl.BlockSpec(memory_space=pltpu.VMEM)
ANY = pl.BlockSpec(memory_space=pl.ANY)
SDS = jax.ShapeDtypeStruct
DMA = pltpu.SemaphoreType.DMA


def _c(v):
    return v.astype(MXU)


def _nn(a, b):
    return lax.dot_general(a, b, (((1,), (0,)), ((), ())), preferred_element_type=f32)


def _nt(a, b):
    return lax.dot_general(a, b, (((1,), (1,)), ((), ())), preferred_element_type=f32)


def _tn(a, b):
    return lax.dot_general(a, b, (((0,), (0,)), ((), ())), preferred_element_type=f32)


def _rstd(v):
    return lax.rsqrt(jnp.mean(v * v, axis=-1, keepdims=True) + EPS)


def _norm_bwd(dy, v, r, g):
    dyg = dy * g
    dv = r * dyg - v * (r * r * r) * jnp.mean(dyg * v, axis=-1, keepdims=True)
    return dv, jnp.sum(dy * v * r, axis=0, keepdims=True)


def _split3(v):
    hi = _c(v)
    r1 = v - hi.astype(f32)
    mid = _c(r1)
    return hi, mid, _c(r1 - mid.astype(f32))


def _rowsum_mxu(v, width):
    ones = jnp.ones((v.shape[1], width), MXU)
    return sum(_nn(a, ones) for a in _split3(v))


def _seg_sums(v):
    r = lax.broadcasted_iota(jnp.int32, (2 * HD, 2 * HD), 0) // HD
    c = lax.broadcasted_iota(jnp.int32, (2 * HD, 2 * HD), 1) // HD
    bd = (r == c).astype(MXU)
    outs = []
    for b in range(v.shape[1] // (2 * HD)):
        outs.append(sum(_nn(a, bd) for a in _split3(v[:, b * 2 * HD:(b + 1) * 2 * HD])))
    return outs[0] if len(outs) == 1 else jnp.concatenate(outs, axis=1)


def _lanes(g, width):
    return jnp.concatenate([g] * (width // HD), axis=1)


def _heads_rstd(v):
    return lax.rsqrt(_seg_sums(v * v) * (1.0 / HD) + EPS)


def _heads_norm_bwd(dy, v, g):
    r = _heads_rstd(v)
    gl = _lanes(g, v.shape[1])
    dyg = dy * gl
    dv = r * dyg - v * (r * r * r) * (_seg_sums(dyg * v) * (1.0 / HD))
    dgl = jnp.sum(dy * v * r, axis=0, keepdims=True)
    return dv, sum(dgl[:, s * HD:(s + 1) * HD] for s in range(v.shape[1] // HD))


def _exp_scores(s, extra=None):
    m = jnp.max(s, axis=-1, keepdims=True)
    if extra is None:
        return jnp.exp(s - m), None
    m = jnp.maximum(m, extra)
    return jnp.exp(s - m), jnp.exp(extra - m)


def _place():
    return lax.axis_index("x"), lax.axis_index("y"), lax.axis_index("c")


SMALL_AT = {"norm_mix": (0, 0, 1024), "norm_mem": (1, 0, 1024), "norm_ffn": (2, 0, 1024),
            "out_norm_attn": (3, 0, ATT_W), "out_norm_conv": (3, ATT_W, CONV_W), "out_norm_mem": (3, ATT_W + CONV_W, MEM_W),
            "conv_b": (4, 3 * CONV_W, CONV_W), "q_norm": (5, 0, HD), "k_norm": (5, HD, HD), "mem_q_norm": (5, 2 * HD, HD),
            "mem_k_norm": (5, 3 * HD, HD), "attn_sinks": (5, 256, N_Q)}
SMALL = ("norm_mix", "norm_mem", "norm_ffn", "out_norm_attn", "out_norm_conv", "out_norm_mem", "conv_w", "conv_b",
         "q_norm", "k_norm", "mem_q_norm", "mem_k_norm", "attn_sinks")


def _small(pk_ref, name):
    r, c0, w = SMALL_AT[name]
    return pk_ref[r:r + 1, c0:c0 + w]


def _pack_small(d):
    z = lambda n: jnp.zeros((1, n), f32)
    row3 = jnp.concatenate([d["out_norm_attn"], d["out_norm_conv"], d["out_norm_mem"]], axis=1)
    row4 = jnp.concatenate([d["conv_w"].reshape(1, 3 * HD), z(3 * CONV_W - 3 * HD), d["conv_b"]], axis=1)
    row5 = jnp.concatenate([d["q_norm"], d["k_norm"], d["mem_q_norm"], d["mem_k_norm"], d["attn_sinks"],
                            z(1024 - 4 * HD - N_Q)], axis=1)
    return jnp.concatenate([d["norm_mix"], d["norm_mem"], d["norm_ffn"], row3, row4, row5, z(1024), z(1024)], axis=0)


def _other_chips(x, y):
    return [(1 - x, y), (x, 1 - y), (1 - x, 1 - y)]


Exchange = collections.namedtuple("Exchange", "ins outs sems start finish relay relay_steps_before_end", defaults=(None, 0))


def _run(name, body, grid, ins, in_specs, out_shape, out_specs, scratch=(), vmem_mib=32, exchange=None):
    ins, in_specs, out_shape, out_specs, scratch = list(ins), list(in_specs), list(out_shape), list(out_specs), list(scratch)
    ni, no, ns = len(ins), len(out_shape), len(scratch)
    ex = exchange
    if ex is not None:
        nxi, nxo = len(ex.ins), len(ex.outs)

    def call_body(*refs):
        if ex is None:
            body(*refs)
            return
        a, xa = refs[:ni], refs[ni:ni + nxi]
        o, xo = refs[ni + nxi:ni + nxi + no], refs[ni + nxi + no:ni + nxi + no + nxo]
        s, xs = refs[ni + nxi + no + nxo:ni + nxi + no + nxo + ns], refs[ni + nxi + no + nxo + ns:]
        if grid:
            first = functools.reduce(jnp.logical_and, [pl.program_id(d) == 0 for d in range(len(grid))])
            last = functools.reduce(jnp.logical_and, [pl.program_id(d) == grid[d] - 1 for d in range(len(grid))])
            pl.when(first)(lambda: ex.start(xa, xo, xs))
            body(*a, *o, *s)
            if ex.relay is not None:
                early = functools.reduce(jnp.logical_and, [pl.program_id(d) == grid[d] - 1 for d in range(len(grid) - 1)],
                                         pl.program_id(len(grid) - 1) == grid[-1] - 1 - ex.relay_steps_before_end)
                pl.when(early)(lambda: ex.relay(xa, xo, xs))
            pl.when(last)(lambda: ex.finish(xa, xo, xs))
        else:
            ex.start(xa, xo, xs)
            if body is not None:
                body(*a, *o, *s)
            if ex.relay is not None:
                ex.relay(xa, xo, xs)
            ex.finish(xa, xo, xs)

    if ex is not None:
        ins, in_specs = ins + list(ex.ins), in_specs + [ANY] * nxi
        out_shape, out_specs = out_shape + list(ex.outs), out_specs + [ANY] * nxo
        scratch = scratch + list(ex.sems)
    kw = dict(grid=grid) if grid else {}
    res = pl.pallas_call(
        call_body, name=name, out_shape=out_shape, in_specs=in_specs, out_specs=out_specs, scratch_shapes=scratch,
        compiler_params=pltpu.CompilerParams(dimension_semantics=("arbitrary",) * len(grid) if grid else None,
                                             vmem_limit_bytes=vmem_mib * VMEM_MIB), **kw)(*ins)
    res = list(res)
    return (res[:no], res[no:]) if ex is not None else res


def _remote(src, dst, ssem, rsem, dev):
    return pltpu.make_async_remote_copy(src_ref=src, dst_ref=dst, send_sem=ssem, recv_sem=rsem,
                                        device_id=dev, device_id_type=MESH)


def gather_exchange(shards, split, relay_early=0):
    n = len(shards)

    def rows(ref, e, kk, half=None):
        R = shards[e].shape[0]
        if half is None:
            return ref.at[pl.ds(pl.multiple_of(kk * R, 8), R)]
        return ref.at[pl.ds(pl.multiple_of(kk * R + half * (R // 2), 8), R // 2)]

    def ici(src, dst, sm, e, j, chip_j, x, y, c):
        k = 2 * x + y
        if split[e]:
            s = src[e].at[pl.ds(pl.multiple_of(c * (shards[e].shape[0] // 2), 8), shards[e].shape[0] // 2)]
            return _remote(s, rows(dst[e], e, k, c), sm[0].at[6 * e + j], sm[1].at[6 * e + j], (*chip_j, c))
        return _remote(src[e], rows(dst[e], e, k), sm[0].at[6 * e + j], sm[1].at[6 * e + j], (*chip_j, c))

    def landed(dst, e, chip_j, c):
        kj = 2 * chip_j[0] + chip_j[1]
        return rows(dst[e], e, kj, c) if split[e] else rows(dst[e], e, kj)

    def forward(dst, sm, e, j, chip_j, x, y, c, sender_c):
        kj = 2 * chip_j[0] + chip_j[1]
        r = rows(dst[e], e, kj, sender_c)
        return _remote(r, r, sm[0].at[6 * e + 3 + j], sm[1].at[6 * e + 3 + j], (x, y, 1 - c))

    def local(src, dst, sm, e, x, y):
        return pltpu.make_async_copy(src[e], rows(dst[e], e, 2 * x + y), sm[2].at[e])

    def start(src, dst, sm):
        x, y, c = _place()
        for e in range(n):
            local(src, dst, sm, e, x, y).start()
            for j, chip_j in enumerate(_other_chips(x, y)):
                ici(src, dst, sm, e, j, chip_j, x, y, c).start()

    def relay(src, dst, sm):
        x, y, c = _place()
        for e in range(n):
            for j, chip_j in enumerate(_other_chips(x, y)):
                r = landed(dst, e, chip_j, c)
                _remote(r, r, sm[0].at[6 * e + j], sm[1].at[6 * e + j], (*chip_j, c)).wait_recv()
                if split[e]:
                    forward(dst, sm, e, j, chip_j, x, y, c, c).start()

    def finish(src, dst, sm):
        x, y, c = _place()
        chips = _other_chips(x, y)
        for e in range(n):
            for j, chip_j in enumerate(chips):
                if split[e]:
                    forward(dst, sm, e, j, chip_j, x, y, c, 1 - c).wait_recv()
        for e in range(n):
            for j, chip_j in enumerate(chips):
                ici(src, dst, sm, e, j, chip_j, x, y, c).wait_send()
                if split[e]:
                    forward(dst, sm, e, j, chip_j, x, y, c, c).wait_send()
            local(src, dst, sm, e, x, y).wait()

    outs = [SDS((4 * s.shape[0], s.shape[1]), s.dtype) for s in shards]
    return Exchange(list(shards), outs, [DMA((6 * n,)), DMA((6 * n,)), DMA((n,))], start, finish, relay, relay_early)


def halves_exchange(grads):
    n = len(grads)

    def copy(g, st, sm, e, x, y, c):
        return _remote(g[e].at[:, 1 - c], st[e], sm[0].at[e], sm[1].at[e], (x, y, 1 - c))

    def start(g, st, sm):
        x, y, c = _place()
        for e in range(n):
            copy(g, st, sm, e, x, y, c).start()

    def finish(g, st, sm):
        x, y, c = _place()
        for e in range(n):
            copy(g, st, sm, e, x, y, c).wait()

    outs = [SDS((4,) + a.shape[2:], a.dtype) for a in grads]
    return Exchange(list(grads), outs, [DMA((n,)), DMA((n,))], start, finish)


def scatter_exchange(parts):
    n = len(parts)

    def ici(p, st, sm, e, j, chip_j, x, y, c):
        k, kj = 2 * x + y, 2 * chip_j[0] + chip_j[1]
        return _remote(p[e].at[kj], st[e].at[c, k], sm[0].at[8 * e + j], sm[1].at[8 * e + j], (*chip_j, c))

    def own(p, st, sm, e, x, y, c):
        k = 2 * x + y
        return _remote(p[e].at[k], st[e].at[c, k], sm[0].at[8 * e + 3], sm[1].at[8 * e + 3], (x, y, 1 - c))

    def forward(st, sm, e, j, chip_j, x, y, c, sender_c):
        kj = 2 * chip_j[0] + chip_j[1]
        r = st[e].at[sender_c, kj]
        return _remote(r, r, sm[0].at[8 * e + 4 + j], sm[1].at[8 * e + 4 + j], (x, y, 1 - c))

    def local(p, st, sm, e, x, y, c):
        k = 2 * x + y
        return pltpu.make_async_copy(p[e].at[k], st[e].at[c, k], sm[2].at[e])

    def start(p, st, sm):
        x, y, c = _place()
        for e in range(n):
            local(p, st, sm, e, x, y, c).start()
            own(p, st, sm, e, x, y, c).start()
            for j, chip_j in enumerate(_other_chips(x, y)):
                ici(p, st, sm, e, j, chip_j, x, y, c).start()

    def relay(p, st, sm):
        x, y, c = _place()
        for e in range(n):
            for j, chip_j in enumerate(_other_chips(x, y)):
                kj = 2 * chip_j[0] + chip_j[1]
                r = st[e].at[c, kj]
                _remote(r, r, sm[0].at[8 * e + j], sm[1].at[8 * e + j], (*chip_j, c)).wait_recv()
                forward(st, sm, e, j, chip_j, x, y, c, c).start()

    def finish(p, st, sm):
        x, y, c = _place()
        k = 2 * x + y
        chips = _other_chips(x, y)
        for e in range(n):
            r = st[e].at[1 - c, k]
            _remote(r, r, sm[0].at[8 * e + 3], sm[1].at[8 * e + 3], (x, y, 1 - c)).wait_recv()
            for j, chip_j in enumerate(chips):
                forward(st, sm, e, j, chip_j, x, y, c, 1 - c).wait_recv()
        for e in range(n):
            own(p, st, sm, e, x, y, c).wait_send()
            for j, chip_j in enumerate(chips):
                ici(p, st, sm, e, j, chip_j, x, y, c).wait_send()
                forward(st, sm, e, j, chip_j, x, y, c, c).wait_send()
            local(p, st, sm, e, x, y, c).wait()

    outs = [SDS((2,) + a.shape, a.dtype) for a in parts]
    return Exchange(list(parts), outs, [DMA((8 * n,)), DMA((8 * n,)), DMA((n,))], start, finish, relay)


def tail_reduce(d_norm_mix, d_norm_mem, d_norm_ffn, d_gains, d_cw8, d_cbias, d_qg, d_kg, d_mqg, d_mkg, d_sink8, loss8, tail):
    n = len(tail)
    halves = halves_exchange(tail)
    scatter = scatter_exchange([SDS((4,) + a.shape[2:], WIRE) for a in tail])

    def body(nm_ref, nmem_ref, nf_ref, gn_ref, cw_ref, cb_ref, qg_ref, kg_ref, mqg_ref, mkg_ref, sk_ref, ls_ref, *rest):
        g, o_ref, st = rest[:n], rest[n], rest[n + 1:2 * n + 1]
        buf, ssem, rsem = rest[2 * n + 1:2 * n + 4]
        own, sib, part = (rest[2 * n + 4 + i * n:2 * n + 4 + (i + 1) * n] for i in range(3))
        lsem = rest[5 * n + 4]
        hsem, xsem = rest[5 * n + 5:5 * n + 7], rest[5 * n + 7:]
        x, y, c = _place()
        loads = [pltpu.make_async_copy(g[e].at[:, c], own[e], lsem.at[e]) for e in range(n)]
        for ld in loads:
            ld.start()
        halves.start(g, sib, hsem)
        me = 4 * x + 2 * y + c
        mine = buf.at[me]
        mine[...] = jnp.zeros((8, 1024), f32)
        mine[0:1, :] = nm_ref[...]
        mine[1:2, :] = nmem_ref[...]
        mine[2:3, :] = nf_ref[...]
        mine[3:4, :] = gn_ref[...]
        for j in range(3):
            mine[4:5, pl.ds(j * CONV_W, CONV_W)] = cw_ref[j:j + 1, :]
        mine[4:5, pl.ds(3 * CONV_W, CONV_W)] = cb_ref[...]
        for j, r in enumerate((qg_ref, kg_ref, mqg_ref, mkg_ref)):
            mine[5:6, pl.ds(j * HD, HD)] = r[...]
        mine[5:6, pl.ds(256, 128)] = sk_ref[0:1, :]
        mine[5:6, pl.ds(384, 128)] = ls_ref[0:1, :]

        def peer_of(m):
            return (1 - x if m & 4 else x, 1 - y if m & 2 else y, 1 - c if m & 1 else c)

        for m in range(1, 8):
            _remote(mine, mine, ssem.at[m - 1], rsem.at[m - 1], peer_of(m)).start()
        for ld in loads:
            ld.wait()
        halves.finish(g, sib, hsem)
        for e in range(n):
            part[e][...] = (own[e][...] + sib[e][...]).astype(WIRE)
        scatter.start(part, st, xsem)
        scatter.relay(part, st, xsem)
        scatter.finish(part, st, xsem)
        for m in range(1, 8):
            p = peer_of(m)
            got = buf.at[4 * p[0] + 2 * p[1] + p[2]]
            _remote(got, got, ssem.at[m - 1], rsem.at[m - 1], p).wait_recv()
        for m in range(1, 8):
            _remote(mine, mine, ssem.at[m - 1], rsem.at[m - 1], peer_of(m)).wait_send()
        acc = buf[0]
        for d in range(1, 8):
            acc = acc + buf[d]
        o_ref[...] = acc

    ins = [d_norm_mix, d_norm_mem, d_norm_ffn, d_gains, d_cw8, d_cbias, d_qg, d_kg, d_mqg, d_mkg, d_sink8, loss8]
    half_shape = [(4,) + a.shape[2:] for a in tail]
    scratch = ([pltpu.VMEM((8, 8, 1024), f32), DMA((7,)), DMA((7,))]
               + [pltpu.VMEM(s, f32) for s in half_shape] * 2 + [pltpu.VMEM(s, WIRE) for s in half_shape]
               + [DMA((n,))] + list(halves.sems) + list(scatter.sems))
    res = _run("tail_reduce", body, (), ins + list(tail), [VM] * len(ins) + [ANY] * n,
               [SDS((8, 1024), f32)] + list(scatter.outs), [VM] + [ANY] * n, scratch=scratch, vmem_mib=40)
    return res[0], res[1:]


def add_halves(cidx, grads, stages, name, nch=2):
    n = len(grads)

    def body(c_ref, *refs):
        g, st, o = refs[:n], refs[n:2 * n], refs[2 * n:]
        for e in range(n):
            o[e][...] = (g[e][...] + st[e][...]).astype(WIRE)

    in_specs, out_specs, out_shape = [], [], []
    for a in grads:
        hr, C = a.shape[2], a.shape[3]
        in_specs.append(pl.BlockSpec((None, None, hr // nch, C), lambda s, q, c_ref: (s, c_ref[0], q, 0)))
    for a in stages:
        hr, C = a.shape[1], a.shape[2]
        in_specs.append(pl.BlockSpec((None, hr // nch, C), lambda s, q, c_ref: (s, q, 0)))
        out_specs.append(pl.BlockSpec((None, hr // nch, C), lambda s, q, c_ref: (s, q, 0)))
        out_shape.append(SDS(a.shape, WIRE))
    return pl.pallas_call(
        body, name=name, out_shape=out_shape,
        grid_spec=pltpu.PrefetchScalarGridSpec(num_scalar_prefetch=1, grid=(4, nch), in_specs=in_specs, out_specs=out_specs),
        compiler_params=pltpu.CompilerParams(dimension_semantics=("arbitrary", "arbitrary")),
    )(cidx, *grads, *stages)


def _adamw_math(w, g, m, v):
    m = ADAM_B1 * m + (1.0 - ADAM_B1) * g
    v = ADAM_B2 * v + (1.0 - ADAM_B2) * (g * g)
    m_hat = m / (1.0 - ADAM_B1 ** ADAM_STEP)
    v_hat = v / (1.0 - ADAM_B2 ** ADAM_STEP)
    delta = -ADAM_LR * (m_hat / (jnp.sqrt(v_hat) + ADAM_EPS) + ADAM_WD * w)
    return delta, m, v


def _sum_chips(st):
    return ((st[0].astype(f32) + st[1].astype(f32)) + st[2].astype(f32)) + st[3].astype(f32)


def adamw_big(name, stages, ws, ms, vs, nstep, exchange=None):
    n = len(stages)

    def body(*refs):
        st, w, m, v = refs[:n], refs[n:2 * n], refs[2 * n:3 * n], refs[3 * n:4 * n]
        outs = refs[4 * n:]
        for e in range(n):
            g = jnp.concatenate([_sum_chips(st[e].at[0]), _sum_chips(st[e].at[1])], axis=0)
            d, mm, vv = _adamw_math(w[e][...], g, m[e][...], v[e][...])
            outs[4 * e][...] = g
            outs[4 * e + 1][...] = d
            outs[4 * e + 2][...] = mm
            outs[4 * e + 3][...] = vv

    st_specs, w_specs = [], []
    for e in range(n):
        _, _, hr, C = stages[e].shape
        st_specs.append(pl.BlockSpec((2, 4, hr, C // nstep), lambda i: (0, 0, 0, i)))
        w_specs.append(pl.BlockSpec((2 * hr, C // nstep), lambda i: (0, i)))
    out_specs = [s for s in w_specs for _ in range(4)]
    out_shape = [SDS(w.shape, f32) for w in ws for _ in range(4)]
    res = _run(name, body, (nstep,), list(stages) + list(ws) + list(ms) + list(vs), st_specs + w_specs * 3,
               out_shape, out_specs, vmem_mib=48, exchange=exchange)
    res, sent = res if exchange is not None else (res, None)
    return [res[4 * e:4 * e + 4] for e in range(n)], sent


def adamw_small(tot, pk_w, pk_m, pk_v, shapes):
    def body(tot_ref, w_ref, m_ref, v_ref, *outs):
        x, y, _ = _place()
        chip = 2 * x + y
        taps = []
        for j in range(3):
            mine = tot_ref[4:5, j * CONV_W:j * CONV_W + HD]
            for s in range(1, 4):
                mine = jnp.where(chip == s, tot_ref[4:5, j * CONV_W + s * HD:j * CONV_W + (s + 1) * HD], mine)
            taps.append(mine)
        row4 = jnp.concatenate(taps + [jnp.zeros((1, 3 * CONV_W - 3 * HD), f32), tot_ref[4:5, 3 * CONV_W:]], axis=1)
        tot_v = tot_ref[...]
        row = lax.broadcasted_iota(jnp.int32, tot_v.shape, 0)
        g = jnp.where(row == 4, jnp.broadcast_to(row4, tot_v.shape), tot_v)
        d, mm, vv = _adamw_math(w_ref[...], g, m_ref[...], v_ref[...])
        for i, name in enumerate(SMALL):
            for k, val in enumerate((g, d, mm, vv)):
                if name == "conv_w":
                    outs[4 * i + k][...] = jnp.concatenate([val[4:5, j * HD:(j + 1) * HD] for j in range(3)], axis=0)[None]
                else:
                    r, c0, w = SMALL_AT[name]
                    outs[4 * i + k][...] = val[r:r + 1, c0:c0 + w]

    out_shape = [SDS(shapes[k], f32) for k in SMALL for _ in range(4)]
    res = _run("adamw_small", body, (), [tot, pk_w, pk_m, pk_v], [VM] * 4, out_shape, [VM] * len(out_shape))
    return {k: res[4 * i:4 * i + 4] for i, k in enumerate(SMALL)}


def prep_weights(name, shards, exchange=None):
    n = len(shards)

    def body(*refs):
        for e in range(n):
            refs[n + e][...] = _c(refs[e][...])

    return _run(name, body, (), shards, [VM] * n, [SDS(a.shape, MXU) for a in shards], [VM] * n, vmem_mib=48, exchange=exchange)


def mem_kv_fwd(mem2d, pk, wmkv):
    M, D = mem2d.shape

    def body(m_ref, pk_ref, w_ref, mn_ref, kv_ref, km_ref, vm_ref):
        m = m_ref[...]
        mn = _c(m * _rstd(m) * _small(pk_ref, "norm_mem"))
        mn_ref[...] = mn
        kv = _nn(mn, w_ref[...])
        kv_ref[...] = kv
        kk = kv[:, :MEM_W]
        km_ref[...] = _c(kk * _heads_rstd(kk) * _lanes(_small(pk_ref, "mem_k_norm"), MEM_W))
        vm_ref[...] = _c(kv[:, MEM_W:])

    return _run("mem_kv_fwd", body, (), [mem2d, pk, wmkv], [VM] * 3,
                [SDS((M, D), MXU), SDS((M, 2 * MEM_W), f32), SDS((M, MEM_W), MXU), SDS((M, MEM_W), MXU)], [VM] * 4)


QKV_W = ATT_W + 2 * KV_W + MEM_W


def in_proj_fwd(x2d, pk, winT, tm, exchange):
    T, D = x2d.shape
    P = winT.shape[0]

    def body(x_ref, pk_ref, w_ref, xn_ref, proj_ref, qkv_ref):
        xv = x_ref[...]
        xn = _c(xv * _rstd(xv) * _small(pk_ref, "norm_mix"))
        xn_ref[...] = xn
        proj = _nt(xn, w_ref[...])
        proj_ref[...] = proj
        q, k = proj[:, :ATT_W], proj[:, ATT_W:ATT_W + KV_W]
        qm = proj[:, P - MEM_W:]
        qkv_ref[...] = jnp.concatenate(
            [_c(q * _heads_rstd(q) * _lanes(_small(pk_ref, "q_norm"), ATT_W)),
             _c(k * _heads_rstd(k) * _lanes(_small(pk_ref, "k_norm"), KV_W)),
             _c(proj[:, ATT_W + KV_W:ATT_W + 2 * KV_W]),
             _c(qm * _heads_rstd(qm) * _lanes(_small(pk_ref, "mem_q_norm"), MEM_W))], axis=1)

    return _run("in_proj_fwd", body, (T // tm,), [x2d, pk, winT],
                [pl.BlockSpec((tm, D), lambda i: (i, 0)), VM, VM],
                [SDS((T, D), MXU), SDS((T, P), f32), SDS((T, QKV_W), MXU)],
                [pl.BlockSpec((tm, D), lambda i: (i, 0)), pl.BlockSpec((tm, P), lambda i: (i, 0)),
                 pl.BlockSpec((tm, QKV_W), lambda i: (i, 0))],
                vmem_mib=40, exchange=exchange)


def _swa_bias_table():
    r = np.arange(GQA * BLK)[:, None]
    k = np.arange(2 * BLK)[None, :]
    dist = (r % BLK) + BLK - k
    band = (dist >= 0) & (dist < BLK)
    tab = np.empty((2, N_KV, GQA * BLK, 2 * BLK), np.float32)
    for later in range(2):
        valid = band & ((k >= BLK) | (later == 1))
        for g in range(N_KV):
            slope = 2.0 ** -(g * GQA + r // BLK + 1.0)
            tab[later, g] = np.where(valid, -slope * dist, NEG)
    return jnp.asarray(tab)


def _sink_column(g, sk_ref):
    hrow = lax.broadcasted_iota(jnp.int32, (GQA * BLK, 1), 0) // BLK
    sink = jnp.zeros((GQA * BLK, 1), f32)
    for hh in range(GQA):
        sink = jnp.where(hrow == hh, sk_ref[g * GQA + hh:g * GQA + hh + 1, 0:1], sink)
    return sink


def _stack_heads(v, g):
    return jnp.concatenate([v[:, (g * GQA + hh) * HD:(g * GQA + hh + 1) * HD] for hh in range(GQA)], axis=0)


def attn_fwd(qkv, sink_rows, BL, S, exchange):
    NB = S // BLK
    T = BL * S

    def body(q_ref, kc_ref, kp_ref, vc_ref, vp_ref, sk_ref, tab_ref, o_ref):
        q = q_ref[...]
        k2 = jnp.concatenate([kp_ref[...], kc_ref[...]], axis=0)
        v2 = jnp.concatenate([vp_ref[...], vc_ref[...]], axis=0)
        ones = jnp.ones((2 * BLK, HD), MXU)
        for g in range(N_KV):
            kn, vh = k2[:, g * HD:(g + 1) * HD], v2[:, g * HD:(g + 1) * HD]
            s = _nt(_stack_heads(q, g), kn) * (HD ** -0.5) + tab_ref[g]
            e, es = _exp_scores(s, _sink_column(g, sk_ref))
            eb = _c(e)
            o = _nn(eb, vh) * (1.0 / (_nn(eb, ones) + es))
            for hh in range(GQA):
                o_ref[:, pl.ds((g * GQA + hh) * HD, HD)] = o[hh * BLK:(hh + 1) * BLK]

    cur = lambda col: (lambda b, j: (b * NB + j, col))
    prev = lambda col: (lambda b, j: (b * NB + jnp.maximum(j - 1, 0), col))
    return _run("attn_fwd", body, (BL, NB), [qkv, qkv, qkv, qkv, qkv, sink_rows, _swa_bias_table()],
                [pl.BlockSpec((BLK, ATT_W), cur(0)),
                 pl.BlockSpec((BLK, KV_W), cur(4)), pl.BlockSpec((BLK, KV_W), prev(4)),
                 pl.BlockSpec((BLK, KV_W), cur(5)), pl.BlockSpec((BLK, KV_W), prev(5)),
                 pl.BlockSpec((8, 128), lambda b, j: (0, 0)),
                 pl.BlockSpec((None, N_KV, GQA * BLK, 2 * BLK), lambda b, j: (jnp.minimum(j, 1), 0, 0, 0))],
                [SDS((T, ATT_W), f32)], [pl.BlockSpec((BLK, ATT_W), cur(0))], exchange=exchange)


def _conv_taps(u, uh):
    row = lax.broadcasted_iota(jnp.int32, u.shape, 0)
    u1 = jnp.where(row == 0, uh[7:8, :], pltpu.roll(u, 1, 0))
    u2 = jnp.where(row == 0, uh[6:7, :], jnp.where(row == 1, uh[7:8, :], pltpu.roll(u, 2, 0)))
    return u1, u2


def _mem_head(qm, km, vm, h):
    qh, kh, vh = (a[:, h * HD:(h + 1) * HD] for a in (qm, km, vm))
    e, _ = _exp_scores(_nt(qh, kh) * (HD ** -0.5))
    return qh, kh, vh, e


def mixer_tail_fwd(x2d, attn_out, proj, qkv, km, vm, conv_w8, pk, wout, S, tm, exchange):
    T, D = x2d.shape
    NM = km.shape[0] // (T // S)

    def body(x_ref, ao_ref, ch_ref, cb_ref, cc_ref, chh_ref, cch_ref, qm_ref, km_ref, vm_ref, cw_ref, pk_ref,
             wout_ref, co_ref, mo_ref, mg_ref, x1_ref, h_ref):
        first = (pl.program_id(0) * tm) % S == 0
        u = cc_ref[...] * ch_ref[...]
        uh = jnp.where(first, 0.0, cch_ref[...] * chh_ref[...])
        u1, u2 = _conv_taps(u, uh)
        conv = cw_ref[0:1, :] * u2 + cw_ref[1:2, :] * u1 + cw_ref[2:3, :] * u + _small(pk_ref, "conv_b")
        conv_out = cb_ref[...] * conv
        co_ref[...] = conv_out
        qm, kmv, vmv = qm_ref[...], km_ref[...], vm_ref[...]
        ones = jnp.ones((NM, HD), MXU)
        for h in range(N_MEMH):
            _, _, vh, e = _mem_head(qm, kmv, vmv, h)
            eb = _c(e)
            mo_ref[:, pl.ds(h * HD, HD)] = _nn(eb, vh) * (1.0 / _nn(eb, ones))
        mem_out = mo_ref[...]
        ao = ao_ref[...]
        merged = _c(jnp.concatenate([ao * _rstd(ao) * _small(pk_ref, "out_norm_attn"),
                                     conv_out * _rstd(conv_out) * _small(pk_ref, "out_norm_conv"),
                                     mem_out * _rstd(mem_out) * _small(pk_ref, "out_norm_mem")], axis=1))
        mg_ref[...] = merged
        x1 = x_ref[...] + _nn(merged, wout_ref[...])
        x1_ref[...] = x1
        h_ref[...] = _c(x1 * _rstd(x1) * _small(pk_ref, "norm_ffn"))

    tile = lambda w, col: pl.BlockSpec((tm, w), lambda i: (i, col))
    halo = lambda col: pl.BlockSpec((8, CONV_W), lambda i: (jnp.maximum(i * (tm // 8) - 1, 0), col))
    seq = pl.BlockSpec((NM, MEM_W), lambda i: ((i * tm) // S, 0))
    small = lambda a: pl.BlockSpec(a.shape, lambda i: (0, 0))
    return _run("mixer_tail_fwd", body, (T // tm,),
                [x2d, attn_out, proj, proj, proj, proj, proj, qkv, km, vm, conv_w8, pk, wout],
                [tile(D, 0), tile(ATT_W, 0), tile(CONV_W, 3), tile(CONV_W, 4), tile(CONV_W, 5), halo(3), halo(5),
                 tile(MEM_W, 3), seq, seq, VM, VM, VM],
                [SDS((T, CONV_W), f32), SDS((T, MEM_W), f32), SDS((T, D), MXU), SDS((T, D), f32), SDS((T, D), MXU)],
                [tile(CONV_W, 0), tile(MEM_W, 0), tile(D, 0), tile(D, 0), tile(D, 0)], vmem_mib=40, exchange=exchange)


def ffn_fwd_bwd(h, x1, tgt, wgT, wuT, wd, pk, tm):
    T, D = x1.shape
    F = wd.shape[0]

    def body(h_ref, x1_ref, t_ref, wg_ref, wu_ref, wd_ref, pk_ref,
             dx1_ref, dx2_ref, act_ref, dg_ref, du_ref, loss_ref, dgf_ref):
        @pl.when(pl.program_id(0) == 0)
        def _():
            loss_ref[...] = jnp.zeros_like(loss_ref)
            dgf_ref[...] = jnp.zeros_like(dgf_ref)

        hv = h_ref[...]
        gate = _nt(hv, wg_ref[...])
        up = _nt(hv, wu_ref[...])
        sg = jax.nn.sigmoid(gate)
        sl = gate * sg
        act = _c(sl * up)
        act_ref[...] = act
        x1v = x1_ref[...]
        diff = (x1v + _nn(act, wd_ref[...])) - t_ref[...]
        loss_ref[...] += 0.5 * jnp.sum(jnp.sum(diff * diff, axis=-1, keepdims=True) / D, axis=0, keepdims=True)
        dx2 = diff / D
        dx2b = _c(dx2)
        dx2_ref[...] = dx2b
        d_act = _nt(dx2b, wd_ref[...])
        d_up = _c(d_act * sl)
        d_gate = _c(d_act * up * (sg * (1.0 + gate * (1.0 - sg))))
        du_ref[...] = d_up
        dg_ref[...] = d_gate
        dh = _nn(d_gate, wg_ref[...]) + _nn(d_up, wu_ref[...])
        dv, dgf = _norm_bwd(dh, x1v, _rstd(x1v), _small(pk_ref, "norm_ffn"))
        dx1_ref[...] = dx2 + dv
        dgf_ref[...] += dgf

    tile = lambda w: pl.BlockSpec((tm, w), lambda i: (i, 0))
    return _run("ffn_fwd_bwd", body, (T // tm,), [h, x1, tgt, wgT, wuT, wd, pk],
                [tile(D), tile(D), tile(D), VM, VM, VM, VM],
                [SDS((T, D), f32), SDS((T, D), MXU), SDS((T, F), MXU), SDS((T, F), MXU), SDS((T, F), MXU),
                 SDS((8, 128), f32), SDS((1, D), f32)],
                [tile(D), tile(D), tile(F), tile(F), tile(F), pl.BlockSpec((8, 128), lambda i: (0, 0)),
                 pl.BlockSpec((1, D), lambda i: (0, 0))], vmem_mib=56)


def matmul_tn(a, b, name, tmo, tk):
    T, M = a.shape
    N = b.shape[1]

    def body(a_ref, b_ref, o_ref):
        @pl.when(pl.program_id(1) == 0)
        def _():
            o_ref[...] = jnp.zeros_like(o_ref)

        o_ref[...] += _tn(a_ref[...], b_ref[...])

    return _run(name, body, (M // tmo, T // tk), [a, b],
                [pl.BlockSpec((tk, tmo), lambda m, k: (k, m)), pl.BlockSpec((tk, N), lambda m, k: (k, 0))],
                [SDS((M, N), f32)], [pl.BlockSpec((tmo, N), lambda m, k: (m, 0))], vmem_mib=48)[0]


def out_proj_bwd(dx1, merged, attn_out, conv_out, mem_out, pk, wout, tm):
    T, D = dx1.shape

    def body(dx1_ref, mg_ref, ao_ref, co_ref, mo_ref, pk_ref, w_ref,
             dao_ref, dco_ref, dmo_ref, dw_ref, dgain_ref):
        @pl.when(pl.program_id(0) == 0)
        def _():
            dw_ref[...] = jnp.zeros_like(dw_ref)
            dgain_ref[...] = jnp.zeros_like(dgain_ref)

        dxb = _c(dx1_ref[...])
        dw_ref[...] += _tn(mg_ref[...], dxb)
        dmg = _nt(dxb, w_ref[...])
        ao, co, mo = ao_ref[...], co_ref[...], mo_ref[...]
        da, ga = _norm_bwd(dmg[:, :ATT_W], ao, _rstd(ao), _small(pk_ref, "out_norm_attn"))
        dc, gc = _norm_bwd(dmg[:, ATT_W:ATT_W + CONV_W], co, _rstd(co), _small(pk_ref, "out_norm_conv"))
        dm, gm = _norm_bwd(dmg[:, ATT_W + CONV_W:], mo, _rstd(mo), _small(pk_ref, "out_norm_mem"))
        dao_ref[...] = da
        dco_ref[...] = dc
        dmo_ref[...] = dm
        dgain_ref[...] += jnp.concatenate([ga, gc, gm], axis=1)

    tile = lambda w: pl.BlockSpec((tm, w), lambda i: (i, 0))
    return _run("out_proj_bwd", body, (T // tm,), [dx1, merged, attn_out, conv_out, mem_out, pk, wout],
                [tile(D), tile(D), tile(ATT_W), tile(CONV_W), tile(MEM_W), VM, VM],
                [SDS((T, ATT_W), f32), SDS((T, CONV_W), f32), SDS((T, MEM_W), f32), SDS((D, D), f32), SDS((1, D), f32)],
                [tile(ATT_W), tile(CONV_W), tile(MEM_W), pl.BlockSpec((D, D), lambda i: (0, 0)),
                 pl.BlockSpec((1, D), lambda i: (0, 0))], vmem_mib=40)


def attn_bwd(qkv, d_attn, attn_out, sink_rows, BL, S, exchange):
    NB = S // BLK
    T = BL * S

    def body(q_ref, kc_ref, kp_ref, vc_ref, vp_ref, do_ref, ao_ref, sk_ref, tab_ref,
             dq_ref, dk_ref, dv_ref, dsk_ref, pend_k, pend_v):
        b, j = pl.program_id(0), pl.program_id(1)

        @pl.when((b == 0) & (j == 0))
        def _():
            dsk_ref[...] = jnp.zeros_like(dsk_ref)

        @pl.when(j == 0)
        def _():
            pend_k[...] = jnp.zeros_like(pend_k)
            pend_v[...] = jnp.zeros_like(pend_v)

        @pl.when(j < NB)
        def _():
            q, do, ao = q_ref[...], do_ref[...], ao_ref[...]
            k2 = jnp.concatenate([kp_ref[...], kc_ref[...]], axis=0)
            v2 = jnp.concatenate([vp_ref[...], vc_ref[...]], axis=0)
            lane = lax.broadcasted_iota(jnp.int32, (8, 128), 1)
            ones_w = jnp.ones((2 * BLK, 2 * BLK), MXU)
            dsk = jnp.zeros((8, 128), f32)
            dks, dvs = [], []
            for g in range(N_KV):
                kn, vh = k2[:, g * HD:(g + 1) * HD], v2[:, g * HD:(g + 1) * HD]
                qs = _stack_heads(q, g)
                s = _nt(qs, kn) * (HD ** -0.5) + tab_ref[g]
                e, es = _exp_scores(s, _sink_column(g, sk_ref))
                eb = _c(e)
                inv_w = 1.0 / (_nn(eb, ones_w) + es)
                inv_n = inv_w[:, :HD]
                dos = _stack_heads(do, g)
                delta = _rowsum_mxu(dos * _stack_heads(ao, g), 2 * BLK)
                dp = _nt(_c(dos), vh)
                ds = _c(e * inv_w * (dp - delta) * (HD ** -0.5))
                t = es * inv_n[:, 0:1] * delta[:, 0:1]
                for hh in range(GQA):
                    dsk = dsk + jnp.where(lane == g * GQA + hh, -jnp.sum(t[hh * BLK:(hh + 1) * BLK]), 0.0)
                dvs.append(_tn(eb, _c(dos * inv_n)))
                dks.append(_tn(ds, qs))
                dqs = _nn(ds, kn)
                for hh in range(GQA):
                    dq_ref[:, pl.ds((g * GQA + hh) * HD, HD)] = dqs[hh * BLK:(hh + 1) * BLK]
            dk2 = jnp.concatenate(dks, axis=1)
            dv2 = jnp.concatenate(dvs, axis=1)
            dk_ref[...] = pend_k[...] + dk2[:BLK]
            dv_ref[...] = pend_v[...] + dv2[:BLK]
            pend_k[...] = dk2[BLK:]
            pend_v[...] = dv2[BLK:]
            dsk_ref[...] += dsk

        @pl.when(j == NB)
        def _():
            dk_ref[...] = pend_k[...]
            dv_ref[...] = pend_v[...]

    cur = lambda col: (lambda b, j: (b * NB + jnp.minimum(j, NB - 1), col))
    prev = lambda col: (lambda b, j: (b * NB + jnp.maximum(j - 1, 0), col))
    small = lambda shape: pl.BlockSpec(shape, lambda b, j: (0, 0))
    return _run("attn_bwd", body, (BL, NB + 1), [qkv, qkv, qkv, qkv, qkv, d_attn, attn_out, sink_rows, _swa_bias_table()],
                [pl.BlockSpec((BLK, ATT_W), cur(0)),
                 pl.BlockSpec((BLK, KV_W), cur(4)), pl.BlockSpec((BLK, KV_W), prev(4)),
                 pl.BlockSpec((BLK, KV_W), cur(5)), pl.BlockSpec((BLK, KV_W), prev(5)),
                 pl.BlockSpec((BLK, ATT_W), cur(0)), pl.BlockSpec((BLK, ATT_W), cur(0)), small((8, 128)),
                 pl.BlockSpec((None, N_KV, GQA * BLK, 2 * BLK), lambda b, j: (jnp.minimum(j, 1), 0, 0, 0))],
                [SDS((T, ATT_W), f32), SDS((T, KV_W), f32), SDS((T, KV_W), f32), SDS((8, 128), f32)],
                [pl.BlockSpec((BLK, ATT_W), cur(0)), pl.BlockSpec((BLK, KV_W), prev(0)),
                 pl.BlockSpec((BLK, KV_W), prev(0)), small((8, 128))],
                scratch=[pltpu.VMEM((BLK, KV_W), f32)] * 2, exchange=exchange)


def mem_conv_bwd(d_mem_out, mem_out, d_conv_out, proj, qkv, km, vm, conv_w8, pk, S, tm, exchange):
    T = d_mem_out.shape[0]
    NM = km.shape[0] // (T // S)

    def body(dmo_ref, mo_ref, dco_ref, ch_ref, cb_ref, cc_ref, chh_ref, cch_ref, qm_ref, km_ref, vm_ref, cw_ref,
             pk_ref, dqm_ref, dkm_ref, dvm_ref, dcb_ref, dcv_ref, dcw_ref, dcbias_ref):
        i = pl.program_id(0)
        first = (i * tm) % S == 0

        @pl.when(i == 0)
        def _():
            dcw_ref[...] = jnp.zeros_like(dcw_ref)
            dcbias_ref[...] = jnp.zeros_like(dcbias_ref)

        @pl.when(first)
        def _():
            dkm_ref[...] = jnp.zeros_like(dkm_ref)
            dvm_ref[...] = jnp.zeros_like(dvm_ref)

        qm, kmv, vmv, dmo, mo = qm_ref[...], km_ref[...], vm_ref[...], dmo_ref[...], mo_ref[...]
        ones_w = jnp.ones((NM, NM), MXU)
        for h in range(N_MEMH):
            qh, kh, vh, e = _mem_head(qm, kmv, vmv, h)
            eb = _c(e)
            doh = dmo[:, h * HD:(h + 1) * HD]
            delta = _rowsum_mxu(doh * mo[:, h * HD:(h + 1) * HD], NM)
            dp = _nt(_c(doh), vh)
            inv_w = 1.0 / _nn(eb, ones_w)
            ds = _c(e * inv_w * (dp - delta) * (HD ** -0.5))
            dvm_ref[:, pl.ds(h * HD, HD)] += _tn(eb, _c(doh * inv_w[:, :HD]))
            dkm_ref[:, pl.ds(h * HD, HD)] += _tn(ds, qh)
            dqm_ref[:, pl.ds(h * HD, HD)] = _nn(ds, kh)

        u = cc_ref[...] * ch_ref[...]
        uh = jnp.where(first, 0.0, cch_ref[...] * chh_ref[...])
        u1, u2 = _conv_taps(u, uh)
        conv = cw_ref[0:1, :] * u2 + cw_ref[1:2, :] * u1 + cw_ref[2:3, :] * u + _small(pk_ref, "conv_b")
        dy = dco_ref[...]
        dcb_ref[...] = dy * conv
        dcv = dy * cb_ref[...]
        dcv_ref[...] = dcv
        dcbias_ref[...] += jnp.sum(dcv, axis=0, keepdims=True)
        dcw_ref[0:1, :] += jnp.sum(dcv * u2, axis=0, keepdims=True)
        dcw_ref[1:2, :] += jnp.sum(dcv * u1, axis=0, keepdims=True)
        dcw_ref[2:3, :] += jnp.sum(dcv * u, axis=0, keepdims=True)

    tile = lambda w, col: pl.BlockSpec((tm, w), lambda i: (i, col))
    halo = lambda col: pl.BlockSpec((8, CONV_W), lambda i: (jnp.maximum(i * (tm // 8) - 1, 0), col))
    seq = pl.BlockSpec((NM, MEM_W), lambda i: ((i * tm) // S, 0))
    const = lambda shape: pl.BlockSpec(shape, lambda i: (0, 0))
    return _run("mem_conv_bwd", body, (T // tm,),
                [d_mem_out, mem_out, d_conv_out, proj, proj, proj, proj, proj, qkv, km, vm, conv_w8, pk],
                [tile(MEM_W, 0), tile(MEM_W, 0), tile(CONV_W, 0), tile(CONV_W, 3), tile(CONV_W, 4), tile(CONV_W, 5),
                 halo(3), halo(5), tile(MEM_W, 3), seq, seq, VM, VM],
                [SDS((T, MEM_W), f32), SDS(km.shape, f32), SDS(km.shape, f32),
                 SDS((T, CONV_W), f32), SDS((T, CONV_W), f32), SDS((8, CONV_W), f32), SDS((1, CONV_W), f32)],
                [tile(MEM_W, 0), seq, seq, tile(CONV_W, 0), tile(CONV_W, 0), const((8, CONV_W)), const((1, CONV_W))],
                exchange=exchange)


def in_proj_bwd(dqn, dkn, dv, dcb, dcv, dqmn, proj, conv_w8, xn, x2d, dx1, pk, winT, S, tm, stages, ws, ms, vs):
    T, D = x2d.shape
    P = winT.shape[0]
    last_blk = T // 8 - 1
    n = len(stages)
    nsteps = T // tm
    tile_w = ws[0].shape[1] // (nsteps // 2)
    turn = [e * 2 // n for e in range(n)]

    def body(dq_ref, dk_ref, dv_ref, dcb_ref, dcv_ref, dcvn_ref, dqm_ref, qa_ref, ka_ref, ch_ref, cc_ref, qma_ref,
             cw_ref, xn_ref, x_ref, dx1_ref, pk_ref, w_ref, *rest):
        st, aw, am, av = (rest[k * n:(k + 1) * n] for k in range(4))
        dx_ref, dw_ref, dg_ref, dqg_ref, dkg_ref, dmqg_ref = rest[4 * n:4 * n + 6]
        aouts = rest[4 * n + 6:]
        i = pl.program_id(0)

        for parity in range(2):
            @pl.when(i % 2 == parity)
            def _(parity=parity):
                for e in range(n):
                    if turn[e] == parity:
                        g = jnp.concatenate([_sum_chips(st[e].at[0]), _sum_chips(st[e].at[1])], axis=0)
                        d, mm, vv = _adamw_math(aw[e][...], g, am[e][...], av[e][...])
                        for k, val in enumerate((g, d, mm, vv)):
                            aouts[4 * e + k][...] = val

        @pl.when(i == 0)
        def _():
            dw_ref[...] = jnp.zeros_like(dw_ref)
            dg_ref[...] = jnp.zeros_like(dg_ref)
            dqg_ref[...] = jnp.zeros_like(dqg_ref)
            dkg_ref[...] = jnp.zeros_like(dkg_ref)
            dmqg_ref[...] = jnp.zeros_like(dmqg_ref)

        dqa, gq = _heads_norm_bwd(dq_ref[...], qa_ref[...], _small(pk_ref, "q_norm"))
        dka, gk = _heads_norm_bwd(dk_ref[...], ka_ref[...], _small(pk_ref, "k_norm"))
        dqma, gmq = _heads_norm_bwd(dqm_ref[...], qma_ref[...], _small(pk_ref, "mem_q_norm"))
        dqg_ref[...] += gq
        dkg_ref[...] += gk
        dmqg_ref[...] += gmq

        last = ((i + 1) * tm) % S == 0
        dcv = dcv_ref[...]
        nxt = jnp.where(last, 0.0, dcvn_ref[...])
        row = lax.broadcasted_iota(jnp.int32, dcv.shape, 0)
        n1 = jnp.where(row == tm - 1, nxt[0:1, :], pltpu.roll(dcv, tm - 1, 0))
        n2 = jnp.where(row == tm - 2, nxt[0:1, :], jnp.where(row == tm - 1, nxt[1:2, :], pltpu.roll(dcv, tm - 2, 0)))
        du = cw_ref[2:3, :] * dcv + cw_ref[1:2, :] * n1 + cw_ref[0:1, :] * n2
        d_proj = jnp.concatenate([_c(dqa), _c(dka), _c(dv_ref[...]), _c(du * cc_ref[...]),
                                  _c(dcb_ref[...]), _c(du * ch_ref[...]), _c(dqma)], axis=1)
        dw_ref[...] += _tn(d_proj, xn_ref[...])
        xv = x_ref[...]
        dv_, dg = _norm_bwd(_nn(d_proj, w_ref[...]), xv, _rstd(xv), _small(pk_ref, "norm_mix"))
        dx_ref[...] = dx1_ref[...] + dv_
        dg_ref[...] += dg

    tile = lambda w, col=0: pl.BlockSpec((tm, w), lambda i: (i, col))
    nhalo = pl.BlockSpec((8, CONV_W), lambda i: (jnp.minimum((i + 1) * (tm // 8), last_blk), 0))
    const = lambda shape: pl.BlockSpec(shape, lambda i: (0, 0))
    st_specs = [pl.BlockSpec((2, 4, s.shape[2], tile_w), lambda i: (0, 0, 0, i // 2)) for s in stages]
    w_specs = [pl.BlockSpec((w.shape[0], tile_w), lambda i: (0, i // 2)) for w in ws]
    res = _run("in_proj_bwd", body, (nsteps,),
               [dqn, dkn, dv, dcb, dcv, dcv, dqmn, proj, proj, proj, proj, proj, conv_w8, xn, x2d, dx1, pk, winT]
               + list(stages) + list(ws) + list(ms) + list(vs),
               [tile(ATT_W), tile(KV_W), tile(KV_W), tile(CONV_W), tile(CONV_W), nhalo, tile(MEM_W),
                tile(ATT_W, 0), tile(KV_W, 4), tile(CONV_W, 3), tile(CONV_W, 5), tile(MEM_W, 6), VM,
                tile(D), tile(D), tile(D), VM, VM] + st_specs + w_specs * 3,
               [SDS((T, D), f32), SDS((P, D), f32), SDS((1, D), f32), SDS((1, HD), f32), SDS((1, HD), f32),
                SDS((1, HD), f32)] + [SDS(w.shape, f32) for w in ws for _ in range(4)],
               [tile(D), pl.BlockSpec((P, D), lambda i: (0, 0)), const((1, D)), const((1, HD)), const((1, HD)),
                const((1, HD))] + [s for s in w_specs for _ in range(4)],
               vmem_mib=56)
    return res[:6], [res[6 + 4 * e:10 + 4 * e] for e in range(n)]


def mem_kv_bwd(dkm, dvm, kv, memn, mem2d, pk, wmkv):
    def body(dkm_ref, dvm_ref, kv_ref, mn_ref, m_ref, pk_ref, w_ref, dw_ref, dg_ref, dkg_ref):
        dkk, dkg = _heads_norm_bwd(dkm_ref[...], kv_ref[:, :MEM_W], _small(pk_ref, "mem_k_norm"))
        dkg_ref[...] = dkg
        dkv = _c(jnp.concatenate([dkk, dvm_ref[...]], axis=1))
        dw_ref[...] = _tn(mn_ref[...], dkv)
        mv = m_ref[...]
        dg_ref[...] = jnp.sum(_nt(dkv, w_ref[...]) * mv * _rstd(mv), axis=0, keepdims=True)

    return _run("mem_kv_bwd", body, (), [dkm, dvm, kv, memn, mem2d, pk, wmkv], [VM] * 7,
                [SDS(wmkv.shape, f32), SDS((1, mem2d.shape[1]), f32), SDS((1, HD), f32)], [VM] * 3, vmem_mib=40)


def _halves_view(g):
    return g.reshape(4, 2, g.shape[0] // 8, g.shape[1])


def kernel(x, mem, norm_mix, w_in, q_norm, k_norm, attn_sinks, conv_w, conv_b, norm_mem, w_mem_kv, mem_q_norm, mem_k_norm, out_norm_attn, out_norm_conv, out_norm_mem, w_out, norm_ffn, w_gate, w_up, w_down, loss_target, m_norm_mix, m_w_in, m_q_norm, m_k_norm, m_attn_sinks, m_conv_w, m_conv_b, m_norm_mem, m_w_mem_kv, m_mem_q_norm, m_mem_k_norm, m_out_norm_attn, m_out_norm_conv, m_out_norm_mem, m_w_out, m_norm_ffn, m_w_gate, m_w_up, m_w_down, v_norm_mix, v_w_in, v_q_norm, v_k_norm, v_attn_sinks, v_conv_w, v_conv_b, v_norm_mem, v_w_mem_kv, v_mem_q_norm, v_mem_k_norm, v_out_norm_attn, v_out_norm_conv, v_out_norm_mem, v_w_out, v_norm_ffn, v_w_gate, v_w_up, v_w_down):
    BL, S, D = x.shape
    T = BL * S
    TM = 256
    _, _, ci = _place()
    cidx = ci.reshape(1).astype(jnp.int32)
    w_small = dict(norm_mix=norm_mix, norm_mem=norm_mem, norm_ffn=norm_ffn, out_norm_attn=out_norm_attn,
                   out_norm_conv=out_norm_conv, out_norm_mem=out_norm_mem, conv_w=conv_w, conv_b=conv_b, q_norm=q_norm,
                   k_norm=k_norm, mem_q_norm=mem_q_norm, mem_k_norm=mem_k_norm, attn_sinks=attn_sinks)
    m_small = dict(norm_mix=m_norm_mix, norm_mem=m_norm_mem, norm_ffn=m_norm_ffn, out_norm_attn=m_out_norm_attn,
                   out_norm_conv=m_out_norm_conv, out_norm_mem=m_out_norm_mem, conv_w=m_conv_w, conv_b=m_conv_b,
                   q_norm=m_q_norm, k_norm=m_k_norm, mem_q_norm=m_mem_q_norm, mem_k_norm=m_mem_k_norm,
                   attn_sinks=m_attn_sinks)
    v_small = dict(norm_mix=v_norm_mix, norm_mem=v_norm_mem, norm_ffn=v_norm_ffn, out_norm_attn=v_out_norm_attn,
                   out_norm_conv=v_out_norm_conv, out_norm_mem=v_out_norm_mem, conv_w=v_conv_w, conv_b=v_conv_b,
                   q_norm=v_q_norm, k_norm=v_k_norm, mem_q_norm=v_mem_q_norm, mem_k_norm=v_mem_k_norm,
                   attn_sinks=v_attn_sinks)
    pk = _pack_small(w_small)

    rowblocks = lambda a, b, c, d, e, f: [a[0].T, b[0].T, c[0].T, d[0], e[0], f[0]]
    w_rb = rowblocks(w_in, w_gate, w_up, w_down, w_out, w_mem_kv)
    m_rb = rowblocks(m_w_in, m_w_gate, m_w_up, m_w_down, m_w_out, m_w_mem_kv)
    v_rb = rowblocks(v_w_in, v_w_gate, v_w_up, v_w_down, v_w_out, v_w_mem_kv)
    (winT_s,) = prep_weights("prep_w_in", w_rb[:1])
    cw_pad = jnp.zeros((8, 128), f32).at[:3, :HD].set(conv_w[0])
    (wgT_s, wuT_s, wd_s, wout_s, wmkv_s), (winT, cw_all) = prep_weights(
        "gather_w_in", w_rb[1:], gather_exchange([winT_s, cw_pad], [True, False]))
    conv_w_full = jnp.transpose(cw_all.reshape(4, 8, 128)[:, :3, :HD], (1, 0, 2)).reshape(3, CONV_W)
    conv_w8 = jnp.zeros((8, CONV_W), f32).at[:3].set(conv_w_full)
    sink_rows = jnp.broadcast_to(attn_sinks.reshape(N_Q, 1), (N_Q, 128))

    x2d = x.reshape(T, D)
    mem2d = mem.reshape(-1, D)
    (xn, proj, qkv), (wgT,) = in_proj_fwd(x2d, pk, winT, TM, gather_exchange([wgT_s], [True]))
    (attn_out,), (wuT, wout, wmkv) = attn_fwd(qkv, sink_rows, BL, S,
                                              gather_exchange([wuT_s, wout_s, wmkv_s], [True, True, True], relay_early=3))
    memn, kv, km, vm = mem_kv_fwd(mem2d, pk, wmkv)
    (conv_out, mem_out, merged, x1, h), (wd,) = mixer_tail_fwd(
        x2d, attn_out, proj, qkv, km, vm, conv_w8, pk, wout, S, TM, gather_exchange([wd_s], [True]))

    dx1, dx2b, act, d_gate, d_up, loss8, d_norm_ffn = ffn_fwd_bwd(h, x1, loss_target.reshape(T, D), wgT, wuT, wd, pk, TM)
    F = wd.shape[0]
    g_wd = matmul_tn(act, dx2b, "dw_down", F // 2, min(T, 1024))
    g_wgT = matmul_tn(d_gate, h, "dw_gate", F // 2, min(T, 1024))
    g_wuT = matmul_tn(d_up, h, "dw_up", F // 2, min(T, 1024))

    d_attn, d_conv_out, d_mem_out, g_wout, d_gains = out_proj_bwd(dx1, merged, attn_out, conv_out, mem_out, pk, wout, TM)
    late = [_halves_view(g) for g in (g_wgT, g_wuT, g_wd, g_wout)]
    (dqmn, dkm, dvm, dcb, dcv, d_cw8, d_cbias), late_sib = mem_conv_bwd(
        d_mem_out, mem_out, d_conv_out, proj, qkv, km, vm, conv_w8, pk, S, TM, halves_exchange(late))
    late_part = add_halves(cidx, late, late_sib, "grad_add_halves_ffn")
    (dqn, dkn, dv, d_sink8), late_stage = attn_bwd(qkv, d_attn, attn_out, sink_rows, BL, S, scatter_exchange(late_part))
    (g_x, g_winT, d_norm_mix, d_qg, d_kg, d_mqg), late_res = in_proj_bwd(
        dqn, dkn, dv, dcb, dcv, dqmn, proj, conv_w8, xn, x2d, dx1, pk, winT, S, TM,
        late_stage, w_rb[1:5], m_rb[1:5], v_rb[1:5])
    g_wmkv, d_norm_mem, d_mkg = mem_kv_bwd(dkm, dvm, kv, memn, mem2d, pk, wmkv)

    tot, tail_stage = tail_reduce(d_norm_mix, d_norm_mem, d_norm_ffn, d_gains, d_cw8, d_cbias, d_qg, d_kg, d_mqg, d_mkg,
                                  d_sink8, loss8, [_halves_view(g) for g in (g_winT, g_wmkv)])
    loss = tot[5, 384]
    tail_res, _ = adamw_big("adamw_tail", tail_stage, [w_rb[0], w_rb[5]], [m_rb[0], m_rb[5]], [v_rb[0], v_rb[5]], 4)
    res = {"w_in": [a.T[None] for a in tail_res[0]], "w_gate": [a.T[None] for a in late_res[0]],
           "w_up": [a.T[None] for a in late_res[1]], "w_down": [a[None] for a in late_res[2]],
           "w_out": [a[None] for a in late_res[3]], "w_mem_kv": [a[None] for a in tail_res[1]]}
    res.update(adamw_small(tot, pk, _pack_small(m_small), _pack_small(v_small), {k: w_small[k].shape for k in SMALL}))

    order = ["norm_mix", "w_in", "q_norm", "k_norm", "attn_sinks", "conv_w", "conv_b", "norm_mem", "w_mem_kv",
             "mem_q_norm", "mem_k_norm", "out_norm_attn", "out_norm_conv", "out_norm_mem", "w_out", "norm_ffn",
             "w_gate", "w_up", "w_down"]
    return (loss, g_x.reshape(BL, S, D), *[res[n][0] for n in order], *[res[n][1] for n in order],
            *[res[n][2] for n in order], *[res[n][3] for n in order])
```

```python
import collections
import functools

import jax
import jax.numpy as jnp
import numpy as np
from jax import lax
from jax.experimental import pallas as pl
from jax.experimental.pallas import tpu as pltpu

f32 = jnp.float32
MXU = jnp.bfloat16
WIRE = jnp.bfloat16
EPS = 1e-6
NEG = -1e30
HD = 64
BLK = 128
N_Q, N_KV, N_MEMH = 8, 2, 4
GQA = N_Q // N_KV
ATT_W, KV_W, CONV_W, MEM_W = 512, 128, 256, 256
VMEM_MIB = 1024 * 1024
ADAM_LR, ADAM_B1, ADAM_B2, ADAM_EPS, ADAM_WD, ADAM_STEP = 0.001, 0.9, 0.999, 1e-08, 0.01, 10

MESH = pl.DeviceIdType.MESH
VM = pl.BlockSpec(memory_space=pltpu.VMEM)
ANY = pl.BlockSpec(memory_space=pl.ANY)
SDS = jax.ShapeDtypeStruct
DMA = pltpu.SemaphoreType.DMA


def _c(v):
    return v.astype(MXU)


def _nn(a, b):
    return lax.dot_general(a, b, (((1,), (0,)), ((), ())), preferred_element_type=f32)


def _nt(a, b):
    return lax.dot_general(a, b, (((1,), (1,)), ((), ())), preferred_element_type=f32)


def _tn(a, b):
    return lax.dot_general(a, b, (((0,), (0,)), ((), ())), preferred_element_type=f32)


def _rstd(v):
    return lax.rsqrt(jnp.mean(v * v, axis=-1, keepdims=True) + EPS)


def _norm_bwd(dy, v, r, g):
    dyg = dy * g
    dv = r * dyg - v * (r * r * r) * jnp.mean(dyg * v, axis=-1, keepdims=True)
    return dv, jnp.sum(dy * v * r, axis=0, keepdims=True)


def _split3(v):
    hi = _c(v)
    r1 = v - hi.astype(f32)
    mid = _c(r1)
    return hi, mid, _c(r1 - mid.astype(f32))


def _rowsum_mxu(v, width):
    ones = jnp.ones((v.shape[1], width), MXU)
    return sum(_nn(a, ones) for a in _split3(v))


def _seg_sums(v):
    r = lax.broadcasted_iota(jnp.int32, (2 * HD, 2 * HD), 0) // HD
    c = lax.broadcasted_iota(jnp.int32, (2 * HD, 2 * HD), 1) // HD
    bd = (r == c).astype(MXU)
    outs = []
    for b in range(v.shape[1] // (2 * HD)):
        outs.append(sum(_nn(a, bd) for a in _split3(v[:, b * 2 * HD:(b + 1) * 2 * HD])))
    return outs[0] if len(outs) == 1 else jnp.concatenate(outs, axis=1)


def _lanes(g, width):
    return jnp.concatenate([g] * (width // HD), axis=1)


def _heads_rstd(v):
    return lax.rsqrt(_seg_sums(v * v) * (1.0 / HD) + EPS)


def _heads_norm_bwd(dy, v, g):
    r = _heads_rstd(v)
    gl = _lanes(g, v.shape[1])
    dyg = dy * gl
    dv = r * dyg - v * (r * r * r) * (_seg_sums(dyg * v) * (1.0 / HD))
    dgl = jnp.sum(dy * v * r, axis=0, keepdims=True)
    return dv, sum(dgl[:, s * HD:(s + 1) * HD] for s in range(v.shape[1] // HD))


def _exp_scores(s, extra=None):
    m = jnp.max(s, axis=-1, keepdims=True)
    if extra is None:
        return jnp.exp(s - m), None
    m = jnp.maximum(m, extra)
    return jnp.exp(s - m), jnp.exp(extra - m)


def _place():
    return lax.axis_index("x"), lax.axis_index("y"), lax.axis_index("c")


SMALL_AT = {"norm_mix": (0, 0, 1024), "norm_mem": (1, 0, 1024), "norm_ffn": (2, 0, 1024),
            "out_norm_attn": (3, 0, ATT_W), "out_norm_conv": (3, ATT_W, CONV_W), "out_norm_mem": (3, ATT_W + CONV_W, MEM_W),
            "conv_b": (4, 3 * CONV_W, CONV_W), "q_norm": (5, 0, HD), "k_norm": (5, HD, HD), "mem_q_norm": (5, 2 * HD, HD),
            "mem_k_norm": (5, 3 * HD, HD), "attn_sinks": (5, 256, N_Q)}
SMALL = ("norm_mix", "norm_mem", "norm_ffn", "out_norm_attn", "out_norm_conv", "out_norm_mem", "conv_w", "conv_b",
         "q_norm", "k_norm", "mem_q_norm", "mem_k_norm", "attn_sinks")


def _small(pk_ref, name):
    r, c0, w = SMALL_AT[name]
    return pk_ref[r:r + 1, c0:c0 + w]


def _pack_small(d):
    z = lambda n: jnp.zeros((1, n), f32)
    row3 = jnp.concatenate([d["out_norm_attn"], d["out_norm_conv"], d["out_norm_mem"]], axis=1)
    row4 = jnp.concatenate([d["conv_w"].reshape(1, 3 * HD), z(3 * CONV_W - 3 * HD), d["conv_b"]], axis=1)
    row5 = jnp.concatenate([d["q_norm"], d["k_norm"], d["mem_q_norm"], d["mem_k_norm"], d["attn_sinks"],
                            z(1024 - 4 * HD - N_Q)], axis=1)
    return jnp.concatenate([d["norm_mix"], d["norm_mem"], d["norm_ffn"], row3, row4, row5, z(1024), z(1024)], axis=0)


def _other_chips(x, y):
    return [(1 - x, y), (x, 1 - y), (1 - x, 1 - y)]


Exchange = collections.namedtuple("Exchange", "ins outs sems start finish relay relay_steps_before_end", defaults=(None, 0))


def _run(name, body, grid, ins, in_specs, out_shape, out_specs, scratch=(), vmem_mib=32, exchange=None):
    ins, in_specs, out_shape, out_specs, scratch = list(ins), list(in_specs), list(out_shape), list(out_specs), list(scratch)
    ni, no, ns = len(ins), len(out_shape), len(scratch)
    ex = exchange
    if ex is not None:
        nxi, nxo = len(ex.ins), len(ex.outs)

    def call_body(*refs):
        if ex is None:
            body(*refs)
            return
        a, xa = refs[:ni], refs[ni:ni + nxi]
        o, xo = refs[ni + nxi:ni + nxi + no], refs[ni + nxi + no:ni + nxi + no + nxo]
        s, xs = refs[ni + nxi + no + nxo:ni + nxi + no + nxo + ns], refs[ni + nxi + no + nxo + ns:]
        if grid:
            first = functools.reduce(jnp.logical_and, [pl.program_id(d) == 0 for d in range(len(grid))])
            last = functools.reduce(jnp.logical_and, [pl.program_id(d) == grid[d] - 1 for d in range(len(grid))])
            pl.when(first)(lambda: ex.start(xa, xo, xs))
            body(*a, *o, *s)
            if ex.relay is not None:
                early = functools.reduce(jnp.logical_and, [pl.program_id(d) == grid[d] - 1 for d in range(len(grid) - 1)],
                                         pl.program_id(len(grid) - 1) == grid[-1] - 1 - ex.relay_steps_before_end)
                pl.when(early)(lambda: ex.relay(xa, xo, xs))
            pl.when(last)(lambda: ex.finish(xa, xo, xs))
        else:
            ex.start(xa, xo, xs)
            if body is not None:
                body(*a, *o, *s)
            if ex.relay is not None:
                ex.relay(xa, xo, xs)
            ex.finish(xa, xo, xs)

    if ex is not None:
        ins, in_specs = ins + list(ex.ins), in_specs + [ANY] * nxi
        out_shape, out_specs = out_shape + list(ex.outs), out_specs + [ANY] * nxo
        scratch = scratch + list(ex.sems)
    kw = dict(grid=grid) if grid else {}
    res = pl.pallas_call(
        call_body, name=name, out_shape=out_shape, in_specs=in_specs, out_specs=out_specs, scratch_shapes=scratch,
        compiler_params=pltpu.CompilerParams(dimension_semantics=("arbitrary",) * len(grid) if grid else None,
                                             vmem_limit_bytes=vmem_mib * VMEM_MIB), **kw)(*ins)
    res = list(res)
    return (res[:no], res[no:]) if ex is not None else res


def _remote(src, dst, ssem, rsem, dev):
    return pltpu.make_async_remote_copy(src_ref=src, dst_ref=dst, send_sem=ssem, recv_sem=rsem,
                                        device_id=dev, device_id_type=MESH)


def gather_exchange(shards, split, relay_early=0):
    n = len(shards)

    def rows(ref, e, kk, half=None):
        R = shards[e].shape[0]
        if half is None:
            return ref.at[pl.ds(pl.multiple_of(kk * R, 8), R)]
        return ref.at[pl.ds(pl.multiple_of(kk * R + half * (R // 2), 8), R // 2)]

    def ici(src, dst, sm, e, j, chip_j, x, y, c):
        k = 2 * x + y
        if split[e]:
            s = src[e].at[pl.ds(pl.multiple_of(c * (shards[e].shape[0] // 2), 8), shards[e].shape[0] // 2)]
            return _remote(s, rows(dst[e], e, k, c), sm[0].at[6 * e + j], sm[1].at[6 * e + j], (*chip_j, c))
        return _remote(src[e], rows(dst[e], e, k), sm[0].at[6 * e + j], sm[1].at[6 * e + j], (*chip_j, c))

    def landed(dst, e, chip_j, c):
        kj = 2 * chip_j[0] + chip_j[1]
        return rows(dst[e], e, kj, c) if split[e] else rows(dst[e], e, kj)

    def forward(dst, sm, e, j, chip_j, x, y, c, sender_c):
        kj = 2 * chip_j[0] + chip_j[1]
        r = rows(dst[e], e, kj, sender_c)
        return _remote(r, r, sm[0].at[6 * e + 3 + j], sm[1].at[6 * e + 3 + j], (x, y, 1 - c))

    def local(src, dst, sm, e, x, y):
        return pltpu.make_async_copy(src[e], rows(dst[e], e, 2 * x + y), sm[2].at[e])

    def start(src, dst, sm):
        x, y, c = _place()
        for e in range(n):
            local(src, dst, sm, e, x, y).start()
            for j, chip_j in enumerate(_other_chips(x, y)):
                ici(src, dst, sm, e, j, chip_j, x, y, c).start()

    def relay(src, dst, sm):
        x, y, c = _place()
        for e in range(n):
            for j, chip_j in enumerate(_other_chips(x, y)):
                r = landed(dst, e, chip_j, c)
                _remote(r, r, sm[0].at[6 * e + j], sm[1].at[6 * e + j], (*chip_j, c)).wait_recv()
                if split[e]:
                    forward(dst, sm, e, j, chip_j, x, y, c, c).start()

    def finish(src, dst, sm):
        x, y, c = _place()
        chips = _other_chips(x, y)
        for e in range(n):
            for j, chip_j in enumerate(chips):
                if split[e]:
                    forward(dst, sm, e, j, chip_j, x, y, c, 1 - c).wait_recv()
        for e in range(n):
            for j, chip_j in enumerate(chips):
                ici(src, dst, sm, e, j, chip_j, x, y, c).wait_send()
                if split[e]:
                    forward(dst, sm, e, j, chip_j, x, y, c, c).wait_send()
            local(src, dst, sm, e, x, y).wait()

    outs = [SDS((4 * s.shape[0], s.shape[1]), s.dtype) for s in shards]
    return Exchange(list(shards), outs, [DMA((6 * n,)), DMA((6 * n,)), DMA((n,))], start, finish, relay, relay_early)


def halves_exchange(grads):
    n = len(grads)

    def copy(g, st, sm, e, x, y, c):
        return _remote(g[e].at[:, 1 - c], st[e], sm[0].at[e], sm[1].at[e], (x, y, 1 - c))

    def start(g, st, sm):
        x, y, c = _place()
        for e in range(n):
            copy(g, st, sm, e, x, y, c).start()

    def finish(g, st, sm):
        x, y, c = _place()
        for e in range(n):
            copy(g, st, sm, e, x, y, c).wait()

    outs = [SDS((4,) + a.shape[2:], a.dtype) for a in grads]
    return Exchange(list(grads), outs, [DMA((n,)), DMA((n,))], start, finish)


def scatter_exchange(parts):
    n = len(parts)

    def ici(p, st, sm, e, j, chip_j, x, y, c):
        k, kj = 2 * x + y, 2 * chip_j[0] + chip_j[1]
        return _remote(p[e].at[kj], st[e].at[c, k], sm[0].at[8 * e + j], sm[1].at[8 * e + j], (*chip_j, c))

    def own(p, st, sm, e, x, y, c):
        k = 2 * x + y
        return _remote(p[e].at[k], st[e].at[c, k], sm[0].at[8 * e + 3], sm[1].at[8 * e + 3], (x, y, 1 - c))

    def forward(st, sm, e, j, chip_j, x, y, c, sender_c):
        kj = 2 * chip_j[0] + chip_j[1]
        r = st[e].at[sender_c, kj]
        return _remote(r, r, sm[0].at[8 * e + 4 + j], sm[1].at[8 * e + 4 + j], (x, y, 1 - c))

    def local(p, st, sm, e, x, y, c):
        k = 2 * x + y
        return pltpu.make_async_copy(p[e].at[k], st[e].at[c, k], sm[2].at[e])

    def start(p, st, sm):
        x, y, c = _place()
        for e in range(n):
            local(p, st, sm, e, x, y, c).start()
            own(p, st, sm, e, x, y, c).start()
            for j, chip_j in enumerate(_other_chips(x, y)):
                ici(p, st, sm, e, j, chip_j, x, y, c).start()

    def relay(p, st, sm):
        x, y, c = _place()
        for e in range(n):
            for j, chip_j in enumerate(_other_chips(x, y)):
                kj = 2 * chip_j[0] + chip_j[1]
                r = st[e].at[c, kj]
                _remote(r, r, sm[0].at[8 * e + j], sm[1].at[8 * e + j], (*chip_j, c)).wait_recv()
                forward(st, sm, e, j, chip_j, x, y, c, c).start()

    def finish(p, st, sm):
        x, y, c = _place()
        k = 2 * x + y
        chips = _other_chips(x, y)
        for e in range(n):
            r = st[e].at[1 - c, k]
            _remote(r, r, sm[0].at[8 * e + 3], sm[1].at[8 * e + 3], (x, y, 1 - c)).wait_recv()
            for j, chip_j in enumerate(chips):
                forward(st, sm, e, j, chip_j, x, y, c, 1 - c).wait_recv()
        for e in range(n):
            own(p, st, sm, e, x, y, c).wait_send()
            for j, chip_j in enumerate(chips):
                ici(p, st, sm, e, j, chip_j, x, y, c).wait_send()
                forward(st, sm, e, j, chip_j, x, y, c, c).wait_send()
            local(p, st, sm, e, x, y, c).wait()

    outs = [SDS((2,) + a.shape, a.dtype) for a in parts]
    return Exchange(list(parts), outs, [DMA((8 * n,)), DMA((8 * n,)), DMA((n,))], start, finish, relay)


def tail_reduce(d_norm_mix, d_norm_mem, d_norm_ffn, d_gains, d_cw8, d_cbias, d_qg, d_kg, d_mqg, d_mkg, d_sink8, loss8, tail):
    n = len(tail)
    halves = halves_exchange(tail)
    scatter = scatter_exchange([SDS((4,) + a.shape[2:], WIRE) for a in tail])

    def body(nm_ref, nmem_ref, nf_ref, gn_ref, cw_ref, cb_ref, qg_ref, kg_ref, mqg_ref, mkg_ref, sk_ref, ls_ref, *rest):
        g, o_ref, st = rest[:n], rest[n], rest[n + 1:2 * n + 1]
        buf, ssem, rsem = rest[2 * n + 1:2 * n + 4]
        own, sib, part = (rest[2 * n + 4 + i * n:2 * n + 4 + (i + 1) * n] for i in range(3))
        lsem = rest[5 * n + 4]
        hsem, xsem = rest[5 * n + 5:5 * n + 7], rest[5 * n + 7:]
        x, y, c = _place()
        loads = [pltpu.make_async_copy(g[e].at[:, c], own[e], lsem.at[e]) for e in range(n)]
        for ld in loads:
            ld.start()
        halves.start(g, sib, hsem)
        me = 4 * x + 2 * y + c
        mine = buf.at[me]
        mine[...] = jnp.zeros((8, 1024), f32)
        mine[0:1, :] = nm_ref[...]
        mine[1:2, :] = nmem_ref[...]
        mine[2:3, :] = nf_ref[...]
        mine[3:4, :] = gn_ref[...]
        for j in range(3):
            mine[4:5, pl.ds(j * CONV_W, CONV_W)] = cw_ref[j:j + 1, :]
        mine[4:5, pl.ds(3 * CONV_W, CONV_W)] = cb_ref[...]
        for j, r in enumerate((qg_ref, kg_ref, mqg_ref, mkg_ref)):
            mine[5:6, pl.ds(j * HD, HD)] = r[...]
        mine[5:6, pl.ds(256, 128)] = sk_ref[0:1, :]
        mine[5:6, pl.ds(384, 128)] = ls_ref[0:1, :]

        def peer_of(m):
            return (1 - x if m & 4 else x, 1 - y if m & 2 else y, 1 - c if m & 1 else c)

        for m in range(1, 8):
            _remote(mine, mine, ssem.at[m - 1], rsem.at[m - 1], peer_of(m)).start()
        for ld in loads:
            ld.wait()
        halves.finish(g, sib, hsem)
        for e in range(n):
            part[e][...] = (own[e][...] + sib[e][...]).astype(WIRE)
        scatter.start(part, st, xsem)
        scatter.relay(part, st, xsem)
        scatter.finish(part, st, xsem)
        for m in range(1, 8):
            p = peer_of(m)
            got = buf.at[4 * p[0] + 2 * p[1] + p[2]]
            _remote(got, got, ssem.at[m - 1], rsem.at[m - 1], p).wait_recv()
        for m in range(1, 8):
            _remote(mine, mine, ssem.at[m - 1], rsem.at[m - 1], peer_of(m)).wait_send()
        acc = buf[0]
        for d in range(1, 8):
            acc = acc + buf[d]
        o_ref[...] = acc

    ins = [d_norm_mix, d_norm_mem, d_norm_ffn, d_gains, d_cw8, d_cbias, d_qg, d_kg, d_mqg, d_mkg, d_sink8, loss8]
    half_shape = [(4,) + a.shape[2:] for a in tail]
    scratch = ([pltpu.VMEM((8, 8, 1024), f32), DMA((7,)), DMA((7,))]
               + [pltpu.VMEM(s, f32) for s in half_shape] * 2 + [pltpu.VMEM(s, WIRE) for s in half_shape]
               + [DMA((n,))] + list(halves.sems) + list(scatter.sems))
    res = _run("tail_reduce", body, (), ins + list(tail), [VM] * len(ins) + [ANY] * n,
               [SDS((8, 1024), f32)] + list(scatter.outs), [VM] + [ANY] * n, scratch=scratch, vmem_mib=40)
    return res[0], res[1:]


def add_halves(cidx, grads, stages, name, nch=2):
    n = len(grads)

    def body(c_ref, *refs):
        g, st, o = refs[:n], refs[n:2 * n], refs[2 * n:]
        for e in range(n):
            o[e][...] = (g[e][...] + st[e][...]).astype(WIRE)

    in_specs, out_specs, out_shape = [], [], []
    for a in grads:
        hr, C = a.shape[2], a.shape[3]
        in_specs.append(pl.BlockSpec((None, None, hr // nch, C), lambda s, q, c_ref: (s, c_ref[0], q, 0)))
    for a in stages:
        hr, C = a.shape[1], a.shape[2]
        in_specs.append(pl.BlockSpec((None, hr // nch, C), lambda s, q, c_ref: (s, q, 0)))
        out_specs.append(pl.BlockSpec((None, hr // nch, C), lambda s, q, c_ref: (s, q, 0)))
        out_shape.append(SDS(a.shape, WIRE))
    return pl.pallas_call(
        body, name=name, out_shape=out_shape,
        grid_spec=pltpu.PrefetchScalarGridSpec(num_scalar_prefetch=1, grid=(4, nch), in_specs=in_specs, out_specs=out_specs),
        compiler_params=pltpu.CompilerParams(dimension_semantics=("arbitrary", "arbitrary")),
    )(cidx, *grads, *stages)


def _adamw_math(w, g, m, v):
    m = ADAM_B1 * m + (1.0 - ADAM_B1) * g
    v = ADAM_B2 * v + (1.0 - ADAM_B2) * (g * g)
    m_hat = m / (1.0 - ADAM_B1 ** ADAM_STEP)
    v_hat = v / (1.0 - ADAM_B2 ** ADAM_STEP)
    delta = -ADAM_LR * (m_hat / (jnp.sqrt(v_hat) + ADAM_EPS) + ADAM_WD * w)
    return delta, m, v


def _sum_chips(st):
    return ((st[0].astype(f32) + st[1].astype(f32)) + st[2].astype(f32)) + st[3].astype(f32)


def adamw_big(name, stages, ws, ms, vs, nstep, exchange=None):
    n = len(stages)

    def body(*refs):
        st, w, m, v = refs[:n], refs[n:2 * n], refs[2 * n:3 * n], refs[3 * n:4 * n]
        outs = refs[4 * n:]
        for e in range(n):
            g = jnp.concatenate([_sum_chips(st[e].at[0]), _sum_chips(st[e].at[1])], axis=0)
            d, mm, vv = _adamw_math(w[e][...], g, m[e][...], v[e][...])
            outs[4 * e][...] = g
            outs[4 * e + 1][...] = d
            outs[4 * e + 2][...] = mm
            outs[4 * e + 3][...] = vv

    st_specs, w_specs = [], []
    for e in range(n):
        _, _, hr, C = stages[e].shape
        st_specs.append(pl.BlockSpec((2, 4, hr, C // nstep), lambda i: (0, 0, 0, i)))
        w_specs.append(pl.BlockSpec((2 * hr, C // nstep), lambda i: (0, i)))
    out_specs = [s for s in w_specs for _ in range(4)]
    out_shape = [SDS(w.shape, f32) for w in ws for _ in range(4)]
    res = _run(name, body, (nstep,), list(stages) + list(ws) + list(ms) + list(vs), st_specs + w_specs * 3,
               out_shape, out_specs, vmem_mib=48, exchange=exchange)
    res, sent = res if exchange is not None else (res, None)
    return [res[4 * e:4 * e + 4] for e in range(n)], sent


def adamw_small(tot, pk_w, pk_m, pk_v, shapes):
    def body(tot_ref, w_ref, m_ref, v_ref, *outs):
        x, y, _ = _place()
        chip = 2 * x + y
        taps = []
        for j in range(3):
            mine = tot_ref[4:5, j * CONV_W:j * CONV_W + HD]
            for s in range(1, 4):
                mine = jnp.where(chip == s, tot_ref[4:5, j * CONV_W + s * HD:j * CONV_W + (s + 1) * HD], mine)
            taps.append(mine)
        row4 = jnp.concatenate(taps + [jnp.zeros((1, 3 * CONV_W - 3 * HD), f32), tot_ref[4:5, 3 * CONV_W:]], axis=1)
        tot_v = tot_ref[...]
        row = lax.broadcasted_iota(jnp.int32, tot_v.shape, 0)
        g = jnp.where(row == 4, jnp.broadcast_to(row4, tot_v.shape), tot_v)
        d, mm, vv = _adamw_math(w_ref[...], g, m_ref[...], v_ref[...])
        for i, name in enumerate(SMALL):
            for k, val in enumerate((g, d, mm, vv)):
                if name == "conv_w":
                    outs[4 * i + k][...] = jnp.concatenate([val[4:5, j * HD:(j + 1) * HD] for j in range(3)], axis=0)[None]
                else:
                    r, c0, w = SMALL_AT[name]
                    outs[4 * i + k][...] = val[r:r + 1, c0:c0 + w]

    out_shape = [SDS(shapes[k], f32) for k in SMALL for _ in range(4)]
    res = _run("adamw_small", body, (), [tot, pk_w, pk_m, pk_v], [VM] * 4, out_shape, [VM] * len(out_shape))
    return {k: res[4 * i:4 * i + 4] for i, k in enumerate(SMALL)}


def prep_weights(name, shards, exchange=None):
    n = len(shards)

    def body(*refs):
        for e in range(n):
            refs[n + e][...] = _c(refs[e][...])

    return _run(name, body, (), shards, [VM] * n, [SDS(a.shape, MXU) for a in shards], [VM] * n, vmem_mib=48, exchange=exchange)


def mem_kv_fwd(mem2d, pk, wmkv):
    M, D = mem2d.shape

    def body(m_ref, pk_ref, w_ref, mn_ref, kv_ref, km_ref, vm_ref):
        m = m_ref[...]
        mn = _c(m * _rstd(m) * _small(pk_ref, "norm_mem"))
        mn_ref[...] = mn
        kv = _nn(mn, w_ref[...])
        kv_ref[...] = kv
        kk = kv[:, :MEM_W]
        km_ref[...] = _c(kk * _heads_rstd(kk) * _lanes(_small(pk_ref, "mem_k_norm"), MEM_W))
        vm_ref[...] = _c(kv[:, MEM_W:])

    return _run("mem_kv_fwd", body, (), [mem2d, pk, wmkv], [VM] * 3,
                [SDS((M, D), MXU), SDS((M, 2 * MEM_W), f32), SDS((M, MEM_W), MXU), SDS((M, MEM_W), MXU)], [VM] * 4)


QKV_W = ATT_W + 2 * KV_W + MEM_W


def in_proj_fwd(x2d, pk, winT, tm, exchange):
    T, D = x2d.shape
    P = winT.shape[0]

    def body(x_ref, pk_ref, w_ref, xn_ref, proj_ref, qkv_ref):
        xv = x_ref[...]
        xn = _c(xv * _rstd(xv) * _small(pk_ref, "norm_mix"))
        xn_ref[...] = xn
        proj = _nt(xn, w_ref[...])
        proj_ref[...] = proj
        q, k = proj[:, :ATT_W], proj[:, ATT_W:ATT_W + KV_W]
        qm = proj[:, P - MEM_W:]
        qkv_ref[...] = jnp.concatenate(
            [_c(q * _heads_rstd(q) * _lanes(_small(pk_ref, "q_norm"), ATT_W)),
             _c(k * _heads_rstd(k) * _lanes(_small(pk_ref, "k_norm"), KV_W)),
             _c(proj[:, ATT_W + KV_W:ATT_W + 2 * KV_W]),
             _c(qm * _heads_rstd(qm) * _lanes(_small(pk_ref, "mem_q_norm"), MEM_W))], axis=1)

    return _run("in_proj_fwd", body, (T // tm,), [x2d, pk, winT],
                [pl.BlockSpec((tm, D), lambda i: (i, 0)), VM, VM],
                [SDS((T, D), MXU), SDS((T, P), f32), SDS((T, QKV_W), MXU)],
                [pl.BlockSpec((tm, D), lambda i: (i, 0)), pl.BlockSpec((tm, P), lambda i: (i, 0)),
                 pl.BlockSpec((tm, QKV_W), lambda i: (i, 0))],
                vmem_mib=40, exchange=exchange)


def _swa_bias_table():
    r = np.arange(GQA * BLK)[:, None]
    k = np.arange(2 * BLK)[None, :]
    dist = (r % BLK) + BLK - k
    band = (dist >= 0) & (dist < BLK)
    tab = np.empty((2, N_KV, GQA * BLK, 2 * BLK), np.float32)
    for later in range(2):
        valid = band & ((k >= BLK) | (later == 1))
        for g in range(N_KV):
            slope = 2.0 ** -(g * GQA + r // BLK + 1.0)
            tab[later, g] = np.where(valid, -slope * dist, NEG)
    return jnp.asarray(tab)


def _sink_column(g, sk_ref):
    hrow = lax.broadcasted_iota(jnp.int32, (GQA * BLK, 1), 0) // BLK
    sink = jnp.zeros((GQA * BLK, 1), f32)
    for hh in range(GQA):
        sink = jnp.where(hrow == hh, sk_ref[g * GQA + hh:g * GQA + hh + 1, 0:1], sink)
    return sink


def _stack_heads(v, g):
    return jnp.concatenate([v[:, (g * GQA + hh) * HD:(g * GQA + hh + 1) * HD] for hh in range(GQA)], axis=0)


def attn_fwd(qkv, sink_rows, BL, S, exchange):
    NB = S // BLK
    T = BL * S

    def body(q_ref, kc_ref, kp_ref, vc_ref, vp_ref, sk_ref, tab_ref, o_ref):
        q = q_ref[...]
        k2 = jnp.concatenate([kp_ref[...], kc_ref[...]], axis=0)
        v2 = jnp.concatenate([vp_ref[...], vc_ref[...]], axis=0)
        ones = jnp.ones((2 * BLK, HD), MXU)
        for g in range(N_KV):
            kn, vh = k2[:, g * HD:(g + 1) * HD], v2[:, g * HD:(g + 1) * HD]
            s = _nt(_stack_heads(q, g), kn) * (HD ** -0.5) + tab_ref[g]
            e, es = _exp_scores(s, _sink_column(g, sk_ref))
            eb = _c(e)
            o = _nn(eb, vh) * (1.0 / (_nn(eb, ones) + es))
            for hh in range(GQA):
                o_ref[:, pl.ds((g * GQA + hh) * HD, HD)] = o[hh * BLK:(hh + 1) * BLK]

    cur = lambda col: (lambda b, j: (b * NB + j, col))
    prev = lambda col: (lambda b, j: (b * NB + jnp.maximum(j - 1, 0), col))
    return _run("attn_fwd", body, (BL, NB), [qkv, qkv, qkv, qkv, qkv, sink_rows, _swa_bias_table()],
                [pl.BlockSpec((BLK, ATT_W), cur(0)),
                 pl.BlockSpec((BLK, KV_W), cur(4)), pl.BlockSpec((BLK, KV_W), prev(4)),
                 pl.BlockSpec((BLK, KV_W), cur(5)), pl.BlockSpec((BLK, KV_W), prev(5)),
                 pl.BlockSpec((8, 128), lambda b, j: (0, 0)),
                 pl.BlockSpec((None, N_KV, GQA * BLK, 2 * BLK), lambda b, j: (jnp.minimum(j, 1), 0, 0, 0))],
                [SDS((T, ATT_W), f32)], [pl.BlockSpec((BLK, ATT_W), cur(0))], exchange=exchange)


def _conv_taps(u, uh):
    row = lax.broadcasted_iota(jnp.int32, u.shape, 0)
    u1 = jnp.where(row == 0, uh[7:8, :], pltpu.roll(u, 1, 0))
    u2 = jnp.where(row == 0, uh[6:7, :], jnp.where(row == 1, uh[7:8, :], pltpu.roll(u, 2, 0)))
    return u1, u2


def _mem_head(qm, km, vm, h):
    qh, kh, vh = (a[:, h * HD:(h + 1) * HD] for a in (qm, km, vm))
    e, _ = _exp_scores(_nt(qh, kh) * (HD ** -0.5))
    return qh, kh, vh, e


def mixer_tail_fwd(x2d, attn_out, proj, qkv, km, vm, conv_w8, pk, wout, S, tm, exchange):
    T, D = x2d.shape
    NM = km.shape[0] // (T // S)

    def body(x_ref, ao_ref, ch_ref, cb_ref, cc_ref, chh_ref, cch_ref, qm_ref, km_ref, vm_ref, cw_ref, pk_ref,
             wout_ref, co_ref, mo_ref, mg_ref, x1_ref, h_ref):
        first = (pl.program_id(0) * tm) % S == 0
        u = cc_ref[...] * ch_ref[...]
        uh = jnp.where(first, 0.0, cch_ref[...] * chh_ref[...])
        u1, u2 = _conv_taps(u, uh)
        conv = cw_ref[0:1, :] * u2 + cw_ref[1:2, :] * u1 + cw_ref[2:3, :] * u + _small(pk_ref, "conv_b")
        conv_out = cb_ref[...] * conv
        co_ref[...] = conv_out
        qm, kmv, vmv = qm_ref[...], km_ref[...], vm_ref[...]
        ones = jnp.ones((NM, HD), MXU)
        for h in range(N_MEMH):
            _, _, vh, e = _mem_head(qm, kmv, vmv, h)
            eb = _c(e)
            mo_ref[:, pl.ds(h * HD, HD)] = _nn(eb, vh) * (1.0 / _nn(eb, ones))
        mem_out = mo_ref[...]
        ao = ao_ref[...]
        merged = _c(jnp.concatenate([ao * _rstd(ao) * _small(pk_ref, "out_norm_attn"),
                                     conv_out * _rstd(conv_out) * _small(pk_ref, "out_norm_conv"),
                                     mem_out * _rstd(mem_out) * _small(pk_ref, "out_norm_mem")], axis=1))
        mg_ref[...] = merged
        x1 = x_ref[...] + _nn(merged, wout_ref[...])
        x1_ref[...] = x1
        h_ref[...] = _c(x1 * _rstd(x1) * _small(pk_ref, "norm_ffn"))

    tile = lambda w, col: pl.BlockSpec((tm, w), lambda i: (i, col))
    halo = lambda col: pl.BlockSpec((8, CONV_W), lambda i: (jnp.maximum(i * (tm // 8) - 1, 0), col))
    seq = pl.BlockSpec((NM, MEM_W), lambda i: ((i * tm) // S, 0))
    small = lambda a: pl.BlockSpec(a.shape, lambda i: (0, 0))
    return _run("mixer_tail_fwd", body, (T // tm,),
                [x2d, attn_out, proj, proj, proj, proj, proj, qkv, km, vm, conv_w8, pk, wout],
                [tile(D, 0), tile(ATT_W, 0), tile(CONV_W, 3), tile(CONV_W, 4), tile(CONV_W, 5), halo(3), halo(5),
                 tile(MEM_W, 3), seq, seq, VM, VM, VM],
                [SDS((T, CONV_W), f32), SDS((T, MEM_W), f32), SDS((T, D), MXU), SDS((T, D), f32), SDS((T, D), MXU)],
                [tile(CONV_W, 0), tile(MEM_W, 0), tile(D, 0), tile(D, 0), tile(D, 0)], vmem_mib=40, exchange=exchange)


def ffn_fwd_bwd(h, x1, tgt, wgT, wuT, wd, pk, tm):
    T, D = x1.shape
    F = wd.shape[0]

    def body(h_ref, x1_ref, t_ref, wg_ref, wu_ref, wd_ref, pk_ref,
             dx1_ref, dx2_ref, act_ref, dg_ref, du_ref, loss_ref, dgf_ref):
        @pl.when(pl.program_id(0) == 0)
        def _():
            loss_ref[...] = jnp.zeros_like(loss_ref)
            dgf_ref[...] = jnp.zeros_like(dgf_ref)

        hv = h_ref[...]
        gate = _nt(hv, wg_ref[...])
        up = _nt(hv, wu_ref[...])
        sg = jax.nn.sigmoid(gate)
        sl = gate * sg
        act = _c(sl * up)
        act_ref[...] = act
        x1v = x1_ref[...]
        diff = (x1v + _nn(act, wd_ref[...])) - t_ref[...]
        loss_ref[...] += 0.5 * jnp.sum(jnp.sum(diff * diff, axis=-1, keepdims=True) / D, axis=0, keepdims=True)
        dx2 = diff / D
        dx2b = _c(dx2)
        dx2_ref[...] = dx2b
        d_act = _nt(dx2b, wd_ref[...])
        d_up = _c(d_act * sl)
        d_gate = _c(d_act * up * (sg * (1.0 + gate * (1.0 - sg))))
        du_ref[...] = d_up
        dg_ref[...] = d_gate
        dh = _nn(d_gate, wg_ref[...]) + _nn(d_up, wu_ref[...])
        dv, dgf = _norm_bwd(dh, x1v, _rstd(x1v), _small(pk_ref, "norm_ffn"))
        dx1_ref[...] = dx2 + dv
        dgf_ref[...] += dgf

    tile = lambda w: pl.BlockSpec((tm, w), lambda i: (i, 0))
    return _run("ffn_fwd_bwd", body, (T // tm,), [h, x1, tgt, wgT, wuT, wd, pk],
                [tile(D), tile(D), tile(D), VM, VM, VM, VM],
                [SDS((T, D), f32), SDS((T, D), MXU), SDS((T, F), MXU), SDS((T, F), MXU), SDS((T, F), MXU),
                 SDS((8, 128), f32), SDS((1, D), f32)],
                [tile(D), tile(D), tile(F), tile(F), tile(F), pl.BlockSpec((8, 128), lambda i: (0, 0)),
                 pl.BlockSpec((1, D), lambda i: (0, 0))], vmem_mib=56)


def matmul_tn(a, b, name, tmo, tk):
    T, M = a.shape
    N = b.shape[1]

    def body(a_ref, b_ref, o_ref):
        @pl.when(pl.program_id(1) == 0)
        def _():
            o_ref[...] = jnp.zeros_like(o_ref)

        o_ref[...] += _tn(a_ref[...], b_ref[...])

    return _run(name, body, (M // tmo, T // tk), [a, b],
                [pl.BlockSpec((tk, tmo), lambda m, k: (k, m)), pl.BlockSpec((tk, N), lambda m, k: (k, 0))],
                [SDS((M, N), f32)], [pl.BlockSpec((tmo, N), lambda m, k: (m, 0))], vmem_mib=48)[0]


def out_proj_bwd(dx1, merged, attn_out, conv_out, mem_out, pk, wout, tm):
    T, D = dx1.shape

    def body(dx1_ref, mg_ref, ao_ref, co_ref, mo_ref, pk_ref, w_ref,
             dao_ref, dco_ref, dmo_ref, dw_ref, dgain_ref):
        @pl.when(pl.program_id(0) == 0)
        def _():
            dw_ref[...] = jnp.zeros_like(dw_ref)
            dgain_ref[...] = jnp.zeros_like(dgain_ref)

        dxb = _c(dx1_ref[...])
        dw_ref[...] += _tn(mg_ref[...], dxb)
        dmg = _nt(dxb, w_ref[...])
        ao, co, mo = ao_ref[...], co_ref[...], mo_ref[...]
        da, ga = _norm_bwd(dmg[:, :ATT_W], ao, _rstd(ao), _small(pk_ref, "out_norm_attn"))
        dc, gc = _norm_bwd(dmg[:, ATT_W:ATT_W + CONV_W], co, _rstd(co), _small(pk_ref, "out_norm_conv"))
        dm, gm = _norm_bwd(dmg[:, ATT_W + CONV_W:], mo, _rstd(mo), _small(pk_ref, "out_norm_mem"))
        dao_ref[...] = da
        dco_ref[...] = dc
        dmo_ref[...] = dm
        dgain_ref[...] += jnp.concatenate([ga, gc, gm], axis=1)

    tile = lambda w: pl.BlockSpec((tm, w), lambda i: (i, 0))
    return _run("out_proj_bwd", body, (T // tm,), [dx1, merged, attn_out, conv_out, mem_out, pk, wout],
                [tile(D), tile(D), tile(ATT_W), tile(CONV_W), tile(MEM_W), VM, VM],
                [SDS((T, ATT_W), f32), SDS((T, CONV_W), f32), SDS((T, MEM_W), f32), SDS((D, D), f32), SDS((1, D), f32)],
                [tile(ATT_W), tile(CONV_W), tile(MEM_W), pl.BlockSpec((D, D), lambda i: (0, 0)),
                 pl.BlockSpec((1, D), lambda i: (0, 0))], vmem_mib=40)


def attn_bwd(qkv, d_attn, attn_out, sink_rows, BL, S, exchange):
    NB = S // BLK
    T = BL * S

    def body(q_ref, kc_ref, kp_ref, vc_ref, vp_ref, do_ref, ao_ref, sk_ref, tab_ref,
             dq_ref, dk_ref, dv_ref, dsk_ref, pend_k, pend_v):
        b, j = pl.program_id(0), pl.program_id(1)

        @pl.when((b == 0) & (j == 0))
        def _():
            dsk_ref[...] = jnp.zeros_like(dsk_ref)

        @pl.when(j == 0)
        def _():
            pend_k[...] = jnp.zeros_like(pend_k)
            pend_v[...] = jnp.zeros_like(pend_v)

        @pl.when(j < NB)
        def _():
            q, do, ao = q_ref[...], do_ref[...], ao_ref[...]
            k2 = jnp.concatenate([kp_ref[...], kc_ref[...]], axis=0)
            v2 = jnp.concatenate([vp_ref[...], vc_ref[...]], axis=0)
            lane = lax.broadcasted_iota(jnp.int32, (8, 128), 1)
            ones_w = jnp.ones((2 * BLK, 2 * BLK), MXU)
            dsk = jnp.zeros((8, 128), f32)
            dks, dvs = [], []
            for g in range(N_KV):
                kn, vh = k2[:, g * HD:(g + 1) * HD], v2[:, g * HD:(g + 1) * HD]
                qs = _stack_heads(q, g)
                s = _nt(qs, kn) * (HD ** -0.5) + tab_ref[g]
                e, es = _exp_scores(s, _sink_column(g, sk_ref))
                eb = _c(e)
                inv_w = 1.0 / (_nn(eb, ones_w) + es)
                inv_n = inv_w[:, :HD]
                dos = _stack_heads(do, g)
                delta = _rowsum_mxu(dos * _stack_heads(ao, g), 2 * BLK)
                dp = _nt(_c(dos), vh)
                ds = _c(e * inv_w * (dp - delta) * (HD ** -0.5))
                t = es * inv_n[:, 0:1] * delta[:, 0:1]
                for hh in range(GQA):
                    dsk = dsk + jnp.where(lane == g * GQA + hh, -jnp.sum(t[hh * BLK:(hh + 1) * BLK]), 0.0)
                dvs.append(_tn(eb, _c(dos * inv_n)))
                dks.append(_tn(ds, qs))
                dqs = _nn(ds, kn)
                for hh in range(GQA):
                    dq_ref[:, pl.ds((g * GQA + hh) * HD, HD)] = dqs[hh * BLK:(hh + 1) * BLK]
            dk2 = jnp.concatenate(dks, axis=1)
            dv2 = jnp.concatenate(dvs, axis=1)
            dk_ref[...] = pend_k[...] + dk2[:BLK]
            dv_ref[...] = pend_v[...] + dv2[:BLK]
            pend_k[...] = dk2[BLK:]
            pend_v[...] = dv2[BLK:]
            dsk_ref[...] += dsk

        @pl.when(j == NB)
        def _():
            dk_ref[...] = pend_k[...]
            dv_ref[...] = pend_v[...]

    cur = lambda col: (lambda b, j: (b * NB + jnp.minimum(j, NB - 1), col))
    prev = lambda col: (lambda b, j: (b * NB + jnp.maximum(j - 1, 0), col))
    small = lambda shape: pl.BlockSpec(shape, lambda b, j: (0, 0))
    return _run("attn_bwd", body, (BL, NB + 1), [qkv, qkv, qkv, qkv, qkv, d_attn, attn_out, sink_rows, _swa_bias_table()],
                [pl.BlockSpec((BLK, ATT_W), cur(0)),
                 pl.BlockSpec((BLK, KV_W), cur(4)), pl.BlockSpec((BLK, KV_W), prev(4)),
                 pl.BlockSpec((BLK, KV_W), cur(5)), pl.BlockSpec((BLK, KV_W), prev(5)),
                 pl.BlockSpec((BLK, ATT_W), cur(0)), pl.BlockSpec((BLK, ATT_W), cur(0)), small((8, 128)),
                 pl.BlockSpec((None, N_KV, GQA * BLK, 2 * BLK), lambda b, j: (jnp.minimum(j, 1), 0, 0, 0))],
                [SDS((T, ATT_W), f32), SDS((T, KV_W), f32), SDS((T, KV_W), f32), SDS((8, 128), f32)],
                [pl.BlockSpec((BLK, ATT_W), cur(0)), pl.BlockSpec((BLK, KV_W), prev(0)),
                 pl.BlockSpec((BLK, KV_W), prev(0)), small((8, 128))],
                scratch=[pltpu.VMEM((BLK, KV_W), f32)] * 2, exchange=exchange)


def mem_conv_bwd(d_mem_out, mem_out, d_conv_out, proj, qkv, km, vm, conv_w8, pk, S, tm, exchange):
    T = d_mem_out.shape[0]
    NM = km.shape[0] // (T // S)

    def body(dmo_ref, mo_ref, dco_ref, ch_ref, cb_ref, cc_ref, chh_ref, cch_ref, qm_ref, km_ref, vm_ref, cw_ref,
             pk_ref, dqm_ref, dkm_ref, dvm_ref, dcb_ref, dcv_ref, dcw_ref, dcbias_ref):
        i = pl.program_id(0)
        first = (i * tm) % S == 0

        @pl.when(i == 0)
        def _():
            dcw_ref[...] = jnp.zeros_like(dcw_ref)
            dcbias_ref[...] = jnp.zeros_like(dcbias_ref)

        @pl.when(first)
        def _():
            dkm_ref[...] = jnp.zeros_like(dkm_ref)
            dvm_ref[...] = jnp.zeros_like(dvm_ref)

        qm, kmv, vmv, dmo, mo = qm_ref[...], km_ref[...], vm_ref[...], dmo_ref[...], mo_ref[...]
        ones_w = jnp.ones((NM, NM), MXU)
        for h in range(N_MEMH):
            qh, kh, vh, e = _mem_head(qm, kmv, vmv, h)
            eb = _c(e)
            doh = dmo[:, h * HD:(h + 1) * HD]
            delta = _rowsum_mxu(doh * mo[:, h * HD:(h + 1) * HD], NM)
            dp = _nt(_c(doh), vh)
            inv_w = 1.0 / _nn(eb, ones_w)
            ds = _c(e * inv_w * (dp - delta) * (HD ** -0.5))
            dvm_ref[:, pl.ds(h * HD, HD)] += _tn(eb, _c(doh * inv_w[:, :HD]))
            dkm_ref[:, pl.ds(h * HD, HD)] += _tn(ds, qh)
            dqm_ref[:, pl.ds(h * HD, HD)] = _nn(ds, kh)

        u = cc_ref[...] * ch_ref[...]
        uh = jnp.where(first, 0.0, cch_ref[...] * chh_ref[...])
        u1, u2 = _conv_taps(u, uh)
        conv = cw_ref[0:1, :] * u2 + cw_ref[1:2, :] * u1 + cw_ref[2:3, :] * u + _small(pk_ref, "conv_b")
        dy = dco_ref[...]
        dcb_ref[...] = dy * conv
        dcv = dy * cb_ref[...]
        dcv_ref[...] = dcv
        dcbias_ref[...] += jnp.sum(dcv, axis=0, keepdims=True)
        dcw_ref[0:1, :] += jnp.sum(dcv * u2, axis=0, keepdims=True)
        dcw_ref[1:2, :] += jnp.sum(dcv * u1, axis=0, keepdims=True)
        dcw_ref[2:3, :] += jnp.sum(dcv * u, axis=0, keepdims=True)

    tile = lambda w, col: pl.BlockSpec((tm, w), lambda i: (i, col))
    halo = lambda col: pl.BlockSpec((8, CONV_W), lambda i: (jnp.maximum(i * (tm // 8) - 1, 0), col))
    seq = pl.BlockSpec((NM, MEM_W), lambda i: ((i * tm) // S, 0))
    const = lambda shape: pl.BlockSpec(shape, lambda i: (0, 0))
    return _run("mem_conv_bwd", body, (T // tm,),
                [d_mem_out, mem_out, d_conv_out, proj, proj, proj, proj, proj, qkv, km, vm, conv_w8, pk],
                [tile(MEM_W, 0), tile(MEM_W, 0), tile(CONV_W, 0), tile(CONV_W, 3), tile(CONV_W, 4), tile(CONV_W, 5),
                 halo(3), halo(5), tile(MEM_W, 3), seq, seq, VM, VM],
                [SDS((T, MEM_W), f32), SDS(km.shape, f32), SDS(km.shape, f32),
                 SDS((T, CONV_W), f32), SDS((T, CONV_W), f32), SDS((8, CONV_W), f32), SDS((1, CONV_W), f32)],
                [tile(MEM_W, 0), seq, seq, tile(CONV_W, 0), tile(CONV_W, 0), const((8, CONV_W)), const((1, CONV_W))],
                exchange=exchange)


def in_proj_bwd(dqn, dkn, dv, dcb, dcv, dqmn, proj, conv_w8, xn, x2d, dx1, pk, winT, S, tm, stages, ws, ms, vs):
    T, D = x2d.shape
    P = winT.shape[0]
    last_blk = T // 8 - 1
    n = len(stages)
    nsteps = T // tm
    tile_w = ws[0].shape[1] // (nsteps // 2)
    turn = [e * 2 // n for e in range(n)]

    def body(dq_ref, dk_ref, dv_ref, dcb_ref, dcv_ref, dcvn_ref, dqm_ref, qa_ref, ka_ref, ch_ref, cc_ref, qma_ref,
             cw_ref, xn_ref, x_ref, dx1_ref, pk_ref, w_ref, *rest):
        st, aw, am, av = (rest[k * n:(k + 1) * n] for k in range(4))
        dx_ref, dw_ref, dg_ref, dqg_ref, dkg_ref, dmqg_ref = rest[4 * n:4 * n + 6]
        aouts = rest[4 * n + 6:]
        i = pl.program_id(0)

        for parity in range(2):
            @pl.when(i % 2 == parity)
            def _(parity=parity):
                for e in range(n):
                    if turn[e] == parity:
                        g = jnp.concatenate([_sum_chips(st[e].at[0]), _sum_chips(st[e].at[1])], axis=0)
                        d, mm, vv = _adamw_math(aw[e][...], g, am[e][...], av[e][...])
                        for k, val in enumerate((g, d, mm, vv)):
                            aouts[4 * e + k][...] = val

        @pl.when(i == 0)
        def _():
            dw_ref[...] = jnp.zeros_like(dw_ref)
            dg_ref[...] = jnp.zeros_like(dg_ref)
            dqg_ref[...] = jnp.zeros_like(dqg_ref)
            dkg_ref[...] = jnp.zeros_like(dkg_ref)
            dmqg_ref[...] = jnp.zeros_like(dmqg_ref)

        dqa, gq = _heads_norm_bwd(dq_ref[...], qa_ref[...], _small(pk_ref, "q_norm"))
        dka, gk = _heads_norm_bwd(dk_ref[...], ka_ref[...], _small(pk_ref, "k_norm"))
        dqma, gmq = _heads_norm_bwd(dqm_ref[...], qma_ref[...], _small(pk_ref, "mem_q_norm"))
        dqg_ref[...] += gq
        dkg_ref[...] += gk
        dmqg_ref[...] += gmq

        last = ((i + 1) * tm) % S == 0
        dcv = dcv_ref[...]
        nxt = jnp.where(last, 0.0, dcvn_ref[...])
        row = lax.broadcasted_iota(jnp.int32, dcv.shape, 0)
        n1 = jnp.where(row == tm - 1, nxt[0:1, :], pltpu.roll(dcv, tm - 1, 0))
        n2 = jnp.where(row == tm - 2, nxt[0:1, :], jnp.where(row == tm - 1, nxt[1:2, :], pltpu.roll(dcv, tm - 2, 0)))
        du = cw_ref[2:3, :] * dcv + cw_ref[1:2, :] * n1 + cw_ref[0:1, :] * n2
        d_proj = jnp.concatenate([_c(dqa), _c(dka), _c(dv_ref[...]), _c(du * cc_ref[...]),
                                  _c(dcb_ref[...]), _c(du * ch_ref[...]), _c(dqma)], axis=1)
        dw_ref[...] += _tn(d_proj, xn_ref[...])
        xv = x_ref[...]
        dv_, dg = _norm_bwd(_nn(d_proj, w_ref[...]), xv, _rstd(xv), _small(pk_ref, "norm_mix"))
        dx_ref[...] = dx1_ref[...] + dv_
        dg_ref[...] += dg

    tile = lambda w, col=0: pl.BlockSpec((tm, w), lambda i: (i, col))
    nhalo = pl.BlockSpec((8, CONV_W), lambda i: (jnp.minimum((i + 1) * (tm // 8), last_blk), 0))
    const = lambda shape: pl.BlockSpec(shape, lambda i: (0, 0))
    st_specs = [pl.BlockSpec((2, 4, s.shape[2], tile_w), lambda i: (0, 0, 0, i // 2)) for s in stages]
    w_specs = [pl.BlockSpec((w.shape[0], tile_w), lambda i: (0, i // 2)) for w in ws]
    res = _run("in_proj_bwd", body, (nsteps,),
               [dqn, dkn, dv, dcb, dcv, dcv, dqmn, proj, proj, proj, proj, proj, conv_w8, xn, x2d, dx1, pk, winT]
               + list(stages) + list(ws) + list(ms) + list(vs),
               [tile(ATT_W), tile(KV_W), tile(KV_W), tile(CONV_W), tile(CONV_W), nhalo, tile(MEM_W),
                tile(ATT_W, 0), tile(KV_W, 4), tile(CONV_W, 3), tile(CONV_W, 5), tile(MEM_W, 6), VM,
                tile(D), tile(D), tile(D), VM, VM] + st_specs + w_specs * 3,
               [SDS((T, D), f32), SDS((P, D), f32), SDS((1, D), f32), SDS((1, HD), f32), SDS((1, HD), f32),
                SDS((1, HD), f32)] + [SDS(w.shape, f32) for w in ws for _ in range(4)],
               [tile(D), pl.BlockSpec((P, D), lambda i: (0, 0)), const((1, D)), const((1, HD)), const((1, HD)),
                const((1, HD))] + [s for s in w_specs for _ in range(4)],
               vmem_mib=56)
    return res[:6], [res[6 + 4 * e:10 + 4 * e] for e in range(n)]


def mem_kv_bwd(dkm, dvm, kv, memn, mem2d, pk, wmkv):
    def body(dkm_ref, dvm_ref, kv_ref, mn_ref, m_ref, pk_ref, w_ref, dw_ref, dg_ref, dkg_ref):
        dkk, dkg = _heads_norm_bwd(dkm_ref[...], kv_ref[:, :MEM_W], _small(pk_ref, "mem_k_norm"))
        dkg_ref[...] = dkg
        dkv = _c(jnp.concatenate([dkk, dvm_ref[...]], axis=1))
        dw_ref[...] = _tn(mn_ref[...], dkv)
        mv = m_ref[...]
        dg_ref[...] = jnp.sum(_nt(dkv, w_ref[...]) * mv * _rstd(mv), axis=0, keepdims=True)

    return _run("mem_kv_bwd", body, (), [dkm, dvm, kv, memn, mem2d, pk, wmkv], [VM] * 7,
                [SDS(wmkv.shape, f32), SDS((1, mem2d.shape[1]), f32), SDS((1, HD), f32)], [VM] * 3, vmem_mib=40)


def _halves_view(g):
    return g.reshape(4, 2, g.shape[0] // 8, g.shape[1])


def kernel(x, mem, norm_mix, w_in, q_norm, k_norm, attn_sinks, conv_w, conv_b, norm_mem, w_mem_kv, mem_q_norm, mem_k_norm, out_norm_attn, out_norm_conv, out_norm_mem, w_out, norm_ffn, w_gate, w_up, w_down, loss_target, m_norm_mix, m_w_in, m_q_norm, m_k_norm, m_attn_sinks, m_conv_w, m_conv_b, m_norm_mem, m_w_mem_kv, m_mem_q_norm, m_mem_k_norm, m_out_norm_attn, m_out_norm_conv, m_out_norm_mem, m_w_out, m_norm_ffn, m_w_gate, m_w_up, m_w_down, v_norm_mix, v_w_in, v_q_norm, v_k_norm, v_attn_sinks, v_conv_w, v_conv_b, v_norm_mem, v_w_mem_kv, v_mem_q_norm, v_mem_k_norm, v_out_norm_attn, v_out_norm_conv, v_out_norm_mem, v_w_out, v_norm_ffn, v_w_gate, v_w_up, v_w_down):
    BL, S, D = x.shape
    T = BL * S
    TM = 256
    TM_BIG = min(512, S)
    _, _, ci = _place()
    cidx = ci.reshape(1).astype(jnp.int32)
    w_small = dict(norm_mix=norm_mix, norm_mem=norm_mem, norm_ffn=norm_ffn, out_norm_attn=out_norm_attn,
                   out_norm_conv=out_norm_conv, out_norm_mem=out_norm_mem, conv_w=conv_w, conv_b=conv_b, q_norm=q_norm,
                   k_norm=k_norm, mem_q_norm=mem_q_norm, mem_k_norm=mem_k_norm, attn_sinks=attn_sinks)
    m_small = dict(norm_mix=m_norm_mix, norm_mem=m_norm_mem, norm_ffn=m_norm_ffn, out_norm_attn=m_out_norm_attn,
                   out_norm_conv=m_out_norm_conv, out_norm_mem=m_out_norm_mem, conv_w=m_conv_w, conv_b=m_conv_b,
                   q_norm=m_q_norm, k_norm=m_k_norm, mem_q_norm=m_mem_q_norm, mem_k_norm=m_mem_k_norm,
                   attn_sinks=m_attn_sinks)
    v_small = dict(norm_mix=v_norm_mix, norm_mem=v_norm_mem, norm_ffn=v_norm_ffn, out_norm_attn=v_out_norm_attn,
                   out_norm_conv=v_out_norm_conv, out_norm_mem=v_out_norm_mem, conv_w=v_conv_w, conv_b=v_conv_b,
                   q_norm=v_q_norm, k_norm=v_k_norm, mem_q_norm=v_mem_q_norm, mem_k_norm=v_mem_k_norm,
                   attn_sinks=v_attn_sinks)
    pk = _pack_small(w_small)

    rowblocks = lambda a, b, c, d, e, f: [a[0].T, b[0].T, c[0].T, d[0], e[0], f[0]]
    w_rb = rowblocks(w_in, w_gate, w_up, w_down, w_out, w_mem_kv)
    m_rb = rowblocks(m_w_in, m_w_gate, m_w_up, m_w_down, m_w_out, m_w_mem_kv)
    v_rb = rowblocks(v_w_in, v_w_gate, v_w_up, v_w_down, v_w_out, v_w_mem_kv)
    (winT_s,) = prep_weights("prep_w_in", w_rb[:1])
    cw_pad = jnp.zeros((8, 128), f32).at[:3, :HD].set(conv_w[0])
    (wgT_s, wuT_s, wd_s, wout_s, wmkv_s), (winT, cw_all) = prep_weights(
        "gather_w_in", w_rb[1:], gather_exchange([winT_s, cw_pad], [True, False]))
    conv_w_full = jnp.transpose(cw_all.reshape(4, 8, 128)[:, :3, :HD], (1, 0, 2)).reshape(3, CONV_W)
    conv_w8 = jnp.zeros((8, CONV_W), f32).at[:3].set(conv_w_full)
    sink_rows = jnp.broadcast_to(attn_sinks.reshape(N_Q, 1), (N_Q, 128))

    x2d = x.reshape(T, D)
    mem2d = mem.reshape(-1, D)
    (xn, proj, qkv), (wgT,) = in_proj_fwd(x2d, pk, winT, TM_BIG, gather_exchange([wgT_s], [True]))
    (attn_out,), (wuT, wout, wmkv) = attn_fwd(qkv, sink_rows, BL, S,
                                              gather_exchange([wuT_s, wout_s, wmkv_s], [True, True, True], relay_early=3))
    memn, kv, km, vm = mem_kv_fwd(mem2d, pk, wmkv)
    (conv_out, mem_out, merged, x1, h), (wd,) = mixer_tail_fwd(
        x2d, attn_out, proj, qkv, km, vm, conv_w8, pk, wout, S, TM_BIG, gather_exchange([wd_s], [True]))

    dx1, dx2b, act, d_gate, d_up, loss8, d_norm_ffn = ffn_fwd_bwd(h, x1, loss_target.reshape(T, D), wgT, wuT, wd, pk, TM)
    F = wd.shape[0]
    g_wd = matmul_tn(act, dx2b, "dw_down", F // 2, min(T, 1024))
    g_wgT = matmul_tn(d_gate, h, "dw_gate", F // 2, min(T, 1024))
    g_wuT = matmul_tn(d_up, h, "dw_up", F // 2, min(T, 1024))

    d_attn, d_conv_out, d_mem_out, g_wout, d_gains = out_proj_bwd(dx1, merged, attn_out, conv_out, mem_out, pk, wout, TM_BIG)
    late = [_halves_view(g) for g in (g_wgT, g_wuT, g_wd, g_wout)]
    (dqmn, dkm, dvm, dcb, dcv, d_cw8, d_cbias), late_sib = mem_conv_bwd(
        d_mem_out, mem_out, d_conv_out, proj, qkv, km, vm, conv_w8, pk, S, TM_BIG, halves_exchange(late))
    late_part = add_halves(cidx, late, late_sib, "grad_add_halves_ffn")
    (dqn, dkn, dv, d_sink8), late_stage = attn_bwd(qkv, d_attn, attn_out, sink_rows, BL, S, scatter_exchange(late_part))
    (g_x, g_winT, d_norm_mix, d_qg, d_kg, d_mqg), late_res = in_proj_bwd(
        dqn, dkn, dv, dcb, dcv, dqmn, proj, conv_w8, xn, x2d, dx1, pk, winT, S, TM,
        late_stage, w_rb[1:5], m_rb[1:5], v_rb[1:5])
    g_wmkv, d_norm_mem, d_mkg = mem_kv_bwd(dkm, dvm, kv, memn, mem2d, pk, wmkv)

    tot, tail_stage = tail_reduce(d_norm_mix, d_norm_mem, d_norm_ffn, d_gains, d_cw8, d_cbias, d_qg, d_kg, d_mqg, d_mkg,
                                  d_sink8, loss8, [_halves_view(g) for g in (g_winT, g_wmkv)])
    loss = tot[5, 384]
    tail_res, _ = adamw_big("adamw_tail", tail_stage, [w_rb[0], w_rb[5]], [m_rb[0], m_rb[5]], [v_rb[0], v_rb[5]], 4)
    res = {"w_in": [a.T[None] for a in tail_res[0]], "w_gate": [a.T[None] for a in late_res[0]],
           "w_up": [a.T[None] for a in late_res[1]], "w_down": [a[None] for a in late_res[2]],
           "w_out": [a[None] for a in late_res[3]], "w_mem_kv": [a[None] for a in tail_res[1]]}
    res.update(adamw_small(tot, pk, _pack_small(m_small), _pack_small(v_small), {k: w_small[k].shape for k in SMALL}))

    order = ["norm_mix", "w_in", "q_norm", "k_norm", "attn_sinks", "conv_w", "conv_b", "norm_mem", "w_mem_kv",
             "mem_q_norm", "mem_k_norm", "out_norm_attn", "out_norm_conv", "out_norm_mem", "w_out", "norm_ffn",
             "w_gate", "w_up", "w_down"]
    return (loss, g_x.reshape(BL, S, D), *[res[n][0] for n in order], *[res[n][1] for n in order],
            *[res[n][2] for n in order], *[res[n][3] for n in order])
```

```python
import collections
import functools

import jax
import jax.numpy as jnp
import numpy as np
from jax import lax
from jax.experimental import pallas as pl
from jax.experimental.pallas import tpu as pltpu

f32 = jnp.float32
MXU = jnp.bfloat16
WIRE = jnp.bfloat16
EPS = 1e-6
NEG = -1e30
HD = 64
BLK = 128
N_Q, N_KV, N_MEMH = 8, 2, 4
GQA = N_Q // N_KV
ATT_W, KV_W, CONV_W, MEM_W = 512, 128, 256, 256
VMEM_MIB = 1024 * 1024
ADAM_LR, ADAM_B1, ADAM_B2, ADAM_EPS, ADAM_WD, ADAM_STEP = 0.001, 0.9, 0.999, 1e-08, 0.01, 10

MESH = pl.DeviceIdType.MESH
VM = pl.BlockSpec(memory_space=pltpu.VMEM)
ANY = pl.BlockSpec(memory_space=pl.ANY)
SDS = jax.ShapeDtypeStruct
DMA = pltpu.SemaphoreType.DMA


def _c(v):
    return v.astype(MXU)


def _nn(a, b):
    return lax.dot_general(a, b, (((1,), (0,)), ((), ())), preferred_element_type=f32)


def _nt(a, b):
    return lax.dot_general(a, b, (((1,), (1,)), ((), ())), preferred_element_type=f32)


def _tn(a, b):
    return lax.dot_general(a, b, (((0,), (0,)), ((), ())), preferred_element_type=f32)


def _rstd(v):
    return lax.rsqrt(jnp.mean(v * v, axis=-1, keepdims=True) + EPS)


def _norm_bwd(dy, v, r, g):
    dyg = dy * g
    dv = r * dyg - v * (r * r * r) * jnp.mean(dyg * v, axis=-1, keepdims=True)
    return dv, jnp.sum(dy * v * r, axis=0, keepdims=True)


def _split3(v):
    hi = _c(v)
    r1 = v - hi.astype(f32)
    mid = _c(r1)
    return hi, mid, _c(r1 - mid.astype(f32))


def _rowsum_mxu(v, width):
    ones = jnp.ones((v.shape[1], width), MXU)
    return sum(_nn(a, ones) for a in _split3(v))


def _seg_sums(v):
    r = lax.broadcasted_iota(jnp.int32, (2 * HD, 2 * HD), 0) // HD
    c = lax.broadcasted_iota(jnp.int32, (2 * HD, 2 * HD), 1) // HD
    bd = (r == c).astype(MXU)
    outs = []
    for b in range(v.shape[1] // (2 * HD)):
        outs.append(sum(_nn(a, bd) for a in _split3(v[:, b * 2 * HD:(b + 1) * 2 * HD])))
    return outs[0] if len(outs) == 1 else jnp.concatenate(outs, axis=1)


def _lanes(g, width):
    return jnp.concatenate([g] * (width // HD), axis=1)


def _heads_rstd(v):
    return lax.rsqrt(_seg_sums(v * v) * (1.0 / HD) + EPS)


def _heads_norm_bwd(dy, v, g):
    r = _heads_rstd(v)
    gl = _lanes(g, v.shape[1])
    dyg = dy * gl
    dv = r * dyg - v * (r * r * r) * (_seg_sums(dyg * v) * (1.0 / HD))
    dgl = jnp.sum(dy * v * r, axis=0, keepdims=True)
    return dv, sum(dgl[:, s * HD:(s + 1) * HD] for s in range(v.shape[1] // HD))


def _exp_scores(s, extra=None):
    m = jnp.max(s, axis=-1, keepdims=True)
    if extra is None:
        return jnp.exp(s - m), None
    m = jnp.maximum(m, extra)
    return jnp.exp(s - m), jnp.exp(extra - m)


def _place():
    return lax.axis_index("x"), lax.axis_index("y"), lax.axis_index("c")


SMALL_AT = {"norm_mix": (0, 0, 1024), "norm_mem": (1, 0, 1024), "norm_ffn": (2, 0, 1024),
            "out_norm_attn": (3, 0, ATT_W), "out_norm_conv": (3, ATT_W, CONV_W), "out_norm_mem": (3, ATT_W + CONV_W, MEM_W),
            "conv_b": (4, 3 * CONV_W, CONV_W), "q_norm": (5, 0, HD), "k_norm": (5, HD, HD), "mem_q_norm": (5, 2 * HD, HD),
            "mem_k_norm": (5, 3 * HD, HD), "attn_sinks": (5, 256, N_Q)}
SMALL = ("norm_mix", "norm_mem", "norm_ffn", "out_norm_attn", "out_norm_conv", "out_norm_mem", "conv_w", "conv_b",
         "q_norm", "k_norm", "mem_q_norm", "mem_k_norm", "attn_sinks")


def _small(pk_ref, name):
    r, c0, w = SMALL_AT[name]
    return pk_ref[r:r + 1, c0:c0 + w]


def _pack_small(d):
    z = lambda n: jnp.zeros((1, n), f32)
    row3 = jnp.concatenate([d["out_norm_attn"], d["out_norm_conv"], d["out_norm_mem"]], axis=1)
    row4 = jnp.concatenate([d["conv_w"].reshape(1, 3 * HD), z(3 * CONV_W - 3 * HD), d["conv_b"]], axis=1)
    row5 = jnp.concatenate([d["q_norm"], d["k_norm"], d["mem_q_norm"], d["mem_k_norm"], d["attn_sinks"],
                            z(1024 - 4 * HD - N_Q)], axis=1)
    return jnp.concatenate([d["norm_mix"], d["norm_mem"], d["norm_ffn"], row3, row4, row5, z(1024), z(1024)], axis=0)


def _other_chips(x, y):
    return [(1 - x, y), (x, 1 - y), (1 - x, 1 - y)]


Exchange = collections.namedtuple("Exchange", "ins outs sems start finish relay relay_steps_before_end", defaults=(None, 0))


def _run(name, body, grid, ins, in_specs, out_shape, out_specs, scratch=(), vmem_mib=32, exchange=None):
    ins, in_specs, out_shape, out_specs, scratch = list(ins), list(in_specs), list(out_shape), list(out_specs), list(scratch)
    ni, no, ns = len(ins), len(out_shape), len(scratch)
    ex = exchange
    if ex is not None:
        nxi, nxo = len(ex.ins), len(ex.outs)

    def call_body(*refs):
        if ex is None:
            body(*refs)
            return
        a, xa = refs[:ni], refs[ni:ni + nxi]
        o, xo = refs[ni + nxi:ni + nxi + no], refs[ni + nxi + no:ni + nxi + no + nxo]
        s, xs = refs[ni + nxi + no + nxo:ni + nxi + no + nxo + ns], refs[ni + nxi + no + nxo + ns:]
        if grid:
            first = functools.reduce(jnp.logical_and, [pl.program_id(d) == 0 for d in range(len(grid))])
            last = functools.reduce(jnp.logical_and, [pl.program_id(d) == grid[d] - 1 for d in range(len(grid))])
            pl.when(first)(lambda: ex.start(xa, xo, xs))
            body(*a, *o, *s)
            if ex.relay is not None:
                early = functools.reduce(jnp.logical_and, [pl.program_id(d) == grid[d] - 1 for d in range(len(grid) - 1)],
                                         pl.program_id(len(grid) - 1) == max(grid[-1] - 1 - ex.relay_steps_before_end, 0))
                pl.when(early)(lambda: ex.relay(xa, xo, xs))
            pl.when(last)(lambda: ex.finish(xa, xo, xs))
        else:
            ex.start(xa, xo, xs)
            if body is not None:
                body(*a, *o, *s)
            if ex.relay is not None:
                ex.relay(xa, xo, xs)
            ex.finish(xa, xo, xs)

    if ex is not None:
        ins, in_specs = ins + list(ex.ins), in_specs + [ANY] * nxi
        out_shape, out_specs = out_shape + list(ex.outs), out_specs + [ANY] * nxo
        scratch = scratch + list(ex.sems)
    kw = dict(grid=grid) if grid else {}
    res = pl.pallas_call(
        call_body, name=name, out_shape=out_shape, in_specs=in_specs, out_specs=out_specs, scratch_shapes=scratch,
        compiler_params=pltpu.CompilerParams(dimension_semantics=("arbitrary",) * len(grid) if grid else None,
                                             vmem_limit_bytes=vmem_mib * VMEM_MIB), **kw)(*ins)
    res = list(res)
    return (res[:no], res[no:]) if ex is not None else res


def _remote(src, dst, ssem, rsem, dev):
    return pltpu.make_async_remote_copy(src_ref=src, dst_ref=dst, send_sem=ssem, recv_sem=rsem,
                                        device_id=dev, device_id_type=MESH)


def gather_exchange(shards, split, relay_early=0):
    n = len(shards)

    def rows(ref, e, kk, half=None):
        R = shards[e].shape[0]
        if half is None:
            return ref.at[pl.ds(pl.multiple_of(kk * R, 8), R)]
        return ref.at[pl.ds(pl.multiple_of(kk * R + half * (R // 2), 8), R // 2)]

    def ici(src, dst, sm, e, j, chip_j, x, y, c):
        k = 2 * x + y
        if split[e]:
            s = src[e].at[pl.ds(pl.multiple_of(c * (shards[e].shape[0] // 2), 8), shards[e].shape[0] // 2)]
            return _remote(s, rows(dst[e], e, k, c), sm[0].at[6 * e + j], sm[1].at[6 * e + j], (*chip_j, c))
        return _remote(src[e], rows(dst[e], e, k), sm[0].at[6 * e + j], sm[1].at[6 * e + j], (*chip_j, c))

    def landed(dst, e, chip_j, c):
        kj = 2 * chip_j[0] + chip_j[1]
        return rows(dst[e], e, kj, c) if split[e] else rows(dst[e], e, kj)

    def forward(dst, sm, e, j, chip_j, x, y, c, sender_c):
        kj = 2 * chip_j[0] + chip_j[1]
        r = rows(dst[e], e, kj, sender_c)
        return _remote(r, r, sm[0].at[6 * e + 3 + j], sm[1].at[6 * e + 3 + j], (x, y, 1 - c))

    def local(src, dst, sm, e, x, y):
        return pltpu.make_async_copy(src[e], rows(dst[e], e, 2 * x + y), sm[2].at[e])

    def start(src, dst, sm):
        x, y, c = _place()
        for e in range(n):
            local(src, dst, sm, e, x, y).start()
            for j, chip_j in enumerate(_other_chips(x, y)):
                ici(src, dst, sm, e, j, chip_j, x, y, c).start()

    def relay(src, dst, sm):
        x, y, c = _place()
        for e in range(n):
            for j, chip_j in enumerate(_other_chips(x, y)):
                r = landed(dst, e, chip_j, c)
                _remote(r, r, sm[0].at[6 * e + j], sm[1].at[6 * e + j], (*chip_j, c)).wait_recv()
                if split[e]:
                    forward(dst, sm, e, j, chip_j, x, y, c, c).start()

    def finish(src, dst, sm):
        x, y, c = _place()
        chips = _other_chips(x, y)
        for e in range(n):
            for j, chip_j in enumerate(chips):
                if split[e]:
                    forward(dst, sm, e, j, chip_j, x, y, c, 1 - c).wait_recv()
        for e in range(n):
            for j, chip_j in enumerate(chips):
                ici(src, dst, sm, e, j, chip_j, x, y, c).wait_send()
                if split[e]:
                    forward(dst, sm, e, j, chip_j, x, y, c, c).wait_send()
            local(src, dst, sm, e, x, y).wait()

    outs = [SDS((4 * s.shape[0], s.shape[1]), s.dtype) for s in shards]
    return Exchange(list(shards), outs, [DMA((6 * n,)), DMA((6 * n,)), DMA((n,))], start, finish, relay, relay_early)


def halves_exchange(grads):
    n = len(grads)

    def copy(g, st, sm, e, x, y, c):
        return _remote(g[e].at[:, 1 - c], st[e], sm[0].at[e], sm[1].at[e], (x, y, 1 - c))

    def start(g, st, sm):
        x, y, c = _place()
        for e in range(n):
            copy(g, st, sm, e, x, y, c).start()

    def finish(g, st, sm):
        x, y, c = _place()
        for e in range(n):
            copy(g, st, sm, e, x, y, c).wait()

    outs = [SDS((4,) + a.shape[2:], a.dtype) for a in grads]
    return Exchange(list(grads), outs, [DMA((n,)), DMA((n,))], start, finish)


def scatter_exchange(parts):
    n = len(parts)

    def ici(p, st, sm, e, j, chip_j, x, y, c):
        k, kj = 2 * x + y, 2 * chip_j[0] + chip_j[1]
        return _remote(p[e].at[kj], st[e].at[c, k], sm[0].at[8 * e + j], sm[1].at[8 * e + j], (*chip_j, c))

    def own(p, st, sm, e, x, y, c):
        k = 2 * x + y
        return _remote(p[e].at[k], st[e].at[c, k], sm[0].at[8 * e + 3], sm[1].at[8 * e + 3], (x, y, 1 - c))

    def forward(st, sm, e, j, chip_j, x, y, c, sender_c):
        kj = 2 * chip_j[0] + chip_j[1]
        r = st[e].at[sender_c, kj]
        return _remote(r, r, sm[0].at[8 * e + 4 + j], sm[1].at[8 * e + 4 + j], (x, y, 1 - c))

    def local(p, st, sm, e, x, y, c):
        k = 2 * x + y
        return pltpu.make_async_copy(p[e].at[k], st[e].at[c, k], sm[2].at[e])

    def start(p, st, sm):
        x, y, c = _place()
        for e in range(n):
            local(p, st, sm, e, x, y, c).start()
            own(p, st, sm, e, x, y, c).start()
            for j, chip_j in enumerate(_other_chips(x, y)):
                ici(p, st, sm, e, j, chip_j, x, y, c).start()

    def relay(p, st, sm):
        x, y, c = _place()
        for e in range(n):
            for j, chip_j in enumerate(_other_chips(x, y)):
                kj = 2 * chip_j[0] + chip_j[1]
                r = st[e].at[c, kj]
                _remote(r, r, sm[0].at[8 * e + j], sm[1].at[8 * e + j], (*chip_j, c)).wait_recv()
                forward(st, sm, e, j, chip_j, x, y, c, c).start()

    def finish(p, st, sm):
        x, y, c = _place()
        k = 2 * x + y
        chips = _other_chips(x, y)
        for e in range(n):
            r = st[e].at[1 - c, k]
            _remote(r, r, sm[0].at[8 * e + 3], sm[1].at[8 * e + 3], (x, y, 1 - c)).wait_recv()
            for j, chip_j in enumerate(chips):
                forward(st, sm, e, j, chip_j, x, y, c, 1 - c).wait_recv()
        for e in range(n):
            own(p, st, sm, e, x, y, c).wait_send()
            for j, chip_j in enumerate(chips):
                ici(p, st, sm, e, j, chip_j, x, y, c).wait_send()
                forward(st, sm, e, j, chip_j, x, y, c, c).wait_send()
            local(p, st, sm, e, x, y, c).wait()

    outs = [SDS((2,) + a.shape, a.dtype) for a in parts]
    return Exchange(list(parts), outs, [DMA((8 * n,)), DMA((8 * n,)), DMA((n,))], start, finish, relay)


def tail_reduce(d_norm_mix, d_norm_mem, d_norm_ffn, d_gains, d_cw8, d_cbias, d_qg, d_kg, d_mqg, d_mkg, d_sink8, loss8, tail):
    n = len(tail)
    halves = halves_exchange(tail)
    scatter = scatter_exchange([SDS((4,) + a.shape[2:], WIRE) for a in tail])

    def body(nm_ref, nmem_ref, nf_ref, gn_ref, cw_ref, cb_ref, qg_ref, kg_ref, mqg_ref, mkg_ref, sk_ref, ls_ref, *rest):
        g, o_ref, st = rest[:n], rest[n], rest[n + 1:2 * n + 1]
        buf, ssem, rsem = rest[2 * n + 1:2 * n + 4]
        own, sib, part = (rest[2 * n + 4 + i * n:2 * n + 4 + (i + 1) * n] for i in range(3))
        lsem = rest[5 * n + 4]
        hsem, xsem = rest[5 * n + 5:5 * n + 7], rest[5 * n + 7:]
        x, y, c = _place()
        loads = [pltpu.make_async_copy(g[e].at[:, c], own[e], lsem.at[e]) for e in range(n)]
        for ld in loads:
            ld.start()
        halves.start(g, sib, hsem)
        me = 4 * x + 2 * y + c
        mine = buf.at[me]
        mine[...] = jnp.zeros((8, 1024), f32)
        mine[0:1, :] = nm_ref[...]
        mine[1:2, :] = nmem_ref[...]
        mine[2:3, :] = nf_ref[...]
        mine[3:4, :] = gn_ref[...]
        for j in range(3):
            mine[4:5, pl.ds(j * CONV_W, CONV_W)] = cw_ref[j:j + 1, :]
        mine[4:5, pl.ds(3 * CONV_W, CONV_W)] = cb_ref[...]
        for j, r in enumerate((qg_ref, kg_ref, mqg_ref, mkg_ref)):
            mine[5:6, pl.ds(j * HD, HD)] = r[...]
        mine[5:6, pl.ds(256, 128)] = sk_ref[0:1, :]
        mine[5:6, pl.ds(384, 128)] = ls_ref[0:1, :]

        def peer_of(m):
            return (1 - x if m & 4 else x, 1 - y if m & 2 else y, 1 - c if m & 1 else c)

        for m in range(1, 8):
            _remote(mine, mine, ssem.at[m - 1], rsem.at[m - 1], peer_of(m)).start()
        for ld in loads:
            ld.wait()
        halves.finish(g, sib, hsem)
        for e in range(n):
            part[e][...] = (own[e][...] + sib[e][...]).astype(WIRE)
        scatter.start(part, st, xsem)
        scatter.relay(part, st, xsem)
        scatter.finish(part, st, xsem)
        for m in range(1, 8):
            p = peer_of(m)
            got = buf.at[4 * p[0] + 2 * p[1] + p[2]]
            _remote(got, got, ssem.at[m - 1], rsem.at[m - 1], p).wait_recv()
        for m in range(1, 8):
            _remote(mine, mine, ssem.at[m - 1], rsem.at[m - 1], peer_of(m)).wait_send()
        acc = buf[0]
        for d in range(1, 8):
            acc = acc + buf[d]
        o_ref[...] = acc

    ins = [d_norm_mix, d_norm_mem, d_norm_ffn, d_gains, d_cw8, d_cbias, d_qg, d_kg, d_mqg, d_mkg, d_sink8, loss8]
    half_shape = [(4,) + a.shape[2:] for a in tail]
    scratch = ([pltpu.VMEM((8, 8, 1024), f32), DMA((7,)), DMA((7,))]
               + [pltpu.VMEM(s, f32) for s in half_shape] * 2 + [pltpu.VMEM(s, WIRE) for s in half_shape]
               + [DMA((n,))] + list(halves.sems) + list(scatter.sems))
    res = _run("tail_reduce", body, (), ins + list(tail), [VM] * len(ins) + [ANY] * n,
               [SDS((8, 1024), f32)] + list(scatter.outs), [VM] + [ANY] * n, scratch=scratch, vmem_mib=40)
    return res[0], res[1:]


def add_halves(cidx, grads, stages, name, nch=2):
    n = len(grads)

    def body(c_ref, *refs):
        g, st, o = refs[:n], refs[n:2 * n], refs[2 * n:]
        for e in range(n):
            o[e][...] = (g[e][...] + st[e][...]).astype(WIRE)

    in_specs, out_specs, out_shape = [], [], []
    for a in grads:
        hr, C = a.shape[2], a.shape[3]
        in_specs.append(pl.BlockSpec((None, None, hr // nch, C), lambda s, q, c_ref: (s, c_ref[0], q, 0)))
    for a in stages:
        hr, C = a.shape[1], a.shape[2]
        in_specs.append(pl.BlockSpec((None, hr // nch, C), lambda s, q, c_ref: (s, q, 0)))
        out_specs.append(pl.BlockSpec((None, hr // nch, C), lambda s, q, c_ref: (s, q, 0)))
        out_shape.append(SDS(a.shape, WIRE))
    return pl.pallas_call(
        body, name=name, out_shape=out_shape,
        grid_spec=pltpu.PrefetchScalarGridSpec(num_scalar_prefetch=1, grid=(4, nch), in_specs=in_specs, out_specs=out_specs),
        compiler_params=pltpu.CompilerParams(dimension_semantics=("arbitrary", "arbitrary")),
    )(cidx, *grads, *stages)


def _adamw_math(w, g, m, v):
    m = ADAM_B1 * m + (1.0 - ADAM_B1) * g
    v = ADAM_B2 * v + (1.0 - ADAM_B2) * (g * g)
    m_hat = m / (1.0 - ADAM_B1 ** ADAM_STEP)
    v_hat = v / (1.0 - ADAM_B2 ** ADAM_STEP)
    delta = -ADAM_LR * (m_hat / (jnp.sqrt(v_hat) + ADAM_EPS) + ADAM_WD * w)
    return delta, m, v


def _sum_chips(st):
    return ((st[0].astype(f32) + st[1].astype(f32)) + st[2].astype(f32)) + st[3].astype(f32)


def adamw_big(name, stages, ws, ms, vs, nstep, exchange=None):
    n = len(stages)

    def body(*refs):
        st, w, m, v = refs[:n], refs[n:2 * n], refs[2 * n:3 * n], refs[3 * n:4 * n]
        outs = refs[4 * n:]
        for e in range(n):
            g = jnp.concatenate([_sum_chips(st[e].at[0]), _sum_chips(st[e].at[1])], axis=0)
            d, mm, vv = _adamw_math(w[e][...], g, m[e][...], v[e][...])
            outs[4 * e][...] = g
            outs[4 * e + 1][...] = d
            outs[4 * e + 2][...] = mm
            outs[4 * e + 3][...] = vv

    st_specs, w_specs = [], []
    for e in range(n):
        _, _, hr, C = stages[e].shape
        st_specs.append(pl.BlockSpec((2, 4, hr, C // nstep), lambda i: (0, 0, 0, i)))
        w_specs.append(pl.BlockSpec((2 * hr, C // nstep), lambda i: (0, i)))
    out_specs = [s for s in w_specs for _ in range(4)]
    out_shape = [SDS(w.shape, f32) for w in ws for _ in range(4)]
    res = _run(name, body, (nstep,), list(stages) + list(ws) + list(ms) + list(vs), st_specs + w_specs * 3,
               out_shape, out_specs, vmem_mib=48, exchange=exchange)
    res, sent = res if exchange is not None else (res, None)
    return [res[4 * e:4 * e + 4] for e in range(n)], sent


def adamw_small(tot, pk_w, pk_m, pk_v, shapes):
    def body(tot_ref, w_ref, m_ref, v_ref, *outs):
        x, y, _ = _place()
        chip = 2 * x + y
        taps = []
        for j in range(3):
            mine = tot_ref[4:5, j * CONV_W:j * CONV_W + HD]
            for s in range(1, 4):
                mine = jnp.where(chip == s, tot_ref[4:5, j * CONV_W + s * HD:j * CONV_W + (s + 1) * HD], mine)
            taps.append(mine)
        row4 = jnp.concatenate(taps + [jnp.zeros((1, 3 * CONV_W - 3 * HD), f32), tot_ref[4:5, 3 * CONV_W:]], axis=1)
        tot_v = tot_ref[...]
        row = lax.broadcasted_iota(jnp.int32, tot_v.shape, 0)
        g = jnp.where(row == 4, jnp.broadcast_to(row4, tot_v.shape), tot_v)
        d, mm, vv = _adamw_math(w_ref[...], g, m_ref[...], v_ref[...])
        for i, name in enumerate(SMALL):
            for k, val in enumerate((g, d, mm, vv)):
                if name == "conv_w":
                    outs[4 * i + k][...] = jnp.concatenate([val[4:5, j * HD:(j + 1) * HD] for j in range(3)], axis=0)[None]
                else:
                    r, c0, w = SMALL_AT[name]
                    outs[4 * i + k][...] = val[r:r + 1, c0:c0 + w]

    out_shape = [SDS(shapes[k], f32) for k in SMALL for _ in range(4)]
    res = _run("adamw_small", body, (), [tot, pk_w, pk_m, pk_v], [VM] * 4, out_shape, [VM] * len(out_shape))
    return {k: res[4 * i:4 * i + 4] for i, k in enumerate(SMALL)}


def prep_weights(name, shards, exchange=None):
    n = len(shards)

    def body(*refs):
        for e in range(n):
            refs[n + e][...] = _c(refs[e][...])

    return _run(name, body, (), shards, [VM] * n, [SDS(a.shape, MXU) for a in shards], [VM] * n, vmem_mib=48, exchange=exchange)


def mem_kv_fwd(mem2d, pk, wmkv):
    M, D = mem2d.shape

    def body(m_ref, pk_ref, w_ref, mn_ref, kv_ref, km_ref, vm_ref):
        m = m_ref[...]
        mn = _c(m * _rstd(m) * _small(pk_ref, "norm_mem"))
        mn_ref[...] = mn
        kv = _nn(mn, w_ref[...])
        kv_ref[...] = kv
        kk = kv[:, :MEM_W]
        km_ref[...] = _c(kk * _heads_rstd(kk) * _lanes(_small(pk_ref, "mem_k_norm"), MEM_W))
        vm_ref[...] = _c(kv[:, MEM_W:])

    return _run("mem_kv_fwd", body, (), [mem2d, pk, wmkv], [VM] * 3,
                [SDS((M, D), MXU), SDS((M, 2 * MEM_W), f32), SDS((M, MEM_W), MXU), SDS((M, MEM_W), MXU)], [VM] * 4)


QKV_W = ATT_W + 2 * KV_W + MEM_W


def in_proj_fwd(x2d, pk, winT, tm, exchange):
    T, D = x2d.shape
    P = winT.shape[0]

    def body(x_ref, pk_ref, w_ref, xn_ref, proj_ref, qkv_ref):
        xv = x_ref[...]
        xn = _c(xv * _rstd(xv) * _small(pk_ref, "norm_mix"))
        xn_ref[...] = xn
        proj = _nt(xn, w_ref[...])
        proj_ref[...] = proj
        q, k = proj[:, :ATT_W], proj[:, ATT_W:ATT_W + KV_W]
        qm = proj[:, P - MEM_W:]
        qkv_ref[...] = jnp.concatenate(
            [_c(q * _heads_rstd(q) * _lanes(_small(pk_ref, "q_norm"), ATT_W)),
             _c(k * _heads_rstd(k) * _lanes(_small(pk_ref, "k_norm"), KV_W)),
             _c(proj[:, ATT_W + KV_W:ATT_W + 2 * KV_W]),
             _c(qm * _heads_rstd(qm) * _lanes(_small(pk_ref, "mem_q_norm"), MEM_W))], axis=1)

    return _run("in_proj_fwd", body, (T // tm,), [x2d, pk, winT],
                [pl.BlockSpec((tm, D), lambda i: (i, 0)), VM, VM],
                [SDS((T, D), MXU), SDS((T, P), f32), SDS((T, QKV_W), MXU)],
                [pl.BlockSpec((tm, D), lambda i: (i, 0)), pl.BlockSpec((tm, P), lambda i: (i, 0)),
                 pl.BlockSpec((tm, QKV_W), lambda i: (i, 0))],
                vmem_mib=40, exchange=exchange)


def _swa_bias_table():
    r = np.arange(GQA * BLK)[:, None]
    k = np.arange(2 * BLK)[None, :]
    dist = (r % BLK) + BLK - k
    band = (dist >= 0) & (dist < BLK)
    tab = np.empty((2, N_KV, GQA * BLK, 2 * BLK), np.float32)
    for later in range(2):
        valid = band & ((k >= BLK) | (later == 1))
        for g in range(N_KV):
            slope = 2.0 ** -(g * GQA + r // BLK + 1.0)
            tab[later, g] = np.where(valid, -slope * dist, NEG)
    return jnp.asarray(tab)


def _sink_column(g, sk_ref):
    hrow = lax.broadcasted_iota(jnp.int32, (GQA * BLK, 1), 0) // BLK
    sink = jnp.zeros((GQA * BLK, 1), f32)
    for hh in range(GQA):
        sink = jnp.where(hrow == hh, sk_ref[g * GQA + hh:g * GQA + hh + 1, 0:1], sink)
    return sink


def _stack_heads(v, g):
    return jnp.concatenate([v[:, (g * GQA + hh) * HD:(g * GQA + hh + 1) * HD] for hh in range(GQA)], axis=0)


def attn_fwd(qkv, sink_rows, BL, S, exchange, qb=2):
    NS = S // (qb * BLK)
    T = BL * S

    def body(q_ref, kc_ref, kp_ref, vc_ref, vp_ref, sk_ref, tab_ref, o_ref):
        j = pl.program_id(1)
        kall = jnp.concatenate([kp_ref[...], kc_ref[...]], axis=0)
        vall = jnp.concatenate([vp_ref[...], vc_ref[...]], axis=0)
        ones = jnp.ones((2 * BLK, HD), MXU)
        for b in range(qb):
            q = q_ref[pl.ds(b * BLK, BLK), :]
            k2, v2 = kall[b * BLK:(b + 2) * BLK], vall[b * BLK:(b + 2) * BLK]
            later = jnp.minimum(j, 1) if b == 0 else 1
            for g in range(N_KV):
                kn, vh = k2[:, g * HD:(g + 1) * HD], v2[:, g * HD:(g + 1) * HD]
                s = _nt(_stack_heads(q, g), kn) * (HD ** -0.5) + tab_ref[later, g]
                e, es = _exp_scores(s, _sink_column(g, sk_ref))
                eb = _c(e)
                o = _nn(eb, vh) * (1.0 / (_nn(eb, ones) + es))
                for hh in range(GQA):
                    o_ref[pl.ds(b * BLK, BLK), pl.ds((g * GQA + hh) * HD, HD)] = o[hh * BLK:(hh + 1) * BLK]

    cur = lambda col: (lambda b, j: (b * NS + j, col))
    prev = lambda col: (lambda b, j: (qb * (b * NS + j) - jnp.minimum(j, 1), col))
    return _run("attn_fwd", body, (BL, NS), [qkv, qkv, qkv, qkv, qkv, sink_rows, _swa_bias_table()],
                [pl.BlockSpec((qb * BLK, ATT_W), cur(0)),
                 pl.BlockSpec((qb * BLK, KV_W), cur(4)), pl.BlockSpec((BLK, KV_W), prev(4)),
                 pl.BlockSpec((qb * BLK, KV_W), cur(5)), pl.BlockSpec((BLK, KV_W), prev(5)),
                 pl.BlockSpec((8, 128), lambda b, j: (0, 0)), VM],
                [SDS((T, ATT_W), f32)], [pl.BlockSpec((qb * BLK, ATT_W), cur(0))], exchange=exchange)


def _conv_taps(u, uh):
    row = lax.broadcasted_iota(jnp.int32, u.shape, 0)
    u1 = jnp.where(row == 0, uh[7:8, :], pltpu.roll(u, 1, 0))
    u2 = jnp.where(row == 0, uh[6:7, :], jnp.where(row == 1, uh[7:8, :], pltpu.roll(u, 2, 0)))
    return u1, u2


def _mem_head(qm, km, vm, h):
    qh, kh, vh = (a[:, h * HD:(h + 1) * HD] for a in (qm, km, vm))
    e, _ = _exp_scores(_nt(qh, kh) * (HD ** -0.5))
    return qh, kh, vh, e


def mixer_tail_fwd(x2d, attn_out, proj, qkv, km, vm, conv_w8, pk, wout, S, tm, exchange):
    T, D = x2d.shape
    NM = km.shape[0] // (T // S)

    def body(x_ref, ao_ref, ch_ref, cb_ref, cc_ref, chh_ref, cch_ref, qm_ref, km_ref, vm_ref, cw_ref, pk_ref,
             wout_ref, co_ref, mo_ref, mg_ref, x1_ref, h_ref):
        first = (pl.program_id(0) * tm) % S == 0
        u = cc_ref[...] * ch_ref[...]
        uh = jnp.where(first, 0.0, cch_ref[...] * chh_ref[...])
        u1, u2 = _conv_taps(u, uh)
        conv = cw_ref[0:1, :] * u2 + cw_ref[1:2, :] * u1 + cw_ref[2:3, :] * u + _small(pk_ref, "conv_b")
        conv_out = cb_ref[...] * conv
        co_ref[...] = conv_out
        qm, kmv, vmv = qm_ref[...], km_ref[...], vm_ref[...]
        ones = jnp.ones((NM, HD), MXU)
        for h in range(N_MEMH):
            _, _, vh, e = _mem_head(qm, kmv, vmv, h)
            eb = _c(e)
            mo_ref[:, pl.ds(h * HD, HD)] = _nn(eb, vh) * (1.0 / _nn(eb, ones))
        mem_out = mo_ref[...]
        ao = ao_ref[...]
        merged = _c(jnp.concatenate([ao * _rstd(ao) * _small(pk_ref, "out_norm_attn"),
                                     conv_out * _rstd(conv_out) * _small(pk_ref, "out_norm_conv"),
                                     mem_out * _rstd(mem_out) * _small(pk_ref, "out_norm_mem")], axis=1))
        mg_ref[...] = merged
        x1 = x_ref[...] + _nn(merged, wout_ref[...])
        x1_ref[...] = x1
        h_ref[...] = _c(x1 * _rstd(x1) * _small(pk_ref, "norm_ffn"))

    tile = lambda w, col: pl.BlockSpec((tm, w), lambda i: (i, col))
    halo = lambda col: pl.BlockSpec((8, CONV_W), lambda i: (jnp.maximum(i * (tm // 8) - 1, 0), col))
    seq = pl.BlockSpec((NM, MEM_W), lambda i: ((i * tm) // S, 0))
    small = lambda a: pl.BlockSpec(a.shape, lambda i: (0, 0))
    return _run("mixer_tail_fwd", body, (T // tm,),
                [x2d, attn_out, proj, proj, proj, proj, proj, qkv, km, vm, conv_w8, pk, wout],
                [tile(D, 0), tile(ATT_W, 0), tile(CONV_W, 3), tile(CONV_W, 4), tile(CONV_W, 5), halo(3), halo(5),
                 tile(MEM_W, 3), seq, seq, VM, VM, VM],
                [SDS((T, CONV_W), f32), SDS((T, MEM_W), f32), SDS((T, D), MXU), SDS((T, D), f32), SDS((T, D), MXU)],
                [tile(CONV_W, 0), tile(MEM_W, 0), tile(D, 0), tile(D, 0), tile(D, 0)], vmem_mib=40, exchange=exchange)


def ffn_fwd_bwd(h, x1, tgt, wgT, wuT, wd, pk, tm):
    T, D = x1.shape
    F = wd.shape[0]

    def body(h_ref, x1_ref, t_ref, wg_ref, wu_ref, wd_ref, pk_ref,
             dx1_ref, dx2_ref, act_ref, dg_ref, du_ref, loss_ref, dgf_ref):
        @pl.when(pl.program_id(0) == 0)
        def _():
            loss_ref[...] = jnp.zeros_like(loss_ref)
            dgf_ref[...] = jnp.zeros_like(dgf_ref)

        hv = h_ref[...]
        gate = _nt(hv, wg_ref[...])
        up = _nt(hv, wu_ref[...])
        sg = jax.nn.sigmoid(gate)
        sl = gate * sg
        act = _c(sl * up)
        act_ref[...] = act
        x1v = x1_ref[...]
        diff = (x1v + _nn(act, wd_ref[...])) - t_ref[...]
        loss_ref[...] += 0.5 * jnp.sum(jnp.sum(diff * diff, axis=-1, keepdims=True) / D, axis=0, keepdims=True)
        dx2 = diff / D
        dx2b = _c(dx2)
        dx2_ref[...] = dx2b
        d_act = _nt(dx2b, wd_ref[...])
        d_up = _c(d_act * sl)
        d_gate = _c(d_act * up * (sg * (1.0 + gate * (1.0 - sg))))
        du_ref[...] = d_up
        dg_ref[...] = d_gate
        dh = _nn(d_gate, wg_ref[...]) + _nn(d_up, wu_ref[...])
        dv, dgf = _norm_bwd(dh, x1v, _rstd(x1v), _small(pk_ref, "norm_ffn"))
        dx1_ref[...] = dx2 + dv
        dgf_ref[...] += dgf

    tile = lambda w: pl.BlockSpec((tm, w), lambda i: (i, 0))
    return _run("ffn_fwd_bwd", body, (T // tm,), [h, x1, tgt, wgT, wuT, wd, pk],
                [tile(D), tile(D), tile(D), VM, VM, VM, VM],
                [SDS((T, D), f32), SDS((T, D), MXU), SDS((T, F), MXU), SDS((T, F), MXU), SDS((T, F), MXU),
                 SDS((8, 128), f32), SDS((1, D), f32)],
                [tile(D), tile(D), tile(F), tile(F), tile(F), pl.BlockSpec((8, 128), lambda i: (0, 0)),
                 pl.BlockSpec((1, D), lambda i: (0, 0))], vmem_mib=56)


def matmul_tn(a, b, name, tmo, tk):
    T, M = a.shape
    N = b.shape[1]

    def body(a_ref, b_ref, o_ref):
        @pl.when(pl.program_id(1) == 0)
        def _():
            o_ref[...] = jnp.zeros_like(o_ref)

        o_ref[...] += _tn(a_ref[...], b_ref[...])

    return _run(name, body, (M // tmo, T // tk), [a, b],
                [pl.BlockSpec((tk, tmo), lambda m, k: (k, m)), pl.BlockSpec((tk, N), lambda m, k: (k, 0))],
                [SDS((M, N), f32)], [pl.BlockSpec((tmo, N), lambda m, k: (m, 0))], vmem_mib=48)[0]


def out_proj_bwd(dx1, merged, attn_out, conv_out, mem_out, pk, wout, tm):
    T, D = dx1.shape

    def body(dx1_ref, mg_ref, ao_ref, co_ref, mo_ref, pk_ref, w_ref,
             dao_ref, dco_ref, dmo_ref, dw_ref, dgain_ref):
        @pl.when(pl.program_id(0) == 0)
        def _():
            dw_ref[...] = jnp.zeros_like(dw_ref)
            dgain_ref[...] = jnp.zeros_like(dgain_ref)

        dxb = _c(dx1_ref[...])
        dw_ref[...] += _tn(mg_ref[...], dxb)
        dmg = _nt(dxb, w_ref[...])
        ao, co, mo = ao_ref[...], co_ref[...], mo_ref[...]
        da, ga = _norm_bwd(dmg[:, :ATT_W], ao, _rstd(ao), _small(pk_ref, "out_norm_attn"))
        dc, gc = _norm_bwd(dmg[:, ATT_W:ATT_W + CONV_W], co, _rstd(co), _small(pk_ref, "out_norm_conv"))
        dm, gm = _norm_bwd(dmg[:, ATT_W + CONV_W:], mo, _rstd(mo), _small(pk_ref, "out_norm_mem"))
        dao_ref[...] = da
        dco_ref[...] = dc
        dmo_ref[...] = dm
        dgain_ref[...] += jnp.concatenate([ga, gc, gm], axis=1)

    tile = lambda w: pl.BlockSpec((tm, w), lambda i: (i, 0))
    return _run("out_proj_bwd", body, (T // tm,), [dx1, merged, attn_out, conv_out, mem_out, pk, wout],
                [tile(D), tile(D), tile(ATT_W), tile(CONV_W), tile(MEM_W), VM, VM],
                [SDS((T, ATT_W), f32), SDS((T, CONV_W), f32), SDS((T, MEM_W), f32), SDS((D, D), f32), SDS((1, D), f32)],
                [tile(ATT_W), tile(CONV_W), tile(MEM_W), pl.BlockSpec((D, D), lambda i: (0, 0)),
                 pl.BlockSpec((1, D), lambda i: (0, 0))], vmem_mib=40)


def attn_bwd(qkv, d_attn, attn_out, sink_rows, BL, S, exchange):
    NB = S // BLK
    T = BL * S

    def body(q_ref, kc_ref, kp_ref, vc_ref, vp_ref, do_ref, ao_ref, sk_ref, tab_ref,
             dq_ref, dk_ref, dv_ref, dsk_ref, pend_k, pend_v):
        b, j = pl.program_id(0), pl.program_id(1)

        @pl.when((b == 0) & (j == 0))
        def _():
            dsk_ref[...] = jnp.zeros_like(dsk_ref)

        @pl.when(j == 0)
        def _():
            pend_k[...] = jnp.zeros_like(pend_k)
            pend_v[...] = jnp.zeros_like(pend_v)

        @pl.when(j < NB)
        def _():
            q, do, ao = q_ref[...], do_ref[...], ao_ref[...]
            k2 = jnp.concatenate([kp_ref[...], kc_ref[...]], axis=0)
            v2 = jnp.concatenate([vp_ref[...], vc_ref[...]], axis=0)
            lane = lax.broadcasted_iota(jnp.int32, (8, 128), 1)
            ones_w = jnp.ones((2 * BLK, 2 * BLK), MXU)
            dsk = jnp.zeros((8, 128), f32)
            dks, dvs = [], []
            for g in range(N_KV):
                kn, vh = k2[:, g * HD:(g + 1) * HD], v2[:, g * HD:(g + 1) * HD]
                qs = _stack_heads(q, g)
                s = _nt(qs, kn) * (HD ** -0.5) + tab_ref[g]
                e, es = _exp_scores(s, _sink_column(g, sk_ref))
                eb = _c(e)
                inv_w = 1.0 / (_nn(eb, ones_w) + es)
                inv_n = inv_w[:, :HD]
                dos = _stack_heads(do, g)
                delta = _rowsum_mxu(dos * _stack_heads(ao, g), 2 * BLK)
                dp = _nt(_c(dos), vh)
                ds = _c(e * inv_w * (dp - delta) * (HD ** -0.5))
                t = es * inv_n[:, 0:1] * delta[:, 0:1]
                for hh in range(GQA):
                    dsk = dsk + jnp.where(lane == g * GQA + hh, -jnp.sum(t[hh * BLK:(hh + 1) * BLK]), 0.0)
                dvs.append(_tn(eb, _c(dos * inv_n)))
                dks.append(_tn(ds, qs))
                dqs = _nn(ds, kn)
                for hh in range(GQA):
                    dq_ref[:, pl.ds((g * GQA + hh) * HD, HD)] = dqs[hh * BLK:(hh + 1) * BLK]
            dk2 = jnp.concatenate(dks, axis=1)
            dv2 = jnp.concatenate(dvs, axis=1)
            dk_ref[...] = pend_k[...] + dk2[:BLK]
            dv_ref[...] = pend_v[...] + dv2[:BLK]
            pend_k[...] = dk2[BLK:]
            pend_v[...] = dv2[BLK:]
            dsk_ref[...] += dsk

        @pl.when(j == NB)
        def _():
            dk_ref[...] = pend_k[...]
            dv_ref[...] = pend_v[...]

    cur = lambda col: (lambda b, j: (b * NB + jnp.minimum(j, NB - 1), col))
    prev = lambda col: (lambda b, j: (b * NB + jnp.maximum(j - 1, 0), col))
    small = lambda shape: pl.BlockSpec(shape, lambda b, j: (0, 0))
    return _run("attn_bwd", body, (BL, NB + 1), [qkv, qkv, qkv, qkv, qkv, d_attn, attn_out, sink_rows, _swa_bias_table()],
                [pl.BlockSpec((BLK, ATT_W), cur(0)),
                 pl.BlockSpec((BLK, KV_W), cur(4)), pl.BlockSpec((BLK, KV_W), prev(4)),
                 pl.BlockSpec((BLK, KV_W), cur(5)), pl.BlockSpec((BLK, KV_W), prev(5)),
                 pl.BlockSpec((BLK, ATT_W), cur(0)), pl.BlockSpec((BLK, ATT_W), cur(0)), small((8, 128)),
                 pl.BlockSpec((None, N_KV, GQA * BLK, 2 * BLK), lambda b, j: (jnp.minimum(j, 1), 0, 0, 0))],
                [SDS((T, ATT_W), f32), SDS((T, KV_W), f32), SDS((T, KV_W), f32), SDS((8, 128), f32)],
                [pl.BlockSpec((BLK, ATT_W), cur(0)), pl.BlockSpec((BLK, KV_W), prev(0)),
                 pl.BlockSpec((BLK, KV_W), prev(0)), small((8, 128))],
                scratch=[pltpu.VMEM((BLK, KV_W), f32)] * 2, exchange=exchange)


def mem_conv_bwd(d_mem_out, mem_out, d_conv_out, proj, qkv, km, vm, conv_w8, pk, S, tm, exchange):
    T = d_mem_out.shape[0]
    NM = km.shape[0] // (T // S)

    def body(dmo_ref, mo_ref, dco_ref, ch_ref, cb_ref, cc_ref, chh_ref, cch_ref, qm_ref, km_ref, vm_ref, cw_ref,
             pk_ref, dqm_ref, dkm_ref, dvm_ref, dcb_ref, dcv_ref, dcw_ref, dcbias_ref):
        i = pl.program_id(0)
        first = (i * tm) % S == 0

        @pl.when(i == 0)
        def _():
            dcw_ref[...] = jnp.zeros_like(dcw_ref)
            dcbias_ref[...] = jnp.zeros_like(dcbias_ref)

        @pl.when(first)
        def _():
            dkm_ref[...] = jnp.zeros_like(dkm_ref)
            dvm_ref[...] = jnp.zeros_like(dvm_ref)

        qm, kmv, vmv, dmo, mo = qm_ref[...], km_ref[...], vm_ref[...], dmo_ref[...], mo_ref[...]
        ones_w = jnp.ones((NM, NM), MXU)
        for h in range(N_MEMH):
            qh, kh, vh, e = _mem_head(qm, kmv, vmv, h)
            eb = _c(e)
            doh = dmo[:, h * HD:(h + 1) * HD]
            delta = _rowsum_mxu(doh * mo[:, h * HD:(h + 1) * HD], NM)
            dp = _nt(_c(doh), vh)
            inv_w = 1.0 / _nn(eb, ones_w)
            ds = _c(e * inv_w * (dp - delta) * (HD ** -0.5))
            dvm_ref[:, pl.ds(h * HD, HD)] += _tn(eb, _c(doh * inv_w[:, :HD]))
            dkm_ref[:, pl.ds(h * HD, HD)] += _tn(ds, qh)
            dqm_ref[:, pl.ds(h * HD, HD)] = _nn(ds, kh)

        u = cc_ref[...] * ch_ref[...]
        uh = jnp.where(first, 0.0, cch_ref[...] * chh_ref[...])
        u1, u2 = _conv_taps(u, uh)
        conv = cw_ref[0:1, :] * u2 + cw_ref[1:2, :] * u1 + cw_ref[2:3, :] * u + _small(pk_ref, "conv_b")
        dy = dco_ref[...]
        dcb_ref[...] = dy * conv
        dcv = dy * cb_ref[...]
        dcv_ref[...] = dcv
        dcbias_ref[...] += jnp.sum(dcv, axis=0, keepdims=True)
        dcw_ref[0:1, :] += jnp.sum(dcv * u2, axis=0, keepdims=True)
        dcw_ref[1:2, :] += jnp.sum(dcv * u1, axis=0, keepdims=True)
        dcw_ref[2:3, :] += jnp.sum(dcv * u, axis=0, keepdims=True)

    tile = lambda w, col: pl.BlockSpec((tm, w), lambda i: (i, col))
    halo = lambda col: pl.BlockSpec((8, CONV_W), lambda i: (jnp.maximum(i * (tm // 8) - 1, 0), col))
    seq = pl.BlockSpec((NM, MEM_W), lambda i: ((i * tm) // S, 0))
    const = lambda shape: pl.BlockSpec(shape, lambda i: (0, 0))
    return _run("mem_conv_bwd", body, (T // tm,),
                [d_mem_out, mem_out, d_conv_out, proj, proj, proj, proj, proj, qkv, km, vm, conv_w8, pk],
                [tile(MEM_W, 0), tile(MEM_W, 0), tile(CONV_W, 0), tile(CONV_W, 3), tile(CONV_W, 4), tile(CONV_W, 5),
                 halo(3), halo(5), tile(MEM_W, 3), seq, seq, VM, VM],
                [SDS((T, MEM_W), f32), SDS(km.shape, f32), SDS(km.shape, f32),
                 SDS((T, CONV_W), f32), SDS((T, CONV_W), f32), SDS((8, CONV_W), f32), SDS((1, CONV_W), f32)],
                [tile(MEM_W, 0), seq, seq, tile(CONV_W, 0), tile(CONV_W, 0), const((8, CONV_W)), const((1, CONV_W))],
                vmem_mib=48, exchange=exchange)


def in_proj_bwd(dqn, dkn, dv, dcb, dcv, dqmn, proj, conv_w8, xn, x2d, dx1, pk, winT, S, tm, stages, ws, ms, vs):
    T, D = x2d.shape
    P = winT.shape[0]
    last_blk = T // 8 - 1
    n = len(stages)
    nsteps = T // tm
    tile_w = ws[0].shape[1] // (nsteps // 2)
    turn = [e * 2 // n for e in range(n)]

    def body(dq_ref, dk_ref, dv_ref, dcb_ref, dcv_ref, dcvn_ref, dqm_ref, qa_ref, ka_ref, ch_ref, cc_ref, qma_ref,
             cw_ref, xn_ref, x_ref, dx1_ref, pk_ref, w_ref, *rest):
        st, aw, am, av = (rest[k * n:(k + 1) * n] for k in range(4))
        dx_ref, dw_ref, dg_ref, dqg_ref, dkg_ref, dmqg_ref = rest[4 * n:4 * n + 6]
        aouts = rest[4 * n + 6:]
        i = pl.program_id(0)

        for parity in range(2):
            @pl.when(i % 2 == parity)
            def _(parity=parity):
                for e in range(n):
                    if turn[e] == parity:
                        g = jnp.concatenate([_sum_chips(st[e].at[0]), _sum_chips(st[e].at[1])], axis=0)
                        d, mm, vv = _adamw_math(aw[e][...], g, am[e][...], av[e][...])
                        for k, val in enumerate((g, d, mm, vv)):
                            aouts[4 * e + k][...] = val

        @pl.when(i == 0)
        def _():
            dw_ref[...] = jnp.zeros_like(dw_ref)
            dg_ref[...] = jnp.zeros_like(dg_ref)
            dqg_ref[...] = jnp.zeros_like(dqg_ref)
            dkg_ref[...] = jnp.zeros_like(dkg_ref)
            dmqg_ref[...] = jnp.zeros_like(dmqg_ref)

        dqa, gq = _heads_norm_bwd(dq_ref[...], qa_ref[...], _small(pk_ref, "q_norm"))
        dka, gk = _heads_norm_bwd(dk_ref[...], ka_ref[...], _small(pk_ref, "k_norm"))
        dqma, gmq = _heads_norm_bwd(dqm_ref[...], qma_ref[...], _small(pk_ref, "mem_q_norm"))
        dqg_ref[...] += gq
        dkg_ref[...] += gk
        dmqg_ref[...] += gmq

        last = ((i + 1) * tm) % S == 0
        dcv = dcv_ref[...]
        nxt = jnp.where(last, 0.0, dcvn_ref[...])
        row = lax.broadcasted_iota(jnp.int32, dcv.shape, 0)
        n1 = jnp.where(row == tm - 1, nxt[0:1, :], pltpu.roll(dcv, tm - 1, 0))
        n2 = jnp.where(row == tm - 2, nxt[0:1, :], jnp.where(row == tm - 1, nxt[1:2, :], pltpu.roll(dcv, tm - 2, 0)))
        du = cw_ref[2:3, :] * dcv + cw_ref[1:2, :] * n1 + cw_ref[0:1, :] * n2
        d_proj = jnp.concatenate([_c(dqa), _c(dka), _c(dv_ref[...]), _c(du * cc_ref[...]),
                                  _c(dcb_ref[...]), _c(du * ch_ref[...]), _c(dqma)], axis=1)
        dw_ref[...] += _tn(d_proj, xn_ref[...])
        xv = x_ref[...]
        dv_, dg = _norm_bwd(_nn(d_proj, w_ref[...]), xv, _rstd(xv), _small(pk_ref, "norm_mix"))
        dx_ref[...] = dx1_ref[...] + dv_
        dg_ref[...] += dg

    tile = lambda w, col=0: pl.BlockSpec((tm, w), lambda i: (i, col))
    nhalo = pl.BlockSpec((8, CONV_W), lambda i: (jnp.minimum((i + 1) * (tm // 8), last_blk), 0))
    const = lambda shape: pl.BlockSpec(shape, lambda i: (0, 0))
    st_specs = [pl.BlockSpec((2, 4, s.shape[2], tile_w), lambda i: (0, 0, 0, i // 2)) for s in stages]
    w_specs = [pl.BlockSpec((w.shape[0], tile_w), lambda i: (0, i // 2)) for w in ws]
    res = _run("in_proj_bwd", body, (nsteps,),
               [dqn, dkn, dv, dcb, dcv, dcv, dqmn, proj, proj, proj, proj, proj, conv_w8, xn, x2d, dx1, pk, winT]
               + list(stages) + list(ws) + list(ms) + list(vs),
               [tile(ATT_W), tile(KV_W), tile(KV_W), tile(CONV_W), tile(CONV_W), nhalo, tile(MEM_W),
                tile(ATT_W, 0), tile(KV_W, 4), tile(CONV_W, 3), tile(CONV_W, 5), tile(MEM_W, 6), VM,
                tile(D), tile(D), tile(D), VM, VM] + st_specs + w_specs * 3,
               [SDS((T, D), f32), SDS((P, D), f32), SDS((1, D), f32), SDS((1, HD), f32), SDS((1, HD), f32),
                SDS((1, HD), f32)] + [SDS(w.shape, f32) for w in ws for _ in range(4)],
               [tile(D), pl.BlockSpec((P, D), lambda i: (0, 0)), const((1, D)), const((1, HD)), const((1, HD)),
                const((1, HD))] + [s for s in w_specs for _ in range(4)],
               vmem_mib=56)
    return res[:6], [res[6 + 4 * e:10 + 4 * e] for e in range(n)]


def mem_kv_bwd(dkm, dvm, kv, memn, mem2d, pk, wmkv):
    def body(dkm_ref, dvm_ref, kv_ref, mn_ref, m_ref, pk_ref, w_ref, dw_ref, dg_ref, dkg_ref):
        dkk, dkg = _heads_norm_bwd(dkm_ref[...], kv_ref[:, :MEM_W], _small(pk_ref, "mem_k_norm"))
        dkg_ref[...] = dkg
        dkv = _c(jnp.concatenate([dkk, dvm_ref[...]], axis=1))
        dw_ref[...] = _tn(mn_ref[...], dkv)
        mv = m_ref[...]
        dg_ref[...] = jnp.sum(_nt(dkv, w_ref[...]) * mv * _rstd(mv), axis=0, keepdims=True)

    return _run("mem_kv_bwd", body, (), [dkm, dvm, kv, memn, mem2d, pk, wmkv], [VM] * 7,
                [SDS(wmkv.shape, f32), SDS((1, mem2d.shape[1]), f32), SDS((1, HD), f32)], [VM] * 3, vmem_mib=40)


def _halves_view(g):
    return g.reshape(4, 2, g.shape[0] // 8, g.shape[1])


def kernel(x, mem, norm_mix, w_in, q_norm, k_norm, attn_sinks, conv_w, conv_b, norm_mem, w_mem_kv, mem_q_norm, mem_k_norm, out_norm_attn, out_norm_conv, out_norm_mem, w_out, norm_ffn, w_gate, w_up, w_down, loss_target, m_norm_mix, m_w_in, m_q_norm, m_k_norm, m_attn_sinks, m_conv_w, m_conv_b, m_norm_mem, m_w_mem_kv, m_mem_q_norm, m_mem_k_norm, m_out_norm_attn, m_out_norm_conv, m_out_norm_mem, m_w_out, m_norm_ffn, m_w_gate, m_w_up, m_w_down, v_norm_mix, v_w_in, v_q_norm, v_k_norm, v_attn_sinks, v_conv_w, v_conv_b, v_norm_mem, v_w_mem_kv, v_mem_q_norm, v_mem_k_norm, v_out_norm_attn, v_out_norm_conv, v_out_norm_mem, v_w_out, v_norm_ffn, v_w_gate, v_w_up, v_w_down):
    BL, S, D = x.shape
    T = BL * S
    TM = 256
    TM_BIG = min(512, S)
    _, _, ci = _place()
    cidx = ci.reshape(1).astype(jnp.int32)
    w_small = dict(norm_mix=norm_mix, norm_mem=norm_mem, norm_ffn=norm_ffn, out_norm_attn=out_norm_attn,
                   out_norm_conv=out_norm_conv, out_norm_mem=out_norm_mem, conv_w=conv_w, conv_b=conv_b, q_norm=q_norm,
                   k_norm=k_norm, mem_q_norm=mem_q_norm, mem_k_norm=mem_k_norm, attn_sinks=attn_sinks)
    m_small = dict(norm_mix=m_norm_mix, norm_mem=m_norm_mem, norm_ffn=m_norm_ffn, out_norm_attn=m_out_norm_attn,
                   out_norm_conv=m_out_norm_conv, out_norm_mem=m_out_norm_mem, conv_w=m_conv_w, conv_b=m_conv_b,
                   q_norm=m_q_norm, k_norm=m_k_norm, mem_q_norm=m_mem_q_norm, mem_k_norm=m_mem_k_norm,
                   attn_sinks=m_attn_sinks)
    v_small = dict(norm_mix=v_norm_mix, norm_mem=v_norm_mem, norm_ffn=v_norm_ffn, out_norm_attn=v_out_norm_attn,
                   out_norm_conv=v_out_norm_conv, out_norm_mem=v_out_norm_mem, conv_w=v_conv_w, conv_b=v_conv_b,
                   q_norm=v_q_norm, k_norm=v_k_norm, mem_q_norm=v_mem_q_norm, mem_k_norm=v_mem_k_norm,
                   attn_sinks=v_attn_sinks)
    pk = _pack_small(w_small)

    rowblocks = lambda a, b, c, d, e, f: [a[0].T, b[0].T, c[0].T, d[0], e[0], f[0]]
    w_rb = rowblocks(w_in, w_gate, w_up, w_down, w_out, w_mem_kv)
    m_rb = rowblocks(m_w_in, m_w_gate, m_w_up, m_w_down, m_w_out, m_w_mem_kv)
    v_rb = rowblocks(v_w_in, v_w_gate, v_w_up, v_w_down, v_w_out, v_w_mem_kv)
    (winT_s,) = prep_weights("prep_w_in", w_rb[:1])
    cw_pad = jnp.zeros((8, 128), f32).at[:3, :HD].set(conv_w[0])
    (wgT_s, wuT_s, wd_s, wout_s, wmkv_s), (winT, cw_all) = prep_weights(
        "gather_w_in", w_rb[1:], gather_exchange([winT_s, cw_pad], [True, False]))
    conv_w_full = jnp.transpose(cw_all.reshape(4, 8, 128)[:, :3, :HD], (1, 0, 2)).reshape(3, CONV_W)
    conv_w8 = jnp.zeros((8, CONV_W), f32).at[:3].set(conv_w_full)
    sink_rows = jnp.broadcast_to(attn_sinks.reshape(N_Q, 1), (N_Q, 128))

    x2d = x.reshape(T, D)
    mem2d = mem.reshape(-1, D)
    (xn, proj, qkv), (wgT,) = in_proj_fwd(x2d, pk, winT, TM_BIG, gather_exchange([wgT_s], [True]))
    (attn_out,), (wuT, wout, wmkv) = attn_fwd(qkv, sink_rows, BL, S,
                                              gather_exchange([wuT_s, wout_s, wmkv_s], [True, True, True], relay_early=3))
    memn, kv, km, vm = mem_kv_fwd(mem2d, pk, wmkv)
    (conv_out, mem_out, merged, x1, h), (wd,) = mixer_tail_fwd(
        x2d, attn_out, proj, qkv, km, vm, conv_w8, pk, wout, S, TM_BIG, gather_exchange([wd_s], [True]))

    dx1, dx2b, act, d_gate, d_up, loss8, d_norm_ffn = ffn_fwd_bwd(h, x1, loss_target.reshape(T, D), wgT, wuT, wd, pk, TM)
    F = wd.shape[0]
    g_wd = matmul_tn(act, dx2b, "dw_down", F // 2, min(T, 1024))
    g_wgT = matmul_tn(d_gate, h, "dw_gate", F // 2, min(T, 1024))
    g_wuT = matmul_tn(d_up, h, "dw_up", F // 2, min(T, 1024))

    d_attn, d_conv_out, d_mem_out, g_wout, d_gains = out_proj_bwd(dx1, merged, attn_out, conv_out, mem_out, pk, wout, TM_BIG)
    late = [_halves_view(g) for g in (g_wgT, g_wuT, g_wd, g_wout)]
    (dqmn, dkm, dvm, dcb, dcv, d_cw8, d_cbias), late_sib = mem_conv_bwd(
        d_mem_out, mem_out, d_conv_out, proj, qkv, km, vm, conv_w8, pk, S, min(1024, S), halves_exchange(late))
    late_part = add_halves(cidx, late, late_sib, "grad_add_halves_ffn")
    (dqn, dkn, dv, d_sink8), late_stage = attn_bwd(qkv, d_attn, attn_out, sink_rows, BL, S, scatter_exchange(late_part))
    (g_x, g_winT, d_norm_mix, d_qg, d_kg, d_mqg), late_res = in_proj_bwd(
        dqn, dkn, dv, dcb, dcv, dqmn, proj, conv_w8, xn, x2d, dx1, pk, winT, S, TM,
        late_stage, w_rb[1:5], m_rb[1:5], v_rb[1:5])
    g_wmkv, d_norm_mem, d_mkg = mem_kv_bwd(dkm, dvm, kv, memn, mem2d, pk, wmkv)

    tot, tail_stage = tail_reduce(d_norm_mix, d_norm_mem, d_norm_ffn, d_gains, d_cw8, d_cbias, d_qg, d_kg, d_mqg, d_mkg,
                                  d_sink8, loss8, [_halves_view(g) for g in (g_winT, g_wmkv)])
    loss = tot[5, 384]
    tail_res, _ = adamw_big("adamw_tail", tail_stage, [w_rb[0], w_rb[5]], [m_rb[0], m_rb[5]], [v_rb[0], v_rb[5]], 4)
    res = {"w_in": [a.T[None] for a in tail_res[0]], "w_gate": [a.T[None] for a in late_res[0]],
           "w_up": [a.T[None] for a in late_res[1]], "w_down": [a[None] for a in late_res[2]],
           "w_out": [a[None] for a in late_res[3]], "w_mem_kv": [a[None] for a in tail_res[1]]}
    res.update(adamw_small(tot, pk, _pack_small(m_small), _pack_small(v_small), {k: w_small[k].shape for k in SMALL}))

    order = ["norm_mix", "w_in", "q_norm", "k_norm", "attn_sinks", "conv_w", "conv_b", "norm_mem", "w_mem_kv",
             "mem_q_norm", "mem_k_norm", "out_norm_attn", "out_norm_conv", "out_norm_mem", "w_out", "norm_ffn",
             "w_gate", "w_up", "w_down"]
    return (loss, g_x.reshape(BL, S, D), *[res[n][0] for n in order], *[res[n][1] for n in order],
            *[res[n][2] for n in order], *[res[n][3] for n in order])
```

```python
import collections
import functools

import jax
import jax.numpy as jnp
import numpy as np
from jax import lax
from jax.experimental import pallas as pl
from jax.experimental.pallas import tpu as pltpu

f32 = jnp.float32
MXU = jnp.bfloat16
WIRE = jnp.bfloat16
EPS = 1e-6
NEG = -1e30
HD = 64
BLK = 128
N_Q, N_KV, N_MEMH = 8, 2, 4
GQA = N_Q // N_KV
ATT_W, KV_W, CONV_W, MEM_W = 512, 128, 256, 256
VMEM_MIB = 1024 * 1024
ADAM_LR, ADAM_B1, ADAM_B2, ADAM_EPS, ADAM_WD, ADAM_STEP = 0.001, 0.9, 0.999, 1e-08, 0.01, 10

MESH = pl.DeviceIdType.MESH
VM = pl.BlockSpec(memory_space=pltpu.VMEM)
ANY = pl.BlockSpec(memory_space=pl.ANY)
SDS = jax.ShapeDtypeStruct
DMA = pltpu.SemaphoreType.DMA


def _c(v):
    return v.astype(MXU)


def _nn(a, b):
    return lax.dot_general(a, b, (((1,), (0,)), ((), ())), preferred_element_type=f32)


def _nt(a, b):
    return lax.dot_general(a, b, (((1,), (1,)), ((), ())), preferred_element_type=f32)


def _tn(a, b):
    return lax.dot_general(a, b, (((0,), (0,)), ((), ())), preferred_element_type=f32)


def _rstd(v):
    return lax.rsqrt(jnp.mean(v * v, axis=-1, keepdims=True) + EPS)


def _norm_bwd(dy, v, r, g):
    dyg = dy * g
    dv = r * dyg - v * (r * r * r) * jnp.mean(dyg * v, axis=-1, keepdims=True)
    return dv, jnp.sum(dy * v * r, axis=0, keepdims=True)


def _split3(v):
    hi = _c(v)
    r1 = v - hi.astype(f32)
    mid = _c(r1)
    return hi, mid, _c(r1 - mid.astype(f32))


def _rowsum_mxu(v, width):
    ones = jnp.ones((v.shape[1], width), MXU)
    return sum(_nn(a, ones) for a in _split3(v))


def _seg_sums(v):
    r = lax.broadcasted_iota(jnp.int32, (2 * HD, 2 * HD), 0) // HD
    c = lax.broadcasted_iota(jnp.int32, (2 * HD, 2 * HD), 1) // HD
    bd = (r == c).astype(MXU)
    outs = []
    for b in range(v.shape[1] // (2 * HD)):
        outs.append(sum(_nn(a, bd) for a in _split3(v[:, b * 2 * HD:(b + 1) * 2 * HD])))
    return outs[0] if len(outs) == 1 else jnp.concatenate(outs, axis=1)


def _lanes(g, width):
    return jnp.concatenate([g] * (width // HD), axis=1)


def _heads_rstd(v):
    return lax.rsqrt(_seg_sums(v * v) * (1.0 / HD) + EPS)


def _heads_norm_bwd(dy, v, g):
    r = _heads_rstd(v)
    gl = _lanes(g, v.shape[1])
    dyg = dy * gl
    dv = r * dyg - v * (r * r * r) * (_seg_sums(dyg * v) * (1.0 / HD))
    dgl = jnp.sum(dy * v * r, axis=0, keepdims=True)
    return dv, sum(dgl[:, s * HD:(s + 1) * HD] for s in range(v.shape[1] // HD))


def _exp_scores(s, extra=None):
    m = jnp.max(s, axis=-1, keepdims=True)
    if extra is None:
        return jnp.exp(s - m), None
    m = jnp.maximum(m, extra)
    return jnp.exp(s - m), jnp.exp(extra - m)


def _place():
    return lax.axis_index("x"), lax.axis_index("y"), lax.axis_index("c")


SMALL_AT = {"norm_mix": (0, 0, 1024), "norm_mem": (1, 0, 1024), "norm_ffn": (2, 0, 1024),
            "out_norm_attn": (3, 0, ATT_W), "out_norm_conv": (3, ATT_W, CONV_W), "out_norm_mem": (3, ATT_W + CONV_W, MEM_W),
            "conv_b": (4, 3 * CONV_W, CONV_W), "q_norm": (5, 0, HD), "k_norm": (5, HD, HD), "mem_q_norm": (5, 2 * HD, HD),
            "mem_k_norm": (5, 3 * HD, HD), "attn_sinks": (5, 256, N_Q)}
SMALL = ("norm_mix", "norm_mem", "norm_ffn", "out_norm_attn", "out_norm_conv", "out_norm_mem", "conv_w", "conv_b",
         "q_norm", "k_norm", "mem_q_norm", "mem_k_norm", "attn_sinks")


def _small(pk_ref, name):
    r, c0, w = SMALL_AT[name]
    return pk_ref[r:r + 1, c0:c0 + w]


def _pack_small(d):
    z = lambda n: jnp.zeros((1, n), f32)
    row3 = jnp.concatenate([d["out_norm_attn"], d["out_norm_conv"], d["out_norm_mem"]], axis=1)
    row4 = jnp.concatenate([d["conv_w"].reshape(1, 3 * HD), z(3 * CONV_W - 3 * HD), d["conv_b"]], axis=1)
    row5 = jnp.concatenate([d["q_norm"], d["k_norm"], d["mem_q_norm"], d["mem_k_norm"], d["attn_sinks"],
                            z(1024 - 4 * HD - N_Q)], axis=1)
    return jnp.concatenate([d["norm_mix"], d["norm_mem"], d["norm_ffn"], row3, row4, row5, z(1024), z(1024)], axis=0)


def _other_chips(x, y):
    return [(1 - x, y), (x, 1 - y), (1 - x, 1 - y)]


Exchange = collections.namedtuple("Exchange", "ins outs sems start finish relay relay_steps_before_end aliases",
                                  defaults=(None, 0, {}))


def _together(exchanges):
    def parts(refs, key):
        out, at = [], 0
        for ex in exchanges:
            out.append(refs[at:at + len(getattr(ex, key))])
            at += len(getattr(ex, key))
        return out

    def phase(name):
        def run(xa, xo, xs):
            for ex, a, o, s in zip(exchanges, parts(xa, "ins"), parts(xo, "outs"), parts(xs, "sems")):
                if getattr(ex, name) is not None:
                    getattr(ex, name)(a, o, s)
        return run

    aliases, ai, ao = {}, 0, 0
    for ex in exchanges:
        aliases.update({ai + i: ao + o for i, o in ex.aliases.items()})
        ai, ao = ai + len(ex.ins), ao + len(ex.outs)
    return Exchange([a for ex in exchanges for a in ex.ins], [o for ex in exchanges for o in ex.outs],
                    [s for ex in exchanges for s in ex.sems], phase("start"), phase("finish"), phase("relay"),
                    max(ex.relay_steps_before_end for ex in exchanges), aliases)


def _run(name, body, grid, ins, in_specs, out_shape, out_specs, scratch=(), vmem_mib=32, exchange=None):
    ins, in_specs, out_shape, out_specs, scratch = list(ins), list(in_specs), list(out_shape), list(out_specs), list(scratch)
    ni, no, ns = len(ins), len(out_shape), len(scratch)
    ex = exchange
    if ex is not None:
        nxi, nxo = len(ex.ins), len(ex.outs)

    def call_body(*refs):
        if ex is None:
            body(*refs)
            return
        a, xa = refs[:ni], refs[ni:ni + nxi]
        o, xo = refs[ni + nxi:ni + nxi + no], refs[ni + nxi + no:ni + nxi + no + nxo]
        s, xs = refs[ni + nxi + no + nxo:ni + nxi + no + nxo + ns], refs[ni + nxi + no + nxo + ns:]
        if grid:
            first = functools.reduce(jnp.logical_and, [pl.program_id(d) == 0 for d in range(len(grid))])
            last = functools.reduce(jnp.logical_and, [pl.program_id(d) == grid[d] - 1 for d in range(len(grid))])
            pl.when(first)(lambda: ex.start(xa, xo, xs))
            body(*a, *o, *s)
            if ex.relay is not None:
                early = functools.reduce(jnp.logical_and, [pl.program_id(d) == grid[d] - 1 for d in range(len(grid) - 1)],
                                         pl.program_id(len(grid) - 1) == max(grid[-1] - 1 - ex.relay_steps_before_end, 0))
                pl.when(early)(lambda: ex.relay(xa, xo, xs))
            pl.when(last)(lambda: ex.finish(xa, xo, xs))
        else:
            ex.start(xa, xo, xs)
            if body is not None:
                body(*a, *o, *s)
            if ex.relay is not None:
                ex.relay(xa, xo, xs)
            ex.finish(xa, xo, xs)

    kw = dict(grid=grid) if grid else {}
    if ex is not None:
        if ex.aliases:
            kw["input_output_aliases"] = {ni + i: no + o for i, o in ex.aliases.items()}
        ins, in_specs = ins + list(ex.ins), in_specs + [ANY] * nxi
        out_shape, out_specs = out_shape + list(ex.outs), out_specs + [ANY] * nxo
        scratch = scratch + list(ex.sems)
    res = pl.pallas_call(
        call_body, name=name, out_shape=out_shape, in_specs=in_specs, out_specs=out_specs, scratch_shapes=scratch,
        compiler_params=pltpu.CompilerParams(dimension_semantics=("arbitrary",) * len(grid) if grid else None,
                                             vmem_limit_bytes=vmem_mib * VMEM_MIB), **kw)(*ins)
    res = list(res)
    return (res[:no], res[no:]) if ex is not None else res


def _remote(src, dst, ssem, rsem, dev):
    return pltpu.make_async_remote_copy(src_ref=src, dst_ref=dst, send_sem=ssem, recv_sem=rsem,
                                        device_id=dev, device_id_type=MESH)


def gather_exchange(shards, split, relay_early=0):
    n = len(shards)

    def rows(ref, e, kk, half=None):
        R = shards[e].shape[0]
        if half is None:
            return ref.at[pl.ds(pl.multiple_of(kk * R, 8), R)]
        return ref.at[pl.ds(pl.multiple_of(kk * R + half * (R // 2), 8), R // 2)]

    def ici(src, dst, sm, e, j, chip_j, x, y, c):
        k = 2 * x + y
        if split[e]:
            s = src[e].at[pl.ds(pl.multiple_of(c * (shards[e].shape[0] // 2), 8), shards[e].shape[0] // 2)]
            return _remote(s, rows(dst[e], e, k, c), sm[0].at[6 * e + j], sm[1].at[6 * e + j], (*chip_j, c))
        return _remote(src[e], rows(dst[e], e, k), sm[0].at[6 * e + j], sm[1].at[6 * e + j], (*chip_j, c))

    def landed(dst, e, chip_j, c):
        kj = 2 * chip_j[0] + chip_j[1]
        return rows(dst[e], e, kj, c) if split[e] else rows(dst[e], e, kj)

    def forward(dst, sm, e, j, chip_j, x, y, c, sender_c):
        kj = 2 * chip_j[0] + chip_j[1]
        r = rows(dst[e], e, kj, sender_c)
        return _remote(r, r, sm[0].at[6 * e + 3 + j], sm[1].at[6 * e + 3 + j], (x, y, 1 - c))

    def local(src, dst, sm, e, x, y):
        return pltpu.make_async_copy(src[e], rows(dst[e], e, 2 * x + y), sm[2].at[e])

    def start(src, dst, sm):
        x, y, c = _place()
        for e in range(n):
            local(src, dst, sm, e, x, y).start()
            for j, chip_j in enumerate(_other_chips(x, y)):
                ici(src, dst, sm, e, j, chip_j, x, y, c).start()

    def relay(src, dst, sm):
        x, y, c = _place()
        for e in range(n):
            for j, chip_j in enumerate(_other_chips(x, y)):
                r = landed(dst, e, chip_j, c)
                _remote(r, r, sm[0].at[6 * e + j], sm[1].at[6 * e + j], (*chip_j, c)).wait_recv()
                if split[e]:
                    forward(dst, sm, e, j, chip_j, x, y, c, c).start()

    def finish(src, dst, sm):
        x, y, c = _place()
        chips = _other_chips(x, y)
        for e in range(n):
            for j, chip_j in enumerate(chips):
                if split[e]:
                    forward(dst, sm, e, j, chip_j, x, y, c, 1 - c).wait_recv()
        for e in range(n):
            for j, chip_j in enumerate(chips):
                ici(src, dst, sm, e, j, chip_j, x, y, c).wait_send()
                if split[e]:
                    forward(dst, sm, e, j, chip_j, x, y, c, c).wait_send()
            local(src, dst, sm, e, x, y).wait()

    outs = [SDS((4 * s.shape[0], s.shape[1]), s.dtype) for s in shards]
    return Exchange(list(shards), outs, [DMA((6 * n,)), DMA((6 * n,)), DMA((n,))], start, finish, relay, relay_early)


def _block_rows(ref, R, kk, half, quarter=None):
    hr = R // 2
    if quarter is None:
        return ref.at[pl.ds(pl.multiple_of(kk * R + half * hr, 8), hr)]
    return ref.at[pl.ds(pl.multiple_of(kk * R + half * hr + quarter * (hr // 2), 8), hr // 2)]


def gather_near_exchange(shards, relay_early=0):
    n = len(shards)
    R = [s.shape[0] for s in shards]

    def ici(src, dst, sm, e, j, chip_j, x, y, c):
        half = src[e].at[pl.ds(pl.multiple_of(c * (R[e] // 2), 8), R[e] // 2)]
        return _remote(half, _block_rows(dst[e], R[e], 2 * x + y, c), sm[0].at[4 * e + j], sm[1].at[4 * e + j], (*chip_j, c))

    def forward(dst, sm, e, j, chip_j, x, y, c, sender_c):
        r = _block_rows(dst[e], R[e], 2 * chip_j[0] + chip_j[1], sender_c)
        return _remote(r, r, sm[0].at[4 * e + 2 + j], sm[1].at[4 * e + 2 + j], (x, y, 1 - c))

    def local(src, dst, sm, e, x, y):
        return pltpu.make_async_copy(src[e], dst[e].at[pl.ds(pl.multiple_of((2 * x + y) * R[e], 8), R[e])], sm[2].at[e])

    def start(src, dst, sm):
        x, y, c = _place()
        for e in range(n):
            local(src, dst, sm, e, x, y).start()
            for j, chip_j in enumerate(_other_chips(x, y)[:2]):
                ici(src, dst, sm, e, j, chip_j, x, y, c).start()

    def relay(src, dst, sm):
        x, y, c = _place()
        for e in range(n):
            for j, chip_j in enumerate(_other_chips(x, y)[:2]):
                r = _block_rows(dst[e], R[e], 2 * chip_j[0] + chip_j[1], c)
                _remote(r, r, sm[0].at[4 * e + j], sm[1].at[4 * e + j], (*chip_j, c)).wait_recv()
                forward(dst, sm, e, j, chip_j, x, y, c, c).start()

    def finish(src, dst, sm):
        x, y, c = _place()
        near = _other_chips(x, y)[:2]
        for e in range(n):
            for j, chip_j in enumerate(near):
                forward(dst, sm, e, j, chip_j, x, y, c, 1 - c).wait_recv()
        for e in range(n):
            for j, chip_j in enumerate(near):
                ici(src, dst, sm, e, j, chip_j, x, y, c).wait_send()
                forward(dst, sm, e, j, chip_j, x, y, c, c).wait_send()
            local(src, dst, sm, e, x, y).wait()

    outs = [SDS((4 * s.shape[0], s.shape[1]), s.dtype) for s in shards]
    return Exchange(list(shards), outs, [DMA((4 * n,)), DMA((4 * n,)), DMA((n,))], start, finish, relay, relay_early)


def gather_far_exchange(bufs, relay_early=0):
    n = len(bufs)
    R = [b.shape[0] // 4 for b in bufs]

    def send(src, dst, sm, e, j, x, y, c):
        to, of = _other_chips(x, y)[j], _other_chips(x, y)[1 - j]
        kk = 2 * of[0] + of[1]
        return _remote(_block_rows(src[e], R[e], kk, c, j), _block_rows(dst[e], R[e], kk, c, j),
                       sm[0].at[4 * e + j], sm[1].at[4 * e + j], (*to, c))

    def landed(dst, e, j, x, y, half):
        return _block_rows(dst[e], R[e], 2 * (1 - x) + (1 - y), half, j)

    def forward(dst, sm, e, j, x, y, c, sender_c):
        r = landed(dst, e, j, x, y, sender_c)
        return _remote(r, r, sm[0].at[4 * e + 2 + j], sm[1].at[4 * e + 2 + j], (x, y, 1 - c))

    def start(src, dst, sm):
        x, y, c = _place()
        for e in range(n):
            for j in range(2):
                send(src, dst, sm, e, j, x, y, c).start()

    def relay(src, dst, sm):
        x, y, c = _place()
        for e in range(n):
            for j in range(2):
                r = landed(dst, e, j, x, y, c)
                _remote(r, r, sm[0].at[4 * e + j], sm[1].at[4 * e + j], (*_other_chips(x, y)[j], c)).wait_recv()
                forward(dst, sm, e, j, x, y, c, c).start()

    def finish(src, dst, sm):
        x, y, c = _place()
        for e in range(n):
            for j in range(2):
                forward(dst, sm, e, j, x, y, c, 1 - c).wait_recv()
        for e in range(n):
            for j in range(2):
                send(src, dst, sm, e, j, x, y, c).wait_send()
                forward(dst, sm, e, j, x, y, c, c).wait_send()

    outs = [SDS(b.shape, b.dtype) for b in bufs]
    return Exchange(list(bufs), outs, [DMA((4 * n,)), DMA((4 * n,))], start, finish, relay, relay_early,
                    {i: i for i in range(n)})


def gather_two_legs(shards):
    near = gather_near_exchange(shards)
    far = gather_far_exchange(near.outs)

    def finish(src, dst, sm):
        near.relay(src, dst, sm[:3])
        near.finish(src, dst, sm[:3])
        far.start(dst, dst, sm[3:])
        far.relay(dst, dst, sm[3:])
        far.finish(dst, dst, sm[3:])

    return Exchange(near.ins, near.outs, list(near.sems) + list(far.sems),
                    lambda src, dst, sm: near.start(src, dst, sm[:3]), finish)


def halves_exchange(grads):
    n = len(grads)

    def copy(g, st, sm, e, x, y, c):
        return _remote(g[e].at[:, 1 - c], st[e], sm[0].at[e], sm[1].at[e], (x, y, 1 - c))

    def start(g, st, sm):
        x, y, c = _place()
        for e in range(n):
            copy(g, st, sm, e, x, y, c).start()

    def finish(g, st, sm):
        x, y, c = _place()
        for e in range(n):
            copy(g, st, sm, e, x, y, c).wait()

    outs = [SDS((4,) + a.shape[2:], a.dtype) for a in grads]
    return Exchange(list(grads), outs, [DMA((n,)), DMA((n,))], start, finish)


def scatter_exchange(parts):
    n = len(parts)

    def ici(p, st, sm, e, j, chip_j, x, y, c):
        k, kj = 2 * x + y, 2 * chip_j[0] + chip_j[1]
        return _remote(p[e].at[kj], st[e].at[c, k], sm[0].at[8 * e + j], sm[1].at[8 * e + j], (*chip_j, c))

    def own(p, st, sm, e, x, y, c):
        k = 2 * x + y
        return _remote(p[e].at[k], st[e].at[c, k], sm[0].at[8 * e + 3], sm[1].at[8 * e + 3], (x, y, 1 - c))

    def forward(st, sm, e, j, chip_j, x, y, c, sender_c):
        kj = 2 * chip_j[0] + chip_j[1]
        r = st[e].at[sender_c, kj]
        return _remote(r, r, sm[0].at[8 * e + 4 + j], sm[1].at[8 * e + 4 + j], (x, y, 1 - c))

    def local(p, st, sm, e, x, y, c):
        k = 2 * x + y
        return pltpu.make_async_copy(p[e].at[k], st[e].at[c, k], sm[2].at[e])

    def start(p, st, sm):
        x, y, c = _place()
        for e in range(n):
            local(p, st, sm, e, x, y, c).start()
            own(p, st, sm, e, x, y, c).start()
            for j, chip_j in enumerate(_other_chips(x, y)):
                ici(p, st, sm, e, j, chip_j, x, y, c).start()

    def relay(p, st, sm):
        x, y, c = _place()
        for e in range(n):
            for j, chip_j in enumerate(_other_chips(x, y)):
                kj = 2 * chip_j[0] + chip_j[1]
                r = st[e].at[c, kj]
                _remote(r, r, sm[0].at[8 * e + j], sm[1].at[8 * e + j], (*chip_j, c)).wait_recv()
                forward(st, sm, e, j, chip_j, x, y, c, c).start()

    def finish(p, st, sm):
        x, y, c = _place()
        k = 2 * x + y
        chips = _other_chips(x, y)
        for e in range(n):
            r = st[e].at[1 - c, k]
            _remote(r, r, sm[0].at[8 * e + 3], sm[1].at[8 * e + 3], (x, y, 1 - c)).wait_recv()
            for j, chip_j in enumerate(chips):
                forward(st, sm, e, j, chip_j, x, y, c, 1 - c).wait_recv()
        for e in range(n):
            own(p, st, sm, e, x, y, c).wait_send()
            for j, chip_j in enumerate(chips):
                ici(p, st, sm, e, j, chip_j, x, y, c).wait_send()
                forward(st, sm, e, j, chip_j, x, y, c, c).wait_send()
            local(p, st, sm, e, x, y, c).wait()

    outs = [SDS((2,) + a.shape, a.dtype) for a in parts]
    return Exchange(list(parts), outs, [DMA((8 * n,)), DMA((8 * n,)), DMA((n,))], start, finish, relay)


def tail_reduce(d_norm_mix, d_norm_mem, d_norm_ffn, d_gains, d_cw8, d_cbias, d_qg, d_kg, d_mqg, d_mkg, d_sink8, loss8, tail):
    n = len(tail)
    halves = halves_exchange(tail)
    scatter = scatter_exchange([SDS((4,) + a.shape[2:], WIRE) for a in tail])

    def body(nm_ref, nmem_ref, nf_ref, gn_ref, cw_ref, cb_ref, qg_ref, kg_ref, mqg_ref, mkg_ref, sk_ref, ls_ref, *rest):
        g, o_ref, st = rest[:n], rest[n], rest[n + 1:2 * n + 1]
        buf, ssem, rsem = rest[2 * n + 1:2 * n + 4]
        own, sib, part = (rest[2 * n + 4 + i * n:2 * n + 4 + (i + 1) * n] for i in range(3))
        lsem = rest[5 * n + 4]
        hsem, xsem = rest[5 * n + 5:5 * n + 7], rest[5 * n + 7:]
        x, y, c = _place()
        loads = [pltpu.make_async_copy(g[e].at[:, c], own[e], lsem.at[e]) for e in range(n)]
        for ld in loads:
            ld.start()
        halves.start(g, sib, hsem)
        me = 4 * x + 2 * y + c
        mine = buf.at[me]
        mine[...] = jnp.zeros((8, 1024), f32)
        mine[0:1, :] = nm_ref[...]
        mine[1:2, :] = nmem_ref[...]
        mine[2:3, :] = nf_ref[...]
        mine[3:4, :] = gn_ref[...]
        for j in range(3):
            mine[4:5, pl.ds(j * CONV_W, CONV_W)] = cw_ref[j:j + 1, :]
        mine[4:5, pl.ds(3 * CONV_W, CONV_W)] = cb_ref[...]
        for j, r in enumerate((qg_ref, kg_ref, mqg_ref, mkg_ref)):
            mine[5:6, pl.ds(j * HD, HD)] = r[...]
        mine[5:6, pl.ds(256, 128)] = sk_ref[0:1, :]
        mine[5:6, pl.ds(384, 128)] = ls_ref[0:1, :]

        def peer_of(m):
            return (1 - x if m & 4 else x, 1 - y if m & 2 else y, 1 - c if m & 1 else c)

        for m in range(1, 8):
            _remote(mine, mine, ssem.at[m - 1], rsem.at[m - 1], peer_of(m)).start()
        for ld in loads:
            ld.wait()
        halves.finish(g, sib, hsem)
        for e in range(n):
            part[e][...] = (own[e][...] + sib[e][...]).astype(WIRE)
        scatter.start(part, st, xsem)
        scatter.relay(part, st, xsem)
        scatter.finish(part, st, xsem)
        for m in range(1, 8):
            p = peer_of(m)
            got = buf.at[4 * p[0] + 2 * p[1] + p[2]]
            _remote(got, got, ssem.at[m - 1], rsem.at[m - 1], p).wait_recv()
        for m in range(1, 8):
            _remote(mine, mine, ssem.at[m - 1], rsem.at[m - 1], peer_of(m)).wait_send()
        acc = buf[0]
        for d in range(1, 8):
            acc = acc + buf[d]
        o_ref[...] = acc

    ins = [d_norm_mix, d_norm_mem, d_norm_ffn, d_gains, d_cw8, d_cbias, d_qg, d_kg, d_mqg, d_mkg, d_sink8, loss8]
    half_shape = [(4,) + a.shape[2:] for a in tail]
    scratch = ([pltpu.VMEM((8, 8, 1024), f32), DMA((7,)), DMA((7,))]
               + [pltpu.VMEM(s, f32) for s in half_shape] * 2 + [pltpu.VMEM(s, WIRE) for s in half_shape]
               + [DMA((n,))] + list(halves.sems) + list(scatter.sems))
    res = _run("tail_reduce", body, (), ins + list(tail), [VM] * len(ins) + [ANY] * n,
               [SDS((8, 1024), f32)] + list(scatter.outs), [VM] + [ANY] * n, scratch=scratch, vmem_mib=40)
    return res[0], res[1:]


def add_halves(cidx, grads, stages, name, nch=2):
    n = len(grads)

    def body(c_ref, *refs):
        g, st, o = refs[:n], refs[n:2 * n], refs[2 * n:]
        for e in range(n):
            o[e][...] = (g[e][...] + st[e][...]).astype(WIRE)

    in_specs, out_specs, out_shape = [], [], []
    for a in grads:
        hr, C = a.shape[2], a.shape[3]
        in_specs.append(pl.BlockSpec((None, None, hr // nch, C), lambda s, q, c_ref: (s, c_ref[0], q, 0)))
    for a in stages:
        hr, C = a.shape[1], a.shape[2]
        in_specs.append(pl.BlockSpec((None, hr // nch, C), lambda s, q, c_ref: (s, q, 0)))
        out_specs.append(pl.BlockSpec((None, hr // nch, C), lambda s, q, c_ref: (s, q, 0)))
        out_shape.append(SDS(a.shape, WIRE))
    return pl.pallas_call(
        body, name=name, out_shape=out_shape,
        grid_spec=pltpu.PrefetchScalarGridSpec(num_scalar_prefetch=1, grid=(4, nch), in_specs=in_specs, out_specs=out_specs),
        compiler_params=pltpu.CompilerParams(dimension_semantics=("arbitrary", "arbitrary")),
    )(cidx, *grads, *stages)


def _adamw_math(w, g, m, v):
    m = ADAM_B1 * m + (1.0 - ADAM_B1) * g
    v = ADAM_B2 * v + (1.0 - ADAM_B2) * (g * g)
    m_hat = m / (1.0 - ADAM_B1 ** ADAM_STEP)
    v_hat = v / (1.0 - ADAM_B2 ** ADAM_STEP)
    delta = -ADAM_LR * (m_hat / (jnp.sqrt(v_hat) + ADAM_EPS) + ADAM_WD * w)
    return delta, m, v


def _sum_chips(st):
    return ((st[0].astype(f32) + st[1].astype(f32)) + st[2].astype(f32)) + st[3].astype(f32)


def adamw_big(name, stages, ws, ms, vs, nstep, exchange=None):
    n = len(stages)

    def body(*refs):
        st, w, m, v = refs[:n], refs[n:2 * n], refs[2 * n:3 * n], refs[3 * n:4 * n]
        outs = refs[4 * n:]
        for e in range(n):
            g = jnp.concatenate([_sum_chips(st[e].at[0]), _sum_chips(st[e].at[1])], axis=0)
            d, mm, vv = _adamw_math(w[e][...], g, m[e][...], v[e][...])
            outs[4 * e][...] = g
            outs[4 * e + 1][...] = d
            outs[4 * e + 2][...] = mm
            outs[4 * e + 3][...] = vv

    st_specs, w_specs = [], []
    for e in range(n):
        _, _, hr, C = stages[e].shape
        st_specs.append(pl.BlockSpec((2, 4, hr, C // nstep), lambda i: (0, 0, 0, i)))
        w_specs.append(pl.BlockSpec((2 * hr, C // nstep), lambda i: (0, i)))
    out_specs = [s for s in w_specs for _ in range(4)]
    out_shape = [SDS(w.shape, f32) for w in ws for _ in range(4)]
    res = _run(name, body, (nstep,), list(stages) + list(ws) + list(ms) + list(vs), st_specs + w_specs * 3,
               out_shape, out_specs, vmem_mib=48, exchange=exchange)
    res, sent = res if exchange is not None else (res, None)
    return [res[4 * e:4 * e + 4] for e in range(n)], sent


def adamw_small(tot, pk_w, pk_m, pk_v, shapes):
    def body(tot_ref, w_ref, m_ref, v_ref, *outs):
        x, y, _ = _place()
        chip = 2 * x + y
        taps = []
        for j in range(3):
            mine = tot_ref[4:5, j * CONV_W:j * CONV_W + HD]
            for s in range(1, 4):
                mine = jnp.where(chip == s, tot_ref[4:5, j * CONV_W + s * HD:j * CONV_W + (s + 1) * HD], mine)
            taps.append(mine)
        row4 = jnp.concatenate(taps + [jnp.zeros((1, 3 * CONV_W - 3 * HD), f32), tot_ref[4:5, 3 * CONV_W:]], axis=1)
        tot_v = tot_ref[...]
        row = lax.broadcasted_iota(jnp.int32, tot_v.shape, 0)
        g = jnp.where(row == 4, jnp.broadcast_to(row4, tot_v.shape), tot_v)
        d, mm, vv = _adamw_math(w_ref[...], g, m_ref[...], v_ref[...])
        for i, name in enumerate(SMALL):
            for k, val in enumerate((g, d, mm, vv)):
                if name == "conv_w":
                    outs[4 * i + k][...] = jnp.concatenate([val[4:5, j * HD:(j + 1) * HD] for j in range(3)], axis=0)[None]
                else:
                    r, c0, w = SMALL_AT[name]
                    outs[4 * i + k][...] = val[r:r + 1, c0:c0 + w]

    out_shape = [SDS(shapes[k], f32) for k in SMALL for _ in range(4)]
    res = _run("adamw_small", body, (), [tot, pk_w, pk_m, pk_v], [VM] * 4, out_shape, [VM] * len(out_shape))
    return {k: res[4 * i:4 * i + 4] for i, k in enumerate(SMALL)}


def prep_weights(name, shards, exchange=None):
    n = len(shards)

    def body(*refs):
        for e in range(n):
            refs[n + e][...] = _c(refs[e][...])

    return _run(name, body, (), shards, [VM] * n, [SDS(a.shape, MXU) for a in shards], [VM] * n, vmem_mib=48, exchange=exchange)


def mem_kv_fwd(mem2d, pk, wmkv):
    M, D = mem2d.shape

    def body(m_ref, pk_ref, w_ref, mn_ref, kv_ref, km_ref, vm_ref):
        m = m_ref[...]
        mn = _c(m * _rstd(m) * _small(pk_ref, "norm_mem"))
        mn_ref[...] = mn
        kv = _nn(mn, w_ref[...])
        kv_ref[...] = kv
        kk = kv[:, :MEM_W]
        km_ref[...] = _c(kk * _heads_rstd(kk) * _lanes(_small(pk_ref, "mem_k_norm"), MEM_W))
        vm_ref[...] = _c(kv[:, MEM_W:])

    return _run("mem_kv_fwd", body, (), [mem2d, pk, wmkv], [VM] * 3,
                [SDS((M, D), MXU), SDS((M, 2 * MEM_W), f32), SDS((M, MEM_W), MXU), SDS((M, MEM_W), MXU)], [VM] * 4)


QKV_W = ATT_W + 2 * KV_W + MEM_W


def in_proj_fwd(x2d, pk, winT, tm, exchange):
    T, D = x2d.shape
    P = winT.shape[0]

    def body(x_ref, pk_ref, w_ref, xn_ref, proj_ref, qkv_ref):
        xv = x_ref[...]
        xn = _c(xv * _rstd(xv) * _small(pk_ref, "norm_mix"))
        xn_ref[...] = xn
        proj = _nt(xn, w_ref[...])
        proj_ref[...] = proj
        q, k = proj[:, :ATT_W], proj[:, ATT_W:ATT_W + KV_W]
        qm = proj[:, P - MEM_W:]
        qkv_ref[...] = jnp.concatenate(
            [_c(q * _heads_rstd(q) * _lanes(_small(pk_ref, "q_norm"), ATT_W)),
             _c(k * _heads_rstd(k) * _lanes(_small(pk_ref, "k_norm"), KV_W)),
             _c(proj[:, ATT_W + KV_W:ATT_W + 2 * KV_W]),
             _c(qm * _heads_rstd(qm) * _lanes(_small(pk_ref, "mem_q_norm"), MEM_W))], axis=1)

    return _run("in_proj_fwd", body, (T // tm,), [x2d, pk, winT],
                [pl.BlockSpec((tm, D), lambda i: (i, 0)), VM, VM],
                [SDS((T, D), MXU), SDS((T, P), f32), SDS((T, QKV_W), MXU)],
                [pl.BlockSpec((tm, D), lambda i: (i, 0)), pl.BlockSpec((tm, P), lambda i: (i, 0)),
                 pl.BlockSpec((tm, QKV_W), lambda i: (i, 0))],
                vmem_mib=40, exchange=exchange)


def _swa_bias_table():
    r = np.arange(GQA * BLK)[:, None]
    k = np.arange(2 * BLK)[None, :]
    dist = (r % BLK) + BLK - k
    band = (dist >= 0) & (dist < BLK)
    tab = np.empty((2, N_KV, GQA * BLK, 2 * BLK), np.float32)
    for later in range(2):
        valid = band & ((k >= BLK) | (later == 1))
        for g in range(N_KV):
            slope = 2.0 ** -(g * GQA + r // BLK + 1.0)
            tab[later, g] = np.where(valid, -slope * dist, NEG)
    return jnp.asarray(tab)


def _sink_column(g, sk_ref):
    hrow = lax.broadcasted_iota(jnp.int32, (GQA * BLK, 1), 0) // BLK
    sink = jnp.zeros((GQA * BLK, 1), f32)
    for hh in range(GQA):
        sink = jnp.where(hrow == hh, sk_ref[g * GQA + hh:g * GQA + hh + 1, 0:1], sink)
    return sink


def _stack_heads(v, g):
    return jnp.concatenate([v[:, (g * GQA + hh) * HD:(g * GQA + hh + 1) * HD] for hh in range(GQA)], axis=0)


def attn_fwd(qkv, sink_rows, BL, S, exchange, qb=2):
    NS = S // (qb * BLK)
    T = BL * S

    def body(q_ref, kc_ref, kp_ref, vc_ref, vp_ref, sk_ref, tab_ref, o_ref):
        j = pl.program_id(1)
        kall = jnp.concatenate([kp_ref[...], kc_ref[...]], axis=0)
        vall = jnp.concatenate([vp_ref[...], vc_ref[...]], axis=0)
        ones = jnp.ones((2 * BLK, HD), MXU)
        for b in range(qb):
            q = q_ref[pl.ds(b * BLK, BLK), :]
            k2, v2 = kall[b * BLK:(b + 2) * BLK], vall[b * BLK:(b + 2) * BLK]
            later = jnp.minimum(j, 1) if b == 0 else 1
            for g in range(N_KV):
                kn, vh = k2[:, g * HD:(g + 1) * HD], v2[:, g * HD:(g + 1) * HD]
                s = _nt(_stack_heads(q, g), kn) * (HD ** -0.5) + tab_ref[later, g]
                e, es = _exp_scores(s, _sink_column(g, sk_ref))
                eb = _c(e)
                o = _nn(eb, vh) * (1.0 / (_nn(eb, ones) + es))
                for hh in range(GQA):
                    o_ref[pl.ds(b * BLK, BLK), pl.ds((g * GQA + hh) * HD, HD)] = o[hh * BLK:(hh + 1) * BLK]

    cur = lambda col: (lambda b, j: (b * NS + j, col))
    prev = lambda col: (lambda b, j: (qb * (b * NS + j) - jnp.minimum(j, 1), col))
    return _run("attn_fwd", body, (BL, NS), [qkv, qkv, qkv, qkv, qkv, sink_rows, _swa_bias_table()],
                [pl.BlockSpec((qb * BLK, ATT_W), cur(0)),
                 pl.BlockSpec((qb * BLK, KV_W), cur(4)), pl.BlockSpec((BLK, KV_W), prev(4)),
                 pl.BlockSpec((qb * BLK, KV_W), cur(5)), pl.BlockSpec((BLK, KV_W), prev(5)),
                 pl.BlockSpec((8, 128), lambda b, j: (0, 0)), VM],
                [SDS((T, ATT_W), f32)], [pl.BlockSpec((qb * BLK, ATT_W), cur(0))], exchange=exchange)


def _conv_taps(u, uh):
    row = lax.broadcasted_iota(jnp.int32, u.shape, 0)
    u1 = jnp.where(row == 0, uh[7:8, :], pltpu.roll(u, 1, 0))
    u2 = jnp.where(row == 0, uh[6:7, :], jnp.where(row == 1, uh[7:8, :], pltpu.roll(u, 2, 0)))
    return u1, u2


def _mem_head(qm, km, vm, h):
    qh, kh, vh = (a[:, h * HD:(h + 1) * HD] for a in (qm, km, vm))
    e, _ = _exp_scores(_nt(qh, kh) * (HD ** -0.5))
    return qh, kh, vh, e


def mixer_tail_fwd(x2d, attn_out, proj, qkv, km, vm, conv_w8, pk, wout, S, tm, exchange):
    T, D = x2d.shape
    NM = km.shape[0] // (T // S)

    def body(x_ref, ao_ref, ch_ref, cb_ref, cc_ref, chh_ref, cch_ref, qm_ref, km_ref, vm_ref, cw_ref, pk_ref,
             wout_ref, co_ref, mo_ref, mg_ref, x1_ref, h_ref):
        first = (pl.program_id(0) * tm) % S == 0
        u = cc_ref[...] * ch_ref[...]
        uh = jnp.where(first, 0.0, cch_ref[...] * chh_ref[...])
        u1, u2 = _conv_taps(u, uh)
        conv = cw_ref[0:1, :] * u2 + cw_ref[1:2, :] * u1 + cw_ref[2:3, :] * u + _small(pk_ref, "conv_b")
        conv_out = cb_ref[...] * conv
        co_ref[...] = conv_out
        qm, kmv, vmv = qm_ref[...], km_ref[...], vm_ref[...]
        ones = jnp.ones((NM, HD), MXU)
        for h in range(N_MEMH):
            _, _, vh, e = _mem_head(qm, kmv, vmv, h)
            eb = _c(e)
            mo_ref[:, pl.ds(h * HD, HD)] = _nn(eb, vh) * (1.0 / _nn(eb, ones))
        mem_out = mo_ref[...]
        ao = ao_ref[...]
        merged = _c(jnp.concatenate([ao * _rstd(ao) * _small(pk_ref, "out_norm_attn"),
                                     conv_out * _rstd(conv_out) * _small(pk_ref, "out_norm_conv"),
                                     mem_out * _rstd(mem_out) * _small(pk_ref, "out_norm_mem")], axis=1))
        mg_ref[...] = merged
        x1 = x_ref[...] + _nn(merged, wout_ref[...])
        x1_ref[...] = x1
        h_ref[...] = _c(x1 * _rstd(x1) * _small(pk_ref, "norm_ffn"))

    tile = lambda w, col: pl.BlockSpec((tm, w), lambda i: (i, col))
    halo = lambda col: pl.BlockSpec((8, CONV_W), lambda i: (jnp.maximum(i * (tm // 8) - 1, 0), col))
    seq = pl.BlockSpec((NM, MEM_W), lambda i: ((i * tm) // S, 0))
    small = lambda a: pl.BlockSpec(a.shape, lambda i: (0, 0))
    return _run("mixer_tail_fwd", body, (T // tm,),
                [x2d, attn_out, proj, proj, proj, proj, proj, qkv, km, vm, conv_w8, pk, wout],
                [tile(D, 0), tile(ATT_W, 0), tile(CONV_W, 3), tile(CONV_W, 4), tile(CONV_W, 5), halo(3), halo(5),
                 tile(MEM_W, 3), seq, seq, VM, VM, VM],
                [SDS((T, CONV_W), f32), SDS((T, MEM_W), f32), SDS((T, D), MXU), SDS((T, D), f32), SDS((T, D), MXU)],
                [tile(CONV_W, 0), tile(MEM_W, 0), tile(D, 0), tile(D, 0), tile(D, 0)], vmem_mib=40, exchange=exchange)


def ffn_fwd_bwd(h, x1, tgt, wgT, wuT, wd, pk, tm):
    T, D = x1.shape
    F = wd.shape[0]

    def body(h_ref, x1_ref, t_ref, wg_ref, wu_ref, wd_ref, pk_ref,
             dx1_ref, dx2_ref, act_ref, dg_ref, du_ref, loss_ref, dgf_ref):
        @pl.when(pl.program_id(0) == 0)
        def _():
            loss_ref[...] = jnp.zeros_like(loss_ref)
            dgf_ref[...] = jnp.zeros_like(dgf_ref)

        hv = h_ref[...]
        gate = _nt(hv, wg_ref[...])
        up = _nt(hv, wu_ref[...])
        sg = jax.nn.sigmoid(gate)
        sl = gate * sg
        act = _c(sl * up)
        act_ref[...] = act
        x1v = x1_ref[...]
        diff = (x1v + _nn(act, wd_ref[...])) - t_ref[...]
        loss_ref[...] += 0.5 * jnp.sum(jnp.sum(diff * diff, axis=-1, keepdims=True) / D, axis=0, keepdims=True)
        dx2 = diff / D
        dx2b = _c(dx2)
        dx2_ref[...] = dx2b
        d_act = _nt(dx2b, wd_ref[...])
        d_up = _c(d_act * sl)
        d_gate = _c(d_act * up * (sg * (1.0 + gate * (1.0 - sg))))
        du_ref[...] = d_up
        dg_ref[...] = d_gate
        dh = _nn(d_gate, wg_ref[...]) + _nn(d_up, wu_ref[...])
        dv, dgf = _norm_bwd(dh, x1v, _rstd(x1v), _small(pk_ref, "norm_ffn"))
        dx1_ref[...] = dx2 + dv
        dgf_ref[...] += dgf

    tile = lambda w: pl.BlockSpec((tm, w), lambda i: (i, 0))
    return _run("ffn_fwd_bwd", body, (T // tm,), [h, x1, tgt, wgT, wuT, wd, pk],
                [tile(D), tile(D), tile(D), VM, VM, VM, VM],
                [SDS((T, D), f32), SDS((T, D), MXU), SDS((T, F), MXU), SDS((T, F), MXU), SDS((T, F), MXU),
                 SDS((8, 128), f32), SDS((1, D), f32)],
                [tile(D), tile(D), tile(F), tile(F), tile(F), pl.BlockSpec((8, 128), lambda i: (0, 0)),
                 pl.BlockSpec((1, D), lambda i: (0, 0))], vmem_mib=56)


def matmul_tn(a, b, name, tmo, tk):
    T, M = a.shape
    N = b.shape[1]

    def body(a_ref, b_ref, o_ref):
        @pl.when(pl.program_id(1) == 0)
        def _():
            o_ref[...] = jnp.zeros_like(o_ref)

        o_ref[...] += _tn(a_ref[...], b_ref[...])

    return _run(name, body, (M // tmo, T // tk), [a, b],
                [pl.BlockSpec((tk, tmo), lambda m, k: (k, m)), pl.BlockSpec((tk, N), lambda m, k: (k, 0))],
                [SDS((M, N), f32)], [pl.BlockSpec((tmo, N), lambda m, k: (m, 0))], vmem_mib=48)[0]


def out_proj_bwd(dx1, merged, attn_out, conv_out, mem_out, pk, wout, tm):
    T, D = dx1.shape

    def body(dx1_ref, mg_ref, ao_ref, co_ref, mo_ref, pk_ref, w_ref,
             dao_ref, dco_ref, dmo_ref, dw_ref, dgain_ref):
        @pl.when(pl.program_id(0) == 0)
        def _():
            dw_ref[...] = jnp.zeros_like(dw_ref)
            dgain_ref[...] = jnp.zeros_like(dgain_ref)

        dxb = _c(dx1_ref[...])
        dw_ref[...] += _tn(mg_ref[...], dxb)
        dmg = _nt(dxb, w_ref[...])
        ao, co, mo = ao_ref[...], co_ref[...], mo_ref[...]
        da, ga = _norm_bwd(dmg[:, :ATT_W], ao, _rstd(ao), _small(pk_ref, "out_norm_attn"))
        dc, gc = _norm_bwd(dmg[:, ATT_W:ATT_W + CONV_W], co, _rstd(co), _small(pk_ref, "out_norm_conv"))
        dm, gm = _norm_bwd(dmg[:, ATT_W + CONV_W:], mo, _rstd(mo), _small(pk_ref, "out_norm_mem"))
        dao_ref[...] = da
        dco_ref[...] = dc
        dmo_ref[...] = dm
        dgain_ref[...] += jnp.concatenate([ga, gc, gm], axis=1)

    tile = lambda w: pl.BlockSpec((tm, w), lambda i: (i, 0))
    return _run("out_proj_bwd", body, (T // tm,), [dx1, merged, attn_out, conv_out, mem_out, pk, wout],
                [tile(D), tile(D), tile(ATT_W), tile(CONV_W), tile(MEM_W), VM, VM],
                [SDS((T, ATT_W), f32), SDS((T, CONV_W), f32), SDS((T, MEM_W), f32), SDS((D, D), f32), SDS((1, D), f32)],
                [tile(ATT_W), tile(CONV_W), tile(MEM_W), pl.BlockSpec((D, D), lambda i: (0, 0)),
                 pl.BlockSpec((1, D), lambda i: (0, 0))], vmem_mib=40)


def attn_bwd(qkv, d_attn, attn_out, sink_rows, BL, S, exchange):
    NB = S // BLK
    T = BL * S

    def body(q_ref, kc_ref, kp_ref, vc_ref, vp_ref, do_ref, ao_ref, sk_ref, tab_ref,
             dq_ref, dk_ref, dv_ref, dsk_ref, pend_k, pend_v):
        b, j = pl.program_id(0), pl.program_id(1)

        @pl.when((b == 0) & (j == 0))
        def _():
            dsk_ref[...] = jnp.zeros_like(dsk_ref)

        @pl.when(j == 0)
        def _():
            pend_k[...] = jnp.zeros_like(pend_k)
            pend_v[...] = jnp.zeros_like(pend_v)

        @pl.when(j < NB)
        def _():
            q, do, ao = q_ref[...], do_ref[...], ao_ref[...]
            k2 = jnp.concatenate([kp_ref[...], kc_ref[...]], axis=0)
            v2 = jnp.concatenate([vp_ref[...], vc_ref[...]], axis=0)
            lane = lax.broadcasted_iota(jnp.int32, (8, 128), 1)
            ones_w = jnp.ones((2 * BLK, 2 * BLK), MXU)
            dsk = jnp.zeros((8, 128), f32)
            dks, dvs = [], []
            for g in range(N_KV):
                kn, vh = k2[:, g * HD:(g + 1) * HD], v2[:, g * HD:(g + 1) * HD]
                qs = _stack_heads(q, g)
                s = _nt(qs, kn) * (HD ** -0.5) + tab_ref[g]
                e, es = _exp_scores(s, _sink_column(g, sk_ref))
                eb = _c(e)
                inv_w = 1.0 / (_nn(eb, ones_w) + es)
                inv_n = inv_w[:, :HD]
                dos = _stack_heads(do, g)
                delta = _rowsum_mxu(dos * _stack_heads(ao, g), 2 * BLK)
                dp = _nt(_c(dos), vh)
                ds = _c(e * inv_w * (dp - delta) * (HD ** -0.5))
                t = es * inv_n[:, 0:1] * delta[:, 0:1]
                for hh in range(GQA):
                    dsk = dsk + jnp.where(lane == g * GQA + hh, -jnp.sum(t[hh * BLK:(hh + 1) * BLK]), 0.0)
                dvs.append(_tn(eb, _c(dos * inv_n)))
                dks.append(_tn(ds, qs))
                dqs = _nn(ds, kn)
                for hh in range(GQA):
                    dq_ref[:, pl.ds((g * GQA + hh) * HD, HD)] = dqs[hh * BLK:(hh + 1) * BLK]
            dk2 = jnp.concatenate(dks, axis=1)
            dv2 = jnp.concatenate(dvs, axis=1)
            dk_ref[...] = pend_k[...] + dk2[:BLK]
            dv_ref[...] = pend_v[...] + dv2[:BLK]
            pend_k[...] = dk2[BLK:]
            pend_v[...] = dv2[BLK:]
            dsk_ref[...] += dsk

        @pl.when(j == NB)
        def _():
            dk_ref[...] = pend_k[...]
            dv_ref[...] = pend_v[...]

    cur = lambda col: (lambda b, j: (b * NB + jnp.minimum(j, NB - 1), col))
    prev = lambda col: (lambda b, j: (b * NB + jnp.maximum(j - 1, 0), col))
    small = lambda shape: pl.BlockSpec(shape, lambda b, j: (0, 0))
    return _run("attn_bwd", body, (BL, NB + 1), [qkv, qkv, qkv, qkv, qkv, d_attn, attn_out, sink_rows, _swa_bias_table()],
                [pl.BlockSpec((BLK, ATT_W), cur(0)),
                 pl.BlockSpec((BLK, KV_W), cur(4)), pl.BlockSpec((BLK, KV_W), prev(4)),
                 pl.BlockSpec((BLK, KV_W), cur(5)), pl.BlockSpec((BLK, KV_W), prev(5)),
                 pl.BlockSpec((BLK, ATT_W), cur(0)), pl.BlockSpec((BLK, ATT_W), cur(0)), small((8, 128)),
                 pl.BlockSpec((None, N_KV, GQA * BLK, 2 * BLK), lambda b, j: (jnp.minimum(j, 1), 0, 0, 0))],
                [SDS((T, ATT_W), f32), SDS((T, KV_W), f32), SDS((T, KV_W), f32), SDS((8, 128), f32)],
                [pl.BlockSpec((BLK, ATT_W), cur(0)), pl.BlockSpec((BLK, KV_W), prev(0)),
                 pl.BlockSpec((BLK, KV_W), prev(0)), small((8, 128))],
                scratch=[pltpu.VMEM((BLK, KV_W), f32)] * 2, exchange=exchange)


def mem_conv_bwd(d_mem_out, mem_out, d_conv_out, proj, qkv, km, vm, conv_w8, pk, S, tm, exchange):
    T = d_mem_out.shape[0]
    NM = km.shape[0] // (T // S)

    def body(dmo_ref, mo_ref, dco_ref, ch_ref, cb_ref, cc_ref, chh_ref, cch_ref, qm_ref, km_ref, vm_ref, cw_ref,
             pk_ref, dqm_ref, dkm_ref, dvm_ref, dcb_ref, dcv_ref, dcw_ref, dcbias_ref):
        i = pl.program_id(0)
        first = (i * tm) % S == 0

        @pl.when(i == 0)
        def _():
            dcw_ref[...] = jnp.zeros_like(dcw_ref)
            dcbias_ref[...] = jnp.zeros_like(dcbias_ref)

        @pl.when(first)
        def _():
            dkm_ref[...] = jnp.zeros_like(dkm_ref)
            dvm_ref[...] = jnp.zeros_like(dvm_ref)

        qm, kmv, vmv, dmo, mo = qm_ref[...], km_ref[...], vm_ref[...], dmo_ref[...], mo_ref[...]
        ones_w = jnp.ones((NM, NM), MXU)
        for h in range(N_MEMH):
            qh, kh, vh, e = _mem_head(qm, kmv, vmv, h)
            eb = _c(e)
            doh = dmo[:, h * HD:(h + 1) * HD]
            delta = _rowsum_mxu(doh * mo[:, h * HD:(h + 1) * HD], NM)
            dp = _nt(_c(doh), vh)
            inv_w = 1.0 / _nn(eb, ones_w)
            ds = _c(e * inv_w * (dp - delta) * (HD ** -0.5))
            dvm_ref[:, pl.ds(h * HD, HD)] += _tn(eb, _c(doh * inv_w[:, :HD]))
            dkm_ref[:, pl.ds(h * HD, HD)] += _tn(ds, qh)
            dqm_ref[:, pl.ds(h * HD, HD)] = _nn(ds, kh)

        u = cc_ref[...] * ch_ref[...]
        uh = jnp.where(first, 0.0, cch_ref[...] * chh_ref[...])
        u1, u2 = _conv_taps(u, uh)
        conv = cw_ref[0:1, :] * u2 + cw_ref[1:2, :] * u1 + cw_ref[2:3, :] * u + _small(pk_ref, "conv_b")
        dy = dco_ref[...]
        dcb_ref[...] = dy * conv
        dcv = dy * cb_ref[...]
        dcv_ref[...] = dcv
        dcbias_ref[...] += jnp.sum(dcv, axis=0, keepdims=True)
        dcw_ref[0:1, :] += jnp.sum(dcv * u2, axis=0, keepdims=True)
        dcw_ref[1:2, :] += jnp.sum(dcv * u1, axis=0, keepdims=True)
        dcw_ref[2:3, :] += jnp.sum(dcv * u, axis=0, keepdims=True)

    tile = lambda w, col: pl.BlockSpec((tm, w), lambda i: (i, col))
    halo = lambda col: pl.BlockSpec((8, CONV_W), lambda i: (jnp.maximum(i * (tm // 8) - 1, 0), col))
    seq = pl.BlockSpec((NM, MEM_W), lambda i: ((i * tm) // S, 0))
    const = lambda shape: pl.BlockSpec(shape, lambda i: (0, 0))
    return _run("mem_conv_bwd", body, (T // tm,),
                [d_mem_out, mem_out, d_conv_out, proj, proj, proj, proj, proj, qkv, km, vm, conv_w8, pk],
                [tile(MEM_W, 0), tile(MEM_W, 0), tile(CONV_W, 0), tile(CONV_W, 3), tile(CONV_W, 4), tile(CONV_W, 5),
                 halo(3), halo(5), tile(MEM_W, 3), seq, seq, VM, VM],
                [SDS((T, MEM_W), f32), SDS(km.shape, f32), SDS(km.shape, f32),
                 SDS((T, CONV_W), f32), SDS((T, CONV_W), f32), SDS((8, CONV_W), f32), SDS((1, CONV_W), f32)],
                [tile(MEM_W, 0), seq, seq, tile(CONV_W, 0), tile(CONV_W, 0), const((8, CONV_W)), const((1, CONV_W))],
                vmem_mib=48, exchange=exchange)


def in_proj_bwd(dqn, dkn, dv, dcb, dcv, dqmn, proj, conv_w8, xn, x2d, dx1, pk, winT, S, tm, stages, ws, ms, vs):
    T, D = x2d.shape
    P = winT.shape[0]
    last_blk = T // 8 - 1
    n = len(stages)
    nsteps = T // tm
    tile_w = ws[0].shape[1] // (nsteps // 2)
    turn = [e * 2 // n for e in range(n)]

    def body(dq_ref, dk_ref, dv_ref, dcb_ref, dcv_ref, dcvn_ref, dqm_ref, qa_ref, ka_ref, ch_ref, cc_ref, qma_ref,
             cw_ref, xn_ref, x_ref, dx1_ref, pk_ref, w_ref, *rest):
        st, aw, am, av = (rest[k * n:(k + 1) * n] for k in range(4))
        dx_ref, dw_ref, dg_ref, dqg_ref, dkg_ref, dmqg_ref = rest[4 * n:4 * n + 6]
        aouts = rest[4 * n + 6:]
        i = pl.program_id(0)

        for parity in range(2):
            @pl.when(i % 2 == parity)
            def _(parity=parity):
                for e in range(n):
                    if turn[e] == parity:
                        g = jnp.concatenate([_sum_chips(st[e].at[0]), _sum_chips(st[e].at[1])], axis=0)
                        d, mm, vv = _adamw_math(aw[e][...], g, am[e][...], av[e][...])
                        for k, val in enumerate((g, d, mm, vv)):
                            aouts[4 * e + k][...] = val

        @pl.when(i == 0)
        def _():
            dw_ref[...] = jnp.zeros_like(dw_ref)
            dg_ref[...] = jnp.zeros_like(dg_ref)
            dqg_ref[...] = jnp.zeros_like(dqg_ref)
            dkg_ref[...] = jnp.zeros_like(dkg_ref)
            dmqg_ref[...] = jnp.zeros_like(dmqg_ref)

        dqa, gq = _heads_norm_bwd(dq_ref[...], qa_ref[...], _small(pk_ref, "q_norm"))
        dka, gk = _heads_norm_bwd(dk_ref[...], ka_ref[...], _small(pk_ref, "k_norm"))
        dqma, gmq = _heads_norm_bwd(dqm_ref[...], qma_ref[...], _small(pk_ref, "mem_q_norm"))
        dqg_ref[...] += gq
        dkg_ref[...] += gk
        dmqg_ref[...] += gmq

        last = ((i + 1) * tm) % S == 0
        dcv = dcv_ref[...]
        nxt = jnp.where(last, 0.0, dcvn_ref[...])
        row = lax.broadcasted_iota(jnp.int32, dcv.shape, 0)
        n1 = jnp.where(row == tm - 1, nxt[0:1, :], pltpu.roll(dcv, tm - 1, 0))
        n2 = jnp.where(row == tm - 2, nxt[0:1, :], jnp.where(row == tm - 1, nxt[1:2, :], pltpu.roll(dcv, tm - 2, 0)))
        du = cw_ref[2:3, :] * dcv + cw_ref[1:2, :] * n1 + cw_ref[0:1, :] * n2
        d_proj = jnp.concatenate([_c(dqa), _c(dka), _c(dv_ref[...]), _c(du * cc_ref[...]),
                                  _c(dcb_ref[...]), _c(du * ch_ref[...]), _c(dqma)], axis=1)
        dw_ref[...] += _tn(d_proj, xn_ref[...])
        xv = x_ref[...]
        dv_, dg = _norm_bwd(_nn(d_proj, w_ref[...]), xv, _rstd(xv), _small(pk_ref, "norm_mix"))
        dx_ref[...] = dx1_ref[...] + dv_
        dg_ref[...] += dg

    tile = lambda w, col=0: pl.BlockSpec((tm, w), lambda i: (i, col))
    nhalo = pl.BlockSpec((8, CONV_W), lambda i: (jnp.minimum((i + 1) * (tm // 8), last_blk), 0))
    const = lambda shape: pl.BlockSpec(shape, lambda i: (0, 0))
    st_specs = [pl.BlockSpec((2, 4, s.shape[2], tile_w), lambda i: (0, 0, 0, i // 2)) for s in stages]
    w_specs = [pl.BlockSpec((w.shape[0], tile_w), lambda i: (0, i // 2)) for w in ws]
    res = _run("in_proj_bwd", body, (nsteps,),
               [dqn, dkn, dv, dcb, dcv, dcv, dqmn, proj, proj, proj, proj, proj, conv_w8, xn, x2d, dx1, pk, winT]
               + list(stages) + list(ws) + list(ms) + list(vs),
               [tile(ATT_W), tile(KV_W), tile(KV_W), tile(CONV_W), tile(CONV_W), nhalo, tile(MEM_W),
                tile(ATT_W, 0), tile(KV_W, 4), tile(CONV_W, 3), tile(CONV_W, 5), tile(MEM_W, 6), VM,
                tile(D), tile(D), tile(D), VM, VM] + st_specs + w_specs * 3,
               [SDS((T, D), f32), SDS((P, D), f32), SDS((1, D), f32), SDS((1, HD), f32), SDS((1, HD), f32),
                SDS((1, HD), f32)] + [SDS(w.shape, f32) for w in ws for _ in range(4)],
               [tile(D), pl.BlockSpec((P, D), lambda i: (0, 0)), const((1, D)), const((1, HD)), const((1, HD)),
                const((1, HD))] + [s for s in w_specs for _ in range(4)],
               vmem_mib=56)
    return res[:6], [res[6 + 4 * e:10 + 4 * e] for e in range(n)]


def mem_kv_bwd(dkm, dvm, kv, memn, mem2d, pk, wmkv):
    def body(dkm_ref, dvm_ref, kv_ref, mn_ref, m_ref, pk_ref, w_ref, dw_ref, dg_ref, dkg_ref):
        dkk, dkg = _heads_norm_bwd(dkm_ref[...], kv_ref[:, :MEM_W], _small(pk_ref, "mem_k_norm"))
        dkg_ref[...] = dkg
        dkv = _c(jnp.concatenate([dkk, dvm_ref[...]], axis=1))
        dw_ref[...] = _tn(mn_ref[...], dkv)
        mv = m_ref[...]
        dg_ref[...] = jnp.sum(_nt(dkv, w_ref[...]) * mv * _rstd(mv), axis=0, keepdims=True)

    return _run("mem_kv_bwd", body, (), [dkm, dvm, kv, memn, mem2d, pk, wmkv], [VM] * 7,
                [SDS(wmkv.shape, f32), SDS((1, mem2d.shape[1]), f32), SDS((1, HD), f32)], [VM] * 3, vmem_mib=40)


def _halves_view(g):
    return g.reshape(4, 2, g.shape[0] // 8, g.shape[1])


def kernel(x, mem, norm_mix, w_in, q_norm, k_norm, attn_sinks, conv_w, conv_b, norm_mem, w_mem_kv, mem_q_norm, mem_k_norm, out_norm_attn, out_norm_conv, out_norm_mem, w_out, norm_ffn, w_gate, w_up, w_down, loss_target, m_norm_mix, m_w_in, m_q_norm, m_k_norm, m_attn_sinks, m_conv_w, m_conv_b, m_norm_mem, m_w_mem_kv, m_mem_q_norm, m_mem_k_norm, m_out_norm_attn, m_out_norm_conv, m_out_norm_mem, m_w_out, m_norm_ffn, m_w_gate, m_w_up, m_w_down, v_norm_mix, v_w_in, v_q_norm, v_k_norm, v_attn_sinks, v_conv_w, v_conv_b, v_norm_mem, v_w_mem_kv, v_mem_q_norm, v_mem_k_norm, v_out_norm_attn, v_out_norm_conv, v_out_norm_mem, v_w_out, v_norm_ffn, v_w_gate, v_w_up, v_w_down):
    BL, S, D = x.shape
    T = BL * S
    TM = 256
    TM_BIG = min(512, S)
    _, _, ci = _place()
    cidx = ci.reshape(1).astype(jnp.int32)
    w_small = dict(norm_mix=norm_mix, norm_mem=norm_mem, norm_ffn=norm_ffn, out_norm_attn=out_norm_attn,
                   out_norm_conv=out_norm_conv, out_norm_mem=out_norm_mem, conv_w=conv_w, conv_b=conv_b, q_norm=q_norm,
                   k_norm=k_norm, mem_q_norm=mem_q_norm, mem_k_norm=mem_k_norm, attn_sinks=attn_sinks)
    m_small = dict(norm_mix=m_norm_mix, norm_mem=m_norm_mem, norm_ffn=m_norm_ffn, out_norm_attn=m_out_norm_attn,
                   out_norm_conv=m_out_norm_conv, out_norm_mem=m_out_norm_mem, conv_w=m_conv_w, conv_b=m_conv_b,
                   q_norm=m_q_norm, k_norm=m_k_norm, mem_q_norm=m_mem_q_norm, mem_k_norm=m_mem_k_norm,
                   attn_sinks=m_attn_sinks)
    v_small = dict(norm_mix=v_norm_mix, norm_mem=v_norm_mem, norm_ffn=v_norm_ffn, out_norm_attn=v_out_norm_attn,
                   out_norm_conv=v_out_norm_conv, out_norm_mem=v_out_norm_mem, conv_w=v_conv_w, conv_b=v_conv_b,
                   q_norm=v_q_norm, k_norm=v_k_norm, mem_q_norm=v_mem_q_norm, mem_k_norm=v_mem_k_norm,
                   attn_sinks=v_attn_sinks)
    pk = _pack_small(w_small)

    rowblocks = lambda a, b, c, d, e, f: [a[0].T, b[0].T, c[0].T, d[0], e[0], f[0]]
    w_rb = rowblocks(w_in, w_gate, w_up, w_down, w_out, w_mem_kv)
    m_rb = rowblocks(m_w_in, m_w_gate, m_w_up, m_w_down, m_w_out, m_w_mem_kv)
    v_rb = rowblocks(v_w_in, v_w_gate, v_w_up, v_w_down, v_w_out, v_w_mem_kv)
    (winT_s,) = prep_weights("prep_w_in", w_rb[:1])
    cw_pad = jnp.zeros((8, 128), f32).at[:3, :HD].set(conv_w[0])
    (wgT_s, wuT_s, wd_s, wout_s, wmkv_s), (winT, cw_all) = prep_weights(
        "gather_w_in", w_rb[1:], _together([gather_two_legs([winT_s]), gather_exchange([cw_pad], [False])]))
    conv_w_full = jnp.transpose(cw_all.reshape(4, 8, 128)[:, :3, :HD], (1, 0, 2)).reshape(3, CONV_W)
    conv_w8 = jnp.zeros((8, CONV_W), f32).at[:3].set(conv_w_full)
    sink_rows = jnp.broadcast_to(attn_sinks.reshape(N_Q, 1), (N_Q, 128))

    x2d = x.reshape(T, D)
    mem2d = mem.reshape(-1, D)
    (xn, proj, qkv), near1 = in_proj_fwd(x2d, pk, winT, TM_BIG, gather_near_exchange([wgT_s, wout_s, wmkv_s]))
    (attn_out,), (wgT, wout, wmkv, *near2) = attn_fwd(
        qkv, sink_rows, BL, S, _together([gather_far_exchange(near1, relay_early=2), gather_near_exchange([wuT_s, wd_s])]))
    memn, kv, km, vm = mem_kv_fwd(mem2d, pk, wmkv)
    (conv_out, mem_out, merged, x1, h), (wuT, wd) = mixer_tail_fwd(
        x2d, attn_out, proj, qkv, km, vm, conv_w8, pk, wout, S, TM_BIG, gather_far_exchange(near2, relay_early=2))

    dx1, dx2b, act, d_gate, d_up, loss8, d_norm_ffn = ffn_fwd_bwd(h, x1, loss_target.reshape(T, D), wgT, wuT, wd, pk, TM)
    F = wd.shape[0]
    g_wd = matmul_tn(act, dx2b, "dw_down", F // 2, min(T, 1024))
    g_wgT = matmul_tn(d_gate, h, "dw_gate", F // 2, min(T, 1024))
    g_wuT = matmul_tn(d_up, h, "dw_up", F // 2, min(T, 1024))

    d_attn, d_conv_out, d_mem_out, g_wout, d_gains = out_proj_bwd(dx1, merged, attn_out, conv_out, mem_out, pk, wout, TM_BIG)
    late = [_halves_view(g) for g in (g_wgT, g_wuT, g_wd, g_wout)]
    (dqmn, dkm, dvm, dcb, dcv, d_cw8, d_cbias), late_sib = mem_conv_bwd(
        d_mem_out, mem_out, d_conv_out, proj, qkv, km, vm, conv_w8, pk, S, min(1024, S), halves_exchange(late))
    late_part = add_halves(cidx, late, late_sib, "grad_add_halves_ffn")
    (dqn, dkn, dv, d_sink8), late_stage = attn_bwd(qkv, d_attn, attn_out, sink_rows, BL, S, scatter_exchange(late_part))
    (g_x, g_winT, d_norm_mix, d_qg, d_kg, d_mqg), late_res = in_proj_bwd(
        dqn, dkn, dv, dcb, dcv, dqmn, proj, conv_w8, xn, x2d, dx1, pk, winT, S, TM,
        late_stage, w_rb[1:5], m_rb[1:5], v_rb[1:5])
    g_wmkv, d_norm_mem, d_mkg = mem_kv_bwd(dkm, dvm, kv, memn, mem2d, pk, wmkv)

    tot, tail_stage = tail_reduce(d_norm_mix, d_norm_mem, d_norm_ffn, d_gains, d_cw8, d_cbias, d_qg, d_kg, d_mqg, d_mkg,
                                  d_sink8, loss8, [_halves_view(g) for g in (g_winT, g_wmkv)])
    loss = tot[5, 384]
    tail_res, _ = adamw_big("adamw_tail", tail_stage, [w_rb[0], w_rb[5]], [m_rb[0], m_rb[5]], [v_rb[0], v_rb[5]], 4)
    res = {"w_in": [a.T[None] for a in tail_res[0]], "w_gate": [a.T[None] for a in late_res[0]],
           "w_up": [a.T[None] for a in late_res[1]], "w_down": [a[None] for a in late_res[2]],
           "w_out": [a[None] for a in late_res[3]], "w_mem_kv": [a[None] for a in tail_res[1]]}
    res.update(adamw_small(tot, pk, _pack_small(m_small), _pack_small(v_small), {k: w_small[k].shape for k in SMALL}))

    order = ["norm_mix", "w_in", "q_norm", "k_norm", "attn_sinks", "conv_w", "conv_b", "norm_mem", "w_mem_kv",
             "mem_q_norm", "mem_k_norm", "out_norm_attn", "out_norm_conv", "out_norm_mem", "w_out", "norm_ffn",
             "w_gate", "w_up", "w_down"]
    return (loss, g_x.reshape(BL, S, D), *[res[n][0] for n in order], *[res[n][1] for n in order],
            *[res[n][2] for n in order], *[res[n][3] for n in order])
```

```python
import collections
import functools

import jax
import jax.numpy as jnp
import numpy as np
from jax import lax
from jax.experimental import pallas as pl
from jax.experimental.pallas import tpu as pltpu

f32 = jnp.float32
MXU = jnp.bfloat16
WIRE = jnp.bfloat16
EPS = 1e-6
NEG = -1e30
HD = 64
BLK = 128
N_Q, N_KV, N_MEMH = 8, 2, 4
GQA = N_Q // N_KV
ATT_W, KV_W, CONV_W, MEM_W = 512, 128, 256, 256
VMEM_MIB = 1024 * 1024
ADAM_LR, ADAM_B1, ADAM_B2, ADAM_EPS, ADAM_WD, ADAM_STEP = 0.001, 0.9, 0.999, 1e-08, 0.01, 10

MESH = pl.DeviceIdType.MESH
VM = pl.BlockSpec(memory_space=pltpu.VMEM)
ANY = pl.BlockSpec(memory_space=pl.ANY)
SDS = jax.ShapeDtypeStruct
DMA = pltpu.SemaphoreType.DMA


def _c(v):
    return v.astype(MXU)


def _nn(a, b):
    return lax.dot_general(a, b, (((1,), (0,)), ((), ())), preferred_element_type=f32)


def _nt(a, b):
    return lax.dot_general(a, b, (((1,), (1,)), ((), ())), preferred_element_type=f32)


def _tn(a, b):
    return lax.dot_general(a, b, (((0,), (0,)), ((), ())), preferred_element_type=f32)


def _rstd(v):
    return lax.rsqrt(jnp.mean(v * v, axis=-1, keepdims=True) + EPS)


def _norm_bwd(dy, v, r, g):
    dyg = dy * g
    dv = r * dyg - v * (r * r * r) * jnp.mean(dyg * v, axis=-1, keepdims=True)
    return dv, jnp.sum(dy * v * r, axis=0, keepdims=True)


def _split3(v):
    hi = _c(v)
    r1 = v - hi.astype(f32)
    mid = _c(r1)
    return hi, mid, _c(r1 - mid.astype(f32))


def _rowsum_mxu(v, width):
    ones = jnp.ones((v.shape[1], width), MXU)
    return sum(_nn(a, ones) for a in _split3(v))


def _seg_sums(v):
    r = lax.broadcasted_iota(jnp.int32, (2 * HD, 2 * HD), 0) // HD
    c = lax.broadcasted_iota(jnp.int32, (2 * HD, 2 * HD), 1) // HD
    bd = (r == c).astype(MXU)
    outs = []
    for b in range(v.shape[1] // (2 * HD)):
        outs.append(sum(_nn(a, bd) for a in _split3(v[:, b * 2 * HD:(b + 1) * 2 * HD])))
    return outs[0] if len(outs) == 1 else jnp.concatenate(outs, axis=1)


def _lanes(g, width):
    return jnp.concatenate([g] * (width // HD), axis=1)


def _heads_rstd(v):
    return lax.rsqrt(_seg_sums(v * v) * (1.0 / HD) + EPS)


def _heads_norm_bwd(dy, v, g):
    r = _heads_rstd(v)
    gl = _lanes(g, v.shape[1])
    dyg = dy * gl
    dv = r * dyg - v * (r * r * r) * (_seg_sums(dyg * v) * (1.0 / HD))
    dgl = jnp.sum(dy * v * r, axis=0, keepdims=True)
    return dv, sum(dgl[:, s * HD:(s + 1) * HD] for s in range(v.shape[1] // HD))


def _exp_scores(s, extra=None):
    m = jnp.max(s, axis=-1, keepdims=True)
    if extra is None:
        return jnp.exp(s - m), None
    m = jnp.maximum(m, extra)
    return jnp.exp(s - m), jnp.exp(extra - m)


def _place():
    return lax.axis_index("x"), lax.axis_index("y"), lax.axis_index("c")


SMALL_AT = {"norm_mix": (0, 0, 1024), "norm_mem": (1, 0, 1024), "norm_ffn": (2, 0, 1024),
            "out_norm_attn": (3, 0, ATT_W), "out_norm_conv": (3, ATT_W, CONV_W), "out_norm_mem": (3, ATT_W + CONV_W, MEM_W),
            "conv_b": (4, 3 * CONV_W, CONV_W), "q_norm": (5, 0, HD), "k_norm": (5, HD, HD), "mem_q_norm": (5, 2 * HD, HD),
            "mem_k_norm": (5, 3 * HD, HD), "attn_sinks": (5, 256, N_Q)}
SMALL = ("norm_mix", "norm_mem", "norm_ffn", "out_norm_attn", "out_norm_conv", "out_norm_mem", "conv_w", "conv_b",
         "q_norm", "k_norm", "mem_q_norm", "mem_k_norm", "attn_sinks")


def _small(pk_ref, name):
    r, c0, w = SMALL_AT[name]
    return pk_ref[r:r + 1, c0:c0 + w]


def _pack_small(d):
    z = lambda n: jnp.zeros((1, n), f32)
    row3 = jnp.concatenate([d["out_norm_attn"], d["out_norm_conv"], d["out_norm_mem"]], axis=1)
    row4 = jnp.concatenate([d["conv_w"].reshape(1, 3 * HD), z(3 * CONV_W - 3 * HD), d["conv_b"]], axis=1)
    row5 = jnp.concatenate([d["q_norm"], d["k_norm"], d["mem_q_norm"], d["mem_k_norm"], d["attn_sinks"],
                            z(1024 - 4 * HD - N_Q)], axis=1)
    return jnp.concatenate([d["norm_mix"], d["norm_mem"], d["norm_ffn"], row3, row4, row5, z(1024), z(1024)], axis=0)


def _other_chips(x, y):
    return [(1 - x, y), (x, 1 - y), (1 - x, 1 - y)]


Exchange = collections.namedtuple("Exchange", "ins outs sems start finish relay relay_steps_before_end aliases",
                                  defaults=(None, 0, {}))


def _together(exchanges):
    def parts(refs, key):
        out, at = [], 0
        for ex in exchanges:
            out.append(refs[at:at + len(getattr(ex, key))])
            at += len(getattr(ex, key))
        return out

    def phase(name):
        def run(xa, xo, xs):
            for ex, a, o, s in zip(exchanges, parts(xa, "ins"), parts(xo, "outs"), parts(xs, "sems")):
                if getattr(ex, name) is not None:
                    getattr(ex, name)(a, o, s)
        return run

    aliases, ai, ao = {}, 0, 0
    for ex in exchanges:
        aliases.update({ai + i: ao + o for i, o in ex.aliases.items()})
        ai, ao = ai + len(ex.ins), ao + len(ex.outs)
    return Exchange([a for ex in exchanges for a in ex.ins], [o for ex in exchanges for o in ex.outs],
                    [s for ex in exchanges for s in ex.sems], phase("start"), phase("finish"), phase("relay"),
                    max(ex.relay_steps_before_end for ex in exchanges), aliases)


def _run(name, body, grid, ins, in_specs, out_shape, out_specs, scratch=(), vmem_mib=32, exchange=None):
    ins, in_specs, out_shape, out_specs, scratch = list(ins), list(in_specs), list(out_shape), list(out_specs), list(scratch)
    ni, no, ns = len(ins), len(out_shape), len(scratch)
    ex = exchange
    if ex is not None:
        nxi, nxo = len(ex.ins), len(ex.outs)

    def call_body(*refs):
        if ex is None:
            body(*refs)
            return
        a, xa = refs[:ni], refs[ni:ni + nxi]
        o, xo = refs[ni + nxi:ni + nxi + no], refs[ni + nxi + no:ni + nxi + no + nxo]
        s, xs = refs[ni + nxi + no + nxo:ni + nxi + no + nxo + ns], refs[ni + nxi + no + nxo + ns:]
        if grid:
            first = functools.reduce(jnp.logical_and, [pl.program_id(d) == 0 for d in range(len(grid))])
            last = functools.reduce(jnp.logical_and, [pl.program_id(d) == grid[d] - 1 for d in range(len(grid))])
            pl.when(first)(lambda: ex.start(xa, xo, xs))
            body(*a, *o, *s)
            if ex.relay is not None:
                early = functools.reduce(jnp.logical_and, [pl.program_id(d) == grid[d] - 1 for d in range(len(grid) - 1)],
                                         pl.program_id(len(grid) - 1) == max(grid[-1] - 1 - ex.relay_steps_before_end, 0))
                pl.when(early)(lambda: ex.relay(xa, xo, xs))
            pl.when(last)(lambda: ex.finish(xa, xo, xs))
        else:
            ex.start(xa, xo, xs)
            if body is not None:
                body(*a, *o, *s)
            if ex.relay is not None:
                ex.relay(xa, xo, xs)
            ex.finish(xa, xo, xs)

    kw = dict(grid=grid) if grid else {}
    if ex is not None:
        if ex.aliases:
            kw["input_output_aliases"] = {ni + i: no + o for i, o in ex.aliases.items()}
        ins, in_specs = ins + list(ex.ins), in_specs + [ANY] * nxi
        out_shape, out_specs = out_shape + list(ex.outs), out_specs + [ANY] * nxo
        scratch = scratch + list(ex.sems)
    res = pl.pallas_call(
        call_body, name=name, out_shape=out_shape, in_specs=in_specs, out_specs=out_specs, scratch_shapes=scratch,
        compiler_params=pltpu.CompilerParams(dimension_semantics=("arbitrary",) * len(grid) if grid else None,
                                             vmem_limit_bytes=vmem_mib * VMEM_MIB), **kw)(*ins)
    res = list(res)
    return (res[:no], res[no:]) if ex is not None else res


def _remote(src, dst, ssem, rsem, dev):
    return pltpu.make_async_remote_copy(src_ref=src, dst_ref=dst, send_sem=ssem, recv_sem=rsem,
                                        device_id=dev, device_id_type=MESH)


def gather_exchange(shards, split, relay_early=0):
    n = len(shards)

    def rows(ref, e, kk, half=None):
        R = shards[e].shape[0]
        if half is None:
            return ref.at[pl.ds(pl.multiple_of(kk * R, 8), R)]
        return ref.at[pl.ds(pl.multiple_of(kk * R + half * (R // 2), 8), R // 2)]

    def ici(src, dst, sm, e, j, chip_j, x, y, c):
        k = 2 * x + y
        if split[e]:
            s = src[e].at[pl.ds(pl.multiple_of(c * (shards[e].shape[0] // 2), 8), shards[e].shape[0] // 2)]
            return _remote(s, rows(dst[e], e, k, c), sm[0].at[6 * e + j], sm[1].at[6 * e + j], (*chip_j, c))
        return _remote(src[e], rows(dst[e], e, k), sm[0].at[6 * e + j], sm[1].at[6 * e + j], (*chip_j, c))

    def landed(dst, e, chip_j, c):
        kj = 2 * chip_j[0] + chip_j[1]
        return rows(dst[e], e, kj, c) if split[e] else rows(dst[e], e, kj)

    def forward(dst, sm, e, j, chip_j, x, y, c, sender_c):
        kj = 2 * chip_j[0] + chip_j[1]
        r = rows(dst[e], e, kj, sender_c)
        return _remote(r, r, sm[0].at[6 * e + 3 + j], sm[1].at[6 * e + 3 + j], (x, y, 1 - c))

    def local(src, dst, sm, e, x, y):
        return pltpu.make_async_copy(src[e], rows(dst[e], e, 2 * x + y), sm[2].at[e])

    def start(src, dst, sm):
        x, y, c = _place()
        for e in range(n):
            local(src, dst, sm, e, x, y).start()
            for j, chip_j in enumerate(_other_chips(x, y)):
                ici(src, dst, sm, e, j, chip_j, x, y, c).start()

    def relay(src, dst, sm):
        x, y, c = _place()
        for e in range(n):
            for j, chip_j in enumerate(_other_chips(x, y)):
                r = landed(dst, e, chip_j, c)
                _remote(r, r, sm[0].at[6 * e + j], sm[1].at[6 * e + j], (*chip_j, c)).wait_recv()
                if split[e]:
                    forward(dst, sm, e, j, chip_j, x, y, c, c).start()

    def finish(src, dst, sm):
        x, y, c = _place()
        chips = _other_chips(x, y)
        for e in range(n):
            for j, chip_j in enumerate(chips):
                if split[e]:
                    forward(dst, sm, e, j, chip_j, x, y, c, 1 - c).wait_recv()
        for e in range(n):
            for j, chip_j in enumerate(chips):
                ici(src, dst, sm, e, j, chip_j, x, y, c).wait_send()
                if split[e]:
                    forward(dst, sm, e, j, chip_j, x, y, c, c).wait_send()
            local(src, dst, sm, e, x, y).wait()

    outs = [SDS((4 * s.shape[0], s.shape[1]), s.dtype) for s in shards]
    return Exchange(list(shards), outs, [DMA((6 * n,)), DMA((6 * n,)), DMA((n,))], start, finish, relay, relay_early)


def _block_rows(ref, R, kk, half, quarter=None):
    hr = R // 2
    if quarter is None:
        return ref.at[pl.ds(pl.multiple_of(kk * R + half * hr, 8), hr)]
    return ref.at[pl.ds(pl.multiple_of(kk * R + half * hr + quarter * (hr // 2), 8), hr // 2)]


def gather_near_exchange(shards, relay_early=0):
    n = len(shards)
    R = [s.shape[0] for s in shards]

    def ici(src, dst, sm, e, j, chip_j, x, y, c):
        half = src[e].at[pl.ds(pl.multiple_of(c * (R[e] // 2), 8), R[e] // 2)]
        return _remote(half, _block_rows(dst[e], R[e], 2 * x + y, c), sm[0].at[4 * e + j], sm[1].at[4 * e + j], (*chip_j, c))

    def forward(dst, sm, e, j, chip_j, x, y, c, sender_c):
        r = _block_rows(dst[e], R[e], 2 * chip_j[0] + chip_j[1], sender_c)
        return _remote(r, r, sm[0].at[4 * e + 2 + j], sm[1].at[4 * e + 2 + j], (x, y, 1 - c))

    def local(src, dst, sm, e, x, y):
        return pltpu.make_async_copy(src[e], dst[e].at[pl.ds(pl.multiple_of((2 * x + y) * R[e], 8), R[e])], sm[2].at[e])

    def start(src, dst, sm):
        x, y, c = _place()
        for e in range(n):
            local(src, dst, sm, e, x, y).start()
            for j, chip_j in enumerate(_other_chips(x, y)[:2]):
                ici(src, dst, sm, e, j, chip_j, x, y, c).start()

    def relay(src, dst, sm):
        x, y, c = _place()
        for e in range(n):
            for j, chip_j in enumerate(_other_chips(x, y)[:2]):
                r = _block_rows(dst[e], R[e], 2 * chip_j[0] + chip_j[1], c)
                _remote(r, r, sm[0].at[4 * e + j], sm[1].at[4 * e + j], (*chip_j, c)).wait_recv()
                forward(dst, sm, e, j, chip_j, x, y, c, c).start()

    def finish(src, dst, sm):
        x, y, c = _place()
        near = _other_chips(x, y)[:2]
        for e in range(n):
            for j, chip_j in enumerate(near):
                forward(dst, sm, e, j, chip_j, x, y, c, 1 - c).wait_recv()
        for e in range(n):
            for j, chip_j in enumerate(near):
                ici(src, dst, sm, e, j, chip_j, x, y, c).wait_send()
                forward(dst, sm, e, j, chip_j, x, y, c, c).wait_send()
            local(src, dst, sm, e, x, y).wait()

    outs = [SDS((4 * s.shape[0], s.shape[1]), s.dtype) for s in shards]
    return Exchange(list(shards), outs, [DMA((4 * n,)), DMA((4 * n,)), DMA((n,))], start, finish, relay, relay_early)


def gather_far_exchange(bufs, relay_early=0):
    n = len(bufs)
    R = [b.shape[0] // 4 for b in bufs]

    def send(src, dst, sm, e, j, x, y, c):
        to, of = _other_chips(x, y)[j], _other_chips(x, y)[1 - j]
        kk = 2 * of[0] + of[1]
        return _remote(_block_rows(src[e], R[e], kk, c, j), _block_rows(dst[e], R[e], kk, c, j),
                       sm[0].at[4 * e + j], sm[1].at[4 * e + j], (*to, c))

    def landed(dst, e, j, x, y, half):
        return _block_rows(dst[e], R[e], 2 * (1 - x) + (1 - y), half, j)

    def forward(dst, sm, e, j, x, y, c, sender_c):
        r = landed(dst, e, j, x, y, sender_c)
        return _remote(r, r, sm[0].at[4 * e + 2 + j], sm[1].at[4 * e + 2 + j], (x, y, 1 - c))

    def start(src, dst, sm):
        x, y, c = _place()
        for e in range(n):
            for j in range(2):
                send(src, dst, sm, e, j, x, y, c).start()

    def relay(src, dst, sm):
        x, y, c = _place()
        for e in range(n):
            for j in range(2):
                r = landed(dst, e, j, x, y, c)
                _remote(r, r, sm[0].at[4 * e + j], sm[1].at[4 * e + j], (*_other_chips(x, y)[j], c)).wait_recv()
                forward(dst, sm, e, j, x, y, c, c).start()

    def finish(src, dst, sm):
        x, y, c = _place()
        for e in range(n):
            for j in range(2):
                forward(dst, sm, e, j, x, y, c, 1 - c).wait_recv()
        for e in range(n):
            for j in range(2):
                send(src, dst, sm, e, j, x, y, c).wait_send()
                forward(dst, sm, e, j, x, y, c, c).wait_send()

    outs = [SDS(b.shape, b.dtype) for b in bufs]
    return Exchange(list(bufs), outs, [DMA((4 * n,)), DMA((4 * n,))], start, finish, relay, relay_early,
                    {i: i for i in range(n)})


def gather_two_legs(shards):
    near = gather_near_exchange(shards)
    far = gather_far_exchange(near.outs)

    def finish(src, dst, sm):
        near.relay(src, dst, sm[:3])
        near.finish(src, dst, sm[:3])
        far.start(dst, dst, sm[3:])
        far.relay(dst, dst, sm[3:])
        far.finish(dst, dst, sm[3:])

    return Exchange(near.ins, near.outs, list(near.sems) + list(far.sems),
                    lambda src, dst, sm: near.start(src, dst, sm[:3]), finish)


def halves_exchange(grads):
    n = len(grads)

    def copy(g, st, sm, e, x, y, c):
        return _remote(g[e].at[:, 1 - c], st[e], sm[0].at[e], sm[1].at[e], (x, y, 1 - c))

    def start(g, st, sm):
        x, y, c = _place()
        for e in range(n):
            copy(g, st, sm, e, x, y, c).start()

    def finish(g, st, sm):
        x, y, c = _place()
        for e in range(n):
            copy(g, st, sm, e, x, y, c).wait()

    outs = [SDS((4,) + a.shape[2:], a.dtype) for a in grads]
    return Exchange(list(grads), outs, [DMA((n,)), DMA((n,))], start, finish)


def scatter_exchange(parts):
    n = len(parts)

    def ici(p, st, sm, e, j, chip_j, x, y, c):
        k, kj = 2 * x + y, 2 * chip_j[0] + chip_j[1]
        return _remote(p[e].at[kj], st[e].at[c, k], sm[0].at[8 * e + j], sm[1].at[8 * e + j], (*chip_j, c))

    def own(p, st, sm, e, x, y, c):
        k = 2 * x + y
        return _remote(p[e].at[k], st[e].at[c, k], sm[0].at[8 * e + 3], sm[1].at[8 * e + 3], (x, y, 1 - c))

    def forward(st, sm, e, j, chip_j, x, y, c, sender_c):
        kj = 2 * chip_j[0] + chip_j[1]
        r = st[e].at[sender_c, kj]
        return _remote(r, r, sm[0].at[8 * e + 4 + j], sm[1].at[8 * e + 4 + j], (x, y, 1 - c))

    def local(p, st, sm, e, x, y, c):
        k = 2 * x + y
        return pltpu.make_async_copy(p[e].at[k], st[e].at[c, k], sm[2].at[e])

    def start(p, st, sm):
        x, y, c = _place()
        for e in range(n):
            local(p, st, sm, e, x, y, c).start()
            own(p, st, sm, e, x, y, c).start()
            for j, chip_j in enumerate(_other_chips(x, y)):
                ici(p, st, sm, e, j, chip_j, x, y, c).start()

    def relay(p, st, sm):
        x, y, c = _place()
        for e in range(n):
            for j, chip_j in enumerate(_other_chips(x, y)):
                kj = 2 * chip_j[0] + chip_j[1]
                r = st[e].at[c, kj]
                _remote(r, r, sm[0].at[8 * e + j], sm[1].at[8 * e + j], (*chip_j, c)).wait_recv()
                forward(st, sm, e, j, chip_j, x, y, c, c).start()

    def finish(p, st, sm):
        x, y, c = _place()
        k = 2 * x + y
        chips = _other_chips(x, y)
        for e in range(n):
            r = st[e].at[1 - c, k]
            _remote(r, r, sm[0].at[8 * e + 3], sm[1].at[8 * e + 3], (x, y, 1 - c)).wait_recv()
            for j, chip_j in enumerate(chips):
                forward(st, sm, e, j, chip_j, x, y, c, 1 - c).wait_recv()
        for e in range(n):
            own(p, st, sm, e, x, y, c).wait_send()
            for j, chip_j in enumerate(chips):
                ici(p, st, sm, e, j, chip_j, x, y, c).wait_send()
                forward(st, sm, e, j, chip_j, x, y, c, c).wait_send()
            local(p, st, sm, e, x, y, c).wait()

    outs = [SDS((2,) + a.shape, a.dtype) for a in parts]
    return Exchange(list(parts), outs, [DMA((8 * n,)), DMA((8 * n,)), DMA((n,))], start, finish, relay)


def tail_reduce(d_norm_mix, d_norm_mem, d_norm_ffn, d_gains, d_cw8, d_cbias, d_qg, d_kg, d_mqg, d_mkg, d_sink8, loss8, tail):
    n = len(tail)
    halves = halves_exchange(tail)
    scatter = scatter_exchange([SDS((4,) + a.shape[2:], WIRE) for a in tail])

    def body(nm_ref, nmem_ref, nf_ref, gn_ref, cw_ref, cb_ref, qg_ref, kg_ref, mqg_ref, mkg_ref, sk_ref, ls_ref, *rest):
        g, o_ref, st = rest[:n], rest[n], rest[n + 1:2 * n + 1]
        buf, ssem, rsem = rest[2 * n + 1:2 * n + 4]
        own, sib, part = (rest[2 * n + 4 + i * n:2 * n + 4 + (i + 1) * n] for i in range(3))
        lsem = rest[5 * n + 4]
        hsem, xsem = rest[5 * n + 5:5 * n + 7], rest[5 * n + 7:]
        x, y, c = _place()
        loads = [pltpu.make_async_copy(g[e].at[:, c], own[e], lsem.at[e]) for e in range(n)]
        for ld in loads:
            ld.start()
        halves.start(g, sib, hsem)
        me = 4 * x + 2 * y + c
        mine = buf.at[me]
        mine[...] = jnp.zeros((8, 1024), f32)
        mine[0:1, :] = nm_ref[...]
        mine[1:2, :] = nmem_ref[...]
        mine[2:3, :] = nf_ref[...]
        mine[3:4, :] = gn_ref[...]
        for j in range(3):
            mine[4:5, pl.ds(j * CONV_W, CONV_W)] = cw_ref[j:j + 1, :]
        mine[4:5, pl.ds(3 * CONV_W, CONV_W)] = cb_ref[...]
        for j, r in enumerate((qg_ref, kg_ref, mqg_ref, mkg_ref)):
            mine[5:6, pl.ds(j * HD, HD)] = r[...]
        mine[5:6, pl.ds(256, 128)] = sk_ref[0:1, :]
        mine[5:6, pl.ds(384, 128)] = ls_ref[0:1, :]

        def peer_of(m):
            return (1 - x if m & 4 else x, 1 - y if m & 2 else y, 1 - c if m & 1 else c)

        for m in range(1, 8):
            _remote(mine, mine, ssem.at[m - 1], rsem.at[m - 1], peer_of(m)).start()
        for ld in loads:
            ld.wait()
        halves.finish(g, sib, hsem)
        for e in range(n):
            part[e][...] = (own[e][...] + sib[e][...]).astype(WIRE)
        scatter.start(part, st, xsem)
        scatter.relay(part, st, xsem)
        scatter.finish(part, st, xsem)
        for m in range(1, 8):
            p = peer_of(m)
            got = buf.at[4 * p[0] + 2 * p[1] + p[2]]
            _remote(got, got, ssem.at[m - 1], rsem.at[m - 1], p).wait_recv()
        for m in range(1, 8):
            _remote(mine, mine, ssem.at[m - 1], rsem.at[m - 1], peer_of(m)).wait_send()
        acc = buf[0]
        for d in range(1, 8):
            acc = acc + buf[d]
        o_ref[...] = acc

    ins = [d_norm_mix, d_norm_mem, d_norm_ffn, d_gains, d_cw8, d_cbias, d_qg, d_kg, d_mqg, d_mkg, d_sink8, loss8]
    half_shape = [(4,) + a.shape[2:] for a in tail]
    scratch = ([pltpu.VMEM((8, 8, 1024), f32), DMA((7,)), DMA((7,))]
               + [pltpu.VMEM(s, f32) for s in half_shape] * 2 + [pltpu.VMEM(s, WIRE) for s in half_shape]
               + [DMA((n,))] + list(halves.sems) + list(scatter.sems))
    res = _run("tail_reduce", body, (), ins + list(tail), [VM] * len(ins) + [ANY] * n,
               [SDS((8, 1024), f32)] + list(scatter.outs), [VM] + [ANY] * n, scratch=scratch, vmem_mib=40)
    return res[0], res[1:]


def add_halves(cidx, grads, stages, name, nch=2):
    n = len(grads)

    def body(c_ref, *refs):
        g, st, o = refs[:n], refs[n:2 * n], refs[2 * n:]
        for e in range(n):
            o[e][...] = (g[e][...] + st[e][...]).astype(WIRE)

    in_specs, out_specs, out_shape = [], [], []
    for a in grads:
        hr, C = a.shape[2], a.shape[3]
        in_specs.append(pl.BlockSpec((None, None, hr // nch, C), lambda s, q, c_ref: (s, c_ref[0], q, 0)))
    for a in stages:
        hr, C = a.shape[1], a.shape[2]
        in_specs.append(pl.BlockSpec((None, hr // nch, C), lambda s, q, c_ref: (s, q, 0)))
        out_specs.append(pl.BlockSpec((None, hr // nch, C), lambda s, q, c_ref: (s, q, 0)))
        out_shape.append(SDS(a.shape, WIRE))
    return pl.pallas_call(
        body, name=name, out_shape=out_shape,
        grid_spec=pltpu.PrefetchScalarGridSpec(num_scalar_prefetch=1, grid=(4, nch), in_specs=in_specs, out_specs=out_specs),
        compiler_params=pltpu.CompilerParams(dimension_semantics=("arbitrary", "arbitrary")),
    )(cidx, *grads, *stages)


def _adamw_math(w, g, m, v):
    m = ADAM_B1 * m + (1.0 - ADAM_B1) * g
    v = ADAM_B2 * v + (1.0 - ADAM_B2) * (g * g)
    m_hat = m / (1.0 - ADAM_B1 ** ADAM_STEP)
    v_hat = v / (1.0 - ADAM_B2 ** ADAM_STEP)
    delta = -ADAM_LR * (m_hat / (jnp.sqrt(v_hat) + ADAM_EPS) + ADAM_WD * w)
    return delta, m, v


def _sum_chips(st):
    return ((st[0].astype(f32) + st[1].astype(f32)) + st[2].astype(f32)) + st[3].astype(f32)


def adamw_big(name, stages, ws, ms, vs, nstep, exchange=None):
    n = len(stages)

    def body(*refs):
        st, w, m, v = refs[:n], refs[n:2 * n], refs[2 * n:3 * n], refs[3 * n:4 * n]
        outs = refs[4 * n:]
        for e in range(n):
            g = jnp.concatenate([_sum_chips(st[e].at[0]), _sum_chips(st[e].at[1])], axis=0)
            d, mm, vv = _adamw_math(w[e][...], g, m[e][...], v[e][...])
            outs[4 * e][...] = g
            outs[4 * e + 1][...] = d
            outs[4 * e + 2][...] = mm
            outs[4 * e + 3][...] = vv

    st_specs, w_specs = [], []
    for e in range(n):
        _, _, hr, C = stages[e].shape
        st_specs.append(pl.BlockSpec((2, 4, hr, C // nstep), lambda i: (0, 0, 0, i)))
        w_specs.append(pl.BlockSpec((2 * hr, C // nstep), lambda i: (0, i)))
    out_specs = [s for s in w_specs for _ in range(4)]
    out_shape = [SDS(w.shape, f32) for w in ws for _ in range(4)]
    res = _run(name, body, (nstep,), list(stages) + list(ws) + list(ms) + list(vs), st_specs + w_specs * 3,
               out_shape, out_specs, vmem_mib=48, exchange=exchange)
    res, sent = res if exchange is not None else (res, None)
    return [res[4 * e:4 * e + 4] for e in range(n)], sent


def adamw_small(tot, pk_w, pk_m, pk_v, shapes):
    def body(tot_ref, w_ref, m_ref, v_ref, *outs):
        x, y, _ = _place()
        chip = 2 * x + y
        taps = []
        for j in range(3):
            mine = tot_ref[4:5, j * CONV_W:j * CONV_W + HD]
            for s in range(1, 4):
                mine = jnp.where(chip == s, tot_ref[4:5, j * CONV_W + s * HD:j * CONV_W + (s + 1) * HD], mine)
            taps.append(mine)
        row4 = jnp.concatenate(taps + [jnp.zeros((1, 3 * CONV_W - 3 * HD), f32), tot_ref[4:5, 3 * CONV_W:]], axis=1)
        tot_v = tot_ref[...]
        row = lax.broadcasted_iota(jnp.int32, tot_v.shape, 0)
        g = jnp.where(row == 4, jnp.broadcast_to(row4, tot_v.shape), tot_v)
        d, mm, vv = _adamw_math(w_ref[...], g, m_ref[...], v_ref[...])
        for i, name in enumerate(SMALL):
            for k, val in enumerate((g, d, mm, vv)):
                if name == "conv_w":
                    outs[4 * i + k][...] = jnp.concatenate([val[4:5, j * HD:(j + 1) * HD] for j in range(3)], axis=0)[None]
                else:
                    r, c0, w = SMALL_AT[name]
                    outs[4 * i + k][...] = val[r:r + 1, c0:c0 + w]

    out_shape = [SDS(shapes[k], f32) for k in SMALL for _ in range(4)]
    res = _run("adamw_small", body, (), [tot, pk_w, pk_m, pk_v], [VM] * 4, out_shape, [VM] * len(out_shape))
    return {k: res[4 * i:4 * i + 4] for i, k in enumerate(SMALL)}


def prep_weights(name, shards, exchange=None):
    n = len(shards)

    def body(*refs):
        for e in range(n):
            refs[n + e][...] = _c(refs[e][...])

    return _run(name, body, (), shards, [VM] * n, [SDS(a.shape, MXU) for a in shards], [VM] * n, vmem_mib=48, exchange=exchange)


def mem_kv_fwd(mem2d, pk, wmkv):
    M, D = mem2d.shape

    def body(m_ref, pk_ref, w_ref, mn_ref, kv_ref, km_ref, vm_ref):
        m = m_ref[...]
        mn = _c(m * _rstd(m) * _small(pk_ref, "norm_mem"))
        mn_ref[...] = mn
        kv = _nn(mn, w_ref[...])
        kv_ref[...] = kv
        kk = kv[:, :MEM_W]
        km_ref[...] = _c(kk * _heads_rstd(kk) * _lanes(_small(pk_ref, "mem_k_norm"), MEM_W))
        vm_ref[...] = _c(kv[:, MEM_W:])

    return _run("mem_kv_fwd", body, (), [mem2d, pk, wmkv], [VM] * 3,
                [SDS((M, D), MXU), SDS((M, 2 * MEM_W), f32), SDS((M, MEM_W), MXU), SDS((M, MEM_W), MXU)], [VM] * 4)


QKV_W = ATT_W + 2 * KV_W + MEM_W


def in_proj_fwd(x2d, pk, winT, tm, exchange):
    T, D = x2d.shape
    P = winT.shape[0]

    def body(x_ref, pk_ref, w_ref, xn_ref, proj_ref, qkv_ref):
        xv = x_ref[...]
        xn = _c(xv * _rstd(xv) * _small(pk_ref, "norm_mix"))
        xn_ref[...] = xn
        proj = _nt(xn, w_ref[...])
        proj_ref[...] = proj
        q, k = proj[:, :ATT_W], proj[:, ATT_W:ATT_W + KV_W]
        qm = proj[:, P - MEM_W:]
        qkv_ref[...] = jnp.concatenate(
            [_c(q * _heads_rstd(q) * _lanes(_small(pk_ref, "q_norm"), ATT_W)),
             _c(k * _heads_rstd(k) * _lanes(_small(pk_ref, "k_norm"), KV_W)),
             _c(proj[:, ATT_W + KV_W:ATT_W + 2 * KV_W]),
             _c(qm * _heads_rstd(qm) * _lanes(_small(pk_ref, "mem_q_norm"), MEM_W))], axis=1)

    return _run("in_proj_fwd", body, (T // tm,), [x2d, pk, winT],
                [pl.BlockSpec((tm, D), lambda i: (i, 0)), VM, VM],
                [SDS((T, D), MXU), SDS((T, P), f32), SDS((T, QKV_W), MXU)],
                [pl.BlockSpec((tm, D), lambda i: (i, 0)), pl.BlockSpec((tm, P), lambda i: (i, 0)),
                 pl.BlockSpec((tm, QKV_W), lambda i: (i, 0))],
                vmem_mib=40, exchange=exchange)


def _swa_bias_table():
    r = np.arange(GQA * BLK)[:, None]
    k = np.arange(2 * BLK)[None, :]
    dist = (r % BLK) + BLK - k
    band = (dist >= 0) & (dist < BLK)
    tab = np.empty((2, N_KV, GQA * BLK, 2 * BLK), np.float32)
    for later in range(2):
        valid = band & ((k >= BLK) | (later == 1))
        for g in range(N_KV):
            slope = 2.0 ** -(g * GQA + r // BLK + 1.0)
            tab[later, g] = np.where(valid, -slope * dist, NEG)
    return jnp.asarray(tab)


def _sink_column(g, sk_ref):
    hrow = lax.broadcasted_iota(jnp.int32, (GQA * BLK, 1), 0) // BLK
    sink = jnp.zeros((GQA * BLK, 1), f32)
    for hh in range(GQA):
        sink = jnp.where(hrow == hh, sk_ref[g * GQA + hh:g * GQA + hh + 1, 0:1], sink)
    return sink


def _stack_heads(v, g):
    return jnp.concatenate([v[:, (g * GQA + hh) * HD:(g * GQA + hh + 1) * HD] for hh in range(GQA)], axis=0)


def attn_fwd(qkv, sink_rows, BL, S, exchange, qb=2):
    NS = S // (qb * BLK)
    T = BL * S

    def body(q_ref, kc_ref, kp_ref, vc_ref, vp_ref, sk_ref, tab_ref, o_ref):
        j = pl.program_id(1)
        kall = jnp.concatenate([kp_ref[...], kc_ref[...]], axis=0)
        vall = jnp.concatenate([vp_ref[...], vc_ref[...]], axis=0)
        ones = jnp.ones((2 * BLK, HD), MXU)
        for b in range(qb):
            q = q_ref[pl.ds(b * BLK, BLK), :]
            k2, v2 = kall[b * BLK:(b + 2) * BLK], vall[b * BLK:(b + 2) * BLK]
            later = jnp.minimum(j, 1) if b == 0 else 1
            for g in range(N_KV):
                kn, vh = k2[:, g * HD:(g + 1) * HD], v2[:, g * HD:(g + 1) * HD]
                s = _nt(_stack_heads(q, g), kn) * (HD ** -0.5) + tab_ref[later, g]
                e, es = _exp_scores(s, _sink_column(g, sk_ref))
                eb = _c(e)
                o = _nn(eb, vh) * (1.0 / (_nn(eb, ones) + es))
                for hh in range(GQA):
                    o_ref[pl.ds(b * BLK, BLK), pl.ds((g * GQA + hh) * HD, HD)] = o[hh * BLK:(hh + 1) * BLK]

    cur = lambda col: (lambda b, j: (b * NS + j, col))
    prev = lambda col: (lambda b, j: (qb * (b * NS + j) - jnp.minimum(j, 1), col))
    return _run("attn_fwd", body, (BL, NS), [qkv, qkv, qkv, qkv, qkv, sink_rows, _swa_bias_table()],
                [pl.BlockSpec((qb * BLK, ATT_W), cur(0)),
                 pl.BlockSpec((qb * BLK, KV_W), cur(4)), pl.BlockSpec((BLK, KV_W), prev(4)),
                 pl.BlockSpec((qb * BLK, KV_W), cur(5)), pl.BlockSpec((BLK, KV_W), prev(5)),
                 pl.BlockSpec((8, 128), lambda b, j: (0, 0)), VM],
                [SDS((T, ATT_W), f32)], [pl.BlockSpec((qb * BLK, ATT_W), cur(0))], exchange=exchange)


def _conv_taps(u, uh):
    row = lax.broadcasted_iota(jnp.int32, u.shape, 0)
    u1 = jnp.where(row == 0, uh[7:8, :], pltpu.roll(u, 1, 0))
    u2 = jnp.where(row == 0, uh[6:7, :], jnp.where(row == 1, uh[7:8, :], pltpu.roll(u, 2, 0)))
    return u1, u2


def _mem_head(qm, km, vm, h):
    qh, kh, vh = (a[:, h * HD:(h + 1) * HD] for a in (qm, km, vm))
    e, _ = _exp_scores(_nt(qh, kh) * (HD ** -0.5))
    return qh, kh, vh, e


def mixer_tail_fwd(x2d, attn_out, proj, qkv, km, vm, conv_w8, pk, wout, S, tm, exchange):
    T, D = x2d.shape
    NM = km.shape[0] // (T // S)

    def body(x_ref, ao_ref, ch_ref, cb_ref, cc_ref, chh_ref, cch_ref, qm_ref, km_ref, vm_ref, cw_ref, pk_ref,
             wout_ref, co_ref, mo_ref, mg_ref, x1_ref, h_ref):
        first = (pl.program_id(0) * tm) % S == 0
        u = cc_ref[...] * ch_ref[...]
        uh = jnp.where(first, 0.0, cch_ref[...] * chh_ref[...])
        u1, u2 = _conv_taps(u, uh)
        conv = cw_ref[0:1, :] * u2 + cw_ref[1:2, :] * u1 + cw_ref[2:3, :] * u + _small(pk_ref, "conv_b")
        conv_out = cb_ref[...] * conv
        co_ref[...] = conv_out
        qm, kmv, vmv = qm_ref[...], km_ref[...], vm_ref[...]
        ones = jnp.ones((NM, HD), MXU)
        for h in range(N_MEMH):
            _, _, vh, e = _mem_head(qm, kmv, vmv, h)
            eb = _c(e)
            mo_ref[:, pl.ds(h * HD, HD)] = _nn(eb, vh) * (1.0 / _nn(eb, ones))
        mem_out = mo_ref[...]
        ao = ao_ref[...]
        merged = _c(jnp.concatenate([ao * _rstd(ao) * _small(pk_ref, "out_norm_attn"),
                                     conv_out * _rstd(conv_out) * _small(pk_ref, "out_norm_conv"),
                                     mem_out * _rstd(mem_out) * _small(pk_ref, "out_norm_mem")], axis=1))
        mg_ref[...] = merged
        x1 = x_ref[...] + _nn(merged, wout_ref[...])
        x1_ref[...] = x1
        h_ref[...] = _c(x1 * _rstd(x1) * _small(pk_ref, "norm_ffn"))

    tile = lambda w, col: pl.BlockSpec((tm, w), lambda i: (i, col))
    halo = lambda col: pl.BlockSpec((8, CONV_W), lambda i: (jnp.maximum(i * (tm // 8) - 1, 0), col))
    seq = pl.BlockSpec((NM, MEM_W), lambda i: ((i * tm) // S, 0))
    small = lambda a: pl.BlockSpec(a.shape, lambda i: (0, 0))
    return _run("mixer_tail_fwd", body, (T // tm,),
                [x2d, attn_out, proj, proj, proj, proj, proj, qkv, km, vm, conv_w8, pk, wout],
                [tile(D, 0), tile(ATT_W, 0), tile(CONV_W, 3), tile(CONV_W, 4), tile(CONV_W, 5), halo(3), halo(5),
                 tile(MEM_W, 3), seq, seq, VM, VM, VM],
                [SDS((T, CONV_W), f32), SDS((T, MEM_W), f32), SDS((T, D), MXU), SDS((T, D), f32), SDS((T, D), MXU)],
                [tile(CONV_W, 0), tile(MEM_W, 0), tile(D, 0), tile(D, 0), tile(D, 0)], vmem_mib=40, exchange=exchange)


def ffn_fwd_bwd(h, x1, tgt, wgT, wuT, wd, pk, tm):
    T, D = x1.shape
    F = wd.shape[0]

    def body(h_ref, x1_ref, t_ref, wg_ref, wu_ref, wd_ref, pk_ref,
             dx1_ref, dx2_ref, act_ref, dg_ref, du_ref, loss_ref, dgf_ref):
        @pl.when(pl.program_id(0) == 0)
        def _():
            loss_ref[...] = jnp.zeros_like(loss_ref)
            dgf_ref[...] = jnp.zeros_like(dgf_ref)

        hv = h_ref[...]
        gate = _nt(hv, wg_ref[...])
        up = _nt(hv, wu_ref[...])
        sg = jax.nn.sigmoid(gate)
        sl = gate * sg
        act = _c(sl * up)
        act_ref[...] = act
        x1v = x1_ref[...]
        diff = (x1v + _nn(act, wd_ref[...])) - t_ref[...]
        loss_ref[...] += 0.5 * jnp.sum(jnp.sum(diff * diff, axis=-1, keepdims=True) / D, axis=0, keepdims=True)
        dx2 = diff / D
        dx2b = _c(dx2)
        dx2_ref[...] = dx2b
        d_act = _nt(dx2b, wd_ref[...])
        d_up = _c(d_act * sl)
        d_gate = _c(d_act * up * (sg * (1.0 + gate * (1.0 - sg))))
        du_ref[...] = d_up
        dg_ref[...] = d_gate
        dh = _nn(d_gate, wg_ref[...]) + _nn(d_up, wu_ref[...])
        dv, dgf = _norm_bwd(dh, x1v, _rstd(x1v), _small(pk_ref, "norm_ffn"))
        dx1_ref[...] = dx2 + dv
        dgf_ref[...] += dgf

    tile = lambda w: pl.BlockSpec((tm, w), lambda i: (i, 0))
    return _run("ffn_fwd_bwd", body, (T // tm,), [h, x1, tgt, wgT, wuT, wd, pk],
                [tile(D), tile(D), tile(D), VM, VM, VM, VM],
                [SDS((T, D), f32), SDS((T, D), MXU), SDS((T, F), MXU), SDS((T, F), MXU), SDS((T, F), MXU),
                 SDS((8, 128), f32), SDS((1, D), f32)],
                [tile(D), tile(D), tile(F), tile(F), tile(F), pl.BlockSpec((8, 128), lambda i: (0, 0)),
                 pl.BlockSpec((1, D), lambda i: (0, 0))], vmem_mib=56)


def matmul_tn(a, b, name, tmo, tk):
    T, M = a.shape
    N = b.shape[1]

    def body(a_ref, b_ref, o_ref):
        @pl.when(pl.program_id(1) == 0)
        def _():
            o_ref[...] = jnp.zeros_like(o_ref)

        o_ref[...] += _tn(a_ref[...], b_ref[...])

    return _run(name, body, (M // tmo, T // tk), [a, b],
                [pl.BlockSpec((tk, tmo), lambda m, k: (k, m)), pl.BlockSpec((tk, N), lambda m, k: (k, 0))],
                [SDS((M, N), f32)], [pl.BlockSpec((tmo, N), lambda m, k: (m, 0))], vmem_mib=48)[0]


def out_proj_bwd(dx1, merged, attn_out, conv_out, mem_out, pk, wout, tm):
    T, D = dx1.shape

    def body(dx1_ref, mg_ref, ao_ref, co_ref, mo_ref, pk_ref, w_ref,
             dao_ref, dco_ref, dmo_ref, dw_ref, dgain_ref):
        @pl.when(pl.program_id(0) == 0)
        def _():
            dw_ref[...] = jnp.zeros_like(dw_ref)
            dgain_ref[...] = jnp.zeros_like(dgain_ref)

        dxb = _c(dx1_ref[...])
        dw_ref[...] += _tn(mg_ref[...], dxb)
        dmg = _nt(dxb, w_ref[...])
        ao, co, mo = ao_ref[...], co_ref[...], mo_ref[...]
        da, ga = _norm_bwd(dmg[:, :ATT_W], ao, _rstd(ao), _small(pk_ref, "out_norm_attn"))
        dc, gc = _norm_bwd(dmg[:, ATT_W:ATT_W + CONV_W], co, _rstd(co), _small(pk_ref, "out_norm_conv"))
        dm, gm = _norm_bwd(dmg[:, ATT_W + CONV_W:], mo, _rstd(mo), _small(pk_ref, "out_norm_mem"))
        dao_ref[...] = da
        dco_ref[...] = dc
        dmo_ref[...] = dm
        dgain_ref[...] += jnp.concatenate([ga, gc, gm], axis=1)

    tile = lambda w: pl.BlockSpec((tm, w), lambda i: (i, 0))
    return _run("out_proj_bwd", body, (T // tm,), [dx1, merged, attn_out, conv_out, mem_out, pk, wout],
                [tile(D), tile(D), tile(ATT_W), tile(CONV_W), tile(MEM_W), VM, VM],
                [SDS((T, ATT_W), f32), SDS((T, CONV_W), f32), SDS((T, MEM_W), f32), SDS((D, D), f32), SDS((1, D), f32)],
                [tile(ATT_W), tile(CONV_W), tile(MEM_W), pl.BlockSpec((D, D), lambda i: (0, 0)),
                 pl.BlockSpec((1, D), lambda i: (0, 0))], vmem_mib=40)


def attn_bwd(qkv, d_attn, attn_out, sink_rows, BL, S, exchange):
    NB = S // BLK
    T = BL * S

    def body(q_ref, kc_ref, kp_ref, vc_ref, vp_ref, do_ref, ao_ref, sk_ref, tab_ref,
             dq_ref, dk_ref, dv_ref, dsk_ref, pend_k, pend_v):
        b, j = pl.program_id(0), pl.program_id(1)

        @pl.when((b == 0) & (j == 0))
        def _():
            dsk_ref[...] = jnp.zeros_like(dsk_ref)

        @pl.when(j == 0)
        def _():
            pend_k[...] = jnp.zeros_like(pend_k)
            pend_v[...] = jnp.zeros_like(pend_v)

        @pl.when(j < NB)
        def _():
            q, do, ao = q_ref[...], do_ref[...], ao_ref[...]
            k2 = jnp.concatenate([kp_ref[...], kc_ref[...]], axis=0)
            v2 = jnp.concatenate([vp_ref[...], vc_ref[...]], axis=0)
            lane = lax.broadcasted_iota(jnp.int32, (8, 128), 1)
            ones_w = jnp.ones((2 * BLK, 2 * BLK), MXU)
            dsk = jnp.zeros((8, 128), f32)
            dks, dvs = [], []
            for g in range(N_KV):
                kn, vh = k2[:, g * HD:(g + 1) * HD], v2[:, g * HD:(g + 1) * HD]
                qs = _stack_heads(q, g)
                s = _nt(qs, kn) * (HD ** -0.5) + tab_ref[g]
                e, es = _exp_scores(s, _sink_column(g, sk_ref))
                eb = _c(e)
                inv_w = 1.0 / (_nn(eb, ones_w) + es)
                inv_n = inv_w[:, :HD]
                dos = _stack_heads(do, g)
                delta = _rowsum_mxu(dos * _stack_heads(ao, g), 2 * BLK)
                dp = _nt(_c(dos), vh)
                ds = _c(e * inv_w * (dp - delta) * (HD ** -0.5))
                t = es * inv_n[:, 0:1] * delta[:, 0:1]
                for hh in range(GQA):
                    dsk = dsk + jnp.where(lane == g * GQA + hh, -jnp.sum(t[hh * BLK:(hh + 1) * BLK]), 0.0)
                dvs.append(_tn(eb, _c(dos * inv_n)))
                dks.append(_tn(ds, qs))
                dqs = _nn(ds, kn)
                for hh in range(GQA):
                    dq_ref[:, pl.ds((g * GQA + hh) * HD, HD)] = dqs[hh * BLK:(hh + 1) * BLK]
            dk2 = jnp.concatenate(dks, axis=1)
            dv2 = jnp.concatenate(dvs, axis=1)
            dk_ref[...] = pend_k[...] + dk2[:BLK]
            dv_ref[...] = pend_v[...] + dv2[:BLK]
            pend_k[...] = dk2[BLK:]
            pend_v[...] = dv2[BLK:]
            dsk_ref[...] += dsk

        @pl.when(j == NB)
        def _():
            dk_ref[...] = pend_k[...]
            dv_ref[...] = pend_v[...]

    cur = lambda col: (lambda b, j: (b * NB + jnp.minimum(j, NB - 1), col))
    prev = lambda col: (lambda b, j: (b * NB + jnp.maximum(j - 1, 0), col))
    small = lambda shape: pl.BlockSpec(shape, lambda b, j: (0, 0))
    return _run("attn_bwd", body, (BL, NB + 1), [qkv, qkv, qkv, qkv, qkv, d_attn, attn_out, sink_rows, _swa_bias_table()],
                [pl.BlockSpec((BLK, ATT_W), cur(0)),
                 pl.BlockSpec((BLK, KV_W), cur(4)), pl.BlockSpec((BLK, KV_W), prev(4)),
                 pl.BlockSpec((BLK, KV_W), cur(5)), pl.BlockSpec((BLK, KV_W), prev(5)),
                 pl.BlockSpec((BLK, ATT_W), cur(0)), pl.BlockSpec((BLK, ATT_W), cur(0)), small((8, 128)),
                 pl.BlockSpec((None, N_KV, GQA * BLK, 2 * BLK), lambda b, j: (jnp.minimum(j, 1), 0, 0, 0))],
                [SDS((T, ATT_W), f32), SDS((T, KV_W), f32), SDS((T, KV_W), f32), SDS((8, 128), f32)],
                [pl.BlockSpec((BLK, ATT_W), cur(0)), pl.BlockSpec((BLK, KV_W), prev(0)),
                 pl.BlockSpec((BLK, KV_W), prev(0)), small((8, 128))],
                scratch=[pltpu.VMEM((BLK, KV_W), f32)] * 2, exchange=exchange)


def mem_conv_bwd(d_mem_out, mem_out, d_conv_out, proj, qkv, km, vm, conv_w8, pk, S, tm, exchange):
    T = d_mem_out.shape[0]
    NM = km.shape[0] // (T // S)

    def body(dmo_ref, mo_ref, dco_ref, ch_ref, cb_ref, cc_ref, chh_ref, cch_ref, qm_ref, km_ref, vm_ref, cw_ref,
             pk_ref, dqm_ref, dkm_ref, dvm_ref, dcb_ref, dcv_ref, dcw_ref, dcbias_ref):
        i = pl.program_id(0)
        first = (i * tm) % S == 0

        @pl.when(i == 0)
        def _():
            dcw_ref[...] = jnp.zeros_like(dcw_ref)
            dcbias_ref[...] = jnp.zeros_like(dcbias_ref)

        @pl.when(first)
        def _():
            dkm_ref[...] = jnp.zeros_like(dkm_ref)
            dvm_ref[...] = jnp.zeros_like(dvm_ref)

        qm, kmv, vmv, dmo, mo = qm_ref[...], km_ref[...], vm_ref[...], dmo_ref[...], mo_ref[...]
        ones_w = jnp.ones((NM, NM), MXU)
        for h in range(N_MEMH):
            qh, kh, vh, e = _mem_head(qm, kmv, vmv, h)
            eb = _c(e)
            doh = dmo[:, h * HD:(h + 1) * HD]
            delta = _rowsum_mxu(doh * mo[:, h * HD:(h + 1) * HD], NM)
            dp = _nt(_c(doh), vh)
            inv_w = 1.0 / _nn(eb, ones_w)
            ds = _c(e * inv_w * (dp - delta) * (HD ** -0.5))
            dvm_ref[:, pl.ds(h * HD, HD)] += _tn(eb, _c(doh * inv_w[:, :HD]))
            dkm_ref[:, pl.ds(h * HD, HD)] += _tn(ds, qh)
            dqm_ref[:, pl.ds(h * HD, HD)] = _nn(ds, kh)

        u = cc_ref[...] * ch_ref[...]
        uh = jnp.where(first, 0.0, cch_ref[...] * chh_ref[...])
        u1, u2 = _conv_taps(u, uh)
        conv = cw_ref[0:1, :] * u2 + cw_ref[1:2, :] * u1 + cw_ref[2:3, :] * u + _small(pk_ref, "conv_b")
        dy = dco_ref[...]
        dcb_ref[...] = dy * conv
        dcv = dy * cb_ref[...]
        dcv_ref[...] = dcv
        dcbias_ref[...] += jnp.sum(dcv, axis=0, keepdims=True)
        dcw_ref[0:1, :] += jnp.sum(dcv * u2, axis=0, keepdims=True)
        dcw_ref[1:2, :] += jnp.sum(dcv * u1, axis=0, keepdims=True)
        dcw_ref[2:3, :] += jnp.sum(dcv * u, axis=0, keepdims=True)

    tile = lambda w, col: pl.BlockSpec((tm, w), lambda i: (i, col))
    halo = lambda col: pl.BlockSpec((8, CONV_W), lambda i: (jnp.maximum(i * (tm // 8) - 1, 0), col))
    seq = pl.BlockSpec((NM, MEM_W), lambda i: ((i * tm) // S, 0))
    const = lambda shape: pl.BlockSpec(shape, lambda i: (0, 0))
    return _run("mem_conv_bwd", body, (T // tm,),
                [d_mem_out, mem_out, d_conv_out, proj, proj, proj, proj, proj, qkv, km, vm, conv_w8, pk],
                [tile(MEM_W, 0), tile(MEM_W, 0), tile(CONV_W, 0), tile(CONV_W, 3), tile(CONV_W, 4), tile(CONV_W, 5),
                 halo(3), halo(5), tile(MEM_W, 3), seq, seq, VM, VM],
                [SDS((T, MEM_W), f32), SDS(km.shape, f32), SDS(km.shape, f32),
                 SDS((T, CONV_W), f32), SDS((T, CONV_W), f32), SDS((8, CONV_W), f32), SDS((1, CONV_W), f32)],
                [tile(MEM_W, 0), seq, seq, tile(CONV_W, 0), tile(CONV_W, 0), const((8, CONV_W)), const((1, CONV_W))],
                vmem_mib=48, exchange=exchange)


def in_proj_bwd(dqn, dkn, dv, dcb, dcv, dqmn, proj, conv_w8, xn, x2d, dx1, pk, winT, S, tm, stages, ws, ms, vs):
    T, D = x2d.shape
    P = winT.shape[0]
    last_blk = T // 8 - 1
    n = len(stages)
    nsteps = T // tm
    tile_w = ws[0].shape[1] // (nsteps // 2)
    turn = [e * 2 // n for e in range(n)]

    def body(dq_ref, dk_ref, dv_ref, dcb_ref, dcv_ref, dcvn_ref, dqm_ref, qa_ref, ka_ref, ch_ref, cc_ref, qma_ref,
             cw_ref, xn_ref, x_ref, dx1_ref, pk_ref, w_ref, *rest):
        st, aw, am, av = (rest[k * n:(k + 1) * n] for k in range(4))
        dx_ref, dw_ref, dg_ref, dqg_ref, dkg_ref, dmqg_ref = rest[4 * n:4 * n + 6]
        aouts = rest[4 * n + 6:]
        i = pl.program_id(0)

        for parity in range(2):
            @pl.when(i % 2 == parity)
            def _(parity=parity):
                for e in range(n):
                    if turn[e] == parity:
                        g = jnp.concatenate([_sum_chips(st[e].at[0]), _sum_chips(st[e].at[1])], axis=0)
                        d, mm, vv = _adamw_math(aw[e][...], g, am[e][...], av[e][...])
                        for k, val in enumerate((g, d, mm, vv)):
                            aouts[4 * e + k][...] = val

        @pl.when(i == 0)
        def _():
            dw_ref[...] = jnp.zeros_like(dw_ref)
            dg_ref[...] = jnp.zeros_like(dg_ref)
            dqg_ref[...] = jnp.zeros_like(dqg_ref)
            dkg_ref[...] = jnp.zeros_like(dkg_ref)
            dmqg_ref[...] = jnp.zeros_like(dmqg_ref)

        dqa, gq = _heads_norm_bwd(dq_ref[...], qa_ref[...], _small(pk_ref, "q_norm"))
        dka, gk = _heads_norm_bwd(dk_ref[...], ka_ref[...], _small(pk_ref, "k_norm"))
        dqma, gmq = _heads_norm_bwd(dqm_ref[...], qma_ref[...], _small(pk_ref, "mem_q_norm"))
        dqg_ref[...] += gq
        dkg_ref[...] += gk
        dmqg_ref[...] += gmq

        last = ((i + 1) * tm) % S == 0
        dcv = dcv_ref[...]
        nxt = jnp.where(last, 0.0, dcvn_ref[...])
        row = lax.broadcasted_iota(jnp.int32, dcv.shape, 0)
        n1 = jnp.where(row == tm - 1, nxt[0:1, :], pltpu.roll(dcv, tm - 1, 0))
        n2 = jnp.where(row == tm - 2, nxt[0:1, :], jnp.where(row == tm - 1, nxt[1:2, :], pltpu.roll(dcv, tm - 2, 0)))
        du = cw_ref[2:3, :] * dcv + cw_ref[1:2, :] * n1 + cw_ref[0:1, :] * n2
        d_proj = jnp.concatenate([_c(dqa), _c(dka), _c(dv_ref[...]), _c(du * cc_ref[...]),
                                  _c(dcb_ref[...]), _c(du * ch_ref[...]), _c(dqma)], axis=1)
        dw_ref[...] += _tn(d_proj, xn_ref[...])
        xv = x_ref[...]
        dv_, dg = _norm_bwd(_nn(d_proj, w_ref[...]), xv, _rstd(xv), _small(pk_ref, "norm_mix"))
        dx_ref[...] = dx1_ref[...] + dv_
        dg_ref[...] += dg

    tile = lambda w, col=0: pl.BlockSpec((tm, w), lambda i: (i, col))
    nhalo = pl.BlockSpec((8, CONV_W), lambda i: (jnp.minimum((i + 1) * (tm // 8), last_blk), 0))
    const = lambda shape: pl.BlockSpec(shape, lambda i: (0, 0))
    st_specs = [pl.BlockSpec((2, 4, s.shape[2], tile_w), lambda i: (0, 0, 0, i // 2)) for s in stages]
    w_specs = [pl.BlockSpec((w.shape[0], tile_w), lambda i: (0, i // 2)) for w in ws]
    res = _run("in_proj_bwd", body, (nsteps,),
               [dqn, dkn, dv, dcb, dcv, dcv, dqmn, proj, proj, proj, proj, proj, conv_w8, xn, x2d, dx1, pk, winT]
               + list(stages) + list(ws) + list(ms) + list(vs),
               [tile(ATT_W), tile(KV_W), tile(KV_W), tile(CONV_W), tile(CONV_W), nhalo, tile(MEM_W),
                tile(ATT_W, 0), tile(KV_W, 4), tile(CONV_W, 3), tile(CONV_W, 5), tile(MEM_W, 6), VM,
                tile(D), tile(D), tile(D), VM, VM] + st_specs + w_specs * 3,
               [SDS((T, D), f32), SDS((P, D), f32), SDS((1, D), f32), SDS((1, HD), f32), SDS((1, HD), f32),
                SDS((1, HD), f32)] + [SDS(w.shape, f32) for w in ws for _ in range(4)],
               [tile(D), pl.BlockSpec((P, D), lambda i: (0, 0)), const((1, D)), const((1, HD)), const((1, HD)),
                const((1, HD))] + [s for s in w_specs for _ in range(4)],
               vmem_mib=56)
    return res[:6], [res[6 + 4 * e:10 + 4 * e] for e in range(n)]


def mem_kv_bwd(dkm, dvm, kv, memn, mem2d, pk, wmkv):
    def body(dkm_ref, dvm_ref, kv_ref, mn_ref, m_ref, pk_ref, w_ref, dw_ref, dg_ref, dkg_ref):
        dkk, dkg = _heads_norm_bwd(dkm_ref[...], kv_ref[:, :MEM_W], _small(pk_ref, "mem_k_norm"))
        dkg_ref[...] = dkg
        dkv = _c(jnp.concatenate([dkk, dvm_ref[...]], axis=1))
        dw_ref[...] = _tn(mn_ref[...], dkv)
        mv = m_ref[...]
        dg_ref[...] = jnp.sum(_nt(dkv, w_ref[...]) * mv * _rstd(mv), axis=0, keepdims=True)

    return _run("mem_kv_bwd", body, (), [dkm, dvm, kv, memn, mem2d, pk, wmkv], [VM] * 7,
                [SDS(wmkv.shape, f32), SDS((1, mem2d.shape[1]), f32), SDS((1, HD), f32)], [VM] * 3, vmem_mib=40)


def _halves_view(g):
    return g.reshape(4, 2, g.shape[0] // 8, g.shape[1])


def kernel(x, mem, norm_mix, w_in, q_norm, k_norm, attn_sinks, conv_w, conv_b, norm_mem, w_mem_kv, mem_q_norm, mem_k_norm, out_norm_attn, out_norm_conv, out_norm_mem, w_out, norm_ffn, w_gate, w_up, w_down, loss_target, m_norm_mix, m_w_in, m_q_norm, m_k_norm, m_attn_sinks, m_conv_w, m_conv_b, m_norm_mem, m_w_mem_kv, m_mem_q_norm, m_mem_k_norm, m_out_norm_attn, m_out_norm_conv, m_out_norm_mem, m_w_out, m_norm_ffn, m_w_gate, m_w_up, m_w_down, v_norm_mix, v_w_in, v_q_norm, v_k_norm, v_attn_sinks, v_conv_w, v_conv_b, v_norm_mem, v_w_mem_kv, v_mem_q_norm, v_mem_k_norm, v_out_norm_attn, v_out_norm_conv, v_out_norm_mem, v_w_out, v_norm_ffn, v_w_gate, v_w_up, v_w_down):
    BL, S, D = x.shape
    T = BL * S
    TM = 256
    TM_BIG = min(512, S)
    _, _, ci = _place()
    cidx = ci.reshape(1).astype(jnp.int32)
    w_small = dict(norm_mix=norm_mix, norm_mem=norm_mem, norm_ffn=norm_ffn, out_norm_attn=out_norm_attn,
                   out_norm_conv=out_norm_conv, out_norm_mem=out_norm_mem, conv_w=conv_w, conv_b=conv_b, q_norm=q_norm,
                   k_norm=k_norm, mem_q_norm=mem_q_norm, mem_k_norm=mem_k_norm, attn_sinks=attn_sinks)
    m_small = dict(norm_mix=m_norm_mix, norm_mem=m_norm_mem, norm_ffn=m_norm_ffn, out_norm_attn=m_out_norm_attn,
                   out_norm_conv=m_out_norm_conv, out_norm_mem=m_out_norm_mem, conv_w=m_conv_w, conv_b=m_conv_b,
                   q_norm=m_q_norm, k_norm=m_k_norm, mem_q_norm=m_mem_q_norm, mem_k_norm=m_mem_k_norm,
                   attn_sinks=m_attn_sinks)
    v_small = dict(norm_mix=v_norm_mix, norm_mem=v_norm_mem, norm_ffn=v_norm_ffn, out_norm_attn=v_out_norm_attn,
                   out_norm_conv=v_out_norm_conv, out_norm_mem=v_out_norm_mem, conv_w=v_conv_w, conv_b=v_conv_b,
                   q_norm=v_q_norm, k_norm=v_k_norm, mem_q_norm=v_mem_q_norm, mem_k_norm=v_mem_k_norm,
                   attn_sinks=v_attn_sinks)
    pk = _pack_small(w_small)

    rowblocks = lambda a, b, c, d, e, f: [a[0].T, b[0].T, c[0].T, d[0], e[0], f[0]]
    w_rb = rowblocks(w_in, w_gate, w_up, w_down, w_out, w_mem_kv)
    m_rb = rowblocks(m_w_in, m_w_gate, m_w_up, m_w_down, m_w_out, m_w_mem_kv)
    v_rb = rowblocks(v_w_in, v_w_gate, v_w_up, v_w_down, v_w_out, v_w_mem_kv)
    (winT_s,) = prep_weights("prep_w_in", w_rb[:1])
    cw_pad = jnp.zeros((8, 128), f32).at[:3, :HD].set(conv_w[0])
    (wgT_s, wuT_s, wd_s, wout_s, wmkv_s), (winT, cw_all) = prep_weights(
        "gather_w_in", w_rb[1:], _together([gather_two_legs([winT_s]), gather_exchange([cw_pad], [False])]))
    conv_w_full = jnp.transpose(cw_all.reshape(4, 8, 128)[:, :3, :HD], (1, 0, 2)).reshape(3, CONV_W)
    conv_w8 = jnp.zeros((8, CONV_W), f32).at[:3].set(conv_w_full)
    sink_rows = jnp.broadcast_to(attn_sinks.reshape(N_Q, 1), (N_Q, 128))

    x2d = x.reshape(T, D)
    mem2d = mem.reshape(-1, D)
    (xn, proj, qkv), near1 = in_proj_fwd(x2d, pk, winT, TM_BIG, gather_near_exchange([wgT_s, wout_s, wmkv_s], relay_early=1))
    (attn_out,), (wgT, wout, wmkv, *near2) = attn_fwd(
        qkv, sink_rows, BL, S, _together([gather_far_exchange(near1, relay_early=2), gather_near_exchange([wuT_s, wd_s])]))
    memn, kv, km, vm = mem_kv_fwd(mem2d, pk, wmkv)
    (conv_out, mem_out, merged, x1, h), (wuT, wd) = mixer_tail_fwd(
        x2d, attn_out, proj, qkv, km, vm, conv_w8, pk, wout, S, TM_BIG, gather_far_exchange(near2, relay_early=2))

    dx1, dx2b, act, d_gate, d_up, loss8, d_norm_ffn = ffn_fwd_bwd(h, x1, loss_target.reshape(T, D), wgT, wuT, wd, pk, TM)
    F = wd.shape[0]
    g_wd = matmul_tn(act, dx2b, "dw_down", F // 2, min(T, 1024))
    g_wgT = matmul_tn(d_gate, h, "dw_gate", F // 2, min(T, 1024))
    g_wuT = matmul_tn(d_up, h, "dw_up", F // 2, min(T, 1024))

    d_attn, d_conv_out, d_mem_out, g_wout, d_gains = out_proj_bwd(dx1, merged, attn_out, conv_out, mem_out, pk, wout, TM_BIG)
    late = [_halves_view(g) for g in (g_wgT, g_wuT, g_wd, g_wout)]
    (dqmn, dkm, dvm, dcb, dcv, d_cw8, d_cbias), late_sib = mem_conv_bwd(
        d_mem_out, mem_out, d_conv_out, proj, qkv, km, vm, conv_w8, pk, S, min(1024, S), halves_exchange(late))
    late_part = add_halves(cidx, late, late_sib, "grad_add_halves_ffn")
    (dqn, dkn, dv, d_sink8), late_stage = attn_bwd(qkv, d_attn, attn_out, sink_rows, BL, S, scatter_exchange(late_part))
    (g_x, g_winT, d_norm_mix, d_qg, d_kg, d_mqg), late_res = in_proj_bwd(
        dqn, dkn, dv, dcb, dcv, dqmn, proj, conv_w8, xn, x2d, dx1, pk, winT, S, TM,
        late_stage, w_rb[1:5], m_rb[1:5], v_rb[1:5])
    g_wmkv, d_norm_mem, d_mkg = mem_kv_bwd(dkm, dvm, kv, memn, mem2d, pk, wmkv)

    tot, tail_stage = tail_reduce(d_norm_mix, d_norm_mem, d_norm_ffn, d_gains, d_cw8, d_cbias, d_qg, d_kg, d_mqg, d_mkg,
                                  d_sink8, loss8, [_halves_view(g) for g in (g_winT, g_wmkv)])
    loss = tot[5, 384]
    tail_res, _ = adamw_big("adamw_tail", tail_stage, [w_rb[0], w_rb[5]], [m_rb[0], m_rb[5]], [v_rb[0], v_rb[5]], 4)
    res = {"w_in": [a.T[None] for a in tail_res[0]], "w_gate": [a.T[None] for a in late_res[0]],
           "w_up": [a.T[None] for a in late_res[1]], "w_down": [a[None] for a in late_res[2]],
           "w_out": [a[None] for a in late_res[3]], "w_mem_kv": [a[None] for a in tail_res[1]]}
    res.update(adamw_small(tot, pk, _pack_small(m_small), _pack_small(v_small), {k: w_small[k].shape for k in SMALL}))

    order = ["norm_mix", "w_in", "q_norm", "k_norm", "attn_sinks", "conv_w", "conv_b", "norm_mem", "w_mem_kv",
             "mem_q_norm", "mem_k_norm", "out_norm_attn", "out_norm_conv", "out_norm_mem", "w_out", "norm_ffn",
             "w_gate", "w_up", "w_down"]
    return (loss, g_x.reshape(BL, S, D), *[res[n][0] for n in order], *[res[n][1] for n in order],
            *[res[n][2] for n in order], *[res[n][3] for n in order])
```

```python
import collections
import functools

import jax
import jax.numpy as jnp
import numpy as np
from jax import lax
from jax.experimental import pallas as pl
from jax.experimental.pallas import tpu as pltpu

f32 = jnp.float32
MXU = jnp.bfloat16
WIRE = jnp.bfloat16
EPS = 1e-6
NEG = -1e30
HD = 64
BLK = 128
N_Q, N_KV, N_MEMH = 8, 2, 4
GQA = N_Q // N_KV
ATT_W, KV_W, CONV_W, MEM_W = 512, 128, 256, 256
VMEM_MIB = 1024 * 1024
ADAM_LR, ADAM_B1, ADAM_B2, ADAM_EPS, ADAM_WD, ADAM_STEP = 0.001, 0.9, 0.999, 1e-08, 0.01, 10

MESH = pl.DeviceIdType.MESH
VM = pl.BlockSpec(memory_space=pltpu.VMEM)
ANY = pl.BlockSpec(memory_space=pl.ANY)
SDS = jax.ShapeDtypeStruct
DMA = pltpu.SemaphoreType.DMA


def _c(v):
    return v.astype(MXU)


def _nn(a, b):
    return lax.dot_general(a, b, (((1,), (0,)), ((), ())), preferred_element_type=f32)


def _nt(a, b):
    return lax.dot_general(a, b, (((1,), (1,)), ((), ())), preferred_element_type=f32)


def _tn(a, b):
    return lax.dot_general(a, b, (((0,), (0,)), ((), ())), preferred_element_type=f32)


def _rstd(v):
    return lax.rsqrt(jnp.mean(v * v, axis=-1, keepdims=True) + EPS)


def _norm_bwd(dy, v, r, g):
    dyg = dy * g
    dv = r * dyg - v * (r * r * r) * jnp.mean(dyg * v, axis=-1, keepdims=True)
    return dv, jnp.sum(dy * v * r, axis=0, keepdims=True)


def _split3(v):
    hi = _c(v)
    r1 = v - hi.astype(f32)
    mid = _c(r1)
    return hi, mid, _c(r1 - mid.astype(f32))


def _rowsum_mxu(v, width):
    ones = jnp.ones((v.shape[1], width), MXU)
    return sum(_nn(a, ones) for a in _split3(v))


def _seg_sums(v):
    r = lax.broadcasted_iota(jnp.int32, (2 * HD, 2 * HD), 0) // HD
    c = lax.broadcasted_iota(jnp.int32, (2 * HD, 2 * HD), 1) // HD
    bd = (r == c).astype(MXU)
    outs = []
    for b in range(v.shape[1] // (2 * HD)):
        outs.append(sum(_nn(a, bd) for a in _split3(v[:, b * 2 * HD:(b + 1) * 2 * HD])))
    return outs[0] if len(outs) == 1 else jnp.concatenate(outs, axis=1)


def _lanes(g, width):
    return jnp.concatenate([g] * (width // HD), axis=1)


def _heads_rstd(v):
    return lax.rsqrt(_seg_sums(v * v) * (1.0 / HD) + EPS)


def _heads_norm_bwd(dy, v, g):
    r = _heads_rstd(v)
    gl = _lanes(g, v.shape[1])
    dyg = dy * gl
    dv = r * dyg - v * (r * r * r) * (_seg_sums(dyg * v) * (1.0 / HD))
    dgl = jnp.sum(dy * v * r, axis=0, keepdims=True)
    return dv, sum(dgl[:, s * HD:(s + 1) * HD] for s in range(v.shape[1] // HD))


def _exp_scores(s, extra=None):
    m = jnp.max(s, axis=-1, keepdims=True)
    if extra is None:
        return jnp.exp(s - m), None
    m = jnp.maximum(m, extra)
    return jnp.exp(s - m), jnp.exp(extra - m)


def _place():
    return lax.axis_index("x"), lax.axis_index("y"), lax.axis_index("c")


SMALL_AT = {"norm_mix": (0, 0, 1024), "norm_mem": (1, 0, 1024), "norm_ffn": (2, 0, 1024),
            "out_norm_attn": (3, 0, ATT_W), "out_norm_conv": (3, ATT_W, CONV_W), "out_norm_mem": (3, ATT_W + CONV_W, MEM_W),
            "conv_b": (4, 3 * CONV_W, CONV_W), "q_norm": (5, 0, HD), "k_norm": (5, HD, HD), "mem_q_norm": (5, 2 * HD, HD),
            "mem_k_norm": (5, 3 * HD, HD), "attn_sinks": (5, 256, N_Q)}
SMALL = ("norm_mix", "norm_mem", "norm_ffn", "out_norm_attn", "out_norm_conv", "out_norm_mem", "conv_w", "conv_b",
         "q_norm", "k_norm", "mem_q_norm", "mem_k_norm", "attn_sinks")


def _small(pk_ref, name):
    r, c0, w = SMALL_AT[name]
    return pk_ref[r:r + 1, c0:c0 + w]


def _pack_small(d):
    z = lambda n: jnp.zeros((1, n), f32)
    row3 = jnp.concatenate([d["out_norm_attn"], d["out_norm_conv"], d["out_norm_mem"]], axis=1)
    row4 = jnp.concatenate([d["conv_w"].reshape(1, 3 * HD), z(3 * CONV_W - 3 * HD), d["conv_b"]], axis=1)
    row5 = jnp.concatenate([d["q_norm"], d["k_norm"], d["mem_q_norm"], d["mem_k_norm"], d["attn_sinks"],
                            z(1024 - 4 * HD - N_Q)], axis=1)
    return jnp.concatenate([d["norm_mix"], d["norm_mem"], d["norm_ffn"], row3, row4, row5, z(1024), z(1024)], axis=0)


def _other_chips(x, y):
    return [(1 - x, y), (x, 1 - y), (1 - x, 1 - y)]


Exchange = collections.namedtuple("Exchange", "ins outs sems start finish relay relay_steps_before_end aliases",
                                  defaults=(None, 0, {}))


def _together(exchanges):
    def parts(refs, key):
        out, at = [], 0
        for ex in exchanges:
            out.append(refs[at:at + len(getattr(ex, key))])
            at += len(getattr(ex, key))
        return out

    def phase(name):
        def run(xa, xo, xs):
            for ex, a, o, s in zip(exchanges, parts(xa, "ins"), parts(xo, "outs"), parts(xs, "sems")):
                if getattr(ex, name) is not None:
                    getattr(ex, name)(a, o, s)
        return run

    aliases, ai, ao = {}, 0, 0
    for ex in exchanges:
        aliases.update({ai + i: ao + o for i, o in ex.aliases.items()})
        ai, ao = ai + len(ex.ins), ao + len(ex.outs)
    return Exchange([a for ex in exchanges for a in ex.ins], [o for ex in exchanges for o in ex.outs],
                    [s for ex in exchanges for s in ex.sems], phase("start"), phase("finish"), phase("relay"),
                    max(ex.relay_steps_before_end for ex in exchanges), aliases)


def _run(name, body, grid, ins, in_specs, out_shape, out_specs, scratch=(), vmem_mib=32, exchange=None):
    ins, in_specs, out_shape, out_specs, scratch = list(ins), list(in_specs), list(out_shape), list(out_specs), list(scratch)
    ni, no, ns = len(ins), len(out_shape), len(scratch)
    ex = exchange
    if ex is not None:
        nxi, nxo = len(ex.ins), len(ex.outs)

    def call_body(*refs):
        if ex is None:
            body(*refs)
            return
        a, xa = refs[:ni], refs[ni:ni + nxi]
        o, xo = refs[ni + nxi:ni + nxi + no], refs[ni + nxi + no:ni + nxi + no + nxo]
        s, xs = refs[ni + nxi + no + nxo:ni + nxi + no + nxo + ns], refs[ni + nxi + no + nxo + ns:]
        if grid:
            first = functools.reduce(jnp.logical_and, [pl.program_id(d) == 0 for d in range(len(grid))])
            last = functools.reduce(jnp.logical_and, [pl.program_id(d) == grid[d] - 1 for d in range(len(grid))])
            pl.when(first)(lambda: ex.start(xa, xo, xs))
            body(*a, *o, *s)
            if ex.relay is not None:
                early = functools.reduce(jnp.logical_and, [pl.program_id(d) == grid[d] - 1 for d in range(len(grid) - 1)],
                                         pl.program_id(len(grid) - 1) == max(grid[-1] - 1 - ex.relay_steps_before_end, 0))
                pl.when(early)(lambda: ex.relay(xa, xo, xs))
            pl.when(last)(lambda: ex.finish(xa, xo, xs))
        else:
            ex.start(xa, xo, xs)
            if body is not None:
                body(*a, *o, *s)
            if ex.relay is not None:
                ex.relay(xa, xo, xs)
            ex.finish(xa, xo, xs)

    kw = dict(grid=grid) if grid else {}
    if ex is not None:
        if ex.aliases:
            kw["input_output_aliases"] = {ni + i: no + o for i, o in ex.aliases.items()}
        ins, in_specs = ins + list(ex.ins), in_specs + [ANY] * nxi
        out_shape, out_specs = out_shape + list(ex.outs), out_specs + [ANY] * nxo
        scratch = scratch + list(ex.sems)
    res = pl.pallas_call(
        call_body, name=name, out_shape=out_shape, in_specs=in_specs, out_specs=out_specs, scratch_shapes=scratch,
        compiler_params=pltpu.CompilerParams(dimension_semantics=("arbitrary",) * len(grid) if grid else None,
                                             vmem_limit_bytes=vmem_mib * VMEM_MIB), **kw)(*ins)
    res = list(res)
    return (res[:no], res[no:]) if ex is not None else res


def _remote(src, dst, ssem, rsem, dev):
    return pltpu.make_async_remote_copy(src_ref=src, dst_ref=dst, send_sem=ssem, recv_sem=rsem,
                                        device_id=dev, device_id_type=MESH)


def gather_exchange(shards, split, relay_early=0):
    n = len(shards)

    def rows(ref, e, kk, half=None):
        R = shards[e].shape[0]
        if half is None:
            return ref.at[pl.ds(pl.multiple_of(kk * R, 8), R)]
        return ref.at[pl.ds(pl.multiple_of(kk * R + half * (R // 2), 8), R // 2)]

    def ici(src, dst, sm, e, j, chip_j, x, y, c):
        k = 2 * x + y
        if split[e]:
            s = src[e].at[pl.ds(pl.multiple_of(c * (shards[e].shape[0] // 2), 8), shards[e].shape[0] // 2)]
            return _remote(s, rows(dst[e], e, k, c), sm[0].at[6 * e + j], sm[1].at[6 * e + j], (*chip_j, c))
        return _remote(src[e], rows(dst[e], e, k), sm[0].at[6 * e + j], sm[1].at[6 * e + j], (*chip_j, c))

    def landed(dst, e, chip_j, c):
        kj = 2 * chip_j[0] + chip_j[1]
        return rows(dst[e], e, kj, c) if split[e] else rows(dst[e], e, kj)

    def forward(dst, sm, e, j, chip_j, x, y, c, sender_c):
        kj = 2 * chip_j[0] + chip_j[1]
        r = rows(dst[e], e, kj, sender_c)
        return _remote(r, r, sm[0].at[6 * e + 3 + j], sm[1].at[6 * e + 3 + j], (x, y, 1 - c))

    def local(src, dst, sm, e, x, y):
        return pltpu.make_async_copy(src[e], rows(dst[e], e, 2 * x + y), sm[2].at[e])

    def start(src, dst, sm):
        x, y, c = _place()
        for e in range(n):
            local(src, dst, sm, e, x, y).start()
            for j, chip_j in enumerate(_other_chips(x, y)):
                ici(src, dst, sm, e, j, chip_j, x, y, c).start()

    def relay(src, dst, sm):
        x, y, c = _place()
        for e in range(n):
            for j, chip_j in enumerate(_other_chips(x, y)):
                r = landed(dst, e, chip_j, c)
                _remote(r, r, sm[0].at[6 * e + j], sm[1].at[6 * e + j], (*chip_j, c)).wait_recv()
                if split[e]:
                    forward(dst, sm, e, j, chip_j, x, y, c, c).start()

    def finish(src, dst, sm):
        x, y, c = _place()
        chips = _other_chips(x, y)
        for e in range(n):
            for j, chip_j in enumerate(chips):
                if split[e]:
                    forward(dst, sm, e, j, chip_j, x, y, c, 1 - c).wait_recv()
        for e in range(n):
            for j, chip_j in enumerate(chips):
                ici(src, dst, sm, e, j, chip_j, x, y, c).wait_send()
                if split[e]:
                    forward(dst, sm, e, j, chip_j, x, y, c, c).wait_send()
            local(src, dst, sm, e, x, y).wait()

    outs = [SDS((4 * s.shape[0], s.shape[1]), s.dtype) for s in shards]
    return Exchange(list(shards), outs, [DMA((6 * n,)), DMA((6 * n,)), DMA((n,))], start, finish, relay, relay_early)


def _block_rows(ref, R, kk, half, quarter=None):
    hr = R // 2
    if quarter is None:
        return ref.at[pl.ds(pl.multiple_of(kk * R + half * hr, 8), hr)]
    return ref.at[pl.ds(pl.multiple_of(kk * R + half * hr + quarter * (hr // 2), 8), hr // 2)]


def gather_near_exchange(shards, relay_early=0):
    n = len(shards)
    R = [s.shape[0] for s in shards]

    def ici(src, dst, sm, e, j, chip_j, x, y, c):
        half = src[e].at[pl.ds(pl.multiple_of(c * (R[e] // 2), 8), R[e] // 2)]
        return _remote(half, _block_rows(dst[e], R[e], 2 * x + y, c), sm[0].at[4 * e + j], sm[1].at[4 * e + j], (*chip_j, c))

    def forward(dst, sm, e, j, chip_j, x, y, c, sender_c):
        r = _block_rows(dst[e], R[e], 2 * chip_j[0] + chip_j[1], sender_c)
        return _remote(r, r, sm[0].at[4 * e + 2 + j], sm[1].at[4 * e + 2 + j], (x, y, 1 - c))

    def local(src, dst, sm, e, x, y):
        return pltpu.make_async_copy(src[e], dst[e].at[pl.ds(pl.multiple_of((2 * x + y) * R[e], 8), R[e])], sm[2].at[e])

    def start(src, dst, sm):
        x, y, c = _place()
        for e in range(n):
            local(src, dst, sm, e, x, y).start()
            for j, chip_j in enumerate(_other_chips(x, y)[:2]):
                ici(src, dst, sm, e, j, chip_j, x, y, c).start()

    def relay(src, dst, sm):
        x, y, c = _place()
        for e in range(n):
            for j, chip_j in enumerate(_other_chips(x, y)[:2]):
                r = _block_rows(dst[e], R[e], 2 * chip_j[0] + chip_j[1], c)
                _remote(r, r, sm[0].at[4 * e + j], sm[1].at[4 * e + j], (*chip_j, c)).wait_recv()
                forward(dst, sm, e, j, chip_j, x, y, c, c).start()

    def finish(src, dst, sm):
        x, y, c = _place()
        near = _other_chips(x, y)[:2]
        for e in range(n):
            for j, chip_j in enumerate(near):
                forward(dst, sm, e, j, chip_j, x, y, c, 1 - c).wait_recv()
        for e in range(n):
            for j, chip_j in enumerate(near):
                ici(src, dst, sm, e, j, chip_j, x, y, c).wait_send()
                forward(dst, sm, e, j, chip_j, x, y, c, c).wait_send()
            local(src, dst, sm, e, x, y).wait()

    outs = [SDS((4 * s.shape[0], s.shape[1]), s.dtype) for s in shards]
    return Exchange(list(shards), outs, [DMA((4 * n,)), DMA((4 * n,)), DMA((n,))], start, finish, relay, relay_early)


def gather_far_exchange(bufs, relay_early=0):
    n = len(bufs)
    R = [b.shape[0] // 4 for b in bufs]

    def send(src, dst, sm, e, j, x, y, c):
        to, of = _other_chips(x, y)[j], _other_chips(x, y)[1 - j]
        kk = 2 * of[0] + of[1]
        return _remote(_block_rows(src[e], R[e], kk, c, j), _block_rows(dst[e], R[e], kk, c, j),
                       sm[0].at[4 * e + j], sm[1].at[4 * e + j], (*to, c))

    def landed(dst, e, j, x, y, half):
        return _block_rows(dst[e], R[e], 2 * (1 - x) + (1 - y), half, j)

    def forward(dst, sm, e, j, x, y, c, sender_c):
        r = landed(dst, e, j, x, y, sender_c)
        return _remote(r, r, sm[0].at[4 * e + 2 + j], sm[1].at[4 * e + 2 + j], (x, y, 1 - c))

    def start(src, dst, sm):
        x, y, c = _place()
        for e in range(n):
            for j in range(2):
                send(src, dst, sm, e, j, x, y, c).start()

    def relay(src, dst, sm):
        x, y, c = _place()
        for e in range(n):
            for j in range(2):
                r = landed(dst, e, j, x, y, c)
                _remote(r, r, sm[0].at[4 * e + j], sm[1].at[4 * e + j], (*_other_chips(x, y)[j], c)).wait_recv()
                forward(dst, sm, e, j, x, y, c, c).start()

    def finish(src, dst, sm):
        x, y, c = _place()
        for e in range(n):
            for j in range(2):
                forward(dst, sm, e, j, x, y, c, 1 - c).wait_recv()
        for e in range(n):
            for j in range(2):
                send(src, dst, sm, e, j, x, y, c).wait_send()
                forward(dst, sm, e, j, x, y, c, c).wait_send()

    outs = [SDS(b.shape, b.dtype) for b in bufs]
    return Exchange(list(bufs), outs, [DMA((4 * n,)), DMA((4 * n,))], start, finish, relay, relay_early,
                    {i: i for i in range(n)})


def gather_two_legs(shards):
    near = gather_near_exchange(shards)
    far = gather_far_exchange(near.outs)

    def finish(src, dst, sm):
        near.relay(src, dst, sm[:3])
        near.finish(src, dst, sm[:3])
        far.start(dst, dst, sm[3:])
        far.relay(dst, dst, sm[3:])
        far.finish(dst, dst, sm[3:])

    return Exchange(near.ins, near.outs, list(near.sems) + list(far.sems),
                    lambda src, dst, sm: near.start(src, dst, sm[:3]), finish)


def halves_exchange(grads):
    n = len(grads)

    def copy(g, st, sm, e, x, y, c):
        return _remote(g[e].at[:, 1 - c], st[e], sm[0].at[e], sm[1].at[e], (x, y, 1 - c))

    def start(g, st, sm):
        x, y, c = _place()
        for e in range(n):
            copy(g, st, sm, e, x, y, c).start()

    def finish(g, st, sm):
        x, y, c = _place()
        for e in range(n):
            copy(g, st, sm, e, x, y, c).wait()

    outs = [SDS((4,) + a.shape[2:], a.dtype) for a in grads]
    return Exchange(list(grads), outs, [DMA((n,)), DMA((n,))], start, finish)


def scatter_exchange(parts):
    n = len(parts)

    def ici(p, st, sm, e, j, chip_j, x, y, c):
        k, kj = 2 * x + y, 2 * chip_j[0] + chip_j[1]
        return _remote(p[e].at[kj], st[e].at[c, k], sm[0].at[8 * e + j], sm[1].at[8 * e + j], (*chip_j, c))

    def own(p, st, sm, e, x, y, c):
        k = 2 * x + y
        return _remote(p[e].at[k], st[e].at[c, k], sm[0].at[8 * e + 3], sm[1].at[8 * e + 3], (x, y, 1 - c))

    def forward(st, sm, e, j, chip_j, x, y, c, sender_c):
        kj = 2 * chip_j[0] + chip_j[1]
        r = st[e].at[sender_c, kj]
        return _remote(r, r, sm[0].at[8 * e + 4 + j], sm[1].at[8 * e + 4 + j], (x, y, 1 - c))

    def local(p, st, sm, e, x, y, c):
        k = 2 * x + y
        return pltpu.make_async_copy(p[e].at[k], st[e].at[c, k], sm[2].at[e])

    def start(p, st, sm, before_slot=None):
        x, y, c = _place()
        for j, chip_j in enumerate(_other_chips(x, y)):
            if before_slot is not None:
                before_slot(j, 2 * chip_j[0] + chip_j[1])
            for e in range(n):
                ici(p, st, sm, e, j, chip_j, x, y, c).start()
        if before_slot is not None:
            before_slot(3, 2 * x + y)
        for e in range(n):
            local(p, st, sm, e, x, y, c).start()
            own(p, st, sm, e, x, y, c).start()

    def relay(p, st, sm):
        x, y, c = _place()
        for e in range(n):
            for j, chip_j in enumerate(_other_chips(x, y)):
                kj = 2 * chip_j[0] + chip_j[1]
                r = st[e].at[c, kj]
                _remote(r, r, sm[0].at[8 * e + j], sm[1].at[8 * e + j], (*chip_j, c)).wait_recv()
                forward(st, sm, e, j, chip_j, x, y, c, c).start()

    def finish(p, st, sm):
        x, y, c = _place()
        k = 2 * x + y
        chips = _other_chips(x, y)
        for e in range(n):
            r = st[e].at[1 - c, k]
            _remote(r, r, sm[0].at[8 * e + 3], sm[1].at[8 * e + 3], (x, y, 1 - c)).wait_recv()
            for j, chip_j in enumerate(chips):
                forward(st, sm, e, j, chip_j, x, y, c, 1 - c).wait_recv()
        for e in range(n):
            own(p, st, sm, e, x, y, c).wait_send()
            for j, chip_j in enumerate(chips):
                ici(p, st, sm, e, j, chip_j, x, y, c).wait_send()
                forward(st, sm, e, j, chip_j, x, y, c, c).wait_send()
            local(p, st, sm, e, x, y, c).wait()

    outs = [SDS((2,) + a.shape, a.dtype) for a in parts]
    return Exchange(list(parts), outs, [DMA((8 * n,)), DMA((8 * n,)), DMA((n,))], start, finish, relay)


def tail_reduce(d_norm_mix, d_norm_mem, d_norm_ffn, d_gains, d_cw8, d_cbias, d_qg, d_kg, d_mqg, d_mkg, d_sink8, loss8, tail):
    n = len(tail)
    scatter = scatter_exchange([SDS((4,) + a.shape[2:], WIRE) for a in tail])

    def half_copy(g, sib, hsem, e, j, slot, x, y, c):
        return _remote(g[e].at[slot, 1 - c], sib[e].at[slot], hsem[0].at[4 * e + j], hsem[1].at[4 * e + j], (x, y, 1 - c))

    def body(nm_ref, nmem_ref, nf_ref, gn_ref, cw_ref, cb_ref, qg_ref, kg_ref, mqg_ref, mkg_ref, sk_ref, ls_ref, *rest):
        g, o_ref, st = rest[:n], rest[n], rest[n + 1:2 * n + 1]
        buf, ssem, rsem = rest[2 * n + 1:2 * n + 4]
        own, sib, part = (rest[2 * n + 4 + i * n:2 * n + 4 + (i + 1) * n] for i in range(3))
        lsem = rest[5 * n + 4]
        hsem, xsem = rest[5 * n + 5:5 * n + 7], rest[5 * n + 7:]
        x, y, c = _place()
        loads = [pltpu.make_async_copy(g[e].at[:, c], own[e], lsem.at[e]) for e in range(n)]
        for ld in loads:
            ld.start()
        for j, slot in enumerate([2 * cx + cy for cx, cy in _other_chips(x, y)] + [2 * x + y]):
            for e in range(n):
                half_copy(g, sib, hsem, e, j, slot, x, y, c).start()
        me = 4 * x + 2 * y + c
        mine = buf.at[me]
        mine[...] = jnp.zeros((8, 1024), f32)
        mine[0:1, :] = nm_ref[...]
        mine[1:2, :] = nmem_ref[...]
        mine[2:3, :] = nf_ref[...]
        mine[3:4, :] = gn_ref[...]
        for j in range(3):
            mine[4:5, pl.ds(j * CONV_W, CONV_W)] = cw_ref[j:j + 1, :]
        mine[4:5, pl.ds(3 * CONV_W, CONV_W)] = cb_ref[...]
        for j, r in enumerate((qg_ref, kg_ref, mqg_ref, mkg_ref)):
            mine[5:6, pl.ds(j * HD, HD)] = r[...]
        mine[5:6, pl.ds(256, 128)] = sk_ref[0:1, :]
        mine[5:6, pl.ds(384, 128)] = ls_ref[0:1, :]

        def peer_of(m):
            return (1 - x if m & 4 else x, 1 - y if m & 2 else y, 1 - c if m & 1 else c)

        for m in range(1, 8):
            _remote(mine, mine, ssem.at[m - 1], rsem.at[m - 1], peer_of(m)).start()
        for ld in loads:
            ld.wait()

        def chip_partial(j, slot):
            for e in range(n):
                half_copy(g, sib, hsem, e, j, slot, x, y, c).wait()
                part[e][slot] = (own[e][slot] + sib[e][slot]).astype(WIRE)

        scatter.start(part, st, xsem, chip_partial)
        scatter.relay(part, st, xsem)
        scatter.finish(part, st, xsem)
        for m in range(1, 8):
            p = peer_of(m)
            got = buf.at[4 * p[0] + 2 * p[1] + p[2]]
            _remote(got, got, ssem.at[m - 1], rsem.at[m - 1], p).wait_recv()
        for m in range(1, 8):
            _remote(mine, mine, ssem.at[m - 1], rsem.at[m - 1], peer_of(m)).wait_send()
        acc = buf[0]
        for d in range(1, 8):
            acc = acc + buf[d]
        o_ref[...] = acc

    ins = [d_norm_mix, d_norm_mem, d_norm_ffn, d_gains, d_cw8, d_cbias, d_qg, d_kg, d_mqg, d_mkg, d_sink8, loss8]
    half_shape = [(4,) + a.shape[2:] for a in tail]
    scratch = ([pltpu.VMEM((8, 8, 1024), f32), DMA((7,)), DMA((7,))]
               + [pltpu.VMEM(s, f32) for s in half_shape] * 2 + [pltpu.VMEM(s, WIRE) for s in half_shape]
               + [DMA((n,)), DMA((4 * n,)), DMA((4 * n,))] + list(scatter.sems))
    res = _run("tail_reduce", body, (), ins + list(tail), [VM] * len(ins) + [ANY] * n,
               [SDS((8, 1024), f32)] + list(scatter.outs), [VM] + [ANY] * n, scratch=scratch, vmem_mib=40)
    return res[0], res[1:]


def add_halves(cidx, grads, stages, name, nch=2):
    n = len(grads)

    def body(c_ref, *refs):
        g, st, o = refs[:n], refs[n:2 * n], refs[2 * n:]
        for e in range(n):
            o[e][...] = (g[e][...] + st[e][...]).astype(WIRE)

    in_specs, out_specs, out_shape = [], [], []
    for a in grads:
        hr, C = a.shape[2], a.shape[3]
        in_specs.append(pl.BlockSpec((None, None, hr // nch, C), lambda s, q, c_ref: (s, c_ref[0], q, 0)))
    for a in stages:
        hr, C = a.shape[1], a.shape[2]
        in_specs.append(pl.BlockSpec((None, hr // nch, C), lambda s, q, c_ref: (s, q, 0)))
        out_specs.append(pl.BlockSpec((None, hr // nch, C), lambda s, q, c_ref: (s, q, 0)))
        out_shape.append(SDS(a.shape, WIRE))
    return pl.pallas_call(
        body, name=name, out_shape=out_shape,
        grid_spec=pltpu.PrefetchScalarGridSpec(num_scalar_prefetch=1, grid=(4, nch), in_specs=in_specs, out_specs=out_specs),
        compiler_params=pltpu.CompilerParams(dimension_semantics=("arbitrary", "arbitrary")),
    )(cidx, *grads, *stages)


def _adamw_math(w, g, m, v):
    m = ADAM_B1 * m + (1.0 - ADAM_B1) * g
    v = ADAM_B2 * v + (1.0 - ADAM_B2) * (g * g)
    m_hat = m / (1.0 - ADAM_B1 ** ADAM_STEP)
    v_hat = v / (1.0 - ADAM_B2 ** ADAM_STEP)
    delta = -ADAM_LR * (m_hat / (jnp.sqrt(v_hat) + ADAM_EPS) + ADAM_WD * w)
    return delta, m, v


def _sum_chips(st):
    return ((st[0].astype(f32) + st[1].astype(f32)) + st[2].astype(f32)) + st[3].astype(f32)


def adamw_big(name, stages, ws, ms, vs, nstep, exchange=None):
    n = len(stages)

    def body(*refs):
        st, w, m, v = refs[:n], refs[n:2 * n], refs[2 * n:3 * n], refs[3 * n:4 * n]
        outs = refs[4 * n:]
        for e in range(n):
            g = jnp.concatenate([_sum_chips(st[e].at[0]), _sum_chips(st[e].at[1])], axis=0)
            d, mm, vv = _adamw_math(w[e][...], g, m[e][...], v[e][...])
            outs[4 * e][...] = g
            outs[4 * e + 1][...] = d
            outs[4 * e + 2][...] = mm
            outs[4 * e + 3][...] = vv

    st_specs, w_specs = [], []
    for e in range(n):
        _, _, hr, C = stages[e].shape
        st_specs.append(pl.BlockSpec((2, 4, hr, C // nstep), lambda i: (0, 0, 0, i)))
        w_specs.append(pl.BlockSpec((2 * hr, C // nstep), lambda i: (0, i)))
    out_specs = [s for s in w_specs for _ in range(4)]
    out_shape = [SDS(w.shape, f32) for w in ws for _ in range(4)]
    res = _run(name, body, (nstep,), list(stages) + list(ws) + list(ms) + list(vs), st_specs + w_specs * 3,
               out_shape, out_specs, vmem_mib=48, exchange=exchange)
    res, sent = res if exchange is not None else (res, None)
    return [res[4 * e:4 * e + 4] for e in range(n)], sent


def adamw_small(tot, pk_w, pk_m, pk_v, shapes):
    def body(tot_ref, w_ref, m_ref, v_ref, *outs):
        x, y, _ = _place()
        chip = 2 * x + y
        taps = []
        for j in range(3):
            mine = tot_ref[4:5, j * CONV_W:j * CONV_W + HD]
            for s in range(1, 4):
                mine = jnp.where(chip == s, tot_ref[4:5, j * CONV_W + s * HD:j * CONV_W + (s + 1) * HD], mine)
            taps.append(mine)
        row4 = jnp.concatenate(taps + [jnp.zeros((1, 3 * CONV_W - 3 * HD), f32), tot_ref[4:5, 3 * CONV_W:]], axis=1)
        tot_v = tot_ref[...]
        row = lax.broadcasted_iota(jnp.int32, tot_v.shape, 0)
        g = jnp.where(row == 4, jnp.broadcast_to(row4, tot_v.shape), tot_v)
        d, mm, vv = _adamw_math(w_ref[...], g, m_ref[...], v_ref[...])
        for i, name in enumerate(SMALL):
            for k, val in enumerate((g, d, mm, vv)):
                if name == "conv_w":
                    outs[4 * i + k][...] = jnp.concatenate([val[4:5, j * HD:(j + 1) * HD] for j in range(3)], axis=0)[None]
                else:
                    r, c0, w = SMALL_AT[name]
                    outs[4 * i + k][...] = val[r:r + 1, c0:c0 + w]

    out_shape = [SDS(shapes[k], f32) for k in SMALL for _ in range(4)]
    res = _run("adamw_small", body, (), [tot, pk_w, pk_m, pk_v], [VM] * 4, out_shape, [VM] * len(out_shape))
    return {k: res[4 * i:4 * i + 4] for i, k in enumerate(SMALL)}


def prep_weights(name, shards, exchange=None):
    n = len(shards)

    def body(*refs):
        for e in range(n):
            refs[n + e][...] = _c(refs[e][...])

    return _run(name, body, (), shards, [VM] * n, [SDS(a.shape, MXU) for a in shards], [VM] * n, vmem_mib=48, exchange=exchange)


def mem_kv_fwd(mem2d, pk, wmkv):
    M, D = mem2d.shape

    def body(m_ref, pk_ref, w_ref, mn_ref, kv_ref, km_ref, vm_ref):
        m = m_ref[...]
        mn = _c(m * _rstd(m) * _small(pk_ref, "norm_mem"))
        mn_ref[...] = mn
        kv = _nn(mn, w_ref[...])
        kv_ref[...] = kv
        kk = kv[:, :MEM_W]
        km_ref[...] = _c(kk * _heads_rstd(kk) * _lanes(_small(pk_ref, "mem_k_norm"), MEM_W))
        vm_ref[...] = _c(kv[:, MEM_W:])

    return _run("mem_kv_fwd", body, (), [mem2d, pk, wmkv], [VM] * 3,
                [SDS((M, D), MXU), SDS((M, 2 * MEM_W), f32), SDS((M, MEM_W), MXU), SDS((M, MEM_W), MXU)], [VM] * 4)


QKV_W = ATT_W + 2 * KV_W + MEM_W


def in_proj_fwd(x2d, pk, winT, tm, exchange):
    T, D = x2d.shape
    P = winT.shape[0]

    def body(x_ref, pk_ref, w_ref, xn_ref, proj_ref, qkv_ref):
        xv = x_ref[...]
        xn = _c(xv * _rstd(xv) * _small(pk_ref, "norm_mix"))
        xn_ref[...] = xn
        proj = _nt(xn, w_ref[...])
        proj_ref[...] = proj
        q, k = proj[:, :ATT_W], proj[:, ATT_W:ATT_W + KV_W]
        qm = proj[:, P - MEM_W:]
        qkv_ref[...] = jnp.concatenate(
            [_c(q * _heads_rstd(q) * _lanes(_small(pk_ref, "q_norm"), ATT_W)),
             _c(k * _heads_rstd(k) * _lanes(_small(pk_ref, "k_norm"), KV_W)),
             _c(proj[:, ATT_W + KV_W:ATT_W + 2 * KV_W]),
             _c(qm * _heads_rstd(qm) * _lanes(_small(pk_ref, "mem_q_norm"), MEM_W))], axis=1)

    return _run("in_proj_fwd", body, (T // tm,), [x2d, pk, winT],
                [pl.BlockSpec((tm, D), lambda i: (i, 0)), VM, VM],
                [SDS((T, D), MXU), SDS((T, P), f32), SDS((T, QKV_W), MXU)],
                [pl.BlockSpec((tm, D), lambda i: (i, 0)), pl.BlockSpec((tm, P), lambda i: (i, 0)),
                 pl.BlockSpec((tm, QKV_W), lambda i: (i, 0))],
                vmem_mib=40, exchange=exchange)


def _swa_bias_table():
    r = np.arange(GQA * BLK)[:, None]
    k = np.arange(2 * BLK)[None, :]
    dist = (r % BLK) + BLK - k
    band = (dist >= 0) & (dist < BLK)
    tab = np.empty((2, N_KV, GQA * BLK, 2 * BLK), np.float32)
    for later in range(2):
        valid = band & ((k >= BLK) | (later == 1))
        for g in range(N_KV):
            slope = 2.0 ** -(g * GQA + r // BLK + 1.0)
            tab[later, g] = np.where(valid, -slope * dist, NEG)
    return jnp.asarray(tab)


def _sink_column(g, sk_ref):
    hrow = lax.broadcasted_iota(jnp.int32, (GQA * BLK, 1), 0) // BLK
    sink = jnp.zeros((GQA * BLK, 1), f32)
    for hh in range(GQA):
        sink = jnp.where(hrow == hh, sk_ref[g * GQA + hh:g * GQA + hh + 1, 0:1], sink)
    return sink


def _stack_heads(v, g):
    return jnp.concatenate([v[:, (g * GQA + hh) * HD:(g * GQA + hh + 1) * HD] for hh in range(GQA)], axis=0)


def attn_fwd(qkv, sink_rows, BL, S, exchange, qb=2):
    NS = S // (qb * BLK)
    T = BL * S

    def body(q_ref, kc_ref, kp_ref, vc_ref, vp_ref, sk_ref, tab_ref, o_ref):
        j = pl.program_id(1)
        kall = jnp.concatenate([kp_ref[...], kc_ref[...]], axis=0)
        vall = jnp.concatenate([vp_ref[...], vc_ref[...]], axis=0)
        ones = jnp.ones((2 * BLK, HD), MXU)
        for b in range(qb):
            q = q_ref[pl.ds(b * BLK, BLK), :]
            k2, v2 = kall[b * BLK:(b + 2) * BLK], vall[b * BLK:(b + 2) * BLK]
            later = jnp.minimum(j, 1) if b == 0 else 1
            for g in range(N_KV):
                kn, vh = k2[:, g * HD:(g + 1) * HD], v2[:, g * HD:(g + 1) * HD]
                s = _nt(_stack_heads(q, g), kn) * (HD ** -0.5) + tab_ref[later, g]
                e, es = _exp_scores(s, _sink_column(g, sk_ref))
                eb = _c(e)
                o = _nn(eb, vh) * (1.0 / (_nn(eb, ones) + es))
                for hh in range(GQA):
                    o_ref[pl.ds(b * BLK, BLK), pl.ds((g * GQA + hh) * HD, HD)] = o[hh * BLK:(hh + 1) * BLK]

    cur = lambda col: (lambda b, j: (b * NS + j, col))
    prev = lambda col: (lambda b, j: (qb * (b * NS + j) - jnp.minimum(j, 1), col))
    return _run("attn_fwd", body, (BL, NS), [qkv, qkv, qkv, qkv, qkv, sink_rows, _swa_bias_table()],
                [pl.BlockSpec((qb * BLK, ATT_W), cur(0)),
                 pl.BlockSpec((qb * BLK, KV_W), cur(4)), pl.BlockSpec((BLK, KV_W), prev(4)),
                 pl.BlockSpec((qb * BLK, KV_W), cur(5)), pl.BlockSpec((BLK, KV_W), prev(5)),
                 pl.BlockSpec((8, 128), lambda b, j: (0, 0)), VM],
                [SDS((T, ATT_W), f32)], [pl.BlockSpec((qb * BLK, ATT_W), cur(0))], exchange=exchange)


def _conv_taps(u, uh):
    row = lax.broadcasted_iota(jnp.int32, u.shape, 0)
    u1 = jnp.where(row == 0, uh[7:8, :], pltpu.roll(u, 1, 0))
    u2 = jnp.where(row == 0, uh[6:7, :], jnp.where(row == 1, uh[7:8, :], pltpu.roll(u, 2, 0)))
    return u1, u2


def _mem_head(qm, km, vm, h):
    qh, kh, vh = (a[:, h * HD:(h + 1) * HD] for a in (qm, km, vm))
    e, _ = _exp_scores(_nt(qh, kh) * (HD ** -0.5))
    return qh, kh, vh, e


def mixer_tail_fwd(x2d, attn_out, proj, qkv, km, vm, conv_w8, pk, wout, S, tm, exchange):
    T, D = x2d.shape
    NM = km.shape[0] // (T // S)

    def body(x_ref, ao_ref, ch_ref, cb_ref, cc_ref, chh_ref, cch_ref, qm_ref, km_ref, vm_ref, cw_ref, pk_ref,
             wout_ref, co_ref, mo_ref, mg_ref, x1_ref, h_ref):
        first = (pl.program_id(0) * tm) % S == 0
        u = cc_ref[...] * ch_ref[...]
        uh = jnp.where(first, 0.0, cch_ref[...] * chh_ref[...])
        u1, u2 = _conv_taps(u, uh)
        conv = cw_ref[0:1, :] * u2 + cw_ref[1:2, :] * u1 + cw_ref[2:3, :] * u + _small(pk_ref, "conv_b")
        conv_out = cb_ref[...] * conv
        co_ref[...] = conv_out
        qm, kmv, vmv = qm_ref[...], km_ref[...], vm_ref[...]
        ones = jnp.ones((NM, HD), MXU)
        for h in range(N_MEMH):
            _, _, vh, e = _mem_head(qm, kmv, vmv, h)
            eb = _c(e)
            mo_ref[:, pl.ds(h * HD, HD)] = _nn(eb, vh) * (1.0 / _nn(eb, ones))
        mem_out = mo_ref[...]
        ao = ao_ref[...]
        merged = _c(jnp.concatenate([ao * _rstd(ao) * _small(pk_ref, "out_norm_attn"),
                                     conv_out * _rstd(conv_out) * _small(pk_ref, "out_norm_conv"),
                                     mem_out * _rstd(mem_out) * _small(pk_ref, "out_norm_mem")], axis=1))
        mg_ref[...] = merged
        x1 = x_ref[...] + _nn(merged, wout_ref[...])
        x1_ref[...] = x1
        h_ref[...] = _c(x1 * _rstd(x1) * _small(pk_ref, "norm_ffn"))

    tile = lambda w, col: pl.BlockSpec((tm, w), lambda i: (i, col))
    halo = lambda col: pl.BlockSpec((8, CONV_W), lambda i: (jnp.maximum(i * (tm // 8) - 1, 0), col))
    seq = pl.BlockSpec((NM, MEM_W), lambda i: ((i * tm) // S, 0))
    small = lambda a: pl.BlockSpec(a.shape, lambda i: (0, 0))
    return _run("mixer_tail_fwd", body, (T // tm,),
                [x2d, attn_out, proj, proj, proj, proj, proj, qkv, km, vm, conv_w8, pk, wout],
                [tile(D, 0), tile(ATT_W, 0), tile(CONV_W, 3), tile(CONV_W, 4), tile(CONV_W, 5), halo(3), halo(5),
                 tile(MEM_W, 3), seq, seq, VM, VM, VM],
                [SDS((T, CONV_W), f32), SDS((T, MEM_W), f32), SDS((T, D), MXU), SDS((T, D), f32), SDS((T, D), MXU)],
                [tile(CONV_W, 0), tile(MEM_W, 0), tile(D, 0), tile(D, 0), tile(D, 0)], vmem_mib=40, exchange=exchange)


def ffn_fwd_bwd(h, x1, tgt, wgT, wuT, wd, pk, tm):
    T, D = x1.shape
    F = wd.shape[0]

    def body(h_ref, x1_ref, t_ref, wg_ref, wu_ref, wd_ref, pk_ref,
             dx1_ref, dx2_ref, act_ref, dg_ref, du_ref, loss_ref, dgf_ref):
        @pl.when(pl.program_id(0) == 0)
        def _():
            loss_ref[...] = jnp.zeros_like(loss_ref)
            dgf_ref[...] = jnp.zeros_like(dgf_ref)

        hv = h_ref[...]
        gate = _nt(hv, wg_ref[...])
        up = _nt(hv, wu_ref[...])
        sg = jax.nn.sigmoid(gate)
        sl = gate * sg
        act = _c(sl * up)
        act_ref[...] = act
        x1v = x1_ref[...]
        diff = (x1v + _nn(act, wd_ref[...])) - t_ref[...]
        loss_ref[...] += 0.5 * jnp.sum(jnp.sum(diff * diff, axis=-1, keepdims=True) / D, axis=0, keepdims=True)
        dx2 = diff / D
        dx2b = _c(dx2)
        dx2_ref[...] = dx2b
        d_act = _nt(dx2b, wd_ref[...])
        d_up = _c(d_act * sl)
        d_gate = _c(d_act * up * (sg * (1.0 + gate * (1.0 - sg))))
        du_ref[...] = d_up
        dg_ref[...] = d_gate
        dh = _nn(d_gate, wg_ref[...]) + _nn(d_up, wu_ref[...])
        dv, dgf = _norm_bwd(dh, x1v, _rstd(x1v), _small(pk_ref, "norm_ffn"))
        dx1_ref[...] = dx2 + dv
        dgf_ref[...] += dgf

    tile = lambda w: pl.BlockSpec((tm, w), lambda i: (i, 0))
    return _run("ffn_fwd_bwd", body, (T // tm,), [h, x1, tgt, wgT, wuT, wd, pk],
                [tile(D), tile(D), tile(D), VM, VM, VM, VM],
                [SDS((T, D), f32), SDS((T, D), MXU), SDS((T, F), MXU), SDS((T, F), MXU), SDS((T, F), MXU),
                 SDS((8, 128), f32), SDS((1, D), f32)],
                [tile(D), tile(D), tile(F), tile(F), tile(F), pl.BlockSpec((8, 128), lambda i: (0, 0)),
                 pl.BlockSpec((1, D), lambda i: (0, 0))], vmem_mib=56)


def matmul_tn(a, b, name, tmo, tk):
    T, M = a.shape
    N = b.shape[1]

    def body(a_ref, b_ref, o_ref):
        @pl.when(pl.program_id(1) == 0)
        def _():
            o_ref[...] = jnp.zeros_like(o_ref)

        o_ref[...] += _tn(a_ref[...], b_ref[...])

    return _run(name, body, (M // tmo, T // tk), [a, b],
                [pl.BlockSpec((tk, tmo), lambda m, k: (k, m)), pl.BlockSpec((tk, N), lambda m, k: (k, 0))],
                [SDS((M, N), f32)], [pl.BlockSpec((tmo, N), lambda m, k: (m, 0))], vmem_mib=48)[0]


def out_proj_bwd(dx1, merged, attn_out, conv_out, mem_out, pk, wout, tm):
    T, D = dx1.shape

    def body(dx1_ref, mg_ref, ao_ref, co_ref, mo_ref, pk_ref, w_ref,
             dao_ref, dco_ref, dmo_ref, dw_ref, dgain_ref):
        @pl.when(pl.program_id(0) == 0)
        def _():
            dw_ref[...] = jnp.zeros_like(dw_ref)
            dgain_ref[...] = jnp.zeros_like(dgain_ref)

        dxb = _c(dx1_ref[...])
        dw_ref[...] += _tn(mg_ref[...], dxb)
        dmg = _nt(dxb, w_ref[...])
        ao, co, mo = ao_ref[...], co_ref[...], mo_ref[...]
        da, ga = _norm_bwd(dmg[:, :ATT_W], ao, _rstd(ao), _small(pk_ref, "out_norm_attn"))
        dc, gc = _norm_bwd(dmg[:, ATT_W:ATT_W + CONV_W], co, _rstd(co), _small(pk_ref, "out_norm_conv"))
        dm, gm = _norm_bwd(dmg[:, ATT_W + CONV_W:], mo, _rstd(mo), _small(pk_ref, "out_norm_mem"))
        dao_ref[...] = da
        dco_ref[...] = dc
        dmo_ref[...] = dm
        dgain_ref[...] += jnp.concatenate([ga, gc, gm], axis=1)

    tile = lambda w: pl.BlockSpec((tm, w), lambda i: (i, 0))
    return _run("out_proj_bwd", body, (T // tm,), [dx1, merged, attn_out, conv_out, mem_out, pk, wout],
                [tile(D), tile(D), tile(ATT_W), tile(CONV_W), tile(MEM_W), VM, VM],
                [SDS((T, ATT_W), f32), SDS((T, CONV_W), f32), SDS((T, MEM_W), f32), SDS((D, D), f32), SDS((1, D), f32)],
                [tile(ATT_W), tile(CONV_W), tile(MEM_W), pl.BlockSpec((D, D), lambda i: (0, 0)),
                 pl.BlockSpec((1, D), lambda i: (0, 0))], vmem_mib=40)


def attn_bwd(qkv, d_attn, attn_out, sink_rows, BL, S, exchange):
    NB = S // BLK
    T = BL * S

    def body(q_ref, kc_ref, kp_ref, vc_ref, vp_ref, do_ref, ao_ref, sk_ref, tab_ref,
             dq_ref, dk_ref, dv_ref, dsk_ref, pend_k, pend_v):
        b, j = pl.program_id(0), pl.program_id(1)

        @pl.when((b == 0) & (j == 0))
        def _():
            dsk_ref[...] = jnp.zeros_like(dsk_ref)

        @pl.when(j == 0)
        def _():
            pend_k[...] = jnp.zeros_like(pend_k)
            pend_v[...] = jnp.zeros_like(pend_v)

        @pl.when(j < NB)
        def _():
            q, do, ao = q_ref[...], do_ref[...], ao_ref[...]
            k2 = jnp.concatenate([kp_ref[...], kc_ref[...]], axis=0)
            v2 = jnp.concatenate([vp_ref[...], vc_ref[...]], axis=0)
            lane = lax.broadcasted_iota(jnp.int32, (8, 128), 1)
            ones_w = jnp.ones((2 * BLK, 2 * BLK), MXU)
            dsk = jnp.zeros((8, 128), f32)
            dks, dvs = [], []
            for g in range(N_KV):
                kn, vh = k2[:, g * HD:(g + 1) * HD], v2[:, g * HD:(g + 1) * HD]
                qs = _stack_heads(q, g)
                s = _nt(qs, kn) * (HD ** -0.5) + tab_ref[g]
                e, es = _exp_scores(s, _sink_column(g, sk_ref))
                eb = _c(e)
                inv_w = 1.0 / (_nn(eb, ones_w) + es)
                inv_n = inv_w[:, :HD]
                dos = _stack_heads(do, g)
                delta = _rowsum_mxu(dos * _stack_heads(ao, g), 2 * BLK)
                dp = _nt(_c(dos), vh)
                ds = _c(e * inv_w * (dp - delta) * (HD ** -0.5))
                t = es * inv_n[:, 0:1] * delta[:, 0:1]
                for hh in range(GQA):
                    dsk = dsk + jnp.where(lane == g * GQA + hh, -jnp.sum(t[hh * BLK:(hh + 1) * BLK]), 0.0)
                dvs.append(_tn(eb, _c(dos * inv_n)))
                dks.append(_tn(ds, qs))
                dqs = _nn(ds, kn)
                for hh in range(GQA):
                    dq_ref[:, pl.ds((g * GQA + hh) * HD, HD)] = dqs[hh * BLK:(hh + 1) * BLK]
            dk2 = jnp.concatenate(dks, axis=1)
            dv2 = jnp.concatenate(dvs, axis=1)
            dk_ref[...] = pend_k[...] + dk2[:BLK]
            dv_ref[...] = pend_v[...] + dv2[:BLK]
            pend_k[...] = dk2[BLK:]
            pend_v[...] = dv2[BLK:]
            dsk_ref[...] += dsk

        @pl.when(j == NB)
        def _():
            dk_ref[...] = pend_k[...]
            dv_ref[...] = pend_v[...]

    cur = lambda col: (lambda b, j: (b * NB + jnp.minimum(j, NB - 1), col))
    prev = lambda col: (lambda b, j: (b * NB + jnp.maximum(j - 1, 0), col))
    small = lambda shape: pl.BlockSpec(shape, lambda b, j: (0, 0))
    return _run("attn_bwd", body, (BL, NB + 1), [qkv, qkv, qkv, qkv, qkv, d_attn, attn_out, sink_rows, _swa_bias_table()],
                [pl.BlockSpec((BLK, ATT_W), cur(0)),
                 pl.BlockSpec((BLK, KV_W), cur(4)), pl.BlockSpec((BLK, KV_W), prev(4)),
                 pl.BlockSpec((BLK, KV_W), cur(5)), pl.BlockSpec((BLK, KV_W), prev(5)),
                 pl.BlockSpec((BLK, ATT_W), cur(0)), pl.BlockSpec((BLK, ATT_W), cur(0)), small((8, 128)),
                 pl.BlockSpec((None, N_KV, GQA * BLK, 2 * BLK), lambda b, j: (jnp.minimum(j, 1), 0, 0, 0))],
                [SDS((T, ATT_W), f32), SDS((T, KV_W), f32), SDS((T, KV_W), f32), SDS((8, 128), f32)],
                [pl.BlockSpec((BLK, ATT_W), cur(0)), pl.BlockSpec((BLK, KV_W), prev(0)),
                 pl.BlockSpec((BLK, KV_W), prev(0)), small((8, 128))],
                scratch=[pltpu.VMEM((BLK, KV_W), f32)] * 2, exchange=exchange)


def mem_conv_bwd(d_mem_out, mem_out, d_conv_out, proj, qkv, km, vm, conv_w8, pk, S, tm, exchange):
    T = d_mem_out.shape[0]
    NM = km.shape[0] // (T // S)

    def body(dmo_ref, mo_ref, dco_ref, ch_ref, cb_ref, cc_ref, chh_ref, cch_ref, qm_ref, km_ref, vm_ref, cw_ref,
             pk_ref, dqm_ref, dkm_ref, dvm_ref, dcb_ref, dcv_ref, dcw_ref, dcbias_ref):
        i = pl.program_id(0)
        first = (i * tm) % S == 0

        @pl.when(i == 0)
        def _():
            dcw_ref[...] = jnp.zeros_like(dcw_ref)
            dcbias_ref[...] = jnp.zeros_like(dcbias_ref)

        @pl.when(first)
        def _():
            dkm_ref[...] = jnp.zeros_like(dkm_ref)
            dvm_ref[...] = jnp.zeros_like(dvm_ref)

        qm, kmv, vmv, dmo, mo = qm_ref[...], km_ref[...], vm_ref[...], dmo_ref[...], mo_ref[...]
        ones_w = jnp.ones((NM, NM), MXU)
        for h in range(N_MEMH):
            qh, kh, vh, e = _mem_head(qm, kmv, vmv, h)
            eb = _c(e)
            doh = dmo[:, h * HD:(h + 1) * HD]
            delta = _rowsum_mxu(doh * mo[:, h * HD:(h + 1) * HD], NM)
            dp = _nt(_c(doh), vh)
            inv_w = 1.0 / _nn(eb, ones_w)
            ds = _c(e * inv_w * (dp - delta) * (HD ** -0.5))
            dvm_ref[:, pl.ds(h * HD, HD)] += _tn(eb, _c(doh * inv_w[:, :HD]))
            dkm_ref[:, pl.ds(h * HD, HD)] += _tn(ds, qh)
            dqm_ref[:, pl.ds(h * HD, HD)] = _nn(ds, kh)

        u = cc_ref[...] * ch_ref[...]
        uh = jnp.where(first, 0.0, cch_ref[...] * chh_ref[...])
        u1, u2 = _conv_taps(u, uh)
        conv = cw_ref[0:1, :] * u2 + cw_ref[1:2, :] * u1 + cw_ref[2:3, :] * u + _small(pk_ref, "conv_b")
        dy = dco_ref[...]
        dcb_ref[...] = dy * conv
        dcv = dy * cb_ref[...]
        dcv_ref[...] = dcv
        dcbias_ref[...] += jnp.sum(dcv, axis=0, keepdims=True)
        dcw_ref[0:1, :] += jnp.sum(dcv * u2, axis=0, keepdims=True)
        dcw_ref[1:2, :] += jnp.sum(dcv * u1, axis=0, keepdims=True)
        dcw_ref[2:3, :] += jnp.sum(dcv * u, axis=0, keepdims=True)

    tile = lambda w, col: pl.BlockSpec((tm, w), lambda i: (i, col))
    halo = lambda col: pl.BlockSpec((8, CONV_W), lambda i: (jnp.maximum(i * (tm // 8) - 1, 0), col))
    seq = pl.BlockSpec((NM, MEM_W), lambda i: ((i * tm) // S, 0))
    const = lambda shape: pl.BlockSpec(shape, lambda i: (0, 0))
    return _run("mem_conv_bwd", body, (T // tm,),
                [d_mem_out, mem_out, d_conv_out, proj, proj, proj, proj, proj, qkv, km, vm, conv_w8, pk],
                [tile(MEM_W, 0), tile(MEM_W, 0), tile(CONV_W, 0), tile(CONV_W, 3), tile(CONV_W, 4), tile(CONV_W, 5),
                 halo(3), halo(5), tile(MEM_W, 3), seq, seq, VM, VM],
                [SDS((T, MEM_W), f32), SDS(km.shape, f32), SDS(km.shape, f32),
                 SDS((T, CONV_W), f32), SDS((T, CONV_W), f32), SDS((8, CONV_W), f32), SDS((1, CONV_W), f32)],
                [tile(MEM_W, 0), seq, seq, tile(CONV_W, 0), tile(CONV_W, 0), const((8, CONV_W)), const((1, CONV_W))],
                vmem_mib=48, exchange=exchange)


def in_proj_bwd(dqn, dkn, dv, dcb, dcv, dqmn, proj, conv_w8, xn, x2d, dx1, pk, winT, S, tm, stages, ws, ms, vs):
    T, D = x2d.shape
    P = winT.shape[0]
    last_blk = T // 8 - 1
    n = len(stages)
    nsteps = T // tm
    tile_w = ws[0].shape[1] // (nsteps // 2)
    turn = [e * 2 // n for e in range(n)]

    def body(dq_ref, dk_ref, dv_ref, dcb_ref, dcv_ref, dcvn_ref, dqm_ref, qa_ref, ka_ref, ch_ref, cc_ref, qma_ref,
             cw_ref, xn_ref, x_ref, dx1_ref, pk_ref, w_ref, *rest):
        st, aw, am, av = (rest[k * n:(k + 1) * n] for k in range(4))
        dx_ref, dw_ref, dg_ref, dqg_ref, dkg_ref, dmqg_ref = rest[4 * n:4 * n + 6]
        aouts = rest[4 * n + 6:]
        i = pl.program_id(0)

        for parity in range(2):
            @pl.when(i % 2 == parity)
            def _(parity=parity):
                for e in range(n):
                    if turn[e] == parity:
                        g = jnp.concatenate([_sum_chips(st[e].at[0]), _sum_chips(st[e].at[1])], axis=0)
                        d, mm, vv = _adamw_math(aw[e][...], g, am[e][...], av[e][...])
                        for k, val in enumerate((g, d, mm, vv)):
                            aouts[4 * e + k][...] = val

        @pl.when(i == 0)
        def _():
            dw_ref[...] = jnp.zeros_like(dw_ref)
            dg_ref[...] = jnp.zeros_like(dg_ref)
            dqg_ref[...] = jnp.zeros_like(dqg_ref)
            dkg_ref[...] = jnp.zeros_like(dkg_ref)
            dmqg_ref[...] = jnp.zeros_like(dmqg_ref)

        dqa, gq = _heads_norm_bwd(dq_ref[...], qa_ref[...], _small(pk_ref, "q_norm"))
        dka, gk = _heads_norm_bwd(dk_ref[...], ka_ref[...], _small(pk_ref, "k_norm"))
        dqma, gmq = _heads_norm_bwd(dqm_ref[...], qma_ref[...], _small(pk_ref, "mem_q_norm"))
        dqg_ref[...] += gq
        dkg_ref[...] += gk
        dmqg_ref[...] += gmq

        last = ((i + 1) * tm) % S == 0
        dcv = dcv_ref[...]
        nxt = jnp.where(last, 0.0, dcvn_ref[...])
        row = lax.broadcasted_iota(jnp.int32, dcv.shape, 0)
        n1 = jnp.where(row == tm - 1, nxt[0:1, :], pltpu.roll(dcv, tm - 1, 0))
        n2 = jnp.where(row == tm - 2, nxt[0:1, :], jnp.where(row == tm - 1, nxt[1:2, :], pltpu.roll(dcv, tm - 2, 0)))
        du = cw_ref[2:3, :] * dcv + cw_ref[1:2, :] * n1 + cw_ref[0:1, :] * n2
        d_proj = jnp.concatenate([_c(dqa), _c(dka), _c(dv_ref[...]), _c(du * cc_ref[...]),
                                  _c(dcb_ref[...]), _c(du * ch_ref[...]), _c(dqma)], axis=1)
        dw_ref[...] += _tn(d_proj, xn_ref[...])
        xv = x_ref[...]
        dv_, dg = _norm_bwd(_nn(d_proj, w_ref[...]), xv, _rstd(xv), _small(pk_ref, "norm_mix"))
        dx_ref[...] = dx1_ref[...] + dv_
        dg_ref[...] += dg

    tile = lambda w, col=0: pl.BlockSpec((tm, w), lambda i: (i, col))
    nhalo = pl.BlockSpec((8, CONV_W), lambda i: (jnp.minimum((i + 1) * (tm // 8), last_blk), 0))
    const = lambda shape: pl.BlockSpec(shape, lambda i: (0, 0))
    st_specs = [pl.BlockSpec((2, 4, s.shape[2], tile_w), lambda i: (0, 0, 0, i // 2)) for s in stages]
    w_specs = [pl.BlockSpec((w.shape[0], tile_w), lambda i: (0, i // 2)) for w in ws]
    res = _run("in_proj_bwd", body, (nsteps,),
               [dqn, dkn, dv, dcb, dcv, dcv, dqmn, proj, proj, proj, proj, proj, conv_w8, xn, x2d, dx1, pk, winT]
               + list(stages) + list(ws) + list(ms) + list(vs),
               [tile(ATT_W), tile(KV_W), tile(KV_W), tile(CONV_W), tile(CONV_W), nhalo, tile(MEM_W),
                tile(ATT_W, 0), tile(KV_W, 4), tile(CONV_W, 3), tile(CONV_W, 5), tile(MEM_W, 6), VM,
                tile(D), tile(D), tile(D), VM, VM] + st_specs + w_specs * 3,
               [SDS((T, D), f32), SDS((P, D), f32), SDS((1, D), f32), SDS((1, HD), f32), SDS((1, HD), f32),
                SDS((1, HD), f32)] + [SDS(w.shape, f32) for w in ws for _ in range(4)],
               [tile(D), pl.BlockSpec((P, D), lambda i: (0, 0)), const((1, D)), const((1, HD)), const((1, HD)),
                const((1, HD))] + [s for s in w_specs for _ in range(4)],
               vmem_mib=56)
    return res[:6], [res[6 + 4 * e:10 + 4 * e] for e in range(n)]


def mem_kv_bwd(dkm, dvm, kv, memn, mem2d, pk, wmkv):
    def body(dkm_ref, dvm_ref, kv_ref, mn_ref, m_ref, pk_ref, w_ref, dw_ref, dg_ref, dkg_ref):
        dkk, dkg = _heads_norm_bwd(dkm_ref[...], kv_ref[:, :MEM_W], _small(pk_ref, "mem_k_norm"))
        dkg_ref[...] = dkg
        dkv = _c(jnp.concatenate([dkk, dvm_ref[...]], axis=1))
        dw_ref[...] = _tn(mn_ref[...], dkv)
        mv = m_ref[...]
        dg_ref[...] = jnp.sum(_nt(dkv, w_ref[...]) * mv * _rstd(mv), axis=0, keepdims=True)

    return _run("mem_kv_bwd", body, (), [dkm, dvm, kv, memn, mem2d, pk, wmkv], [VM] * 7,
                [SDS(wmkv.shape, f32), SDS((1, mem2d.shape[1]), f32), SDS((1, HD), f32)], [VM] * 3, vmem_mib=40)


def _halves_view(g):
    return g.reshape(4, 2, g.shape[0] // 8, g.shape[1])


def kernel(x, mem, norm_mix, w_in, q_norm, k_norm, attn_sinks, conv_w, conv_b, norm_mem, w_mem_kv, mem_q_norm, mem_k_norm, out_norm_attn, out_norm_conv, out_norm_mem, w_out, norm_ffn, w_gate, w_up, w_down, loss_target, m_norm_mix, m_w_in, m_q_norm, m_k_norm, m_attn_sinks, m_conv_w, m_conv_b, m_norm_mem, m_w_mem_kv, m_mem_q_norm, m_mem_k_norm, m_out_norm_attn, m_out_norm_conv, m_out_norm_mem, m_w_out, m_norm_ffn, m_w_gate, m_w_up, m_w_down, v_norm_mix, v_w_in, v_q_norm, v_k_norm, v_attn_sinks, v_conv_w, v_conv_b, v_norm_mem, v_w_mem_kv, v_mem_q_norm, v_mem_k_norm, v_out_norm_attn, v_out_norm_conv, v_out_norm_mem, v_w_out, v_norm_ffn, v_w_gate, v_w_up, v_w_down):
    BL, S, D = x.shape
    T = BL * S
    TM = 256
    TM_BIG = min(512, S)
    _, _, ci = _place()
    cidx = ci.reshape(1).astype(jnp.int32)
    w_small = dict(norm_mix=norm_mix, norm_mem=norm_mem, norm_ffn=norm_ffn, out_norm_attn=out_norm_attn,
                   out_norm_conv=out_norm_conv, out_norm_mem=out_norm_mem, conv_w=conv_w, conv_b=conv_b, q_norm=q_norm,
                   k_norm=k_norm, mem_q_norm=mem_q_norm, mem_k_norm=mem_k_norm, attn_sinks=attn_sinks)
    m_small = dict(norm_mix=m_norm_mix, norm_mem=m_norm_mem, norm_ffn=m_norm_ffn, out_norm_attn=m_out_norm_attn,
                   out_norm_conv=m_out_norm_conv, out_norm_mem=m_out_norm_mem, conv_w=m_conv_w, conv_b=m_conv_b,
                   q_norm=m_q_norm, k_norm=m_k_norm, mem_q_norm=m_mem_q_norm, mem_k_norm=m_mem_k_norm,
                   attn_sinks=m_attn_sinks)
    v_small = dict(norm_mix=v_norm_mix, norm_mem=v_norm_mem, norm_ffn=v_norm_ffn, out_norm_attn=v_out_norm_attn,
                   out_norm_conv=v_out_norm_conv, out_norm_mem=v_out_norm_mem, conv_w=v_conv_w, conv_b=v_conv_b,
                   q_norm=v_q_norm, k_norm=v_k_norm, mem_q_norm=v_mem_q_norm, mem_k_norm=v_mem_k_norm,
                   attn_sinks=v_attn_sinks)
    pk = _pack_small(w_small)

    rowblocks = lambda a, b, c, d, e, f: [a[0].T, b[0].T, c[0].T, d[0], e[0], f[0]]
    w_rb = rowblocks(w_in, w_gate, w_up, w_down, w_out, w_mem_kv)
    m_rb = rowblocks(m_w_in, m_w_gate, m_w_up, m_w_down, m_w_out, m_w_mem_kv)
    v_rb = rowblocks(v_w_in, v_w_gate, v_w_up, v_w_down, v_w_out, v_w_mem_kv)
    (winT_s,) = prep_weights("prep_w_in", w_rb[:1])
    cw_pad = jnp.zeros((8, 128), f32).at[:3, :HD].set(conv_w[0])
    (wgT_s, wuT_s, wd_s, wout_s, wmkv_s), (winT, cw_all) = prep_weights(
        "gather_w_in", w_rb[1:], _together([gather_two_legs([winT_s]), gather_exchange([cw_pad], [False])]))
    conv_w_full = jnp.transpose(cw_all.reshape(4, 8, 128)[:, :3, :HD], (1, 0, 2)).reshape(3, CONV_W)
    conv_w8 = jnp.zeros((8, CONV_W), f32).at[:3].set(conv_w_full)
    sink_rows = jnp.broadcast_to(attn_sinks.reshape(N_Q, 1), (N_Q, 128))

    x2d = x.reshape(T, D)
    mem2d = mem.reshape(-1, D)
    (xn, proj, qkv), near1 = in_proj_fwd(x2d, pk, winT, TM_BIG, gather_near_exchange([wgT_s, wout_s, wmkv_s], relay_early=1))
    (attn_out,), (wgT, wout, wmkv, *near2) = attn_fwd(
        qkv, sink_rows, BL, S, _together([gather_far_exchange(near1, relay_early=2), gather_near_exchange([wuT_s, wd_s])]))
    memn, kv, km, vm = mem_kv_fwd(mem2d, pk, wmkv)
    (conv_out, mem_out, merged, x1, h), (wuT, wd) = mixer_tail_fwd(
        x2d, attn_out, proj, qkv, km, vm, conv_w8, pk, wout, S, TM_BIG, gather_far_exchange(near2, relay_early=2))

    dx1, dx2b, act, d_gate, d_up, loss8, d_norm_ffn = ffn_fwd_bwd(h, x1, loss_target.reshape(T, D), wgT, wuT, wd, pk, TM)
    F = wd.shape[0]
    g_wd = matmul_tn(act, dx2b, "dw_down", F // 2, min(T, 1024))
    g_wgT = matmul_tn(d_gate, h, "dw_gate", F // 2, min(T, 1024))
    g_wuT = matmul_tn(d_up, h, "dw_up", F // 2, min(T, 1024))

    d_attn, d_conv_out, d_mem_out, g_wout, d_gains = out_proj_bwd(dx1, merged, attn_out, conv_out, mem_out, pk, wout, TM_BIG)
    late = [_halves_view(g) for g in (g_wgT, g_wuT, g_wd, g_wout)]
    (dqmn, dkm, dvm, dcb, dcv, d_cw8, d_cbias), late_sib = mem_conv_bwd(
        d_mem_out, mem_out, d_conv_out, proj, qkv, km, vm, conv_w8, pk, S, min(1024, S), halves_exchange(late))
    late_part = add_halves(cidx, late, late_sib, "grad_add_halves_ffn")
    (dqn, dkn, dv, d_sink8), late_stage = attn_bwd(qkv, d_attn, attn_out, sink_rows, BL, S, scatter_exchange(late_part))
    (g_x, g_winT, d_norm_mix, d_qg, d_kg, d_mqg), late_res = in_proj_bwd(
        dqn, dkn, dv, dcb, dcv, dqmn, proj, conv_w8, xn, x2d, dx1, pk, winT, S, TM,
        late_stage, w_rb[1:5], m_rb[1:5], v_rb[1:5])
    g_wmkv, d_norm_mem, d_mkg = mem_kv_bwd(dkm, dvm, kv, memn, mem2d, pk, wmkv)

    tot, tail_stage = tail_reduce(d_norm_mix, d_norm_mem, d_norm_ffn, d_gains, d_cw8, d_cbias, d_qg, d_kg, d_mqg, d_mkg,
                                  d_sink8, loss8, [_halves_view(g) for g in (g_winT, g_wmkv)])
    loss = tot[5, 384]
    tail_res, _ = adamw_big("adamw_tail", tail_stage, [w_rb[0], w_rb[5]], [m_rb[0], m_rb[5]], [v_rb[0], v_rb[5]], 4)
    res = {"w_in": [a.T[None] for a in tail_res[0]], "w_gate": [a.T[None] for a in late_res[0]],
           "w_up": [a.T[None] for a in late_res[1]], "w_down": [a[None] for a in late_res[2]],
           "w_out": [a[None] for a in late_res[3]], "w_mem_kv": [a[None] for a in tail_res[1]]}
    res.update(adamw_small(tot, pk, _pack_small(m_small), _pack_small(v_small), {k: w_small[k].shape for k in SMALL}))

    order = ["norm_mix", "w_in", "q_norm", "k_norm", "attn_sinks", "conv_w", "conv_b", "norm_mem", "w_mem_kv",
             "mem_q_norm", "mem_k_norm", "out_norm_attn", "out_norm_conv", "out_norm_mem", "w_out", "norm_ffn",
             "w_gate", "w_up", "w_down"]
    return (loss, g_x.reshape(BL, S, D), *[res[n][0] for n in order], *[res[n][1] for n in order],
            *[res[n][2] for n in order], *[res[n][3] for n in order])
```

```python
import collections
import functools

import jax
import jax.numpy as jnp
import numpy as np
from jax import lax
from jax.experimental import pallas as pl
from jax.experimental.pallas import tpu as pltpu

f32 = jnp.float32
MXU = jnp.bfloat16
WIRE = jnp.bfloat16
EPS = 1e-6
NEG = -1e30
HD = 64
BLK = 128
N_Q, N_KV, N_MEMH = 8, 2, 4
GQA = N_Q // N_KV
ATT_W, KV_W, CONV_W, MEM_W = 512, 128, 256, 256
VMEM_MIB = 1024 * 1024
ADAM_LR, ADAM_B1, ADAM_B2, ADAM_EPS, ADAM_WD, ADAM_STEP = 0.001, 0.9, 0.999, 1e-08, 0.01, 10

MESH = pl.DeviceIdType.MESH
VM = pl.BlockSpec(memory_space=pltpu.VMEM)
ANY = pl.BlockSpec(memory_space=pl.ANY)
SDS = jax.ShapeDtypeStruct
DMA = pltpu.SemaphoreType.DMA


def _c(v):
    return v.astype(MXU)


def _nn(a, b):
    return lax.dot_general(a, b, (((1,), (0,)), ((), ())), preferred_element_type=f32)


def _nt(a, b):
    return lax.dot_general(a, b, (((1,), (1,)), ((), ())), preferred_element_type=f32)


def _tn(a, b):
    return lax.dot_general(a, b, (((0,), (0,)), ((), ())), preferred_element_type=f32)


def _rstd(v):
    return lax.rsqrt(jnp.mean(v * v, axis=-1, keepdims=True) + EPS)


def _norm_bwd(dy, v, r, g):
    dyg = dy * g
    dv = r * dyg - v * (r * r * r) * jnp.mean(dyg * v, axis=-1, keepdims=True)
    return dv, jnp.sum(dy * v * r, axis=0, keepdims=True)


def _split3(v):
    hi = _c(v)
    r1 = v - hi.astype(f32)
    mid = _c(r1)
    return hi, mid, _c(r1 - mid.astype(f32))


def _rowsum_mxu(v, width):
    ones = jnp.ones((v.shape[1], width), MXU)
    return sum(_nn(a, ones) for a in _split3(v))


def _seg_sums(v):
    r = lax.broadcasted_iota(jnp.int32, (2 * HD, 2 * HD), 0) // HD
    c = lax.broadcasted_iota(jnp.int32, (2 * HD, 2 * HD), 1) // HD
    bd = (r == c).astype(MXU)
    outs = []
    for b in range(v.shape[1] // (2 * HD)):
        outs.append(sum(_nn(a, bd) for a in _split3(v[:, b * 2 * HD:(b + 1) * 2 * HD])))
    return outs[0] if len(outs) == 1 else jnp.concatenate(outs, axis=1)


def _lanes(g, width):
    return jnp.concatenate([g] * (width // HD), axis=1)


def _heads_rstd(v):
    return lax.rsqrt(_seg_sums(v * v) * (1.0 / HD) + EPS)


def _heads_norm_bwd(dy, v, g):
    r = _heads_rstd(v)
    gl = _lanes(g, v.shape[1])
    dyg = dy * gl
    dv = r * dyg - v * (r * r * r) * (_seg_sums(dyg * v) * (1.0 / HD))
    dgl = jnp.sum(dy * v * r, axis=0, keepdims=True)
    return dv, sum(dgl[:, s * HD:(s + 1) * HD] for s in range(v.shape[1] // HD))


def _exp_scores(s, extra=None):
    m = jnp.max(s, axis=-1, keepdims=True)
    if extra is None:
        return jnp.exp(s - m), None
    m = jnp.maximum(m, extra)
    return jnp.exp(s - m), jnp.exp(extra - m)


def _place():
    return lax.axis_index("x"), lax.axis_index("y"), lax.axis_index("c")


SMALL_AT = {"norm_mix": (0, 0, 1024), "norm_mem": (1, 0, 1024), "norm_ffn": (2, 0, 1024),
            "out_norm_attn": (3, 0, ATT_W), "out_norm_conv": (3, ATT_W, CONV_W), "out_norm_mem": (3, ATT_W + CONV_W, MEM_W),
            "conv_b": (4, 3 * CONV_W, CONV_W), "q_norm": (5, 0, HD), "k_norm": (5, HD, HD), "mem_q_norm": (5, 2 * HD, HD),
            "mem_k_norm": (5, 3 * HD, HD), "attn_sinks": (5, 256, N_Q)}
SMALL = ("norm_mix", "norm_mem", "norm_ffn", "out_norm_attn", "out_norm_conv", "out_norm_mem", "conv_w", "conv_b",
         "q_norm", "k_norm", "mem_q_norm", "mem_k_norm", "attn_sinks")


def _small(pk_ref, name):
    r, c0, w = SMALL_AT[name]
    return pk_ref[r:r + 1, c0:c0 + w]


def _pack_small(d):
    z = lambda n: jnp.zeros((1, n), f32)
    row3 = jnp.concatenate([d["out_norm_attn"], d["out_norm_conv"], d["out_norm_mem"]], axis=1)
    row4 = jnp.concatenate([d["conv_w"].reshape(1, 3 * HD), z(3 * CONV_W - 3 * HD), d["conv_b"]], axis=1)
    row5 = jnp.concatenate([d["q_norm"], d["k_norm"], d["mem_q_norm"], d["mem_k_norm"], d["attn_sinks"],
                            z(1024 - 4 * HD - N_Q)], axis=1)
    return jnp.concatenate([d["norm_mix"], d["norm_mem"], d["norm_ffn"], row3, row4, row5, z(1024), z(1024)], axis=0)


def _other_chips(x, y):
    return [(1 - x, y), (x, 1 - y), (1 - x, 1 - y)]


Exchange = collections.namedtuple("Exchange", "ins outs sems start finish relay relay_steps_before_end aliases",
                                  defaults=(None, 0, {}))


def _together(exchanges):
    def parts(refs, key):
        out, at = [], 0
        for ex in exchanges:
            out.append(refs[at:at + len(getattr(ex, key))])
            at += len(getattr(ex, key))
        return out

    def phase(name):
        def run(xa, xo, xs):
            for ex, a, o, s in zip(exchanges, parts(xa, "ins"), parts(xo, "outs"), parts(xs, "sems")):
                if getattr(ex, name) is not None:
                    getattr(ex, name)(a, o, s)
        return run

    aliases, ai, ao = {}, 0, 0
    for ex in exchanges:
        aliases.update({ai + i: ao + o for i, o in ex.aliases.items()})
        ai, ao = ai + len(ex.ins), ao + len(ex.outs)
    return Exchange([a for ex in exchanges for a in ex.ins], [o for ex in exchanges for o in ex.outs],
                    [s for ex in exchanges for s in ex.sems], phase("start"), phase("finish"), phase("relay"),
                    max(ex.relay_steps_before_end for ex in exchanges), aliases)


def _run(name, body, grid, ins, in_specs, out_shape, out_specs, scratch=(), vmem_mib=32, exchange=None):
    ins, in_specs, out_shape, out_specs, scratch = list(ins), list(in_specs), list(out_shape), list(out_specs), list(scratch)
    ni, no, ns = len(ins), len(out_shape), len(scratch)
    ex = exchange
    if ex is not None:
        nxi, nxo = len(ex.ins), len(ex.outs)

    def call_body(*refs):
        if ex is None:
            body(*refs)
            return
        a, xa = refs[:ni], refs[ni:ni + nxi]
        o, xo = refs[ni + nxi:ni + nxi + no], refs[ni + nxi + no:ni + nxi + no + nxo]
        s, xs = refs[ni + nxi + no + nxo:ni + nxi + no + nxo + ns], refs[ni + nxi + no + nxo + ns:]
        if grid:
            first = functools.reduce(jnp.logical_and, [pl.program_id(d) == 0 for d in range(len(grid))])
            last = functools.reduce(jnp.logical_and, [pl.program_id(d) == grid[d] - 1 for d in range(len(grid))])
            pl.when(first)(lambda: ex.start(xa, xo, xs))
            body(*a, *o, *s)
            if ex.relay is not None:
                early = functools.reduce(jnp.logical_and, [pl.program_id(d) == grid[d] - 1 for d in range(len(grid) - 1)],
                                         pl.program_id(len(grid) - 1) == max(grid[-1] - 1 - ex.relay_steps_before_end, 0))
                pl.when(early)(lambda: ex.relay(xa, xo, xs))
            pl.when(last)(lambda: ex.finish(xa, xo, xs))
        else:
            ex.start(xa, xo, xs)
            if body is not None:
                body(*a, *o, *s)
            if ex.relay is not None:
                ex.relay(xa, xo, xs)
            ex.finish(xa, xo, xs)

    kw = dict(grid=grid) if grid else {}
    if ex is not None:
        if ex.aliases:
            kw["input_output_aliases"] = {ni + i: no + o for i, o in ex.aliases.items()}
        ins, in_specs = ins + list(ex.ins), in_specs + [ANY] * nxi
        out_shape, out_specs = out_shape + list(ex.outs), out_specs + [ANY] * nxo
        scratch = scratch + list(ex.sems)
    res = pl.pallas_call(
        call_body, name=name, out_shape=out_shape, in_specs=in_specs, out_specs=out_specs, scratch_shapes=scratch,
        compiler_params=pltpu.CompilerParams(dimension_semantics=("arbitrary",) * len(grid) if grid else None,
                                             vmem_limit_bytes=vmem_mib * VMEM_MIB), **kw)(*ins)
    res = list(res)
    return (res[:no], res[no:]) if ex is not None else res


def _remote(src, dst, ssem, rsem, dev):
    return pltpu.make_async_remote_copy(src_ref=src, dst_ref=dst, send_sem=ssem, recv_sem=rsem,
                                        device_id=dev, device_id_type=MESH)


def gather_exchange(shards, split, relay_early=0):
    n = len(shards)

    def rows(ref, e, kk, half=None):
        R = shards[e].shape[0]
        if half is None:
            return ref.at[pl.ds(pl.multiple_of(kk * R, 8), R)]
        return ref.at[pl.ds(pl.multiple_of(kk * R + half * (R // 2), 8), R // 2)]

    def ici(src, dst, sm, e, j, chip_j, x, y, c):
        k = 2 * x + y
        if split[e]:
            s = src[e].at[pl.ds(pl.multiple_of(c * (shards[e].shape[0] // 2), 8), shards[e].shape[0] // 2)]
            return _remote(s, rows(dst[e], e, k, c), sm[0].at[6 * e + j], sm[1].at[6 * e + j], (*chip_j, c))
        return _remote(src[e], rows(dst[e], e, k), sm[0].at[6 * e + j], sm[1].at[6 * e + j], (*chip_j, c))

    def landed(dst, e, chip_j, c):
        kj = 2 * chip_j[0] + chip_j[1]
        return rows(dst[e], e, kj, c) if split[e] else rows(dst[e], e, kj)

    def forward(dst, sm, e, j, chip_j, x, y, c, sender_c):
        kj = 2 * chip_j[0] + chip_j[1]
        r = rows(dst[e], e, kj, sender_c)
        return _remote(r, r, sm[0].at[6 * e + 3 + j], sm[1].at[6 * e + 3 + j], (x, y, 1 - c))

    def local(src, dst, sm, e, x, y):
        return pltpu.make_async_copy(src[e], rows(dst[e], e, 2 * x + y), sm[2].at[e])

    def start(src, dst, sm):
        x, y, c = _place()
        for e in range(n):
            local(src, dst, sm, e, x, y).start()
            for j, chip_j in enumerate(_other_chips(x, y)):
                ici(src, dst, sm, e, j, chip_j, x, y, c).start()

    def relay(src, dst, sm):
        x, y, c = _place()
        for e in range(n):
            for j, chip_j in enumerate(_other_chips(x, y)):
                r = landed(dst, e, chip_j, c)
                _remote(r, r, sm[0].at[6 * e + j], sm[1].at[6 * e + j], (*chip_j, c)).wait_recv()
                if split[e]:
                    forward(dst, sm, e, j, chip_j, x, y, c, c).start()

    def finish(src, dst, sm):
        x, y, c = _place()
        chips = _other_chips(x, y)
        for e in range(n):
            for j, chip_j in enumerate(chips):
                if split[e]:
                    forward(dst, sm, e, j, chip_j, x, y, c, 1 - c).wait_recv()
        for e in range(n):
            for j, chip_j in enumerate(chips):
                ici(src, dst, sm, e, j, chip_j, x, y, c).wait_send()
                if split[e]:
                    forward(dst, sm, e, j, chip_j, x, y, c, c).wait_send()
            local(src, dst, sm, e, x, y).wait()

    outs = [SDS((4 * s.shape[0], s.shape[1]), s.dtype) for s in shards]
    return Exchange(list(shards), outs, [DMA((6 * n,)), DMA((6 * n,)), DMA((n,))], start, finish, relay, relay_early)


def _block_rows(ref, R, kk, half, quarter=None):
    hr = R // 2
    if quarter is None:
        return ref.at[pl.ds(pl.multiple_of(kk * R + half * hr, 8), hr)]
    return ref.at[pl.ds(pl.multiple_of(kk * R + half * hr + quarter * (hr // 2), 8), hr // 2)]


def gather_near_exchange(shards, relay_early=0):
    n = len(shards)
    R = [s.shape[0] for s in shards]

    def ici(src, dst, sm, e, j, chip_j, x, y, c):
        half = src[e].at[pl.ds(pl.multiple_of(c * (R[e] // 2), 8), R[e] // 2)]
        return _remote(half, _block_rows(dst[e], R[e], 2 * x + y, c), sm[0].at[4 * e + j], sm[1].at[4 * e + j], (*chip_j, c))

    def forward(dst, sm, e, j, chip_j, x, y, c, sender_c):
        r = _block_rows(dst[e], R[e], 2 * chip_j[0] + chip_j[1], sender_c)
        return _remote(r, r, sm[0].at[4 * e + 2 + j], sm[1].at[4 * e + 2 + j], (x, y, 1 - c))

    def local(src, dst, sm, e, x, y):
        return pltpu.make_async_copy(src[e], dst[e].at[pl.ds(pl.multiple_of((2 * x + y) * R[e], 8), R[e])], sm[2].at[e])

    def start(src, dst, sm):
        x, y, c = _place()
        for e in range(n):
            local(src, dst, sm, e, x, y).start()
            for j, chip_j in enumerate(_other_chips(x, y)[:2]):
                ici(src, dst, sm, e, j, chip_j, x, y, c).start()

    def relay(src, dst, sm):
        x, y, c = _place()
        for e in range(n):
            for j, chip_j in enumerate(_other_chips(x, y)[:2]):
                r = _block_rows(dst[e], R[e], 2 * chip_j[0] + chip_j[1], c)
                _remote(r, r, sm[0].at[4 * e + j], sm[1].at[4 * e + j], (*chip_j, c)).wait_recv()
                forward(dst, sm, e, j, chip_j, x, y, c, c).start()

    def finish(src, dst, sm):
        x, y, c = _place()
        near = _other_chips(x, y)[:2]
        for e in range(n):
            for j, chip_j in enumerate(near):
                forward(dst, sm, e, j, chip_j, x, y, c, 1 - c).wait_recv()
        for e in range(n):
            for j, chip_j in enumerate(near):
                ici(src, dst, sm, e, j, chip_j, x, y, c).wait_send()
                forward(dst, sm, e, j, chip_j, x, y, c, c).wait_send()
            local(src, dst, sm, e, x, y).wait()

    outs = [SDS((4 * s.shape[0], s.shape[1]), s.dtype) for s in shards]
    return Exchange(list(shards), outs, [DMA((4 * n,)), DMA((4 * n,)), DMA((n,))], start, finish, relay, relay_early)


def gather_far_exchange(bufs, relay_early=0):
    n = len(bufs)
    R = [b.shape[0] // 4 for b in bufs]

    def send(src, dst, sm, e, j, x, y, c):
        to, of = _other_chips(x, y)[j], _other_chips(x, y)[1 - j]
        kk = 2 * of[0] + of[1]
        return _remote(_block_rows(src[e], R[e], kk, c, j), _block_rows(dst[e], R[e], kk, c, j),
                       sm[0].at[4 * e + j], sm[1].at[4 * e + j], (*to, c))

    def landed(dst, e, j, x, y, half):
        return _block_rows(dst[e], R[e], 2 * (1 - x) + (1 - y), half, j)

    def forward(dst, sm, e, j, x, y, c, sender_c):
        r = landed(dst, e, j, x, y, sender_c)
        return _remote(r, r, sm[0].at[4 * e + 2 + j], sm[1].at[4 * e + 2 + j], (x, y, 1 - c))

    def start(src, dst, sm):
        x, y, c = _place()
        for e in range(n):
            for j in range(2):
                send(src, dst, sm, e, j, x, y, c).start()

    def relay(src, dst, sm):
        x, y, c = _place()
        for e in range(n):
            for j in range(2):
                r = landed(dst, e, j, x, y, c)
                _remote(r, r, sm[0].at[4 * e + j], sm[1].at[4 * e + j], (*_other_chips(x, y)[j], c)).wait_recv()
                forward(dst, sm, e, j, x, y, c, c).start()

    def finish(src, dst, sm):
        x, y, c = _place()
        for e in range(n):
            for j in range(2):
                forward(dst, sm, e, j, x, y, c, 1 - c).wait_recv()
        for e in range(n):
            for j in range(2):
                send(src, dst, sm, e, j, x, y, c).wait_send()
                forward(dst, sm, e, j, x, y, c, c).wait_send()

    outs = [SDS(b.shape, b.dtype) for b in bufs]
    return Exchange(list(bufs), outs, [DMA((4 * n,)), DMA((4 * n,))], start, finish, relay, relay_early,
                    {i: i for i in range(n)})


def gather_two_legs(shards):
    near = gather_near_exchange(shards)
    far = gather_far_exchange(near.outs)

    def finish(src, dst, sm):
        near.relay(src, dst, sm[:3])
        near.finish(src, dst, sm[:3])
        far.start(dst, dst, sm[3:])
        far.relay(dst, dst, sm[3:])
        far.finish(dst, dst, sm[3:])

    return Exchange(near.ins, near.outs, list(near.sems) + list(far.sems),
                    lambda src, dst, sm: near.start(src, dst, sm[:3]), finish)


def halves_exchange(grads):
    n = len(grads)

    def copy(g, st, sm, e, x, y, c):
        return _remote(g[e].at[:, 1 - c], st[e], sm[0].at[e], sm[1].at[e], (x, y, 1 - c))

    def start(g, st, sm):
        x, y, c = _place()
        for e in range(n):
            copy(g, st, sm, e, x, y, c).start()

    def finish(g, st, sm):
        x, y, c = _place()
        for e in range(n):
            copy(g, st, sm, e, x, y, c).wait()

    outs = [SDS((4,) + a.shape[2:], a.dtype) for a in grads]
    return Exchange(list(grads), outs, [DMA((n,)), DMA((n,))], start, finish)


def scatter_exchange(parts):
    n = len(parts)

    def ici(p, st, sm, e, j, chip_j, x, y, c):
        k, kj = 2 * x + y, 2 * chip_j[0] + chip_j[1]
        return _remote(p[e].at[kj], st[e].at[c, k], sm[0].at[8 * e + j], sm[1].at[8 * e + j], (*chip_j, c))

    def own(p, st, sm, e, x, y, c):
        k = 2 * x + y
        return _remote(p[e].at[k], st[e].at[c, k], sm[0].at[8 * e + 3], sm[1].at[8 * e + 3], (x, y, 1 - c))

    def forward(st, sm, e, j, chip_j, x, y, c, sender_c):
        kj = 2 * chip_j[0] + chip_j[1]
        r = st[e].at[sender_c, kj]
        return _remote(r, r, sm[0].at[8 * e + 4 + j], sm[1].at[8 * e + 4 + j], (x, y, 1 - c))

    def local(p, st, sm, e, x, y, c):
        k = 2 * x + y
        return pltpu.make_async_copy(p[e].at[k], st[e].at[c, k], sm[2].at[e])

    def start(p, st, sm, before_slot=None):
        x, y, c = _place()
        for j, chip_j in enumerate(_other_chips(x, y)):
            if before_slot is not None:
                before_slot(j, 2 * chip_j[0] + chip_j[1])
            for e in range(n):
                ici(p, st, sm, e, j, chip_j, x, y, c).start()
        if before_slot is not None:
            before_slot(3, 2 * x + y)
        for e in range(n):
            local(p, st, sm, e, x, y, c).start()
            own(p, st, sm, e, x, y, c).start()

    def relay(p, st, sm):
        x, y, c = _place()
        for e in range(n):
            for j, chip_j in enumerate(_other_chips(x, y)):
                kj = 2 * chip_j[0] + chip_j[1]
                r = st[e].at[c, kj]
                _remote(r, r, sm[0].at[8 * e + j], sm[1].at[8 * e + j], (*chip_j, c)).wait_recv()
                forward(st, sm, e, j, chip_j, x, y, c, c).start()

    def finish(p, st, sm):
        x, y, c = _place()
        k = 2 * x + y
        chips = _other_chips(x, y)
        for e in range(n):
            r = st[e].at[1 - c, k]
            _remote(r, r, sm[0].at[8 * e + 3], sm[1].at[8 * e + 3], (x, y, 1 - c)).wait_recv()
            for j, chip_j in enumerate(chips):
                forward(st, sm, e, j, chip_j, x, y, c, 1 - c).wait_recv()
        for e in range(n):
            own(p, st, sm, e, x, y, c).wait_send()
            for j, chip_j in enumerate(chips):
                ici(p, st, sm, e, j, chip_j, x, y, c).wait_send()
                forward(st, sm, e, j, chip_j, x, y, c, c).wait_send()
            local(p, st, sm, e, x, y, c).wait()

    outs = [SDS((2,) + a.shape, a.dtype) for a in parts]
    return Exchange(list(parts), outs, [DMA((8 * n,)), DMA((8 * n,)), DMA((n,))], start, finish, relay)


def tail_reduce(d_norm_mix, d_norm_mem, d_norm_ffn, d_gains, d_cw8, d_cbias, d_qg, d_kg, d_mqg, d_mkg, d_sink8, loss8, tail):
    n = len(tail)
    scatter = scatter_exchange([SDS((4,) + a.shape[2:], WIRE) for a in tail])

    def half_copy(g, sib, hsem, e, j, slot, x, y, c):
        return _remote(g[e].at[slot, 1 - c], sib[e].at[slot], hsem[0].at[4 * e + j], hsem[1].at[4 * e + j], (x, y, 1 - c))

    def body(nm_ref, nmem_ref, nf_ref, gn_ref, cw_ref, cb_ref, qg_ref, kg_ref, mqg_ref, mkg_ref, sk_ref, ls_ref, *rest):
        g, o_ref, st = rest[:n], rest[n], rest[n + 1:2 * n + 1]
        buf, ssem, rsem = rest[2 * n + 1:2 * n + 4]
        own, sib, part = (rest[2 * n + 4 + i * n:2 * n + 4 + (i + 1) * n] for i in range(3))
        lsem = rest[5 * n + 4]
        hsem, xsem = rest[5 * n + 5:5 * n + 7], rest[5 * n + 7:]
        x, y, c = _place()
        loads = [pltpu.make_async_copy(g[e].at[:, c], own[e], lsem.at[e]) for e in range(n)]
        for ld in loads:
            ld.start()
        for j, slot in enumerate([2 * cx + cy for cx, cy in _other_chips(x, y)] + [2 * x + y]):
            for e in range(n):
                half_copy(g, sib, hsem, e, j, slot, x, y, c).start()
        me = 4 * x + 2 * y + c
        mine = buf.at[me]
        mine[...] = jnp.zeros((8, 1024), f32)
        mine[0:1, :] = nm_ref[...]
        mine[1:2, :] = nmem_ref[...]
        mine[2:3, :] = nf_ref[...]
        mine[3:4, :] = gn_ref[...]
        for j in range(3):
            mine[4:5, pl.ds(j * CONV_W, CONV_W)] = cw_ref[j:j + 1, :]
        mine[4:5, pl.ds(3 * CONV_W, CONV_W)] = cb_ref[...]
        for j, r in enumerate((qg_ref, kg_ref, mqg_ref, mkg_ref)):
            mine[5:6, pl.ds(j * HD, HD)] = r[...]
        mine[5:6, pl.ds(256, 128)] = sk_ref[0:1, :]
        mine[5:6, pl.ds(384, 128)] = ls_ref[0:1, :]

        def peer_of(m):
            return (1 - x if m & 4 else x, 1 - y if m & 2 else y, 1 - c if m & 1 else c)

        for m in range(1, 8):
            _remote(mine, mine, ssem.at[m - 1], rsem.at[m - 1], peer_of(m)).start()
        for ld in loads:
            ld.wait()

        def chip_partial(j, slot):
            for e in range(n):
                half_copy(g, sib, hsem, e, j, slot, x, y, c).wait()
                part[e][slot] = (own[e][slot] + sib[e][slot]).astype(WIRE)

        scatter.start(part, st, xsem, chip_partial)
        scatter.relay(part, st, xsem)
        scatter.finish(part, st, xsem)
        for m in range(1, 8):
            p = peer_of(m)
            got = buf.at[4 * p[0] + 2 * p[1] + p[2]]
            _remote(got, got, ssem.at[m - 1], rsem.at[m - 1], p).wait_recv()
        for m in range(1, 8):
            _remote(mine, mine, ssem.at[m - 1], rsem.at[m - 1], peer_of(m)).wait_send()
        acc = buf[0]
        for d in range(1, 8):
            acc = acc + buf[d]
        o_ref[...] = acc

    ins = [d_norm_mix, d_norm_mem, d_norm_ffn, d_gains, d_cw8, d_cbias, d_qg, d_kg, d_mqg, d_mkg, d_sink8, loss8]
    half_shape = [(4,) + a.shape[2:] for a in tail]
    scratch = ([pltpu.VMEM((8, 8, 1024), f32), DMA((7,)), DMA((7,))]
               + [pltpu.VMEM(s, f32) for s in half_shape] * 2 + [pltpu.VMEM(s, WIRE) for s in half_shape]
               + [DMA((n,)), DMA((4 * n,)), DMA((4 * n,))] + list(scatter.sems))
    res = _run("tail_reduce", body, (), ins + list(tail), [VM] * len(ins) + [ANY] * n,
               [SDS((8, 1024), f32)] + list(scatter.outs), [VM] + [ANY] * n, scratch=scratch, vmem_mib=40)
    return res[0], res[1:]


def add_halves(cidx, grads, stages, name, nch=2):
    n = len(grads)

    def body(c_ref, *refs):
        g, st, o = refs[:n], refs[n:2 * n], refs[2 * n:]
        for e in range(n):
            o[e][...] = (g[e][...] + st[e][...].astype(f32)).astype(WIRE)

    in_specs, out_specs, out_shape = [], [], []
    for a in grads:
        hr, C = a.shape[2], a.shape[3]
        in_specs.append(pl.BlockSpec((None, None, hr // nch, C), lambda s, q, c_ref: (s, c_ref[0], q, 0)))
    for a in stages:
        hr, C = a.shape[1], a.shape[2]
        in_specs.append(pl.BlockSpec((None, hr // nch, C), lambda s, q, c_ref: (s, q, 0)))
        out_specs.append(pl.BlockSpec((None, hr // nch, C), lambda s, q, c_ref: (s, q, 0)))
        out_shape.append(SDS(a.shape, WIRE))
    return pl.pallas_call(
        body, name=name, out_shape=out_shape,
        grid_spec=pltpu.PrefetchScalarGridSpec(num_scalar_prefetch=1, grid=(4, nch), in_specs=in_specs, out_specs=out_specs),
        compiler_params=pltpu.CompilerParams(dimension_semantics=("arbitrary", "arbitrary")),
    )(cidx, *grads, *stages)


def _adamw_math(w, g, m, v):
    m = ADAM_B1 * m + (1.0 - ADAM_B1) * g
    v = ADAM_B2 * v + (1.0 - ADAM_B2) * (g * g)
    m_hat = m / (1.0 - ADAM_B1 ** ADAM_STEP)
    v_hat = v / (1.0 - ADAM_B2 ** ADAM_STEP)
    delta = -ADAM_LR * (m_hat / (jnp.sqrt(v_hat) + ADAM_EPS) + ADAM_WD * w)
    return delta, m, v


def _sum_chips(st):
    return ((st[0].astype(f32) + st[1].astype(f32)) + st[2].astype(f32)) + st[3].astype(f32)


def adamw_big(name, stages, ws, ms, vs, nstep, exchange=None):
    n = len(stages)

    def body(*refs):
        st, w, m, v = refs[:n], refs[n:2 * n], refs[2 * n:3 * n], refs[3 * n:4 * n]
        outs = refs[4 * n:]
        for e in range(n):
            g = jnp.concatenate([_sum_chips(st[e].at[0]), _sum_chips(st[e].at[1])], axis=0)
            d, mm, vv = _adamw_math(w[e][...], g, m[e][...], v[e][...])
            outs[4 * e][...] = g
            outs[4 * e + 1][...] = d
            outs[4 * e + 2][...] = mm
            outs[4 * e + 3][...] = vv

    st_specs, w_specs = [], []
    for e in range(n):
        _, _, hr, C = stages[e].shape
        st_specs.append(pl.BlockSpec((2, 4, hr, C // nstep), lambda i: (0, 0, 0, i)))
        w_specs.append(pl.BlockSpec((2 * hr, C // nstep), lambda i: (0, i)))
    out_specs = [s for s in w_specs for _ in range(4)]
    out_shape = [SDS(w.shape, f32) for w in ws for _ in range(4)]
    res = _run(name, body, (nstep,), list(stages) + list(ws) + list(ms) + list(vs), st_specs + w_specs * 3,
               out_shape, out_specs, vmem_mib=48, exchange=exchange)
    res, sent = res if exchange is not None else (res, None)
    return [res[4 * e:4 * e + 4] for e in range(n)], sent


def adamw_small(tot, pk_w, pk_m, pk_v, shapes):
    def body(tot_ref, w_ref, m_ref, v_ref, *outs):
        x, y, _ = _place()
        chip = 2 * x + y
        taps = []
        for j in range(3):
            mine = tot_ref[4:5, j * CONV_W:j * CONV_W + HD]
            for s in range(1, 4):
                mine = jnp.where(chip == s, tot_ref[4:5, j * CONV_W + s * HD:j * CONV_W + (s + 1) * HD], mine)
            taps.append(mine)
        row4 = jnp.concatenate(taps + [jnp.zeros((1, 3 * CONV_W - 3 * HD), f32), tot_ref[4:5, 3 * CONV_W:]], axis=1)
        tot_v = tot_ref[...]
        row = lax.broadcasted_iota(jnp.int32, tot_v.shape, 0)
        g = jnp.where(row == 4, jnp.broadcast_to(row4, tot_v.shape), tot_v)
        d, mm, vv = _adamw_math(w_ref[...], g, m_ref[...], v_ref[...])
        for i, name in enumerate(SMALL):
            for k, val in enumerate((g, d, mm, vv)):
                if name == "conv_w":
                    outs[4 * i + k][...] = jnp.concatenate([val[4:5, j * HD:(j + 1) * HD] for j in range(3)], axis=0)[None]
                else:
                    r, c0, w = SMALL_AT[name]
                    outs[4 * i + k][...] = val[r:r + 1, c0:c0 + w]

    out_shape = [SDS(shapes[k], f32) for k in SMALL for _ in range(4)]
    res = _run("adamw_small", body, (), [tot, pk_w, pk_m, pk_v], [VM] * 4, out_shape, [VM] * len(out_shape))
    return {k: res[4 * i:4 * i + 4] for i, k in enumerate(SMALL)}


def prep_weights(name, shards, exchange=None):
    n = len(shards)

    def body(*refs):
        for e in range(n):
            refs[n + e][...] = _c(refs[e][...])

    return _run(name, body, (), shards, [VM] * n, [SDS(a.shape, MXU) for a in shards], [VM] * n, vmem_mib=48, exchange=exchange)


def mem_kv_fwd(mem2d, pk, wmkv):
    M, D = mem2d.shape

    def body(m_ref, pk_ref, w_ref, mn_ref, kv_ref, km_ref, vm_ref):
        m = m_ref[...]
        mn = _c(m * _rstd(m) * _small(pk_ref, "norm_mem"))
        mn_ref[...] = mn
        kv = _nn(mn, w_ref[...])
        kv_ref[...] = kv
        kk = kv[:, :MEM_W]
        km_ref[...] = _c(kk * _heads_rstd(kk) * _lanes(_small(pk_ref, "mem_k_norm"), MEM_W))
        vm_ref[...] = _c(kv[:, MEM_W:])

    return _run("mem_kv_fwd", body, (), [mem2d, pk, wmkv], [VM] * 3,
                [SDS((M, D), MXU), SDS((M, 2 * MEM_W), f32), SDS((M, MEM_W), MXU), SDS((M, MEM_W), MXU)], [VM] * 4)


QKV_W = ATT_W + 2 * KV_W + MEM_W


def in_proj_fwd(x2d, pk, winT, tm, exchange):
    T, D = x2d.shape
    P = winT.shape[0]

    def body(x_ref, pk_ref, w_ref, xn_ref, proj_ref, qkv_ref):
        xv = x_ref[...]
        xn = _c(xv * _rstd(xv) * _small(pk_ref, "norm_mix"))
        xn_ref[...] = xn
        proj = _nt(xn, w_ref[...])
        proj_ref[...] = proj
        q, k = proj[:, :ATT_W], proj[:, ATT_W:ATT_W + KV_W]
        qm = proj[:, P - MEM_W:]
        qkv_ref[...] = jnp.concatenate(
            [_c(q * _heads_rstd(q) * _lanes(_small(pk_ref, "q_norm"), ATT_W)),
             _c(k * _heads_rstd(k) * _lanes(_small(pk_ref, "k_norm"), KV_W)),
             _c(proj[:, ATT_W + KV_W:ATT_W + 2 * KV_W]),
             _c(qm * _heads_rstd(qm) * _lanes(_small(pk_ref, "mem_q_norm"), MEM_W))], axis=1)

    return _run("in_proj_fwd", body, (T // tm,), [x2d, pk, winT],
                [pl.BlockSpec((tm, D), lambda i: (i, 0)), VM, VM],
                [SDS((T, D), MXU), SDS((T, P), f32), SDS((T, QKV_W), MXU)],
                [pl.BlockSpec((tm, D), lambda i: (i, 0)), pl.BlockSpec((tm, P), lambda i: (i, 0)),
                 pl.BlockSpec((tm, QKV_W), lambda i: (i, 0))],
                vmem_mib=40, exchange=exchange)


def _swa_bias_table():
    r = np.arange(GQA * BLK)[:, None]
    k = np.arange(2 * BLK)[None, :]
    dist = (r % BLK) + BLK - k
    band = (dist >= 0) & (dist < BLK)
    tab = np.empty((2, N_KV, GQA * BLK, 2 * BLK), np.float32)
    for later in range(2):
        valid = band & ((k >= BLK) | (later == 1))
        for g in range(N_KV):
            slope = 2.0 ** -(g * GQA + r // BLK + 1.0)
            tab[later, g] = np.where(valid, -slope * dist, NEG)
    return jnp.asarray(tab)


def _sink_column(g, sk_ref):
    hrow = lax.broadcasted_iota(jnp.int32, (GQA * BLK, 1), 0) // BLK
    sink = jnp.zeros((GQA * BLK, 1), f32)
    for hh in range(GQA):
        sink = jnp.where(hrow == hh, sk_ref[g * GQA + hh:g * GQA + hh + 1, 0:1], sink)
    return sink


def _stack_heads(v, g):
    return jnp.concatenate([v[:, (g * GQA + hh) * HD:(g * GQA + hh + 1) * HD] for hh in range(GQA)], axis=0)


def attn_fwd(qkv, sink_rows, BL, S, exchange, qb=2):
    NS = S // (qb * BLK)
    T = BL * S

    def body(q_ref, kc_ref, kp_ref, vc_ref, vp_ref, sk_ref, tab_ref, o_ref):
        j = pl.program_id(1)
        kall = jnp.concatenate([kp_ref[...], kc_ref[...]], axis=0)
        vall = jnp.concatenate([vp_ref[...], vc_ref[...]], axis=0)
        ones = jnp.ones((2 * BLK, HD), MXU)
        for b in range(qb):
            q = q_ref[pl.ds(b * BLK, BLK), :]
            k2, v2 = kall[b * BLK:(b + 2) * BLK], vall[b * BLK:(b + 2) * BLK]
            later = jnp.minimum(j, 1) if b == 0 else 1
            for g in range(N_KV):
                kn, vh = k2[:, g * HD:(g + 1) * HD], v2[:, g * HD:(g + 1) * HD]
                s = _nt(_stack_heads(q, g), kn) * (HD ** -0.5) + tab_ref[later, g]
                e, es = _exp_scores(s, _sink_column(g, sk_ref))
                eb = _c(e)
                o = _nn(eb, vh) * (1.0 / (_nn(eb, ones) + es))
                for hh in range(GQA):
                    o_ref[pl.ds(b * BLK, BLK), pl.ds((g * GQA + hh) * HD, HD)] = o[hh * BLK:(hh + 1) * BLK]

    cur = lambda col: (lambda b, j: (b * NS + j, col))
    prev = lambda col: (lambda b, j: (qb * (b * NS + j) - jnp.minimum(j, 1), col))
    return _run("attn_fwd", body, (BL, NS), [qkv, qkv, qkv, qkv, qkv, sink_rows, _swa_bias_table()],
                [pl.BlockSpec((qb * BLK, ATT_W), cur(0)),
                 pl.BlockSpec((qb * BLK, KV_W), cur(4)), pl.BlockSpec((BLK, KV_W), prev(4)),
                 pl.BlockSpec((qb * BLK, KV_W), cur(5)), pl.BlockSpec((BLK, KV_W), prev(5)),
                 pl.BlockSpec((8, 128), lambda b, j: (0, 0)), VM],
                [SDS((T, ATT_W), f32)], [pl.BlockSpec((qb * BLK, ATT_W), cur(0))], exchange=exchange)


def _conv_taps(u, uh):
    row = lax.broadcasted_iota(jnp.int32, u.shape, 0)
    u1 = jnp.where(row == 0, uh[7:8, :], pltpu.roll(u, 1, 0))
    u2 = jnp.where(row == 0, uh[6:7, :], jnp.where(row == 1, uh[7:8, :], pltpu.roll(u, 2, 0)))
    return u1, u2


def _mem_head(qm, km, vm, h):
    qh, kh, vh = (a[:, h * HD:(h + 1) * HD] for a in (qm, km, vm))
    e, _ = _exp_scores(_nt(qh, kh) * (HD ** -0.5))
    return qh, kh, vh, e


def mixer_tail_fwd(x2d, attn_out, proj, qkv, km, vm, conv_w8, pk, wout, S, tm, exchange):
    T, D = x2d.shape
    NM = km.shape[0] // (T // S)

    def body(x_ref, ao_ref, ch_ref, cb_ref, cc_ref, chh_ref, cch_ref, qm_ref, km_ref, vm_ref, cw_ref, pk_ref,
             wout_ref, co_ref, mo_ref, mg_ref, x1_ref, h_ref):
        first = (pl.program_id(0) * tm) % S == 0
        u = cc_ref[...] * ch_ref[...]
        uh = jnp.where(first, 0.0, cch_ref[...] * chh_ref[...])
        u1, u2 = _conv_taps(u, uh)
        conv = cw_ref[0:1, :] * u2 + cw_ref[1:2, :] * u1 + cw_ref[2:3, :] * u + _small(pk_ref, "conv_b")
        conv_out = cb_ref[...] * conv
        co_ref[...] = conv_out
        qm, kmv, vmv = qm_ref[...], km_ref[...], vm_ref[...]
        ones = jnp.ones((NM, HD), MXU)
        for h in range(N_MEMH):
            _, _, vh, e = _mem_head(qm, kmv, vmv, h)
            eb = _c(e)
            mo_ref[:, pl.ds(h * HD, HD)] = _nn(eb, vh) * (1.0 / _nn(eb, ones))
        mem_out = mo_ref[...]
        ao = ao_ref[...]
        merged = _c(jnp.concatenate([ao * _rstd(ao) * _small(pk_ref, "out_norm_attn"),
                                     conv_out * _rstd(conv_out) * _small(pk_ref, "out_norm_conv"),
                                     mem_out * _rstd(mem_out) * _small(pk_ref, "out_norm_mem")], axis=1))
        mg_ref[...] = merged
        x1 = x_ref[...] + _nn(merged, wout_ref[...])
        x1_ref[...] = x1
        h_ref[...] = _c(x1 * _rstd(x1) * _small(pk_ref, "norm_ffn"))

    tile = lambda w, col: pl.BlockSpec((tm, w), lambda i: (i, col))
    halo = lambda col: pl.BlockSpec((8, CONV_W), lambda i: (jnp.maximum(i * (tm // 8) - 1, 0), col))
    seq = pl.BlockSpec((NM, MEM_W), lambda i: ((i * tm) // S, 0))
    small = lambda a: pl.BlockSpec(a.shape, lambda i: (0, 0))
    return _run("mixer_tail_fwd", body, (T // tm,),
                [x2d, attn_out, proj, proj, proj, proj, proj, qkv, km, vm, conv_w8, pk, wout],
                [tile(D, 0), tile(ATT_W, 0), tile(CONV_W, 3), tile(CONV_W, 4), tile(CONV_W, 5), halo(3), halo(5),
                 tile(MEM_W, 3), seq, seq, VM, VM, VM],
                [SDS((T, CONV_W), f32), SDS((T, MEM_W), f32), SDS((T, D), MXU), SDS((T, D), f32), SDS((T, D), MXU)],
                [tile(CONV_W, 0), tile(MEM_W, 0), tile(D, 0), tile(D, 0), tile(D, 0)], vmem_mib=40, exchange=exchange)


def ffn_fwd_bwd(h, x1, tgt, wgT, wuT, wd, pk, tm):
    T, D = x1.shape
    F = wd.shape[0]

    def body(h_ref, x1_ref, t_ref, wg_ref, wu_ref, wd_ref, pk_ref,
             dx1_ref, dx2_ref, act_ref, dg_ref, du_ref, loss_ref, dgf_ref):
        @pl.when(pl.program_id(0) == 0)
        def _():
            loss_ref[...] = jnp.zeros_like(loss_ref)
            dgf_ref[...] = jnp.zeros_like(dgf_ref)

        hv = h_ref[...]
        gate = _nt(hv, wg_ref[...])
        up = _nt(hv, wu_ref[...])
        sg = jax.nn.sigmoid(gate)
        sl = gate * sg
        act = _c(sl * up)
        act_ref[...] = act
        x1v = x1_ref[...]
        diff = (x1v + _nn(act, wd_ref[...])) - t_ref[...]
        loss_ref[...] += 0.5 * jnp.sum(jnp.sum(diff * diff, axis=-1, keepdims=True) / D, axis=0, keepdims=True)
        dx2 = diff / D
        dx2b = _c(dx2)
        dx2_ref[...] = dx2b
        d_act = _nt(dx2b, wd_ref[...])
        d_up = _c(d_act * sl)
        d_gate = _c(d_act * up * (sg * (1.0 + gate * (1.0 - sg))))
        du_ref[...] = d_up
        dg_ref[...] = d_gate
        dh = _nn(d_gate, wg_ref[...]) + _nn(d_up, wu_ref[...])
        dv, dgf = _norm_bwd(dh, x1v, _rstd(x1v), _small(pk_ref, "norm_ffn"))
        dx1_ref[...] = dx2 + dv
        dgf_ref[...] += dgf

    tile = lambda w: pl.BlockSpec((tm, w), lambda i: (i, 0))
    return _run("ffn_fwd_bwd", body, (T // tm,), [h, x1, tgt, wgT, wuT, wd, pk],
                [tile(D), tile(D), tile(D), VM, VM, VM, VM],
                [SDS((T, D), f32), SDS((T, D), MXU), SDS((T, F), MXU), SDS((T, F), MXU), SDS((T, F), MXU),
                 SDS((8, 128), f32), SDS((1, D), f32)],
                [tile(D), tile(D), tile(F), tile(F), tile(F), pl.BlockSpec((8, 128), lambda i: (0, 0)),
                 pl.BlockSpec((1, D), lambda i: (0, 0))], vmem_mib=56)


def matmul_tn(a, b, name, tmo, tk):
    T, M = a.shape
    N = b.shape[1]
    nk = T // tk

    def body(a_ref, b_ref, o_ref, ow_ref):
        @pl.when(pl.program_id(1) == 0)
        def _():
            o_ref[...] = jnp.zeros_like(o_ref)

        o_ref[...] += _tn(a_ref[...], b_ref[...])

        @pl.when(pl.program_id(1) == nk - 1)
        def _():
            ow_ref[...] = o_ref[...].astype(WIRE)

    out = pl.BlockSpec((tmo, N), lambda m, k: (m, 0))
    return _run(name, body, (M // tmo, nk), [a, b],
                [pl.BlockSpec((tk, tmo), lambda m, k: (k, m)), pl.BlockSpec((tk, N), lambda m, k: (k, 0))],
                [SDS((M, N), f32), SDS((M, N), WIRE)], [out, out], vmem_mib=48)


def out_proj_bwd(dx1, merged, attn_out, conv_out, mem_out, pk, wout, tm):
    T, D = dx1.shape

    def body(dx1_ref, mg_ref, ao_ref, co_ref, mo_ref, pk_ref, w_ref,
             dao_ref, dco_ref, dmo_ref, dw_ref, dww_ref, dgain_ref):
        @pl.when(pl.program_id(0) == 0)
        def _():
            dw_ref[...] = jnp.zeros_like(dw_ref)
            dgain_ref[...] = jnp.zeros_like(dgain_ref)

        dxb = _c(dx1_ref[...])
        dw_ref[...] += _tn(mg_ref[...], dxb)

        @pl.when(pl.program_id(0) == T // tm - 1)
        def _():
            dww_ref[...] = dw_ref[...].astype(WIRE)

        dmg = _nt(dxb, w_ref[...])
        ao, co, mo = ao_ref[...], co_ref[...], mo_ref[...]
        da, ga = _norm_bwd(dmg[:, :ATT_W], ao, _rstd(ao), _small(pk_ref, "out_norm_attn"))
        dc, gc = _norm_bwd(dmg[:, ATT_W:ATT_W + CONV_W], co, _rstd(co), _small(pk_ref, "out_norm_conv"))
        dm, gm = _norm_bwd(dmg[:, ATT_W + CONV_W:], mo, _rstd(mo), _small(pk_ref, "out_norm_mem"))
        dao_ref[...] = da
        dco_ref[...] = dc
        dmo_ref[...] = dm
        dgain_ref[...] += jnp.concatenate([ga, gc, gm], axis=1)

    tile = lambda w: pl.BlockSpec((tm, w), lambda i: (i, 0))
    return _run("out_proj_bwd", body, (T // tm,), [dx1, merged, attn_out, conv_out, mem_out, pk, wout],
                [tile(D), tile(D), tile(ATT_W), tile(CONV_W), tile(MEM_W), VM, VM],
                [SDS((T, ATT_W), f32), SDS((T, CONV_W), f32), SDS((T, MEM_W), f32), SDS((D, D), f32), SDS((D, D), WIRE),
                 SDS((1, D), f32)],
                [tile(ATT_W), tile(CONV_W), tile(MEM_W), pl.BlockSpec((D, D), lambda i: (0, 0)),
                 pl.BlockSpec((D, D), lambda i: (0, 0)), pl.BlockSpec((1, D), lambda i: (0, 0))], vmem_mib=40)


def attn_bwd(qkv, d_attn, attn_out, sink_rows, BL, S, exchange):
    NB = S // BLK
    T = BL * S

    def body(q_ref, kc_ref, kp_ref, vc_ref, vp_ref, do_ref, ao_ref, sk_ref, tab_ref,
             dq_ref, dk_ref, dv_ref, dsk_ref, pend_k, pend_v):
        b, j = pl.program_id(0), pl.program_id(1)

        @pl.when((b == 0) & (j == 0))
        def _():
            dsk_ref[...] = jnp.zeros_like(dsk_ref)

        @pl.when(j == 0)
        def _():
            pend_k[...] = jnp.zeros_like(pend_k)
            pend_v[...] = jnp.zeros_like(pend_v)

        @pl.when(j < NB)
        def _():
            q, do, ao = q_ref[...], do_ref[...], ao_ref[...]
            k2 = jnp.concatenate([kp_ref[...], kc_ref[...]], axis=0)
            v2 = jnp.concatenate([vp_ref[...], vc_ref[...]], axis=0)
            lane = lax.broadcasted_iota(jnp.int32, (8, 128), 1)
            ones_w = jnp.ones((2 * BLK, 2 * BLK), MXU)
            dsk = jnp.zeros((8, 128), f32)
            dks, dvs = [], []
            for g in range(N_KV):
                kn, vh = k2[:, g * HD:(g + 1) * HD], v2[:, g * HD:(g + 1) * HD]
                qs = _stack_heads(q, g)
                s = _nt(qs, kn) * (HD ** -0.5) + tab_ref[g]
                e, es = _exp_scores(s, _sink_column(g, sk_ref))
                eb = _c(e)
                inv_w = 1.0 / (_nn(eb, ones_w) + es)
                inv_n = inv_w[:, :HD]
                dos = _stack_heads(do, g)
                delta = _rowsum_mxu(dos * _stack_heads(ao, g), 2 * BLK)
                dp = _nt(_c(dos), vh)
                ds = _c(e * inv_w * (dp - delta) * (HD ** -0.5))
                t = es * inv_n[:, 0:1] * delta[:, 0:1]
                for hh in range(GQA):
                    dsk = dsk + jnp.where(lane == g * GQA + hh, -jnp.sum(t[hh * BLK:(hh + 1) * BLK]), 0.0)
                dvs.append(_tn(eb, _c(dos * inv_n)))
                dks.append(_tn(ds, qs))
                dqs = _nn(ds, kn)
                for hh in range(GQA):
                    dq_ref[:, pl.ds((g * GQA + hh) * HD, HD)] = dqs[hh * BLK:(hh + 1) * BLK]
            dk2 = jnp.concatenate(dks, axis=1)
            dv2 = jnp.concatenate(dvs, axis=1)
            dk_ref[...] = pend_k[...] + dk2[:BLK]
            dv_ref[...] = pend_v[...] + dv2[:BLK]
            pend_k[...] = dk2[BLK:]
            pend_v[...] = dv2[BLK:]
            dsk_ref[...] += dsk

        @pl.when(j == NB)
        def _():
            dk_ref[...] = pend_k[...]
            dv_ref[...] = pend_v[...]

    cur = lambda col: (lambda b, j: (b * NB + jnp.minimum(j, NB - 1), col))
    prev = lambda col: (lambda b, j: (b * NB + jnp.maximum(j - 1, 0), col))
    small = lambda shape: pl.BlockSpec(shape, lambda b, j: (0, 0))
    return _run("attn_bwd", body, (BL, NB + 1), [qkv, qkv, qkv, qkv, qkv, d_attn, attn_out, sink_rows, _swa_bias_table()],
                [pl.BlockSpec((BLK, ATT_W), cur(0)),
                 pl.BlockSpec((BLK, KV_W), cur(4)), pl.BlockSpec((BLK, KV_W), prev(4)),
                 pl.BlockSpec((BLK, KV_W), cur(5)), pl.BlockSpec((BLK, KV_W), prev(5)),
                 pl.BlockSpec((BLK, ATT_W), cur(0)), pl.BlockSpec((BLK, ATT_W), cur(0)), small((8, 128)),
                 pl.BlockSpec((None, N_KV, GQA * BLK, 2 * BLK), lambda b, j: (jnp.minimum(j, 1), 0, 0, 0))],
                [SDS((T, ATT_W), f32), SDS((T, KV_W), f32), SDS((T, KV_W), f32), SDS((8, 128), f32)],
                [pl.BlockSpec((BLK, ATT_W), cur(0)), pl.BlockSpec((BLK, KV_W), prev(0)),
                 pl.BlockSpec((BLK, KV_W), prev(0)), small((8, 128))],
                scratch=[pltpu.VMEM((BLK, KV_W), f32)] * 2, exchange=exchange)


def mem_conv_bwd(d_mem_out, mem_out, d_conv_out, proj, qkv, km, vm, conv_w8, pk, S, tm, exchange):
    T = d_mem_out.shape[0]
    NM = km.shape[0] // (T // S)

    def body(dmo_ref, mo_ref, dco_ref, ch_ref, cb_ref, cc_ref, chh_ref, cch_ref, qm_ref, km_ref, vm_ref, cw_ref,
             pk_ref, dqm_ref, dkm_ref, dvm_ref, dcb_ref, dcv_ref, dcw_ref, dcbias_ref):
        i = pl.program_id(0)
        first = (i * tm) % S == 0

        @pl.when(i == 0)
        def _():
            dcw_ref[...] = jnp.zeros_like(dcw_ref)
            dcbias_ref[...] = jnp.zeros_like(dcbias_ref)

        @pl.when(first)
        def _():
            dkm_ref[...] = jnp.zeros_like(dkm_ref)
            dvm_ref[...] = jnp.zeros_like(dvm_ref)

        qm, kmv, vmv, dmo, mo = qm_ref[...], km_ref[...], vm_ref[...], dmo_ref[...], mo_ref[...]
        ones_w = jnp.ones((NM, NM), MXU)
        for h in range(N_MEMH):
            qh, kh, vh, e = _mem_head(qm, kmv, vmv, h)
            eb = _c(e)
            doh = dmo[:, h * HD:(h + 1) * HD]
            delta = _rowsum_mxu(doh * mo[:, h * HD:(h + 1) * HD], NM)
            dp = _nt(_c(doh), vh)
            inv_w = 1.0 / _nn(eb, ones_w)
            ds = _c(e * inv_w * (dp - delta) * (HD ** -0.5))
            dvm_ref[:, pl.ds(h * HD, HD)] += _tn(eb, _c(doh * inv_w[:, :HD]))
            dkm_ref[:, pl.ds(h * HD, HD)] += _tn(ds, qh)
            dqm_ref[:, pl.ds(h * HD, HD)] = _nn(ds, kh)

        u = cc_ref[...] * ch_ref[...]
        uh = jnp.where(first, 0.0, cch_ref[...] * chh_ref[...])
        u1, u2 = _conv_taps(u, uh)
        conv = cw_ref[0:1, :] * u2 + cw_ref[1:2, :] * u1 + cw_ref[2:3, :] * u + _small(pk_ref, "conv_b")
        dy = dco_ref[...]
        dcb_ref[...] = dy * conv
        dcv = dy * cb_ref[...]
        dcv_ref[...] = dcv
        dcbias_ref[...] += jnp.sum(dcv, axis=0, keepdims=True)
        dcw_ref[0:1, :] += jnp.sum(dcv * u2, axis=0, keepdims=True)
        dcw_ref[1:2, :] += jnp.sum(dcv * u1, axis=0, keepdims=True)
        dcw_ref[2:3, :] += jnp.sum(dcv * u, axis=0, keepdims=True)

    tile = lambda w, col: pl.BlockSpec((tm, w), lambda i: (i, col))
    halo = lambda col: pl.BlockSpec((8, CONV_W), lambda i: (jnp.maximum(i * (tm // 8) - 1, 0), col))
    seq = pl.BlockSpec((NM, MEM_W), lambda i: ((i * tm) // S, 0))
    const = lambda shape: pl.BlockSpec(shape, lambda i: (0, 0))
    return _run("mem_conv_bwd", body, (T // tm,),
                [d_mem_out, mem_out, d_conv_out, proj, proj, proj, proj, proj, qkv, km, vm, conv_w8, pk],
                [tile(MEM_W, 0), tile(MEM_W, 0), tile(CONV_W, 0), tile(CONV_W, 3), tile(CONV_W, 4), tile(CONV_W, 5),
                 halo(3), halo(5), tile(MEM_W, 3), seq, seq, VM, VM],
                [SDS((T, MEM_W), f32), SDS(km.shape, f32), SDS(km.shape, f32),
                 SDS((T, CONV_W), f32), SDS((T, CONV_W), f32), SDS((8, CONV_W), f32), SDS((1, CONV_W), f32)],
                [tile(MEM_W, 0), seq, seq, tile(CONV_W, 0), tile(CONV_W, 0), const((8, CONV_W)), const((1, CONV_W))],
                vmem_mib=48, exchange=exchange)


def in_proj_bwd(dqn, dkn, dv, dcb, dcv, dqmn, proj, conv_w8, xn, x2d, dx1, pk, winT, S, tm, stages, ws, ms, vs):
    T, D = x2d.shape
    P = winT.shape[0]
    last_blk = T // 8 - 1
    n = len(stages)
    nsteps = T // tm
    tile_w = ws[0].shape[1] // (nsteps // 2)
    turn = [e * 2 // n for e in range(n)]

    def body(dq_ref, dk_ref, dv_ref, dcb_ref, dcv_ref, dcvn_ref, dqm_ref, qa_ref, ka_ref, ch_ref, cc_ref, qma_ref,
             cw_ref, xn_ref, x_ref, dx1_ref, pk_ref, w_ref, *rest):
        st, aw, am, av = (rest[k * n:(k + 1) * n] for k in range(4))
        dx_ref, dw_ref, dg_ref, dqg_ref, dkg_ref, dmqg_ref = rest[4 * n:4 * n + 6]
        aouts = rest[4 * n + 6:]
        i = pl.program_id(0)

        for parity in range(2):
            @pl.when(i % 2 == parity)
            def _(parity=parity):
                for e in range(n):
                    if turn[e] == parity:
                        g = jnp.concatenate([_sum_chips(st[e].at[0]), _sum_chips(st[e].at[1])], axis=0)
                        d, mm, vv = _adamw_math(aw[e][...], g, am[e][...], av[e][...])
                        for k, val in enumerate((g, d, mm, vv)):
                            aouts[4 * e + k][...] = val

        @pl.when(i == 0)
        def _():
            dw_ref[...] = jnp.zeros_like(dw_ref)
            dg_ref[...] = jnp.zeros_like(dg_ref)
            dqg_ref[...] = jnp.zeros_like(dqg_ref)
            dkg_ref[...] = jnp.zeros_like(dkg_ref)
            dmqg_ref[...] = jnp.zeros_like(dmqg_ref)

        dqa, gq = _heads_norm_bwd(dq_ref[...], qa_ref[...], _small(pk_ref, "q_norm"))
        dka, gk = _heads_norm_bwd(dk_ref[...], ka_ref[...], _small(pk_ref, "k_norm"))
        dqma, gmq = _heads_norm_bwd(dqm_ref[...], qma_ref[...], _small(pk_ref, "mem_q_norm"))
        dqg_ref[...] += gq
        dkg_ref[...] += gk
        dmqg_ref[...] += gmq

        last = ((i + 1) * tm) % S == 0
        dcv = dcv_ref[...]
        nxt = jnp.where(last, 0.0, dcvn_ref[...])
        row = lax.broadcasted_iota(jnp.int32, dcv.shape, 0)
        n1 = jnp.where(row == tm - 1, nxt[0:1, :], pltpu.roll(dcv, tm - 1, 0))
        n2 = jnp.where(row == tm - 2, nxt[0:1, :], jnp.where(row == tm - 1, nxt[1:2, :], pltpu.roll(dcv, tm - 2, 0)))
        du = cw_ref[2:3, :] * dcv + cw_ref[1:2, :] * n1 + cw_ref[0:1, :] * n2
        d_proj = jnp.concatenate([_c(dqa), _c(dka), _c(dv_ref[...]), _c(du * cc_ref[...]),
                                  _c(dcb_ref[...]), _c(du * ch_ref[...]), _c(dqma)], axis=1)
        dw_ref[...] += _tn(d_proj, xn_ref[...])
        xv = x_ref[...]
        dv_, dg = _norm_bwd(_nn(d_proj, w_ref[...]), xv, _rstd(xv), _small(pk_ref, "norm_mix"))
        dx_ref[...] = dx1_ref[...] + dv_
        dg_ref[...] += dg

    tile = lambda w, col=0: pl.BlockSpec((tm, w), lambda i: (i, col))
    nhalo = pl.BlockSpec((8, CONV_W), lambda i: (jnp.minimum((i + 1) * (tm // 8), last_blk), 0))
    const = lambda shape: pl.BlockSpec(shape, lambda i: (0, 0))
    st_specs = [pl.BlockSpec((2, 4, s.shape[2], tile_w), lambda i: (0, 0, 0, i // 2)) for s in stages]
    w_specs = [pl.BlockSpec((w.shape[0], tile_w), lambda i: (0, i // 2)) for w in ws]
    res = _run("in_proj_bwd", body, (nsteps,),
               [dqn, dkn, dv, dcb, dcv, dcv, dqmn, proj, proj, proj, proj, proj, conv_w8, xn, x2d, dx1, pk, winT]
               + list(stages) + list(ws) + list(ms) + list(vs),
               [tile(ATT_W), tile(KV_W), tile(KV_W), tile(CONV_W), tile(CONV_W), nhalo, tile(MEM_W),
                tile(ATT_W, 0), tile(KV_W, 4), tile(CONV_W, 3), tile(CONV_W, 5), tile(MEM_W, 6), VM,
                tile(D), tile(D), tile(D), VM, VM] + st_specs + w_specs * 3,
               [SDS((T, D), f32), SDS((P, D), f32), SDS((1, D), f32), SDS((1, HD), f32), SDS((1, HD), f32),
                SDS((1, HD), f32)] + [SDS(w.shape, f32) for w in ws for _ in range(4)],
               [tile(D), pl.BlockSpec((P, D), lambda i: (0, 0)), const((1, D)), const((1, HD)), const((1, HD)),
                const((1, HD))] + [s for s in w_specs for _ in range(4)],
               vmem_mib=56)
    return res[:6], [res[6 + 4 * e:10 + 4 * e] for e in range(n)]


def mem_kv_bwd(dkm, dvm, kv, memn, mem2d, pk, wmkv):
    def body(dkm_ref, dvm_ref, kv_ref, mn_ref, m_ref, pk_ref, w_ref, dw_ref, dg_ref, dkg_ref):
        dkk, dkg = _heads_norm_bwd(dkm_ref[...], kv_ref[:, :MEM_W], _small(pk_ref, "mem_k_norm"))
        dkg_ref[...] = dkg
        dkv = _c(jnp.concatenate([dkk, dvm_ref[...]], axis=1))
        dw_ref[...] = _tn(mn_ref[...], dkv)
        mv = m_ref[...]
        dg_ref[...] = jnp.sum(_nt(dkv, w_ref[...]) * mv * _rstd(mv), axis=0, keepdims=True)

    return _run("mem_kv_bwd", body, (), [dkm, dvm, kv, memn, mem2d, pk, wmkv], [VM] * 7,
                [SDS(wmkv.shape, f32), SDS((1, mem2d.shape[1]), f32), SDS((1, HD), f32)], [VM] * 3, vmem_mib=40)


def _halves_view(g):
    return g.reshape(4, 2, g.shape[0] // 8, g.shape[1])


def kernel(x, mem, norm_mix, w_in, q_norm, k_norm, attn_sinks, conv_w, conv_b, norm_mem, w_mem_kv, mem_q_norm, mem_k_norm, out_norm_attn, out_norm_conv, out_norm_mem, w_out, norm_ffn, w_gate, w_up, w_down, loss_target, m_norm_mix, m_w_in, m_q_norm, m_k_norm, m_attn_sinks, m_conv_w, m_conv_b, m_norm_mem, m_w_mem_kv, m_mem_q_norm, m_mem_k_norm, m_out_norm_attn, m_out_norm_conv, m_out_norm_mem, m_w_out, m_norm_ffn, m_w_gate, m_w_up, m_w_down, v_norm_mix, v_w_in, v_q_norm, v_k_norm, v_attn_sinks, v_conv_w, v_conv_b, v_norm_mem, v_w_mem_kv, v_mem_q_norm, v_mem_k_norm, v_out_norm_attn, v_out_norm_conv, v_out_norm_mem, v_w_out, v_norm_ffn, v_w_gate, v_w_up, v_w_down):
    BL, S, D = x.shape
    T = BL * S
    TM = 256
    TM_BIG = min(512, S)
    _, _, ci = _place()
    cidx = ci.reshape(1).astype(jnp.int32)
    w_small = dict(norm_mix=norm_mix, norm_mem=norm_mem, norm_ffn=norm_ffn, out_norm_attn=out_norm_attn,
                   out_norm_conv=out_norm_conv, out_norm_mem=out_norm_mem, conv_w=conv_w, conv_b=conv_b, q_norm=q_norm,
                   k_norm=k_norm, mem_q_norm=mem_q_norm, mem_k_norm=mem_k_norm, attn_sinks=attn_sinks)
    m_small = dict(norm_mix=m_norm_mix, norm_mem=m_norm_mem, norm_ffn=m_norm_ffn, out_norm_attn=m_out_norm_attn,
                   out_norm_conv=m_out_norm_conv, out_norm_mem=m_out_norm_mem, conv_w=m_conv_w, conv_b=m_conv_b,
                   q_norm=m_q_norm, k_norm=m_k_norm, mem_q_norm=m_mem_q_norm, mem_k_norm=m_mem_k_norm,
                   attn_sinks=m_attn_sinks)
    v_small = dict(norm_mix=v_norm_mix, norm_mem=v_norm_mem, norm_ffn=v_norm_ffn, out_norm_attn=v_out_norm_attn,
                   out_norm_conv=v_out_norm_conv, out_norm_mem=v_out_norm_mem, conv_w=v_conv_w, conv_b=v_conv_b,
                   q_norm=v_q_norm, k_norm=v_k_norm, mem_q_norm=v_mem_q_norm, mem_k_norm=v_mem_k_norm,
                   attn_sinks=v_attn_sinks)
    pk = _pack_small(w_small)

    rowblocks = lambda a, b, c, d, e, f: [a[0].T, b[0].T, c[0].T, d[0], e[0], f[0]]
    w_rb = rowblocks(w_in, w_gate, w_up, w_down, w_out, w_mem_kv)
    m_rb = rowblocks(m_w_in, m_w_gate, m_w_up, m_w_down, m_w_out, m_w_mem_kv)
    v_rb = rowblocks(v_w_in, v_w_gate, v_w_up, v_w_down, v_w_out, v_w_mem_kv)
    (winT_s,) = prep_weights("prep_w_in", w_rb[:1])
    cw_pad = jnp.zeros((8, 128), f32).at[:3, :HD].set(conv_w[0])
    (wgT_s, wuT_s, wd_s, wout_s, wmkv_s), (winT, cw_all) = prep_weights(
        "gather_w_in", w_rb[1:], _together([gather_two_legs([winT_s]), gather_exchange([cw_pad], [False])]))
    conv_w_full = jnp.transpose(cw_all.reshape(4, 8, 128)[:, :3, :HD], (1, 0, 2)).reshape(3, CONV_W)
    conv_w8 = jnp.zeros((8, CONV_W), f32).at[:3].set(conv_w_full)
    sink_rows = jnp.broadcast_to(attn_sinks.reshape(N_Q, 1), (N_Q, 128))

    x2d = x.reshape(T, D)
    mem2d = mem.reshape(-1, D)
    (xn, proj, qkv), near1 = in_proj_fwd(x2d, pk, winT, TM_BIG, gather_near_exchange([wgT_s, wout_s, wmkv_s], relay_early=1))
    (attn_out,), (wgT, wout, wmkv, *near2) = attn_fwd(
        qkv, sink_rows, BL, S, _together([gather_far_exchange(near1, relay_early=2), gather_near_exchange([wuT_s, wd_s])]))
    memn, kv, km, vm = mem_kv_fwd(mem2d, pk, wmkv)
    (conv_out, mem_out, merged, x1, h), (wuT, wd) = mixer_tail_fwd(
        x2d, attn_out, proj, qkv, km, vm, conv_w8, pk, wout, S, TM_BIG, gather_far_exchange(near2, relay_early=2))

    dx1, dx2b, act, d_gate, d_up, loss8, d_norm_ffn = ffn_fwd_bwd(h, x1, loss_target.reshape(T, D), wgT, wuT, wd, pk, TM)
    F = wd.shape[0]
    g_wd, g_wd_w = matmul_tn(act, dx2b, "dw_down", F // 2, min(T, 1024))
    g_wgT, g_wgT_w = matmul_tn(d_gate, h, "dw_gate", F // 2, min(T, 1024))
    g_wuT, g_wuT_w = matmul_tn(d_up, h, "dw_up", F // 2, min(T, 1024))

    d_attn, d_conv_out, d_mem_out, g_wout, g_wout_w, d_gains = out_proj_bwd(
        dx1, merged, attn_out, conv_out, mem_out, pk, wout, TM_BIG)
    late = [_halves_view(g) for g in (g_wgT, g_wuT, g_wd, g_wout)]
    late_w = [_halves_view(g) for g in (g_wgT_w, g_wuT_w, g_wd_w, g_wout_w)]
    (dqmn, dkm, dvm, dcb, dcv, d_cw8, d_cbias), late_sib = mem_conv_bwd(
        d_mem_out, mem_out, d_conv_out, proj, qkv, km, vm, conv_w8, pk, S, min(1024, S), halves_exchange(late_w))
    late_part = add_halves(cidx, late, late_sib, "grad_add_halves_ffn")
    (dqn, dkn, dv, d_sink8), late_stage = attn_bwd(qkv, d_attn, attn_out, sink_rows, BL, S, scatter_exchange(late_part))
    (g_x, g_winT, d_norm_mix, d_qg, d_kg, d_mqg), late_res = in_proj_bwd(
        dqn, dkn, dv, dcb, dcv, dqmn, proj, conv_w8, xn, x2d, dx1, pk, winT, S, TM,
        late_stage, w_rb[1:5], m_rb[1:5], v_rb[1:5])
    g_wmkv, d_norm_mem, d_mkg = mem_kv_bwd(dkm, dvm, kv, memn, mem2d, pk, wmkv)

    tot, tail_stage = tail_reduce(d_norm_mix, d_norm_mem, d_norm_ffn, d_gains, d_cw8, d_cbias, d_qg, d_kg, d_mqg, d_mkg,
                                  d_sink8, loss8, [_halves_view(g) for g in (g_winT, g_wmkv)])
    loss = tot[5, 384]
    tail_res, _ = adamw_big("adamw_tail", tail_stage, [w_rb[0], w_rb[5]], [m_rb[0], m_rb[5]], [v_rb[0], v_rb[5]], 4)
    res = {"w_in": [a.T[None] for a in tail_res[0]], "w_gate": [a.T[None] for a in late_res[0]],
           "w_up": [a.T[None] for a in late_res[1]], "w_down": [a[None] for a in late_res[2]],
           "w_out": [a[None] for a in late_res[3]], "w_mem_kv": [a[None] for a in tail_res[1]]}
    res.update(adamw_small(tot, pk, _pack_small(m_small), _pack_small(v_small), {k: w_small[k].shape for k in SMALL}))

    order = ["norm_mix", "w_in", "q_norm", "k_norm", "attn_sinks", "conv_w", "conv_b", "norm_mem", "w_mem_kv",
             "mem_q_norm", "mem_k_norm", "out_norm_attn", "out_norm_conv", "out_norm_mem", "w_out", "norm_ffn",
             "w_gate", "w_up", "w_down"]
    return (loss, g_x.reshape(BL, S, D), *[res[n][0] for n in order], *[res[n][1] for n in order],
            *[res[n][2] for n in order], *[res[n][3] for n in order])
```

```python
import collections
import functools

import jax
import jax.numpy as jnp
import numpy as np
from jax import lax
from jax.experimental import pallas as pl
from jax.experimental.pallas import tpu as pltpu

f32 = jnp.float32
MXU = jnp.bfloat16
WIRE = jnp.bfloat16
EPS = 1e-6
NEG = -1e30
HD = 64
BLK = 128
N_Q, N_KV, N_MEMH = 8, 2, 4
GQA = N_Q // N_KV
ATT_W, KV_W, CONV_W, MEM_W = 512, 128, 256, 256
VMEM_MIB = 1024 * 1024
ADAM_LR, ADAM_B1, ADAM_B2, ADAM_EPS, ADAM_WD, ADAM_STEP = 0.001, 0.9, 0.999, 1e-08, 0.01, 10

MESH = pl.DeviceIdType.MESH
VM = pl.BlockSpec(memory_space=pltpu.VMEM)
ANY = pl.BlockSpec(memory_space=pl.ANY)
SDS = jax.ShapeDtypeStruct
DMA = pltpu.SemaphoreType.DMA


def _c(v):
    return v.astype(MXU)


def _nn(a, b):
    return lax.dot_general(a, b, (((1,), (0,)), ((), ())), preferred_element_type=f32)


def _nt(a, b):
    return lax.dot_general(a, b, (((1,), (1,)), ((), ())), preferred_element_type=f32)


def _tn(a, b):
    return lax.dot_general(a, b, (((0,), (0,)), ((), ())), preferred_element_type=f32)


def _rstd(v):
    return lax.rsqrt(jnp.mean(v * v, axis=-1, keepdims=True) + EPS)


def _norm_bwd(dy, v, r, g):
    dyg = dy * g
    dv = r * dyg - v * (r * r * r) * jnp.mean(dyg * v, axis=-1, keepdims=True)
    return dv, jnp.sum(dy * v * r, axis=0, keepdims=True)


def _split3(v):
    hi = _c(v)
    r1 = v - hi.astype(f32)
    mid = _c(r1)
    return hi, mid, _c(r1 - mid.astype(f32))


def _rowsum_mxu(v, width):
    ones = jnp.ones((v.shape[1], width), MXU)
    return sum(_nn(a, ones) for a in _split3(v))


def _seg_sums(v):
    r = lax.broadcasted_iota(jnp.int32, (2 * HD, 2 * HD), 0) // HD
    c = lax.broadcasted_iota(jnp.int32, (2 * HD, 2 * HD), 1) // HD
    bd = (r == c).astype(MXU)
    outs = []
    for b in range(v.shape[1] // (2 * HD)):
        outs.append(sum(_nn(a, bd) for a in _split3(v[:, b * 2 * HD:(b + 1) * 2 * HD])))
    return outs[0] if len(outs) == 1 else jnp.concatenate(outs, axis=1)


def _lanes(g, width):
    return jnp.concatenate([g] * (width // HD), axis=1)


def _heads_rstd(v):
    return lax.rsqrt(_seg_sums(v * v) * (1.0 / HD) + EPS)


def _heads_norm_bwd(dy, v, g):
    r = _heads_rstd(v)
    gl = _lanes(g, v.shape[1])
    dyg = dy * gl
    dv = r * dyg - v * (r * r * r) * (_seg_sums(dyg * v) * (1.0 / HD))
    dgl = jnp.sum(dy * v * r, axis=0, keepdims=True)
    return dv, sum(dgl[:, s * HD:(s + 1) * HD] for s in range(v.shape[1] // HD))


def _exp_scores(s, extra=None):
    m = jnp.max(s, axis=-1, keepdims=True)
    if extra is None:
        return jnp.exp(s - m), None
    m = jnp.maximum(m, extra)
    return jnp.exp(s - m), jnp.exp(extra - m)


def _place():
    return lax.axis_index("x"), lax.axis_index("y"), lax.axis_index("c")


SMALL_AT = {"norm_mix": (0, 0, 1024), "norm_mem": (1, 0, 1024), "norm_ffn": (2, 0, 1024),
            "out_norm_attn": (3, 0, ATT_W), "out_norm_conv": (3, ATT_W, CONV_W), "out_norm_mem": (3, ATT_W + CONV_W, MEM_W),
            "conv_b": (4, 3 * CONV_W, CONV_W), "q_norm": (5, 0, HD), "k_norm": (5, HD, HD), "mem_q_norm": (5, 2 * HD, HD),
            "mem_k_norm": (5, 3 * HD, HD), "attn_sinks": (5, 256, N_Q)}
SMALL = ("norm_mix", "norm_mem", "norm_ffn", "out_norm_attn", "out_norm_conv", "out_norm_mem", "conv_w", "conv_b",
         "q_norm", "k_norm", "mem_q_norm", "mem_k_norm", "attn_sinks")


def _small(pk_ref, name):
    r, c0, w = SMALL_AT[name]
    return pk_ref[r:r + 1, c0:c0 + w]


def _pack_small(d):
    z = lambda n: jnp.zeros((1, n), f32)
    row3 = jnp.concatenate([d["out_norm_attn"], d["out_norm_conv"], d["out_norm_mem"]], axis=1)
    row4 = jnp.concatenate([d["conv_w"].reshape(1, 3 * HD), z(3 * CONV_W - 3 * HD), d["conv_b"]], axis=1)
    row5 = jnp.concatenate([d["q_norm"], d["k_norm"], d["mem_q_norm"], d["mem_k_norm"], d["attn_sinks"],
                            z(1024 - 4 * HD - N_Q)], axis=1)
    return jnp.concatenate([d["norm_mix"], d["norm_mem"], d["norm_ffn"], row3, row4, row5, z(1024), z(1024)], axis=0)


def _other_chips(x, y):
    return [(1 - x, y), (x, 1 - y), (1 - x, 1 - y)]


Exchange = collections.namedtuple("Exchange", "ins outs sems start finish relays aliases", defaults=((), {}))


def _together(exchanges):
    def bounds(key):
        at, out = 0, []
        for ex in exchanges:
            out.append((at, at + len(getattr(ex, key))))
            at += len(getattr(ex, key))
        return out

    bi, bo, bs = bounds("ins"), bounds("outs"), bounds("sems")

    def of(i, fn):
        return lambda xa, xo, xs: fn(xa[bi[i][0]:bi[i][1]], xo[bo[i][0]:bo[i][1]], xs[bs[i][0]:bs[i][1]])

    def every(name):
        fns = [of(i, getattr(ex, name)) for i, ex in enumerate(exchanges)]

        def run(xa, xo, xs):
            for fn in fns:
                fn(xa, xo, xs)
        return run

    aliases = {}
    for i, ex in enumerate(exchanges):
        aliases.update({bi[i][0] + a: bo[i][0] + o for a, o in ex.aliases.items()})
    return Exchange([a for ex in exchanges for a in ex.ins], [o for ex in exchanges for o in ex.outs],
                    [s for ex in exchanges for s in ex.sems], every("start"), every("finish"),
                    [(sbe, of(i, fn)) for i, ex in enumerate(exchanges) for sbe, fn in ex.relays], aliases)


def _run(name, body, grid, ins, in_specs, out_shape, out_specs, scratch=(), vmem_mib=32, exchange=None):
    ins, in_specs, out_shape, out_specs, scratch = list(ins), list(in_specs), list(out_shape), list(out_specs), list(scratch)
    ni, no, ns = len(ins), len(out_shape), len(scratch)
    ex = exchange
    if ex is not None:
        nxi, nxo = len(ex.ins), len(ex.outs)

    def call_body(*refs):
        if ex is None:
            body(*refs)
            return
        a, xa = refs[:ni], refs[ni:ni + nxi]
        o, xo = refs[ni + nxi:ni + nxi + no], refs[ni + nxi + no:ni + nxi + no + nxo]
        s, xs = refs[ni + nxi + no + nxo:ni + nxi + no + nxo + ns], refs[ni + nxi + no + nxo + ns:]
        if grid:
            first = functools.reduce(jnp.logical_and, [pl.program_id(d) == 0 for d in range(len(grid))])
            last = functools.reduce(jnp.logical_and, [pl.program_id(d) == grid[d] - 1 for d in range(len(grid))])
            pl.when(first)(lambda: ex.start(xa, xo, xs))
            body(*a, *o, *s)
            nsteps = functools.reduce(lambda p, q: p * q, grid)
            for before_end, fn in ex.relays:
                at = np.unravel_index(max(nsteps - 1 - before_end, 0), grid)
                here = functools.reduce(jnp.logical_and, [pl.program_id(d) == int(at[d]) for d in range(len(grid))])
                pl.when(here)(functools.partial(fn, xa, xo, xs))
            pl.when(last)(lambda: ex.finish(xa, xo, xs))
        else:
            ex.start(xa, xo, xs)
            if body is not None:
                body(*a, *o, *s)
            for _, fn in ex.relays:
                fn(xa, xo, xs)
            ex.finish(xa, xo, xs)

    kw = dict(grid=grid) if grid else {}
    if ex is not None:
        if ex.aliases:
            kw["input_output_aliases"] = {ni + i: no + o for i, o in ex.aliases.items()}
        ins, in_specs = ins + list(ex.ins), in_specs + [ANY] * nxi
        out_shape, out_specs = out_shape + list(ex.outs), out_specs + [ANY] * nxo
        scratch = scratch + list(ex.sems)
    res = pl.pallas_call(
        call_body, name=name, out_shape=out_shape, in_specs=in_specs, out_specs=out_specs, scratch_shapes=scratch,
        compiler_params=pltpu.CompilerParams(dimension_semantics=("arbitrary",) * len(grid) if grid else None,
                                             vmem_limit_bytes=vmem_mib * VMEM_MIB), **kw)(*ins)
    res = list(res)
    return (res[:no], res[no:]) if ex is not None else res


def _remote(src, dst, ssem, rsem, dev):
    return pltpu.make_async_remote_copy(src_ref=src, dst_ref=dst, send_sem=ssem, recv_sem=rsem,
                                        device_id=dev, device_id_type=MESH)


def gather_exchange(shards, split, relay_early=0):
    n = len(shards)

    def rows(ref, e, kk, half=None):
        R = shards[e].shape[0]
        if half is None:
            return ref.at[pl.ds(pl.multiple_of(kk * R, 8), R)]
        return ref.at[pl.ds(pl.multiple_of(kk * R + half * (R // 2), 8), R // 2)]

    def ici(src, dst, sm, e, j, chip_j, x, y, c):
        k = 2 * x + y
        if split[e]:
            s = src[e].at[pl.ds(pl.multiple_of(c * (shards[e].shape[0] // 2), 8), shards[e].shape[0] // 2)]
            return _remote(s, rows(dst[e], e, k, c), sm[0].at[6 * e + j], sm[1].at[6 * e + j], (*chip_j, c))
        return _remote(src[e], rows(dst[e], e, k), sm[0].at[6 * e + j], sm[1].at[6 * e + j], (*chip_j, c))

    def landed(dst, e, chip_j, c):
        kj = 2 * chip_j[0] + chip_j[1]
        return rows(dst[e], e, kj, c) if split[e] else rows(dst[e], e, kj)

    def forward(dst, sm, e, j, chip_j, x, y, c, sender_c):
        kj = 2 * chip_j[0] + chip_j[1]
        r = rows(dst[e], e, kj, sender_c)
        return _remote(r, r, sm[0].at[6 * e + 3 + j], sm[1].at[6 * e + 3 + j], (x, y, 1 - c))

    def local(src, dst, sm, e, x, y):
        return pltpu.make_async_copy(src[e], rows(dst[e], e, 2 * x + y), sm[2].at[e])

    def start(src, dst, sm):
        x, y, c = _place()
        for e in range(n):
            local(src, dst, sm, e, x, y).start()
            for j, chip_j in enumerate(_other_chips(x, y)):
                ici(src, dst, sm, e, j, chip_j, x, y, c).start()

    def relay(src, dst, sm):
        x, y, c = _place()
        for e in range(n):
            for j, chip_j in enumerate(_other_chips(x, y)):
                r = landed(dst, e, chip_j, c)
                _remote(r, r, sm[0].at[6 * e + j], sm[1].at[6 * e + j], (*chip_j, c)).wait_recv()
                if split[e]:
                    forward(dst, sm, e, j, chip_j, x, y, c, c).start()

    def finish(src, dst, sm):
        x, y, c = _place()
        chips = _other_chips(x, y)
        for e in range(n):
            for j, chip_j in enumerate(chips):
                if split[e]:
                    forward(dst, sm, e, j, chip_j, x, y, c, 1 - c).wait_recv()
        for e in range(n):
            for j, chip_j in enumerate(chips):
                ici(src, dst, sm, e, j, chip_j, x, y, c).wait_send()
                if split[e]:
                    forward(dst, sm, e, j, chip_j, x, y, c, c).wait_send()
            local(src, dst, sm, e, x, y).wait()

    outs = [SDS((4 * s.shape[0], s.shape[1]), s.dtype) for s in shards]
    return Exchange(list(shards), outs, [DMA((6 * n,)), DMA((6 * n,)), DMA((n,))], start, finish, [(relay_early, relay)])


def _block_rows(ref, R, kk, half, quarter=None):
    hr = R // 2
    if quarter is None:
        return ref.at[pl.ds(pl.multiple_of(kk * R + half * hr, 8), hr)]
    return ref.at[pl.ds(pl.multiple_of(kk * R + half * hr + quarter * (hr // 2), 8), hr // 2)]


def gather_near_exchange(shards, relay_early=0):
    n = len(shards)
    R = [s.shape[0] for s in shards]

    def ici(src, dst, sm, e, j, chip_j, x, y, c):
        half = src[e].at[pl.ds(pl.multiple_of(c * (R[e] // 2), 8), R[e] // 2)]
        return _remote(half, _block_rows(dst[e], R[e], 2 * x + y, c), sm[0].at[4 * e + j], sm[1].at[4 * e + j], (*chip_j, c))

    def forward(dst, sm, e, j, chip_j, x, y, c, sender_c):
        r = _block_rows(dst[e], R[e], 2 * chip_j[0] + chip_j[1], sender_c)
        return _remote(r, r, sm[0].at[4 * e + 2 + j], sm[1].at[4 * e + 2 + j], (x, y, 1 - c))

    def local(src, dst, sm, e, x, y):
        return pltpu.make_async_copy(src[e], dst[e].at[pl.ds(pl.multiple_of((2 * x + y) * R[e], 8), R[e])], sm[2].at[e])

    def start(src, dst, sm):
        x, y, c = _place()
        for e in range(n):
            local(src, dst, sm, e, x, y).start()
            for j, chip_j in enumerate(_other_chips(x, y)[:2]):
                ici(src, dst, sm, e, j, chip_j, x, y, c).start()

    def relay(src, dst, sm):
        x, y, c = _place()
        for e in range(n):
            for j, chip_j in enumerate(_other_chips(x, y)[:2]):
                r = _block_rows(dst[e], R[e], 2 * chip_j[0] + chip_j[1], c)
                _remote(r, r, sm[0].at[4 * e + j], sm[1].at[4 * e + j], (*chip_j, c)).wait_recv()
                forward(dst, sm, e, j, chip_j, x, y, c, c).start()

    def finish(src, dst, sm):
        x, y, c = _place()
        near = _other_chips(x, y)[:2]
        for e in range(n):
            for j, chip_j in enumerate(near):
                forward(dst, sm, e, j, chip_j, x, y, c, 1 - c).wait_recv()
        for e in range(n):
            for j, chip_j in enumerate(near):
                ici(src, dst, sm, e, j, chip_j, x, y, c).wait_send()
                forward(dst, sm, e, j, chip_j, x, y, c, c).wait_send()
            local(src, dst, sm, e, x, y).wait()

    outs = [SDS((4 * s.shape[0], s.shape[1]), s.dtype) for s in shards]
    return Exchange(list(shards), outs, [DMA((4 * n,)), DMA((4 * n,)), DMA((n,))], start, finish, [(relay_early, relay)])


def gather_far_exchange(bufs, relay_early=0):
    n = len(bufs)
    R = [b.shape[0] // 4 for b in bufs]

    def send(src, dst, sm, e, j, x, y, c):
        to, of = _other_chips(x, y)[j], _other_chips(x, y)[1 - j]
        kk = 2 * of[0] + of[1]
        return _remote(_block_rows(src[e], R[e], kk, c, j), _block_rows(dst[e], R[e], kk, c, j),
                       sm[0].at[4 * e + j], sm[1].at[4 * e + j], (*to, c))

    def landed(dst, e, j, x, y, half):
        return _block_rows(dst[e], R[e], 2 * (1 - x) + (1 - y), half, j)

    def forward(dst, sm, e, j, x, y, c, sender_c):
        r = landed(dst, e, j, x, y, sender_c)
        return _remote(r, r, sm[0].at[4 * e + 2 + j], sm[1].at[4 * e + 2 + j], (x, y, 1 - c))

    def start(src, dst, sm):
        x, y, c = _place()
        for e in range(n):
            for j in range(2):
                send(src, dst, sm, e, j, x, y, c).start()

    def relay(src, dst, sm):
        x, y, c = _place()
        for e in range(n):
            for j in range(2):
                r = landed(dst, e, j, x, y, c)
                _remote(r, r, sm[0].at[4 * e + j], sm[1].at[4 * e + j], (*_other_chips(x, y)[j], c)).wait_recv()
                forward(dst, sm, e, j, x, y, c, c).start()

    def finish(src, dst, sm):
        x, y, c = _place()
        for e in range(n):
            for j in range(2):
                forward(dst, sm, e, j, x, y, c, 1 - c).wait_recv()
        for e in range(n):
            for j in range(2):
                send(src, dst, sm, e, j, x, y, c).wait_send()
                forward(dst, sm, e, j, x, y, c, c).wait_send()

    outs = [SDS(b.shape, b.dtype) for b in bufs]
    return Exchange(list(bufs), outs, [DMA((4 * n,)), DMA((4 * n,))], start, finish, [(relay_early, relay)],
                    {i: i for i in range(n)})


def gather_two_legs(shards):
    near = gather_near_exchange(shards)
    far = gather_far_exchange(near.outs)

    def finish(src, dst, sm):
        near.relays[0][1](src, dst, sm[:3])
        near.finish(src, dst, sm[:3])
        far.start(dst, dst, sm[3:])
        far.relays[0][1](dst, dst, sm[3:])
        far.finish(dst, dst, sm[3:])

    return Exchange(near.ins, near.outs, list(near.sems) + list(far.sems),
                    lambda src, dst, sm: near.start(src, dst, sm[:3]), finish)


def halves_exchange(grads):
    n = len(grads)

    def copy(g, st, sm, e, x, y, c):
        return _remote(g[e].at[:, 1 - c], st[e], sm[0].at[e], sm[1].at[e], (x, y, 1 - c))

    def start(g, st, sm):
        x, y, c = _place()
        for e in range(n):
            copy(g, st, sm, e, x, y, c).start()

    def finish(g, st, sm):
        x, y, c = _place()
        for e in range(n):
            copy(g, st, sm, e, x, y, c).wait()

    outs = [SDS((4,) + a.shape[2:], a.dtype) for a in grads]
    return Exchange(list(grads), outs, [DMA((n,)), DMA((n,))], start, finish)


def scatter_exchange(parts, relay_before_end=None):
    n = len(parts)
    relay_before_end = relay_before_end or [0] * n

    def ici(p, st, sm, e, j, chip_j, x, y, c):
        k, kj = 2 * x + y, 2 * chip_j[0] + chip_j[1]
        return _remote(p[e].at[kj], st[e].at[c, k], sm[0].at[8 * e + j], sm[1].at[8 * e + j], (*chip_j, c))

    def own(p, st, sm, e, x, y, c):
        k = 2 * x + y
        return _remote(p[e].at[k], st[e].at[c, k], sm[0].at[8 * e + 3], sm[1].at[8 * e + 3], (x, y, 1 - c))

    def forward(st, sm, e, j, chip_j, x, y, c, sender_c):
        kj = 2 * chip_j[0] + chip_j[1]
        r = st[e].at[sender_c, kj]
        return _remote(r, r, sm[0].at[8 * e + 4 + j], sm[1].at[8 * e + 4 + j], (x, y, 1 - c))

    def local(p, st, sm, e, x, y, c):
        k = 2 * x + y
        return pltpu.make_async_copy(p[e].at[k], st[e].at[c, k], sm[2].at[e])

    def start(p, st, sm, before_slot=None):
        x, y, c = _place()
        for j, chip_j in enumerate(_other_chips(x, y)):
            if before_slot is not None:
                before_slot(j, 2 * chip_j[0] + chip_j[1])
            for e in range(n):
                ici(p, st, sm, e, j, chip_j, x, y, c).start()
        if before_slot is not None:
            before_slot(3, 2 * x + y)
        for e in range(n):
            local(p, st, sm, e, x, y, c).start()
            own(p, st, sm, e, x, y, c).start()

    def relay(e, p, st, sm):
        x, y, c = _place()
        for j, chip_j in enumerate(_other_chips(x, y)):
            kj = 2 * chip_j[0] + chip_j[1]
            r = st[e].at[c, kj]
            _remote(r, r, sm[0].at[8 * e + j], sm[1].at[8 * e + j], (*chip_j, c)).wait_recv()
            forward(st, sm, e, j, chip_j, x, y, c, c).start()

    def finish(p, st, sm):
        x, y, c = _place()
        k = 2 * x + y
        chips = _other_chips(x, y)
        for e in range(n):
            r = st[e].at[1 - c, k]
            _remote(r, r, sm[0].at[8 * e + 3], sm[1].at[8 * e + 3], (x, y, 1 - c)).wait_recv()
            for j, chip_j in enumerate(chips):
                forward(st, sm, e, j, chip_j, x, y, c, 1 - c).wait_recv()
        for e in range(n):
            own(p, st, sm, e, x, y, c).wait_send()
            for j, chip_j in enumerate(chips):
                ici(p, st, sm, e, j, chip_j, x, y, c).wait_send()
                forward(st, sm, e, j, chip_j, x, y, c, c).wait_send()
            local(p, st, sm, e, x, y, c).wait()

    outs = [SDS((2,) + a.shape, a.dtype) for a in parts]
    return Exchange(list(parts), outs, [DMA((8 * n,)), DMA((8 * n,)), DMA((n,))], start, finish,
                    [(relay_before_end[e], functools.partial(relay, e)) for e in range(n)])


def tail_reduce(d_norm_mix, d_norm_mem, d_norm_ffn, d_gains, d_cw8, d_cbias, d_qg, d_kg, d_mqg, d_mkg, d_sink8, loss8, tail):
    n = len(tail)
    scatter = scatter_exchange([SDS((4,) + a.shape[2:], WIRE) for a in tail])

    def half_copy(g, sib, hsem, e, j, slot, x, y, c):
        return _remote(g[e].at[slot, 1 - c], sib[e].at[slot], hsem[0].at[4 * e + j], hsem[1].at[4 * e + j], (x, y, 1 - c))

    def body(nm_ref, nmem_ref, nf_ref, gn_ref, cw_ref, cb_ref, qg_ref, kg_ref, mqg_ref, mkg_ref, sk_ref, ls_ref, *rest):
        g, o_ref, st = rest[:n], rest[n], rest[n + 1:2 * n + 1]
        buf, ssem, rsem = rest[2 * n + 1:2 * n + 4]
        own, sib, part = (rest[2 * n + 4 + i * n:2 * n + 4 + (i + 1) * n] for i in range(3))
        lsem = rest[5 * n + 4]
        hsem, xsem = rest[5 * n + 5:5 * n + 7], rest[5 * n + 7:]
        x, y, c = _place()
        loads = [pltpu.make_async_copy(g[e].at[:, c], own[e], lsem.at[e]) for e in range(n)]
        for ld in loads:
            ld.start()
        for j, slot in enumerate([2 * cx + cy for cx, cy in _other_chips(x, y)] + [2 * x + y]):
            for e in range(n):
                half_copy(g, sib, hsem, e, j, slot, x, y, c).start()
        me = 4 * x + 2 * y + c
        mine = buf.at[me]
        mine[...] = jnp.zeros((8, 1024), f32)
        mine[0:1, :] = nm_ref[...]
        mine[1:2, :] = nmem_ref[...]
        mine[2:3, :] = nf_ref[...]
        mine[3:4, :] = gn_ref[...]
        for j in range(3):
            mine[4:5, pl.ds(j * CONV_W, CONV_W)] = cw_ref[j:j + 1, :]
        mine[4:5, pl.ds(3 * CONV_W, CONV_W)] = cb_ref[...]
        for j, r in enumerate((qg_ref, kg_ref, mqg_ref, mkg_ref)):
            mine[5:6, pl.ds(j * HD, HD)] = r[...]
        mine[5:6, pl.ds(256, 128)] = sk_ref[0:1, :]
        mine[5:6, pl.ds(384, 128)] = ls_ref[0:1, :]

        def peer_of(m):
            return (1 - x if m & 4 else x, 1 - y if m & 2 else y, 1 - c if m & 1 else c)

        for m in range(1, 8):
            _remote(mine, mine, ssem.at[m - 1], rsem.at[m - 1], peer_of(m)).start()
        for ld in loads:
            ld.wait()

        def chip_partial(j, slot):
            for e in range(n):
                half_copy(g, sib, hsem, e, j, slot, x, y, c).wait()
                part[e][slot] = (own[e][slot] + sib[e][slot]).astype(WIRE)

        scatter.start(part, st, xsem, chip_partial)
        for _, hand_on in scatter.relays:
            hand_on(part, st, xsem)
        scatter.finish(part, st, xsem)
        for m in range(1, 8):
            p = peer_of(m)
            got = buf.at[4 * p[0] + 2 * p[1] + p[2]]
            _remote(got, got, ssem.at[m - 1], rsem.at[m - 1], p).wait_recv()
        for m in range(1, 8):
            _remote(mine, mine, ssem.at[m - 1], rsem.at[m - 1], peer_of(m)).wait_send()
        acc = buf[0]
        for d in range(1, 8):
            acc = acc + buf[d]
        o_ref[...] = acc

    ins = [d_norm_mix, d_norm_mem, d_norm_ffn, d_gains, d_cw8, d_cbias, d_qg, d_kg, d_mqg, d_mkg, d_sink8, loss8]
    half_shape = [(4,) + a.shape[2:] for a in tail]
    scratch = ([pltpu.VMEM((8, 8, 1024), f32), DMA((7,)), DMA((7,))]
               + [pltpu.VMEM(s, f32) for s in half_shape] * 2 + [pltpu.VMEM(s, WIRE) for s in half_shape]
               + [DMA((n,)), DMA((4 * n,)), DMA((4 * n,))] + list(scatter.sems))
    res = _run("tail_reduce", body, (), ins + list(tail), [VM] * len(ins) + [ANY] * n,
               [SDS((8, 1024), f32)] + list(scatter.outs), [VM] + [ANY] * n, scratch=scratch, vmem_mib=40)
    return res[0], res[1:]


def add_halves(cidx, grads, stages, name, nch=2):
    n = len(grads)

    def body(c_ref, *refs):
        g, st, o = refs[:n], refs[n:2 * n], refs[2 * n:]
        for e in range(n):
            o[e][...] = (g[e][...] + st[e][...]).astype(WIRE)

    in_specs, out_specs, out_shape = [], [], []
    for a in grads:
        hr, C = a.shape[2], a.shape[3]
        in_specs.append(pl.BlockSpec((None, None, hr // nch, C), lambda s, q, c_ref: (s, c_ref[0], q, 0)))
    for a in stages:
        hr, C = a.shape[1], a.shape[2]
        in_specs.append(pl.BlockSpec((None, hr // nch, C), lambda s, q, c_ref: (s, q, 0)))
        out_specs.append(pl.BlockSpec((None, hr // nch, C), lambda s, q, c_ref: (s, q, 0)))
        out_shape.append(SDS(a.shape, WIRE))
    return pl.pallas_call(
        body, name=name, out_shape=out_shape,
        grid_spec=pltpu.PrefetchScalarGridSpec(num_scalar_prefetch=1, grid=(4, nch), in_specs=in_specs, out_specs=out_specs),
        compiler_params=pltpu.CompilerParams(dimension_semantics=("arbitrary", "arbitrary")),
    )(cidx, *grads, *stages)


def _adamw_math(w, g, m, v):
    m = ADAM_B1 * m + (1.0 - ADAM_B1) * g
    v = ADAM_B2 * v + (1.0 - ADAM_B2) * (g * g)
    m_hat = m / (1.0 - ADAM_B1 ** ADAM_STEP)
    v_hat = v / (1.0 - ADAM_B2 ** ADAM_STEP)
    delta = -ADAM_LR * (m_hat / (jnp.sqrt(v_hat) + ADAM_EPS) + ADAM_WD * w)
    return delta, m, v


def _sum_chips(st):
    return ((st[0].astype(f32) + st[1].astype(f32)) + st[2].astype(f32)) + st[3].astype(f32)


def adamw_big(name, stages, ws, ms, vs, nstep, exchange=None):
    n = len(stages)

    def body(*refs):
        st, w, m, v = refs[:n], refs[n:2 * n], refs[2 * n:3 * n], refs[3 * n:4 * n]
        outs = refs[4 * n:]
        for e in range(n):
            g = jnp.concatenate([_sum_chips(st[e].at[0]), _sum_chips(st[e].at[1])], axis=0)
            d, mm, vv = _adamw_math(w[e][...], g, m[e][...], v[e][...])
            outs[4 * e][...] = g
            outs[4 * e + 1][...] = d
            outs[4 * e + 2][...] = mm
            outs[4 * e + 3][...] = vv

    st_specs, w_specs = [], []
    for e in range(n):
        _, _, hr, C = stages[e].shape
        st_specs.append(pl.BlockSpec((2, 4, hr, C // nstep), lambda i: (0, 0, 0, i)))
        w_specs.append(pl.BlockSpec((2 * hr, C // nstep), lambda i: (0, i)))
    out_specs = [s for s in w_specs for _ in range(4)]
    out_shape = [SDS(w.shape, f32) for w in ws for _ in range(4)]
    res = _run(name, body, (nstep,), list(stages) + list(ws) + list(ms) + list(vs), st_specs + w_specs * 3,
               out_shape, out_specs, vmem_mib=48, exchange=exchange)
    res, sent = res if exchange is not None else (res, None)
    return [res[4 * e:4 * e + 4] for e in range(n)], sent


def adamw_small(tot, pk_w, pk_m, pk_v, shapes):
    def body(tot_ref, w_ref, m_ref, v_ref, *outs):
        x, y, _ = _place()
        chip = 2 * x + y
        taps = []
        for j in range(3):
            mine = tot_ref[4:5, j * CONV_W:j * CONV_W + HD]
            for s in range(1, 4):
                mine = jnp.where(chip == s, tot_ref[4:5, j * CONV_W + s * HD:j * CONV_W + (s + 1) * HD], mine)
            taps.append(mine)
        row4 = jnp.concatenate(taps + [jnp.zeros((1, 3 * CONV_W - 3 * HD), f32), tot_ref[4:5, 3 * CONV_W:]], axis=1)
        tot_v = tot_ref[...]
        row = lax.broadcasted_iota(jnp.int32, tot_v.shape, 0)
        g = jnp.where(row == 4, jnp.broadcast_to(row4, tot_v.shape), tot_v)
        d, mm, vv = _adamw_math(w_ref[...], g, m_ref[...], v_ref[...])
        for i, name in enumerate(SMALL):
            for k, val in enumerate((g, d, mm, vv)):
                if name == "conv_w":
                    outs[4 * i + k][...] = jnp.concatenate([val[4:5, j * HD:(j + 1) * HD] for j in range(3)], axis=0)[None]
                else:
                    r, c0, w = SMALL_AT[name]
                    outs[4 * i + k][...] = val[r:r + 1, c0:c0 + w]

    out_shape = [SDS(shapes[k], f32) for k in SMALL for _ in range(4)]
    res = _run("adamw_small", body, (), [tot, pk_w, pk_m, pk_v], [VM] * 4, out_shape, [VM] * len(out_shape))
    return {k: res[4 * i:4 * i + 4] for i, k in enumerate(SMALL)}


def prep_weights(name, shards, exchange=None):
    n = len(shards)

    def body(*refs):
        for e in range(n):
            refs[n + e][...] = _c(refs[e][...])

    return _run(name, body, (), shards, [VM] * n, [SDS(a.shape, MXU) for a in shards], [VM] * n, vmem_mib=48, exchange=exchange)


def mem_kv_fwd(mem2d, pk, wmkv):
    M, D = mem2d.shape

    def body(m_ref, pk_ref, w_ref, mn_ref, kv_ref, km_ref, vm_ref):
        m = m_ref[...]
        mn = _c(m * _rstd(m) * _small(pk_ref, "norm_mem"))
        mn_ref[...] = mn
        kv = _nn(mn, w_ref[...])
        kv_ref[...] = kv
        kk = kv[:, :MEM_W]
        km_ref[...] = _c(kk * _heads_rstd(kk) * _lanes(_small(pk_ref, "mem_k_norm"), MEM_W))
        vm_ref[...] = _c(kv[:, MEM_W:])

    return _run("mem_kv_fwd", body, (), [mem2d, pk, wmkv], [VM] * 3,
                [SDS((M, D), MXU), SDS((M, 2 * MEM_W), f32), SDS((M, MEM_W), MXU), SDS((M, MEM_W), MXU)], [VM] * 4)


QKV_W = ATT_W + 2 * KV_W + MEM_W


def in_proj_fwd(x2d, pk, winT, tm, exchange):
    T, D = x2d.shape
    P = winT.shape[0]

    def body(x_ref, pk_ref, w_ref, xn_ref, proj_ref, qkv_ref):
        xv = x_ref[...]
        xn = _c(xv * _rstd(xv) * _small(pk_ref, "norm_mix"))
        xn_ref[...] = xn
        proj = _nt(xn, w_ref[...])
        proj_ref[...] = proj
        q, k = proj[:, :ATT_W], proj[:, ATT_W:ATT_W + KV_W]
        qm = proj[:, P - MEM_W:]
        qkv_ref[...] = jnp.concatenate(
            [_c(q * _heads_rstd(q) * _lanes(_small(pk_ref, "q_norm"), ATT_W)),
             _c(k * _heads_rstd(k) * _lanes(_small(pk_ref, "k_norm"), KV_W)),
             _c(proj[:, ATT_W + KV_W:ATT_W + 2 * KV_W]),
             _c(qm * _heads_rstd(qm) * _lanes(_small(pk_ref, "mem_q_norm"), MEM_W))], axis=1)

    return _run("in_proj_fwd", body, (T // tm,), [x2d, pk, winT],
                [pl.BlockSpec((tm, D), lambda i: (i, 0)), VM, VM],
                [SDS((T, D), MXU), SDS((T, P), f32), SDS((T, QKV_W), MXU)],
                [pl.BlockSpec((tm, D), lambda i: (i, 0)), pl.BlockSpec((tm, P), lambda i: (i, 0)),
                 pl.BlockSpec((tm, QKV_W), lambda i: (i, 0))],
                vmem_mib=40, exchange=exchange)


def _swa_bias_table():
    r = np.arange(GQA * BLK)[:, None]
    k = np.arange(2 * BLK)[None, :]
    dist = (r % BLK) + BLK - k
    band = (dist >= 0) & (dist < BLK)
    tab = np.empty((2, N_KV, GQA * BLK, 2 * BLK), np.float32)
    for later in range(2):
        valid = band & ((k >= BLK) | (later == 1))
        for g in range(N_KV):
            slope = 2.0 ** -(g * GQA + r // BLK + 1.0)
            tab[later, g] = np.where(valid, -slope * dist, NEG)
    return jnp.asarray(tab)


def _sink_column(g, sk_ref):
    hrow = lax.broadcasted_iota(jnp.int32, (GQA * BLK, 1), 0) // BLK
    sink = jnp.zeros((GQA * BLK, 1), f32)
    for hh in range(GQA):
        sink = jnp.where(hrow == hh, sk_ref[g * GQA + hh:g * GQA + hh + 1, 0:1], sink)
    return sink


def _stack_heads(v, g):
    return jnp.concatenate([v[:, (g * GQA + hh) * HD:(g * GQA + hh + 1) * HD] for hh in range(GQA)], axis=0)


def attn_fwd(qkv, sink_rows, BL, S, exchange, qb=2):
    NS = S // (qb * BLK)
    T = BL * S

    def body(q_ref, kc_ref, kp_ref, vc_ref, vp_ref, sk_ref, tab_ref, o_ref):
        j = pl.program_id(1)
        kall = jnp.concatenate([kp_ref[...], kc_ref[...]], axis=0)
        vall = jnp.concatenate([vp_ref[...], vc_ref[...]], axis=0)
        ones = jnp.ones((2 * BLK, HD), MXU)
        for b in range(qb):
            q = q_ref[pl.ds(b * BLK, BLK), :]
            k2, v2 = kall[b * BLK:(b + 2) * BLK], vall[b * BLK:(b + 2) * BLK]
            later = jnp.minimum(j, 1) if b == 0 else 1
            for g in range(N_KV):
                kn, vh = k2[:, g * HD:(g + 1) * HD], v2[:, g * HD:(g + 1) * HD]
                s = _nt(_stack_heads(q, g), kn) * (HD ** -0.5) + tab_ref[later, g]
                e, es = _exp_scores(s, _sink_column(g, sk_ref))
                eb = _c(e)
                o = _nn(eb, vh) * (1.0 / (_nn(eb, ones) + es))
                for hh in range(GQA):
                    o_ref[pl.ds(b * BLK, BLK), pl.ds((g * GQA + hh) * HD, HD)] = o[hh * BLK:(hh + 1) * BLK]

    cur = lambda col: (lambda b, j: (b * NS + j, col))
    prev = lambda col: (lambda b, j: (qb * (b * NS + j) - jnp.minimum(j, 1), col))
    return _run("attn_fwd", body, (BL, NS), [qkv, qkv, qkv, qkv, qkv, sink_rows, _swa_bias_table()],
                [pl.BlockSpec((qb * BLK, ATT_W), cur(0)),
                 pl.BlockSpec((qb * BLK, KV_W), cur(4)), pl.BlockSpec((BLK, KV_W), prev(4)),
                 pl.BlockSpec((qb * BLK, KV_W), cur(5)), pl.BlockSpec((BLK, KV_W), prev(5)),
                 pl.BlockSpec((8, 128), lambda b, j: (0, 0)), VM],
                [SDS((T, ATT_W), f32)], [pl.BlockSpec((qb * BLK, ATT_W), cur(0))], exchange=exchange)


def _conv_taps(u, uh):
    row = lax.broadcasted_iota(jnp.int32, u.shape, 0)
    u1 = jnp.where(row == 0, uh[7:8, :], pltpu.roll(u, 1, 0))
    u2 = jnp.where(row == 0, uh[6:7, :], jnp.where(row == 1, uh[7:8, :], pltpu.roll(u, 2, 0)))
    return u1, u2


def _mem_head(qm, km, vm, h):
    qh, kh, vh = (a[:, h * HD:(h + 1) * HD] for a in (qm, km, vm))
    e, _ = _exp_scores(_nt(qh, kh) * (HD ** -0.5))
    return qh, kh, vh, e


def mixer_tail_fwd(x2d, attn_out, proj, qkv, km, vm, conv_w8, pk, wout, S, tm, exchange):
    T, D = x2d.shape
    NM = km.shape[0] // (T // S)

    def body(x_ref, ao_ref, ch_ref, cb_ref, cc_ref, chh_ref, cch_ref, qm_ref, km_ref, vm_ref, cw_ref, pk_ref,
             wout_ref, co_ref, mo_ref, mg_ref, x1_ref, h_ref):
        first = (pl.program_id(0) * tm) % S == 0
        u = cc_ref[...] * ch_ref[...]
        uh = jnp.where(first, 0.0, cch_ref[...] * chh_ref[...])
        u1, u2 = _conv_taps(u, uh)
        conv = cw_ref[0:1, :] * u2 + cw_ref[1:2, :] * u1 + cw_ref[2:3, :] * u + _small(pk_ref, "conv_b")
        conv_out = cb_ref[...] * conv
        co_ref[...] = conv_out
        qm, kmv, vmv = qm_ref[...], km_ref[...], vm_ref[...]
        ones = jnp.ones((NM, HD), MXU)
        for h in range(N_MEMH):
            _, _, vh, e = _mem_head(qm, kmv, vmv, h)
            eb = _c(e)
            mo_ref[:, pl.ds(h * HD, HD)] = _nn(eb, vh) * (1.0 / _nn(eb, ones))
        mem_out = mo_ref[...]
        ao = ao_ref[...]
        merged = _c(jnp.concatenate([ao * _rstd(ao) * _small(pk_ref, "out_norm_attn"),
                                     conv_out * _rstd(conv_out) * _small(pk_ref, "out_norm_conv"),
                                     mem_out * _rstd(mem_out) * _small(pk_ref, "out_norm_mem")], axis=1))
        mg_ref[...] = merged
        x1 = x_ref[...] + _nn(merged, wout_ref[...])
        x1_ref[...] = x1
        h_ref[...] = _c(x1 * _rstd(x1) * _small(pk_ref, "norm_ffn"))

    tile = lambda w, col: pl.BlockSpec((tm, w), lambda i: (i, col))
    halo = lambda col: pl.BlockSpec((8, CONV_W), lambda i: (jnp.maximum(i * (tm // 8) - 1, 0), col))
    seq = pl.BlockSpec((NM, MEM_W), lambda i: ((i * tm) // S, 0))
    small = lambda a: pl.BlockSpec(a.shape, lambda i: (0, 0))
    return _run("mixer_tail_fwd", body, (T // tm,),
                [x2d, attn_out, proj, proj, proj, proj, proj, qkv, km, vm, conv_w8, pk, wout],
                [tile(D, 0), tile(ATT_W, 0), tile(CONV_W, 3), tile(CONV_W, 4), tile(CONV_W, 5), halo(3), halo(5),
                 tile(MEM_W, 3), seq, seq, VM, VM, VM],
                [SDS((T, CONV_W), f32), SDS((T, MEM_W), f32), SDS((T, D), MXU), SDS((T, D), f32), SDS((T, D), MXU)],
                [tile(CONV_W, 0), tile(MEM_W, 0), tile(D, 0), tile(D, 0), tile(D, 0)], vmem_mib=40, exchange=exchange)


def ffn_fwd_bwd(h, x1, tgt, wgT, wuT, wd, pk, tm):
    T, D = x1.shape
    F = wd.shape[0]

    def body(h_ref, x1_ref, t_ref, wg_ref, wu_ref, wd_ref, pk_ref,
             dx1_ref, dx2_ref, act_ref, dg_ref, du_ref, loss_ref, dgf_ref):
        @pl.when(pl.program_id(0) == 0)
        def _():
            loss_ref[...] = jnp.zeros_like(loss_ref)
            dgf_ref[...] = jnp.zeros_like(dgf_ref)

        hv = h_ref[...]
        gate = _nt(hv, wg_ref[...])
        up = _nt(hv, wu_ref[...])
        sg = jax.nn.sigmoid(gate)
        sl = gate * sg
        act = _c(sl * up)
        act_ref[...] = act
        x1v = x1_ref[...]
        diff = (x1v + _nn(act, wd_ref[...])) - t_ref[...]
        loss_ref[...] += 0.5 * jnp.sum(jnp.sum(diff * diff, axis=-1, keepdims=True) / D, axis=0, keepdims=True)
        dx2 = diff / D
        dx2b = _c(dx2)
        dx2_ref[...] = dx2b
        d_act = _nt(dx2b, wd_ref[...])
        d_up = _c(d_act * sl)
        d_gate = _c(d_act * up * (sg * (1.0 + gate * (1.0 - sg))))
        du_ref[...] = d_up
        dg_ref[...] = d_gate
        dh = _nn(d_gate, wg_ref[...]) + _nn(d_up, wu_ref[...])
        dv, dgf = _norm_bwd(dh, x1v, _rstd(x1v), _small(pk_ref, "norm_ffn"))
        dx1_ref[...] = dx2 + dv
        dgf_ref[...] += dgf

    tile = lambda w: pl.BlockSpec((tm, w), lambda i: (i, 0))
    return _run("ffn_fwd_bwd", body, (T // tm,), [h, x1, tgt, wgT, wuT, wd, pk],
                [tile(D), tile(D), tile(D), VM, VM, VM, VM],
                [SDS((T, D), f32), SDS((T, D), MXU), SDS((T, F), MXU), SDS((T, F), MXU), SDS((T, F), MXU),
                 SDS((8, 128), f32), SDS((1, D), f32)],
                [tile(D), tile(D), tile(F), tile(F), tile(F), pl.BlockSpec((8, 128), lambda i: (0, 0)),
                 pl.BlockSpec((1, D), lambda i: (0, 0))], vmem_mib=56)


def matmul_tn(a, b, name, tmo, tk):
    T, M = a.shape
    N = b.shape[1]

    def body(a_ref, b_ref, o_ref):
        @pl.when(pl.program_id(1) == 0)
        def _():
            o_ref[...] = jnp.zeros_like(o_ref)

        o_ref[...] += _tn(a_ref[...], b_ref[...])

    return _run(name, body, (M // tmo, T // tk), [a, b],
                [pl.BlockSpec((tk, tmo), lambda m, k: (k, m)), pl.BlockSpec((tk, N), lambda m, k: (k, 0))],
                [SDS((M, N), f32)], [pl.BlockSpec((tmo, N), lambda m, k: (m, 0))], vmem_mib=48)[0]


def out_proj_bwd(dx1, merged, attn_out, conv_out, mem_out, pk, wout, tm):
    T, D = dx1.shape

    def body(dx1_ref, mg_ref, ao_ref, co_ref, mo_ref, pk_ref, w_ref,
             dao_ref, dco_ref, dmo_ref, dw_ref, dgain_ref):
        @pl.when(pl.program_id(0) == 0)
        def _():
            dw_ref[...] = jnp.zeros_like(dw_ref)
            dgain_ref[...] = jnp.zeros_like(dgain_ref)

        dxb = _c(dx1_ref[...])
        dw_ref[...] += _tn(mg_ref[...], dxb)
        dmg = _nt(dxb, w_ref[...])
        ao, co, mo = ao_ref[...], co_ref[...], mo_ref[...]
        da, ga = _norm_bwd(dmg[:, :ATT_W], ao, _rstd(ao), _small(pk_ref, "out_norm_attn"))
        dc, gc = _norm_bwd(dmg[:, ATT_W:ATT_W + CONV_W], co, _rstd(co), _small(pk_ref, "out_norm_conv"))
        dm, gm = _norm_bwd(dmg[:, ATT_W + CONV_W:], mo, _rstd(mo), _small(pk_ref, "out_norm_mem"))
        dao_ref[...] = da
        dco_ref[...] = dc
        dmo_ref[...] = dm
        dgain_ref[...] += jnp.concatenate([ga, gc, gm], axis=1)

    tile = lambda w: pl.BlockSpec((tm, w), lambda i: (i, 0))
    return _run("out_proj_bwd", body, (T // tm,), [dx1, merged, attn_out, conv_out, mem_out, pk, wout],
                [tile(D), tile(D), tile(ATT_W), tile(CONV_W), tile(MEM_W), VM, VM],
                [SDS((T, ATT_W), f32), SDS((T, CONV_W), f32), SDS((T, MEM_W), f32), SDS((D, D), f32), SDS((1, D), f32)],
                [tile(ATT_W), tile(CONV_W), tile(MEM_W), pl.BlockSpec((D, D), lambda i: (0, 0)),
                 pl.BlockSpec((1, D), lambda i: (0, 0))], vmem_mib=40)


def attn_bwd(qkv, d_attn, attn_out, sink_rows, BL, S, exchange):
    NB = S // BLK
    T = BL * S

    def body(q_ref, kc_ref, kp_ref, vc_ref, vp_ref, do_ref, ao_ref, sk_ref, tab_ref,
             dq_ref, dk_ref, dv_ref, dsk_ref, pend_k, pend_v):
        b, j = pl.program_id(0), pl.program_id(1)

        @pl.when((b == 0) & (j == 0))
        def _():
            dsk_ref[...] = jnp.zeros_like(dsk_ref)

        @pl.when(j == 0)
        def _():
            pend_k[...] = jnp.zeros_like(pend_k)
            pend_v[...] = jnp.zeros_like(pend_v)

        @pl.when(j < NB)
        def _():
            q, do, ao = q_ref[...], do_ref[...], ao_ref[...]
            k2 = jnp.concatenate([kp_ref[...], kc_ref[...]], axis=0)
            v2 = jnp.concatenate([vp_ref[...], vc_ref[...]], axis=0)
            lane = lax.broadcasted_iota(jnp.int32, (8, 128), 1)
            ones_w = jnp.ones((2 * BLK, 2 * BLK), MXU)
            dsk = jnp.zeros((8, 128), f32)
            dks, dvs = [], []
            for g in range(N_KV):
                kn, vh = k2[:, g * HD:(g + 1) * HD], v2[:, g * HD:(g + 1) * HD]
                qs = _stack_heads(q, g)
                s = _nt(qs, kn) * (HD ** -0.5) + tab_ref[g]
                e, es = _exp_scores(s, _sink_column(g, sk_ref))
                eb = _c(e)
                inv_w = 1.0 / (_nn(eb, ones_w) + es)
                inv_n = inv_w[:, :HD]
                dos = _stack_heads(do, g)
                delta = _rowsum_mxu(dos * _stack_heads(ao, g), 2 * BLK)
                dp = _nt(_c(dos), vh)
                ds = _c(e * inv_w * (dp - delta) * (HD ** -0.5))
                t = es * inv_n[:, 0:1] * delta[:, 0:1]
                for hh in range(GQA):
                    dsk = dsk + jnp.where(lane == g * GQA + hh, -jnp.sum(t[hh * BLK:(hh + 1) * BLK]), 0.0)
                dvs.append(_tn(eb, _c(dos * inv_n)))
                dks.append(_tn(ds, qs))
                dqs = _nn(ds, kn)
                for hh in range(GQA):
                    dq_ref[:, pl.ds((g * GQA + hh) * HD, HD)] = dqs[hh * BLK:(hh + 1) * BLK]
            dk2 = jnp.concatenate(dks, axis=1)
            dv2 = jnp.concatenate(dvs, axis=1)
            dk_ref[...] = pend_k[...] + dk2[:BLK]
            dv_ref[...] = pend_v[...] + dv2[:BLK]
            pend_k[...] = dk2[BLK:]
            pend_v[...] = dv2[BLK:]
            dsk_ref[...] += dsk

        @pl.when(j == NB)
        def _():
            dk_ref[...] = pend_k[...]
            dv_ref[...] = pend_v[...]

    cur = lambda col: (lambda b, j: (b * NB + jnp.minimum(j, NB - 1), col))
    prev = lambda col: (lambda b, j: (b * NB + jnp.maximum(j - 1, 0), col))
    small = lambda shape: pl.BlockSpec(shape, lambda b, j: (0, 0))
    return _run("attn_bwd", body, (BL, NB + 1), [qkv, qkv, qkv, qkv, qkv, d_attn, attn_out, sink_rows, _swa_bias_table()],
                [pl.BlockSpec((BLK, ATT_W), cur(0)),
                 pl.BlockSpec((BLK, KV_W), cur(4)), pl.BlockSpec((BLK, KV_W), prev(4)),
                 pl.BlockSpec((BLK, KV_W), cur(5)), pl.BlockSpec((BLK, KV_W), prev(5)),
                 pl.BlockSpec((BLK, ATT_W), cur(0)), pl.BlockSpec((BLK, ATT_W), cur(0)), small((8, 128)),
                 pl.BlockSpec((None, N_KV, GQA * BLK, 2 * BLK), lambda b, j: (jnp.minimum(j, 1), 0, 0, 0))],
                [SDS((T, ATT_W), f32), SDS((T, KV_W), f32), SDS((T, KV_W), f32), SDS((8, 128), f32)],
                [pl.BlockSpec((BLK, ATT_W), cur(0)), pl.BlockSpec((BLK, KV_W), prev(0)),
                 pl.BlockSpec((BLK, KV_W), prev(0)), small((8, 128))],
                scratch=[pltpu.VMEM((BLK, KV_W), f32)] * 2, exchange=exchange)


def mem_conv_bwd(d_mem_out, mem_out, d_conv_out, proj, qkv, km, vm, conv_w8, pk, S, tm, exchange):
    T = d_mem_out.shape[0]
    NM = km.shape[0] // (T // S)

    def body(dmo_ref, mo_ref, dco_ref, ch_ref, cb_ref, cc_ref, chh_ref, cch_ref, qm_ref, km_ref, vm_ref, cw_ref,
             pk_ref, dqm_ref, dkm_ref, dvm_ref, dcb_ref, dcv_ref, dcw_ref, dcbias_ref):
        i = pl.program_id(0)
        first = (i * tm) % S == 0

        @pl.when(i == 0)
        def _():
            dcw_ref[...] = jnp.zeros_like(dcw_ref)
            dcbias_ref[...] = jnp.zeros_like(dcbias_ref)

        @pl.when(first)
        def _():
            dkm_ref[...] = jnp.zeros_like(dkm_ref)
            dvm_ref[...] = jnp.zeros_like(dvm_ref)

        qm, kmv, vmv, dmo, mo = qm_ref[...], km_ref[...], vm_ref[...], dmo_ref[...], mo_ref[...]
        ones_w = jnp.ones((NM, NM), MXU)
        for h in range(N_MEMH):
            qh, kh, vh, e = _mem_head(qm, kmv, vmv, h)
            eb = _c(e)
            doh = dmo[:, h * HD:(h + 1) * HD]
            delta = _rowsum_mxu(doh * mo[:, h * HD:(h + 1) * HD], NM)
            dp = _nt(_c(doh), vh)
            inv_w = 1.0 / _nn(eb, ones_w)
            ds = _c(e * inv_w * (dp - delta) * (HD ** -0.5))
            dvm_ref[:, pl.ds(h * HD, HD)] += _tn(eb, _c(doh * inv_w[:, :HD]))
            dkm_ref[:, pl.ds(h * HD, HD)] += _tn(ds, qh)
            dqm_ref[:, pl.ds(h * HD, HD)] = _nn(ds, kh)

        u = cc_ref[...] * ch_ref[...]
        uh = jnp.where(first, 0.0, cch_ref[...] * chh_ref[...])
        u1, u2 = _conv_taps(u, uh)
        conv = cw_ref[0:1, :] * u2 + cw_ref[1:2, :] * u1 + cw_ref[2:3, :] * u + _small(pk_ref, "conv_b")
        dy = dco_ref[...]
        dcb_ref[...] = dy * conv
        dcv = dy * cb_ref[...]
        dcv_ref[...] = dcv
        dcbias_ref[...] += jnp.sum(dcv, axis=0, keepdims=True)
        dcw_ref[0:1, :] += jnp.sum(dcv * u2, axis=0, keepdims=True)
        dcw_ref[1:2, :] += jnp.sum(dcv * u1, axis=0, keepdims=True)
        dcw_ref[2:3, :] += jnp.sum(dcv * u, axis=0, keepdims=True)

    tile = lambda w, col: pl.BlockSpec((tm, w), lambda i: (i, col))
    halo = lambda col: pl.BlockSpec((8, CONV_W), lambda i: (jnp.maximum(i * (tm // 8) - 1, 0), col))
    seq = pl.BlockSpec((NM, MEM_W), lambda i: ((i * tm) // S, 0))
    const = lambda shape: pl.BlockSpec(shape, lambda i: (0, 0))
    return _run("mem_conv_bwd", body, (T // tm,),
                [d_mem_out, mem_out, d_conv_out, proj, proj, proj, proj, proj, qkv, km, vm, conv_w8, pk],
                [tile(MEM_W, 0), tile(MEM_W, 0), tile(CONV_W, 0), tile(CONV_W, 3), tile(CONV_W, 4), tile(CONV_W, 5),
                 halo(3), halo(5), tile(MEM_W, 3), seq, seq, VM, VM],
                [SDS((T, MEM_W), f32), SDS(km.shape, f32), SDS(km.shape, f32),
                 SDS((T, CONV_W), f32), SDS((T, CONV_W), f32), SDS((8, CONV_W), f32), SDS((1, CONV_W), f32)],
                [tile(MEM_W, 0), seq, seq, tile(CONV_W, 0), tile(CONV_W, 0), const((8, CONV_W)), const((1, CONV_W))],
                vmem_mib=48, exchange=exchange)


def in_proj_bwd(dqn, dkn, dv, dcb, dcv, dqmn, proj, conv_w8, xn, x2d, dx1, pk, winT, S, tm, stages, ws, ms, vs):
    T, D = x2d.shape
    P = winT.shape[0]
    last_blk = T // 8 - 1
    n = len(stages)
    nsteps = T // tm
    tile_w = ws[0].shape[1] // (nsteps // 2)
    turn = [e * 2 // n for e in range(n)]

    def body(dq_ref, dk_ref, dv_ref, dcb_ref, dcv_ref, dcvn_ref, dqm_ref, qa_ref, ka_ref, ch_ref, cc_ref, qma_ref,
             cw_ref, xn_ref, x_ref, dx1_ref, pk_ref, w_ref, *rest):
        st, aw, am, av = (rest[k * n:(k + 1) * n] for k in range(4))
        dx_ref, dw_ref, dg_ref, dqg_ref, dkg_ref, dmqg_ref = rest[4 * n:4 * n + 6]
        aouts = rest[4 * n + 6:]
        i = pl.program_id(0)

        for parity in range(2):
            @pl.when(i % 2 == parity)
            def _(parity=parity):
                for e in range(n):
                    if turn[e] == parity:
                        g = jnp.concatenate([_sum_chips(st[e].at[0]), _sum_chips(st[e].at[1])], axis=0)
                        d, mm, vv = _adamw_math(aw[e][...], g, am[e][...], av[e][...])
                        for k, val in enumerate((g, d, mm, vv)):
                            aouts[4 * e + k][...] = val

        @pl.when(i == 0)
        def _():
            dw_ref[...] = jnp.zeros_like(dw_ref)
            dg_ref[...] = jnp.zeros_like(dg_ref)
            dqg_ref[...] = jnp.zeros_like(dqg_ref)
            dkg_ref[...] = jnp.zeros_like(dkg_ref)
            dmqg_ref[...] = jnp.zeros_like(dmqg_ref)

        dqa, gq = _heads_norm_bwd(dq_ref[...], qa_ref[...], _small(pk_ref, "q_norm"))
        dka, gk = _heads_norm_bwd(dk_ref[...], ka_ref[...], _small(pk_ref, "k_norm"))
        dqma, gmq = _heads_norm_bwd(dqm_ref[...], qma_ref[...], _small(pk_ref, "mem_q_norm"))
        dqg_ref[...] += gq
        dkg_ref[...] += gk
        dmqg_ref[...] += gmq

        last = ((i + 1) * tm) % S == 0
        dcv = dcv_ref[...]
        nxt = jnp.where(last, 0.0, dcvn_ref[...])
        row = lax.broadcasted_iota(jnp.int32, dcv.shape, 0)
        n1 = jnp.where(row == tm - 1, nxt[0:1, :], pltpu.roll(dcv, tm - 1, 0))
        n2 = jnp.where(row == tm - 2, nxt[0:1, :], jnp.where(row == tm - 1, nxt[1:2, :], pltpu.roll(dcv, tm - 2, 0)))
        du = cw_ref[2:3, :] * dcv + cw_ref[1:2, :] * n1 + cw_ref[0:1, :] * n2
        d_proj = jnp.concatenate([_c(dqa), _c(dka), _c(dv_ref[...]), _c(du * cc_ref[...]),
                                  _c(dcb_ref[...]), _c(du * ch_ref[...]), _c(dqma)], axis=1)
        dw_ref[...] += _tn(d_proj, xn_ref[...])
        xv = x_ref[...]
        dv_, dg = _norm_bwd(_nn(d_proj, w_ref[...]), xv, _rstd(xv), _small(pk_ref, "norm_mix"))
        dx_ref[...] = dx1_ref[...] + dv_
        dg_ref[...] += dg

    tile = lambda w, col=0: pl.BlockSpec((tm, w), lambda i: (i, col))
    nhalo = pl.BlockSpec((8, CONV_W), lambda i: (jnp.minimum((i + 1) * (tm // 8), last_blk), 0))
    const = lambda shape: pl.BlockSpec(shape, lambda i: (0, 0))
    st_specs = [pl.BlockSpec((2, 4, s.shape[2], tile_w), lambda i: (0, 0, 0, i // 2)) for s in stages]
    w_specs = [pl.BlockSpec((w.shape[0], tile_w), lambda i: (0, i // 2)) for w in ws]
    res = _run("in_proj_bwd", body, (nsteps,),
               [dqn, dkn, dv, dcb, dcv, dcv, dqmn, proj, proj, proj, proj, proj, conv_w8, xn, x2d, dx1, pk, winT]
               + list(stages) + list(ws) + list(ms) + list(vs),
               [tile(ATT_W), tile(KV_W), tile(KV_W), tile(CONV_W), tile(CONV_W), nhalo, tile(MEM_W),
                tile(ATT_W, 0), tile(KV_W, 4), tile(CONV_W, 3), tile(CONV_W, 5), tile(MEM_W, 6), VM,
                tile(D), tile(D), tile(D), VM, VM] + st_specs + w_specs * 3,
               [SDS((T, D), f32), SDS((P, D), f32), SDS((1, D), f32), SDS((1, HD), f32), SDS((1, HD), f32),
                SDS((1, HD), f32)] + [SDS(w.shape, f32) for w in ws for _ in range(4)],
               [tile(D), pl.BlockSpec((P, D), lambda i: (0, 0)), const((1, D)), const((1, HD)), const((1, HD)),
                const((1, HD))] + [s for s in w_specs for _ in range(4)],
               vmem_mib=56)
    return res[:6], [res[6 + 4 * e:10 + 4 * e] for e in range(n)]


def mem_kv_bwd(dkm, dvm, kv, memn, mem2d, pk, wmkv):
    def body(dkm_ref, dvm_ref, kv_ref, mn_ref, m_ref, pk_ref, w_ref, dw_ref, dg_ref, dkg_ref):
        dkk, dkg = _heads_norm_bwd(dkm_ref[...], kv_ref[:, :MEM_W], _small(pk_ref, "mem_k_norm"))
        dkg_ref[...] = dkg
        dkv = _c(jnp.concatenate([dkk, dvm_ref[...]], axis=1))
        dw_ref[...] = _tn(mn_ref[...], dkv)
        mv = m_ref[...]
        dg_ref[...] = jnp.sum(_nt(dkv, w_ref[...]) * mv * _rstd(mv), axis=0, keepdims=True)

    return _run("mem_kv_bwd", body, (), [dkm, dvm, kv, memn, mem2d, pk, wmkv], [VM] * 7,
                [SDS(wmkv.shape, f32), SDS((1, mem2d.shape[1]), f32), SDS((1, HD), f32)], [VM] * 3, vmem_mib=40)


def _halves_view(g):
    return g.reshape(4, 2, g.shape[0] // 8, g.shape[1])


def kernel(x, mem, norm_mix, w_in, q_norm, k_norm, attn_sinks, conv_w, conv_b, norm_mem, w_mem_kv, mem_q_norm, mem_k_norm, out_norm_attn, out_norm_conv, out_norm_mem, w_out, norm_ffn, w_gate, w_up, w_down, loss_target, m_norm_mix, m_w_in, m_q_norm, m_k_norm, m_attn_sinks, m_conv_w, m_conv_b, m_norm_mem, m_w_mem_kv, m_mem_q_norm, m_mem_k_norm, m_out_norm_attn, m_out_norm_conv, m_out_norm_mem, m_w_out, m_norm_ffn, m_w_gate, m_w_up, m_w_down, v_norm_mix, v_w_in, v_q_norm, v_k_norm, v_attn_sinks, v_conv_w, v_conv_b, v_norm_mem, v_w_mem_kv, v_mem_q_norm, v_mem_k_norm, v_out_norm_attn, v_out_norm_conv, v_out_norm_mem, v_w_out, v_norm_ffn, v_w_gate, v_w_up, v_w_down):
    BL, S, D = x.shape
    T = BL * S
    TM = 256
    TM_BIG = min(512, S)
    _, _, ci = _place()
    cidx = ci.reshape(1).astype(jnp.int32)
    w_small = dict(norm_mix=norm_mix, norm_mem=norm_mem, norm_ffn=norm_ffn, out_norm_attn=out_norm_attn,
                   out_norm_conv=out_norm_conv, out_norm_mem=out_norm_mem, conv_w=conv_w, conv_b=conv_b, q_norm=q_norm,
                   k_norm=k_norm, mem_q_norm=mem_q_norm, mem_k_norm=mem_k_norm, attn_sinks=attn_sinks)
    m_small = dict(norm_mix=m_norm_mix, norm_mem=m_norm_mem, norm_ffn=m_norm_ffn, out_norm_attn=m_out_norm_attn,
                   out_norm_conv=m_out_norm_conv, out_norm_mem=m_out_norm_mem, conv_w=m_conv_w, conv_b=m_conv_b,
                   q_norm=m_q_norm, k_norm=m_k_norm, mem_q_norm=m_mem_q_norm, mem_k_norm=m_mem_k_norm,
                   attn_sinks=m_attn_sinks)
    v_small = dict(norm_mix=v_norm_mix, norm_mem=v_norm_mem, norm_ffn=v_norm_ffn, out_norm_attn=v_out_norm_attn,
                   out_norm_conv=v_out_norm_conv, out_norm_mem=v_out_norm_mem, conv_w=v_conv_w, conv_b=v_conv_b,
                   q_norm=v_q_norm, k_norm=v_k_norm, mem_q_norm=v_mem_q_norm, mem_k_norm=v_mem_k_norm,
                   attn_sinks=v_attn_sinks)
    pk = _pack_small(w_small)

    rowblocks = lambda a, b, c, d, e, f: [a[0].T, b[0].T, c[0].T, d[0], e[0], f[0]]
    w_rb = rowblocks(w_in, w_gate, w_up, w_down, w_out, w_mem_kv)
    m_rb = rowblocks(m_w_in, m_w_gate, m_w_up, m_w_down, m_w_out, m_w_mem_kv)
    v_rb = rowblocks(v_w_in, v_w_gate, v_w_up, v_w_down, v_w_out, v_w_mem_kv)
    (winT_s,) = prep_weights("prep_w_in", w_rb[:1])
    cw_pad = jnp.zeros((8, 128), f32).at[:3, :HD].set(conv_w[0])
    (wgT_s, wuT_s, wd_s, wout_s, wmkv_s), (winT, cw_all) = prep_weights(
        "gather_w_in", w_rb[1:], _together([gather_two_legs([winT_s]), gather_exchange([cw_pad], [False])]))
    conv_w_full = jnp.transpose(cw_all.reshape(4, 8, 128)[:, :3, :HD], (1, 0, 2)).reshape(3, CONV_W)
    conv_w8 = jnp.zeros((8, CONV_W), f32).at[:3].set(conv_w_full)
    sink_rows = jnp.broadcast_to(attn_sinks.reshape(N_Q, 1), (N_Q, 128))

    x2d = x.reshape(T, D)
    mem2d = mem.reshape(-1, D)
    (xn, proj, qkv), near1 = in_proj_fwd(x2d, pk, winT, TM_BIG, gather_near_exchange([wgT_s, wout_s, wmkv_s], relay_early=1))
    (attn_out,), (wgT, wout, wmkv, *near2) = attn_fwd(
        qkv, sink_rows, BL, S, _together([gather_far_exchange(near1, relay_early=2), gather_near_exchange([wuT_s, wd_s])]))
    memn, kv, km, vm = mem_kv_fwd(mem2d, pk, wmkv)
    (conv_out, mem_out, merged, x1, h), (wuT, wd) = mixer_tail_fwd(
        x2d, attn_out, proj, qkv, km, vm, conv_w8, pk, wout, S, TM_BIG, gather_far_exchange(near2, relay_early=2))

    dx1, dx2b, act, d_gate, d_up, loss8, d_norm_ffn = ffn_fwd_bwd(h, x1, loss_target.reshape(T, D), wgT, wuT, wd, pk, TM)
    F = wd.shape[0]
    g_wd = matmul_tn(act, dx2b, "dw_down", F // 2, min(T, 1024))
    g_wgT = matmul_tn(d_gate, h, "dw_gate", F // 2, min(T, 1024))
    g_wuT = matmul_tn(d_up, h, "dw_up", F // 2, min(T, 1024))

    d_attn, d_conv_out, d_mem_out, g_wout, d_gains = out_proj_bwd(dx1, merged, attn_out, conv_out, mem_out, pk, wout, TM_BIG)
    late = [_halves_view(g) for g in (g_wgT, g_wuT, g_wd, g_wout)]
    (dqmn, dkm, dvm, dcb, dcv, d_cw8, d_cbias), late_sib = mem_conv_bwd(
        d_mem_out, mem_out, d_conv_out, proj, qkv, km, vm, conv_w8, pk, S, min(1024, S), halves_exchange(late))
    late_part = add_halves(cidx, late, late_sib, "grad_add_halves_ffn")
    (dqn, dkn, dv, d_sink8), late_stage = attn_bwd(qkv, d_attn, attn_out, sink_rows, BL, S,
                                                   scatter_exchange(late_part, relay_before_end=[21, 12, 3, 0]))
    (g_x, g_winT, d_norm_mix, d_qg, d_kg, d_mqg), late_res = in_proj_bwd(
        dqn, dkn, dv, dcb, dcv, dqmn, proj, conv_w8, xn, x2d, dx1, pk, winT, S, TM,
        late_stage, w_rb[1:5], m_rb[1:5], v_rb[1:5])
    g_wmkv, d_norm_mem, d_mkg = mem_kv_bwd(dkm, dvm, kv, memn, mem2d, pk, wmkv)

    tot, tail_stage = tail_reduce(d_norm_mix, d_norm_mem, d_norm_ffn, d_gains, d_cw8, d_cbias, d_qg, d_kg, d_mqg, d_mkg,
                                  d_sink8, loss8, [_halves_view(g) for g in (g_winT, g_wmkv)])
    loss = tot[5, 384]
    tail_res, _ = adamw_big("adamw_tail", tail_stage, [w_rb[0], w_rb[5]], [m_rb[0], m_rb[5]], [v_rb[0], v_rb[5]], 4)
    res = {"w_in": [a.T[None] for a in tail_res[0]], "w_gate": [a.T[None] for a in late_res[0]],
           "w_up": [a.T[None] for a in late_res[1]], "w_down": [a[None] for a in late_res[2]],
           "w_out": [a[None] for a in late_res[3]], "w_mem_kv": [a[None] for a in tail_res[1]]}
    res.update(adamw_small(tot, pk, _pack_small(m_small), _pack_small(v_small), {k: w_small[k].shape for k in SMALL}))

    order = ["norm_mix", "w_in", "q_norm", "k_norm", "attn_sinks", "conv_w", "conv_b", "norm_mem", "w_mem_kv",
             "mem_q_norm", "mem_k_norm", "out_norm_attn", "out_norm_conv", "out_norm_mem", "w_out", "norm_ffn",
             "w_gate", "w_up", "w_down"]
    return (loss, g_x.reshape(BL, S, D), *[res[n][0] for n in order], *[res[n][1] for n in order],
            *[res[n][2] for n in order], *[res[n][3] for n in order])
```

```python
import collections
import functools

import jax
import jax.numpy as jnp
import numpy as np
from jax import lax
from jax.experimental import pallas as pl
from jax.experimental.pallas import tpu as pltpu

f32 = jnp.float32
MXU = jnp.bfloat16
WIRE = jnp.bfloat16
EPS = 1e-6
NEG = -1e30
HD = 64
BLK = 128
N_Q, N_KV, N_MEMH = 8, 2, 4
GQA = N_Q // N_KV
ATT_W, KV_W, CONV_W, MEM_W = 512, 128, 256, 256
VMEM_MIB = 1024 * 1024
ADAM_LR, ADAM_B1, ADAM_B2, ADAM_EPS, ADAM_WD, ADAM_STEP = 0.001, 0.9, 0.999, 1e-08, 0.01, 10

MESH = pl.DeviceIdType.MESH
VM = pl.BlockSpec(memory_space=pltpu.VMEM)
ANY = pl.BlockSpec(memory_space=pl.ANY)
SDS = jax.ShapeDtypeStruct
DMA = pltpu.SemaphoreType.DMA


def _c(v):
    return v.astype(MXU)


def _nn(a, b):
    return lax.dot_general(a, b, (((1,), (0,)), ((), ())), preferred_element_type=f32)


def _nt(a, b):
    return lax.dot_general(a, b, (((1,), (1,)), ((), ())), preferred_element_type=f32)


def _tn(a, b):
    return lax.dot_general(a, b, (((0,), (0,)), ((), ())), preferred_element_type=f32)


def _rstd(v):
    return lax.rsqrt(jnp.mean(v * v, axis=-1, keepdims=True) + EPS)


def _norm_bwd(dy, v, r, g):
    dyg = dy * g
    dv = r * dyg - v * (r * r * r) * jnp.mean(dyg * v, axis=-1, keepdims=True)
    return dv, jnp.sum(dy * v * r, axis=0, keepdims=True)


def _split3(v):
    hi = _c(v)
    r1 = v - hi.astype(f32)
    mid = _c(r1)
    return hi, mid, _c(r1 - mid.astype(f32))


def _rowsum_mxu(v, width):
    ones = jnp.ones((v.shape[1], width), MXU)
    return sum(_nn(a, ones) for a in _split3(v))


def _seg_sums(v):
    r = lax.broadcasted_iota(jnp.int32, (2 * HD, 2 * HD), 0) // HD
    c = lax.broadcasted_iota(jnp.int32, (2 * HD, 2 * HD), 1) // HD
    bd = (r == c).astype(MXU)
    outs = []
    for b in range(v.shape[1] // (2 * HD)):
        outs.append(sum(_nn(a, bd) for a in _split3(v[:, b * 2 * HD:(b + 1) * 2 * HD])))
    return outs[0] if len(outs) == 1 else jnp.concatenate(outs, axis=1)


def _lanes(g, width):
    return jnp.concatenate([g] * (width // HD), axis=1)


def _heads_rstd(v):
    return lax.rsqrt(_seg_sums(v * v) * (1.0 / HD) + EPS)


def _heads_norm_bwd(dy, v, g):
    r = _heads_rstd(v)
    gl = _lanes(g, v.shape[1])
    dyg = dy * gl
    dv = r * dyg - v * (r * r * r) * (_seg_sums(dyg * v) * (1.0 / HD))
    dgl = jnp.sum(dy * v * r, axis=0, keepdims=True)
    return dv, sum(dgl[:, s * HD:(s + 1) * HD] for s in range(v.shape[1] // HD))


def _exp_scores(s, extra=None):
    m = jnp.max(s, axis=-1, keepdims=True)
    if extra is None:
        return jnp.exp(s - m), None
    m = jnp.maximum(m, extra)
    return jnp.exp(s - m), jnp.exp(extra - m)


def _place():
    return lax.axis_index("x"), lax.axis_index("y"), lax.axis_index("c")


SMALL_AT = {"norm_mix": (0, 0, 1024), "norm_mem": (1, 0, 1024), "norm_ffn": (2, 0, 1024),
            "out_norm_attn": (3, 0, ATT_W), "out_norm_conv": (3, ATT_W, CONV_W), "out_norm_mem": (3, ATT_W + CONV_W, MEM_W),
            "conv_b": (4, 3 * CONV_W, CONV_W), "q_norm": (5, 0, HD), "k_norm": (5, HD, HD), "mem_q_norm": (5, 2 * HD, HD),
            "mem_k_norm": (5, 3 * HD, HD), "attn_sinks": (5, 256, N_Q)}
SMALL = ("norm_mix", "norm_mem", "norm_ffn", "out_norm_attn", "out_norm_conv", "out_norm_mem", "conv_w", "conv_b",
         "q_norm", "k_norm", "mem_q_norm", "mem_k_norm", "attn_sinks")


def _small(pk_ref, name):
    r, c0, w = SMALL_AT[name]
    return pk_ref[r:r + 1, c0:c0 + w]


def _pack_small(d):
    z = lambda n: jnp.zeros((1, n), f32)
    row3 = jnp.concatenate([d["out_norm_attn"], d["out_norm_conv"], d["out_norm_mem"]], axis=1)
    row4 = jnp.concatenate([d["conv_w"].reshape(1, 3 * HD), z(3 * CONV_W - 3 * HD), d["conv_b"]], axis=1)
    row5 = jnp.concatenate([d["q_norm"], d["k_norm"], d["mem_q_norm"], d["mem_k_norm"], d["attn_sinks"],
                            z(1024 - 4 * HD - N_Q)], axis=1)
    return jnp.concatenate([d["norm_mix"], d["norm_mem"], d["norm_ffn"], row3, row4, row5, z(1024), z(1024)], axis=0)


def _other_chips(x, y):
    return [(1 - x, y), (x, 1 - y), (1 - x, 1 - y)]


Exchange = collections.namedtuple("Exchange", "ins outs sems start finish relays aliases", defaults=((), {}))


def _together(exchanges):
    def bounds(key):
        at, out = 0, []
        for ex in exchanges:
            out.append((at, at + len(getattr(ex, key))))
            at += len(getattr(ex, key))
        return out

    bi, bo, bs = bounds("ins"), bounds("outs"), bounds("sems")

    def of(i, fn):
        return lambda xa, xo, xs: fn(xa[bi[i][0]:bi[i][1]], xo[bo[i][0]:bo[i][1]], xs[bs[i][0]:bs[i][1]])

    def every(name):
        fns = [of(i, getattr(ex, name)) for i, ex in enumerate(exchanges)]

        def run(xa, xo, xs):
            for fn in fns:
                fn(xa, xo, xs)
        return run

    aliases = {}
    for i, ex in enumerate(exchanges):
        aliases.update({bi[i][0] + a: bo[i][0] + o for a, o in ex.aliases.items()})
    return Exchange([a for ex in exchanges for a in ex.ins], [o for ex in exchanges for o in ex.outs],
                    [s for ex in exchanges for s in ex.sems], every("start"), every("finish"),
                    [(sbe, of(i, fn)) for i, ex in enumerate(exchanges) for sbe, fn in ex.relays], aliases)


def _run(name, body, grid, ins, in_specs, out_shape, out_specs, scratch=(), vmem_mib=32, exchange=None):
    ins, in_specs, out_shape, out_specs, scratch = list(ins), list(in_specs), list(out_shape), list(out_specs), list(scratch)
    ni, no, ns = len(ins), len(out_shape), len(scratch)
    ex = exchange
    if ex is not None:
        nxi, nxo = len(ex.ins), len(ex.outs)

    def call_body(*refs):
        if ex is None:
            body(*refs)
            return
        a, xa = refs[:ni], refs[ni:ni + nxi]
        o, xo = refs[ni + nxi:ni + nxi + no], refs[ni + nxi + no:ni + nxi + no + nxo]
        s, xs = refs[ni + nxi + no + nxo:ni + nxi + no + nxo + ns], refs[ni + nxi + no + nxo + ns:]
        if grid:
            first = functools.reduce(jnp.logical_and, [pl.program_id(d) == 0 for d in range(len(grid))])
            last = functools.reduce(jnp.logical_and, [pl.program_id(d) == grid[d] - 1 for d in range(len(grid))])
            pl.when(first)(lambda: ex.start(xa, xo, xs))
            body(*a, *o, *s)
            nsteps = functools.reduce(lambda p, q: p * q, grid)
            for before_end, fn in ex.relays:
                at = np.unravel_index(max(nsteps - 1 - before_end, 0), grid)
                here = functools.reduce(jnp.logical_and, [pl.program_id(d) == int(at[d]) for d in range(len(grid))])
                pl.when(here)(functools.partial(fn, xa, xo, xs))
            pl.when(last)(lambda: ex.finish(xa, xo, xs))
        else:
            ex.start(xa, xo, xs)
            if body is not None:
                body(*a, *o, *s)
            for _, fn in ex.relays:
                fn(xa, xo, xs)
            ex.finish(xa, xo, xs)

    kw = dict(grid=grid) if grid else {}
    if ex is not None:
        if ex.aliases:
            kw["input_output_aliases"] = {ni + i: no + o for i, o in ex.aliases.items()}
        ins, in_specs = ins + list(ex.ins), in_specs + [ANY] * nxi
        out_shape, out_specs = out_shape + list(ex.outs), out_specs + [ANY] * nxo
        scratch = scratch + list(ex.sems)
    res = pl.pallas_call(
        call_body, name=name, out_shape=out_shape, in_specs=in_specs, out_specs=out_specs, scratch_shapes=scratch,
        compiler_params=pltpu.CompilerParams(dimension_semantics=("arbitrary",) * len(grid) if grid else None,
                                             vmem_limit_bytes=vmem_mib * VMEM_MIB), **kw)(*ins)
    res = list(res)
    return (res[:no], res[no:]) if ex is not None else res


def _remote(src, dst, ssem, rsem, dev):
    return pltpu.make_async_remote_copy(src_ref=src, dst_ref=dst, send_sem=ssem, recv_sem=rsem,
                                        device_id=dev, device_id_type=MESH)


def gather_exchange(shards, split, relay_early=0):
    n = len(shards)

    def rows(ref, e, kk, half=None):
        R = shards[e].shape[0]
        if half is None:
            return ref.at[pl.ds(pl.multiple_of(kk * R, 8), R)]
        return ref.at[pl.ds(pl.multiple_of(kk * R + half * (R // 2), 8), R // 2)]

    def ici(src, dst, sm, e, j, chip_j, x, y, c):
        k = 2 * x + y
        if split[e]:
            s = src[e].at[pl.ds(pl.multiple_of(c * (shards[e].shape[0] // 2), 8), shards[e].shape[0] // 2)]
            return _remote(s, rows(dst[e], e, k, c), sm[0].at[6 * e + j], sm[1].at[6 * e + j], (*chip_j, c))
        return _remote(src[e], rows(dst[e], e, k), sm[0].at[6 * e + j], sm[1].at[6 * e + j], (*chip_j, c))

    def landed(dst, e, chip_j, c):
        kj = 2 * chip_j[0] + chip_j[1]
        return rows(dst[e], e, kj, c) if split[e] else rows(dst[e], e, kj)

    def forward(dst, sm, e, j, chip_j, x, y, c, sender_c):
        kj = 2 * chip_j[0] + chip_j[1]
        r = rows(dst[e], e, kj, sender_c)
        return _remote(r, r, sm[0].at[6 * e + 3 + j], sm[1].at[6 * e + 3 + j], (x, y, 1 - c))

    def local(src, dst, sm, e, x, y):
        return pltpu.make_async_copy(src[e], rows(dst[e], e, 2 * x + y), sm[2].at[e])

    def start(src, dst, sm):
        x, y, c = _place()
        for e in range(n):
            local(src, dst, sm, e, x, y).start()
            for j, chip_j in enumerate(_other_chips(x, y)):
                ici(src, dst, sm, e, j, chip_j, x, y, c).start()

    def relay(src, dst, sm):
        x, y, c = _place()
        for e in range(n):
            for j, chip_j in enumerate(_other_chips(x, y)):
                r = landed(dst, e, chip_j, c)
                _remote(r, r, sm[0].at[6 * e + j], sm[1].at[6 * e + j], (*chip_j, c)).wait_recv()
                if split[e]:
                    forward(dst, sm, e, j, chip_j, x, y, c, c).start()

    def finish(src, dst, sm):
        x, y, c = _place()
        chips = _other_chips(x, y)
        for e in range(n):
            for j, chip_j in enumerate(chips):
                if split[e]:
                    forward(dst, sm, e, j, chip_j, x, y, c, 1 - c).wait_recv()
        for e in range(n):
            for j, chip_j in enumerate(chips):
                ici(src, dst, sm, e, j, chip_j, x, y, c).wait_send()
                if split[e]:
                    forward(dst, sm, e, j, chip_j, x, y, c, c).wait_send()
            local(src, dst, sm, e, x, y).wait()

    outs = [SDS((4 * s.shape[0], s.shape[1]), s.dtype) for s in shards]
    return Exchange(list(shards), outs, [DMA((6 * n,)), DMA((6 * n,)), DMA((n,))], start, finish, [(relay_early, relay)])


def _block_rows(ref, R, kk, half, quarter=None):
    hr = R // 2
    if quarter is None:
        return ref.at[pl.ds(pl.multiple_of(kk * R + half * hr, 8), hr)]
    return ref.at[pl.ds(pl.multiple_of(kk * R + half * hr + quarter * (hr // 2), 8), hr // 2)]


def gather_near_exchange(shards, relay_early=0):
    n = len(shards)
    R = [s.shape[0] for s in shards]

    def ici(src, dst, sm, e, j, chip_j, x, y, c):
        half = src[e].at[pl.ds(pl.multiple_of(c * (R[e] // 2), 8), R[e] // 2)]
        return _remote(half, _block_rows(dst[e], R[e], 2 * x + y, c), sm[0].at[4 * e + j], sm[1].at[4 * e + j], (*chip_j, c))

    def forward(dst, sm, e, j, chip_j, x, y, c, sender_c):
        r = _block_rows(dst[e], R[e], 2 * chip_j[0] + chip_j[1], sender_c)
        return _remote(r, r, sm[0].at[4 * e + 2 + j], sm[1].at[4 * e + 2 + j], (x, y, 1 - c))

    def local(src, dst, sm, e, x, y):
        return pltpu.make_async_copy(src[e], dst[e].at[pl.ds(pl.multiple_of((2 * x + y) * R[e], 8), R[e])], sm[2].at[e])

    def start(src, dst, sm):
        x, y, c = _place()
        for e in range(n):
            local(src, dst, sm, e, x, y).start()
            for j, chip_j in enumerate(_other_chips(x, y)[:2]):
                ici(src, dst, sm, e, j, chip_j, x, y, c).start()

    def relay(src, dst, sm):
        x, y, c = _place()
        for e in range(n):
            for j, chip_j in enumerate(_other_chips(x, y)[:2]):
                r = _block_rows(dst[e], R[e], 2 * chip_j[0] + chip_j[1], c)
                _remote(r, r, sm[0].at[4 * e + j], sm[1].at[4 * e + j], (*chip_j, c)).wait_recv()
                forward(dst, sm, e, j, chip_j, x, y, c, c).start()

    def finish(src, dst, sm):
        x, y, c = _place()
        near = _other_chips(x, y)[:2]
        for e in range(n):
            for j, chip_j in enumerate(near):
                forward(dst, sm, e, j, chip_j, x, y, c, 1 - c).wait_recv()
        for e in range(n):
            for j, chip_j in enumerate(near):
                ici(src, dst, sm, e, j, chip_j, x, y, c).wait_send()
                forward(dst, sm, e, j, chip_j, x, y, c, c).wait_send()
            local(src, dst, sm, e, x, y).wait()

    outs = [SDS((4 * s.shape[0], s.shape[1]), s.dtype) for s in shards]
    return Exchange(list(shards), outs, [DMA((4 * n,)), DMA((4 * n,)), DMA((n,))], start, finish, [(relay_early, relay)])


def gather_far_exchange(bufs, relay_early=0):
    n = len(bufs)
    R = [b.shape[0] // 4 for b in bufs]

    def send(src, dst, sm, e, j, x, y, c):
        to, of = _other_chips(x, y)[j], _other_chips(x, y)[1 - j]
        kk = 2 * of[0] + of[1]
        return _remote(_block_rows(src[e], R[e], kk, c, j), _block_rows(dst[e], R[e], kk, c, j),
                       sm[0].at[4 * e + j], sm[1].at[4 * e + j], (*to, c))

    def landed(dst, e, j, x, y, half):
        return _block_rows(dst[e], R[e], 2 * (1 - x) + (1 - y), half, j)

    def forward(dst, sm, e, j, x, y, c, sender_c):
        r = landed(dst, e, j, x, y, sender_c)
        return _remote(r, r, sm[0].at[4 * e + 2 + j], sm[1].at[4 * e + 2 + j], (x, y, 1 - c))

    def start(src, dst, sm):
        x, y, c = _place()
        for e in range(n):
            for j in range(2):
                send(src, dst, sm, e, j, x, y, c).start()

    def relay(src, dst, sm):
        x, y, c = _place()
        for e in range(n):
            for j in range(2):
                r = landed(dst, e, j, x, y, c)
                _remote(r, r, sm[0].at[4 * e + j], sm[1].at[4 * e + j], (*_other_chips(x, y)[j], c)).wait_recv()
                forward(dst, sm, e, j, x, y, c, c).start()

    def finish(src, dst, sm):
        x, y, c = _place()
        for e in range(n):
            for j in range(2):
                forward(dst, sm, e, j, x, y, c, 1 - c).wait_recv()
        for e in range(n):
            for j in range(2):
                send(src, dst, sm, e, j, x, y, c).wait_send()
                forward(dst, sm, e, j, x, y, c, c).wait_send()

    outs = [SDS(b.shape, b.dtype) for b in bufs]
    return Exchange(list(bufs), outs, [DMA((4 * n,)), DMA((4 * n,))], start, finish, [(relay_early, relay)],
                    {i: i for i in range(n)})


def gather_two_legs(shards):
    near = gather_near_exchange(shards)
    far = gather_far_exchange(near.outs)

    def finish(src, dst, sm):
        near.relays[0][1](src, dst, sm[:3])
        near.finish(src, dst, sm[:3])
        far.start(dst, dst, sm[3:])
        far.relays[0][1](dst, dst, sm[3:])
        far.finish(dst, dst, sm[3:])

    return Exchange(near.ins, near.outs, list(near.sems) + list(far.sems),
                    lambda src, dst, sm: near.start(src, dst, sm[:3]), finish)


def halves_exchange(grads):
    n = len(grads)

    def copy(g, st, sm, e, x, y, c):
        return _remote(g[e].at[:, 1 - c], st[e], sm[0].at[e], sm[1].at[e], (x, y, 1 - c))

    def start(g, st, sm):
        x, y, c = _place()
        for e in range(n):
            copy(g, st, sm, e, x, y, c).start()

    def finish(g, st, sm):
        x, y, c = _place()
        for e in range(n):
            copy(g, st, sm, e, x, y, c).wait()

    outs = [SDS((4,) + a.shape[2:], a.dtype) for a in grads]
    return Exchange(list(grads), outs, [DMA((n,)), DMA((n,))], start, finish)


def scatter_exchange(parts):
    n = len(parts)

    def ici(p, st, sm, e, j, chip_j, x, y, c):
        k, kj = 2 * x + y, 2 * chip_j[0] + chip_j[1]
        return _remote(p[e].at[kj], st[e].at[c, k], sm[0].at[8 * e + j], sm[1].at[8 * e + j], (*chip_j, c))

    def own(p, st, sm, e, x, y, c):
        k = 2 * x + y
        return _remote(p[e].at[k], st[e].at[c, k], sm[0].at[8 * e + 3], sm[1].at[8 * e + 3], (x, y, 1 - c))

    def forward(st, sm, e, j, chip_j, x, y, c, sender_c):
        kj = 2 * chip_j[0] + chip_j[1]
        r = st[e].at[sender_c, kj]
        return _remote(r, r, sm[0].at[8 * e + 4 + j], sm[1].at[8 * e + 4 + j], (x, y, 1 - c))

    def local(p, st, sm, e, x, y, c):
        k = 2 * x + y
        return pltpu.make_async_copy(p[e].at[k], st[e].at[c, k], sm[2].at[e])

    def start(p, st, sm, before_slot=None):
        x, y, c = _place()
        for j, chip_j in enumerate(_other_chips(x, y)):
            if before_slot is not None:
                before_slot(j, 2 * chip_j[0] + chip_j[1])
            for e in range(n):
                ici(p, st, sm, e, j, chip_j, x, y, c).start()
        if before_slot is not None:
            before_slot(3, 2 * x + y)
        for e in range(n):
            local(p, st, sm, e, x, y, c).start()
            own(p, st, sm, e, x, y, c).start()

    def relay(e, p, st, sm):
        x, y, c = _place()
        for j, chip_j in enumerate(_other_chips(x, y)):
            kj = 2 * chip_j[0] + chip_j[1]
            r = st[e].at[c, kj]
            _remote(r, r, sm[0].at[8 * e + j], sm[1].at[8 * e + j], (*chip_j, c)).wait_recv()
            forward(st, sm, e, j, chip_j, x, y, c, c).start()

    def finish(p, st, sm):
        x, y, c = _place()
        k = 2 * x + y
        chips = _other_chips(x, y)
        for e in range(n):
            r = st[e].at[1 - c, k]
            _remote(r, r, sm[0].at[8 * e + 3], sm[1].at[8 * e + 3], (x, y, 1 - c)).wait_recv()
            for j, chip_j in enumerate(chips):
                forward(st, sm, e, j, chip_j, x, y, c, 1 - c).wait_recv()
        for e in range(n):
            own(p, st, sm, e, x, y, c).wait_send()
            for j, chip_j in enumerate(chips):
                ici(p, st, sm, e, j, chip_j, x, y, c).wait_send()
                forward(st, sm, e, j, chip_j, x, y, c, c).wait_send()
            local(p, st, sm, e, x, y, c).wait()

    outs = [SDS((2,) + a.shape, a.dtype) for a in parts]
    return Exchange(list(parts), outs, [DMA((8 * n,)), DMA((8 * n,)), DMA((n,))], start, finish,
                    [(0, functools.partial(relay, e)) for e in range(n)])


def tail_reduce(d_norm_mix, d_norm_mem, d_norm_ffn, d_gains, d_cw8, d_cbias, d_qg, d_kg, d_mqg, d_mkg, d_sink8, loss8, tail):
    n = len(tail)
    scatter = scatter_exchange([SDS((4,) + a.shape[2:], WIRE) for a in tail])

    def half_copy(g, sib, hsem, e, j, slot, x, y, c):
        return _remote(g[e].at[slot, 1 - c], sib[e].at[slot], hsem[0].at[4 * e + j], hsem[1].at[4 * e + j], (x, y, 1 - c))

    def body(nm_ref, nmem_ref, nf_ref, gn_ref, cw_ref, cb_ref, qg_ref, kg_ref, mqg_ref, mkg_ref, sk_ref, ls_ref, *rest):
        g, o_ref, st = rest[:n], rest[n], rest[n + 1:2 * n + 1]
        buf, ssem, rsem = rest[2 * n + 1:2 * n + 4]
        own, sib, part = (rest[2 * n + 4 + i * n:2 * n + 4 + (i + 1) * n] for i in range(3))
        lsem = rest[5 * n + 4]
        hsem, xsem = rest[5 * n + 5:5 * n + 7], rest[5 * n + 7:]
        x, y, c = _place()
        loads = [pltpu.make_async_copy(g[e].at[:, c], own[e], lsem.at[e]) for e in range(n)]
        for ld in loads:
            ld.start()
        for j, slot in enumerate([2 * cx + cy for cx, cy in _other_chips(x, y)] + [2 * x + y]):
            for e in range(n):
                half_copy(g, sib, hsem, e, j, slot, x, y, c).start()
        me = 4 * x + 2 * y + c
        mine = buf.at[me]
        mine[...] = jnp.zeros((8, 1024), f32)
        mine[0:1, :] = nm_ref[...]
        mine[1:2, :] = nmem_ref[...]
        mine[2:3, :] = nf_ref[...]
        mine[3:4, :] = gn_ref[...]
        for j in range(3):
            mine[4:5, pl.ds(j * CONV_W, CONV_W)] = cw_ref[j:j + 1, :]
        mine[4:5, pl.ds(3 * CONV_W, CONV_W)] = cb_ref[...]
        for j, r in enumerate((qg_ref, kg_ref, mqg_ref, mkg_ref)):
            mine[5:6, pl.ds(j * HD, HD)] = r[...]
        mine[5:6, pl.ds(256, 128)] = sk_ref[0:1, :]
        mine[5:6, pl.ds(384, 128)] = ls_ref[0:1, :]

        def peer_of(m):
            return (1 - x if m & 4 else x, 1 - y if m & 2 else y, 1 - c if m & 1 else c)

        for m in range(1, 8):
            _remote(mine, mine, ssem.at[m - 1], rsem.at[m - 1], peer_of(m)).start()
        for ld in loads:
            ld.wait()

        def chip_partial(j, slot):
            for e in range(n):
                half_copy(g, sib, hsem, e, j, slot, x, y, c).wait()
                part[e][slot] = (own[e][slot] + sib[e][slot]).astype(WIRE)

        scatter.start(part, st, xsem, chip_partial)
        for _, hand_on in scatter.relays:
            hand_on(part, st, xsem)
        scatter.finish(part, st, xsem)
        for m in range(1, 8):
            p = peer_of(m)
            got = buf.at[4 * p[0] + 2 * p[1] + p[2]]
            _remote(got, got, ssem.at[m - 1], rsem.at[m - 1], p).wait_recv()
        for m in range(1, 8):
            _remote(mine, mine, ssem.at[m - 1], rsem.at[m - 1], peer_of(m)).wait_send()
        acc = buf[0]
        for d in range(1, 8):
            acc = acc + buf[d]
        o_ref[...] = acc

    ins = [d_norm_mix, d_norm_mem, d_norm_ffn, d_gains, d_cw8, d_cbias, d_qg, d_kg, d_mqg, d_mkg, d_sink8, loss8]
    half_shape = [(4,) + a.shape[2:] for a in tail]
    scratch = ([pltpu.VMEM((8, 8, 1024), f32), DMA((7,)), DMA((7,))]
               + [pltpu.VMEM(s, f32) for s in half_shape] * 2 + [pltpu.VMEM(s, WIRE) for s in half_shape]
               + [DMA((n,)), DMA((4 * n,)), DMA((4 * n,))] + list(scatter.sems))
    res = _run("tail_reduce", body, (), ins + list(tail), [VM] * len(ins) + [ANY] * n,
               [SDS((8, 1024), f32)] + list(scatter.outs), [VM] + [ANY] * n, scratch=scratch, vmem_mib=40)
    return res[0], res[1:]


def add_halves(cidx, grads, stages, name, nch=2):
    n = len(grads)

    def body(c_ref, *refs):
        g, st, o = refs[:n], refs[n:2 * n], refs[2 * n:]
        for e in range(n):
            o[e][...] = (g[e][...] + st[e][...]).astype(WIRE)

    in_specs, out_specs, out_shape = [], [], []
    for a in grads:
        hr, C = a.shape[2], a.shape[3]
        in_specs.append(pl.BlockSpec((None, None, hr // nch, C), lambda s, q, c_ref: (s, c_ref[0], q, 0)))
    for a in stages:
        hr, C = a.shape[1], a.shape[2]
        in_specs.append(pl.BlockSpec((None, hr // nch, C), lambda s, q, c_ref: (s, q, 0)))
        out_specs.append(pl.BlockSpec((None, hr // nch, C), lambda s, q, c_ref: (s, q, 0)))
        out_shape.append(SDS(a.shape, WIRE))
    return pl.pallas_call(
        body, name=name, out_shape=out_shape,
        grid_spec=pltpu.PrefetchScalarGridSpec(num_scalar_prefetch=1, grid=(4, nch), in_specs=in_specs, out_specs=out_specs),
        compiler_params=pltpu.CompilerParams(dimension_semantics=("arbitrary", "arbitrary")),
    )(cidx, *grads, *stages)


def _adamw_math(w, g, m, v):
    m = ADAM_B1 * m + (1.0 - ADAM_B1) * g
    v = ADAM_B2 * v + (1.0 - ADAM_B2) * (g * g)
    m_hat = m / (1.0 - ADAM_B1 ** ADAM_STEP)
    v_hat = v / (1.0 - ADAM_B2 ** ADAM_STEP)
    delta = -ADAM_LR * (m_hat / (jnp.sqrt(v_hat) + ADAM_EPS) + ADAM_WD * w)
    return delta, m, v


def _sum_chips(st):
    return ((st[0].astype(f32) + st[1].astype(f32)) + st[2].astype(f32)) + st[3].astype(f32)


def adamw_big(name, stages, ws, ms, vs, nstep, exchange=None):
    n = len(stages)

    def body(*refs):
        st, w, m, v = refs[:n], refs[n:2 * n], refs[2 * n:3 * n], refs[3 * n:4 * n]
        outs = refs[4 * n:]
        for e in range(n):
            g = jnp.concatenate([_sum_chips(st[e].at[0]), _sum_chips(st[e].at[1])], axis=0)
            d, mm, vv = _adamw_math(w[e][...], g, m[e][...], v[e][...])
            outs[4 * e][...] = g
            outs[4 * e + 1][...] = d
            outs[4 * e + 2][...] = mm
            outs[4 * e + 3][...] = vv

    st_specs, w_specs = [], []
    for e in range(n):
        _, _, hr, C = stages[e].shape
        st_specs.append(pl.BlockSpec((2, 4, hr, C // nstep), lambda i: (0, 0, 0, i)))
        w_specs.append(pl.BlockSpec((2 * hr, C // nstep), lambda i: (0, i)))
    out_specs = [s for s in w_specs for _ in range(4)]
    out_shape = [SDS(w.shape, f32) for w in ws for _ in range(4)]
    res = _run(name, body, (nstep,), list(stages) + list(ws) + list(ms) + list(vs), st_specs + w_specs * 3,
               out_shape, out_specs, vmem_mib=48, exchange=exchange)
    res, sent = res if exchange is not None else (res, None)
    return [res[4 * e:4 * e + 4] for e in range(n)], sent


def adamw_small(tot, pk_w, pk_m, pk_v, shapes):
    def body(tot_ref, w_ref, m_ref, v_ref, *outs):
        x, y, _ = _place()
        chip = 2 * x + y
        taps = []
        for j in range(3):
            mine = tot_ref[4:5, j * CONV_W:j * CONV_W + HD]
            for s in range(1, 4):
                mine = jnp.where(chip == s, tot_ref[4:5, j * CONV_W + s * HD:j * CONV_W + (s + 1) * HD], mine)
            taps.append(mine)
        row4 = jnp.concatenate(taps + [jnp.zeros((1, 3 * CONV_W - 3 * HD), f32), tot_ref[4:5, 3 * CONV_W:]], axis=1)
        tot_v = tot_ref[...]
        row = lax.broadcasted_iota(jnp.int32, tot_v.shape, 0)
        g = jnp.where(row == 4, jnp.broadcast_to(row4, tot_v.shape), tot_v)
        d, mm, vv = _adamw_math(w_ref[...], g, m_ref[...], v_ref[...])
        for i, name in enumerate(SMALL):
            for k, val in enumerate((g, d, mm, vv)):
                if name == "conv_w":
                    outs[4 * i + k][...] = jnp.concatenate([val[4:5, j * HD:(j + 1) * HD] for j in range(3)], axis=0)[None]
                else:
                    r, c0, w = SMALL_AT[name]
                    outs[4 * i + k][...] = val[r:r + 1, c0:c0 + w]

    out_shape = [SDS(shapes[k], f32) for k in SMALL for _ in range(4)]
    res = _run("adamw_small", body, (), [tot, pk_w, pk_m, pk_v], [VM] * 4, out_shape, [VM] * len(out_shape))
    return {k: res[4 * i:4 * i + 4] for i, k in enumerate(SMALL)}


def prep_weights(name, shards, exchange=None):
    n = len(shards)

    def body(*refs):
        for e in range(n):
            refs[n + e][...] = _c(refs[e][...])

    return _run(name, body, (), shards, [VM] * n, [SDS(a.shape, MXU) for a in shards], [VM] * n, vmem_mib=48, exchange=exchange)


def mem_kv_fwd(mem2d, pk, wmkv):
    M, D = mem2d.shape

    def body(m_ref, pk_ref, w_ref, mn_ref, kv_ref, km_ref, vm_ref):
        m = m_ref[...]
        mn = _c(m * _rstd(m) * _small(pk_ref, "norm_mem"))
        mn_ref[...] = mn
        kv = _nn(mn, w_ref[...])
        kv_ref[...] = kv
        kk = kv[:, :MEM_W]
        km_ref[...] = _c(kk * _heads_rstd(kk) * _lanes(_small(pk_ref, "mem_k_norm"), MEM_W))
        vm_ref[...] = _c(kv[:, MEM_W:])

    return _run("mem_kv_fwd", body, (), [mem2d, pk, wmkv], [VM] * 3,
                [SDS((M, D), MXU), SDS((M, 2 * MEM_W), f32), SDS((M, MEM_W), MXU), SDS((M, MEM_W), MXU)], [VM] * 4)


QKV_W = ATT_W + 2 * KV_W + MEM_W


def in_proj_fwd(x2d, pk, winT, tm, exchange):
    T, D = x2d.shape
    P = winT.shape[0]

    def body(x_ref, pk_ref, w_ref, xn_ref, proj_ref, qkv_ref):
        xv = x_ref[...]
        xn = _c(xv * _rstd(xv) * _small(pk_ref, "norm_mix"))
        xn_ref[...] = xn
        proj = _nt(xn, w_ref[...])
        proj_ref[...] = proj
        q, k = proj[:, :ATT_W], proj[:, ATT_W:ATT_W + KV_W]
        qm = proj[:, P - MEM_W:]
        qkv_ref[...] = jnp.concatenate(
            [_c(q * _heads_rstd(q) * _lanes(_small(pk_ref, "q_norm"), ATT_W)),
             _c(k * _heads_rstd(k) * _lanes(_small(pk_ref, "k_norm"), KV_W)),
             _c(proj[:, ATT_W + KV_W:ATT_W + 2 * KV_W]),
             _c(qm * _heads_rstd(qm) * _lanes(_small(pk_ref, "mem_q_norm"), MEM_W))], axis=1)

    return _run("in_proj_fwd", body, (T // tm,), [x2d, pk, winT],
                [pl.BlockSpec((tm, D), lambda i: (i, 0)), VM, VM],
                [SDS((T, D), MXU), SDS((T, P), f32), SDS((T, QKV_W), MXU)],
                [pl.BlockSpec((tm, D), lambda i: (i, 0)), pl.BlockSpec((tm, P), lambda i: (i, 0)),
                 pl.BlockSpec((tm, QKV_W), lambda i: (i, 0))],
                vmem_mib=40, exchange=exchange)


def _swa_bias_table():
    r = np.arange(GQA * BLK)[:, None]
    k = np.arange(2 * BLK)[None, :]
    dist = (r % BLK) + BLK - k
    band = (dist >= 0) & (dist < BLK)
    tab = np.empty((2, N_KV, GQA * BLK, 2 * BLK), np.float32)
    for later in range(2):
        valid = band & ((k >= BLK) | (later == 1))
        for g in range(N_KV):
            slope = 2.0 ** -(g * GQA + r // BLK + 1.0)
            tab[later, g] = np.where(valid, -slope * dist, NEG)
    return jnp.asarray(tab)


def _sink_column(g, sk_ref):
    hrow = lax.broadcasted_iota(jnp.int32, (GQA * BLK, 1), 0) // BLK
    sink = jnp.zeros((GQA * BLK, 1), f32)
    for hh in range(GQA):
        sink = jnp.where(hrow == hh, sk_ref[g * GQA + hh:g * GQA + hh + 1, 0:1], sink)
    return sink


def _stack_heads(v, g):
    return jnp.concatenate([v[:, (g * GQA + hh) * HD:(g * GQA + hh + 1) * HD] for hh in range(GQA)], axis=0)


def attn_fwd(qkv, sink_rows, BL, S, exchange, qb=2):
    NS = S // (qb * BLK)
    T = BL * S

    def body(q_ref, kc_ref, kp_ref, vc_ref, vp_ref, sk_ref, tab_ref, o_ref):
        j = pl.program_id(1)
        kall = jnp.concatenate([kp_ref[...], kc_ref[...]], axis=0)
        vall = jnp.concatenate([vp_ref[...], vc_ref[...]], axis=0)
        ones = jnp.ones((2 * BLK, HD), MXU)
        for b in range(qb):
            q = q_ref[pl.ds(b * BLK, BLK), :]
            k2, v2 = kall[b * BLK:(b + 2) * BLK], vall[b * BLK:(b + 2) * BLK]
            later = jnp.minimum(j, 1) if b == 0 else 1
            for g in range(N_KV):
                kn, vh = k2[:, g * HD:(g + 1) * HD], v2[:, g * HD:(g + 1) * HD]
                s = _nt(_stack_heads(q, g), kn) * (HD ** -0.5) + tab_ref[later, g]
                e, es = _exp_scores(s, _sink_column(g, sk_ref))
                eb = _c(e)
                o = _nn(eb, vh) * (1.0 / (_nn(eb, ones) + es))
                for hh in range(GQA):
                    o_ref[pl.ds(b * BLK, BLK), pl.ds((g * GQA + hh) * HD, HD)] = o[hh * BLK:(hh + 1) * BLK]

    cur = lambda col: (lambda b, j: (b * NS + j, col))
    prev = lambda col: (lambda b, j: (qb * (b * NS + j) - jnp.minimum(j, 1), col))
    return _run("attn_fwd", body, (BL, NS), [qkv, qkv, qkv, qkv, qkv, sink_rows, _swa_bias_table()],
                [pl.BlockSpec((qb * BLK, ATT_W), cur(0)),
                 pl.BlockSpec((qb * BLK, KV_W), cur(4)), pl.BlockSpec((BLK, KV_W), prev(4)),
                 pl.BlockSpec((qb * BLK, KV_W), cur(5)), pl.BlockSpec((BLK, KV_W), prev(5)),
                 pl.BlockSpec((8, 128), lambda b, j: (0, 0)), VM],
                [SDS((T, ATT_W), f32)], [pl.BlockSpec((qb * BLK, ATT_W), cur(0))], exchange=exchange)


def _conv_taps(u, uh):
    row = lax.broadcasted_iota(jnp.int32, u.shape, 0)
    u1 = jnp.where(row == 0, uh[7:8, :], pltpu.roll(u, 1, 0))
    u2 = jnp.where(row == 0, uh[6:7, :], jnp.where(row == 1, uh[7:8, :], pltpu.roll(u, 2, 0)))
    return u1, u2


def _mem_head(qm, km, vm, h):
    qh, kh, vh = (a[:, h * HD:(h + 1) * HD] for a in (qm, km, vm))
    e, _ = _exp_scores(_nt(qh, kh) * (HD ** -0.5))
    return qh, kh, vh, e


def mixer_tail_fwd(x2d, attn_out, proj, qkv, km, vm, conv_w8, pk, wout, S, tm, exchange):
    T, D = x2d.shape
    NM = km.shape[0] // (T // S)

    def body(x_ref, ao_ref, ch_ref, cb_ref, cc_ref, chh_ref, cch_ref, qm_ref, km_ref, vm_ref, cw_ref, pk_ref,
             wout_ref, co_ref, mo_ref, mg_ref, x1_ref, h_ref):
        first = (pl.program_id(0) * tm) % S == 0
        u = cc_ref[...] * ch_ref[...]
        uh = jnp.where(first, 0.0, cch_ref[...] * chh_ref[...])
        u1, u2 = _conv_taps(u, uh)
        conv = cw_ref[0:1, :] * u2 + cw_ref[1:2, :] * u1 + cw_ref[2:3, :] * u + _small(pk_ref, "conv_b")
        conv_out = cb_ref[...] * conv
        co_ref[...] = conv_out
        qm, kmv, vmv = qm_ref[...], km_ref[...], vm_ref[...]
        ones = jnp.ones((NM, HD), MXU)
        for h in range(N_MEMH):
            _, _, vh, e = _mem_head(qm, kmv, vmv, h)
            eb = _c(e)
            mo_ref[:, pl.ds(h * HD, HD)] = _nn(eb, vh) * (1.0 / _nn(eb, ones))
        mem_out = mo_ref[...]
        ao = ao_ref[...]
        merged = _c(jnp.concatenate([ao * _rstd(ao) * _small(pk_ref, "out_norm_attn"),
                                     conv_out * _rstd(conv_out) * _small(pk_ref, "out_norm_conv"),
                                     mem_out * _rstd(mem_out) * _small(pk_ref, "out_norm_mem")], axis=1))
        mg_ref[...] = merged
        x1 = x_ref[...] + _nn(merged, wout_ref[...])
        x1_ref[...] = x1
        h_ref[...] = _c(x1 * _rstd(x1) * _small(pk_ref, "norm_ffn"))

    tile = lambda w, col: pl.BlockSpec((tm, w), lambda i: (i, col))
    halo = lambda col: pl.BlockSpec((8, CONV_W), lambda i: (jnp.maximum(i * (tm // 8) - 1, 0), col))
    seq = pl.BlockSpec((NM, MEM_W), lambda i: ((i * tm) // S, 0))
    small = lambda a: pl.BlockSpec(a.shape, lambda i: (0, 0))
    return _run("mixer_tail_fwd", body, (T // tm,),
                [x2d, attn_out, proj, proj, proj, proj, proj, qkv, km, vm, conv_w8, pk, wout],
                [tile(D, 0), tile(ATT_W, 0), tile(CONV_W, 3), tile(CONV_W, 4), tile(CONV_W, 5), halo(3), halo(5),
                 tile(MEM_W, 3), seq, seq, VM, VM, VM],
                [SDS((T, CONV_W), f32), SDS((T, MEM_W), f32), SDS((T, D), MXU), SDS((T, D), f32), SDS((T, D), MXU)],
                [tile(CONV_W, 0), tile(MEM_W, 0), tile(D, 0), tile(D, 0), tile(D, 0)], vmem_mib=40, exchange=exchange)


def ffn_fwd_bwd(h, x1, tgt, wgT, wuT, wd, pk, tm):
    T, D = x1.shape
    F = wd.shape[0]

    def body(h_ref, x1_ref, t_ref, wg_ref, wu_ref, wd_ref, pk_ref,
             dx1_ref, dx2_ref, act_ref, dg_ref, du_ref, loss_ref, dgf_ref):
        @pl.when(pl.program_id(0) == 0)
        def _():
            loss_ref[...] = jnp.zeros_like(loss_ref)
            dgf_ref[...] = jnp.zeros_like(dgf_ref)

        hv = h_ref[...]
        gate = _nt(hv, wg_ref[...])
        up = _nt(hv, wu_ref[...])
        sg = jax.nn.sigmoid(gate)
        sl = gate * sg
        act = _c(sl * up)
        act_ref[...] = act
        x1v = x1_ref[...]
        diff = (x1v + _nn(act, wd_ref[...])) - t_ref[...]
        loss_ref[...] += 0.5 * jnp.sum(jnp.sum(diff * diff, axis=-1, keepdims=True) / D, axis=0, keepdims=True)
        dx2 = diff / D
        dx2b = _c(dx2)
        dx2_ref[...] = dx2b
        d_act = _nt(dx2b, wd_ref[...])
        d_up = _c(d_act * sl)
        d_gate = _c(d_act * up * (sg * (1.0 + gate * (1.0 - sg))))
        du_ref[...] = d_up
        dg_ref[...] = d_gate
        dh = _nn(d_gate, wg_ref[...]) + _nn(d_up, wu_ref[...])
        dv, dgf = _norm_bwd(dh, x1v, _rstd(x1v), _small(pk_ref, "norm_ffn"))
        dx1_ref[...] = dx2 + dv
        dgf_ref[...] += dgf

    tile = lambda w: pl.BlockSpec((tm, w), lambda i: (i, 0))
    return _run("ffn_fwd_bwd", body, (T // tm,), [h, x1, tgt, wgT, wuT, wd, pk],
                [tile(D), tile(D), tile(D), VM, VM, VM, VM],
                [SDS((T, D), f32), SDS((T, D), MXU), SDS((T, F), MXU), SDS((T, F), MXU), SDS((T, F), MXU),
                 SDS((8, 128), f32), SDS((1, D), f32)],
                [tile(D), tile(D), tile(F), tile(F), tile(F), pl.BlockSpec((8, 128), lambda i: (0, 0)),
                 pl.BlockSpec((1, D), lambda i: (0, 0))], vmem_mib=56)


def matmul_tn(a, b, name, tmo, tk):
    T, M = a.shape
    N = b.shape[1]

    def body(a_ref, b_ref, o_ref):
        @pl.when(pl.program_id(1) == 0)
        def _():
            o_ref[...] = jnp.zeros_like(o_ref)

        o_ref[...] += _tn(a_ref[...], b_ref[...])

    return _run(name, body, (M // tmo, T // tk), [a, b],
                [pl.BlockSpec((tk, tmo), lambda m, k: (k, m)), pl.BlockSpec((tk, N), lambda m, k: (k, 0))],
                [SDS((M, N), f32)], [pl.BlockSpec((tmo, N), lambda m, k: (m, 0))], vmem_mib=48)[0]


def out_proj_bwd(dx1, merged, attn_out, conv_out, mem_out, pk, wout, tm):
    T, D = dx1.shape

    def body(dx1_ref, mg_ref, ao_ref, co_ref, mo_ref, pk_ref, w_ref,
             dao_ref, dco_ref, dmo_ref, dw_ref, dgain_ref):
        @pl.when(pl.program_id(0) == 0)
        def _():
            dw_ref[...] = jnp.zeros_like(dw_ref)
            dgain_ref[...] = jnp.zeros_like(dgain_ref)

        dxb = _c(dx1_ref[...])
        dw_ref[...] += _tn(mg_ref[...], dxb)
        dmg = _nt(dxb, w_ref[...])
        ao, co, mo = ao_ref[...], co_ref[...], mo_ref[...]
        da, ga = _norm_bwd(dmg[:, :ATT_W], ao, _rstd(ao), _small(pk_ref, "out_norm_attn"))
        dc, gc = _norm_bwd(dmg[:, ATT_W:ATT_W + CONV_W], co, _rstd(co), _small(pk_ref, "out_norm_conv"))
        dm, gm = _norm_bwd(dmg[:, ATT_W + CONV_W:], mo, _rstd(mo), _small(pk_ref, "out_norm_mem"))
        dao_ref[...] = da
        dco_ref[...] = dc
        dmo_ref[...] = dm
        dgain_ref[...] += jnp.concatenate([ga, gc, gm], axis=1)

    tile = lambda w: pl.BlockSpec((tm, w), lambda i: (i, 0))
    return _run("out_proj_bwd", body, (T // tm,), [dx1, merged, attn_out, conv_out, mem_out, pk, wout],
                [tile(D), tile(D), tile(ATT_W), tile(CONV_W), tile(MEM_W), VM, VM],
                [SDS((T, ATT_W), f32), SDS((T, CONV_W), f32), SDS((T, MEM_W), f32), SDS((D, D), f32), SDS((1, D), f32)],
                [tile(ATT_W), tile(CONV_W), tile(MEM_W), pl.BlockSpec((D, D), lambda i: (0, 0)),
                 pl.BlockSpec((1, D), lambda i: (0, 0))], vmem_mib=40)


def attn_bwd(qkv, d_attn, attn_out, sink_rows, BL, S, exchange):
    NB = S // BLK
    T = BL * S

    def body(q_ref, kc_ref, kp_ref, vc_ref, vp_ref, do_ref, ao_ref, sk_ref, tab_ref,
             dq_ref, dk_ref, dv_ref, dsk_ref, pend_k, pend_v):
        b, j = pl.program_id(0), pl.program_id(1)

        @pl.when((b == 0) & (j == 0))
        def _():
            dsk_ref[...] = jnp.zeros_like(dsk_ref)

        @pl.when(j == 0)
        def _():
            pend_k[...] = jnp.zeros_like(pend_k)
            pend_v[...] = jnp.zeros_like(pend_v)

        @pl.when(j < NB)
        def _():
            q, do, ao = q_ref[...], do_ref[...], ao_ref[...]
            k2 = jnp.concatenate([kp_ref[...], kc_ref[...]], axis=0)
            v2 = jnp.concatenate([vp_ref[...], vc_ref[...]], axis=0)
            lane = lax.broadcasted_iota(jnp.int32, (8, 128), 1)
            ones_w = jnp.ones((2 * BLK, 2 * BLK), MXU)
            dsk = jnp.zeros((8, 128), f32)
            dks, dvs = [], []
            for g in range(N_KV):
                kn, vh = k2[:, g * HD:(g + 1) * HD], v2[:, g * HD:(g + 1) * HD]
                qs = _stack_heads(q, g)
                s = _nt(qs, kn) * (HD ** -0.5) + tab_ref[g]
                e, es = _exp_scores(s, _sink_column(g, sk_ref))
                eb = _c(e)
                inv_w = 1.0 / (_nn(eb, ones_w) + es)
                inv_n = inv_w[:, :HD]
                dos = _stack_heads(do, g)
                delta = _rowsum_mxu(dos * _stack_heads(ao, g), 2 * BLK)
                dp = _nt(_c(dos), vh)
                ds = _c(e * inv_w * (dp - delta) * (HD ** -0.5))
                t = es * inv_n[:, 0:1] * delta[:, 0:1]
                for hh in range(GQA):
                    dsk = dsk + jnp.where(lane == g * GQA + hh, -jnp.sum(t[hh * BLK:(hh + 1) * BLK]), 0.0)
                dvs.append(_tn(eb, _c(dos * inv_n)))
                dks.append(_tn(ds, qs))
                dqs = _nn(ds, kn)
                for hh in range(GQA):
                    dq_ref[:, pl.ds((g * GQA + hh) * HD, HD)] = dqs[hh * BLK:(hh + 1) * BLK]
            dk2 = jnp.concatenate(dks, axis=1)
            dv2 = jnp.concatenate(dvs, axis=1)
            dk_ref[...] = pend_k[...] + dk2[:BLK]
            dv_ref[...] = pend_v[...] + dv2[:BLK]
            pend_k[...] = dk2[BLK:]
            pend_v[...] = dv2[BLK:]
            dsk_ref[...] += dsk

        @pl.when(j == NB)
        def _():
            dk_ref[...] = pend_k[...]
            dv_ref[...] = pend_v[...]

    cur = lambda col: (lambda b, j: (b * NB + jnp.minimum(j, NB - 1), col))
    prev = lambda col: (lambda b, j: (b * NB + jnp.maximum(j - 1, 0), col))
    small = lambda shape: pl.BlockSpec(shape, lambda b, j: (0, 0))
    return _run("attn_bwd", body, (BL, NB + 1), [qkv, qkv, qkv, qkv, qkv, d_attn, attn_out, sink_rows, _swa_bias_table()],
                [pl.BlockSpec((BLK, ATT_W), cur(0)),
                 pl.BlockSpec((BLK, KV_W), cur(4)), pl.BlockSpec((BLK, KV_W), prev(4)),
                 pl.BlockSpec((BLK, KV_W), cur(5)), pl.BlockSpec((BLK, KV_W), prev(5)),
                 pl.BlockSpec((BLK, ATT_W), cur(0)), pl.BlockSpec((BLK, ATT_W), cur(0)), small((8, 128)),
                 pl.BlockSpec((None, N_KV, GQA * BLK, 2 * BLK), lambda b, j: (jnp.minimum(j, 1), 0, 0, 0))],
                [SDS((T, ATT_W), f32), SDS((T, KV_W), f32), SDS((T, KV_W), f32), SDS((8, 128), f32)],
                [pl.BlockSpec((BLK, ATT_W), cur(0)), pl.BlockSpec((BLK, KV_W), prev(0)),
                 pl.BlockSpec((BLK, KV_W), prev(0)), small((8, 128))],
                scratch=[pltpu.VMEM((BLK, KV_W), f32)] * 2, exchange=exchange)


def mem_conv_bwd(d_mem_out, mem_out, d_conv_out, proj, qkv, km, vm, conv_w8, pk, S, tm, exchange):
    T = d_mem_out.shape[0]
    NM = km.shape[0] // (T // S)

    def body(dmo_ref, mo_ref, dco_ref, ch_ref, cb_ref, cc_ref, chh_ref, cch_ref, qm_ref, km_ref, vm_ref, cw_ref,
             pk_ref, dqm_ref, dkm_ref, dvm_ref, dcb_ref, dcv_ref, dcw_ref, dcbias_ref):
        i = pl.program_id(0)
        first = (i * tm) % S == 0

        @pl.when(i == 0)
        def _():
            dcw_ref[...] = jnp.zeros_like(dcw_ref)
            dcbias_ref[...] = jnp.zeros_like(dcbias_ref)

        @pl.when(first)
        def _():
            dkm_ref[...] = jnp.zeros_like(dkm_ref)
            dvm_ref[...] = jnp.zeros_like(dvm_ref)

        qm, kmv, vmv, dmo, mo = qm_ref[...], km_ref[...], vm_ref[...], dmo_ref[...], mo_ref[...]
        ones_w = jnp.ones((NM, NM), MXU)
        for h in range(N_MEMH):
            qh, kh, vh, e = _mem_head(qm, kmv, vmv, h)
            eb = _c(e)
            doh = dmo[:, h * HD:(h + 1) * HD]
            delta = _rowsum_mxu(doh * mo[:, h * HD:(h + 1) * HD], NM)
            dp = _nt(_c(doh), vh)
            inv_w = 1.0 / _nn(eb, ones_w)
            ds = _c(e * inv_w * (dp - delta) * (HD ** -0.5))
            dvm_ref[:, pl.ds(h * HD, HD)] += _tn(eb, _c(doh * inv_w[:, :HD]))
            dkm_ref[:, pl.ds(h * HD, HD)] += _tn(ds, qh)
            dqm_ref[:, pl.ds(h * HD, HD)] = _nn(ds, kh)

        u = cc_ref[...] * ch_ref[...]
        uh = jnp.where(first, 0.0, cch_ref[...] * chh_ref[...])
        u1, u2 = _conv_taps(u, uh)
        conv = cw_ref[0:1, :] * u2 + cw_ref[1:2, :] * u1 + cw_ref[2:3, :] * u + _small(pk_ref, "conv_b")
        dy = dco_ref[...]
        dcb_ref[...] = dy * conv
        dcv = dy * cb_ref[...]
        dcv_ref[...] = dcv
        dcbias_ref[...] += jnp.sum(dcv, axis=0, keepdims=True)
        dcw_ref[0:1, :] += jnp.sum(dcv * u2, axis=0, keepdims=True)
        dcw_ref[1:2, :] += jnp.sum(dcv * u1, axis=0, keepdims=True)
        dcw_ref[2:3, :] += jnp.sum(dcv * u, axis=0, keepdims=True)

    tile = lambda w, col: pl.BlockSpec((tm, w), lambda i: (i, col))
    halo = lambda col: pl.BlockSpec((8, CONV_W), lambda i: (jnp.maximum(i * (tm // 8) - 1, 0), col))
    seq = pl.BlockSpec((NM, MEM_W), lambda i: ((i * tm) // S, 0))
    const = lambda shape: pl.BlockSpec(shape, lambda i: (0, 0))
    return _run("mem_conv_bwd", body, (T // tm,),
                [d_mem_out, mem_out, d_conv_out, proj, proj, proj, proj, proj, qkv, km, vm, conv_w8, pk],
                [tile(MEM_W, 0), tile(MEM_W, 0), tile(CONV_W, 0), tile(CONV_W, 3), tile(CONV_W, 4), tile(CONV_W, 5),
                 halo(3), halo(5), tile(MEM_W, 3), seq, seq, VM, VM],
                [SDS((T, MEM_W), f32), SDS(km.shape, f32), SDS(km.shape, f32),
                 SDS((T, CONV_W), f32), SDS((T, CONV_W), f32), SDS((8, CONV_W), f32), SDS((1, CONV_W), f32)],
                [tile(MEM_W, 0), seq, seq, tile(CONV_W, 0), tile(CONV_W, 0), const((8, CONV_W)), const((1, CONV_W))],
                vmem_mib=48, exchange=exchange)


def in_proj_bwd(dqn, dkn, dv, dcb, dcv, dqmn, proj, conv_w8, xn, x2d, dx1, pk, winT, S, tm, stages, ws, ms, vs):
    T, D = x2d.shape
    P = winT.shape[0]
    last_blk = T // 8 - 1
    n = len(stages)
    nsteps = T // tm
    tile_w = ws[0].shape[1] // (nsteps // 2)
    turn = [e * 2 // n for e in range(n)]

    def body(dq_ref, dk_ref, dv_ref, dcb_ref, dcv_ref, dcvn_ref, dqm_ref, qa_ref, ka_ref, ch_ref, cc_ref, qma_ref,
             cw_ref, xn_ref, x_ref, dx1_ref, pk_ref, w_ref, *rest):
        st, aw, am, av = (rest[k * n:(k + 1) * n] for k in range(4))
        dx_ref, dw_ref, dg_ref, dqg_ref, dkg_ref, dmqg_ref = rest[4 * n:4 * n + 6]
        aouts = rest[4 * n + 6:]
        i = pl.program_id(0)

        for parity in range(2):
            @pl.when(i % 2 == parity)
            def _(parity=parity):
                for e in range(n):
                    if turn[e] == parity:
                        g = jnp.concatenate([_sum_chips(st[e].at[0]), _sum_chips(st[e].at[1])], axis=0)
                        d, mm, vv = _adamw_math(aw[e][...], g, am[e][...], av[e][...])
                        for k, val in enumerate((g, d, mm, vv)):
                            aouts[4 * e + k][...] = val

        @pl.when(i == 0)
        def _():
            dw_ref[...] = jnp.zeros_like(dw_ref)
            dg_ref[...] = jnp.zeros_like(dg_ref)
            dqg_ref[...] = jnp.zeros_like(dqg_ref)
            dkg_ref[...] = jnp.zeros_like(dkg_ref)
            dmqg_ref[...] = jnp.zeros_like(dmqg_ref)

        dqa, gq = _heads_norm_bwd(dq_ref[...], qa_ref[...], _small(pk_ref, "q_norm"))
        dka, gk = _heads_norm_bwd(dk_ref[...], ka_ref[...], _small(pk_ref, "k_norm"))
        dqma, gmq = _heads_norm_bwd(dqm_ref[...], qma_ref[...], _small(pk_ref, "mem_q_norm"))
        dqg_ref[...] += gq
        dkg_ref[...] += gk
        dmqg_ref[...] += gmq

        last = ((i + 1) * tm) % S == 0
        dcv = dcv_ref[...]
        nxt = jnp.where(last, 0.0, dcvn_ref[...])
        row = lax.broadcasted_iota(jnp.int32, dcv.shape, 0)
        n1 = jnp.where(row == tm - 1, nxt[0:1, :], pltpu.roll(dcv, tm - 1, 0))
        n2 = jnp.where(row == tm - 2, nxt[0:1, :], jnp.where(row == tm - 1, nxt[1:2, :], pltpu.roll(dcv, tm - 2, 0)))
        du = cw_ref[2:3, :] * dcv + cw_ref[1:2, :] * n1 + cw_ref[0:1, :] * n2
        d_proj = jnp.concatenate([_c(dqa), _c(dka), _c(dv_ref[...]), _c(du * cc_ref[...]),
                                  _c(dcb_ref[...]), _c(du * ch_ref[...]), _c(dqma)], axis=1)
        dw_ref[...] += _tn(d_proj, xn_ref[...])
        xv = x_ref[...]
        dv_, dg = _norm_bwd(_nn(d_proj, w_ref[...]), xv, _rstd(xv), _small(pk_ref, "norm_mix"))
        dx_ref[...] = dx1_ref[...] + dv_
        dg_ref[...] += dg

    tile = lambda w, col=0: pl.BlockSpec((tm, w), lambda i: (i, col))
    nhalo = pl.BlockSpec((8, CONV_W), lambda i: (jnp.minimum((i + 1) * (tm // 8), last_blk), 0))
    const = lambda shape: pl.BlockSpec(shape, lambda i: (0, 0))
    st_specs = [pl.BlockSpec((2, 4, s.shape[2], tile_w), lambda i: (0, 0, 0, i // 2)) for s in stages]
    w_specs = [pl.BlockSpec((w.shape[0], tile_w), lambda i: (0, i // 2)) for w in ws]
    res = _run("in_proj_bwd", body, (nsteps,),
               [dqn, dkn, dv, dcb, dcv, dcv, dqmn, proj, proj, proj, proj, proj, conv_w8, xn, x2d, dx1, pk, winT]
               + list(stages) + list(ws) + list(ms) + list(vs),
               [tile(ATT_W), tile(KV_W), tile(KV_W), tile(CONV_W), tile(CONV_W), nhalo, tile(MEM_W),
                tile(ATT_W, 0), tile(KV_W, 4), tile(CONV_W, 3), tile(CONV_W, 5), tile(MEM_W, 6), VM,
                tile(D), tile(D), tile(D), VM, VM] + st_specs + w_specs * 3,
               [SDS((T, D), f32), SDS((P, D), f32), SDS((1, D), f32), SDS((1, HD), f32), SDS((1, HD), f32),
                SDS((1, HD), f32)] + [SDS(w.shape, f32) for w in ws for _ in range(4)],
               [tile(D), pl.BlockSpec((P, D), lambda i: (0, 0)), const((1, D)), const((1, HD)), const((1, HD)),
                const((1, HD))] + [s for s in w_specs for _ in range(4)],
               vmem_mib=56)
    return res[:6], [res[6 + 4 * e:10 + 4 * e] for e in range(n)]


def mem_kv_bwd(dkm, dvm, kv, memn, mem2d, pk, wmkv):
    def body(dkm_ref, dvm_ref, kv_ref, mn_ref, m_ref, pk_ref, w_ref, dw_ref, dg_ref, dkg_ref):
        dkk, dkg = _heads_norm_bwd(dkm_ref[...], kv_ref[:, :MEM_W], _small(pk_ref, "mem_k_norm"))
        dkg_ref[...] = dkg
        dkv = _c(jnp.concatenate([dkk, dvm_ref[...]], axis=1))
        dw_ref[...] = _tn(mn_ref[...], dkv)
        mv = m_ref[...]
        dg_ref[...] = jnp.sum(_nt(dkv, w_ref[...]) * mv * _rstd(mv), axis=0, keepdims=True)

    return _run("mem_kv_bwd", body, (), [dkm, dvm, kv, memn, mem2d, pk, wmkv], [VM] * 7,
                [SDS(wmkv.shape, f32), SDS((1, mem2d.shape[1]), f32), SDS((1, HD), f32)], [VM] * 3, vmem_mib=40)


def _halves_view(g):
    return g.reshape(4, 2, g.shape[0] // 8, g.shape[1])


def kernel(x, mem, norm_mix, w_in, q_norm, k_norm, attn_sinks, conv_w, conv_b, norm_mem, w_mem_kv, mem_q_norm, mem_k_norm, out_norm_attn, out_norm_conv, out_norm_mem, w_out, norm_ffn, w_gate, w_up, w_down, loss_target, m_norm_mix, m_w_in, m_q_norm, m_k_norm, m_attn_sinks, m_conv_w, m_conv_b, m_norm_mem, m_w_mem_kv, m_mem_q_norm, m_mem_k_norm, m_out_norm_attn, m_out_norm_conv, m_out_norm_mem, m_w_out, m_norm_ffn, m_w_gate, m_w_up, m_w_down, v_norm_mix, v_w_in, v_q_norm, v_k_norm, v_attn_sinks, v_conv_w, v_conv_b, v_norm_mem, v_w_mem_kv, v_mem_q_norm, v_mem_k_norm, v_out_norm_attn, v_out_norm_conv, v_out_norm_mem, v_w_out, v_norm_ffn, v_w_gate, v_w_up, v_w_down):
    BL, S, D = x.shape
    T = BL * S
    TM = 256
    TM_BIG = min(512, S)
    _, _, ci = _place()
    cidx = ci.reshape(1).astype(jnp.int32)
    w_small = dict(norm_mix=norm_mix, norm_mem=norm_mem, norm_ffn=norm_ffn, out_norm_attn=out_norm_attn,
                   out_norm_conv=out_norm_conv, out_norm_mem=out_norm_mem, conv_w=conv_w, conv_b=conv_b, q_norm=q_norm,
                   k_norm=k_norm, mem_q_norm=mem_q_norm, mem_k_norm=mem_k_norm, attn_sinks=attn_sinks)
    m_small = dict(norm_mix=m_norm_mix, norm_mem=m_norm_mem, norm_ffn=m_norm_ffn, out_norm_attn=m_out_norm_attn,
                   out_norm_conv=m_out_norm_conv, out_norm_mem=m_out_norm_mem, conv_w=m_conv_w, conv_b=m_conv_b,
                   q_norm=m_q_norm, k_norm=m_k_norm, mem_q_norm=m_mem_q_norm, mem_k_norm=m_mem_k_norm,
                   attn_sinks=m_attn_sinks)
    v_small = dict(norm_mix=v_norm_mix, norm_mem=v_norm_mem, norm_ffn=v_norm_ffn, out_norm_attn=v_out_norm_attn,
                   out_norm_conv=v_out_norm_conv, out_norm_mem=v_out_norm_mem, conv_w=v_conv_w, conv_b=v_conv_b,
                   q_norm=v_q_norm, k_norm=v_k_norm, mem_q_norm=v_mem_q_norm, mem_k_norm=v_mem_k_norm,
                   attn_sinks=v_attn_sinks)
    pk = _pack_small(w_small)

    rowblocks = lambda a, b, c, d, e, f: [a[0].T, b[0].T, c[0].T, d[0], e[0], f[0]]
    w_rb = rowblocks(w_in, w_gate, w_up, w_down, w_out, w_mem_kv)
    m_rb = rowblocks(m_w_in, m_w_gate, m_w_up, m_w_down, m_w_out, m_w_mem_kv)
    v_rb = rowblocks(v_w_in, v_w_gate, v_w_up, v_w_down, v_w_out, v_w_mem_kv)
    (winT_s,) = prep_weights("prep_w_in", w_rb[:1])
    cw_pad = jnp.zeros((8, 128), f32).at[:3, :HD].set(conv_w[0])
    (wgT_s, wuT_s, wd_s, wout_s, wmkv_s), (winT, cw_all) = prep_weights(
        "gather_w_in", w_rb[1:], _together([gather_two_legs([winT_s]), gather_exchange([cw_pad], [False])]))
    conv_w_full = jnp.transpose(cw_all.reshape(4, 8, 128)[:, :3, :HD], (1, 0, 2)).reshape(3, CONV_W)
    conv_w8 = jnp.zeros((8, CONV_W), f32).at[:3].set(conv_w_full)
    sink_rows = jnp.broadcast_to(attn_sinks.reshape(N_Q, 1), (N_Q, 128))

    x2d = x.reshape(T, D)
    mem2d = mem.reshape(-1, D)
    (xn, proj, qkv), near1 = in_proj_fwd(x2d, pk, winT, TM_BIG, gather_near_exchange([wgT_s, wout_s, wmkv_s], relay_early=1))
    (attn_out,), (wgT, wout, wmkv, *near2) = attn_fwd(
        qkv, sink_rows, BL, S, _together([gather_far_exchange(near1, relay_early=2), gather_near_exchange([wuT_s, wd_s])]))
    memn, kv, km, vm = mem_kv_fwd(mem2d, pk, wmkv)
    (conv_out, mem_out, merged, x1, h), (wuT, wd) = mixer_tail_fwd(
        x2d, attn_out, proj, qkv, km, vm, conv_w8, pk, wout, S, TM_BIG, gather_far_exchange(near2, relay_early=2))

    dx1, dx2b, act, d_gate, d_up, loss8, d_norm_ffn = ffn_fwd_bwd(h, x1, loss_target.reshape(T, D), wgT, wuT, wd, pk, TM)
    F = wd.shape[0]
    g_wd = matmul_tn(act, dx2b, "dw_down", F // 2, min(T, 1024))
    g_wgT = matmul_tn(d_gate, h, "dw_gate", F // 2, min(T, 1024))
    g_wuT = matmul_tn(d_up, h, "dw_up", F // 2, min(T, 1024))

    d_attn, d_conv_out, d_mem_out, g_wout, d_gains = out_proj_bwd(dx1, merged, attn_out, conv_out, mem_out, pk, wout, TM_BIG)
    late = [_halves_view(g) for g in (g_wgT, g_wuT, g_wd, g_wout)]
    (dqmn, dkm, dvm, dcb, dcv, d_cw8, d_cbias), late_sib = mem_conv_bwd(
        d_mem_out, mem_out, d_conv_out, proj, qkv, km, vm, conv_w8, pk, S, min(1024, S), halves_exchange(late))
    late_part = add_halves(cidx, late, late_sib, "grad_add_halves_ffn")
    (dqn, dkn, dv, d_sink8), late_stage = attn_bwd(qkv, d_attn, attn_out, sink_rows, BL, S, scatter_exchange(late_part))
    (g_x, g_winT, d_norm_mix, d_qg, d_kg, d_mqg), late_res = in_proj_bwd(
        dqn, dkn, dv, dcb, dcv, dqmn, proj, conv_w8, xn, x2d, dx1, pk, winT, S, TM,
        late_stage, w_rb[1:5], m_rb[1:5], v_rb[1:5])
    g_wmkv, d_norm_mem, d_mkg = mem_kv_bwd(dkm, dvm, kv, memn, mem2d, pk, wmkv)

    tot, tail_stage = tail_reduce(d_norm_mix, d_norm_mem, d_norm_ffn, d_gains, d_cw8, d_cbias, d_qg, d_kg, d_mqg, d_mkg,
                                  d_sink8, loss8, [_halves_view(g) for g in (g_winT, g_wmkv)])
    loss = tot[5, 384]
    tail_res, _ = adamw_big("adamw_tail", tail_stage, [w_rb[0], w_rb[5]], [m_rb[0], m_rb[5]], [v_rb[0], v_rb[5]], 4)
    res = {"w_in": [a.T[None] for a in tail_res[0]], "w_gate": [a.T[None] for a in late_res[0]],
           "w_up": [a.T[None] for a in late_res[1]], "w_down": [a[None] for a in late_res[2]],
           "w_out": [a[None] for a in late_res[3]], "w_mem_kv": [a[None] for a in tail_res[1]]}
    res.update(adamw_small(tot, pk, _pack_small(m_small), _pack_small(v_small), {k: w_small[k].shape for k in SMALL}))

    order = ["norm_mix", "w_in", "q_norm", "k_norm", "attn_sinks", "conv_w", "conv_b", "norm_mem", "w_mem_kv",
             "mem_q_norm", "mem_k_norm", "out_norm_attn", "out_norm_conv", "out_norm_mem", "w_out", "norm_ffn",
             "w_gate", "w_up", "w_down"]
    return (loss, g_x.reshape(BL, S, D), *[res[n][0] for n in order], *[res[n][1] for n in order],
            *[res[n][2] for n in order], *[res[n][3] for n in order])
```

```python
import collections
import functools

import jax
import jax.numpy as jnp
import numpy as np
from jax import lax
from jax.experimental import pallas as pl
from jax.experimental.pallas import tpu as pltpu

f32 = jnp.float32
MXU = jnp.bfloat16
WIRE = jnp.bfloat16
EPS = 1e-6
NEG = -1e30
HD = 64
BLK = 128
N_Q, N_KV, N_MEMH = 8, 2, 4
GQA = N_Q // N_KV
ATT_W, KV_W, CONV_W, MEM_W = 512, 128, 256, 256
VMEM_MIB = 1024 * 1024
ADAM_LR, ADAM_B1, ADAM_B2, ADAM_EPS, ADAM_WD, ADAM_STEP = 0.001, 0.9, 0.999, 1e-08, 0.01, 10

MESH = pl.DeviceIdType.MESH
VM = pl.BlockSpec(memory_space=pltpu.VMEM)
ANY = pl.BlockSpec(memory_space=pl.ANY)
SDS = jax.ShapeDtypeStruct
DMA = pltpu.SemaphoreType.DMA


def _c(v):
    return v.astype(MXU)


def _nn(a, b):
    return lax.dot_general(a, b, (((1,), (0,)), ((), ())), preferred_element_type=f32)


def _nt(a, b):
    return lax.dot_general(a, b, (((1,), (1,)), ((), ())), preferred_element_type=f32)


def _tn(a, b):
    return lax.dot_general(a, b, (((0,), (0,)), ((), ())), preferred_element_type=f32)


def _rstd(v):
    return lax.rsqrt(jnp.mean(v * v, axis=-1, keepdims=True) + EPS)


def _norm_bwd(dy, v, r, g):
    dyg = dy * g
    dv = r * dyg - v * (r * r * r) * jnp.mean(dyg * v, axis=-1, keepdims=True)
    return dv, jnp.sum(dy * v * r, axis=0, keepdims=True)


def _split3(v):
    hi = _c(v)
    r1 = v - hi.astype(f32)
    mid = _c(r1)
    return hi, mid, _c(r1 - mid.astype(f32))


def _rowsum_mxu(v, width):
    ones = jnp.ones((v.shape[1], width), MXU)
    return sum(_nn(a, ones) for a in _split3(v))


def _seg_sums(v):
    r = lax.broadcasted_iota(jnp.int32, (2 * HD, 2 * HD), 0) // HD
    c = lax.broadcasted_iota(jnp.int32, (2 * HD, 2 * HD), 1) // HD
    bd = (r == c).astype(MXU)
    outs = []
    for b in range(v.shape[1] // (2 * HD)):
        outs.append(sum(_nn(a, bd) for a in _split3(v[:, b * 2 * HD:(b + 1) * 2 * HD])))
    return outs[0] if len(outs) == 1 else jnp.concatenate(outs, axis=1)


def _lanes(g, width):
    return jnp.concatenate([g] * (width // HD), axis=1)


def _heads_rstd(v):
    return lax.rsqrt(_seg_sums(v * v) * (1.0 / HD) + EPS)


def _heads_norm_bwd(dy, v, g):
    r = _heads_rstd(v)
    gl = _lanes(g, v.shape[1])
    dyg = dy * gl
    dv = r * dyg - v * (r * r * r) * (_seg_sums(dyg * v) * (1.0 / HD))
    dgl = jnp.sum(dy * v * r, axis=0, keepdims=True)
    return dv, sum(dgl[:, s * HD:(s + 1) * HD] for s in range(v.shape[1] // HD))


def _exp_scores(s, extra=None):
    m = jnp.max(s, axis=-1, keepdims=True)
    if extra is None:
        return jnp.exp(s - m), None
    m = jnp.maximum(m, extra)
    return jnp.exp(s - m), jnp.exp(extra - m)


def _place():
    return lax.axis_index("x"), lax.axis_index("y"), lax.axis_index("c")


SMALL_AT = {"norm_mix": (0, 0, 1024), "norm_mem": (1, 0, 1024), "norm_ffn": (2, 0, 1024),
            "out_norm_attn": (3, 0, ATT_W), "out_norm_conv": (3, ATT_W, CONV_W), "out_norm_mem": (3, ATT_W + CONV_W, MEM_W),
            "conv_b": (4, 3 * CONV_W, CONV_W), "q_norm": (5, 0, HD), "k_norm": (5, HD, HD), "mem_q_norm": (5, 2 * HD, HD),
            "mem_k_norm": (5, 3 * HD, HD), "attn_sinks": (5, 256, N_Q)}
SMALL = ("norm_mix", "norm_mem", "norm_ffn", "out_norm_attn", "out_norm_conv", "out_norm_mem", "conv_w", "conv_b",
         "q_norm", "k_norm", "mem_q_norm", "mem_k_norm", "attn_sinks")


def _small(pk_ref, name):
    r, c0, w = SMALL_AT[name]
    return pk_ref[r:r + 1, c0:c0 + w]


def _pack_small(d):
    z = lambda n: jnp.zeros((1, n), f32)
    row3 = jnp.concatenate([d["out_norm_attn"], d["out_norm_conv"], d["out_norm_mem"]], axis=1)
    row4 = jnp.concatenate([d["conv_w"].reshape(1, 3 * HD), z(3 * CONV_W - 3 * HD), d["conv_b"]], axis=1)
    row5 = jnp.concatenate([d["q_norm"], d["k_norm"], d["mem_q_norm"], d["mem_k_norm"], d["attn_sinks"],
                            z(1024 - 4 * HD - N_Q)], axis=1)
    return jnp.concatenate([d["norm_mix"], d["norm_mem"], d["norm_ffn"], row3, row4, row5, z(1024), z(1024)], axis=0)


def _other_chips(x, y):
    return [(1 - x, y), (x, 1 - y), (1 - x, 1 - y)]


Exchange = collections.namedtuple("Exchange", "ins outs sems start finish relays aliases", defaults=((), {}))


def _together(exchanges):
    def bounds(key):
        at, out = 0, []
        for ex in exchanges:
            out.append((at, at + len(getattr(ex, key))))
            at += len(getattr(ex, key))
        return out

    bi, bo, bs = bounds("ins"), bounds("outs"), bounds("sems")

    def of(i, fn):
        return lambda xa, xo, xs: fn(xa[bi[i][0]:bi[i][1]], xo[bo[i][0]:bo[i][1]], xs[bs[i][0]:bs[i][1]])

    def every(name):
        fns = [of(i, getattr(ex, name)) for i, ex in enumerate(exchanges)]

        def run(xa, xo, xs):
            for fn in fns:
                fn(xa, xo, xs)
        return run

    aliases = {}
    for i, ex in enumerate(exchanges):
        aliases.update({bi[i][0] + a: bo[i][0] + o for a, o in ex.aliases.items()})
    return Exchange([a for ex in exchanges for a in ex.ins], [o for ex in exchanges for o in ex.outs],
                    [s for ex in exchanges for s in ex.sems], every("start"), every("finish"),
                    [(sbe, of(i, fn)) for i, ex in enumerate(exchanges) for sbe, fn in ex.relays], aliases)


def _run(name, body, grid, ins, in_specs, out_shape, out_specs, scratch=(), vmem_mib=32, exchange=None):
    ins, in_specs, out_shape, out_specs, scratch = list(ins), list(in_specs), list(out_shape), list(out_specs), list(scratch)
    ni, no, ns = len(ins), len(out_shape), len(scratch)
    ex = exchange
    if ex is not None:
        nxi, nxo = len(ex.ins), len(ex.outs)

    def call_body(*refs):
        if ex is None:
            body(*refs)
            return
        a, xa = refs[:ni], refs[ni:ni + nxi]
        o, xo = refs[ni + nxi:ni + nxi + no], refs[ni + nxi + no:ni + nxi + no + nxo]
        s, xs = refs[ni + nxi + no + nxo:ni + nxi + no + nxo + ns], refs[ni + nxi + no + nxo + ns:]
        if grid:
            first = functools.reduce(jnp.logical_and, [pl.program_id(d) == 0 for d in range(len(grid))])
            last = functools.reduce(jnp.logical_and, [pl.program_id(d) == grid[d] - 1 for d in range(len(grid))])
            pl.when(first)(lambda: ex.start(xa, xo, xs))
            body(*a, *o, *s)
            nsteps = functools.reduce(lambda p, q: p * q, grid)
            for before_end, fn in ex.relays:
                at = np.unravel_index(max(nsteps - 1 - before_end, 0), grid)
                here = functools.reduce(jnp.logical_and, [pl.program_id(d) == int(at[d]) for d in range(len(grid))])
                pl.when(here)(functools.partial(fn, xa, xo, xs))
            pl.when(last)(lambda: ex.finish(xa, xo, xs))
        else:
            ex.start(xa, xo, xs)
            if body is not None:
                body(*a, *o, *s)
            for _, fn in ex.relays:
                fn(xa, xo, xs)
            ex.finish(xa, xo, xs)

    kw = dict(grid=grid) if grid else {}
    if ex is not None:
        if ex.aliases:
            kw["input_output_aliases"] = {ni + i: no + o for i, o in ex.aliases.items()}
        ins, in_specs = ins + list(ex.ins), in_specs + [ANY] * nxi
        out_shape, out_specs = out_shape + list(ex.outs), out_specs + [ANY] * nxo
        scratch = scratch + list(ex.sems)
    res = pl.pallas_call(
        call_body, name=name, out_shape=out_shape, in_specs=in_specs, out_specs=out_specs, scratch_shapes=scratch,
        compiler_params=pltpu.CompilerParams(dimension_semantics=("arbitrary",) * len(grid) if grid else None,
                                             vmem_limit_bytes=vmem_mib * VMEM_MIB), **kw)(*ins)
    res = list(res)
    return (res[:no], res[no:]) if ex is not None else res


def _remote(src, dst, ssem, rsem, dev):
    return pltpu.make_async_remote_copy(src_ref=src, dst_ref=dst, send_sem=ssem, recv_sem=rsem,
                                        device_id=dev, device_id_type=MESH)


def gather_exchange(shards, split, relay_early=0):
    n = len(shards)

    def rows(ref, e, kk, half=None):
        R = shards[e].shape[0]
        if half is None:
            return ref.at[pl.ds(pl.multiple_of(kk * R, 8), R)]
        return ref.at[pl.ds(pl.multiple_of(kk * R + half * (R // 2), 8), R // 2)]

    def ici(src, dst, sm, e, j, chip_j, x, y, c):
        k = 2 * x + y
        if split[e]:
            s = src[e].at[pl.ds(pl.multiple_of(c * (shards[e].shape[0] // 2), 8), shards[e].shape[0] // 2)]
            return _remote(s, rows(dst[e], e, k, c), sm[0].at[6 * e + j], sm[1].at[6 * e + j], (*chip_j, c))
        return _remote(src[e], rows(dst[e], e, k), sm[0].at[6 * e + j], sm[1].at[6 * e + j], (*chip_j, c))

    def landed(dst, e, chip_j, c):
        kj = 2 * chip_j[0] + chip_j[1]
        return rows(dst[e], e, kj, c) if split[e] else rows(dst[e], e, kj)

    def forward(dst, sm, e, j, chip_j, x, y, c, sender_c):
        kj = 2 * chip_j[0] + chip_j[1]
        r = rows(dst[e], e, kj, sender_c)
        return _remote(r, r, sm[0].at[6 * e + 3 + j], sm[1].at[6 * e + 3 + j], (x, y, 1 - c))

    def local(src, dst, sm, e, x, y):
        return pltpu.make_async_copy(src[e], rows(dst[e], e, 2 * x + y), sm[2].at[e])

    def start(src, dst, sm):
        x, y, c = _place()
        for e in range(n):
            local(src, dst, sm, e, x, y).start()
            for j, chip_j in enumerate(_other_chips(x, y)):
                ici(src, dst, sm, e, j, chip_j, x, y, c).start()

    def relay(src, dst, sm):
        x, y, c = _place()
        for e in range(n):
            for j, chip_j in enumerate(_other_chips(x, y)):
                r = landed(dst, e, chip_j, c)
                _remote(r, r, sm[0].at[6 * e + j], sm[1].at[6 * e + j], (*chip_j, c)).wait_recv()
                if split[e]:
                    forward(dst, sm, e, j, chip_j, x, y, c, c).start()

    def finish(src, dst, sm):
        x, y, c = _place()
        chips = _other_chips(x, y)
        for e in range(n):
            for j, chip_j in enumerate(chips):
                if split[e]:
                    forward(dst, sm, e, j, chip_j, x, y, c, 1 - c).wait_recv()
        for e in range(n):
            for j, chip_j in enumerate(chips):
                ici(src, dst, sm, e, j, chip_j, x, y, c).wait_send()
                if split[e]:
                    forward(dst, sm, e, j, chip_j, x, y, c, c).wait_send()
            local(src, dst, sm, e, x, y).wait()

    outs = [SDS((4 * s.shape[0], s.shape[1]), s.dtype) for s in shards]
    return Exchange(list(shards), outs, [DMA((6 * n,)), DMA((6 * n,)), DMA((n,))], start, finish, [(relay_early, relay)])


def _block_rows(ref, R, kk, half, quarter=None):
    hr = R // 2
    if quarter is None:
        return ref.at[pl.ds(pl.multiple_of(kk * R + half * hr, 8), hr)]
    return ref.at[pl.ds(pl.multiple_of(kk * R + half * hr + quarter * (hr // 2), 8), hr // 2)]


def gather_near_exchange(shards, relay_early=0):
    n = len(shards)
    R = [s.shape[0] for s in shards]

    def ici(src, dst, sm, e, j, chip_j, x, y, c):
        half = src[e].at[pl.ds(pl.multiple_of(c * (R[e] // 2), 8), R[e] // 2)]
        return _remote(half, _block_rows(dst[e], R[e], 2 * x + y, c), sm[0].at[4 * e + j], sm[1].at[4 * e + j], (*chip_j, c))

    def forward(dst, sm, e, j, chip_j, x, y, c, sender_c):
        r = _block_rows(dst[e], R[e], 2 * chip_j[0] + chip_j[1], sender_c)
        return _remote(r, r, sm[0].at[4 * e + 2 + j], sm[1].at[4 * e + 2 + j], (x, y, 1 - c))

    def local(src, dst, sm, e, x, y):
        return pltpu.make_async_copy(src[e], dst[e].at[pl.ds(pl.multiple_of((2 * x + y) * R[e], 8), R[e])], sm[2].at[e])

    def start(src, dst, sm):
        x, y, c = _place()
        for e in range(n):
            local(src, dst, sm, e, x, y).start()
            for j, chip_j in enumerate(_other_chips(x, y)[:2]):
                ici(src, dst, sm, e, j, chip_j, x, y, c).start()

    def relay(src, dst, sm):
        x, y, c = _place()
        for e in range(n):
            for j, chip_j in enumerate(_other_chips(x, y)[:2]):
                r = _block_rows(dst[e], R[e], 2 * chip_j[0] + chip_j[1], c)
                _remote(r, r, sm[0].at[4 * e + j], sm[1].at[4 * e + j], (*chip_j, c)).wait_recv()
                forward(dst, sm, e, j, chip_j, x, y, c, c).start()

    def finish(src, dst, sm):
        x, y, c = _place()
        near = _other_chips(x, y)[:2]
        for e in range(n):
            for j, chip_j in enumerate(near):
                forward(dst, sm, e, j, chip_j, x, y, c, 1 - c).wait_recv()
        for e in range(n):
            for j, chip_j in enumerate(near):
                ici(src, dst, sm, e, j, chip_j, x, y, c).wait_send()
                forward(dst, sm, e, j, chip_j, x, y, c, c).wait_send()
            local(src, dst, sm, e, x, y).wait()

    outs = [SDS((4 * s.shape[0], s.shape[1]), s.dtype) for s in shards]
    return Exchange(list(shards), outs, [DMA((4 * n,)), DMA((4 * n,)), DMA((n,))], start, finish, [(relay_early, relay)])


def gather_far_exchange(bufs, relay_early=0):
    n = len(bufs)
    R = [b.shape[0] // 4 for b in bufs]

    def send(src, dst, sm, e, j, x, y, c):
        to, of = _other_chips(x, y)[j], _other_chips(x, y)[1 - j]
        kk = 2 * of[0] + of[1]
        return _remote(_block_rows(src[e], R[e], kk, c, j), _block_rows(dst[e], R[e], kk, c, j),
                       sm[0].at[4 * e + j], sm[1].at[4 * e + j], (*to, c))

    def landed(dst, e, j, x, y, half):
        return _block_rows(dst[e], R[e], 2 * (1 - x) + (1 - y), half, j)

    def forward(dst, sm, e, j, x, y, c, sender_c):
        r = landed(dst, e, j, x, y, sender_c)
        return _remote(r, r, sm[0].at[4 * e + 2 + j], sm[1].at[4 * e + 2 + j], (x, y, 1 - c))

    def start(src, dst, sm):
        x, y, c = _place()
        for e in range(n):
            for j in range(2):
                send(src, dst, sm, e, j, x, y, c).start()

    def relay(src, dst, sm):
        x, y, c = _place()
        for e in range(n):
            for j in range(2):
                r = landed(dst, e, j, x, y, c)
                _remote(r, r, sm[0].at[4 * e + j], sm[1].at[4 * e + j], (*_other_chips(x, y)[j], c)).wait_recv()
                forward(dst, sm, e, j, x, y, c, c).start()

    def finish(src, dst, sm):
        x, y, c = _place()
        for e in range(n):
            for j in range(2):
                forward(dst, sm, e, j, x, y, c, 1 - c).wait_recv()
        for e in range(n):
            for j in range(2):
                send(src, dst, sm, e, j, x, y, c).wait_send()
                forward(dst, sm, e, j, x, y, c, c).wait_send()

    outs = [SDS(b.shape, b.dtype) for b in bufs]
    return Exchange(list(bufs), outs, [DMA((4 * n,)), DMA((4 * n,))], start, finish, [(relay_early, relay)],
                    {i: i for i in range(n)})


def gather_two_legs(shards):
    near = gather_near_exchange(shards)
    far = gather_far_exchange(near.outs)

    def finish(src, dst, sm):
        near.relays[0][1](src, dst, sm[:3])
        near.finish(src, dst, sm[:3])
        far.start(dst, dst, sm[3:])
        far.relays[0][1](dst, dst, sm[3:])
        far.finish(dst, dst, sm[3:])

    return Exchange(near.ins, near.outs, list(near.sems) + list(far.sems),
                    lambda src, dst, sm: near.start(src, dst, sm[:3]), finish)


def halves_exchange(grads):
    n = len(grads)

    def copy(g, st, sm, e, x, y, c):
        return _remote(g[e].at[:, 1 - c], st[e], sm[0].at[e], sm[1].at[e], (x, y, 1 - c))

    def start(g, st, sm):
        x, y, c = _place()
        for e in range(n):
            copy(g, st, sm, e, x, y, c).start()

    def finish(g, st, sm):
        x, y, c = _place()
        for e in range(n):
            copy(g, st, sm, e, x, y, c).wait()

    outs = [SDS((4,) + a.shape[2:], a.dtype) for a in grads]
    return Exchange(list(grads), outs, [DMA((n,)), DMA((n,))], start, finish)


def scatter_exchange(parts):
    n = len(parts)

    def ici(p, st, sm, e, j, chip_j, x, y, c):
        k, kj = 2 * x + y, 2 * chip_j[0] + chip_j[1]
        return _remote(p[e].at[kj], st[e].at[c, k], sm[0].at[8 * e + j], sm[1].at[8 * e + j], (*chip_j, c))

    def own(p, st, sm, e, x, y, c):
        k = 2 * x + y
        return _remote(p[e].at[k], st[e].at[c, k], sm[0].at[8 * e + 3], sm[1].at[8 * e + 3], (x, y, 1 - c))

    def forward(st, sm, e, j, chip_j, x, y, c, sender_c):
        kj = 2 * chip_j[0] + chip_j[1]
        r = st[e].at[sender_c, kj]
        return _remote(r, r, sm[0].at[8 * e + 4 + j], sm[1].at[8 * e + 4 + j], (x, y, 1 - c))

    def local(p, st, sm, e, x, y, c):
        k = 2 * x + y
        return pltpu.make_async_copy(p[e].at[k], st[e].at[c, k], sm[2].at[e])

    def start(p, st, sm, before_slot=None):
        x, y, c = _place()
        for j, chip_j in enumerate(_other_chips(x, y)):
            if before_slot is not None:
                before_slot(j, 2 * chip_j[0] + chip_j[1])
            for e in range(n):
                ici(p, st, sm, e, j, chip_j, x, y, c).start()
        if before_slot is not None:
            before_slot(3, 2 * x + y)
        for e in range(n):
            local(p, st, sm, e, x, y, c).start()
            own(p, st, sm, e, x, y, c).start()

    def relay(e, p, st, sm):
        x, y, c = _place()
        for j, chip_j in enumerate(_other_chips(x, y)):
            kj = 2 * chip_j[0] + chip_j[1]
            r = st[e].at[c, kj]
            _remote(r, r, sm[0].at[8 * e + j], sm[1].at[8 * e + j], (*chip_j, c)).wait_recv()
            forward(st, sm, e, j, chip_j, x, y, c, c).start()

    def finish(p, st, sm):
        x, y, c = _place()
        k = 2 * x + y
        chips = _other_chips(x, y)
        for e in range(n):
            r = st[e].at[1 - c, k]
            _remote(r, r, sm[0].at[8 * e + 3], sm[1].at[8 * e + 3], (x, y, 1 - c)).wait_recv()
            for j, chip_j in enumerate(chips):
                forward(st, sm, e, j, chip_j, x, y, c, 1 - c).wait_recv()
        for e in range(n):
            own(p, st, sm, e, x, y, c).wait_send()
            for j, chip_j in enumerate(chips):
                ici(p, st, sm, e, j, chip_j, x, y, c).wait_send()
                forward(st, sm, e, j, chip_j, x, y, c, c).wait_send()
            local(p, st, sm, e, x, y, c).wait()

    outs = [SDS((2,) + a.shape, a.dtype) for a in parts]
    return Exchange(list(parts), outs, [DMA((8 * n,)), DMA((8 * n,)), DMA((n,))], start, finish,
                    [(0, functools.partial(relay, e)) for e in range(n)])


def tail_reduce(d_norm_mix, d_norm_mem, d_norm_ffn, d_gains, d_cw8, d_cbias, d_qg, d_kg, d_mqg, d_mkg, d_sink8, loss8, tail):
    n = len(tail)
    scatter = scatter_exchange([SDS((4,) + a.shape[2:], WIRE) for a in tail])

    def half_copy(g, sib, hsem, e, j, slot, x, y, c):
        return _remote(g[e].at[slot, 1 - c], sib[e].at[slot], hsem[0].at[4 * e + j], hsem[1].at[4 * e + j], (x, y, 1 - c))

    def body(nm_ref, nmem_ref, nf_ref, gn_ref, cw_ref, cb_ref, qg_ref, kg_ref, mqg_ref, mkg_ref, sk_ref, ls_ref, *rest):
        g, o_ref, st = rest[:n], rest[n], rest[n + 1:2 * n + 1]
        buf, ssem, rsem = rest[2 * n + 1:2 * n + 4]
        own, sib, part = (rest[2 * n + 4 + i * n:2 * n + 4 + (i + 1) * n] for i in range(3))
        lsem = rest[5 * n + 4]
        hsem, xsem = rest[5 * n + 5:5 * n + 7], rest[5 * n + 7:]
        x, y, c = _place()
        loads = [pltpu.make_async_copy(g[e].at[:, c], own[e], lsem.at[e]) for e in range(n)]
        for ld in loads:
            ld.start()
        for j, slot in enumerate([2 * cx + cy for cx, cy in _other_chips(x, y)] + [2 * x + y]):
            for e in range(n):
                half_copy(g, sib, hsem, e, j, slot, x, y, c).start()
        me = 4 * x + 2 * y + c
        mine = buf.at[me]
        mine[...] = jnp.zeros((8, 1024), f32)
        mine[0:1, :] = nm_ref[...]
        mine[1:2, :] = nmem_ref[...]
        mine[2:3, :] = nf_ref[...]
        mine[3:4, :] = gn_ref[...]
        for j in range(3):
            mine[4:5, pl.ds(j * CONV_W, CONV_W)] = cw_ref[j:j + 1, :]
        mine[4:5, pl.ds(3 * CONV_W, CONV_W)] = cb_ref[...]
        for j, r in enumerate((qg_ref, kg_ref, mqg_ref, mkg_ref)):
            mine[5:6, pl.ds(j * HD, HD)] = r[...]
        mine[5:6, pl.ds(256, 128)] = sk_ref[0:1, :]
        mine[5:6, pl.ds(384, 128)] = ls_ref[0:1, :]

        def peer_of(m):
            return (1 - x if m & 4 else x, 1 - y if m & 2 else y, 1 - c if m & 1 else c)

        for m in range(1, 8):
            _remote(mine, mine, ssem.at[m - 1], rsem.at[m - 1], peer_of(m)).start()
        for ld in loads:
            ld.wait()

        def chip_partial(j, slot):
            for e in range(n):
                half_copy(g, sib, hsem, e, j, slot, x, y, c).wait()
                part[e][slot] = (own[e][slot] + sib[e][slot]).astype(WIRE)

        scatter.start(part, st, xsem, chip_partial)
        for _, hand_on in scatter.relays:
            hand_on(part, st, xsem)
        scatter.finish(part, st, xsem)
        for m in range(1, 8):
            p = peer_of(m)
            got = buf.at[4 * p[0] + 2 * p[1] + p[2]]
            _remote(got, got, ssem.at[m - 1], rsem.at[m - 1], p).wait_recv()
        for m in range(1, 8):
            _remote(mine, mine, ssem.at[m - 1], rsem.at[m - 1], peer_of(m)).wait_send()
        acc = buf[0]
        for d in range(1, 8):
            acc = acc + buf[d]
        o_ref[...] = acc

    ins = [d_norm_mix, d_norm_mem, d_norm_ffn, d_gains, d_cw8, d_cbias, d_qg, d_kg, d_mqg, d_mkg, d_sink8, loss8]
    half_shape = [(4,) + a.shape[2:] for a in tail]
    scratch = ([pltpu.VMEM((8, 8, 1024), f32), DMA((7,)), DMA((7,))]
               + [pltpu.VMEM(s, f32) for s in half_shape] * 2 + [pltpu.VMEM(s, WIRE) for s in half_shape]
               + [DMA((n,)), DMA((4 * n,)), DMA((4 * n,))] + list(scatter.sems))
    res = _run("tail_reduce", body, (), ins + list(tail), [VM] * len(ins) + [ANY] * n,
               [SDS((8, 1024), f32)] + list(scatter.outs), [VM] + [ANY] * n, scratch=scratch, vmem_mib=40)
    return res[0], res[1:]


def add_halves(cidx, grads, stages, name, nch=2):
    n = len(grads)

    def body(c_ref, *refs):
        g, st, o = refs[:n], refs[n:2 * n], refs[2 * n:]
        for e in range(n):
            o[e][...] = (g[e][...] + st[e][...]).astype(WIRE)

    in_specs, out_specs, out_shape = [], [], []
    for a in grads:
        hr, C = a.shape[2], a.shape[3]
        in_specs.append(pl.BlockSpec((None, None, hr // nch, C), lambda s, q, c_ref: (s, c_ref[0], q, 0)))
    for a in stages:
        hr, C = a.shape[1], a.shape[2]
        in_specs.append(pl.BlockSpec((None, hr // nch, C), lambda s, q, c_ref: (s, q, 0)))
        out_specs.append(pl.BlockSpec((None, hr // nch, C), lambda s, q, c_ref: (s, q, 0)))
        out_shape.append(SDS(a.shape, WIRE))
    return pl.pallas_call(
        body, name=name, out_shape=out_shape,
        grid_spec=pltpu.PrefetchScalarGridSpec(num_scalar_prefetch=1, grid=(4, nch), in_specs=in_specs, out_specs=out_specs),
        compiler_params=pltpu.CompilerParams(dimension_semantics=("arbitrary", "arbitrary")),
    )(cidx, *grads, *stages)


def _adamw_math(w, g, m, v):
    m = ADAM_B1 * m + (1.0 - ADAM_B1) * g
    v = ADAM_B2 * v + (1.0 - ADAM_B2) * (g * g)
    m_hat = m / (1.0 - ADAM_B1 ** ADAM_STEP)
    v_hat = v / (1.0 - ADAM_B2 ** ADAM_STEP)
    delta = -ADAM_LR * (m_hat / (jnp.sqrt(v_hat) + ADAM_EPS) + ADAM_WD * w)
    return delta, m, v


def _sum_chips(st):
    return ((st[0].astype(f32) + st[1].astype(f32)) + st[2].astype(f32)) + st[3].astype(f32)


def adamw_big(name, stages, ws, ms, vs, nstep, exchange=None):
    n = len(stages)

    def body(*refs):
        st, w, m, v = refs[:n], refs[n:2 * n], refs[2 * n:3 * n], refs[3 * n:4 * n]
        outs = refs[4 * n:]
        for e in range(n):
            g = jnp.concatenate([_sum_chips(st[e].at[0]), _sum_chips(st[e].at[1])], axis=0)
            d, mm, vv = _adamw_math(w[e][...], g, m[e][...], v[e][...])
            outs[4 * e][...] = g
            outs[4 * e + 1][...] = d
            outs[4 * e + 2][...] = mm
            outs[4 * e + 3][...] = vv

    st_specs, w_specs = [], []
    for e in range(n):
        _, _, hr, C = stages[e].shape
        st_specs.append(pl.BlockSpec((2, 4, hr, C // nstep), lambda i: (0, 0, 0, i)))
        w_specs.append(pl.BlockSpec((2 * hr, C // nstep), lambda i: (0, i)))
    out_specs = [s for s in w_specs for _ in range(4)]
    out_shape = [SDS(w.shape, f32) for w in ws for _ in range(4)]
    res = _run(name, body, (nstep,), list(stages) + list(ws) + list(ms) + list(vs), st_specs + w_specs * 3,
               out_shape, out_specs, vmem_mib=48, exchange=exchange)
    res, sent = res if exchange is not None else (res, None)
    return [res[4 * e:4 * e + 4] for e in range(n)], sent


def adamw_small(tot, pk_w, pk_m, pk_v, shapes):
    def body(tot_ref, w_ref, m_ref, v_ref, *outs):
        x, y, _ = _place()
        chip = 2 * x + y
        taps = []
        for j in range(3):
            mine = tot_ref[4:5, j * CONV_W:j * CONV_W + HD]
            for s in range(1, 4):
                mine = jnp.where(chip == s, tot_ref[4:5, j * CONV_W + s * HD:j * CONV_W + (s + 1) * HD], mine)
            taps.append(mine)
        row4 = jnp.concatenate(taps + [jnp.zeros((1, 3 * CONV_W - 3 * HD), f32), tot_ref[4:5, 3 * CONV_W:]], axis=1)
        tot_v = tot_ref[...]
        row = lax.broadcasted_iota(jnp.int32, tot_v.shape, 0)
        g = jnp.where(row == 4, jnp.broadcast_to(row4, tot_v.shape), tot_v)
        d, mm, vv = _adamw_math(w_ref[...], g, m_ref[...], v_ref[...])
        for i, name in enumerate(SMALL):
            for k, val in enumerate((g, d, mm, vv)):
                if name == "conv_w":
                    outs[4 * i + k][...] = jnp.concatenate([val[4:5, j * HD:(j + 1) * HD] for j in range(3)], axis=0)[None]
                else:
                    r, c0, w = SMALL_AT[name]
                    outs[4 * i + k][...] = val[r:r + 1, c0:c0 + w]

    out_shape = [SDS(shapes[k], f32) for k in SMALL for _ in range(4)]
    res = _run("adamw_small", body, (), [tot, pk_w, pk_m, pk_v], [VM] * 4, out_shape, [VM] * len(out_shape))
    return {k: res[4 * i:4 * i + 4] for i, k in enumerate(SMALL)}


def prep_weights(name, shards, exchange=None):
    n = len(shards)

    def body(*refs):
        for e in range(n):
            refs[n + e][...] = _c(refs[e][...])

    return _run(name, body, (), shards, [VM] * n, [SDS(a.shape, MXU) for a in shards], [VM] * n, vmem_mib=48, exchange=exchange)


def mem_kv_fwd(mem2d, pk, wmkv):
    M, D = mem2d.shape

    def body(m_ref, pk_ref, w_ref, mn_ref, kv_ref, km_ref, vm_ref):
        m = m_ref[...]
        mn = _c(m * _rstd(m) * _small(pk_ref, "norm_mem"))
        mn_ref[...] = mn
        kv = _nn(mn, w_ref[...])
        kv_ref[...] = kv
        kk = kv[:, :MEM_W]
        km_ref[...] = _c(kk * _heads_rstd(kk) * _lanes(_small(pk_ref, "mem_k_norm"), MEM_W))
        vm_ref[...] = _c(kv[:, MEM_W:])

    return _run("mem_kv_fwd", body, (), [mem2d, pk, wmkv], [VM] * 3,
                [SDS((M, D), MXU), SDS((M, 2 * MEM_W), f32), SDS((M, MEM_W), MXU), SDS((M, MEM_W), MXU)], [VM] * 4)


QKV_W = ATT_W + 2 * KV_W + MEM_W


def in_proj_fwd(x2d, pk, winT, tm, exchange):
    T, D = x2d.shape
    P = winT.shape[0]

    def body(x_ref, pk_ref, w_ref, xn_ref, proj_ref, qkv_ref):
        xv = x_ref[...]
        xn = _c(xv * _rstd(xv) * _small(pk_ref, "norm_mix"))
        xn_ref[...] = xn
        proj = _nt(xn, w_ref[...])
        proj_ref[...] = proj
        q, k = proj[:, :ATT_W], proj[:, ATT_W:ATT_W + KV_W]
        qm = proj[:, P - MEM_W:]
        qkv_ref[...] = jnp.concatenate(
            [_c(q * _heads_rstd(q) * _lanes(_small(pk_ref, "q_norm"), ATT_W)),
             _c(k * _heads_rstd(k) * _lanes(_small(pk_ref, "k_norm"), KV_W)),
             _c(proj[:, ATT_W + KV_W:ATT_W + 2 * KV_W]),
             _c(qm * _heads_rstd(qm) * _lanes(_small(pk_ref, "mem_q_norm"), MEM_W))], axis=1)

    return _run("in_proj_fwd", body, (T // tm,), [x2d, pk, winT],
                [pl.BlockSpec((tm, D), lambda i: (i, 0)), VM, VM],
                [SDS((T, D), MXU), SDS((T, P), f32), SDS((T, QKV_W), MXU)],
                [pl.BlockSpec((tm, D), lambda i: (i, 0)), pl.BlockSpec((tm, P), lambda i: (i, 0)),
                 pl.BlockSpec((tm, QKV_W), lambda i: (i, 0))],
                vmem_mib=40, exchange=exchange)


def _swa_bias_table():
    r = np.arange(GQA * BLK)[:, None]
    k = np.arange(2 * BLK)[None, :]
    dist = (r % BLK) + BLK - k
    band = (dist >= 0) & (dist < BLK)
    tab = np.empty((2, N_KV, GQA * BLK, 2 * BLK), np.float32)
    for later in range(2):
        valid = band & ((k >= BLK) | (later == 1))
        for g in range(N_KV):
            slope = 2.0 ** -(g * GQA + r // BLK + 1.0)
            tab[later, g] = np.where(valid, -slope * dist, NEG)
    return jnp.asarray(tab)


def _sink_column(g, sk_ref):
    hrow = lax.broadcasted_iota(jnp.int32, (GQA * BLK, 1), 0) // BLK
    sink = jnp.zeros((GQA * BLK, 1), f32)
    for hh in range(GQA):
        sink = jnp.where(hrow == hh, sk_ref[g * GQA + hh:g * GQA + hh + 1, 0:1], sink)
    return sink


def _stack_heads(v, g):
    return jnp.concatenate([v[:, (g * GQA + hh) * HD:(g * GQA + hh + 1) * HD] for hh in range(GQA)], axis=0)


def attn_fwd(qkv, sink_rows, BL, S, exchange, qb=2):
    NS = S // (qb * BLK)
    T = BL * S

    def body(q_ref, kc_ref, kp_ref, vc_ref, vp_ref, sk_ref, tab_ref, o_ref):
        j = pl.program_id(1)
        kall = jnp.concatenate([kp_ref[...], kc_ref[...]], axis=0)
        vall = jnp.concatenate([vp_ref[...], vc_ref[...]], axis=0)
        ones = jnp.ones((2 * BLK, HD), MXU)
        for b in range(qb):
            q = q_ref[pl.ds(b * BLK, BLK), :]
            k2, v2 = kall[b * BLK:(b + 2) * BLK], vall[b * BLK:(b + 2) * BLK]
            later = jnp.minimum(j, 1) if b == 0 else 1
            for g in range(N_KV):
                kn, vh = k2[:, g * HD:(g + 1) * HD], v2[:, g * HD:(g + 1) * HD]
                s = _nt(_stack_heads(q, g), kn) * (HD ** -0.5) + tab_ref[later, g]
                e, es = _exp_scores(s, _sink_column(g, sk_ref))
                eb = _c(e)
                o = _nn(eb, vh) * (1.0 / (_nn(eb, ones) + es))
                for hh in range(GQA):
                    o_ref[pl.ds(b * BLK, BLK), pl.ds((g * GQA + hh) * HD, HD)] = o[hh * BLK:(hh + 1) * BLK]

    cur = lambda col: (lambda b, j: (b * NS + j, col))
    prev = lambda col: (lambda b, j: (qb * (b * NS + j) - jnp.minimum(j, 1), col))
    return _run("attn_fwd", body, (BL, NS), [qkv, qkv, qkv, qkv, qkv, sink_rows, _swa_bias_table()],
                [pl.BlockSpec((qb * BLK, ATT_W), cur(0)),
                 pl.BlockSpec((qb * BLK, KV_W), cur(4)), pl.BlockSpec((BLK, KV_W), prev(4)),
                 pl.BlockSpec((qb * BLK, KV_W), cur(5)), pl.BlockSpec((BLK, KV_W), prev(5)),
                 pl.BlockSpec((8, 128), lambda b, j: (0, 0)), VM],
                [SDS((T, ATT_W), f32)], [pl.BlockSpec((qb * BLK, ATT_W), cur(0))], exchange=exchange)


def _conv_taps(u, uh):
    row = lax.broadcasted_iota(jnp.int32, u.shape, 0)
    u1 = jnp.where(row == 0, uh[7:8, :], pltpu.roll(u, 1, 0))
    u2 = jnp.where(row == 0, uh[6:7, :], jnp.where(row == 1, uh[7:8, :], pltpu.roll(u, 2, 0)))
    return u1, u2


def _mem_head(qm, km, vm, h):
    qh, kh, vh = (a[:, h * HD:(h + 1) * HD] for a in (qm, km, vm))
    e, _ = _exp_scores(_nt(qh, kh) * (HD ** -0.5))
    return qh, kh, vh, e


def mixer_tail_fwd(x2d, attn_out, proj, qkv, km, vm, conv_w8, pk, wout, S, tm, exchange):
    T, D = x2d.shape
    NM = km.shape[0] // (T // S)

    def body(x_ref, ao_ref, ch_ref, cb_ref, cc_ref, chh_ref, cch_ref, qm_ref, km_ref, vm_ref, cw_ref, pk_ref,
             wout_ref, co_ref, mo_ref, mg_ref, x1_ref, h_ref):
        first = (pl.program_id(0) * tm) % S == 0
        u = cc_ref[...] * ch_ref[...]
        uh = jnp.where(first, 0.0, cch_ref[...] * chh_ref[...])
        u1, u2 = _conv_taps(u, uh)
        conv = cw_ref[0:1, :] * u2 + cw_ref[1:2, :] * u1 + cw_ref[2:3, :] * u + _small(pk_ref, "conv_b")
        conv_out = cb_ref[...] * conv
        co_ref[...] = conv_out
        qm, kmv, vmv = qm_ref[...], km_ref[...], vm_ref[...]
        ones = jnp.ones((NM, HD), MXU)
        for h in range(N_MEMH):
            _, _, vh, e = _mem_head(qm, kmv, vmv, h)
            eb = _c(e)
            mo_ref[:, pl.ds(h * HD, HD)] = _nn(eb, vh) * (1.0 / _nn(eb, ones))
        mem_out = mo_ref[...]
        ao = ao_ref[...]
        merged = _c(jnp.concatenate([ao * _rstd(ao) * _small(pk_ref, "out_norm_attn"),
                                     conv_out * _rstd(conv_out) * _small(pk_ref, "out_norm_conv"),
                                     mem_out * _rstd(mem_out) * _small(pk_ref, "out_norm_mem")], axis=1))
        mg_ref[...] = merged
        x1 = x_ref[...] + _nn(merged, wout_ref[...])
        x1_ref[...] = x1
        h_ref[...] = _c(x1 * _rstd(x1) * _small(pk_ref, "norm_ffn"))

    tile = lambda w, col: pl.BlockSpec((tm, w), lambda i: (i, col))
    halo = lambda col: pl.BlockSpec((8, CONV_W), lambda i: (jnp.maximum(i * (tm // 8) - 1, 0), col))
    seq = pl.BlockSpec((NM, MEM_W), lambda i: ((i * tm) // S, 0))
    small = lambda a: pl.BlockSpec(a.shape, lambda i: (0, 0))
    return _run("mixer_tail_fwd", body, (T // tm,),
                [x2d, attn_out, proj, proj, proj, proj, proj, qkv, km, vm, conv_w8, pk, wout],
                [tile(D, 0), tile(ATT_W, 0), tile(CONV_W, 3), tile(CONV_W, 4), tile(CONV_W, 5), halo(3), halo(5),
                 tile(MEM_W, 3), seq, seq, VM, VM, VM],
                [SDS((T, CONV_W), f32), SDS((T, MEM_W), f32), SDS((T, D), MXU), SDS((T, D), f32), SDS((T, D), MXU)],
                [tile(CONV_W, 0), tile(MEM_W, 0), tile(D, 0), tile(D, 0), tile(D, 0)], vmem_mib=40, exchange=exchange)


def ffn_fwd_bwd(h, x1, tgt, wgT, wuT, wd, pk, tm):
    T, D = x1.shape
    F = wd.shape[0]

    def body(h_ref, x1_ref, t_ref, wg_ref, wu_ref, wd_ref, pk_ref,
             dx1_ref, dx2_ref, act_ref, dg_ref, du_ref, loss_ref, dgf_ref):
        @pl.when(pl.program_id(0) == 0)
        def _():
            loss_ref[...] = jnp.zeros_like(loss_ref)
            dgf_ref[...] = jnp.zeros_like(dgf_ref)

        hv = h_ref[...]
        gate = _nt(hv, wg_ref[...])
        up = _nt(hv, wu_ref[...])
        sg = jax.nn.sigmoid(gate)
        sl = gate * sg
        act = _c(sl * up)
        act_ref[...] = act
        x1v = x1_ref[...]
        diff = (x1v + _nn(act, wd_ref[...])) - t_ref[...]
        loss_ref[...] += 0.5 * jnp.sum(jnp.sum(diff * diff, axis=-1, keepdims=True) / D, axis=0, keepdims=True)
        dx2 = diff / D
        dx2b = _c(dx2)
        dx2_ref[...] = dx2b
        d_act = _nt(dx2b, wd_ref[...])
        d_up = _c(d_act * sl)
        d_gate = _c(d_act * up * (sg * (1.0 + gate * (1.0 - sg))))
        du_ref[...] = d_up
        dg_ref[...] = d_gate
        dh = _nn(d_gate, wg_ref[...]) + _nn(d_up, wu_ref[...])
        dv, dgf = _norm_bwd(dh, x1v, _rstd(x1v), _small(pk_ref, "norm_ffn"))
        dx1_ref[...] = dx2 + dv
        dgf_ref[...] += dgf

    tile = lambda w: pl.BlockSpec((tm, w), lambda i: (i, 0))
    return _run("ffn_fwd_bwd", body, (T // tm,), [h, x1, tgt, wgT, wuT, wd, pk],
                [tile(D), tile(D), tile(D), VM, VM, VM, VM],
                [SDS((T, D), f32), SDS((T, D), MXU), SDS((T, F), MXU), SDS((T, F), MXU), SDS((T, F), MXU),
                 SDS((8, 128), f32), SDS((1, D), f32)],
                [tile(D), tile(D), tile(F), tile(F), tile(F), pl.BlockSpec((8, 128), lambda i: (0, 0)),
                 pl.BlockSpec((1, D), lambda i: (0, 0))], vmem_mib=56)


def matmul_tn(a, b, name, tmo, tk):
    T, M = a.shape
    N = b.shape[1]

    def body(a_ref, b_ref, o_ref):
        @pl.when(pl.program_id(1) == 0)
        def _():
            o_ref[...] = jnp.zeros_like(o_ref)

        o_ref[...] += _tn(a_ref[...], b_ref[...])

    return _run(name, body, (M // tmo, T // tk), [a, b],
                [pl.BlockSpec((tk, tmo), lambda m, k: (k, m)), pl.BlockSpec((tk, N), lambda m, k: (k, 0))],
                [SDS((M, N), f32)], [pl.BlockSpec((tmo, N), lambda m, k: (m, 0))], vmem_mib=48)[0]


def out_proj_bwd(dx1, merged, attn_out, conv_out, mem_out, pk, wout, tm):
    T, D = dx1.shape

    def body(dx1_ref, mg_ref, ao_ref, co_ref, mo_ref, pk_ref, w_ref,
             dao_ref, dco_ref, dmo_ref, dw_ref, dgain_ref):
        @pl.when(pl.program_id(0) == 0)
        def _():
            dw_ref[...] = jnp.zeros_like(dw_ref)
            dgain_ref[...] = jnp.zeros_like(dgain_ref)

        dxb = _c(dx1_ref[...])
        dw_ref[...] += _tn(mg_ref[...], dxb)
        dmg = _nt(dxb, w_ref[...])
        ao, co, mo = ao_ref[...], co_ref[...], mo_ref[...]
        da, ga = _norm_bwd(dmg[:, :ATT_W], ao, _rstd(ao), _small(pk_ref, "out_norm_attn"))
        dc, gc = _norm_bwd(dmg[:, ATT_W:ATT_W + CONV_W], co, _rstd(co), _small(pk_ref, "out_norm_conv"))
        dm, gm = _norm_bwd(dmg[:, ATT_W + CONV_W:], mo, _rstd(mo), _small(pk_ref, "out_norm_mem"))
        dao_ref[...] = da
        dco_ref[...] = dc
        dmo_ref[...] = dm
        dgain_ref[...] += jnp.concatenate([ga, gc, gm], axis=1)

    tile = lambda w: pl.BlockSpec((tm, w), lambda i: (i, 0))
    return _run("out_proj_bwd", body, (T // tm,), [dx1, merged, attn_out, conv_out, mem_out, pk, wout],
                [tile(D), tile(D), tile(ATT_W), tile(CONV_W), tile(MEM_W), VM, VM],
                [SDS((T, ATT_W), f32), SDS((T, CONV_W), f32), SDS((T, MEM_W), f32), SDS((D, D), f32), SDS((1, D), f32)],
                [tile(ATT_W), tile(CONV_W), tile(MEM_W), pl.BlockSpec((D, D), lambda i: (0, 0)),
                 pl.BlockSpec((1, D), lambda i: (0, 0))], vmem_mib=40)


def attn_bwd(qkv, d_attn, attn_out, sink_rows, BL, S, exchange):
    NB = S // BLK
    T = BL * S

    def body(q_ref, kc_ref, kp_ref, vc_ref, vp_ref, do_ref, ao_ref, sk_ref, tab_ref,
             dq_ref, dk_ref, dv_ref, dsk_ref, pend_k, pend_v):
        b, j = pl.program_id(0), pl.program_id(1)

        @pl.when((b == 0) & (j == 0))
        def _():
            dsk_ref[...] = jnp.zeros_like(dsk_ref)

        @pl.when(j == 0)
        def _():
            pend_k[...] = jnp.zeros_like(pend_k)
            pend_v[...] = jnp.zeros_like(pend_v)

        @pl.when(j < NB)
        def _():
            q, do, ao = q_ref[...], do_ref[...], ao_ref[...]
            k2 = jnp.concatenate([kp_ref[...], kc_ref[...]], axis=0)
            v2 = jnp.concatenate([vp_ref[...], vc_ref[...]], axis=0)
            lane = lax.broadcasted_iota(jnp.int32, (8, 128), 1)
            ones_w = jnp.ones((2 * BLK, 2 * BLK), MXU)
            dsk = jnp.zeros((8, 128), f32)
            dks, dvs = [], []
            for g in range(N_KV):
                kn, vh = k2[:, g * HD:(g + 1) * HD], v2[:, g * HD:(g + 1) * HD]
                qs = _stack_heads(q, g)
                s = _nt(qs, kn) * (HD ** -0.5) + tab_ref[g]
                e, es = _exp_scores(s, _sink_column(g, sk_ref))
                eb = _c(e)
                inv_w = 1.0 / (_nn(eb, ones_w) + es)
                inv_n = inv_w[:, :HD]
                dos = _stack_heads(do, g)
                delta = _rowsum_mxu(dos * _stack_heads(ao, g), 2 * BLK)
                dp = _nt(_c(dos), vh)
                ds = _c(e * inv_w * (dp - delta) * (HD ** -0.5))
                t = es * inv_n[:, 0:1] * delta[:, 0:1]
                for hh in range(GQA):
                    dsk = dsk + jnp.where(lane == g * GQA + hh, -jnp.sum(t[hh * BLK:(hh + 1) * BLK]), 0.0)
                dvs.append(_tn(eb, _c(dos * inv_n)))
                dks.append(_tn(ds, qs))
                dqs = _nn(ds, kn)
                for hh in range(GQA):
                    dq_ref[:, pl.ds((g * GQA + hh) * HD, HD)] = dqs[hh * BLK:(hh + 1) * BLK]
            dk2 = jnp.concatenate(dks, axis=1)
            dv2 = jnp.concatenate(dvs, axis=1)
            dk_ref[...] = pend_k[...] + dk2[:BLK]
            dv_ref[...] = pend_v[...] + dv2[:BLK]
            pend_k[...] = dk2[BLK:]
            pend_v[...] = dv2[BLK:]
            dsk_ref[...] += dsk

        @pl.when(j == NB)
        def _():
            dk_ref[...] = pend_k[...]
            dv_ref[...] = pend_v[...]

    cur = lambda col: (lambda b, j: (b * NB + jnp.minimum(j, NB - 1), col))
    prev = lambda col: (lambda b, j: (b * NB + jnp.maximum(j - 1, 0), col))
    small = lambda shape: pl.BlockSpec(shape, lambda b, j: (0, 0))
    return _run("attn_bwd", body, (BL, NB + 1), [qkv, qkv, qkv, qkv, qkv, d_attn, attn_out, sink_rows, _swa_bias_table()],
                [pl.BlockSpec((BLK, ATT_W), cur(0)),
                 pl.BlockSpec((BLK, KV_W), cur(4)), pl.BlockSpec((BLK, KV_W), prev(4)),
                 pl.BlockSpec((BLK, KV_W), cur(5)), pl.BlockSpec((BLK, KV_W), prev(5)),
                 pl.BlockSpec((BLK, ATT_W), cur(0)), pl.BlockSpec((BLK, ATT_W), cur(0)), small((8, 128)),
                 pl.BlockSpec((None, N_KV, GQA * BLK, 2 * BLK), lambda b, j: (jnp.minimum(j, 1), 0, 0, 0))],
                [SDS((T, ATT_W), f32), SDS((T, KV_W), f32), SDS((T, KV_W), f32), SDS((8, 128), f32)],
                [pl.BlockSpec((BLK, ATT_W), cur(0)), pl.BlockSpec((BLK, KV_W), prev(0)),
                 pl.BlockSpec((BLK, KV_W), prev(0)), small((8, 128))],
                scratch=[pltpu.VMEM((BLK, KV_W), f32)] * 2, exchange=exchange)


def mem_conv_bwd(d_mem_out, mem_out, d_conv_out, proj, qkv, km, vm, conv_w8, pk, S, tm, exchange):
    T = d_mem_out.shape[0]
    NM = km.shape[0] // (T // S)

    def body(dmo_ref, mo_ref, dco_ref, ch_ref, cb_ref, cc_ref, chh_ref, cch_ref, qm_ref, km_ref, vm_ref, cw_ref,
             pk_ref, dqm_ref, dkm_ref, dvm_ref, dcb_ref, dcv_ref, dcw_ref, dcbias_ref):
        i = pl.program_id(0)
        first = (i * tm) % S == 0

        @pl.when(i == 0)
        def _():
            dcw_ref[...] = jnp.zeros_like(dcw_ref)
            dcbias_ref[...] = jnp.zeros_like(dcbias_ref)

        @pl.when(first)
        def _():
            dkm_ref[...] = jnp.zeros_like(dkm_ref)
            dvm_ref[...] = jnp.zeros_like(dvm_ref)

        qm, kmv, vmv, dmo, mo = qm_ref[...], km_ref[...], vm_ref[...], dmo_ref[...], mo_ref[...]
        ones_w = jnp.ones((NM, NM), MXU)
        for h in range(N_MEMH):
            qh, kh, vh, e = _mem_head(qm, kmv, vmv, h)
            eb = _c(e)
            doh = dmo[:, h * HD:(h + 1) * HD]
            delta = _rowsum_mxu(doh * mo[:, h * HD:(h + 1) * HD], NM)
            dp = _nt(_c(doh), vh)
            inv_w = 1.0 / _nn(eb, ones_w)
            ds = _c(e * inv_w * (dp - delta) * (HD ** -0.5))
            dvm_ref[:, pl.ds(h * HD, HD)] += _tn(eb, _c(doh * inv_w[:, :HD]))
            dkm_ref[:, pl.ds(h * HD, HD)] += _tn(ds, qh)
            dqm_ref[:, pl.ds(h * HD, HD)] = _nn(ds, kh)

        u = cc_ref[...] * ch_ref[...]
        uh = jnp.where(first, 0.0, cch_ref[...] * chh_ref[...])
        u1, u2 = _conv_taps(u, uh)
        conv = cw_ref[0:1, :] * u2 + cw_ref[1:2, :] * u1 + cw_ref[2:3, :] * u + _small(pk_ref, "conv_b")
        dy = dco_ref[...]
        dcb_ref[...] = dy * conv
        dcv = dy * cb_ref[...]
        dcv_ref[...] = dcv
        dcbias_ref[...] += jnp.sum(dcv, axis=0, keepdims=True)
        dcw_ref[0:1, :] += jnp.sum(dcv * u2, axis=0, keepdims=True)
        dcw_ref[1:2, :] += jnp.sum(dcv * u1, axis=0, keepdims=True)
        dcw_ref[2:3, :] += jnp.sum(dcv * u, axis=0, keepdims=True)

    tile = lambda w, col: pl.BlockSpec((tm, w), lambda i: (i, col))
    halo = lambda col: pl.BlockSpec((8, CONV_W), lambda i: (jnp.maximum(i * (tm // 8) - 1, 0), col))
    seq = pl.BlockSpec((NM, MEM_W), lambda i: ((i * tm) // S, 0))
    const = lambda shape: pl.BlockSpec(shape, lambda i: (0, 0))
    return _run("mem_conv_bwd", body, (T // tm,),
                [d_mem_out, mem_out, d_conv_out, proj, proj, proj, proj, proj, qkv, km, vm, conv_w8, pk],
                [tile(MEM_W, 0), tile(MEM_W, 0), tile(CONV_W, 0), tile(CONV_W, 3), tile(CONV_W, 4), tile(CONV_W, 5),
                 halo(3), halo(5), tile(MEM_W, 3), seq, seq, VM, VM],
                [SDS((T, MEM_W), f32), SDS(km.shape, f32), SDS(km.shape, f32),
                 SDS((T, CONV_W), f32), SDS((T, CONV_W), f32), SDS((8, CONV_W), f32), SDS((1, CONV_W), f32)],
                [tile(MEM_W, 0), seq, seq, tile(CONV_W, 0), tile(CONV_W, 0), const((8, CONV_W)), const((1, CONV_W))],
                vmem_mib=48, exchange=exchange)


def in_proj_bwd(dqn, dkn, dv, dcb, dcv, dqmn, proj, conv_w8, xn, x2d, dx1, pk, winT, S, tm, stages, ws, ms, vs):
    T, D = x2d.shape
    P = winT.shape[0]
    last_blk = T // 8 - 1
    n = len(stages)
    nsteps = T // tm
    tile_w = ws[0].shape[1] // (nsteps // 2)
    turn = [e * 2 // n for e in range(n)]

    def body(dq_ref, dk_ref, dv_ref, dcb_ref, dcv_ref, dcvn_ref, dqm_ref, qa_ref, ka_ref, ch_ref, cc_ref, qma_ref,
             cw_ref, xn_ref, x_ref, dx1_ref, pk_ref, w_ref, *rest):
        st, aw, am, av = (rest[k * n:(k + 1) * n] for k in range(4))
        dx_ref, dw_ref, dg_ref, dqg_ref, dkg_ref, dmqg_ref = rest[4 * n:4 * n + 6]
        aouts = rest[4 * n + 6:]
        i = pl.program_id(0)

        for parity in range(2):
            @pl.when(i % 2 == parity)
            def _(parity=parity):
                for e in range(n):
                    if turn[e] == parity:
                        g = jnp.concatenate([_sum_chips(st[e].at[0]), _sum_chips(st[e].at[1])], axis=0)
                        d, mm, vv = _adamw_math(aw[e][...], g, am[e][...], av[e][...])
                        for k, val in enumerate((g, d, mm, vv)):
                            aouts[4 * e + k][...] = val

        @pl.when(i == 0)
        def _():
            dw_ref[...] = jnp.zeros_like(dw_ref)
            dg_ref[...] = jnp.zeros_like(dg_ref)
            dqg_ref[...] = jnp.zeros_like(dqg_ref)
            dkg_ref[...] = jnp.zeros_like(dkg_ref)
            dmqg_ref[...] = jnp.zeros_like(dmqg_ref)

        dqa, gq = _heads_norm_bwd(dq_ref[...], qa_ref[...], _small(pk_ref, "q_norm"))
        dka, gk = _heads_norm_bwd(dk_ref[...], ka_ref[...], _small(pk_ref, "k_norm"))
        dqma, gmq = _heads_norm_bwd(dqm_ref[...], qma_ref[...], _small(pk_ref, "mem_q_norm"))
        dqg_ref[...] += gq
        dkg_ref[...] += gk
        dmqg_ref[...] += gmq

        last = ((i + 1) * tm) % S == 0
        dcv = dcv_ref[...]
        nxt = jnp.where(last, 0.0, dcvn_ref[...])
        row = lax.broadcasted_iota(jnp.int32, dcv.shape, 0)
        n1 = jnp.where(row == tm - 1, nxt[0:1, :], pltpu.roll(dcv, tm - 1, 0))
        n2 = jnp.where(row == tm - 2, nxt[0:1, :], jnp.where(row == tm - 1, nxt[1:2, :], pltpu.roll(dcv, tm - 2, 0)))
        du = cw_ref[2:3, :] * dcv + cw_ref[1:2, :] * n1 + cw_ref[0:1, :] * n2
        d_proj = jnp.concatenate([_c(dqa), _c(dka), _c(dv_ref[...]), _c(du * cc_ref[...]),
                                  _c(dcb_ref[...]), _c(du * ch_ref[...]), _c(dqma)], axis=1)
        dw_ref[...] += _tn(d_proj, xn_ref[...])
        xv = x_ref[...]
        dv_, dg = _norm_bwd(_nn(d_proj, w_ref[...]), xv, _rstd(xv), _small(pk_ref, "norm_mix"))
        dx_ref[...] = dx1_ref[...] + dv_
        dg_ref[...] += dg

    tile = lambda w, col=0: pl.BlockSpec((tm, w), lambda i: (i, col))
    nhalo = pl.BlockSpec((8, CONV_W), lambda i: (jnp.minimum((i + 1) * (tm // 8), last_blk), 0))
    const = lambda shape: pl.BlockSpec(shape, lambda i: (0, 0))
    st_specs = [pl.BlockSpec((2, 4, s.shape[2], tile_w), lambda i: (0, 0, 0, i // 2)) for s in stages]
    w_specs = [pl.BlockSpec((w.shape[0], tile_w), lambda i: (0, i // 2)) for w in ws]
    res = _run("in_proj_bwd", body, (nsteps,),
               [dqn, dkn, dv, dcb, dcv, dcv, dqmn, proj, proj, proj, proj, proj, conv_w8, xn, x2d, dx1, pk, winT]
               + list(stages) + list(ws) + list(ms) + list(vs),
               [tile(ATT_W), tile(KV_W), tile(KV_W), tile(CONV_W), tile(CONV_W), nhalo, tile(MEM_W),
                tile(ATT_W, 0), tile(KV_W, 4), tile(CONV_W, 3), tile(CONV_W, 5), tile(MEM_W, 6), VM,
                tile(D), tile(D), tile(D), VM, VM] + st_specs + w_specs * 3,
               [SDS((T, D), f32), SDS((P, D), f32), SDS((1, D), f32), SDS((1, HD), f32), SDS((1, HD), f32),
                SDS((1, HD), f32)] + [SDS(w.shape, f32) for w in ws for _ in range(4)],
               [tile(D), pl.BlockSpec((P, D), lambda i: (0, 0)), const((1, D)), const((1, HD)), const((1, HD)),
                const((1, HD))] + [s for s in w_specs for _ in range(4)],
               vmem_mib=56)
    return res[:6], [res[6 + 4 * e:10 + 4 * e] for e in range(n)]


def mem_kv_bwd(dkm, dvm, kv, memn, mem2d, pk, wmkv):
    def body(dkm_ref, dvm_ref, kv_ref, mn_ref, m_ref, pk_ref, w_ref, dw_ref, dg_ref, dkg_ref):
        dkk, dkg = _heads_norm_bwd(dkm_ref[...], kv_ref[:, :MEM_W], _small(pk_ref, "mem_k_norm"))
        dkg_ref[...] = dkg
        dkv = _c(jnp.concatenate([dkk, dvm_ref[...]], axis=1))
        dw_ref[...] = _tn(mn_ref[...], dkv)
        mv = m_ref[...]
        dg_ref[...] = jnp.sum(_nt(dkv, w_ref[...]) * mv * _rstd(mv), axis=0, keepdims=True)

    return _run("mem_kv_bwd", body, (), [dkm, dvm, kv, memn, mem2d, pk, wmkv], [VM] * 7,
                [SDS(wmkv.shape, f32), SDS((1, mem2d.shape[1]), f32), SDS((1, HD), f32)], [VM] * 3, vmem_mib=40)


def _halves_view(g):
    return g.reshape(4, 2, g.shape[0] // 8, g.shape[1])


def kernel(x, mem, norm_mix, w_in, q_norm, k_norm, attn_sinks, conv_w, conv_b, norm_mem, w_mem_kv, mem_q_norm, mem_k_norm, out_norm_attn, out_norm_conv, out_norm_mem, w_out, norm_ffn, w_gate, w_up, w_down, loss_target, m_norm_mix, m_w_in, m_q_norm, m_k_norm, m_attn_sinks, m_conv_w, m_conv_b, m_norm_mem, m_w_mem_kv, m_mem_q_norm, m_mem_k_norm, m_out_norm_attn, m_out_norm_conv, m_out_norm_mem, m_w_out, m_norm_ffn, m_w_gate, m_w_up, m_w_down, v_norm_mix, v_w_in, v_q_norm, v_k_norm, v_attn_sinks, v_conv_w, v_conv_b, v_norm_mem, v_w_mem_kv, v_mem_q_norm, v_mem_k_norm, v_out_norm_attn, v_out_norm_conv, v_out_norm_mem, v_w_out, v_norm_ffn, v_w_gate, v_w_up, v_w_down):
    BL, S, D = x.shape
    T = BL * S
    TM = 256
    TM_BIG = min(512, S)
    _, _, ci = _place()
    cidx = ci.reshape(1).astype(jnp.int32)
    w_small = dict(norm_mix=norm_mix, norm_mem=norm_mem, norm_ffn=norm_ffn, out_norm_attn=out_norm_attn,
                   out_norm_conv=out_norm_conv, out_norm_mem=out_norm_mem, conv_w=conv_w, conv_b=conv_b, q_norm=q_norm,
                   k_norm=k_norm, mem_q_norm=mem_q_norm, mem_k_norm=mem_k_norm, attn_sinks=attn_sinks)
    m_small = dict(norm_mix=m_norm_mix, norm_mem=m_norm_mem, norm_ffn=m_norm_ffn, out_norm_attn=m_out_norm_attn,
                   out_norm_conv=m_out_norm_conv, out_norm_mem=m_out_norm_mem, conv_w=m_conv_w, conv_b=m_conv_b,
                   q_norm=m_q_norm, k_norm=m_k_norm, mem_q_norm=m_mem_q_norm, mem_k_norm=m_mem_k_norm,
                   attn_sinks=m_attn_sinks)
    v_small = dict(norm_mix=v_norm_mix, norm_mem=v_norm_mem, norm_ffn=v_norm_ffn, out_norm_attn=v_out_norm_attn,
                   out_norm_conv=v_out_norm_conv, out_norm_mem=v_out_norm_mem, conv_w=v_conv_w, conv_b=v_conv_b,
                   q_norm=v_q_norm, k_norm=v_k_norm, mem_q_norm=v_mem_q_norm, mem_k_norm=v_mem_k_norm,
                   attn_sinks=v_attn_sinks)
    pk = _pack_small(w_small)

    rowblocks = lambda a, b, c, d, e, f: [a[0].T, b[0].T, c[0].T, d[0], e[0], f[0]]
    w_rb = rowblocks(w_in, w_gate, w_up, w_down, w_out, w_mem_kv)
    m_rb = rowblocks(m_w_in, m_w_gate, m_w_up, m_w_down, m_w_out, m_w_mem_kv)
    v_rb = rowblocks(v_w_in, v_w_gate, v_w_up, v_w_down, v_w_out, v_w_mem_kv)
    (winT_s,) = prep_weights("prep_w_in", w_rb[:1])
    cw_pad = jnp.zeros((8, 128), f32).at[:3, :HD].set(conv_w[0])
    (wgT_s, wuT_s, wd_s, wout_s, wmkv_s), (winT, cw_all) = prep_weights(
        "gather_w_in", w_rb[1:], _together([gather_two_legs([winT_s]), gather_exchange([cw_pad], [False])]))
    conv_w_full = jnp.transpose(cw_all.reshape(4, 8, 128)[:, :3, :HD], (1, 0, 2)).reshape(3, CONV_W)
    conv_w8 = jnp.zeros((8, CONV_W), f32).at[:3].set(conv_w_full)
    sink_rows = jnp.broadcast_to(attn_sinks.reshape(N_Q, 1), (N_Q, 128))

    x2d = x.reshape(T, D)
    mem2d = mem.reshape(-1, D)
    (xn, proj, qkv), near1 = in_proj_fwd(x2d, pk, winT, TM_BIG, gather_near_exchange([wgT_s, wout_s, wmkv_s], relay_early=1))
    (attn_out,), (wgT, wout, wmkv, *near2) = attn_fwd(
        qkv, sink_rows, BL, S, _together([gather_far_exchange(near1, relay_early=2), gather_near_exchange([wuT_s, wd_s], relay_early=2)]))
    memn, kv, km, vm = mem_kv_fwd(mem2d, pk, wmkv)
    (conv_out, mem_out, merged, x1, h), (wuT, wd) = mixer_tail_fwd(
        x2d, attn_out, proj, qkv, km, vm, conv_w8, pk, wout, S, TM_BIG, gather_far_exchange(near2, relay_early=2))

    dx1, dx2b, act, d_gate, d_up, loss8, d_norm_ffn = ffn_fwd_bwd(h, x1, loss_target.reshape(T, D), wgT, wuT, wd, pk, TM)
    F = wd.shape[0]
    g_wd = matmul_tn(act, dx2b, "dw_down", F // 2, min(T, 1024))
    g_wgT = matmul_tn(d_gate, h, "dw_gate", F // 2, min(T, 1024))
    g_wuT = matmul_tn(d_up, h, "dw_up", F // 2, min(T, 1024))

    d_attn, d_conv_out, d_mem_out, g_wout, d_gains = out_proj_bwd(dx1, merged, attn_out, conv_out, mem_out, pk, wout, TM_BIG)
    late = [_halves_view(g) for g in (g_wgT, g_wuT, g_wd, g_wout)]
    (dqmn, dkm, dvm, dcb, dcv, d_cw8, d_cbias), late_sib = mem_conv_bwd(
        d_mem_out, mem_out, d_conv_out, proj, qkv, km, vm, conv_w8, pk, S, min(1024, S), halves_exchange(late))
    late_part = add_halves(cidx, late, late_sib, "grad_add_halves_ffn")
    (dqn, dkn, dv, d_sink8), late_stage = attn_bwd(qkv, d_attn, attn_out, sink_rows, BL, S, scatter_exchange(late_part))
    (g_x, g_winT, d_norm_mix, d_qg, d_kg, d_mqg), late_res = in_proj_bwd(
        dqn, dkn, dv, dcb, dcv, dqmn, proj, conv_w8, xn, x2d, dx1, pk, winT, S, TM,
        late_stage, w_rb[1:5], m_rb[1:5], v_rb[1:5])
    g_wmkv, d_norm_mem, d_mkg = mem_kv_bwd(dkm, dvm, kv, memn, mem2d, pk, wmkv)

    tot, tail_stage = tail_reduce(d_norm_mix, d_norm_mem, d_norm_ffn, d_gains, d_cw8, d_cbias, d_qg, d_kg, d_mqg, d_mkg,
                                  d_sink8, loss8, [_halves_view(g) for g in (g_winT, g_wmkv)])
    loss = tot[5, 384]
    tail_res, _ = adamw_big("adamw_tail", tail_stage, [w_rb[0], w_rb[5]], [m_rb[0], m_rb[5]], [v_rb[0], v_rb[5]], 4)
    res = {"w_in": [a.T[None] for a in tail_res[0]], "w_gate": [a.T[None] for a in late_res[0]],
           "w_up": [a.T[None] for a in late_res[1]], "w_down": [a[None] for a in late_res[2]],
           "w_out": [a[None] for a in late_res[3]], "w_mem_kv": [a[None] for a in tail_res[1]]}
    res.update(adamw_small(tot, pk, _pack_small(m_small), _pack_small(v_small), {k: w_small[k].shape for k in SMALL}))

    order = ["norm_mix", "w_in", "q_norm", "k_norm", "attn_sinks", "conv_w", "conv_b", "norm_mem", "w_mem_kv",
             "mem_q_norm", "mem_k_norm", "out_norm_attn", "out_norm_conv", "out_norm_mem", "w_out", "norm_ffn",
             "w_gate", "w_up", "w_down"]
    return (loss, g_x.reshape(BL, S, D), *[res[n][0] for n in order], *[res[n][1] for n in order],
            *[res[n][2] for n in order], *[res[n][3] for n in order])
```

```python
import collections
import functools

import jax
import jax.numpy as jnp
import numpy as np
from jax import lax
from jax.experimental import pallas as pl
from jax.experimental.pallas import tpu as pltpu

f32 = jnp.float32
MXU = jnp.bfloat16
WIRE = jnp.bfloat16
EPS = 1e-6
NEG = -1e30
HD = 64
BLK = 128
N_Q, N_KV, N_MEMH = 8, 2, 4
GQA = N_Q // N_KV
ATT_W, KV_W, CONV_W, MEM_W = 512, 128, 256, 256
VMEM_MIB = 1024 * 1024
ADAM_LR, ADAM_B1, ADAM_B2, ADAM_EPS, ADAM_WD, ADAM_STEP = 0.001, 0.9, 0.999, 1e-08, 0.01, 10

MESH = pl.DeviceIdType.MESH
VM = pl.BlockSpec(memory_space=pltpu.VMEM)
ANY = pl.BlockSpec(memory_space=pl.ANY)
SDS = jax.ShapeDtypeStruct
DMA = pltpu.SemaphoreType.DMA


def _c(v):
    return v.astype(MXU)


def _nn(a, b):
    return lax.dot_general(a, b, (((1,), (0,)), ((), ())), preferred_element_type=f32)


def _nt(a, b):
    return lax.dot_general(a, b, (((1,), (1,)), ((), ())), preferred_element_type=f32)


def _tn(a, b):
    return lax.dot_general(a, b, (((0,), (0,)), ((), ())), preferred_element_type=f32)


def _rstd(v):
    return lax.rsqrt(jnp.mean(v * v, axis=-1, keepdims=True) + EPS)


def _norm_bwd(dy, v, r, g):
    dyg = dy * g
    dv = r * dyg - v * (r * r * r) * jnp.mean(dyg * v, axis=-1, keepdims=True)
    return dv, jnp.sum(dy * v * r, axis=0, keepdims=True)


def _split3(v):
    hi = _c(v)
    r1 = v - hi.astype(f32)
    mid = _c(r1)
    return hi, mid, _c(r1 - mid.astype(f32))


def _rowsum_mxu(v, width):
    ones = jnp.ones((v.shape[1], width), MXU)
    return sum(_nn(a, ones) for a in _split3(v))


def _seg_sums(v):
    r = lax.broadcasted_iota(jnp.int32, (2 * HD, 2 * HD), 0) // HD
    c = lax.broadcasted_iota(jnp.int32, (2 * HD, 2 * HD), 1) // HD
    bd = (r == c).astype(MXU)
    outs = []
    for b in range(v.shape[1] // (2 * HD)):
        outs.append(sum(_nn(a, bd) for a in _split3(v[:, b * 2 * HD:(b + 1) * 2 * HD])))
    return outs[0] if len(outs) == 1 else jnp.concatenate(outs, axis=1)


def _lanes(g, width):
    return jnp.concatenate([g] * (width // HD), axis=1)


def _heads_rstd(v):
    return lax.rsqrt(_seg_sums(v * v) * (1.0 / HD) + EPS)


def _heads_norm_bwd(dy, v, g):
    r = _heads_rstd(v)
    gl = _lanes(g, v.shape[1])
    dyg = dy * gl
    dv = r * dyg - v * (r * r * r) * (_seg_sums(dyg * v) * (1.0 / HD))
    dgl = jnp.sum(dy * v * r, axis=0, keepdims=True)
    return dv, sum(dgl[:, s * HD:(s + 1) * HD] for s in range(v.shape[1] // HD))


def _exp_scores(s, extra=None):
    m = jnp.max(s, axis=-1, keepdims=True)
    if extra is None:
        return jnp.exp(s - m), None
    m = jnp.maximum(m, extra)
    return jnp.exp(s - m), jnp.exp(extra - m)


def _place():
    return lax.axis_index("x"), lax.axis_index("y"), lax.axis_index("c")


SMALL_AT = {"norm_mix": (0, 0, 1024), "norm_mem": (1, 0, 1024), "norm_ffn": (2, 0, 1024),
            "out_norm_attn": (3, 0, ATT_W), "out_norm_conv": (3, ATT_W, CONV_W), "out_norm_mem": (3, ATT_W + CONV_W, MEM_W),
            "conv_b": (4, 3 * CONV_W, CONV_W), "q_norm": (5, 0, HD), "k_norm": (5, HD, HD), "mem_q_norm": (5, 2 * HD, HD),
            "mem_k_norm": (5, 3 * HD, HD), "attn_sinks": (5, 256, N_Q)}
SMALL = ("norm_mix", "norm_mem", "norm_ffn", "out_norm_attn", "out_norm_conv", "out_norm_mem", "conv_w", "conv_b",
         "q_norm", "k_norm", "mem_q_norm", "mem_k_norm", "attn_sinks")


def _small(pk_ref, name):
    r, c0, w = SMALL_AT[name]
    return pk_ref[r:r + 1, c0:c0 + w]


def _pack_small(d):
    z = lambda n: jnp.zeros((1, n), f32)
    row3 = jnp.concatenate([d["out_norm_attn"], d["out_norm_conv"], d["out_norm_mem"]], axis=1)
    row4 = jnp.concatenate([d["conv_w"].reshape(1, 3 * HD), z(3 * CONV_W - 3 * HD), d["conv_b"]], axis=1)
    row5 = jnp.concatenate([d["q_norm"], d["k_norm"], d["mem_q_norm"], d["mem_k_norm"], d["attn_sinks"],
                            z(1024 - 4 * HD - N_Q)], axis=1)
    return jnp.concatenate([d["norm_mix"], d["norm_mem"], d["norm_ffn"], row3, row4, row5, z(1024), z(1024)], axis=0)


def _other_chips(x, y):
    return [(1 - x, y), (x, 1 - y), (1 - x, 1 - y)]


Exchange = collections.namedtuple("Exchange", "ins outs sems start finish relays aliases", defaults=((), {}))


def _together(exchanges):
    def bounds(key):
        at, out = 0, []
        for ex in exchanges:
            out.append((at, at + len(getattr(ex, key))))
            at += len(getattr(ex, key))
        return out

    bi, bo, bs = bounds("ins"), bounds("outs"), bounds("sems")

    def of(i, fn):
        return lambda xa, xo, xs: fn(xa[bi[i][0]:bi[i][1]], xo[bo[i][0]:bo[i][1]], xs[bs[i][0]:bs[i][1]])

    def every(name):
        fns = [of(i, getattr(ex, name)) for i, ex in enumerate(exchanges)]

        def run(xa, xo, xs):
            for fn in fns:
                fn(xa, xo, xs)
        return run

    aliases = {}
    for i, ex in enumerate(exchanges):
        aliases.update({bi[i][0] + a: bo[i][0] + o for a, o in ex.aliases.items()})
    return Exchange([a for ex in exchanges for a in ex.ins], [o for ex in exchanges for o in ex.outs],
                    [s for ex in exchanges for s in ex.sems], every("start"), every("finish"),
                    [(sbe, of(i, fn)) for i, ex in enumerate(exchanges) for sbe, fn in ex.relays], aliases)


def _run(name, body, grid, ins, in_specs, out_shape, out_specs, scratch=(), vmem_mib=32, exchange=None):
    ins, in_specs, out_shape, out_specs, scratch = list(ins), list(in_specs), list(out_shape), list(out_specs), list(scratch)
    ni, no, ns = len(ins), len(out_shape), len(scratch)
    ex = exchange
    if ex is not None:
        nxi, nxo = len(ex.ins), len(ex.outs)

    def call_body(*refs):
        if ex is None:
            body(*refs)
            return
        a, xa = refs[:ni], refs[ni:ni + nxi]
        o, xo = refs[ni + nxi:ni + nxi + no], refs[ni + nxi + no:ni + nxi + no + nxo]
        s, xs = refs[ni + nxi + no + nxo:ni + nxi + no + nxo + ns], refs[ni + nxi + no + nxo + ns:]
        if grid:
            first = functools.reduce(jnp.logical_and, [pl.program_id(d) == 0 for d in range(len(grid))])
            last = functools.reduce(jnp.logical_and, [pl.program_id(d) == grid[d] - 1 for d in range(len(grid))])
            pl.when(first)(lambda: ex.start(xa, xo, xs))
            body(*a, *o, *s)
            nsteps = functools.reduce(lambda p, q: p * q, grid)
            for before_end, fn in ex.relays:
                at = np.unravel_index(max(nsteps - 1 - before_end, 0), grid)
                here = functools.reduce(jnp.logical_and, [pl.program_id(d) == int(at[d]) for d in range(len(grid))])
                pl.when(here)(functools.partial(fn, xa, xo, xs))
            pl.when(last)(lambda: ex.finish(xa, xo, xs))
        else:
            ex.start(xa, xo, xs)
            if body is not None:
                body(*a, *o, *s)
            for _, fn in ex.relays:
                fn(xa, xo, xs)
            ex.finish(xa, xo, xs)

    kw = dict(grid=grid) if grid else {}
    if ex is not None:
        if ex.aliases:
            kw["input_output_aliases"] = {ni + i: no + o for i, o in ex.aliases.items()}
        ins, in_specs = ins + list(ex.ins), in_specs + [ANY] * nxi
        out_shape, out_specs = out_shape + list(ex.outs), out_specs + [ANY] * nxo
        scratch = scratch + list(ex.sems)
    res = pl.pallas_call(
        call_body, name=name, out_shape=out_shape, in_specs=in_specs, out_specs=out_specs, scratch_shapes=scratch,
        compiler_params=pltpu.CompilerParams(dimension_semantics=("arbitrary",) * len(grid) if grid else None,
                                             vmem_limit_bytes=vmem_mib * VMEM_MIB), **kw)(*ins)
    res = list(res)
    return (res[:no], res[no:]) if ex is not None else res


def _remote(src, dst, ssem, rsem, dev):
    return pltpu.make_async_remote_copy(src_ref=src, dst_ref=dst, send_sem=ssem, recv_sem=rsem,
                                        device_id=dev, device_id_type=MESH)


def gather_exchange(shards, split, relay_early=0):
    n = len(shards)

    def rows(ref, e, kk, half=None):
        R = shards[e].shape[0]
        if half is None:
            return ref.at[pl.ds(pl.multiple_of(kk * R, 8), R)]
        return ref.at[pl.ds(pl.multiple_of(kk * R + half * (R // 2), 8), R // 2)]

    def ici(src, dst, sm, e, j, chip_j, x, y, c):
        k = 2 * x + y
        if split[e]:
            s = src[e].at[pl.ds(pl.multiple_of(c * (shards[e].shape[0] // 2), 8), shards[e].shape[0] // 2)]
            return _remote(s, rows(dst[e], e, k, c), sm[0].at[6 * e + j], sm[1].at[6 * e + j], (*chip_j, c))
        return _remote(src[e], rows(dst[e], e, k), sm[0].at[6 * e + j], sm[1].at[6 * e + j], (*chip_j, c))

    def landed(dst, e, chip_j, c):
        kj = 2 * chip_j[0] + chip_j[1]
        return rows(dst[e], e, kj, c) if split[e] else rows(dst[e], e, kj)

    def forward(dst, sm, e, j, chip_j, x, y, c, sender_c):
        kj = 2 * chip_j[0] + chip_j[1]
        r = rows(dst[e], e, kj, sender_c)
        return _remote(r, r, sm[0].at[6 * e + 3 + j], sm[1].at[6 * e + 3 + j], (x, y, 1 - c))

    def local(src, dst, sm, e, x, y):
        return pltpu.make_async_copy(src[e], rows(dst[e], e, 2 * x + y), sm[2].at[e])

    def start(src, dst, sm):
        x, y, c = _place()
        for e in range(n):
            local(src, dst, sm, e, x, y).start()
            for j, chip_j in enumerate(_other_chips(x, y)):
                ici(src, dst, sm, e, j, chip_j, x, y, c).start()

    def relay(src, dst, sm):
        x, y, c = _place()
        for e in range(n):
            for j, chip_j in enumerate(_other_chips(x, y)):
                r = landed(dst, e, chip_j, c)
                _remote(r, r, sm[0].at[6 * e + j], sm[1].at[6 * e + j], (*chip_j, c)).wait_recv()
                if split[e]:
                    forward(dst, sm, e, j, chip_j, x, y, c, c).start()

    def finish(src, dst, sm):
        x, y, c = _place()
        chips = _other_chips(x, y)
        for e in range(n):
            for j, chip_j in enumerate(chips):
                if split[e]:
                    forward(dst, sm, e, j, chip_j, x, y, c, 1 - c).wait_recv()
        for e in range(n):
            for j, chip_j in enumerate(chips):
                ici(src, dst, sm, e, j, chip_j, x, y, c).wait_send()
                if split[e]:
                    forward(dst, sm, e, j, chip_j, x, y, c, c).wait_send()
            local(src, dst, sm, e, x, y).wait()

    outs = [SDS((4 * s.shape[0], s.shape[1]), s.dtype) for s in shards]
    return Exchange(list(shards), outs, [DMA((6 * n,)), DMA((6 * n,)), DMA((n,))], start, finish, [(relay_early, relay)])


def _block_rows(ref, R, kk, half, quarter=None):
    hr = R // 2
    if quarter is None:
        return ref.at[pl.ds(pl.multiple_of(kk * R + half * hr, 8), hr)]
    return ref.at[pl.ds(pl.multiple_of(kk * R + half * hr + quarter * (hr // 2), 8), hr // 2)]


def gather_near_exchange(shards, relay_early=0):
    n = len(shards)
    R = [s.shape[0] for s in shards]

    def ici(src, dst, sm, e, j, chip_j, x, y, c):
        half = src[e].at[pl.ds(pl.multiple_of(c * (R[e] // 2), 8), R[e] // 2)]
        return _remote(half, _block_rows(dst[e], R[e], 2 * x + y, c), sm[0].at[4 * e + j], sm[1].at[4 * e + j], (*chip_j, c))

    def forward(dst, sm, e, j, chip_j, x, y, c, sender_c):
        r = _block_rows(dst[e], R[e], 2 * chip_j[0] + chip_j[1], sender_c)
        return _remote(r, r, sm[0].at[4 * e + 2 + j], sm[1].at[4 * e + 2 + j], (x, y, 1 - c))

    def local(src, dst, sm, e, x, y):
        return pltpu.make_async_copy(src[e], dst[e].at[pl.ds(pl.multiple_of((2 * x + y) * R[e], 8), R[e])], sm[2].at[e])

    def start(src, dst, sm):
        x, y, c = _place()
        for e in range(n):
            local(src, dst, sm, e, x, y).start()
            for j, chip_j in enumerate(_other_chips(x, y)[:2]):
                ici(src, dst, sm, e, j, chip_j, x, y, c).start()

    def relay(src, dst, sm):
        x, y, c = _place()
        for e in range(n):
            for j, chip_j in enumerate(_other_chips(x, y)[:2]):
                r = _block_rows(dst[e], R[e], 2 * chip_j[0] + chip_j[1], c)
                _remote(r, r, sm[0].at[4 * e + j], sm[1].at[4 * e + j], (*chip_j, c)).wait_recv()
                forward(dst, sm, e, j, chip_j, x, y, c, c).start()

    def finish(src, dst, sm):
        x, y, c = _place()
        near = _other_chips(x, y)[:2]
        for e in range(n):
            for j, chip_j in enumerate(near):
                forward(dst, sm, e, j, chip_j, x, y, c, 1 - c).wait_recv()
        for e in range(n):
            for j, chip_j in enumerate(near):
                ici(src, dst, sm, e, j, chip_j, x, y, c).wait_send()
                forward(dst, sm, e, j, chip_j, x, y, c, c).wait_send()
            local(src, dst, sm, e, x, y).wait()

    outs = [SDS((4 * s.shape[0], s.shape[1]), s.dtype) for s in shards]
    return Exchange(list(shards), outs, [DMA((4 * n,)), DMA((4 * n,)), DMA((n,))], start, finish, [(relay_early, relay)])


def gather_far_exchange(bufs, relay_early=0):
    n = len(bufs)
    R = [b.shape[0] // 4 for b in bufs]

    def send(src, dst, sm, e, j, x, y, c):
        to, of = _other_chips(x, y)[j], _other_chips(x, y)[1 - j]
        kk = 2 * of[0] + of[1]
        return _remote(_block_rows(src[e], R[e], kk, c, j), _block_rows(dst[e], R[e], kk, c, j),
                       sm[0].at[4 * e + j], sm[1].at[4 * e + j], (*to, c))

    def landed(dst, e, j, x, y, half):
        return _block_rows(dst[e], R[e], 2 * (1 - x) + (1 - y), half, j)

    def forward(dst, sm, e, j, x, y, c, sender_c):
        r = landed(dst, e, j, x, y, sender_c)
        return _remote(r, r, sm[0].at[4 * e + 2 + j], sm[1].at[4 * e + 2 + j], (x, y, 1 - c))

    def start(src, dst, sm):
        x, y, c = _place()
        for e in range(n):
            for j in range(2):
                send(src, dst, sm, e, j, x, y, c).start()

    def relay(src, dst, sm):
        x, y, c = _place()
        for e in range(n):
            for j in range(2):
                r = landed(dst, e, j, x, y, c)
                _remote(r, r, sm[0].at[4 * e + j], sm[1].at[4 * e + j], (*_other_chips(x, y)[j], c)).wait_recv()
                forward(dst, sm, e, j, x, y, c, c).start()

    def finish(src, dst, sm):
        x, y, c = _place()
        for e in range(n):
            for j in range(2):
                forward(dst, sm, e, j, x, y, c, 1 - c).wait_recv()
        for e in range(n):
            for j in range(2):
                send(src, dst, sm, e, j, x, y, c).wait_send()
                forward(dst, sm, e, j, x, y, c, c).wait_send()

    outs = [SDS(b.shape, b.dtype) for b in bufs]
    return Exchange(list(bufs), outs, [DMA((4 * n,)), DMA((4 * n,))], start, finish, [(relay_early, relay)],
                    {i: i for i in range(n)})


def gather_two_legs(shards):
    near = gather_near_exchange(shards)
    far = gather_far_exchange(near.outs)

    def finish(src, dst, sm):
        near.relays[0][1](src, dst, sm[:3])
        near.finish(src, dst, sm[:3])
        far.start(dst, dst, sm[3:])
        far.relays[0][1](dst, dst, sm[3:])
        far.finish(dst, dst, sm[3:])

    return Exchange(near.ins, near.outs, list(near.sems) + list(far.sems),
                    lambda src, dst, sm: near.start(src, dst, sm[:3]), finish)


def halves_exchange(grads):
    n = len(grads)

    def copy(g, st, sm, e, x, y, c):
        return _remote(g[e].at[:, 1 - c], st[e], sm[0].at[e], sm[1].at[e], (x, y, 1 - c))

    def start(g, st, sm):
        x, y, c = _place()
        for e in range(n):
            copy(g, st, sm, e, x, y, c).start()

    def finish(g, st, sm):
        x, y, c = _place()
        for e in range(n):
            copy(g, st, sm, e, x, y, c).wait()

    outs = [SDS((4,) + a.shape[2:], a.dtype) for a in grads]
    return Exchange(list(grads), outs, [DMA((n,)), DMA((n,))], start, finish)


def scatter_exchange(parts, relay_before_end=None):
    n = len(parts)
    by_entry = relay_before_end is not None
    relay_before_end = relay_before_end or [0] * n

    def ici(p, st, sm, e, j, chip_j, x, y, c):
        k, kj = 2 * x + y, 2 * chip_j[0] + chip_j[1]
        return _remote(p[e].at[kj], st[e].at[c, k], sm[0].at[8 * e + j], sm[1].at[8 * e + j], (*chip_j, c))

    def own(p, st, sm, e, x, y, c):
        k = 2 * x + y
        return _remote(p[e].at[k], st[e].at[c, k], sm[0].at[8 * e + 3], sm[1].at[8 * e + 3], (x, y, 1 - c))

    def forward(st, sm, e, j, chip_j, x, y, c, sender_c):
        kj = 2 * chip_j[0] + chip_j[1]
        r = st[e].at[sender_c, kj]
        return _remote(r, r, sm[0].at[8 * e + 4 + j], sm[1].at[8 * e + 4 + j], (x, y, 1 - c))

    def local(p, st, sm, e, x, y, c):
        k = 2 * x + y
        return pltpu.make_async_copy(p[e].at[k], st[e].at[c, k], sm[2].at[e])

    def start(p, st, sm, before_slot=None):
        x, y, c = _place()
        if by_entry:
            for e in range(n):
                for j, chip_j in enumerate(_other_chips(x, y)):
                    ici(p, st, sm, e, j, chip_j, x, y, c).start()
                local(p, st, sm, e, x, y, c).start()
                own(p, st, sm, e, x, y, c).start()
            return
        for j, chip_j in enumerate(_other_chips(x, y)):
            if before_slot is not None:
                before_slot(j, 2 * chip_j[0] + chip_j[1])
            for e in range(n):
                ici(p, st, sm, e, j, chip_j, x, y, c).start()
        if before_slot is not None:
            before_slot(3, 2 * x + y)
        for e in range(n):
            local(p, st, sm, e, x, y, c).start()
            own(p, st, sm, e, x, y, c).start()

    def relay(e, p, st, sm):
        x, y, c = _place()
        for j, chip_j in enumerate(_other_chips(x, y)):
            kj = 2 * chip_j[0] + chip_j[1]
            r = st[e].at[c, kj]
            _remote(r, r, sm[0].at[8 * e + j], sm[1].at[8 * e + j], (*chip_j, c)).wait_recv()
            forward(st, sm, e, j, chip_j, x, y, c, c).start()

    def finish(p, st, sm):
        x, y, c = _place()
        k = 2 * x + y
        chips = _other_chips(x, y)
        for e in range(n):
            r = st[e].at[1 - c, k]
            _remote(r, r, sm[0].at[8 * e + 3], sm[1].at[8 * e + 3], (x, y, 1 - c)).wait_recv()
            for j, chip_j in enumerate(chips):
                forward(st, sm, e, j, chip_j, x, y, c, 1 - c).wait_recv()
        for e in range(n):
            own(p, st, sm, e, x, y, c).wait_send()
            for j, chip_j in enumerate(chips):
                ici(p, st, sm, e, j, chip_j, x, y, c).wait_send()
                forward(st, sm, e, j, chip_j, x, y, c, c).wait_send()
            local(p, st, sm, e, x, y, c).wait()

    outs = [SDS((2,) + a.shape, a.dtype) for a in parts]
    return Exchange(list(parts), outs, [DMA((8 * n,)), DMA((8 * n,)), DMA((n,))], start, finish,
                    [(relay_before_end[e], functools.partial(relay, e)) for e in range(n)])


def tail_reduce(d_norm_mix, d_norm_mem, d_norm_ffn, d_gains, d_cw8, d_cbias, d_qg, d_kg, d_mqg, d_mkg, d_sink8, loss8, tail):
    n = len(tail)
    scatter = scatter_exchange([SDS((4,) + a.shape[2:], WIRE) for a in tail])

    def half_copy(g, sib, hsem, e, j, slot, x, y, c):
        return _remote(g[e].at[slot, 1 - c], sib[e].at[slot], hsem[0].at[4 * e + j], hsem[1].at[4 * e + j], (x, y, 1 - c))

    def body(nm_ref, nmem_ref, nf_ref, gn_ref, cw_ref, cb_ref, qg_ref, kg_ref, mqg_ref, mkg_ref, sk_ref, ls_ref, *rest):
        g, o_ref, st = rest[:n], rest[n], rest[n + 1:2 * n + 1]
        buf, ssem, rsem = rest[2 * n + 1:2 * n + 4]
        own, sib, part = (rest[2 * n + 4 + i * n:2 * n + 4 + (i + 1) * n] for i in range(3))
        lsem = rest[5 * n + 4]
        hsem, xsem = rest[5 * n + 5:5 * n + 7], rest[5 * n + 7:]
        x, y, c = _place()
        loads = [pltpu.make_async_copy(g[e].at[:, c], own[e], lsem.at[e]) for e in range(n)]
        for ld in loads:
            ld.start()
        for j, slot in enumerate([2 * cx + cy for cx, cy in _other_chips(x, y)] + [2 * x + y]):
            for e in range(n):
                half_copy(g, sib, hsem, e, j, slot, x, y, c).start()
        me = 4 * x + 2 * y + c
        mine = buf.at[me]
        mine[...] = jnp.zeros((8, 1024), f32)
        mine[0:1, :] = nm_ref[...]
        mine[1:2, :] = nmem_ref[...]
        mine[2:3, :] = nf_ref[...]
        mine[3:4, :] = gn_ref[...]
        for j in range(3):
            mine[4:5, pl.ds(j * CONV_W, CONV_W)] = cw_ref[j:j + 1, :]
        mine[4:5, pl.ds(3 * CONV_W, CONV_W)] = cb_ref[...]
        for j, r in enumerate((qg_ref, kg_ref, mqg_ref, mkg_ref)):
            mine[5:6, pl.ds(j * HD, HD)] = r[...]
        mine[5:6, pl.ds(256, 128)] = sk_ref[0:1, :]
        mine[5:6, pl.ds(384, 128)] = ls_ref[0:1, :]

        def peer_of(m):
            return (1 - x if m & 4 else x, 1 - y if m & 2 else y, 1 - c if m & 1 else c)

        for m in range(1, 8):
            _remote(mine, mine, ssem.at[m - 1], rsem.at[m - 1], peer_of(m)).start()
        for ld in loads:
            ld.wait()

        def chip_partial(j, slot):
            for e in range(n):
                half_copy(g, sib, hsem, e, j, slot, x, y, c).wait()
                part[e][slot] = (own[e][slot] + sib[e][slot]).astype(WIRE)

        scatter.start(part, st, xsem, chip_partial)
        for _, hand_on in scatter.relays:
            hand_on(part, st, xsem)
        scatter.finish(part, st, xsem)
        for m in range(1, 8):
            p = peer_of(m)
            got = buf.at[4 * p[0] + 2 * p[1] + p[2]]
            _remote(got, got, ssem.at[m - 1], rsem.at[m - 1], p).wait_recv()
        for m in range(1, 8):
            _remote(mine, mine, ssem.at[m - 1], rsem.at[m - 1], peer_of(m)).wait_send()
        acc = buf[0]
        for d in range(1, 8):
            acc = acc + buf[d]
        o_ref[...] = acc

    ins = [d_norm_mix, d_norm_mem, d_norm_ffn, d_gains, d_cw8, d_cbias, d_qg, d_kg, d_mqg, d_mkg, d_sink8, loss8]
    half_shape = [(4,) + a.shape[2:] for a in tail]
    scratch = ([pltpu.VMEM((8, 8, 1024), f32), DMA((7,)), DMA((7,))]
               + [pltpu.VMEM(s, f32) for s in half_shape] * 2 + [pltpu.VMEM(s, WIRE) for s in half_shape]
               + [DMA((n,)), DMA((4 * n,)), DMA((4 * n,))] + list(scatter.sems))
    res = _run("tail_reduce", body, (), ins + list(tail), [VM] * len(ins) + [ANY] * n,
               [SDS((8, 1024), f32)] + list(scatter.outs), [VM] + [ANY] * n, scratch=scratch, vmem_mib=40)
    return res[0], res[1:]


def add_halves(cidx, grads, stages, name, nch=2):
    n = len(grads)

    def body(c_ref, *refs):
        g, st, o = refs[:n], refs[n:2 * n], refs[2 * n:]
        for e in range(n):
            o[e][...] = (g[e][...] + st[e][...]).astype(WIRE)

    in_specs, out_specs, out_shape = [], [], []
    for a in grads:
        hr, C = a.shape[2], a.shape[3]
        in_specs.append(pl.BlockSpec((None, None, hr // nch, C), lambda s, q, c_ref: (s, c_ref[0], q, 0)))
    for a in stages:
        hr, C = a.shape[1], a.shape[2]
        in_specs.append(pl.BlockSpec((None, hr // nch, C), lambda s, q, c_ref: (s, q, 0)))
        out_specs.append(pl.BlockSpec((None, hr // nch, C), lambda s, q, c_ref: (s, q, 0)))
        out_shape.append(SDS(a.shape, WIRE))
    return pl.pallas_call(
        body, name=name, out_shape=out_shape,
        grid_spec=pltpu.PrefetchScalarGridSpec(num_scalar_prefetch=1, grid=(4, nch), in_specs=in_specs, out_specs=out_specs),
        compiler_params=pltpu.CompilerParams(dimension_semantics=("arbitrary", "arbitrary")),
    )(cidx, *grads, *stages)


def _adamw_math(w, g, m, v):
    m = ADAM_B1 * m + (1.0 - ADAM_B1) * g
    v = ADAM_B2 * v + (1.0 - ADAM_B2) * (g * g)
    m_hat = m / (1.0 - ADAM_B1 ** ADAM_STEP)
    v_hat = v / (1.0 - ADAM_B2 ** ADAM_STEP)
    delta = -ADAM_LR * (m_hat / (jnp.sqrt(v_hat) + ADAM_EPS) + ADAM_WD * w)
    return delta, m, v


def _sum_chips(st):
    return ((st[0].astype(f32) + st[1].astype(f32)) + st[2].astype(f32)) + st[3].astype(f32)


def adamw_big(name, stages, ws, ms, vs, nstep, exchange=None):
    n = len(stages)

    def body(*refs):
        st, w, m, v = refs[:n], refs[n:2 * n], refs[2 * n:3 * n], refs[3 * n:4 * n]
        outs = refs[4 * n:]
        for e in range(n):
            g = jnp.concatenate([_sum_chips(st[e].at[0]), _sum_chips(st[e].at[1])], axis=0)
            d, mm, vv = _adamw_math(w[e][...], g, m[e][...], v[e][...])
            outs[4 * e][...] = g
            outs[4 * e + 1][...] = d
            outs[4 * e + 2][...] = mm
            outs[4 * e + 3][...] = vv

    st_specs, w_specs = [], []
    for e in range(n):
        _, _, hr, C = stages[e].shape
        st_specs.append(pl.BlockSpec((2, 4, hr, C // nstep), lambda i: (0, 0, 0, i)))
        w_specs.append(pl.BlockSpec((2 * hr, C // nstep), lambda i: (0, i)))
    out_specs = [s for s in w_specs for _ in range(4)]
    out_shape = [SDS(w.shape, f32) for w in ws for _ in range(4)]
    res = _run(name, body, (nstep,), list(stages) + list(ws) + list(ms) + list(vs), st_specs + w_specs * 3,
               out_shape, out_specs, vmem_mib=48, exchange=exchange)
    res, sent = res if exchange is not None else (res, None)
    return [res[4 * e:4 * e + 4] for e in range(n)], sent


def adamw_small(tot, pk_w, pk_m, pk_v, shapes):
    def body(tot_ref, w_ref, m_ref, v_ref, *outs):
        x, y, _ = _place()
        chip = 2 * x + y
        taps = []
        for j in range(3):
            mine = tot_ref[4:5, j * CONV_W:j * CONV_W + HD]
            for s in range(1, 4):
                mine = jnp.where(chip == s, tot_ref[4:5, j * CONV_W + s * HD:j * CONV_W + (s + 1) * HD], mine)
            taps.append(mine)
        row4 = jnp.concatenate(taps + [jnp.zeros((1, 3 * CONV_W - 3 * HD), f32), tot_ref[4:5, 3 * CONV_W:]], axis=1)
        tot_v = tot_ref[...]
        row = lax.broadcasted_iota(jnp.int32, tot_v.shape, 0)
        g = jnp.where(row == 4, jnp.broadcast_to(row4, tot_v.shape), tot_v)
        d, mm, vv = _adamw_math(w_ref[...], g, m_ref[...], v_ref[...])
        for i, name in enumerate(SMALL):
            for k, val in enumerate((g, d, mm, vv)):
                if name == "conv_w":
                    outs[4 * i + k][...] = jnp.concatenate([val[4:5, j * HD:(j + 1) * HD] for j in range(3)], axis=0)[None]
                else:
                    r, c0, w = SMALL_AT[name]
                    outs[4 * i + k][...] = val[r:r + 1, c0:c0 + w]

    out_shape = [SDS(shapes[k], f32) for k in SMALL for _ in range(4)]
    res = _run("adamw_small", body, (), [tot, pk_w, pk_m, pk_v], [VM] * 4, out_shape, [VM] * len(out_shape))
    return {k: res[4 * i:4 * i + 4] for i, k in enumerate(SMALL)}


def prep_weights(name, shards, exchange=None):
    n = len(shards)

    def body(*refs):
        for e in range(n):
            refs[n + e][...] = _c(refs[e][...])

    return _run(name, body, (), shards, [VM] * n, [SDS(a.shape, MXU) for a in shards], [VM] * n, vmem_mib=48, exchange=exchange)


def mem_kv_fwd(mem2d, pk, wmkv):
    M, D = mem2d.shape

    def body(m_ref, pk_ref, w_ref, mn_ref, kv_ref, km_ref, vm_ref):
        m = m_ref[...]
        mn = _c(m * _rstd(m) * _small(pk_ref, "norm_mem"))
        mn_ref[...] = mn
        kv = _nn(mn, w_ref[...])
        kv_ref[...] = kv
        kk = kv[:, :MEM_W]
        km_ref[...] = _c(kk * _heads_rstd(kk) * _lanes(_small(pk_ref, "mem_k_norm"), MEM_W))
        vm_ref[...] = _c(kv[:, MEM_W:])

    return _run("mem_kv_fwd", body, (), [mem2d, pk, wmkv], [VM] * 3,
                [SDS((M, D), MXU), SDS((M, 2 * MEM_W), f32), SDS((M, MEM_W), MXU), SDS((M, MEM_W), MXU)], [VM] * 4)


QKV_W = ATT_W + 2 * KV_W + MEM_W


def in_proj_fwd(x2d, pk, winT, tm, exchange):
    T, D = x2d.shape
    P = winT.shape[0]

    def body(x_ref, pk_ref, w_ref, xn_ref, proj_ref, qkv_ref):
        xv = x_ref[...]
        xn = _c(xv * _rstd(xv) * _small(pk_ref, "norm_mix"))
        xn_ref[...] = xn
        proj = _nt(xn, w_ref[...])
        proj_ref[...] = proj
        q, k = proj[:, :ATT_W], proj[:, ATT_W:ATT_W + KV_W]
        qm = proj[:, P - MEM_W:]
        qkv_ref[...] = jnp.concatenate(
            [_c(q * _heads_rstd(q) * _lanes(_small(pk_ref, "q_norm"), ATT_W)),
             _c(k * _heads_rstd(k) * _lanes(_small(pk_ref, "k_norm"), KV_W)),
             _c(proj[:, ATT_W + KV_W:ATT_W + 2 * KV_W]),
             _c(qm * _heads_rstd(qm) * _lanes(_small(pk_ref, "mem_q_norm"), MEM_W))], axis=1)

    return _run("in_proj_fwd", body, (T // tm,), [x2d, pk, winT],
                [pl.BlockSpec((tm, D), lambda i: (i, 0)), VM, VM],
                [SDS((T, D), MXU), SDS((T, P), f32), SDS((T, QKV_W), MXU)],
                [pl.BlockSpec((tm, D), lambda i: (i, 0)), pl.BlockSpec((tm, P), lambda i: (i, 0)),
                 pl.BlockSpec((tm, QKV_W), lambda i: (i, 0))],
                vmem_mib=40, exchange=exchange)


def _swa_bias_table():
    r = np.arange(GQA * BLK)[:, None]
    k = np.arange(2 * BLK)[None, :]
    dist = (r % BLK) + BLK - k
    band = (dist >= 0) & (dist < BLK)
    tab = np.empty((2, N_KV, GQA * BLK, 2 * BLK), np.float32)
    for later in range(2):
        valid = band & ((k >= BLK) | (later == 1))
        for g in range(N_KV):
            slope = 2.0 ** -(g * GQA + r // BLK + 1.0)
            tab[later, g] = np.where(valid, -slope * dist, NEG)
    return jnp.asarray(tab)


def _sink_column(g, sk_ref):
    hrow = lax.broadcasted_iota(jnp.int32, (GQA * BLK, 1), 0) // BLK
    sink = jnp.zeros((GQA * BLK, 1), f32)
    for hh in range(GQA):
        sink = jnp.where(hrow == hh, sk_ref[g * GQA + hh:g * GQA + hh + 1, 0:1], sink)
    return sink


def _stack_heads(v, g):
    return jnp.concatenate([v[:, (g * GQA + hh) * HD:(g * GQA + hh + 1) * HD] for hh in range(GQA)], axis=0)


def attn_fwd(qkv, sink_rows, BL, S, exchange, qb=2):
    NS = S // (qb * BLK)
    T = BL * S

    def body(q_ref, kc_ref, kp_ref, vc_ref, vp_ref, sk_ref, tab_ref, o_ref):
        j = pl.program_id(1)
        kall = jnp.concatenate([kp_ref[...], kc_ref[...]], axis=0)
        vall = jnp.concatenate([vp_ref[...], vc_ref[...]], axis=0)
        ones = jnp.ones((2 * BLK, HD), MXU)
        for b in range(qb):
            q = q_ref[pl.ds(b * BLK, BLK), :]
            k2, v2 = kall[b * BLK:(b + 2) * BLK], vall[b * BLK:(b + 2) * BLK]
            later = jnp.minimum(j, 1) if b == 0 else 1
            for g in range(N_KV):
                kn, vh = k2[:, g * HD:(g + 1) * HD], v2[:, g * HD:(g + 1) * HD]
                s = _nt(_stack_heads(q, g), kn) * (HD ** -0.5) + tab_ref[later, g]
                e, es = _exp_scores(s, _sink_column(g, sk_ref))
                eb = _c(e)
                o = _nn(eb, vh) * (1.0 / (_nn(eb, ones) + es))
                for hh in range(GQA):
                    o_ref[pl.ds(b * BLK, BLK), pl.ds((g * GQA + hh) * HD, HD)] = o[hh * BLK:(hh + 1) * BLK]

    cur = lambda col: (lambda b, j: (b * NS + j, col))
    prev = lambda col: (lambda b, j: (qb * (b * NS + j) - jnp.minimum(j, 1), col))
    return _run("attn_fwd", body, (BL, NS), [qkv, qkv, qkv, qkv, qkv, sink_rows, _swa_bias_table()],
                [pl.BlockSpec((qb * BLK, ATT_W), cur(0)),
                 pl.BlockSpec((qb * BLK, KV_W), cur(4)), pl.BlockSpec((BLK, KV_W), prev(4)),
                 pl.BlockSpec((qb * BLK, KV_W), cur(5)), pl.BlockSpec((BLK, KV_W), prev(5)),
                 pl.BlockSpec((8, 128), lambda b, j: (0, 0)), VM],
                [SDS((T, ATT_W), f32)], [pl.BlockSpec((qb * BLK, ATT_W), cur(0))], exchange=exchange)


def _conv_taps(u, uh):
    row = lax.broadcasted_iota(jnp.int32, u.shape, 0)
    u1 = jnp.where(row == 0, uh[7:8, :], pltpu.roll(u, 1, 0))
    u2 = jnp.where(row == 0, uh[6:7, :], jnp.where(row == 1, uh[7:8, :], pltpu.roll(u, 2, 0)))
    return u1, u2


def _mem_head(qm, km, vm, h):
    qh, kh, vh = (a[:, h * HD:(h + 1) * HD] for a in (qm, km, vm))
    e, _ = _exp_scores(_nt(qh, kh) * (HD ** -0.5))
    return qh, kh, vh, e


def mixer_tail_fwd(x2d, attn_out, proj, qkv, km, vm, conv_w8, pk, wout, S, tm, exchange):
    T, D = x2d.shape
    NM = km.shape[0] // (T // S)

    def body(x_ref, ao_ref, ch_ref, cb_ref, cc_ref, chh_ref, cch_ref, qm_ref, km_ref, vm_ref, cw_ref, pk_ref,
             wout_ref, co_ref, mo_ref, mg_ref, x1_ref, h_ref):
        first = (pl.program_id(0) * tm) % S == 0
        u = cc_ref[...] * ch_ref[...]
        uh = jnp.where(first, 0.0, cch_ref[...] * chh_ref[...])
        u1, u2 = _conv_taps(u, uh)
        conv = cw_ref[0:1, :] * u2 + cw_ref[1:2, :] * u1 + cw_ref[2:3, :] * u + _small(pk_ref, "conv_b")
        conv_out = cb_ref[...] * conv
        co_ref[...] = conv_out
        qm, kmv, vmv = qm_ref[...], km_ref[...], vm_ref[...]
        ones = jnp.ones((NM, HD), MXU)
        for h in range(N_MEMH):
            _, _, vh, e = _mem_head(qm, kmv, vmv, h)
            eb = _c(e)
            mo_ref[:, pl.ds(h * HD, HD)] = _nn(eb, vh) * (1.0 / _nn(eb, ones))
        mem_out = mo_ref[...]
        ao = ao_ref[...]
        merged = _c(jnp.concatenate([ao * _rstd(ao) * _small(pk_ref, "out_norm_attn"),
                                     conv_out * _rstd(conv_out) * _small(pk_ref, "out_norm_conv"),
                                     mem_out * _rstd(mem_out) * _small(pk_ref, "out_norm_mem")], axis=1))
        mg_ref[...] = merged
        x1 = x_ref[...] + _nn(merged, wout_ref[...])
        x1_ref[...] = x1
        h_ref[...] = _c(x1 * _rstd(x1) * _small(pk_ref, "norm_ffn"))

    tile = lambda w, col: pl.BlockSpec((tm, w), lambda i: (i, col))
    halo = lambda col: pl.BlockSpec((8, CONV_W), lambda i: (jnp.maximum(i * (tm // 8) - 1, 0), col))
    seq = pl.BlockSpec((NM, MEM_W), lambda i: ((i * tm) // S, 0))
    small = lambda a: pl.BlockSpec(a.shape, lambda i: (0, 0))
    return _run("mixer_tail_fwd", body, (T // tm,),
                [x2d, attn_out, proj, proj, proj, proj, proj, qkv, km, vm, conv_w8, pk, wout],
                [tile(D, 0), tile(ATT_W, 0), tile(CONV_W, 3), tile(CONV_W, 4), tile(CONV_W, 5), halo(3), halo(5),
                 tile(MEM_W, 3), seq, seq, VM, VM, VM],
                [SDS((T, CONV_W), f32), SDS((T, MEM_W), f32), SDS((T, D), MXU), SDS((T, D), f32), SDS((T, D), MXU)],
                [tile(CONV_W, 0), tile(MEM_W, 0), tile(D, 0), tile(D, 0), tile(D, 0)], vmem_mib=40, exchange=exchange)


def ffn_fwd_bwd(h, x1, tgt, wgT, wuT, wd, pk, tm):
    T, D = x1.shape
    F = wd.shape[0]

    def body(h_ref, x1_ref, t_ref, wg_ref, wu_ref, wd_ref, pk_ref,
             dx1_ref, dx2_ref, act_ref, dg_ref, du_ref, loss_ref, dgf_ref):
        @pl.when(pl.program_id(0) == 0)
        def _():
            loss_ref[...] = jnp.zeros_like(loss_ref)
            dgf_ref[...] = jnp.zeros_like(dgf_ref)

        hv = h_ref[...]
        gate = _nt(hv, wg_ref[...])
        up = _nt(hv, wu_ref[...])
        sg = jax.nn.sigmoid(gate)
        sl = gate * sg
        act = _c(sl * up)
        act_ref[...] = act
        x1v = x1_ref[...]
        diff = (x1v + _nn(act, wd_ref[...])) - t_ref[...]
        loss_ref[...] += 0.5 * jnp.sum(jnp.sum(diff * diff, axis=-1, keepdims=True) / D, axis=0, keepdims=True)
        dx2 = diff / D
        dx2b = _c(dx2)
        dx2_ref[...] = dx2b
        d_act = _nt(dx2b, wd_ref[...])
        d_up = _c(d_act * sl)
        d_gate = _c(d_act * up * (sg * (1.0 + gate * (1.0 - sg))))
        du_ref[...] = d_up
        dg_ref[...] = d_gate
        dh = _nn(d_gate, wg_ref[...]) + _nn(d_up, wu_ref[...])
        dv, dgf = _norm_bwd(dh, x1v, _rstd(x1v), _small(pk_ref, "norm_ffn"))
        dx1_ref[...] = dx2 + dv
        dgf_ref[...] += dgf

    tile = lambda w: pl.BlockSpec((tm, w), lambda i: (i, 0))
    return _run("ffn_fwd_bwd", body, (T // tm,), [h, x1, tgt, wgT, wuT, wd, pk],
                [tile(D), tile(D), tile(D), VM, VM, VM, VM],
                [SDS((T, D), f32), SDS((T, D), MXU), SDS((T, F), MXU), SDS((T, F), MXU), SDS((T, F), MXU),
                 SDS((8, 128), f32), SDS((1, D), f32)],
                [tile(D), tile(D), tile(F), tile(F), tile(F), pl.BlockSpec((8, 128), lambda i: (0, 0)),
                 pl.BlockSpec((1, D), lambda i: (0, 0))], vmem_mib=56)


def matmul_tn(a, b, name, tmo, tk):
    T, M = a.shape
    N = b.shape[1]

    def body(a_ref, b_ref, o_ref):
        @pl.when(pl.program_id(1) == 0)
        def _():
            o_ref[...] = jnp.zeros_like(o_ref)

        o_ref[...] += _tn(a_ref[...], b_ref[...])

    return _run(name, body, (M // tmo, T // tk), [a, b],
                [pl.BlockSpec((tk, tmo), lambda m, k: (k, m)), pl.BlockSpec((tk, N), lambda m, k: (k, 0))],
                [SDS((M, N), f32)], [pl.BlockSpec((tmo, N), lambda m, k: (m, 0))], vmem_mib=48)[0]


def out_proj_bwd(dx1, merged, attn_out, conv_out, mem_out, pk, wout, tm):
    T, D = dx1.shape

    def body(dx1_ref, mg_ref, ao_ref, co_ref, mo_ref, pk_ref, w_ref,
             dao_ref, dco_ref, dmo_ref, dw_ref, dgain_ref):
        @pl.when(pl.program_id(0) == 0)
        def _():
            dw_ref[...] = jnp.zeros_like(dw_ref)
            dgain_ref[...] = jnp.zeros_like(dgain_ref)

        dxb = _c(dx1_ref[...])
        dw_ref[...] += _tn(mg_ref[...], dxb)
        dmg = _nt(dxb, w_ref[...])
        ao, co, mo = ao_ref[...], co_ref[...], mo_ref[...]
        da, ga = _norm_bwd(dmg[:, :ATT_W], ao, _rstd(ao), _small(pk_ref, "out_norm_attn"))
        dc, gc = _norm_bwd(dmg[:, ATT_W:ATT_W + CONV_W], co, _rstd(co), _small(pk_ref, "out_norm_conv"))
        dm, gm = _norm_bwd(dmg[:, ATT_W + CONV_W:], mo, _rstd(mo), _small(pk_ref, "out_norm_mem"))
        dao_ref[...] = da
        dco_ref[...] = dc
        dmo_ref[...] = dm
        dgain_ref[...] += jnp.concatenate([ga, gc, gm], axis=1)

    tile = lambda w: pl.BlockSpec((tm, w), lambda i: (i, 0))
    return _run("out_proj_bwd", body, (T // tm,), [dx1, merged, attn_out, conv_out, mem_out, pk, wout],
                [tile(D), tile(D), tile(ATT_W), tile(CONV_W), tile(MEM_W), VM, VM],
                [SDS((T, ATT_W), f32), SDS((T, CONV_W), f32), SDS((T, MEM_W), f32), SDS((D, D), f32), SDS((1, D), f32)],
                [tile(ATT_W), tile(CONV_W), tile(MEM_W), pl.BlockSpec((D, D), lambda i: (0, 0)),
                 pl.BlockSpec((1, D), lambda i: (0, 0))], vmem_mib=40)


def attn_bwd(qkv, d_attn, attn_out, sink_rows, BL, S, exchange):
    NB = S // BLK
    T = BL * S

    def body(q_ref, kc_ref, kp_ref, vc_ref, vp_ref, do_ref, ao_ref, sk_ref, tab_ref,
             dq_ref, dk_ref, dv_ref, dsk_ref, pend_k, pend_v):
        b, j = pl.program_id(0), pl.program_id(1)

        @pl.when((b == 0) & (j == 0))
        def _():
            dsk_ref[...] = jnp.zeros_like(dsk_ref)

        @pl.when(j == 0)
        def _():
            pend_k[...] = jnp.zeros_like(pend_k)
            pend_v[...] = jnp.zeros_like(pend_v)

        @pl.when(j < NB)
        def _():
            q, do, ao = q_ref[...], do_ref[...], ao_ref[...]
            k2 = jnp.concatenate([kp_ref[...], kc_ref[...]], axis=0)
            v2 = jnp.concatenate([vp_ref[...], vc_ref[...]], axis=0)
            lane = lax.broadcasted_iota(jnp.int32, (8, 128), 1)
            ones_w = jnp.ones((2 * BLK, 2 * BLK), MXU)
            dsk = jnp.zeros((8, 128), f32)
            dks, dvs = [], []
            for g in range(N_KV):
                kn, vh = k2[:, g * HD:(g + 1) * HD], v2[:, g * HD:(g + 1) * HD]
                qs = _stack_heads(q, g)
                s = _nt(qs, kn) * (HD ** -0.5) + tab_ref[g]
                e, es = _exp_scores(s, _sink_column(g, sk_ref))
                eb = _c(e)
                inv_w = 1.0 / (_nn(eb, ones_w) + es)
                inv_n = inv_w[:, :HD]
                dos = _stack_heads(do, g)
                delta = _rowsum_mxu(dos * _stack_heads(ao, g), 2 * BLK)
                dp = _nt(_c(dos), vh)
                ds = _c(e * inv_w * (dp - delta) * (HD ** -0.5))
                t = es * inv_n[:, 0:1] * delta[:, 0:1]
                for hh in range(GQA):
                    dsk = dsk + jnp.where(lane == g * GQA + hh, -jnp.sum(t[hh * BLK:(hh + 1) * BLK]), 0.0)
                dvs.append(_tn(eb, _c(dos * inv_n)))
                dks.append(_tn(ds, qs))
                dqs = _nn(ds, kn)
                for hh in range(GQA):
                    dq_ref[:, pl.ds((g * GQA + hh) * HD, HD)] = dqs[hh * BLK:(hh + 1) * BLK]
            dk2 = jnp.concatenate(dks, axis=1)
            dv2 = jnp.concatenate(dvs, axis=1)
            dk_ref[...] = pend_k[...] + dk2[:BLK]
            dv_ref[...] = pend_v[...] + dv2[:BLK]
            pend_k[...] = dk2[BLK:]
            pend_v[...] = dv2[BLK:]
            dsk_ref[...] += dsk

        @pl.when(j == NB)
        def _():
            dk_ref[...] = pend_k[...]
            dv_ref[...] = pend_v[...]

    cur = lambda col: (lambda b, j: (b * NB + jnp.minimum(j, NB - 1), col))
    prev = lambda col: (lambda b, j: (b * NB + jnp.maximum(j - 1, 0), col))
    small = lambda shape: pl.BlockSpec(shape, lambda b, j: (0, 0))
    return _run("attn_bwd", body, (BL, NB + 1), [qkv, qkv, qkv, qkv, qkv, d_attn, attn_out, sink_rows, _swa_bias_table()],
                [pl.BlockSpec((BLK, ATT_W), cur(0)),
                 pl.BlockSpec((BLK, KV_W), cur(4)), pl.BlockSpec((BLK, KV_W), prev(4)),
                 pl.BlockSpec((BLK, KV_W), cur(5)), pl.BlockSpec((BLK, KV_W), prev(5)),
                 pl.BlockSpec((BLK, ATT_W), cur(0)), pl.BlockSpec((BLK, ATT_W), cur(0)), small((8, 128)),
                 pl.BlockSpec((None, N_KV, GQA * BLK, 2 * BLK), lambda b, j: (jnp.minimum(j, 1), 0, 0, 0))],
                [SDS((T, ATT_W), f32), SDS((T, KV_W), f32), SDS((T, KV_W), f32), SDS((8, 128), f32)],
                [pl.BlockSpec((BLK, ATT_W), cur(0)), pl.BlockSpec((BLK, KV_W), prev(0)),
                 pl.BlockSpec((BLK, KV_W), prev(0)), small((8, 128))],
                scratch=[pltpu.VMEM((BLK, KV_W), f32)] * 2, exchange=exchange)


def mem_conv_bwd(d_mem_out, mem_out, d_conv_out, proj, qkv, km, vm, conv_w8, pk, S, tm, exchange):
    T = d_mem_out.shape[0]
    NM = km.shape[0] // (T // S)

    def body(dmo_ref, mo_ref, dco_ref, ch_ref, cb_ref, cc_ref, chh_ref, cch_ref, qm_ref, km_ref, vm_ref, cw_ref,
             pk_ref, dqm_ref, dkm_ref, dvm_ref, dcb_ref, dcv_ref, dcw_ref, dcbias_ref):
        i = pl.program_id(0)
        first = (i * tm) % S == 0

        @pl.when(i == 0)
        def _():
            dcw_ref[...] = jnp.zeros_like(dcw_ref)
            dcbias_ref[...] = jnp.zeros_like(dcbias_ref)

        @pl.when(first)
        def _():
            dkm_ref[...] = jnp.zeros_like(dkm_ref)
            dvm_ref[...] = jnp.zeros_like(dvm_ref)

        qm, kmv, vmv, dmo, mo = qm_ref[...], km_ref[...], vm_ref[...], dmo_ref[...], mo_ref[...]
        ones_w = jnp.ones((NM, NM), MXU)
        for h in range(N_MEMH):
            qh, kh, vh, e = _mem_head(qm, kmv, vmv, h)
            eb = _c(e)
            doh = dmo[:, h * HD:(h + 1) * HD]
            delta = _rowsum_mxu(doh * mo[:, h * HD:(h + 1) * HD], NM)
            dp = _nt(_c(doh), vh)
            inv_w = 1.0 / _nn(eb, ones_w)
            ds = _c(e * inv_w * (dp - delta) * (HD ** -0.5))
            dvm_ref[:, pl.ds(h * HD, HD)] += _tn(eb, _c(doh * inv_w[:, :HD]))
            dkm_ref[:, pl.ds(h * HD, HD)] += _tn(ds, qh)
            dqm_ref[:, pl.ds(h * HD, HD)] = _nn(ds, kh)

        u = cc_ref[...] * ch_ref[...]
        uh = jnp.where(first, 0.0, cch_ref[...] * chh_ref[...])
        u1, u2 = _conv_taps(u, uh)
        conv = cw_ref[0:1, :] * u2 + cw_ref[1:2, :] * u1 + cw_ref[2:3, :] * u + _small(pk_ref, "conv_b")
        dy = dco_ref[...]
        dcb_ref[...] = dy * conv
        dcv = dy * cb_ref[...]
        dcv_ref[...] = dcv
        dcbias_ref[...] += jnp.sum(dcv, axis=0, keepdims=True)
        dcw_ref[0:1, :] += jnp.sum(dcv * u2, axis=0, keepdims=True)
        dcw_ref[1:2, :] += jnp.sum(dcv * u1, axis=0, keepdims=True)
        dcw_ref[2:3, :] += jnp.sum(dcv * u, axis=0, keepdims=True)

    tile = lambda w, col: pl.BlockSpec((tm, w), lambda i: (i, col))
    halo = lambda col: pl.BlockSpec((8, CONV_W), lambda i: (jnp.maximum(i * (tm // 8) - 1, 0), col))
    seq = pl.BlockSpec((NM, MEM_W), lambda i: ((i * tm) // S, 0))
    const = lambda shape: pl.BlockSpec(shape, lambda i: (0, 0))
    return _run("mem_conv_bwd", body, (T // tm,),
                [d_mem_out, mem_out, d_conv_out, proj, proj, proj, proj, proj, qkv, km, vm, conv_w8, pk],
                [tile(MEM_W, 0), tile(MEM_W, 0), tile(CONV_W, 0), tile(CONV_W, 3), tile(CONV_W, 4), tile(CONV_W, 5),
                 halo(3), halo(5), tile(MEM_W, 3), seq, seq, VM, VM],
                [SDS((T, MEM_W), f32), SDS(km.shape, f32), SDS(km.shape, f32),
                 SDS((T, CONV_W), f32), SDS((T, CONV_W), f32), SDS((8, CONV_W), f32), SDS((1, CONV_W), f32)],
                [tile(MEM_W, 0), seq, seq, tile(CONV_W, 0), tile(CONV_W, 0), const((8, CONV_W)), const((1, CONV_W))],
                vmem_mib=48, exchange=exchange)


def in_proj_bwd(dqn, dkn, dv, dcb, dcv, dqmn, proj, conv_w8, xn, x2d, dx1, pk, winT, S, tm, stages, ws, ms, vs):
    T, D = x2d.shape
    P = winT.shape[0]
    last_blk = T // 8 - 1
    n = len(stages)
    nsteps = T // tm
    tile_w = ws[0].shape[1] // (nsteps // 2)
    turn = [e * 2 // n for e in range(n)]

    def body(dq_ref, dk_ref, dv_ref, dcb_ref, dcv_ref, dcvn_ref, dqm_ref, qa_ref, ka_ref, ch_ref, cc_ref, qma_ref,
             cw_ref, xn_ref, x_ref, dx1_ref, pk_ref, w_ref, *rest):
        st, aw, am, av = (rest[k * n:(k + 1) * n] for k in range(4))
        dx_ref, dw_ref, dg_ref, dqg_ref, dkg_ref, dmqg_ref = rest[4 * n:4 * n + 6]
        aouts = rest[4 * n + 6:]
        i = pl.program_id(0)

        for parity in range(2):
            @pl.when(i % 2 == parity)
            def _(parity=parity):
                for e in range(n):
                    if turn[e] == parity:
                        g = jnp.concatenate([_sum_chips(st[e].at[0]), _sum_chips(st[e].at[1])], axis=0)
                        d, mm, vv = _adamw_math(aw[e][...], g, am[e][...], av[e][...])
                        for k, val in enumerate((g, d, mm, vv)):
                            aouts[4 * e + k][...] = val

        @pl.when(i == 0)
        def _():
            dw_ref[...] = jnp.zeros_like(dw_ref)
            dg_ref[...] = jnp.zeros_like(dg_ref)
            dqg_ref[...] = jnp.zeros_like(dqg_ref)
            dkg_ref[...] = jnp.zeros_like(dkg_ref)
            dmqg_ref[...] = jnp.zeros_like(dmqg_ref)

        dqa, gq = _heads_norm_bwd(dq_ref[...], qa_ref[...], _small(pk_ref, "q_norm"))
        dka, gk = _heads_norm_bwd(dk_ref[...], ka_ref[...], _small(pk_ref, "k_norm"))
        dqma, gmq = _heads_norm_bwd(dqm_ref[...], qma_ref[...], _small(pk_ref, "mem_q_norm"))
        dqg_ref[...] += gq
        dkg_ref[...] += gk
        dmqg_ref[...] += gmq

        last = ((i + 1) * tm) % S == 0
        dcv = dcv_ref[...]
        nxt = jnp.where(last, 0.0, dcvn_ref[...])
        row = lax.broadcasted_iota(jnp.int32, dcv.shape, 0)
        n1 = jnp.where(row == tm - 1, nxt[0:1, :], pltpu.roll(dcv, tm - 1, 0))
        n2 = jnp.where(row == tm - 2, nxt[0:1, :], jnp.where(row == tm - 1, nxt[1:2, :], pltpu.roll(dcv, tm - 2, 0)))
        du = cw_ref[2:3, :] * dcv + cw_ref[1:2, :] * n1 + cw_ref[0:1, :] * n2
        d_proj = jnp.concatenate([_c(dqa), _c(dka), _c(dv_ref[...]), _c(du * cc_ref[...]),
                                  _c(dcb_ref[...]), _c(du * ch_ref[...]), _c(dqma)], axis=1)
        dw_ref[...] += _tn(d_proj, xn_ref[...])
        xv = x_ref[...]
        dv_, dg = _norm_bwd(_nn(d_proj, w_ref[...]), xv, _rstd(xv), _small(pk_ref, "norm_mix"))
        dx_ref[...] = dx1_ref[...] + dv_
        dg_ref[...] += dg

    tile = lambda w, col=0: pl.BlockSpec((tm, w), lambda i: (i, col))
    nhalo = pl.BlockSpec((8, CONV_W), lambda i: (jnp.minimum((i + 1) * (tm // 8), last_blk), 0))
    const = lambda shape: pl.BlockSpec(shape, lambda i: (0, 0))
    st_specs = [pl.BlockSpec((2, 4, s.shape[2], tile_w), lambda i: (0, 0, 0, i // 2)) for s in stages]
    w_specs = [pl.BlockSpec((w.shape[0], tile_w), lambda i: (0, i // 2)) for w in ws]
    res = _run("in_proj_bwd", body, (nsteps,),
               [dqn, dkn, dv, dcb, dcv, dcv, dqmn, proj, proj, proj, proj, proj, conv_w8, xn, x2d, dx1, pk, winT]
               + list(stages) + list(ws) + list(ms) + list(vs),
               [tile(ATT_W), tile(KV_W), tile(KV_W), tile(CONV_W), tile(CONV_W), nhalo, tile(MEM_W),
                tile(ATT_W, 0), tile(KV_W, 4), tile(CONV_W, 3), tile(CONV_W, 5), tile(MEM_W, 6), VM,
                tile(D), tile(D), tile(D), VM, VM] + st_specs + w_specs * 3,
               [SDS((T, D), f32), SDS((P, D), f32), SDS((1, D), f32), SDS((1, HD), f32), SDS((1, HD), f32),
                SDS((1, HD), f32)] + [SDS(w.shape, f32) for w in ws for _ in range(4)],
               [tile(D), pl.BlockSpec((P, D), lambda i: (0, 0)), const((1, D)), const((1, HD)), const((1, HD)),
                const((1, HD))] + [s for s in w_specs for _ in range(4)],
               vmem_mib=56)
    return res[:6], [res[6 + 4 * e:10 + 4 * e] for e in range(n)]


def mem_kv_bwd(dkm, dvm, kv, memn, mem2d, pk, wmkv):
    def body(dkm_ref, dvm_ref, kv_ref, mn_ref, m_ref, pk_ref, w_ref, dw_ref, dg_ref, dkg_ref):
        dkk, dkg = _heads_norm_bwd(dkm_ref[...], kv_ref[:, :MEM_W], _small(pk_ref, "mem_k_norm"))
        dkg_ref[...] = dkg
        dkv = _c(jnp.concatenate([dkk, dvm_ref[...]], axis=1))
        dw_ref[...] = _tn(mn_ref[...], dkv)
        mv = m_ref[...]
        dg_ref[...] = jnp.sum(_nt(dkv, w_ref[...]) * mv * _rstd(mv), axis=0, keepdims=True)

    return _run("mem_kv_bwd", body, (), [dkm, dvm, kv, memn, mem2d, pk, wmkv], [VM] * 7,
                [SDS(wmkv.shape, f32), SDS((1, mem2d.shape[1]), f32), SDS((1, HD), f32)], [VM] * 3, vmem_mib=40)


def _halves_view(g):
    return g.reshape(4, 2, g.shape[0] // 8, g.shape[1])


def kernel(x, mem, norm_mix, w_in, q_norm, k_norm, attn_sinks, conv_w, conv_b, norm_mem, w_mem_kv, mem_q_norm, mem_k_norm, out_norm_attn, out_norm_conv, out_norm_mem, w_out, norm_ffn, w_gate, w_up, w_down, loss_target, m_norm_mix, m_w_in, m_q_norm, m_k_norm, m_attn_sinks, m_conv_w, m_conv_b, m_norm_mem, m_w_mem_kv, m_mem_q_norm, m_mem_k_norm, m_out_norm_attn, m_out_norm_conv, m_out_norm_mem, m_w_out, m_norm_ffn, m_w_gate, m_w_up, m_w_down, v_norm_mix, v_w_in, v_q_norm, v_k_norm, v_attn_sinks, v_conv_w, v_conv_b, v_norm_mem, v_w_mem_kv, v_mem_q_norm, v_mem_k_norm, v_out_norm_attn, v_out_norm_conv, v_out_norm_mem, v_w_out, v_norm_ffn, v_w_gate, v_w_up, v_w_down):
    BL, S, D = x.shape
    T = BL * S
    TM = 256
    TM_BIG = min(512, S)
    _, _, ci = _place()
    cidx = ci.reshape(1).astype(jnp.int32)
    w_small = dict(norm_mix=norm_mix, norm_mem=norm_mem, norm_ffn=norm_ffn, out_norm_attn=out_norm_attn,
                   out_norm_conv=out_norm_conv, out_norm_mem=out_norm_mem, conv_w=conv_w, conv_b=conv_b, q_norm=q_norm,
                   k_norm=k_norm, mem_q_norm=mem_q_norm, mem_k_norm=mem_k_norm, attn_sinks=attn_sinks)
    m_small = dict(norm_mix=m_norm_mix, norm_mem=m_norm_mem, norm_ffn=m_norm_ffn, out_norm_attn=m_out_norm_attn,
                   out_norm_conv=m_out_norm_conv, out_norm_mem=m_out_norm_mem, conv_w=m_conv_w, conv_b=m_conv_b,
                   q_norm=m_q_norm, k_norm=m_k_norm, mem_q_norm=m_mem_q_norm, mem_k_norm=m_mem_k_norm,
                   attn_sinks=m_attn_sinks)
    v_small = dict(norm_mix=v_norm_mix, norm_mem=v_norm_mem, norm_ffn=v_norm_ffn, out_norm_attn=v_out_norm_attn,
                   out_norm_conv=v_out_norm_conv, out_norm_mem=v_out_norm_mem, conv_w=v_conv_w, conv_b=v_conv_b,
                   q_norm=v_q_norm, k_norm=v_k_norm, mem_q_norm=v_mem_q_norm, mem_k_norm=v_mem_k_norm,
                   attn_sinks=v_attn_sinks)
    pk = _pack_small(w_small)

    rowblocks = lambda a, b, c, d, e, f: [a[0].T, b[0].T, c[0].T, d[0], e[0], f[0]]
    w_rb = rowblocks(w_in, w_gate, w_up, w_down, w_out, w_mem_kv)
    m_rb = rowblocks(m_w_in, m_w_gate, m_w_up, m_w_down, m_w_out, m_w_mem_kv)
    v_rb = rowblocks(v_w_in, v_w_gate, v_w_up, v_w_down, v_w_out, v_w_mem_kv)
    (winT_s,) = prep_weights("prep_w_in", w_rb[:1])
    cw_pad = jnp.zeros((8, 128), f32).at[:3, :HD].set(conv_w[0])
    (wgT_s, wuT_s, wd_s, wout_s, wmkv_s), (winT, cw_all) = prep_weights(
        "gather_w_in", w_rb[1:], _together([gather_two_legs([winT_s]), gather_exchange([cw_pad], [False])]))
    conv_w_full = jnp.transpose(cw_all.reshape(4, 8, 128)[:, :3, :HD], (1, 0, 2)).reshape(3, CONV_W)
    conv_w8 = jnp.zeros((8, CONV_W), f32).at[:3].set(conv_w_full)
    sink_rows = jnp.broadcast_to(attn_sinks.reshape(N_Q, 1), (N_Q, 128))

    x2d = x.reshape(T, D)
    mem2d = mem.reshape(-1, D)
    (xn, proj, qkv), near1 = in_proj_fwd(x2d, pk, winT, TM_BIG, gather_near_exchange([wgT_s, wout_s, wmkv_s], relay_early=1))
    (attn_out,), (wgT, wout, wmkv, *near2) = attn_fwd(
        qkv, sink_rows, BL, S, _together([gather_far_exchange(near1, relay_early=2), gather_near_exchange([wuT_s, wd_s], relay_early=2)]))
    memn, kv, km, vm = mem_kv_fwd(mem2d, pk, wmkv)
    (conv_out, mem_out, merged, x1, h), (wuT, wd) = mixer_tail_fwd(
        x2d, attn_out, proj, qkv, km, vm, conv_w8, pk, wout, S, TM_BIG, gather_far_exchange(near2, relay_early=2))

    dx1, dx2b, act, d_gate, d_up, loss8, d_norm_ffn = ffn_fwd_bwd(h, x1, loss_target.reshape(T, D), wgT, wuT, wd, pk, TM)
    F = wd.shape[0]
    g_wd = matmul_tn(act, dx2b, "dw_down", F // 2, min(T, 1024))
    g_wgT = matmul_tn(d_gate, h, "dw_gate", F // 2, min(T, 1024))
    g_wuT = matmul_tn(d_up, h, "dw_up", F // 2, min(T, 1024))

    d_attn, d_conv_out, d_mem_out, g_wout, d_gains = out_proj_bwd(dx1, merged, attn_out, conv_out, mem_out, pk, wout, TM_BIG)
    late = [_halves_view(g) for g in (g_wgT, g_wuT, g_wd, g_wout)]
    (dqmn, dkm, dvm, dcb, dcv, d_cw8, d_cbias), late_sib = mem_conv_bwd(
        d_mem_out, mem_out, d_conv_out, proj, qkv, km, vm, conv_w8, pk, S, min(1024, S), halves_exchange(late))
    late_part = add_halves(cidx, late, late_sib, "grad_add_halves_ffn")
    (dqn, dkn, dv, d_sink8), late_stage = attn_bwd(qkv, d_attn, attn_out, sink_rows, BL, S,
                                                   scatter_exchange(late_part, relay_before_end=[20, 11, 2, 0]))
    (g_x, g_winT, d_norm_mix, d_qg, d_kg, d_mqg), late_res = in_proj_bwd(
        dqn, dkn, dv, dcb, dcv, dqmn, proj, conv_w8, xn, x2d, dx1, pk, winT, S, TM,
        late_stage, w_rb[1:5], m_rb[1:5], v_rb[1:5])
    g_wmkv, d_norm_mem, d_mkg = mem_kv_bwd(dkm, dvm, kv, memn, mem2d, pk, wmkv)

    tot, tail_stage = tail_reduce(d_norm_mix, d_norm_mem, d_norm_ffn, d_gains, d_cw8, d_cbias, d_qg, d_kg, d_mqg, d_mkg,
                                  d_sink8, loss8, [_halves_view(g) for g in (g_winT, g_wmkv)])
    loss = tot[5, 384]
    tail_res, _ = adamw_big("adamw_tail", tail_stage, [w_rb[0], w_rb[5]], [m_rb[0], m_rb[5]], [v_rb[0], v_rb[5]], 4)
    res = {"w_in": [a.T[None] for a in tail_res[0]], "w_gate": [a.T[None] for a in late_res[0]],
           "w_up": [a.T[None] for a in late_res[1]], "w_down": [a[None] for a in late_res[2]],
           "w_out": [a[None] for a in late_res[3]], "w_mem_kv": [a[None] for a in tail_res[1]]}
    res.update(adamw_small(tot, pk, _pack_small(m_small), _pack_small(v_small), {k: w_small[k].shape for k in SMALL}))

    order = ["norm_mix", "w_in", "q_norm", "k_norm", "attn_sinks", "conv_w", "conv_b", "norm_mem", "w_mem_kv",
             "mem_q_norm", "mem_k_norm", "out_norm_attn", "out_norm_conv", "out_norm_mem", "w_out", "norm_ffn",
             "w_gate", "w_up", "w_down"]
    return (loss, g_x.reshape(BL, S, D), *[res[n][0] for n in order], *[res[n][1] for n in order],
            *[res[n][2] for n in order], *[res[n][3] for n in order])
```

```python
import collections
import functools

import jax
import jax.numpy as jnp
import numpy as np
from jax import lax
from jax.experimental import pallas as pl
from jax.experimental.pallas import tpu as pltpu

f32 = jnp.float32
MXU = jnp.bfloat16
WIRE = jnp.bfloat16
EPS = 1e-6
NEG = -1e30
HD = 64
BLK = 128
N_Q, N_KV, N_MEMH = 8, 2, 4
GQA = N_Q // N_KV
ATT_W, KV_W, CONV_W, MEM_W = 512, 128, 256, 256
VMEM_MIB = 1024 * 1024
ADAM_LR, ADAM_B1, ADAM_B2, ADAM_EPS, ADAM_WD, ADAM_STEP = 0.001, 0.9, 0.999, 1e-08, 0.01, 10

MESH = pl.DeviceIdType.MESH
VM = pl.BlockSpec(memory_space=pltpu.VMEM)
ANY = pl.BlockSpec(memory_space=pl.ANY)
SDS = jax.ShapeDtypeStruct
DMA = pltpu.SemaphoreType.DMA


def _c(v):
    return v.astype(MXU)


def _nn(a, b):
    return lax.dot_general(a, b, (((1,), (0,)), ((), ())), preferred_element_type=f32)


def _nt(a, b):
    return lax.dot_general(a, b, (((1,), (1,)), ((), ())), preferred_element_type=f32)


def _tn(a, b):
    return lax.dot_general(a, b, (((0,), (0,)), ((), ())), preferred_element_type=f32)


def _rstd(v):
    return lax.rsqrt(jnp.mean(v * v, axis=-1, keepdims=True) + EPS)


def _norm_bwd(dy, v, r, g):
    dyg = dy * g
    dv = r * dyg - v * (r * r * r) * jnp.mean(dyg * v, axis=-1, keepdims=True)
    return dv, jnp.sum(dy * v * r, axis=0, keepdims=True)


def _split3(v):
    hi = _c(v)
    r1 = v - hi.astype(f32)
    mid = _c(r1)
    return hi, mid, _c(r1 - mid.astype(f32))


def _rowsum_mxu(v, width):
    ones = jnp.ones((v.shape[1], width), MXU)
    return sum(_nn(a, ones) for a in _split3(v))


def _seg_sums(v):
    r = lax.broadcasted_iota(jnp.int32, (2 * HD, 2 * HD), 0) // HD
    c = lax.broadcasted_iota(jnp.int32, (2 * HD, 2 * HD), 1) // HD
    bd = (r == c).astype(MXU)
    outs = []
    for b in range(v.shape[1] // (2 * HD)):
        outs.append(sum(_nn(a, bd) for a in _split3(v[:, b * 2 * HD:(b + 1) * 2 * HD])))
    return outs[0] if len(outs) == 1 else jnp.concatenate(outs, axis=1)


def _lanes(g, width):
    return jnp.concatenate([g] * (width // HD), axis=1)


def _heads_rstd(v):
    return lax.rsqrt(_seg_sums(v * v) * (1.0 / HD) + EPS)


def _heads_norm_bwd(dy, v, g):
    r = _heads_rstd(v)
    gl = _lanes(g, v.shape[1])
    dyg = dy * gl
    dv = r * dyg - v * (r * r * r) * (_seg_sums(dyg * v) * (1.0 / HD))
    dgl = jnp.sum(dy * v * r, axis=0, keepdims=True)
    return dv, sum(dgl[:, s * HD:(s + 1) * HD] for s in range(v.shape[1] // HD))


def _exp_scores(s, extra=None):
    m = jnp.max(s, axis=-1, keepdims=True)
    if extra is None:
        return jnp.exp(s - m), None
    m = jnp.maximum(m, extra)
    return jnp.exp(s - m), jnp.exp(extra - m)


def _place():
    return lax.axis_index("x"), lax.axis_index("y"), lax.axis_index("c")


SMALL_AT = {"norm_mix": (0, 0, 1024), "norm_mem": (1, 0, 1024), "norm_ffn": (2, 0, 1024),
            "out_norm_attn": (3, 0, ATT_W), "out_norm_conv": (3, ATT_W, CONV_W), "out_norm_mem": (3, ATT_W + CONV_W, MEM_W),
            "conv_b": (4, 3 * CONV_W, CONV_W), "q_norm": (5, 0, HD), "k_norm": (5, HD, HD), "mem_q_norm": (5, 2 * HD, HD),
            "mem_k_norm": (5, 3 * HD, HD), "attn_sinks": (5, 256, N_Q)}
SMALL = ("norm_mix", "norm_mem", "norm_ffn", "out_norm_attn", "out_norm_conv", "out_norm_mem", "conv_w", "conv_b",
         "q_norm", "k_norm", "mem_q_norm", "mem_k_norm", "attn_sinks")


def _small(pk_ref, name):
    r, c0, w = SMALL_AT[name]
    return pk_ref[r:r + 1, c0:c0 + w]


def _pack_small(d):
    z = lambda n: jnp.zeros((1, n), f32)
    row3 = jnp.concatenate([d["out_norm_attn"], d["out_norm_conv"], d["out_norm_mem"]], axis=1)
    row4 = jnp.concatenate([d["conv_w"].reshape(1, 3 * HD), z(3 * CONV_W - 3 * HD), d["conv_b"]], axis=1)
    row5 = jnp.concatenate([d["q_norm"], d["k_norm"], d["mem_q_norm"], d["mem_k_norm"], d["attn_sinks"],
                            z(1024 - 4 * HD - N_Q)], axis=1)
    return jnp.concatenate([d["norm_mix"], d["norm_mem"], d["norm_ffn"], row3, row4, row5, z(1024), z(1024)], axis=0)


def _other_chips(x, y):
    return [(1 - x, y), (x, 1 - y), (1 - x, 1 - y)]


Exchange = collections.namedtuple("Exchange", "ins outs sems start finish relays aliases", defaults=((), {}))


def _together(exchanges):
    def bounds(key):
        at, out = 0, []
        for ex in exchanges:
            out.append((at, at + len(getattr(ex, key))))
            at += len(getattr(ex, key))
        return out

    bi, bo, bs = bounds("ins"), bounds("outs"), bounds("sems")

    def of(i, fn):
        return lambda xa, xo, xs: fn(xa[bi[i][0]:bi[i][1]], xo[bo[i][0]:bo[i][1]], xs[bs[i][0]:bs[i][1]])

    def every(name):
        fns = [of(i, getattr(ex, name)) for i, ex in enumerate(exchanges)]

        def run(xa, xo, xs):
            for fn in fns:
                fn(xa, xo, xs)
        return run

    aliases = {}
    for i, ex in enumerate(exchanges):
        aliases.update({bi[i][0] + a: bo[i][0] + o for a, o in ex.aliases.items()})
    return Exchange([a for ex in exchanges for a in ex.ins], [o for ex in exchanges for o in ex.outs],
                    [s for ex in exchanges for s in ex.sems], every("start"), every("finish"),
                    [(sbe, of(i, fn)) for i, ex in enumerate(exchanges) for sbe, fn in ex.relays], aliases)


def _run(name, body, grid, ins, in_specs, out_shape, out_specs, scratch=(), vmem_mib=32, exchange=None):
    ins, in_specs, out_shape, out_specs, scratch = list(ins), list(in_specs), list(out_shape), list(out_specs), list(scratch)
    ni, no, ns = len(ins), len(out_shape), len(scratch)
    ex = exchange
    if ex is not None:
        nxi, nxo = len(ex.ins), len(ex.outs)

    def call_body(*refs):
        if ex is None:
            body(*refs)
            return
        a, xa = refs[:ni], refs[ni:ni + nxi]
        o, xo = refs[ni + nxi:ni + nxi + no], refs[ni + nxi + no:ni + nxi + no + nxo]
        s, xs = refs[ni + nxi + no + nxo:ni + nxi + no + nxo + ns], refs[ni + nxi + no + nxo + ns:]
        if grid:
            first = functools.reduce(jnp.logical_and, [pl.program_id(d) == 0 for d in range(len(grid))])
            last = functools.reduce(jnp.logical_and, [pl.program_id(d) == grid[d] - 1 for d in range(len(grid))])
            pl.when(first)(lambda: ex.start(xa, xo, xs))
            body(*a, *o, *s)
            nsteps = functools.reduce(lambda p, q: p * q, grid)
            for before_end, fn in ex.relays:
                at = np.unravel_index(max(nsteps - 1 - before_end, 0), grid)
                here = functools.reduce(jnp.logical_and, [pl.program_id(d) == int(at[d]) for d in range(len(grid))])
                pl.when(here)(functools.partial(fn, xa, xo, xs))
            pl.when(last)(lambda: ex.finish(xa, xo, xs))
        else:
            ex.start(xa, xo, xs)
            if body is not None:
                body(*a, *o, *s)
            for _, fn in ex.relays:
                fn(xa, xo, xs)
            ex.finish(xa, xo, xs)

    kw = dict(grid=grid) if grid else {}
    if ex is not None:
        if ex.aliases:
            kw["input_output_aliases"] = {ni + i: no + o for i, o in ex.aliases.items()}
        ins, in_specs = ins + list(ex.ins), in_specs + [ANY] * nxi
        out_shape, out_specs = out_shape + list(ex.outs), out_specs + [ANY] * nxo
        scratch = scratch + list(ex.sems)
    res = pl.pallas_call(
        call_body, name=name, out_shape=out_shape, in_specs=in_specs, out_specs=out_specs, scratch_shapes=scratch,
        compiler_params=pltpu.CompilerParams(dimension_semantics=("arbitrary",) * len(grid) if grid else None,
                                             vmem_limit_bytes=vmem_mib * VMEM_MIB), **kw)(*ins)
    res = list(res)
    return (res[:no], res[no:]) if ex is not None else res


def _remote(src, dst, ssem, rsem, dev):
    return pltpu.make_async_remote_copy(src_ref=src, dst_ref=dst, send_sem=ssem, recv_sem=rsem,
                                        device_id=dev, device_id_type=MESH)


def gather_exchange(shards, split, relay_early=0):
    n = len(shards)

    def rows(ref, e, kk, half=None):
        R = shards[e].shape[0]
        if half is None:
            return ref.at[pl.ds(pl.multiple_of(kk * R, 8), R)]
        return ref.at[pl.ds(pl.multiple_of(kk * R + half * (R // 2), 8), R // 2)]

    def ici(src, dst, sm, e, j, chip_j, x, y, c):
        k = 2 * x + y
        if split[e]:
            s = src[e].at[pl.ds(pl.multiple_of(c * (shards[e].shape[0] // 2), 8), shards[e].shape[0] // 2)]
            return _remote(s, rows(dst[e], e, k, c), sm[0].at[6 * e + j], sm[1].at[6 * e + j], (*chip_j, c))
        return _remote(src[e], rows(dst[e], e, k), sm[0].at[6 * e + j], sm[1].at[6 * e + j], (*chip_j, c))

    def landed(dst, e, chip_j, c):
        kj = 2 * chip_j[0] + chip_j[1]
        return rows(dst[e], e, kj, c) if split[e] else rows(dst[e], e, kj)

    def forward(dst, sm, e, j, chip_j, x, y, c, sender_c):
        kj = 2 * chip_j[0] + chip_j[1]
        r = rows(dst[e], e, kj, sender_c)
        return _remote(r, r, sm[0].at[6 * e + 3 + j], sm[1].at[6 * e + 3 + j], (x, y, 1 - c))

    def local(src, dst, sm, e, x, y):
        return pltpu.make_async_copy(src[e], rows(dst[e], e, 2 * x + y), sm[2].at[e])

    def start(src, dst, sm):
        x, y, c = _place()
        for e in range(n):
            local(src, dst, sm, e, x, y).start()
            for j, chip_j in enumerate(_other_chips(x, y)):
                ici(src, dst, sm, e, j, chip_j, x, y, c).start()

    def relay(src, dst, sm):
        x, y, c = _place()
        for e in range(n):
            for j, chip_j in enumerate(_other_chips(x, y)):
                r = landed(dst, e, chip_j, c)
                _remote(r, r, sm[0].at[6 * e + j], sm[1].at[6 * e + j], (*chip_j, c)).wait_recv()
                if split[e]:
                    forward(dst, sm, e, j, chip_j, x, y, c, c).start()

    def finish(src, dst, sm):
        x, y, c = _place()
        chips = _other_chips(x, y)
        for e in range(n):
            for j, chip_j in enumerate(chips):
                if split[e]:
                    forward(dst, sm, e, j, chip_j, x, y, c, 1 - c).wait_recv()
        for e in range(n):
            for j, chip_j in enumerate(chips):
                ici(src, dst, sm, e, j, chip_j, x, y, c).wait_send()
                if split[e]:
                    forward(dst, sm, e, j, chip_j, x, y, c, c).wait_send()
            local(src, dst, sm, e, x, y).wait()

    outs = [SDS((4 * s.shape[0], s.shape[1]), s.dtype) for s in shards]
    return Exchange(list(shards), outs, [DMA((6 * n,)), DMA((6 * n,)), DMA((n,))], start, finish, [(relay_early, relay)])


def _block_rows(ref, R, kk, half, quarter=None):
    hr = R // 2
    if quarter is None:
        return ref.at[pl.ds(pl.multiple_of(kk * R + half * hr, 8), hr)]
    return ref.at[pl.ds(pl.multiple_of(kk * R + half * hr + quarter * (hr // 2), 8), hr // 2)]


def gather_near_exchange(shards, relay_early=0):
    n = len(shards)
    R = [s.shape[0] for s in shards]

    def ici(src, dst, sm, e, j, chip_j, x, y, c):
        half = src[e].at[pl.ds(pl.multiple_of(c * (R[e] // 2), 8), R[e] // 2)]
        return _remote(half, _block_rows(dst[e], R[e], 2 * x + y, c), sm[0].at[4 * e + j], sm[1].at[4 * e + j], (*chip_j, c))

    def forward(dst, sm, e, j, chip_j, x, y, c, sender_c):
        r = _block_rows(dst[e], R[e], 2 * chip_j[0] + chip_j[1], sender_c)
        return _remote(r, r, sm[0].at[4 * e + 2 + j], sm[1].at[4 * e + 2 + j], (x, y, 1 - c))

    def local(src, dst, sm, e, x, y):
        return pltpu.make_async_copy(src[e], dst[e].at[pl.ds(pl.multiple_of((2 * x + y) * R[e], 8), R[e])], sm[2].at[e])

    def start(src, dst, sm):
        x, y, c = _place()
        for e in range(n):
            local(src, dst, sm, e, x, y).start()
            for j, chip_j in enumerate(_other_chips(x, y)[:2]):
                ici(src, dst, sm, e, j, chip_j, x, y, c).start()

    def relay(src, dst, sm):
        x, y, c = _place()
        for e in range(n):
            for j, chip_j in enumerate(_other_chips(x, y)[:2]):
                r = _block_rows(dst[e], R[e], 2 * chip_j[0] + chip_j[1], c)
                _remote(r, r, sm[0].at[4 * e + j], sm[1].at[4 * e + j], (*chip_j, c)).wait_recv()
                forward(dst, sm, e, j, chip_j, x, y, c, c).start()

    def finish(src, dst, sm):
        x, y, c = _place()
        near = _other_chips(x, y)[:2]
        for e in range(n):
            for j, chip_j in enumerate(near):
                forward(dst, sm, e, j, chip_j, x, y, c, 1 - c).wait_recv()
        for e in range(n):
            for j, chip_j in enumerate(near):
                ici(src, dst, sm, e, j, chip_j, x, y, c).wait_send()
                forward(dst, sm, e, j, chip_j, x, y, c, c).wait_send()
            local(src, dst, sm, e, x, y).wait()

    outs = [SDS((4 * s.shape[0], s.shape[1]), s.dtype) for s in shards]
    return Exchange(list(shards), outs, [DMA((4 * n,)), DMA((4 * n,)), DMA((n,))], start, finish, [(relay_early, relay)])


def gather_far_exchange(bufs, relay_early=0):
    n = len(bufs)
    R = [b.shape[0] // 4 for b in bufs]

    def send(src, dst, sm, e, j, x, y, c):
        to, of = _other_chips(x, y)[j], _other_chips(x, y)[1 - j]
        kk = 2 * of[0] + of[1]
        return _remote(_block_rows(src[e], R[e], kk, c, j), _block_rows(dst[e], R[e], kk, c, j),
                       sm[0].at[4 * e + j], sm[1].at[4 * e + j], (*to, c))

    def landed(dst, e, j, x, y, half):
        return _block_rows(dst[e], R[e], 2 * (1 - x) + (1 - y), half, j)

    def forward(dst, sm, e, j, x, y, c, sender_c):
        r = landed(dst, e, j, x, y, sender_c)
        return _remote(r, r, sm[0].at[4 * e + 2 + j], sm[1].at[4 * e + 2 + j], (x, y, 1 - c))

    def start(src, dst, sm):
        x, y, c = _place()
        for e in range(n):
            for j in range(2):
                send(src, dst, sm, e, j, x, y, c).start()

    def relay(src, dst, sm):
        x, y, c = _place()
        for e in range(n):
            for j in range(2):
                r = landed(dst, e, j, x, y, c)
                _remote(r, r, sm[0].at[4 * e + j], sm[1].at[4 * e + j], (*_other_chips(x, y)[j], c)).wait_recv()
                forward(dst, sm, e, j, x, y, c, c).start()

    def finish(src, dst, sm):
        x, y, c = _place()
        for e in range(n):
            for j in range(2):
                forward(dst, sm, e, j, x, y, c, 1 - c).wait_recv()
        for e in range(n):
            for j in range(2):
                send(src, dst, sm, e, j, x, y, c).wait_send()
                forward(dst, sm, e, j, x, y, c, c).wait_send()

    outs = [SDS(b.shape, b.dtype) for b in bufs]
    return Exchange(list(bufs), outs, [DMA((4 * n,)), DMA((4 * n,))], start, finish, [(relay_early, relay)],
                    {i: i for i in range(n)})


def gather_two_legs(shards):
    near = gather_near_exchange(shards)
    far = gather_far_exchange(near.outs)

    def finish(src, dst, sm):
        near.relays[0][1](src, dst, sm[:3])
        near.finish(src, dst, sm[:3])
        far.start(dst, dst, sm[3:])
        far.relays[0][1](dst, dst, sm[3:])
        far.finish(dst, dst, sm[3:])

    return Exchange(near.ins, near.outs, list(near.sems) + list(far.sems),
                    lambda src, dst, sm: near.start(src, dst, sm[:3]), finish)


def halves_exchange(grads):
    n = len(grads)

    def copy(g, st, sm, e, x, y, c):
        return _remote(g[e].at[:, 1 - c], st[e], sm[0].at[e], sm[1].at[e], (x, y, 1 - c))

    def start(g, st, sm):
        x, y, c = _place()
        for e in range(n):
            copy(g, st, sm, e, x, y, c).start()

    def finish(g, st, sm):
        x, y, c = _place()
        for e in range(n):
            copy(g, st, sm, e, x, y, c).wait()

    outs = [SDS((4,) + a.shape[2:], a.dtype) for a in grads]
    return Exchange(list(grads), outs, [DMA((n,)), DMA((n,))], start, finish)


def scatter_exchange(parts, relay_before_end=None):
    n = len(parts)
    by_entry = relay_before_end is not None
    relay_before_end = relay_before_end or [0] * n

    def ici(p, st, sm, e, j, chip_j, x, y, c):
        k, kj = 2 * x + y, 2 * chip_j[0] + chip_j[1]
        return _remote(p[e].at[kj], st[e].at[c, k], sm[0].at[8 * e + j], sm[1].at[8 * e + j], (*chip_j, c))

    def own(p, st, sm, e, x, y, c):
        k = 2 * x + y
        return _remote(p[e].at[k], st[e].at[c, k], sm[0].at[8 * e + 3], sm[1].at[8 * e + 3], (x, y, 1 - c))

    def forward(st, sm, e, j, chip_j, x, y, c, sender_c):
        kj = 2 * chip_j[0] + chip_j[1]
        r = st[e].at[sender_c, kj]
        return _remote(r, r, sm[0].at[8 * e + 4 + j], sm[1].at[8 * e + 4 + j], (x, y, 1 - c))

    def local(p, st, sm, e, x, y, c):
        k = 2 * x + y
        return pltpu.make_async_copy(p[e].at[k], st[e].at[c, k], sm[2].at[e])

    def start(p, st, sm, before_slot=None):
        x, y, c = _place()
        if by_entry:
            for e in range(n):
                for j, chip_j in enumerate(_other_chips(x, y)):
                    ici(p, st, sm, e, j, chip_j, x, y, c).start()
                local(p, st, sm, e, x, y, c).start()
                own(p, st, sm, e, x, y, c).start()
            return
        for j, chip_j in enumerate(_other_chips(x, y)):
            if before_slot is not None:
                before_slot(j, 2 * chip_j[0] + chip_j[1])
            for e in range(n):
                ici(p, st, sm, e, j, chip_j, x, y, c).start()
        if before_slot is not None:
            before_slot(3, 2 * x + y)
        for e in range(n):
            local(p, st, sm, e, x, y, c).start()
            own(p, st, sm, e, x, y, c).start()

    def relay(e, p, st, sm):
        x, y, c = _place()
        for j, chip_j in enumerate(_other_chips(x, y)):
            kj = 2 * chip_j[0] + chip_j[1]
            r = st[e].at[c, kj]
            _remote(r, r, sm[0].at[8 * e + j], sm[1].at[8 * e + j], (*chip_j, c)).wait_recv()
            forward(st, sm, e, j, chip_j, x, y, c, c).start()

    def finish(p, st, sm):
        x, y, c = _place()
        k = 2 * x + y
        chips = _other_chips(x, y)
        for e in range(n):
            r = st[e].at[1 - c, k]
            _remote(r, r, sm[0].at[8 * e + 3], sm[1].at[8 * e + 3], (x, y, 1 - c)).wait_recv()
            for j, chip_j in enumerate(chips):
                forward(st, sm, e, j, chip_j, x, y, c, 1 - c).wait_recv()
        for e in range(n):
            own(p, st, sm, e, x, y, c).wait_send()
            for j, chip_j in enumerate(chips):
                ici(p, st, sm, e, j, chip_j, x, y, c).wait_send()
                forward(st, sm, e, j, chip_j, x, y, c, c).wait_send()
            local(p, st, sm, e, x, y, c).wait()

    outs = [SDS((2,) + a.shape, a.dtype) for a in parts]
    return Exchange(list(parts), outs, [DMA((8 * n,)), DMA((8 * n,)), DMA((n,))], start, finish,
                    [(relay_before_end[e], functools.partial(relay, e)) for e in range(n)])


def tail_reduce(d_norm_mix, d_norm_mem, d_norm_ffn, d_gains, d_cw8, d_cbias, d_qg, d_kg, d_mqg, d_mkg, d_sink8, loss8, tail):
    n = len(tail)
    scatter = scatter_exchange([SDS((4,) + a.shape[2:], WIRE) for a in tail])

    def half_copy(g, sib, hsem, e, j, slot, x, y, c):
        return _remote(g[e].at[slot, 1 - c], sib[e].at[slot], hsem[0].at[4 * e + j], hsem[1].at[4 * e + j], (x, y, 1 - c))

    def body(nm_ref, nmem_ref, nf_ref, gn_ref, cw_ref, cb_ref, qg_ref, kg_ref, mqg_ref, mkg_ref, sk_ref, ls_ref, *rest):
        g, o_ref, st = rest[:n], rest[n], rest[n + 1:2 * n + 1]
        buf, ssem, rsem = rest[2 * n + 1:2 * n + 4]
        own, sib, part = (rest[2 * n + 4 + i * n:2 * n + 4 + (i + 1) * n] for i in range(3))
        lsem = rest[5 * n + 4]
        hsem, xsem = rest[5 * n + 5:5 * n + 7], rest[5 * n + 7:]
        x, y, c = _place()
        loads = [pltpu.make_async_copy(g[e].at[:, c], own[e], lsem.at[e]) for e in range(n)]
        for ld in loads:
            ld.start()
        for j, slot in enumerate([2 * cx + cy for cx, cy in _other_chips(x, y)] + [2 * x + y]):
            for e in range(n):
                half_copy(g, sib, hsem, e, j, slot, x, y, c).start()
        me = 4 * x + 2 * y + c
        mine = buf.at[me]
        mine[...] = jnp.zeros((8, 1024), f32)
        mine[0:1, :] = nm_ref[...]
        mine[1:2, :] = nmem_ref[...]
        mine[2:3, :] = nf_ref[...]
        mine[3:4, :] = gn_ref[...]
        for j in range(3):
            mine[4:5, pl.ds(j * CONV_W, CONV_W)] = cw_ref[j:j + 1, :]
        mine[4:5, pl.ds(3 * CONV_W, CONV_W)] = cb_ref[...]
        for j, r in enumerate((qg_ref, kg_ref, mqg_ref, mkg_ref)):
            mine[5:6, pl.ds(j * HD, HD)] = r[...]
        mine[5:6, pl.ds(256, 128)] = sk_ref[0:1, :]
        mine[5:6, pl.ds(384, 128)] = ls_ref[0:1, :]

        def peer_of(m):
            return (1 - x if m & 4 else x, 1 - y if m & 2 else y, 1 - c if m & 1 else c)

        for m in range(1, 8):
            _remote(mine, mine, ssem.at[m - 1], rsem.at[m - 1], peer_of(m)).start()
        for ld in loads:
            ld.wait()

        def chip_partial(j, slot):
            for e in range(n):
                half_copy(g, sib, hsem, e, j, slot, x, y, c).wait()
                part[e][slot] = (own[e][slot] + sib[e][slot]).astype(WIRE)

        scatter.start(part, st, xsem, chip_partial)
        for _, hand_on in scatter.relays:
            hand_on(part, st, xsem)
        scatter.finish(part, st, xsem)
        for m in range(1, 8):
            p = peer_of(m)
            got = buf.at[4 * p[0] + 2 * p[1] + p[2]]
            _remote(got, got, ssem.at[m - 1], rsem.at[m - 1], p).wait_recv()
        for m in range(1, 8):
            _remote(mine, mine, ssem.at[m - 1], rsem.at[m - 1], peer_of(m)).wait_send()
        acc = buf[0]
        for d in range(1, 8):
            acc = acc + buf[d]
        o_ref[...] = acc

    ins = [d_norm_mix, d_norm_mem, d_norm_ffn, d_gains, d_cw8, d_cbias, d_qg, d_kg, d_mqg, d_mkg, d_sink8, loss8]
    half_shape = [(4,) + a.shape[2:] for a in tail]
    scratch = ([pltpu.VMEM((8, 8, 1024), f32), DMA((7,)), DMA((7,))]
               + [pltpu.VMEM(s, f32) for s in half_shape] * 2 + [pltpu.VMEM(s, WIRE) for s in half_shape]
               + [DMA((n,)), DMA((4 * n,)), DMA((4 * n,))] + list(scatter.sems))
    res = _run("tail_reduce", body, (), ins + list(tail), [VM] * len(ins) + [ANY] * n,
               [SDS((8, 1024), f32)] + list(scatter.outs), [VM] + [ANY] * n, scratch=scratch, vmem_mib=40)
    return res[0], res[1:]


def add_halves(cidx, grads, stages, name, nch=2):
    n = len(grads)

    def body(c_ref, *refs):
        g, st, o = refs[:n], refs[n:2 * n], refs[2 * n:]
        for e in range(n):
            o[e][...] = (g[e][...] + st[e][...]).astype(WIRE)

    in_specs, out_specs, out_shape = [], [], []
    for a in grads:
        hr, C = a.shape[2], a.shape[3]
        in_specs.append(pl.BlockSpec((None, None, hr // nch, C), lambda s, q, c_ref: (s, c_ref[0], q, 0)))
    for a in stages:
        hr, C = a.shape[1], a.shape[2]
        in_specs.append(pl.BlockSpec((None, hr // nch, C), lambda s, q, c_ref: (s, q, 0)))
        out_specs.append(pl.BlockSpec((None, hr // nch, C), lambda s, q, c_ref: (s, q, 0)))
        out_shape.append(SDS(a.shape, WIRE))
    return pl.pallas_call(
        body, name=name, out_shape=out_shape,
        grid_spec=pltpu.PrefetchScalarGridSpec(num_scalar_prefetch=1, grid=(4, nch), in_specs=in_specs, out_specs=out_specs),
        compiler_params=pltpu.CompilerParams(dimension_semantics=("arbitrary", "arbitrary")),
    )(cidx, *grads, *stages)


def _adamw_math(w, g, m, v):
    m = ADAM_B1 * m + (1.0 - ADAM_B1) * g
    v = ADAM_B2 * v + (1.0 - ADAM_B2) * (g * g)
    m_hat = m / (1.0 - ADAM_B1 ** ADAM_STEP)
    v_hat = v / (1.0 - ADAM_B2 ** ADAM_STEP)
    delta = -ADAM_LR * (m_hat / (jnp.sqrt(v_hat) + ADAM_EPS) + ADAM_WD * w)
    return delta, m, v


def _sum_chips(st):
    return ((st[0].astype(f32) + st[1].astype(f32)) + st[2].astype(f32)) + st[3].astype(f32)


def adamw_big(name, stages, ws, ms, vs, nstep, exchange=None):
    n = len(stages)

    def body(*refs):
        st, w, m, v = refs[:n], refs[n:2 * n], refs[2 * n:3 * n], refs[3 * n:4 * n]
        outs = refs[4 * n:]
        for e in range(n):
            g = jnp.concatenate([_sum_chips(st[e].at[0]), _sum_chips(st[e].at[1])], axis=0)
            d, mm, vv = _adamw_math(w[e][...], g, m[e][...], v[e][...])
            outs[4 * e][...] = g
            outs[4 * e + 1][...] = d
            outs[4 * e + 2][...] = mm
            outs[4 * e + 3][...] = vv

    st_specs, w_specs = [], []
    for e in range(n):
        _, _, hr, C = stages[e].shape
        st_specs.append(pl.BlockSpec((2, 4, hr, C // nstep), lambda i: (0, 0, 0, i)))
        w_specs.append(pl.BlockSpec((2 * hr, C // nstep), lambda i: (0, i)))
    out_specs = [s for s in w_specs for _ in range(4)]
    out_shape = [SDS(w.shape, f32) for w in ws for _ in range(4)]
    res = _run(name, body, (nstep,), list(stages) + list(ws) + list(ms) + list(vs), st_specs + w_specs * 3,
               out_shape, out_specs, vmem_mib=48, exchange=exchange)
    res, sent = res if exchange is not None else (res, None)
    return [res[4 * e:4 * e + 4] for e in range(n)], sent


def adamw_small(tot, pk_w, pk_m, pk_v, shapes):
    def body(tot_ref, w_ref, m_ref, v_ref, *outs):
        x, y, _ = _place()
        chip = 2 * x + y
        taps = []
        for j in range(3):
            mine = tot_ref[4:5, j * CONV_W:j * CONV_W + HD]
            for s in range(1, 4):
                mine = jnp.where(chip == s, tot_ref[4:5, j * CONV_W + s * HD:j * CONV_W + (s + 1) * HD], mine)
            taps.append(mine)
        row4 = jnp.concatenate(taps + [jnp.zeros((1, 3 * CONV_W - 3 * HD), f32), tot_ref[4:5, 3 * CONV_W:]], axis=1)
        tot_v = tot_ref[...]
        row = lax.broadcasted_iota(jnp.int32, tot_v.shape, 0)
        g = jnp.where(row == 4, jnp.broadcast_to(row4, tot_v.shape), tot_v)
        d, mm, vv = _adamw_math(w_ref[...], g, m_ref[...], v_ref[...])
        for i, name in enumerate(SMALL):
            for k, val in enumerate((g, d, mm, vv)):
                if name == "conv_w":
                    outs[4 * i + k][...] = jnp.concatenate([val[4:5, j * HD:(j + 1) * HD] for j in range(3)], axis=0)[None]
                else:
                    r, c0, w = SMALL_AT[name]
                    outs[4 * i + k][...] = val[r:r + 1, c0:c0 + w]

    out_shape = [SDS(shapes[k], f32) for k in SMALL for _ in range(4)]
    res = _run("adamw_small", body, (), [tot, pk_w, pk_m, pk_v], [VM] * 4, out_shape, [VM] * len(out_shape))
    return {k: res[4 * i:4 * i + 4] for i, k in enumerate(SMALL)}


def prep_weights(name, shards, exchange=None):
    n = len(shards)

    def body(*refs):
        for e in range(n):
            refs[n + e][...] = _c(refs[e][...])

    return _run(name, body, (), shards, [VM] * n, [SDS(a.shape, MXU) for a in shards], [VM] * n, vmem_mib=48, exchange=exchange)


def mem_kv_fwd(mem2d, pk, wmkv):
    M, D = mem2d.shape

    def body(m_ref, pk_ref, w_ref, mn_ref, kv_ref, km_ref, vm_ref):
        m = m_ref[...]
        mn = _c(m * _rstd(m) * _small(pk_ref, "norm_mem"))
        mn_ref[...] = mn
        kv = _nn(mn, w_ref[...])
        kv_ref[...] = kv
        kk = kv[:, :MEM_W]
        km_ref[...] = _c(kk * _heads_rstd(kk) * _lanes(_small(pk_ref, "mem_k_norm"), MEM_W))
        vm_ref[...] = _c(kv[:, MEM_W:])

    return _run("mem_kv_fwd", body, (), [mem2d, pk, wmkv], [VM] * 3,
                [SDS((M, D), MXU), SDS((M, 2 * MEM_W), f32), SDS((M, MEM_W), MXU), SDS((M, MEM_W), MXU)], [VM] * 4)


QKV_W = ATT_W + 2 * KV_W + MEM_W


def in_proj_fwd(x2d, pk, winT, tm, exchange):
    T, D = x2d.shape
    P = winT.shape[0]

    def body(x_ref, pk_ref, w_ref, xn_ref, proj_ref, qkv_ref):
        xv = x_ref[...]
        xn = _c(xv * _rstd(xv) * _small(pk_ref, "norm_mix"))
        xn_ref[...] = xn
        proj = _nt(xn, w_ref[...])
        proj_ref[...] = proj
        q, k = proj[:, :ATT_W], proj[:, ATT_W:ATT_W + KV_W]
        qm = proj[:, P - MEM_W:]
        qkv_ref[...] = jnp.concatenate(
            [_c(q * _heads_rstd(q) * _lanes(_small(pk_ref, "q_norm"), ATT_W)),
             _c(k * _heads_rstd(k) * _lanes(_small(pk_ref, "k_norm"), KV_W)),
             _c(proj[:, ATT_W + KV_W:ATT_W + 2 * KV_W]),
             _c(qm * _heads_rstd(qm) * _lanes(_small(pk_ref, "mem_q_norm"), MEM_W))], axis=1)

    return _run("in_proj_fwd", body, (T // tm,), [x2d, pk, winT],
                [pl.BlockSpec((tm, D), lambda i: (i, 0)), VM, VM],
                [SDS((T, D), MXU), SDS((T, P), f32), SDS((T, QKV_W), MXU)],
                [pl.BlockSpec((tm, D), lambda i: (i, 0)), pl.BlockSpec((tm, P), lambda i: (i, 0)),
                 pl.BlockSpec((tm, QKV_W), lambda i: (i, 0))],
                vmem_mib=40, exchange=exchange)


def _swa_bias_table():
    r = np.arange(GQA * BLK)[:, None]
    k = np.arange(2 * BLK)[None, :]
    dist = (r % BLK) + BLK - k
    band = (dist >= 0) & (dist < BLK)
    tab = np.empty((2, N_KV, GQA * BLK, 2 * BLK), np.float32)
    for later in range(2):
        valid = band & ((k >= BLK) | (later == 1))
        for g in range(N_KV):
            slope = 2.0 ** -(g * GQA + r // BLK + 1.0)
            tab[later, g] = np.where(valid, -slope * dist, NEG)
    return jnp.asarray(tab)


def _sink_column(g, sk_ref):
    hrow = lax.broadcasted_iota(jnp.int32, (GQA * BLK, 1), 0) // BLK
    sink = jnp.zeros((GQA * BLK, 1), f32)
    for hh in range(GQA):
        sink = jnp.where(hrow == hh, sk_ref[g * GQA + hh:g * GQA + hh + 1, 0:1], sink)
    return sink


def _stack_heads(v, g):
    return jnp.concatenate([v[:, (g * GQA + hh) * HD:(g * GQA + hh + 1) * HD] for hh in range(GQA)], axis=0)


def attn_fwd(qkv, sink_rows, BL, S, exchange, qb=2):
    NS = S // (qb * BLK)
    T = BL * S

    def body(q_ref, kc_ref, kp_ref, vc_ref, vp_ref, sk_ref, tab_ref, o_ref):
        j = pl.program_id(1)
        kall = jnp.concatenate([kp_ref[...], kc_ref[...]], axis=0)
        vall = jnp.concatenate([vp_ref[...], vc_ref[...]], axis=0)
        ones = jnp.ones((2 * BLK, HD), MXU)
        for b in range(qb):
            q = q_ref[pl.ds(b * BLK, BLK), :]
            k2, v2 = kall[b * BLK:(b + 2) * BLK], vall[b * BLK:(b + 2) * BLK]
            later = jnp.minimum(j, 1) if b == 0 else 1
            for g in range(N_KV):
                kn, vh = k2[:, g * HD:(g + 1) * HD], v2[:, g * HD:(g + 1) * HD]
                s = _nt(_stack_heads(q, g), kn) * (HD ** -0.5) + tab_ref[later, g]
                e, es = _exp_scores(s, _sink_column(g, sk_ref))
                eb = _c(e)
                o = _nn(eb, vh) * (1.0 / (_nn(eb, ones) + es))
                for hh in range(GQA):
                    o_ref[pl.ds(b * BLK, BLK), pl.ds((g * GQA + hh) * HD, HD)] = o[hh * BLK:(hh + 1) * BLK]

    cur = lambda col: (lambda b, j: (b * NS + j, col))
    prev = lambda col: (lambda b, j: (qb * (b * NS + j) - jnp.minimum(j, 1), col))
    return _run("attn_fwd", body, (BL, NS), [qkv, qkv, qkv, qkv, qkv, sink_rows, _swa_bias_table()],
                [pl.BlockSpec((qb * BLK, ATT_W), cur(0)),
                 pl.BlockSpec((qb * BLK, KV_W), cur(4)), pl.BlockSpec((BLK, KV_W), prev(4)),
                 pl.BlockSpec((qb * BLK, KV_W), cur(5)), pl.BlockSpec((BLK, KV_W), prev(5)),
                 pl.BlockSpec((8, 128), lambda b, j: (0, 0)), VM],
                [SDS((T, ATT_W), f32)], [pl.BlockSpec((qb * BLK, ATT_W), cur(0))], exchange=exchange)


def _conv_taps(u, uh):
    row = lax.broadcasted_iota(jnp.int32, u.shape, 0)
    u1 = jnp.where(row == 0, uh[7:8, :], pltpu.roll(u, 1, 0))
    u2 = jnp.where(row == 0, uh[6:7, :], jnp.where(row == 1, uh[7:8, :], pltpu.roll(u, 2, 0)))
    return u1, u2


def _mem_head(qm, km, vm, h):
    qh, kh, vh = (a[:, h * HD:(h + 1) * HD] for a in (qm, km, vm))
    e, _ = _exp_scores(_nt(qh, kh) * (HD ** -0.5))
    return qh, kh, vh, e


def mixer_tail_fwd(x2d, attn_out, proj, qkv, km, vm, conv_w8, pk, wout, S, tm, exchange):
    T, D = x2d.shape
    NM = km.shape[0] // (T // S)

    def body(x_ref, ao_ref, ch_ref, cb_ref, cc_ref, chh_ref, cch_ref, qm_ref, km_ref, vm_ref, cw_ref, pk_ref,
             wout_ref, co_ref, mo_ref, mg_ref, x1_ref, h_ref):
        first = (pl.program_id(0) * tm) % S == 0
        u = cc_ref[...] * ch_ref[...]
        uh = jnp.where(first, 0.0, cch_ref[...] * chh_ref[...])
        u1, u2 = _conv_taps(u, uh)
        conv = cw_ref[0:1, :] * u2 + cw_ref[1:2, :] * u1 + cw_ref[2:3, :] * u + _small(pk_ref, "conv_b")
        conv_out = cb_ref[...] * conv
        co_ref[...] = conv_out
        qm, kmv, vmv = qm_ref[...], km_ref[...], vm_ref[...]
        ones = jnp.ones((NM, HD), MXU)
        for h in range(N_MEMH):
            _, _, vh, e = _mem_head(qm, kmv, vmv, h)
            eb = _c(e)
            mo_ref[:, pl.ds(h * HD, HD)] = _nn(eb, vh) * (1.0 / _nn(eb, ones))
        mem_out = mo_ref[...]
        ao = ao_ref[...]
        merged = _c(jnp.concatenate([ao * _rstd(ao) * _small(pk_ref, "out_norm_attn"),
                                     conv_out * _rstd(conv_out) * _small(pk_ref, "out_norm_conv"),
                                     mem_out * _rstd(mem_out) * _small(pk_ref, "out_norm_mem")], axis=1))
        mg_ref[...] = merged
        x1 = x_ref[...] + _nn(merged, wout_ref[...])
        x1_ref[...] = x1
        h_ref[...] = _c(x1 * _rstd(x1) * _small(pk_ref, "norm_ffn"))

    tile = lambda w, col: pl.BlockSpec((tm, w), lambda i: (i, col))
    halo = lambda col: pl.BlockSpec((8, CONV_W), lambda i: (jnp.maximum(i * (tm // 8) - 1, 0), col))
    seq = pl.BlockSpec((NM, MEM_W), lambda i: ((i * tm) // S, 0))
    small = lambda a: pl.BlockSpec(a.shape, lambda i: (0, 0))
    return _run("mixer_tail_fwd", body, (T // tm,),
                [x2d, attn_out, proj, proj, proj, proj, proj, qkv, km, vm, conv_w8, pk, wout],
                [tile(D, 0), tile(ATT_W, 0), tile(CONV_W, 3), tile(CONV_W, 4), tile(CONV_W, 5), halo(3), halo(5),
                 tile(MEM_W, 3), seq, seq, VM, VM, VM],
                [SDS((T, CONV_W), f32), SDS((T, MEM_W), f32), SDS((T, D), MXU), SDS((T, D), f32), SDS((T, D), MXU)],
                [tile(CONV_W, 0), tile(MEM_W, 0), tile(D, 0), tile(D, 0), tile(D, 0)], vmem_mib=40, exchange=exchange)


def ffn_fwd_bwd(h, x1, tgt, wgT, wuT, wd, pk, tm):
    T, D = x1.shape
    F = wd.shape[0]

    def body(h_ref, x1_ref, t_ref, wg_ref, wu_ref, wd_ref, pk_ref,
             dx1_ref, dx2_ref, act_ref, dg_ref, du_ref, loss_ref, dgf_ref):
        @pl.when(pl.program_id(0) == 0)
        def _():
            loss_ref[...] = jnp.zeros_like(loss_ref)
            dgf_ref[...] = jnp.zeros_like(dgf_ref)

        hv = h_ref[...]
        gate = _nt(hv, wg_ref[...])
        up = _nt(hv, wu_ref[...])
        sg = jax.nn.sigmoid(gate)
        sl = gate * sg
        act = _c(sl * up)
        act_ref[...] = act
        x1v = x1_ref[...]
        diff = (x1v + _nn(act, wd_ref[...])) - t_ref[...]
        loss_ref[...] += 0.5 * jnp.sum(jnp.sum(diff * diff, axis=-1, keepdims=True) / D, axis=0, keepdims=True)
        dx2 = diff / D
        dx2b = _c(dx2)
        dx2_ref[...] = dx2b
        d_act = _nt(dx2b, wd_ref[...])
        d_up = _c(d_act * sl)
        d_gate = _c(d_act * up * (sg * (1.0 + gate * (1.0 - sg))))
        du_ref[...] = d_up
        dg_ref[...] = d_gate
        dh = _nn(d_gate, wg_ref[...]) + _nn(d_up, wu_ref[...])
        dv, dgf = _norm_bwd(dh, x1v, _rstd(x1v), _small(pk_ref, "norm_ffn"))
        dx1_ref[...] = dx2 + dv
        dgf_ref[...] += dgf

    tile = lambda w: pl.BlockSpec((tm, w), lambda i: (i, 0))
    return _run("ffn_fwd_bwd", body, (T // tm,), [h, x1, tgt, wgT, wuT, wd, pk],
                [tile(D), tile(D), tile(D), VM, VM, VM, VM],
                [SDS((T, D), f32), SDS((T, D), MXU), SDS((T, F), MXU), SDS((T, F), MXU), SDS((T, F), MXU),
                 SDS((8, 128), f32), SDS((1, D), f32)],
                [tile(D), tile(D), tile(F), tile(F), tile(F), pl.BlockSpec((8, 128), lambda i: (0, 0)),
                 pl.BlockSpec((1, D), lambda i: (0, 0))], vmem_mib=56)


def matmul_tn(a, b, name, tmo, tk):
    T, M = a.shape
    N = b.shape[1]

    def body(a_ref, b_ref, o_ref):
        @pl.when(pl.program_id(1) == 0)
        def _():
            o_ref[...] = jnp.zeros_like(o_ref)

        o_ref[...] += _tn(a_ref[...], b_ref[...])

    return _run(name, body, (M // tmo, T // tk), [a, b],
                [pl.BlockSpec((tk, tmo), lambda m, k: (k, m)), pl.BlockSpec((tk, N), lambda m, k: (k, 0))],
                [SDS((M, N), f32)], [pl.BlockSpec((tmo, N), lambda m, k: (m, 0))], vmem_mib=48)[0]


def out_proj_bwd(dx1, merged, attn_out, conv_out, mem_out, pk, wout, tm):
    T, D = dx1.shape

    def body(dx1_ref, mg_ref, ao_ref, co_ref, mo_ref, pk_ref, w_ref,
             dao_ref, dco_ref, dmo_ref, dw_ref, dgain_ref):
        @pl.when(pl.program_id(0) == 0)
        def _():
            dw_ref[...] = jnp.zeros_like(dw_ref)
            dgain_ref[...] = jnp.zeros_like(dgain_ref)

        dxb = _c(dx1_ref[...])
        dw_ref[...] += _tn(mg_ref[...], dxb)
        dmg = _nt(dxb, w_ref[...])
        ao, co, mo = ao_ref[...], co_ref[...], mo_ref[...]
        da, ga = _norm_bwd(dmg[:, :ATT_W], ao, _rstd(ao), _small(pk_ref, "out_norm_attn"))
        dc, gc = _norm_bwd(dmg[:, ATT_W:ATT_W + CONV_W], co, _rstd(co), _small(pk_ref, "out_norm_conv"))
        dm, gm = _norm_bwd(dmg[:, ATT_W + CONV_W:], mo, _rstd(mo), _small(pk_ref, "out_norm_mem"))
        dao_ref[...] = da
        dco_ref[...] = dc
        dmo_ref[...] = dm
        dgain_ref[...] += jnp.concatenate([ga, gc, gm], axis=1)

    tile = lambda w: pl.BlockSpec((tm, w), lambda i: (i, 0))
    return _run("out_proj_bwd", body, (T // tm,), [dx1, merged, attn_out, conv_out, mem_out, pk, wout],
                [tile(D), tile(D), tile(ATT_W), tile(CONV_W), tile(MEM_W), VM, VM],
                [SDS((T, ATT_W), f32), SDS((T, CONV_W), f32), SDS((T, MEM_W), f32), SDS((D, D), f32), SDS((1, D), f32)],
                [tile(ATT_W), tile(CONV_W), tile(MEM_W), pl.BlockSpec((D, D), lambda i: (0, 0)),
                 pl.BlockSpec((1, D), lambda i: (0, 0))], vmem_mib=40)


def attn_bwd(qkv, d_attn, attn_out, sink_rows, BL, S, exchange):
    NB = S // BLK
    T = BL * S

    def body(q_ref, kc_ref, kp_ref, vc_ref, vp_ref, do_ref, ao_ref, sk_ref, tab_ref,
             dq_ref, dk_ref, dv_ref, dsk_ref, pend_k, pend_v):
        b, j = pl.program_id(0), pl.program_id(1)

        @pl.when((b == 0) & (j == 0))
        def _():
            dsk_ref[...] = jnp.zeros_like(dsk_ref)

        @pl.when(j == 0)
        def _():
            pend_k[...] = jnp.zeros_like(pend_k)
            pend_v[...] = jnp.zeros_like(pend_v)

        @pl.when(j < NB)
        def _():
            q, do, ao = q_ref[...], do_ref[...], ao_ref[...]
            k2 = jnp.concatenate([kp_ref[...], kc_ref[...]], axis=0)
            v2 = jnp.concatenate([vp_ref[...], vc_ref[...]], axis=0)
            lane = lax.broadcasted_iota(jnp.int32, (8, 128), 1)
            ones_w = jnp.ones((2 * BLK, 2 * BLK), MXU)
            dsk = jnp.zeros((8, 128), f32)
            dks, dvs = [], []
            for g in range(N_KV):
                kn, vh = k2[:, g * HD:(g + 1) * HD], v2[:, g * HD:(g + 1) * HD]
                qs = _stack_heads(q, g)
                s = _nt(qs, kn) * (HD ** -0.5) + tab_ref[g]
                e, es = _exp_scores(s, _sink_column(g, sk_ref))
                eb = _c(e)
                inv_w = 1.0 / (_nn(eb, ones_w) + es)
                inv_n = inv_w[:, :HD]
                dos = _stack_heads(do, g)
                delta = _rowsum_mxu(dos * _stack_heads(ao, g), 2 * BLK)
                dp = _nt(_c(dos), vh)
                ds = _c(e * inv_w * (dp - delta) * (HD ** -0.5))
                t = es * inv_n[:, 0:1] * delta[:, 0:1]
                for hh in range(GQA):
                    dsk = dsk + jnp.where(lane == g * GQA + hh, -jnp.sum(t[hh * BLK:(hh + 1) * BLK]), 0.0)
                dvs.append(_tn(eb, _c(dos * inv_n)))
                dks.append(_tn(ds, qs))
                dqs = _nn(ds, kn)
                for hh in range(GQA):
                    dq_ref[:, pl.ds((g * GQA + hh) * HD, HD)] = dqs[hh * BLK:(hh + 1) * BLK]
            dk2 = jnp.concatenate(dks, axis=1)
            dv2 = jnp.concatenate(dvs, axis=1)
            dk_ref[...] = pend_k[...] + dk2[:BLK]
            dv_ref[...] = pend_v[...] + dv2[:BLK]
            pend_k[...] = dk2[BLK:]
            pend_v[...] = dv2[BLK:]
            dsk_ref[...] += dsk

        @pl.when(j == NB)
        def _():
            dk_ref[...] = pend_k[...]
            dv_ref[...] = pend_v[...]

    cur = lambda col: (lambda b, j: (b * NB + jnp.minimum(j, NB - 1), col))
    prev = lambda col: (lambda b, j: (b * NB + jnp.maximum(j - 1, 0), col))
    small = lambda shape: pl.BlockSpec(shape, lambda b, j: (0, 0))
    return _run("attn_bwd", body, (BL, NB + 1), [qkv, qkv, qkv, qkv, qkv, d_attn, attn_out, sink_rows, _swa_bias_table()],
                [pl.BlockSpec((BLK, ATT_W), cur(0)),
                 pl.BlockSpec((BLK, KV_W), cur(4)), pl.BlockSpec((BLK, KV_W), prev(4)),
                 pl.BlockSpec((BLK, KV_W), cur(5)), pl.BlockSpec((BLK, KV_W), prev(5)),
                 pl.BlockSpec((BLK, ATT_W), cur(0)), pl.BlockSpec((BLK, ATT_W), cur(0)), small((8, 128)),
                 pl.BlockSpec((None, N_KV, GQA * BLK, 2 * BLK), lambda b, j: (jnp.minimum(j, 1), 0, 0, 0))],
                [SDS((T, ATT_W), f32), SDS((T, KV_W), f32), SDS((T, KV_W), f32), SDS((8, 128), f32)],
                [pl.BlockSpec((BLK, ATT_W), cur(0)), pl.BlockSpec((BLK, KV_W), prev(0)),
                 pl.BlockSpec((BLK, KV_W), prev(0)), small((8, 128))],
                scratch=[pltpu.VMEM((BLK, KV_W), f32)] * 2, exchange=exchange)


def mem_conv_bwd(d_mem_out, mem_out, d_conv_out, proj, qkv, km, vm, conv_w8, pk, S, tm, exchange):
    T = d_mem_out.shape[0]
    NM = km.shape[0] // (T // S)

    def body(dmo_ref, mo_ref, dco_ref, ch_ref, cb_ref, cc_ref, chh_ref, cch_ref, qm_ref, km_ref, vm_ref, cw_ref,
             pk_ref, dqm_ref, dkm_ref, dvm_ref, dcb_ref, dcv_ref, dcw_ref, dcbias_ref):
        i = pl.program_id(0)
        first = (i * tm) % S == 0

        @pl.when(i == 0)
        def _():
            dcw_ref[...] = jnp.zeros_like(dcw_ref)
            dcbias_ref[...] = jnp.zeros_like(dcbias_ref)

        @pl.when(first)
        def _():
            dkm_ref[...] = jnp.zeros_like(dkm_ref)
            dvm_ref[...] = jnp.zeros_like(dvm_ref)

        qm, kmv, vmv, dmo, mo = qm_ref[...], km_ref[...], vm_ref[...], dmo_ref[...], mo_ref[...]
        ones_w = jnp.ones((NM, NM), MXU)
        for h in range(N_MEMH):
            qh, kh, vh, e = _mem_head(qm, kmv, vmv, h)
            eb = _c(e)
            doh = dmo[:, h * HD:(h + 1) * HD]
            delta = _rowsum_mxu(doh * mo[:, h * HD:(h + 1) * HD], NM)
            dp = _nt(_c(doh), vh)
            inv_w = 1.0 / _nn(eb, ones_w)
            ds = _c(e * inv_w * (dp - delta) * (HD ** -0.5))
            dvm_ref[:, pl.ds(h * HD, HD)] += _tn(eb, _c(doh * inv_w[:, :HD]))
            dkm_ref[:, pl.ds(h * HD, HD)] += _tn(ds, qh)
            dqm_ref[:, pl.ds(h * HD, HD)] = _nn(ds, kh)

        u = cc_ref[...] * ch_ref[...]
        uh = jnp.where(first, 0.0, cch_ref[...] * chh_ref[...])
        u1, u2 = _conv_taps(u, uh)
        conv = cw_ref[0:1, :] * u2 + cw_ref[1:2, :] * u1 + cw_ref[2:3, :] * u + _small(pk_ref, "conv_b")
        dy = dco_ref[...]
        dcb_ref[...] = dy * conv
        dcv = dy * cb_ref[...]
        dcv_ref[...] = dcv
        dcbias_ref[...] += jnp.sum(dcv, axis=0, keepdims=True)
        dcw_ref[0:1, :] += jnp.sum(dcv * u2, axis=0, keepdims=True)
        dcw_ref[1:2, :] += jnp.sum(dcv * u1, axis=0, keepdims=True)
        dcw_ref[2:3, :] += jnp.sum(dcv * u, axis=0, keepdims=True)

    tile = lambda w, col: pl.BlockSpec((tm, w), lambda i: (i, col))
    halo = lambda col: pl.BlockSpec((8, CONV_W), lambda i: (jnp.maximum(i * (tm // 8) - 1, 0), col))
    seq = pl.BlockSpec((NM, MEM_W), lambda i: ((i * tm) // S, 0))
    const = lambda shape: pl.BlockSpec(shape, lambda i: (0, 0))
    return _run("mem_conv_bwd", body, (T // tm,),
                [d_mem_out, mem_out, d_conv_out, proj, proj, proj, proj, proj, qkv, km, vm, conv_w8, pk],
                [tile(MEM_W, 0), tile(MEM_W, 0), tile(CONV_W, 0), tile(CONV_W, 3), tile(CONV_W, 4), tile(CONV_W, 5),
                 halo(3), halo(5), tile(MEM_W, 3), seq, seq, VM, VM],
                [SDS((T, MEM_W), f32), SDS(km.shape, f32), SDS(km.shape, f32),
                 SDS((T, CONV_W), f32), SDS((T, CONV_W), f32), SDS((8, CONV_W), f32), SDS((1, CONV_W), f32)],
                [tile(MEM_W, 0), seq, seq, tile(CONV_W, 0), tile(CONV_W, 0), const((8, CONV_W)), const((1, CONV_W))],
                vmem_mib=48, exchange=exchange)


def in_proj_bwd(dqn, dkn, dv, dcb, dcv, dqmn, proj, conv_w8, xn, x2d, dx1, pk, winT, S, tm, stages, ws, ms, vs):
    T, D = x2d.shape
    P = winT.shape[0]
    last_blk = T // 8 - 1
    n = len(stages)
    nsteps = T // tm
    tile_w = ws[0].shape[1] // (nsteps // 2)
    turn = [e * 2 // n for e in range(n)]

    def body(dq_ref, dk_ref, dv_ref, dcb_ref, dcv_ref, dcvn_ref, dqm_ref, qa_ref, ka_ref, ch_ref, cc_ref, qma_ref,
             cw_ref, xn_ref, x_ref, dx1_ref, pk_ref, w_ref, *rest):
        st, aw, am, av = (rest[k * n:(k + 1) * n] for k in range(4))
        dx_ref, dw_ref, dg_ref, dqg_ref, dkg_ref, dmqg_ref = rest[4 * n:4 * n + 6]
        aouts = rest[4 * n + 6:]
        i = pl.program_id(0)

        for parity in range(2):
            @pl.when(i % 2 == parity)
            def _(parity=parity):
                for e in range(n):
                    if turn[e] == parity:
                        g = jnp.concatenate([_sum_chips(st[e].at[0]), _sum_chips(st[e].at[1])], axis=0)
                        d, mm, vv = _adamw_math(aw[e][...], g, am[e][...], av[e][...])
                        for k, val in enumerate((g, d, mm, vv)):
                            aouts[4 * e + k][...] = val

        @pl.when(i == 0)
        def _():
            dw_ref[...] = jnp.zeros_like(dw_ref)
            dg_ref[...] = jnp.zeros_like(dg_ref)
            dqg_ref[...] = jnp.zeros_like(dqg_ref)
            dkg_ref[...] = jnp.zeros_like(dkg_ref)
            dmqg_ref[...] = jnp.zeros_like(dmqg_ref)

        dqa, gq = _heads_norm_bwd(dq_ref[...], qa_ref[...], _small(pk_ref, "q_norm"))
        dka, gk = _heads_norm_bwd(dk_ref[...], ka_ref[...], _small(pk_ref, "k_norm"))
        dqma, gmq = _heads_norm_bwd(dqm_ref[...], qma_ref[...], _small(pk_ref, "mem_q_norm"))
        dqg_ref[...] += gq
        dkg_ref[...] += gk
        dmqg_ref[...] += gmq

        last = ((i + 1) * tm) % S == 0
        dcv = dcv_ref[...]
        nxt = jnp.where(last, 0.0, dcvn_ref[...])
        row = lax.broadcasted_iota(jnp.int32, dcv.shape, 0)
        n1 = jnp.where(row == tm - 1, nxt[0:1, :], pltpu.roll(dcv, tm - 1, 0))
        n2 = jnp.where(row == tm - 2, nxt[0:1, :], jnp.where(row == tm - 1, nxt[1:2, :], pltpu.roll(dcv, tm - 2, 0)))
        du = cw_ref[2:3, :] * dcv + cw_ref[1:2, :] * n1 + cw_ref[0:1, :] * n2
        d_proj = jnp.concatenate([_c(dqa), _c(dka), _c(dv_ref[...]), _c(du * cc_ref[...]),
                                  _c(dcb_ref[...]), _c(du * ch_ref[...]), _c(dqma)], axis=1)
        dw_ref[...] += _tn(d_proj, xn_ref[...])
        xv = x_ref[...]
        dv_, dg = _norm_bwd(_nn(d_proj, w_ref[...]), xv, _rstd(xv), _small(pk_ref, "norm_mix"))
        dx_ref[...] = dx1_ref[...] + dv_
        dg_ref[...] += dg

    tile = lambda w, col=0: pl.BlockSpec((tm, w), lambda i: (i, col))
    nhalo = pl.BlockSpec((8, CONV_W), lambda i: (jnp.minimum((i + 1) * (tm // 8), last_blk), 0))
    const = lambda shape: pl.BlockSpec(shape, lambda i: (0, 0))
    st_specs = [pl.BlockSpec((2, 4, s.shape[2], tile_w), lambda i: (0, 0, 0, i // 2)) for s in stages]
    w_specs = [pl.BlockSpec((w.shape[0], tile_w), lambda i: (0, i // 2)) for w in ws]
    res = _run("in_proj_bwd", body, (nsteps,),
               [dqn, dkn, dv, dcb, dcv, dcv, dqmn, proj, proj, proj, proj, proj, conv_w8, xn, x2d, dx1, pk, winT]
               + list(stages) + list(ws) + list(ms) + list(vs),
               [tile(ATT_W), tile(KV_W), tile(KV_W), tile(CONV_W), tile(CONV_W), nhalo, tile(MEM_W),
                tile(ATT_W, 0), tile(KV_W, 4), tile(CONV_W, 3), tile(CONV_W, 5), tile(MEM_W, 6), VM,
                tile(D), tile(D), tile(D), VM, VM] + st_specs + w_specs * 3,
               [SDS((T, D), f32), SDS((P, D), f32), SDS((1, D), f32), SDS((1, HD), f32), SDS((1, HD), f32),
                SDS((1, HD), f32)] + [SDS(w.shape, f32) for w in ws for _ in range(4)],
               [tile(D), pl.BlockSpec((P, D), lambda i: (0, 0)), const((1, D)), const((1, HD)), const((1, HD)),
                const((1, HD))] + [s for s in w_specs for _ in range(4)],
               vmem_mib=56)
    return res[:6], [res[6 + 4 * e:10 + 4 * e] for e in range(n)]


def mem_kv_bwd(dkm, dvm, kv, memn, mem2d, pk, wmkv):
    def body(dkm_ref, dvm_ref, kv_ref, mn_ref, m_ref, pk_ref, w_ref, dw_ref, dg_ref, dkg_ref):
        dkk, dkg = _heads_norm_bwd(dkm_ref[...], kv_ref[:, :MEM_W], _small(pk_ref, "mem_k_norm"))
        dkg_ref[...] = dkg
        dkv = _c(jnp.concatenate([dkk, dvm_ref[...]], axis=1))
        dw_ref[...] = _tn(mn_ref[...], dkv)
        mv = m_ref[...]
        dg_ref[...] = jnp.sum(_nt(dkv, w_ref[...]) * mv * _rstd(mv), axis=0, keepdims=True)

    return _run("mem_kv_bwd", body, (), [dkm, dvm, kv, memn, mem2d, pk, wmkv], [VM] * 7,
                [SDS(wmkv.shape, f32), SDS((1, mem2d.shape[1]), f32), SDS((1, HD), f32)], [VM] * 3, vmem_mib=40)


def _halves_view(g):
    return g.reshape(4, 2, g.shape[0] // 8, g.shape[1])


def kernel(x, mem, norm_mix, w_in, q_norm, k_norm, attn_sinks, conv_w, conv_b, norm_mem, w_mem_kv, mem_q_norm, mem_k_norm, out_norm_attn, out_norm_conv, out_norm_mem, w_out, norm_ffn, w_gate, w_up, w_down, loss_target, m_norm_mix, m_w_in, m_q_norm, m_k_norm, m_attn_sinks, m_conv_w, m_conv_b, m_norm_mem, m_w_mem_kv, m_mem_q_norm, m_mem_k_norm, m_out_norm_attn, m_out_norm_conv, m_out_norm_mem, m_w_out, m_norm_ffn, m_w_gate, m_w_up, m_w_down, v_norm_mix, v_w_in, v_q_norm, v_k_norm, v_attn_sinks, v_conv_w, v_conv_b, v_norm_mem, v_w_mem_kv, v_mem_q_norm, v_mem_k_norm, v_out_norm_attn, v_out_norm_conv, v_out_norm_mem, v_w_out, v_norm_ffn, v_w_gate, v_w_up, v_w_down):
    BL, S, D = x.shape
    T = BL * S
    TM = 256
    TM_BIG = min(512, S)
    _, _, ci = _place()
    cidx = ci.reshape(1).astype(jnp.int32)
    w_small = dict(norm_mix=norm_mix, norm_mem=norm_mem, norm_ffn=norm_ffn, out_norm_attn=out_norm_attn,
                   out_norm_conv=out_norm_conv, out_norm_mem=out_norm_mem, conv_w=conv_w, conv_b=conv_b, q_norm=q_norm,
                   k_norm=k_norm, mem_q_norm=mem_q_norm, mem_k_norm=mem_k_norm, attn_sinks=attn_sinks)
    m_small = dict(norm_mix=m_norm_mix, norm_mem=m_norm_mem, norm_ffn=m_norm_ffn, out_norm_attn=m_out_norm_attn,
                   out_norm_conv=m_out_norm_conv, out_norm_mem=m_out_norm_mem, conv_w=m_conv_w, conv_b=m_conv_b,
                   q_norm=m_q_norm, k_norm=m_k_norm, mem_q_norm=m_mem_q_norm, mem_k_norm=m_mem_k_norm,
                   attn_sinks=m_attn_sinks)
    v_small = dict(norm_mix=v_norm_mix, norm_mem=v_norm_mem, norm_ffn=v_norm_ffn, out_norm_attn=v_out_norm_attn,
                   out_norm_conv=v_out_norm_conv, out_norm_mem=v_out_norm_mem, conv_w=v_conv_w, conv_b=v_conv_b,
                   q_norm=v_q_norm, k_norm=v_k_norm, mem_q_norm=v_mem_q_norm, mem_k_norm=v_mem_k_norm,
                   attn_sinks=v_attn_sinks)
    pk = _pack_small(w_small)

    rowblocks = lambda a, b, c, d, e, f: [a[0].T, b[0].T, c[0].T, d[0], e[0], f[0]]
    w_rb = rowblocks(w_in, w_gate, w_up, w_down, w_out, w_mem_kv)
    m_rb = rowblocks(m_w_in, m_w_gate, m_w_up, m_w_down, m_w_out, m_w_mem_kv)
    v_rb = rowblocks(v_w_in, v_w_gate, v_w_up, v_w_down, v_w_out, v_w_mem_kv)
    (winT_s,) = prep_weights("prep_w_in", w_rb[:1])
    cw_pad = jnp.zeros((8, 128), f32).at[:3, :HD].set(conv_w[0])
    (wgT_s, wuT_s, wd_s, wout_s, wmkv_s), (winT, cw_all) = prep_weights(
        "gather_w_in", w_rb[1:], _together([gather_two_legs([winT_s]), gather_exchange([cw_pad], [False])]))
    conv_w_full = jnp.transpose(cw_all.reshape(4, 8, 128)[:, :3, :HD], (1, 0, 2)).reshape(3, CONV_W)
    conv_w8 = jnp.zeros((8, CONV_W), f32).at[:3].set(conv_w_full)
    sink_rows = jnp.broadcast_to(attn_sinks.reshape(N_Q, 1), (N_Q, 128))

    x2d = x.reshape(T, D)
    mem2d = mem.reshape(-1, D)
    (xn, proj, qkv), near1 = in_proj_fwd(x2d, pk, winT, TM_BIG, gather_near_exchange([wgT_s, wout_s, wmkv_s], relay_early=1))
    (attn_out,), (wgT, wout, wmkv, *near2) = attn_fwd(
        qkv, sink_rows, BL, S, _together([gather_far_exchange(near1, relay_early=4), gather_near_exchange([wuT_s, wd_s], relay_early=1)]))
    memn, kv, km, vm = mem_kv_fwd(mem2d, pk, wmkv)
    (conv_out, mem_out, merged, x1, h), (wuT, wd) = mixer_tail_fwd(
        x2d, attn_out, proj, qkv, km, vm, conv_w8, pk, wout, S, TM_BIG, gather_far_exchange(near2, relay_early=2))

    dx1, dx2b, act, d_gate, d_up, loss8, d_norm_ffn = ffn_fwd_bwd(h, x1, loss_target.reshape(T, D), wgT, wuT, wd, pk, TM)
    F = wd.shape[0]
    g_wd = matmul_tn(act, dx2b, "dw_down", F // 2, min(T, 1024))
    g_wgT = matmul_tn(d_gate, h, "dw_gate", F // 2, min(T, 1024))
    g_wuT = matmul_tn(d_up, h, "dw_up", F // 2, min(T, 1024))

    d_attn, d_conv_out, d_mem_out, g_wout, d_gains = out_proj_bwd(dx1, merged, attn_out, conv_out, mem_out, pk, wout, TM_BIG)
    late = [_halves_view(g) for g in (g_wgT, g_wuT, g_wd, g_wout)]
    (dqmn, dkm, dvm, dcb, dcv, d_cw8, d_cbias), late_sib = mem_conv_bwd(
        d_mem_out, mem_out, d_conv_out, proj, qkv, km, vm, conv_w8, pk, S, min(1024, S), halves_exchange(late))
    late_part = add_halves(cidx, late, late_sib, "grad_add_halves_ffn")
    (dqn, dkn, dv, d_sink8), late_stage = attn_bwd(qkv, d_attn, attn_out, sink_rows, BL, S,
                                                   scatter_exchange(late_part, relay_before_end=[21, 12, 3, 0]))
    (g_x, g_winT, d_norm_mix, d_qg, d_kg, d_mqg), late_res = in_proj_bwd(
        dqn, dkn, dv, dcb, dcv, dqmn, proj, conv_w8, xn, x2d, dx1, pk, winT, S, TM,
        late_stage, w_rb[1:5], m_rb[1:5], v_rb[1:5])
    g_wmkv, d_norm_mem, d_mkg = mem_kv_bwd(dkm, dvm, kv, memn, mem2d, pk, wmkv)

    tot, tail_stage = tail_reduce(d_norm_mix, d_norm_mem, d_norm_ffn, d_gains, d_cw8, d_cbias, d_qg, d_kg, d_mqg, d_mkg,
                                  d_sink8, loss8, [_halves_view(g) for g in (g_winT, g_wmkv)])
    loss = tot[5, 384]
    tail_res, _ = adamw_big("adamw_tail", tail_stage, [w_rb[0], w_rb[5]], [m_rb[0], m_rb[5]], [v_rb[0], v_rb[5]], 4)
    res = {"w_in": [a.T[None] for a in tail_res[0]], "w_gate": [a.T[None] for a in late_res[0]],
           "w_up": [a.T[None] for a in late_res[1]], "w_down": [a[None] for a in late_res[2]],
           "w_out": [a[None] for a in late_res[3]], "w_mem_kv": [a[None] for a in tail_res[1]]}
    res.update(adamw_small(tot, pk, _pack_small(m_small), _pack_small(v_small), {k: w_small[k].shape for k in SMALL}))

    order = ["norm_mix", "w_in", "q_norm", "k_norm", "attn_sinks", "conv_w", "conv_b", "norm_mem", "w_mem_kv",
             "mem_q_norm", "mem_k_norm", "out_norm_attn", "out_norm_conv", "out_norm_mem", "w_out", "norm_ffn",
             "w_gate", "w_up", "w_down"]
    return (loss, g_x.reshape(BL, S, D), *[res[n][0] for n in order], *[res[n][1] for n in order],
            *[res[n][2] for n in order], *[res[n][3] for n in order])
```

```python
import collections
import functools

import jax
import jax.numpy as jnp
import numpy as np
from jax import lax
from jax.experimental import pallas as pl
from jax.experimental.pallas import tpu as pltpu

f32 = jnp.float32
MXU = jnp.bfloat16
WIRE = jnp.bfloat16
EPS = 1e-6
NEG = -1e30
HD = 64
BLK = 128
N_Q, N_KV, N_MEMH = 8, 2, 4
GQA = N_Q // N_KV
ATT_W, KV_W, CONV_W, MEM_W = 512, 128, 256, 256
VMEM_MIB = 1024 * 1024
ADAM_LR, ADAM_B1, ADAM_B2, ADAM_EPS, ADAM_WD, ADAM_STEP = 0.001, 0.9, 0.999, 1e-08, 0.01, 10

MESH = pl.DeviceIdType.MESH
VM = pl.BlockSpec(memory_space=pltpu.VMEM)
ANY = pl.BlockSpec(memory_space=pl.ANY)
SDS = jax.ShapeDtypeStruct
DMA = pltpu.SemaphoreType.DMA


def _c(v):
    return v.astype(MXU)


def _nn(a, b):
    return lax.dot_general(a, b, (((1,), (0,)), ((), ())), preferred_element_type=f32)


def _nt(a, b):
    return lax.dot_general(a, b, (((1,), (1,)), ((), ())), preferred_element_type=f32)


def _tn(a, b):
    return lax.dot_general(a, b, (((0,), (0,)), ((), ())), preferred_element_type=f32)


def _rstd(v):
    return lax.rsqrt(jnp.mean(v * v, axis=-1, keepdims=True) + EPS)


def _norm_bwd(dy, v, r, g):
    dyg = dy * g
    dv = r * dyg - v * (r * r * r) * jnp.mean(dyg * v, axis=-1, keepdims=True)
    return dv, jnp.sum(dy * v * r, axis=0, keepdims=True)


def _split3(v):
    hi = _c(v)
    r1 = v - hi.astype(f32)
    mid = _c(r1)
    return hi, mid, _c(r1 - mid.astype(f32))


def _rowsum_mxu(v, width):
    ones = jnp.ones((v.shape[1], width), MXU)
    return sum(_nn(a, ones) for a in _split3(v))


def _seg_sums(v):
    r = lax.broadcasted_iota(jnp.int32, (2 * HD, 2 * HD), 0) // HD
    c = lax.broadcasted_iota(jnp.int32, (2 * HD, 2 * HD), 1) // HD
    bd = (r == c).astype(MXU)
    outs = []
    for b in range(v.shape[1] // (2 * HD)):
        outs.append(sum(_nn(a, bd) for a in _split3(v[:, b * 2 * HD:(b + 1) * 2 * HD])))
    return outs[0] if len(outs) == 1 else jnp.concatenate(outs, axis=1)


def _lanes(g, width):
    return jnp.concatenate([g] * (width // HD), axis=1)


def _heads_rstd(v):
    return lax.rsqrt(_seg_sums(v * v) * (1.0 / HD) + EPS)


def _heads_norm_bwd(dy, v, g):
    r = _heads_rstd(v)
    gl = _lanes(g, v.shape[1])
    dyg = dy * gl
    dv = r * dyg - v * (r * r * r) * (_seg_sums(dyg * v) * (1.0 / HD))
    dgl = jnp.sum(dy * v * r, axis=0, keepdims=True)
    return dv, sum(dgl[:, s * HD:(s + 1) * HD] for s in range(v.shape[1] // HD))


def _exp_scores(s, extra=None):
    m = jnp.max(s, axis=-1, keepdims=True)
    if extra is None:
        return jnp.exp(s - m), None
    m = jnp.maximum(m, extra)
    return jnp.exp(s - m), jnp.exp(extra - m)


def _place():
    return lax.axis_index("x"), lax.axis_index("y"), lax.axis_index("c")


SMALL_AT = {"norm_mix": (0, 0, 1024), "norm_mem": (1, 0, 1024), "norm_ffn": (2, 0, 1024),
            "out_norm_attn": (3, 0, ATT_W), "out_norm_conv": (3, ATT_W, CONV_W), "out_norm_mem": (3, ATT_W + CONV_W, MEM_W),
            "conv_b": (4, 3 * CONV_W, CONV_W), "q_norm": (5, 0, HD), "k_norm": (5, HD, HD), "mem_q_norm": (5, 2 * HD, HD),
            "mem_k_norm": (5, 3 * HD, HD), "attn_sinks": (5, 256, N_Q)}
SMALL = ("norm_mix", "norm_mem", "norm_ffn", "out_norm_attn", "out_norm_conv", "out_norm_mem", "conv_w", "conv_b",
         "q_norm", "k_norm", "mem_q_norm", "mem_k_norm", "attn_sinks")


def _small(pk_ref, name):
    r, c0, w = SMALL_AT[name]
    return pk_ref[r:r + 1, c0:c0 + w]


def _pack_small(d):
    z = lambda n: jnp.zeros((1, n), f32)
    row3 = jnp.concatenate([d["out_norm_attn"], d["out_norm_conv"], d["out_norm_mem"]], axis=1)
    row4 = jnp.concatenate([d["conv_w"].reshape(1, 3 * HD), z(3 * CONV_W - 3 * HD), d["conv_b"]], axis=1)
    row5 = jnp.concatenate([d["q_norm"], d["k_norm"], d["mem_q_norm"], d["mem_k_norm"], d["attn_sinks"],
                            z(1024 - 4 * HD - N_Q)], axis=1)
    return jnp.concatenate([d["norm_mix"], d["norm_mem"], d["norm_ffn"], row3, row4, row5, z(1024), z(1024)], axis=0)


def _other_chips(x, y):
    return [(1 - x, y), (x, 1 - y), (1 - x, 1 - y)]


Exchange = collections.namedtuple("Exchange", "ins outs sems start finish relays aliases", defaults=((), {}))


def _together(exchanges):
    def bounds(key):
        at, out = 0, []
        for ex in exchanges:
            out.append((at, at + len(getattr(ex, key))))
            at += len(getattr(ex, key))
        return out

    bi, bo, bs = bounds("ins"), bounds("outs"), bounds("sems")

    def of(i, fn):
        return lambda xa, xo, xs: fn(xa[bi[i][0]:bi[i][1]], xo[bo[i][0]:bo[i][1]], xs[bs[i][0]:bs[i][1]])

    def every(name):
        fns = [of(i, getattr(ex, name)) for i, ex in enumerate(exchanges)]

        def run(xa, xo, xs):
            for fn in fns:
                fn(xa, xo, xs)
        return run

    aliases = {}
    for i, ex in enumerate(exchanges):
        aliases.update({bi[i][0] + a: bo[i][0] + o for a, o in ex.aliases.items()})
    return Exchange([a for ex in exchanges for a in ex.ins], [o for ex in exchanges for o in ex.outs],
                    [s for ex in exchanges for s in ex.sems], every("start"), every("finish"),
                    [(sbe, of(i, fn)) for i, ex in enumerate(exchanges) for sbe, fn in ex.relays], aliases)


def _run(name, body, grid, ins, in_specs, out_shape, out_specs, scratch=(), vmem_mib=32, exchange=None):
    ins, in_specs, out_shape, out_specs, scratch = list(ins), list(in_specs), list(out_shape), list(out_specs), list(scratch)
    ni, no, ns = len(ins), len(out_shape), len(scratch)
    ex = exchange
    if ex is not None:
        nxi, nxo = len(ex.ins), len(ex.outs)

    def call_body(*refs):
        if ex is None:
            body(*refs)
            return
        a, xa = refs[:ni], refs[ni:ni + nxi]
        o, xo = refs[ni + nxi:ni + nxi + no], refs[ni + nxi + no:ni + nxi + no + nxo]
        s, xs = refs[ni + nxi + no + nxo:ni + nxi + no + nxo + ns], refs[ni + nxi + no + nxo + ns:]
        if grid:
            first = functools.reduce(jnp.logical_and, [pl.program_id(d) == 0 for d in range(len(grid))])
            last = functools.reduce(jnp.logical_and, [pl.program_id(d) == grid[d] - 1 for d in range(len(grid))])
            pl.when(first)(lambda: ex.start(xa, xo, xs))
            body(*a, *o, *s)
            nsteps = functools.reduce(lambda p, q: p * q, grid)
            for before_end, fn in ex.relays:
                at = np.unravel_index(max(nsteps - 1 - before_end, 0), grid)
                here = functools.reduce(jnp.logical_and, [pl.program_id(d) == int(at[d]) for d in range(len(grid))])
                pl.when(here)(functools.partial(fn, xa, xo, xs))
            pl.when(last)(lambda: ex.finish(xa, xo, xs))
        else:
            ex.start(xa, xo, xs)
            if body is not None:
                body(*a, *o, *s)
            for _, fn in ex.relays:
                fn(xa, xo, xs)
            ex.finish(xa, xo, xs)

    kw = dict(grid=grid) if grid else {}
    if ex is not None:
        if ex.aliases:
            kw["input_output_aliases"] = {ni + i: no + o for i, o in ex.aliases.items()}
        ins, in_specs = ins + list(ex.ins), in_specs + [ANY] * nxi
        out_shape, out_specs = out_shape + list(ex.outs), out_specs + [ANY] * nxo
        scratch = scratch + list(ex.sems)
    res = pl.pallas_call(
        call_body, name=name, out_shape=out_shape, in_specs=in_specs, out_specs=out_specs, scratch_shapes=scratch,
        compiler_params=pltpu.CompilerParams(dimension_semantics=("arbitrary",) * len(grid) if grid else None,
                                             vmem_limit_bytes=vmem_mib * VMEM_MIB), **kw)(*ins)
    res = list(res)
    return (res[:no], res[no:]) if ex is not None else res


def _remote(src, dst, ssem, rsem, dev):
    return pltpu.make_async_remote_copy(src_ref=src, dst_ref=dst, send_sem=ssem, recv_sem=rsem,
                                        device_id=dev, device_id_type=MESH)


def gather_exchange(shards, split, relay_early=0):
    n = len(shards)

    def rows(ref, e, kk, half=None):
        R = shards[e].shape[0]
        if half is None:
            return ref.at[pl.ds(pl.multiple_of(kk * R, 8), R)]
        return ref.at[pl.ds(pl.multiple_of(kk * R + half * (R // 2), 8), R // 2)]

    def ici(src, dst, sm, e, j, chip_j, x, y, c):
        k = 2 * x + y
        if split[e]:
            s = src[e].at[pl.ds(pl.multiple_of(c * (shards[e].shape[0] // 2), 8), shards[e].shape[0] // 2)]
            return _remote(s, rows(dst[e], e, k, c), sm[0].at[6 * e + j], sm[1].at[6 * e + j], (*chip_j, c))
        return _remote(src[e], rows(dst[e], e, k), sm[0].at[6 * e + j], sm[1].at[6 * e + j], (*chip_j, c))

    def landed(dst, e, chip_j, c):
        kj = 2 * chip_j[0] + chip_j[1]
        return rows(dst[e], e, kj, c) if split[e] else rows(dst[e], e, kj)

    def forward(dst, sm, e, j, chip_j, x, y, c, sender_c):
        kj = 2 * chip_j[0] + chip_j[1]
        r = rows(dst[e], e, kj, sender_c)
        return _remote(r, r, sm[0].at[6 * e + 3 + j], sm[1].at[6 * e + 3 + j], (x, y, 1 - c))

    def local(src, dst, sm, e, x, y):
        return pltpu.make_async_copy(src[e], rows(dst[e], e, 2 * x + y), sm[2].at[e])

    def start(src, dst, sm):
        x, y, c = _place()
        for e in range(n):
            local(src, dst, sm, e, x, y).start()
            for j, chip_j in enumerate(_other_chips(x, y)):
                ici(src, dst, sm, e, j, chip_j, x, y, c).start()

    def relay(src, dst, sm):
        x, y, c = _place()
        for e in range(n):
            for j, chip_j in enumerate(_other_chips(x, y)):
                r = landed(dst, e, chip_j, c)
                _remote(r, r, sm[0].at[6 * e + j], sm[1].at[6 * e + j], (*chip_j, c)).wait_recv()
                if split[e]:
                    forward(dst, sm, e, j, chip_j, x, y, c, c).start()

    def finish(src, dst, sm):
        x, y, c = _place()
        chips = _other_chips(x, y)
        for e in range(n):
            for j, chip_j in enumerate(chips):
                if split[e]:
                    forward(dst, sm, e, j, chip_j, x, y, c, 1 - c).wait_recv()
        for e in range(n):
            for j, chip_j in enumerate(chips):
                ici(src, dst, sm, e, j, chip_j, x, y, c).wait_send()
                if split[e]:
                    forward(dst, sm, e, j, chip_j, x, y, c, c).wait_send()
            local(src, dst, sm, e, x, y).wait()

    outs = [SDS((4 * s.shape[0], s.shape[1]), s.dtype) for s in shards]
    return Exchange(list(shards), outs, [DMA((6 * n,)), DMA((6 * n,)), DMA((n,))], start, finish, [(relay_early, relay)])


def _block_rows(ref, R, kk, half, quarter=None):
    hr = R // 2
    if quarter is None:
        return ref.at[pl.ds(pl.multiple_of(kk * R + half * hr, 8), hr)]
    return ref.at[pl.ds(pl.multiple_of(kk * R + half * hr + quarter * (hr // 2), 8), hr // 2)]


def gather_near_exchange(shards, relay_early=0):
    n = len(shards)
    R = [s.shape[0] for s in shards]

    def ici(src, dst, sm, e, j, chip_j, x, y, c):
        half = src[e].at[pl.ds(pl.multiple_of(c * (R[e] // 2), 8), R[e] // 2)]
        return _remote(half, _block_rows(dst[e], R[e], 2 * x + y, c), sm[0].at[4 * e + j], sm[1].at[4 * e + j], (*chip_j, c))

    def forward(dst, sm, e, j, chip_j, x, y, c, sender_c):
        r = _block_rows(dst[e], R[e], 2 * chip_j[0] + chip_j[1], sender_c)
        return _remote(r, r, sm[0].at[4 * e + 2 + j], sm[1].at[4 * e + 2 + j], (x, y, 1 - c))

    def local(src, dst, sm, e, x, y):
        return pltpu.make_async_copy(src[e], dst[e].at[pl.ds(pl.multiple_of((2 * x + y) * R[e], 8), R[e])], sm[2].at[e])

    def start(src, dst, sm):
        x, y, c = _place()
        for e in range(n):
            local(src, dst, sm, e, x, y).start()
            for j, chip_j in enumerate(_other_chips(x, y)[:2]):
                ici(src, dst, sm, e, j, chip_j, x, y, c).start()

    def relay(src, dst, sm):
        x, y, c = _place()
        for e in range(n):
            for j, chip_j in enumerate(_other_chips(x, y)[:2]):
                r = _block_rows(dst[e], R[e], 2 * chip_j[0] + chip_j[1], c)
                _remote(r, r, sm[0].at[4 * e + j], sm[1].at[4 * e + j], (*chip_j, c)).wait_recv()
                forward(dst, sm, e, j, chip_j, x, y, c, c).start()

    def finish(src, dst, sm):
        x, y, c = _place()
        near = _other_chips(x, y)[:2]
        for e in range(n):
            for j, chip_j in enumerate(near):
                forward(dst, sm, e, j, chip_j, x, y, c, 1 - c).wait_recv()
        for e in range(n):
            for j, chip_j in enumerate(near):
                ici(src, dst, sm, e, j, chip_j, x, y, c).wait_send()
                forward(dst, sm, e, j, chip_j, x, y, c, c).wait_send()
            local(src, dst, sm, e, x, y).wait()

    outs = [SDS((4 * s.shape[0], s.shape[1]), s.dtype) for s in shards]
    return Exchange(list(shards), outs, [DMA((4 * n,)), DMA((4 * n,)), DMA((n,))], start, finish, [(relay_early, relay)])


def gather_far_exchange(bufs, relay_early=0):
    n = len(bufs)
    R = [b.shape[0] // 4 for b in bufs]

    def send(src, dst, sm, e, j, x, y, c):
        to, of = _other_chips(x, y)[j], _other_chips(x, y)[1 - j]
        kk = 2 * of[0] + of[1]
        return _remote(_block_rows(src[e], R[e], kk, c, j), _block_rows(dst[e], R[e], kk, c, j),
                       sm[0].at[4 * e + j], sm[1].at[4 * e + j], (*to, c))

    def landed(dst, e, j, x, y, half):
        return _block_rows(dst[e], R[e], 2 * (1 - x) + (1 - y), half, j)

    def forward(dst, sm, e, j, x, y, c, sender_c):
        r = landed(dst, e, j, x, y, sender_c)
        return _remote(r, r, sm[0].at[4 * e + 2 + j], sm[1].at[4 * e + 2 + j], (x, y, 1 - c))

    def start(src, dst, sm):
        x, y, c = _place()
        for e in range(n):
            for j in range(2):
                send(src, dst, sm, e, j, x, y, c).start()

    def relay(src, dst, sm):
        x, y, c = _place()
        for e in range(n):
            for j in range(2):
                r = landed(dst, e, j, x, y, c)
                _remote(r, r, sm[0].at[4 * e + j], sm[1].at[4 * e + j], (*_other_chips(x, y)[j], c)).wait_recv()
                forward(dst, sm, e, j, x, y, c, c).start()

    def finish(src, dst, sm):
        x, y, c = _place()
        for e in range(n):
            for j in range(2):
                forward(dst, sm, e, j, x, y, c, 1 - c).wait_recv()
        for e in range(n):
            for j in range(2):
                send(src, dst, sm, e, j, x, y, c).wait_send()
                forward(dst, sm, e, j, x, y, c, c).wait_send()

    outs = [SDS(b.shape, b.dtype) for b in bufs]
    return Exchange(list(bufs), outs, [DMA((4 * n,)), DMA((4 * n,))], start, finish, [(relay_early, relay)],
                    {i: i for i in range(n)})


def gather_two_legs(shards):
    near = gather_near_exchange(shards)
    far = gather_far_exchange(near.outs)

    def finish(src, dst, sm):
        near.relays[0][1](src, dst, sm[:3])
        near.finish(src, dst, sm[:3])
        far.start(dst, dst, sm[3:])
        far.relays[0][1](dst, dst, sm[3:])
        far.finish(dst, dst, sm[3:])

    return Exchange(near.ins, near.outs, list(near.sems) + list(far.sems),
                    lambda src, dst, sm: near.start(src, dst, sm[:3]), finish)


def halves_exchange(grads):
    n = len(grads)

    def copy(g, st, sm, e, x, y, c):
        return _remote(g[e].at[:, 1 - c], st[e], sm[0].at[e], sm[1].at[e], (x, y, 1 - c))

    def start(g, st, sm):
        x, y, c = _place()
        for e in range(n):
            copy(g, st, sm, e, x, y, c).start()

    def finish(g, st, sm):
        x, y, c = _place()
        for e in range(n):
            copy(g, st, sm, e, x, y, c).wait()

    outs = [SDS((4,) + a.shape[2:], a.dtype) for a in grads]
    return Exchange(list(grads), outs, [DMA((n,)), DMA((n,))], start, finish)


def scatter_exchange(parts, relay_before_end=None, want_issue=False):
    n = len(parts)
    by_entry = relay_before_end is not None
    relay_before_end = relay_before_end or [0] * n

    def ici(p, st, sm, e, j, chip_j, x, y, c):
        k, kj = 2 * x + y, 2 * chip_j[0] + chip_j[1]
        return _remote(p[e].at[kj], st[e].at[c, k], sm[0].at[8 * e + j], sm[1].at[8 * e + j], (*chip_j, c))

    def own(p, st, sm, e, x, y, c):
        k = 2 * x + y
        return _remote(p[e].at[k], st[e].at[c, k], sm[0].at[8 * e + 3], sm[1].at[8 * e + 3], (x, y, 1 - c))

    def forward(st, sm, e, j, chip_j, x, y, c, sender_c):
        kj = 2 * chip_j[0] + chip_j[1]
        r = st[e].at[sender_c, kj]
        return _remote(r, r, sm[0].at[8 * e + 4 + j], sm[1].at[8 * e + 4 + j], (x, y, 1 - c))

    def local(p, st, sm, e, x, y, c):
        k = 2 * x + y
        return pltpu.make_async_copy(p[e].at[k], st[e].at[c, k], sm[2].at[e])

    def issue(e, p, st, sm):
        x, y, c = _place()
        for j, chip_j in enumerate(_other_chips(x, y)):
            ici(p, st, sm, e, j, chip_j, x, y, c).start()
        local(p, st, sm, e, x, y, c).start()
        own(p, st, sm, e, x, y, c).start()

    def start(p, st, sm, before_slot=None):
        x, y, c = _place()
        if by_entry:
            for e in range(n):
                issue(e, p, st, sm)
            return
        for j, chip_j in enumerate(_other_chips(x, y)):
            if before_slot is not None:
                before_slot(j, 2 * chip_j[0] + chip_j[1])
            for e in range(n):
                ici(p, st, sm, e, j, chip_j, x, y, c).start()
        if before_slot is not None:
            before_slot(3, 2 * x + y)
        for e in range(n):
            local(p, st, sm, e, x, y, c).start()
            own(p, st, sm, e, x, y, c).start()

    def relay(e, p, st, sm):
        x, y, c = _place()
        for j, chip_j in enumerate(_other_chips(x, y)):
            kj = 2 * chip_j[0] + chip_j[1]
            r = st[e].at[c, kj]
            _remote(r, r, sm[0].at[8 * e + j], sm[1].at[8 * e + j], (*chip_j, c)).wait_recv()
            forward(st, sm, e, j, chip_j, x, y, c, c).start()

    def finish(p, st, sm):
        x, y, c = _place()
        k = 2 * x + y
        chips = _other_chips(x, y)
        for e in range(n):
            r = st[e].at[1 - c, k]
            _remote(r, r, sm[0].at[8 * e + 3], sm[1].at[8 * e + 3], (x, y, 1 - c)).wait_recv()
            for j, chip_j in enumerate(chips):
                forward(st, sm, e, j, chip_j, x, y, c, 1 - c).wait_recv()
        for e in range(n):
            own(p, st, sm, e, x, y, c).wait_send()
            for j, chip_j in enumerate(chips):
                ici(p, st, sm, e, j, chip_j, x, y, c).wait_send()
                forward(st, sm, e, j, chip_j, x, y, c, c).wait_send()
            local(p, st, sm, e, x, y, c).wait()

    outs = [SDS((2,) + a.shape, a.dtype) for a in parts]
    ex = Exchange(list(parts), outs, [DMA((8 * n,)), DMA((8 * n,)), DMA((n,))], start, finish,
                  [(relay_before_end[e], functools.partial(relay, e)) for e in range(n)])
    return (ex, issue) if want_issue else ex


def reduce_scatter_exchange(grads, sibs, nsteps, load_step, send_step, relay_step):
    n = len(grads)
    hrs = [g.shape[2] for g in grads]
    C = grads[0].shape[3]
    scatter, issue = scatter_exchange([SDS((4,) + g.shape[2:], WIRE) for g in grads], want_issue=True)
    hand_on = [fn for _, fn in scatter.relays]

    def refs(xa, xs):
        return xa[:n], xa[n:], xs[:3], xs[3], xs[4], xs[5], xs[6:]

    def loads(e, g, sb, lsem, own_st, sib_st):
        _, _, c = _place()
        into = lambda st: st.at[e % 2, :, pl.ds(0, hrs[e])]
        return (pltpu.make_async_copy(g[e].at[:, c], into(own_st), lsem.at[2 * e]),
                pltpu.make_async_copy(sb[e], into(sib_st), lsem.at[2 * e + 1]))

    def load(e, xa, xo, xs):
        g, sb, _, lsem, own_st, sib_st, _ = refs(xa, xs)
        for cp in loads(e, g, sb, lsem, own_st, sib_st):
            cp.start()

    def send(e, xa, xo, xs):
        g, sb, sm, lsem, own_st, sib_st, part = refs(xa, xs)
        for cp in loads(e, g, sb, lsem, own_st, sib_st):
            cp.wait()
        part[e][...] = (own_st[e % 2, :, 0:hrs[e]] + sib_st[e % 2, :, 0:hrs[e]]).astype(WIRE)
        issue(e, part, xo, sm)

    def relay(e, xa, xo, xs):
        _, _, sm, _, _, _, part = refs(xa, xs)
        hand_on[e](part, xo, sm)

    def finish(xa, xo, xs):
        _, _, sm, _, _, _, part = refs(xa, xs)
        scatter.finish(part, xo, sm)

    plan = sorted([(min(step[e], nsteps - 1), phase, e) for phase, step in enumerate((load_step, send_step, relay_step))
                   for e in range(n)])
    stage = (load, send, relay)
    relays = [(nsteps - 1 - at, functools.partial(stage[phase], e)) for at, phase, e in plan]
    scratch = (list(scatter.sems) + [DMA((2 * n,))] + [pltpu.VMEM((2, 4, max(hrs), C), f32)] * 2
               + [pltpu.VMEM((4, hr, C), WIRE) for hr in hrs])
    return Exchange(list(grads) + list(sibs), scatter.outs, scratch, lambda xa, xo, xs: None, finish, relays)


def tail_reduce(d_norm_mix, d_norm_mem, d_norm_ffn, d_gains, d_cw8, d_cbias, d_qg, d_kg, d_mqg, d_mkg, d_sink8, loss8, tail):
    n = len(tail)
    scatter = scatter_exchange([SDS((4,) + a.shape[2:], WIRE) for a in tail])

    def half_copy(g, sib, hsem, e, j, slot, x, y, c):
        return _remote(g[e].at[slot, 1 - c], sib[e].at[slot], hsem[0].at[4 * e + j], hsem[1].at[4 * e + j], (x, y, 1 - c))

    def body(nm_ref, nmem_ref, nf_ref, gn_ref, cw_ref, cb_ref, qg_ref, kg_ref, mqg_ref, mkg_ref, sk_ref, ls_ref, *rest):
        g, o_ref, st = rest[:n], rest[n], rest[n + 1:2 * n + 1]
        buf, ssem, rsem = rest[2 * n + 1:2 * n + 4]
        own, sib, part = (rest[2 * n + 4 + i * n:2 * n + 4 + (i + 1) * n] for i in range(3))
        lsem = rest[5 * n + 4]
        hsem, xsem = rest[5 * n + 5:5 * n + 7], rest[5 * n + 7:]
        x, y, c = _place()
        loads = [pltpu.make_async_copy(g[e].at[:, c], own[e], lsem.at[e]) for e in range(n)]
        for ld in loads:
            ld.start()
        for j, slot in enumerate([2 * cx + cy for cx, cy in _other_chips(x, y)] + [2 * x + y]):
            for e in range(n):
                half_copy(g, sib, hsem, e, j, slot, x, y, c).start()
        me = 4 * x + 2 * y + c
        mine = buf.at[me]
        mine[...] = jnp.zeros((8, 1024), f32)
        mine[0:1, :] = nm_ref[...]
        mine[1:2, :] = nmem_ref[...]
        mine[2:3, :] = nf_ref[...]
        mine[3:4, :] = gn_ref[...]
        for j in range(3):
            mine[4:5, pl.ds(j * CONV_W, CONV_W)] = cw_ref[j:j + 1, :]
        mine[4:5, pl.ds(3 * CONV_W, CONV_W)] = cb_ref[...]
        for j, r in enumerate((qg_ref, kg_ref, mqg_ref, mkg_ref)):
            mine[5:6, pl.ds(j * HD, HD)] = r[...]
        mine[5:6, pl.ds(256, 128)] = sk_ref[0:1, :]
        mine[5:6, pl.ds(384, 128)] = ls_ref[0:1, :]

        def peer_of(m):
            return (1 - x if m & 4 else x, 1 - y if m & 2 else y, 1 - c if m & 1 else c)

        for m in range(1, 8):
            _remote(mine, mine, ssem.at[m - 1], rsem.at[m - 1], peer_of(m)).start()
        for ld in loads:
            ld.wait()

        def chip_partial(j, slot):
            for e in range(n):
                half_copy(g, sib, hsem, e, j, slot, x, y, c).wait()
                part[e][slot] = (own[e][slot] + sib[e][slot]).astype(WIRE)

        scatter.start(part, st, xsem, chip_partial)
        for _, hand_on in scatter.relays:
            hand_on(part, st, xsem)
        scatter.finish(part, st, xsem)
        for m in range(1, 8):
            p = peer_of(m)
            got = buf.at[4 * p[0] + 2 * p[1] + p[2]]
            _remote(got, got, ssem.at[m - 1], rsem.at[m - 1], p).wait_recv()
        for m in range(1, 8):
            _remote(mine, mine, ssem.at[m - 1], rsem.at[m - 1], peer_of(m)).wait_send()
        acc = buf[0]
        for d in range(1, 8):
            acc = acc + buf[d]
        o_ref[...] = acc

    ins = [d_norm_mix, d_norm_mem, d_norm_ffn, d_gains, d_cw8, d_cbias, d_qg, d_kg, d_mqg, d_mkg, d_sink8, loss8]
    half_shape = [(4,) + a.shape[2:] for a in tail]
    scratch = ([pltpu.VMEM((8, 8, 1024), f32), DMA((7,)), DMA((7,))]
               + [pltpu.VMEM(s, f32) for s in half_shape] * 2 + [pltpu.VMEM(s, WIRE) for s in half_shape]
               + [DMA((n,)), DMA((4 * n,)), DMA((4 * n,))] + list(scatter.sems))
    res = _run("tail_reduce", body, (), ins + list(tail), [VM] * len(ins) + [ANY] * n,
               [SDS((8, 1024), f32)] + list(scatter.outs), [VM] + [ANY] * n, scratch=scratch, vmem_mib=40)
    return res[0], res[1:]


def _adamw_math(w, g, m, v):
    m = ADAM_B1 * m + (1.0 - ADAM_B1) * g
    v = ADAM_B2 * v + (1.0 - ADAM_B2) * (g * g)
    m_hat = m / (1.0 - ADAM_B1 ** ADAM_STEP)
    v_hat = v / (1.0 - ADAM_B2 ** ADAM_STEP)
    delta = -ADAM_LR * (m_hat / (jnp.sqrt(v_hat) + ADAM_EPS) + ADAM_WD * w)
    return delta, m, v


def _sum_chips(st):
    return ((st[0].astype(f32) + st[1].astype(f32)) + st[2].astype(f32)) + st[3].astype(f32)


def adamw_big(name, stages, ws, ms, vs, nstep, exchange=None):
    n = len(stages)

    def body(*refs):
        st, w, m, v = refs[:n], refs[n:2 * n], refs[2 * n:3 * n], refs[3 * n:4 * n]
        outs = refs[4 * n:]
        for e in range(n):
            g = jnp.concatenate([_sum_chips(st[e].at[0]), _sum_chips(st[e].at[1])], axis=0)
            d, mm, vv = _adamw_math(w[e][...], g, m[e][...], v[e][...])
            outs[4 * e][...] = g
            outs[4 * e + 1][...] = d
            outs[4 * e + 2][...] = mm
            outs[4 * e + 3][...] = vv

    st_specs, w_specs = [], []
    for e in range(n):
        _, _, hr, C = stages[e].shape
        st_specs.append(pl.BlockSpec((2, 4, hr, C // nstep), lambda i: (0, 0, 0, i)))
        w_specs.append(pl.BlockSpec((2 * hr, C // nstep), lambda i: (0, i)))
    out_specs = [s for s in w_specs for _ in range(4)]
    out_shape = [SDS(w.shape, f32) for w in ws for _ in range(4)]
    res = _run(name, body, (nstep,), list(stages) + list(ws) + list(ms) + list(vs), st_specs + w_specs * 3,
               out_shape, out_specs, vmem_mib=48, exchange=exchange)
    res, sent = res if exchange is not None else (res, None)
    return [res[4 * e:4 * e + 4] for e in range(n)], sent


def adamw_small(tot, pk_w, pk_m, pk_v, shapes):
    def body(tot_ref, w_ref, m_ref, v_ref, *outs):
        x, y, _ = _place()
        chip = 2 * x + y
        taps = []
        for j in range(3):
            mine = tot_ref[4:5, j * CONV_W:j * CONV_W + HD]
            for s in range(1, 4):
                mine = jnp.where(chip == s, tot_ref[4:5, j * CONV_W + s * HD:j * CONV_W + (s + 1) * HD], mine)
            taps.append(mine)
        row4 = jnp.concatenate(taps + [jnp.zeros((1, 3 * CONV_W - 3 * HD), f32), tot_ref[4:5, 3 * CONV_W:]], axis=1)
        tot_v = tot_ref[...]
        row = lax.broadcasted_iota(jnp.int32, tot_v.shape, 0)
        g = jnp.where(row == 4, jnp.broadcast_to(row4, tot_v.shape), tot_v)
        d, mm, vv = _adamw_math(w_ref[...], g, m_ref[...], v_ref[...])
        for i, name in enumerate(SMALL):
            for k, val in enumerate((g, d, mm, vv)):
                if name == "conv_w":
                    outs[4 * i + k][...] = jnp.concatenate([val[4:5, j * HD:(j + 1) * HD] for j in range(3)], axis=0)[None]
                else:
                    r, c0, w = SMALL_AT[name]
                    outs[4 * i + k][...] = val[r:r + 1, c0:c0 + w]

    out_shape = [SDS(shapes[k], f32) for k in SMALL for _ in range(4)]
    res = _run("adamw_small", body, (), [tot, pk_w, pk_m, pk_v], [VM] * 4, out_shape, [VM] * len(out_shape))
    return {k: res[4 * i:4 * i + 4] for i, k in enumerate(SMALL)}


def prep_weights(name, shards, exchange=None):
    n = len(shards)

    def body(*refs):
        for e in range(n):
            refs[n + e][...] = _c(refs[e][...])

    return _run(name, body, (), shards, [VM] * n, [SDS(a.shape, MXU) for a in shards], [VM] * n, vmem_mib=48, exchange=exchange)


def mem_kv_fwd(mem2d, pk, wmkv):
    M, D = mem2d.shape

    def body(m_ref, pk_ref, w_ref, mn_ref, kv_ref, km_ref, vm_ref):
        m = m_ref[...]
        mn = _c(m * _rstd(m) * _small(pk_ref, "norm_mem"))
        mn_ref[...] = mn
        kv = _nn(mn, w_ref[...])
        kv_ref[...] = kv
        kk = kv[:, :MEM_W]
        km_ref[...] = _c(kk * _heads_rstd(kk) * _lanes(_small(pk_ref, "mem_k_norm"), MEM_W))
        vm_ref[...] = _c(kv[:, MEM_W:])

    return _run("mem_kv_fwd", body, (), [mem2d, pk, wmkv], [VM] * 3,
                [SDS((M, D), MXU), SDS((M, 2 * MEM_W), f32), SDS((M, MEM_W), MXU), SDS((M, MEM_W), MXU)], [VM] * 4)


QKV_W = ATT_W + 2 * KV_W + MEM_W


def in_proj_fwd(x2d, pk, winT, tm, exchange):
    T, D = x2d.shape
    P = winT.shape[0]

    def body(x_ref, pk_ref, w_ref, xn_ref, proj_ref, qkv_ref):
        xv = x_ref[...]
        xn = _c(xv * _rstd(xv) * _small(pk_ref, "norm_mix"))
        xn_ref[...] = xn
        proj = _nt(xn, w_ref[...])
        proj_ref[...] = proj
        q, k = proj[:, :ATT_W], proj[:, ATT_W:ATT_W + KV_W]
        qm = proj[:, P - MEM_W:]
        qkv_ref[...] = jnp.concatenate(
            [_c(q * _heads_rstd(q) * _lanes(_small(pk_ref, "q_norm"), ATT_W)),
             _c(k * _heads_rstd(k) * _lanes(_small(pk_ref, "k_norm"), KV_W)),
             _c(proj[:, ATT_W + KV_W:ATT_W + 2 * KV_W]),
             _c(qm * _heads_rstd(qm) * _lanes(_small(pk_ref, "mem_q_norm"), MEM_W))], axis=1)

    return _run("in_proj_fwd", body, (T // tm,), [x2d, pk, winT],
                [pl.BlockSpec((tm, D), lambda i: (i, 0)), VM, VM],
                [SDS((T, D), MXU), SDS((T, P), f32), SDS((T, QKV_W), MXU)],
                [pl.BlockSpec((tm, D), lambda i: (i, 0)), pl.BlockSpec((tm, P), lambda i: (i, 0)),
                 pl.BlockSpec((tm, QKV_W), lambda i: (i, 0))],
                vmem_mib=40, exchange=exchange)


def _swa_bias_table():
    r = np.arange(GQA * BLK)[:, None]
    k = np.arange(2 * BLK)[None, :]
    dist = (r % BLK) + BLK - k
    band = (dist >= 0) & (dist < BLK)
    tab = np.empty((2, N_KV, GQA * BLK, 2 * BLK), np.float32)
    for later in range(2):
        valid = band & ((k >= BLK) | (later == 1))
        for g in range(N_KV):
            slope = 2.0 ** -(g * GQA + r // BLK + 1.0)
            tab[later, g] = np.where(valid, -slope * dist, NEG)
    return jnp.asarray(tab)


def _sink_column(g, sk_ref):
    hrow = lax.broadcasted_iota(jnp.int32, (GQA * BLK, 1), 0) // BLK
    sink = jnp.zeros((GQA * BLK, 1), f32)
    for hh in range(GQA):
        sink = jnp.where(hrow == hh, sk_ref[g * GQA + hh:g * GQA + hh + 1, 0:1], sink)
    return sink


def _stack_heads(v, g):
    return jnp.concatenate([v[:, (g * GQA + hh) * HD:(g * GQA + hh + 1) * HD] for hh in range(GQA)], axis=0)


def attn_fwd(qkv, sink_rows, BL, S, exchange, qb=2):
    NS = S // (qb * BLK)
    T = BL * S

    def body(q_ref, kc_ref, kp_ref, vc_ref, vp_ref, sk_ref, tab_ref, o_ref):
        j = pl.program_id(1)
        kall = jnp.concatenate([kp_ref[...], kc_ref[...]], axis=0)
        vall = jnp.concatenate([vp_ref[...], vc_ref[...]], axis=0)
        ones = jnp.ones((2 * BLK, HD), MXU)
        for b in range(qb):
            q = q_ref[pl.ds(b * BLK, BLK), :]
            k2, v2 = kall[b * BLK:(b + 2) * BLK], vall[b * BLK:(b + 2) * BLK]
            later = jnp.minimum(j, 1) if b == 0 else 1
            for g in range(N_KV):
                kn, vh = k2[:, g * HD:(g + 1) * HD], v2[:, g * HD:(g + 1) * HD]
                s = _nt(_stack_heads(q, g), kn) * (HD ** -0.5) + tab_ref[later, g]
                e, es = _exp_scores(s, _sink_column(g, sk_ref))
                eb = _c(e)
                o = _nn(eb, vh) * (1.0 / (_nn(eb, ones) + es))
                for hh in range(GQA):
                    o_ref[pl.ds(b * BLK, BLK), pl.ds((g * GQA + hh) * HD, HD)] = o[hh * BLK:(hh + 1) * BLK]

    cur = lambda col: (lambda b, j: (b * NS + j, col))
    prev = lambda col: (lambda b, j: (qb * (b * NS + j) - jnp.minimum(j, 1), col))
    return _run("attn_fwd", body, (BL, NS), [qkv, qkv, qkv, qkv, qkv, sink_rows, _swa_bias_table()],
                [pl.BlockSpec((qb * BLK, ATT_W), cur(0)),
                 pl.BlockSpec((qb * BLK, KV_W), cur(4)), pl.BlockSpec((BLK, KV_W), prev(4)),
                 pl.BlockSpec((qb * BLK, KV_W), cur(5)), pl.BlockSpec((BLK, KV_W), prev(5)),
                 pl.BlockSpec((8, 128), lambda b, j: (0, 0)), VM],
                [SDS((T, ATT_W), f32)], [pl.BlockSpec((qb * BLK, ATT_W), cur(0))], exchange=exchange)


def _conv_taps(u, uh):
    row = lax.broadcasted_iota(jnp.int32, u.shape, 0)
    u1 = jnp.where(row == 0, uh[7:8, :], pltpu.roll(u, 1, 0))
    u2 = jnp.where(row == 0, uh[6:7, :], jnp.where(row == 1, uh[7:8, :], pltpu.roll(u, 2, 0)))
    return u1, u2


def _mem_head(qm, km, vm, h):
    qh, kh, vh = (a[:, h * HD:(h + 1) * HD] for a in (qm, km, vm))
    e, _ = _exp_scores(_nt(qh, kh) * (HD ** -0.5))
    return qh, kh, vh, e


def mixer_tail_fwd(x2d, attn_out, proj, qkv, km, vm, conv_w8, pk, wout, S, tm, exchange):
    T, D = x2d.shape
    NM = km.shape[0] // (T // S)

    def body(x_ref, ao_ref, ch_ref, cb_ref, cc_ref, chh_ref, cch_ref, qm_ref, km_ref, vm_ref, cw_ref, pk_ref,
             wout_ref, co_ref, mo_ref, mg_ref, x1_ref, h_ref):
        first = (pl.program_id(0) * tm) % S == 0
        u = cc_ref[...] * ch_ref[...]
        uh = jnp.where(first, 0.0, cch_ref[...] * chh_ref[...])
        u1, u2 = _conv_taps(u, uh)
        conv = cw_ref[0:1, :] * u2 + cw_ref[1:2, :] * u1 + cw_ref[2:3, :] * u + _small(pk_ref, "conv_b")
        conv_out = cb_ref[...] * conv
        co_ref[...] = conv_out
        qm, kmv, vmv = qm_ref[...], km_ref[...], vm_ref[...]
        ones = jnp.ones((NM, HD), MXU)
        for h in range(N_MEMH):
            _, _, vh, e = _mem_head(qm, kmv, vmv, h)
            eb = _c(e)
            mo_ref[:, pl.ds(h * HD, HD)] = _nn(eb, vh) * (1.0 / _nn(eb, ones))
        mem_out = mo_ref[...]
        ao = ao_ref[...]
        merged = _c(jnp.concatenate([ao * _rstd(ao) * _small(pk_ref, "out_norm_attn"),
                                     conv_out * _rstd(conv_out) * _small(pk_ref, "out_norm_conv"),
                                     mem_out * _rstd(mem_out) * _small(pk_ref, "out_norm_mem")], axis=1))
        mg_ref[...] = merged
        x1 = x_ref[...] + _nn(merged, wout_ref[...])
        x1_ref[...] = x1
        h_ref[...] = _c(x1 * _rstd(x1) * _small(pk_ref, "norm_ffn"))

    tile = lambda w, col: pl.BlockSpec((tm, w), lambda i: (i, col))
    halo = lambda col: pl.BlockSpec((8, CONV_W), lambda i: (jnp.maximum(i * (tm // 8) - 1, 0), col))
    seq = pl.BlockSpec((NM, MEM_W), lambda i: ((i * tm) // S, 0))
    small = lambda a: pl.BlockSpec(a.shape, lambda i: (0, 0))
    return _run("mixer_tail_fwd", body, (T // tm,),
                [x2d, attn_out, proj, proj, proj, proj, proj, qkv, km, vm, conv_w8, pk, wout],
                [tile(D, 0), tile(ATT_W, 0), tile(CONV_W, 3), tile(CONV_W, 4), tile(CONV_W, 5), halo(3), halo(5),
                 tile(MEM_W, 3), seq, seq, VM, VM, VM],
                [SDS((T, CONV_W), f32), SDS((T, MEM_W), f32), SDS((T, D), MXU), SDS((T, D), f32), SDS((T, D), MXU)],
                [tile(CONV_W, 0), tile(MEM_W, 0), tile(D, 0), tile(D, 0), tile(D, 0)], vmem_mib=40, exchange=exchange)


def ffn_fwd_bwd(h, x1, tgt, wgT, wuT, wd, pk, tm):
    T, D = x1.shape
    F = wd.shape[0]

    def body(h_ref, x1_ref, t_ref, wg_ref, wu_ref, wd_ref, pk_ref,
             dx1_ref, dx2_ref, act_ref, dg_ref, du_ref, loss_ref, dgf_ref):
        @pl.when(pl.program_id(0) == 0)
        def _():
            loss_ref[...] = jnp.zeros_like(loss_ref)
            dgf_ref[...] = jnp.zeros_like(dgf_ref)

        hv = h_ref[...]
        gate = _nt(hv, wg_ref[...])
        up = _nt(hv, wu_ref[...])
        sg = jax.nn.sigmoid(gate)
        sl = gate * sg
        act = _c(sl * up)
        act_ref[...] = act
        x1v = x1_ref[...]
        diff = (x1v + _nn(act, wd_ref[...])) - t_ref[...]
        loss_ref[...] += 0.5 * jnp.sum(jnp.sum(diff * diff, axis=-1, keepdims=True) / D, axis=0, keepdims=True)
        dx2 = diff / D
        dx2b = _c(dx2)
        dx2_ref[...] = dx2b
        d_act = _nt(dx2b, wd_ref[...])
        d_up = _c(d_act * sl)
        d_gate = _c(d_act * up * (sg * (1.0 + gate * (1.0 - sg))))
        du_ref[...] = d_up
        dg_ref[...] = d_gate
        dh = _nn(d_gate, wg_ref[...]) + _nn(d_up, wu_ref[...])
        dv, dgf = _norm_bwd(dh, x1v, _rstd(x1v), _small(pk_ref, "norm_ffn"))
        dx1_ref[...] = dx2 + dv
        dgf_ref[...] += dgf

    tile = lambda w: pl.BlockSpec((tm, w), lambda i: (i, 0))
    return _run("ffn_fwd_bwd", body, (T // tm,), [h, x1, tgt, wgT, wuT, wd, pk],
                [tile(D), tile(D), tile(D), VM, VM, VM, VM],
                [SDS((T, D), f32), SDS((T, D), MXU), SDS((T, F), MXU), SDS((T, F), MXU), SDS((T, F), MXU),
                 SDS((8, 128), f32), SDS((1, D), f32)],
                [tile(D), tile(D), tile(F), tile(F), tile(F), pl.BlockSpec((8, 128), lambda i: (0, 0)),
                 pl.BlockSpec((1, D), lambda i: (0, 0))], vmem_mib=56)


def matmul_tn(a, b, name, tmo, tk):
    T, M = a.shape
    N = b.shape[1]

    def body(a_ref, b_ref, o_ref):
        @pl.when(pl.program_id(1) == 0)
        def _():
            o_ref[...] = jnp.zeros_like(o_ref)

        o_ref[...] += _tn(a_ref[...], b_ref[...])

    return _run(name, body, (M // tmo, T // tk), [a, b],
                [pl.BlockSpec((tk, tmo), lambda m, k: (k, m)), pl.BlockSpec((tk, N), lambda m, k: (k, 0))],
                [SDS((M, N), f32)], [pl.BlockSpec((tmo, N), lambda m, k: (m, 0))], vmem_mib=48)[0]


def out_proj_bwd(dx1, merged, attn_out, conv_out, mem_out, pk, wout, tm):
    T, D = dx1.shape

    def body(dx1_ref, mg_ref, ao_ref, co_ref, mo_ref, pk_ref, w_ref,
             dao_ref, dco_ref, dmo_ref, dw_ref, dgain_ref):
        @pl.when(pl.program_id(0) == 0)
        def _():
            dw_ref[...] = jnp.zeros_like(dw_ref)
            dgain_ref[...] = jnp.zeros_like(dgain_ref)

        dxb = _c(dx1_ref[...])
        dw_ref[...] += _tn(mg_ref[...], dxb)
        dmg = _nt(dxb, w_ref[...])
        ao, co, mo = ao_ref[...], co_ref[...], mo_ref[...]
        da, ga = _norm_bwd(dmg[:, :ATT_W], ao, _rstd(ao), _small(pk_ref, "out_norm_attn"))
        dc, gc = _norm_bwd(dmg[:, ATT_W:ATT_W + CONV_W], co, _rstd(co), _small(pk_ref, "out_norm_conv"))
        dm, gm = _norm_bwd(dmg[:, ATT_W + CONV_W:], mo, _rstd(mo), _small(pk_ref, "out_norm_mem"))
        dao_ref[...] = da
        dco_ref[...] = dc
        dmo_ref[...] = dm
        dgain_ref[...] += jnp.concatenate([ga, gc, gm], axis=1)

    tile = lambda w: pl.BlockSpec((tm, w), lambda i: (i, 0))
    return _run("out_proj_bwd", body, (T // tm,), [dx1, merged, attn_out, conv_out, mem_out, pk, wout],
                [tile(D), tile(D), tile(ATT_W), tile(CONV_W), tile(MEM_W), VM, VM],
                [SDS((T, ATT_W), f32), SDS((T, CONV_W), f32), SDS((T, MEM_W), f32), SDS((D, D), f32), SDS((1, D), f32)],
                [tile(ATT_W), tile(CONV_W), tile(MEM_W), pl.BlockSpec((D, D), lambda i: (0, 0)),
                 pl.BlockSpec((1, D), lambda i: (0, 0))], vmem_mib=40)


def attn_bwd(qkv, d_attn, attn_out, sink_rows, BL, S, exchange):
    NB = S // BLK
    T = BL * S

    def body(q_ref, kc_ref, kp_ref, vc_ref, vp_ref, do_ref, ao_ref, sk_ref, tab_ref,
             dq_ref, dk_ref, dv_ref, dsk_ref, pend_k, pend_v):
        b, j = pl.program_id(0), pl.program_id(1)

        @pl.when((b == 0) & (j == 0))
        def _():
            dsk_ref[...] = jnp.zeros_like(dsk_ref)

        @pl.when(j == 0)
        def _():
            pend_k[...] = jnp.zeros_like(pend_k)
            pend_v[...] = jnp.zeros_like(pend_v)

        @pl.when(j < NB)
        def _():
            q, do, ao = q_ref[...], do_ref[...], ao_ref[...]
            k2 = jnp.concatenate([kp_ref[...], kc_ref[...]], axis=0)
            v2 = jnp.concatenate([vp_ref[...], vc_ref[...]], axis=0)
            lane = lax.broadcasted_iota(jnp.int32, (8, 128), 1)
            ones_w = jnp.ones((2 * BLK, 2 * BLK), MXU)
            dsk = jnp.zeros((8, 128), f32)
            dks, dvs = [], []
            for g in range(N_KV):
                kn, vh = k2[:, g * HD:(g + 1) * HD], v2[:, g * HD:(g + 1) * HD]
                qs = _stack_heads(q, g)
                s = _nt(qs, kn) * (HD ** -0.5) + tab_ref[g]
                e, es = _exp_scores(s, _sink_column(g, sk_ref))
                eb = _c(e)
                inv_w = 1.0 / (_nn(eb, ones_w) + es)
                inv_n = inv_w[:, :HD]
                dos = _stack_heads(do, g)
                delta = _rowsum_mxu(dos * _stack_heads(ao, g), 2 * BLK)
                dp = _nt(_c(dos), vh)
                ds = _c(e * inv_w * (dp - delta) * (HD ** -0.5))
                t = es * inv_n[:, 0:1] * delta[:, 0:1]
                for hh in range(GQA):
                    dsk = dsk + jnp.where(lane == g * GQA + hh, -jnp.sum(t[hh * BLK:(hh + 1) * BLK]), 0.0)
                dvs.append(_tn(eb, _c(dos * inv_n)))
                dks.append(_tn(ds, qs))
                dqs = _nn(ds, kn)
                for hh in range(GQA):
                    dq_ref[:, pl.ds((g * GQA + hh) * HD, HD)] = dqs[hh * BLK:(hh + 1) * BLK]
            dk2 = jnp.concatenate(dks, axis=1)
            dv2 = jnp.concatenate(dvs, axis=1)
            dk_ref[...] = pend_k[...] + dk2[:BLK]
            dv_ref[...] = pend_v[...] + dv2[:BLK]
            pend_k[...] = dk2[BLK:]
            pend_v[...] = dv2[BLK:]
            dsk_ref[...] += dsk

        @pl.when(j == NB)
        def _():
            dk_ref[...] = pend_k[...]
            dv_ref[...] = pend_v[...]

    cur = lambda col: (lambda b, j: (b * NB + jnp.minimum(j, NB - 1), col))
    prev = lambda col: (lambda b, j: (b * NB + jnp.maximum(j - 1, 0), col))
    small = lambda shape: pl.BlockSpec(shape, lambda b, j: (0, 0))
    return _run("attn_bwd", body, (BL, NB + 1), [qkv, qkv, qkv, qkv, qkv, d_attn, attn_out, sink_rows, _swa_bias_table()],
                [pl.BlockSpec((BLK, ATT_W), cur(0)),
                 pl.BlockSpec((BLK, KV_W), cur(4)), pl.BlockSpec((BLK, KV_W), prev(4)),
                 pl.BlockSpec((BLK, KV_W), cur(5)), pl.BlockSpec((BLK, KV_W), prev(5)),
                 pl.BlockSpec((BLK, ATT_W), cur(0)), pl.BlockSpec((BLK, ATT_W), cur(0)), small((8, 128)),
                 pl.BlockSpec((None, N_KV, GQA * BLK, 2 * BLK), lambda b, j: (jnp.minimum(j, 1), 0, 0, 0))],
                [SDS((T, ATT_W), f32), SDS((T, KV_W), f32), SDS((T, KV_W), f32), SDS((8, 128), f32)],
                [pl.BlockSpec((BLK, ATT_W), cur(0)), pl.BlockSpec((BLK, KV_W), prev(0)),
                 pl.BlockSpec((BLK, KV_W), prev(0)), small((8, 128))],
                scratch=[pltpu.VMEM((BLK, KV_W), f32)] * 2, vmem_mib=56, exchange=exchange)


def mem_conv_bwd(d_mem_out, mem_out, d_conv_out, proj, qkv, km, vm, conv_w8, pk, S, tm, exchange):
    T = d_mem_out.shape[0]
    NM = km.shape[0] // (T // S)

    def body(dmo_ref, mo_ref, dco_ref, ch_ref, cb_ref, cc_ref, chh_ref, cch_ref, qm_ref, km_ref, vm_ref, cw_ref,
             pk_ref, dqm_ref, dkm_ref, dvm_ref, dcb_ref, dcv_ref, dcw_ref, dcbias_ref):
        i = pl.program_id(0)
        first = (i * tm) % S == 0

        @pl.when(i == 0)
        def _():
            dcw_ref[...] = jnp.zeros_like(dcw_ref)
            dcbias_ref[...] = jnp.zeros_like(dcbias_ref)

        @pl.when(first)
        def _():
            dkm_ref[...] = jnp.zeros_like(dkm_ref)
            dvm_ref[...] = jnp.zeros_like(dvm_ref)

        qm, kmv, vmv, dmo, mo = qm_ref[...], km_ref[...], vm_ref[...], dmo_ref[...], mo_ref[...]
        ones_w = jnp.ones((NM, NM), MXU)
        for h in range(N_MEMH):
            qh, kh, vh, e = _mem_head(qm, kmv, vmv, h)
            eb = _c(e)
            doh = dmo[:, h * HD:(h + 1) * HD]
            delta = _rowsum_mxu(doh * mo[:, h * HD:(h + 1) * HD], NM)
            dp = _nt(_c(doh), vh)
            inv_w = 1.0 / _nn(eb, ones_w)
            ds = _c(e * inv_w * (dp - delta) * (HD ** -0.5))
            dvm_ref[:, pl.ds(h * HD, HD)] += _tn(eb, _c(doh * inv_w[:, :HD]))
            dkm_ref[:, pl.ds(h * HD, HD)] += _tn(ds, qh)
            dqm_ref[:, pl.ds(h * HD, HD)] = _nn(ds, kh)

        u = cc_ref[...] * ch_ref[...]
        uh = jnp.where(first, 0.0, cch_ref[...] * chh_ref[...])
        u1, u2 = _conv_taps(u, uh)
        conv = cw_ref[0:1, :] * u2 + cw_ref[1:2, :] * u1 + cw_ref[2:3, :] * u + _small(pk_ref, "conv_b")
        dy = dco_ref[...]
        dcb_ref[...] = dy * conv
        dcv = dy * cb_ref[...]
        dcv_ref[...] = dcv
        dcbias_ref[...] += jnp.sum(dcv, axis=0, keepdims=True)
        dcw_ref[0:1, :] += jnp.sum(dcv * u2, axis=0, keepdims=True)
        dcw_ref[1:2, :] += jnp.sum(dcv * u1, axis=0, keepdims=True)
        dcw_ref[2:3, :] += jnp.sum(dcv * u, axis=0, keepdims=True)

    tile = lambda w, col: pl.BlockSpec((tm, w), lambda i: (i, col))
    halo = lambda col: pl.BlockSpec((8, CONV_W), lambda i: (jnp.maximum(i * (tm // 8) - 1, 0), col))
    seq = pl.BlockSpec((NM, MEM_W), lambda i: ((i * tm) // S, 0))
    const = lambda shape: pl.BlockSpec(shape, lambda i: (0, 0))
    return _run("mem_conv_bwd", body, (T // tm,),
                [d_mem_out, mem_out, d_conv_out, proj, proj, proj, proj, proj, qkv, km, vm, conv_w8, pk],
                [tile(MEM_W, 0), tile(MEM_W, 0), tile(CONV_W, 0), tile(CONV_W, 3), tile(CONV_W, 4), tile(CONV_W, 5),
                 halo(3), halo(5), tile(MEM_W, 3), seq, seq, VM, VM],
                [SDS((T, MEM_W), f32), SDS(km.shape, f32), SDS(km.shape, f32),
                 SDS((T, CONV_W), f32), SDS((T, CONV_W), f32), SDS((8, CONV_W), f32), SDS((1, CONV_W), f32)],
                [tile(MEM_W, 0), seq, seq, tile(CONV_W, 0), tile(CONV_W, 0), const((8, CONV_W)), const((1, CONV_W))],
                vmem_mib=48, exchange=exchange)


def in_proj_bwd(dqn, dkn, dv, dcb, dcv, dqmn, proj, conv_w8, xn, x2d, dx1, pk, winT, S, tm, stages, ws, ms, vs):
    T, D = x2d.shape
    P = winT.shape[0]
    last_blk = T // 8 - 1
    n = len(stages)
    nsteps = T // tm
    tile_w = ws[0].shape[1] // (nsteps // 2)
    turn = [e * 2 // n for e in range(n)]

    def body(dq_ref, dk_ref, dv_ref, dcb_ref, dcv_ref, dcvn_ref, dqm_ref, qa_ref, ka_ref, ch_ref, cc_ref, qma_ref,
             cw_ref, xn_ref, x_ref, dx1_ref, pk_ref, w_ref, *rest):
        st, aw, am, av = (rest[k * n:(k + 1) * n] for k in range(4))
        dx_ref, dw_ref, dg_ref, dqg_ref, dkg_ref, dmqg_ref = rest[4 * n:4 * n + 6]
        aouts = rest[4 * n + 6:]
        i = pl.program_id(0)

        for parity in range(2):
            @pl.when(i % 2 == parity)
            def _(parity=parity):
                for e in range(n):
                    if turn[e] == parity:
                        g = jnp.concatenate([_sum_chips(st[e].at[0]), _sum_chips(st[e].at[1])], axis=0)
                        d, mm, vv = _adamw_math(aw[e][...], g, am[e][...], av[e][...])
                        for k, val in enumerate((g, d, mm, vv)):
                            aouts[4 * e + k][...] = val

        @pl.when(i == 0)
        def _():
            dw_ref[...] = jnp.zeros_like(dw_ref)
            dg_ref[...] = jnp.zeros_like(dg_ref)
            dqg_ref[...] = jnp.zeros_like(dqg_ref)
            dkg_ref[...] = jnp.zeros_like(dkg_ref)
            dmqg_ref[...] = jnp.zeros_like(dmqg_ref)

        dqa, gq = _heads_norm_bwd(dq_ref[...], qa_ref[...], _small(pk_ref, "q_norm"))
        dka, gk = _heads_norm_bwd(dk_ref[...], ka_ref[...], _small(pk_ref, "k_norm"))
        dqma, gmq = _heads_norm_bwd(dqm_ref[...], qma_ref[...], _small(pk_ref, "mem_q_norm"))
        dqg_ref[...] += gq
        dkg_ref[...] += gk
        dmqg_ref[...] += gmq

        last = ((i + 1) * tm) % S == 0
        dcv = dcv_ref[...]
        nxt = jnp.where(last, 0.0, dcvn_ref[...])
        row = lax.broadcasted_iota(jnp.int32, dcv.shape, 0)
        n1 = jnp.where(row == tm - 1, nxt[0:1, :], pltpu.roll(dcv, tm - 1, 0))
        n2 = jnp.where(row == tm - 2, nxt[0:1, :], jnp.where(row == tm - 1, nxt[1:2, :], pltpu.roll(dcv, tm - 2, 0)))
        du = cw_ref[2:3, :] * dcv + cw_ref[1:2, :] * n1 + cw_ref[0:1, :] * n2
        d_proj = jnp.concatenate([_c(dqa), _c(dka), _c(dv_ref[...]), _c(du * cc_ref[...]),
                                  _c(dcb_ref[...]), _c(du * ch_ref[...]), _c(dqma)], axis=1)
        dw_ref[...] += _tn(d_proj, xn_ref[...])
        xv = x_ref[...]
        dv_, dg = _norm_bwd(_nn(d_proj, w_ref[...]), xv, _rstd(xv), _small(pk_ref, "norm_mix"))
        dx_ref[...] = dx1_ref[...] + dv_
        dg_ref[...] += dg

    tile = lambda w, col=0: pl.BlockSpec((tm, w), lambda i: (i, col))
    nhalo = pl.BlockSpec((8, CONV_W), lambda i: (jnp.minimum((i + 1) * (tm // 8), last_blk), 0))
    const = lambda shape: pl.BlockSpec(shape, lambda i: (0, 0))
    st_specs = [pl.BlockSpec((2, 4, s.shape[2], tile_w), lambda i: (0, 0, 0, i // 2)) for s in stages]
    w_specs = [pl.BlockSpec((w.shape[0], tile_w), lambda i: (0, i // 2)) for w in ws]
    res = _run("in_proj_bwd", body, (nsteps,),
               [dqn, dkn, dv, dcb, dcv, dcv, dqmn, proj, proj, proj, proj, proj, conv_w8, xn, x2d, dx1, pk, winT]
               + list(stages) + list(ws) + list(ms) + list(vs),
               [tile(ATT_W), tile(KV_W), tile(KV_W), tile(CONV_W), tile(CONV_W), nhalo, tile(MEM_W),
                tile(ATT_W, 0), tile(KV_W, 4), tile(CONV_W, 3), tile(CONV_W, 5), tile(MEM_W, 6), VM,
                tile(D), tile(D), tile(D), VM, VM] + st_specs + w_specs * 3,
               [SDS((T, D), f32), SDS((P, D), f32), SDS((1, D), f32), SDS((1, HD), f32), SDS((1, HD), f32),
                SDS((1, HD), f32)] + [SDS(w.shape, f32) for w in ws for _ in range(4)],
               [tile(D), pl.BlockSpec((P, D), lambda i: (0, 0)), const((1, D)), const((1, HD)), const((1, HD)),
                const((1, HD))] + [s for s in w_specs for _ in range(4)],
               vmem_mib=56)
    return res[:6], [res[6 + 4 * e:10 + 4 * e] for e in range(n)]


def mem_kv_bwd(dkm, dvm, kv, memn, mem2d, pk, wmkv):
    def body(dkm_ref, dvm_ref, kv_ref, mn_ref, m_ref, pk_ref, w_ref, dw_ref, dg_ref, dkg_ref):
        dkk, dkg = _heads_norm_bwd(dkm_ref[...], kv_ref[:, :MEM_W], _small(pk_ref, "mem_k_norm"))
        dkg_ref[...] = dkg
        dkv = _c(jnp.concatenate([dkk, dvm_ref[...]], axis=1))
        dw_ref[...] = _tn(mn_ref[...], dkv)
        mv = m_ref[...]
        dg_ref[...] = jnp.sum(_nt(dkv, w_ref[...]) * mv * _rstd(mv), axis=0, keepdims=True)

    return _run("mem_kv_bwd", body, (), [dkm, dvm, kv, memn, mem2d, pk, wmkv], [VM] * 7,
                [SDS(wmkv.shape, f32), SDS((1, mem2d.shape[1]), f32), SDS((1, HD), f32)], [VM] * 3, vmem_mib=40)


def _halves_view(g):
    return g.reshape(4, 2, g.shape[0] // 8, g.shape[1])


def kernel(x, mem, norm_mix, w_in, q_norm, k_norm, attn_sinks, conv_w, conv_b, norm_mem, w_mem_kv, mem_q_norm, mem_k_norm, out_norm_attn, out_norm_conv, out_norm_mem, w_out, norm_ffn, w_gate, w_up, w_down, loss_target, m_norm_mix, m_w_in, m_q_norm, m_k_norm, m_attn_sinks, m_conv_w, m_conv_b, m_norm_mem, m_w_mem_kv, m_mem_q_norm, m_mem_k_norm, m_out_norm_attn, m_out_norm_conv, m_out_norm_mem, m_w_out, m_norm_ffn, m_w_gate, m_w_up, m_w_down, v_norm_mix, v_w_in, v_q_norm, v_k_norm, v_attn_sinks, v_conv_w, v_conv_b, v_norm_mem, v_w_mem_kv, v_mem_q_norm, v_mem_k_norm, v_out_norm_attn, v_out_norm_conv, v_out_norm_mem, v_w_out, v_norm_ffn, v_w_gate, v_w_up, v_w_down):
    BL, S, D = x.shape
    T = BL * S
    TM = 256
    TM_BIG = min(512, S)
    w_small = dict(norm_mix=norm_mix, norm_mem=norm_mem, norm_ffn=norm_ffn, out_norm_attn=out_norm_attn,
                   out_norm_conv=out_norm_conv, out_norm_mem=out_norm_mem, conv_w=conv_w, conv_b=conv_b, q_norm=q_norm,
                   k_norm=k_norm, mem_q_norm=mem_q_norm, mem_k_norm=mem_k_norm, attn_sinks=attn_sinks)
    m_small = dict(norm_mix=m_norm_mix, norm_mem=m_norm_mem, norm_ffn=m_norm_ffn, out_norm_attn=m_out_norm_attn,
                   out_norm_conv=m_out_norm_conv, out_norm_mem=m_out_norm_mem, conv_w=m_conv_w, conv_b=m_conv_b,
                   q_norm=m_q_norm, k_norm=m_k_norm, mem_q_norm=m_mem_q_norm, mem_k_norm=m_mem_k_norm,
                   attn_sinks=m_attn_sinks)
    v_small = dict(norm_mix=v_norm_mix, norm_mem=v_norm_mem, norm_ffn=v_norm_ffn, out_norm_attn=v_out_norm_attn,
                   out_norm_conv=v_out_norm_conv, out_norm_mem=v_out_norm_mem, conv_w=v_conv_w, conv_b=v_conv_b,
                   q_norm=v_q_norm, k_norm=v_k_norm, mem_q_norm=v_mem_q_norm, mem_k_norm=v_mem_k_norm,
                   attn_sinks=v_attn_sinks)
    pk = _pack_small(w_small)

    rowblocks = lambda a, b, c, d, e, f: [a[0].T, b[0].T, c[0].T, d[0], e[0], f[0]]
    w_rb = rowblocks(w_in, w_gate, w_up, w_down, w_out, w_mem_kv)
    m_rb = rowblocks(m_w_in, m_w_gate, m_w_up, m_w_down, m_w_out, m_w_mem_kv)
    v_rb = rowblocks(v_w_in, v_w_gate, v_w_up, v_w_down, v_w_out, v_w_mem_kv)
    (winT_s,) = prep_weights("prep_w_in", w_rb[:1])
    cw_pad = jnp.zeros((8, 128), f32).at[:3, :HD].set(conv_w[0])
    (wgT_s, wuT_s, wd_s, wout_s, wmkv_s), (winT, cw_all) = prep_weights(
        "gather_w_in", w_rb[1:], _together([gather_two_legs([winT_s]), gather_exchange([cw_pad], [False])]))
    conv_w_full = jnp.transpose(cw_all.reshape(4, 8, 128)[:, :3, :HD], (1, 0, 2)).reshape(3, CONV_W)
    conv_w8 = jnp.zeros((8, CONV_W), f32).at[:3].set(conv_w_full)
    sink_rows = jnp.broadcast_to(attn_sinks.reshape(N_Q, 1), (N_Q, 128))

    x2d = x.reshape(T, D)
    mem2d = mem.reshape(-1, D)
    (xn, proj, qkv), near1 = in_proj_fwd(x2d, pk, winT, TM_BIG, gather_near_exchange([wgT_s, wout_s, wmkv_s], relay_early=1))
    (attn_out,), (wgT, wout, wmkv, *near2) = attn_fwd(
        qkv, sink_rows, BL, S, _together([gather_far_exchange(near1, relay_early=2), gather_near_exchange([wuT_s, wd_s], relay_early=2)]))
    memn, kv, km, vm = mem_kv_fwd(mem2d, pk, wmkv)
    (conv_out, mem_out, merged, x1, h), (wuT, wd) = mixer_tail_fwd(
        x2d, attn_out, proj, qkv, km, vm, conv_w8, pk, wout, S, TM_BIG, gather_far_exchange(near2, relay_early=2))

    dx1, dx2b, act, d_gate, d_up, loss8, d_norm_ffn = ffn_fwd_bwd(h, x1, loss_target.reshape(T, D), wgT, wuT, wd, pk, TM)
    F = wd.shape[0]
    g_wd = matmul_tn(act, dx2b, "dw_down", F // 2, min(T, 1024))
    g_wgT = matmul_tn(d_gate, h, "dw_gate", F // 2, min(T, 1024))
    g_wuT = matmul_tn(d_up, h, "dw_up", F // 2, min(T, 1024))

    d_attn, d_conv_out, d_mem_out, g_wout, d_gains = out_proj_bwd(dx1, merged, attn_out, conv_out, mem_out, pk, wout, TM_BIG)
    late = [_halves_view(g) for g in (g_wgT, g_wuT, g_wd, g_wout)]
    (dqmn, dkm, dvm, dcb, dcv, d_cw8, d_cbias), late_sib = mem_conv_bwd(
        d_mem_out, mem_out, d_conv_out, proj, qkv, km, vm, conv_w8, pk, S, min(1024, S), halves_exchange(late))
    (dqn, dkn, dv, d_sink8), late_stage = attn_bwd(
        qkv, d_attn, attn_out, sink_rows, BL, S,
        reduce_scatter_exchange(late, late_sib, BL * (S // BLK + 1), load_step=[0, 4, 8, 12], send_step=[2, 6, 10, 14],
                                relay_step=[14, 23, 32, 33]))
    (g_x, g_winT, d_norm_mix, d_qg, d_kg, d_mqg), late_res = in_proj_bwd(
        dqn, dkn, dv, dcb, dcv, dqmn, proj, conv_w8, xn, x2d, dx1, pk, winT, S, TM,
        late_stage, w_rb[1:5], m_rb[1:5], v_rb[1:5])
    g_wmkv, d_norm_mem, d_mkg = mem_kv_bwd(dkm, dvm, kv, memn, mem2d, pk, wmkv)

    tot, tail_stage = tail_reduce(d_norm_mix, d_norm_mem, d_norm_ffn, d_gains, d_cw8, d_cbias, d_qg, d_kg, d_mqg, d_mkg,
                                  d_sink8, loss8, [_halves_view(g) for g in (g_winT, g_wmkv)])
    loss = tot[5, 384]
    tail_res, _ = adamw_big("adamw_tail", tail_stage, [w_rb[0], w_rb[5]], [m_rb[0], m_rb[5]], [v_rb[0], v_rb[5]], 4)
    res = {"w_in": [a.T[None] for a in tail_res[0]], "w_gate": [a.T[None] for a in late_res[0]],
           "w_up": [a.T[None] for a in late_res[1]], "w_down": [a[None] for a in late_res[2]],
           "w_out": [a[None] for a in late_res[3]], "w_mem_kv": [a[None] for a in tail_res[1]]}
    res.update(adamw_small(tot, pk, _pack_small(m_small), _pack_small(v_small), {k: w_small[k].shape for k in SMALL}))

    order = ["norm_mix", "w_in", "q_norm", "k_norm", "attn_sinks", "conv_w", "conv_b", "norm_mem", "w_mem_kv",
             "mem_q_norm", "mem_k_norm", "out_norm_attn", "out_norm_conv", "out_norm_mem", "w_out", "norm_ffn",
             "w_gate", "w_up", "w_down"]
    return (loss, g_x.reshape(BL, S, D), *[res[n][0] for n in order], *[res[n][1] for n in order],
            *[res[n][2] for n in order], *[res[n][3] for n in order])
```

```python
import collections
import functools

import jax
import jax.numpy as jnp
import numpy as np
from jax import lax
from jax.experimental import pallas as pl
from jax.experimental.pallas import tpu as pltpu

f32 = jnp.float32
MXU = jnp.bfloat16
WIRE = jnp.bfloat16
EPS = 1e-6
NEG = -1e30
HD = 64
BLK = 128
N_Q, N_KV, N_MEMH = 8, 2, 4
GQA = N_Q // N_KV
ATT_W, KV_W, CONV_W, MEM_W = 512, 128, 256, 256
VMEM_MIB = 1024 * 1024
ADAM_LR, ADAM_B1, ADAM_B2, ADAM_EPS, ADAM_WD, ADAM_STEP = 0.001, 0.9, 0.999, 1e-08, 0.01, 10

MESH = pl.DeviceIdType.MESH
VM = pl.BlockSpec(memory_space=pltpu.VMEM)
ANY = pl.BlockSpec(memory_space=pl.ANY)
SDS = jax.ShapeDtypeStruct
DMA = pltpu.SemaphoreType.DMA


def _c(v):
    return v.astype(MXU)


def _nn(a, b):
    return lax.dot_general(a, b, (((1,), (0,)), ((), ())), preferred_element_type=f32)


def _nt(a, b):
    return lax.dot_general(a, b, (((1,), (1,)), ((), ())), preferred_element_type=f32)


def _tn(a, b):
    return lax.dot_general(a, b, (((0,), (0,)), ((), ())), preferred_element_type=f32)


def _rstd(v):
    return lax.rsqrt(jnp.mean(v * v, axis=-1, keepdims=True) + EPS)


def _norm_bwd(dy, v, r, g):
    dyg = dy * g
    dv = r * dyg - v * (r * r * r) * jnp.mean(dyg * v, axis=-1, keepdims=True)
    return dv, jnp.sum(dy * v * r, axis=0, keepdims=True)


def _split3(v):
    hi = _c(v)
    r1 = v - hi.astype(f32)
    mid = _c(r1)
    return hi, mid, _c(r1 - mid.astype(f32))


def _rowsum_mxu(v, width):
    ones = jnp.ones((v.shape[1], width), MXU)
    return sum(_nn(a, ones) for a in _split3(v))


def _seg_sums(v):
    r = lax.broadcasted_iota(jnp.int32, (2 * HD, 2 * HD), 0) // HD
    c = lax.broadcasted_iota(jnp.int32, (2 * HD, 2 * HD), 1) // HD
    bd = (r == c).astype(MXU)
    outs = []
    for b in range(v.shape[1] // (2 * HD)):
        outs.append(sum(_nn(a, bd) for a in _split3(v[:, b * 2 * HD:(b + 1) * 2 * HD])))
    return outs[0] if len(outs) == 1 else jnp.concatenate(outs, axis=1)


def _lanes(g, width):
    return jnp.concatenate([g] * (width // HD), axis=1)


def _heads_rstd(v):
    return lax.rsqrt(_seg_sums(v * v) * (1.0 / HD) + EPS)


def _heads_norm_bwd(dy, v, g):
    r = _heads_rstd(v)
    gl = _lanes(g, v.shape[1])
    dyg = dy * gl
    dv = r * dyg - v * (r * r * r) * (_seg_sums(dyg * v) * (1.0 / HD))
    dgl = jnp.sum(dy * v * r, axis=0, keepdims=True)
    return dv, sum(dgl[:, s * HD:(s + 1) * HD] for s in range(v.shape[1] // HD))


def _exp_scores(s, extra=None):
    m = jnp.max(s, axis=-1, keepdims=True)
    if extra is None:
        return jnp.exp(s - m), None
    m = jnp.maximum(m, extra)
    return jnp.exp(s - m), jnp.exp(extra - m)


def _place():
    return lax.axis_index("x"), lax.axis_index("y"), lax.axis_index("c")


SMALL_AT = {"norm_mix": (0, 0, 1024), "norm_mem": (1, 0, 1024), "norm_ffn": (2, 0, 1024),
            "out_norm_attn": (3, 0, ATT_W), "out_norm_conv": (3, ATT_W, CONV_W), "out_norm_mem": (3, ATT_W + CONV_W, MEM_W),
            "conv_b": (4, 3 * CONV_W, CONV_W), "q_norm": (5, 0, HD), "k_norm": (5, HD, HD), "mem_q_norm": (5, 2 * HD, HD),
            "mem_k_norm": (5, 3 * HD, HD), "attn_sinks": (5, 256, N_Q)}
SMALL = ("norm_mix", "norm_mem", "norm_ffn", "out_norm_attn", "out_norm_conv", "out_norm_mem", "conv_w", "conv_b",
         "q_norm", "k_norm", "mem_q_norm", "mem_k_norm", "attn_sinks")


def _small(pk_ref, name):
    r, c0, w = SMALL_AT[name]
    return pk_ref[r:r + 1, c0:c0 + w]


def _pack_small(d):
    z = lambda n: jnp.zeros((1, n), f32)
    row3 = jnp.concatenate([d["out_norm_attn"], d["out_norm_conv"], d["out_norm_mem"]], axis=1)
    row4 = jnp.concatenate([d["conv_w"].reshape(1, 3 * HD), z(3 * CONV_W - 3 * HD), d["conv_b"]], axis=1)
    row5 = jnp.concatenate([d["q_norm"], d["k_norm"], d["mem_q_norm"], d["mem_k_norm"], d["attn_sinks"],
                            z(1024 - 4 * HD - N_Q)], axis=1)
    return jnp.concatenate([d["norm_mix"], d["norm_mem"], d["norm_ffn"], row3, row4, row5, z(1024), z(1024)], axis=0)


def _other_chips(x, y):
    return [(1 - x, y), (x, 1 - y), (1 - x, 1 - y)]


Exchange = collections.namedtuple("Exchange", "ins outs sems start finish relays aliases", defaults=((), {}))


def _together(exchanges):
    def bounds(key):
        at, out = 0, []
        for ex in exchanges:
            out.append((at, at + len(getattr(ex, key))))
            at += len(getattr(ex, key))
        return out

    bi, bo, bs = bounds("ins"), bounds("outs"), bounds("sems")

    def of(i, fn):
        return lambda xa, xo, xs: fn(xa[bi[i][0]:bi[i][1]], xo[bo[i][0]:bo[i][1]], xs[bs[i][0]:bs[i][1]])

    def every(name):
        fns = [of(i, getattr(ex, name)) for i, ex in enumerate(exchanges)]

        def run(xa, xo, xs):
            for fn in fns:
                fn(xa, xo, xs)
        return run

    aliases = {}
    for i, ex in enumerate(exchanges):
        aliases.update({bi[i][0] + a: bo[i][0] + o for a, o in ex.aliases.items()})
    return Exchange([a for ex in exchanges for a in ex.ins], [o for ex in exchanges for o in ex.outs],
                    [s for ex in exchanges for s in ex.sems], every("start"), every("finish"),
                    [(sbe, of(i, fn)) for i, ex in enumerate(exchanges) for sbe, fn in ex.relays], aliases)


def _run(name, body, grid, ins, in_specs, out_shape, out_specs, scratch=(), vmem_mib=32, exchange=None):
    ins, in_specs, out_shape, out_specs, scratch = list(ins), list(in_specs), list(out_shape), list(out_specs), list(scratch)
    ni, no, ns = len(ins), len(out_shape), len(scratch)
    ex = exchange
    if ex is not None:
        nxi, nxo = len(ex.ins), len(ex.outs)

    def call_body(*refs):
        if ex is None:
            body(*refs)
            return
        a, xa = refs[:ni], refs[ni:ni + nxi]
        o, xo = refs[ni + nxi:ni + nxi + no], refs[ni + nxi + no:ni + nxi + no + nxo]
        s, xs = refs[ni + nxi + no + nxo:ni + nxi + no + nxo + ns], refs[ni + nxi + no + nxo + ns:]
        if grid:
            first = functools.reduce(jnp.logical_and, [pl.program_id(d) == 0 for d in range(len(grid))])
            last = functools.reduce(jnp.logical_and, [pl.program_id(d) == grid[d] - 1 for d in range(len(grid))])
            pl.when(first)(lambda: ex.start(xa, xo, xs))
            body(*a, *o, *s)
            nsteps = functools.reduce(lambda p, q: p * q, grid)
            for before_end, fn in ex.relays:
                at = np.unravel_index(max(nsteps - 1 - before_end, 0), grid)
                here = functools.reduce(jnp.logical_and, [pl.program_id(d) == int(at[d]) for d in range(len(grid))])
                pl.when(here)(functools.partial(fn, xa, xo, xs))
            pl.when(last)(lambda: ex.finish(xa, xo, xs))
        else:
            ex.start(xa, xo, xs)
            if body is not None:
                body(*a, *o, *s)
            for _, fn in ex.relays:
                fn(xa, xo, xs)
            ex.finish(xa, xo, xs)

    kw = dict(grid=grid) if grid else {}
    if ex is not None:
        if ex.aliases:
            kw["input_output_aliases"] = {ni + i: no + o for i, o in ex.aliases.items()}
        ins, in_specs = ins + list(ex.ins), in_specs + [ANY] * nxi
        out_shape, out_specs = out_shape + list(ex.outs), out_specs + [ANY] * nxo
        scratch = scratch + list(ex.sems)
    res = pl.pallas_call(
        call_body, name=name, out_shape=out_shape, in_specs=in_specs, out_specs=out_specs, scratch_shapes=scratch,
        compiler_params=pltpu.CompilerParams(dimension_semantics=("arbitrary",) * len(grid) if grid else None,
                                             vmem_limit_bytes=vmem_mib * VMEM_MIB), **kw)(*ins)
    res = list(res)
    return (res[:no], res[no:]) if ex is not None else res


def _remote(src, dst, ssem, rsem, dev):
    return pltpu.make_async_remote_copy(src_ref=src, dst_ref=dst, send_sem=ssem, recv_sem=rsem,
                                        device_id=dev, device_id_type=MESH)


def gather_exchange(shards, split, relay_early=0):
    n = len(shards)

    def rows(ref, e, kk, half=None):
        R = shards[e].shape[0]
        if half is None:
            return ref.at[pl.ds(pl.multiple_of(kk * R, 8), R)]
        return ref.at[pl.ds(pl.multiple_of(kk * R + half * (R // 2), 8), R // 2)]

    def ici(src, dst, sm, e, j, chip_j, x, y, c):
        k = 2 * x + y
        if split[e]:
            s = src[e].at[pl.ds(pl.multiple_of(c * (shards[e].shape[0] // 2), 8), shards[e].shape[0] // 2)]
            return _remote(s, rows(dst[e], e, k, c), sm[0].at[6 * e + j], sm[1].at[6 * e + j], (*chip_j, c))
        return _remote(src[e], rows(dst[e], e, k), sm[0].at[6 * e + j], sm[1].at[6 * e + j], (*chip_j, c))

    def landed(dst, e, chip_j, c):
        kj = 2 * chip_j[0] + chip_j[1]
        return rows(dst[e], e, kj, c) if split[e] else rows(dst[e], e, kj)

    def forward(dst, sm, e, j, chip_j, x, y, c, sender_c):
        kj = 2 * chip_j[0] + chip_j[1]
        r = rows(dst[e], e, kj, sender_c)
        return _remote(r, r, sm[0].at[6 * e + 3 + j], sm[1].at[6 * e + 3 + j], (x, y, 1 - c))

    def local(src, dst, sm, e, x, y):
        return pltpu.make_async_copy(src[e], rows(dst[e], e, 2 * x + y), sm[2].at[e])

    def start(src, dst, sm):
        x, y, c = _place()
        for e in range(n):
            local(src, dst, sm, e, x, y).start()
            for j, chip_j in enumerate(_other_chips(x, y)):
                ici(src, dst, sm, e, j, chip_j, x, y, c).start()

    def relay(src, dst, sm):
        x, y, c = _place()
        for e in range(n):
            for j, chip_j in enumerate(_other_chips(x, y)):
                r = landed(dst, e, chip_j, c)
                _remote(r, r, sm[0].at[6 * e + j], sm[1].at[6 * e + j], (*chip_j, c)).wait_recv()
                if split[e]:
                    forward(dst, sm, e, j, chip_j, x, y, c, c).start()

    def finish(src, dst, sm):
        x, y, c = _place()
        chips = _other_chips(x, y)
        for e in range(n):
            for j, chip_j in enumerate(chips):
                if split[e]:
                    forward(dst, sm, e, j, chip_j, x, y, c, 1 - c).wait_recv()
        for e in range(n):
            for j, chip_j in enumerate(chips):
                ici(src, dst, sm, e, j, chip_j, x, y, c).wait_send()
                if split[e]:
                    forward(dst, sm, e, j, chip_j, x, y, c, c).wait_send()
            local(src, dst, sm, e, x, y).wait()

    outs = [SDS((4 * s.shape[0], s.shape[1]), s.dtype) for s in shards]
    return Exchange(list(shards), outs, [DMA((6 * n,)), DMA((6 * n,)), DMA((n,))], start, finish, [(relay_early, relay)])


def _block_rows(ref, R, kk, half, quarter=None):
    hr = R // 2
    if quarter is None:
        return ref.at[pl.ds(pl.multiple_of(kk * R + half * hr, 8), hr)]
    return ref.at[pl.ds(pl.multiple_of(kk * R + half * hr + quarter * (hr // 2), 8), hr // 2)]


def gather_near_exchange(shards, relay_early=0):
    n = len(shards)
    R = [s.shape[0] for s in shards]

    def ici(src, dst, sm, e, j, chip_j, x, y, c):
        half = src[e].at[pl.ds(pl.multiple_of(c * (R[e] // 2), 8), R[e] // 2)]
        return _remote(half, _block_rows(dst[e], R[e], 2 * x + y, c), sm[0].at[4 * e + j], sm[1].at[4 * e + j], (*chip_j, c))

    def forward(dst, sm, e, j, chip_j, x, y, c, sender_c):
        r = _block_rows(dst[e], R[e], 2 * chip_j[0] + chip_j[1], sender_c)
        return _remote(r, r, sm[0].at[4 * e + 2 + j], sm[1].at[4 * e + 2 + j], (x, y, 1 - c))

    def local(src, dst, sm, e, x, y):
        return pltpu.make_async_copy(src[e], dst[e].at[pl.ds(pl.multiple_of((2 * x + y) * R[e], 8), R[e])], sm[2].at[e])

    def start(src, dst, sm):
        x, y, c = _place()
        for e in range(n):
            local(src, dst, sm, e, x, y).start()
            for j, chip_j in enumerate(_other_chips(x, y)[:2]):
                ici(src, dst, sm, e, j, chip_j, x, y, c).start()

    def relay(src, dst, sm):
        x, y, c = _place()
        for e in range(n):
            for j, chip_j in enumerate(_other_chips(x, y)[:2]):
                r = _block_rows(dst[e], R[e], 2 * chip_j[0] + chip_j[1], c)
                _remote(r, r, sm[0].at[4 * e + j], sm[1].at[4 * e + j], (*chip_j, c)).wait_recv()
                forward(dst, sm, e, j, chip_j, x, y, c, c).start()

    def finish(src, dst, sm):
        x, y, c = _place()
        near = _other_chips(x, y)[:2]
        for e in range(n):
            for j, chip_j in enumerate(near):
                forward(dst, sm, e, j, chip_j, x, y, c, 1 - c).wait_recv()
        for e in range(n):
            for j, chip_j in enumerate(near):
                ici(src, dst, sm, e, j, chip_j, x, y, c).wait_send()
                forward(dst, sm, e, j, chip_j, x, y, c, c).wait_send()
            local(src, dst, sm, e, x, y).wait()

    outs = [SDS((4 * s.shape[0], s.shape[1]), s.dtype) for s in shards]
    return Exchange(list(shards), outs, [DMA((4 * n,)), DMA((4 * n,)), DMA((n,))], start, finish, [(relay_early, relay)])


def gather_far_exchange(bufs, relay_early=0):
    n = len(bufs)
    R = [b.shape[0] // 4 for b in bufs]

    def send(src, dst, sm, e, j, x, y, c):
        to, of = _other_chips(x, y)[j], _other_chips(x, y)[1 - j]
        kk = 2 * of[0] + of[1]
        return _remote(_block_rows(src[e], R[e], kk, c, j), _block_rows(dst[e], R[e], kk, c, j),
                       sm[0].at[4 * e + j], sm[1].at[4 * e + j], (*to, c))

    def landed(dst, e, j, x, y, half):
        return _block_rows(dst[e], R[e], 2 * (1 - x) + (1 - y), half, j)

    def forward(dst, sm, e, j, x, y, c, sender_c):
        r = landed(dst, e, j, x, y, sender_c)
        return _remote(r, r, sm[0].at[4 * e + 2 + j], sm[1].at[4 * e + 2 + j], (x, y, 1 - c))

    def start(src, dst, sm):
        x, y, c = _place()
        for e in range(n):
            for j in range(2):
                send(src, dst, sm, e, j, x, y, c).start()

    def relay(src, dst, sm):
        x, y, c = _place()
        for e in range(n):
            for j in range(2):
                r = landed(dst, e, j, x, y, c)
                _remote(r, r, sm[0].at[4 * e + j], sm[1].at[4 * e + j], (*_other_chips(x, y)[j], c)).wait_recv()
                forward(dst, sm, e, j, x, y, c, c).start()

    def finish(src, dst, sm):
        x, y, c = _place()
        for e in range(n):
            for j in range(2):
                forward(dst, sm, e, j, x, y, c, 1 - c).wait_recv()
        for e in range(n):
            for j in range(2):
                send(src, dst, sm, e, j, x, y, c).wait_send()
                forward(dst, sm, e, j, x, y, c, c).wait_send()

    outs = [SDS(b.shape, b.dtype) for b in bufs]
    return Exchange(list(bufs), outs, [DMA((4 * n,)), DMA((4 * n,))], start, finish, [(relay_early, relay)],
                    {i: i for i in range(n)})


def gather_two_legs(shards):
    near = gather_near_exchange(shards)
    far = gather_far_exchange(near.outs)

    def finish(src, dst, sm):
        near.relays[0][1](src, dst, sm[:3])
        near.finish(src, dst, sm[:3])
        far.start(dst, dst, sm[3:])
        far.relays[0][1](dst, dst, sm[3:])
        far.finish(dst, dst, sm[3:])

    return Exchange(near.ins, near.outs, list(near.sems) + list(far.sems),
                    lambda src, dst, sm: near.start(src, dst, sm[:3]), finish)


def halves_exchange(grads):
    n = len(grads)

    def copy(g, st, sm, e, x, y, c):
        return _remote(g[e].at[:, 1 - c], st[e], sm[0].at[e], sm[1].at[e], (x, y, 1 - c))

    def start(g, st, sm):
        x, y, c = _place()
        for e in range(n):
            copy(g, st, sm, e, x, y, c).start()

    def finish(g, st, sm):
        x, y, c = _place()
        for e in range(n):
            copy(g, st, sm, e, x, y, c).wait()

    outs = [SDS((4,) + a.shape[2:], a.dtype) for a in grads]
    return Exchange(list(grads), outs, [DMA((n,)), DMA((n,))], start, finish)


def scatter_exchange(parts, relay_before_end=None, want_issue=False):
    n = len(parts)
    by_entry = relay_before_end is not None
    relay_before_end = relay_before_end or [0] * n

    def ici(p, st, sm, e, j, chip_j, x, y, c):
        k, kj = 2 * x + y, 2 * chip_j[0] + chip_j[1]
        return _remote(p[e].at[kj], st[e].at[c, k], sm[0].at[8 * e + j], sm[1].at[8 * e + j], (*chip_j, c))

    def own(p, st, sm, e, x, y, c):
        k = 2 * x + y
        return _remote(p[e].at[k], st[e].at[c, k], sm[0].at[8 * e + 3], sm[1].at[8 * e + 3], (x, y, 1 - c))

    def forward(st, sm, e, j, chip_j, x, y, c, sender_c):
        kj = 2 * chip_j[0] + chip_j[1]
        r = st[e].at[sender_c, kj]
        return _remote(r, r, sm[0].at[8 * e + 4 + j], sm[1].at[8 * e + 4 + j], (x, y, 1 - c))

    def local(p, st, sm, e, x, y, c):
        k = 2 * x + y
        return pltpu.make_async_copy(p[e].at[k], st[e].at[c, k], sm[2].at[e])

    def issue(e, p, st, sm):
        x, y, c = _place()
        for j, chip_j in enumerate(_other_chips(x, y)):
            ici(p, st, sm, e, j, chip_j, x, y, c).start()
        local(p, st, sm, e, x, y, c).start()
        own(p, st, sm, e, x, y, c).start()

    def start(p, st, sm, before_slot=None):
        x, y, c = _place()
        if by_entry:
            for e in range(n):
                issue(e, p, st, sm)
            return
        for j, chip_j in enumerate(_other_chips(x, y)):
            if before_slot is not None:
                before_slot(j, 2 * chip_j[0] + chip_j[1])
            for e in range(n):
                ici(p, st, sm, e, j, chip_j, x, y, c).start()
        if before_slot is not None:
            before_slot(3, 2 * x + y)
        for e in range(n):
            local(p, st, sm, e, x, y, c).start()
            own(p, st, sm, e, x, y, c).start()

    def relay(e, p, st, sm):
        x, y, c = _place()
        for j, chip_j in enumerate(_other_chips(x, y)):
            kj = 2 * chip_j[0] + chip_j[1]
            r = st[e].at[c, kj]
            _remote(r, r, sm[0].at[8 * e + j], sm[1].at[8 * e + j], (*chip_j, c)).wait_recv()
            forward(st, sm, e, j, chip_j, x, y, c, c).start()

    def finish(p, st, sm):
        x, y, c = _place()
        k = 2 * x + y
        chips = _other_chips(x, y)
        for e in range(n):
            r = st[e].at[1 - c, k]
            _remote(r, r, sm[0].at[8 * e + 3], sm[1].at[8 * e + 3], (x, y, 1 - c)).wait_recv()
            for j, chip_j in enumerate(chips):
                forward(st, sm, e, j, chip_j, x, y, c, 1 - c).wait_recv()
        for e in range(n):
            own(p, st, sm, e, x, y, c).wait_send()
            for j, chip_j in enumerate(chips):
                ici(p, st, sm, e, j, chip_j, x, y, c).wait_send()
                forward(st, sm, e, j, chip_j, x, y, c, c).wait_send()
            local(p, st, sm, e, x, y, c).wait()

    outs = [SDS((2,) + a.shape, a.dtype) for a in parts]
    ex = Exchange(list(parts), outs, [DMA((8 * n,)), DMA((8 * n,)), DMA((n,))], start, finish,
                  [(relay_before_end[e], functools.partial(relay, e)) for e in range(n)])
    return (ex, issue) if want_issue else ex


def reduce_scatter_exchange(grads, sibs, nsteps, load_step, send_step, relay_step):
    n = len(grads)
    hrs = [g.shape[2] for g in grads]
    C = grads[0].shape[3]
    scatter, issue = scatter_exchange([SDS((4,) + g.shape[2:], WIRE) for g in grads], want_issue=True)
    hand_on = [fn for _, fn in scatter.relays]

    def refs(xa, xs):
        return xa[:n], xa[n:], xs[:3], xs[3], xs[4], xs[5], xs[6:]

    def loads(e, g, sb, lsem, own_st, sib_st):
        _, _, c = _place()
        into = lambda st: st.at[e % 2, :, pl.ds(0, hrs[e])]
        return (pltpu.make_async_copy(g[e].at[:, c], into(own_st), lsem.at[2 * e]),
                pltpu.make_async_copy(sb[e], into(sib_st), lsem.at[2 * e + 1]))

    def load(e, xa, xo, xs):
        g, sb, _, lsem, own_st, sib_st, _ = refs(xa, xs)
        for cp in loads(e, g, sb, lsem, own_st, sib_st):
            cp.start()

    def send(e, xa, xo, xs):
        g, sb, sm, lsem, own_st, sib_st, part = refs(xa, xs)
        for cp in loads(e, g, sb, lsem, own_st, sib_st):
            cp.wait()
        part[e][...] = (own_st[e % 2, :, 0:hrs[e]] + sib_st[e % 2, :, 0:hrs[e]]).astype(WIRE)
        issue(e, part, xo, sm)

    def relay(e, xa, xo, xs):
        _, _, sm, _, _, _, part = refs(xa, xs)
        hand_on[e](part, xo, sm)

    def finish(xa, xo, xs):
        _, _, sm, _, _, _, part = refs(xa, xs)
        scatter.finish(part, xo, sm)

    plan = sorted([(min(step[e], nsteps - 1), phase, e) for phase, step in enumerate((load_step, send_step, relay_step))
                   for e in range(n)])
    stage = (load, send, relay)
    relays = [(nsteps - 1 - at, functools.partial(stage[phase], e)) for at, phase, e in plan]
    scratch = (list(scatter.sems) + [DMA((2 * n,))] + [pltpu.VMEM((2, 4, max(hrs), C), f32)] * 2
               + [pltpu.VMEM((4, hr, C), WIRE) for hr in hrs])
    return Exchange(list(grads) + list(sibs), scatter.outs, scratch, lambda xa, xo, xs: None, finish, relays)


def tail_reduce(d_norm_mix, d_norm_mem, d_norm_ffn, d_gains, d_cw8, d_cbias, d_qg, d_kg, d_mqg, d_mkg, d_sink8, loss8, tail):
    n = len(tail)
    scatter = scatter_exchange([SDS((4,) + a.shape[2:], WIRE) for a in tail])

    def half_copy(g, sib, hsem, e, j, slot, x, y, c):
        return _remote(g[e].at[slot, 1 - c], sib[e].at[slot], hsem[0].at[4 * e + j], hsem[1].at[4 * e + j], (x, y, 1 - c))

    def body(nm_ref, nmem_ref, nf_ref, gn_ref, cw_ref, cb_ref, qg_ref, kg_ref, mqg_ref, mkg_ref, sk_ref, ls_ref, *rest):
        g, o_ref, st = rest[:n], rest[n], rest[n + 1:2 * n + 1]
        buf, ssem, rsem = rest[2 * n + 1:2 * n + 4]
        own, sib, part = (rest[2 * n + 4 + i * n:2 * n + 4 + (i + 1) * n] for i in range(3))
        lsem = rest[5 * n + 4]
        hsem, xsem = rest[5 * n + 5:5 * n + 7], rest[5 * n + 7:]
        x, y, c = _place()
        loads = [pltpu.make_async_copy(g[e].at[:, c], own[e], lsem.at[e]) for e in range(n)]
        for ld in loads:
            ld.start()
        for j, slot in enumerate([2 * cx + cy for cx, cy in _other_chips(x, y)] + [2 * x + y]):
            for e in range(n):
                half_copy(g, sib, hsem, e, j, slot, x, y, c).start()
        me = 4 * x + 2 * y + c
        mine = buf.at[me]
        mine[...] = jnp.zeros((8, 1024), f32)
        mine[0:1, :] = nm_ref[...]
        mine[1:2, :] = nmem_ref[...]
        mine[2:3, :] = nf_ref[...]
        mine[3:4, :] = gn_ref[...]
        for j in range(3):
            mine[4:5, pl.ds(j * CONV_W, CONV_W)] = cw_ref[j:j + 1, :]
        mine[4:5, pl.ds(3 * CONV_W, CONV_W)] = cb_ref[...]
        for j, r in enumerate((qg_ref, kg_ref, mqg_ref, mkg_ref)):
            mine[5:6, pl.ds(j * HD, HD)] = r[...]
        mine[5:6, pl.ds(256, 128)] = sk_ref[0:1, :]
        mine[5:6, pl.ds(384, 128)] = ls_ref[0:1, :]

        def peer_of(m):
            return (1 - x if m & 4 else x, 1 - y if m & 2 else y, 1 - c if m & 1 else c)

        for m in range(1, 8):
            _remote(mine, mine, ssem.at[m - 1], rsem.at[m - 1], peer_of(m)).start()
        for ld in loads:
            ld.wait()

        def chip_partial(j, slot):
            for e in range(n):
                half_copy(g, sib, hsem, e, j, slot, x, y, c).wait()
                part[e][slot] = (own[e][slot] + sib[e][slot]).astype(WIRE)

        scatter.start(part, st, xsem, chip_partial)
        for _, hand_on in scatter.relays:
            hand_on(part, st, xsem)
        scatter.finish(part, st, xsem)
        for m in range(1, 8):
            p = peer_of(m)
            got = buf.at[4 * p[0] + 2 * p[1] + p[2]]
            _remote(got, got, ssem.at[m - 1], rsem.at[m - 1], p).wait_recv()
        for m in range(1, 8):
            _remote(mine, mine, ssem.at[m - 1], rsem.at[m - 1], peer_of(m)).wait_send()
        acc = buf[0]
        for d in range(1, 8):
            acc = acc + buf[d]
        o_ref[...] = acc

    ins = [d_norm_mix, d_norm_mem, d_norm_ffn, d_gains, d_cw8, d_cbias, d_qg, d_kg, d_mqg, d_mkg, d_sink8, loss8]
    half_shape = [(4,) + a.shape[2:] for a in tail]
    scratch = ([pltpu.VMEM((8, 8, 1024), f32), DMA((7,)), DMA((7,))]
               + [pltpu.VMEM(s, f32) for s in half_shape] * 2 + [pltpu.VMEM(s, WIRE) for s in half_shape]
               + [DMA((n,)), DMA((4 * n,)), DMA((4 * n,))] + list(scatter.sems))
    res = _run("tail_reduce", body, (), ins + list(tail), [VM] * len(ins) + [ANY] * n,
               [SDS((8, 1024), f32)] + list(scatter.outs), [VM] + [ANY] * n, scratch=scratch, vmem_mib=40)
    return res[0], res[1:]


def _adamw_math(w, g, m, v):
    m = ADAM_B1 * m + (1.0 - ADAM_B1) * g
    v = ADAM_B2 * v + (1.0 - ADAM_B2) * (g * g)
    m_hat = m / (1.0 - ADAM_B1 ** ADAM_STEP)
    v_hat = v / (1.0 - ADAM_B2 ** ADAM_STEP)
    delta = -ADAM_LR * (m_hat / (jnp.sqrt(v_hat) + ADAM_EPS) + ADAM_WD * w)
    return delta, m, v


def _sum_chips(st):
    return ((st[0].astype(f32) + st[1].astype(f32)) + st[2].astype(f32)) + st[3].astype(f32)


def adamw_big(name, stages, ws, ms, vs, nstep, exchange=None):
    n = len(stages)

    def body(*refs):
        st, w, m, v = refs[:n], refs[n:2 * n], refs[2 * n:3 * n], refs[3 * n:4 * n]
        outs = refs[4 * n:]
        for e in range(n):
            g = jnp.concatenate([_sum_chips(st[e].at[0]), _sum_chips(st[e].at[1])], axis=0)
            d, mm, vv = _adamw_math(w[e][...], g, m[e][...], v[e][...])
            outs[4 * e][...] = g
            outs[4 * e + 1][...] = d
            outs[4 * e + 2][...] = mm
            outs[4 * e + 3][...] = vv

    st_specs, w_specs = [], []
    for e in range(n):
        _, _, hr, C = stages[e].shape
        st_specs.append(pl.BlockSpec((2, 4, hr, C // nstep), lambda i: (0, 0, 0, i)))
        w_specs.append(pl.BlockSpec((2 * hr, C // nstep), lambda i: (0, i)))
    out_specs = [s for s in w_specs for _ in range(4)]
    out_shape = [SDS(w.shape, f32) for w in ws for _ in range(4)]
    res = _run(name, body, (nstep,), list(stages) + list(ws) + list(ms) + list(vs), st_specs + w_specs * 3,
               out_shape, out_specs, vmem_mib=48, exchange=exchange)
    res, sent = res if exchange is not None else (res, None)
    return [res[4 * e:4 * e + 4] for e in range(n)], sent


def adamw_small(tot, pk_w, pk_m, pk_v, shapes):
    def body(tot_ref, w_ref, m_ref, v_ref, *outs):
        x, y, _ = _place()
        chip = 2 * x + y
        taps = []
        for j in range(3):
            mine = tot_ref[4:5, j * CONV_W:j * CONV_W + HD]
            for s in range(1, 4):
                mine = jnp.where(chip == s, tot_ref[4:5, j * CONV_W + s * HD:j * CONV_W + (s + 1) * HD], mine)
            taps.append(mine)
        row4 = jnp.concatenate(taps + [jnp.zeros((1, 3 * CONV_W - 3 * HD), f32), tot_ref[4:5, 3 * CONV_W:]], axis=1)
        tot_v = tot_ref[...]
        row = lax.broadcasted_iota(jnp.int32, tot_v.shape, 0)
        g = jnp.where(row == 4, jnp.broadcast_to(row4, tot_v.shape), tot_v)
        d, mm, vv = _adamw_math(w_ref[...], g, m_ref[...], v_ref[...])
        for i, name in enumerate(SMALL):
            for k, val in enumerate((g, d, mm, vv)):
                if name == "conv_w":
                    outs[4 * i + k][...] = jnp.concatenate([val[4:5, j * HD:(j + 1) * HD] for j in range(3)], axis=0)[None]
                else:
                    r, c0, w = SMALL_AT[name]
                    outs[4 * i + k][...] = val[r:r + 1, c0:c0 + w]

    out_shape = [SDS(shapes[k], f32) for k in SMALL for _ in range(4)]
    res = _run("adamw_small", body, (), [tot, pk_w, pk_m, pk_v], [VM] * 4, out_shape, [VM] * len(out_shape))
    return {k: res[4 * i:4 * i + 4] for i, k in enumerate(SMALL)}


def prep_weights(name, shards, exchange=None):
    n = len(shards)

    def body(*refs):
        for e in range(n):
            refs[n + e][...] = _c(refs[e][...])

    return _run(name, body, (), shards, [VM] * n, [SDS(a.shape, MXU) for a in shards], [VM] * n, vmem_mib=48, exchange=exchange)


def mem_kv_fwd(mem2d, pk, wmkv):
    M, D = mem2d.shape

    def body(m_ref, pk_ref, w_ref, mn_ref, kv_ref, km_ref, vm_ref):
        m = m_ref[...]
        mn = _c(m * _rstd(m) * _small(pk_ref, "norm_mem"))
        mn_ref[...] = mn
        kv = _nn(mn, w_ref[...])
        kv_ref[...] = kv
        kk = kv[:, :MEM_W]
        km_ref[...] = _c(kk * _heads_rstd(kk) * _lanes(_small(pk_ref, "mem_k_norm"), MEM_W))
        vm_ref[...] = _c(kv[:, MEM_W:])

    return _run("mem_kv_fwd", body, (), [mem2d, pk, wmkv], [VM] * 3,
                [SDS((M, D), MXU), SDS((M, 2 * MEM_W), f32), SDS((M, MEM_W), MXU), SDS((M, MEM_W), MXU)], [VM] * 4)


QKV_W = ATT_W + 2 * KV_W + MEM_W


def in_proj_fwd(x2d, pk, winT, tm, exchange):
    T, D = x2d.shape
    P = winT.shape[0]

    def body(x_ref, pk_ref, w_ref, xn_ref, proj_ref, qkv_ref):
        xv = x_ref[...]
        xn = _c(xv * _rstd(xv) * _small(pk_ref, "norm_mix"))
        xn_ref[...] = xn
        proj = _nt(xn, w_ref[...])
        proj_ref[...] = proj
        q, k = proj[:, :ATT_W], proj[:, ATT_W:ATT_W + KV_W]
        qm = proj[:, P - MEM_W:]
        qkv_ref[...] = jnp.concatenate(
            [_c(q * _heads_rstd(q) * _lanes(_small(pk_ref, "q_norm"), ATT_W)),
             _c(k * _heads_rstd(k) * _lanes(_small(pk_ref, "k_norm"), KV_W)),
             _c(proj[:, ATT_W + KV_W:ATT_W + 2 * KV_W]),
             _c(qm * _heads_rstd(qm) * _lanes(_small(pk_ref, "mem_q_norm"), MEM_W))], axis=1)

    return _run("in_proj_fwd", body, (T // tm,), [x2d, pk, winT],
                [pl.BlockSpec((tm, D), lambda i: (i, 0)), VM, VM],
                [SDS((T, D), MXU), SDS((T, P), f32), SDS((T, QKV_W), MXU)],
                [pl.BlockSpec((tm, D), lambda i: (i, 0)), pl.BlockSpec((tm, P), lambda i: (i, 0)),
                 pl.BlockSpec((tm, QKV_W), lambda i: (i, 0))],
                vmem_mib=40, exchange=exchange)


def _swa_bias_table():
    r = np.arange(GQA * BLK)[:, None]
    k = np.arange(2 * BLK)[None, :]
    dist = (r % BLK) + BLK - k
    band = (dist >= 0) & (dist < BLK)
    tab = np.empty((2, N_KV, GQA * BLK, 2 * BLK), np.float32)
    for later in range(2):
        valid = band & ((k >= BLK) | (later == 1))
        for g in range(N_KV):
            slope = 2.0 ** -(g * GQA + r // BLK + 1.0)
            tab[later, g] = np.where(valid, -slope * dist, NEG)
    return jnp.asarray(tab)


def _sink_column(g, sk_ref):
    hrow = lax.broadcasted_iota(jnp.int32, (GQA * BLK, 1), 0) // BLK
    sink = jnp.zeros((GQA * BLK, 1), f32)
    for hh in range(GQA):
        sink = jnp.where(hrow == hh, sk_ref[g * GQA + hh:g * GQA + hh + 1, 0:1], sink)
    return sink


def _stack_heads(v, g):
    return jnp.concatenate([v[:, (g * GQA + hh) * HD:(g * GQA + hh + 1) * HD] for hh in range(GQA)], axis=0)


def attn_fwd(qkv, sink_rows, BL, S, exchange, qb=2):
    NS = S // (qb * BLK)
    T = BL * S

    def body(q_ref, kc_ref, kp_ref, vc_ref, vp_ref, sk_ref, tab_ref, o_ref):
        j = pl.program_id(1)
        kall = jnp.concatenate([kp_ref[...], kc_ref[...]], axis=0)
        vall = jnp.concatenate([vp_ref[...], vc_ref[...]], axis=0)
        ones = jnp.ones((2 * BLK, HD), MXU)
        for b in range(qb):
            q = q_ref[pl.ds(b * BLK, BLK), :]
            k2, v2 = kall[b * BLK:(b + 2) * BLK], vall[b * BLK:(b + 2) * BLK]
            later = jnp.minimum(j, 1) if b == 0 else 1
            for g in range(N_KV):
                kn, vh = k2[:, g * HD:(g + 1) * HD], v2[:, g * HD:(g + 1) * HD]
                s = _nt(_stack_heads(q, g), kn) * (HD ** -0.5) + tab_ref[later, g]
                e, es = _exp_scores(s, _sink_column(g, sk_ref))
                eb = _c(e)
                o = _nn(eb, vh) * (1.0 / (_nn(eb, ones) + es))
                for hh in range(GQA):
                    o_ref[pl.ds(b * BLK, BLK), pl.ds((g * GQA + hh) * HD, HD)] = o[hh * BLK:(hh + 1) * BLK]

    cur = lambda col: (lambda b, j: (b * NS + j, col))
    prev = lambda col: (lambda b, j: (qb * (b * NS + j) - jnp.minimum(j, 1), col))
    return _run("attn_fwd", body, (BL, NS), [qkv, qkv, qkv, qkv, qkv, sink_rows, _swa_bias_table()],
                [pl.BlockSpec((qb * BLK, ATT_W), cur(0)),
                 pl.BlockSpec((qb * BLK, KV_W), cur(4)), pl.BlockSpec((BLK, KV_W), prev(4)),
                 pl.BlockSpec((qb * BLK, KV_W), cur(5)), pl.BlockSpec((BLK, KV_W), prev(5)),
                 pl.BlockSpec((8, 128), lambda b, j: (0, 0)), VM],
                [SDS((T, ATT_W), f32)], [pl.BlockSpec((qb * BLK, ATT_W), cur(0))], exchange=exchange)


def _conv_taps(u, uh):
    row = lax.broadcasted_iota(jnp.int32, u.shape, 0)
    u1 = jnp.where(row == 0, uh[7:8, :], pltpu.roll(u, 1, 0))
    u2 = jnp.where(row == 0, uh[6:7, :], jnp.where(row == 1, uh[7:8, :], pltpu.roll(u, 2, 0)))
    return u1, u2


def _mem_head(qm, km, vm, h):
    qh, kh, vh = (a[:, h * HD:(h + 1) * HD] for a in (qm, km, vm))
    e, _ = _exp_scores(_nt(qh, kh) * (HD ** -0.5))
    return qh, kh, vh, e


def mixer_tail_fwd(x2d, attn_out, proj, qkv, km, vm, conv_w8, pk, wout, S, tm, exchange):
    T, D = x2d.shape
    NM = km.shape[0] // (T // S)

    def body(x_ref, ao_ref, ch_ref, cb_ref, cc_ref, chh_ref, cch_ref, qm_ref, km_ref, vm_ref, cw_ref, pk_ref,
             wout_ref, co_ref, mo_ref, mg_ref, x1_ref, h_ref):
        first = (pl.program_id(0) * tm) % S == 0
        u = cc_ref[...] * ch_ref[...]
        uh = jnp.where(first, 0.0, cch_ref[...] * chh_ref[...])
        u1, u2 = _conv_taps(u, uh)
        conv = cw_ref[0:1, :] * u2 + cw_ref[1:2, :] * u1 + cw_ref[2:3, :] * u + _small(pk_ref, "conv_b")
        conv_out = cb_ref[...] * conv
        co_ref[...] = conv_out
        qm, kmv, vmv = qm_ref[...], km_ref[...], vm_ref[...]
        ones = jnp.ones((NM, HD), MXU)
        for h in range(N_MEMH):
            _, _, vh, e = _mem_head(qm, kmv, vmv, h)
            eb = _c(e)
            mo_ref[:, pl.ds(h * HD, HD)] = _nn(eb, vh) * (1.0 / _nn(eb, ones))
        mem_out = mo_ref[...]
        ao = ao_ref[...]
        merged = _c(jnp.concatenate([ao * _rstd(ao) * _small(pk_ref, "out_norm_attn"),
                                     conv_out * _rstd(conv_out) * _small(pk_ref, "out_norm_conv"),
                                     mem_out * _rstd(mem_out) * _small(pk_ref, "out_norm_mem")], axis=1))
        mg_ref[...] = merged
        x1 = x_ref[...] + _nn(merged, wout_ref[...])
        x1_ref[...] = x1
        h_ref[...] = _c(x1 * _rstd(x1) * _small(pk_ref, "norm_ffn"))

    tile = lambda w, col: pl.BlockSpec((tm, w), lambda i: (i, col))
    halo = lambda col: pl.BlockSpec((8, CONV_W), lambda i: (jnp.maximum(i * (tm // 8) - 1, 0), col))
    seq = pl.BlockSpec((NM, MEM_W), lambda i: ((i * tm) // S, 0))
    small = lambda a: pl.BlockSpec(a.shape, lambda i: (0, 0))
    return _run("mixer_tail_fwd", body, (T // tm,),
                [x2d, attn_out, proj, proj, proj, proj, proj, qkv, km, vm, conv_w8, pk, wout],
                [tile(D, 0), tile(ATT_W, 0), tile(CONV_W, 3), tile(CONV_W, 4), tile(CONV_W, 5), halo(3), halo(5),
                 tile(MEM_W, 3), seq, seq, VM, VM, VM],
                [SDS((T, CONV_W), f32), SDS((T, MEM_W), f32), SDS((T, D), MXU), SDS((T, D), f32), SDS((T, D), MXU)],
                [tile(CONV_W, 0), tile(MEM_W, 0), tile(D, 0), tile(D, 0), tile(D, 0)], vmem_mib=40, exchange=exchange)


def ffn_fwd_bwd(h, x1, tgt, wgT, wuT, wd, pk, tm):
    T, D = x1.shape
    F = wd.shape[0]

    def body(h_ref, x1_ref, t_ref, wg_ref, wu_ref, wd_ref, pk_ref,
             dx1_ref, dx2_ref, act_ref, dg_ref, du_ref, loss_ref, dgf_ref):
        @pl.when(pl.program_id(0) == 0)
        def _():
            loss_ref[...] = jnp.zeros_like(loss_ref)
            dgf_ref[...] = jnp.zeros_like(dgf_ref)

        hv = h_ref[...]
        gate = _nt(hv, wg_ref[...])
        up = _nt(hv, wu_ref[...])
        sg = jax.nn.sigmoid(gate)
        sl = gate * sg
        act = _c(sl * up)
        act_ref[...] = act
        x1v = x1_ref[...]
        diff = (x1v + _nn(act, wd_ref[...])) - t_ref[...]
        loss_ref[...] += 0.5 * jnp.sum(jnp.sum(diff * diff, axis=-1, keepdims=True) / D, axis=0, keepdims=True)
        dx2 = diff / D
        dx2b = _c(dx2)
        dx2_ref[...] = dx2b
        d_act = _nt(dx2b, wd_ref[...])
        d_up = _c(d_act * sl)
        d_gate = _c(d_act * up * (sg * (1.0 + gate * (1.0 - sg))))
        du_ref[...] = d_up
        dg_ref[...] = d_gate
        dh = _nn(d_gate, wg_ref[...]) + _nn(d_up, wu_ref[...])
        dv, dgf = _norm_bwd(dh, x1v, _rstd(x1v), _small(pk_ref, "norm_ffn"))
        dx1_ref[...] = dx2 + dv
        dgf_ref[...] += dgf

    tile = lambda w: pl.BlockSpec((tm, w), lambda i: (i, 0))
    return _run("ffn_fwd_bwd", body, (T // tm,), [h, x1, tgt, wgT, wuT, wd, pk],
                [tile(D), tile(D), tile(D), VM, VM, VM, VM],
                [SDS((T, D), f32), SDS((T, D), MXU), SDS((T, F), MXU), SDS((T, F), MXU), SDS((T, F), MXU),
                 SDS((8, 128), f32), SDS((1, D), f32)],
                [tile(D), tile(D), tile(F), tile(F), tile(F), pl.BlockSpec((8, 128), lambda i: (0, 0)),
                 pl.BlockSpec((1, D), lambda i: (0, 0))], vmem_mib=56)


def matmul_tn(a, b, name, tmo, tk):
    T, M = a.shape
    N = b.shape[1]

    def body(a_ref, b_ref, o_ref):
        @pl.when(pl.program_id(1) == 0)
        def _():
            o_ref[...] = jnp.zeros_like(o_ref)

        o_ref[...] += _tn(a_ref[...], b_ref[...])

    return _run(name, body, (M // tmo, T // tk), [a, b],
                [pl.BlockSpec((tk, tmo), lambda m, k: (k, m)), pl.BlockSpec((tk, N), lambda m, k: (k, 0))],
                [SDS((M, N), f32)], [pl.BlockSpec((tmo, N), lambda m, k: (m, 0))], vmem_mib=48)[0]


def out_proj_bwd(dx1, merged, attn_out, conv_out, mem_out, pk, wout, tm):
    T, D = dx1.shape

    def body(dx1_ref, mg_ref, ao_ref, co_ref, mo_ref, pk_ref, w_ref,
             dao_ref, dco_ref, dmo_ref, dw_ref, dgain_ref):
        @pl.when(pl.program_id(0) == 0)
        def _():
            dw_ref[...] = jnp.zeros_like(dw_ref)
            dgain_ref[...] = jnp.zeros_like(dgain_ref)

        dxb = _c(dx1_ref[...])
        dw_ref[...] += _tn(mg_ref[...], dxb)
        dmg = _nt(dxb, w_ref[...])
        ao, co, mo = ao_ref[...], co_ref[...], mo_ref[...]
        da, ga = _norm_bwd(dmg[:, :ATT_W], ao, _rstd(ao), _small(pk_ref, "out_norm_attn"))
        dc, gc = _norm_bwd(dmg[:, ATT_W:ATT_W + CONV_W], co, _rstd(co), _small(pk_ref, "out_norm_conv"))
        dm, gm = _norm_bwd(dmg[:, ATT_W + CONV_W:], mo, _rstd(mo), _small(pk_ref, "out_norm_mem"))
        dao_ref[...] = da
        dco_ref[...] = dc
        dmo_ref[...] = dm
        dgain_ref[...] += jnp.concatenate([ga, gc, gm], axis=1)

    tile = lambda w: pl.BlockSpec((tm, w), lambda i: (i, 0))
    return _run("out_proj_bwd", body, (T // tm,), [dx1, merged, attn_out, conv_out, mem_out, pk, wout],
                [tile(D), tile(D), tile(ATT_W), tile(CONV_W), tile(MEM_W), VM, VM],
                [SDS((T, ATT_W), f32), SDS((T, CONV_W), f32), SDS((T, MEM_W), f32), SDS((D, D), f32), SDS((1, D), f32)],
                [tile(ATT_W), tile(CONV_W), tile(MEM_W), pl.BlockSpec((D, D), lambda i: (0, 0)),
                 pl.BlockSpec((1, D), lambda i: (0, 0))], vmem_mib=40)


def attn_bwd(qkv, d_attn, attn_out, sink_rows, BL, S, exchange):
    NB = S // BLK
    T = BL * S

    def body(q_ref, kc_ref, kp_ref, vc_ref, vp_ref, do_ref, ao_ref, sk_ref, tab_ref,
             dq_ref, dk_ref, dv_ref, dsk_ref, pend_k, pend_v):
        b, j = pl.program_id(0), pl.program_id(1)

        @pl.when((b == 0) & (j == 0))
        def _():
            dsk_ref[...] = jnp.zeros_like(dsk_ref)

        @pl.when(j == 0)
        def _():
            pend_k[...] = jnp.zeros_like(pend_k)
            pend_v[...] = jnp.zeros_like(pend_v)

        @pl.when(j < NB)
        def _():
            q, do, ao = q_ref[...], do_ref[...], ao_ref[...]
            k2 = jnp.concatenate([kp_ref[...], kc_ref[...]], axis=0)
            v2 = jnp.concatenate([vp_ref[...], vc_ref[...]], axis=0)
            lane = lax.broadcasted_iota(jnp.int32, (8, 128), 1)
            ones_w = jnp.ones((2 * BLK, 2 * BLK), MXU)
            dsk = jnp.zeros((8, 128), f32)
            dks, dvs = [], []
            for g in range(N_KV):
                kn, vh = k2[:, g * HD:(g + 1) * HD], v2[:, g * HD:(g + 1) * HD]
                qs = _stack_heads(q, g)
                s = _nt(qs, kn) * (HD ** -0.5) + tab_ref[g]
                e, es = _exp_scores(s, _sink_column(g, sk_ref))
                eb = _c(e)
                inv_w = 1.0 / (_nn(eb, ones_w) + es)
                inv_n = inv_w[:, :HD]
                dos = _stack_heads(do, g)
                delta = _rowsum_mxu(dos * _stack_heads(ao, g), 2 * BLK)
                dp = _nt(_c(dos), vh)
                ds = _c(e * inv_w * (dp - delta) * (HD ** -0.5))
                t = es * inv_n[:, 0:1] * delta[:, 0:1]
                for hh in range(GQA):
                    dsk = dsk + jnp.where(lane == g * GQA + hh, -jnp.sum(t[hh * BLK:(hh + 1) * BLK]), 0.0)
                dvs.append(_tn(eb, _c(dos * inv_n)))
                dks.append(_tn(ds, qs))
                dqs = _nn(ds, kn)
                for hh in range(GQA):
                    dq_ref[:, pl.ds((g * GQA + hh) * HD, HD)] = dqs[hh * BLK:(hh + 1) * BLK]
            dk2 = jnp.concatenate(dks, axis=1)
            dv2 = jnp.concatenate(dvs, axis=1)
            dk_ref[...] = pend_k[...] + dk2[:BLK]
            dv_ref[...] = pend_v[...] + dv2[:BLK]
            pend_k[...] = dk2[BLK:]
            pend_v[...] = dv2[BLK:]
            dsk_ref[...] += dsk

        @pl.when(j == NB)
        def _():
            dk_ref[...] = pend_k[...]
            dv_ref[...] = pend_v[...]

    cur = lambda col: (lambda b, j: (b * NB + jnp.minimum(j, NB - 1), col))
    prev = lambda col: (lambda b, j: (b * NB + jnp.maximum(j - 1, 0), col))
    small = lambda shape: pl.BlockSpec(shape, lambda b, j: (0, 0))
    return _run("attn_bwd", body, (BL, NB + 1), [qkv, qkv, qkv, qkv, qkv, d_attn, attn_out, sink_rows, _swa_bias_table()],
                [pl.BlockSpec((BLK, ATT_W), cur(0)),
                 pl.BlockSpec((BLK, KV_W), cur(4)), pl.BlockSpec((BLK, KV_W), prev(4)),
                 pl.BlockSpec((BLK, KV_W), cur(5)), pl.BlockSpec((BLK, KV_W), prev(5)),
                 pl.BlockSpec((BLK, ATT_W), cur(0)), pl.BlockSpec((BLK, ATT_W), cur(0)), small((8, 128)),
                 pl.BlockSpec((None, N_KV, GQA * BLK, 2 * BLK), lambda b, j: (jnp.minimum(j, 1), 0, 0, 0))],
                [SDS((T, ATT_W), f32), SDS((T, KV_W), f32), SDS((T, KV_W), f32), SDS((8, 128), f32)],
                [pl.BlockSpec((BLK, ATT_W), cur(0)), pl.BlockSpec((BLK, KV_W), prev(0)),
                 pl.BlockSpec((BLK, KV_W), prev(0)), small((8, 128))],
                scratch=[pltpu.VMEM((BLK, KV_W), f32)] * 2, vmem_mib=56, exchange=exchange)


def mem_conv_bwd(d_mem_out, mem_out, d_conv_out, proj, qkv, km, vm, conv_w8, pk, S, tm, exchange):
    T = d_mem_out.shape[0]
    NM = km.shape[0] // (T // S)

    def body(dmo_ref, mo_ref, dco_ref, ch_ref, cb_ref, cc_ref, chh_ref, cch_ref, qm_ref, km_ref, vm_ref, cw_ref,
             pk_ref, dqm_ref, dkm_ref, dvm_ref, dcb_ref, dcv_ref, dcw_ref, dcbias_ref):
        i = pl.program_id(0)
        first = (i * tm) % S == 0

        @pl.when(i == 0)
        def _():
            dcw_ref[...] = jnp.zeros_like(dcw_ref)
            dcbias_ref[...] = jnp.zeros_like(dcbias_ref)

        @pl.when(first)
        def _():
            dkm_ref[...] = jnp.zeros_like(dkm_ref)
            dvm_ref[...] = jnp.zeros_like(dvm_ref)

        qm, kmv, vmv, dmo, mo = qm_ref[...], km_ref[...], vm_ref[...], dmo_ref[...], mo_ref[...]
        ones_w = jnp.ones((NM, NM), MXU)
        for h in range(N_MEMH):
            qh, kh, vh, e = _mem_head(qm, kmv, vmv, h)
            eb = _c(e)
            doh = dmo[:, h * HD:(h + 1) * HD]
            delta = _rowsum_mxu(doh * mo[:, h * HD:(h + 1) * HD], NM)
            dp = _nt(_c(doh), vh)
            inv_w = 1.0 / _nn(eb, ones_w)
            ds = _c(e * inv_w * (dp - delta) * (HD ** -0.5))
            dvm_ref[:, pl.ds(h * HD, HD)] += _tn(eb, _c(doh * inv_w[:, :HD]))
            dkm_ref[:, pl.ds(h * HD, HD)] += _tn(ds, qh)
            dqm_ref[:, pl.ds(h * HD, HD)] = _nn(ds, kh)

        u = cc_ref[...] * ch_ref[...]
        uh = jnp.where(first, 0.0, cch_ref[...] * chh_ref[...])
        u1, u2 = _conv_taps(u, uh)
        conv = cw_ref[0:1, :] * u2 + cw_ref[1:2, :] * u1 + cw_ref[2:3, :] * u + _small(pk_ref, "conv_b")
        dy = dco_ref[...]
        dcb_ref[...] = dy * conv
        dcv = dy * cb_ref[...]
        dcv_ref[...] = dcv
        dcbias_ref[...] += jnp.sum(dcv, axis=0, keepdims=True)
        dcw_ref[0:1, :] += jnp.sum(dcv * u2, axis=0, keepdims=True)
        dcw_ref[1:2, :] += jnp.sum(dcv * u1, axis=0, keepdims=True)
        dcw_ref[2:3, :] += jnp.sum(dcv * u, axis=0, keepdims=True)

    tile = lambda w, col: pl.BlockSpec((tm, w), lambda i: (i, col))
    halo = lambda col: pl.BlockSpec((8, CONV_W), lambda i: (jnp.maximum(i * (tm // 8) - 1, 0), col))
    seq = pl.BlockSpec((NM, MEM_W), lambda i: ((i * tm) // S, 0))
    const = lambda shape: pl.BlockSpec(shape, lambda i: (0, 0))
    return _run("mem_conv_bwd", body, (T // tm,),
                [d_mem_out, mem_out, d_conv_out, proj, proj, proj, proj, proj, qkv, km, vm, conv_w8, pk],
                [tile(MEM_W, 0), tile(MEM_W, 0), tile(CONV_W, 0), tile(CONV_W, 3), tile(CONV_W, 4), tile(CONV_W, 5),
                 halo(3), halo(5), tile(MEM_W, 3), seq, seq, VM, VM],
                [SDS((T, MEM_W), f32), SDS(km.shape, f32), SDS(km.shape, f32),
                 SDS((T, CONV_W), f32), SDS((T, CONV_W), f32), SDS((8, CONV_W), f32), SDS((1, CONV_W), f32)],
                [tile(MEM_W, 0), seq, seq, tile(CONV_W, 0), tile(CONV_W, 0), const((8, CONV_W)), const((1, CONV_W))],
                vmem_mib=48, exchange=exchange)


def in_proj_bwd(dqn, dkn, dv, dcb, dcv, dqmn, proj, conv_w8, xn, x2d, dx1, pk, winT, S, tm, stages, ws, ms, vs):
    T, D = x2d.shape
    P = winT.shape[0]
    last_blk = T // 8 - 1
    n = len(stages)
    nsteps = T // tm
    tile_w = ws[0].shape[1] // (nsteps // 2)
    turn = [e * 2 // n for e in range(n)]

    def body(dq_ref, dk_ref, dv_ref, dcb_ref, dcv_ref, dcvn_ref, dqm_ref, qa_ref, ka_ref, ch_ref, cc_ref, qma_ref,
             cw_ref, xn_ref, x_ref, dx1_ref, pk_ref, w_ref, *rest):
        st, aw, am, av = (rest[k * n:(k + 1) * n] for k in range(4))
        dx_ref, dw_ref, dg_ref, dqg_ref, dkg_ref, dmqg_ref = rest[4 * n:4 * n + 6]
        aouts = rest[4 * n + 6:]
        i = pl.program_id(0)

        for parity in range(2):
            @pl.when(i % 2 == parity)
            def _(parity=parity):
                for e in range(n):
                    if turn[e] == parity:
                        g = jnp.concatenate([_sum_chips(st[e].at[0]), _sum_chips(st[e].at[1])], axis=0)
                        d, mm, vv = _adamw_math(aw[e][...], g, am[e][...], av[e][...])
                        for k, val in enumerate((g, d, mm, vv)):
                            aouts[4 * e + k][...] = val

        @pl.when(i == 0)
        def _():
            dw_ref[...] = jnp.zeros_like(dw_ref)
            dg_ref[...] = jnp.zeros_like(dg_ref)
            dqg_ref[...] = jnp.zeros_like(dqg_ref)
            dkg_ref[...] = jnp.zeros_like(dkg_ref)
            dmqg_ref[...] = jnp.zeros_like(dmqg_ref)

        dqa, gq = _heads_norm_bwd(dq_ref[...], qa_ref[...], _small(pk_ref, "q_norm"))
        dka, gk = _heads_norm_bwd(dk_ref[...], ka_ref[...], _small(pk_ref, "k_norm"))
        dqma, gmq = _heads_norm_bwd(dqm_ref[...], qma_ref[...], _small(pk_ref, "mem_q_norm"))
        dqg_ref[...] += gq
        dkg_ref[...] += gk
        dmqg_ref[...] += gmq

        last = ((i + 1) * tm) % S == 0
        dcv = dcv_ref[...]
        nxt = jnp.where(last, 0.0, dcvn_ref[...])
        row = lax.broadcasted_iota(jnp.int32, dcv.shape, 0)
        n1 = jnp.where(row == tm - 1, nxt[0:1, :], pltpu.roll(dcv, tm - 1, 0))
        n2 = jnp.where(row == tm - 2, nxt[0:1, :], jnp.where(row == tm - 1, nxt[1:2, :], pltpu.roll(dcv, tm - 2, 0)))
        du = cw_ref[2:3, :] * dcv + cw_ref[1:2, :] * n1 + cw_ref[0:1, :] * n2
        d_proj = jnp.concatenate([_c(dqa), _c(dka), _c(dv_ref[...]), _c(du * cc_ref[...]),
                                  _c(dcb_ref[...]), _c(du * ch_ref[...]), _c(dqma)], axis=1)
        dw_ref[...] += _tn(d_proj, xn_ref[...])
        xv = x_ref[...]
        dv_, dg = _norm_bwd(_nn(d_proj, w_ref[...]), xv, _rstd(xv), _small(pk_ref, "norm_mix"))
        dx_ref[...] = dx1_ref[...] + dv_
        dg_ref[...] += dg

    tile = lambda w, col=0: pl.BlockSpec((tm, w), lambda i: (i, col))
    nhalo = pl.BlockSpec((8, CONV_W), lambda i: (jnp.minimum((i + 1) * (tm // 8), last_blk), 0))
    const = lambda shape: pl.BlockSpec(shape, lambda i: (0, 0))
    st_specs = [pl.BlockSpec((2, 4, s.shape[2], tile_w), lambda i: (0, 0, 0, i // 2)) for s in stages]
    w_specs = [pl.BlockSpec((w.shape[0], tile_w), lambda i: (0, i // 2)) for w in ws]
    res = _run("in_proj_bwd", body, (nsteps,),
               [dqn, dkn, dv, dcb, dcv, dcv, dqmn, proj, proj, proj, proj, proj, conv_w8, xn, x2d, dx1, pk, winT]
               + list(stages) + list(ws) + list(ms) + list(vs),
               [tile(ATT_W), tile(KV_W), tile(KV_W), tile(CONV_W), tile(CONV_W), nhalo, tile(MEM_W),
                tile(ATT_W, 0), tile(KV_W, 4), tile(CONV_W, 3), tile(CONV_W, 5), tile(MEM_W, 6), VM,
                tile(D), tile(D), tile(D), VM, VM] + st_specs + w_specs * 3,
               [SDS((T, D), f32), SDS((P, D), f32), SDS((1, D), f32), SDS((1, HD), f32), SDS((1, HD), f32),
                SDS((1, HD), f32)] + [SDS(w.shape, f32) for w in ws for _ in range(4)],
               [tile(D), pl.BlockSpec((P, D), lambda i: (0, 0)), const((1, D)), const((1, HD)), const((1, HD)),
                const((1, HD))] + [s for s in w_specs for _ in range(4)],
               vmem_mib=56)
    return res[:6], [res[6 + 4 * e:10 + 4 * e] for e in range(n)]


def mem_kv_bwd(dkm, dvm, kv, memn, mem2d, pk, wmkv):
    def body(dkm_ref, dvm_ref, kv_ref, mn_ref, m_ref, pk_ref, w_ref, dw_ref, dg_ref, dkg_ref):
        dkk, dkg = _heads_norm_bwd(dkm_ref[...], kv_ref[:, :MEM_W], _small(pk_ref, "mem_k_norm"))
        dkg_ref[...] = dkg
        dkv = _c(jnp.concatenate([dkk, dvm_ref[...]], axis=1))
        dw_ref[...] = _tn(mn_ref[...], dkv)
        mv = m_ref[...]
        dg_ref[...] = jnp.sum(_nt(dkv, w_ref[...]) * mv * _rstd(mv), axis=0, keepdims=True)

    return _run("mem_kv_bwd", body, (), [dkm, dvm, kv, memn, mem2d, pk, wmkv], [VM] * 7,
                [SDS(wmkv.shape, f32), SDS((1, mem2d.shape[1]), f32), SDS((1, HD), f32)], [VM] * 3, vmem_mib=40)


def _halves_view(g):
    return g.reshape(4, 2, g.shape[0] // 8, g.shape[1])


def kernel(x, mem, norm_mix, w_in, q_norm, k_norm, attn_sinks, conv_w, conv_b, norm_mem, w_mem_kv, mem_q_norm, mem_k_norm, out_norm_attn, out_norm_conv, out_norm_mem, w_out, norm_ffn, w_gate, w_up, w_down, loss_target, m_norm_mix, m_w_in, m_q_norm, m_k_norm, m_attn_sinks, m_conv_w, m_conv_b, m_norm_mem, m_w_mem_kv, m_mem_q_norm, m_mem_k_norm, m_out_norm_attn, m_out_norm_conv, m_out_norm_mem, m_w_out, m_norm_ffn, m_w_gate, m_w_up, m_w_down, v_norm_mix, v_w_in, v_q_norm, v_k_norm, v_attn_sinks, v_conv_w, v_conv_b, v_norm_mem, v_w_mem_kv, v_mem_q_norm, v_mem_k_norm, v_out_norm_attn, v_out_norm_conv, v_out_norm_mem, v_w_out, v_norm_ffn, v_w_gate, v_w_up, v_w_down):
    BL, S, D = x.shape
    T = BL * S
    TM = 256
    TM_BIG = min(512, S)
    w_small = dict(norm_mix=norm_mix, norm_mem=norm_mem, norm_ffn=norm_ffn, out_norm_attn=out_norm_attn,
                   out_norm_conv=out_norm_conv, out_norm_mem=out_norm_mem, conv_w=conv_w, conv_b=conv_b, q_norm=q_norm,
                   k_norm=k_norm, mem_q_norm=mem_q_norm, mem_k_norm=mem_k_norm, attn_sinks=attn_sinks)
    m_small = dict(norm_mix=m_norm_mix, norm_mem=m_norm_mem, norm_ffn=m_norm_ffn, out_norm_attn=m_out_norm_attn,
                   out_norm_conv=m_out_norm_conv, out_norm_mem=m_out_norm_mem, conv_w=m_conv_w, conv_b=m_conv_b,
                   q_norm=m_q_norm, k_norm=m_k_norm, mem_q_norm=m_mem_q_norm, mem_k_norm=m_mem_k_norm,
                   attn_sinks=m_attn_sinks)
    v_small = dict(norm_mix=v_norm_mix, norm_mem=v_norm_mem, norm_ffn=v_norm_ffn, out_norm_attn=v_out_norm_attn,
                   out_norm_conv=v_out_norm_conv, out_norm_mem=v_out_norm_mem, conv_w=v_conv_w, conv_b=v_conv_b,
                   q_norm=v_q_norm, k_norm=v_k_norm, mem_q_norm=v_mem_q_norm, mem_k_norm=v_mem_k_norm,
                   attn_sinks=v_attn_sinks)
    pk = _pack_small(w_small)

    rowblocks = lambda a, b, c, d, e, f: [a[0].T, b[0].T, c[0].T, d[0], e[0], f[0]]
    w_rb = rowblocks(w_in, w_gate, w_up, w_down, w_out, w_mem_kv)
    m_rb = rowblocks(m_w_in, m_w_gate, m_w_up, m_w_down, m_w_out, m_w_mem_kv)
    v_rb = rowblocks(v_w_in, v_w_gate, v_w_up, v_w_down, v_w_out, v_w_mem_kv)
    (winT_s,) = prep_weights("prep_w_in", w_rb[:1])
    cw_pad = jnp.zeros((8, 128), f32).at[:3, :HD].set(conv_w[0])
    (wgT_s, wuT_s, wd_s, wout_s, wmkv_s), (winT, cw_all) = prep_weights(
        "gather_w_in", w_rb[1:], _together([gather_two_legs([winT_s]), gather_exchange([cw_pad], [False])]))
    conv_w_full = jnp.transpose(cw_all.reshape(4, 8, 128)[:, :3, :HD], (1, 0, 2)).reshape(3, CONV_W)
    conv_w8 = jnp.zeros((8, CONV_W), f32).at[:3].set(conv_w_full)
    sink_rows = jnp.broadcast_to(attn_sinks.reshape(N_Q, 1), (N_Q, 128))

    x2d = x.reshape(T, D)
    mem2d = mem.reshape(-1, D)
    (xn, proj, qkv), near1 = in_proj_fwd(x2d, pk, winT, TM_BIG, gather_near_exchange([wgT_s, wout_s, wmkv_s], relay_early=1))
    (attn_out,), (wgT, wout, wmkv, *near2) = attn_fwd(
        qkv, sink_rows, BL, S, _together([gather_far_exchange(near1, relay_early=2), gather_near_exchange([wuT_s, wd_s], relay_early=2)]))
    memn, kv, km, vm = mem_kv_fwd(mem2d, pk, wmkv)
    (conv_out, mem_out, merged, x1, h), (wuT, wd) = mixer_tail_fwd(
        x2d, attn_out, proj, qkv, km, vm, conv_w8, pk, wout, S, TM_BIG, gather_far_exchange(near2, relay_early=2))

    dx1, dx2b, act, d_gate, d_up, loss8, d_norm_ffn = ffn_fwd_bwd(h, x1, loss_target.reshape(T, D), wgT, wuT, wd, pk, TM)
    F = wd.shape[0]
    g_wd = matmul_tn(act, dx2b, "dw_down", F // 2, min(T, 1024))
    g_wgT = matmul_tn(d_gate, h, "dw_gate", F // 2, min(T, 1024))
    g_wuT = matmul_tn(d_up, h, "dw_up", F // 2, min(T, 1024))

    d_attn, d_conv_out, d_mem_out, g_wout, d_gains = out_proj_bwd(dx1, merged, attn_out, conv_out, mem_out, pk, wout, TM_BIG)
    late = [_halves_view(g) for g in (g_wgT, g_wuT, g_wd, g_wout)]
    (dqmn, dkm, dvm, dcb, dcv, d_cw8, d_cbias), late_sib = mem_conv_bwd(
        d_mem_out, mem_out, d_conv_out, proj, qkv, km, vm, conv_w8, pk, S, min(1024, S), halves_exchange(late))
    (dqn, dkn, dv, d_sink8), late_stage = attn_bwd(
        qkv, d_attn, attn_out, sink_rows, BL, S,
        reduce_scatter_exchange(late, late_sib, BL * (S // BLK + 1), load_step=[0, 3, 6, 9], send_step=[1, 5, 8, 11],
                                relay_step=[13, 22, 31, 33]))
    (g_x, g_winT, d_norm_mix, d_qg, d_kg, d_mqg), late_res = in_proj_bwd(
        dqn, dkn, dv, dcb, dcv, dqmn, proj, conv_w8, xn, x2d, dx1, pk, winT, S, TM,
        late_stage, w_rb[1:5], m_rb[1:5], v_rb[1:5])
    g_wmkv, d_norm_mem, d_mkg = mem_kv_bwd(dkm, dvm, kv, memn, mem2d, pk, wmkv)

    tot, tail_stage = tail_reduce(d_norm_mix, d_norm_mem, d_norm_ffn, d_gains, d_cw8, d_cbias, d_qg, d_kg, d_mqg, d_mkg,
                                  d_sink8, loss8, [_halves_view(g) for g in (g_winT, g_wmkv)])
    loss = tot[5, 384]
    tail_res, _ = adamw_big("adamw_tail", tail_stage, [w_rb[0], w_rb[5]], [m_rb[0], m_rb[5]], [v_rb[0], v_rb[5]], 4)
    res = {"w_in": [a.T[None] for a in tail_res[0]], "w_gate": [a.T[None] for a in late_res[0]],
           "w_up": [a.T[None] for a in late_res[1]], "w_down": [a[None] for a in late_res[2]],
           "w_out": [a[None] for a in late_res[3]], "w_mem_kv": [a[None] for a in tail_res[1]]}
    res.update(adamw_small(tot, pk, _pack_small(m_small), _pack_small(v_small), {k: w_small[k].shape for k in SMALL}))

    order = ["norm_mix", "w_in", "q_norm", "k_norm", "attn_sinks", "conv_w", "conv_b", "norm_mem", "w_mem_kv",
             "mem_q_norm", "mem_k_norm", "out_norm_attn", "out_norm_conv", "out_norm_mem", "w_out", "norm_ffn",
             "w_gate", "w_up", "w_down"]
    return (loss, g_x.reshape(BL, S, D), *[res[n][0] for n in order], *[res[n][1] for n in order],
            *[res[n][2] for n in order], *[res[n][3] for n in order])
```

```python
import collections
import functools

import jax
import jax.numpy as jnp
import numpy as np
from jax import lax
from jax.experimental import pallas as pl
from jax.experimental.pallas import tpu as pltpu

f32 = jnp.float32
MXU = jnp.bfloat16
WIRE = jnp.bfloat16
EPS = 1e-6
NEG = -1e30
HD = 64
BLK = 128
N_Q, N_KV, N_MEMH = 8, 2, 4
GQA = N_Q // N_KV
ATT_W, KV_W, CONV_W, MEM_W = 512, 128, 256, 256
VMEM_MIB = 1024 * 1024
ADAM_LR, ADAM_B1, ADAM_B2, ADAM_EPS, ADAM_WD, ADAM_STEP = 0.001, 0.9, 0.999, 1e-08, 0.01, 10

MESH = pl.DeviceIdType.MESH
VM = pl.BlockSpec(memory_space=pltpu.VMEM)
ANY = pl.BlockSpec(memory_space=pl.ANY)
SDS = jax.ShapeDtypeStruct
DMA = pltpu.SemaphoreType.DMA


def _c(v):
    return v.astype(MXU)


def _nn(a, b):
    return lax.dot_general(a, b, (((1,), (0,)), ((), ())), preferred_element_type=f32)


def _nt(a, b):
    return lax.dot_general(a, b, (((1,), (1,)), ((), ())), preferred_element_type=f32)


def _tn(a, b):
    return lax.dot_general(a, b, (((0,), (0,)), ((), ())), preferred_element_type=f32)


def _rstd(v):
    return lax.rsqrt(jnp.mean(v * v, axis=-1, keepdims=True) + EPS)


def _norm_bwd(dy, v, r, g):
    dyg = dy * g
    dv = r * dyg - v * (r * r * r) * jnp.mean(dyg * v, axis=-1, keepdims=True)
    return dv, jnp.sum(dy * v * r, axis=0, keepdims=True)


def _split3(v):
    hi = _c(v)
    r1 = v - hi.astype(f32)
    mid = _c(r1)
    return hi, mid, _c(r1 - mid.astype(f32))


def _rowsum_mxu(v, width):
    ones = jnp.ones((v.shape[1], width), MXU)
    return sum(_nn(a, ones) for a in _split3(v))


def _seg_sums(v):
    r = lax.broadcasted_iota(jnp.int32, (2 * HD, 2 * HD), 0) // HD
    c = lax.broadcasted_iota(jnp.int32, (2 * HD, 2 * HD), 1) // HD
    bd = (r == c).astype(MXU)
    outs = []
    for b in range(v.shape[1] // (2 * HD)):
        outs.append(sum(_nn(a, bd) for a in _split3(v[:, b * 2 * HD:(b + 1) * 2 * HD])))
    return outs[0] if len(outs) == 1 else jnp.concatenate(outs, axis=1)


def _lanes(g, width):
    return jnp.concatenate([g] * (width // HD), axis=1)


def _heads_rstd(v):
    return lax.rsqrt(_seg_sums(v * v) * (1.0 / HD) + EPS)


def _heads_norm_bwd(dy, v, g):
    r = _heads_rstd(v)
    gl = _lanes(g, v.shape[1])
    dyg = dy * gl
    dv = r * dyg - v * (r * r * r) * (_seg_sums(dyg * v) * (1.0 / HD))
    dgl = jnp.sum(dy * v * r, axis=0, keepdims=True)
    return dv, sum(dgl[:, s * HD:(s + 1) * HD] for s in range(v.shape[1] // HD))


def _exp_scores(s, extra=None):
    m = jnp.max(s, axis=-1, keepdims=True)
    if extra is None:
        return jnp.exp(s - m), None
    m = jnp.maximum(m, extra)
    return jnp.exp(s - m), jnp.exp(extra - m)


def _place():
    return lax.axis_index("x"), lax.axis_index("y"), lax.axis_index("c")


SMALL_AT = {"norm_mix": (0, 0, 1024), "norm_mem": (1, 0, 1024), "norm_ffn": (2, 0, 1024),
            "out_norm_attn": (3, 0, ATT_W), "out_norm_conv": (3, ATT_W, CONV_W), "out_norm_mem": (3, ATT_W + CONV_W, MEM_W),
            "conv_b": (4, 3 * CONV_W, CONV_W), "q_norm": (5, 0, HD), "k_norm": (5, HD, HD), "mem_q_norm": (5, 2 * HD, HD),
            "mem_k_norm": (5, 3 * HD, HD), "attn_sinks": (5, 256, N_Q)}
SMALL = ("norm_mix", "norm_mem", "norm_ffn", "out_norm_attn", "out_norm_conv", "out_norm_mem", "conv_w", "conv_b",
         "q_norm", "k_norm", "mem_q_norm", "mem_k_norm", "attn_sinks")


def _small(pk_ref, name):
    r, c0, w = SMALL_AT[name]
    return pk_ref[r:r + 1, c0:c0 + w]


def _pack_small(d):
    z = lambda n: jnp.zeros((1, n), f32)
    row3 = jnp.concatenate([d["out_norm_attn"], d["out_norm_conv"], d["out_norm_mem"]], axis=1)
    row4 = jnp.concatenate([d["conv_w"].reshape(1, 3 * HD), z(3 * CONV_W - 3 * HD), d["conv_b"]], axis=1)
    row5 = jnp.concatenate([d["q_norm"], d["k_norm"], d["mem_q_norm"], d["mem_k_norm"], d["attn_sinks"],
                            z(1024 - 4 * HD - N_Q)], axis=1)
    return jnp.concatenate([d["norm_mix"], d["norm_mem"], d["norm_ffn"], row3, row4, row5, z(1024), z(1024)], axis=0)


def _other_chips(x, y):
    return [(1 - x, y), (x, 1 - y), (1 - x, 1 - y)]


Exchange = collections.namedtuple("Exchange", "ins outs sems start finish relays aliases", defaults=((), {}))


def _together(exchanges):
    def bounds(key):
        at, out = 0, []
        for ex in exchanges:
            out.append((at, at + len(getattr(ex, key))))
            at += len(getattr(ex, key))
        return out

    bi, bo, bs = bounds("ins"), bounds("outs"), bounds("sems")

    def of(i, fn):
        return lambda xa, xo, xs: fn(xa[bi[i][0]:bi[i][1]], xo[bo[i][0]:bo[i][1]], xs[bs[i][0]:bs[i][1]])

    def every(name):
        fns = [of(i, getattr(ex, name)) for i, ex in enumerate(exchanges)]

        def run(xa, xo, xs):
            for fn in fns:
                fn(xa, xo, xs)
        return run

    aliases = {}
    for i, ex in enumerate(exchanges):
        aliases.update({bi[i][0] + a: bo[i][0] + o for a, o in ex.aliases.items()})
    return Exchange([a for ex in exchanges for a in ex.ins], [o for ex in exchanges for o in ex.outs],
                    [s for ex in exchanges for s in ex.sems], every("start"), every("finish"),
                    [(sbe, of(i, fn)) for i, ex in enumerate(exchanges) for sbe, fn in ex.relays], aliases)


def _run(name, body, grid, ins, in_specs, out_shape, out_specs, scratch=(), vmem_mib=32, exchange=None):
    ins, in_specs, out_shape, out_specs, scratch = list(ins), list(in_specs), list(out_shape), list(out_specs), list(scratch)
    ni, no, ns = len(ins), len(out_shape), len(scratch)
    ex = exchange
    if ex is not None:
        nxi, nxo = len(ex.ins), len(ex.outs)

    def call_body(*refs):
        if ex is None:
            body(*refs)
            return
        a, xa = refs[:ni], refs[ni:ni + nxi]
        o, xo = refs[ni + nxi:ni + nxi + no], refs[ni + nxi + no:ni + nxi + no + nxo]
        s, xs = refs[ni + nxi + no + nxo:ni + nxi + no + nxo + ns], refs[ni + nxi + no + nxo + ns:]
        if grid:
            first = functools.reduce(jnp.logical_and, [pl.program_id(d) == 0 for d in range(len(grid))])
            last = functools.reduce(jnp.logical_and, [pl.program_id(d) == grid[d] - 1 for d in range(len(grid))])
            pl.when(first)(lambda: ex.start(xa, xo, xs))
            body(*a, *o, *s)
            nsteps = functools.reduce(lambda p, q: p * q, grid)
            for before_end, fn in ex.relays:
                at = np.unravel_index(max(nsteps - 1 - before_end, 0), grid)
                here = functools.reduce(jnp.logical_and, [pl.program_id(d) == int(at[d]) for d in range(len(grid))])
                pl.when(here)(functools.partial(fn, xa, xo, xs))
            pl.when(last)(lambda: ex.finish(xa, xo, xs))
        else:
            ex.start(xa, xo, xs)
            if body is not None:
                body(*a, *o, *s)
            for _, fn in ex.relays:
                fn(xa, xo, xs)
            ex.finish(xa, xo, xs)

    kw = dict(grid=grid) if grid else {}
    if ex is not None:
        if ex.aliases:
            kw["input_output_aliases"] = {ni + i: no + o for i, o in ex.aliases.items()}
        ins, in_specs = ins + list(ex.ins), in_specs + [ANY] * nxi
        out_shape, out_specs = out_shape + list(ex.outs), out_specs + [ANY] * nxo
        scratch = scratch + list(ex.sems)
    res = pl.pallas_call(
        call_body, name=name, out_shape=out_shape, in_specs=in_specs, out_specs=out_specs, scratch_shapes=scratch,
        compiler_params=pltpu.CompilerParams(dimension_semantics=("arbitrary",) * len(grid) if grid else None,
                                             vmem_limit_bytes=vmem_mib * VMEM_MIB), **kw)(*ins)
    res = list(res)
    return (res[:no], res[no:]) if ex is not None else res


def _remote(src, dst, ssem, rsem, dev):
    return pltpu.make_async_remote_copy(src_ref=src, dst_ref=dst, send_sem=ssem, recv_sem=rsem,
                                        device_id=dev, device_id_type=MESH)


def gather_exchange(shards, split, relay_early=0):
    n = len(shards)

    def rows(ref, e, kk, half=None):
        R = shards[e].shape[0]
        if half is None:
            return ref.at[pl.ds(pl.multiple_of(kk * R, 8), R)]
        return ref.at[pl.ds(pl.multiple_of(kk * R + half * (R // 2), 8), R // 2)]

    def ici(src, dst, sm, e, j, chip_j, x, y, c):
        k = 2 * x + y
        if split[e]:
            s = src[e].at[pl.ds(pl.multiple_of(c * (shards[e].shape[0] // 2), 8), shards[e].shape[0] // 2)]
            return _remote(s, rows(dst[e], e, k, c), sm[0].at[6 * e + j], sm[1].at[6 * e + j], (*chip_j, c))
        return _remote(src[e], rows(dst[e], e, k), sm[0].at[6 * e + j], sm[1].at[6 * e + j], (*chip_j, c))

    def landed(dst, e, chip_j, c):
        kj = 2 * chip_j[0] + chip_j[1]
        return rows(dst[e], e, kj, c) if split[e] else rows(dst[e], e, kj)

    def forward(dst, sm, e, j, chip_j, x, y, c, sender_c):
        kj = 2 * chip_j[0] + chip_j[1]
        r = rows(dst[e], e, kj, sender_c)
        return _remote(r, r, sm[0].at[6 * e + 3 + j], sm[1].at[6 * e + 3 + j], (x, y, 1 - c))

    def local(src, dst, sm, e, x, y):
        return pltpu.make_async_copy(src[e], rows(dst[e], e, 2 * x + y), sm[2].at[e])

    def start(src, dst, sm):
        x, y, c = _place()
        for e in range(n):
            local(src, dst, sm, e, x, y).start()
            for j, chip_j in enumerate(_other_chips(x, y)):
                ici(src, dst, sm, e, j, chip_j, x, y, c).start()

    def relay(src, dst, sm):
        x, y, c = _place()
        for e in range(n):
            for j, chip_j in enumerate(_other_chips(x, y)):
                r = landed(dst, e, chip_j, c)
                _remote(r, r, sm[0].at[6 * e + j], sm[1].at[6 * e + j], (*chip_j, c)).wait_recv()
                if split[e]:
                    forward(dst, sm, e, j, chip_j, x, y, c, c).start()

    def finish(src, dst, sm):
        x, y, c = _place()
        chips = _other_chips(x, y)
        for e in range(n):
            for j, chip_j in enumerate(chips):
                if split[e]:
                    forward(dst, sm, e, j, chip_j, x, y, c, 1 - c).wait_recv()
        for e in range(n):
            for j, chip_j in enumerate(chips):
                ici(src, dst, sm, e, j, chip_j, x, y, c).wait_send()
                if split[e]:
                    forward(dst, sm, e, j, chip_j, x, y, c, c).wait_send()
            local(src, dst, sm, e, x, y).wait()

    outs = [SDS((4 * s.shape[0], s.shape[1]), s.dtype) for s in shards]
    return Exchange(list(shards), outs, [DMA((6 * n,)), DMA((6 * n,)), DMA((n,))], start, finish, [(relay_early, relay)])


def _block_rows(ref, R, kk, half, quarter=None):
    hr = R // 2
    if quarter is None:
        return ref.at[pl.ds(pl.multiple_of(kk * R + half * hr, 8), hr)]
    return ref.at[pl.ds(pl.multiple_of(kk * R + half * hr + quarter * (hr // 2), 8), hr // 2)]


def gather_near_exchange(shards, relay_early=0):
    n = len(shards)
    R = [s.shape[0] for s in shards]

    def ici(src, dst, sm, e, j, chip_j, x, y, c):
        half = src[e].at[pl.ds(pl.multiple_of(c * (R[e] // 2), 8), R[e] // 2)]
        return _remote(half, _block_rows(dst[e], R[e], 2 * x + y, c), sm[0].at[4 * e + j], sm[1].at[4 * e + j], (*chip_j, c))

    def forward(dst, sm, e, j, chip_j, x, y, c, sender_c):
        r = _block_rows(dst[e], R[e], 2 * chip_j[0] + chip_j[1], sender_c)
        return _remote(r, r, sm[0].at[4 * e + 2 + j], sm[1].at[4 * e + 2 + j], (x, y, 1 - c))

    def local(src, dst, sm, e, x, y):
        return pltpu.make_async_copy(src[e], dst[e].at[pl.ds(pl.multiple_of((2 * x + y) * R[e], 8), R[e])], sm[2].at[e])

    def start(src, dst, sm):
        x, y, c = _place()
        for e in range(n):
            local(src, dst, sm, e, x, y).start()
            for j, chip_j in enumerate(_other_chips(x, y)[:2]):
                ici(src, dst, sm, e, j, chip_j, x, y, c).start()

    def relay(src, dst, sm):
        x, y, c = _place()
        for e in range(n):
            for j, chip_j in enumerate(_other_chips(x, y)[:2]):
                r = _block_rows(dst[e], R[e], 2 * chip_j[0] + chip_j[1], c)
                _remote(r, r, sm[0].at[4 * e + j], sm[1].at[4 * e + j], (*chip_j, c)).wait_recv()
                forward(dst, sm, e, j, chip_j, x, y, c, c).start()

    def finish(src, dst, sm):
        x, y, c = _place()
        near = _other_chips(x, y)[:2]
        for e in range(n):
            for j, chip_j in enumerate(near):
                forward(dst, sm, e, j, chip_j, x, y, c, 1 - c).wait_recv()
        for e in range(n):
            for j, chip_j in enumerate(near):
                ici(src, dst, sm, e, j, chip_j, x, y, c).wait_send()
                forward(dst, sm, e, j, chip_j, x, y, c, c).wait_send()
            local(src, dst, sm, e, x, y).wait()

    outs = [SDS((4 * s.shape[0], s.shape[1]), s.dtype) for s in shards]
    return Exchange(list(shards), outs, [DMA((4 * n,)), DMA((4 * n,)), DMA((n,))], start, finish, [(relay_early, relay)])


def gather_far_exchange(bufs, relay_early=0):
    n = len(bufs)
    R = [b.shape[0] // 4 for b in bufs]

    def send(src, dst, sm, e, j, x, y, c):
        to, of = _other_chips(x, y)[j], _other_chips(x, y)[1 - j]
        kk = 2 * of[0] + of[1]
        return _remote(_block_rows(src[e], R[e], kk, c, j), _block_rows(dst[e], R[e], kk, c, j),
                       sm[0].at[4 * e + j], sm[1].at[4 * e + j], (*to, c))

    def landed(dst, e, j, x, y, half):
        return _block_rows(dst[e], R[e], 2 * (1 - x) + (1 - y), half, j)

    def forward(dst, sm, e, j, x, y, c, sender_c):
        r = landed(dst, e, j, x, y, sender_c)
        return _remote(r, r, sm[0].at[4 * e + 2 + j], sm[1].at[4 * e + 2 + j], (x, y, 1 - c))

    def start(src, dst, sm):
        x, y, c = _place()
        for e in range(n):
            for j in range(2):
                send(src, dst, sm, e, j, x, y, c).start()

    def relay(src, dst, sm):
        x, y, c = _place()
        for e in range(n):
            for j in range(2):
                r = landed(dst, e, j, x, y, c)
                _remote(r, r, sm[0].at[4 * e + j], sm[1].at[4 * e + j], (*_other_chips(x, y)[j], c)).wait_recv()
                forward(dst, sm, e, j, x, y, c, c).start()

    def finish(src, dst, sm):
        x, y, c = _place()
        for e in range(n):
            for j in range(2):
                forward(dst, sm, e, j, x, y, c, 1 - c).wait_recv()
        for e in range(n):
            for j in range(2):
                send(src, dst, sm, e, j, x, y, c).wait_send()
                forward(dst, sm, e, j, x, y, c, c).wait_send()

    outs = [SDS(b.shape, b.dtype) for b in bufs]
    return Exchange(list(bufs), outs, [DMA((4 * n,)), DMA((4 * n,))], start, finish, [(relay_early, relay)],
                    {i: i for i in range(n)})


def gather_two_legs(shards):
    near = gather_near_exchange(shards)
    far = gather_far_exchange(near.outs)

    def finish(src, dst, sm):
        near.relays[0][1](src, dst, sm[:3])
        near.finish(src, dst, sm[:3])
        far.start(dst, dst, sm[3:])
        far.relays[0][1](dst, dst, sm[3:])
        far.finish(dst, dst, sm[3:])

    return Exchange(near.ins, near.outs, list(near.sems) + list(far.sems),
                    lambda src, dst, sm: near.start(src, dst, sm[:3]), finish)


def halves_exchange(grads):
    n = len(grads)

    def copy(g, st, sm, e, x, y, c):
        return _remote(g[e].at[:, 1 - c], st[e], sm[0].at[e], sm[1].at[e], (x, y, 1 - c))

    def start(g, st, sm):
        x, y, c = _place()
        for e in range(n):
            copy(g, st, sm, e, x, y, c).start()

    def finish(g, st, sm):
        x, y, c = _place()
        for e in range(n):
            copy(g, st, sm, e, x, y, c).wait()

    outs = [SDS((4,) + a.shape[2:], a.dtype) for a in grads]
    return Exchange(list(grads), outs, [DMA((n,)), DMA((n,))], start, finish)


def scatter_exchange(parts, relay_before_end=None, want_issue=False):
    n = len(parts)
    by_entry = relay_before_end is not None
    relay_before_end = relay_before_end or [0] * n

    def ici(p, st, sm, e, j, chip_j, x, y, c):
        k, kj = 2 * x + y, 2 * chip_j[0] + chip_j[1]
        return _remote(p[e].at[kj], st[e].at[c, k], sm[0].at[8 * e + j], sm[1].at[8 * e + j], (*chip_j, c))

    def own(p, st, sm, e, x, y, c):
        k = 2 * x + y
        return _remote(p[e].at[k], st[e].at[c, k], sm[0].at[8 * e + 3], sm[1].at[8 * e + 3], (x, y, 1 - c))

    def forward(st, sm, e, j, chip_j, x, y, c, sender_c):
        kj = 2 * chip_j[0] + chip_j[1]
        r = st[e].at[sender_c, kj]
        return _remote(r, r, sm[0].at[8 * e + 4 + j], sm[1].at[8 * e + 4 + j], (x, y, 1 - c))

    def local(p, st, sm, e, x, y, c):
        k = 2 * x + y
        return pltpu.make_async_copy(p[e].at[k], st[e].at[c, k], sm[2].at[e])

    def issue(e, p, st, sm):
        x, y, c = _place()
        for j, chip_j in enumerate(_other_chips(x, y)):
            ici(p, st, sm, e, j, chip_j, x, y, c).start()
        local(p, st, sm, e, x, y, c).start()
        own(p, st, sm, e, x, y, c).start()

    def start(p, st, sm, before_slot=None):
        x, y, c = _place()
        if by_entry:
            for e in range(n):
                issue(e, p, st, sm)
            return
        for j, chip_j in enumerate(_other_chips(x, y)):
            if before_slot is not None:
                before_slot(j, 2 * chip_j[0] + chip_j[1])
            for e in range(n):
                ici(p, st, sm, e, j, chip_j, x, y, c).start()
        if before_slot is not None:
            before_slot(3, 2 * x + y)
        for e in range(n):
            local(p, st, sm, e, x, y, c).start()
            own(p, st, sm, e, x, y, c).start()

    def relay(e, p, st, sm):
        x, y, c = _place()
        for j, chip_j in enumerate(_other_chips(x, y)):
            kj = 2 * chip_j[0] + chip_j[1]
            r = st[e].at[c, kj]
            _remote(r, r, sm[0].at[8 * e + j], sm[1].at[8 * e + j], (*chip_j, c)).wait_recv()
            forward(st, sm, e, j, chip_j, x, y, c, c).start()

    def finish(p, st, sm):
        x, y, c = _place()
        k = 2 * x + y
        chips = _other_chips(x, y)
        for e in range(n):
            r = st[e].at[1 - c, k]
            _remote(r, r, sm[0].at[8 * e + 3], sm[1].at[8 * e + 3], (x, y, 1 - c)).wait_recv()
            for j, chip_j in enumerate(chips):
                forward(st, sm, e, j, chip_j, x, y, c, 1 - c).wait_recv()
        for e in range(n):
            own(p, st, sm, e, x, y, c).wait_send()
            for j, chip_j in enumerate(chips):
                ici(p, st, sm, e, j, chip_j, x, y, c).wait_send()
                forward(st, sm, e, j, chip_j, x, y, c, c).wait_send()
            local(p, st, sm, e, x, y, c).wait()

    outs = [SDS((2,) + a.shape, a.dtype) for a in parts]
    ex = Exchange(list(parts), outs, [DMA((8 * n,)), DMA((8 * n,)), DMA((n,))], start, finish,
                  [(relay_before_end[e], functools.partial(relay, e)) for e in range(n)])
    return (ex, issue) if want_issue else ex


def reduce_scatter_exchange(grads, nsteps, load_step, send_step, relay_step):
    n = len(grads)
    hrs = [g.shape[2] for g in grads]
    C = grads[0].shape[3]
    scatter, issue = scatter_exchange([SDS((4,) + g.shape[2:], WIRE) for g in grads], want_issue=True)
    hand_on = [fn for _, fn in scatter.relays]

    def refs(xs):
        return xs[:3], xs[3], xs[4], xs[5], xs[6], xs[7:7 + n], xs[7 + n:]

    def push(e, g, psem, qsem, sib_st):
        x, y, c = _place()
        return _remote(g[e].at[:, 1 - c], sib_st[e], psem.at[e], qsem.at[e], (x, y, 1 - c))

    def fetch(e, g, lsem, own_st):
        _, _, c = _place()
        return pltpu.make_async_copy(g[e].at[:, c], own_st.at[e % 2, :, pl.ds(0, hrs[e])], lsem.at[e])

    def start(g, xo, xs):
        _, _, psem, qsem, _, sib_st, _ = refs(xs)
        for e in range(n):
            push(e, g, psem, qsem, sib_st).start()

    def load(e, g, xo, xs):
        _, lsem, _, _, own_st, _, _ = refs(xs)
        fetch(e, g, lsem, own_st).start()

    def send(e, g, xo, xs):
        sm, lsem, psem, qsem, own_st, sib_st, part = refs(xs)
        fetch(e, g, lsem, own_st).wait()
        push(e, g, psem, qsem, sib_st).wait_recv()
        part[e][...] = (own_st[e % 2, :, 0:hrs[e]] + sib_st[e][...]).astype(WIRE)
        issue(e, part, xo, sm)

    def relay(e, g, xo, xs):
        sm, _, _, _, _, _, part = refs(xs)
        hand_on[e](part, xo, sm)

    def finish(g, xo, xs):
        sm, _, psem, qsem, _, sib_st, part = refs(xs)
        scatter.finish(part, xo, sm)
        for e in range(n):
            push(e, g, psem, qsem, sib_st).wait_send()

    plan = sorted([(min(step[e], nsteps - 1), phase, e) for phase, step in enumerate((load_step, send_step, relay_step))
                   for e in range(n)])
    stage = (load, send, relay)
    relays = [(nsteps - 1 - at, functools.partial(stage[phase], e)) for at, phase, e in plan]
    scratch = (list(scatter.sems) + [DMA((n,)), DMA((n,)), DMA((n,))] + [pltpu.VMEM((2, 4, max(hrs), C), f32)]
               + [pltpu.VMEM((4, hr, C), f32) for hr in hrs] + [pltpu.VMEM((4, hr, C), WIRE) for hr in hrs])
    return Exchange(list(grads), scatter.outs, scratch, start, finish, relays)


def tail_reduce(d_norm_mix, d_norm_mem, d_norm_ffn, d_gains, d_cw8, d_cbias, d_qg, d_kg, d_mqg, d_mkg, d_sink8, loss8, tail):
    n = len(tail)
    scatter = scatter_exchange([SDS((4,) + a.shape[2:], WIRE) for a in tail])

    def half_copy(g, sib, hsem, e, j, slot, x, y, c):
        return _remote(g[e].at[slot, 1 - c], sib[e].at[slot], hsem[0].at[4 * e + j], hsem[1].at[4 * e + j], (x, y, 1 - c))

    def body(nm_ref, nmem_ref, nf_ref, gn_ref, cw_ref, cb_ref, qg_ref, kg_ref, mqg_ref, mkg_ref, sk_ref, ls_ref, *rest):
        g, o_ref, st = rest[:n], rest[n], rest[n + 1:2 * n + 1]
        buf, ssem, rsem = rest[2 * n + 1:2 * n + 4]
        own, sib, part = (rest[2 * n + 4 + i * n:2 * n + 4 + (i + 1) * n] for i in range(3))
        lsem = rest[5 * n + 4]
        hsem, xsem = rest[5 * n + 5:5 * n + 7], rest[5 * n + 7:]
        x, y, c = _place()
        loads = [pltpu.make_async_copy(g[e].at[:, c], own[e], lsem.at[e]) for e in range(n)]
        for ld in loads:
            ld.start()
        for j, slot in enumerate([2 * cx + cy for cx, cy in _other_chips(x, y)] + [2 * x + y]):
            for e in range(n):
                half_copy(g, sib, hsem, e, j, slot, x, y, c).start()
        me = 4 * x + 2 * y + c
        mine = buf.at[me]
        mine[...] = jnp.zeros((8, 1024), f32)
        mine[0:1, :] = nm_ref[...]
        mine[1:2, :] = nmem_ref[...]
        mine[2:3, :] = nf_ref[...]
        mine[3:4, :] = gn_ref[...]
        for j in range(3):
            mine[4:5, pl.ds(j * CONV_W, CONV_W)] = cw_ref[j:j + 1, :]
        mine[4:5, pl.ds(3 * CONV_W, CONV_W)] = cb_ref[...]
        for j, r in enumerate((qg_ref, kg_ref, mqg_ref, mkg_ref)):
            mine[5:6, pl.ds(j * HD, HD)] = r[...]
        mine[5:6, pl.ds(256, 128)] = sk_ref[0:1, :]
        mine[5:6, pl.ds(384, 128)] = ls_ref[0:1, :]

        def peer_of(m):
            return (1 - x if m & 4 else x, 1 - y if m & 2 else y, 1 - c if m & 1 else c)

        for m in range(1, 8):
            _remote(mine, mine, ssem.at[m - 1], rsem.at[m - 1], peer_of(m)).start()
        for ld in loads:
            ld.wait()

        def chip_partial(j, slot):
            for e in range(n):
                half_copy(g, sib, hsem, e, j, slot, x, y, c).wait()
                part[e][slot] = (own[e][slot] + sib[e][slot]).astype(WIRE)

        scatter.start(part, st, xsem, chip_partial)
        for _, hand_on in scatter.relays:
            hand_on(part, st, xsem)
        scatter.finish(part, st, xsem)
        for m in range(1, 8):
            p = peer_of(m)
            got = buf.at[4 * p[0] + 2 * p[1] + p[2]]
            _remote(got, got, ssem.at[m - 1], rsem.at[m - 1], p).wait_recv()
        for m in range(1, 8):
            _remote(mine, mine, ssem.at[m - 1], rsem.at[m - 1], peer_of(m)).wait_send()
        acc = buf[0]
        for d in range(1, 8):
            acc = acc + buf[d]
        o_ref[...] = acc

    ins = [d_norm_mix, d_norm_mem, d_norm_ffn, d_gains, d_cw8, d_cbias, d_qg, d_kg, d_mqg, d_mkg, d_sink8, loss8]
    half_shape = [(4,) + a.shape[2:] for a in tail]
    scratch = ([pltpu.VMEM((8, 8, 1024), f32), DMA((7,)), DMA((7,))]
               + [pltpu.VMEM(s, f32) for s in half_shape] * 2 + [pltpu.VMEM(s, WIRE) for s in half_shape]
               + [DMA((n,)), DMA((4 * n,)), DMA((4 * n,))] + list(scatter.sems))
    res = _run("tail_reduce", body, (), ins + list(tail), [VM] * len(ins) + [ANY] * n,
               [SDS((8, 1024), f32)] + list(scatter.outs), [VM] + [ANY] * n, scratch=scratch, vmem_mib=40)
    return res[0], res[1:]


def _adamw_math(w, g, m, v):
    m = ADAM_B1 * m + (1.0 - ADAM_B1) * g
    v = ADAM_B2 * v + (1.0 - ADAM_B2) * (g * g)
    m_hat = m / (1.0 - ADAM_B1 ** ADAM_STEP)
    v_hat = v / (1.0 - ADAM_B2 ** ADAM_STEP)
    delta = -ADAM_LR * (m_hat / (jnp.sqrt(v_hat) + ADAM_EPS) + ADAM_WD * w)
    return delta, m, v


def _sum_chips(st):
    return ((st[0].astype(f32) + st[1].astype(f32)) + st[2].astype(f32)) + st[3].astype(f32)


def adamw_big(name, stages, ws, ms, vs, nstep, exchange=None):
    n = len(stages)

    def body(*refs):
        st, w, m, v = refs[:n], refs[n:2 * n], refs[2 * n:3 * n], refs[3 * n:4 * n]
        outs = refs[4 * n:]
        for e in range(n):
            g = jnp.concatenate([_sum_chips(st[e].at[0]), _sum_chips(st[e].at[1])], axis=0)
            d, mm, vv = _adamw_math(w[e][...], g, m[e][...], v[e][...])
            outs[4 * e][...] = g
            outs[4 * e + 1][...] = d
            outs[4 * e + 2][...] = mm
            outs[4 * e + 3][...] = vv

    st_specs, w_specs = [], []
    for e in range(n):
        _, _, hr, C = stages[e].shape
        st_specs.append(pl.BlockSpec((2, 4, hr, C // nstep), lambda i: (0, 0, 0, i)))
        w_specs.append(pl.BlockSpec((2 * hr, C // nstep), lambda i: (0, i)))
    out_specs = [s for s in w_specs for _ in range(4)]
    out_shape = [SDS(w.shape, f32) for w in ws for _ in range(4)]
    res = _run(name, body, (nstep,), list(stages) + list(ws) + list(ms) + list(vs), st_specs + w_specs * 3,
               out_shape, out_specs, vmem_mib=48, exchange=exchange)
    res, sent = res if exchange is not None else (res, None)
    return [res[4 * e:4 * e + 4] for e in range(n)], sent


def adamw_small(tot, pk_w, pk_m, pk_v, shapes):
    def body(tot_ref, w_ref, m_ref, v_ref, *outs):
        x, y, _ = _place()
        chip = 2 * x + y
        taps = []
        for j in range(3):
            mine = tot_ref[4:5, j * CONV_W:j * CONV_W + HD]
            for s in range(1, 4):
                mine = jnp.where(chip == s, tot_ref[4:5, j * CONV_W + s * HD:j * CONV_W + (s + 1) * HD], mine)
            taps.append(mine)
        row4 = jnp.concatenate(taps + [jnp.zeros((1, 3 * CONV_W - 3 * HD), f32), tot_ref[4:5, 3 * CONV_W:]], axis=1)
        tot_v = tot_ref[...]
        row = lax.broadcasted_iota(jnp.int32, tot_v.shape, 0)
        g = jnp.where(row == 4, jnp.broadcast_to(row4, tot_v.shape), tot_v)
        d, mm, vv = _adamw_math(w_ref[...], g, m_ref[...], v_ref[...])
        for i, name in enumerate(SMALL):
            for k, val in enumerate((g, d, mm, vv)):
                if name == "conv_w":
                    outs[4 * i + k][...] = jnp.concatenate([val[4:5, j * HD:(j + 1) * HD] for j in range(3)], axis=0)[None]
                else:
                    r, c0, w = SMALL_AT[name]
                    outs[4 * i + k][...] = val[r:r + 1, c0:c0 + w]

    out_shape = [SDS(shapes[k], f32) for k in SMALL for _ in range(4)]
    res = _run("adamw_small", body, (), [tot, pk_w, pk_m, pk_v], [VM] * 4, out_shape, [VM] * len(out_shape))
    return {k: res[4 * i:4 * i + 4] for i, k in enumerate(SMALL)}


def prep_weights(name, shards, exchange=None):
    n = len(shards)

    def body(*refs):
        for e in range(n):
            refs[n + e][...] = _c(refs[e][...])

    return _run(name, body, (), shards, [VM] * n, [SDS(a.shape, MXU) for a in shards], [VM] * n, vmem_mib=48, exchange=exchange)


def mem_kv_fwd(mem2d, pk, wmkv):
    M, D = mem2d.shape

    def body(m_ref, pk_ref, w_ref, mn_ref, kv_ref, km_ref, vm_ref):
        m = m_ref[...]
        mn = _c(m * _rstd(m) * _small(pk_ref, "norm_mem"))
        mn_ref[...] = mn
        kv = _nn(mn, w_ref[...])
        kv_ref[...] = kv
        kk = kv[:, :MEM_W]
        km_ref[...] = _c(kk * _heads_rstd(kk) * _lanes(_small(pk_ref, "mem_k_norm"), MEM_W))
        vm_ref[...] = _c(kv[:, MEM_W:])

    return _run("mem_kv_fwd", body, (), [mem2d, pk, wmkv], [VM] * 3,
                [SDS((M, D), MXU), SDS((M, 2 * MEM_W), f32), SDS((M, MEM_W), MXU), SDS((M, MEM_W), MXU)], [VM] * 4)


QKV_W = ATT_W + 2 * KV_W + MEM_W


def in_proj_fwd(x2d, pk, winT, tm, exchange):
    T, D = x2d.shape
    P = winT.shape[0]

    def body(x_ref, pk_ref, w_ref, xn_ref, proj_ref, qkv_ref):
        xv = x_ref[...]
        xn = _c(xv * _rstd(xv) * _small(pk_ref, "norm_mix"))
        xn_ref[...] = xn
        proj = _nt(xn, w_ref[...])
        proj_ref[...] = proj
        q, k = proj[:, :ATT_W], proj[:, ATT_W:ATT_W + KV_W]
        qm = proj[:, P - MEM_W:]
        qkv_ref[...] = jnp.concatenate(
            [_c(q * _heads_rstd(q) * _lanes(_small(pk_ref, "q_norm"), ATT_W)),
             _c(k * _heads_rstd(k) * _lanes(_small(pk_ref, "k_norm"), KV_W)),
             _c(proj[:, ATT_W + KV_W:ATT_W + 2 * KV_W]),
             _c(qm * _heads_rstd(qm) * _lanes(_small(pk_ref, "mem_q_norm"), MEM_W))], axis=1)

    return _run("in_proj_fwd", body, (T // tm,), [x2d, pk, winT],
                [pl.BlockSpec((tm, D), lambda i: (i, 0)), VM, VM],
                [SDS((T, D), MXU), SDS((T, P), f32), SDS((T, QKV_W), MXU)],
                [pl.BlockSpec((tm, D), lambda i: (i, 0)), pl.BlockSpec((tm, P), lambda i: (i, 0)),
                 pl.BlockSpec((tm, QKV_W), lambda i: (i, 0))],
                vmem_mib=40, exchange=exchange)


def _swa_bias_table():
    r = np.arange(GQA * BLK)[:, None]
    k = np.arange(2 * BLK)[None, :]
    dist = (r % BLK) + BLK - k
    band = (dist >= 0) & (dist < BLK)
    tab = np.empty((2, N_KV, GQA * BLK, 2 * BLK), np.float32)
    for later in range(2):
        valid = band & ((k >= BLK) | (later == 1))
        for g in range(N_KV):
            slope = 2.0 ** -(g * GQA + r // BLK + 1.0)
            tab[later, g] = np.where(valid, -slope * dist, NEG)
    return jnp.asarray(tab)


def _sink_column(g, sk_ref):
    hrow = lax.broadcasted_iota(jnp.int32, (GQA * BLK, 1), 0) // BLK
    sink = jnp.zeros((GQA * BLK, 1), f32)
    for hh in range(GQA):
        sink = jnp.where(hrow == hh, sk_ref[g * GQA + hh:g * GQA + hh + 1, 0:1], sink)
    return sink


def _stack_heads(v, g):
    return jnp.concatenate([v[:, (g * GQA + hh) * HD:(g * GQA + hh + 1) * HD] for hh in range(GQA)], axis=0)


def attn_fwd(qkv, sink_rows, BL, S, exchange, qb=2):
    NS = S // (qb * BLK)
    T = BL * S

    def body(q_ref, kc_ref, kp_ref, vc_ref, vp_ref, sk_ref, tab_ref, o_ref):
        j = pl.program_id(1)
        kall = jnp.concatenate([kp_ref[...], kc_ref[...]], axis=0)
        vall = jnp.concatenate([vp_ref[...], vc_ref[...]], axis=0)
        ones = jnp.ones((2 * BLK, HD), MXU)
        for b in range(qb):
            q = q_ref[pl.ds(b * BLK, BLK), :]
            k2, v2 = kall[b * BLK:(b + 2) * BLK], vall[b * BLK:(b + 2) * BLK]
            later = jnp.minimum(j, 1) if b == 0 else 1
            for g in range(N_KV):
                kn, vh = k2[:, g * HD:(g + 1) * HD], v2[:, g * HD:(g + 1) * HD]
                s = _nt(_stack_heads(q, g), kn) * (HD ** -0.5) + tab_ref[later, g]
                e, es = _exp_scores(s, _sink_column(g, sk_ref))
                eb = _c(e)
                o = _nn(eb, vh) * (1.0 / (_nn(eb, ones) + es))
                for hh in range(GQA):
                    o_ref[pl.ds(b * BLK, BLK), pl.ds((g * GQA + hh) * HD, HD)] = o[hh * BLK:(hh + 1) * BLK]

    cur = lambda col: (lambda b, j: (b * NS + j, col))
    prev = lambda col: (lambda b, j: (qb * (b * NS + j) - jnp.minimum(j, 1), col))
    return _run("attn_fwd", body, (BL, NS), [qkv, qkv, qkv, qkv, qkv, sink_rows, _swa_bias_table()],
                [pl.BlockSpec((qb * BLK, ATT_W), cur(0)),
                 pl.BlockSpec((qb * BLK, KV_W), cur(4)), pl.BlockSpec((BLK, KV_W), prev(4)),
                 pl.BlockSpec((qb * BLK, KV_W), cur(5)), pl.BlockSpec((BLK, KV_W), prev(5)),
                 pl.BlockSpec((8, 128), lambda b, j: (0, 0)), VM],
                [SDS((T, ATT_W), f32)], [pl.BlockSpec((qb * BLK, ATT_W), cur(0))], exchange=exchange)


def _conv_taps(u, uh):
    row = lax.broadcasted_iota(jnp.int32, u.shape, 0)
    u1 = jnp.where(row == 0, uh[7:8, :], pltpu.roll(u, 1, 0))
    u2 = jnp.where(row == 0, uh[6:7, :], jnp.where(row == 1, uh[7:8, :], pltpu.roll(u, 2, 0)))
    return u1, u2


def _mem_head(qm, km, vm, h):
    qh, kh, vh = (a[:, h * HD:(h + 1) * HD] for a in (qm, km, vm))
    e, _ = _exp_scores(_nt(qh, kh) * (HD ** -0.5))
    return qh, kh, vh, e


def mixer_tail_fwd(x2d, attn_out, proj, qkv, km, vm, conv_w8, pk, wout, S, tm, exchange):
    T, D = x2d.shape
    NM = km.shape[0] // (T // S)

    def body(x_ref, ao_ref, ch_ref, cb_ref, cc_ref, chh_ref, cch_ref, qm_ref, km_ref, vm_ref, cw_ref, pk_ref,
             wout_ref, co_ref, mo_ref, mg_ref, x1_ref, h_ref):
        first = (pl.program_id(0) * tm) % S == 0
        u = cc_ref[...] * ch_ref[...]
        uh = jnp.where(first, 0.0, cch_ref[...] * chh_ref[...])
        u1, u2 = _conv_taps(u, uh)
        conv = cw_ref[0:1, :] * u2 + cw_ref[1:2, :] * u1 + cw_ref[2:3, :] * u + _small(pk_ref, "conv_b")
        conv_out = cb_ref[...] * conv
        co_ref[...] = conv_out
        qm, kmv, vmv = qm_ref[...], km_ref[...], vm_ref[...]
        ones = jnp.ones((NM, HD), MXU)
        for h in range(N_MEMH):
            _, _, vh, e = _mem_head(qm, kmv, vmv, h)
            eb = _c(e)
            mo_ref[:, pl.ds(h * HD, HD)] = _nn(eb, vh) * (1.0 / _nn(eb, ones))
        mem_out = mo_ref[...]
        ao = ao_ref[...]
        merged = _c(jnp.concatenate([ao * _rstd(ao) * _small(pk_ref, "out_norm_attn"),
                                     conv_out * _rstd(conv_out) * _small(pk_ref, "out_norm_conv"),
                                     mem_out * _rstd(mem_out) * _small(pk_ref, "out_norm_mem")], axis=1))
        mg_ref[...] = merged
        x1 = x_ref[...] + _nn(merged, wout_ref[...])
        x1_ref[...] = x1
        h_ref[...] = _c(x1 * _rstd(x1) * _small(pk_ref, "norm_ffn"))

    tile = lambda w, col: pl.BlockSpec((tm, w), lambda i: (i, col))
    halo = lambda col: pl.BlockSpec((8, CONV_W), lambda i: (jnp.maximum(i * (tm // 8) - 1, 0), col))
    seq = pl.BlockSpec((NM, MEM_W), lambda i: ((i * tm) // S, 0))
    small = lambda a: pl.BlockSpec(a.shape, lambda i: (0, 0))
    return _run("mixer_tail_fwd", body, (T // tm,),
                [x2d, attn_out, proj, proj, proj, proj, proj, qkv, km, vm, conv_w8, pk, wout],
                [tile(D, 0), tile(ATT_W, 0), tile(CONV_W, 3), tile(CONV_W, 4), tile(CONV_W, 5), halo(3), halo(5),
                 tile(MEM_W, 3), seq, seq, VM, VM, VM],
                [SDS((T, CONV_W), f32), SDS((T, MEM_W), f32), SDS((T, D), MXU), SDS((T, D), f32), SDS((T, D), MXU)],
                [tile(CONV_W, 0), tile(MEM_W, 0), tile(D, 0), tile(D, 0), tile(D, 0)], vmem_mib=40, exchange=exchange)


def ffn_fwd_bwd(h, x1, tgt, wgT, wuT, wd, pk, tm):
    T, D = x1.shape
    F = wd.shape[0]

    def body(h_ref, x1_ref, t_ref, wg_ref, wu_ref, wd_ref, pk_ref,
             dx1_ref, dx2_ref, act_ref, dg_ref, du_ref, loss_ref, dgf_ref):
        @pl.when(pl.program_id(0) == 0)
        def _():
            loss_ref[...] = jnp.zeros_like(loss_ref)
            dgf_ref[...] = jnp.zeros_like(dgf_ref)

        hv = h_ref[...]
        gate = _nt(hv, wg_ref[...])
        up = _nt(hv, wu_ref[...])
        sg = jax.nn.sigmoid(gate)
        sl = gate * sg
        act = _c(sl * up)
        act_ref[...] = act
        x1v = x1_ref[...]
        diff = (x1v + _nn(act, wd_ref[...])) - t_ref[...]
        loss_ref[...] += 0.5 * jnp.sum(jnp.sum(diff * diff, axis=-1, keepdims=True) / D, axis=0, keepdims=True)
        dx2 = diff / D
        dx2b = _c(dx2)
        dx2_ref[...] = dx2b
        d_act = _nt(dx2b, wd_ref[...])
        d_up = _c(d_act * sl)
        d_gate = _c(d_act * up * (sg * (1.0 + gate * (1.0 - sg))))
        du_ref[...] = d_up
        dg_ref[...] = d_gate
        dh = _nn(d_gate, wg_ref[...]) + _nn(d_up, wu_ref[...])
        dv, dgf = _norm_bwd(dh, x1v, _rstd(x1v), _small(pk_ref, "norm_ffn"))
        dx1_ref[...] = dx2 + dv
        dgf_ref[...] += dgf

    tile = lambda w: pl.BlockSpec((tm, w), lambda i: (i, 0))
    return _run("ffn_fwd_bwd", body, (T // tm,), [h, x1, tgt, wgT, wuT, wd, pk],
                [tile(D), tile(D), tile(D), VM, VM, VM, VM],
                [SDS((T, D), f32), SDS((T, D), MXU), SDS((T, F), MXU), SDS((T, F), MXU), SDS((T, F), MXU),
                 SDS((8, 128), f32), SDS((1, D), f32)],
                [tile(D), tile(D), tile(F), tile(F), tile(F), pl.BlockSpec((8, 128), lambda i: (0, 0)),
                 pl.BlockSpec((1, D), lambda i: (0, 0))], vmem_mib=56)


def matmul_tn(a, b, name, tmo, tk):
    T, M = a.shape
    N = b.shape[1]

    def body(a_ref, b_ref, o_ref):
        @pl.when(pl.program_id(1) == 0)
        def _():
            o_ref[...] = jnp.zeros_like(o_ref)

        o_ref[...] += _tn(a_ref[...], b_ref[...])

    return _run(name, body, (M // tmo, T // tk), [a, b],
                [pl.BlockSpec((tk, tmo), lambda m, k: (k, m)), pl.BlockSpec((tk, N), lambda m, k: (k, 0))],
                [SDS((M, N), f32)], [pl.BlockSpec((tmo, N), lambda m, k: (m, 0))], vmem_mib=48)[0]


def out_proj_bwd(dx1, merged, attn_out, conv_out, mem_out, pk, wout, tm):
    T, D = dx1.shape

    def body(dx1_ref, mg_ref, ao_ref, co_ref, mo_ref, pk_ref, w_ref,
             dao_ref, dco_ref, dmo_ref, dw_ref, dgain_ref):
        @pl.when(pl.program_id(0) == 0)
        def _():
            dw_ref[...] = jnp.zeros_like(dw_ref)
            dgain_ref[...] = jnp.zeros_like(dgain_ref)

        dxb = _c(dx1_ref[...])
        dw_ref[...] += _tn(mg_ref[...], dxb)
        dmg = _nt(dxb, w_ref[...])
        ao, co, mo = ao_ref[...], co_ref[...], mo_ref[...]
        da, ga = _norm_bwd(dmg[:, :ATT_W], ao, _rstd(ao), _small(pk_ref, "out_norm_attn"))
        dc, gc = _norm_bwd(dmg[:, ATT_W:ATT_W + CONV_W], co, _rstd(co), _small(pk_ref, "out_norm_conv"))
        dm, gm = _norm_bwd(dmg[:, ATT_W + CONV_W:], mo, _rstd(mo), _small(pk_ref, "out_norm_mem"))
        dao_ref[...] = da
        dco_ref[...] = dc
        dmo_ref[...] = dm
        dgain_ref[...] += jnp.concatenate([ga, gc, gm], axis=1)

    tile = lambda w: pl.BlockSpec((tm, w), lambda i: (i, 0))
    return _run("out_proj_bwd", body, (T // tm,), [dx1, merged, attn_out, conv_out, mem_out, pk, wout],
                [tile(D), tile(D), tile(ATT_W), tile(CONV_W), tile(MEM_W), VM, VM],
                [SDS((T, ATT_W), f32), SDS((T, CONV_W), f32), SDS((T, MEM_W), f32), SDS((D, D), f32), SDS((1, D), f32)],
                [tile(ATT_W), tile(CONV_W), tile(MEM_W), pl.BlockSpec((D, D), lambda i: (0, 0)),
                 pl.BlockSpec((1, D), lambda i: (0, 0))], vmem_mib=40)


def attn_bwd(qkv, d_attn, attn_out, sink_rows, BL, S, exchange):
    NB = S // BLK
    T = BL * S

    def body(q_ref, kc_ref, kp_ref, vc_ref, vp_ref, do_ref, ao_ref, sk_ref, tab_ref,
             dq_ref, dk_ref, dv_ref, dsk_ref, pend_k, pend_v):
        b, j = pl.program_id(0), pl.program_id(1)

        @pl.when((b == 0) & (j == 0))
        def _():
            dsk_ref[...] = jnp.zeros_like(dsk_ref)

        @pl.when(j == 0)
        def _():
            pend_k[...] = jnp.zeros_like(pend_k)
            pend_v[...] = jnp.zeros_like(pend_v)

        @pl.when(j < NB)
        def _():
            q, do, ao = q_ref[...], do_ref[...], ao_ref[...]
            k2 = jnp.concatenate([kp_ref[...], kc_ref[...]], axis=0)
            v2 = jnp.concatenate([vp_ref[...], vc_ref[...]], axis=0)
            lane = lax.broadcasted_iota(jnp.int32, (8, 128), 1)
            ones_w = jnp.ones((2 * BLK, 2 * BLK), MXU)
            dsk = jnp.zeros((8, 128), f32)
            dks, dvs = [], []
            for g in range(N_KV):
                kn, vh = k2[:, g * HD:(g + 1) * HD], v2[:, g * HD:(g + 1) * HD]
                qs = _stack_heads(q, g)
                s = _nt(qs, kn) * (HD ** -0.5) + tab_ref[g]
                e, es = _exp_scores(s, _sink_column(g, sk_ref))
                eb = _c(e)
                inv_w = 1.0 / (_nn(eb, ones_w) + es)
                inv_n = inv_w[:, :HD]
                dos = _stack_heads(do, g)
                delta = _rowsum_mxu(dos * _stack_heads(ao, g), 2 * BLK)
                dp = _nt(_c(dos), vh)
                ds = _c(e * inv_w * (dp - delta) * (HD ** -0.5))
                t = es * inv_n[:, 0:1] * delta[:, 0:1]
                for hh in range(GQA):
                    dsk = dsk + jnp.where(lane == g * GQA + hh, -jnp.sum(t[hh * BLK:(hh + 1) * BLK]), 0.0)
                dvs.append(_tn(eb, _c(dos * inv_n)))
                dks.append(_tn(ds, qs))
                dqs = _nn(ds, kn)
                for hh in range(GQA):
                    dq_ref[:, pl.ds((g * GQA + hh) * HD, HD)] = dqs[hh * BLK:(hh + 1) * BLK]
            dk2 = jnp.concatenate(dks, axis=1)
            dv2 = jnp.concatenate(dvs, axis=1)
            dk_ref[...] = pend_k[...] + dk2[:BLK]
            dv_ref[...] = pend_v[...] + dv2[:BLK]
            pend_k[...] = dk2[BLK:]
            pend_v[...] = dv2[BLK:]
            dsk_ref[...] += dsk

        @pl.when(j == NB)
        def _():
            dk_ref[...] = pend_k[...]
            dv_ref[...] = pend_v[...]

    cur = lambda col: (lambda b, j: (b * NB + jnp.minimum(j, NB - 1), col))
    prev = lambda col: (lambda b, j: (b * NB + jnp.maximum(j - 1, 0), col))
    small = lambda shape: pl.BlockSpec(shape, lambda b, j: (0, 0))
    return _run("attn_bwd", body, (BL, NB + 1), [qkv, qkv, qkv, qkv, qkv, d_attn, attn_out, sink_rows, _swa_bias_table()],
                [pl.BlockSpec((BLK, ATT_W), cur(0)),
                 pl.BlockSpec((BLK, KV_W), cur(4)), pl.BlockSpec((BLK, KV_W), prev(4)),
                 pl.BlockSpec((BLK, KV_W), cur(5)), pl.BlockSpec((BLK, KV_W), prev(5)),
                 pl.BlockSpec((BLK, ATT_W), cur(0)), pl.BlockSpec((BLK, ATT_W), cur(0)), small((8, 128)),
                 pl.BlockSpec((None, N_KV, GQA * BLK, 2 * BLK), lambda b, j: (jnp.minimum(j, 1), 0, 0, 0))],
                [SDS((T, ATT_W), f32), SDS((T, KV_W), f32), SDS((T, KV_W), f32), SDS((8, 128), f32)],
                [pl.BlockSpec((BLK, ATT_W), cur(0)), pl.BlockSpec((BLK, KV_W), prev(0)),
                 pl.BlockSpec((BLK, KV_W), prev(0)), small((8, 128))],
                scratch=[pltpu.VMEM((BLK, KV_W), f32)] * 2, vmem_mib=56, exchange=exchange)


def mem_conv_bwd(d_mem_out, mem_out, d_conv_out, proj, qkv, km, vm, conv_w8, pk, S, tm, exchange):
    T = d_mem_out.shape[0]
    NM = km.shape[0] // (T // S)

    def body(dmo_ref, mo_ref, dco_ref, ch_ref, cb_ref, cc_ref, chh_ref, cch_ref, qm_ref, km_ref, vm_ref, cw_ref,
             pk_ref, dqm_ref, dkm_ref, dvm_ref, dcb_ref, dcv_ref, dcw_ref, dcbias_ref):
        i = pl.program_id(0)
        first = (i * tm) % S == 0

        @pl.when(i == 0)
        def _():
            dcw_ref[...] = jnp.zeros_like(dcw_ref)
            dcbias_ref[...] = jnp.zeros_like(dcbias_ref)

        @pl.when(first)
        def _():
            dkm_ref[...] = jnp.zeros_like(dkm_ref)
            dvm_ref[...] = jnp.zeros_like(dvm_ref)

        qm, kmv, vmv, dmo, mo = qm_ref[...], km_ref[...], vm_ref[...], dmo_ref[...], mo_ref[...]
        ones_w = jnp.ones((NM, NM), MXU)
        for h in range(N_MEMH):
            qh, kh, vh, e = _mem_head(qm, kmv, vmv, h)
            eb = _c(e)
            doh = dmo[:, h * HD:(h + 1) * HD]
            delta = _rowsum_mxu(doh * mo[:, h * HD:(h + 1) * HD], NM)
            dp = _nt(_c(doh), vh)
            inv_w = 1.0 / _nn(eb, ones_w)
            ds = _c(e * inv_w * (dp - delta) * (HD ** -0.5))
            dvm_ref[:, pl.ds(h * HD, HD)] += _tn(eb, _c(doh * inv_w[:, :HD]))
            dkm_ref[:, pl.ds(h * HD, HD)] += _tn(ds, qh)
            dqm_ref[:, pl.ds(h * HD, HD)] = _nn(ds, kh)

        u = cc_ref[...] * ch_ref[...]
        uh = jnp.where(first, 0.0, cch_ref[...] * chh_ref[...])
        u1, u2 = _conv_taps(u, uh)
        conv = cw_ref[0:1, :] * u2 + cw_ref[1:2, :] * u1 + cw_ref[2:3, :] * u + _small(pk_ref, "conv_b")
        dy = dco_ref[...]
        dcb_ref[...] = dy * conv
        dcv = dy * cb_ref[...]
        dcv_ref[...] = dcv
        dcbias_ref[...] += jnp.sum(dcv, axis=0, keepdims=True)
        dcw_ref[0:1, :] += jnp.sum(dcv * u2, axis=0, keepdims=True)
        dcw_ref[1:2, :] += jnp.sum(dcv * u1, axis=0, keepdims=True)
        dcw_ref[2:3, :] += jnp.sum(dcv * u, axis=0, keepdims=True)

    tile = lambda w, col: pl.BlockSpec((tm, w), lambda i: (i, col))
    halo = lambda col: pl.BlockSpec((8, CONV_W), lambda i: (jnp.maximum(i * (tm // 8) - 1, 0), col))
    seq = pl.BlockSpec((NM, MEM_W), lambda i: ((i * tm) // S, 0))
    const = lambda shape: pl.BlockSpec(shape, lambda i: (0, 0))
    return _run("mem_conv_bwd", body, (T // tm,),
                [d_mem_out, mem_out, d_conv_out, proj, proj, proj, proj, proj, qkv, km, vm, conv_w8, pk],
                [tile(MEM_W, 0), tile(MEM_W, 0), tile(CONV_W, 0), tile(CONV_W, 3), tile(CONV_W, 4), tile(CONV_W, 5),
                 halo(3), halo(5), tile(MEM_W, 3), seq, seq, VM, VM],
                [SDS((T, MEM_W), f32), SDS(km.shape, f32), SDS(km.shape, f32),
                 SDS((T, CONV_W), f32), SDS((T, CONV_W), f32), SDS((8, CONV_W), f32), SDS((1, CONV_W), f32)],
                [tile(MEM_W, 0), seq, seq, tile(CONV_W, 0), tile(CONV_W, 0), const((8, CONV_W)), const((1, CONV_W))],
                vmem_mib=48, exchange=exchange)


def in_proj_bwd(dqn, dkn, dv, dcb, dcv, dqmn, proj, conv_w8, xn, x2d, dx1, pk, winT, S, tm, stages, ws, ms, vs):
    T, D = x2d.shape
    P = winT.shape[0]
    last_blk = T // 8 - 1
    n = len(stages)
    nsteps = T // tm
    tile_w = ws[0].shape[1] // (nsteps // 2)
    turn = [e * 2 // n for e in range(n)]

    def body(dq_ref, dk_ref, dv_ref, dcb_ref, dcv_ref, dcvn_ref, dqm_ref, qa_ref, ka_ref, ch_ref, cc_ref, qma_ref,
             cw_ref, xn_ref, x_ref, dx1_ref, pk_ref, w_ref, *rest):
        st, aw, am, av = (rest[k * n:(k + 1) * n] for k in range(4))
        dx_ref, dw_ref, dg_ref, dqg_ref, dkg_ref, dmqg_ref = rest[4 * n:4 * n + 6]
        aouts = rest[4 * n + 6:]
        i = pl.program_id(0)

        for parity in range(2):
            @pl.when(i % 2 == parity)
            def _(parity=parity):
                for e in range(n):
                    if turn[e] == parity:
                        g = jnp.concatenate([_sum_chips(st[e].at[0]), _sum_chips(st[e].at[1])], axis=0)
                        d, mm, vv = _adamw_math(aw[e][...], g, am[e][...], av[e][...])
                        for k, val in enumerate((g, d, mm, vv)):
                            aouts[4 * e + k][...] = val

        @pl.when(i == 0)
        def _():
            dw_ref[...] = jnp.zeros_like(dw_ref)
            dg_ref[...] = jnp.zeros_like(dg_ref)
            dqg_ref[...] = jnp.zeros_like(dqg_ref)
            dkg_ref[...] = jnp.zeros_like(dkg_ref)
            dmqg_ref[...] = jnp.zeros_like(dmqg_ref)

        dqa, gq = _heads_norm_bwd(dq_ref[...], qa_ref[...], _small(pk_ref, "q_norm"))
        dka, gk = _heads_norm_bwd(dk_ref[...], ka_ref[...], _small(pk_ref, "k_norm"))
        dqma, gmq = _heads_norm_bwd(dqm_ref[...], qma_ref[...], _small(pk_ref, "mem_q_norm"))
        dqg_ref[...] += gq
        dkg_ref[...] += gk
        dmqg_ref[...] += gmq

        last = ((i + 1) * tm) % S == 0
        dcv = dcv_ref[...]
        nxt = jnp.where(last, 0.0, dcvn_ref[...])
        row = lax.broadcasted_iota(jnp.int32, dcv.shape, 0)
        n1 = jnp.where(row == tm - 1, nxt[0:1, :], pltpu.roll(dcv, tm - 1, 0))
        n2 = jnp.where(row == tm - 2, nxt[0:1, :], jnp.where(row == tm - 1, nxt[1:2, :], pltpu.roll(dcv, tm - 2, 0)))
        du = cw_ref[2:3, :] * dcv + cw_ref[1:2, :] * n1 + cw_ref[0:1, :] * n2
        d_proj = jnp.concatenate([_c(dqa), _c(dka), _c(dv_ref[...]), _c(du * cc_ref[...]),
                                  _c(dcb_ref[...]), _c(du * ch_ref[...]), _c(dqma)], axis=1)
        dw_ref[...] += _tn(d_proj, xn_ref[...])
        xv = x_ref[...]
        dv_, dg = _norm_bwd(_nn(d_proj, w_ref[...]), xv, _rstd(xv), _small(pk_ref, "norm_mix"))
        dx_ref[...] = dx1_ref[...] + dv_
        dg_ref[...] += dg

    tile = lambda w, col=0: pl.BlockSpec((tm, w), lambda i: (i, col))
    nhalo = pl.BlockSpec((8, CONV_W), lambda i: (jnp.minimum((i + 1) * (tm // 8), last_blk), 0))
    const = lambda shape: pl.BlockSpec(shape, lambda i: (0, 0))
    st_specs = [pl.BlockSpec((2, 4, s.shape[2], tile_w), lambda i: (0, 0, 0, i // 2)) for s in stages]
    w_specs = [pl.BlockSpec((w.shape[0], tile_w), lambda i: (0, i // 2)) for w in ws]
    res = _run("in_proj_bwd", body, (nsteps,),
               [dqn, dkn, dv, dcb, dcv, dcv, dqmn, proj, proj, proj, proj, proj, conv_w8, xn, x2d, dx1, pk, winT]
               + list(stages) + list(ws) + list(ms) + list(vs),
               [tile(ATT_W), tile(KV_W), tile(KV_W), tile(CONV_W), tile(CONV_W), nhalo, tile(MEM_W),
                tile(ATT_W, 0), tile(KV_W, 4), tile(CONV_W, 3), tile(CONV_W, 5), tile(MEM_W, 6), VM,
                tile(D), tile(D), tile(D), VM, VM] + st_specs + w_specs * 3,
               [SDS((T, D), f32), SDS((P, D), f32), SDS((1, D), f32), SDS((1, HD), f32), SDS((1, HD), f32),
                SDS((1, HD), f32)] + [SDS(w.shape, f32) for w in ws for _ in range(4)],
               [tile(D), pl.BlockSpec((P, D), lambda i: (0, 0)), const((1, D)), const((1, HD)), const((1, HD)),
                const((1, HD))] + [s for s in w_specs for _ in range(4)],
               vmem_mib=56)
    return res[:6], [res[6 + 4 * e:10 + 4 * e] for e in range(n)]


def mem_kv_bwd(dkm, dvm, kv, memn, mem2d, pk, wmkv):
    def body(dkm_ref, dvm_ref, kv_ref, mn_ref, m_ref, pk_ref, w_ref, dw_ref, dg_ref, dkg_ref):
        dkk, dkg = _heads_norm_bwd(dkm_ref[...], kv_ref[:, :MEM_W], _small(pk_ref, "mem_k_norm"))
        dkg_ref[...] = dkg
        dkv = _c(jnp.concatenate([dkk, dvm_ref[...]], axis=1))
        dw_ref[...] = _tn(mn_ref[...], dkv)
        mv = m_ref[...]
        dg_ref[...] = jnp.sum(_nt(dkv, w_ref[...]) * mv * _rstd(mv), axis=0, keepdims=True)

    return _run("mem_kv_bwd", body, (), [dkm, dvm, kv, memn, mem2d, pk, wmkv], [VM] * 7,
                [SDS(wmkv.shape, f32), SDS((1, mem2d.shape[1]), f32), SDS((1, HD), f32)], [VM] * 3, vmem_mib=40)


def _halves_view(g):
    return g.reshape(4, 2, g.shape[0] // 8, g.shape[1])


def kernel(x, mem, norm_mix, w_in, q_norm, k_norm, attn_sinks, conv_w, conv_b, norm_mem, w_mem_kv, mem_q_norm, mem_k_norm, out_norm_attn, out_norm_conv, out_norm_mem, w_out, norm_ffn, w_gate, w_up, w_down, loss_target, m_norm_mix, m_w_in, m_q_norm, m_k_norm, m_attn_sinks, m_conv_w, m_conv_b, m_norm_mem, m_w_mem_kv, m_mem_q_norm, m_mem_k_norm, m_out_norm_attn, m_out_norm_conv, m_out_norm_mem, m_w_out, m_norm_ffn, m_w_gate, m_w_up, m_w_down, v_norm_mix, v_w_in, v_q_norm, v_k_norm, v_attn_sinks, v_conv_w, v_conv_b, v_norm_mem, v_w_mem_kv, v_mem_q_norm, v_mem_k_norm, v_out_norm_attn, v_out_norm_conv, v_out_norm_mem, v_w_out, v_norm_ffn, v_w_gate, v_w_up, v_w_down):
    BL, S, D = x.shape
    T = BL * S
    TM = 256
    TM_BIG = min(512, S)
    w_small = dict(norm_mix=norm_mix, norm_mem=norm_mem, norm_ffn=norm_ffn, out_norm_attn=out_norm_attn,
                   out_norm_conv=out_norm_conv, out_norm_mem=out_norm_mem, conv_w=conv_w, conv_b=conv_b, q_norm=q_norm,
                   k_norm=k_norm, mem_q_norm=mem_q_norm, mem_k_norm=mem_k_norm, attn_sinks=attn_sinks)
    m_small = dict(norm_mix=m_norm_mix, norm_mem=m_norm_mem, norm_ffn=m_norm_ffn, out_norm_attn=m_out_norm_attn,
                   out_norm_conv=m_out_norm_conv, out_norm_mem=m_out_norm_mem, conv_w=m_conv_w, conv_b=m_conv_b,
                   q_norm=m_q_norm, k_norm=m_k_norm, mem_q_norm=m_mem_q_norm, mem_k_norm=m_mem_k_norm,
                   attn_sinks=m_attn_sinks)
    v_small = dict(norm_mix=v_norm_mix, norm_mem=v_norm_mem, norm_ffn=v_norm_ffn, out_norm_attn=v_out_norm_attn,
                   out_norm_conv=v_out_norm_conv, out_norm_mem=v_out_norm_mem, conv_w=v_conv_w, conv_b=v_conv_b,
                   q_norm=v_q_norm, k_norm=v_k_norm, mem_q_norm=v_mem_q_norm, mem_k_norm=v_mem_k_norm,
                   attn_sinks=v_attn_sinks)
    pk = _pack_small(w_small)

    rowblocks = lambda a, b, c, d, e, f: [a[0].T, b[0].T, c[0].T, d[0], e[0], f[0]]
    w_rb = rowblocks(w_in, w_gate, w_up, w_down, w_out, w_mem_kv)
    m_rb = rowblocks(m_w_in, m_w_gate, m_w_up, m_w_down, m_w_out, m_w_mem_kv)
    v_rb = rowblocks(v_w_in, v_w_gate, v_w_up, v_w_down, v_w_out, v_w_mem_kv)
    (winT_s,) = prep_weights("prep_w_in", w_rb[:1])
    cw_pad = jnp.zeros((8, 128), f32).at[:3, :HD].set(conv_w[0])
    (wgT_s, wuT_s, wd_s, wout_s, wmkv_s), (winT, cw_all) = prep_weights(
        "gather_w_in", w_rb[1:], _together([gather_two_legs([winT_s]), gather_exchange([cw_pad], [False])]))
    conv_w_full = jnp.transpose(cw_all.reshape(4, 8, 128)[:, :3, :HD], (1, 0, 2)).reshape(3, CONV_W)
    conv_w8 = jnp.zeros((8, CONV_W), f32).at[:3].set(conv_w_full)
    sink_rows = jnp.broadcast_to(attn_sinks.reshape(N_Q, 1), (N_Q, 128))

    x2d = x.reshape(T, D)
    mem2d = mem.reshape(-1, D)
    (xn, proj, qkv), near1 = in_proj_fwd(x2d, pk, winT, TM_BIG, gather_near_exchange([wgT_s, wout_s, wmkv_s], relay_early=1))
    (attn_out,), (wgT, wout, wmkv, *near2) = attn_fwd(
        qkv, sink_rows, BL, S, _together([gather_far_exchange(near1, relay_early=2), gather_near_exchange([wuT_s, wd_s], relay_early=2)]))
    memn, kv, km, vm = mem_kv_fwd(mem2d, pk, wmkv)
    (conv_out, mem_out, merged, x1, h), (wuT, wd) = mixer_tail_fwd(
        x2d, attn_out, proj, qkv, km, vm, conv_w8, pk, wout, S, TM_BIG, gather_far_exchange(near2, relay_early=2))

    dx1, dx2b, act, d_gate, d_up, loss8, d_norm_ffn = ffn_fwd_bwd(h, x1, loss_target.reshape(T, D), wgT, wuT, wd, pk, TM)
    F = wd.shape[0]
    g_wd = matmul_tn(act, dx2b, "dw_down", F // 2, min(T, 1024))
    g_wgT = matmul_tn(d_gate, h, "dw_gate", F // 2, min(T, 1024))
    g_wuT = matmul_tn(d_up, h, "dw_up", F // 2, min(T, 1024))

    d_attn, d_conv_out, d_mem_out, g_wout, d_gains = out_proj_bwd(dx1, merged, attn_out, conv_out, mem_out, pk, wout, TM_BIG)
    dqmn, dkm, dvm, dcb, dcv, d_cw8, d_cbias = mem_conv_bwd(
        d_mem_out, mem_out, d_conv_out, proj, qkv, km, vm, conv_w8, pk, S, min(1024, S), None)
    (dqn, dkn, dv, d_sink8), (st_wout, st_wgT, st_wuT, st_wd) = attn_bwd(
        qkv, d_attn, attn_out, sink_rows, BL, S,
        reduce_scatter_exchange([_halves_view(g) for g in (g_wout, g_wgT, g_wuT, g_wd)], BL * (S // BLK + 1),
                                load_step=[0, 1, 4, 7], send_step=[1, 4, 7, 10], relay_step=[6, 16, 25, 33]))
    (g_x, g_winT, d_norm_mix, d_qg, d_kg, d_mqg), late_res = in_proj_bwd(
        dqn, dkn, dv, dcb, dcv, dqmn, proj, conv_w8, xn, x2d, dx1, pk, winT, S, TM,
        [st_wgT, st_wuT, st_wd, st_wout], w_rb[1:5], m_rb[1:5], v_rb[1:5])
    g_wmkv, d_norm_mem, d_mkg = mem_kv_bwd(dkm, dvm, kv, memn, mem2d, pk, wmkv)

    tot, tail_stage = tail_reduce(d_norm_mix, d_norm_mem, d_norm_ffn, d_gains, d_cw8, d_cbias, d_qg, d_kg, d_mqg, d_mkg,
                                  d_sink8, loss8, [_halves_view(g) for g in (g_winT, g_wmkv)])
    loss = tot[5, 384]
    tail_res, _ = adamw_big("adamw_tail", tail_stage, [w_rb[0], w_rb[5]], [m_rb[0], m_rb[5]], [v_rb[0], v_rb[5]], 4)
    res = {"w_in": [a.T[None] for a in tail_res[0]], "w_gate": [a.T[None] for a in late_res[0]],
           "w_up": [a.T[None] for a in late_res[1]], "w_down": [a[None] for a in late_res[2]],
           "w_out": [a[None] for a in late_res[3]], "w_mem_kv": [a[None] for a in tail_res[1]]}
    res.update(adamw_small(tot, pk, _pack_small(m_small), _pack_small(v_small), {k: w_small[k].shape for k in SMALL}))

    order = ["norm_mix", "w_in", "q_norm", "k_norm", "attn_sinks", "conv_w", "conv_b", "norm_mem", "w_mem_kv",
             "mem_q_norm", "mem_k_norm", "out_norm_attn", "out_norm_conv", "out_norm_mem", "w_out", "norm_ffn",
             "w_gate", "w_up", "w_down"]
    return (loss, g_x.reshape(BL, S, D), *[res[n][0] for n in order], *[res[n][1] for n in order],
            *[res[n][2] for n in order], *[res[n][3] for n in order])
```

```python
import collections
import functools

import jax
import jax.numpy as jnp
import numpy as np
from jax import lax
from jax.experimental import pallas as pl
from jax.experimental.pallas import tpu as pltpu

f32 = jnp.float32
MXU = jnp.bfloat16
WIRE = jnp.bfloat16
EPS = 1e-6
NEG = -1e30
HD = 64
BLK = 128
N_Q, N_KV, N_MEMH = 8, 2, 4
GQA = N_Q // N_KV
ATT_W, KV_W, CONV_W, MEM_W = 512, 128, 256, 256
VMEM_MIB = 1024 * 1024
VMEM_LIMIT = 60 * VMEM_MIB
ADAM_LR, ADAM_B1, ADAM_B2, ADAM_EPS, ADAM_WD, ADAM_STEP = 0.001, 0.9, 0.999, 1e-08, 0.01, 10

MESH = pl.DeviceIdType.MESH
VM = pl.BlockSpec(memory_space=pltpu.VMEM)
ANY = pl.BlockSpec(memory_space=pl.ANY)
SDS = jax.ShapeDtypeStruct
DMA = pltpu.SemaphoreType.DMA


def _c(v):
    return v.astype(MXU)


def _nn(a, b):
    return lax.dot_general(a, b, (((1,), (0,)), ((), ())), preferred_element_type=f32)


def _nt(a, b):
    return lax.dot_general(a, b, (((1,), (1,)), ((), ())), preferred_element_type=f32)


def _tn(a, b):
    return lax.dot_general(a, b, (((0,), (0,)), ((), ())), preferred_element_type=f32)


def _rstd(v):
    return lax.rsqrt(jnp.mean(v * v, axis=-1, keepdims=True) + EPS)


def _norm_bwd(dy, v, r, g):
    dyg = dy * g
    dv = r * dyg - v * (r * r * r) * jnp.mean(dyg * v, axis=-1, keepdims=True)
    return dv, jnp.sum(dy * v * r, axis=0, keepdims=True)


def _split3(v):
    hi = _c(v)
    r1 = v - hi.astype(f32)
    mid = _c(r1)
    return hi, mid, _c(r1 - mid.astype(f32))


def _rowsum_mxu(v, width):
    ones = jnp.ones((v.shape[1], width), MXU)
    return sum(_nn(a, ones) for a in _split3(v))


def _seg_sums(v):
    r = lax.broadcasted_iota(jnp.int32, (2 * HD, 2 * HD), 0) // HD
    c = lax.broadcasted_iota(jnp.int32, (2 * HD, 2 * HD), 1) // HD
    bd = (r == c).astype(MXU)
    outs = []
    for b in range(v.shape[1] // (2 * HD)):
        outs.append(sum(_nn(a, bd) for a in _split3(v[:, b * 2 * HD:(b + 1) * 2 * HD])))
    return outs[0] if len(outs) == 1 else jnp.concatenate(outs, axis=1)


def _lanes(g, width):
    return jnp.concatenate([g] * (width // HD), axis=1)


def _heads_rstd(v):
    return lax.rsqrt(_seg_sums(v * v) * (1.0 / HD) + EPS)


def _heads_norm_bwd(dy, v, g):
    r = _heads_rstd(v)
    gl = _lanes(g, v.shape[1])
    dyg = dy * gl
    dv = r * dyg - v * (r * r * r) * (_seg_sums(dyg * v) * (1.0 / HD))
    dgl = jnp.sum(dy * v * r, axis=0, keepdims=True)
    return dv, sum(dgl[:, s * HD:(s + 1) * HD] for s in range(v.shape[1] // HD))


def _exp_scores(s, extra=None):
    m = jnp.max(s, axis=-1, keepdims=True)
    if extra is None:
        return jnp.exp(s - m), None
    m = jnp.maximum(m, extra)
    return jnp.exp(s - m), jnp.exp(extra - m)


def _place():
    return lax.axis_index("x"), lax.axis_index("y"), lax.axis_index("c")


SMALL_AT = {"norm_mix": (0, 0, 1024), "norm_mem": (1, 0, 1024), "norm_ffn": (2, 0, 1024),
            "out_norm_attn": (3, 0, ATT_W), "out_norm_conv": (3, ATT_W, CONV_W), "out_norm_mem": (3, ATT_W + CONV_W, MEM_W),
            "conv_b": (4, 3 * CONV_W, CONV_W), "q_norm": (5, 0, HD), "k_norm": (5, HD, HD), "mem_q_norm": (5, 2 * HD, HD),
            "mem_k_norm": (5, 3 * HD, HD), "attn_sinks": (5, 256, N_Q)}
SMALL = ("norm_mix", "norm_mem", "norm_ffn", "out_norm_attn", "out_norm_conv", "out_norm_mem", "conv_w", "conv_b",
         "q_norm", "k_norm", "mem_q_norm", "mem_k_norm", "attn_sinks")


def _small(pk_ref, name):
    r, c0, w = SMALL_AT[name]
    return pk_ref[r:r + 1, c0:c0 + w]


def _pack_small(d):
    z = lambda n: jnp.zeros((1, n), f32)
    row3 = jnp.concatenate([d["out_norm_attn"], d["out_norm_conv"], d["out_norm_mem"]], axis=1)
    row4 = jnp.concatenate([d["conv_w"].reshape(1, 3 * HD), z(3 * CONV_W - 3 * HD), d["conv_b"]], axis=1)
    row5 = jnp.concatenate([d["q_norm"], d["k_norm"], d["mem_q_norm"], d["mem_k_norm"], d["attn_sinks"],
                            z(1024 - 4 * HD - N_Q)], axis=1)
    return jnp.concatenate([d["norm_mix"], d["norm_mem"], d["norm_ffn"], row3, row4, row5, z(1024), z(1024)], axis=0)


def _other_chips(x, y):
    return [(1 - x, y), (x, 1 - y), (1 - x, 1 - y)]


Exchange = collections.namedtuple("Exchange", "ins outs sems start finish relays aliases", defaults=((), {}))


def _together(exchanges):
    def bounds(key):
        at, out = 0, []
        for ex in exchanges:
            out.append((at, at + len(getattr(ex, key))))
            at += len(getattr(ex, key))
        return out

    bi, bo, bs = bounds("ins"), bounds("outs"), bounds("sems")

    def of(i, fn):
        return lambda xa, xo, xs: fn(xa[bi[i][0]:bi[i][1]], xo[bo[i][0]:bo[i][1]], xs[bs[i][0]:bs[i][1]])

    def every(name):
        fns = [of(i, getattr(ex, name)) for i, ex in enumerate(exchanges)]

        def run(xa, xo, xs):
            for fn in fns:
                fn(xa, xo, xs)
        return run

    aliases = {}
    for i, ex in enumerate(exchanges):
        aliases.update({bi[i][0] + a: bo[i][0] + o for a, o in ex.aliases.items()})
    return Exchange([a for ex in exchanges for a in ex.ins], [o for ex in exchanges for o in ex.outs],
                    [s for ex in exchanges for s in ex.sems], every("start"), every("finish"),
                    [(sbe, of(i, fn)) for i, ex in enumerate(exchanges) for sbe, fn in ex.relays], aliases)


def _run(name, body, grid, ins, in_specs, out_shape, out_specs, scratch=(), exchange=None):
    ins, in_specs, out_shape, out_specs, scratch = list(ins), list(in_specs), list(out_shape), list(out_specs), list(scratch)
    ni, no, ns = len(ins), len(out_shape), len(scratch)
    ex = exchange
    if ex is not None:
        nxi, nxo = len(ex.ins), len(ex.outs)

    def call_body(*refs):
        if ex is None:
            body(*refs)
            return
        a, xa = refs[:ni], refs[ni:ni + nxi]
        o, xo = refs[ni + nxi:ni + nxi + no], refs[ni + nxi + no:ni + nxi + no + nxo]
        s, xs = refs[ni + nxi + no + nxo:ni + nxi + no + nxo + ns], refs[ni + nxi + no + nxo + ns:]
        if grid:
            first = functools.reduce(jnp.logical_and, [pl.program_id(d) == 0 for d in range(len(grid))])
            last = functools.reduce(jnp.logical_and, [pl.program_id(d) == grid[d] - 1 for d in range(len(grid))])
            pl.when(first)(lambda: ex.start(xa, xo, xs))
            body(*a, *o, *s)
            nsteps = functools.reduce(lambda p, q: p * q, grid)
            for before_end, fn in ex.relays:
                at = np.unravel_index(max(nsteps - 1 - before_end, 0), grid)
                here = functools.reduce(jnp.logical_and, [pl.program_id(d) == int(at[d]) for d in range(len(grid))])
                pl.when(here)(functools.partial(fn, xa, xo, xs))
            pl.when(last)(lambda: ex.finish(xa, xo, xs))
        else:
            ex.start(xa, xo, xs)
            if body is not None:
                body(*a, *o, *s)
            for _, fn in ex.relays:
                fn(xa, xo, xs)
            ex.finish(xa, xo, xs)

    kw = dict(grid=grid) if grid else {}
    if ex is not None:
        if ex.aliases:
            kw["input_output_aliases"] = {ni + i: no + o for i, o in ex.aliases.items()}
        ins, in_specs = ins + list(ex.ins), in_specs + [ANY] * nxi
        out_shape, out_specs = out_shape + list(ex.outs), out_specs + [ANY] * nxo
        scratch = scratch + list(ex.sems)
    res = pl.pallas_call(
        call_body, name=name, out_shape=out_shape, in_specs=in_specs, out_specs=out_specs, scratch_shapes=scratch,
        compiler_params=pltpu.CompilerParams(dimension_semantics=("arbitrary",) * len(grid) if grid else None,
                                             vmem_limit_bytes=VMEM_LIMIT), **kw)(*ins)
    res = list(res)
    return (res[:no], res[no:]) if ex is not None else res


def _remote(src, dst, ssem, rsem, dev):
    return pltpu.make_async_remote_copy(src_ref=src, dst_ref=dst, send_sem=ssem, recv_sem=rsem,
                                        device_id=dev, device_id_type=MESH)


def gather_exchange(shards, split, relay_early=0):
    n = len(shards)

    def rows(ref, e, kk, half=None):
        R = shards[e].shape[0]
        if half is None:
            return ref.at[pl.ds(pl.multiple_of(kk * R, 8), R)]
        return ref.at[pl.ds(pl.multiple_of(kk * R + half * (R // 2), 8), R // 2)]

    def ici(src, dst, sm, e, j, chip_j, x, y, c):
        k = 2 * x + y
        if split[e]:
            s = src[e].at[pl.ds(pl.multiple_of(c * (shards[e].shape[0] // 2), 8), shards[e].shape[0] // 2)]
            return _remote(s, rows(dst[e], e, k, c), sm[0].at[6 * e + j], sm[1].at[6 * e + j], (*chip_j, c))
        return _remote(src[e], rows(dst[e], e, k), sm[0].at[6 * e + j], sm[1].at[6 * e + j], (*chip_j, c))

    def landed(dst, e, chip_j, c):
        kj = 2 * chip_j[0] + chip_j[1]
        return rows(dst[e], e, kj, c) if split[e] else rows(dst[e], e, kj)

    def forward(dst, sm, e, j, chip_j, x, y, c, sender_c):
        kj = 2 * chip_j[0] + chip_j[1]
        r = rows(dst[e], e, kj, sender_c)
        return _remote(r, r, sm[0].at[6 * e + 3 + j], sm[1].at[6 * e + 3 + j], (x, y, 1 - c))

    def local(src, dst, sm, e, x, y):
        return pltpu.make_async_copy(src[e], rows(dst[e], e, 2 * x + y), sm[2].at[e])

    def start(src, dst, sm):
        x, y, c = _place()
        for e in range(n):
            local(src, dst, sm, e, x, y).start()
            for j, chip_j in enumerate(_other_chips(x, y)):
                ici(src, dst, sm, e, j, chip_j, x, y, c).start()

    def relay(src, dst, sm):
        x, y, c = _place()
        for e in range(n):
            for j, chip_j in enumerate(_other_chips(x, y)):
                r = landed(dst, e, chip_j, c)
                _remote(r, r, sm[0].at[6 * e + j], sm[1].at[6 * e + j], (*chip_j, c)).wait_recv()
                if split[e]:
                    forward(dst, sm, e, j, chip_j, x, y, c, c).start()

    def finish(src, dst, sm):
        x, y, c = _place()
        chips = _other_chips(x, y)
        for e in range(n):
            for j, chip_j in enumerate(chips):
                if split[e]:
                    forward(dst, sm, e, j, chip_j, x, y, c, 1 - c).wait_recv()
        for e in range(n):
            for j, chip_j in enumerate(chips):
                ici(src, dst, sm, e, j, chip_j, x, y, c).wait_send()
                if split[e]:
                    forward(dst, sm, e, j, chip_j, x, y, c, c).wait_send()
            local(src, dst, sm, e, x, y).wait()

    outs = [SDS((4 * s.shape[0], s.shape[1]), s.dtype) for s in shards]
    return Exchange(list(shards), outs, [DMA((6 * n,)), DMA((6 * n,)), DMA((n,))], start, finish, [(relay_early, relay)])


def _block_rows(ref, R, kk, half, quarter=None):
    hr = R // 2
    if quarter is None:
        return ref.at[pl.ds(pl.multiple_of(kk * R + half * hr, 8), hr)]
    return ref.at[pl.ds(pl.multiple_of(kk * R + half * hr + quarter * (hr // 2), 8), hr // 2)]


def gather_near_exchange(shards, relay_early=0):
    n = len(shards)
    R = [s.shape[0] for s in shards]

    def ici(src, dst, sm, e, j, chip_j, x, y, c):
        half = src[e].at[pl.ds(pl.multiple_of(c * (R[e] // 2), 8), R[e] // 2)]
        return _remote(half, _block_rows(dst[e], R[e], 2 * x + y, c), sm[0].at[4 * e + j], sm[1].at[4 * e + j], (*chip_j, c))

    def forward(dst, sm, e, j, chip_j, x, y, c, sender_c):
        r = _block_rows(dst[e], R[e], 2 * chip_j[0] + chip_j[1], sender_c)
        return _remote(r, r, sm[0].at[4 * e + 2 + j], sm[1].at[4 * e + 2 + j], (x, y, 1 - c))

    def local(src, dst, sm, e, x, y):
        return pltpu.make_async_copy(src[e], dst[e].at[pl.ds(pl.multiple_of((2 * x + y) * R[e], 8), R[e])], sm[2].at[e])

    def start(src, dst, sm):
        x, y, c = _place()
        for e in range(n):
            local(src, dst, sm, e, x, y).start()
            for j, chip_j in enumerate(_other_chips(x, y)[:2]):
                ici(src, dst, sm, e, j, chip_j, x, y, c).start()

    def relay(src, dst, sm):
        x, y, c = _place()
        for e in range(n):
            for j, chip_j in enumerate(_other_chips(x, y)[:2]):
                r = _block_rows(dst[e], R[e], 2 * chip_j[0] + chip_j[1], c)
                _remote(r, r, sm[0].at[4 * e + j], sm[1].at[4 * e + j], (*chip_j, c)).wait_recv()
                forward(dst, sm, e, j, chip_j, x, y, c, c).start()

    def finish(src, dst, sm):
        x, y, c = _place()
        near = _other_chips(x, y)[:2]
        for e in range(n):
            for j, chip_j in enumerate(near):
                forward(dst, sm, e, j, chip_j, x, y, c, 1 - c).wait_recv()
        for e in range(n):
            for j, chip_j in enumerate(near):
                ici(src, dst, sm, e, j, chip_j, x, y, c).wait_send()
                forward(dst, sm, e, j, chip_j, x, y, c, c).wait_send()
            local(src, dst, sm, e, x, y).wait()

    outs = [SDS((4 * s.shape[0], s.shape[1]), s.dtype) for s in shards]
    return Exchange(list(shards), outs, [DMA((4 * n,)), DMA((4 * n,)), DMA((n,))], start, finish, [(relay_early, relay)])


def gather_far_exchange(bufs, relay_early=0):
    n = len(bufs)
    R = [b.shape[0] // 4 for b in bufs]

    def send(src, dst, sm, e, j, x, y, c):
        to, of = _other_chips(x, y)[j], _other_chips(x, y)[1 - j]
        kk = 2 * of[0] + of[1]
        return _remote(_block_rows(src[e], R[e], kk, c, j), _block_rows(dst[e], R[e], kk, c, j),
                       sm[0].at[4 * e + j], sm[1].at[4 * e + j], (*to, c))

    def landed(dst, e, j, x, y, half):
        return _block_rows(dst[e], R[e], 2 * (1 - x) + (1 - y), half, j)

    def forward(dst, sm, e, j, x, y, c, sender_c):
        r = landed(dst, e, j, x, y, sender_c)
        return _remote(r, r, sm[0].at[4 * e + 2 + j], sm[1].at[4 * e + 2 + j], (x, y, 1 - c))

    def start(src, dst, sm):
        x, y, c = _place()
        for e in range(n):
            for j in range(2):
                send(src, dst, sm, e, j, x, y, c).start()

    def relay(src, dst, sm):
        x, y, c = _place()
        for e in range(n):
            for j in range(2):
                r = landed(dst, e, j, x, y, c)
                _remote(r, r, sm[0].at[4 * e + j], sm[1].at[4 * e + j], (*_other_chips(x, y)[j], c)).wait_recv()
                forward(dst, sm, e, j, x, y, c, c).start()

    def finish(src, dst, sm):
        x, y, c = _place()
        for e in range(n):
            for j in range(2):
                forward(dst, sm, e, j, x, y, c, 1 - c).wait_recv()
        for e in range(n):
            for j in range(2):
                send(src, dst, sm, e, j, x, y, c).wait_send()
                forward(dst, sm, e, j, x, y, c, c).wait_send()

    outs = [SDS(b.shape, b.dtype) for b in bufs]
    return Exchange(list(bufs), outs, [DMA((4 * n,)), DMA((4 * n,))], start, finish, [(relay_early, relay)],
                    {i: i for i in range(n)})


def gather_two_legs(shards):
    near = gather_near_exchange(shards)
    far = gather_far_exchange(near.outs)

    def finish(src, dst, sm):
        near.relays[0][1](src, dst, sm[:3])
        near.finish(src, dst, sm[:3])
        far.start(dst, dst, sm[3:])
        far.relays[0][1](dst, dst, sm[3:])
        far.finish(dst, dst, sm[3:])

    return Exchange(near.ins, near.outs, list(near.sems) + list(far.sems),
                    lambda src, dst, sm: near.start(src, dst, sm[:3]), finish)


def scatter_exchange(parts, relay_before_end=None, want_issue=False):
    n = len(parts)
    by_entry = relay_before_end is not None
    relay_before_end = relay_before_end or [0] * n

    def ici(p, st, sm, e, j, chip_j, x, y, c):
        k, kj = 2 * x + y, 2 * chip_j[0] + chip_j[1]
        return _remote(p[e].at[kj], st[e].at[c, k], sm[0].at[8 * e + j], sm[1].at[8 * e + j], (*chip_j, c))

    def own(p, st, sm, e, x, y, c):
        k = 2 * x + y
        return _remote(p[e].at[k], st[e].at[c, k], sm[0].at[8 * e + 3], sm[1].at[8 * e + 3], (x, y, 1 - c))

    def forward(st, sm, e, j, chip_j, x, y, c, sender_c):
        kj = 2 * chip_j[0] + chip_j[1]
        r = st[e].at[sender_c, kj]
        return _remote(r, r, sm[0].at[8 * e + 4 + j], sm[1].at[8 * e + 4 + j], (x, y, 1 - c))

    def local(p, st, sm, e, x, y, c):
        k = 2 * x + y
        return pltpu.make_async_copy(p[e].at[k], st[e].at[c, k], sm[2].at[e])

    def issue(e, p, st, sm):
        x, y, c = _place()
        for j, chip_j in enumerate(_other_chips(x, y)):
            ici(p, st, sm, e, j, chip_j, x, y, c).start()
        local(p, st, sm, e, x, y, c).start()
        own(p, st, sm, e, x, y, c).start()

    def start(p, st, sm, before_slot=None):
        x, y, c = _place()
        if by_entry:
            for e in range(n):
                issue(e, p, st, sm)
            return
        for j, chip_j in enumerate(_other_chips(x, y)):
            if before_slot is not None:
                before_slot(j, 2 * chip_j[0] + chip_j[1])
            for e in range(n):
                ici(p, st, sm, e, j, chip_j, x, y, c).start()
        if before_slot is not None:
            before_slot(3, 2 * x + y)
        for e in range(n):
            local(p, st, sm, e, x, y, c).start()
            own(p, st, sm, e, x, y, c).start()

    def relay(e, p, st, sm):
        x, y, c = _place()
        for j, chip_j in enumerate(_other_chips(x, y)):
            kj = 2 * chip_j[0] + chip_j[1]
            r = st[e].at[c, kj]
            _remote(r, r, sm[0].at[8 * e + j], sm[1].at[8 * e + j], (*chip_j, c)).wait_recv()
            forward(st, sm, e, j, chip_j, x, y, c, c).start()

    def finish(p, st, sm):
        x, y, c = _place()
        k = 2 * x + y
        chips = _other_chips(x, y)
        for e in range(n):
            r = st[e].at[1 - c, k]
            _remote(r, r, sm[0].at[8 * e + 3], sm[1].at[8 * e + 3], (x, y, 1 - c)).wait_recv()
            for j, chip_j in enumerate(chips):
                forward(st, sm, e, j, chip_j, x, y, c, 1 - c).wait_recv()
        for e in range(n):
            own(p, st, sm, e, x, y, c).wait_send()
            for j, chip_j in enumerate(chips):
                ici(p, st, sm, e, j, chip_j, x, y, c).wait_send()
                forward(st, sm, e, j, chip_j, x, y, c, c).wait_send()
            local(p, st, sm, e, x, y, c).wait()

    outs = [SDS((2,) + a.shape, a.dtype) for a in parts]
    ex = Exchange(list(parts), outs, [DMA((8 * n,)), DMA((8 * n,)), DMA((n,))], start, finish,
                  [(relay_before_end[e], functools.partial(relay, e)) for e in range(n)])
    return (ex, issue) if want_issue else ex


def reduce_scatter_exchange(grads, nsteps, load_step, send_step, relay_step):
    n = len(grads)
    hrs = [g.shape[2] for g in grads]
    C = grads[0].shape[3]
    scatter, issue = scatter_exchange([SDS((4,) + g.shape[2:], WIRE) for g in grads], want_issue=True)
    hand_on = [fn for _, fn in scatter.relays]

    def refs(xs):
        return xs[:3], xs[3], xs[4], xs[5], xs[6], xs[7:7 + n], xs[7 + n:]

    def push(e, g, psem, qsem, sib_st):
        x, y, c = _place()
        return _remote(g[e].at[:, 1 - c], sib_st[e], psem.at[e], qsem.at[e], (x, y, 1 - c))

    def fetch(e, g, lsem, own_st):
        _, _, c = _place()
        return pltpu.make_async_copy(g[e].at[:, c], own_st.at[e % 2, :, pl.ds(0, hrs[e])], lsem.at[e])

    def start(g, xo, xs):
        _, _, psem, qsem, _, sib_st, _ = refs(xs)
        for e in range(n):
            push(e, g, psem, qsem, sib_st).start()

    def load(e, g, xo, xs):
        _, lsem, _, _, own_st, _, _ = refs(xs)
        fetch(e, g, lsem, own_st).start()

    def send(e, g, xo, xs):
        sm, lsem, psem, qsem, own_st, sib_st, part = refs(xs)
        fetch(e, g, lsem, own_st).wait()
        push(e, g, psem, qsem, sib_st).wait_recv()
        part[e][...] = (own_st[e % 2, :, 0:hrs[e]] + sib_st[e][...]).astype(WIRE)
        issue(e, part, xo, sm)

    def relay(e, g, xo, xs):
        sm, _, _, _, _, _, part = refs(xs)
        hand_on[e](part, xo, sm)

    def finish(g, xo, xs):
        sm, _, psem, qsem, _, sib_st, part = refs(xs)
        scatter.finish(part, xo, sm)
        for e in range(n):
            push(e, g, psem, qsem, sib_st).wait_send()

    plan = sorted([(min(step[e], nsteps - 1), phase, e) for phase, step in enumerate((load_step, send_step, relay_step))
                   for e in range(n)])
    stage = (load, send, relay)
    relays = [(nsteps - 1 - at, functools.partial(stage[phase], e)) for at, phase, e in plan]
    scratch = (list(scatter.sems) + [DMA((n,)), DMA((n,)), DMA((n,))] + [pltpu.VMEM((2, 4, max(hrs), C), f32)]
               + [pltpu.VMEM((4, hr, C), f32) for hr in hrs] + [pltpu.VMEM((4, hr, C), WIRE) for hr in hrs])
    return Exchange(list(grads), scatter.outs, scratch, start, finish, relays)


def tail_reduce(d_norm_mix, d_norm_mem, d_norm_ffn, d_gains, d_cw8, d_cbias, d_qg, d_kg, d_mqg, d_mkg, d_sink8, loss8, tail):
    n = len(tail)
    scatter = scatter_exchange([SDS((4,) + a.shape[2:], WIRE) for a in tail])

    def half_copy(g, sib, hsem, e, j, slot, x, y, c):
        return _remote(g[e].at[slot, 1 - c], sib[e].at[slot], hsem[0].at[4 * e + j], hsem[1].at[4 * e + j], (x, y, 1 - c))

    def body(nm_ref, nmem_ref, nf_ref, gn_ref, cw_ref, cb_ref, qg_ref, kg_ref, mqg_ref, mkg_ref, sk_ref, ls_ref, *rest):
        g, o_ref, st = rest[:n], rest[n], rest[n + 1:2 * n + 1]
        buf, ssem, rsem = rest[2 * n + 1:2 * n + 4]
        own, sib, part = (rest[2 * n + 4 + i * n:2 * n + 4 + (i + 1) * n] for i in range(3))
        lsem = rest[5 * n + 4]
        hsem, xsem = rest[5 * n + 5:5 * n + 7], rest[5 * n + 7:]
        x, y, c = _place()
        loads = [pltpu.make_async_copy(g[e].at[:, c], own[e], lsem.at[e]) for e in range(n)]
        for ld in loads:
            ld.start()
        for j, slot in enumerate([2 * cx + cy for cx, cy in _other_chips(x, y)] + [2 * x + y]):
            for e in range(n):
                half_copy(g, sib, hsem, e, j, slot, x, y, c).start()
        me = 4 * x + 2 * y + c
        mine = buf.at[me]
        mine[...] = jnp.zeros((8, 1024), f32)
        mine[0:1, :] = nm_ref[...]
        mine[1:2, :] = nmem_ref[...]
        mine[2:3, :] = nf_ref[...]
        mine[3:4, :] = gn_ref[...]
        for j in range(3):
            mine[4:5, pl.ds(j * CONV_W, CONV_W)] = cw_ref[j:j + 1, :]
        mine[4:5, pl.ds(3 * CONV_W, CONV_W)] = cb_ref[...]
        for j, r in enumerate((qg_ref, kg_ref, mqg_ref, mkg_ref)):
            mine[5:6, pl.ds(j * HD, HD)] = r[...]
        mine[5:6, pl.ds(256, 128)] = sk_ref[0:1, :]
        mine[5:6, pl.ds(384, 128)] = ls_ref[0:1, :]

        def peer_of(m):
            return (1 - x if m & 4 else x, 1 - y if m & 2 else y, 1 - c if m & 1 else c)

        for m in range(1, 8):
            _remote(mine, mine, ssem.at[m - 1], rsem.at[m - 1], peer_of(m)).start()
        for ld in loads:
            ld.wait()

        def chip_partial(j, slot):
            for e in range(n):
                half_copy(g, sib, hsem, e, j, slot, x, y, c).wait()
                part[e][slot] = (own[e][slot] + sib[e][slot]).astype(WIRE)

        scatter.start(part, st, xsem, chip_partial)
        for _, hand_on in scatter.relays:
            hand_on(part, st, xsem)
        scatter.finish(part, st, xsem)
        for m in range(1, 8):
            p = peer_of(m)
            got = buf.at[4 * p[0] + 2 * p[1] + p[2]]
            _remote(got, got, ssem.at[m - 1], rsem.at[m - 1], p).wait_recv()
        for m in range(1, 8):
            _remote(mine, mine, ssem.at[m - 1], rsem.at[m - 1], peer_of(m)).wait_send()
        acc = buf[0]
        for d in range(1, 8):
            acc = acc + buf[d]
        o_ref[...] = acc

    ins = [d_norm_mix, d_norm_mem, d_norm_ffn, d_gains, d_cw8, d_cbias, d_qg, d_kg, d_mqg, d_mkg, d_sink8, loss8]
    half_shape = [(4,) + a.shape[2:] for a in tail]
    scratch = ([pltpu.VMEM((8, 8, 1024), f32), DMA((7,)), DMA((7,))]
               + [pltpu.VMEM(s, f32) for s in half_shape] * 2 + [pltpu.VMEM(s, WIRE) for s in half_shape]
               + [DMA((n,)), DMA((4 * n,)), DMA((4 * n,))] + list(scatter.sems))
    res = _run("tail_reduce", body, (), ins + list(tail), [VM] * len(ins) + [ANY] * n,
               [SDS((8, 1024), f32)] + list(scatter.outs), [VM] + [ANY] * n, scratch=scratch)
    return res[0], res[1:]


def _adamw_math(w, g, m, v):
    m = ADAM_B1 * m + (1.0 - ADAM_B1) * g
    v = ADAM_B2 * v + (1.0 - ADAM_B2) * (g * g)
    m_hat = m / (1.0 - ADAM_B1 ** ADAM_STEP)
    v_hat = v / (1.0 - ADAM_B2 ** ADAM_STEP)
    delta = -ADAM_LR * (m_hat / (jnp.sqrt(v_hat) + ADAM_EPS) + ADAM_WD * w)
    return delta, m, v


def _sum_chips(st):
    return ((st[0].astype(f32) + st[1].astype(f32)) + st[2].astype(f32)) + st[3].astype(f32)


def adamw_big(name, stages, ws, ms, vs, nstep, exchange=None):
    n = len(stages)

    def body(*refs):
        st, w, m, v = refs[:n], refs[n:2 * n], refs[2 * n:3 * n], refs[3 * n:4 * n]
        outs = refs[4 * n:]
        for e in range(n):
            g = jnp.concatenate([_sum_chips(st[e].at[0]), _sum_chips(st[e].at[1])], axis=0)
            d, mm, vv = _adamw_math(w[e][...], g, m[e][...], v[e][...])
            outs[4 * e][...] = g
            outs[4 * e + 1][...] = d
            outs[4 * e + 2][...] = mm
            outs[4 * e + 3][...] = vv

    st_specs, w_specs = [], []
    for e in range(n):
        _, _, hr, C = stages[e].shape
        st_specs.append(pl.BlockSpec((2, 4, hr, C // nstep), lambda i: (0, 0, 0, i)))
        w_specs.append(pl.BlockSpec((2 * hr, C // nstep), lambda i: (0, i)))
    out_specs = [s for s in w_specs for _ in range(4)]
    out_shape = [SDS(w.shape, f32) for w in ws for _ in range(4)]
    res = _run(name, body, (nstep,), list(stages) + list(ws) + list(ms) + list(vs), st_specs + w_specs * 3,
               out_shape, out_specs, exchange=exchange)
    res, sent = res if exchange is not None else (res, None)
    return [res[4 * e:4 * e + 4] for e in range(n)], sent


def adamw_small(tot, pk_w, pk_m, pk_v, shapes):
    def body(tot_ref, w_ref, m_ref, v_ref, *outs):
        x, y, _ = _place()
        chip = 2 * x + y
        taps = []
        for j in range(3):
            mine = tot_ref[4:5, j * CONV_W:j * CONV_W + HD]
            for s in range(1, 4):
                mine = jnp.where(chip == s, tot_ref[4:5, j * CONV_W + s * HD:j * CONV_W + (s + 1) * HD], mine)
            taps.append(mine)
        row4 = jnp.concatenate(taps + [jnp.zeros((1, 3 * CONV_W - 3 * HD), f32), tot_ref[4:5, 3 * CONV_W:]], axis=1)
        tot_v = tot_ref[...]
        row = lax.broadcasted_iota(jnp.int32, tot_v.shape, 0)
        g = jnp.where(row == 4, jnp.broadcast_to(row4, tot_v.shape), tot_v)
        d, mm, vv = _adamw_math(w_ref[...], g, m_ref[...], v_ref[...])
        for i, name in enumerate(SMALL):
            for k, val in enumerate((g, d, mm, vv)):
                if name == "conv_w":
                    outs[4 * i + k][...] = jnp.concatenate([val[4:5, j * HD:(j + 1) * HD] for j in range(3)], axis=0)[None]
                else:
                    r, c0, w = SMALL_AT[name]
                    outs[4 * i + k][...] = val[r:r + 1, c0:c0 + w]

    out_shape = [SDS(shapes[k], f32) for k in SMALL for _ in range(4)]
    res = _run("adamw_small", body, (), [tot, pk_w, pk_m, pk_v], [VM] * 4, out_shape, [VM] * len(out_shape))
    return {k: res[4 * i:4 * i + 4] for i, k in enumerate(SMALL)}


def prep_weights(name, shards, exchange=None):
    n = len(shards)

    def body(*refs):
        for e in range(n):
            refs[n + e][...] = _c(refs[e][...])

    return _run(name, body, (), shards, [VM] * n, [SDS(a.shape, MXU) for a in shards], [VM] * n, exchange=exchange)


def mem_kv_fwd(mem2d, pk, wmkv):
    M, D = mem2d.shape

    def body(m_ref, pk_ref, w_ref, mn_ref, kv_ref, km_ref, vm_ref):
        m = m_ref[...]
        mn = _c(m * _rstd(m) * _small(pk_ref, "norm_mem"))
        mn_ref[...] = mn
        kv = _nn(mn, w_ref[...])
        kv_ref[...] = kv
        kk = kv[:, :MEM_W]
        km_ref[...] = _c(kk * _heads_rstd(kk) * _lanes(_small(pk_ref, "mem_k_norm"), MEM_W))
        vm_ref[...] = _c(kv[:, MEM_W:])

    return _run("mem_kv_fwd", body, (), [mem2d, pk, wmkv], [VM] * 3,
                [SDS((M, D), MXU), SDS((M, 2 * MEM_W), f32), SDS((M, MEM_W), MXU), SDS((M, MEM_W), MXU)], [VM] * 4)


QKV_W = ATT_W + 2 * KV_W + MEM_W


def in_proj_fwd(x2d, pk, winT, tm, exchange):
    T, D = x2d.shape
    P = winT.shape[0]

    def body(x_ref, pk_ref, w_ref, xn_ref, proj_ref, qkv_ref):
        xv = x_ref[...]
        xn = _c(xv * _rstd(xv) * _small(pk_ref, "norm_mix"))
        xn_ref[...] = xn
        proj = _nt(xn, w_ref[...])
        proj_ref[...] = proj
        q, k = proj[:, :ATT_W], proj[:, ATT_W:ATT_W + KV_W]
        qm = proj[:, P - MEM_W:]
        qkv_ref[...] = jnp.concatenate(
            [_c(q * _heads_rstd(q) * _lanes(_small(pk_ref, "q_norm"), ATT_W)),
             _c(k * _heads_rstd(k) * _lanes(_small(pk_ref, "k_norm"), KV_W)),
             _c(proj[:, ATT_W + KV_W:ATT_W + 2 * KV_W]),
             _c(qm * _heads_rstd(qm) * _lanes(_small(pk_ref, "mem_q_norm"), MEM_W))], axis=1)

    return _run("in_proj_fwd", body, (T // tm,), [x2d, pk, winT],
                [pl.BlockSpec((tm, D), lambda i: (i, 0)), VM, VM],
                [SDS((T, D), MXU), SDS((T, P), f32), SDS((T, QKV_W), MXU)],
                [pl.BlockSpec((tm, D), lambda i: (i, 0)), pl.BlockSpec((tm, P), lambda i: (i, 0)),
                 pl.BlockSpec((tm, QKV_W), lambda i: (i, 0))],
                exchange=exchange)


def _swa_bias_table():
    r = np.arange(GQA * BLK)[:, None]
    k = np.arange(2 * BLK)[None, :]
    dist = (r % BLK) + BLK - k
    band = (dist >= 0) & (dist < BLK)
    tab = np.empty((2, N_KV, GQA * BLK, 2 * BLK), np.float32)
    for later in range(2):
        valid = band & ((k >= BLK) | (later == 1))
        for g in range(N_KV):
            slope = 2.0 ** -(g * GQA + r // BLK + 1.0)
            tab[later, g] = np.where(valid, -slope * dist, NEG)
    return jnp.asarray(tab)


def _sink_column(g, sk_ref):
    hrow = lax.broadcasted_iota(jnp.int32, (GQA * BLK, 1), 0) // BLK
    sink = jnp.zeros((GQA * BLK, 1), f32)
    for hh in range(GQA):
        sink = jnp.where(hrow == hh, sk_ref[g * GQA + hh:g * GQA + hh + 1, 0:1], sink)
    return sink


def _stack_heads(v, g):
    return jnp.concatenate([v[:, (g * GQA + hh) * HD:(g * GQA + hh + 1) * HD] for hh in range(GQA)], axis=0)


def attn_fwd(qkv, sink_rows, BL, S, exchange, qb=2):
    NS = S // (qb * BLK)
    T = BL * S

    def body(q_ref, kc_ref, kp_ref, vc_ref, vp_ref, sk_ref, tab_ref, o_ref):
        j = pl.program_id(1)
        kall = jnp.concatenate([kp_ref[...], kc_ref[...]], axis=0)
        vall = jnp.concatenate([vp_ref[...], vc_ref[...]], axis=0)
        ones = jnp.ones((2 * BLK, HD), MXU)
        for b in range(qb):
            q = q_ref[pl.ds(b * BLK, BLK), :]
            k2, v2 = kall[b * BLK:(b + 2) * BLK], vall[b * BLK:(b + 2) * BLK]
            later = jnp.minimum(j, 1) if b == 0 else 1
            for g in range(N_KV):
                kn, vh = k2[:, g * HD:(g + 1) * HD], v2[:, g * HD:(g + 1) * HD]
                s = _nt(_stack_heads(q, g), kn) * (HD ** -0.5) + tab_ref[later, g]
                e, es = _exp_scores(s, _sink_column(g, sk_ref))
                eb = _c(e)
                o = _nn(eb, vh) * (1.0 / (_nn(eb, ones) + es))
                for hh in range(GQA):
                    o_ref[pl.ds(b * BLK, BLK), pl.ds((g * GQA + hh) * HD, HD)] = o[hh * BLK:(hh + 1) * BLK]

    cur = lambda col: (lambda b, j: (b * NS + j, col))
    prev = lambda col: (lambda b, j: (qb * (b * NS + j) - jnp.minimum(j, 1), col))
    return _run("attn_fwd", body, (BL, NS), [qkv, qkv, qkv, qkv, qkv, sink_rows, _swa_bias_table()],
                [pl.BlockSpec((qb * BLK, ATT_W), cur(0)),
                 pl.BlockSpec((qb * BLK, KV_W), cur(4)), pl.BlockSpec((BLK, KV_W), prev(4)),
                 pl.BlockSpec((qb * BLK, KV_W), cur(5)), pl.BlockSpec((BLK, KV_W), prev(5)),
                 pl.BlockSpec((8, 128), lambda b, j: (0, 0)), VM],
                [SDS((T, ATT_W), f32)], [pl.BlockSpec((qb * BLK, ATT_W), cur(0))], exchange=exchange)


def _conv_taps(u, uh):
    row = lax.broadcasted_iota(jnp.int32, u.shape, 0)
    u1 = jnp.where(row == 0, uh[7:8, :], pltpu.roll(u, 1, 0))
    u2 = jnp.where(row == 0, uh[6:7, :], jnp.where(row == 1, uh[7:8, :], pltpu.roll(u, 2, 0)))
    return u1, u2


def _mem_head(qm, km, vm, h):
    qh, kh, vh = (a[:, h * HD:(h + 1) * HD] for a in (qm, km, vm))
    e, _ = _exp_scores(_nt(qh, kh) * (HD ** -0.5))
    return qh, kh, vh, e


def mixer_tail_fwd(x2d, attn_out, proj, qkv, km, vm, conv_w8, pk, wout, S, tm, exchange):
    T, D = x2d.shape
    NM = km.shape[0] // (T // S)

    def body(x_ref, ao_ref, ch_ref, cb_ref, cc_ref, chh_ref, cch_ref, qm_ref, km_ref, vm_ref, cw_ref, pk_ref,
             wout_ref, co_ref, mo_ref, mg_ref, x1_ref, h_ref):
        first = (pl.program_id(0) * tm) % S == 0
        u = cc_ref[...] * ch_ref[...]
        uh = jnp.where(first, 0.0, cch_ref[...] * chh_ref[...])
        u1, u2 = _conv_taps(u, uh)
        conv = cw_ref[0:1, :] * u2 + cw_ref[1:2, :] * u1 + cw_ref[2:3, :] * u + _small(pk_ref, "conv_b")
        conv_out = cb_ref[...] * conv
        co_ref[...] = conv_out
        qm, kmv, vmv = qm_ref[...], km_ref[...], vm_ref[...]
        ones = jnp.ones((NM, HD), MXU)
        for h in range(N_MEMH):
            _, _, vh, e = _mem_head(qm, kmv, vmv, h)
            eb = _c(e)
            mo_ref[:, pl.ds(h * HD, HD)] = _nn(eb, vh) * (1.0 / _nn(eb, ones))
        mem_out = mo_ref[...]
        ao = ao_ref[...]
        merged = _c(jnp.concatenate([ao * _rstd(ao) * _small(pk_ref, "out_norm_attn"),
                                     conv_out * _rstd(conv_out) * _small(pk_ref, "out_norm_conv"),
                                     mem_out * _rstd(mem_out) * _small(pk_ref, "out_norm_mem")], axis=1))
        mg_ref[...] = merged
        x1 = x_ref[...] + _nn(merged, wout_ref[...])
        x1_ref[...] = x1
        h_ref[...] = _c(x1 * _rstd(x1) * _small(pk_ref, "norm_ffn"))

    tile = lambda w, col: pl.BlockSpec((tm, w), lambda i: (i, col))
    halo = lambda col: pl.BlockSpec((8, CONV_W), lambda i: (jnp.maximum(i * (tm // 8) - 1, 0), col))
    seq = pl.BlockSpec((NM, MEM_W), lambda i: ((i * tm) // S, 0))
    small = lambda a: pl.BlockSpec(a.shape, lambda i: (0, 0))
    return _run("mixer_tail_fwd", body, (T // tm,),
                [x2d, attn_out, proj, proj, proj, proj, proj, qkv, km, vm, conv_w8, pk, wout],
                [tile(D, 0), tile(ATT_W, 0), tile(CONV_W, 3), tile(CONV_W, 4), tile(CONV_W, 5), halo(3), halo(5),
                 tile(MEM_W, 3), seq, seq, VM, VM, VM],
                [SDS((T, CONV_W), f32), SDS((T, MEM_W), f32), SDS((T, D), MXU), SDS((T, D), f32), SDS((T, D), MXU)],
                [tile(CONV_W, 0), tile(MEM_W, 0), tile(D, 0), tile(D, 0), tile(D, 0)], exchange=exchange)


def ffn_fwd_bwd(h, x1, tgt, wgT, wuT, wd, pk, tm):
    T, D = x1.shape
    F = wd.shape[0]

    def body(h_ref, x1_ref, t_ref, wg_ref, wu_ref, wd_ref, pk_ref,
             dx1_ref, dx2_ref, act_ref, dg_ref, du_ref, loss_ref, dgf_ref):
        @pl.when(pl.program_id(0) == 0)
        def _():
            loss_ref[...] = jnp.zeros_like(loss_ref)
            dgf_ref[...] = jnp.zeros_like(dgf_ref)

        hv = h_ref[...]
        gate = _nt(hv, wg_ref[...])
        up = _nt(hv, wu_ref[...])
        sg = jax.nn.sigmoid(gate)
        sl = gate * sg
        act = _c(sl * up)
        act_ref[...] = act
        x1v = x1_ref[...]
        diff = (x1v + _nn(act, wd_ref[...])) - t_ref[...]
        loss_ref[...] += 0.5 * jnp.sum(jnp.sum(diff * diff, axis=-1, keepdims=True) / D, axis=0, keepdims=True)
        dx2 = diff / D
        dx2b = _c(dx2)
        dx2_ref[...] = dx2b
        d_act = _nt(dx2b, wd_ref[...])
        d_up = _c(d_act * sl)
        d_gate = _c(d_act * up * (sg * (1.0 + gate * (1.0 - sg))))
        du_ref[...] = d_up
        dg_ref[...] = d_gate
        dh = _nn(d_gate, wg_ref[...]) + _nn(d_up, wu_ref[...])
        dv, dgf = _norm_bwd(dh, x1v, _rstd(x1v), _small(pk_ref, "norm_ffn"))
        dx1_ref[...] = dx2 + dv
        dgf_ref[...] += dgf

    tile = lambda w: pl.BlockSpec((tm, w), lambda i: (i, 0))
    return _run("ffn_fwd_bwd", body, (T // tm,), [h, x1, tgt, wgT, wuT, wd, pk],
                [tile(D), tile(D), tile(D), VM, VM, VM, VM],
                [SDS((T, D), f32), SDS((T, D), MXU), SDS((T, F), MXU), SDS((T, F), MXU), SDS((T, F), MXU),
                 SDS((8, 128), f32), SDS((1, D), f32)],
                [tile(D), tile(D), tile(F), tile(F), tile(F), pl.BlockSpec((8, 128), lambda i: (0, 0)),
                 pl.BlockSpec((1, D), lambda i: (0, 0))])


def matmul_tn(a, b, name, tmo, tk):
    T, M = a.shape
    N = b.shape[1]

    def body(a_ref, b_ref, o_ref):
        @pl.when(pl.program_id(1) == 0)
        def _():
            o_ref[...] = jnp.zeros_like(o_ref)

        o_ref[...] += _tn(a_ref[...], b_ref[...])

    return _run(name, body, (M // tmo, T // tk), [a, b],
                [pl.BlockSpec((tk, tmo), lambda m, k: (k, m)), pl.BlockSpec((tk, N), lambda m, k: (k, 0))],
                [SDS((M, N), f32)], [pl.BlockSpec((tmo, N), lambda m, k: (m, 0))])[0]


def out_proj_bwd(dx1, merged, attn_out, conv_out, mem_out, pk, wout, tm):
    T, D = dx1.shape

    def body(dx1_ref, mg_ref, ao_ref, co_ref, mo_ref, pk_ref, w_ref,
             dao_ref, dco_ref, dmo_ref, dw_ref, dgain_ref):
        @pl.when(pl.program_id(0) == 0)
        def _():
            dw_ref[...] = jnp.zeros_like(dw_ref)
            dgain_ref[...] = jnp.zeros_like(dgain_ref)

        dxb = _c(dx1_ref[...])
        dw_ref[...] += _tn(mg_ref[...], dxb)
        dmg = _nt(dxb, w_ref[...])
        ao, co, mo = ao_ref[...], co_ref[...], mo_ref[...]
        da, ga = _norm_bwd(dmg[:, :ATT_W], ao, _rstd(ao), _small(pk_ref, "out_norm_attn"))
        dc, gc = _norm_bwd(dmg[:, ATT_W:ATT_W + CONV_W], co, _rstd(co), _small(pk_ref, "out_norm_conv"))
        dm, gm = _norm_bwd(dmg[:, ATT_W + CONV_W:], mo, _rstd(mo), _small(pk_ref, "out_norm_mem"))
        dao_ref[...] = da
        dco_ref[...] = dc
        dmo_ref[...] = dm
        dgain_ref[...] += jnp.concatenate([ga, gc, gm], axis=1)

    tile = lambda w: pl.BlockSpec((tm, w), lambda i: (i, 0))
    return _run("out_proj_bwd", body, (T // tm,), [dx1, merged, attn_out, conv_out, mem_out, pk, wout],
                [tile(D), tile(D), tile(ATT_W), tile(CONV_W), tile(MEM_W), VM, VM],
                [SDS((T, ATT_W), f32), SDS((T, CONV_W), f32), SDS((T, MEM_W), f32), SDS((D, D), f32), SDS((1, D), f32)],
                [tile(ATT_W), tile(CONV_W), tile(MEM_W), pl.BlockSpec((D, D), lambda i: (0, 0)),
                 pl.BlockSpec((1, D), lambda i: (0, 0))])


def attn_bwd(qkv, d_attn, attn_out, sink_rows, BL, S, exchange):
    NB = S // BLK
    T = BL * S

    def body(q_ref, kc_ref, kp_ref, vc_ref, vp_ref, do_ref, ao_ref, sk_ref, tab_ref,
             dq_ref, dk_ref, dv_ref, dsk_ref, pend_k, pend_v):
        b, j = pl.program_id(0), pl.program_id(1)

        @pl.when((b == 0) & (j == 0))
        def _():
            dsk_ref[...] = jnp.zeros_like(dsk_ref)

        @pl.when(j == 0)
        def _():
            pend_k[...] = jnp.zeros_like(pend_k)
            pend_v[...] = jnp.zeros_like(pend_v)

        @pl.when(j < NB)
        def _():
            q, do, ao = q_ref[...], do_ref[...], ao_ref[...]
            k2 = jnp.concatenate([kp_ref[...], kc_ref[...]], axis=0)
            v2 = jnp.concatenate([vp_ref[...], vc_ref[...]], axis=0)
            lane = lax.broadcasted_iota(jnp.int32, (8, 128), 1)
            ones_w = jnp.ones((2 * BLK, 2 * BLK), MXU)
            dsk = jnp.zeros((8, 128), f32)
            dks, dvs = [], []
            for g in range(N_KV):
                kn, vh = k2[:, g * HD:(g + 1) * HD], v2[:, g * HD:(g + 1) * HD]
                qs = _stack_heads(q, g)
                s = _nt(qs, kn) * (HD ** -0.5) + tab_ref[g]
                e, es = _exp_scores(s, _sink_column(g, sk_ref))
                eb = _c(e)
                inv_w = 1.0 / (_nn(eb, ones_w) + es)
                inv_n = inv_w[:, :HD]
                dos = _stack_heads(do, g)
                delta = _rowsum_mxu(dos * _stack_heads(ao, g), 2 * BLK)
                dp = _nt(_c(dos), vh)
                ds = _c(e * inv_w * (dp - delta) * (HD ** -0.5))
                t = es * inv_n[:, 0:1] * delta[:, 0:1]
                for hh in range(GQA):
                    dsk = dsk + jnp.where(lane == g * GQA + hh, -jnp.sum(t[hh * BLK:(hh + 1) * BLK]), 0.0)
                dvs.append(_tn(eb, _c(dos * inv_n)))
                dks.append(_tn(ds, qs))
                dqs = _nn(ds, kn)
                for hh in range(GQA):
                    dq_ref[:, pl.ds((g * GQA + hh) * HD, HD)] = dqs[hh * BLK:(hh + 1) * BLK]
            dk2 = jnp.concatenate(dks, axis=1)
            dv2 = jnp.concatenate(dvs, axis=1)
            dk_ref[...] = pend_k[...] + dk2[:BLK]
            dv_ref[...] = pend_v[...] + dv2[:BLK]
            pend_k[...] = dk2[BLK:]
            pend_v[...] = dv2[BLK:]
            dsk_ref[...] += dsk

        @pl.when(j == NB)
        def _():
            dk_ref[...] = pend_k[...]
            dv_ref[...] = pend_v[...]

    cur = lambda col: (lambda b, j: (b * NB + jnp.minimum(j, NB - 1), col))
    prev = lambda col: (lambda b, j: (b * NB + jnp.maximum(j - 1, 0), col))
    small = lambda shape: pl.BlockSpec(shape, lambda b, j: (0, 0))
    return _run("attn_bwd", body, (BL, NB + 1), [qkv, qkv, qkv, qkv, qkv, d_attn, attn_out, sink_rows, _swa_bias_table()],
                [pl.BlockSpec((BLK, ATT_W), cur(0)),
                 pl.BlockSpec((BLK, KV_W), cur(4)), pl.BlockSpec((BLK, KV_W), prev(4)),
                 pl.BlockSpec((BLK, KV_W), cur(5)), pl.BlockSpec((BLK, KV_W), prev(5)),
                 pl.BlockSpec((BLK, ATT_W), cur(0)), pl.BlockSpec((BLK, ATT_W), cur(0)), small((8, 128)),
                 pl.BlockSpec((None, N_KV, GQA * BLK, 2 * BLK), lambda b, j: (jnp.minimum(j, 1), 0, 0, 0))],
                [SDS((T, ATT_W), f32), SDS((T, KV_W), f32), SDS((T, KV_W), f32), SDS((8, 128), f32)],
                [pl.BlockSpec((BLK, ATT_W), cur(0)), pl.BlockSpec((BLK, KV_W), prev(0)),
                 pl.BlockSpec((BLK, KV_W), prev(0)), small((8, 128))],
                scratch=[pltpu.VMEM((BLK, KV_W), f32)] * 2, exchange=exchange)


def mem_conv_bwd(d_mem_out, mem_out, d_conv_out, proj, qkv, km, vm, conv_w8, pk, S, tm, exchange):
    T = d_mem_out.shape[0]
    NM = km.shape[0] // (T // S)

    def body(dmo_ref, mo_ref, dco_ref, ch_ref, cb_ref, cc_ref, chh_ref, cch_ref, qm_ref, km_ref, vm_ref, cw_ref,
             pk_ref, dqm_ref, dkm_ref, dvm_ref, dcb_ref, dcv_ref, dcw_ref, dcbias_ref):
        i = pl.program_id(0)
        first = (i * tm) % S == 0

        @pl.when(i == 0)
        def _():
            dcw_ref[...] = jnp.zeros_like(dcw_ref)
            dcbias_ref[...] = jnp.zeros_like(dcbias_ref)

        @pl.when(first)
        def _():
            dkm_ref[...] = jnp.zeros_like(dkm_ref)
            dvm_ref[...] = jnp.zeros_like(dvm_ref)

        qm, kmv, vmv, dmo, mo = qm_ref[...], km_ref[...], vm_ref[...], dmo_ref[...], mo_ref[...]
        ones_w = jnp.ones((NM, NM), MXU)
        for h in range(N_MEMH):
            qh, kh, vh, e = _mem_head(qm, kmv, vmv, h)
            eb = _c(e)
            doh = dmo[:, h * HD:(h + 1) * HD]
            delta = _rowsum_mxu(doh * mo[:, h * HD:(h + 1) * HD], NM)
            dp = _nt(_c(doh), vh)
            inv_w = 1.0 / _nn(eb, ones_w)
            ds = _c(e * inv_w * (dp - delta) * (HD ** -0.5))
            dvm_ref[:, pl.ds(h * HD, HD)] += _tn(eb, _c(doh * inv_w[:, :HD]))
            dkm_ref[:, pl.ds(h * HD, HD)] += _tn(ds, qh)
            dqm_ref[:, pl.ds(h * HD, HD)] = _nn(ds, kh)

        u = cc_ref[...] * ch_ref[...]
        uh = jnp.where(first, 0.0, cch_ref[...] * chh_ref[...])
        u1, u2 = _conv_taps(u, uh)
        conv = cw_ref[0:1, :] * u2 + cw_ref[1:2, :] * u1 + cw_ref[2:3, :] * u + _small(pk_ref, "conv_b")
        dy = dco_ref[...]
        dcb_ref[...] = dy * conv
        dcv = dy * cb_ref[...]
        dcv_ref[...] = dcv
        dcbias_ref[...] += jnp.sum(dcv, axis=0, keepdims=True)
        dcw_ref[0:1, :] += jnp.sum(dcv * u2, axis=0, keepdims=True)
        dcw_ref[1:2, :] += jnp.sum(dcv * u1, axis=0, keepdims=True)
        dcw_ref[2:3, :] += jnp.sum(dcv * u, axis=0, keepdims=True)

    tile = lambda w, col: pl.BlockSpec((tm, w), lambda i: (i, col))
    halo = lambda col: pl.BlockSpec((8, CONV_W), lambda i: (jnp.maximum(i * (tm // 8) - 1, 0), col))
    seq = pl.BlockSpec((NM, MEM_W), lambda i: ((i * tm) // S, 0))
    const = lambda shape: pl.BlockSpec(shape, lambda i: (0, 0))
    return _run("mem_conv_bwd", body, (T // tm,),
                [d_mem_out, mem_out, d_conv_out, proj, proj, proj, proj, proj, qkv, km, vm, conv_w8, pk],
                [tile(MEM_W, 0), tile(MEM_W, 0), tile(CONV_W, 0), tile(CONV_W, 3), tile(CONV_W, 4), tile(CONV_W, 5),
                 halo(3), halo(5), tile(MEM_W, 3), seq, seq, VM, VM],
                [SDS((T, MEM_W), f32), SDS(km.shape, f32), SDS(km.shape, f32),
                 SDS((T, CONV_W), f32), SDS((T, CONV_W), f32), SDS((8, CONV_W), f32), SDS((1, CONV_W), f32)],
                [tile(MEM_W, 0), seq, seq, tile(CONV_W, 0), tile(CONV_W, 0), const((8, CONV_W)), const((1, CONV_W))],
                exchange=exchange)


def in_proj_bwd(dqn, dkn, dv, dcb, dcv, dqmn, proj, conv_w8, xn, x2d, dx1, pk, winT, S, tm, stages, ws, ms, vs):
    T, D = x2d.shape
    P = winT.shape[0]
    last_blk = T // 8 - 1
    n = len(stages)
    nsteps = T // tm
    tile_w = ws[0].shape[1] // (nsteps // 2)
    turn = [e * 2 // n for e in range(n)]

    def body(dq_ref, dk_ref, dv_ref, dcb_ref, dcv_ref, dcvn_ref, dqm_ref, qa_ref, ka_ref, ch_ref, cc_ref, qma_ref,
             cw_ref, xn_ref, x_ref, dx1_ref, pk_ref, w_ref, *rest):
        st, aw, am, av = (rest[k * n:(k + 1) * n] for k in range(4))
        dx_ref, dw_ref, dg_ref, dqg_ref, dkg_ref, dmqg_ref = rest[4 * n:4 * n + 6]
        aouts = rest[4 * n + 6:]
        i = pl.program_id(0)

        for parity in range(2):
            @pl.when(i % 2 == parity)
            def _(parity=parity):
                for e in range(n):
                    if turn[e] == parity:
                        g = jnp.concatenate([_sum_chips(st[e].at[0]), _sum_chips(st[e].at[1])], axis=0)
                        d, mm, vv = _adamw_math(aw[e][...], g, am[e][...], av[e][...])
                        for k, val in enumerate((g, d, mm, vv)):
                            aouts[4 * e + k][...] = val

        @pl.when(i == 0)
        def _():
            dw_ref[...] = jnp.zeros_like(dw_ref)
            dg_ref[...] = jnp.zeros_like(dg_ref)
            dqg_ref[...] = jnp.zeros_like(dqg_ref)
            dkg_ref[...] = jnp.zeros_like(dkg_ref)
            dmqg_ref[...] = jnp.zeros_like(dmqg_ref)

        dqa, gq = _heads_norm_bwd(dq_ref[...], qa_ref[...], _small(pk_ref, "q_norm"))
        dka, gk = _heads_norm_bwd(dk_ref[...], ka_ref[...], _small(pk_ref, "k_norm"))
        dqma, gmq = _heads_norm_bwd(dqm_ref[...], qma_ref[...], _small(pk_ref, "mem_q_norm"))
        dqg_ref[...] += gq
        dkg_ref[...] += gk
        dmqg_ref[...] += gmq

        last = ((i + 1) * tm) % S == 0
        dcv = dcv_ref[...]
        nxt = jnp.where(last, 0.0, dcvn_ref[...])
        row = lax.broadcasted_iota(jnp.int32, dcv.shape, 0)
        n1 = jnp.where(row == tm - 1, nxt[0:1, :], pltpu.roll(dcv, tm - 1, 0))
        n2 = jnp.where(row == tm - 2, nxt[0:1, :], jnp.where(row == tm - 1, nxt[1:2, :], pltpu.roll(dcv, tm - 2, 0)))
        du = cw_ref[2:3, :] * dcv + cw_ref[1:2, :] * n1 + cw_ref[0:1, :] * n2
        d_proj = jnp.concatenate([_c(dqa), _c(dka), _c(dv_ref[...]), _c(du * cc_ref[...]),
                                  _c(dcb_ref[...]), _c(du * ch_ref[...]), _c(dqma)], axis=1)
        dw_ref[...] += _tn(d_proj, xn_ref[...])
        xv = x_ref[...]
        dv_, dg = _norm_bwd(_nn(d_proj, w_ref[...]), xv, _rstd(xv), _small(pk_ref, "norm_mix"))
        dx_ref[...] = dx1_ref[...] + dv_
        dg_ref[...] += dg

    tile = lambda w, col=0: pl.BlockSpec((tm, w), lambda i: (i, col))
    nhalo = pl.BlockSpec((8, CONV_W), lambda i: (jnp.minimum((i + 1) * (tm // 8), last_blk), 0))
    const = lambda shape: pl.BlockSpec(shape, lambda i: (0, 0))
    st_specs = [pl.BlockSpec((2, 4, s.shape[2], tile_w), lambda i: (0, 0, 0, i // 2)) for s in stages]
    w_specs = [pl.BlockSpec((w.shape[0], tile_w), lambda i: (0, i // 2)) for w in ws]
    res = _run("in_proj_bwd", body, (nsteps,),
               [dqn, dkn, dv, dcb, dcv, dcv, dqmn, proj, proj, proj, proj, proj, conv_w8, xn, x2d, dx1, pk, winT]
               + list(stages) + list(ws) + list(ms) + list(vs),
               [tile(ATT_W), tile(KV_W), tile(KV_W), tile(CONV_W), tile(CONV_W), nhalo, tile(MEM_W),
                tile(ATT_W, 0), tile(KV_W, 4), tile(CONV_W, 3), tile(CONV_W, 5), tile(MEM_W, 6), VM,
                tile(D), tile(D), tile(D), VM, VM] + st_specs + w_specs * 3,
               [SDS((T, D), f32), SDS((P, D), f32), SDS((1, D), f32), SDS((1, HD), f32), SDS((1, HD), f32),
                SDS((1, HD), f32)] + [SDS(w.shape, f32) for w in ws for _ in range(4)],
               [tile(D), pl.BlockSpec((P, D), lambda i: (0, 0)), const((1, D)), const((1, HD)), const((1, HD)),
                const((1, HD))] + [s for s in w_specs for _ in range(4)])
    return res[:6], [res[6 + 4 * e:10 + 4 * e] for e in range(n)]


def mem_kv_bwd(dkm, dvm, kv, memn, mem2d, pk, wmkv):
    def body(dkm_ref, dvm_ref, kv_ref, mn_ref, m_ref, pk_ref, w_ref, dw_ref, dg_ref, dkg_ref):
        dkk, dkg = _heads_norm_bwd(dkm_ref[...], kv_ref[:, :MEM_W], _small(pk_ref, "mem_k_norm"))
        dkg_ref[...] = dkg
        dkv = _c(jnp.concatenate([dkk, dvm_ref[...]], axis=1))
        dw_ref[...] = _tn(mn_ref[...], dkv)
        mv = m_ref[...]
        dg_ref[...] = jnp.sum(_nt(dkv, w_ref[...]) * mv * _rstd(mv), axis=0, keepdims=True)

    return _run("mem_kv_bwd", body, (), [dkm, dvm, kv, memn, mem2d, pk, wmkv], [VM] * 7,
                [SDS(wmkv.shape, f32), SDS((1, mem2d.shape[1]), f32), SDS((1, HD), f32)], [VM] * 3)


def _halves_view(g):
    return g.reshape(4, 2, g.shape[0] // 8, g.shape[1])


def kernel(x, mem, norm_mix, w_in, q_norm, k_norm, attn_sinks, conv_w, conv_b, norm_mem, w_mem_kv, mem_q_norm, mem_k_norm, out_norm_attn, out_norm_conv, out_norm_mem, w_out, norm_ffn, w_gate, w_up, w_down, loss_target, m_norm_mix, m_w_in, m_q_norm, m_k_norm, m_attn_sinks, m_conv_w, m_conv_b, m_norm_mem, m_w_mem_kv, m_mem_q_norm, m_mem_k_norm, m_out_norm_attn, m_out_norm_conv, m_out_norm_mem, m_w_out, m_norm_ffn, m_w_gate, m_w_up, m_w_down, v_norm_mix, v_w_in, v_q_norm, v_k_norm, v_attn_sinks, v_conv_w, v_conv_b, v_norm_mem, v_w_mem_kv, v_mem_q_norm, v_mem_k_norm, v_out_norm_attn, v_out_norm_conv, v_out_norm_mem, v_w_out, v_norm_ffn, v_w_gate, v_w_up, v_w_down):
    BL, S, D = x.shape
    T = BL * S
    TM = 256
    TM_BIG = min(512, S)
    w_small = dict(norm_mix=norm_mix, norm_mem=norm_mem, norm_ffn=norm_ffn, out_norm_attn=out_norm_attn,
                   out_norm_conv=out_norm_conv, out_norm_mem=out_norm_mem, conv_w=conv_w, conv_b=conv_b, q_norm=q_norm,
                   k_norm=k_norm, mem_q_norm=mem_q_norm, mem_k_norm=mem_k_norm, attn_sinks=attn_sinks)
    m_small = dict(norm_mix=m_norm_mix, norm_mem=m_norm_mem, norm_ffn=m_norm_ffn, out_norm_attn=m_out_norm_attn,
                   out_norm_conv=m_out_norm_conv, out_norm_mem=m_out_norm_mem, conv_w=m_conv_w, conv_b=m_conv_b,
                   q_norm=m_q_norm, k_norm=m_k_norm, mem_q_norm=m_mem_q_norm, mem_k_norm=m_mem_k_norm,
                   attn_sinks=m_attn_sinks)
    v_small = dict(norm_mix=v_norm_mix, norm_mem=v_norm_mem, norm_ffn=v_norm_ffn, out_norm_attn=v_out_norm_attn,
                   out_norm_conv=v_out_norm_conv, out_norm_mem=v_out_norm_mem, conv_w=v_conv_w, conv_b=v_conv_b,
                   q_norm=v_q_norm, k_norm=v_k_norm, mem_q_norm=v_mem_q_norm, mem_k_norm=v_mem_k_norm,
                   attn_sinks=v_attn_sinks)
    pk = _pack_small(w_small)

    rowblocks = lambda a, b, c, d, e, f: [a[0].T, b[0].T, c[0].T, d[0], e[0], f[0]]
    w_rb = rowblocks(w_in, w_gate, w_up, w_down, w_out, w_mem_kv)
    m_rb = rowblocks(m_w_in, m_w_gate, m_w_up, m_w_down, m_w_out, m_w_mem_kv)
    v_rb = rowblocks(v_w_in, v_w_gate, v_w_up, v_w_down, v_w_out, v_w_mem_kv)
    (winT_s,) = prep_weights("prep_w_in", w_rb[:1])
    cw_pad = jnp.zeros((8, 128), f32).at[:3, :HD].set(conv_w[0])
    (wgT_s, wuT_s, wd_s, wout_s, wmkv_s), (winT, cw_all) = prep_weights(
        "gather_w_in", w_rb[1:], _together([gather_two_legs([winT_s]), gather_exchange([cw_pad], [False])]))
    conv_w_full = jnp.transpose(cw_all.reshape(4, 8, 128)[:, :3, :HD], (1, 0, 2)).reshape(3, CONV_W)
    conv_w8 = jnp.zeros((8, CONV_W), f32).at[:3].set(conv_w_full)
    sink_rows = jnp.broadcast_to(attn_sinks.reshape(N_Q, 1), (N_Q, 128))

    x2d = x.reshape(T, D)
    mem2d = mem.reshape(-1, D)
    (xn, proj, qkv), near1 = in_proj_fwd(x2d, pk, winT, TM_BIG, gather_near_exchange([wgT_s, wout_s, wmkv_s], relay_early=1))
    (attn_out,), (wgT, wout, wmkv, *near2) = attn_fwd(
        qkv, sink_rows, BL, S, _together([gather_far_exchange(near1, relay_early=2), gather_near_exchange([wuT_s, wd_s], relay_early=2)]))
    memn, kv, km, vm = mem_kv_fwd(mem2d, pk, wmkv)
    (conv_out, mem_out, merged, x1, h), (wuT, wd) = mixer_tail_fwd(
        x2d, attn_out, proj, qkv, km, vm, conv_w8, pk, wout, S, TM_BIG, gather_far_exchange(near2, relay_early=2))

    dx1, dx2b, act, d_gate, d_up, loss8, d_norm_ffn = ffn_fwd_bwd(h, x1, loss_target.reshape(T, D), wgT, wuT, wd, pk, TM)
    F = wd.shape[0]
    g_wd = matmul_tn(act, dx2b, "dw_down", F // 2, min(T, 1024))
    g_wgT = matmul_tn(d_gate, h, "dw_gate", F // 2, min(T, 1024))
    g_wuT = matmul_tn(d_up, h, "dw_up", F // 2, min(T, 1024))

    d_attn, d_conv_out, d_mem_out, g_wout, d_gains = out_proj_bwd(dx1, merged, attn_out, conv_out, mem_out, pk, wout, TM_BIG)
    dqmn, dkm, dvm, dcb, dcv, d_cw8, d_cbias = mem_conv_bwd(
        d_mem_out, mem_out, d_conv_out, proj, qkv, km, vm, conv_w8, pk, S, min(1024, S), None)
    (dqn, dkn, dv, d_sink8), (st_wout, st_wgT, st_wuT, st_wd) = attn_bwd(
        qkv, d_attn, attn_out, sink_rows, BL, S,
        reduce_scatter_exchange([_halves_view(g) for g in (g_wout, g_wgT, g_wuT, g_wd)], BL * (S // BLK + 1),
                                load_step=[0, 1, 4, 7], send_step=[1, 4, 7, 10], relay_step=[6, 16, 25, 33]))
    (g_x, g_winT, d_norm_mix, d_qg, d_kg, d_mqg), late_res = in_proj_bwd(
        dqn, dkn, dv, dcb, dcv, dqmn, proj, conv_w8, xn, x2d, dx1, pk, winT, S, TM,
        [st_wgT, st_wuT, st_wd, st_wout], w_rb[1:5], m_rb[1:5], v_rb[1:5])
    g_wmkv, d_norm_mem, d_mkg = mem_kv_bwd(dkm, dvm, kv, memn, mem2d, pk, wmkv)

    tot, tail_stage = tail_reduce(d_norm_mix, d_norm_mem, d_norm_ffn, d_gains, d_cw8, d_cbias, d_qg, d_kg, d_mqg, d_mkg,
                                  d_sink8, loss8, [_halves_view(g) for g in (g_winT, g_wmkv)])
    loss = tot[5, 384]
    tail_res, _ = adamw_big("adamw_tail", tail_stage, [w_rb[0], w_rb[5]], [m_rb[0], m_rb[5]], [v_rb[0], v_rb[5]], 4)
    res = {"w_in": [a.T[None] for a in tail_res[0]], "w_gate": [a.T[None] for a in late_res[0]],
           "w_up": [a.T[None] for a in late_res[1]], "w_down": [a[None] for a in late_res[2]],
           "w_out": [a[None] for a in late_res[3]], "w_mem_kv": [a[None] for a in tail_res[1]]}
    res.update(adamw_small(tot, pk, _pack_small(m_small), _pack_small(v_small), {k: w_small[k].shape for k in SMALL}))

    order = ["norm_mix", "w_in", "q_norm", "k_norm", "attn_sinks", "conv_w", "conv_b", "norm_mem", "w_mem_kv",
             "mem_q_norm", "mem_k_norm", "out_norm_attn", "out_norm_conv", "out_norm_mem", "w_out", "norm_ffn",
             "w_gate", "w_up", "w_down"]
    return (loss, g_x.reshape(BL, S, D), *[res[n][0] for n in order], *[res[n][1] for n in order],
            *[res[n][2] for n in order], *[res[n][3] for n in order])
```

```python
import collections
import functools

import jax
import jax.numpy as jnp
import numpy as np
from jax import lax
from jax.experimental import pallas as pl
from jax.experimental.pallas import tpu as pltpu

f32 = jnp.float32
MXU = jnp.bfloat16
WIRE = jnp.bfloat16
EPS = 1e-6
NEG = -1e30
HD = 64
BLK = 128
N_Q, N_KV, N_MEMH = 8, 2, 4
GQA = N_Q // N_KV
ATT_W, KV_W, CONV_W, MEM_W = 512, 128, 256, 256
VMEM_MIB = 1024 * 1024
ADAM_LR, ADAM_B1, ADAM_B2, ADAM_EPS, ADAM_WD, ADAM_STEP = 0.001, 0.9, 0.999, 1e-08, 0.01, 10

MESH = pl.DeviceIdType.MESH
VM = pl.BlockSpec(memory_space=pltpu.VMEM)
ANY = pl.BlockSpec(memory_space=pl.ANY)
SDS = jax.ShapeDtypeStruct
DMA = pltpu.SemaphoreType.DMA


def _c(v):
    return v.astype(MXU)


def _nn(a, b):
    return lax.dot_general(a, b, (((1,), (0,)), ((), ())), preferred_element_type=f32)


def _nt(a, b):
    return lax.dot_general(a, b, (((1,), (1,)), ((), ())), preferred_element_type=f32)


def _tn(a, b):
    return lax.dot_general(a, b, (((0,), (0,)), ((), ())), preferred_element_type=f32)


def _rstd(v):
    return lax.rsqrt(jnp.mean(v * v, axis=-1, keepdims=True) + EPS)


def _norm_bwd(dy, v, r, g):
    dyg = dy * g
    dv = r * dyg - v * (r * r * r) * jnp.mean(dyg * v, axis=-1, keepdims=True)
    return dv, jnp.sum(dy * v * r, axis=0, keepdims=True)


def _split3(v):
    hi = _c(v)
    r1 = v - hi.astype(f32)
    mid = _c(r1)
    return hi, mid, _c(r1 - mid.astype(f32))


def _rowsum_mxu(v, width):
    ones = jnp.ones((v.shape[1], width), MXU)
    return sum(_nn(a, ones) for a in _split3(v))


def _seg_sums(v):
    r = lax.broadcasted_iota(jnp.int32, (2 * HD, 2 * HD), 0) // HD
    c = lax.broadcasted_iota(jnp.int32, (2 * HD, 2 * HD), 1) // HD
    bd = (r == c).astype(MXU)
    outs = []
    for b in range(v.shape[1] // (2 * HD)):
        outs.append(sum(_nn(a, bd) for a in _split3(v[:, b * 2 * HD:(b + 1) * 2 * HD])))
    return outs[0] if len(outs) == 1 else jnp.concatenate(outs, axis=1)


def _lanes(g, width):
    return jnp.concatenate([g] * (width // HD), axis=1)


def _heads_rstd(v):
    return lax.rsqrt(_seg_sums(v * v) * (1.0 / HD) + EPS)


def _heads_norm_bwd(dy, v, g):
    r = _heads_rstd(v)
    gl = _lanes(g, v.shape[1])
    dyg = dy * gl
    dv = r * dyg - v * (r * r * r) * (_seg_sums(dyg * v) * (1.0 / HD))
    dgl = jnp.sum(dy * v * r, axis=0, keepdims=True)
    return dv, sum(dgl[:, s * HD:(s + 1) * HD] for s in range(v.shape[1] // HD))


def _exp_scores(s, extra=None):
    m = jnp.max(s, axis=-1, keepdims=True)
    if extra is None:
        return jnp.exp(s - m), None
    m = jnp.maximum(m, extra)
    return jnp.exp(s - m), jnp.exp(extra - m)


def _place():
    return lax.axis_index("x"), lax.axis_index("y"), lax.axis_index("c")


SMALL_AT = {"norm_mix": (0, 0, 1024), "norm_mem": (1, 0, 1024), "norm_ffn": (2, 0, 1024),
            "out_norm_attn": (3, 0, ATT_W), "out_norm_conv": (3, ATT_W, CONV_W), "out_norm_mem": (3, ATT_W + CONV_W, MEM_W),
            "conv_b": (4, 3 * CONV_W, CONV_W), "q_norm": (5, 0, HD), "k_norm": (5, HD, HD), "mem_q_norm": (5, 2 * HD, HD),
            "mem_k_norm": (5, 3 * HD, HD), "attn_sinks": (5, 256, N_Q)}
SMALL = ("norm_mix", "norm_mem", "norm_ffn", "out_norm_attn", "out_norm_conv", "out_norm_mem", "conv_w", "conv_b",
         "q_norm", "k_norm", "mem_q_norm", "mem_k_norm", "attn_sinks")


def _small(pk_ref, name):
    r, c0, w = SMALL_AT[name]
    return pk_ref[r:r + 1, c0:c0 + w]


def _pack_small(d):
    z = lambda n: jnp.zeros((1, n), f32)
    row3 = jnp.concatenate([d["out_norm_attn"], d["out_norm_conv"], d["out_norm_mem"]], axis=1)
    row4 = jnp.concatenate([d["conv_w"].reshape(1, 3 * HD), z(3 * CONV_W - 3 * HD), d["conv_b"]], axis=1)
    row5 = jnp.concatenate([d["q_norm"], d["k_norm"], d["mem_q_norm"], d["mem_k_norm"], d["attn_sinks"],
                            z(1024 - 4 * HD - N_Q)], axis=1)
    return jnp.concatenate([d["norm_mix"], d["norm_mem"], d["norm_ffn"], row3, row4, row5, z(1024), z(1024)], axis=0)


def _other_chips(x, y):
    return [(1 - x, y), (x, 1 - y), (1 - x, 1 - y)]


Exchange = collections.namedtuple("Exchange", "ins outs sems start finish relays aliases", defaults=((), {}))


def _together(exchanges):
    def bounds(key):
        at, out = 0, []
        for ex in exchanges:
            out.append((at, at + len(getattr(ex, key))))
            at += len(getattr(ex, key))
        return out

    bi, bo, bs = bounds("ins"), bounds("outs"), bounds("sems")

    def of(i, fn):
        return lambda xa, xo, xs: fn(xa[bi[i][0]:bi[i][1]], xo[bo[i][0]:bo[i][1]], xs[bs[i][0]:bs[i][1]])

    def every(name):
        fns = [of(i, getattr(ex, name)) for i, ex in enumerate(exchanges)]

        def run(xa, xo, xs):
            for fn in fns:
                fn(xa, xo, xs)
        return run

    aliases = {}
    for i, ex in enumerate(exchanges):
        aliases.update({bi[i][0] + a: bo[i][0] + o for a, o in ex.aliases.items()})
    return Exchange([a for ex in exchanges for a in ex.ins], [o for ex in exchanges for o in ex.outs],
                    [s for ex in exchanges for s in ex.sems], every("start"), every("finish"),
                    [(sbe, of(i, fn)) for i, ex in enumerate(exchanges) for sbe, fn in ex.relays], aliases)


def _run(name, body, grid, ins, in_specs, out_shape, out_specs, scratch=(), vmem_mib=8, exchange=None):
    ins, in_specs, out_shape, out_specs, scratch = list(ins), list(in_specs), list(out_shape), list(out_specs), list(scratch)
    ni, no, ns = len(ins), len(out_shape), len(scratch)
    ex = exchange
    if ex is not None:
        nxi, nxo = len(ex.ins), len(ex.outs)

    def call_body(*refs):
        if ex is None:
            body(*refs)
            return
        a, xa = refs[:ni], refs[ni:ni + nxi]
        o, xo = refs[ni + nxi:ni + nxi + no], refs[ni + nxi + no:ni + nxi + no + nxo]
        s, xs = refs[ni + nxi + no + nxo:ni + nxi + no + nxo + ns], refs[ni + nxi + no + nxo + ns:]
        if grid:
            first = functools.reduce(jnp.logical_and, [pl.program_id(d) == 0 for d in range(len(grid))])
            last = functools.reduce(jnp.logical_and, [pl.program_id(d) == grid[d] - 1 for d in range(len(grid))])
            pl.when(first)(lambda: ex.start(xa, xo, xs))
            body(*a, *o, *s)
            nsteps = functools.reduce(lambda p, q: p * q, grid)
            for before_end, fn in ex.relays:
                at = np.unravel_index(max(nsteps - 1 - before_end, 0), grid)
                here = functools.reduce(jnp.logical_and, [pl.program_id(d) == int(at[d]) for d in range(len(grid))])
                pl.when(here)(functools.partial(fn, xa, xo, xs))
            pl.when(last)(lambda: ex.finish(xa, xo, xs))
        else:
            ex.start(xa, xo, xs)
            if body is not None:
                body(*a, *o, *s)
            for _, fn in ex.relays:
                fn(xa, xo, xs)
            ex.finish(xa, xo, xs)

    kw = dict(grid=grid) if grid else {}
    if ex is not None:
        if ex.aliases:
            kw["input_output_aliases"] = {ni + i: no + o for i, o in ex.aliases.items()}
        ins, in_specs = ins + list(ex.ins), in_specs + [ANY] * nxi
        out_shape, out_specs = out_shape + list(ex.outs), out_specs + [ANY] * nxo
        scratch = scratch + list(ex.sems)
    res = pl.pallas_call(
        call_body, name=name, out_shape=out_shape, in_specs=in_specs, out_specs=out_specs, scratch_shapes=scratch,
        compiler_params=pltpu.CompilerParams(dimension_semantics=("arbitrary",) * len(grid) if grid else None,
                                             vmem_limit_bytes=vmem_mib * VMEM_MIB), **kw)(*ins)
    res = list(res)
    return (res[:no], res[no:]) if ex is not None else res


def _remote(src, dst, ssem, rsem, dev):
    return pltpu.make_async_remote_copy(src_ref=src, dst_ref=dst, send_sem=ssem, recv_sem=rsem,
                                        device_id=dev, device_id_type=MESH)


def gather_exchange(shards, split, relay_early=0):
    n = len(shards)

    def rows(ref, e, kk, half=None):
        R = shards[e].shape[0]
        if half is None:
            return ref.at[pl.ds(pl.multiple_of(kk * R, 8), R)]
        return ref.at[pl.ds(pl.multiple_of(kk * R + half * (R // 2), 8), R // 2)]

    def ici(src, dst, sm, e, j, chip_j, x, y, c):
        k = 2 * x + y
        if split[e]:
            s = src[e].at[pl.ds(pl.multiple_of(c * (shards[e].shape[0] // 2), 8), shards[e].shape[0] // 2)]
            return _remote(s, rows(dst[e], e, k, c), sm[0].at[6 * e + j], sm[1].at[6 * e + j], (*chip_j, c))
        return _remote(src[e], rows(dst[e], e, k), sm[0].at[6 * e + j], sm[1].at[6 * e + j], (*chip_j, c))

    def landed(dst, e, chip_j, c):
        kj = 2 * chip_j[0] + chip_j[1]
        return rows(dst[e], e, kj, c) if split[e] else rows(dst[e], e, kj)

    def forward(dst, sm, e, j, chip_j, x, y, c, sender_c):
        kj = 2 * chip_j[0] + chip_j[1]
        r = rows(dst[e], e, kj, sender_c)
        return _remote(r, r, sm[0].at[6 * e + 3 + j], sm[1].at[6 * e + 3 + j], (x, y, 1 - c))

    def local(src, dst, sm, e, x, y):
        return pltpu.make_async_copy(src[e], rows(dst[e], e, 2 * x + y), sm[2].at[e])

    def start(src, dst, sm):
        x, y, c = _place()
        for e in range(n):
            local(src, dst, sm, e, x, y).start()
            for j, chip_j in enumerate(_other_chips(x, y)):
                ici(src, dst, sm, e, j, chip_j, x, y, c).start()

    def relay(src, dst, sm):
        x, y, c = _place()
        for e in range(n):
            for j, chip_j in enumerate(_other_chips(x, y)):
                r = landed(dst, e, chip_j, c)
                _remote(r, r, sm[0].at[6 * e + j], sm[1].at[6 * e + j], (*chip_j, c)).wait_recv()
                if split[e]:
                    forward(dst, sm, e, j, chip_j, x, y, c, c).start()

    def finish(src, dst, sm):
        x, y, c = _place()
        chips = _other_chips(x, y)
        for e in range(n):
            for j, chip_j in enumerate(chips):
                if split[e]:
                    forward(dst, sm, e, j, chip_j, x, y, c, 1 - c).wait_recv()
        for e in range(n):
            for j, chip_j in enumerate(chips):
                ici(src, dst, sm, e, j, chip_j, x, y, c).wait_send()
                if split[e]:
                    forward(dst, sm, e, j, chip_j, x, y, c, c).wait_send()
            local(src, dst, sm, e, x, y).wait()

    outs = [SDS((4 * s.shape[0], s.shape[1]), s.dtype) for s in shards]
    return Exchange(list(shards), outs, [DMA((6 * n,)), DMA((6 * n,)), DMA((n,))], start, finish, [(relay_early, relay)])


def _block_rows(ref, R, kk, half, quarter=None):
    hr = R // 2
    if quarter is None:
        return ref.at[pl.ds(pl.multiple_of(kk * R + half * hr, 8), hr)]
    return ref.at[pl.ds(pl.multiple_of(kk * R + half * hr + quarter * (hr // 2), 8), hr // 2)]


def gather_near_exchange(shards, relay_early=0):
    n = len(shards)
    R = [s.shape[0] for s in shards]

    def ici(src, dst, sm, e, j, chip_j, x, y, c):
        half = src[e].at[pl.ds(pl.multiple_of(c * (R[e] // 2), 8), R[e] // 2)]
        return _remote(half, _block_rows(dst[e], R[e], 2 * x + y, c), sm[0].at[4 * e + j], sm[1].at[4 * e + j], (*chip_j, c))

    def forward(dst, sm, e, j, chip_j, x, y, c, sender_c):
        r = _block_rows(dst[e], R[e], 2 * chip_j[0] + chip_j[1], sender_c)
        return _remote(r, r, sm[0].at[4 * e + 2 + j], sm[1].at[4 * e + 2 + j], (x, y, 1 - c))

    def local(src, dst, sm, e, x, y):
        return pltpu.make_async_copy(src[e], dst[e].at[pl.ds(pl.multiple_of((2 * x + y) * R[e], 8), R[e])], sm[2].at[e])

    def start(src, dst, sm):
        x, y, c = _place()
        for e in range(n):
            local(src, dst, sm, e, x, y).start()
            for j, chip_j in enumerate(_other_chips(x, y)[:2]):
                ici(src, dst, sm, e, j, chip_j, x, y, c).start()

    def relay(src, dst, sm):
        x, y, c = _place()
        for e in range(n):
            for j, chip_j in enumerate(_other_chips(x, y)[:2]):
                r = _block_rows(dst[e], R[e], 2 * chip_j[0] + chip_j[1], c)
                _remote(r, r, sm[0].at[4 * e + j], sm[1].at[4 * e + j], (*chip_j, c)).wait_recv()
                forward(dst, sm, e, j, chip_j, x, y, c, c).start()

    def finish(src, dst, sm):
        x, y, c = _place()
        near = _other_chips(x, y)[:2]
        for e in range(n):
            for j, chip_j in enumerate(near):
                forward(dst, sm, e, j, chip_j, x, y, c, 1 - c).wait_recv()
        for e in range(n):
            for j, chip_j in enumerate(near):
                ici(src, dst, sm, e, j, chip_j, x, y, c).wait_send()
                forward(dst, sm, e, j, chip_j, x, y, c, c).wait_send()
            local(src, dst, sm, e, x, y).wait()

    outs = [SDS((4 * s.shape[0], s.shape[1]), s.dtype) for s in shards]
    return Exchange(list(shards), outs, [DMA((4 * n,)), DMA((4 * n,)), DMA((n,))], start, finish, [(relay_early, relay)])


def gather_far_exchange(bufs, relay_early=0):
    n = len(bufs)
    R = [b.shape[0] // 4 for b in bufs]

    def send(src, dst, sm, e, j, x, y, c):
        to, of = _other_chips(x, y)[j], _other_chips(x, y)[1 - j]
        kk = 2 * of[0] + of[1]
        return _remote(_block_rows(src[e], R[e], kk, c, j), _block_rows(dst[e], R[e], kk, c, j),
                       sm[0].at[4 * e + j], sm[1].at[4 * e + j], (*to, c))

    def landed(dst, e, j, x, y, half):
        return _block_rows(dst[e], R[e], 2 * (1 - x) + (1 - y), half, j)

    def forward(dst, sm, e, j, x, y, c, sender_c):
        r = landed(dst, e, j, x, y, sender_c)
        return _remote(r, r, sm[0].at[4 * e + 2 + j], sm[1].at[4 * e + 2 + j], (x, y, 1 - c))

    def start(src, dst, sm):
        x, y, c = _place()
        for e in range(n):
            for j in range(2):
                send(src, dst, sm, e, j, x, y, c).start()

    def relay(src, dst, sm):
        x, y, c = _place()
        for e in range(n):
            for j in range(2):
                r = landed(dst, e, j, x, y, c)
                _remote(r, r, sm[0].at[4 * e + j], sm[1].at[4 * e + j], (*_other_chips(x, y)[j], c)).wait_recv()
                forward(dst, sm, e, j, x, y, c, c).start()

    def finish(src, dst, sm):
        x, y, c = _place()
        for e in range(n):
            for j in range(2):
                forward(dst, sm, e, j, x, y, c, 1 - c).wait_recv()
        for e in range(n):
            for j in range(2):
                send(src, dst, sm, e, j, x, y, c).wait_send()
                forward(dst, sm, e, j, x, y, c, c).wait_send()

    outs = [SDS(b.shape, b.dtype) for b in bufs]
    return Exchange(list(bufs), outs, [DMA((4 * n,)), DMA((4 * n,))], start, finish, [(relay_early, relay)],
                    {i: i for i in range(n)})


def gather_two_legs(shards):
    near = gather_near_exchange(shards)
    far = gather_far_exchange(near.outs)

    def finish(src, dst, sm):
        near.relays[0][1](src, dst, sm[:3])
        near.finish(src, dst, sm[:3])
        far.start(dst, dst, sm[3:])
        far.relays[0][1](dst, dst, sm[3:])
        far.finish(dst, dst, sm[3:])

    return Exchange(near.ins, near.outs, list(near.sems) + list(far.sems),
                    lambda src, dst, sm: near.start(src, dst, sm[:3]), finish)


def scatter_exchange(parts, relay_before_end=None, want_issue=False):
    n = len(parts)
    by_entry = relay_before_end is not None
    relay_before_end = relay_before_end or [0] * n

    def ici(p, st, sm, e, j, chip_j, x, y, c):
        k, kj = 2 * x + y, 2 * chip_j[0] + chip_j[1]
        return _remote(p[e].at[kj], st[e].at[c, k], sm[0].at[8 * e + j], sm[1].at[8 * e + j], (*chip_j, c))

    def own(p, st, sm, e, x, y, c):
        k = 2 * x + y
        return _remote(p[e].at[k], st[e].at[c, k], sm[0].at[8 * e + 3], sm[1].at[8 * e + 3], (x, y, 1 - c))

    def forward(st, sm, e, j, chip_j, x, y, c, sender_c):
        kj = 2 * chip_j[0] + chip_j[1]
        r = st[e].at[sender_c, kj]
        return _remote(r, r, sm[0].at[8 * e + 4 + j], sm[1].at[8 * e + 4 + j], (x, y, 1 - c))

    def local(p, st, sm, e, x, y, c):
        k = 2 * x + y
        return pltpu.make_async_copy(p[e].at[k], st[e].at[c, k], sm[2].at[e])

    def issue(e, p, st, sm):
        x, y, c = _place()
        for j, chip_j in enumerate(_other_chips(x, y)):
            ici(p, st, sm, e, j, chip_j, x, y, c).start()
        local(p, st, sm, e, x, y, c).start()
        own(p, st, sm, e, x, y, c).start()

    def start(p, st, sm, before_slot=None):
        x, y, c = _place()
        if by_entry:
            for e in range(n):
                issue(e, p, st, sm)
            return
        for j, chip_j in enumerate(_other_chips(x, y)):
            if before_slot is not None:
                before_slot(j, 2 * chip_j[0] + chip_j[1])
            for e in range(n):
                ici(p, st, sm, e, j, chip_j, x, y, c).start()
        if before_slot is not None:
            before_slot(3, 2 * x + y)
        for e in range(n):
            local(p, st, sm, e, x, y, c).start()
            own(p, st, sm, e, x, y, c).start()

    def relay(e, p, st, sm):
        x, y, c = _place()
        for j, chip_j in enumerate(_other_chips(x, y)):
            kj = 2 * chip_j[0] + chip_j[1]
            r = st[e].at[c, kj]
            _remote(r, r, sm[0].at[8 * e + j], sm[1].at[8 * e + j], (*chip_j, c)).wait_recv()
            forward(st, sm, e, j, chip_j, x, y, c, c).start()

    def finish(p, st, sm):
        x, y, c = _place()
        k = 2 * x + y
        chips = _other_chips(x, y)
        for e in range(n):
            r = st[e].at[1 - c, k]
            _remote(r, r, sm[0].at[8 * e + 3], sm[1].at[8 * e + 3], (x, y, 1 - c)).wait_recv()
            for j, chip_j in enumerate(chips):
                forward(st, sm, e, j, chip_j, x, y, c, 1 - c).wait_recv()
        for e in range(n):
            own(p, st, sm, e, x, y, c).wait_send()
            for j, chip_j in enumerate(chips):
                ici(p, st, sm, e, j, chip_j, x, y, c).wait_send()
                forward(st, sm, e, j, chip_j, x, y, c, c).wait_send()
            local(p, st, sm, e, x, y, c).wait()

    outs = [SDS((2,) + a.shape, a.dtype) for a in parts]
    ex = Exchange(list(parts), outs, [DMA((8 * n,)), DMA((8 * n,)), DMA((n,))], start, finish,
                  [(relay_before_end[e], functools.partial(relay, e)) for e in range(n)])
    return (ex, issue) if want_issue else ex


def reduce_scatter_exchange(grads, nsteps, load_step, send_step, relay_step):
    n = len(grads)
    hrs = [g.shape[2] for g in grads]
    C = grads[0].shape[3]
    scatter, issue = scatter_exchange([SDS((4,) + g.shape[2:], WIRE) for g in grads], want_issue=True)
    hand_on = [fn for _, fn in scatter.relays]

    def refs(xs):
        return xs[:3], xs[3], xs[4], xs[5], xs[6], xs[7:7 + n], xs[7 + n:]

    def push(e, g, psem, qsem, sib_st):
        x, y, c = _place()
        return _remote(g[e].at[:, 1 - c], sib_st[e], psem.at[e], qsem.at[e], (x, y, 1 - c))

    def fetch(e, g, lsem, own_st):
        _, _, c = _place()
        return pltpu.make_async_copy(g[e].at[:, c], own_st.at[e % 2, :, pl.ds(0, hrs[e])], lsem.at[e])

    def start(g, xo, xs):
        _, _, psem, qsem, _, sib_st, _ = refs(xs)
        for e in range(n):
            push(e, g, psem, qsem, sib_st).start()

    def load(e, g, xo, xs):
        _, lsem, _, _, own_st, _, _ = refs(xs)
        fetch(e, g, lsem, own_st).start()

    def send(e, g, xo, xs):
        sm, lsem, psem, qsem, own_st, sib_st, part = refs(xs)
        fetch(e, g, lsem, own_st).wait()
        push(e, g, psem, qsem, sib_st).wait_recv()
        part[e][...] = (own_st[e % 2, :, 0:hrs[e]] + sib_st[e][...]).astype(WIRE)
        issue(e, part, xo, sm)

    def relay(e, g, xo, xs):
        sm, _, _, _, _, _, part = refs(xs)
        hand_on[e](part, xo, sm)

    def finish(g, xo, xs):
        sm, _, psem, qsem, _, sib_st, part = refs(xs)
        scatter.finish(part, xo, sm)
        for e in range(n):
            push(e, g, psem, qsem, sib_st).wait_send()

    plan = sorted([(min(step[e], nsteps - 1), phase, e) for phase, step in enumerate((load_step, send_step, relay_step))
                   for e in range(n)])
    stage = (load, send, relay)
    relays = [(nsteps - 1 - at, functools.partial(stage[phase], e)) for at, phase, e in plan]
    scratch = (list(scatter.sems) + [DMA((n,)), DMA((n,)), DMA((n,))] + [pltpu.VMEM((2, 4, max(hrs), C), f32)]
               + [pltpu.VMEM((4, hr, C), f32) for hr in hrs] + [pltpu.VMEM((4, hr, C), WIRE) for hr in hrs])
    return Exchange(list(grads), scatter.outs, scratch, start, finish, relays)


def tail_reduce(d_norm_mix, d_norm_mem, d_norm_ffn, d_gains, d_cw8, d_cbias, d_qg, d_kg, d_mqg, d_mkg, d_sink8, loss8, tail):
    n = len(tail)
    scatter = scatter_exchange([SDS((4,) + a.shape[2:], WIRE) for a in tail])

    def half_copy(g, sib, hsem, e, j, slot, x, y, c):
        return _remote(g[e].at[slot, 1 - c], sib[e].at[slot], hsem[0].at[4 * e + j], hsem[1].at[4 * e + j], (x, y, 1 - c))

    def body(nm_ref, nmem_ref, nf_ref, gn_ref, cw_ref, cb_ref, qg_ref, kg_ref, mqg_ref, mkg_ref, sk_ref, ls_ref, *rest):
        g, o_ref, st = rest[:n], rest[n], rest[n + 1:2 * n + 1]
        buf, ssem, rsem = rest[2 * n + 1:2 * n + 4]
        own, sib, part = (rest[2 * n + 4 + i * n:2 * n + 4 + (i + 1) * n] for i in range(3))
        lsem = rest[5 * n + 4]
        hsem, xsem = rest[5 * n + 5:5 * n + 7], rest[5 * n + 7:]
        x, y, c = _place()
        loads = [pltpu.make_async_copy(g[e].at[:, c], own[e], lsem.at[e]) for e in range(n)]
        for ld in loads:
            ld.start()
        for j, slot in enumerate([2 * cx + cy for cx, cy in _other_chips(x, y)] + [2 * x + y]):
            for e in range(n):
                half_copy(g, sib, hsem, e, j, slot, x, y, c).start()
        me = 4 * x + 2 * y + c
        mine = buf.at[me]
        mine[...] = jnp.zeros((8, 1024), f32)
        mine[0:1, :] = nm_ref[...]
        mine[1:2, :] = nmem_ref[...]
        mine[2:3, :] = nf_ref[...]
        mine[3:4, :] = gn_ref[...]
        for j in range(3):
            mine[4:5, pl.ds(j * CONV_W, CONV_W)] = cw_ref[j:j + 1, :]
        mine[4:5, pl.ds(3 * CONV_W, CONV_W)] = cb_ref[...]
        for j, r in enumerate((qg_ref, kg_ref, mqg_ref, mkg_ref)):
            mine[5:6, pl.ds(j * HD, HD)] = r[...]
        mine[5:6, pl.ds(256, 128)] = sk_ref[0:1, :]
        mine[5:6, pl.ds(384, 128)] = ls_ref[0:1, :]

        def peer_of(m):
            return (1 - x if m & 4 else x, 1 - y if m & 2 else y, 1 - c if m & 1 else c)

        for m in range(1, 8):
            _remote(mine, mine, ssem.at[m - 1], rsem.at[m - 1], peer_of(m)).start()
        for ld in loads:
            ld.wait()

        def chip_partial(j, slot):
            for e in range(n):
                half_copy(g, sib, hsem, e, j, slot, x, y, c).wait()
                part[e][slot] = (own[e][slot] + sib[e][slot]).astype(WIRE)

        scatter.start(part, st, xsem, chip_partial)
        for _, hand_on in scatter.relays:
            hand_on(part, st, xsem)
        scatter.finish(part, st, xsem)
        for m in range(1, 8):
            p = peer_of(m)
            got = buf.at[4 * p[0] + 2 * p[1] + p[2]]
            _remote(got, got, ssem.at[m - 1], rsem.at[m - 1], p).wait_recv()
        for m in range(1, 8):
            _remote(mine, mine, ssem.at[m - 1], rsem.at[m - 1], peer_of(m)).wait_send()
        acc = buf[0]
        for d in range(1, 8):
            acc = acc + buf[d]
        o_ref[...] = acc

    ins = [d_norm_mix, d_norm_mem, d_norm_ffn, d_gains, d_cw8, d_cbias, d_qg, d_kg, d_mqg, d_mkg, d_sink8, loss8]
    half_shape = [(4,) + a.shape[2:] for a in tail]
    scratch = ([pltpu.VMEM((8, 8, 1024), f32), DMA((7,)), DMA((7,))]
               + [pltpu.VMEM(s, f32) for s in half_shape] * 2 + [pltpu.VMEM(s, WIRE) for s in half_shape]
               + [DMA((n,)), DMA((4 * n,)), DMA((4 * n,))] + list(scatter.sems))
    res = _run("tail_reduce", body, (), ins + list(tail), [VM] * len(ins) + [ANY] * n,
               [SDS((8, 1024), f32)] + list(scatter.outs), [VM] + [ANY] * n, scratch=scratch, vmem_mib=16)
    return res[0], res[1:]


def _adamw_math(w, g, m, v):
    m = ADAM_B1 * m + (1.0 - ADAM_B1) * g
    v = ADAM_B2 * v + (1.0 - ADAM_B2) * (g * g)
    m_hat = m / (1.0 - ADAM_B1 ** ADAM_STEP)
    v_hat = v / (1.0 - ADAM_B2 ** ADAM_STEP)
    delta = -ADAM_LR * (m_hat / (jnp.sqrt(v_hat) + ADAM_EPS) + ADAM_WD * w)
    return delta, m, v


def _sum_chips(st):
    return ((st[0].astype(f32) + st[1].astype(f32)) + st[2].astype(f32)) + st[3].astype(f32)


def adamw_big(name, stages, ws, ms, vs, nstep, exchange=None):
    n = len(stages)

    def body(*refs):
        st, w, m, v = refs[:n], refs[n:2 * n], refs[2 * n:3 * n], refs[3 * n:4 * n]
        outs = refs[4 * n:]
        for e in range(n):
            g = jnp.concatenate([_sum_chips(st[e].at[0]), _sum_chips(st[e].at[1])], axis=0)
            d, mm, vv = _adamw_math(w[e][...], g, m[e][...], v[e][...])
            outs[4 * e][...] = g
            outs[4 * e + 1][...] = d
            outs[4 * e + 2][...] = mm
            outs[4 * e + 3][...] = vv

    st_specs, w_specs = [], []
    for e in range(n):
        _, _, hr, C = stages[e].shape
        st_specs.append(pl.BlockSpec((2, 4, hr, C // nstep), lambda i: (0, 0, 0, i)))
        w_specs.append(pl.BlockSpec((2 * hr, C // nstep), lambda i: (0, i)))
    out_specs = [s for s in w_specs for _ in range(4)]
    out_shape = [SDS(w.shape, f32) for w in ws for _ in range(4)]
    res = _run(name, body, (nstep,), list(stages) + list(ws) + list(ms) + list(vs), st_specs + w_specs * 3,
               out_shape, out_specs, vmem_mib=16, exchange=exchange)
    res, sent = res if exchange is not None else (res, None)
    return [res[4 * e:4 * e + 4] for e in range(n)], sent


def adamw_small(tot, pk_w, pk_m, pk_v, shapes):
    def body(tot_ref, w_ref, m_ref, v_ref, *outs):
        x, y, _ = _place()
        chip = 2 * x + y
        taps = []
        for j in range(3):
            mine = tot_ref[4:5, j * CONV_W:j * CONV_W + HD]
            for s in range(1, 4):
                mine = jnp.where(chip == s, tot_ref[4:5, j * CONV_W + s * HD:j * CONV_W + (s + 1) * HD], mine)
            taps.append(mine)
        row4 = jnp.concatenate(taps + [jnp.zeros((1, 3 * CONV_W - 3 * HD), f32), tot_ref[4:5, 3 * CONV_W:]], axis=1)
        tot_v = tot_ref[...]
        row = lax.broadcasted_iota(jnp.int32, tot_v.shape, 0)
        g = jnp.where(row == 4, jnp.broadcast_to(row4, tot_v.shape), tot_v)
        d, mm, vv = _adamw_math(w_ref[...], g, m_ref[...], v_ref[...])
        for i, name in enumerate(SMALL):
            for k, val in enumerate((g, d, mm, vv)):
                if name == "conv_w":
                    outs[4 * i + k][...] = jnp.concatenate([val[4:5, j * HD:(j + 1) * HD] for j in range(3)], axis=0)[None]
                else:
                    r, c0, w = SMALL_AT[name]
                    outs[4 * i + k][...] = val[r:r + 1, c0:c0 + w]

    out_shape = [SDS(shapes[k], f32) for k in SMALL for _ in range(4)]
    res = _run("adamw_small", body, (), [tot, pk_w, pk_m, pk_v], [VM] * 4, out_shape, [VM] * len(out_shape))
    return {k: res[4 * i:4 * i + 4] for i, k in enumerate(SMALL)}


def prep_weights(name, shards, exchange=None):
    n = len(shards)

    def body(*refs):
        for e in range(n):
            refs[n + e][...] = _c(refs[e][...])

    return _run(name, body, (), shards, [VM] * n, [SDS(a.shape, MXU) for a in shards], [VM] * n, vmem_mib=16, exchange=exchange)


def mem_kv_fwd(mem2d, pk, wmkv):
    M, D = mem2d.shape

    def body(m_ref, pk_ref, w_ref, mn_ref, kv_ref, km_ref, vm_ref):
        m = m_ref[...]
        mn = _c(m * _rstd(m) * _small(pk_ref, "norm_mem"))
        mn_ref[...] = mn
        kv = _nn(mn, w_ref[...])
        kv_ref[...] = kv
        kk = kv[:, :MEM_W]
        km_ref[...] = _c(kk * _heads_rstd(kk) * _lanes(_small(pk_ref, "mem_k_norm"), MEM_W))
        vm_ref[...] = _c(kv[:, MEM_W:])

    return _run("mem_kv_fwd", body, (), [mem2d, pk, wmkv], [VM] * 3,
                [SDS((M, D), MXU), SDS((M, 2 * MEM_W), f32), SDS((M, MEM_W), MXU), SDS((M, MEM_W), MXU)], [VM] * 4)


QKV_W = ATT_W + 2 * KV_W + MEM_W


def in_proj_fwd(x2d, pk, winT, tm, exchange):
    T, D = x2d.shape
    P = winT.shape[0]

    def body(x_ref, pk_ref, w_ref, xn_ref, proj_ref, qkv_ref):
        xv = x_ref[...]
        xn = _c(xv * _rstd(xv) * _small(pk_ref, "norm_mix"))
        xn_ref[...] = xn
        proj = _nt(xn, w_ref[...])
        proj_ref[...] = proj
        q, k = proj[:, :ATT_W], proj[:, ATT_W:ATT_W + KV_W]
        qm = proj[:, P - MEM_W:]
        qkv_ref[...] = jnp.concatenate(
            [_c(q * _heads_rstd(q) * _lanes(_small(pk_ref, "q_norm"), ATT_W)),
             _c(k * _heads_rstd(k) * _lanes(_small(pk_ref, "k_norm"), KV_W)),
             _c(proj[:, ATT_W + KV_W:ATT_W + 2 * KV_W]),
             _c(qm * _heads_rstd(qm) * _lanes(_small(pk_ref, "mem_q_norm"), MEM_W))], axis=1)

    return _run("in_proj_fwd", body, (T // tm,), [x2d, pk, winT],
                [pl.BlockSpec((tm, D), lambda i: (i, 0)), VM, VM],
                [SDS((T, D), MXU), SDS((T, P), f32), SDS((T, QKV_W), MXU)],
                [pl.BlockSpec((tm, D), lambda i: (i, 0)), pl.BlockSpec((tm, P), lambda i: (i, 0)),
                 pl.BlockSpec((tm, QKV_W), lambda i: (i, 0))],
                vmem_mib=24, exchange=exchange)


def _swa_bias_table():
    r = np.arange(GQA * BLK)[:, None]
    k = np.arange(2 * BLK)[None, :]
    dist = (r % BLK) + BLK - k
    band = (dist >= 0) & (dist < BLK)
    tab = np.empty((2, N_KV, GQA * BLK, 2 * BLK), np.float32)
    for later in range(2):
        valid = band & ((k >= BLK) | (later == 1))
        for g in range(N_KV):
            slope = 2.0 ** -(g * GQA + r // BLK + 1.0)
            tab[later, g] = np.where(valid, -slope * dist, NEG)
    return jnp.asarray(tab)


def _sink_column(g, sk_ref):
    hrow = lax.broadcasted_iota(jnp.int32, (GQA * BLK, 1), 0) // BLK
    sink = jnp.zeros((GQA * BLK, 1), f32)
    for hh in range(GQA):
        sink = jnp.where(hrow == hh, sk_ref[g * GQA + hh:g * GQA + hh + 1, 0:1], sink)
    return sink


def _stack_heads(v, g):
    return jnp.concatenate([v[:, (g * GQA + hh) * HD:(g * GQA + hh + 1) * HD] for hh in range(GQA)], axis=0)


def attn_fwd(qkv, sink_rows, BL, S, exchange, qb=2):
    NS = S // (qb * BLK)
    T = BL * S

    def body(q_ref, kc_ref, kp_ref, vc_ref, vp_ref, sk_ref, tab_ref, o_ref):
        j = pl.program_id(1)
        kall = jnp.concatenate([kp_ref[...], kc_ref[...]], axis=0)
        vall = jnp.concatenate([vp_ref[...], vc_ref[...]], axis=0)
        ones = jnp.ones((2 * BLK, HD), MXU)
        for b in range(qb):
            q = q_ref[pl.ds(b * BLK, BLK), :]
            k2, v2 = kall[b * BLK:(b + 2) * BLK], vall[b * BLK:(b + 2) * BLK]
            later = jnp.minimum(j, 1) if b == 0 else 1
            for g in range(N_KV):
                kn, vh = k2[:, g * HD:(g + 1) * HD], v2[:, g * HD:(g + 1) * HD]
                s = _nt(_stack_heads(q, g), kn) * (HD ** -0.5) + tab_ref[later, g]
                e, es = _exp_scores(s, _sink_column(g, sk_ref))
                eb = _c(e)
                o = _nn(eb, vh) * (1.0 / (_nn(eb, ones) + es))
                for hh in range(GQA):
                    o_ref[pl.ds(b * BLK, BLK), pl.ds((g * GQA + hh) * HD, HD)] = o[hh * BLK:(hh + 1) * BLK]

    cur = lambda col: (lambda b, j: (b * NS + j, col))
    prev = lambda col: (lambda b, j: (qb * (b * NS + j) - jnp.minimum(j, 1), col))
    return _run("attn_fwd", body, (BL, NS), [qkv, qkv, qkv, qkv, qkv, sink_rows, _swa_bias_table()],
                [pl.BlockSpec((qb * BLK, ATT_W), cur(0)),
                 pl.BlockSpec((qb * BLK, KV_W), cur(4)), pl.BlockSpec((BLK, KV_W), prev(4)),
                 pl.BlockSpec((qb * BLK, KV_W), cur(5)), pl.BlockSpec((BLK, KV_W), prev(5)),
                 pl.BlockSpec((8, 128), lambda b, j: (0, 0)), VM],
                [SDS((T, ATT_W), f32)], [pl.BlockSpec((qb * BLK, ATT_W), cur(0))], exchange=exchange)


def _conv_taps(u, uh):
    row = lax.broadcasted_iota(jnp.int32, u.shape, 0)
    u1 = jnp.where(row == 0, uh[7:8, :], pltpu.roll(u, 1, 0))
    u2 = jnp.where(row == 0, uh[6:7, :], jnp.where(row == 1, uh[7:8, :], pltpu.roll(u, 2, 0)))
    return u1, u2


def _mem_head(qm, km, vm, h):
    qh, kh, vh = (a[:, h * HD:(h + 1) * HD] for a in (qm, km, vm))
    e, _ = _exp_scores(_nt(qh, kh) * (HD ** -0.5))
    return qh, kh, vh, e


def mixer_tail_fwd(x2d, attn_out, proj, qkv, km, vm, conv_w8, pk, wout, S, tm, exchange):
    T, D = x2d.shape
    NM = km.shape[0] // (T // S)

    def body(x_ref, ao_ref, ch_ref, cb_ref, cc_ref, chh_ref, cch_ref, qm_ref, km_ref, vm_ref, cw_ref, pk_ref,
             wout_ref, co_ref, mo_ref, mg_ref, x1_ref, h_ref):
        first = (pl.program_id(0) * tm) % S == 0
        u = cc_ref[...] * ch_ref[...]
        uh = jnp.where(first, 0.0, cch_ref[...] * chh_ref[...])
        u1, u2 = _conv_taps(u, uh)
        conv = cw_ref[0:1, :] * u2 + cw_ref[1:2, :] * u1 + cw_ref[2:3, :] * u + _small(pk_ref, "conv_b")
        conv_out = cb_ref[...] * conv
        co_ref[...] = conv_out
        qm, kmv, vmv = qm_ref[...], km_ref[...], vm_ref[...]
        ones = jnp.ones((NM, HD), MXU)
        for h in range(N_MEMH):
            _, _, vh, e = _mem_head(qm, kmv, vmv, h)
            eb = _c(e)
            mo_ref[:, pl.ds(h * HD, HD)] = _nn(eb, vh) * (1.0 / _nn(eb, ones))
        mem_out = mo_ref[...]
        ao = ao_ref[...]
        merged = _c(jnp.concatenate([ao * _rstd(ao) * _small(pk_ref, "out_norm_attn"),
                                     conv_out * _rstd(conv_out) * _small(pk_ref, "out_norm_conv"),
                                     mem_out * _rstd(mem_out) * _small(pk_ref, "out_norm_mem")], axis=1))
        mg_ref[...] = merged
        x1 = x_ref[...] + _nn(merged, wout_ref[...])
        x1_ref[...] = x1
        h_ref[...] = _c(x1 * _rstd(x1) * _small(pk_ref, "norm_ffn"))

    tile = lambda w, col: pl.BlockSpec((tm, w), lambda i: (i, col))
    halo = lambda col: pl.BlockSpec((8, CONV_W), lambda i: (jnp.maximum(i * (tm // 8) - 1, 0), col))
    seq = pl.BlockSpec((NM, MEM_W), lambda i: ((i * tm) // S, 0))
    small = lambda a: pl.BlockSpec(a.shape, lambda i: (0, 0))
    return _run("mixer_tail_fwd", body, (T // tm,),
                [x2d, attn_out, proj, proj, proj, proj, proj, qkv, km, vm, conv_w8, pk, wout],
                [tile(D, 0), tile(ATT_W, 0), tile(CONV_W, 3), tile(CONV_W, 4), tile(CONV_W, 5), halo(3), halo(5),
                 tile(MEM_W, 3), seq, seq, VM, VM, VM],
                [SDS((T, CONV_W), f32), SDS((T, MEM_W), f32), SDS((T, D), MXU), SDS((T, D), f32), SDS((T, D), MXU)],
                [tile(CONV_W, 0), tile(MEM_W, 0), tile(D, 0), tile(D, 0), tile(D, 0)], vmem_mib=24, exchange=exchange)


def ffn_fwd_bwd(h, x1, tgt, wgT, wuT, wd, pk, tm):
    T, D = x1.shape
    F = wd.shape[0]

    def body(h_ref, x1_ref, t_ref, wg_ref, wu_ref, wd_ref, pk_ref,
             dx1_ref, dx2_ref, act_ref, dg_ref, du_ref, loss_ref, dgf_ref):
        @pl.when(pl.program_id(0) == 0)
        def _():
            loss_ref[...] = jnp.zeros_like(loss_ref)
            dgf_ref[...] = jnp.zeros_like(dgf_ref)

        hv = h_ref[...]
        gate = _nt(hv, wg_ref[...])
        up = _nt(hv, wu_ref[...])
        sg = jax.nn.sigmoid(gate)
        sl = gate * sg
        act = _c(sl * up)
        act_ref[...] = act
        x1v = x1_ref[...]
        diff = (x1v + _nn(act, wd_ref[...])) - t_ref[...]
        loss_ref[...] += 0.5 * jnp.sum(jnp.sum(diff * diff, axis=-1, keepdims=True) / D, axis=0, keepdims=True)
        dx2 = diff / D
        dx2b = _c(dx2)
        dx2_ref[...] = dx2b
        d_act = _nt(dx2b, wd_ref[...])
        d_up = _c(d_act * sl)
        d_gate = _c(d_act * up * (sg * (1.0 + gate * (1.0 - sg))))
        du_ref[...] = d_up
        dg_ref[...] = d_gate
        dh = _nn(d_gate, wg_ref[...]) + _nn(d_up, wu_ref[...])
        dv, dgf = _norm_bwd(dh, x1v, _rstd(x1v), _small(pk_ref, "norm_ffn"))
        dx1_ref[...] = dx2 + dv
        dgf_ref[...] += dgf

    tile = lambda w: pl.BlockSpec((tm, w), lambda i: (i, 0))
    return _run("ffn_fwd_bwd", body, (T // tm,), [h, x1, tgt, wgT, wuT, wd, pk],
                [tile(D), tile(D), tile(D), VM, VM, VM, VM],
                [SDS((T, D), f32), SDS((T, D), MXU), SDS((T, F), MXU), SDS((T, F), MXU), SDS((T, F), MXU),
                 SDS((8, 128), f32), SDS((1, D), f32)],
                [tile(D), tile(D), tile(F), tile(F), tile(F), pl.BlockSpec((8, 128), lambda i: (0, 0)),
                 pl.BlockSpec((1, D), lambda i: (0, 0))], vmem_mib=56)


def matmul_tn(a, b, name, tmo, tk):
    T, M = a.shape
    N = b.shape[1]

    def body(a_ref, b_ref, o_ref):
        @pl.when(pl.program_id(1) == 0)
        def _():
            o_ref[...] = jnp.zeros_like(o_ref)

        o_ref[...] += _tn(a_ref[...], b_ref[...])

    return _run(name, body, (M // tmo, T // tk), [a, b],
                [pl.BlockSpec((tk, tmo), lambda m, k: (k, m)), pl.BlockSpec((tk, N), lambda m, k: (k, 0))],
                [SDS((M, N), f32)], [pl.BlockSpec((tmo, N), lambda m, k: (m, 0))], vmem_mib=30)[0]


def out_proj_bwd(dx1, merged, attn_out, conv_out, mem_out, pk, wout, tm):
    T, D = dx1.shape

    def body(dx1_ref, mg_ref, ao_ref, co_ref, mo_ref, pk_ref, w_ref,
             dao_ref, dco_ref, dmo_ref, dw_ref, dgain_ref):
        @pl.when(pl.program_id(0) == 0)
        def _():
            dw_ref[...] = jnp.zeros_like(dw_ref)
            dgain_ref[...] = jnp.zeros_like(dgain_ref)

        dxb = _c(dx1_ref[...])
        dw_ref[...] += _tn(mg_ref[...], dxb)
        dmg = _nt(dxb, w_ref[...])
        ao, co, mo = ao_ref[...], co_ref[...], mo_ref[...]
        da, ga = _norm_bwd(dmg[:, :ATT_W], ao, _rstd(ao), _small(pk_ref, "out_norm_attn"))
        dc, gc = _norm_bwd(dmg[:, ATT_W:ATT_W + CONV_W], co, _rstd(co), _small(pk_ref, "out_norm_conv"))
        dm, gm = _norm_bwd(dmg[:, ATT_W + CONV_W:], mo, _rstd(mo), _small(pk_ref, "out_norm_mem"))
        dao_ref[...] = da
        dco_ref[...] = dc
        dmo_ref[...] = dm
        dgain_ref[...] += jnp.concatenate([ga, gc, gm], axis=1)

    tile = lambda w: pl.BlockSpec((tm, w), lambda i: (i, 0))
    return _run("out_proj_bwd", body, (T // tm,), [dx1, merged, attn_out, conv_out, mem_out, pk, wout],
                [tile(D), tile(D), tile(ATT_W), tile(CONV_W), tile(MEM_W), VM, VM],
                [SDS((T, ATT_W), f32), SDS((T, CONV_W), f32), SDS((T, MEM_W), f32), SDS((D, D), f32), SDS((1, D), f32)],
                [tile(ATT_W), tile(CONV_W), tile(MEM_W), pl.BlockSpec((D, D), lambda i: (0, 0)),
                 pl.BlockSpec((1, D), lambda i: (0, 0))], vmem_mib=24)


def attn_bwd(qkv, d_attn, attn_out, sink_rows, BL, S, exchange):
    NB = S // BLK
    T = BL * S

    def body(q_ref, kc_ref, kp_ref, vc_ref, vp_ref, do_ref, ao_ref, sk_ref, tab_ref,
             dq_ref, dk_ref, dv_ref, dsk_ref, pend_k, pend_v):
        b, j = pl.program_id(0), pl.program_id(1)

        @pl.when((b == 0) & (j == 0))
        def _():
            dsk_ref[...] = jnp.zeros_like(dsk_ref)

        @pl.when(j == 0)
        def _():
            pend_k[...] = jnp.zeros_like(pend_k)
            pend_v[...] = jnp.zeros_like(pend_v)

        @pl.when(j < NB)
        def _():
            q, do, ao = q_ref[...], do_ref[...], ao_ref[...]
            k2 = jnp.concatenate([kp_ref[...], kc_ref[...]], axis=0)
            v2 = jnp.concatenate([vp_ref[...], vc_ref[...]], axis=0)
            lane = lax.broadcasted_iota(jnp.int32, (8, 128), 1)
            ones_w = jnp.ones((2 * BLK, 2 * BLK), MXU)
            dsk = jnp.zeros((8, 128), f32)
            dks, dvs = [], []
            for g in range(N_KV):
                kn, vh = k2[:, g * HD:(g + 1) * HD], v2[:, g * HD:(g + 1) * HD]
                qs = _stack_heads(q, g)
                s = _nt(qs, kn) * (HD ** -0.5) + tab_ref[g]
                e, es = _exp_scores(s, _sink_column(g, sk_ref))
                eb = _c(e)
                inv_w = 1.0 / (_nn(eb, ones_w) + es)
                inv_n = inv_w[:, :HD]
                dos = _stack_heads(do, g)
                delta = _rowsum_mxu(dos * _stack_heads(ao, g), 2 * BLK)
                dp = _nt(_c(dos), vh)
                ds = _c(e * inv_w * (dp - delta) * (HD ** -0.5))
                t = es * inv_n[:, 0:1] * delta[:, 0:1]
                for hh in range(GQA):
                    dsk = dsk + jnp.where(lane == g * GQA + hh, -jnp.sum(t[hh * BLK:(hh + 1) * BLK]), 0.0)
                dvs.append(_tn(eb, _c(dos * inv_n)))
                dks.append(_tn(ds, qs))
                dqs = _nn(ds, kn)
                for hh in range(GQA):
                    dq_ref[:, pl.ds((g * GQA + hh) * HD, HD)] = dqs[hh * BLK:(hh + 1) * BLK]
            dk2 = jnp.concatenate(dks, axis=1)
            dv2 = jnp.concatenate(dvs, axis=1)
            dk_ref[...] = pend_k[...] + dk2[:BLK]
            dv_ref[...] = pend_v[...] + dv2[:BLK]
            pend_k[...] = dk2[BLK:]
            pend_v[...] = dv2[BLK:]
            dsk_ref[...] += dsk

        @pl.when(j == NB)
        def _():
            dk_ref[...] = pend_k[...]
            dv_ref[...] = pend_v[...]

    cur = lambda col: (lambda b, j: (b * NB + jnp.minimum(j, NB - 1), col))
    prev = lambda col: (lambda b, j: (b * NB + jnp.maximum(j - 1, 0), col))
    small = lambda shape: pl.BlockSpec(shape, lambda b, j: (0, 0))
    return _run("attn_bwd", body, (BL, NB + 1), [qkv, qkv, qkv, qkv, qkv, d_attn, attn_out, sink_rows, _swa_bias_table()],
                [pl.BlockSpec((BLK, ATT_W), cur(0)),
                 pl.BlockSpec((BLK, KV_W), cur(4)), pl.BlockSpec((BLK, KV_W), prev(4)),
                 pl.BlockSpec((BLK, KV_W), cur(5)), pl.BlockSpec((BLK, KV_W), prev(5)),
                 pl.BlockSpec((BLK, ATT_W), cur(0)), pl.BlockSpec((BLK, ATT_W), cur(0)), small((8, 128)),
                 pl.BlockSpec((None, N_KV, GQA * BLK, 2 * BLK), lambda b, j: (jnp.minimum(j, 1), 0, 0, 0))],
                [SDS((T, ATT_W), f32), SDS((T, KV_W), f32), SDS((T, KV_W), f32), SDS((8, 128), f32)],
                [pl.BlockSpec((BLK, ATT_W), cur(0)), pl.BlockSpec((BLK, KV_W), prev(0)),
                 pl.BlockSpec((BLK, KV_W), prev(0)), small((8, 128))],
                scratch=[pltpu.VMEM((BLK, KV_W), f32)] * 2, vmem_mib=56, exchange=exchange)


def mem_conv_bwd(d_mem_out, mem_out, d_conv_out, proj, qkv, km, vm, conv_w8, pk, S, tm, exchange):
    T = d_mem_out.shape[0]
    NM = km.shape[0] // (T // S)

    def body(dmo_ref, mo_ref, dco_ref, ch_ref, cb_ref, cc_ref, chh_ref, cch_ref, qm_ref, km_ref, vm_ref, cw_ref,
             pk_ref, dqm_ref, dkm_ref, dvm_ref, dcb_ref, dcv_ref, dcw_ref, dcbias_ref):
        i = pl.program_id(0)
        first = (i * tm) % S == 0

        @pl.when(i == 0)
        def _():
            dcw_ref[...] = jnp.zeros_like(dcw_ref)
            dcbias_ref[...] = jnp.zeros_like(dcbias_ref)

        @pl.when(first)
        def _():
            dkm_ref[...] = jnp.zeros_like(dkm_ref)
            dvm_ref[...] = jnp.zeros_like(dvm_ref)

        qm, kmv, vmv, dmo, mo = qm_ref[...], km_ref[...], vm_ref[...], dmo_ref[...], mo_ref[...]
        ones_w = jnp.ones((NM, NM), MXU)
        for h in range(N_MEMH):
            qh, kh, vh, e = _mem_head(qm, kmv, vmv, h)
            eb = _c(e)
            doh = dmo[:, h * HD:(h + 1) * HD]
            delta = _rowsum_mxu(doh * mo[:, h * HD:(h + 1) * HD], NM)
            dp = _nt(_c(doh), vh)
            inv_w = 1.0 / _nn(eb, ones_w)
            ds = _c(e * inv_w * (dp - delta) * (HD ** -0.5))
            dvm_ref[:, pl.ds(h * HD, HD)] += _tn(eb, _c(doh * inv_w[:, :HD]))
            dkm_ref[:, pl.ds(h * HD, HD)] += _tn(ds, qh)
            dqm_ref[:, pl.ds(h * HD, HD)] = _nn(ds, kh)

        u = cc_ref[...] * ch_ref[...]
        uh = jnp.where(first, 0.0, cch_ref[...] * chh_ref[...])
        u1, u2 = _conv_taps(u, uh)
        conv = cw_ref[0:1, :] * u2 + cw_ref[1:2, :] * u1 + cw_ref[2:3, :] * u + _small(pk_ref, "conv_b")
        dy = dco_ref[...]
        dcb_ref[...] = dy * conv
        dcv = dy * cb_ref[...]
        dcv_ref[...] = dcv
        dcbias_ref[...] += jnp.sum(dcv, axis=0, keepdims=True)
        dcw_ref[0:1, :] += jnp.sum(dcv * u2, axis=0, keepdims=True)
        dcw_ref[1:2, :] += jnp.sum(dcv * u1, axis=0, keepdims=True)
        dcw_ref[2:3, :] += jnp.sum(dcv * u, axis=0, keepdims=True)

    tile = lambda w, col: pl.BlockSpec((tm, w), lambda i: (i, col))
    halo = lambda col: pl.BlockSpec((8, CONV_W), lambda i: (jnp.maximum(i * (tm // 8) - 1, 0), col))
    seq = pl.BlockSpec((NM, MEM_W), lambda i: ((i * tm) // S, 0))
    const = lambda shape: pl.BlockSpec(shape, lambda i: (0, 0))
    return _run("mem_conv_bwd", body, (T // tm,),
                [d_mem_out, mem_out, d_conv_out, proj, proj, proj, proj, proj, qkv, km, vm, conv_w8, pk],
                [tile(MEM_W, 0), tile(MEM_W, 0), tile(CONV_W, 0), tile(CONV_W, 3), tile(CONV_W, 4), tile(CONV_W, 5),
                 halo(3), halo(5), tile(MEM_W, 3), seq, seq, VM, VM],
                [SDS((T, MEM_W), f32), SDS(km.shape, f32), SDS(km.shape, f32),
                 SDS((T, CONV_W), f32), SDS((T, CONV_W), f32), SDS((8, CONV_W), f32), SDS((1, CONV_W), f32)],
                [tile(MEM_W, 0), seq, seq, tile(CONV_W, 0), tile(CONV_W, 0), const((8, CONV_W)), const((1, CONV_W))],
                vmem_mib=28, exchange=exchange)


def in_proj_bwd(dqn, dkn, dv, dcb, dcv, dqmn, proj, conv_w8, xn, x2d, dx1, pk, winT, S, tm, stages, ws, ms, vs):
    T, D = x2d.shape
    P = winT.shape[0]
    last_blk = T // 8 - 1
    n = len(stages)
    nsteps = T // tm
    tile_w = ws[0].shape[1] // (nsteps // 2)
    turn = [e * 2 // n for e in range(n)]

    def body(dq_ref, dk_ref, dv_ref, dcb_ref, dcv_ref, dcvn_ref, dqm_ref, qa_ref, ka_ref, ch_ref, cc_ref, qma_ref,
             cw_ref, xn_ref, x_ref, dx1_ref, pk_ref, w_ref, *rest):
        st, aw, am, av = (rest[k * n:(k + 1) * n] for k in range(4))
        dx_ref, dw_ref, dg_ref, dqg_ref, dkg_ref, dmqg_ref = rest[4 * n:4 * n + 6]
        aouts = rest[4 * n + 6:]
        i = pl.program_id(0)

        for parity in range(2):
            @pl.when(i % 2 == parity)
            def _(parity=parity):
                for e in range(n):
                    if turn[e] == parity:
                        g = jnp.concatenate([_sum_chips(st[e].at[0]), _sum_chips(st[e].at[1])], axis=0)
                        d, mm, vv = _adamw_math(aw[e][...], g, am[e][...], av[e][...])
                        for k, val in enumerate((g, d, mm, vv)):
                            aouts[4 * e + k][...] = val

        @pl.when(i == 0)
        def _():
            dw_ref[...] = jnp.zeros_like(dw_ref)
            dg_ref[...] = jnp.zeros_like(dg_ref)
            dqg_ref[...] = jnp.zeros_like(dqg_ref)
            dkg_ref[...] = jnp.zeros_like(dkg_ref)
            dmqg_ref[...] = jnp.zeros_like(dmqg_ref)

        dqa, gq = _heads_norm_bwd(dq_ref[...], qa_ref[...], _small(pk_ref, "q_norm"))
        dka, gk = _heads_norm_bwd(dk_ref[...], ka_ref[...], _small(pk_ref, "k_norm"))
        dqma, gmq = _heads_norm_bwd(dqm_ref[...], qma_ref[...], _small(pk_ref, "mem_q_norm"))
        dqg_ref[...] += gq
        dkg_ref[...] += gk
        dmqg_ref[...] += gmq

        last = ((i + 1) * tm) % S == 0
        dcv = dcv_ref[...]
        nxt = jnp.where(last, 0.0, dcvn_ref[...])
        row = lax.broadcasted_iota(jnp.int32, dcv.shape, 0)
        n1 = jnp.where(row == tm - 1, nxt[0:1, :], pltpu.roll(dcv, tm - 1, 0))
        n2 = jnp.where(row == tm - 2, nxt[0:1, :], jnp.where(row == tm - 1, nxt[1:2, :], pltpu.roll(dcv, tm - 2, 0)))
        du = cw_ref[2:3, :] * dcv + cw_ref[1:2, :] * n1 + cw_ref[0:1, :] * n2
        d_proj = jnp.concatenate([_c(dqa), _c(dka), _c(dv_ref[...]), _c(du * cc_ref[...]),
                                  _c(dcb_ref[...]), _c(du * ch_ref[...]), _c(dqma)], axis=1)
        dw_ref[...] += _tn(d_proj, xn_ref[...])
        xv = x_ref[...]
        dv_, dg = _norm_bwd(_nn(d_proj, w_ref[...]), xv, _rstd(xv), _small(pk_ref, "norm_mix"))
        dx_ref[...] = dx1_ref[...] + dv_
        dg_ref[...] += dg

    tile = lambda w, col=0: pl.BlockSpec((tm, w), lambda i: (i, col))
    nhalo = pl.BlockSpec((8, CONV_W), lambda i: (jnp.minimum((i + 1) * (tm // 8), last_blk), 0))
    const = lambda shape: pl.BlockSpec(shape, lambda i: (0, 0))
    st_specs = [pl.BlockSpec((2, 4, s.shape[2], tile_w), lambda i: (0, 0, 0, i // 2)) for s in stages]
    w_specs = [pl.BlockSpec((w.shape[0], tile_w), lambda i: (0, i // 2)) for w in ws]
    res = _run("in_proj_bwd", body, (nsteps,),
               [dqn, dkn, dv, dcb, dcv, dcv, dqmn, proj, proj, proj, proj, proj, conv_w8, xn, x2d, dx1, pk, winT]
               + list(stages) + list(ws) + list(ms) + list(vs),
               [tile(ATT_W), tile(KV_W), tile(KV_W), tile(CONV_W), tile(CONV_W), nhalo, tile(MEM_W),
                tile(ATT_W, 0), tile(KV_W, 4), tile(CONV_W, 3), tile(CONV_W, 5), tile(MEM_W, 6), VM,
                tile(D), tile(D), tile(D), VM, VM] + st_specs + w_specs * 3,
               [SDS((T, D), f32), SDS((P, D), f32), SDS((1, D), f32), SDS((1, HD), f32), SDS((1, HD), f32),
                SDS((1, HD), f32)] + [SDS(w.shape, f32) for w in ws for _ in range(4)],
               [tile(D), pl.BlockSpec((P, D), lambda i: (0, 0)), const((1, D)), const((1, HD)), const((1, HD)),
                const((1, HD))] + [s for s in w_specs for _ in range(4)],
               vmem_mib=56)
    return res[:6], [res[6 + 4 * e:10 + 4 * e] for e in range(n)]


def mem_kv_bwd(dkm, dvm, kv, memn, mem2d, pk, wmkv):
    def body(dkm_ref, dvm_ref, kv_ref, mn_ref, m_ref, pk_ref, w_ref, dw_ref, dg_ref, dkg_ref):
        dkk, dkg = _heads_norm_bwd(dkm_ref[...], kv_ref[:, :MEM_W], _small(pk_ref, "mem_k_norm"))
        dkg_ref[...] = dkg
        dkv = _c(jnp.concatenate([dkk, dvm_ref[...]], axis=1))
        dw_ref[...] = _tn(mn_ref[...], dkv)
        mv = m_ref[...]
        dg_ref[...] = jnp.sum(_nt(dkv, w_ref[...]) * mv * _rstd(mv), axis=0, keepdims=True)

    return _run("mem_kv_bwd", body, (), [dkm, dvm, kv, memn, mem2d, pk, wmkv], [VM] * 7,
                [SDS(wmkv.shape, f32), SDS((1, mem2d.shape[1]), f32), SDS((1, HD), f32)], [VM] * 3)


def _halves_view(g):
    return g.reshape(4, 2, g.shape[0] // 8, g.shape[1])


def kernel(x, mem, norm_mix, w_in, q_norm, k_norm, attn_sinks, conv_w, conv_b, norm_mem, w_mem_kv, mem_q_norm, mem_k_norm, out_norm_attn, out_norm_conv, out_norm_mem, w_out, norm_ffn, w_gate, w_up, w_down, loss_target, m_norm_mix, m_w_in, m_q_norm, m_k_norm, m_attn_sinks, m_conv_w, m_conv_b, m_norm_mem, m_w_mem_kv, m_mem_q_norm, m_mem_k_norm, m_out_norm_attn, m_out_norm_conv, m_out_norm_mem, m_w_out, m_norm_ffn, m_w_gate, m_w_up, m_w_down, v_norm_mix, v_w_in, v_q_norm, v_k_norm, v_attn_sinks, v_conv_w, v_conv_b, v_norm_mem, v_w_mem_kv, v_mem_q_norm, v_mem_k_norm, v_out_norm_attn, v_out_norm_conv, v_out_norm_mem, v_w_out, v_norm_ffn, v_w_gate, v_w_up, v_w_down):
    BL, S, D = x.shape
    T = BL * S
    TM = 256
    TM_BIG = min(512, S)
    w_small = dict(norm_mix=norm_mix, norm_mem=norm_mem, norm_ffn=norm_ffn, out_norm_attn=out_norm_attn,
                   out_norm_conv=out_norm_conv, out_norm_mem=out_norm_mem, conv_w=conv_w, conv_b=conv_b, q_norm=q_norm,
                   k_norm=k_norm, mem_q_norm=mem_q_norm, mem_k_norm=mem_k_norm, attn_sinks=attn_sinks)
    m_small = dict(norm_mix=m_norm_mix, norm_mem=m_norm_mem, norm_ffn=m_norm_ffn, out_norm_attn=m_out_norm_attn,
                   out_norm_conv=m_out_norm_conv, out_norm_mem=m_out_norm_mem, conv_w=m_conv_w, conv_b=m_conv_b,
                   q_norm=m_q_norm, k_norm=m_k_norm, mem_q_norm=m_mem_q_norm, mem_k_norm=m_mem_k_norm,
                   attn_sinks=m_attn_sinks)
    v_small = dict(norm_mix=v_norm_mix, norm_mem=v_norm_mem, norm_ffn=v_norm_ffn, out_norm_attn=v_out_norm_attn,
                   out_norm_conv=v_out_norm_conv, out_norm_mem=v_out_norm_mem, conv_w=v_conv_w, conv_b=v_conv_b,
                   q_norm=v_q_norm, k_norm=v_k_norm, mem_q_norm=v_mem_q_norm, mem_k_norm=v_mem_k_norm,
                   attn_sinks=v_attn_sinks)
    pk = _pack_small(w_small)

    rowblocks = lambda a, b, c, d, e, f: [a[0].T, b[0].T, c[0].T, d[0], e[0], f[0]]
    w_rb = rowblocks(w_in, w_gate, w_up, w_down, w_out, w_mem_kv)
    m_rb = rowblocks(m_w_in, m_w_gate, m_w_up, m_w_down, m_w_out, m_w_mem_kv)
    v_rb = rowblocks(v_w_in, v_w_gate, v_w_up, v_w_down, v_w_out, v_w_mem_kv)
    (winT_s,) = prep_weights("prep_w_in", w_rb[:1])
    cw_pad = jnp.zeros((8, 128), f32).at[:3, :HD].set(conv_w[0])
    (wgT_s, wuT_s, wd_s, wout_s, wmkv_s), (winT, cw_all) = prep_weights(
        "gather_w_in", w_rb[1:], _together([gather_two_legs([winT_s]), gather_exchange([cw_pad], [False])]))
    conv_w_full = jnp.transpose(cw_all.reshape(4, 8, 128)[:, :3, :HD], (1, 0, 2)).reshape(3, CONV_W)
    conv_w8 = jnp.zeros((8, CONV_W), f32).at[:3].set(conv_w_full)
    sink_rows = jnp.broadcast_to(attn_sinks.reshape(N_Q, 1), (N_Q, 128))

    x2d = x.reshape(T, D)
    mem2d = mem.reshape(-1, D)
    (xn, proj, qkv), near1 = in_proj_fwd(x2d, pk, winT, TM_BIG, gather_near_exchange([wgT_s, wout_s, wmkv_s], relay_early=1))
    (attn_out,), (wgT, wout, wmkv, *near2) = attn_fwd(
        qkv, sink_rows, BL, S, _together([gather_far_exchange(near1, relay_early=2), gather_near_exchange([wuT_s, wd_s], relay_early=2)]))
    memn, kv, km, vm = mem_kv_fwd(mem2d, pk, wmkv)
    (conv_out, mem_out, merged, x1, h), (wuT, wd) = mixer_tail_fwd(
        x2d, attn_out, proj, qkv, km, vm, conv_w8, pk, wout, S, TM_BIG, gather_far_exchange(near2, relay_early=2))

    dx1, dx2b, act, d_gate, d_up, loss8, d_norm_ffn = ffn_fwd_bwd(h, x1, loss_target.reshape(T, D), wgT, wuT, wd, pk, TM)
    F = wd.shape[0]
    g_wd = matmul_tn(act, dx2b, "dw_down", F // 2, min(T, 1024))
    g_wgT = matmul_tn(d_gate, h, "dw_gate", F // 2, min(T, 1024))
    g_wuT = matmul_tn(d_up, h, "dw_up", F // 2, min(T, 1024))

    d_attn, d_conv_out, d_mem_out, g_wout, d_gains = out_proj_bwd(dx1, merged, attn_out, conv_out, mem_out, pk, wout, TM_BIG)
    dqmn, dkm, dvm, dcb, dcv, d_cw8, d_cbias = mem_conv_bwd(
        d_mem_out, mem_out, d_conv_out, proj, qkv, km, vm, conv_w8, pk, S, min(1024, S), None)
    (dqn, dkn, dv, d_sink8), (st_wout, st_wgT, st_wuT, st_wd) = attn_bwd(
        qkv, d_attn, attn_out, sink_rows, BL, S,
        reduce_scatter_exchange([_halves_view(g) for g in (g_wout, g_wgT, g_wuT, g_wd)], BL * (S // BLK + 1),
                                load_step=[0, 1, 4, 7], send_step=[1, 4, 7, 10], relay_step=[6, 16, 25, 33]))
    (g_x, g_winT, d_norm_mix, d_qg, d_kg, d_mqg), late_res = in_proj_bwd(
        dqn, dkn, dv, dcb, dcv, dqmn, proj, conv_w8, xn, x2d, dx1, pk, winT, S, TM,
        [st_wgT, st_wuT, st_wd, st_wout], w_rb[1:5], m_rb[1:5], v_rb[1:5])
    g_wmkv, d_norm_mem, d_mkg = mem_kv_bwd(dkm, dvm, kv, memn, mem2d, pk, wmkv)

    tot, tail_stage = tail_reduce(d_norm_mix, d_norm_mem, d_norm_ffn, d_gains, d_cw8, d_cbias, d_qg, d_kg, d_mqg, d_mkg,
                                  d_sink8, loss8, [_halves_view(g) for g in (g_winT, g_wmkv)])
    loss = tot[5, 384]
    tail_res, _ = adamw_big("adamw_tail", tail_stage, [w_rb[0], w_rb[5]], [m_rb[0], m_rb[5]], [v_rb[0], v_rb[5]], 4)
    res = {"w_in": [a.T[None] for a in tail_res[0]], "w_gate": [a.T[None] for a in late_res[0]],
           "w_up": [a.T[None] for a in late_res[1]], "w_down": [a[None] for a in late_res[2]],
           "w_out": [a[None] for a in late_res[3]], "w_mem_kv": [a[None] for a in tail_res[1]]}
    res.update(adamw_small(tot, pk, _pack_small(m_small), _pack_small(v_small), {k: w_small[k].shape for k in SMALL}))

    order = ["norm_mix", "w_in", "q_norm", "k_norm", "attn_sinks", "conv_w", "conv_b", "norm_mem", "w_mem_kv",
             "mem_q_norm", "mem_k_norm", "out_norm_attn", "out_norm_conv", "out_norm_mem", "w_out", "norm_ffn",
             "w_gate", "w_up", "w_down"]
    return (loss, g_x.reshape(BL, S, D), *[res[n][0] for n in order], *[res[n][1] for n in order],
            *[res[n][2] for n in order], *[res[n][3] for n in order])
```

```python
import collections
import functools

import jax
import jax.numpy as jnp
import numpy as np
from jax import lax
from jax.experimental import pallas as pl
from jax.experimental.pallas import tpu as pltpu

f32 = jnp.float32
MXU = jnp.bfloat16
WIRE = jnp.bfloat16
EPS = 1e-6
NEG = -1e30
HD = 64
BLK = 128
N_Q, N_KV, N_MEMH = 8, 2, 4
GQA = N_Q // N_KV
ATT_W, KV_W, CONV_W, MEM_W = 512, 128, 256, 256
VMEM_MIB = 1024 * 1024
ADAM_LR, ADAM_B1, ADAM_B2, ADAM_EPS, ADAM_WD, ADAM_STEP = 0.001, 0.9, 0.999, 1e-08, 0.01, 10

MESH = pl.DeviceIdType.MESH
VM = pl.BlockSpec(memory_space=pltpu.VMEM)
ANY = pl.BlockSpec(memory_space=pl.ANY)
SDS = jax.ShapeDtypeStruct
DMA = pltpu.SemaphoreType.DMA


def _c(v):
    return v.astype(MXU)


def _nn(a, b):
    return lax.dot_general(a, b, (((1,), (0,)), ((), ())), preferred_element_type=f32)


def _nt(a, b):
    return lax.dot_general(a, b, (((1,), (1,)), ((), ())), preferred_element_type=f32)


def _tn(a, b):
    return lax.dot_general(a, b, (((0,), (0,)), ((), ())), preferred_element_type=f32)


def _rstd(v):
    return lax.rsqrt(jnp.mean(v * v, axis=-1, keepdims=True) + EPS)


def _norm_bwd(dy, v, r, g):
    dyg = dy * g
    dv = r * dyg - v * (r * r * r) * jnp.mean(dyg * v, axis=-1, keepdims=True)
    return dv, jnp.sum(dy * v * r, axis=0, keepdims=True)


def _split3(v):
    hi = _c(v)
    r1 = v - hi.astype(f32)
    mid = _c(r1)
    return hi, mid, _c(r1 - mid.astype(f32))


def _rowsum_mxu(v, width):
    ones = jnp.ones((v.shape[1], width), MXU)
    return sum(_nn(a, ones) for a in _split3(v))


def _seg_sums(v):
    r = lax.broadcasted_iota(jnp.int32, (2 * HD, 2 * HD), 0) // HD
    c = lax.broadcasted_iota(jnp.int32, (2 * HD, 2 * HD), 1) // HD
    bd = (r == c).astype(MXU)
    outs = []
    for b in range(v.shape[1] // (2 * HD)):
        outs.append(sum(_nn(a, bd) for a in _split3(v[:, b * 2 * HD:(b + 1) * 2 * HD])))
    return outs[0] if len(outs) == 1 else jnp.concatenate(outs, axis=1)


def _lanes(g, width):
    return jnp.concatenate([g] * (width // HD), axis=1)


def _heads_rstd(v):
    return lax.rsqrt(_seg_sums(v * v) * (1.0 / HD) + EPS)


def _heads_norm_bwd(dy, v, g):
    r = _heads_rstd(v)
    gl = _lanes(g, v.shape[1])
    dyg = dy * gl
    dv = r * dyg - v * (r * r * r) * (_seg_sums(dyg * v) * (1.0 / HD))
    dgl = jnp.sum(dy * v * r, axis=0, keepdims=True)
    return dv, sum(dgl[:, s * HD:(s + 1) * HD] for s in range(v.shape[1] // HD))


def _exp_scores(s, extra=None):
    m = jnp.max(s, axis=-1, keepdims=True)
    if extra is None:
        return jnp.exp(s - m), None
    m = jnp.maximum(m, extra)
    return jnp.exp(s - m), jnp.exp(extra - m)


def _place():
    return lax.axis_index("x"), lax.axis_index("y"), lax.axis_index("c")


SMALL_AT = {"norm_mix": (0, 0, 1024), "norm_mem": (1, 0, 1024), "norm_ffn": (2, 0, 1024),
            "out_norm_attn": (3, 0, ATT_W), "out_norm_conv": (3, ATT_W, CONV_W), "out_norm_mem": (3, ATT_W + CONV_W, MEM_W),
            "conv_b": (4, 3 * CONV_W, CONV_W), "q_norm": (5, 0, HD), "k_norm": (5, HD, HD), "mem_q_norm": (5, 2 * HD, HD),
            "mem_k_norm": (5, 3 * HD, HD), "attn_sinks": (5, 256, N_Q)}
SMALL = ("norm_mix", "norm_mem", "norm_ffn", "out_norm_attn", "out_norm_conv", "out_norm_mem", "conv_w", "conv_b",
         "q_norm", "k_norm", "mem_q_norm", "mem_k_norm", "attn_sinks")


def _small(pk_ref, name):
    r, c0, w = SMALL_AT[name]
    return pk_ref[r:r + 1, c0:c0 + w]


def _pack_small(d):
    z = lambda n: jnp.zeros((1, n), f32)
    row3 = jnp.concatenate([d["out_norm_attn"], d["out_norm_conv"], d["out_norm_mem"]], axis=1)
    row4 = jnp.concatenate([d["conv_w"].reshape(1, 3 * HD), z(3 * CONV_W - 3 * HD), d["conv_b"]], axis=1)
    row5 = jnp.concatenate([d["q_norm"], d["k_norm"], d["mem_q_norm"], d["mem_k_norm"], d["attn_sinks"],
                            z(1024 - 4 * HD - N_Q)], axis=1)
    return jnp.concatenate([d["norm_mix"], d["norm_mem"], d["norm_ffn"], row3, row4, row5, z(1024), z(1024)], axis=0)


def _other_chips(x, y):
    return [(1 - x, y), (x, 1 - y), (1 - x, 1 - y)]


Exchange = collections.namedtuple("Exchange", "ins outs sems start finish relays aliases", defaults=((), {}))


def _together(exchanges):
    def bounds(key):
        at, out = 0, []
        for ex in exchanges:
            out.append((at, at + len(getattr(ex, key))))
            at += len(getattr(ex, key))
        return out

    bi, bo, bs = bounds("ins"), bounds("outs"), bounds("sems")

    def of(i, fn):
        return lambda xa, xo, xs: fn(xa[bi[i][0]:bi[i][1]], xo[bo[i][0]:bo[i][1]], xs[bs[i][0]:bs[i][1]])

    def every(name):
        fns = [of(i, getattr(ex, name)) for i, ex in enumerate(exchanges)]

        def run(xa, xo, xs):
            for fn in fns:
                fn(xa, xo, xs)
        return run

    aliases = {}
    for i, ex in enumerate(exchanges):
        aliases.update({bi[i][0] + a: bo[i][0] + o for a, o in ex.aliases.items()})
    return Exchange([a for ex in exchanges for a in ex.ins], [o for ex in exchanges for o in ex.outs],
                    [s for ex in exchanges for s in ex.sems], every("start"), every("finish"),
                    [(sbe, of(i, fn)) for i, ex in enumerate(exchanges) for sbe, fn in ex.relays], aliases)


def _run(name, body, grid, ins, in_specs, out_shape, out_specs, scratch=(), vmem_mib=32, exchange=None):
    ins, in_specs, out_shape, out_specs, scratch = list(ins), list(in_specs), list(out_shape), list(out_specs), list(scratch)
    ni, no, ns = len(ins), len(out_shape), len(scratch)
    ex = exchange
    if ex is not None:
        nxi, nxo = len(ex.ins), len(ex.outs)

    def call_body(*refs):
        if ex is None:
            body(*refs)
            return
        a, xa = refs[:ni], refs[ni:ni + nxi]
        o, xo = refs[ni + nxi:ni + nxi + no], refs[ni + nxi + no:ni + nxi + no + nxo]
        s, xs = refs[ni + nxi + no + nxo:ni + nxi + no + nxo + ns], refs[ni + nxi + no + nxo + ns:]
        if grid:
            first = functools.reduce(jnp.logical_and, [pl.program_id(d) == 0 for d in range(len(grid))])
            last = functools.reduce(jnp.logical_and, [pl.program_id(d) == grid[d] - 1 for d in range(len(grid))])
            pl.when(first)(lambda: ex.start(xa, xo, xs))
            body(*a, *o, *s)
            nsteps = functools.reduce(lambda p, q: p * q, grid)
            for before_end, fn in ex.relays:
                at = np.unravel_index(max(nsteps - 1 - before_end, 0), grid)
                here = functools.reduce(jnp.logical_and, [pl.program_id(d) == int(at[d]) for d in range(len(grid))])
                pl.when(here)(functools.partial(fn, xa, xo, xs))
            pl.when(last)(lambda: ex.finish(xa, xo, xs))
        else:
            ex.start(xa, xo, xs)
            if body is not None:
                body(*a, *o, *s)
            for _, fn in ex.relays:
                fn(xa, xo, xs)
            ex.finish(xa, xo, xs)

    kw = dict(grid=grid) if grid else {}
    if ex is not None:
        if ex.aliases:
            kw["input_output_aliases"] = {ni + i: no + o for i, o in ex.aliases.items()}
        ins, in_specs = ins + list(ex.ins), in_specs + [ANY] * nxi
        out_shape, out_specs = out_shape + list(ex.outs), out_specs + [ANY] * nxo
        scratch = scratch + list(ex.sems)
    res = pl.pallas_call(
        call_body, name=name, out_shape=out_shape, in_specs=in_specs, out_specs=out_specs, scratch_shapes=scratch,
        compiler_params=pltpu.CompilerParams(dimension_semantics=("arbitrary",) * len(grid) if grid else None,
                                             vmem_limit_bytes=vmem_mib * VMEM_MIB), **kw)(*ins)
    res = list(res)
    return (res[:no], res[no:]) if ex is not None else res


def _remote(src, dst, ssem, rsem, dev):
    return pltpu.make_async_remote_copy(src_ref=src, dst_ref=dst, send_sem=ssem, recv_sem=rsem,
                                        device_id=dev, device_id_type=MESH)


def gather_exchange(shards, split, relay_early=0):
    n = len(shards)

    def rows(ref, e, kk, half=None):
        R = shards[e].shape[0]
        if half is None:
            return ref.at[pl.ds(pl.multiple_of(kk * R, 8), R)]
        return ref.at[pl.ds(pl.multiple_of(kk * R + half * (R // 2), 8), R // 2)]

    def ici(src, dst, sm, e, j, chip_j, x, y, c):
        k = 2 * x + y
        if split[e]:
            s = src[e].at[pl.ds(pl.multiple_of(c * (shards[e].shape[0] // 2), 8), shards[e].shape[0] // 2)]
            return _remote(s, rows(dst[e], e, k, c), sm[0].at[6 * e + j], sm[1].at[6 * e + j], (*chip_j, c))
        return _remote(src[e], rows(dst[e], e, k), sm[0].at[6 * e + j], sm[1].at[6 * e + j], (*chip_j, c))

    def landed(dst, e, chip_j, c):
        kj = 2 * chip_j[0] + chip_j[1]
        return rows(dst[e], e, kj, c) if split[e] else rows(dst[e], e, kj)

    def forward(dst, sm, e, j, chip_j, x, y, c, sender_c):
        kj = 2 * chip_j[0] + chip_j[1]
        r = rows(dst[e], e, kj, sender_c)
        return _remote(r, r, sm[0].at[6 * e + 3 + j], sm[1].at[6 * e + 3 + j], (x, y, 1 - c))

    def local(src, dst, sm, e, x, y):
        return pltpu.make_async_copy(src[e], rows(dst[e], e, 2 * x + y), sm[2].at[e])

    def start(src, dst, sm):
        x, y, c = _place()
        for e in range(n):
            local(src, dst, sm, e, x, y).start()
            for j, chip_j in enumerate(_other_chips(x, y)):
                ici(src, dst, sm, e, j, chip_j, x, y, c).start()

    def relay(src, dst, sm):
        x, y, c = _place()
        for e in range(n):
            for j, chip_j in enumerate(_other_chips(x, y)):
                r = landed(dst, e, chip_j, c)
                _remote(r, r, sm[0].at[6 * e + j], sm[1].at[6 * e + j], (*chip_j, c)).wait_recv()
                if split[e]:
                    forward(dst, sm, e, j, chip_j, x, y, c, c).start()

    def finish(src, dst, sm):
        x, y, c = _place()
        chips = _other_chips(x, y)
        for e in range(n):
            for j, chip_j in enumerate(chips):
                if split[e]:
                    forward(dst, sm, e, j, chip_j, x, y, c, 1 - c).wait_recv()
        for e in range(n):
            for j, chip_j in enumerate(chips):
                ici(src, dst, sm, e, j, chip_j, x, y, c).wait_send()
                if split[e]:
                    forward(dst, sm, e, j, chip_j, x, y, c, c).wait_send()
            local(src, dst, sm, e, x, y).wait()

    outs = [SDS((4 * s.shape[0], s.shape[1]), s.dtype) for s in shards]
    return Exchange(list(shards), outs, [DMA((6 * n,)), DMA((6 * n,)), DMA((n,))], start, finish, [(relay_early, relay)])


def _block_rows(ref, R, kk, half, quarter=None):
    hr = R // 2
    if quarter is None:
        return ref.at[pl.ds(pl.multiple_of(kk * R + half * hr, 8), hr)]
    return ref.at[pl.ds(pl.multiple_of(kk * R + half * hr + quarter * (hr // 2), 8), hr // 2)]


def gather_near_exchange(shards, relay_early=0):
    n = len(shards)
    R = [s.shape[0] for s in shards]

    def ici(src, dst, sm, e, j, chip_j, x, y, c):
        half = src[e].at[pl.ds(pl.multiple_of(c * (R[e] // 2), 8), R[e] // 2)]
        return _remote(half, _block_rows(dst[e], R[e], 2 * x + y, c), sm[0].at[4 * e + j], sm[1].at[4 * e + j], (*chip_j, c))

    def forward(dst, sm, e, j, chip_j, x, y, c, sender_c):
        r = _block_rows(dst[e], R[e], 2 * chip_j[0] + chip_j[1], sender_c)
        return _remote(r, r, sm[0].at[4 * e + 2 + j], sm[1].at[4 * e + 2 + j], (x, y, 1 - c))

    def local(src, dst, sm, e, x, y):
        return pltpu.make_async_copy(src[e], dst[e].at[pl.ds(pl.multiple_of((2 * x + y) * R[e], 8), R[e])], sm[2].at[e])

    def start(src, dst, sm):
        x, y, c = _place()
        for e in range(n):
            local(src, dst, sm, e, x, y).start()
            for j, chip_j in enumerate(_other_chips(x, y)[:2]):
                ici(src, dst, sm, e, j, chip_j, x, y, c).start()

    def relay(src, dst, sm):
        x, y, c = _place()
        for e in range(n):
            for j, chip_j in enumerate(_other_chips(x, y)[:2]):
                r = _block_rows(dst[e], R[e], 2 * chip_j[0] + chip_j[1], c)
                _remote(r, r, sm[0].at[4 * e + j], sm[1].at[4 * e + j], (*chip_j, c)).wait_recv()
                forward(dst, sm, e, j, chip_j, x, y, c, c).start()

    def finish(src, dst, sm):
        x, y, c = _place()
        near = _other_chips(x, y)[:2]
        for e in range(n):
            for j, chip_j in enumerate(near):
                forward(dst, sm, e, j, chip_j, x, y, c, 1 - c).wait_recv()
        for e in range(n):
            for j, chip_j in enumerate(near):
                ici(src, dst, sm, e, j, chip_j, x, y, c).wait_send()
                forward(dst, sm, e, j, chip_j, x, y, c, c).wait_send()
            local(src, dst, sm, e, x, y).wait()

    outs = [SDS((4 * s.shape[0], s.shape[1]), s.dtype) for s in shards]
    return Exchange(list(shards), outs, [DMA((4 * n,)), DMA((4 * n,)), DMA((n,))], start, finish, [(relay_early, relay)])


def gather_far_exchange(bufs, relay_early=0):
    n = len(bufs)
    R = [b.shape[0] // 4 for b in bufs]

    def send(src, dst, sm, e, j, x, y, c):
        to, of = _other_chips(x, y)[j], _other_chips(x, y)[1 - j]
        kk = 2 * of[0] + of[1]
        return _remote(_block_rows(src[e], R[e], kk, c, j), _block_rows(dst[e], R[e], kk, c, j),
                       sm[0].at[4 * e + j], sm[1].at[4 * e + j], (*to, c))

    def landed(dst, e, j, x, y, half):
        return _block_rows(dst[e], R[e], 2 * (1 - x) + (1 - y), half, j)

    def forward(dst, sm, e, j, x, y, c, sender_c):
        r = landed(dst, e, j, x, y, sender_c)
        return _remote(r, r, sm[0].at[4 * e + 2 + j], sm[1].at[4 * e + 2 + j], (x, y, 1 - c))

    def start(src, dst, sm):
        x, y, c = _place()
        for e in range(n):
            for j in range(2):
                send(src, dst, sm, e, j, x, y, c).start()

    def relay(src, dst, sm):
        x, y, c = _place()
        for e in range(n):
            for j in range(2):
                r = landed(dst, e, j, x, y, c)
                _remote(r, r, sm[0].at[4 * e + j], sm[1].at[4 * e + j], (*_other_chips(x, y)[j], c)).wait_recv()
                forward(dst, sm, e, j, x, y, c, c).start()

    def finish(src, dst, sm):
        x, y, c = _place()
        for e in range(n):
            for j in range(2):
                forward(dst, sm, e, j, x, y, c, 1 - c).wait_recv()
        for e in range(n):
            for j in range(2):
                send(src, dst, sm, e, j, x, y, c).wait_send()
                forward(dst, sm, e, j, x, y, c, c).wait_send()

    outs = [SDS(b.shape, b.dtype) for b in bufs]
    return Exchange(list(bufs), outs, [DMA((4 * n,)), DMA((4 * n,))], start, finish, [(relay_early, relay)],
                    {i: i for i in range(n)})


def gather_two_legs(shards):
    near = gather_near_exchange(shards)
    far = gather_far_exchange(near.outs)

    def finish(src, dst, sm):
        near.relays[0][1](src, dst, sm[:3])
        near.finish(src, dst, sm[:3])
        far.start(dst, dst, sm[3:])
        far.relays[0][1](dst, dst, sm[3:])
        far.finish(dst, dst, sm[3:])

    return Exchange(near.ins, near.outs, list(near.sems) + list(far.sems),
                    lambda src, dst, sm: near.start(src, dst, sm[:3]), finish)


def scatter_exchange(parts, relay_before_end=None, want_issue=False):
    n = len(parts)
    by_entry = relay_before_end is not None
    relay_before_end = relay_before_end or [0] * n

    def ici(p, st, sm, e, j, chip_j, x, y, c):
        k, kj = 2 * x + y, 2 * chip_j[0] + chip_j[1]
        return _remote(p[e].at[kj], st[e].at[c, k], sm[0].at[8 * e + j], sm[1].at[8 * e + j], (*chip_j, c))

    def own(p, st, sm, e, x, y, c):
        k = 2 * x + y
        return _remote(p[e].at[k], st[e].at[c, k], sm[0].at[8 * e + 3], sm[1].at[8 * e + 3], (x, y, 1 - c))

    def forward(st, sm, e, j, chip_j, x, y, c, sender_c):
        kj = 2 * chip_j[0] + chip_j[1]
        r = st[e].at[sender_c, kj]
        return _remote(r, r, sm[0].at[8 * e + 4 + j], sm[1].at[8 * e + 4 + j], (x, y, 1 - c))

    def local(p, st, sm, e, x, y, c):
        k = 2 * x + y
        return pltpu.make_async_copy(p[e].at[k], st[e].at[c, k], sm[2].at[e])

    def issue(e, p, st, sm):
        x, y, c = _place()
        for j, chip_j in enumerate(_other_chips(x, y)):
            ici(p, st, sm, e, j, chip_j, x, y, c).start()
        local(p, st, sm, e, x, y, c).start()
        own(p, st, sm, e, x, y, c).start()

    def start(p, st, sm, before_slot=None):
        x, y, c = _place()
        if by_entry:
            for e in range(n):
                issue(e, p, st, sm)
            return
        for j, chip_j in enumerate(_other_chips(x, y)):
            if before_slot is not None:
                before_slot(j, 2 * chip_j[0] + chip_j[1])
            for e in range(n):
                ici(p, st, sm, e, j, chip_j, x, y, c).start()
        if before_slot is not None:
            before_slot(3, 2 * x + y)
        for e in range(n):
            local(p, st, sm, e, x, y, c).start()
            own(p, st, sm, e, x, y, c).start()

    def relay(e, p, st, sm):
        x, y, c = _place()
        for j, chip_j in enumerate(_other_chips(x, y)):
            kj = 2 * chip_j[0] + chip_j[1]
            r = st[e].at[c, kj]
            _remote(r, r, sm[0].at[8 * e + j], sm[1].at[8 * e + j], (*chip_j, c)).wait_recv()
            forward(st, sm, e, j, chip_j, x, y, c, c).start()

    def finish(p, st, sm):
        x, y, c = _place()
        k = 2 * x + y
        chips = _other_chips(x, y)
        for e in range(n):
            r = st[e].at[1 - c, k]
            _remote(r, r, sm[0].at[8 * e + 3], sm[1].at[8 * e + 3], (x, y, 1 - c)).wait_recv()
            for j, chip_j in enumerate(chips):
                forward(st, sm, e, j, chip_j, x, y, c, 1 - c).wait_recv()
        for e in range(n):
            own(p, st, sm, e, x, y, c).wait_send()
            for j, chip_j in enumerate(chips):
                ici(p, st, sm, e, j, chip_j, x, y, c).wait_send()
                forward(st, sm, e, j, chip_j, x, y, c, c).wait_send()
            local(p, st, sm, e, x, y, c).wait()

    outs = [SDS((2,) + a.shape, a.dtype) for a in parts]
    ex = Exchange(list(parts), outs, [DMA((8 * n,)), DMA((8 * n,)), DMA((n,))], start, finish,
                  [(relay_before_end[e], functools.partial(relay, e)) for e in range(n)])
    return (ex, issue) if want_issue else ex


def reduce_scatter_exchange(grads, nsteps, load_step, send_step, relay_step):
    n = len(grads)
    hrs = [g.shape[2] for g in grads]
    C = grads[0].shape[3]
    scatter, issue = scatter_exchange([SDS((4,) + g.shape[2:], WIRE) for g in grads], want_issue=True)
    hand_on = [fn for _, fn in scatter.relays]

    def refs(xs):
        return xs[:3], xs[3], xs[4], xs[5], xs[6], xs[7:7 + n], xs[7 + n:]

    def push(e, g, psem, qsem, sib_st):
        x, y, c = _place()
        return _remote(g[e].at[:, 1 - c], sib_st[e], psem.at[e], qsem.at[e], (x, y, 1 - c))

    def fetch(e, g, lsem, own_st):
        _, _, c = _place()
        return pltpu.make_async_copy(g[e].at[:, c], own_st.at[e % 2, :, pl.ds(0, hrs[e])], lsem.at[e])

    def start(g, xo, xs):
        _, _, psem, qsem, _, sib_st, _ = refs(xs)
        for e in range(n):
            push(e, g, psem, qsem, sib_st).start()

    def load(e, g, xo, xs):
        _, lsem, _, _, own_st, _, _ = refs(xs)
        fetch(e, g, lsem, own_st).start()

    def send(e, g, xo, xs):
        sm, lsem, psem, qsem, own_st, sib_st, part = refs(xs)
        fetch(e, g, lsem, own_st).wait()
        push(e, g, psem, qsem, sib_st).wait_recv()
        part[e][...] = (own_st[e % 2, :, 0:hrs[e]] + sib_st[e][...]).astype(WIRE)
        issue(e, part, xo, sm)

    def relay(e, g, xo, xs):
        sm, _, _, _, _, _, part = refs(xs)
        hand_on[e](part, xo, sm)

    def finish(g, xo, xs):
        sm, _, psem, qsem, _, sib_st, part = refs(xs)
        scatter.finish(part, xo, sm)
        for e in range(n):
            push(e, g, psem, qsem, sib_st).wait_send()

    plan = sorted([(min(step[e], nsteps - 1), phase, e) for phase, step in enumerate((load_step, send_step, relay_step))
                   for e in range(n)])
    stage = (load, send, relay)
    relays = [(nsteps - 1 - at, functools.partial(stage[phase], e)) for at, phase, e in plan]
    scratch = (list(scatter.sems) + [DMA((n,)), DMA((n,)), DMA((n,))] + [pltpu.VMEM((2, 4, max(hrs), C), f32)]
               + [pltpu.VMEM((4, hr, C), f32) for hr in hrs] + [pltpu.VMEM((4, hr, C), WIRE) for hr in hrs])
    return Exchange(list(grads), scatter.outs, scratch, start, finish, relays)


def tail_reduce(d_norm_mix, d_norm_mem, d_norm_ffn, d_gains, d_cw8, d_cbias, d_qg, d_kg, d_mqg, d_mkg, d_sink8, loss8, tail):
    n = len(tail)
    scatter = scatter_exchange([SDS((4,) + a.shape[2:], WIRE) for a in tail])

    def half_copy(g, sib, hsem, e, j, slot, x, y, c):
        return _remote(g[e].at[slot, 1 - c], sib[e].at[slot], hsem[0].at[4 * e + j], hsem[1].at[4 * e + j], (x, y, 1 - c))

    def body(nm_ref, nmem_ref, nf_ref, gn_ref, cw_ref, cb_ref, qg_ref, kg_ref, mqg_ref, mkg_ref, sk_ref, ls_ref, *rest):
        g, o_ref, st = rest[:n], rest[n], rest[n + 1:2 * n + 1]
        buf, ssem, rsem = rest[2 * n + 1:2 * n + 4]
        own, sib, part = (rest[2 * n + 4 + i * n:2 * n + 4 + (i + 1) * n] for i in range(3))
        lsem = rest[5 * n + 4]
        hsem, xsem = rest[5 * n + 5:5 * n + 7], rest[5 * n + 7:]
        x, y, c = _place()
        loads = [pltpu.make_async_copy(g[e].at[:, c], own[e], lsem.at[e]) for e in range(n)]
        for ld in loads:
            ld.start()
        for j, slot in enumerate([2 * cx + cy for cx, cy in _other_chips(x, y)] + [2 * x + y]):
            for e in range(n):
                half_copy(g, sib, hsem, e, j, slot, x, y, c).start()
        me = 4 * x + 2 * y + c
        mine = buf.at[me]
        mine[...] = jnp.zeros((8, 1024), f32)
        mine[0:1, :] = nm_ref[...]
        mine[1:2, :] = nmem_ref[...]
        mine[2:3, :] = nf_ref[...]
        mine[3:4, :] = gn_ref[...]
        for j in range(3):
            mine[4:5, pl.ds(j * CONV_W, CONV_W)] = cw_ref[j:j + 1, :]
        mine[4:5, pl.ds(3 * CONV_W, CONV_W)] = cb_ref[...]
        for j, r in enumerate((qg_ref, kg_ref, mqg_ref, mkg_ref)):
            mine[5:6, pl.ds(j * HD, HD)] = r[...]
        mine[5:6, pl.ds(256, 128)] = sk_ref[0:1, :]
        mine[5:6, pl.ds(384, 128)] = ls_ref[0:1, :]

        def peer_of(m):
            return (1 - x if m & 4 else x, 1 - y if m & 2 else y, 1 - c if m & 1 else c)

        for m in range(1, 8):
            _remote(mine, mine, ssem.at[m - 1], rsem.at[m - 1], peer_of(m)).start()
        for ld in loads:
            ld.wait()

        def chip_partial(j, slot):
            for e in range(n):
                half_copy(g, sib, hsem, e, j, slot, x, y, c).wait()
                part[e][slot] = (own[e][slot] + sib[e][slot]).astype(WIRE)

        scatter.start(part, st, xsem, chip_partial)
        for _, hand_on in scatter.relays:
            hand_on(part, st, xsem)
        scatter.finish(part, st, xsem)
        for m in range(1, 8):
            p = peer_of(m)
            got = buf.at[4 * p[0] + 2 * p[1] + p[2]]
            _remote(got, got, ssem.at[m - 1], rsem.at[m - 1], p).wait_recv()
        for m in range(1, 8):
            _remote(mine, mine, ssem.at[m - 1], rsem.at[m - 1], peer_of(m)).wait_send()
        acc = buf[0]
        for d in range(1, 8):
            acc = acc + buf[d]
        o_ref[...] = acc

    ins = [d_norm_mix, d_norm_mem, d_norm_ffn, d_gains, d_cw8, d_cbias, d_qg, d_kg, d_mqg, d_mkg, d_sink8, loss8]
    half_shape = [(4,) + a.shape[2:] for a in tail]
    scratch = ([pltpu.VMEM((8, 8, 1024), f32), DMA((7,)), DMA((7,))]
               + [pltpu.VMEM(s, f32) for s in half_shape] * 2 + [pltpu.VMEM(s, WIRE) for s in half_shape]
               + [DMA((n,)), DMA((4 * n,)), DMA((4 * n,))] + list(scatter.sems))
    res = _run("tail_reduce", body, (), ins + list(tail), [VM] * len(ins) + [ANY] * n,
               [SDS((8, 1024), f32)] + list(scatter.outs), [VM] + [ANY] * n, scratch=scratch, vmem_mib=40)
    return res[0], res[1:]


def _adamw_math(w, g, m, v):
    m = ADAM_B1 * m + (1.0 - ADAM_B1) * g
    v = ADAM_B2 * v + (1.0 - ADAM_B2) * (g * g)
    m_hat = m / (1.0 - ADAM_B1 ** ADAM_STEP)
    v_hat = v / (1.0 - ADAM_B2 ** ADAM_STEP)
    delta = -ADAM_LR * (m_hat / (jnp.sqrt(v_hat) + ADAM_EPS) + ADAM_WD * w)
    return delta, m, v


def _sum_chips(st):
    return ((st[0].astype(f32) + st[1].astype(f32)) + st[2].astype(f32)) + st[3].astype(f32)


def adamw_big(name, stages, ws, ms, vs, nstep, exchange=None):
    n = len(stages)

    def body(*refs):
        st, w, m, v = refs[:n], refs[n:2 * n], refs[2 * n:3 * n], refs[3 * n:4 * n]
        outs = refs[4 * n:]
        for e in range(n):
            g = jnp.concatenate([_sum_chips(st[e].at[0]), _sum_chips(st[e].at[1])], axis=0)
            d, mm, vv = _adamw_math(w[e][...], g, m[e][...], v[e][...])
            outs[4 * e][...] = g
            outs[4 * e + 1][...] = d
            outs[4 * e + 2][...] = mm
            outs[4 * e + 3][...] = vv

    st_specs, w_specs = [], []
    for e in range(n):
        _, _, hr, C = stages[e].shape
        st_specs.append(pl.BlockSpec((2, 4, hr, C // nstep), lambda i: (0, 0, 0, i)))
        w_specs.append(pl.BlockSpec((2 * hr, C // nstep), lambda i: (0, i)))
    out_specs = [s for s in w_specs for _ in range(4)]
    out_shape = [SDS(w.shape, f32) for w in ws for _ in range(4)]
    res = _run(name, body, (nstep,), list(stages) + list(ws) + list(ms) + list(vs), st_specs + w_specs * 3,
               out_shape, out_specs, vmem_mib=16, exchange=exchange)
    res, sent = res if exchange is not None else (res, None)
    return [res[4 * e:4 * e + 4] for e in range(n)], sent


def adamw_small(tot, pk_w, pk_m, pk_v, shapes):
    def body(tot_ref, w_ref, m_ref, v_ref, *outs):
        x, y, _ = _place()
        chip = 2 * x + y
        taps = []
        for j in range(3):
            mine = tot_ref[4:5, j * CONV_W:j * CONV_W + HD]
            for s in range(1, 4):
                mine = jnp.where(chip == s, tot_ref[4:5, j * CONV_W + s * HD:j * CONV_W + (s + 1) * HD], mine)
            taps.append(mine)
        row4 = jnp.concatenate(taps + [jnp.zeros((1, 3 * CONV_W - 3 * HD), f32), tot_ref[4:5, 3 * CONV_W:]], axis=1)
        tot_v = tot_ref[...]
        row = lax.broadcasted_iota(jnp.int32, tot_v.shape, 0)
        g = jnp.where(row == 4, jnp.broadcast_to(row4, tot_v.shape), tot_v)
        d, mm, vv = _adamw_math(w_ref[...], g, m_ref[...], v_ref[...])
        for i, name in enumerate(SMALL):
            for k, val in enumerate((g, d, mm, vv)):
                if name == "conv_w":
                    outs[4 * i + k][...] = jnp.concatenate([val[4:5, j * HD:(j + 1) * HD] for j in range(3)], axis=0)[None]
                else:
                    r, c0, w = SMALL_AT[name]
                    outs[4 * i + k][...] = val[r:r + 1, c0:c0 + w]

    out_shape = [SDS(shapes[k], f32) for k in SMALL for _ in range(4)]
    res = _run("adamw_small", body, (), [tot, pk_w, pk_m, pk_v], [VM] * 4, out_shape, [VM] * len(out_shape))
    return {k: res[4 * i:4 * i + 4] for i, k in enumerate(SMALL)}


def prep_weights(name, shards, exchange=None):
    n = len(shards)

    def body(*refs):
        for e in range(n):
            refs[n + e][...] = _c(refs[e][...])

    return _run(name, body, (), shards, [VM] * n, [SDS(a.shape, MXU) for a in shards], [VM] * n, vmem_mib=16, exchange=exchange)


def mem_kv_fwd(mem2d, pk, wmkv):
    M, D = mem2d.shape

    def body(m_ref, pk_ref, w_ref, mn_ref, kv_ref, km_ref, vm_ref):
        m = m_ref[...]
        mn = _c(m * _rstd(m) * _small(pk_ref, "norm_mem"))
        mn_ref[...] = mn
        kv = _nn(mn, w_ref[...])
        kv_ref[...] = kv
        kk = kv[:, :MEM_W]
        km_ref[...] = _c(kk * _heads_rstd(kk) * _lanes(_small(pk_ref, "mem_k_norm"), MEM_W))
        vm_ref[...] = _c(kv[:, MEM_W:])

    return _run("mem_kv_fwd", body, (), [mem2d, pk, wmkv], [VM] * 3,
                [SDS((M, D), MXU), SDS((M, 2 * MEM_W), f32), SDS((M, MEM_W), MXU), SDS((M, MEM_W), MXU)], [VM] * 4)


QKV_W = ATT_W + 2 * KV_W + MEM_W


def in_proj_fwd(x2d, pk, winT, tm, exchange):
    T, D = x2d.shape
    P = winT.shape[0]

    def body(x_ref, pk_ref, w_ref, xn_ref, proj_ref, qkv_ref):
        xv = x_ref[...]
        xn = _c(xv * _rstd(xv) * _small(pk_ref, "norm_mix"))
        xn_ref[...] = xn
        proj = _nt(xn, w_ref[...])
        proj_ref[...] = proj
        q, k = proj[:, :ATT_W], proj[:, ATT_W:ATT_W + KV_W]
        qm = proj[:, P - MEM_W:]
        qkv_ref[...] = jnp.concatenate(
            [_c(q * _heads_rstd(q) * _lanes(_small(pk_ref, "q_norm"), ATT_W)),
             _c(k * _heads_rstd(k) * _lanes(_small(pk_ref, "k_norm"), KV_W)),
             _c(proj[:, ATT_W + KV_W:ATT_W + 2 * KV_W]),
             _c(qm * _heads_rstd(qm) * _lanes(_small(pk_ref, "mem_q_norm"), MEM_W))], axis=1)

    return _run("in_proj_fwd", body, (T // tm,), [x2d, pk, winT],
                [pl.BlockSpec((tm, D), lambda i: (i, 0)), VM, VM],
                [SDS((T, D), MXU), SDS((T, P), f32), SDS((T, QKV_W), MXU)],
                [pl.BlockSpec((tm, D), lambda i: (i, 0)), pl.BlockSpec((tm, P), lambda i: (i, 0)),
                 pl.BlockSpec((tm, QKV_W), lambda i: (i, 0))],
                vmem_mib=40, exchange=exchange)


def _swa_bias_table():
    r = np.arange(GQA * BLK)[:, None]
    k = np.arange(2 * BLK)[None, :]
    dist = (r % BLK) + BLK - k
    band = (dist >= 0) & (dist < BLK)
    tab = np.empty((2, N_KV, GQA * BLK, 2 * BLK), np.float32)
    for later in range(2):
        valid = band & ((k >= BLK) | (later == 1))
        for g in range(N_KV):
            slope = 2.0 ** -(g * GQA + r // BLK + 1.0)
            tab[later, g] = np.where(valid, -slope * dist, NEG)
    return jnp.asarray(tab)


def _sink_column(g, sk_ref):
    hrow = lax.broadcasted_iota(jnp.int32, (GQA * BLK, 1), 0) // BLK
    sink = jnp.zeros((GQA * BLK, 1), f32)
    for hh in range(GQA):
        sink = jnp.where(hrow == hh, sk_ref[g * GQA + hh:g * GQA + hh + 1, 0:1], sink)
    return sink


def _stack_heads(v, g):
    return jnp.concatenate([v[:, (g * GQA + hh) * HD:(g * GQA + hh + 1) * HD] for hh in range(GQA)], axis=0)


def attn_fwd(qkv, sink_rows, BL, S, exchange, qb=2):
    NS = S // (qb * BLK)
    T = BL * S

    def body(q_ref, kc_ref, kp_ref, vc_ref, vp_ref, sk_ref, tab_ref, o_ref):
        j = pl.program_id(1)
        kall = jnp.concatenate([kp_ref[...], kc_ref[...]], axis=0)
        vall = jnp.concatenate([vp_ref[...], vc_ref[...]], axis=0)
        ones = jnp.ones((2 * BLK, HD), MXU)
        for b in range(qb):
            q = q_ref[pl.ds(b * BLK, BLK), :]
            k2, v2 = kall[b * BLK:(b + 2) * BLK], vall[b * BLK:(b + 2) * BLK]
            later = jnp.minimum(j, 1) if b == 0 else 1
            for g in range(N_KV):
                kn, vh = k2[:, g * HD:(g + 1) * HD], v2[:, g * HD:(g + 1) * HD]
                s = _nt(_stack_heads(q, g), kn) * (HD ** -0.5) + tab_ref[later, g]
                e, es = _exp_scores(s, _sink_column(g, sk_ref))
                eb = _c(e)
                o = _nn(eb, vh) * (1.0 / (_nn(eb, ones) + es))
                for hh in range(GQA):
                    o_ref[pl.ds(b * BLK, BLK), pl.ds((g * GQA + hh) * HD, HD)] = o[hh * BLK:(hh + 1) * BLK]

    cur = lambda col: (lambda b, j: (b * NS + j, col))
    prev = lambda col: (lambda b, j: (qb * (b * NS + j) - jnp.minimum(j, 1), col))
    return _run("attn_fwd", body, (BL, NS), [qkv, qkv, qkv, qkv, qkv, sink_rows, _swa_bias_table()],
                [pl.BlockSpec((qb * BLK, ATT_W), cur(0)),
                 pl.BlockSpec((qb * BLK, KV_W), cur(4)), pl.BlockSpec((BLK, KV_W), prev(4)),
                 pl.BlockSpec((qb * BLK, KV_W), cur(5)), pl.BlockSpec((BLK, KV_W), prev(5)),
                 pl.BlockSpec((8, 128), lambda b, j: (0, 0)), VM],
                [SDS((T, ATT_W), f32)], [pl.BlockSpec((qb * BLK, ATT_W), cur(0))], exchange=exchange)


def _conv_taps(u, uh):
    row = lax.broadcasted_iota(jnp.int32, u.shape, 0)
    u1 = jnp.where(row == 0, uh[7:8, :], pltpu.roll(u, 1, 0))
    u2 = jnp.where(row == 0, uh[6:7, :], jnp.where(row == 1, uh[7:8, :], pltpu.roll(u, 2, 0)))
    return u1, u2


def _mem_head(qm, km, vm, h):
    qh, kh, vh = (a[:, h * HD:(h + 1) * HD] for a in (qm, km, vm))
    e, _ = _exp_scores(_nt(qh, kh) * (HD ** -0.5))
    return qh, kh, vh, e


def mixer_tail_fwd(x2d, attn_out, proj, qkv, km, vm, conv_w8, pk, wout, S, tm, exchange):
    T, D = x2d.shape
    NM = km.shape[0] // (T // S)

    def body(x_ref, ao_ref, ch_ref, cb_ref, cc_ref, chh_ref, cch_ref, qm_ref, km_ref, vm_ref, cw_ref, pk_ref,
             wout_ref, co_ref, mo_ref, mg_ref, x1_ref, h_ref):
        first = (pl.program_id(0) * tm) % S == 0
        u = cc_ref[...] * ch_ref[...]
        uh = jnp.where(first, 0.0, cch_ref[...] * chh_ref[...])
        u1, u2 = _conv_taps(u, uh)
        conv = cw_ref[0:1, :] * u2 + cw_ref[1:2, :] * u1 + cw_ref[2:3, :] * u + _small(pk_ref, "conv_b")
        conv_out = cb_ref[...] * conv
        co_ref[...] = conv_out
        qm, kmv, vmv = qm_ref[...], km_ref[...], vm_ref[...]
        ones = jnp.ones((NM, HD), MXU)
        for h in range(N_MEMH):
            _, _, vh, e = _mem_head(qm, kmv, vmv, h)
            eb = _c(e)
            mo_ref[:, pl.ds(h * HD, HD)] = _nn(eb, vh) * (1.0 / _nn(eb, ones))
        mem_out = mo_ref[...]
        ao = ao_ref[...]
        merged = _c(jnp.concatenate([ao * _rstd(ao) * _small(pk_ref, "out_norm_attn"),
                                     conv_out * _rstd(conv_out) * _small(pk_ref, "out_norm_conv"),
                                     mem_out * _rstd(mem_out) * _small(pk_ref, "out_norm_mem")], axis=1))
        mg_ref[...] = merged
        x1 = x_ref[...] + _nn(merged, wout_ref[...])
        x1_ref[...] = x1
        h_ref[...] = _c(x1 * _rstd(x1) * _small(pk_ref, "norm_ffn"))

    tile = lambda w, col: pl.BlockSpec((tm, w), lambda i: (i, col))
    halo = lambda col: pl.BlockSpec((8, CONV_W), lambda i: (jnp.maximum(i * (tm // 8) - 1, 0), col))
    seq = pl.BlockSpec((NM, MEM_W), lambda i: ((i * tm) // S, 0))
    small = lambda a: pl.BlockSpec(a.shape, lambda i: (0, 0))
    return _run("mixer_tail_fwd", body, (T // tm,),
                [x2d, attn_out, proj, proj, proj, proj, proj, qkv, km, vm, conv_w8, pk, wout],
                [tile(D, 0), tile(ATT_W, 0), tile(CONV_W, 3), tile(CONV_W, 4), tile(CONV_W, 5), halo(3), halo(5),
                 tile(MEM_W, 3), seq, seq, VM, VM, VM],
                [SDS((T, CONV_W), f32), SDS((T, MEM_W), f32), SDS((T, D), MXU), SDS((T, D), f32), SDS((T, D), MXU)],
                [tile(CONV_W, 0), tile(MEM_W, 0), tile(D, 0), tile(D, 0), tile(D, 0)], vmem_mib=40, exchange=exchange)


def ffn_fwd_bwd(h, x1, tgt, wgT, wuT, wd, pk, tm):
    T, D = x1.shape
    F = wd.shape[0]

    def body(h_ref, x1_ref, t_ref, wg_ref, wu_ref, wd_ref, pk_ref,
             dx1_ref, dx2_ref, act_ref, dg_ref, du_ref, loss_ref, dgf_ref):
        @pl.when(pl.program_id(0) == 0)
        def _():
            loss_ref[...] = jnp.zeros_like(loss_ref)
            dgf_ref[...] = jnp.zeros_like(dgf_ref)

        hv = h_ref[...]
        gate = _nt(hv, wg_ref[...])
        up = _nt(hv, wu_ref[...])
        sg = jax.nn.sigmoid(gate)
        sl = gate * sg
        act = _c(sl * up)
        act_ref[...] = act
        x1v = x1_ref[...]
        diff = (x1v + _nn(act, wd_ref[...])) - t_ref[...]
        loss_ref[...] += 0.5 * jnp.sum(jnp.sum(diff * diff, axis=-1, keepdims=True) / D, axis=0, keepdims=True)
        dx2 = diff / D
        dx2b = _c(dx2)
        dx2_ref[...] = dx2b
        d_act = _nt(dx2b, wd_ref[...])
        d_up = _c(d_act * sl)
        d_gate = _c(d_act * up * (sg * (1.0 + gate * (1.0 - sg))))
        du_ref[...] = d_up
        dg_ref[...] = d_gate
        dh = _nn(d_gate, wg_ref[...]) + _nn(d_up, wu_ref[...])
        dv, dgf = _norm_bwd(dh, x1v, _rstd(x1v), _small(pk_ref, "norm_ffn"))
        dx1_ref[...] = dx2 + dv
        dgf_ref[...] += dgf

    tile = lambda w: pl.BlockSpec((tm, w), lambda i: (i, 0))
    return _run("ffn_fwd_bwd", body, (T // tm,), [h, x1, tgt, wgT, wuT, wd, pk],
                [tile(D), tile(D), tile(D), VM, VM, VM, VM],
                [SDS((T, D), f32), SDS((T, D), MXU), SDS((T, F), MXU), SDS((T, F), MXU), SDS((T, F), MXU),
                 SDS((8, 128), f32), SDS((1, D), f32)],
                [tile(D), tile(D), tile(F), tile(F), tile(F), pl.BlockSpec((8, 128), lambda i: (0, 0)),
                 pl.BlockSpec((1, D), lambda i: (0, 0))], vmem_mib=56)


def matmul_tn(a, b, name, tmo, tk):
    T, M = a.shape
    N = b.shape[1]

    def body(a_ref, b_ref, o_ref):
        @pl.when(pl.program_id(1) == 0)
        def _():
            o_ref[...] = jnp.zeros_like(o_ref)

        o_ref[...] += _tn(a_ref[...], b_ref[...])

    return _run(name, body, (M // tmo, T // tk), [a, b],
                [pl.BlockSpec((tk, tmo), lambda m, k: (k, m)), pl.BlockSpec((tk, N), lambda m, k: (k, 0))],
                [SDS((M, N), f32)], [pl.BlockSpec((tmo, N), lambda m, k: (m, 0))], vmem_mib=48)[0]


def out_proj_bwd(dx1, merged, attn_out, conv_out, mem_out, pk, wout, tm):
    T, D = dx1.shape

    def body(dx1_ref, mg_ref, ao_ref, co_ref, mo_ref, pk_ref, w_ref,
             dao_ref, dco_ref, dmo_ref, dw_ref, dgain_ref):
        @pl.when(pl.program_id(0) == 0)
        def _():
            dw_ref[...] = jnp.zeros_like(dw_ref)
            dgain_ref[...] = jnp.zeros_like(dgain_ref)

        dxb = _c(dx1_ref[...])
        dw_ref[...] += _tn(mg_ref[...], dxb)
        dmg = _nt(dxb, w_ref[...])
        ao, co, mo = ao_ref[...], co_ref[...], mo_ref[...]
        da, ga = _norm_bwd(dmg[:, :ATT_W], ao, _rstd(ao), _small(pk_ref, "out_norm_attn"))
        dc, gc = _norm_bwd(dmg[:, ATT_W:ATT_W + CONV_W], co, _rstd(co), _small(pk_ref, "out_norm_conv"))
        dm, gm = _norm_bwd(dmg[:, ATT_W + CONV_W:], mo, _rstd(mo), _small(pk_ref, "out_norm_mem"))
        dao_ref[...] = da
        dco_ref[...] = dc
        dmo_ref[...] = dm
        dgain_ref[...] += jnp.concatenate([ga, gc, gm], axis=1)

    tile = lambda w: pl.BlockSpec((tm, w), lambda i: (i, 0))
    return _run("out_proj_bwd", body, (T // tm,), [dx1, merged, attn_out, conv_out, mem_out, pk, wout],
                [tile(D), tile(D), tile(ATT_W), tile(CONV_W), tile(MEM_W), VM, VM],
                [SDS((T, ATT_W), f32), SDS((T, CONV_W), f32), SDS((T, MEM_W), f32), SDS((D, D), f32), SDS((1, D), f32)],
                [tile(ATT_W), tile(CONV_W), tile(MEM_W), pl.BlockSpec((D, D), lambda i: (0, 0)),
                 pl.BlockSpec((1, D), lambda i: (0, 0))], vmem_mib=40)


def attn_bwd(qkv, d_attn, attn_out, sink_rows, BL, S, exchange):
    NB = S // BLK
    T = BL * S

    def body(q_ref, kc_ref, kp_ref, vc_ref, vp_ref, do_ref, ao_ref, sk_ref, tab_ref,
             dq_ref, dk_ref, dv_ref, dsk_ref, pend_k, pend_v):
        b, j = pl.program_id(0), pl.program_id(1)

        @pl.when((b == 0) & (j == 0))
        def _():
            dsk_ref[...] = jnp.zeros_like(dsk_ref)

        @pl.when(j == 0)
        def _():
            pend_k[...] = jnp.zeros_like(pend_k)
            pend_v[...] = jnp.zeros_like(pend_v)

        @pl.when(j < NB)
        def _():
            q, do, ao = q_ref[...], do_ref[...], ao_ref[...]
            k2 = jnp.concatenate([kp_ref[...], kc_ref[...]], axis=0)
            v2 = jnp.concatenate([vp_ref[...], vc_ref[...]], axis=0)
            lane = lax.broadcasted_iota(jnp.int32, (8, 128), 1)
            ones_w = jnp.ones((2 * BLK, 2 * BLK), MXU)
            dsk = jnp.zeros((8, 128), f32)
            dks, dvs = [], []
            for g in range(N_KV):
                kn, vh = k2[:, g * HD:(g + 1) * HD], v2[:, g * HD:(g + 1) * HD]
                qs = _stack_heads(q, g)
                s = _nt(qs, kn) * (HD ** -0.5) + tab_ref[g]
                e, es = _exp_scores(s, _sink_column(g, sk_ref))
                eb = _c(e)
                inv_w = 1.0 / (_nn(eb, ones_w) + es)
                inv_n = inv_w[:, :HD]
                dos = _stack_heads(do, g)
                delta = _rowsum_mxu(dos * _stack_heads(ao, g), 2 * BLK)
                dp = _nt(_c(dos), vh)
                ds = _c(e * inv_w * (dp - delta) * (HD ** -0.5))
                t = es * inv_n[:, 0:1] * delta[:, 0:1]
                for hh in range(GQA):
                    dsk = dsk + jnp.where(lane == g * GQA + hh, -jnp.sum(t[hh * BLK:(hh + 1) * BLK]), 0.0)
                dvs.append(_tn(eb, _c(dos * inv_n)))
                dks.append(_tn(ds, qs))
                dqs = _nn(ds, kn)
                for hh in range(GQA):
                    dq_ref[:, pl.ds((g * GQA + hh) * HD, HD)] = dqs[hh * BLK:(hh + 1) * BLK]
            dk2 = jnp.concatenate(dks, axis=1)
            dv2 = jnp.concatenate(dvs, axis=1)
            dk_ref[...] = pend_k[...] + dk2[:BLK]
            dv_ref[...] = pend_v[...] + dv2[:BLK]
            pend_k[...] = dk2[BLK:]
            pend_v[...] = dv2[BLK:]
            dsk_ref[...] += dsk

        @pl.when(j == NB)
        def _():
            dk_ref[...] = pend_k[...]
            dv_ref[...] = pend_v[...]

    cur = lambda col: (lambda b, j: (b * NB + jnp.minimum(j, NB - 1), col))
    prev = lambda col: (lambda b, j: (b * NB + jnp.maximum(j - 1, 0), col))
    small = lambda shape: pl.BlockSpec(shape, lambda b, j: (0, 0))
    return _run("attn_bwd", body, (BL, NB + 1), [qkv, qkv, qkv, qkv, qkv, d_attn, attn_out, sink_rows, _swa_bias_table()],
                [pl.BlockSpec((BLK, ATT_W), cur(0)),
                 pl.BlockSpec((BLK, KV_W), cur(4)), pl.BlockSpec((BLK, KV_W), prev(4)),
                 pl.BlockSpec((BLK, KV_W), cur(5)), pl.BlockSpec((BLK, KV_W), prev(5)),
                 pl.BlockSpec((BLK, ATT_W), cur(0)), pl.BlockSpec((BLK, ATT_W), cur(0)), small((8, 128)),
                 pl.BlockSpec((None, N_KV, GQA * BLK, 2 * BLK), lambda b, j: (jnp.minimum(j, 1), 0, 0, 0))],
                [SDS((T, ATT_W), f32), SDS((T, KV_W), f32), SDS((T, KV_W), f32), SDS((8, 128), f32)],
                [pl.BlockSpec((BLK, ATT_W), cur(0)), pl.BlockSpec((BLK, KV_W), prev(0)),
                 pl.BlockSpec((BLK, KV_W), prev(0)), small((8, 128))],
                scratch=[pltpu.VMEM((BLK, KV_W), f32)] * 2, vmem_mib=56, exchange=exchange)


def mem_conv_bwd(d_mem_out, mem_out, d_conv_out, proj, qkv, km, vm, conv_w8, pk, S, tm, exchange):
    T = d_mem_out.shape[0]
    NM = km.shape[0] // (T // S)

    def body(dmo_ref, mo_ref, dco_ref, ch_ref, cb_ref, cc_ref, chh_ref, cch_ref, qm_ref, km_ref, vm_ref, cw_ref,
             pk_ref, dqm_ref, dkm_ref, dvm_ref, dcb_ref, dcv_ref, dcw_ref, dcbias_ref):
        i = pl.program_id(0)
        first = (i * tm) % S == 0

        @pl.when(i == 0)
        def _():
            dcw_ref[...] = jnp.zeros_like(dcw_ref)
            dcbias_ref[...] = jnp.zeros_like(dcbias_ref)

        @pl.when(first)
        def _():
            dkm_ref[...] = jnp.zeros_like(dkm_ref)
            dvm_ref[...] = jnp.zeros_like(dvm_ref)

        qm, kmv, vmv, dmo, mo = qm_ref[...], km_ref[...], vm_ref[...], dmo_ref[...], mo_ref[...]
        ones_w = jnp.ones((NM, NM), MXU)
        for h in range(N_MEMH):
            qh, kh, vh, e = _mem_head(qm, kmv, vmv, h)
            eb = _c(e)
            doh = dmo[:, h * HD:(h + 1) * HD]
            delta = _rowsum_mxu(doh * mo[:, h * HD:(h + 1) * HD], NM)
            dp = _nt(_c(doh), vh)
            inv_w = 1.0 / _nn(eb, ones_w)
            ds = _c(e * inv_w * (dp - delta) * (HD ** -0.5))
            dvm_ref[:, pl.ds(h * HD, HD)] += _tn(eb, _c(doh * inv_w[:, :HD]))
            dkm_ref[:, pl.ds(h * HD, HD)] += _tn(ds, qh)
            dqm_ref[:, pl.ds(h * HD, HD)] = _nn(ds, kh)

        u = cc_ref[...] * ch_ref[...]
        uh = jnp.where(first, 0.0, cch_ref[...] * chh_ref[...])
        u1, u2 = _conv_taps(u, uh)
        conv = cw_ref[0:1, :] * u2 + cw_ref[1:2, :] * u1 + cw_ref[2:3, :] * u + _small(pk_ref, "conv_b")
        dy = dco_ref[...]
        dcb_ref[...] = dy * conv
        dcv = dy * cb_ref[...]
        dcv_ref[...] = dcv
        dcbias_ref[...] += jnp.sum(dcv, axis=0, keepdims=True)
        dcw_ref[0:1, :] += jnp.sum(dcv * u2, axis=0, keepdims=True)
        dcw_ref[1:2, :] += jnp.sum(dcv * u1, axis=0, keepdims=True)
        dcw_ref[2:3, :] += jnp.sum(dcv * u, axis=0, keepdims=True)

    tile = lambda w, col: pl.BlockSpec((tm, w), lambda i: (i, col))
    halo = lambda col: pl.BlockSpec((8, CONV_W), lambda i: (jnp.maximum(i * (tm // 8) - 1, 0), col))
    seq = pl.BlockSpec((NM, MEM_W), lambda i: ((i * tm) // S, 0))
    const = lambda shape: pl.BlockSpec(shape, lambda i: (0, 0))
    return _run("mem_conv_bwd", body, (T // tm,),
                [d_mem_out, mem_out, d_conv_out, proj, proj, proj, proj, proj, qkv, km, vm, conv_w8, pk],
                [tile(MEM_W, 0), tile(MEM_W, 0), tile(CONV_W, 0), tile(CONV_W, 3), tile(CONV_W, 4), tile(CONV_W, 5),
                 halo(3), halo(5), tile(MEM_W, 3), seq, seq, VM, VM],
                [SDS((T, MEM_W), f32), SDS(km.shape, f32), SDS(km.shape, f32),
                 SDS((T, CONV_W), f32), SDS((T, CONV_W), f32), SDS((8, CONV_W), f32), SDS((1, CONV_W), f32)],
                [tile(MEM_W, 0), seq, seq, tile(CONV_W, 0), tile(CONV_W, 0), const((8, CONV_W)), const((1, CONV_W))],
                vmem_mib=48, exchange=exchange)


def in_proj_bwd(dqn, dkn, dv, dcb, dcv, dqmn, proj, conv_w8, xn, x2d, dx1, pk, winT, S, tm, stages, ws, ms, vs):
    T, D = x2d.shape
    P = winT.shape[0]
    last_blk = T // 8 - 1
    n = len(stages)
    nsteps = T // tm
    tile_w = ws[0].shape[1] // (nsteps // 2)
    turn = [e * 2 // n for e in range(n)]

    def body(dq_ref, dk_ref, dv_ref, dcb_ref, dcv_ref, dcvn_ref, dqm_ref, qa_ref, ka_ref, ch_ref, cc_ref, qma_ref,
             cw_ref, xn_ref, x_ref, dx1_ref, pk_ref, w_ref, *rest):
        st, aw, am, av = (rest[k * n:(k + 1) * n] for k in range(4))
        dx_ref, dw_ref, dg_ref, dqg_ref, dkg_ref, dmqg_ref = rest[4 * n:4 * n + 6]
        aouts = rest[4 * n + 6:]
        i = pl.program_id(0)

        for parity in range(2):
            @pl.when(i % 2 == parity)
            def _(parity=parity):
                for e in range(n):
                    if turn[e] == parity:
                        g = jnp.concatenate([_sum_chips(st[e].at[0]), _sum_chips(st[e].at[1])], axis=0)
                        d, mm, vv = _adamw_math(aw[e][...], g, am[e][...], av[e][...])
                        for k, val in enumerate((g, d, mm, vv)):
                            aouts[4 * e + k][...] = val

        @pl.when(i == 0)
        def _():
            dw_ref[...] = jnp.zeros_like(dw_ref)
            dg_ref[...] = jnp.zeros_like(dg_ref)
            dqg_ref[...] = jnp.zeros_like(dqg_ref)
            dkg_ref[...] = jnp.zeros_like(dkg_ref)
            dmqg_ref[...] = jnp.zeros_like(dmqg_ref)

        dqa, gq = _heads_norm_bwd(dq_ref[...], qa_ref[...], _small(pk_ref, "q_norm"))
        dka, gk = _heads_norm_bwd(dk_ref[...], ka_ref[...], _small(pk_ref, "k_norm"))
        dqma, gmq = _heads_norm_bwd(dqm_ref[...], qma_ref[...], _small(pk_ref, "mem_q_norm"))
        dqg_ref[...] += gq
        dkg_ref[...] += gk
        dmqg_ref[...] += gmq

        last = ((i + 1) * tm) % S == 0
        dcv = dcv_ref[...]
        nxt = jnp.where(last, 0.0, dcvn_ref[...])
        row = lax.broadcasted_iota(jnp.int32, dcv.shape, 0)
        n1 = jnp.where(row == tm - 1, nxt[0:1, :], pltpu.roll(dcv, tm - 1, 0))
        n2 = jnp.where(row == tm - 2, nxt[0:1, :], jnp.where(row == tm - 1, nxt[1:2, :], pltpu.roll(dcv, tm - 2, 0)))
        du = cw_ref[2:3, :] * dcv + cw_ref[1:2, :] * n1 + cw_ref[0:1, :] * n2
        d_proj = jnp.concatenate([_c(dqa), _c(dka), _c(dv_ref[...]), _c(du * cc_ref[...]),
                                  _c(dcb_ref[...]), _c(du * ch_ref[...]), _c(dqma)], axis=1)
        dw_ref[...] += _tn(d_proj, xn_ref[...])
        xv = x_ref[...]
        dv_, dg = _norm_bwd(_nn(d_proj, w_ref[...]), xv, _rstd(xv), _small(pk_ref, "norm_mix"))
        dx_ref[...] = dx1_ref[...] + dv_
        dg_ref[...] += dg

    tile = lambda w, col=0: pl.BlockSpec((tm, w), lambda i: (i, col))
    nhalo = pl.BlockSpec((8, CONV_W), lambda i: (jnp.minimum((i + 1) * (tm // 8), last_blk), 0))
    const = lambda shape: pl.BlockSpec(shape, lambda i: (0, 0))
    st_specs = [pl.BlockSpec((2, 4, s.shape[2], tile_w), lambda i: (0, 0, 0, i // 2)) for s in stages]
    w_specs = [pl.BlockSpec((w.shape[0], tile_w), lambda i: (0, i // 2)) for w in ws]
    res = _run("in_proj_bwd", body, (nsteps,),
               [dqn, dkn, dv, dcb, dcv, dcv, dqmn, proj, proj, proj, proj, proj, conv_w8, xn, x2d, dx1, pk, winT]
               + list(stages) + list(ws) + list(ms) + list(vs),
               [tile(ATT_W), tile(KV_W), tile(KV_W), tile(CONV_W), tile(CONV_W), nhalo, tile(MEM_W),
                tile(ATT_W, 0), tile(KV_W, 4), tile(CONV_W, 3), tile(CONV_W, 5), tile(MEM_W, 6), VM,
                tile(D), tile(D), tile(D), VM, VM] + st_specs + w_specs * 3,
               [SDS((T, D), f32), SDS((P, D), f32), SDS((1, D), f32), SDS((1, HD), f32), SDS((1, HD), f32),
                SDS((1, HD), f32)] + [SDS(w.shape, f32) for w in ws for _ in range(4)],
               [tile(D), pl.BlockSpec((P, D), lambda i: (0, 0)), const((1, D)), const((1, HD)), const((1, HD)),
                const((1, HD))] + [s for s in w_specs for _ in range(4)],
               vmem_mib=56)
    return res[:6], [res[6 + 4 * e:10 + 4 * e] for e in range(n)]


def mem_kv_bwd(dkm, dvm, kv, memn, mem2d, pk, wmkv):
    def body(dkm_ref, dvm_ref, kv_ref, mn_ref, m_ref, pk_ref, w_ref, dw_ref, dg_ref, dkg_ref):
        dkk, dkg = _heads_norm_bwd(dkm_ref[...], kv_ref[:, :MEM_W], _small(pk_ref, "mem_k_norm"))
        dkg_ref[...] = dkg
        dkv = _c(jnp.concatenate([dkk, dvm_ref[...]], axis=1))
        dw_ref[...] = _tn(mn_ref[...], dkv)
        mv = m_ref[...]
        dg_ref[...] = jnp.sum(_nt(dkv, w_ref[...]) * mv * _rstd(mv), axis=0, keepdims=True)

    return _run("mem_kv_bwd", body, (), [dkm, dvm, kv, memn, mem2d, pk, wmkv], [VM] * 7,
                [SDS(wmkv.shape, f32), SDS((1, mem2d.shape[1]), f32), SDS((1, HD), f32)], [VM] * 3, vmem_mib=40)


def _halves_view(g):
    return g.reshape(4, 2, g.shape[0] // 8, g.shape[1])


def kernel(x, mem, norm_mix, w_in, q_norm, k_norm, attn_sinks, conv_w, conv_b, norm_mem, w_mem_kv, mem_q_norm, mem_k_norm, out_norm_attn, out_norm_conv, out_norm_mem, w_out, norm_ffn, w_gate, w_up, w_down, loss_target, m_norm_mix, m_w_in, m_q_norm, m_k_norm, m_attn_sinks, m_conv_w, m_conv_b, m_norm_mem, m_w_mem_kv, m_mem_q_norm, m_mem_k_norm, m_out_norm_attn, m_out_norm_conv, m_out_norm_mem, m_w_out, m_norm_ffn, m_w_gate, m_w_up, m_w_down, v_norm_mix, v_w_in, v_q_norm, v_k_norm, v_attn_sinks, v_conv_w, v_conv_b, v_norm_mem, v_w_mem_kv, v_mem_q_norm, v_mem_k_norm, v_out_norm_attn, v_out_norm_conv, v_out_norm_mem, v_w_out, v_norm_ffn, v_w_gate, v_w_up, v_w_down):
    BL, S, D = x.shape
    T = BL * S
    TM = 256
    TM_BIG = min(512, S)
    w_small = dict(norm_mix=norm_mix, norm_mem=norm_mem, norm_ffn=norm_ffn, out_norm_attn=out_norm_attn,
                   out_norm_conv=out_norm_conv, out_norm_mem=out_norm_mem, conv_w=conv_w, conv_b=conv_b, q_norm=q_norm,
                   k_norm=k_norm, mem_q_norm=mem_q_norm, mem_k_norm=mem_k_norm, attn_sinks=attn_sinks)
    m_small = dict(norm_mix=m_norm_mix, norm_mem=m_norm_mem, norm_ffn=m_norm_ffn, out_norm_attn=m_out_norm_attn,
                   out_norm_conv=m_out_norm_conv, out_norm_mem=m_out_norm_mem, conv_w=m_conv_w, conv_b=m_conv_b,
                   q_norm=m_q_norm, k_norm=m_k_norm, mem_q_norm=m_mem_q_norm, mem_k_norm=m_mem_k_norm,
                   attn_sinks=m_attn_sinks)
    v_small = dict(norm_mix=v_norm_mix, norm_mem=v_norm_mem, norm_ffn=v_norm_ffn, out_norm_attn=v_out_norm_attn,
                   out_norm_conv=v_out_norm_conv, out_norm_mem=v_out_norm_mem, conv_w=v_conv_w, conv_b=v_conv_b,
                   q_norm=v_q_norm, k_norm=v_k_norm, mem_q_norm=v_mem_q_norm, mem_k_norm=v_mem_k_norm,
                   attn_sinks=v_attn_sinks)
    pk = _pack_small(w_small)

    rowblocks = lambda a, b, c, d, e, f: [a[0].T, b[0].T, c[0].T, d[0], e[0], f[0]]
    w_rb = rowblocks(w_in, w_gate, w_up, w_down, w_out, w_mem_kv)
    m_rb = rowblocks(m_w_in, m_w_gate, m_w_up, m_w_down, m_w_out, m_w_mem_kv)
    v_rb = rowblocks(v_w_in, v_w_gate, v_w_up, v_w_down, v_w_out, v_w_mem_kv)
    (winT_s,) = prep_weights("prep_w_in", w_rb[:1])
    cw_pad = jnp.zeros((8, 128), f32).at[:3, :HD].set(conv_w[0])
    (wgT_s, wuT_s, wd_s, wout_s, wmkv_s), (winT, cw_all) = prep_weights(
        "gather_w_in", w_rb[1:], _together([gather_two_legs([winT_s]), gather_exchange([cw_pad], [False])]))
    conv_w_full = jnp.transpose(cw_all.reshape(4, 8, 128)[:, :3, :HD], (1, 0, 2)).reshape(3, CONV_W)
    conv_w8 = jnp.zeros((8, CONV_W), f32).at[:3].set(conv_w_full)
    sink_rows = jnp.broadcast_to(attn_sinks.reshape(N_Q, 1), (N_Q, 128))

    x2d = x.reshape(T, D)
    mem2d = mem.reshape(-1, D)
    (xn, proj, qkv), near1 = in_proj_fwd(x2d, pk, winT, TM_BIG, gather_near_exchange([wgT_s, wout_s, wmkv_s], relay_early=1))
    (attn_out,), (wgT, wout, wmkv, *near2) = attn_fwd(
        qkv, sink_rows, BL, S, _together([gather_far_exchange(near1, relay_early=2), gather_near_exchange([wuT_s, wd_s], relay_early=2)]))
    memn, kv, km, vm = mem_kv_fwd(mem2d, pk, wmkv)
    (conv_out, mem_out, merged, x1, h), (wuT, wd) = mixer_tail_fwd(
        x2d, attn_out, proj, qkv, km, vm, conv_w8, pk, wout, S, TM_BIG, gather_far_exchange(near2, relay_early=2))

    dx1, dx2b, act, d_gate, d_up, loss8, d_norm_ffn = ffn_fwd_bwd(h, x1, loss_target.reshape(T, D), wgT, wuT, wd, pk, TM)
    F = wd.shape[0]
    g_wd = matmul_tn(act, dx2b, "dw_down", F // 2, min(T, 1024))
    g_wgT = matmul_tn(d_gate, h, "dw_gate", F // 2, min(T, 1024))
    g_wuT = matmul_tn(d_up, h, "dw_up", F // 2, min(T, 1024))

    d_attn, d_conv_out, d_mem_out, g_wout, d_gains = out_proj_bwd(dx1, merged, attn_out, conv_out, mem_out, pk, wout, TM_BIG)
    dqmn, dkm, dvm, dcb, dcv, d_cw8, d_cbias = mem_conv_bwd(
        d_mem_out, mem_out, d_conv_out, proj, qkv, km, vm, conv_w8, pk, S, min(1024, S), None)
    (dqn, dkn, dv, d_sink8), (st_wout, st_wgT, st_wuT, st_wd) = attn_bwd(
        qkv, d_attn, attn_out, sink_rows, BL, S,
        reduce_scatter_exchange([_halves_view(g) for g in (g_wout, g_wgT, g_wuT, g_wd)], BL * (S // BLK + 1),
                                load_step=[0, 1, 4, 7], send_step=[1, 4, 7, 10], relay_step=[6, 16, 25, 33]))
    (g_x, g_winT, d_norm_mix, d_qg, d_kg, d_mqg), late_res = in_proj_bwd(
        dqn, dkn, dv, dcb, dcv, dqmn, proj, conv_w8, xn, x2d, dx1, pk, winT, S, TM,
        [st_wgT, st_wuT, st_wd, st_wout], w_rb[1:5], m_rb[1:5], v_rb[1:5])
    g_wmkv, d_norm_mem, d_mkg = mem_kv_bwd(dkm, dvm, kv, memn, mem2d, pk, wmkv)

    tot, tail_stage = tail_reduce(d_norm_mix, d_norm_mem, d_norm_ffn, d_gains, d_cw8, d_cbias, d_qg, d_kg, d_mqg, d_mkg,
                                  d_sink8, loss8, [_halves_view(g) for g in (g_winT, g_wmkv)])
    loss = tot[5, 384]
    tail_res, _ = adamw_big("adamw_tail", tail_stage, [w_rb[0], w_rb[5]], [m_rb[0], m_rb[5]], [v_rb[0], v_rb[5]], 4)
    res = {"w_in": [a.T[None] for a in tail_res[0]], "w_gate": [a.T[None] for a in late_res[0]],
           "w_up": [a.T[None] for a in late_res[1]], "w_down": [a[None] for a in late_res[2]],
           "w_out": [a[None] for a in late_res[3]], "w_mem_kv": [a[None] for a in tail_res[1]]}
    res.update(adamw_small(tot, pk, _pack_small(m_small), _pack_small(v_small), {k: w_small[k].shape for k in SMALL}))

    order = ["norm_mix", "w_in", "q_norm", "k_norm", "attn_sinks", "conv_w", "conv_b", "norm_mem", "w_mem_kv",
             "mem_q_norm", "mem_k_norm", "out_norm_attn", "out_norm_conv", "out_norm_mem", "w_out", "norm_ffn",
             "w_gate", "w_up", "w_down"]
    return (loss, g_x.reshape(BL, S, D), *[res[n][0] for n in order], *[res[n][1] for n in order],
            *[res[n][2] for n in order], *[res[n][3] for n in order])
```

```python
import collections
import functools

import jax
import jax.numpy as jnp
import numpy as np
from jax import lax
from jax.experimental import pallas as pl
from jax.experimental.pallas import tpu as pltpu

f32 = jnp.float32
MXU = jnp.bfloat16
WIRE = jnp.bfloat16
EPS = 1e-6
NEG = -1e30
HD = 64
BLK = 128
N_Q, N_KV, N_MEMH = 8, 2, 4
GQA = N_Q // N_KV
ATT_W, KV_W, CONV_W, MEM_W = 512, 128, 256, 256
VMEM_MIB = 1024 * 1024
ADAM_LR, ADAM_B1, ADAM_B2, ADAM_EPS, ADAM_WD, ADAM_STEP = 0.001, 0.9, 0.999, 1e-08, 0.01, 10

MESH = pl.DeviceIdType.MESH
VM = pl.BlockSpec(memory_space=pltpu.VMEM)
ANY = pl.BlockSpec(memory_space=pl.ANY)
SDS = jax.ShapeDtypeStruct
DMA = pltpu.SemaphoreType.DMA
LOCAL_THREAD = 1


def _c(v):
    return v.astype(MXU)


def _nn(a, b):
    return lax.dot_general(a, b, (((1,), (0,)), ((), ())), preferred_element_type=f32)


def _nt(a, b):
    return lax.dot_general(a, b, (((1,), (1,)), ((), ())), preferred_element_type=f32)


def _tn(a, b):
    return lax.dot_general(a, b, (((0,), (0,)), ((), ())), preferred_element_type=f32)


def _rstd(v):
    return lax.rsqrt(jnp.mean(v * v, axis=-1, keepdims=True) + EPS)


def _norm_bwd(dy, v, r, g):
    dyg = dy * g
    dv = r * dyg - v * (r * r * r) * jnp.mean(dyg * v, axis=-1, keepdims=True)
    return dv, jnp.sum(dy * v * r, axis=0, keepdims=True)


def _split3(v):
    hi = _c(v)
    r1 = v - hi.astype(f32)
    mid = _c(r1)
    return hi, mid, _c(r1 - mid.astype(f32))


def _rowsum_mxu(v, width):
    ones = jnp.ones((v.shape[1], width), MXU)
    return sum(_nn(a, ones) for a in _split3(v))


def _seg_sums(v):
    r = lax.broadcasted_iota(jnp.int32, (2 * HD, 2 * HD), 0) // HD
    c = lax.broadcasted_iota(jnp.int32, (2 * HD, 2 * HD), 1) // HD
    bd = (r == c).astype(MXU)
    outs = []
    for b in range(v.shape[1] // (2 * HD)):
        outs.append(sum(_nn(a, bd) for a in _split3(v[:, b * 2 * HD:(b + 1) * 2 * HD])))
    return outs[0] if len(outs) == 1 else jnp.concatenate(outs, axis=1)


def _lanes(g, width):
    return jnp.concatenate([g] * (width // HD), axis=1)


def _heads_rstd(v):
    return lax.rsqrt(_seg_sums(v * v) * (1.0 / HD) + EPS)


def _heads_norm_bwd(dy, v, g):
    r = _heads_rstd(v)
    gl = _lanes(g, v.shape[1])
    dyg = dy * gl
    dv = r * dyg - v * (r * r * r) * (_seg_sums(dyg * v) * (1.0 / HD))
    dgl = jnp.sum(dy * v * r, axis=0, keepdims=True)
    return dv, sum(dgl[:, s * HD:(s + 1) * HD] for s in range(v.shape[1] // HD))


def _exp_scores(s, extra=None):
    m = jnp.max(s, axis=-1, keepdims=True)
    if extra is None:
        return jnp.exp(s - m), None
    m = jnp.maximum(m, extra)
    return jnp.exp(s - m), jnp.exp(extra - m)


def _place():
    return lax.axis_index("x"), lax.axis_index("y"), lax.axis_index("c")


SMALL_AT = {"norm_mix": (0, 0, 1024), "norm_mem": (1, 0, 1024), "norm_ffn": (2, 0, 1024),
            "out_norm_attn": (3, 0, ATT_W), "out_norm_conv": (3, ATT_W, CONV_W), "out_norm_mem": (3, ATT_W + CONV_W, MEM_W),
            "conv_b": (4, 3 * CONV_W, CONV_W), "q_norm": (5, 0, HD), "k_norm": (5, HD, HD), "mem_q_norm": (5, 2 * HD, HD),
            "mem_k_norm": (5, 3 * HD, HD), "attn_sinks": (5, 256, N_Q)}
SMALL = ("norm_mix", "norm_mem", "norm_ffn", "out_norm_attn", "out_norm_conv", "out_norm_mem", "conv_w", "conv_b",
         "q_norm", "k_norm", "mem_q_norm", "mem_k_norm", "attn_sinks")


def _small(pk_ref, name):
    r, c0, w = SMALL_AT[name]
    return pk_ref[r:r + 1, c0:c0 + w]


def _pack_small(d):
    z = lambda n: jnp.zeros((1, n), f32)
    row3 = jnp.concatenate([d["out_norm_attn"], d["out_norm_conv"], d["out_norm_mem"]], axis=1)
    row4 = jnp.concatenate([d["conv_w"].reshape(1, 3 * HD), z(3 * CONV_W - 3 * HD), d["conv_b"]], axis=1)
    row5 = jnp.concatenate([d["q_norm"], d["k_norm"], d["mem_q_norm"], d["mem_k_norm"], d["attn_sinks"],
                            z(1024 - 4 * HD - N_Q)], axis=1)
    return jnp.concatenate([d["norm_mix"], d["norm_mem"], d["norm_ffn"], row3, row4, row5, z(1024), z(1024)], axis=0)


def _other_chips(x, y):
    return [(1 - x, y), (x, 1 - y), (1 - x, 1 - y)]


Exchange = collections.namedtuple("Exchange", "ins outs sems start finish relays aliases", defaults=((), {}))


def _together(exchanges):
    def bounds(key):
        at, out = 0, []
        for ex in exchanges:
            out.append((at, at + len(getattr(ex, key))))
            at += len(getattr(ex, key))
        return out

    bi, bo, bs = bounds("ins"), bounds("outs"), bounds("sems")

    def of(i, fn):
        return lambda xa, xo, xs: fn(xa[bi[i][0]:bi[i][1]], xo[bo[i][0]:bo[i][1]], xs[bs[i][0]:bs[i][1]])

    def every(name):
        fns = [of(i, getattr(ex, name)) for i, ex in enumerate(exchanges)]

        def run(xa, xo, xs):
            for fn in fns:
                fn(xa, xo, xs)
        return run

    aliases = {}
    for i, ex in enumerate(exchanges):
        aliases.update({bi[i][0] + a: bo[i][0] + o for a, o in ex.aliases.items()})
    return Exchange([a for ex in exchanges for a in ex.ins], [o for ex in exchanges for o in ex.outs],
                    [s for ex in exchanges for s in ex.sems], every("start"), every("finish"),
                    [(sbe, of(i, fn)) for i, ex in enumerate(exchanges) for sbe, fn in ex.relays], aliases)


def _run(name, body, grid, ins, in_specs, out_shape, out_specs, scratch=(), vmem_mib=32, exchange=None):
    ins, in_specs, out_shape, out_specs, scratch = list(ins), list(in_specs), list(out_shape), list(out_specs), list(scratch)
    ni, no, ns = len(ins), len(out_shape), len(scratch)
    ex = exchange
    if ex is not None:
        nxi, nxo = len(ex.ins), len(ex.outs)

    def call_body(*refs):
        if ex is None:
            body(*refs)
            return
        a, xa = refs[:ni], refs[ni:ni + nxi]
        o, xo = refs[ni + nxi:ni + nxi + no], refs[ni + nxi + no:ni + nxi + no + nxo]
        s, xs = refs[ni + nxi + no + nxo:ni + nxi + no + nxo + ns], refs[ni + nxi + no + nxo + ns:]
        if grid:
            first = functools.reduce(jnp.logical_and, [pl.program_id(d) == 0 for d in range(len(grid))])
            last = functools.reduce(jnp.logical_and, [pl.program_id(d) == grid[d] - 1 for d in range(len(grid))])
            pl.when(first)(lambda: ex.start(xa, xo, xs))
            body(*a, *o, *s)
            nsteps = functools.reduce(lambda p, q: p * q, grid)
            for before_end, fn in ex.relays:
                at = np.unravel_index(max(nsteps - 1 - before_end, 0), grid)
                here = functools.reduce(jnp.logical_and, [pl.program_id(d) == int(at[d]) for d in range(len(grid))])
                pl.when(here)(functools.partial(fn, xa, xo, xs))
            pl.when(last)(lambda: ex.finish(xa, xo, xs))
        else:
            ex.start(xa, xo, xs)
            if body is not None:
                body(*a, *o, *s)
            for _, fn in ex.relays:
                fn(xa, xo, xs)
            ex.finish(xa, xo, xs)

    kw = dict(grid=grid) if grid else {}
    if ex is not None:
        if ex.aliases:
            kw["input_output_aliases"] = {ni + i: no + o for i, o in ex.aliases.items()}
        ins, in_specs = ins + list(ex.ins), in_specs + [ANY] * nxi
        out_shape, out_specs = out_shape + list(ex.outs), out_specs + [ANY] * nxo
        scratch = scratch + list(ex.sems)
    res = pl.pallas_call(
        call_body, name=name, out_shape=out_shape, in_specs=in_specs, out_specs=out_specs, scratch_shapes=scratch,
        compiler_params=pltpu.CompilerParams(dimension_semantics=("arbitrary",) * len(grid) if grid else None,
                                             vmem_limit_bytes=vmem_mib * VMEM_MIB), **kw)(*ins)
    res = list(res)
    return (res[:no], res[no:]) if ex is not None else res


def _remote(src, dst, ssem, rsem, dev):
    return pltpu.make_async_remote_copy(src_ref=src, dst_ref=dst, send_sem=ssem, recv_sem=rsem,
                                        device_id=dev, device_id_type=MESH)


def gather_exchange(shards, split, relay_early=0):
    n = len(shards)

    def rows(ref, e, kk, half=None):
        R = shards[e].shape[0]
        if half is None:
            return ref.at[pl.ds(pl.multiple_of(kk * R, 8), R)]
        return ref.at[pl.ds(pl.multiple_of(kk * R + half * (R // 2), 8), R // 2)]

    def ici(src, dst, sm, e, j, chip_j, x, y, c):
        k = 2 * x + y
        if split[e]:
            s = src[e].at[pl.ds(pl.multiple_of(c * (shards[e].shape[0] // 2), 8), shards[e].shape[0] // 2)]
            return _remote(s, rows(dst[e], e, k, c), sm[0].at[6 * e + j], sm[1].at[6 * e + j], (*chip_j, c))
        return _remote(src[e], rows(dst[e], e, k), sm[0].at[6 * e + j], sm[1].at[6 * e + j], (*chip_j, c))

    def landed(dst, e, chip_j, c):
        kj = 2 * chip_j[0] + chip_j[1]
        return rows(dst[e], e, kj, c) if split[e] else rows(dst[e], e, kj)

    def forward(dst, sm, e, j, chip_j, x, y, c, sender_c):
        kj = 2 * chip_j[0] + chip_j[1]
        r = rows(dst[e], e, kj, sender_c)
        return _remote(r, r, sm[0].at[6 * e + 3 + j], sm[1].at[6 * e + 3 + j], (x, y, 1 - c))

    def local(src, dst, sm, e, x, y):
        return pltpu.make_async_copy(src[e], rows(dst[e], e, 2 * x + y), sm[2].at[e])

    def start(src, dst, sm):
        x, y, c = _place()
        for e in range(n):
            local(src, dst, sm, e, x, y).start(priority=LOCAL_THREAD)
            for j, chip_j in enumerate(_other_chips(x, y)):
                ici(src, dst, sm, e, j, chip_j, x, y, c).start()

    def relay(src, dst, sm):
        x, y, c = _place()
        for e in range(n):
            for j, chip_j in enumerate(_other_chips(x, y)):
                r = landed(dst, e, chip_j, c)
                _remote(r, r, sm[0].at[6 * e + j], sm[1].at[6 * e + j], (*chip_j, c)).wait_recv()
                if split[e]:
                    forward(dst, sm, e, j, chip_j, x, y, c, c).start()

    def finish(src, dst, sm):
        x, y, c = _place()
        chips = _other_chips(x, y)
        for e in range(n):
            for j, chip_j in enumerate(chips):
                if split[e]:
                    forward(dst, sm, e, j, chip_j, x, y, c, 1 - c).wait_recv()
        for e in range(n):
            for j, chip_j in enumerate(chips):
                ici(src, dst, sm, e, j, chip_j, x, y, c).wait_send()
                if split[e]:
                    forward(dst, sm, e, j, chip_j, x, y, c, c).wait_send()
            local(src, dst, sm, e, x, y).wait()

    outs = [SDS((4 * s.shape[0], s.shape[1]), s.dtype) for s in shards]
    return Exchange(list(shards), outs, [DMA((6 * n,)), DMA((6 * n,)), DMA((n,))], start, finish, [(relay_early, relay)])


def _block_rows(ref, R, kk, half, quarter=None):
    hr = R // 2
    if quarter is None:
        return ref.at[pl.ds(pl.multiple_of(kk * R + half * hr, 8), hr)]
    return ref.at[pl.ds(pl.multiple_of(kk * R + half * hr + quarter * (hr // 2), 8), hr // 2)]


def gather_near_exchange(shards, relay_early=0):
    n = len(shards)
    R = [s.shape[0] for s in shards]

    def ici(src, dst, sm, e, j, chip_j, x, y, c):
        half = src[e].at[pl.ds(pl.multiple_of(c * (R[e] // 2), 8), R[e] // 2)]
        return _remote(half, _block_rows(dst[e], R[e], 2 * x + y, c), sm[0].at[4 * e + j], sm[1].at[4 * e + j], (*chip_j, c))

    def forward(dst, sm, e, j, chip_j, x, y, c, sender_c):
        r = _block_rows(dst[e], R[e], 2 * chip_j[0] + chip_j[1], sender_c)
        return _remote(r, r, sm[0].at[4 * e + 2 + j], sm[1].at[4 * e + 2 + j], (x, y, 1 - c))

    def local(src, dst, sm, e, x, y):
        return pltpu.make_async_copy(src[e], dst[e].at[pl.ds(pl.multiple_of((2 * x + y) * R[e], 8), R[e])], sm[2].at[e])

    def start(src, dst, sm):
        x, y, c = _place()
        for e in range(n):
            local(src, dst, sm, e, x, y).start(priority=LOCAL_THREAD)
            for j, chip_j in enumerate(_other_chips(x, y)[:2]):
                ici(src, dst, sm, e, j, chip_j, x, y, c).start()

    def relay(src, dst, sm):
        x, y, c = _place()
        for e in range(n):
            for j, chip_j in enumerate(_other_chips(x, y)[:2]):
                r = _block_rows(dst[e], R[e], 2 * chip_j[0] + chip_j[1], c)
                _remote(r, r, sm[0].at[4 * e + j], sm[1].at[4 * e + j], (*chip_j, c)).wait_recv()
                forward(dst, sm, e, j, chip_j, x, y, c, c).start()

    def finish(src, dst, sm):
        x, y, c = _place()
        near = _other_chips(x, y)[:2]
        for e in range(n):
            for j, chip_j in enumerate(near):
                forward(dst, sm, e, j, chip_j, x, y, c, 1 - c).wait_recv()
        for e in range(n):
            for j, chip_j in enumerate(near):
                ici(src, dst, sm, e, j, chip_j, x, y, c).wait_send()
                forward(dst, sm, e, j, chip_j, x, y, c, c).wait_send()
            local(src, dst, sm, e, x, y).wait()

    outs = [SDS((4 * s.shape[0], s.shape[1]), s.dtype) for s in shards]
    return Exchange(list(shards), outs, [DMA((4 * n,)), DMA((4 * n,)), DMA((n,))], start, finish, [(relay_early, relay)])


def gather_far_exchange(bufs, relay_early=0):
    n = len(bufs)
    R = [b.shape[0] // 4 for b in bufs]

    def send(src, dst, sm, e, j, x, y, c):
        to, of = _other_chips(x, y)[j], _other_chips(x, y)[1 - j]
        kk = 2 * of[0] + of[1]
        return _remote(_block_rows(src[e], R[e], kk, c, j), _block_rows(dst[e], R[e], kk, c, j),
                       sm[0].at[4 * e + j], sm[1].at[4 * e + j], (*to, c))

    def landed(dst, e, j, x, y, half):
        return _block_rows(dst[e], R[e], 2 * (1 - x) + (1 - y), half, j)

    def forward(dst, sm, e, j, x, y, c, sender_c):
        r = landed(dst, e, j, x, y, sender_c)
        return _remote(r, r, sm[0].at[4 * e + 2 + j], sm[1].at[4 * e + 2 + j], (x, y, 1 - c))

    def start(src, dst, sm):
        x, y, c = _place()
        for e in range(n):
            for j in range(2):
                send(src, dst, sm, e, j, x, y, c).start()

    def relay(src, dst, sm):
        x, y, c = _place()
        for e in range(n):
            for j in range(2):
                r = landed(dst, e, j, x, y, c)
                _remote(r, r, sm[0].at[4 * e + j], sm[1].at[4 * e + j], (*_other_chips(x, y)[j], c)).wait_recv()
                forward(dst, sm, e, j, x, y, c, c).start()

    def finish(src, dst, sm):
        x, y, c = _place()
        for e in range(n):
            for j in range(2):
                forward(dst, sm, e, j, x, y, c, 1 - c).wait_recv()
        for e in range(n):
            for j in range(2):
                send(src, dst, sm, e, j, x, y, c).wait_send()
                forward(dst, sm, e, j, x, y, c, c).wait_send()

    outs = [SDS(b.shape, b.dtype) for b in bufs]
    return Exchange(list(bufs), outs, [DMA((4 * n,)), DMA((4 * n,))], start, finish, [(relay_early, relay)],
                    {i: i for i in range(n)})


def gather_two_legs(shards):
    near = gather_near_exchange(shards)
    far = gather_far_exchange(near.outs)

    def finish(src, dst, sm):
        near.relays[0][1](src, dst, sm[:3])
        near.finish(src, dst, sm[:3])
        far.start(dst, dst, sm[3:])
        far.relays[0][1](dst, dst, sm[3:])
        far.finish(dst, dst, sm[3:])

    return Exchange(near.ins, near.outs, list(near.sems) + list(far.sems),
                    lambda src, dst, sm: near.start(src, dst, sm[:3]), finish)


def scatter_exchange(parts, relay_before_end=None, want_issue=False):
    n = len(parts)
    by_entry = relay_before_end is not None
    relay_before_end = relay_before_end or [0] * n

    def ici(p, st, sm, e, j, chip_j, x, y, c):
        k, kj = 2 * x + y, 2 * chip_j[0] + chip_j[1]
        return _remote(p[e].at[kj], st[e].at[c, k], sm[0].at[8 * e + j], sm[1].at[8 * e + j], (*chip_j, c))

    def own(p, st, sm, e, x, y, c):
        k = 2 * x + y
        return _remote(p[e].at[k], st[e].at[c, k], sm[0].at[8 * e + 3], sm[1].at[8 * e + 3], (x, y, 1 - c))

    def forward(st, sm, e, j, chip_j, x, y, c, sender_c):
        kj = 2 * chip_j[0] + chip_j[1]
        r = st[e].at[sender_c, kj]
        return _remote(r, r, sm[0].at[8 * e + 4 + j], sm[1].at[8 * e + 4 + j], (x, y, 1 - c))

    def local(p, st, sm, e, x, y, c):
        k = 2 * x + y
        return pltpu.make_async_copy(p[e].at[k], st[e].at[c, k], sm[2].at[e])

    def issue(e, p, st, sm):
        x, y, c = _place()
        for j, chip_j in enumerate(_other_chips(x, y)):
            ici(p, st, sm, e, j, chip_j, x, y, c).start()
        local(p, st, sm, e, x, y, c).start(priority=LOCAL_THREAD)
        own(p, st, sm, e, x, y, c).start()

    def start(p, st, sm, before_slot=None):
        x, y, c = _place()
        if by_entry:
            for e in range(n):
                issue(e, p, st, sm)
            return
        for j, chip_j in enumerate(_other_chips(x, y)):
            if before_slot is not None:
                before_slot(j, 2 * chip_j[0] + chip_j[1])
            for e in range(n):
                ici(p, st, sm, e, j, chip_j, x, y, c).start()
        if before_slot is not None:
            before_slot(3, 2 * x + y)
        for e in range(n):
            local(p, st, sm, e, x, y, c).start(priority=LOCAL_THREAD)
            own(p, st, sm, e, x, y, c).start()

    def relay(e, p, st, sm):
        x, y, c = _place()
        for j, chip_j in enumerate(_other_chips(x, y)):
            kj = 2 * chip_j[0] + chip_j[1]
            r = st[e].at[c, kj]
            _remote(r, r, sm[0].at[8 * e + j], sm[1].at[8 * e + j], (*chip_j, c)).wait_recv()
            forward(st, sm, e, j, chip_j, x, y, c, c).start()

    def finish(p, st, sm):
        x, y, c = _place()
        k = 2 * x + y
        chips = _other_chips(x, y)
        for e in range(n):
            r = st[e].at[1 - c, k]
            _remote(r, r, sm[0].at[8 * e + 3], sm[1].at[8 * e + 3], (x, y, 1 - c)).wait_recv()
            for j, chip_j in enumerate(chips):
                forward(st, sm, e, j, chip_j, x, y, c, 1 - c).wait_recv()
        for e in range(n):
            own(p, st, sm, e, x, y, c).wait_send()
            for j, chip_j in enumerate(chips):
                ici(p, st, sm, e, j, chip_j, x, y, c).wait_send()
                forward(st, sm, e, j, chip_j, x, y, c, c).wait_send()
            local(p, st, sm, e, x, y, c).wait()

    outs = [SDS((2,) + a.shape, a.dtype) for a in parts]
    ex = Exchange(list(parts), outs, [DMA((8 * n,)), DMA((8 * n,)), DMA((n,))], start, finish,
                  [(relay_before_end[e], functools.partial(relay, e)) for e in range(n)])
    return (ex, issue) if want_issue else ex


def reduce_scatter_exchange(grads, nsteps, load_step, send_step, relay_step):
    n = len(grads)
    hrs = [g.shape[2] for g in grads]
    C = grads[0].shape[3]
    scatter, issue = scatter_exchange([SDS((4,) + g.shape[2:], WIRE) for g in grads], want_issue=True)
    hand_on = [fn for _, fn in scatter.relays]

    def refs(xs):
        return xs[:3], xs[3], xs[4], xs[5], xs[6], xs[7:7 + n], xs[7 + n:]

    def push(e, g, psem, qsem, sib_st):
        x, y, c = _place()
        return _remote(g[e].at[:, 1 - c], sib_st[e], psem.at[e], qsem.at[e], (x, y, 1 - c))

    def fetch(e, g, lsem, own_st):
        _, _, c = _place()
        return pltpu.make_async_copy(g[e].at[:, c], own_st.at[e % 2, :, pl.ds(0, hrs[e])], lsem.at[e])

    def start(g, xo, xs):
        _, _, psem, qsem, _, sib_st, _ = refs(xs)
        for e in range(n):
            push(e, g, psem, qsem, sib_st).start()

    def load(e, g, xo, xs):
        _, lsem, _, _, own_st, _, _ = refs(xs)
        fetch(e, g, lsem, own_st).start(priority=LOCAL_THREAD)

    def send(e, g, xo, xs):
        sm, lsem, psem, qsem, own_st, sib_st, part = refs(xs)
        fetch(e, g, lsem, own_st).wait()
        push(e, g, psem, qsem, sib_st).wait_recv()
        part[e][...] = (own_st[e % 2, :, 0:hrs[e]] + sib_st[e][...]).astype(WIRE)
        issue(e, part, xo, sm)

    def relay(e, g, xo, xs):
        sm, _, _, _, _, _, part = refs(xs)
        hand_on[e](part, xo, sm)

    def finish(g, xo, xs):
        sm, _, psem, qsem, _, sib_st, part = refs(xs)
        scatter.finish(part, xo, sm)
        for e in range(n):
            push(e, g, psem, qsem, sib_st).wait_send()

    plan = sorted([(min(step[e], nsteps - 1), phase, e) for phase, step in enumerate((load_step, send_step, relay_step))
                   for e in range(n)])
    stage = (load, send, relay)
    relays = [(nsteps - 1 - at, functools.partial(stage[phase], e)) for at, phase, e in plan]
    scratch = (list(scatter.sems) + [DMA((n,)), DMA((n,)), DMA((n,))] + [pltpu.VMEM((2, 4, max(hrs), C), f32)]
               + [pltpu.VMEM((4, hr, C), f32) for hr in hrs] + [pltpu.VMEM((4, hr, C), WIRE) for hr in hrs])
    return Exchange(list(grads), scatter.outs, scratch, start, finish, relays)


def tail_reduce(d_norm_mix, d_norm_mem, d_norm_ffn, d_gains, d_cw8, d_cbias, d_qg, d_kg, d_mqg, d_mkg, d_sink8, loss8, tail):
    n = len(tail)
    scatter = scatter_exchange([SDS((4,) + a.shape[2:], WIRE) for a in tail])

    def half_copy(g, sib, hsem, e, j, slot, x, y, c):
        return _remote(g[e].at[slot, 1 - c], sib[e].at[slot], hsem[0].at[4 * e + j], hsem[1].at[4 * e + j], (x, y, 1 - c))

    def body(nm_ref, nmem_ref, nf_ref, gn_ref, cw_ref, cb_ref, qg_ref, kg_ref, mqg_ref, mkg_ref, sk_ref, ls_ref, *rest):
        g, o_ref, st = rest[:n], rest[n], rest[n + 1:2 * n + 1]
        buf, ssem, rsem = rest[2 * n + 1:2 * n + 4]
        own, sib, part = (rest[2 * n + 4 + i * n:2 * n + 4 + (i + 1) * n] for i in range(3))
        lsem = rest[5 * n + 4]
        hsem, xsem = rest[5 * n + 5:5 * n + 7], rest[5 * n + 7:]
        x, y, c = _place()
        loads = [pltpu.make_async_copy(g[e].at[:, c], own[e], lsem.at[e]) for e in range(n)]
        for ld in loads:
            ld.start(priority=LOCAL_THREAD)
        for j, slot in enumerate([2 * cx + cy for cx, cy in _other_chips(x, y)] + [2 * x + y]):
            for e in range(n):
                half_copy(g, sib, hsem, e, j, slot, x, y, c).start()
        me = 4 * x + 2 * y + c
        mine = buf.at[me]
        mine[...] = jnp.zeros((8, 1024), f32)
        mine[0:1, :] = nm_ref[...]
        mine[1:2, :] = nmem_ref[...]
        mine[2:3, :] = nf_ref[...]
        mine[3:4, :] = gn_ref[...]
        for j in range(3):
            mine[4:5, pl.ds(j * CONV_W, CONV_W)] = cw_ref[j:j + 1, :]
        mine[4:5, pl.ds(3 * CONV_W, CONV_W)] = cb_ref[...]
        for j, r in enumerate((qg_ref, kg_ref, mqg_ref, mkg_ref)):
            mine[5:6, pl.ds(j * HD, HD)] = r[...]
        mine[5:6, pl.ds(256, 128)] = sk_ref[0:1, :]
        mine[5:6, pl.ds(384, 128)] = ls_ref[0:1, :]

        def peer_of(m):
            return (1 - x if m & 4 else x, 1 - y if m & 2 else y, 1 - c if m & 1 else c)

        for m in range(1, 8):
            _remote(mine, mine, ssem.at[m - 1], rsem.at[m - 1], peer_of(m)).start()
        for ld in loads:
            ld.wait()

        def chip_partial(j, slot):
            for e in range(n):
                half_copy(g, sib, hsem, e, j, slot, x, y, c).wait()
                part[e][slot] = (own[e][slot] + sib[e][slot]).astype(WIRE)

        scatter.start(part, st, xsem, chip_partial)
        for _, hand_on in scatter.relays:
            hand_on(part, st, xsem)
        scatter.finish(part, st, xsem)
        for m in range(1, 8):
            p = peer_of(m)
            got = buf.at[4 * p[0] + 2 * p[1] + p[2]]
            _remote(got, got, ssem.at[m - 1], rsem.at[m - 1], p).wait_recv()
        for m in range(1, 8):
            _remote(mine, mine, ssem.at[m - 1], rsem.at[m - 1], peer_of(m)).wait_send()
        acc = buf[0]
        for d in range(1, 8):
            acc = acc + buf[d]
        o_ref[...] = acc

    ins = [d_norm_mix, d_norm_mem, d_norm_ffn, d_gains, d_cw8, d_cbias, d_qg, d_kg, d_mqg, d_mkg, d_sink8, loss8]
    half_shape = [(4,) + a.shape[2:] for a in tail]
    scratch = ([pltpu.VMEM((8, 8, 1024), f32), DMA((7,)), DMA((7,))]
               + [pltpu.VMEM(s, f32) for s in half_shape] * 2 + [pltpu.VMEM(s, WIRE) for s in half_shape]
               + [DMA((n,)), DMA((4 * n,)), DMA((4 * n,))] + list(scatter.sems))
    res = _run("tail_reduce", body, (), ins + list(tail), [VM] * len(ins) + [ANY] * n,
               [SDS((8, 1024), f32)] + list(scatter.outs), [VM] + [ANY] * n, scratch=scratch, vmem_mib=40)
    return res[0], res[1:]


def _adamw_math(w, g, m, v):
    m = ADAM_B1 * m + (1.0 - ADAM_B1) * g
    v = ADAM_B2 * v + (1.0 - ADAM_B2) * (g * g)
    m_hat = m / (1.0 - ADAM_B1 ** ADAM_STEP)
    v_hat = v / (1.0 - ADAM_B2 ** ADAM_STEP)
    delta = -ADAM_LR * (m_hat / (jnp.sqrt(v_hat) + ADAM_EPS) + ADAM_WD * w)
    return delta, m, v


def _sum_chips(st):
    return ((st[0].astype(f32) + st[1].astype(f32)) + st[2].astype(f32)) + st[3].astype(f32)


def adamw_big(name, stages, ws, ms, vs, nstep, exchange=None):
    n = len(stages)

    def body(*refs):
        st, w, m, v = refs[:n], refs[n:2 * n], refs[2 * n:3 * n], refs[3 * n:4 * n]
        outs = refs[4 * n:]
        for e in range(n):
            g = jnp.concatenate([_sum_chips(st[e].at[0]), _sum_chips(st[e].at[1])], axis=0)
            d, mm, vv = _adamw_math(w[e][...], g, m[e][...], v[e][...])
            outs[4 * e][...] = g
            outs[4 * e + 1][...] = d
            outs[4 * e + 2][...] = mm
            outs[4 * e + 3][...] = vv

    st_specs, w_specs = [], []
    for e in range(n):
        _, _, hr, C = stages[e].shape
        st_specs.append(pl.BlockSpec((2, 4, hr, C // nstep), lambda i: (0, 0, 0, i)))
        w_specs.append(pl.BlockSpec((2 * hr, C // nstep), lambda i: (0, i)))
    out_specs = [s for s in w_specs for _ in range(4)]
    out_shape = [SDS(w.shape, f32) for w in ws for _ in range(4)]
    res = _run(name, body, (nstep,), list(stages) + list(ws) + list(ms) + list(vs), st_specs + w_specs * 3,
               out_shape, out_specs, vmem_mib=16, exchange=exchange)
    res, sent = res if exchange is not None else (res, None)
    return [res[4 * e:4 * e + 4] for e in range(n)], sent


def adamw_small(tot, pk_w, pk_m, pk_v, shapes):
    def body(tot_ref, w_ref, m_ref, v_ref, *outs):
        x, y, _ = _place()
        chip = 2 * x + y
        taps = []
        for j in range(3):
            mine = tot_ref[4:5, j * CONV_W:j * CONV_W + HD]
            for s in range(1, 4):
                mine = jnp.where(chip == s, tot_ref[4:5, j * CONV_W + s * HD:j * CONV_W + (s + 1) * HD], mine)
            taps.append(mine)
        row4 = jnp.concatenate(taps + [jnp.zeros((1, 3 * CONV_W - 3 * HD), f32), tot_ref[4:5, 3 * CONV_W:]], axis=1)
        tot_v = tot_ref[...]
        row = lax.broadcasted_iota(jnp.int32, tot_v.shape, 0)
        g = jnp.where(row == 4, jnp.broadcast_to(row4, tot_v.shape), tot_v)
        d, mm, vv = _adamw_math(w_ref[...], g, m_ref[...], v_ref[...])
        for i, name in enumerate(SMALL):
            for k, val in enumerate((g, d, mm, vv)):
                if name == "conv_w":
                    outs[4 * i + k][...] = jnp.concatenate([val[4:5, j * HD:(j + 1) * HD] for j in range(3)], axis=0)[None]
                else:
                    r, c0, w = SMALL_AT[name]
                    outs[4 * i + k][...] = val[r:r + 1, c0:c0 + w]

    out_shape = [SDS(shapes[k], f32) for k in SMALL for _ in range(4)]
    res = _run("adamw_small", body, (), [tot, pk_w, pk_m, pk_v], [VM] * 4, out_shape, [VM] * len(out_shape))
    return {k: res[4 * i:4 * i + 4] for i, k in enumerate(SMALL)}


def prep_weights(name, shards, exchange=None):
    n = len(shards)

    def body(*refs):
        for e in range(n):
            refs[n + e][...] = _c(refs[e][...])

    return _run(name, body, (), shards, [VM] * n, [SDS(a.shape, MXU) for a in shards], [VM] * n, vmem_mib=16, exchange=exchange)


def mem_kv_fwd(mem2d, pk, wmkv):
    M, D = mem2d.shape

    def body(m_ref, pk_ref, w_ref, mn_ref, kv_ref, km_ref, vm_ref):
        m = m_ref[...]
        mn = _c(m * _rstd(m) * _small(pk_ref, "norm_mem"))
        mn_ref[...] = mn
        kv = _nn(mn, w_ref[...])
        kv_ref[...] = kv
        kk = kv[:, :MEM_W]
        km_ref[...] = _c(kk * _heads_rstd(kk) * _lanes(_small(pk_ref, "mem_k_norm"), MEM_W))
        vm_ref[...] = _c(kv[:, MEM_W:])

    return _run("mem_kv_fwd", body, (), [mem2d, pk, wmkv], [VM] * 3,
                [SDS((M, D), MXU), SDS((M, 2 * MEM_W), f32), SDS((M, MEM_W), MXU), SDS((M, MEM_W), MXU)], [VM] * 4)


QKV_W = ATT_W + 2 * KV_W + MEM_W


def in_proj_fwd(x2d, pk, winT, tm, exchange):
    T, D = x2d.shape
    P = winT.shape[0]

    def body(x_ref, pk_ref, w_ref, xn_ref, proj_ref, qkv_ref):
        xv = x_ref[...]
        xn = _c(xv * _rstd(xv) * _small(pk_ref, "norm_mix"))
        xn_ref[...] = xn
        proj = _nt(xn, w_ref[...])
        proj_ref[...] = proj
        q, k = proj[:, :ATT_W], proj[:, ATT_W:ATT_W + KV_W]
        qm = proj[:, P - MEM_W:]
        qkv_ref[...] = jnp.concatenate(
            [_c(q * _heads_rstd(q) * _lanes(_small(pk_ref, "q_norm"), ATT_W)),
             _c(k * _heads_rstd(k) * _lanes(_small(pk_ref, "k_norm"), KV_W)),
             _c(proj[:, ATT_W + KV_W:ATT_W + 2 * KV_W]),
             _c(qm * _heads_rstd(qm) * _lanes(_small(pk_ref, "mem_q_norm"), MEM_W))], axis=1)

    return _run("in_proj_fwd", body, (T // tm,), [x2d, pk, winT],
                [pl.BlockSpec((tm, D), lambda i: (i, 0)), VM, VM],
                [SDS((T, D), MXU), SDS((T, P), f32), SDS((T, QKV_W), MXU)],
                [pl.BlockSpec((tm, D), lambda i: (i, 0)), pl.BlockSpec((tm, P), lambda i: (i, 0)),
                 pl.BlockSpec((tm, QKV_W), lambda i: (i, 0))],
                vmem_mib=40, exchange=exchange)


def _swa_bias_table():
    r = np.arange(GQA * BLK)[:, None]
    k = np.arange(2 * BLK)[None, :]
    dist = (r % BLK) + BLK - k
    band = (dist >= 0) & (dist < BLK)
    tab = np.empty((2, N_KV, GQA * BLK, 2 * BLK), np.float32)
    for later in range(2):
        valid = band & ((k >= BLK) | (later == 1))
        for g in range(N_KV):
            slope = 2.0 ** -(g * GQA + r // BLK + 1.0)
            tab[later, g] = np.where(valid, -slope * dist, NEG)
    return jnp.asarray(tab)


def _sink_column(g, sk_ref):
    hrow = lax.broadcasted_iota(jnp.int32, (GQA * BLK, 1), 0) // BLK
    sink = jnp.zeros((GQA * BLK, 1), f32)
    for hh in range(GQA):
        sink = jnp.where(hrow == hh, sk_ref[g * GQA + hh:g * GQA + hh + 1, 0:1], sink)
    return sink


def _stack_heads(v, g):
    return jnp.concatenate([v[:, (g * GQA + hh) * HD:(g * GQA + hh + 1) * HD] for hh in range(GQA)], axis=0)


def attn_fwd(qkv, sink_rows, BL, S, exchange, qb=2):
    NS = S // (qb * BLK)
    T = BL * S

    def body(q_ref, kc_ref, kp_ref, vc_ref, vp_ref, sk_ref, tab_ref, o_ref):
        j = pl.program_id(1)
        kall = jnp.concatenate([kp_ref[...], kc_ref[...]], axis=0)
        vall = jnp.concatenate([vp_ref[...], vc_ref[...]], axis=0)
        ones = jnp.ones((2 * BLK, HD), MXU)
        for b in range(qb):
            q = q_ref[pl.ds(b * BLK, BLK), :]
            k2, v2 = kall[b * BLK:(b + 2) * BLK], vall[b * BLK:(b + 2) * BLK]
            later = jnp.minimum(j, 1) if b == 0 else 1
            for g in range(N_KV):
                kn, vh = k2[:, g * HD:(g + 1) * HD], v2[:, g * HD:(g + 1) * HD]
                s = _nt(_stack_heads(q, g), kn) * (HD ** -0.5) + tab_ref[later, g]
                e, es = _exp_scores(s, _sink_column(g, sk_ref))
                eb = _c(e)
                o = _nn(eb, vh) * (1.0 / (_nn(eb, ones) + es))
                for hh in range(GQA):
                    o_ref[pl.ds(b * BLK, BLK), pl.ds((g * GQA + hh) * HD, HD)] = o[hh * BLK:(hh + 1) * BLK]

    cur = lambda col: (lambda b, j: (b * NS + j, col))
    prev = lambda col: (lambda b, j: (qb * (b * NS + j) - jnp.minimum(j, 1), col))
    return _run("attn_fwd", body, (BL, NS), [qkv, qkv, qkv, qkv, qkv, sink_rows, _swa_bias_table()],
                [pl.BlockSpec((qb * BLK, ATT_W), cur(0)),
                 pl.BlockSpec((qb * BLK, KV_W), cur(4)), pl.BlockSpec((BLK, KV_W), prev(4)),
                 pl.BlockSpec((qb * BLK, KV_W), cur(5)), pl.BlockSpec((BLK, KV_W), prev(5)),
                 pl.BlockSpec((8, 128), lambda b, j: (0, 0)), VM],
                [SDS((T, ATT_W), f32)], [pl.BlockSpec((qb * BLK, ATT_W), cur(0))], exchange=exchange)


def _conv_taps(u, uh):
    row = lax.broadcasted_iota(jnp.int32, u.shape, 0)
    u1 = jnp.where(row == 0, uh[7:8, :], pltpu.roll(u, 1, 0))
    u2 = jnp.where(row == 0, uh[6:7, :], jnp.where(row == 1, uh[7:8, :], pltpu.roll(u, 2, 0)))
    return u1, u2


def _mem_head(qm, km, vm, h):
    qh, kh, vh = (a[:, h * HD:(h + 1) * HD] for a in (qm, km, vm))
    e, _ = _exp_scores(_nt(qh, kh) * (HD ** -0.5))
    return qh, kh, vh, e


def mixer_tail_fwd(x2d, attn_out, proj, qkv, km, vm, conv_w8, pk, wout, S, tm, exchange):
    T, D = x2d.shape
    NM = km.shape[0] // (T // S)

    def body(x_ref, ao_ref, ch_ref, cb_ref, cc_ref, chh_ref, cch_ref, qm_ref, km_ref, vm_ref, cw_ref, pk_ref,
             wout_ref, co_ref, mo_ref, mg_ref, x1_ref, h_ref):
        first = (pl.program_id(0) * tm) % S == 0
        u = cc_ref[...] * ch_ref[...]
        uh = jnp.where(first, 0.0, cch_ref[...] * chh_ref[...])
        u1, u2 = _conv_taps(u, uh)
        conv = cw_ref[0:1, :] * u2 + cw_ref[1:2, :] * u1 + cw_ref[2:3, :] * u + _small(pk_ref, "conv_b")
        conv_out = cb_ref[...] * conv
        co_ref[...] = conv_out
        qm, kmv, vmv = qm_ref[...], km_ref[...], vm_ref[...]
        ones = jnp.ones((NM, HD), MXU)
        for h in range(N_MEMH):
            _, _, vh, e = _mem_head(qm, kmv, vmv, h)
            eb = _c(e)
            mo_ref[:, pl.ds(h * HD, HD)] = _nn(eb, vh) * (1.0 / _nn(eb, ones))
        mem_out = mo_ref[...]
        ao = ao_ref[...]
        merged = _c(jnp.concatenate([ao * _rstd(ao) * _small(pk_ref, "out_norm_attn"),
                                     conv_out * _rstd(conv_out) * _small(pk_ref, "out_norm_conv"),
                                     mem_out * _rstd(mem_out) * _small(pk_ref, "out_norm_mem")], axis=1))
        mg_ref[...] = merged
        x1 = x_ref[...] + _nn(merged, wout_ref[...])
        x1_ref[...] = x1
        h_ref[...] = _c(x1 * _rstd(x1) * _small(pk_ref, "norm_ffn"))

    tile = lambda w, col: pl.BlockSpec((tm, w), lambda i: (i, col))
    halo = lambda col: pl.BlockSpec((8, CONV_W), lambda i: (jnp.maximum(i * (tm // 8) - 1, 0), col))
    seq = pl.BlockSpec((NM, MEM_W), lambda i: ((i * tm) // S, 0))
    small = lambda a: pl.BlockSpec(a.shape, lambda i: (0, 0))
    return _run("mixer_tail_fwd", body, (T // tm,),
                [x2d, attn_out, proj, proj, proj, proj, proj, qkv, km, vm, conv_w8, pk, wout],
                [tile(D, 0), tile(ATT_W, 0), tile(CONV_W, 3), tile(CONV_W, 4), tile(CONV_W, 5), halo(3), halo(5),
                 tile(MEM_W, 3), seq, seq, VM, VM, VM],
                [SDS((T, CONV_W), f32), SDS((T, MEM_W), f32), SDS((T, D), MXU), SDS((T, D), f32), SDS((T, D), MXU)],
                [tile(CONV_W, 0), tile(MEM_W, 0), tile(D, 0), tile(D, 0), tile(D, 0)], vmem_mib=40, exchange=exchange)


def ffn_fwd_bwd(h, x1, tgt, wgT, wuT, wd, pk, tm):
    T, D = x1.shape
    F = wd.shape[0]

    def body(h_ref, x1_ref, t_ref, wg_ref, wu_ref, wd_ref, pk_ref,
             dx1_ref, dx2_ref, act_ref, dg_ref, du_ref, loss_ref, dgf_ref):
        @pl.when(pl.program_id(0) == 0)
        def _():
            loss_ref[...] = jnp.zeros_like(loss_ref)
            dgf_ref[...] = jnp.zeros_like(dgf_ref)

        hv = h_ref[...]
        gate = _nt(hv, wg_ref[...])
        up = _nt(hv, wu_ref[...])
        sg = jax.nn.sigmoid(gate)
        sl = gate * sg
        act = _c(sl * up)
        act_ref[...] = act
        x1v = x1_ref[...]
        diff = (x1v + _nn(act, wd_ref[...])) - t_ref[...]
        loss_ref[...] += 0.5 * jnp.sum(jnp.sum(diff * diff, axis=-1, keepdims=True) / D, axis=0, keepdims=True)
        dx2 = diff / D
        dx2b = _c(dx2)
        dx2_ref[...] = dx2b
        d_act = _nt(dx2b, wd_ref[...])
        d_up = _c(d_act * sl)
        d_gate = _c(d_act * up * (sg * (1.0 + gate * (1.0 - sg))))
        du_ref[...] = d_up
        dg_ref[...] = d_gate
        dh = _nn(d_gate, wg_ref[...]) + _nn(d_up, wu_ref[...])
        dv, dgf = _norm_bwd(dh, x1v, _rstd(x1v), _small(pk_ref, "norm_ffn"))
        dx1_ref[...] = dx2 + dv
        dgf_ref[...] += dgf

    tile = lambda w: pl.BlockSpec((tm, w), lambda i: (i, 0))
    return _run("ffn_fwd_bwd", body, (T // tm,), [h, x1, tgt, wgT, wuT, wd, pk],
                [tile(D), tile(D), tile(D), VM, VM, VM, VM],
                [SDS((T, D), f32), SDS((T, D), MXU), SDS((T, F), MXU), SDS((T, F), MXU), SDS((T, F), MXU),
                 SDS((8, 128), f32), SDS((1, D), f32)],
                [tile(D), tile(D), tile(F), tile(F), tile(F), pl.BlockSpec((8, 128), lambda i: (0, 0)),
                 pl.BlockSpec((1, D), lambda i: (0, 0))], vmem_mib=56)


def matmul_tn(a, b, name, tmo, tk):
    T, M = a.shape
    N = b.shape[1]

    def body(a_ref, b_ref, o_ref):
        @pl.when(pl.program_id(1) == 0)
        def _():
            o_ref[...] = jnp.zeros_like(o_ref)

        o_ref[...] += _tn(a_ref[...], b_ref[...])

    return _run(name, body, (M // tmo, T // tk), [a, b],
                [pl.BlockSpec((tk, tmo), lambda m, k: (k, m)), pl.BlockSpec((tk, N), lambda m, k: (k, 0))],
                [SDS((M, N), f32)], [pl.BlockSpec((tmo, N), lambda m, k: (m, 0))], vmem_mib=48)[0]


def out_proj_bwd(dx1, merged, attn_out, conv_out, mem_out, pk, wout, tm):
    T, D = dx1.shape

    def body(dx1_ref, mg_ref, ao_ref, co_ref, mo_ref, pk_ref, w_ref,
             dao_ref, dco_ref, dmo_ref, dw_ref, dgain_ref):
        @pl.when(pl.program_id(0) == 0)
        def _():
            dw_ref[...] = jnp.zeros_like(dw_ref)
            dgain_ref[...] = jnp.zeros_like(dgain_ref)

        dxb = _c(dx1_ref[...])
        dw_ref[...] += _tn(mg_ref[...], dxb)
        dmg = _nt(dxb, w_ref[...])
        ao, co, mo = ao_ref[...], co_ref[...], mo_ref[...]
        da, ga = _norm_bwd(dmg[:, :ATT_W], ao, _rstd(ao), _small(pk_ref, "out_norm_attn"))
        dc, gc = _norm_bwd(dmg[:, ATT_W:ATT_W + CONV_W], co, _rstd(co), _small(pk_ref, "out_norm_conv"))
        dm, gm = _norm_bwd(dmg[:, ATT_W + CONV_W:], mo, _rstd(mo), _small(pk_ref, "out_norm_mem"))
        dao_ref[...] = da
        dco_ref[...] = dc
        dmo_ref[...] = dm
        dgain_ref[...] += jnp.concatenate([ga, gc, gm], axis=1)

    tile = lambda w: pl.BlockSpec((tm, w), lambda i: (i, 0))
    return _run("out_proj_bwd", body, (T // tm,), [dx1, merged, attn_out, conv_out, mem_out, pk, wout],
                [tile(D), tile(D), tile(ATT_W), tile(CONV_W), tile(MEM_W), VM, VM],
                [SDS((T, ATT_W), f32), SDS((T, CONV_W), f32), SDS((T, MEM_W), f32), SDS((D, D), f32), SDS((1, D), f32)],
                [tile(ATT_W), tile(CONV_W), tile(MEM_W), pl.BlockSpec((D, D), lambda i: (0, 0)),
                 pl.BlockSpec((1, D), lambda i: (0, 0))], vmem_mib=40)


def attn_bwd(qkv, d_attn, attn_out, sink_rows, BL, S, exchange):
    NB = S // BLK
    T = BL * S

    def body(q_ref, kc_ref, kp_ref, vc_ref, vp_ref, do_ref, ao_ref, sk_ref, tab_ref,
             dq_ref, dk_ref, dv_ref, dsk_ref, pend_k, pend_v):
        b, j = pl.program_id(0), pl.program_id(1)

        @pl.when((b == 0) & (j == 0))
        def _():
            dsk_ref[...] = jnp.zeros_like(dsk_ref)

        @pl.when(j == 0)
        def _():
            pend_k[...] = jnp.zeros_like(pend_k)
            pend_v[...] = jnp.zeros_like(pend_v)

        @pl.when(j < NB)
        def _():
            q, do, ao = q_ref[...], do_ref[...], ao_ref[...]
            k2 = jnp.concatenate([kp_ref[...], kc_ref[...]], axis=0)
            v2 = jnp.concatenate([vp_ref[...], vc_ref[...]], axis=0)
            lane = lax.broadcasted_iota(jnp.int32, (8, 128), 1)
            ones_w = jnp.ones((2 * BLK, 2 * BLK), MXU)
            dsk = jnp.zeros((8, 128), f32)
            dks, dvs = [], []
            for g in range(N_KV):
                kn, vh = k2[:, g * HD:(g + 1) * HD], v2[:, g * HD:(g + 1) * HD]
                qs = _stack_heads(q, g)
                s = _nt(qs, kn) * (HD ** -0.5) + tab_ref[g]
                e, es = _exp_scores(s, _sink_column(g, sk_ref))
                eb = _c(e)
                inv_w = 1.0 / (_nn(eb, ones_w) + es)
                inv_n = inv_w[:, :HD]
                dos = _stack_heads(do, g)
                delta = _rowsum_mxu(dos * _stack_heads(ao, g), 2 * BLK)
                dp = _nt(_c(dos), vh)
                ds = _c(e * inv_w * (dp - delta) * (HD ** -0.5))
                t = es * inv_n[:, 0:1] * delta[:, 0:1]
                for hh in range(GQA):
                    dsk = dsk + jnp.where(lane == g * GQA + hh, -jnp.sum(t[hh * BLK:(hh + 1) * BLK]), 0.0)
                dvs.append(_tn(eb, _c(dos * inv_n)))
                dks.append(_tn(ds, qs))
                dqs = _nn(ds, kn)
                for hh in range(GQA):
                    dq_ref[:, pl.ds((g * GQA + hh) * HD, HD)] = dqs[hh * BLK:(hh + 1) * BLK]
            dk2 = jnp.concatenate(dks, axis=1)
            dv2 = jnp.concatenate(dvs, axis=1)
            dk_ref[...] = pend_k[...] + dk2[:BLK]
            dv_ref[...] = pend_v[...] + dv2[:BLK]
            pend_k[...] = dk2[BLK:]
            pend_v[...] = dv2[BLK:]
            dsk_ref[...] += dsk

        @pl.when(j == NB)
        def _():
            dk_ref[...] = pend_k[...]
            dv_ref[...] = pend_v[...]

    cur = lambda col: (lambda b, j: (b * NB + jnp.minimum(j, NB - 1), col))
    prev = lambda col: (lambda b, j: (b * NB + jnp.maximum(j - 1, 0), col))
    small = lambda shape: pl.BlockSpec(shape, lambda b, j: (0, 0))
    return _run("attn_bwd", body, (BL, NB + 1), [qkv, qkv, qkv, qkv, qkv, d_attn, attn_out, sink_rows, _swa_bias_table()],
                [pl.BlockSpec((BLK, ATT_W), cur(0)),
                 pl.BlockSpec((BLK, KV_W), cur(4)), pl.BlockSpec((BLK, KV_W), prev(4)),
                 pl.BlockSpec((BLK, KV_W), cur(5)), pl.BlockSpec((BLK, KV_W), prev(5)),
                 pl.BlockSpec((BLK, ATT_W), cur(0)), pl.BlockSpec((BLK, ATT_W), cur(0)), small((8, 128)),
                 pl.BlockSpec((None, N_KV, GQA * BLK, 2 * BLK), lambda b, j: (jnp.minimum(j, 1), 0, 0, 0))],
                [SDS((T, ATT_W), f32), SDS((T, KV_W), f32), SDS((T, KV_W), f32), SDS((8, 128), f32)],
                [pl.BlockSpec((BLK, ATT_W), cur(0)), pl.BlockSpec((BLK, KV_W), prev(0)),
                 pl.BlockSpec((BLK, KV_W), prev(0)), small((8, 128))],
                scratch=[pltpu.VMEM((BLK, KV_W), f32)] * 2, vmem_mib=56, exchange=exchange)


def mem_conv_bwd(d_mem_out, mem_out, d_conv_out, proj, qkv, km, vm, conv_w8, pk, S, tm, exchange):
    T = d_mem_out.shape[0]
    NM = km.shape[0] // (T // S)

    def body(dmo_ref, mo_ref, dco_ref, ch_ref, cb_ref, cc_ref, chh_ref, cch_ref, qm_ref, km_ref, vm_ref, cw_ref,
             pk_ref, dqm_ref, dkm_ref, dvm_ref, dcb_ref, dcv_ref, dcw_ref, dcbias_ref):
        i = pl.program_id(0)
        first = (i * tm) % S == 0

        @pl.when(i == 0)
        def _():
            dcw_ref[...] = jnp.zeros_like(dcw_ref)
            dcbias_ref[...] = jnp.zeros_like(dcbias_ref)

        @pl.when(first)
        def _():
            dkm_ref[...] = jnp.zeros_like(dkm_ref)
            dvm_ref[...] = jnp.zeros_like(dvm_ref)

        qm, kmv, vmv, dmo, mo = qm_ref[...], km_ref[...], vm_ref[...], dmo_ref[...], mo_ref[...]
        ones_w = jnp.ones((NM, NM), MXU)
        for h in range(N_MEMH):
            qh, kh, vh, e = _mem_head(qm, kmv, vmv, h)
            eb = _c(e)
            doh = dmo[:, h * HD:(h + 1) * HD]
            delta = _rowsum_mxu(doh * mo[:, h * HD:(h + 1) * HD], NM)
            dp = _nt(_c(doh), vh)
            inv_w = 1.0 / _nn(eb, ones_w)
            ds = _c(e * inv_w * (dp - delta) * (HD ** -0.5))
            dvm_ref[:, pl.ds(h * HD, HD)] += _tn(eb, _c(doh * inv_w[:, :HD]))
            dkm_ref[:, pl.ds(h * HD, HD)] += _tn(ds, qh)
            dqm_ref[:, pl.ds(h * HD, HD)] = _nn(ds, kh)

        u = cc_ref[...] * ch_ref[...]
        uh = jnp.where(first, 0.0, cch_ref[...] * chh_ref[...])
        u1, u2 = _conv_taps(u, uh)
        conv = cw_ref[0:1, :] * u2 + cw_ref[1:2, :] * u1 + cw_ref[2:3, :] * u + _small(pk_ref, "conv_b")
        dy = dco_ref[...]
        dcb_ref[...] = dy * conv
        dcv = dy * cb_ref[...]
        dcv_ref[...] = dcv
        dcbias_ref[...] += jnp.sum(dcv, axis=0, keepdims=True)
        dcw_ref[0:1, :] += jnp.sum(dcv * u2, axis=0, keepdims=True)
        dcw_ref[1:2, :] += jnp.sum(dcv * u1, axis=0, keepdims=True)
        dcw_ref[2:3, :] += jnp.sum(dcv * u, axis=0, keepdims=True)

    tile = lambda w, col: pl.BlockSpec((tm, w), lambda i: (i, col))
    halo = lambda col: pl.BlockSpec((8, CONV_W), lambda i: (jnp.maximum(i * (tm // 8) - 1, 0), col))
    seq = pl.BlockSpec((NM, MEM_W), lambda i: ((i * tm) // S, 0))
    const = lambda shape: pl.BlockSpec(shape, lambda i: (0, 0))
    return _run("mem_conv_bwd", body, (T // tm,),
                [d_mem_out, mem_out, d_conv_out, proj, proj, proj, proj, proj, qkv, km, vm, conv_w8, pk],
                [tile(MEM_W, 0), tile(MEM_W, 0), tile(CONV_W, 0), tile(CONV_W, 3), tile(CONV_W, 4), tile(CONV_W, 5),
                 halo(3), halo(5), tile(MEM_W, 3), seq, seq, VM, VM],
                [SDS((T, MEM_W), f32), SDS(km.shape, f32), SDS(km.shape, f32),
                 SDS((T, CONV_W), f32), SDS((T, CONV_W), f32), SDS((8, CONV_W), f32), SDS((1, CONV_W), f32)],
                [tile(MEM_W, 0), seq, seq, tile(CONV_W, 0), tile(CONV_W, 0), const((8, CONV_W)), const((1, CONV_W))],
                vmem_mib=48, exchange=exchange)


def in_proj_bwd(dqn, dkn, dv, dcb, dcv, dqmn, proj, conv_w8, xn, x2d, dx1, pk, winT, S, tm, stages, ws, ms, vs):
    T, D = x2d.shape
    P = winT.shape[0]
    last_blk = T // 8 - 1
    n = len(stages)
    nsteps = T // tm
    tile_w = ws[0].shape[1] // (nsteps // 2)
    turn = [e * 2 // n for e in range(n)]

    def body(dq_ref, dk_ref, dv_ref, dcb_ref, dcv_ref, dcvn_ref, dqm_ref, qa_ref, ka_ref, ch_ref, cc_ref, qma_ref,
             cw_ref, xn_ref, x_ref, dx1_ref, pk_ref, w_ref, *rest):
        st, aw, am, av = (rest[k * n:(k + 1) * n] for k in range(4))
        dx_ref, dw_ref, dg_ref, dqg_ref, dkg_ref, dmqg_ref = rest[4 * n:4 * n + 6]
        aouts = rest[4 * n + 6:]
        i = pl.program_id(0)

        for parity in range(2):
            @pl.when(i % 2 == parity)
            def _(parity=parity):
                for e in range(n):
                    if turn[e] == parity:
                        g = jnp.concatenate([_sum_chips(st[e].at[0]), _sum_chips(st[e].at[1])], axis=0)
                        d, mm, vv = _adamw_math(aw[e][...], g, am[e][...], av[e][...])
                        for k, val in enumerate((g, d, mm, vv)):
                            aouts[4 * e + k][...] = val

        @pl.when(i == 0)
        def _():
            dw_ref[...] = jnp.zeros_like(dw_ref)
            dg_ref[...] = jnp.zeros_like(dg_ref)
            dqg_ref[...] = jnp.zeros_like(dqg_ref)
            dkg_ref[...] = jnp.zeros_like(dkg_ref)
            dmqg_ref[...] = jnp.zeros_like(dmqg_ref)

        dqa, gq = _heads_norm_bwd(dq_ref[...], qa_ref[...], _small(pk_ref, "q_norm"))
        dka, gk = _heads_norm_bwd(dk_ref[...], ka_ref[...], _small(pk_ref, "k_norm"))
        dqma, gmq = _heads_norm_bwd(dqm_ref[...], qma_ref[...], _small(pk_ref, "mem_q_norm"))
        dqg_ref[...] += gq
        dkg_ref[...] += gk
        dmqg_ref[...] += gmq

        last = ((i + 1) * tm) % S == 0
        dcv = dcv_ref[...]
        nxt = jnp.where(last, 0.0, dcvn_ref[...])
        row = lax.broadcasted_iota(jnp.int32, dcv.shape, 0)
        n1 = jnp.where(row == tm - 1, nxt[0:1, :], pltpu.roll(dcv, tm - 1, 0))
        n2 = jnp.where(row == tm - 2, nxt[0:1, :], jnp.where(row == tm - 1, nxt[1:2, :], pltpu.roll(dcv, tm - 2, 0)))
        du = cw_ref[2:3, :] * dcv + cw_ref[1:2, :] * n1 + cw_ref[0:1, :] * n2
        d_proj = jnp.concatenate([_c(dqa), _c(dka), _c(dv_ref[...]), _c(du * cc_ref[...]),
                                  _c(dcb_ref[...]), _c(du * ch_ref[...]), _c(dqma)], axis=1)
        dw_ref[...] += _tn(d_proj, xn_ref[...])
        xv = x_ref[...]
        dv_, dg = _norm_bwd(_nn(d_proj, w_ref[...]), xv, _rstd(xv), _small(pk_ref, "norm_mix"))
        dx_ref[...] = dx1_ref[...] + dv_
        dg_ref[...] += dg

    tile = lambda w, col=0: pl.BlockSpec((tm, w), lambda i: (i, col))
    nhalo = pl.BlockSpec((8, CONV_W), lambda i: (jnp.minimum((i + 1) * (tm // 8), last_blk), 0))
    const = lambda shape: pl.BlockSpec(shape, lambda i: (0, 0))
    st_specs = [pl.BlockSpec((2, 4, s.shape[2], tile_w), lambda i: (0, 0, 0, i // 2)) for s in stages]
    w_specs = [pl.BlockSpec((w.shape[0], tile_w), lambda i: (0, i // 2)) for w in ws]
    res = _run("in_proj_bwd", body, (nsteps,),
               [dqn, dkn, dv, dcb, dcv, dcv, dqmn, proj, proj, proj, proj, proj, conv_w8, xn, x2d, dx1, pk, winT]
               + list(stages) + list(ws) + list(ms) + list(vs),
               [tile(ATT_W), tile(KV_W), tile(KV_W), tile(CONV_W), tile(CONV_W), nhalo, tile(MEM_W),
                tile(ATT_W, 0), tile(KV_W, 4), tile(CONV_W, 3), tile(CONV_W, 5), tile(MEM_W, 6), VM,
                tile(D), tile(D), tile(D), VM, VM] + st_specs + w_specs * 3,
               [SDS((T, D), f32), SDS((P, D), f32), SDS((1, D), f32), SDS((1, HD), f32), SDS((1, HD), f32),
                SDS((1, HD), f32)] + [SDS(w.shape, f32) for w in ws for _ in range(4)],
               [tile(D), pl.BlockSpec((P, D), lambda i: (0, 0)), const((1, D)), const((1, HD)), const((1, HD)),
                const((1, HD))] + [s for s in w_specs for _ in range(4)],
               vmem_mib=56)
    return res[:6], [res[6 + 4 * e:10 + 4 * e] for e in range(n)]


def mem_kv_bwd(dkm, dvm, kv, memn, mem2d, pk, wmkv):
    def body(dkm_ref, dvm_ref, kv_ref, mn_ref, m_ref, pk_ref, w_ref, dw_ref, dg_ref, dkg_ref):
        dkk, dkg = _heads_norm_bwd(dkm_ref[...], kv_ref[:, :MEM_W], _small(pk_ref, "mem_k_norm"))
        dkg_ref[...] = dkg
        dkv = _c(jnp.concatenate([dkk, dvm_ref[...]], axis=1))
        dw_ref[...] = _tn(mn_ref[...], dkv)
        mv = m_ref[...]
        dg_ref[...] = jnp.sum(_nt(dkv, w_ref[...]) * mv * _rstd(mv), axis=0, keepdims=True)

    return _run("mem_kv_bwd", body, (), [dkm, dvm, kv, memn, mem2d, pk, wmkv], [VM] * 7,
                [SDS(wmkv.shape, f32), SDS((1, mem2d.shape[1]), f32), SDS((1, HD), f32)], [VM] * 3, vmem_mib=40)


def _halves_view(g):
    return g.reshape(4, 2, g.shape[0] // 8, g.shape[1])


def kernel(x, mem, norm_mix, w_in, q_norm, k_norm, attn_sinks, conv_w, conv_b, norm_mem, w_mem_kv, mem_q_norm, mem_k_norm, out_norm_attn, out_norm_conv, out_norm_mem, w_out, norm_ffn, w_gate, w_up, w_down, loss_target, m_norm_mix, m_w_in, m_q_norm, m_k_norm, m_attn_sinks, m_conv_w, m_conv_b, m_norm_mem, m_w_mem_kv, m_mem_q_norm, m_mem_k_norm, m_out_norm_attn, m_out_norm_conv, m_out_norm_mem, m_w_out, m_norm_ffn, m_w_gate, m_w_up, m_w_down, v_norm_mix, v_w_in, v_q_norm, v_k_norm, v_attn_sinks, v_conv_w, v_conv_b, v_norm_mem, v_w_mem_kv, v_mem_q_norm, v_mem_k_norm, v_out_norm_attn, v_out_norm_conv, v_out_norm_mem, v_w_out, v_norm_ffn, v_w_gate, v_w_up, v_w_down):
    BL, S, D = x.shape
    T = BL * S
    TM = 256
    TM_BIG = min(512, S)
    w_small = dict(norm_mix=norm_mix, norm_mem=norm_mem, norm_ffn=norm_ffn, out_norm_attn=out_norm_attn,
                   out_norm_conv=out_norm_conv, out_norm_mem=out_norm_mem, conv_w=conv_w, conv_b=conv_b, q_norm=q_norm,
                   k_norm=k_norm, mem_q_norm=mem_q_norm, mem_k_norm=mem_k_norm, attn_sinks=attn_sinks)
    m_small = dict(norm_mix=m_norm_mix, norm_mem=m_norm_mem, norm_ffn=m_norm_ffn, out_norm_attn=m_out_norm_attn,
                   out_norm_conv=m_out_norm_conv, out_norm_mem=m_out_norm_mem, conv_w=m_conv_w, conv_b=m_conv_b,
                   q_norm=m_q_norm, k_norm=m_k_norm, mem_q_norm=m_mem_q_norm, mem_k_norm=m_mem_k_norm,
                   attn_sinks=m_attn_sinks)
    v_small = dict(norm_mix=v_norm_mix, norm_mem=v_norm_mem, norm_ffn=v_norm_ffn, out_norm_attn=v_out_norm_attn,
                   out_norm_conv=v_out_norm_conv, out_norm_mem=v_out_norm_mem, conv_w=v_conv_w, conv_b=v_conv_b,
                   q_norm=v_q_norm, k_norm=v_k_norm, mem_q_norm=v_mem_q_norm, mem_k_norm=v_mem_k_norm,
                   attn_sinks=v_attn_sinks)
    pk = _pack_small(w_small)

    rowblocks = lambda a, b, c, d, e, f: [a[0].T, b[0].T, c[0].T, d[0], e[0], f[0]]
    w_rb = rowblocks(w_in, w_gate, w_up, w_down, w_out, w_mem_kv)
    m_rb = rowblocks(m_w_in, m_w_gate, m_w_up, m_w_down, m_w_out, m_w_mem_kv)
    v_rb = rowblocks(v_w_in, v_w_gate, v_w_up, v_w_down, v_w_out, v_w_mem_kv)
    (winT_s,) = prep_weights("prep_w_in", w_rb[:1])
    cw_pad = jnp.zeros((8, 128), f32).at[:3, :HD].set(conv_w[0])
    (wgT_s, wuT_s, wd_s, wout_s, wmkv_s), (winT, cw_all) = prep_weights(
        "gather_w_in", w_rb[1:], _together([gather_two_legs([winT_s]), gather_exchange([cw_pad], [False])]))
    conv_w_full = jnp.transpose(cw_all.reshape(4, 8, 128)[:, :3, :HD], (1, 0, 2)).reshape(3, CONV_W)
    conv_w8 = jnp.zeros((8, CONV_W), f32).at[:3].set(conv_w_full)
    sink_rows = jnp.broadcast_to(attn_sinks.reshape(N_Q, 1), (N_Q, 128))

    x2d = x.reshape(T, D)
    mem2d = mem.reshape(-1, D)
    (xn, proj, qkv), near1 = in_proj_fwd(x2d, pk, winT, TM_BIG, gather_near_exchange([wgT_s, wout_s, wmkv_s], relay_early=1))
    (attn_out,), (wgT, wout, wmkv, *near2) = attn_fwd(
        qkv, sink_rows, BL, S, _together([gather_far_exchange(near1, relay_early=2), gather_near_exchange([wuT_s, wd_s], relay_early=2)]))
    memn, kv, km, vm = mem_kv_fwd(mem2d, pk, wmkv)
    (conv_out, mem_out, merged, x1, h), (wuT, wd) = mixer_tail_fwd(
        x2d, attn_out, proj, qkv, km, vm, conv_w8, pk, wout, S, TM_BIG, gather_far_exchange(near2, relay_early=2))

    dx1, dx2b, act, d_gate, d_up, loss8, d_norm_ffn = ffn_fwd_bwd(h, x1, loss_target.reshape(T, D), wgT, wuT, wd, pk, TM)
    F = wd.shape[0]
    g_wd = matmul_tn(act, dx2b, "dw_down", F // 2, min(T, 1024))
    g_wgT = matmul_tn(d_gate, h, "dw_gate", F // 2, min(T, 1024))
    g_wuT = matmul_tn(d_up, h, "dw_up", F // 2, min(T, 1024))

    d_attn, d_conv_out, d_mem_out, g_wout, d_gains = out_proj_bwd(dx1, merged, attn_out, conv_out, mem_out, pk, wout, TM_BIG)
    dqmn, dkm, dvm, dcb, dcv, d_cw8, d_cbias = mem_conv_bwd(
        d_mem_out, mem_out, d_conv_out, proj, qkv, km, vm, conv_w8, pk, S, min(1024, S), None)
    (dqn, dkn, dv, d_sink8), (st_wout, st_wgT, st_wuT, st_wd) = attn_bwd(
        qkv, d_attn, attn_out, sink_rows, BL, S,
        reduce_scatter_exchange([_halves_view(g) for g in (g_wout, g_wgT, g_wuT, g_wd)], BL * (S // BLK + 1),
                                load_step=[0, 1, 4, 7], send_step=[1, 4, 7, 10], relay_step=[6, 16, 25, 33]))
    (g_x, g_winT, d_norm_mix, d_qg, d_kg, d_mqg), late_res = in_proj_bwd(
        dqn, dkn, dv, dcb, dcv, dqmn, proj, conv_w8, xn, x2d, dx1, pk, winT, S, TM,
        [st_wgT, st_wuT, st_wd, st_wout], w_rb[1:5], m_rb[1:5], v_rb[1:5])
    g_wmkv, d_norm_mem, d_mkg = mem_kv_bwd(dkm, dvm, kv, memn, mem2d, pk, wmkv)

    tot, tail_stage = tail_reduce(d_norm_mix, d_norm_mem, d_norm_ffn, d_gains, d_cw8, d_cbias, d_qg, d_kg, d_mqg, d_mkg,
                                  d_sink8, loss8, [_halves_view(g) for g in (g_winT, g_wmkv)])
    loss = tot[5, 384]
    tail_res, _ = adamw_big("adamw_tail", tail_stage, [w_rb[0], w_rb[5]], [m_rb[0], m_rb[5]], [v_rb[0], v_rb[5]], 4)
    res = {"w_in": [a.T[None] for a in tail_res[0]], "w_gate": [a.T[None] for a in late_res[0]],
           "w_up": [a.T[None] for a in late_res[1]], "w_down": [a[None] for a in late_res[2]],
           "w_out": [a[None] for a in late_res[3]], "w_mem_kv": [a[None] for a in tail_res[1]]}
    res.update(adamw_small(tot, pk, _pack_small(m_small), _pack_small(v_small), {k: w_small[k].shape for k in SMALL}))

    order = ["norm_mix", "w_in", "q_norm", "k_norm", "attn_sinks", "conv_w", "conv_b", "norm_mem", "w_mem_kv",
             "mem_q_norm", "mem_k_norm", "out_norm_attn", "out_norm_conv", "out_norm_mem", "w_out", "norm_ffn",
             "w_gate", "w_up", "w_down"]
    return (loss, g_x.reshape(BL, S, D), *[res[n][0] for n in order], *[res[n][1] for n in order],
            *[res[n][2] for n in order], *[res[n][3] for n in order])
```

```python
import collections
import functools

import jax
import jax.numpy as jnp
import numpy as np
from jax import lax
from jax.experimental import pallas as pl
from jax.experimental.pallas import tpu as pltpu

f32 = jnp.float32
MXU = jnp.bfloat16
WIRE = jnp.bfloat16
EPS = 1e-6
NEG = -1e30
HD = 64
BLK = 128
N_Q, N_KV, N_MEMH = 8, 2, 4
GQA = N_Q // N_KV
ATT_W, KV_W, CONV_W, MEM_W = 512, 128, 256, 256
VMEM_MIB = 1024 * 1024
ADAM_LR, ADAM_B1, ADAM_B2, ADAM_EPS, ADAM_WD, ADAM_STEP = 0.001, 0.9, 0.999, 1e-08, 0.01, 10

MESH = pl.DeviceIdType.MESH
VM = pl.BlockSpec(memory_space=pltpu.VMEM)
ANY = pl.BlockSpec(memory_space=pl.ANY)
SDS = jax.ShapeDtypeStruct
DMA = pltpu.SemaphoreType.DMA


def _c(v):
    return v.astype(MXU)


def _nn(a, b):
    return lax.dot_general(a, b, (((1,), (0,)), ((), ())), preferred_element_type=f32)


def _nt(a, b):
    return lax.dot_general(a, b, (((1,), (1,)), ((), ())), preferred_element_type=f32)


def _tn(a, b):
    return lax.dot_general(a, b, (((0,), (0,)), ((), ())), preferred_element_type=f32)


def _rstd(v):
    return lax.rsqrt(jnp.mean(v * v, axis=-1, keepdims=True) + EPS)


def _norm_bwd(dy, v, r, g):
    dyg = dy * g
    dv = r * dyg - v * (r * r * r) * jnp.mean(dyg * v, axis=-1, keepdims=True)
    return dv, jnp.sum(dy * v * r, axis=0, keepdims=True)


def _split3(v):
    hi = _c(v)
    r1 = v - hi.astype(f32)
    mid = _c(r1)
    return hi, mid, _c(r1 - mid.astype(f32))


def _rowsum_mxu(v, width):
    ones = jnp.ones((v.shape[1], width), MXU)
    return sum(_nn(a, ones) for a in _split3(v))


def _seg_sums(v):
    r = lax.broadcasted_iota(jnp.int32, (2 * HD, 2 * HD), 0) // HD
    c = lax.broadcasted_iota(jnp.int32, (2 * HD, 2 * HD), 1) // HD
    bd = (r == c).astype(MXU)
    outs = []
    for b in range(v.shape[1] // (2 * HD)):
        outs.append(sum(_nn(a, bd) for a in _split3(v[:, b * 2 * HD:(b + 1) * 2 * HD])))
    return outs[0] if len(outs) == 1 else jnp.concatenate(outs, axis=1)


def _lanes(g, width):
    return jnp.concatenate([g] * (width // HD), axis=1)


def _heads_rstd(v):
    return lax.rsqrt(_seg_sums(v * v) * (1.0 / HD) + EPS)


def _heads_norm_bwd(dy, v, g):
    r = _heads_rstd(v)
    gl = _lanes(g, v.shape[1])
    dyg = dy * gl
    dv = r * dyg - v * (r * r * r) * (_seg_sums(dyg * v) * (1.0 / HD))
    dgl = jnp.sum(dy * v * r, axis=0, keepdims=True)
    return dv, sum(dgl[:, s * HD:(s + 1) * HD] for s in range(v.shape[1] // HD))


def _exp_scores(s, extra=None):
    m = jnp.max(s, axis=-1, keepdims=True)
    if extra is None:
        return jnp.exp(s - m), None
    m = jnp.maximum(m, extra)
    return jnp.exp(s - m), jnp.exp(extra - m)


def _place():
    return lax.axis_index("x"), lax.axis_index("y"), lax.axis_index("c")


SMALL_AT = {"norm_mix": (0, 0, 1024), "norm_mem": (1, 0, 1024), "norm_ffn": (2, 0, 1024),
            "out_norm_attn": (3, 0, ATT_W), "out_norm_conv": (3, ATT_W, CONV_W), "out_norm_mem": (3, ATT_W + CONV_W, MEM_W),
            "conv_b": (4, 3 * CONV_W, CONV_W), "q_norm": (5, 0, HD), "k_norm": (5, HD, HD), "mem_q_norm": (5, 2 * HD, HD),
            "mem_k_norm": (5, 3 * HD, HD), "attn_sinks": (5, 256, N_Q)}
SMALL = ("norm_mix", "norm_mem", "norm_ffn", "out_norm_attn", "out_norm_conv", "out_norm_mem", "conv_w", "conv_b",
         "q_norm", "k_norm", "mem_q_norm", "mem_k_norm", "attn_sinks")


def _small(pk_ref, name):
    r, c0, w = SMALL_AT[name]
    return pk_ref[r:r + 1, c0:c0 + w]


def _pack_small(d):
    z = lambda n: jnp.zeros((1, n), f32)
    row3 = jnp.concatenate([d["out_norm_attn"], d["out_norm_conv"], d["out_norm_mem"]], axis=1)
    row4 = jnp.concatenate([d["conv_w"].reshape(1, 3 * HD), z(3 * CONV_W - 3 * HD), d["conv_b"]], axis=1)
    row5 = jnp.concatenate([d["q_norm"], d["k_norm"], d["mem_q_norm"], d["mem_k_norm"], d["attn_sinks"],
                            z(1024 - 4 * HD - N_Q)], axis=1)
    return jnp.concatenate([d["norm_mix"], d["norm_mem"], d["norm_ffn"], row3, row4, row5, z(1024), z(1024)], axis=0)


def _other_chips(x, y):
    return [(1 - x, y), (x, 1 - y), (1 - x, 1 - y)]


Exchange = collections.namedtuple("Exchange", "ins outs sems start finish relays aliases", defaults=((), {}))


def _together(exchanges):
    def bounds(key):
        at, out = 0, []
        for ex in exchanges:
            out.append((at, at + len(getattr(ex, key))))
            at += len(getattr(ex, key))
        return out

    bi, bo, bs = bounds("ins"), bounds("outs"), bounds("sems")

    def of(i, fn):
        return lambda xa, xo, xs: fn(xa[bi[i][0]:bi[i][1]], xo[bo[i][0]:bo[i][1]], xs[bs[i][0]:bs[i][1]])

    def every(name):
        fns = [of(i, getattr(ex, name)) for i, ex in enumerate(exchanges)]

        def run(xa, xo, xs):
            for fn in fns:
                fn(xa, xo, xs)
        return run

    aliases = {}
    for i, ex in enumerate(exchanges):
        aliases.update({bi[i][0] + a: bo[i][0] + o for a, o in ex.aliases.items()})
    return Exchange([a for ex in exchanges for a in ex.ins], [o for ex in exchanges for o in ex.outs],
                    [s for ex in exchanges for s in ex.sems], every("start"), every("finish"),
                    [(sbe, of(i, fn)) for i, ex in enumerate(exchanges) for sbe, fn in ex.relays], aliases)


def _run(name, body, grid, ins, in_specs, out_shape, out_specs, scratch=(), vmem_mib=32, exchange=None):
    ins, in_specs, out_shape, out_specs, scratch = list(ins), list(in_specs), list(out_shape), list(out_specs), list(scratch)
    ni, no, ns = len(ins), len(out_shape), len(scratch)
    ex = exchange
    if ex is not None:
        nxi, nxo = len(ex.ins), len(ex.outs)

    def call_body(*refs):
        if ex is None:
            body(*refs)
            return
        a, xa = refs[:ni], refs[ni:ni + nxi]
        o, xo = refs[ni + nxi:ni + nxi + no], refs[ni + nxi + no:ni + nxi + no + nxo]
        s, xs = refs[ni + nxi + no + nxo:ni + nxi + no + nxo + ns], refs[ni + nxi + no + nxo + ns:]
        if grid:
            first = functools.reduce(jnp.logical_and, [pl.program_id(d) == 0 for d in range(len(grid))])
            last = functools.reduce(jnp.logical_and, [pl.program_id(d) == grid[d] - 1 for d in range(len(grid))])
            pl.when(first)(lambda: ex.start(xa, xo, xs))
            body(*a, *o, *s)
            nsteps = functools.reduce(lambda p, q: p * q, grid)
            for before_end, fn in ex.relays:
                at = np.unravel_index(max(nsteps - 1 - before_end, 0), grid)
                here = functools.reduce(jnp.logical_and, [pl.program_id(d) == int(at[d]) for d in range(len(grid))])
                pl.when(here)(functools.partial(fn, xa, xo, xs))
            pl.when(last)(lambda: ex.finish(xa, xo, xs))
        else:
            ex.start(xa, xo, xs)
            if body is not None:
                body(*a, *o, *s)
            for _, fn in ex.relays:
                fn(xa, xo, xs)
            ex.finish(xa, xo, xs)

    kw = dict(grid=grid) if grid else {}
    if ex is not None:
        if ex.aliases:
            kw["input_output_aliases"] = {ni + i: no + o for i, o in ex.aliases.items()}
        ins, in_specs = ins + list(ex.ins), in_specs + [ANY] * nxi
        out_shape, out_specs = out_shape + list(ex.outs), out_specs + [ANY] * nxo
        scratch = scratch + list(ex.sems)
    res = pl.pallas_call(
        call_body, name=name, out_shape=out_shape, in_specs=in_specs, out_specs=out_specs, scratch_shapes=scratch,
        compiler_params=pltpu.CompilerParams(dimension_semantics=("arbitrary",) * len(grid) if grid else None,
                                             vmem_limit_bytes=vmem_mib * VMEM_MIB), **kw)(*ins)
    res = list(res)
    return (res[:no], res[no:]) if ex is not None else res


def _remote(src, dst, ssem, rsem, dev):
    return pltpu.make_async_remote_copy(src_ref=src, dst_ref=dst, send_sem=ssem, recv_sem=rsem,
                                        device_id=dev, device_id_type=MESH)


def gather_exchange(shards, split, relay_early=0):
    n = len(shards)

    def rows(ref, e, kk, half=None):
        R = shards[e].shape[0]
        if half is None:
            return ref.at[pl.ds(pl.multiple_of(kk * R, 8), R)]
        return ref.at[pl.ds(pl.multiple_of(kk * R + half * (R // 2), 8), R // 2)]

    def ici(src, dst, sm, e, j, chip_j, x, y, c):
        k = 2 * x + y
        if split[e]:
            s = src[e].at[pl.ds(pl.multiple_of(c * (shards[e].shape[0] // 2), 8), shards[e].shape[0] // 2)]
            return _remote(s, rows(dst[e], e, k, c), sm[0].at[6 * e + j], sm[1].at[6 * e + j], (*chip_j, c))
        return _remote(src[e], rows(dst[e], e, k), sm[0].at[6 * e + j], sm[1].at[6 * e + j], (*chip_j, c))

    def landed(dst, e, chip_j, c):
        kj = 2 * chip_j[0] + chip_j[1]
        return rows(dst[e], e, kj, c) if split[e] else rows(dst[e], e, kj)

    def forward(dst, sm, e, j, chip_j, x, y, c, sender_c):
        kj = 2 * chip_j[0] + chip_j[1]
        r = rows(dst[e], e, kj, sender_c)
        return _remote(r, r, sm[0].at[6 * e + 3 + j], sm[1].at[6 * e + 3 + j], (x, y, 1 - c))

    def local(src, dst, sm, e, x, y):
        return pltpu.make_async_copy(src[e], rows(dst[e], e, 2 * x + y), sm[2].at[e])

    def start(src, dst, sm):
        x, y, c = _place()
        for e in range(n):
            local(src, dst, sm, e, x, y).start()
            for j, chip_j in enumerate(_other_chips(x, y)):
                ici(src, dst, sm, e, j, chip_j, x, y, c).start()

    def relay(src, dst, sm):
        x, y, c = _place()
        for e in range(n):
            for j, chip_j in enumerate(_other_chips(x, y)):
                r = landed(dst, e, chip_j, c)
                _remote(r, r, sm[0].at[6 * e + j], sm[1].at[6 * e + j], (*chip_j, c)).wait_recv()
                if split[e]:
                    forward(dst, sm, e, j, chip_j, x, y, c, c).start()

    def finish(src, dst, sm):
        x, y, c = _place()
        chips = _other_chips(x, y)
        for e in range(n):
            for j, chip_j in enumerate(chips):
                if split[e]:
                    forward(dst, sm, e, j, chip_j, x, y, c, 1 - c).wait_recv()
        for e in range(n):
            for j, chip_j in enumerate(chips):
                ici(src, dst, sm, e, j, chip_j, x, y, c).wait_send()
                if split[e]:
                    forward(dst, sm, e, j, chip_j, x, y, c, c).wait_send()
            local(src, dst, sm, e, x, y).wait()

    outs = [SDS((4 * s.shape[0], s.shape[1]), s.dtype) for s in shards]
    return Exchange(list(shards), outs, [DMA((6 * n,)), DMA((6 * n,)), DMA((n,))], start, finish, [(relay_early, relay)])


def _block_rows(ref, R, kk, half, quarter=None):
    hr = R // 2
    if quarter is None:
        return ref.at[pl.ds(pl.multiple_of(kk * R + half * hr, 8), hr)]
    return ref.at[pl.ds(pl.multiple_of(kk * R + half * hr + quarter * (hr // 2), 8), hr // 2)]


def gather_near_exchange(shards, relay_early=0):
    n = len(shards)
    R = [s.shape[0] for s in shards]

    def ici(src, dst, sm, e, j, chip_j, x, y, c):
        half = src[e].at[pl.ds(pl.multiple_of(c * (R[e] // 2), 8), R[e] // 2)]
        return _remote(half, _block_rows(dst[e], R[e], 2 * x + y, c), sm[0].at[4 * e + j], sm[1].at[4 * e + j], (*chip_j, c))

    def forward(dst, sm, e, j, chip_j, x, y, c, sender_c):
        r = _block_rows(dst[e], R[e], 2 * chip_j[0] + chip_j[1], sender_c)
        return _remote(r, r, sm[0].at[4 * e + 2 + j], sm[1].at[4 * e + 2 + j], (x, y, 1 - c))

    def local(src, dst, sm, e, x, y):
        return pltpu.make_async_copy(src[e], dst[e].at[pl.ds(pl.multiple_of((2 * x + y) * R[e], 8), R[e])], sm[2].at[e])

    def start(src, dst, sm):
        x, y, c = _place()
        for e in range(n):
            local(src, dst, sm, e, x, y).start()
            for j, chip_j in enumerate(_other_chips(x, y)[:2]):
                ici(src, dst, sm, e, j, chip_j, x, y, c).start()

    def relay(src, dst, sm):
        x, y, c = _place()
        for e in range(n):
            for j, chip_j in enumerate(_other_chips(x, y)[:2]):
                r = _block_rows(dst[e], R[e], 2 * chip_j[0] + chip_j[1], c)
                _remote(r, r, sm[0].at[4 * e + j], sm[1].at[4 * e + j], (*chip_j, c)).wait_recv()
                forward(dst, sm, e, j, chip_j, x, y, c, c).start()

    def finish(src, dst, sm):
        x, y, c = _place()
        near = _other_chips(x, y)[:2]
        for e in range(n):
            for j, chip_j in enumerate(near):
                forward(dst, sm, e, j, chip_j, x, y, c, 1 - c).wait_recv()
        for e in range(n):
            for j, chip_j in enumerate(near):
                ici(src, dst, sm, e, j, chip_j, x, y, c).wait_send()
                forward(dst, sm, e, j, chip_j, x, y, c, c).wait_send()
            local(src, dst, sm, e, x, y).wait()

    outs = [SDS((4 * s.shape[0], s.shape[1]), s.dtype) for s in shards]
    return Exchange(list(shards), outs, [DMA((4 * n,)), DMA((4 * n,)), DMA((n,))], start, finish, [(relay_early, relay)])


def gather_far_exchange(bufs, relay_early=0):
    n = len(bufs)
    R = [b.shape[0] // 4 for b in bufs]

    def send(src, dst, sm, e, j, x, y, c):
        to, of = _other_chips(x, y)[j], _other_chips(x, y)[1 - j]
        kk = 2 * of[0] + of[1]
        return _remote(_block_rows(src[e], R[e], kk, c, j), _block_rows(dst[e], R[e], kk, c, j),
                       sm[0].at[4 * e + j], sm[1].at[4 * e + j], (*to, c))

    def landed(dst, e, j, x, y, half):
        return _block_rows(dst[e], R[e], 2 * (1 - x) + (1 - y), half, j)

    def forward(dst, sm, e, j, x, y, c, sender_c):
        r = landed(dst, e, j, x, y, sender_c)
        return _remote(r, r, sm[0].at[4 * e + 2 + j], sm[1].at[4 * e + 2 + j], (x, y, 1 - c))

    def start(src, dst, sm):
        x, y, c = _place()
        for e in range(n):
            for j in range(2):
                send(src, dst, sm, e, j, x, y, c).start()

    def relay(src, dst, sm):
        x, y, c = _place()
        for e in range(n):
            for j in range(2):
                r = landed(dst, e, j, x, y, c)
                _remote(r, r, sm[0].at[4 * e + j], sm[1].at[4 * e + j], (*_other_chips(x, y)[j], c)).wait_recv()
                forward(dst, sm, e, j, x, y, c, c).start()

    def finish(src, dst, sm):
        x, y, c = _place()
        for e in range(n):
            for j in range(2):
                forward(dst, sm, e, j, x, y, c, 1 - c).wait_recv()
        for e in range(n):
            for j in range(2):
                send(src, dst, sm, e, j, x, y, c).wait_send()
                forward(dst, sm, e, j, x, y, c, c).wait_send()

    outs = [SDS(b.shape, b.dtype) for b in bufs]
    return Exchange(list(bufs), outs, [DMA((4 * n,)), DMA((4 * n,))], start, finish, [(relay_early, relay)],
                    {i: i for i in range(n)})


def gather_two_legs(shards, steps=None):
    near = gather_near_exchange(shards)
    far = gather_far_exchange(near.outs)

    def start(src, dst, sm):
        near.start(src, dst, sm[:3])

    def hand_on(src, dst, sm):
        near.relays[0][1](src, dst, sm[:3])

    def second(src, dst, sm):
        near.finish(src, dst, sm[:3])
        far.start(dst, dst, sm[3:])

    def hand_on_second(src, dst, sm):
        far.relays[0][1](dst, dst, sm[3:])

    def last(src, dst, sm):
        far.finish(dst, dst, sm[3:])

    def finish(src, dst, sm):
        for fn in (hand_on, second, hand_on_second, last):
            fn(src, dst, sm)

    sems = list(near.sems) + list(far.sems)
    if steps is None:
        return Exchange(near.ins, near.outs, sems, start, finish)
    return Exchange(near.ins, near.outs, sems, start, last, list(zip(steps, (hand_on, second, hand_on_second))))


def scatter_exchange(parts, relay_before_end=None, want_issue=False):
    n = len(parts)
    by_entry = relay_before_end is not None
    relay_before_end = relay_before_end or [0] * n

    def ici(p, st, sm, e, j, chip_j, x, y, c):
        k, kj = 2 * x + y, 2 * chip_j[0] + chip_j[1]
        return _remote(p[e].at[kj], st[e].at[c, k], sm[0].at[8 * e + j], sm[1].at[8 * e + j], (*chip_j, c))

    def own(p, st, sm, e, x, y, c):
        k = 2 * x + y
        return _remote(p[e].at[k], st[e].at[c, k], sm[0].at[8 * e + 3], sm[1].at[8 * e + 3], (x, y, 1 - c))

    def forward(st, sm, e, j, chip_j, x, y, c, sender_c):
        kj = 2 * chip_j[0] + chip_j[1]
        r = st[e].at[sender_c, kj]
        return _remote(r, r, sm[0].at[8 * e + 4 + j], sm[1].at[8 * e + 4 + j], (x, y, 1 - c))

    def local(p, st, sm, e, x, y, c):
        k = 2 * x + y
        return pltpu.make_async_copy(p[e].at[k], st[e].at[c, k], sm[2].at[e])

    def issue(e, p, st, sm):
        x, y, c = _place()
        for j, chip_j in enumerate(_other_chips(x, y)):
            ici(p, st, sm, e, j, chip_j, x, y, c).start()
        local(p, st, sm, e, x, y, c).start()
        own(p, st, sm, e, x, y, c).start()

    def start(p, st, sm, before_slot=None):
        x, y, c = _place()
        if by_entry:
            for e in range(n):
                issue(e, p, st, sm)
            return
        for j, chip_j in enumerate(_other_chips(x, y)):
            if before_slot is not None:
                before_slot(j, 2 * chip_j[0] + chip_j[1])
            for e in range(n):
                ici(p, st, sm, e, j, chip_j, x, y, c).start()
        if before_slot is not None:
            before_slot(3, 2 * x + y)
        for e in range(n):
            local(p, st, sm, e, x, y, c).start()
            own(p, st, sm, e, x, y, c).start()

    def relay(e, p, st, sm):
        x, y, c = _place()
        for j, chip_j in enumerate(_other_chips(x, y)):
            kj = 2 * chip_j[0] + chip_j[1]
            r = st[e].at[c, kj]
            _remote(r, r, sm[0].at[8 * e + j], sm[1].at[8 * e + j], (*chip_j, c)).wait_recv()
            forward(st, sm, e, j, chip_j, x, y, c, c).start()

    def finish(p, st, sm):
        x, y, c = _place()
        k = 2 * x + y
        chips = _other_chips(x, y)
        for e in range(n):
            r = st[e].at[1 - c, k]
            _remote(r, r, sm[0].at[8 * e + 3], sm[1].at[8 * e + 3], (x, y, 1 - c)).wait_recv()
            for j, chip_j in enumerate(chips):
                forward(st, sm, e, j, chip_j, x, y, c, 1 - c).wait_recv()
        for e in range(n):
            own(p, st, sm, e, x, y, c).wait_send()
            for j, chip_j in enumerate(chips):
                ici(p, st, sm, e, j, chip_j, x, y, c).wait_send()
                forward(st, sm, e, j, chip_j, x, y, c, c).wait_send()
            local(p, st, sm, e, x, y, c).wait()

    outs = [SDS((2,) + a.shape, a.dtype) for a in parts]
    ex = Exchange(list(parts), outs, [DMA((8 * n,)), DMA((8 * n,)), DMA((n,))], start, finish,
                  [(relay_before_end[e], functools.partial(relay, e)) for e in range(n)])
    return (ex, issue) if want_issue else ex


def reduce_scatter_exchange(grads, nsteps, load_step, send_step, relay_step):
    n = len(grads)
    hrs = [g.shape[2] for g in grads]
    C = grads[0].shape[3]
    scatter, issue = scatter_exchange([SDS((4,) + g.shape[2:], WIRE) for g in grads], want_issue=True)
    hand_on = [fn for _, fn in scatter.relays]

    def refs(xs):
        return xs[:3], xs[3], xs[4], xs[5], xs[6], xs[7:7 + n], xs[7 + n:]

    def push(e, g, psem, qsem, sib_st):
        x, y, c = _place()
        return _remote(g[e].at[:, 1 - c], sib_st[e], psem.at[e], qsem.at[e], (x, y, 1 - c))

    def fetch(e, g, lsem, own_st):
        _, _, c = _place()
        return pltpu.make_async_copy(g[e].at[:, c], own_st.at[e % 2, :, pl.ds(0, hrs[e])], lsem.at[e])

    def start(g, xo, xs):
        _, _, psem, qsem, _, sib_st, _ = refs(xs)
        for e in range(n):
            push(e, g, psem, qsem, sib_st).start()

    def load(e, g, xo, xs):
        _, lsem, _, _, own_st, _, _ = refs(xs)
        fetch(e, g, lsem, own_st).start()

    def send(e, g, xo, xs):
        sm, lsem, psem, qsem, own_st, sib_st, part = refs(xs)
        fetch(e, g, lsem, own_st).wait()
        push(e, g, psem, qsem, sib_st).wait_recv()
        part[e][...] = (own_st[e % 2, :, 0:hrs[e]] + sib_st[e][...]).astype(WIRE)
        issue(e, part, xo, sm)

    def relay(e, g, xo, xs):
        sm, _, _, _, _, _, part = refs(xs)
        hand_on[e](part, xo, sm)

    def finish(g, xo, xs):
        sm, _, psem, qsem, _, sib_st, part = refs(xs)
        scatter.finish(part, xo, sm)
        for e in range(n):
            push(e, g, psem, qsem, sib_st).wait_send()

    plan = sorted([(min(step[e], nsteps - 1), phase, e) for phase, step in enumerate((load_step, send_step, relay_step))
                   for e in range(n)])
    stage = (load, send, relay)
    relays = [(nsteps - 1 - at, functools.partial(stage[phase], e)) for at, phase, e in plan]
    scratch = (list(scatter.sems) + [DMA((n,)), DMA((n,)), DMA((n,))] + [pltpu.VMEM((2, 4, max(hrs), C), f32)]
               + [pltpu.VMEM((4, hr, C), f32) for hr in hrs] + [pltpu.VMEM((4, hr, C), WIRE) for hr in hrs])
    return Exchange(list(grads), scatter.outs, scratch, start, finish, relays)


def tail_reduce(d_norm_mix, d_norm_mem, d_norm_ffn, d_gains, d_cw8, d_cbias, d_qg, d_kg, d_mqg, d_mkg, d_sink8, loss8, tail):
    n = len(tail)
    scatter = scatter_exchange([SDS((4,) + a.shape[2:], WIRE) for a in tail])

    def half_copy(g, sib, hsem, e, j, slot, x, y, c):
        return _remote(g[e].at[slot, 1 - c], sib[e].at[slot], hsem[0].at[4 * e + j], hsem[1].at[4 * e + j], (x, y, 1 - c))

    def body(nm_ref, nmem_ref, nf_ref, gn_ref, cw_ref, cb_ref, qg_ref, kg_ref, mqg_ref, mkg_ref, sk_ref, ls_ref, *rest):
        g, o_ref, st = rest[:n], rest[n], rest[n + 1:2 * n + 1]
        buf, ssem, rsem = rest[2 * n + 1:2 * n + 4]
        own, sib, part = (rest[2 * n + 4 + i * n:2 * n + 4 + (i + 1) * n] for i in range(3))
        lsem = rest[5 * n + 4]
        hsem, xsem = rest[5 * n + 5:5 * n + 7], rest[5 * n + 7:]
        x, y, c = _place()
        loads = [pltpu.make_async_copy(g[e].at[:, c], own[e], lsem.at[e]) for e in range(n)]
        for ld in loads:
            ld.start()
        for j, slot in enumerate([2 * cx + cy for cx, cy in _other_chips(x, y)] + [2 * x + y]):
            for e in range(n):
                half_copy(g, sib, hsem, e, j, slot, x, y, c).start()
        me = 4 * x + 2 * y + c
        mine = buf.at[me]
        mine[...] = jnp.zeros((8, 1024), f32)
        mine[0:1, :] = nm_ref[...]
        mine[1:2, :] = nmem_ref[...]
        mine[2:3, :] = nf_ref[...]
        mine[3:4, :] = gn_ref[...]
        for j in range(3):
            mine[4:5, pl.ds(j * CONV_W, CONV_W)] = cw_ref[j:j + 1, :]
        mine[4:5, pl.ds(3 * CONV_W, CONV_W)] = cb_ref[...]
        for j, r in enumerate((qg_ref, kg_ref, mqg_ref, mkg_ref)):
            mine[5:6, pl.ds(j * HD, HD)] = r[...]
        mine[5:6, pl.ds(256, 128)] = sk_ref[0:1, :]
        mine[5:6, pl.ds(384, 128)] = ls_ref[0:1, :]

        def peer_of(m):
            return (1 - x if m & 4 else x, 1 - y if m & 2 else y, 1 - c if m & 1 else c)

        for m in range(1, 8):
            _remote(mine, mine, ssem.at[m - 1], rsem.at[m - 1], peer_of(m)).start()
        for ld in loads:
            ld.wait()

        def chip_partial(j, slot):
            for e in range(n):
                half_copy(g, sib, hsem, e, j, slot, x, y, c).wait()
                part[e][slot] = (own[e][slot] + sib[e][slot]).astype(WIRE)

        scatter.start(part, st, xsem, chip_partial)
        for _, hand_on in scatter.relays:
            hand_on(part, st, xsem)
        scatter.finish(part, st, xsem)
        for m in range(1, 8):
            p = peer_of(m)
            got = buf.at[4 * p[0] + 2 * p[1] + p[2]]
            _remote(got, got, ssem.at[m - 1], rsem.at[m - 1], p).wait_recv()
        for m in range(1, 8):
            _remote(mine, mine, ssem.at[m - 1], rsem.at[m - 1], peer_of(m)).wait_send()
        acc = buf[0]
        for d in range(1, 8):
            acc = acc + buf[d]
        o_ref[...] = acc

    ins = [d_norm_mix, d_norm_mem, d_norm_ffn, d_gains, d_cw8, d_cbias, d_qg, d_kg, d_mqg, d_mkg, d_sink8, loss8]
    half_shape = [(4,) + a.shape[2:] for a in tail]
    scratch = ([pltpu.VMEM((8, 8, 1024), f32), DMA((7,)), DMA((7,))]
               + [pltpu.VMEM(s, f32) for s in half_shape] * 2 + [pltpu.VMEM(s, WIRE) for s in half_shape]
               + [DMA((n,)), DMA((4 * n,)), DMA((4 * n,))] + list(scatter.sems))
    res = _run("tail_reduce", body, (), ins + list(tail), [VM] * len(ins) + [ANY] * n,
               [SDS((8, 1024), f32)] + list(scatter.outs), [VM] + [ANY] * n, scratch=scratch, vmem_mib=40)
    return res[0], res[1:]


def _adamw_math(w, g, m, v):
    m = ADAM_B1 * m + (1.0 - ADAM_B1) * g
    v = ADAM_B2 * v + (1.0 - ADAM_B2) * (g * g)
    m_hat = m / (1.0 - ADAM_B1 ** ADAM_STEP)
    v_hat = v / (1.0 - ADAM_B2 ** ADAM_STEP)
    delta = -ADAM_LR * (m_hat / (jnp.sqrt(v_hat) + ADAM_EPS) + ADAM_WD * w)
    return delta, m, v


def _sum_chips(st):
    return ((st[0].astype(f32) + st[1].astype(f32)) + st[2].astype(f32)) + st[3].astype(f32)


def adamw_big(name, stages, ws, ms, vs, nstep, exchange=None):
    n = len(stages)

    def body(*refs):
        st, w, m, v = refs[:n], refs[n:2 * n], refs[2 * n:3 * n], refs[3 * n:4 * n]
        outs = refs[4 * n:]
        for e in range(n):
            g = jnp.concatenate([_sum_chips(st[e].at[0]), _sum_chips(st[e].at[1])], axis=0)
            d, mm, vv = _adamw_math(w[e][...], g, m[e][...], v[e][...])
            outs[4 * e][...] = g
            outs[4 * e + 1][...] = d
            outs[4 * e + 2][...] = mm
            outs[4 * e + 3][...] = vv

    st_specs, w_specs = [], []
    for e in range(n):
        _, _, hr, C = stages[e].shape
        st_specs.append(pl.BlockSpec((2, 4, hr, C // nstep), lambda i: (0, 0, 0, i)))
        w_specs.append(pl.BlockSpec((2 * hr, C // nstep), lambda i: (0, i)))
    out_specs = [s for s in w_specs for _ in range(4)]
    out_shape = [SDS(w.shape, f32) for w in ws for _ in range(4)]
    res = _run(name, body, (nstep,), list(stages) + list(ws) + list(ms) + list(vs), st_specs + w_specs * 3,
               out_shape, out_specs, vmem_mib=16, exchange=exchange)
    res, sent = res if exchange is not None else (res, None)
    return [res[4 * e:4 * e + 4] for e in range(n)], sent


def adamw_small(tot, pk_w, pk_m, pk_v, shapes):
    def body(tot_ref, w_ref, m_ref, v_ref, *outs):
        x, y, _ = _place()
        chip = 2 * x + y
        taps = []
        for j in range(3):
            mine = tot_ref[4:5, j * CONV_W:j * CONV_W + HD]
            for s in range(1, 4):
                mine = jnp.where(chip == s, tot_ref[4:5, j * CONV_W + s * HD:j * CONV_W + (s + 1) * HD], mine)
            taps.append(mine)
        row4 = jnp.concatenate(taps + [jnp.zeros((1, 3 * CONV_W - 3 * HD), f32), tot_ref[4:5, 3 * CONV_W:]], axis=1)
        tot_v = tot_ref[...]
        row = lax.broadcasted_iota(jnp.int32, tot_v.shape, 0)
        g = jnp.where(row == 4, jnp.broadcast_to(row4, tot_v.shape), tot_v)
        d, mm, vv = _adamw_math(w_ref[...], g, m_ref[...], v_ref[...])
        for i, name in enumerate(SMALL):
            for k, val in enumerate((g, d, mm, vv)):
                if name == "conv_w":
                    outs[4 * i + k][...] = jnp.concatenate([val[4:5, j * HD:(j + 1) * HD] for j in range(3)], axis=0)[None]
                else:
                    r, c0, w = SMALL_AT[name]
                    outs[4 * i + k][...] = val[r:r + 1, c0:c0 + w]

    out_shape = [SDS(shapes[k], f32) for k in SMALL for _ in range(4)]
    res = _run("adamw_small", body, (), [tot, pk_w, pk_m, pk_v], [VM] * 4, out_shape, [VM] * len(out_shape))
    return {k: res[4 * i:4 * i + 4] for i, k in enumerate(SMALL)}


def prep_weights(name, shards, exchange=None):
    n = len(shards)

    def body(*refs):
        for e in range(n):
            refs[n + e][...] = _c(refs[e][...])

    return _run(name, body, (), shards, [VM] * n, [SDS(a.shape, MXU) for a in shards], [VM] * n, vmem_mib=16, exchange=exchange)


def mem_kv_fwd(mem2d, pk, wmkv):
    M, D = mem2d.shape

    def body(m_ref, pk_ref, w_ref, mn_ref, kv_ref, km_ref, vm_ref):
        m = m_ref[...]
        mn = _c(m * _rstd(m) * _small(pk_ref, "norm_mem"))
        mn_ref[...] = mn
        kv = _nn(mn, w_ref[...])
        kv_ref[...] = kv
        kk = kv[:, :MEM_W]
        km_ref[...] = _c(kk * _heads_rstd(kk) * _lanes(_small(pk_ref, "mem_k_norm"), MEM_W))
        vm_ref[...] = _c(kv[:, MEM_W:])

    return _run("mem_kv_fwd", body, (), [mem2d, pk, wmkv], [VM] * 3,
                [SDS((M, D), MXU), SDS((M, 2 * MEM_W), f32), SDS((M, MEM_W), MXU), SDS((M, MEM_W), MXU)], [VM] * 4)


QKV_W = ATT_W + 2 * KV_W + MEM_W


def in_proj_fwd(x2d, pk, winT, tm, exchange):
    T, D = x2d.shape
    P = winT.shape[0]

    def body(x_ref, pk_ref, w_ref, xn_ref, proj_ref, qkv_ref):
        xv = x_ref[...]
        xn = _c(xv * _rstd(xv) * _small(pk_ref, "norm_mix"))
        xn_ref[...] = xn
        proj = _nt(xn, w_ref[...])
        proj_ref[...] = proj
        q, k = proj[:, :ATT_W], proj[:, ATT_W:ATT_W + KV_W]
        qm = proj[:, P - MEM_W:]
        qkv_ref[...] = jnp.concatenate(
            [_c(q * _heads_rstd(q) * _lanes(_small(pk_ref, "q_norm"), ATT_W)),
             _c(k * _heads_rstd(k) * _lanes(_small(pk_ref, "k_norm"), KV_W)),
             _c(proj[:, ATT_W + KV_W:ATT_W + 2 * KV_W]),
             _c(qm * _heads_rstd(qm) * _lanes(_small(pk_ref, "mem_q_norm"), MEM_W))], axis=1)

    return _run("in_proj_fwd", body, (T // tm,), [x2d, pk, winT],
                [pl.BlockSpec((tm, D), lambda i: (i, 0)), VM, VM],
                [SDS((T, D), MXU), SDS((T, P), f32), SDS((T, QKV_W), MXU)],
                [pl.BlockSpec((tm, D), lambda i: (i, 0)), pl.BlockSpec((tm, P), lambda i: (i, 0)),
                 pl.BlockSpec((tm, QKV_W), lambda i: (i, 0))],
                vmem_mib=40, exchange=exchange)


def _swa_bias_table():
    r = np.arange(GQA * BLK)[:, None]
    k = np.arange(2 * BLK)[None, :]
    dist = (r % BLK) + BLK - k
    band = (dist >= 0) & (dist < BLK)
    tab = np.empty((2, N_KV, GQA * BLK, 2 * BLK), np.float32)
    for later in range(2):
        valid = band & ((k >= BLK) | (later == 1))
        for g in range(N_KV):
            slope = 2.0 ** -(g * GQA + r // BLK + 1.0)
            tab[later, g] = np.where(valid, -slope * dist, NEG)
    return jnp.asarray(tab)


def _sink_column(g, sk_ref):
    hrow = lax.broadcasted_iota(jnp.int32, (GQA * BLK, 1), 0) // BLK
    sink = jnp.zeros((GQA * BLK, 1), f32)
    for hh in range(GQA):
        sink = jnp.where(hrow == hh, sk_ref[g * GQA + hh:g * GQA + hh + 1, 0:1], sink)
    return sink


def _stack_heads(v, g):
    return jnp.concatenate([v[:, (g * GQA + hh) * HD:(g * GQA + hh + 1) * HD] for hh in range(GQA)], axis=0)


def attn_fwd(qkv, sink_rows, BL, S, exchange, qb=2):
    NS = S // (qb * BLK)
    T = BL * S

    def body(q_ref, kc_ref, kp_ref, vc_ref, vp_ref, sk_ref, tab_ref, o_ref):
        j = pl.program_id(1)
        kall = jnp.concatenate([kp_ref[...], kc_ref[...]], axis=0)
        vall = jnp.concatenate([vp_ref[...], vc_ref[...]], axis=0)
        ones = jnp.ones((2 * BLK, HD), MXU)
        for b in range(qb):
            q = q_ref[pl.ds(b * BLK, BLK), :]
            k2, v2 = kall[b * BLK:(b + 2) * BLK], vall[b * BLK:(b + 2) * BLK]
            later = jnp.minimum(j, 1) if b == 0 else 1
            for g in range(N_KV):
                kn, vh = k2[:, g * HD:(g + 1) * HD], v2[:, g * HD:(g + 1) * HD]
                s = _nt(_stack_heads(q, g), kn) * (HD ** -0.5) + tab_ref[later, g]
                e, es = _exp_scores(s, _sink_column(g, sk_ref))
                eb = _c(e)
                o = _nn(eb, vh) * (1.0 / (_nn(eb, ones) + es))
                for hh in range(GQA):
                    o_ref[pl.ds(b * BLK, BLK), pl.ds((g * GQA + hh) * HD, HD)] = o[hh * BLK:(hh + 1) * BLK]

    cur = lambda col: (lambda b, j: (b * NS + j, col))
    prev = lambda col: (lambda b, j: (qb * (b * NS + j) - jnp.minimum(j, 1), col))
    return _run("attn_fwd", body, (BL, NS), [qkv, qkv, qkv, qkv, qkv, sink_rows, _swa_bias_table()],
                [pl.BlockSpec((qb * BLK, ATT_W), cur(0)),
                 pl.BlockSpec((qb * BLK, KV_W), cur(4)), pl.BlockSpec((BLK, KV_W), prev(4)),
                 pl.BlockSpec((qb * BLK, KV_W), cur(5)), pl.BlockSpec((BLK, KV_W), prev(5)),
                 pl.BlockSpec((8, 128), lambda b, j: (0, 0)), VM],
                [SDS((T, ATT_W), f32)], [pl.BlockSpec((qb * BLK, ATT_W), cur(0))], exchange=exchange)


def _conv_taps(u, uh):
    row = lax.broadcasted_iota(jnp.int32, u.shape, 0)
    u1 = jnp.where(row == 0, uh[7:8, :], pltpu.roll(u, 1, 0))
    u2 = jnp.where(row == 0, uh[6:7, :], jnp.where(row == 1, uh[7:8, :], pltpu.roll(u, 2, 0)))
    return u1, u2


def _mem_head(qm, km, vm, h):
    qh, kh, vh = (a[:, h * HD:(h + 1) * HD] for a in (qm, km, vm))
    e, _ = _exp_scores(_nt(qh, kh) * (HD ** -0.5))
    return qh, kh, vh, e


def mixer_tail_fwd(x2d, attn_out, proj, qkv, km, vm, conv_w8, pk, wout, S, tm, exchange):
    T, D = x2d.shape
    NM = km.shape[0] // (T // S)

    def body(x_ref, ao_ref, ch_ref, cb_ref, cc_ref, chh_ref, cch_ref, qm_ref, km_ref, vm_ref, cw_ref, pk_ref,
             wout_ref, co_ref, mo_ref, mg_ref, x1_ref, h_ref):
        first = (pl.program_id(0) * tm) % S == 0
        u = cc_ref[...] * ch_ref[...]
        uh = jnp.where(first, 0.0, cch_ref[...] * chh_ref[...])
        u1, u2 = _conv_taps(u, uh)
        conv = cw_ref[0:1, :] * u2 + cw_ref[1:2, :] * u1 + cw_ref[2:3, :] * u + _small(pk_ref, "conv_b")
        conv_out = cb_ref[...] * conv
        co_ref[...] = conv_out
        qm, kmv, vmv = qm_ref[...], km_ref[...], vm_ref[...]
        ones = jnp.ones((NM, HD), MXU)
        for h in range(N_MEMH):
            _, _, vh, e = _mem_head(qm, kmv, vmv, h)
            eb = _c(e)
            mo_ref[:, pl.ds(h * HD, HD)] = _nn(eb, vh) * (1.0 / _nn(eb, ones))
        mem_out = mo_ref[...]
        ao = ao_ref[...]
        merged = _c(jnp.concatenate([ao * _rstd(ao) * _small(pk_ref, "out_norm_attn"),
                                     conv_out * _rstd(conv_out) * _small(pk_ref, "out_norm_conv"),
                                     mem_out * _rstd(mem_out) * _small(pk_ref, "out_norm_mem")], axis=1))
        mg_ref[...] = merged
        x1 = x_ref[...] + _nn(merged, wout_ref[...])
        x1_ref[...] = x1
        h_ref[...] = _c(x1 * _rstd(x1) * _small(pk_ref, "norm_ffn"))

    tile = lambda w, col: pl.BlockSpec((tm, w), lambda i: (i, col))
    halo = lambda col: pl.BlockSpec((8, CONV_W), lambda i: (jnp.maximum(i * (tm // 8) - 1, 0), col))
    seq = pl.BlockSpec((NM, MEM_W), lambda i: ((i * tm) // S, 0))
    small = lambda a: pl.BlockSpec(a.shape, lambda i: (0, 0))
    return _run("mixer_tail_fwd", body, (T // tm,),
                [x2d, attn_out, proj, proj, proj, proj, proj, qkv, km, vm, conv_w8, pk, wout],
                [tile(D, 0), tile(ATT_W, 0), tile(CONV_W, 3), tile(CONV_W, 4), tile(CONV_W, 5), halo(3), halo(5),
                 tile(MEM_W, 3), seq, seq, VM, VM, VM],
                [SDS((T, CONV_W), f32), SDS((T, MEM_W), f32), SDS((T, D), MXU), SDS((T, D), f32), SDS((T, D), MXU)],
                [tile(CONV_W, 0), tile(MEM_W, 0), tile(D, 0), tile(D, 0), tile(D, 0)], vmem_mib=40, exchange=exchange)


def ffn_fwd_bwd(h, x1, tgt, wgT, wuT, wd, pk, tm):
    T, D = x1.shape
    F = wd.shape[0]

    def body(h_ref, x1_ref, t_ref, wg_ref, wu_ref, wd_ref, pk_ref,
             dx1_ref, dx2_ref, act_ref, dg_ref, du_ref, loss_ref, dgf_ref):
        @pl.when(pl.program_id(0) == 0)
        def _():
            loss_ref[...] = jnp.zeros_like(loss_ref)
            dgf_ref[...] = jnp.zeros_like(dgf_ref)

        hv = h_ref[...]
        gate = _nt(hv, wg_ref[...])
        up = _nt(hv, wu_ref[...])
        sg = jax.nn.sigmoid(gate)
        sl = gate * sg
        act = _c(sl * up)
        act_ref[...] = act
        x1v = x1_ref[...]
        diff = (x1v + _nn(act, wd_ref[...])) - t_ref[...]
        loss_ref[...] += 0.5 * jnp.sum(jnp.sum(diff * diff, axis=-1, keepdims=True) / D, axis=0, keepdims=True)
        dx2 = diff / D
        dx2b = _c(dx2)
        dx2_ref[...] = dx2b
        d_act = _nt(dx2b, wd_ref[...])
        d_up = _c(d_act * sl)
        d_gate = _c(d_act * up * (sg * (1.0 + gate * (1.0 - sg))))
        du_ref[...] = d_up
        dg_ref[...] = d_gate
        dh = _nn(d_gate, wg_ref[...]) + _nn(d_up, wu_ref[...])
        dv, dgf = _norm_bwd(dh, x1v, _rstd(x1v), _small(pk_ref, "norm_ffn"))
        dx1_ref[...] = dx2 + dv
        dgf_ref[...] += dgf

    tile = lambda w: pl.BlockSpec((tm, w), lambda i: (i, 0))
    return _run("ffn_fwd_bwd", body, (T // tm,), [h, x1, tgt, wgT, wuT, wd, pk],
                [tile(D), tile(D), tile(D), VM, VM, VM, VM],
                [SDS((T, D), f32), SDS((T, D), MXU), SDS((T, F), MXU), SDS((T, F), MXU), SDS((T, F), MXU),
                 SDS((8, 128), f32), SDS((1, D), f32)],
                [tile(D), tile(D), tile(F), tile(F), tile(F), pl.BlockSpec((8, 128), lambda i: (0, 0)),
                 pl.BlockSpec((1, D), lambda i: (0, 0))], vmem_mib=56)


def matmul_tn(a, b, name, tmo, tk):
    T, M = a.shape
    N = b.shape[1]

    def body(a_ref, b_ref, o_ref):
        @pl.when(pl.program_id(1) == 0)
        def _():
            o_ref[...] = jnp.zeros_like(o_ref)

        o_ref[...] += _tn(a_ref[...], b_ref[...])

    return _run(name, body, (M // tmo, T // tk), [a, b],
                [pl.BlockSpec((tk, tmo), lambda m, k: (k, m)), pl.BlockSpec((tk, N), lambda m, k: (k, 0))],
                [SDS((M, N), f32)], [pl.BlockSpec((tmo, N), lambda m, k: (m, 0))], vmem_mib=48)[0]


def out_proj_bwd(dx1, merged, attn_out, conv_out, mem_out, pk, wout, tm):
    T, D = dx1.shape

    def body(dx1_ref, mg_ref, ao_ref, co_ref, mo_ref, pk_ref, w_ref,
             dao_ref, dco_ref, dmo_ref, dw_ref, dgain_ref):
        @pl.when(pl.program_id(0) == 0)
        def _():
            dw_ref[...] = jnp.zeros_like(dw_ref)
            dgain_ref[...] = jnp.zeros_like(dgain_ref)

        dxb = _c(dx1_ref[...])
        dw_ref[...] += _tn(mg_ref[...], dxb)
        dmg = _nt(dxb, w_ref[...])
        ao, co, mo = ao_ref[...], co_ref[...], mo_ref[...]
        da, ga = _norm_bwd(dmg[:, :ATT_W], ao, _rstd(ao), _small(pk_ref, "out_norm_attn"))
        dc, gc = _norm_bwd(dmg[:, ATT_W:ATT_W + CONV_W], co, _rstd(co), _small(pk_ref, "out_norm_conv"))
        dm, gm = _norm_bwd(dmg[:, ATT_W + CONV_W:], mo, _rstd(mo), _small(pk_ref, "out_norm_mem"))
        dao_ref[...] = da
        dco_ref[...] = dc
        dmo_ref[...] = dm
        dgain_ref[...] += jnp.concatenate([ga, gc, gm], axis=1)

    tile = lambda w: pl.BlockSpec((tm, w), lambda i: (i, 0))
    return _run("out_proj_bwd", body, (T // tm,), [dx1, merged, attn_out, conv_out, mem_out, pk, wout],
                [tile(D), tile(D), tile(ATT_W), tile(CONV_W), tile(MEM_W), VM, VM],
                [SDS((T, ATT_W), f32), SDS((T, CONV_W), f32), SDS((T, MEM_W), f32), SDS((D, D), f32), SDS((1, D), f32)],
                [tile(ATT_W), tile(CONV_W), tile(MEM_W), pl.BlockSpec((D, D), lambda i: (0, 0)),
                 pl.BlockSpec((1, D), lambda i: (0, 0))], vmem_mib=40)


def attn_bwd(qkv, d_attn, attn_out, sink_rows, BL, S, exchange):
    NB = S // BLK
    T = BL * S

    def body(q_ref, kc_ref, kp_ref, vc_ref, vp_ref, do_ref, ao_ref, sk_ref, tab_ref,
             dq_ref, dk_ref, dv_ref, dsk_ref, pend_k, pend_v):
        b, j = pl.program_id(0), pl.program_id(1)

        @pl.when((b == 0) & (j == 0))
        def _():
            dsk_ref[...] = jnp.zeros_like(dsk_ref)

        @pl.when(j == 0)
        def _():
            pend_k[...] = jnp.zeros_like(pend_k)
            pend_v[...] = jnp.zeros_like(pend_v)

        @pl.when(j < NB)
        def _():
            q, do, ao = q_ref[...], do_ref[...], ao_ref[...]
            k2 = jnp.concatenate([kp_ref[...], kc_ref[...]], axis=0)
            v2 = jnp.concatenate([vp_ref[...], vc_ref[...]], axis=0)
            lane = lax.broadcasted_iota(jnp.int32, (8, 128), 1)
            ones_w = jnp.ones((2 * BLK, 2 * BLK), MXU)
            dsk = jnp.zeros((8, 128), f32)
            dks, dvs = [], []
            for g in range(N_KV):
                kn, vh = k2[:, g * HD:(g + 1) * HD], v2[:, g * HD:(g + 1) * HD]
                qs = _stack_heads(q, g)
                s = _nt(qs, kn) * (HD ** -0.5) + tab_ref[g]
                e, es = _exp_scores(s, _sink_column(g, sk_ref))
                eb = _c(e)
                inv_w = 1.0 / (_nn(eb, ones_w) + es)
                inv_n = inv_w[:, :HD]
                dos = _stack_heads(do, g)
                delta = _rowsum_mxu(dos * _stack_heads(ao, g), 2 * BLK)
                dp = _nt(_c(dos), vh)
                ds = _c(e * inv_w * (dp - delta) * (HD ** -0.5))
                t = es * inv_n[:, 0:1] * delta[:, 0:1]
                for hh in range(GQA):
                    dsk = dsk + jnp.where(lane == g * GQA + hh, -jnp.sum(t[hh * BLK:(hh + 1) * BLK]), 0.0)
                dvs.append(_tn(eb, _c(dos * inv_n)))
                dks.append(_tn(ds, qs))
                dqs = _nn(ds, kn)
                for hh in range(GQA):
                    dq_ref[:, pl.ds((g * GQA + hh) * HD, HD)] = dqs[hh * BLK:(hh + 1) * BLK]
            dk2 = jnp.concatenate(dks, axis=1)
            dv2 = jnp.concatenate(dvs, axis=1)
            dk_ref[...] = pend_k[...] + dk2[:BLK]
            dv_ref[...] = pend_v[...] + dv2[:BLK]
            pend_k[...] = dk2[BLK:]
            pend_v[...] = dv2[BLK:]
            dsk_ref[...] += dsk

        @pl.when(j == NB)
        def _():
            dk_ref[...] = pend_k[...]
            dv_ref[...] = pend_v[...]

    cur = lambda col: (lambda b, j: (b * NB + jnp.minimum(j, NB - 1), col))
    prev = lambda col: (lambda b, j: (b * NB + jnp.maximum(j - 1, 0), col))
    small = lambda shape: pl.BlockSpec(shape, lambda b, j: (0, 0))
    return _run("attn_bwd", body, (BL, NB + 1), [qkv, qkv, qkv, qkv, qkv, d_attn, attn_out, sink_rows, _swa_bias_table()],
                [pl.BlockSpec((BLK, ATT_W), cur(0)),
                 pl.BlockSpec((BLK, KV_W), cur(4)), pl.BlockSpec((BLK, KV_W), prev(4)),
                 pl.BlockSpec((BLK, KV_W), cur(5)), pl.BlockSpec((BLK, KV_W), prev(5)),
                 pl.BlockSpec((BLK, ATT_W), cur(0)), pl.BlockSpec((BLK, ATT_W), cur(0)), small((8, 128)),
                 pl.BlockSpec((None, N_KV, GQA * BLK, 2 * BLK), lambda b, j: (jnp.minimum(j, 1), 0, 0, 0))],
                [SDS((T, ATT_W), f32), SDS((T, KV_W), f32), SDS((T, KV_W), f32), SDS((8, 128), f32)],
                [pl.BlockSpec((BLK, ATT_W), cur(0)), pl.BlockSpec((BLK, KV_W), prev(0)),
                 pl.BlockSpec((BLK, KV_W), prev(0)), small((8, 128))],
                scratch=[pltpu.VMEM((BLK, KV_W), f32)] * 2, vmem_mib=56, exchange=exchange)


def mem_conv_bwd(d_mem_out, mem_out, d_conv_out, proj, qkv, km, vm, conv_w8, pk, S, tm, exchange):
    T = d_mem_out.shape[0]
    NM = km.shape[0] // (T // S)

    def body(dmo_ref, mo_ref, dco_ref, ch_ref, cb_ref, cc_ref, chh_ref, cch_ref, qm_ref, km_ref, vm_ref, cw_ref,
             pk_ref, dqm_ref, dkm_ref, dvm_ref, dcb_ref, dcv_ref, dcw_ref, dcbias_ref):
        i = pl.program_id(0)
        first = (i * tm) % S == 0

        @pl.when(i == 0)
        def _():
            dcw_ref[...] = jnp.zeros_like(dcw_ref)
            dcbias_ref[...] = jnp.zeros_like(dcbias_ref)

        @pl.when(first)
        def _():
            dkm_ref[...] = jnp.zeros_like(dkm_ref)
            dvm_ref[...] = jnp.zeros_like(dvm_ref)

        qm, kmv, vmv, dmo, mo = qm_ref[...], km_ref[...], vm_ref[...], dmo_ref[...], mo_ref[...]
        ones_w = jnp.ones((NM, NM), MXU)
        for h in range(N_MEMH):
            qh, kh, vh, e = _mem_head(qm, kmv, vmv, h)
            eb = _c(e)
            doh = dmo[:, h * HD:(h + 1) * HD]
            delta = _rowsum_mxu(doh * mo[:, h * HD:(h + 1) * HD], NM)
            dp = _nt(_c(doh), vh)
            inv_w = 1.0 / _nn(eb, ones_w)
            ds = _c(e * inv_w * (dp - delta) * (HD ** -0.5))
            dvm_ref[:, pl.ds(h * HD, HD)] += _tn(eb, _c(doh * inv_w[:, :HD]))
            dkm_ref[:, pl.ds(h * HD, HD)] += _tn(ds, qh)
            dqm_ref[:, pl.ds(h * HD, HD)] = _nn(ds, kh)

        u = cc_ref[...] * ch_ref[...]
        uh = jnp.where(first, 0.0, cch_ref[...] * chh_ref[...])
        u1, u2 = _conv_taps(u, uh)
        conv = cw_ref[0:1, :] * u2 + cw_ref[1:2, :] * u1 + cw_ref[2:3, :] * u + _small(pk_ref, "conv_b")
        dy = dco_ref[...]
        dcb_ref[...] = dy * conv
        dcv = dy * cb_ref[...]
        dcv_ref[...] = dcv
        dcbias_ref[...] += jnp.sum(dcv, axis=0, keepdims=True)
        dcw_ref[0:1, :] += jnp.sum(dcv * u2, axis=0, keepdims=True)
        dcw_ref[1:2, :] += jnp.sum(dcv * u1, axis=0, keepdims=True)
        dcw_ref[2:3, :] += jnp.sum(dcv * u, axis=0, keepdims=True)

    tile = lambda w, col: pl.BlockSpec((tm, w), lambda i: (i, col))
    halo = lambda col: pl.BlockSpec((8, CONV_W), lambda i: (jnp.maximum(i * (tm // 8) - 1, 0), col))
    seq = pl.BlockSpec((NM, MEM_W), lambda i: ((i * tm) // S, 0))
    const = lambda shape: pl.BlockSpec(shape, lambda i: (0, 0))
    return _run("mem_conv_bwd", body, (T // tm,),
                [d_mem_out, mem_out, d_conv_out, proj, proj, proj, proj, proj, qkv, km, vm, conv_w8, pk],
                [tile(MEM_W, 0), tile(MEM_W, 0), tile(CONV_W, 0), tile(CONV_W, 3), tile(CONV_W, 4), tile(CONV_W, 5),
                 halo(3), halo(5), tile(MEM_W, 3), seq, seq, VM, VM],
                [SDS((T, MEM_W), f32), SDS(km.shape, f32), SDS(km.shape, f32),
                 SDS((T, CONV_W), f32), SDS((T, CONV_W), f32), SDS((8, CONV_W), f32), SDS((1, CONV_W), f32)],
                [tile(MEM_W, 0), seq, seq, tile(CONV_W, 0), tile(CONV_W, 0), const((8, CONV_W)), const((1, CONV_W))],
                vmem_mib=48, exchange=exchange)


def in_proj_bwd(dqn, dkn, dv, dcb, dcv, dqmn, proj, conv_w8, xn, x2d, dx1, pk, winT, S, tm, stages, ws, ms, vs):
    T, D = x2d.shape
    P = winT.shape[0]
    last_blk = T // 8 - 1
    n = len(stages)
    nsteps = T // tm
    tile_w = ws[0].shape[1] // (nsteps // 2)
    turn = [e * 2 // n for e in range(n)]

    def body(dq_ref, dk_ref, dv_ref, dcb_ref, dcv_ref, dcvn_ref, dqm_ref, qa_ref, ka_ref, ch_ref, cc_ref, qma_ref,
             cw_ref, xn_ref, x_ref, dx1_ref, pk_ref, w_ref, *rest):
        st, aw, am, av = (rest[k * n:(k + 1) * n] for k in range(4))
        dx_ref, dw_ref, dg_ref, dqg_ref, dkg_ref, dmqg_ref = rest[4 * n:4 * n + 6]
        aouts = rest[4 * n + 6:]
        i = pl.program_id(0)

        for parity in range(2):
            @pl.when(i % 2 == parity)
            def _(parity=parity):
                for e in range(n):
                    if turn[e] == parity:
                        g = jnp.concatenate([_sum_chips(st[e].at[0]), _sum_chips(st[e].at[1])], axis=0)
                        d, mm, vv = _adamw_math(aw[e][...], g, am[e][...], av[e][...])
                        for k, val in enumerate((g, d, mm, vv)):
                            aouts[4 * e + k][...] = val

        @pl.when(i == 0)
        def _():
            dw_ref[...] = jnp.zeros_like(dw_ref)
            dg_ref[...] = jnp.zeros_like(dg_ref)
            dqg_ref[...] = jnp.zeros_like(dqg_ref)
            dkg_ref[...] = jnp.zeros_like(dkg_ref)
            dmqg_ref[...] = jnp.zeros_like(dmqg_ref)

        dqa, gq = _heads_norm_bwd(dq_ref[...], qa_ref[...], _small(pk_ref, "q_norm"))
        dka, gk = _heads_norm_bwd(dk_ref[...], ka_ref[...], _small(pk_ref, "k_norm"))
        dqma, gmq = _heads_norm_bwd(dqm_ref[...], qma_ref[...], _small(pk_ref, "mem_q_norm"))
        dqg_ref[...] += gq
        dkg_ref[...] += gk
        dmqg_ref[...] += gmq

        last = ((i + 1) * tm) % S == 0
        dcv = dcv_ref[...]
        nxt = jnp.where(last, 0.0, dcvn_ref[...])
        row = lax.broadcasted_iota(jnp.int32, dcv.shape, 0)
        n1 = jnp.where(row == tm - 1, nxt[0:1, :], pltpu.roll(dcv, tm - 1, 0))
        n2 = jnp.where(row == tm - 2, nxt[0:1, :], jnp.where(row == tm - 1, nxt[1:2, :], pltpu.roll(dcv, tm - 2, 0)))
        du = cw_ref[2:3, :] * dcv + cw_ref[1:2, :] * n1 + cw_ref[0:1, :] * n2
        d_proj = jnp.concatenate([_c(dqa), _c(dka), _c(dv_ref[...]), _c(du * cc_ref[...]),
                                  _c(dcb_ref[...]), _c(du * ch_ref[...]), _c(dqma)], axis=1)
        dw_ref[...] += _tn(d_proj, xn_ref[...])
        xv = x_ref[...]
        dv_, dg = _norm_bwd(_nn(d_proj, w_ref[...]), xv, _rstd(xv), _small(pk_ref, "norm_mix"))
        dx_ref[...] = dx1_ref[...] + dv_
        dg_ref[...] += dg

    tile = lambda w, col=0: pl.BlockSpec((tm, w), lambda i: (i, col))
    nhalo = pl.BlockSpec((8, CONV_W), lambda i: (jnp.minimum((i + 1) * (tm // 8), last_blk), 0))
    const = lambda shape: pl.BlockSpec(shape, lambda i: (0, 0))
    st_specs = [pl.BlockSpec((2, 4, s.shape[2], tile_w), lambda i: (0, 0, 0, i // 2)) for s in stages]
    w_specs = [pl.BlockSpec((w.shape[0], tile_w), lambda i: (0, i // 2)) for w in ws]
    res = _run("in_proj_bwd", body, (nsteps,),
               [dqn, dkn, dv, dcb, dcv, dcv, dqmn, proj, proj, proj, proj, proj, conv_w8, xn, x2d, dx1, pk, winT]
               + list(stages) + list(ws) + list(ms) + list(vs),
               [tile(ATT_W), tile(KV_W), tile(KV_W), tile(CONV_W), tile(CONV_W), nhalo, tile(MEM_W),
                tile(ATT_W, 0), tile(KV_W, 4), tile(CONV_W, 3), tile(CONV_W, 5), tile(MEM_W, 6), VM,
                tile(D), tile(D), tile(D), VM, VM] + st_specs + w_specs * 3,
               [SDS((T, D), f32), SDS((P, D), f32), SDS((1, D), f32), SDS((1, HD), f32), SDS((1, HD), f32),
                SDS((1, HD), f32)] + [SDS(w.shape, f32) for w in ws for _ in range(4)],
               [tile(D), pl.BlockSpec((P, D), lambda i: (0, 0)), const((1, D)), const((1, HD)), const((1, HD)),
                const((1, HD))] + [s for s in w_specs for _ in range(4)],
               vmem_mib=56)
    return res[:6], [res[6 + 4 * e:10 + 4 * e] for e in range(n)]


def mem_kv_bwd(dkm, dvm, kv, memn, mem2d, pk, wmkv):
    def body(dkm_ref, dvm_ref, kv_ref, mn_ref, m_ref, pk_ref, w_ref, dw_ref, dg_ref, dkg_ref):
        dkk, dkg = _heads_norm_bwd(dkm_ref[...], kv_ref[:, :MEM_W], _small(pk_ref, "mem_k_norm"))
        dkg_ref[...] = dkg
        dkv = _c(jnp.concatenate([dkk, dvm_ref[...]], axis=1))
        dw_ref[...] = _tn(mn_ref[...], dkv)
        mv = m_ref[...]
        dg_ref[...] = jnp.sum(_nt(dkv, w_ref[...]) * mv * _rstd(mv), axis=0, keepdims=True)

    return _run("mem_kv_bwd", body, (), [dkm, dvm, kv, memn, mem2d, pk, wmkv], [VM] * 7,
                [SDS(wmkv.shape, f32), SDS((1, mem2d.shape[1]), f32), SDS((1, HD), f32)], [VM] * 3, vmem_mib=40)


def _halves_view(g):
    return g.reshape(4, 2, g.shape[0] // 8, g.shape[1])


def kernel(x, mem, norm_mix, w_in, q_norm, k_norm, attn_sinks, conv_w, conv_b, norm_mem, w_mem_kv, mem_q_norm, mem_k_norm, out_norm_attn, out_norm_conv, out_norm_mem, w_out, norm_ffn, w_gate, w_up, w_down, loss_target, m_norm_mix, m_w_in, m_q_norm, m_k_norm, m_attn_sinks, m_conv_w, m_conv_b, m_norm_mem, m_w_mem_kv, m_mem_q_norm, m_mem_k_norm, m_out_norm_attn, m_out_norm_conv, m_out_norm_mem, m_w_out, m_norm_ffn, m_w_gate, m_w_up, m_w_down, v_norm_mix, v_w_in, v_q_norm, v_k_norm, v_attn_sinks, v_conv_w, v_conv_b, v_norm_mem, v_w_mem_kv, v_mem_q_norm, v_mem_k_norm, v_out_norm_attn, v_out_norm_conv, v_out_norm_mem, v_w_out, v_norm_ffn, v_w_gate, v_w_up, v_w_down):
    BL, S, D = x.shape
    T = BL * S
    TM = 256
    TM_BIG = min(512, S)
    w_small = dict(norm_mix=norm_mix, norm_mem=norm_mem, norm_ffn=norm_ffn, out_norm_attn=out_norm_attn,
                   out_norm_conv=out_norm_conv, out_norm_mem=out_norm_mem, conv_w=conv_w, conv_b=conv_b, q_norm=q_norm,
                   k_norm=k_norm, mem_q_norm=mem_q_norm, mem_k_norm=mem_k_norm, attn_sinks=attn_sinks)
    m_small = dict(norm_mix=m_norm_mix, norm_mem=m_norm_mem, norm_ffn=m_norm_ffn, out_norm_attn=m_out_norm_attn,
                   out_norm_conv=m_out_norm_conv, out_norm_mem=m_out_norm_mem, conv_w=m_conv_w, conv_b=m_conv_b,
                   q_norm=m_q_norm, k_norm=m_k_norm, mem_q_norm=m_mem_q_norm, mem_k_norm=m_mem_k_norm,
                   attn_sinks=m_attn_sinks)
    v_small = dict(norm_mix=v_norm_mix, norm_mem=v_norm_mem, norm_ffn=v_norm_ffn, out_norm_attn=v_out_norm_attn,
                   out_norm_conv=v_out_norm_conv, out_norm_mem=v_out_norm_mem, conv_w=v_conv_w, conv_b=v_conv_b,
                   q_norm=v_q_norm, k_norm=v_k_norm, mem_q_norm=v_mem_q_norm, mem_k_norm=v_mem_k_norm,
                   attn_sinks=v_attn_sinks)
    pk = _pack_small(w_small)

    rowblocks = lambda a, b, c, d, e, f: [a[0].T, b[0].T, c[0].T, d[0], e[0], f[0]]
    w_rb = rowblocks(w_in, w_gate, w_up, w_down, w_out, w_mem_kv)
    m_rb = rowblocks(m_w_in, m_w_gate, m_w_up, m_w_down, m_w_out, m_w_mem_kv)
    v_rb = rowblocks(v_w_in, v_w_gate, v_w_up, v_w_down, v_w_out, v_w_mem_kv)
    (winT_s,) = prep_weights("prep_w_in", w_rb[:1])
    cw_pad = jnp.zeros((8, 128), f32).at[:3, :HD].set(conv_w[0])
    (wgT_s, wuT_s, wd_s, wout_s, wmkv_s), (winT, cw_all) = prep_weights(
        "gather_w_in", w_rb[1:], _together([gather_two_legs([winT_s]), gather_exchange([cw_pad], [False])]))
    conv_w_full = jnp.transpose(cw_all.reshape(4, 8, 128)[:, :3, :HD], (1, 0, 2)).reshape(3, CONV_W)
    conv_w8 = jnp.zeros((8, CONV_W), f32).at[:3].set(conv_w_full)
    sink_rows = jnp.broadcast_to(attn_sinks.reshape(N_Q, 1), (N_Q, 128))

    x2d = x.reshape(T, D)
    mem2d = mem.reshape(-1, D)
    (xn, proj, qkv), (wgT, *near1) = in_proj_fwd(
        x2d, pk, winT, TM_BIG,
        _together([gather_two_legs([wgT_s], steps=(4, 3, 1)), gather_near_exchange([wout_s, wmkv_s], relay_early=1)]))
    (attn_out,), (wout, wmkv, *near2) = attn_fwd(
        qkv, sink_rows, BL, S, _together([gather_far_exchange(near1, relay_early=2), gather_near_exchange([wuT_s, wd_s], relay_early=2)]))
    memn, kv, km, vm = mem_kv_fwd(mem2d, pk, wmkv)
    (conv_out, mem_out, merged, x1, h), (wuT, wd) = mixer_tail_fwd(
        x2d, attn_out, proj, qkv, km, vm, conv_w8, pk, wout, S, TM_BIG, gather_far_exchange(near2, relay_early=2))

    dx1, dx2b, act, d_gate, d_up, loss8, d_norm_ffn = ffn_fwd_bwd(h, x1, loss_target.reshape(T, D), wgT, wuT, wd, pk, TM)
    F = wd.shape[0]
    g_wd = matmul_tn(act, dx2b, "dw_down", F // 2, min(T, 1024))
    g_wgT = matmul_tn(d_gate, h, "dw_gate", F // 2, min(T, 1024))
    g_wuT = matmul_tn(d_up, h, "dw_up", F // 2, min(T, 1024))

    d_attn, d_conv_out, d_mem_out, g_wout, d_gains = out_proj_bwd(dx1, merged, attn_out, conv_out, mem_out, pk, wout, TM_BIG)
    dqmn, dkm, dvm, dcb, dcv, d_cw8, d_cbias = mem_conv_bwd(
        d_mem_out, mem_out, d_conv_out, proj, qkv, km, vm, conv_w8, pk, S, min(1024, S), None)
    (dqn, dkn, dv, d_sink8), (st_wout, st_wgT, st_wuT, st_wd) = attn_bwd(
        qkv, d_attn, attn_out, sink_rows, BL, S,
        reduce_scatter_exchange([_halves_view(g) for g in (g_wout, g_wgT, g_wuT, g_wd)], BL * (S // BLK + 1),
                                load_step=[0, 1, 4, 7], send_step=[1, 4, 7, 10], relay_step=[6, 16, 25, 33]))
    (g_x, g_winT, d_norm_mix, d_qg, d_kg, d_mqg), late_res = in_proj_bwd(
        dqn, dkn, dv, dcb, dcv, dqmn, proj, conv_w8, xn, x2d, dx1, pk, winT, S, TM,
        [st_wgT, st_wuT, st_wd, st_wout], w_rb[1:5], m_rb[1:5], v_rb[1:5])
    g_wmkv, d_norm_mem, d_mkg = mem_kv_bwd(dkm, dvm, kv, memn, mem2d, pk, wmkv)

    tot, tail_stage = tail_reduce(d_norm_mix, d_norm_mem, d_norm_ffn, d_gains, d_cw8, d_cbias, d_qg, d_kg, d_mqg, d_mkg,
                                  d_sink8, loss8, [_halves_view(g) for g in (g_winT, g_wmkv)])
    loss = tot[5, 384]
    tail_res, _ = adamw_big("adamw_tail", tail_stage, [w_rb[0], w_rb[5]], [m_rb[0], m_rb[5]], [v_rb[0], v_rb[5]], 4)
    res = {"w_in": [a.T[None] for a in tail_res[0]], "w_gate": [a.T[None] for a in late_res[0]],
           "w_up": [a.T[None] for a in late_res[1]], "w_down": [a[None] for a in late_res[2]],
           "w_out": [a[None] for a in late_res[3]], "w_mem_kv": [a[None] for a in tail_res[1]]}
    res.update(adamw_small(tot, pk, _pack_small(m_small), _pack_small(v_small), {k: w_small[k].shape for k in SMALL}))

    order = ["norm_mix", "w_in", "q_norm", "k_norm", "attn_sinks", "conv_w", "conv_b", "norm_mem", "w_mem_kv",
             "mem_q_norm", "mem_k_norm", "out_norm_attn", "out_norm_conv", "out_norm_mem", "w_out", "norm_ffn",
             "w_gate", "w_up", "w_down"]
    return (loss, g_x.reshape(BL, S, D), *[res[n][0] for n in order], *[res[n][1] for n in order],
            *[res[n][2] for n in order], *[res[n][3] for n in order])
```

```python
import collections
import functools

import jax
import jax.numpy as jnp
import numpy as np
from jax import lax
from jax.experimental import pallas as pl
from jax.experimental.pallas import tpu as pltpu

f32 = jnp.float32
MXU = jnp.bfloat16
WIRE = jnp.bfloat16
EPS = 1e-6
NEG = -1e30
HD = 64
BLK = 128
N_Q, N_KV, N_MEMH = 8, 2, 4
GQA = N_Q // N_KV
ATT_W, KV_W, CONV_W, MEM_W = 512, 128, 256, 256
VMEM_MIB = 1024 * 1024
ADAM_LR, ADAM_B1, ADAM_B2, ADAM_EPS, ADAM_WD, ADAM_STEP = 0.001, 0.9, 0.999, 1e-08, 0.01, 10

MESH = pl.DeviceIdType.MESH
VM = pl.BlockSpec(memory_space=pltpu.VMEM)
ANY = pl.BlockSpec(memory_space=pl.ANY)
SDS = jax.ShapeDtypeStruct
DMA = pltpu.SemaphoreType.DMA


def _c(v):
    return v.astype(MXU)


def _nn(a, b):
    return lax.dot_general(a, b, (((1,), (0,)), ((), ())), preferred_element_type=f32)


def _nt(a, b):
    return lax.dot_general(a, b, (((1,), (1,)), ((), ())), preferred_element_type=f32)


def _tn(a, b):
    return lax.dot_general(a, b, (((0,), (0,)), ((), ())), preferred_element_type=f32)


def _rstd(v):
    return lax.rsqrt(jnp.mean(v * v, axis=-1, keepdims=True) + EPS)


def _norm_bwd(dy, v, r, g):
    dyg = dy * g
    dv = r * dyg - v * (r * r * r) * jnp.mean(dyg * v, axis=-1, keepdims=True)
    return dv, jnp.sum(dy * v * r, axis=0, keepdims=True)


def _split3(v):
    hi = _c(v)
    r1 = v - hi.astype(f32)
    mid = _c(r1)
    return hi, mid, _c(r1 - mid.astype(f32))


def _rowsum_mxu(v, width):
    ones = jnp.ones((v.shape[1], width), MXU)
    return sum(_nn(a, ones) for a in _split3(v))


def _seg_sums(v):
    r = lax.broadcasted_iota(jnp.int32, (2 * HD, 2 * HD), 0) // HD
    c = lax.broadcasted_iota(jnp.int32, (2 * HD, 2 * HD), 1) // HD
    bd = (r == c).astype(MXU)
    outs = []
    for b in range(v.shape[1] // (2 * HD)):
        outs.append(sum(_nn(a, bd) for a in _split3(v[:, b * 2 * HD:(b + 1) * 2 * HD])))
    return outs[0] if len(outs) == 1 else jnp.concatenate(outs, axis=1)


def _lanes(g, width):
    return jnp.concatenate([g] * (width // HD), axis=1)


def _heads_rstd(v):
    return lax.rsqrt(_seg_sums(v * v) * (1.0 / HD) + EPS)


def _heads_norm_bwd(dy, v, g):
    r = _heads_rstd(v)
    gl = _lanes(g, v.shape[1])
    dyg = dy * gl
    dv = r * dyg - v * (r * r * r) * (_seg_sums(dyg * v) * (1.0 / HD))
    dgl = jnp.sum(dy * v * r, axis=0, keepdims=True)
    return dv, sum(dgl[:, s * HD:(s + 1) * HD] for s in range(v.shape[1] // HD))


def _exp_scores(s, extra=None):
    m = jnp.max(s, axis=-1, keepdims=True)
    if extra is None:
        return jnp.exp(s - m), None
    m = jnp.maximum(m, extra)
    return jnp.exp(s - m), jnp.exp(extra - m)


def _place():
    return lax.axis_index("x"), lax.axis_index("y"), lax.axis_index("c")


SMALL_AT = {"norm_mix": (0, 0, 1024), "norm_mem": (1, 0, 1024), "norm_ffn": (2, 0, 1024),
            "out_norm_attn": (3, 0, ATT_W), "out_norm_conv": (3, ATT_W, CONV_W), "out_norm_mem": (3, ATT_W + CONV_W, MEM_W),
            "conv_b": (4, 3 * CONV_W, CONV_W), "q_norm": (5, 0, HD), "k_norm": (5, HD, HD), "mem_q_norm": (5, 2 * HD, HD),
            "mem_k_norm": (5, 3 * HD, HD), "attn_sinks": (5, 256, N_Q)}
SMALL = ("norm_mix", "norm_mem", "norm_ffn", "out_norm_attn", "out_norm_conv", "out_norm_mem", "conv_w", "conv_b",
         "q_norm", "k_norm", "mem_q_norm", "mem_k_norm", "attn_sinks")


def _small(pk_ref, name):
    r, c0, w = SMALL_AT[name]
    return pk_ref[r:r + 1, c0:c0 + w]


def _pack_small(d):
    z = lambda n: jnp.zeros((1, n), f32)
    row3 = jnp.concatenate([d["out_norm_attn"], d["out_norm_conv"], d["out_norm_mem"]], axis=1)
    row4 = jnp.concatenate([d["conv_w"].reshape(1, 3 * HD), z(3 * CONV_W - 3 * HD), d["conv_b"]], axis=1)
    row5 = jnp.concatenate([d["q_norm"], d["k_norm"], d["mem_q_norm"], d["mem_k_norm"], d["attn_sinks"],
                            z(1024 - 4 * HD - N_Q)], axis=1)
    return jnp.concatenate([d["norm_mix"], d["norm_mem"], d["norm_ffn"], row3, row4, row5, z(1024), z(1024)], axis=0)


def _other_chips(x, y):
    return [(1 - x, y), (x, 1 - y), (1 - x, 1 - y)]


Exchange = collections.namedtuple("Exchange", "ins outs sems start finish relays aliases", defaults=((), {}))


def _together(exchanges):
    def bounds(key):
        at, out = 0, []
        for ex in exchanges:
            out.append((at, at + len(getattr(ex, key))))
            at += len(getattr(ex, key))
        return out

    bi, bo, bs = bounds("ins"), bounds("outs"), bounds("sems")

    def of(i, fn):
        return lambda xa, xo, xs: fn(xa[bi[i][0]:bi[i][1]], xo[bo[i][0]:bo[i][1]], xs[bs[i][0]:bs[i][1]])

    def every(name):
        fns = [of(i, getattr(ex, name)) for i, ex in enumerate(exchanges)]

        def run(xa, xo, xs):
            for fn in fns:
                fn(xa, xo, xs)
        return run

    aliases = {}
    for i, ex in enumerate(exchanges):
        aliases.update({bi[i][0] + a: bo[i][0] + o for a, o in ex.aliases.items()})
    return Exchange([a for ex in exchanges for a in ex.ins], [o for ex in exchanges for o in ex.outs],
                    [s for ex in exchanges for s in ex.sems], every("start"), every("finish"),
                    [(sbe, of(i, fn)) for i, ex in enumerate(exchanges) for sbe, fn in ex.relays], aliases)


def _run(name, body, grid, ins, in_specs, out_shape, out_specs, scratch=(), vmem_mib=32, exchange=None):
    ins, in_specs, out_shape, out_specs, scratch = list(ins), list(in_specs), list(out_shape), list(out_specs), list(scratch)
    ni, no, ns = len(ins), len(out_shape), len(scratch)
    ex = exchange
    if ex is not None:
        nxi, nxo = len(ex.ins), len(ex.outs)

    def call_body(*refs):
        if ex is None:
            body(*refs)
            return
        a, xa = refs[:ni], refs[ni:ni + nxi]
        o, xo = refs[ni + nxi:ni + nxi + no], refs[ni + nxi + no:ni + nxi + no + nxo]
        s, xs = refs[ni + nxi + no + nxo:ni + nxi + no + nxo + ns], refs[ni + nxi + no + nxo + ns:]
        if grid:
            first = functools.reduce(jnp.logical_and, [pl.program_id(d) == 0 for d in range(len(grid))])
            last = functools.reduce(jnp.logical_and, [pl.program_id(d) == grid[d] - 1 for d in range(len(grid))])
            pl.when(first)(lambda: ex.start(xa, xo, xs))
            body(*a, *o, *s)
            nsteps = functools.reduce(lambda p, q: p * q, grid)
            for before_end, fn in ex.relays:
                at = np.unravel_index(max(nsteps - 1 - before_end, 0), grid)
                here = functools.reduce(jnp.logical_and, [pl.program_id(d) == int(at[d]) for d in range(len(grid))])
                pl.when(here)(functools.partial(fn, xa, xo, xs))
            pl.when(last)(lambda: ex.finish(xa, xo, xs))
        else:
            ex.start(xa, xo, xs)
            if body is not None:
                body(*a, *o, *s)
            for _, fn in ex.relays:
                fn(xa, xo, xs)
            ex.finish(xa, xo, xs)

    kw = dict(grid=grid) if grid else {}
    if ex is not None:
        if ex.aliases:
            kw["input_output_aliases"] = {ni + i: no + o for i, o in ex.aliases.items()}
        ins, in_specs = ins + list(ex.ins), in_specs + [ANY] * nxi
        out_shape, out_specs = out_shape + list(ex.outs), out_specs + [ANY] * nxo
        scratch = scratch + list(ex.sems)
    res = pl.pallas_call(
        call_body, name=name, out_shape=out_shape, in_specs=in_specs, out_specs=out_specs, scratch_shapes=scratch,
        compiler_params=pltpu.CompilerParams(dimension_semantics=("arbitrary",) * len(grid) if grid else None,
                                             vmem_limit_bytes=vmem_mib * VMEM_MIB), **kw)(*ins)
    res = list(res)
    return (res[:no], res[no:]) if ex is not None else res


def _remote(src, dst, ssem, rsem, dev):
    return pltpu.make_async_remote_copy(src_ref=src, dst_ref=dst, send_sem=ssem, recv_sem=rsem,
                                        device_id=dev, device_id_type=MESH)


def gather_exchange(shards, split, relay_early=0):
    n = len(shards)

    def rows(ref, e, kk, half=None):
        R = shards[e].shape[0]
        if half is None:
            return ref.at[pl.ds(pl.multiple_of(kk * R, 8), R)]
        return ref.at[pl.ds(pl.multiple_of(kk * R + half * (R // 2), 8), R // 2)]

    def ici(src, dst, sm, e, j, chip_j, x, y, c):
        k = 2 * x + y
        if split[e]:
            s = src[e].at[pl.ds(pl.multiple_of(c * (shards[e].shape[0] // 2), 8), shards[e].shape[0] // 2)]
            return _remote(s, rows(dst[e], e, k, c), sm[0].at[6 * e + j], sm[1].at[6 * e + j], (*chip_j, c))
        return _remote(src[e], rows(dst[e], e, k), sm[0].at[6 * e + j], sm[1].at[6 * e + j], (*chip_j, c))

    def landed(dst, e, chip_j, c):
        kj = 2 * chip_j[0] + chip_j[1]
        return rows(dst[e], e, kj, c) if split[e] else rows(dst[e], e, kj)

    def forward(dst, sm, e, j, chip_j, x, y, c, sender_c):
        kj = 2 * chip_j[0] + chip_j[1]
        r = rows(dst[e], e, kj, sender_c)
        return _remote(r, r, sm[0].at[6 * e + 3 + j], sm[1].at[6 * e + 3 + j], (x, y, 1 - c))

    def local(src, dst, sm, e, x, y):
        return pltpu.make_async_copy(src[e], rows(dst[e], e, 2 * x + y), sm[2].at[e])

    def start(src, dst, sm):
        x, y, c = _place()
        for e in range(n):
            local(src, dst, sm, e, x, y).start()
            for j, chip_j in enumerate(_other_chips(x, y)):
                ici(src, dst, sm, e, j, chip_j, x, y, c).start()

    def relay(src, dst, sm):
        x, y, c = _place()
        for e in range(n):
            for j, chip_j in enumerate(_other_chips(x, y)):
                r = landed(dst, e, chip_j, c)
                _remote(r, r, sm[0].at[6 * e + j], sm[1].at[6 * e + j], (*chip_j, c)).wait_recv()
                if split[e]:
                    forward(dst, sm, e, j, chip_j, x, y, c, c).start()

    def finish(src, dst, sm):
        x, y, c = _place()
        chips = _other_chips(x, y)
        for e in range(n):
            for j, chip_j in enumerate(chips):
                if split[e]:
                    forward(dst, sm, e, j, chip_j, x, y, c, 1 - c).wait_recv()
        for e in range(n):
            for j, chip_j in enumerate(chips):
                ici(src, dst, sm, e, j, chip_j, x, y, c).wait_send()
                if split[e]:
                    forward(dst, sm, e, j, chip_j, x, y, c, c).wait_send()
            local(src, dst, sm, e, x, y).wait()

    outs = [SDS((4 * s.shape[0], s.shape[1]), s.dtype) for s in shards]
    return Exchange(list(shards), outs, [DMA((6 * n,)), DMA((6 * n,)), DMA((n,))], start, finish, [(relay_early, relay)])


def _block_rows(ref, R, kk, half, quarter=None):
    hr = R // 2
    if quarter is None:
        return ref.at[pl.ds(pl.multiple_of(kk * R + half * hr, 8), hr)]
    return ref.at[pl.ds(pl.multiple_of(kk * R + half * hr + quarter * (hr // 2), 8), hr // 2)]


def gather_near_exchange(shards, relay_early=0):
    n = len(shards)
    R = [s.shape[0] for s in shards]

    def ici(src, dst, sm, e, j, chip_j, x, y, c):
        half = src[e].at[pl.ds(pl.multiple_of(c * (R[e] // 2), 8), R[e] // 2)]
        return _remote(half, _block_rows(dst[e], R[e], 2 * x + y, c), sm[0].at[4 * e + j], sm[1].at[4 * e + j], (*chip_j, c))

    def forward(dst, sm, e, j, chip_j, x, y, c, sender_c):
        r = _block_rows(dst[e], R[e], 2 * chip_j[0] + chip_j[1], sender_c)
        return _remote(r, r, sm[0].at[4 * e + 2 + j], sm[1].at[4 * e + 2 + j], (x, y, 1 - c))

    def local(src, dst, sm, e, x, y):
        return pltpu.make_async_copy(src[e], dst[e].at[pl.ds(pl.multiple_of((2 * x + y) * R[e], 8), R[e])], sm[2].at[e])

    def start(src, dst, sm):
        x, y, c = _place()
        for e in range(n):
            local(src, dst, sm, e, x, y).start()
            for j, chip_j in enumerate(_other_chips(x, y)[:2]):
                ici(src, dst, sm, e, j, chip_j, x, y, c).start()

    def relay(src, dst, sm):
        x, y, c = _place()
        for e in range(n):
            for j, chip_j in enumerate(_other_chips(x, y)[:2]):
                r = _block_rows(dst[e], R[e], 2 * chip_j[0] + chip_j[1], c)
                _remote(r, r, sm[0].at[4 * e + j], sm[1].at[4 * e + j], (*chip_j, c)).wait_recv()
                forward(dst, sm, e, j, chip_j, x, y, c, c).start()

    def finish(src, dst, sm):
        x, y, c = _place()
        near = _other_chips(x, y)[:2]
        for e in range(n):
            for j, chip_j in enumerate(near):
                forward(dst, sm, e, j, chip_j, x, y, c, 1 - c).wait_recv()
        for e in range(n):
            for j, chip_j in enumerate(near):
                ici(src, dst, sm, e, j, chip_j, x, y, c).wait_send()
                forward(dst, sm, e, j, chip_j, x, y, c, c).wait_send()
            local(src, dst, sm, e, x, y).wait()

    outs = [SDS((4 * s.shape[0], s.shape[1]), s.dtype) for s in shards]
    return Exchange(list(shards), outs, [DMA((4 * n,)), DMA((4 * n,)), DMA((n,))], start, finish, [(relay_early, relay)])


def gather_far_exchange(bufs, relay_early=0):
    n = len(bufs)
    R = [b.shape[0] // 4 for b in bufs]

    def send(src, dst, sm, e, j, x, y, c):
        to, of = _other_chips(x, y)[j], _other_chips(x, y)[1 - j]
        kk = 2 * of[0] + of[1]
        return _remote(_block_rows(src[e], R[e], kk, c, j), _block_rows(dst[e], R[e], kk, c, j),
                       sm[0].at[4 * e + j], sm[1].at[4 * e + j], (*to, c))

    def landed(dst, e, j, x, y, half):
        return _block_rows(dst[e], R[e], 2 * (1 - x) + (1 - y), half, j)

    def forward(dst, sm, e, j, x, y, c, sender_c):
        r = landed(dst, e, j, x, y, sender_c)
        return _remote(r, r, sm[0].at[4 * e + 2 + j], sm[1].at[4 * e + 2 + j], (x, y, 1 - c))

    def start(src, dst, sm):
        x, y, c = _place()
        for e in range(n):
            for j in range(2):
                send(src, dst, sm, e, j, x, y, c).start()

    def relay(src, dst, sm):
        x, y, c = _place()
        for e in range(n):
            for j in range(2):
                r = landed(dst, e, j, x, y, c)
                _remote(r, r, sm[0].at[4 * e + j], sm[1].at[4 * e + j], (*_other_chips(x, y)[j], c)).wait_recv()
                forward(dst, sm, e, j, x, y, c, c).start()

    def finish(src, dst, sm):
        x, y, c = _place()
        for e in range(n):
            for j in range(2):
                forward(dst, sm, e, j, x, y, c, 1 - c).wait_recv()
        for e in range(n):
            for j in range(2):
                send(src, dst, sm, e, j, x, y, c).wait_send()
                forward(dst, sm, e, j, x, y, c, c).wait_send()

    outs = [SDS(b.shape, b.dtype) for b in bufs]
    return Exchange(list(bufs), outs, [DMA((4 * n,)), DMA((4 * n,))], start, finish, [(relay_early, relay)],
                    {i: i for i in range(n)})


def gather_two_legs(shards):
    near = gather_near_exchange(shards)
    far = gather_far_exchange(near.outs)

    def finish(src, dst, sm):
        near.relays[0][1](src, dst, sm[:3])
        near.finish(src, dst, sm[:3])
        far.start(dst, dst, sm[3:])
        far.relays[0][1](dst, dst, sm[3:])
        far.finish(dst, dst, sm[3:])

    return Exchange(near.ins, near.outs, list(near.sems) + list(far.sems),
                    lambda src, dst, sm: near.start(src, dst, sm[:3]), finish)


def scatter_exchange(parts, relay_before_end=None, want_issue=False):
    n = len(parts)
    by_entry = relay_before_end is not None
    relay_before_end = relay_before_end or [0] * n

    def ici(p, st, sm, e, j, chip_j, x, y, c):
        k, kj = 2 * x + y, 2 * chip_j[0] + chip_j[1]
        return _remote(p[e].at[kj], st[e].at[c, k], sm[0].at[8 * e + j], sm[1].at[8 * e + j], (*chip_j, c))

    def own(p, st, sm, e, x, y, c):
        k = 2 * x + y
        return _remote(p[e].at[k], st[e].at[c, k], sm[0].at[8 * e + 3], sm[1].at[8 * e + 3], (x, y, 1 - c))

    def forward(st, sm, e, j, chip_j, x, y, c, sender_c):
        kj = 2 * chip_j[0] + chip_j[1]
        r = st[e].at[sender_c, kj]
        return _remote(r, r, sm[0].at[8 * e + 4 + j], sm[1].at[8 * e + 4 + j], (x, y, 1 - c))

    def local(p, st, sm, e, x, y, c):
        k = 2 * x + y
        return pltpu.make_async_copy(p[e].at[k], st[e].at[c, k], sm[2].at[e])

    def issue(e, p, st, sm):
        x, y, c = _place()
        for j, chip_j in enumerate(_other_chips(x, y)):
            ici(p, st, sm, e, j, chip_j, x, y, c).start()
        local(p, st, sm, e, x, y, c).start()
        own(p, st, sm, e, x, y, c).start()

    def start(p, st, sm, before_slot=None):
        x, y, c = _place()
        if by_entry:
            for e in range(n):
                issue(e, p, st, sm)
            return
        for j, chip_j in enumerate(_other_chips(x, y)):
            if before_slot is not None:
                before_slot(j, 2 * chip_j[0] + chip_j[1])
            for e in range(n):
                ici(p, st, sm, e, j, chip_j, x, y, c).start()
        if before_slot is not None:
            before_slot(3, 2 * x + y)
        for e in range(n):
            local(p, st, sm, e, x, y, c).start()
            own(p, st, sm, e, x, y, c).start()

    def relay(e, p, st, sm):
        x, y, c = _place()
        for j, chip_j in enumerate(_other_chips(x, y)):
            kj = 2 * chip_j[0] + chip_j[1]
            r = st[e].at[c, kj]
            _remote(r, r, sm[0].at[8 * e + j], sm[1].at[8 * e + j], (*chip_j, c)).wait_recv()
            forward(st, sm, e, j, chip_j, x, y, c, c).start()

    def finish(p, st, sm):
        x, y, c = _place()
        k = 2 * x + y
        chips = _other_chips(x, y)
        for e in range(n):
            r = st[e].at[1 - c, k]
            _remote(r, r, sm[0].at[8 * e + 3], sm[1].at[8 * e + 3], (x, y, 1 - c)).wait_recv()
            for j, chip_j in enumerate(chips):
                forward(st, sm, e, j, chip_j, x, y, c, 1 - c).wait_recv()
        for e in range(n):
            own(p, st, sm, e, x, y, c).wait_send()
            for j, chip_j in enumerate(chips):
                ici(p, st, sm, e, j, chip_j, x, y, c).wait_send()
                forward(st, sm, e, j, chip_j, x, y, c, c).wait_send()
            local(p, st, sm, e, x, y, c).wait()

    outs = [SDS((2,) + a.shape, a.dtype) for a in parts]
    ex = Exchange(list(parts), outs, [DMA((8 * n,)), DMA((8 * n,)), DMA((n,))], start, finish,
                  [(relay_before_end[e], functools.partial(relay, e)) for e in range(n)])
    return (ex, issue) if want_issue else ex


def reduce_scatter_exchange(grads, nsteps, load_step, send_step, relay_step):
    n = len(grads)
    hrs = [g.shape[2] for g in grads]
    C = grads[0].shape[3]
    scatter, issue = scatter_exchange([SDS((4,) + g.shape[2:], WIRE) for g in grads], want_issue=True)
    hand_on = [fn for _, fn in scatter.relays]

    def refs(xs):
        return xs[:3], xs[3], xs[4], xs[5], xs[6], xs[7:7 + n], xs[7 + n:]

    def push(e, g, psem, qsem, sib_st):
        x, y, c = _place()
        return _remote(g[e].at[:, 1 - c], sib_st[e], psem.at[e], qsem.at[e], (x, y, 1 - c))

    def fetch(e, g, lsem, own_st):
        _, _, c = _place()
        return pltpu.make_async_copy(g[e].at[:, c], own_st.at[e % 2, :, pl.ds(0, hrs[e])], lsem.at[e])

    def start(g, xo, xs):
        _, _, psem, qsem, _, sib_st, _ = refs(xs)
        for e in range(n):
            push(e, g, psem, qsem, sib_st).start()

    def load(e, g, xo, xs):
        _, lsem, _, _, own_st, _, _ = refs(xs)
        fetch(e, g, lsem, own_st).start()

    def send(e, g, xo, xs):
        sm, lsem, psem, qsem, own_st, sib_st, part = refs(xs)
        fetch(e, g, lsem, own_st).wait()
        push(e, g, psem, qsem, sib_st).wait_recv()
        part[e][...] = (own_st[e % 2, :, 0:hrs[e]] + sib_st[e][...]).astype(WIRE)
        issue(e, part, xo, sm)

    def relay(e, g, xo, xs):
        sm, _, _, _, _, _, part = refs(xs)
        hand_on[e](part, xo, sm)

    def finish(g, xo, xs):
        sm, _, psem, qsem, _, sib_st, part = refs(xs)
        scatter.finish(part, xo, sm)
        for e in range(n):
            push(e, g, psem, qsem, sib_st).wait_send()

    plan = sorted([(min(step[e], nsteps - 1), phase, e) for phase, step in enumerate((load_step, send_step, relay_step))
                   for e in range(n)])
    stage = (load, send, relay)
    relays = [(nsteps - 1 - at, functools.partial(stage[phase], e)) for at, phase, e in plan]
    scratch = (list(scatter.sems) + [DMA((n,)), DMA((n,)), DMA((n,))] + [pltpu.VMEM((2, 4, max(hrs), C), f32)]
               + [pltpu.VMEM((4, hr, C), f32) for hr in hrs] + [pltpu.VMEM((4, hr, C), WIRE) for hr in hrs])
    return Exchange(list(grads), scatter.outs, scratch, start, finish, relays)


def tail_reduce(d_norm_mix, d_norm_mem, d_norm_ffn, d_gains, d_cw8, d_cbias, d_qg, d_kg, d_mqg, d_mkg, d_sink8, loss8, tail):
    n = len(tail)
    scatter = scatter_exchange([SDS((4,) + a.shape[2:], WIRE) for a in tail])

    def half_copy(g, sib, hsem, e, j, slot, x, y, c):
        return _remote(g[e].at[slot, 1 - c], sib[e].at[slot], hsem[0].at[4 * e + j], hsem[1].at[4 * e + j], (x, y, 1 - c))

    def body(nm_ref, nmem_ref, nf_ref, gn_ref, cw_ref, cb_ref, qg_ref, kg_ref, mqg_ref, mkg_ref, sk_ref, ls_ref, *rest):
        g, o_ref, st = rest[:n], rest[n], rest[n + 1:2 * n + 1]
        buf, ssem, rsem = rest[2 * n + 1:2 * n + 4]
        own, sib, part = (rest[2 * n + 4 + i * n:2 * n + 4 + (i + 1) * n] for i in range(3))
        lsem = rest[5 * n + 4]
        hsem, xsem = rest[5 * n + 5:5 * n + 7], rest[5 * n + 7:]
        x, y, c = _place()
        loads = [pltpu.make_async_copy(g[e].at[:, c], own[e], lsem.at[e]) for e in range(n)]
        for ld in loads:
            ld.start()
        for j, slot in enumerate([2 * cx + cy for cx, cy in _other_chips(x, y)] + [2 * x + y]):
            for e in range(n):
                half_copy(g, sib, hsem, e, j, slot, x, y, c).start()
        me = 4 * x + 2 * y + c
        mine = buf.at[me]
        mine[...] = jnp.zeros((8, 1024), f32)
        mine[0:1, :] = nm_ref[...]
        mine[1:2, :] = nmem_ref[...]
        mine[2:3, :] = nf_ref[...]
        mine[3:4, :] = gn_ref[...]
        for j in range(3):
            mine[4:5, pl.ds(j * CONV_W, CONV_W)] = cw_ref[j:j + 1, :]
        mine[4:5, pl.ds(3 * CONV_W, CONV_W)] = cb_ref[...]
        for j, r in enumerate((qg_ref, kg_ref, mqg_ref, mkg_ref)):
            mine[5:6, pl.ds(j * HD, HD)] = r[...]
        mine[5:6, pl.ds(256, 128)] = sk_ref[0:1, :]
        mine[5:6, pl.ds(384, 128)] = ls_ref[0:1, :]

        def peer_of(m):
            return (1 - x if m & 4 else x, 1 - y if m & 2 else y, 1 - c if m & 1 else c)

        for m in range(1, 8):
            _remote(mine, mine, ssem.at[m - 1], rsem.at[m - 1], peer_of(m)).start()
        for ld in loads:
            ld.wait()

        def chip_partial(j, slot):
            for e in range(n):
                half_copy(g, sib, hsem, e, j, slot, x, y, c).wait()
                part[e][slot] = (own[e][slot] + sib[e][slot]).astype(WIRE)

        scatter.start(part, st, xsem, chip_partial)
        for _, hand_on in scatter.relays:
            hand_on(part, st, xsem)
        scatter.finish(part, st, xsem)
        for m in range(1, 8):
            p = peer_of(m)
            got = buf.at[4 * p[0] + 2 * p[1] + p[2]]
            _remote(got, got, ssem.at[m - 1], rsem.at[m - 1], p).wait_recv()
        for m in range(1, 8):
            _remote(mine, mine, ssem.at[m - 1], rsem.at[m - 1], peer_of(m)).wait_send()
        acc = buf[0]
        for d in range(1, 8):
            acc = acc + buf[d]
        o_ref[...] = acc

    ins = [d_norm_mix, d_norm_mem, d_norm_ffn, d_gains, d_cw8, d_cbias, d_qg, d_kg, d_mqg, d_mkg, d_sink8, loss8]
    half_shape = [(4,) + a.shape[2:] for a in tail]
    scratch = ([pltpu.VMEM((8, 8, 1024), f32), DMA((7,)), DMA((7,))]
               + [pltpu.VMEM(s, f32) for s in half_shape] * 2 + [pltpu.VMEM(s, WIRE) for s in half_shape]
               + [DMA((n,)), DMA((4 * n,)), DMA((4 * n,))] + list(scatter.sems))
    res = _run("tail_reduce", body, (), ins + list(tail), [VM] * len(ins) + [ANY] * n,
               [SDS((8, 1024), f32)] + list(scatter.outs), [VM] + [ANY] * n, scratch=scratch, vmem_mib=40)
    return res[0], res[1:]


def _adamw_math(w, g, m, v):
    m = ADAM_B1 * m + (1.0 - ADAM_B1) * g
    v = ADAM_B2 * v + (1.0 - ADAM_B2) * (g * g)
    m_hat = m / (1.0 - ADAM_B1 ** ADAM_STEP)
    v_hat = v / (1.0 - ADAM_B2 ** ADAM_STEP)
    delta = -ADAM_LR * (m_hat / (jnp.sqrt(v_hat) + ADAM_EPS) + ADAM_WD * w)
    return delta, m, v


def _sum_chips(st):
    return ((st[0].astype(f32) + st[1].astype(f32)) + st[2].astype(f32)) + st[3].astype(f32)


def adamw_big(name, stages, ws, ms, vs, nstep, exchange=None):
    n = len(stages)

    def body(*refs):
        st, w, m, v = refs[:n], refs[n:2 * n], refs[2 * n:3 * n], refs[3 * n:4 * n]
        outs = refs[4 * n:]
        for e in range(n):
            g = jnp.concatenate([_sum_chips(st[e].at[0]), _sum_chips(st[e].at[1])], axis=0)
            d, mm, vv = _adamw_math(w[e][...], g, m[e][...], v[e][...])
            outs[4 * e][...] = g
            outs[4 * e + 1][...] = d
            outs[4 * e + 2][...] = mm
            outs[4 * e + 3][...] = vv

    st_specs, w_specs = [], []
    for e in range(n):
        _, _, hr, C = stages[e].shape
        st_specs.append(pl.BlockSpec((2, 4, hr, C // nstep), lambda i: (0, 0, 0, i)))
        w_specs.append(pl.BlockSpec((2 * hr, C // nstep), lambda i: (0, i)))
    out_specs = [s for s in w_specs for _ in range(4)]
    out_shape = [SDS(w.shape, f32) for w in ws for _ in range(4)]
    res = _run(name, body, (nstep,), list(stages) + list(ws) + list(ms) + list(vs), st_specs + w_specs * 3,
               out_shape, out_specs, vmem_mib=16, exchange=exchange)
    res, sent = res if exchange is not None else (res, None)
    return [res[4 * e:4 * e + 4] for e in range(n)], sent


def adamw_small(tot, pk_w, pk_m, pk_v, shapes):
    def body(tot_ref, w_ref, m_ref, v_ref, *outs):
        x, y, _ = _place()
        chip = 2 * x + y
        taps = []
        for j in range(3):
            mine = tot_ref[4:5, j * CONV_W:j * CONV_W + HD]
            for s in range(1, 4):
                mine = jnp.where(chip == s, tot_ref[4:5, j * CONV_W + s * HD:j * CONV_W + (s + 1) * HD], mine)
            taps.append(mine)
        row4 = jnp.concatenate(taps + [jnp.zeros((1, 3 * CONV_W - 3 * HD), f32), tot_ref[4:5, 3 * CONV_W:]], axis=1)
        tot_v = tot_ref[...]
        row = lax.broadcasted_iota(jnp.int32, tot_v.shape, 0)
        g = jnp.where(row == 4, jnp.broadcast_to(row4, tot_v.shape), tot_v)
        d, mm, vv = _adamw_math(w_ref[...], g, m_ref[...], v_ref[...])
        for i, name in enumerate(SMALL):
            for k, val in enumerate((g, d, mm, vv)):
                if name == "conv_w":
                    outs[4 * i + k][...] = jnp.concatenate([val[4:5, j * HD:(j + 1) * HD] for j in range(3)], axis=0)[None]
                else:
                    r, c0, w = SMALL_AT[name]
                    outs[4 * i + k][...] = val[r:r + 1, c0:c0 + w]

    out_shape = [SDS(shapes[k], f32) for k in SMALL for _ in range(4)]
    res = _run("adamw_small", body, (), [tot, pk_w, pk_m, pk_v], [VM] * 4, out_shape, [VM] * len(out_shape))
    return {k: res[4 * i:4 * i + 4] for i, k in enumerate(SMALL)}


def prep_weights(name, shards, exchange=None):
    n = len(shards)

    def body(*refs):
        for e in range(n):
            refs[n + e][...] = _c(refs[e][...])

    return _run(name, body, (), shards, [VM] * n, [SDS(a.shape, MXU) for a in shards], [VM] * n, vmem_mib=16, exchange=exchange)


def mem_kv_fwd(mem2d, pk, wmkv):
    M, D = mem2d.shape

    def body(m_ref, pk_ref, w_ref, mn_ref, kv_ref, km_ref, vm_ref):
        m = m_ref[...]
        mn = _c(m * _rstd(m) * _small(pk_ref, "norm_mem"))
        mn_ref[...] = mn
        kv = _nn(mn, w_ref[...])
        kv_ref[...] = kv
        kk = kv[:, :MEM_W]
        km_ref[...] = _c(kk * _heads_rstd(kk) * _lanes(_small(pk_ref, "mem_k_norm"), MEM_W))
        vm_ref[...] = _c(kv[:, MEM_W:])

    return _run("mem_kv_fwd", body, (), [mem2d, pk, wmkv], [VM] * 3,
                [SDS((M, D), MXU), SDS((M, 2 * MEM_W), f32), SDS((M, MEM_W), MXU), SDS((M, MEM_W), MXU)], [VM] * 4)


QKV_W = ATT_W + 2 * KV_W + MEM_W


def in_proj_fwd(x2d, pk, winT, tm, exchange):
    T, D = x2d.shape
    P = winT.shape[0]

    def body(x_ref, pk_ref, w_ref, xn_ref, proj_ref, qkv_ref):
        xv = x_ref[...]
        xn = _c(xv * _rstd(xv) * _small(pk_ref, "norm_mix"))
        xn_ref[...] = xn
        proj = _nt(xn, w_ref[...])
        proj_ref[...] = proj
        q, k = proj[:, :ATT_W], proj[:, ATT_W:ATT_W + KV_W]
        qm = proj[:, P - MEM_W:]
        qkv_ref[...] = jnp.concatenate(
            [_c(q * _heads_rstd(q) * _lanes(_small(pk_ref, "q_norm"), ATT_W)),
             _c(k * _heads_rstd(k) * _lanes(_small(pk_ref, "k_norm"), KV_W)),
             _c(proj[:, ATT_W + KV_W:ATT_W + 2 * KV_W]),
             _c(qm * _heads_rstd(qm) * _lanes(_small(pk_ref, "mem_q_norm"), MEM_W))], axis=1)

    return _run("in_proj_fwd", body, (T // tm,), [x2d, pk, winT],
                [pl.BlockSpec((tm, D), lambda i: (i, 0)), VM, VM],
                [SDS((T, D), MXU), SDS((T, P), f32), SDS((T, QKV_W), MXU)],
                [pl.BlockSpec((tm, D), lambda i: (i, 0)), pl.BlockSpec((tm, P), lambda i: (i, 0)),
                 pl.BlockSpec((tm, QKV_W), lambda i: (i, 0))],
                vmem_mib=40, exchange=exchange)


def _swa_bias_table():
    r = np.arange(GQA * BLK)[:, None]
    k = np.arange(2 * BLK)[None, :]
    dist = (r % BLK) + BLK - k
    band = (dist >= 0) & (dist < BLK)
    tab = np.empty((2, N_KV, GQA * BLK, 2 * BLK), np.float32)
    for later in range(2):
        valid = band & ((k >= BLK) | (later == 1))
        for g in range(N_KV):
            slope = 2.0 ** -(g * GQA + r // BLK + 1.0)
            tab[later, g] = np.where(valid, -slope * dist, NEG)
    return jnp.asarray(tab)


def _sink_column(g, sk_ref):
    hrow = lax.broadcasted_iota(jnp.int32, (GQA * BLK, 1), 0) // BLK
    sink = jnp.zeros((GQA * BLK, 1), f32)
    for hh in range(GQA):
        sink = jnp.where(hrow == hh, sk_ref[g * GQA + hh:g * GQA + hh + 1, 0:1], sink)
    return sink


def _stack_heads(v, g):
    return jnp.concatenate([v[:, (g * GQA + hh) * HD:(g * GQA + hh + 1) * HD] for hh in range(GQA)], axis=0)


def attn_fwd(qkv, sink_rows, BL, S, exchange, qb=2):
    NS = S // (qb * BLK)
    T = BL * S

    def body(q_ref, kc_ref, kp_ref, vc_ref, vp_ref, sk_ref, tab_ref, o_ref):
        j = pl.program_id(1)
        kall = jnp.concatenate([kp_ref[...], kc_ref[...]], axis=0)
        vall = jnp.concatenate([vp_ref[...], vc_ref[...]], axis=0)
        ones = jnp.ones((2 * BLK, HD), MXU)
        for b in range(qb):
            q = q_ref[pl.ds(b * BLK, BLK), :]
            k2, v2 = kall[b * BLK:(b + 2) * BLK], vall[b * BLK:(b + 2) * BLK]
            later = jnp.minimum(j, 1) if b == 0 else 1
            for g in range(N_KV):
                kn, vh = k2[:, g * HD:(g + 1) * HD], v2[:, g * HD:(g + 1) * HD]
                s = _nt(_stack_heads(q, g), kn) * (HD ** -0.5) + tab_ref[later, g]
                e, es = _exp_scores(s, _sink_column(g, sk_ref))
                eb = _c(e)
                o = _nn(eb, vh) * (1.0 / (_nn(eb, ones) + es))
                for hh in range(GQA):
                    o_ref[pl.ds(b * BLK, BLK), pl.ds((g * GQA + hh) * HD, HD)] = o[hh * BLK:(hh + 1) * BLK]

    cur = lambda col: (lambda b, j: (b * NS + j, col))
    prev = lambda col: (lambda b, j: (qb * (b * NS + j) - jnp.minimum(j, 1), col))
    return _run("attn_fwd", body, (BL, NS), [qkv, qkv, qkv, qkv, qkv, sink_rows, _swa_bias_table()],
                [pl.BlockSpec((qb * BLK, ATT_W), cur(0)),
                 pl.BlockSpec((qb * BLK, KV_W), cur(4)), pl.BlockSpec((BLK, KV_W), prev(4)),
                 pl.BlockSpec((qb * BLK, KV_W), cur(5)), pl.BlockSpec((BLK, KV_W), prev(5)),
                 pl.BlockSpec((8, 128), lambda b, j: (0, 0)), VM],
                [SDS((T, ATT_W), f32)], [pl.BlockSpec((qb * BLK, ATT_W), cur(0))], exchange=exchange)


def _conv_taps(u, uh):
    row = lax.broadcasted_iota(jnp.int32, u.shape, 0)
    u1 = jnp.where(row == 0, uh[7:8, :], pltpu.roll(u, 1, 0))
    u2 = jnp.where(row == 0, uh[6:7, :], jnp.where(row == 1, uh[7:8, :], pltpu.roll(u, 2, 0)))
    return u1, u2


def _mem_head(qm, km, vm, h):
    qh, kh, vh = (a[:, h * HD:(h + 1) * HD] for a in (qm, km, vm))
    e, _ = _exp_scores(_nt(qh, kh) * (HD ** -0.5))
    return qh, kh, vh, e


def mixer_tail_fwd(x2d, attn_out, proj, qkv, km, vm, conv_w8, pk, wout, S, tm, exchange):
    T, D = x2d.shape
    NM = km.shape[0] // (T // S)

    def body(x_ref, ao_ref, ch_ref, cb_ref, cc_ref, chh_ref, cch_ref, qm_ref, km_ref, vm_ref, cw_ref, pk_ref,
             wout_ref, co_ref, mo_ref, mg_ref, x1_ref, h_ref):
        first = (pl.program_id(0) * tm) % S == 0
        u = cc_ref[...] * ch_ref[...]
        uh = jnp.where(first, 0.0, cch_ref[...] * chh_ref[...])
        u1, u2 = _conv_taps(u, uh)
        conv = cw_ref[0:1, :] * u2 + cw_ref[1:2, :] * u1 + cw_ref[2:3, :] * u + _small(pk_ref, "conv_b")
        conv_out = cb_ref[...] * conv
        co_ref[...] = conv_out
        qm, kmv, vmv = qm_ref[...], km_ref[...], vm_ref[...]
        ones = jnp.ones((NM, HD), MXU)
        for h in range(N_MEMH):
            _, _, vh, e = _mem_head(qm, kmv, vmv, h)
            eb = _c(e)
            mo_ref[:, pl.ds(h * HD, HD)] = _nn(eb, vh) * (1.0 / _nn(eb, ones))
        mem_out = mo_ref[...]
        ao = ao_ref[...]
        merged = _c(jnp.concatenate([ao * _rstd(ao) * _small(pk_ref, "out_norm_attn"),
                                     conv_out * _rstd(conv_out) * _small(pk_ref, "out_norm_conv"),
                                     mem_out * _rstd(mem_out) * _small(pk_ref, "out_norm_mem")], axis=1))
        mg_ref[...] = merged
        x1 = x_ref[...] + _nn(merged, wout_ref[...])
        x1_ref[...] = x1
        h_ref[...] = _c(x1 * _rstd(x1) * _small(pk_ref, "norm_ffn"))

    tile = lambda w, col: pl.BlockSpec((tm, w), lambda i: (i, col))
    halo = lambda col: pl.BlockSpec((8, CONV_W), lambda i: (jnp.maximum(i * (tm // 8) - 1, 0), col))
    seq = pl.BlockSpec((NM, MEM_W), lambda i: ((i * tm) // S, 0))
    small = lambda a: pl.BlockSpec(a.shape, lambda i: (0, 0))
    return _run("mixer_tail_fwd", body, (T // tm,),
                [x2d, attn_out, proj, proj, proj, proj, proj, qkv, km, vm, conv_w8, pk, wout],
                [tile(D, 0), tile(ATT_W, 0), tile(CONV_W, 3), tile(CONV_W, 4), tile(CONV_W, 5), halo(3), halo(5),
                 tile(MEM_W, 3), seq, seq, VM, VM, VM],
                [SDS((T, CONV_W), f32), SDS((T, MEM_W), f32), SDS((T, D), MXU), SDS((T, D), f32), SDS((T, D), MXU)],
                [tile(CONV_W, 0), tile(MEM_W, 0), tile(D, 0), tile(D, 0), tile(D, 0)], vmem_mib=40, exchange=exchange)


def ffn_fwd_bwd(h, x1, tgt, wgT, wuT, wd, pk, tm):
    T, D = x1.shape
    F = wd.shape[0]

    def body(h_ref, x1_ref, t_ref, wg_ref, wu_ref, wd_ref, pk_ref,
             dx1_ref, dx2_ref, act_ref, dg_ref, du_ref, loss_ref, dgf_ref):
        @pl.when(pl.program_id(0) == 0)
        def _():
            loss_ref[...] = jnp.zeros_like(loss_ref)
            dgf_ref[...] = jnp.zeros_like(dgf_ref)

        hv = h_ref[...]
        gate = _nt(hv, wg_ref[...])
        up = _nt(hv, wu_ref[...])
        sg = jax.nn.sigmoid(gate)
        sl = gate * sg
        act = _c(sl * up)
        act_ref[...] = act
        x1v = x1_ref[...]
        diff = (x1v + _nn(act, wd_ref[...])) - t_ref[...]
        loss_ref[...] += 0.5 * jnp.sum(jnp.sum(diff * diff, axis=-1, keepdims=True) / D, axis=0, keepdims=True)
        dx2 = diff / D
        dx2b = _c(dx2)
        dx2_ref[...] = dx2b
        d_act = _nt(dx2b, wd_ref[...])
        d_up = _c(d_act * sl)
        d_gate = _c(d_act * up * (sg * (1.0 + gate * (1.0 - sg))))
        du_ref[...] = d_up
        dg_ref[...] = d_gate
        dh = _nn(d_gate, wg_ref[...]) + _nn(d_up, wu_ref[...])
        dv, dgf = _norm_bwd(dh, x1v, _rstd(x1v), _small(pk_ref, "norm_ffn"))
        dx1_ref[...] = dx2 + dv
        dgf_ref[...] += dgf

    tile = lambda w: pl.BlockSpec((tm, w), lambda i: (i, 0))
    return _run("ffn_fwd_bwd", body, (T // tm,), [h, x1, tgt, wgT, wuT, wd, pk],
                [tile(D), tile(D), tile(D), VM, VM, VM, VM],
                [SDS((T, D), f32), SDS((T, D), MXU), SDS((T, F), MXU), SDS((T, F), MXU), SDS((T, F), MXU),
                 SDS((8, 128), f32), SDS((1, D), f32)],
                [tile(D), tile(D), tile(F), tile(F), tile(F), pl.BlockSpec((8, 128), lambda i: (0, 0)),
                 pl.BlockSpec((1, D), lambda i: (0, 0))], vmem_mib=56)


def matmul_tn(a, b, name, tmo, tk):
    T, M = a.shape
    N = b.shape[1]

    def body(a_ref, b_ref, o_ref):
        @pl.when(pl.program_id(1) == 0)
        def _():
            o_ref[...] = jnp.zeros_like(o_ref)

        o_ref[...] += _tn(a_ref[...], b_ref[...])

    return _run(name, body, (M // tmo, T // tk), [a, b],
                [pl.BlockSpec((tk, tmo), lambda m, k: (k, m)), pl.BlockSpec((tk, N), lambda m, k: (k, 0))],
                [SDS((M, N), f32)], [pl.BlockSpec((tmo, N), lambda m, k: (m, 0))], vmem_mib=48)[0]


def out_proj_bwd(dx1, merged, attn_out, conv_out, mem_out, pk, wout, tm):
    T, D = dx1.shape

    def body(dx1_ref, mg_ref, ao_ref, co_ref, mo_ref, pk_ref, w_ref,
             dao_ref, dco_ref, dmo_ref, dw_ref, dgain_ref):
        @pl.when(pl.program_id(0) == 0)
        def _():
            dw_ref[...] = jnp.zeros_like(dw_ref)
            dgain_ref[...] = jnp.zeros_like(dgain_ref)

        dxb = _c(dx1_ref[...])
        dw_ref[...] += _tn(mg_ref[...], dxb)
        dmg = _nt(dxb, w_ref[...])
        ao, co, mo = ao_ref[...], co_ref[...], mo_ref[...]
        da, ga = _norm_bwd(dmg[:, :ATT_W], ao, _rstd(ao), _small(pk_ref, "out_norm_attn"))
        dc, gc = _norm_bwd(dmg[:, ATT_W:ATT_W + CONV_W], co, _rstd(co), _small(pk_ref, "out_norm_conv"))
        dm, gm = _norm_bwd(dmg[:, ATT_W + CONV_W:], mo, _rstd(mo), _small(pk_ref, "out_norm_mem"))
        dao_ref[...] = da
        dco_ref[...] = dc
        dmo_ref[...] = dm
        dgain_ref[...] += jnp.concatenate([ga, gc, gm], axis=1)

    tile = lambda w: pl.BlockSpec((tm, w), lambda i: (i, 0))
    return _run("out_proj_bwd", body, (T // tm,), [dx1, merged, attn_out, conv_out, mem_out, pk, wout],
                [tile(D), tile(D), tile(ATT_W), tile(CONV_W), tile(MEM_W), VM, VM],
                [SDS((T, ATT_W), f32), SDS((T, CONV_W), f32), SDS((T, MEM_W), f32), SDS((D, D), f32), SDS((1, D), f32)],
                [tile(ATT_W), tile(CONV_W), tile(MEM_W), pl.BlockSpec((D, D), lambda i: (0, 0)),
                 pl.BlockSpec((1, D), lambda i: (0, 0))], vmem_mib=40)


def attn_bwd(qkv, d_attn, attn_out, sink_rows, BL, S, exchange):
    NB = S // BLK
    T = BL * S

    def body(q_ref, kc_ref, kp_ref, vc_ref, vp_ref, do_ref, ao_ref, sk_ref, tab_ref,
             dq_ref, dk_ref, dv_ref, dsk_ref, pend_k, pend_v):
        b, j = pl.program_id(0), pl.program_id(1)

        @pl.when((b == 0) & (j == 0))
        def _():
            dsk_ref[...] = jnp.zeros_like(dsk_ref)

        @pl.when(j == 0)
        def _():
            pend_k[...] = jnp.zeros_like(pend_k)
            pend_v[...] = jnp.zeros_like(pend_v)

        @pl.when(j < NB)
        def _():
            q, do, ao = q_ref[...], do_ref[...], ao_ref[...]
            k2 = jnp.concatenate([kp_ref[...], kc_ref[...]], axis=0)
            v2 = jnp.concatenate([vp_ref[...], vc_ref[...]], axis=0)
            lane = lax.broadcasted_iota(jnp.int32, (8, 128), 1)
            ones_w = jnp.ones((2 * BLK, 2 * BLK), MXU)
            dsk = jnp.zeros((8, 128), f32)
            dks, dvs = [], []
            for g in range(N_KV):
                kn, vh = k2[:, g * HD:(g + 1) * HD], v2[:, g * HD:(g + 1) * HD]
                qs = _stack_heads(q, g)
                s = _nt(qs, kn) * (HD ** -0.5) + tab_ref[g]
                e, es = _exp_scores(s, _sink_column(g, sk_ref))
                eb = _c(e)
                inv_w = 1.0 / (_nn(eb, ones_w) + es)
                inv_n = inv_w[:, :HD]
                dos = _stack_heads(do, g)
                delta = _rowsum_mxu(dos * _stack_heads(ao, g), 2 * BLK)
                dp = _nt(_c(dos), vh)
                ds = _c(e * inv_w * (dp - delta) * (HD ** -0.5))
                t = es * inv_n[:, 0:1] * delta[:, 0:1]
                for hh in range(GQA):
                    dsk = dsk + jnp.where(lane == g * GQA + hh, -jnp.sum(t[hh * BLK:(hh + 1) * BLK]), 0.0)
                dvs.append(_tn(eb, _c(dos * inv_n)))
                dks.append(_tn(ds, qs))
                dqs = _nn(ds, kn)
                for hh in range(GQA):
                    dq_ref[:, pl.ds((g * GQA + hh) * HD, HD)] = dqs[hh * BLK:(hh + 1) * BLK]
            dk2 = jnp.concatenate(dks, axis=1)
            dv2 = jnp.concatenate(dvs, axis=1)
            dk_ref[...] = pend_k[...] + dk2[:BLK]
            dv_ref[...] = pend_v[...] + dv2[:BLK]
            pend_k[...] = dk2[BLK:]
            pend_v[...] = dv2[BLK:]
            dsk_ref[...] += dsk

        @pl.when(j == NB)
        def _():
            dk_ref[...] = pend_k[...]
            dv_ref[...] = pend_v[...]

    cur = lambda col: (lambda b, j: (b * NB + jnp.minimum(j, NB - 1), col))
    prev = lambda col: (lambda b, j: (b * NB + jnp.maximum(j - 1, 0), col))
    small = lambda shape: pl.BlockSpec(shape, lambda b, j: (0, 0))
    return _run("attn_bwd", body, (BL, NB + 1), [qkv, qkv, qkv, qkv, qkv, d_attn, attn_out, sink_rows, _swa_bias_table()],
                [pl.BlockSpec((BLK, ATT_W), cur(0)),
                 pl.BlockSpec((BLK, KV_W), cur(4)), pl.BlockSpec((BLK, KV_W), prev(4)),
                 pl.BlockSpec((BLK, KV_W), cur(5)), pl.BlockSpec((BLK, KV_W), prev(5)),
                 pl.BlockSpec((BLK, ATT_W), cur(0)), pl.BlockSpec((BLK, ATT_W), cur(0)), small((8, 128)),
                 pl.BlockSpec((None, N_KV, GQA * BLK, 2 * BLK), lambda b, j: (jnp.minimum(j, 1), 0, 0, 0))],
                [SDS((T, ATT_W), f32), SDS((T, KV_W), f32), SDS((T, KV_W), f32), SDS((8, 128), f32)],
                [pl.BlockSpec((BLK, ATT_W), cur(0)), pl.BlockSpec((BLK, KV_W), prev(0)),
                 pl.BlockSpec((BLK, KV_W), prev(0)), small((8, 128))],
                scratch=[pltpu.VMEM((BLK, KV_W), f32)] * 2, vmem_mib=56, exchange=exchange)


def mem_conv_bwd(d_mem_out, mem_out, d_conv_out, proj, qkv, km, vm, conv_w8, pk, S, tm, exchange):
    T = d_mem_out.shape[0]
    NM = km.shape[0] // (T // S)

    def body(dmo_ref, mo_ref, dco_ref, ch_ref, cb_ref, cc_ref, chh_ref, cch_ref, qm_ref, km_ref, vm_ref, cw_ref,
             pk_ref, dqm_ref, dkm_ref, dvm_ref, dcb_ref, dcv_ref, dcw_ref, dcbias_ref):
        i = pl.program_id(0)
        first = (i * tm) % S == 0

        @pl.when(i == 0)
        def _():
            dcw_ref[...] = jnp.zeros_like(dcw_ref)
            dcbias_ref[...] = jnp.zeros_like(dcbias_ref)

        @pl.when(first)
        def _():
            dkm_ref[...] = jnp.zeros_like(dkm_ref)
            dvm_ref[...] = jnp.zeros_like(dvm_ref)

        qm, kmv, vmv, dmo, mo = qm_ref[...], km_ref[...], vm_ref[...], dmo_ref[...], mo_ref[...]
        ones_w = jnp.ones((NM, NM), MXU)
        for h in range(N_MEMH):
            qh, kh, vh, e = _mem_head(qm, kmv, vmv, h)
            eb = _c(e)
            doh = dmo[:, h * HD:(h + 1) * HD]
            delta = _rowsum_mxu(doh * mo[:, h * HD:(h + 1) * HD], NM)
            dp = _nt(_c(doh), vh)
            inv_w = 1.0 / _nn(eb, ones_w)
            ds = _c(e * inv_w * (dp - delta) * (HD ** -0.5))
            dvm_ref[:, pl.ds(h * HD, HD)] += _tn(eb, _c(doh * inv_w[:, :HD]))
            dkm_ref[:, pl.ds(h * HD, HD)] += _tn(ds, qh)
            dqm_ref[:, pl.ds(h * HD, HD)] = _nn(ds, kh)

        u = cc_ref[...] * ch_ref[...]
        uh = jnp.where(first, 0.0, cch_ref[...] * chh_ref[...])
        u1, u2 = _conv_taps(u, uh)
        conv = cw_ref[0:1, :] * u2 + cw_ref[1:2, :] * u1 + cw_ref[2:3, :] * u + _small(pk_ref, "conv_b")
        dy = dco_ref[...]
        dcb_ref[...] = dy * conv
        dcv = dy * cb_ref[...]
        dcv_ref[...] = dcv
        dcbias_ref[...] += jnp.sum(dcv, axis=0, keepdims=True)
        dcw_ref[0:1, :] += jnp.sum(dcv * u2, axis=0, keepdims=True)
        dcw_ref[1:2, :] += jnp.sum(dcv * u1, axis=0, keepdims=True)
        dcw_ref[2:3, :] += jnp.sum(dcv * u, axis=0, keepdims=True)

    tile = lambda w, col: pl.BlockSpec((tm, w), lambda i: (i, col))
    halo = lambda col: pl.BlockSpec((8, CONV_W), lambda i: (jnp.maximum(i * (tm // 8) - 1, 0), col))
    seq = pl.BlockSpec((NM, MEM_W), lambda i: ((i * tm) // S, 0))
    const = lambda shape: pl.BlockSpec(shape, lambda i: (0, 0))
    return _run("mem_conv_bwd", body, (T // tm,),
                [d_mem_out, mem_out, d_conv_out, proj, proj, proj, proj, proj, qkv, km, vm, conv_w8, pk],
                [tile(MEM_W, 0), tile(MEM_W, 0), tile(CONV_W, 0), tile(CONV_W, 3), tile(CONV_W, 4), tile(CONV_W, 5),
                 halo(3), halo(5), tile(MEM_W, 3), seq, seq, VM, VM],
                [SDS((T, MEM_W), f32), SDS(km.shape, f32), SDS(km.shape, f32),
                 SDS((T, CONV_W), f32), SDS((T, CONV_W), f32), SDS((8, CONV_W), f32), SDS((1, CONV_W), f32)],
                [tile(MEM_W, 0), seq, seq, tile(CONV_W, 0), tile(CONV_W, 0), const((8, CONV_W)), const((1, CONV_W))],
                vmem_mib=48, exchange=exchange)


def in_proj_bwd(dqn, dkn, dv, dcb, dcv, dqmn, proj, conv_w8, xn, x2d, dx1, pk, winT, S, tm, stages, ws, ms, vs):
    T, D = x2d.shape
    P = winT.shape[0]
    last_blk = T // 8 - 1
    n = len(stages)
    nsteps = T // tm
    tile_w = ws[0].shape[1] // (nsteps // 2)
    turn = [e * 2 // n for e in range(n)]

    def body(dq_ref, dk_ref, dv_ref, dcb_ref, dcv_ref, dcvn_ref, dqm_ref, qa_ref, ka_ref, ch_ref, cc_ref, qma_ref,
             cw_ref, xn_ref, x_ref, dx1_ref, pk_ref, w_ref, *rest):
        st, aw, am, av = (rest[k * n:(k + 1) * n] for k in range(4))
        dx_ref, dw_ref, dg_ref, dqg_ref, dkg_ref, dmqg_ref = rest[4 * n:4 * n + 6]
        aouts = rest[4 * n + 6:]
        i = pl.program_id(0)

        for parity in range(2):
            @pl.when(i % 2 == parity)
            def _(parity=parity):
                for e in range(n):
                    if turn[e] == parity:
                        g = jnp.concatenate([_sum_chips(st[e].at[0]), _sum_chips(st[e].at[1])], axis=0)
                        d, mm, vv = _adamw_math(aw[e][...], g, am[e][...], av[e][...])
                        for k, val in enumerate((g, d, mm, vv)):
                            aouts[4 * e + k][...] = val

        @pl.when(i == 0)
        def _():
            dw_ref[...] = jnp.zeros_like(dw_ref)
            dg_ref[...] = jnp.zeros_like(dg_ref)
            dqg_ref[...] = jnp.zeros_like(dqg_ref)
            dkg_ref[...] = jnp.zeros_like(dkg_ref)
            dmqg_ref[...] = jnp.zeros_like(dmqg_ref)

        dqa, gq = _heads_norm_bwd(dq_ref[...], qa_ref[...], _small(pk_ref, "q_norm"))
        dka, gk = _heads_norm_bwd(dk_ref[...], ka_ref[...], _small(pk_ref, "k_norm"))
        dqma, gmq = _heads_norm_bwd(dqm_ref[...], qma_ref[...], _small(pk_ref, "mem_q_norm"))
        dqg_ref[...] += gq
        dkg_ref[...] += gk
        dmqg_ref[...] += gmq

        last = ((i + 1) * tm) % S == 0
        dcv = dcv_ref[...]
        nxt = jnp.where(last, 0.0, dcvn_ref[...])
        row = lax.broadcasted_iota(jnp.int32, dcv.shape, 0)
        n1 = jnp.where(row == tm - 1, nxt[0:1, :], pltpu.roll(dcv, tm - 1, 0))
        n2 = jnp.where(row == tm - 2, nxt[0:1, :], jnp.where(row == tm - 1, nxt[1:2, :], pltpu.roll(dcv, tm - 2, 0)))
        du = cw_ref[2:3, :] * dcv + cw_ref[1:2, :] * n1 + cw_ref[0:1, :] * n2
        d_proj = jnp.concatenate([_c(dqa), _c(dka), _c(dv_ref[...]), _c(du * cc_ref[...]),
                                  _c(dcb_ref[...]), _c(du * ch_ref[...]), _c(dqma)], axis=1)
        dw_ref[...] += _tn(d_proj, xn_ref[...])
        xv = x_ref[...]
        dv_, dg = _norm_bwd(_nn(d_proj, w_ref[...]), xv, _rstd(xv), _small(pk_ref, "norm_mix"))
        dx_ref[...] = dx1_ref[...] + dv_
        dg_ref[...] += dg

    tile = lambda w, col=0: pl.BlockSpec((tm, w), lambda i: (i, col))
    nhalo = pl.BlockSpec((8, CONV_W), lambda i: (jnp.minimum((i + 1) * (tm // 8), last_blk), 0))
    const = lambda shape: pl.BlockSpec(shape, lambda i: (0, 0))
    st_specs = [pl.BlockSpec((2, 4, s.shape[2], tile_w), lambda i: (0, 0, 0, i // 2)) for s in stages]
    w_specs = [pl.BlockSpec((w.shape[0], tile_w), lambda i: (0, i // 2)) for w in ws]
    res = _run("in_proj_bwd", body, (nsteps,),
               [dqn, dkn, dv, dcb, dcv, dcv, dqmn, proj, proj, proj, proj, proj, conv_w8, xn, x2d, dx1, pk, winT]
               + list(stages) + list(ws) + list(ms) + list(vs),
               [tile(ATT_W), tile(KV_W), tile(KV_W), tile(CONV_W), tile(CONV_W), nhalo, tile(MEM_W),
                tile(ATT_W, 0), tile(KV_W, 4), tile(CONV_W, 3), tile(CONV_W, 5), tile(MEM_W, 6), VM,
                tile(D), tile(D), tile(D), VM, VM] + st_specs + w_specs * 3,
               [SDS((T, D), f32), SDS((P, D), f32), SDS((1, D), f32), SDS((1, HD), f32), SDS((1, HD), f32),
                SDS((1, HD), f32)] + [SDS(w.shape, f32) for w in ws for _ in range(4)],
               [tile(D), pl.BlockSpec((P, D), lambda i: (0, 0)), const((1, D)), const((1, HD)), const((1, HD)),
                const((1, HD))] + [s for s in w_specs for _ in range(4)],
               vmem_mib=56)
    return res[:6], [res[6 + 4 * e:10 + 4 * e] for e in range(n)]


def mem_kv_bwd(dkm, dvm, kv, memn, mem2d, pk, wmkv):
    def body(dkm_ref, dvm_ref, kv_ref, mn_ref, m_ref, pk_ref, w_ref, dw_ref, dg_ref, dkg_ref):
        dkk, dkg = _heads_norm_bwd(dkm_ref[...], kv_ref[:, :MEM_W], _small(pk_ref, "mem_k_norm"))
        dkg_ref[...] = dkg
        dkv = _c(jnp.concatenate([dkk, dvm_ref[...]], axis=1))
        dw_ref[...] = _tn(mn_ref[...], dkv)
        mv = m_ref[...]
        dg_ref[...] = jnp.sum(_nt(dkv, w_ref[...]) * mv * _rstd(mv), axis=0, keepdims=True)

    return _run("mem_kv_bwd", body, (), [dkm, dvm, kv, memn, mem2d, pk, wmkv], [VM] * 7,
                [SDS(wmkv.shape, f32), SDS((1, mem2d.shape[1]), f32), SDS((1, HD), f32)], [VM] * 3, vmem_mib=40)


def _halves_view(g):
    return g.reshape(4, 2, g.shape[0] // 8, g.shape[1])


def kernel(x, mem, norm_mix, w_in, q_norm, k_norm, attn_sinks, conv_w, conv_b, norm_mem, w_mem_kv, mem_q_norm, mem_k_norm, out_norm_attn, out_norm_conv, out_norm_mem, w_out, norm_ffn, w_gate, w_up, w_down, loss_target, m_norm_mix, m_w_in, m_q_norm, m_k_norm, m_attn_sinks, m_conv_w, m_conv_b, m_norm_mem, m_w_mem_kv, m_mem_q_norm, m_mem_k_norm, m_out_norm_attn, m_out_norm_conv, m_out_norm_mem, m_w_out, m_norm_ffn, m_w_gate, m_w_up, m_w_down, v_norm_mix, v_w_in, v_q_norm, v_k_norm, v_attn_sinks, v_conv_w, v_conv_b, v_norm_mem, v_w_mem_kv, v_mem_q_norm, v_mem_k_norm, v_out_norm_attn, v_out_norm_conv, v_out_norm_mem, v_w_out, v_norm_ffn, v_w_gate, v_w_up, v_w_down):
    BL, S, D = x.shape
    T = BL * S
    TM = 256
    TM_BIG = min(512, S)
    w_small = dict(norm_mix=norm_mix, norm_mem=norm_mem, norm_ffn=norm_ffn, out_norm_attn=out_norm_attn,
                   out_norm_conv=out_norm_conv, out_norm_mem=out_norm_mem, conv_w=conv_w, conv_b=conv_b, q_norm=q_norm,
                   k_norm=k_norm, mem_q_norm=mem_q_norm, mem_k_norm=mem_k_norm, attn_sinks=attn_sinks)
    m_small = dict(norm_mix=m_norm_mix, norm_mem=m_norm_mem, norm_ffn=m_norm_ffn, out_norm_attn=m_out_norm_attn,
                   out_norm_conv=m_out_norm_conv, out_norm_mem=m_out_norm_mem, conv_w=m_conv_w, conv_b=m_conv_b,
                   q_norm=m_q_norm, k_norm=m_k_norm, mem_q_norm=m_mem_q_norm, mem_k_norm=m_mem_k_norm,
                   attn_sinks=m_attn_sinks)
    v_small = dict(norm_mix=v_norm_mix, norm_mem=v_norm_mem, norm_ffn=v_norm_ffn, out_norm_attn=v_out_norm_attn,
                   out_norm_conv=v_out_norm_conv, out_norm_mem=v_out_norm_mem, conv_w=v_conv_w, conv_b=v_conv_b,
                   q_norm=v_q_norm, k_norm=v_k_norm, mem_q_norm=v_mem_q_norm, mem_k_norm=v_mem_k_norm,
                   attn_sinks=v_attn_sinks)
    pk = _pack_small(w_small)

    rowblocks = lambda a, b, c, d, e, f: [a[0].T, b[0].T, c[0].T, d[0], e[0], f[0]]
    w_rb = rowblocks(w_in, w_gate, w_up, w_down, w_out, w_mem_kv)
    m_rb = rowblocks(m_w_in, m_w_gate, m_w_up, m_w_down, m_w_out, m_w_mem_kv)
    v_rb = rowblocks(v_w_in, v_w_gate, v_w_up, v_w_down, v_w_out, v_w_mem_kv)
    (winT_s,) = prep_weights("prep_w_in", w_rb[:1])
    cw_pad = jnp.zeros((8, 128), f32).at[:3, :HD].set(conv_w[0])
    (wgT_s, wuT_s, wd_s, wout_s, wmkv_s), (winT, cw_all) = prep_weights(
        "gather_w_in", w_rb[1:], _together([gather_two_legs([winT_s]), gather_exchange([cw_pad], [False])]))
    conv_w_full = jnp.transpose(cw_all.reshape(4, 8, 128)[:, :3, :HD], (1, 0, 2)).reshape(3, CONV_W)
    conv_w8 = jnp.zeros((8, CONV_W), f32).at[:3].set(conv_w_full)
    sink_rows = jnp.broadcast_to(attn_sinks.reshape(N_Q, 1), (N_Q, 128))

    x2d = x.reshape(T, D)
    mem2d = mem.reshape(-1, D)
    (xn, proj, qkv), near1 = in_proj_fwd(x2d, pk, winT, TM_BIG, gather_near_exchange([wgT_s, wout_s, wmkv_s], relay_early=1))
    (attn_out,), (wgT, wout, wmkv, *near2) = attn_fwd(
        qkv, sink_rows, BL, S, _together([gather_far_exchange(near1, relay_early=2), gather_near_exchange([wuT_s, wd_s], relay_early=2)]))
    memn, kv, km, vm = mem_kv_fwd(mem2d, pk, wmkv)
    (conv_out, mem_out, merged, x1, h), (wuT, wd) = mixer_tail_fwd(
        x2d, attn_out, proj, qkv, km, vm, conv_w8, pk, wout, S, TM_BIG, gather_far_exchange(near2, relay_early=2))

    dx1, dx2b, act, d_gate, d_up, loss8, d_norm_ffn = ffn_fwd_bwd(h, x1, loss_target.reshape(T, D), wgT, wuT, wd, pk, TM)
    F = wd.shape[0]
    g_wd = matmul_tn(act, dx2b, "dw_down", F // 2, min(T, 1024))
    g_wgT = matmul_tn(d_gate, h, "dw_gate", F // 2, min(T, 1024))
    g_wuT = matmul_tn(d_up, h, "dw_up", F // 2, min(T, 1024))

    d_attn, d_conv_out, d_mem_out, g_wout, d_gains = out_proj_bwd(dx1, merged, attn_out, conv_out, mem_out, pk, wout, TM_BIG)
    TM_MC = min(1024, S)
    (dqmn, dkm, dvm, dcb, dcv, d_cw8, d_cbias), (st_wout,) = mem_conv_bwd(
        d_mem_out, mem_out, d_conv_out, proj, qkv, km, vm, conv_w8, pk, S, TM_MC,
        reduce_scatter_exchange([_halves_view(g_wout)], T // TM_MC, load_step=[0], send_step=[1], relay_step=[T // TM_MC - 1]))
    (dqn, dkn, dv, d_sink8), (st_wgT, st_wuT, st_wd) = attn_bwd(
        qkv, d_attn, attn_out, sink_rows, BL, S,
        reduce_scatter_exchange([_halves_view(g) for g in (g_wgT, g_wuT, g_wd)], BL * (S // BLK + 1),
                                load_step=[0, 2, 5], send_step=[2, 5, 8], relay_step=[13, 22, 31]))
    (g_x, g_winT, d_norm_mix, d_qg, d_kg, d_mqg), late_res = in_proj_bwd(
        dqn, dkn, dv, dcb, dcv, dqmn, proj, conv_w8, xn, x2d, dx1, pk, winT, S, TM,
        [st_wgT, st_wuT, st_wd, st_wout], w_rb[1:5], m_rb[1:5], v_rb[1:5])
    g_wmkv, d_norm_mem, d_mkg = mem_kv_bwd(dkm, dvm, kv, memn, mem2d, pk, wmkv)

    tot, tail_stage = tail_reduce(d_norm_mix, d_norm_mem, d_norm_ffn, d_gains, d_cw8, d_cbias, d_qg, d_kg, d_mqg, d_mkg,
                                  d_sink8, loss8, [_halves_view(g) for g in (g_winT, g_wmkv)])
    loss = tot[5, 384]
    tail_res, _ = adamw_big("adamw_tail", tail_stage, [w_rb[0], w_rb[5]], [m_rb[0], m_rb[5]], [v_rb[0], v_rb[5]], 4)
    res = {"w_in": [a.T[None] for a in tail_res[0]], "w_gate": [a.T[None] for a in late_res[0]],
           "w_up": [a.T[None] for a in late_res[1]], "w_down": [a[None] for a in late_res[2]],
           "w_out": [a[None] for a in late_res[3]], "w_mem_kv": [a[None] for a in tail_res[1]]}
    res.update(adamw_small(tot, pk, _pack_small(m_small), _pack_small(v_small), {k: w_small[k].shape for k in SMALL}))

    order = ["norm_mix", "w_in", "q_norm", "k_norm", "attn_sinks", "conv_w", "conv_b", "norm_mem", "w_mem_kv",
             "mem_q_norm", "mem_k_norm", "out_norm_attn", "out_norm_conv", "out_norm_mem", "w_out", "norm_ffn",
             "w_gate", "w_up", "w_down"]
    return (loss, g_x.reshape(BL, S, D), *[res[n][0] for n in order], *[res[n][1] for n in order],
            *[res[n][2] for n in order], *[res[n][3] for n in order])
```

```python
import collections
import functools

import jax
import jax.numpy as jnp
import numpy as np
from jax import lax
from jax.experimental import pallas as pl
from jax.experimental.pallas import tpu as pltpu

f32 = jnp.float32
MXU = jnp.bfloat16
WIRE = jnp.bfloat16
EPS = 1e-6
NEG = -1e30
HD = 64
BLK = 128
N_Q, N_KV, N_MEMH = 8, 2, 4
GQA = N_Q // N_KV
ATT_W, KV_W, CONV_W, MEM_W = 512, 128, 256, 256
VMEM_MIB = 1024 * 1024
ADAM_LR, ADAM_B1, ADAM_B2, ADAM_EPS, ADAM_WD, ADAM_STEP = 0.001, 0.9, 0.999, 1e-08, 0.01, 10

MESH = pl.DeviceIdType.MESH
VM = pl.BlockSpec(memory_space=pltpu.VMEM)
ANY = pl.BlockSpec(memory_space=pl.ANY)
SDS = jax.ShapeDtypeStruct
DMA = pltpu.SemaphoreType.DMA


def _c(v):
    return v.astype(MXU)


def _nn(a, b):
    return lax.dot_general(a, b, (((1,), (0,)), ((), ())), preferred_element_type=f32)


def _nt(a, b):
    return lax.dot_general(a, b, (((1,), (1,)), ((), ())), preferred_element_type=f32)


def _tn(a, b):
    return lax.dot_general(a, b, (((0,), (0,)), ((), ())), preferred_element_type=f32)


def _rstd(v):
    return lax.rsqrt(jnp.mean(v * v, axis=-1, keepdims=True) + EPS)


def _norm_bwd(dy, v, r, g):
    dyg = dy * g
    dv = r * dyg - v * (r * r * r) * jnp.mean(dyg * v, axis=-1, keepdims=True)
    return dv, jnp.sum(dy * v * r, axis=0, keepdims=True)


def _split3(v):
    hi = _c(v)
    r1 = v - hi.astype(f32)
    mid = _c(r1)
    return hi, mid, _c(r1 - mid.astype(f32))


def _rowsum_mxu(v, width):
    ones = jnp.ones((v.shape[1], width), MXU)
    return sum(_nn(a, ones) for a in _split3(v))


def _seg_sums(v):
    r = lax.broadcasted_iota(jnp.int32, (2 * HD, 2 * HD), 0) // HD
    c = lax.broadcasted_iota(jnp.int32, (2 * HD, 2 * HD), 1) // HD
    bd = (r == c).astype(MXU)
    outs = []
    for b in range(v.shape[1] // (2 * HD)):
        outs.append(sum(_nn(a, bd) for a in _split3(v[:, b * 2 * HD:(b + 1) * 2 * HD])))
    return outs[0] if len(outs) == 1 else jnp.concatenate(outs, axis=1)


def _lanes(g, width):
    return jnp.concatenate([g] * (width // HD), axis=1)


def _heads_rstd(v):
    return lax.rsqrt(_seg_sums(v * v) * (1.0 / HD) + EPS)


def _heads_norm_bwd(dy, v, g):
    r = _heads_rstd(v)
    gl = _lanes(g, v.shape[1])
    dyg = dy * gl
    dv = r * dyg - v * (r * r * r) * (_seg_sums(dyg * v) * (1.0 / HD))
    dgl = jnp.sum(dy * v * r, axis=0, keepdims=True)
    return dv, sum(dgl[:, s * HD:(s + 1) * HD] for s in range(v.shape[1] // HD))


def _exp_scores(s, extra=None):
    m = jnp.max(s, axis=-1, keepdims=True)
    if extra is None:
        return jnp.exp(s - m), None
    m = jnp.maximum(m, extra)
    return jnp.exp(s - m), jnp.exp(extra - m)


def _place():
    return lax.axis_index("x"), lax.axis_index("y"), lax.axis_index("c")


SMALL_AT = {"norm_mix": (0, 0, 1024), "norm_mem": (1, 0, 1024), "norm_ffn": (2, 0, 1024),
            "out_norm_attn": (3, 0, ATT_W), "out_norm_conv": (3, ATT_W, CONV_W), "out_norm_mem": (3, ATT_W + CONV_W, MEM_W),
            "conv_b": (4, 3 * CONV_W, CONV_W), "q_norm": (5, 0, HD), "k_norm": (5, HD, HD), "mem_q_norm": (5, 2 * HD, HD),
            "mem_k_norm": (5, 3 * HD, HD), "attn_sinks": (5, 256, N_Q)}
SMALL = ("norm_mix", "norm_mem", "norm_ffn", "out_norm_attn", "out_norm_conv", "out_norm_mem", "conv_w", "conv_b",
         "q_norm", "k_norm", "mem_q_norm", "mem_k_norm", "attn_sinks")


def _small(pk_ref, name):
    r, c0, w = SMALL_AT[name]
    return pk_ref[r:r + 1, c0:c0 + w]


def _pack_small(d):
    z = lambda n: jnp.zeros((1, n), f32)
    row3 = jnp.concatenate([d["out_norm_attn"], d["out_norm_conv"], d["out_norm_mem"]], axis=1)
    row4 = jnp.concatenate([d["conv_w"].reshape(1, 3 * HD), z(3 * CONV_W - 3 * HD), d["conv_b"]], axis=1)
    row5 = jnp.concatenate([d["q_norm"], d["k_norm"], d["mem_q_norm"], d["mem_k_norm"], d["attn_sinks"],
                            z(1024 - 4 * HD - N_Q)], axis=1)
    return jnp.concatenate([d["norm_mix"], d["norm_mem"], d["norm_ffn"], row3, row4, row5, z(1024), z(1024)], axis=0)


def _other_chips(x, y):
    return [(1 - x, y), (x, 1 - y), (1 - x, 1 - y)]


Exchange = collections.namedtuple("Exchange", "ins outs sems start finish relays aliases", defaults=((), {}))


def _together(exchanges):
    def bounds(key):
        at, out = 0, []
        for ex in exchanges:
            out.append((at, at + len(getattr(ex, key))))
            at += len(getattr(ex, key))
        return out

    bi, bo, bs = bounds("ins"), bounds("outs"), bounds("sems")

    def of(i, fn):
        return lambda xa, xo, xs: fn(xa[bi[i][0]:bi[i][1]], xo[bo[i][0]:bo[i][1]], xs[bs[i][0]:bs[i][1]])

    def every(name):
        fns = [of(i, getattr(ex, name)) for i, ex in enumerate(exchanges)]

        def run(xa, xo, xs):
            for fn in fns:
                fn(xa, xo, xs)
        return run

    aliases = {}
    for i, ex in enumerate(exchanges):
        aliases.update({bi[i][0] + a: bo[i][0] + o for a, o in ex.aliases.items()})
    return Exchange([a for ex in exchanges for a in ex.ins], [o for ex in exchanges for o in ex.outs],
                    [s for ex in exchanges for s in ex.sems], every("start"), every("finish"),
                    [(sbe, of(i, fn)) for i, ex in enumerate(exchanges) for sbe, fn in ex.relays], aliases)


def _run(name, body, grid, ins, in_specs, out_shape, out_specs, scratch=(), vmem_mib=32, exchange=None):
    ins, in_specs, out_shape, out_specs, scratch = list(ins), list(in_specs), list(out_shape), list(out_specs), list(scratch)
    ni, no, ns = len(ins), len(out_shape), len(scratch)
    ex = exchange
    if ex is not None:
        nxi, nxo = len(ex.ins), len(ex.outs)

    def call_body(*refs):
        if ex is None:
            body(*refs)
            return
        a, xa = refs[:ni], refs[ni:ni + nxi]
        o, xo = refs[ni + nxi:ni + nxi + no], refs[ni + nxi + no:ni + nxi + no + nxo]
        s, xs = refs[ni + nxi + no + nxo:ni + nxi + no + nxo + ns], refs[ni + nxi + no + nxo + ns:]
        if grid:
            first = functools.reduce(jnp.logical_and, [pl.program_id(d) == 0 for d in range(len(grid))])
            last = functools.reduce(jnp.logical_and, [pl.program_id(d) == grid[d] - 1 for d in range(len(grid))])
            pl.when(first)(lambda: ex.start(xa, xo, xs))
            body(*a, *o, *s)
            nsteps = functools.reduce(lambda p, q: p * q, grid)
            for before_end, fn in ex.relays:
                at = np.unravel_index(max(nsteps - 1 - before_end, 0), grid)
                here = functools.reduce(jnp.logical_and, [pl.program_id(d) == int(at[d]) for d in range(len(grid))])
                pl.when(here)(functools.partial(fn, xa, xo, xs))
            pl.when(last)(lambda: ex.finish(xa, xo, xs))
        else:
            ex.start(xa, xo, xs)
            if body is not None:
                body(*a, *o, *s)
            for _, fn in ex.relays:
                fn(xa, xo, xs)
            ex.finish(xa, xo, xs)

    kw = dict(grid=grid) if grid else {}
    if ex is not None:
        if ex.aliases:
            kw["input_output_aliases"] = {ni + i: no + o for i, o in ex.aliases.items()}
        ins, in_specs = ins + list(ex.ins), in_specs + [ANY] * nxi
        out_shape, out_specs = out_shape + list(ex.outs), out_specs + [ANY] * nxo
        scratch = scratch + list(ex.sems)
    res = pl.pallas_call(
        call_body, name=name, out_shape=out_shape, in_specs=in_specs, out_specs=out_specs, scratch_shapes=scratch,
        compiler_params=pltpu.CompilerParams(dimension_semantics=("arbitrary",) * len(grid) if grid else None,
                                             vmem_limit_bytes=vmem_mib * VMEM_MIB), **kw)(*ins)
    res = list(res)
    return (res[:no], res[no:]) if ex is not None else res


def _remote(src, dst, ssem, rsem, dev):
    return pltpu.make_async_remote_copy(src_ref=src, dst_ref=dst, send_sem=ssem, recv_sem=rsem,
                                        device_id=dev, device_id_type=MESH)


def gather_exchange(shards, split, relay_early=0):
    n = len(shards)

    def rows(ref, e, kk, half=None):
        R = shards[e].shape[0]
        if half is None:
            return ref.at[pl.ds(pl.multiple_of(kk * R, 8), R)]
        return ref.at[pl.ds(pl.multiple_of(kk * R + half * (R // 2), 8), R // 2)]

    def ici(src, dst, sm, e, j, chip_j, x, y, c):
        k = 2 * x + y
        if split[e]:
            s = src[e].at[pl.ds(pl.multiple_of(c * (shards[e].shape[0] // 2), 8), shards[e].shape[0] // 2)]
            return _remote(s, rows(dst[e], e, k, c), sm[0].at[6 * e + j], sm[1].at[6 * e + j], (*chip_j, c))
        return _remote(src[e], rows(dst[e], e, k), sm[0].at[6 * e + j], sm[1].at[6 * e + j], (*chip_j, c))

    def landed(dst, e, chip_j, c):
        kj = 2 * chip_j[0] + chip_j[1]
        return rows(dst[e], e, kj, c) if split[e] else rows(dst[e], e, kj)

    def forward(dst, sm, e, j, chip_j, x, y, c, sender_c):
        kj = 2 * chip_j[0] + chip_j[1]
        r = rows(dst[e], e, kj, sender_c)
        return _remote(r, r, sm[0].at[6 * e + 3 + j], sm[1].at[6 * e + 3 + j], (x, y, 1 - c))

    def local(src, dst, sm, e, x, y):
        return pltpu.make_async_copy(src[e], rows(dst[e], e, 2 * x + y), sm[2].at[e])

    def start(src, dst, sm):
        x, y, c = _place()
        for e in range(n):
            local(src, dst, sm, e, x, y).start()
            for j, chip_j in enumerate(_other_chips(x, y)):
                ici(src, dst, sm, e, j, chip_j, x, y, c).start()

    def relay(src, dst, sm):
        x, y, c = _place()
        for e in range(n):
            for j, chip_j in enumerate(_other_chips(x, y)):
                r = landed(dst, e, chip_j, c)
                _remote(r, r, sm[0].at[6 * e + j], sm[1].at[6 * e + j], (*chip_j, c)).wait_recv()
                if split[e]:
                    forward(dst, sm, e, j, chip_j, x, y, c, c).start()

    def finish(src, dst, sm):
        x, y, c = _place()
        chips = _other_chips(x, y)
        for e in range(n):
            for j, chip_j in enumerate(chips):
                if split[e]:
                    forward(dst, sm, e, j, chip_j, x, y, c, 1 - c).wait_recv()
        for e in range(n):
            for j, chip_j in enumerate(chips):
                ici(src, dst, sm, e, j, chip_j, x, y, c).wait_send()
                if split[e]:
                    forward(dst, sm, e, j, chip_j, x, y, c, c).wait_send()
            local(src, dst, sm, e, x, y).wait()

    outs = [SDS((4 * s.shape[0], s.shape[1]), s.dtype) for s in shards]
    return Exchange(list(shards), outs, [DMA((6 * n,)), DMA((6 * n,)), DMA((n,))], start, finish, [(relay_early, relay)])


def _block_rows(ref, R, kk, half, quarter=None):
    hr = R // 2
    if quarter is None:
        return ref.at[pl.ds(pl.multiple_of(kk * R + half * hr, 8), hr)]
    return ref.at[pl.ds(pl.multiple_of(kk * R + half * hr + quarter * (hr // 2), 8), hr // 2)]


def gather_near_exchange(shards, relay_early=0):
    n = len(shards)
    R = [s.shape[0] for s in shards]

    def ici(src, dst, sm, e, j, chip_j, x, y, c):
        half = src[e].at[pl.ds(pl.multiple_of(c * (R[e] // 2), 8), R[e] // 2)]
        return _remote(half, _block_rows(dst[e], R[e], 2 * x + y, c), sm[0].at[4 * e + j], sm[1].at[4 * e + j], (*chip_j, c))

    def forward(dst, sm, e, j, chip_j, x, y, c, sender_c):
        r = _block_rows(dst[e], R[e], 2 * chip_j[0] + chip_j[1], sender_c)
        return _remote(r, r, sm[0].at[4 * e + 2 + j], sm[1].at[4 * e + 2 + j], (x, y, 1 - c))

    def local(src, dst, sm, e, x, y):
        return pltpu.make_async_copy(src[e], dst[e].at[pl.ds(pl.multiple_of((2 * x + y) * R[e], 8), R[e])], sm[2].at[e])

    def start(src, dst, sm):
        x, y, c = _place()
        for e in range(n):
            local(src, dst, sm, e, x, y).start()
            for j, chip_j in enumerate(_other_chips(x, y)[:2]):
                ici(src, dst, sm, e, j, chip_j, x, y, c).start()

    def relay(src, dst, sm):
        x, y, c = _place()
        for e in range(n):
            for j, chip_j in enumerate(_other_chips(x, y)[:2]):
                r = _block_rows(dst[e], R[e], 2 * chip_j[0] + chip_j[1], c)
                _remote(r, r, sm[0].at[4 * e + j], sm[1].at[4 * e + j], (*chip_j, c)).wait_recv()
                forward(dst, sm, e, j, chip_j, x, y, c, c).start()

    def finish(src, dst, sm):
        x, y, c = _place()
        near = _other_chips(x, y)[:2]
        for e in range(n):
            for j, chip_j in enumerate(near):
                forward(dst, sm, e, j, chip_j, x, y, c, 1 - c).wait_recv()
        for e in range(n):
            for j, chip_j in enumerate(near):
                ici(src, dst, sm, e, j, chip_j, x, y, c).wait_send()
                forward(dst, sm, e, j, chip_j, x, y, c, c).wait_send()
            local(src, dst, sm, e, x, y).wait()

    outs = [SDS((4 * s.shape[0], s.shape[1]), s.dtype) for s in shards]
    return Exchange(list(shards), outs, [DMA((4 * n,)), DMA((4 * n,)), DMA((n,))], start, finish, [(relay_early, relay)])


def gather_far_exchange(bufs, relay_early=0):
    n = len(bufs)
    R = [b.shape[0] // 4 for b in bufs]

    def send(src, dst, sm, e, j, x, y, c):
        to, of = _other_chips(x, y)[j], _other_chips(x, y)[1 - j]
        kk = 2 * of[0] + of[1]
        return _remote(_block_rows(src[e], R[e], kk, c, j), _block_rows(dst[e], R[e], kk, c, j),
                       sm[0].at[4 * e + j], sm[1].at[4 * e + j], (*to, c))

    def landed(dst, e, j, x, y, half):
        return _block_rows(dst[e], R[e], 2 * (1 - x) + (1 - y), half, j)

    def forward(dst, sm, e, j, x, y, c, sender_c):
        r = landed(dst, e, j, x, y, sender_c)
        return _remote(r, r, sm[0].at[4 * e + 2 + j], sm[1].at[4 * e + 2 + j], (x, y, 1 - c))

    def start(src, dst, sm):
        x, y, c = _place()
        for e in range(n):
            for j in range(2):
                send(src, dst, sm, e, j, x, y, c).start()

    def relay(src, dst, sm):
        x, y, c = _place()
        for e in range(n):
            for j in range(2):
                r = landed(dst, e, j, x, y, c)
                _remote(r, r, sm[0].at[4 * e + j], sm[1].at[4 * e + j], (*_other_chips(x, y)[j], c)).wait_recv()
                forward(dst, sm, e, j, x, y, c, c).start()

    def finish(src, dst, sm):
        x, y, c = _place()
        for e in range(n):
            for j in range(2):
                forward(dst, sm, e, j, x, y, c, 1 - c).wait_recv()
        for e in range(n):
            for j in range(2):
                send(src, dst, sm, e, j, x, y, c).wait_send()
                forward(dst, sm, e, j, x, y, c, c).wait_send()

    outs = [SDS(b.shape, b.dtype) for b in bufs]
    return Exchange(list(bufs), outs, [DMA((4 * n,)), DMA((4 * n,))], start, finish, [(relay_early, relay)],
                    {i: i for i in range(n)})


def gather_two_legs(shards):
    near = gather_near_exchange(shards)
    far = gather_far_exchange(near.outs)

    def finish(src, dst, sm):
        near.relays[0][1](src, dst, sm[:3])
        near.finish(src, dst, sm[:3])
        far.start(dst, dst, sm[3:])
        far.relays[0][1](dst, dst, sm[3:])
        far.finish(dst, dst, sm[3:])

    return Exchange(near.ins, near.outs, list(near.sems) + list(far.sems),
                    lambda src, dst, sm: near.start(src, dst, sm[:3]), finish)


def gather_two_legs_cast(shards):
    n = len(shards)
    base = gather_two_legs([SDS(s.shape, MXU) for s in shards])
    nb = len(base.sems)

    def fetch(src, sm, e):
        return pltpu.make_async_copy(src[e], sm[nb + 1 + e], sm[nb].at[e])

    def start(src, dst, sm):
        for e in range(n):
            fetch(src, sm, e).start()
        for e in range(n):
            fetch(src, sm, e).wait()
            sm[nb + 1 + n + e][...] = sm[nb + 1 + e][...].astype(MXU)
        base.start(sm[nb + 1 + n:], dst, sm[:nb])

    def finish(src, dst, sm):
        base.finish(sm[nb + 1 + n:], dst, sm[:nb])

    scratch = (list(base.sems) + [DMA((n,))] + [pltpu.VMEM(s.shape, s.dtype) for s in shards]
               + [pltpu.VMEM(s.shape, MXU) for s in shards])
    return Exchange(list(shards), base.outs, scratch, start, finish)


def scatter_exchange(parts, relay_before_end=None, want_issue=False):
    n = len(parts)
    by_entry = relay_before_end is not None
    relay_before_end = relay_before_end or [0] * n

    def ici(p, st, sm, e, j, chip_j, x, y, c):
        k, kj = 2 * x + y, 2 * chip_j[0] + chip_j[1]
        return _remote(p[e].at[kj], st[e].at[c, k], sm[0].at[8 * e + j], sm[1].at[8 * e + j], (*chip_j, c))

    def own(p, st, sm, e, x, y, c):
        k = 2 * x + y
        return _remote(p[e].at[k], st[e].at[c, k], sm[0].at[8 * e + 3], sm[1].at[8 * e + 3], (x, y, 1 - c))

    def forward(st, sm, e, j, chip_j, x, y, c, sender_c):
        kj = 2 * chip_j[0] + chip_j[1]
        r = st[e].at[sender_c, kj]
        return _remote(r, r, sm[0].at[8 * e + 4 + j], sm[1].at[8 * e + 4 + j], (x, y, 1 - c))

    def local(p, st, sm, e, x, y, c):
        k = 2 * x + y
        return pltpu.make_async_copy(p[e].at[k], st[e].at[c, k], sm[2].at[e])

    def issue(e, p, st, sm):
        x, y, c = _place()
        for j, chip_j in enumerate(_other_chips(x, y)):
            ici(p, st, sm, e, j, chip_j, x, y, c).start()
        local(p, st, sm, e, x, y, c).start()
        own(p, st, sm, e, x, y, c).start()

    def start(p, st, sm, before_slot=None):
        x, y, c = _place()
        if by_entry:
            for e in range(n):
                issue(e, p, st, sm)
            return
        for j, chip_j in enumerate(_other_chips(x, y)):
            if before_slot is not None:
                before_slot(j, 2 * chip_j[0] + chip_j[1])
            for e in range(n):
                ici(p, st, sm, e, j, chip_j, x, y, c).start()
        if before_slot is not None:
            before_slot(3, 2 * x + y)
        for e in range(n):
            local(p, st, sm, e, x, y, c).start()
            own(p, st, sm, e, x, y, c).start()

    def relay(e, p, st, sm):
        x, y, c = _place()
        for j, chip_j in enumerate(_other_chips(x, y)):
            kj = 2 * chip_j[0] + chip_j[1]
            r = st[e].at[c, kj]
            _remote(r, r, sm[0].at[8 * e + j], sm[1].at[8 * e + j], (*chip_j, c)).wait_recv()
            forward(st, sm, e, j, chip_j, x, y, c, c).start()

    def finish(p, st, sm):
        x, y, c = _place()
        k = 2 * x + y
        chips = _other_chips(x, y)
        for e in range(n):
            r = st[e].at[1 - c, k]
            _remote(r, r, sm[0].at[8 * e + 3], sm[1].at[8 * e + 3], (x, y, 1 - c)).wait_recv()
            for j, chip_j in enumerate(chips):
                forward(st, sm, e, j, chip_j, x, y, c, 1 - c).wait_recv()
        for e in range(n):
            own(p, st, sm, e, x, y, c).wait_send()
            for j, chip_j in enumerate(chips):
                ici(p, st, sm, e, j, chip_j, x, y, c).wait_send()
                forward(st, sm, e, j, chip_j, x, y, c, c).wait_send()
            local(p, st, sm, e, x, y, c).wait()

    outs = [SDS((2,) + a.shape, a.dtype) for a in parts]
    ex = Exchange(list(parts), outs, [DMA((8 * n,)), DMA((8 * n,)), DMA((n,))], start, finish,
                  [(relay_before_end[e], functools.partial(relay, e)) for e in range(n)])
    return (ex, issue) if want_issue else ex


def reduce_scatter_exchange(grads, nsteps, load_step, send_step, relay_step):
    n = len(grads)
    hrs = [g.shape[2] for g in grads]
    C = grads[0].shape[3]
    scatter, issue = scatter_exchange([SDS((4,) + g.shape[2:], WIRE) for g in grads], want_issue=True)
    hand_on = [fn for _, fn in scatter.relays]

    def refs(xs):
        return xs[:3], xs[3], xs[4], xs[5], xs[6], xs[7:7 + n], xs[7 + n:]

    def push(e, g, psem, qsem, sib_st):
        x, y, c = _place()
        return _remote(g[e].at[:, 1 - c], sib_st[e], psem.at[e], qsem.at[e], (x, y, 1 - c))

    def fetch(e, g, lsem, own_st):
        _, _, c = _place()
        return pltpu.make_async_copy(g[e].at[:, c], own_st.at[e % 2, :, pl.ds(0, hrs[e])], lsem.at[e])

    def start(g, xo, xs):
        _, _, psem, qsem, _, sib_st, _ = refs(xs)
        for e in range(n):
            push(e, g, psem, qsem, sib_st).start()

    def load(e, g, xo, xs):
        _, lsem, _, _, own_st, _, _ = refs(xs)
        fetch(e, g, lsem, own_st).start()

    def send(e, g, xo, xs):
        sm, lsem, psem, qsem, own_st, sib_st, part = refs(xs)
        fetch(e, g, lsem, own_st).wait()
        push(e, g, psem, qsem, sib_st).wait_recv()
        part[e][...] = (own_st[e % 2, :, 0:hrs[e]] + sib_st[e][...]).astype(WIRE)
        issue(e, part, xo, sm)

    def relay(e, g, xo, xs):
        sm, _, _, _, _, _, part = refs(xs)
        hand_on[e](part, xo, sm)

    def finish(g, xo, xs):
        sm, _, psem, qsem, _, sib_st, part = refs(xs)
        scatter.finish(part, xo, sm)
        for e in range(n):
            push(e, g, psem, qsem, sib_st).wait_send()

    plan = sorted([(min(step[e], nsteps - 1), phase, e) for phase, step in enumerate((load_step, send_step, relay_step))
                   for e in range(n)])
    stage = (load, send, relay)
    relays = [(nsteps - 1 - at, functools.partial(stage[phase], e)) for at, phase, e in plan]
    scratch = (list(scatter.sems) + [DMA((n,)), DMA((n,)), DMA((n,))] + [pltpu.VMEM((2, 4, max(hrs), C), f32)]
               + [pltpu.VMEM((4, hr, C), f32) for hr in hrs] + [pltpu.VMEM((4, hr, C), WIRE) for hr in hrs])
    return Exchange(list(grads), scatter.outs, scratch, start, finish, relays)


def tail_reduce(d_norm_mix, d_norm_mem, d_norm_ffn, d_gains, d_cw8, d_cbias, d_qg, d_kg, d_mqg, d_mkg, d_sink8, loss8, tail):
    n = len(tail)
    scatter = scatter_exchange([SDS((4,) + a.shape[2:], WIRE) for a in tail])

    def half_copy(g, sib, hsem, e, j, slot, x, y, c):
        return _remote(g[e].at[slot, 1 - c], sib[e].at[slot], hsem[0].at[4 * e + j], hsem[1].at[4 * e + j], (x, y, 1 - c))

    def body(nm_ref, nmem_ref, nf_ref, gn_ref, cw_ref, cb_ref, qg_ref, kg_ref, mqg_ref, mkg_ref, sk_ref, ls_ref, *rest):
        g, o_ref, st = rest[:n], rest[n], rest[n + 1:2 * n + 1]
        buf, ssem, rsem = rest[2 * n + 1:2 * n + 4]
        own, sib, part = (rest[2 * n + 4 + i * n:2 * n + 4 + (i + 1) * n] for i in range(3))
        lsem = rest[5 * n + 4]
        hsem, xsem = rest[5 * n + 5:5 * n + 7], rest[5 * n + 7:]
        x, y, c = _place()
        loads = [pltpu.make_async_copy(g[e].at[:, c], own[e], lsem.at[e]) for e in range(n)]
        for ld in loads:
            ld.start()
        for j, slot in enumerate([2 * cx + cy for cx, cy in _other_chips(x, y)] + [2 * x + y]):
            for e in range(n):
                half_copy(g, sib, hsem, e, j, slot, x, y, c).start()
        me = 4 * x + 2 * y + c
        mine = buf.at[me]
        mine[...] = jnp.zeros((8, 1024), f32)
        mine[0:1, :] = nm_ref[...]
        mine[1:2, :] = nmem_ref[...]
        mine[2:3, :] = nf_ref[...]
        mine[3:4, :] = gn_ref[...]
        for j in range(3):
            mine[4:5, pl.ds(j * CONV_W, CONV_W)] = cw_ref[j:j + 1, :]
        mine[4:5, pl.ds(3 * CONV_W, CONV_W)] = cb_ref[...]
        for j, r in enumerate((qg_ref, kg_ref, mqg_ref, mkg_ref)):
            mine[5:6, pl.ds(j * HD, HD)] = r[...]
        mine[5:6, pl.ds(256, 128)] = sk_ref[0:1, :]
        mine[5:6, pl.ds(384, 128)] = ls_ref[0:1, :]

        def peer_of(m):
            return (1 - x if m & 4 else x, 1 - y if m & 2 else y, 1 - c if m & 1 else c)

        for m in range(1, 8):
            _remote(mine, mine, ssem.at[m - 1], rsem.at[m - 1], peer_of(m)).start()
        for ld in loads:
            ld.wait()

        def chip_partial(j, slot):
            for e in range(n):
                half_copy(g, sib, hsem, e, j, slot, x, y, c).wait()
                part[e][slot] = (own[e][slot] + sib[e][slot]).astype(WIRE)

        scatter.start(part, st, xsem, chip_partial)
        for _, hand_on in scatter.relays:
            hand_on(part, st, xsem)
        scatter.finish(part, st, xsem)
        for m in range(1, 8):
            p = peer_of(m)
            got = buf.at[4 * p[0] + 2 * p[1] + p[2]]
            _remote(got, got, ssem.at[m - 1], rsem.at[m - 1], p).wait_recv()
        for m in range(1, 8):
            _remote(mine, mine, ssem.at[m - 1], rsem.at[m - 1], peer_of(m)).wait_send()
        acc = buf[0]
        for d in range(1, 8):
            acc = acc + buf[d]
        o_ref[...] = acc

    ins = [d_norm_mix, d_norm_mem, d_norm_ffn, d_gains, d_cw8, d_cbias, d_qg, d_kg, d_mqg, d_mkg, d_sink8, loss8]
    half_shape = [(4,) + a.shape[2:] for a in tail]
    scratch = ([pltpu.VMEM((8, 8, 1024), f32), DMA((7,)), DMA((7,))]
               + [pltpu.VMEM(s, f32) for s in half_shape] * 2 + [pltpu.VMEM(s, WIRE) for s in half_shape]
               + [DMA((n,)), DMA((4 * n,)), DMA((4 * n,))] + list(scatter.sems))
    res = _run("tail_reduce", body, (), ins + list(tail), [VM] * len(ins) + [ANY] * n,
               [SDS((8, 1024), f32)] + list(scatter.outs), [VM] + [ANY] * n, scratch=scratch, vmem_mib=40)
    return res[0], res[1:]


def _adamw_math(w, g, m, v):
    m = ADAM_B1 * m + (1.0 - ADAM_B1) * g
    v = ADAM_B2 * v + (1.0 - ADAM_B2) * (g * g)
    m_hat = m / (1.0 - ADAM_B1 ** ADAM_STEP)
    v_hat = v / (1.0 - ADAM_B2 ** ADAM_STEP)
    delta = -ADAM_LR * (m_hat / (jnp.sqrt(v_hat) + ADAM_EPS) + ADAM_WD * w)
    return delta, m, v


def _sum_chips(st):
    return ((st[0].astype(f32) + st[1].astype(f32)) + st[2].astype(f32)) + st[3].astype(f32)


def adamw_big(name, stages, ws, ms, vs, nstep, exchange=None):
    n = len(stages)

    def body(*refs):
        st, w, m, v = refs[:n], refs[n:2 * n], refs[2 * n:3 * n], refs[3 * n:4 * n]
        outs = refs[4 * n:]
        for e in range(n):
            g = jnp.concatenate([_sum_chips(st[e].at[0]), _sum_chips(st[e].at[1])], axis=0)
            d, mm, vv = _adamw_math(w[e][...], g, m[e][...], v[e][...])
            outs[4 * e][...] = g
            outs[4 * e + 1][...] = d
            outs[4 * e + 2][...] = mm
            outs[4 * e + 3][...] = vv

    st_specs, w_specs = [], []
    for e in range(n):
        _, _, hr, C = stages[e].shape
        st_specs.append(pl.BlockSpec((2, 4, hr, C // nstep), lambda i: (0, 0, 0, i)))
        w_specs.append(pl.BlockSpec((2 * hr, C // nstep), lambda i: (0, i)))
    out_specs = [s for s in w_specs for _ in range(4)]
    out_shape = [SDS(w.shape, f32) for w in ws for _ in range(4)]
    res = _run(name, body, (nstep,), list(stages) + list(ws) + list(ms) + list(vs), st_specs + w_specs * 3,
               out_shape, out_specs, vmem_mib=16, exchange=exchange)
    res, sent = res if exchange is not None else (res, None)
    return [res[4 * e:4 * e + 4] for e in range(n)], sent


def adamw_small(tot, pk_w, pk_m, pk_v, shapes):
    def body(tot_ref, w_ref, m_ref, v_ref, *outs):
        x, y, _ = _place()
        chip = 2 * x + y
        taps = []
        for j in range(3):
            mine = tot_ref[4:5, j * CONV_W:j * CONV_W + HD]
            for s in range(1, 4):
                mine = jnp.where(chip == s, tot_ref[4:5, j * CONV_W + s * HD:j * CONV_W + (s + 1) * HD], mine)
            taps.append(mine)
        row4 = jnp.concatenate(taps + [jnp.zeros((1, 3 * CONV_W - 3 * HD), f32), tot_ref[4:5, 3 * CONV_W:]], axis=1)
        tot_v = tot_ref[...]
        row = lax.broadcasted_iota(jnp.int32, tot_v.shape, 0)
        g = jnp.where(row == 4, jnp.broadcast_to(row4, tot_v.shape), tot_v)
        d, mm, vv = _adamw_math(w_ref[...], g, m_ref[...], v_ref[...])
        for i, name in enumerate(SMALL):
            for k, val in enumerate((g, d, mm, vv)):
                if name == "conv_w":
                    outs[4 * i + k][...] = jnp.concatenate([val[4:5, j * HD:(j + 1) * HD] for j in range(3)], axis=0)[None]
                else:
                    r, c0, w = SMALL_AT[name]
                    outs[4 * i + k][...] = val[r:r + 1, c0:c0 + w]

    out_shape = [SDS(shapes[k], f32) for k in SMALL for _ in range(4)]
    res = _run("adamw_small", body, (), [tot, pk_w, pk_m, pk_v], [VM] * 4, out_shape, [VM] * len(out_shape))
    return {k: res[4 * i:4 * i + 4] for i, k in enumerate(SMALL)}


def prep_weights(name, shards, exchange=None):
    n = len(shards)

    def body(*refs):
        for e in range(n):
            refs[n + e][...] = _c(refs[e][...])

    return _run(name, body, (), shards, [VM] * n, [SDS(a.shape, MXU) for a in shards], [VM] * n, vmem_mib=16, exchange=exchange)


def mem_kv_fwd(mem2d, pk, wmkv):
    M, D = mem2d.shape

    def body(m_ref, pk_ref, w_ref, mn_ref, kv_ref, km_ref, vm_ref):
        m = m_ref[...]
        mn = _c(m * _rstd(m) * _small(pk_ref, "norm_mem"))
        mn_ref[...] = mn
        kv = _nn(mn, w_ref[...])
        kv_ref[...] = kv
        kk = kv[:, :MEM_W]
        km_ref[...] = _c(kk * _heads_rstd(kk) * _lanes(_small(pk_ref, "mem_k_norm"), MEM_W))
        vm_ref[...] = _c(kv[:, MEM_W:])

    return _run("mem_kv_fwd", body, (), [mem2d, pk, wmkv], [VM] * 3,
                [SDS((M, D), MXU), SDS((M, 2 * MEM_W), f32), SDS((M, MEM_W), MXU), SDS((M, MEM_W), MXU)], [VM] * 4)


QKV_W = ATT_W + 2 * KV_W + MEM_W


def in_proj_fwd(x2d, pk, winT, tm, exchange):
    T, D = x2d.shape
    P = winT.shape[0]

    def body(x_ref, pk_ref, w_ref, xn_ref, proj_ref, qkv_ref):
        xv = x_ref[...]
        xn = _c(xv * _rstd(xv) * _small(pk_ref, "norm_mix"))
        xn_ref[...] = xn
        proj = _nt(xn, w_ref[...])
        proj_ref[...] = proj
        q, k = proj[:, :ATT_W], proj[:, ATT_W:ATT_W + KV_W]
        qm = proj[:, P - MEM_W:]
        qkv_ref[...] = jnp.concatenate(
            [_c(q * _heads_rstd(q) * _lanes(_small(pk_ref, "q_norm"), ATT_W)),
             _c(k * _heads_rstd(k) * _lanes(_small(pk_ref, "k_norm"), KV_W)),
             _c(proj[:, ATT_W + KV_W:ATT_W + 2 * KV_W]),
             _c(qm * _heads_rstd(qm) * _lanes(_small(pk_ref, "mem_q_norm"), MEM_W))], axis=1)

    return _run("in_proj_fwd", body, (T // tm,), [x2d, pk, winT],
                [pl.BlockSpec((tm, D), lambda i: (i, 0)), VM, VM],
                [SDS((T, D), MXU), SDS((T, P), f32), SDS((T, QKV_W), MXU)],
                [pl.BlockSpec((tm, D), lambda i: (i, 0)), pl.BlockSpec((tm, P), lambda i: (i, 0)),
                 pl.BlockSpec((tm, QKV_W), lambda i: (i, 0))],
                vmem_mib=40, exchange=exchange)


def _swa_bias_table():
    r = np.arange(GQA * BLK)[:, None]
    k = np.arange(2 * BLK)[None, :]
    dist = (r % BLK) + BLK - k
    band = (dist >= 0) & (dist < BLK)
    tab = np.empty((2, N_KV, GQA * BLK, 2 * BLK), np.float32)
    for later in range(2):
        valid = band & ((k >= BLK) | (later == 1))
        for g in range(N_KV):
            slope = 2.0 ** -(g * GQA + r // BLK + 1.0)
            tab[later, g] = np.where(valid, -slope * dist, NEG)
    return jnp.asarray(tab)


def _sink_column(g, sk_ref):
    hrow = lax.broadcasted_iota(jnp.int32, (GQA * BLK, 1), 0) // BLK
    sink = jnp.zeros((GQA * BLK, 1), f32)
    for hh in range(GQA):
        sink = jnp.where(hrow == hh, sk_ref[g * GQA + hh:g * GQA + hh + 1, 0:1], sink)
    return sink


def _stack_heads(v, g):
    return jnp.concatenate([v[:, (g * GQA + hh) * HD:(g * GQA + hh + 1) * HD] for hh in range(GQA)], axis=0)


def attn_fwd(qkv, sink_rows, BL, S, exchange, qb=2):
    NS = S // (qb * BLK)
    T = BL * S

    def body(q_ref, kc_ref, kp_ref, vc_ref, vp_ref, sk_ref, tab_ref, o_ref):
        j = pl.program_id(1)
        kall = jnp.concatenate([kp_ref[...], kc_ref[...]], axis=0)
        vall = jnp.concatenate([vp_ref[...], vc_ref[...]], axis=0)
        ones = jnp.ones((2 * BLK, HD), MXU)
        for b in range(qb):
            q = q_ref[pl.ds(b * BLK, BLK), :]
            k2, v2 = kall[b * BLK:(b + 2) * BLK], vall[b * BLK:(b + 2) * BLK]
            later = jnp.minimum(j, 1) if b == 0 else 1
            for g in range(N_KV):
                kn, vh = k2[:, g * HD:(g + 1) * HD], v2[:, g * HD:(g + 1) * HD]
                s = _nt(_stack_heads(q, g), kn) * (HD ** -0.5) + tab_ref[later, g]
                e, es = _exp_scores(s, _sink_column(g, sk_ref))
                eb = _c(e)
                o = _nn(eb, vh) * (1.0 / (_nn(eb, ones) + es))
                for hh in range(GQA):
                    o_ref[pl.ds(b * BLK, BLK), pl.ds((g * GQA + hh) * HD, HD)] = o[hh * BLK:(hh + 1) * BLK]

    cur = lambda col: (lambda b, j: (b * NS + j, col))
    prev = lambda col: (lambda b, j: (qb * (b * NS + j) - jnp.minimum(j, 1), col))
    return _run("attn_fwd", body, (BL, NS), [qkv, qkv, qkv, qkv, qkv, sink_rows, _swa_bias_table()],
                [pl.BlockSpec((qb * BLK, ATT_W), cur(0)),
                 pl.BlockSpec((qb * BLK, KV_W), cur(4)), pl.BlockSpec((BLK, KV_W), prev(4)),
                 pl.BlockSpec((qb * BLK, KV_W), cur(5)), pl.BlockSpec((BLK, KV_W), prev(5)),
                 pl.BlockSpec((8, 128), lambda b, j: (0, 0)), VM],
                [SDS((T, ATT_W), f32)], [pl.BlockSpec((qb * BLK, ATT_W), cur(0))], exchange=exchange)


def _conv_taps(u, uh):
    row = lax.broadcasted_iota(jnp.int32, u.shape, 0)
    u1 = jnp.where(row == 0, uh[7:8, :], pltpu.roll(u, 1, 0))
    u2 = jnp.where(row == 0, uh[6:7, :], jnp.where(row == 1, uh[7:8, :], pltpu.roll(u, 2, 0)))
    return u1, u2


def _mem_head(qm, km, vm, h):
    qh, kh, vh = (a[:, h * HD:(h + 1) * HD] for a in (qm, km, vm))
    e, _ = _exp_scores(_nt(qh, kh) * (HD ** -0.5))
    return qh, kh, vh, e


def mixer_tail_fwd(x2d, attn_out, proj, qkv, km, vm, conv_w8, pk, wout, S, tm, exchange):
    T, D = x2d.shape
    NM = km.shape[0] // (T // S)

    def body(x_ref, ao_ref, ch_ref, cb_ref, cc_ref, chh_ref, cch_ref, qm_ref, km_ref, vm_ref, cw_ref, pk_ref,
             wout_ref, co_ref, mo_ref, mg_ref, x1_ref, h_ref):
        first = (pl.program_id(0) * tm) % S == 0
        u = cc_ref[...] * ch_ref[...]
        uh = jnp.where(first, 0.0, cch_ref[...] * chh_ref[...])
        u1, u2 = _conv_taps(u, uh)
        conv = cw_ref[0:1, :] * u2 + cw_ref[1:2, :] * u1 + cw_ref[2:3, :] * u + _small(pk_ref, "conv_b")
        conv_out = cb_ref[...] * conv
        co_ref[...] = conv_out
        qm, kmv, vmv = qm_ref[...], km_ref[...], vm_ref[...]
        ones = jnp.ones((NM, HD), MXU)
        for h in range(N_MEMH):
            _, _, vh, e = _mem_head(qm, kmv, vmv, h)
            eb = _c(e)
            mo_ref[:, pl.ds(h * HD, HD)] = _nn(eb, vh) * (1.0 / _nn(eb, ones))
        mem_out = mo_ref[...]
        ao = ao_ref[...]
        merged = _c(jnp.concatenate([ao * _rstd(ao) * _small(pk_ref, "out_norm_attn"),
                                     conv_out * _rstd(conv_out) * _small(pk_ref, "out_norm_conv"),
                                     mem_out * _rstd(mem_out) * _small(pk_ref, "out_norm_mem")], axis=1))
        mg_ref[...] = merged
        x1 = x_ref[...] + _nn(merged, wout_ref[...])
        x1_ref[...] = x1
        h_ref[...] = _c(x1 * _rstd(x1) * _small(pk_ref, "norm_ffn"))

    tile = lambda w, col: pl.BlockSpec((tm, w), lambda i: (i, col))
    halo = lambda col: pl.BlockSpec((8, CONV_W), lambda i: (jnp.maximum(i * (tm // 8) - 1, 0), col))
    seq = pl.BlockSpec((NM, MEM_W), lambda i: ((i * tm) // S, 0))
    small = lambda a: pl.BlockSpec(a.shape, lambda i: (0, 0))
    return _run("mixer_tail_fwd", body, (T // tm,),
                [x2d, attn_out, proj, proj, proj, proj, proj, qkv, km, vm, conv_w8, pk, wout],
                [tile(D, 0), tile(ATT_W, 0), tile(CONV_W, 3), tile(CONV_W, 4), tile(CONV_W, 5), halo(3), halo(5),
                 tile(MEM_W, 3), seq, seq, VM, VM, VM],
                [SDS((T, CONV_W), f32), SDS((T, MEM_W), f32), SDS((T, D), MXU), SDS((T, D), f32), SDS((T, D), MXU)],
                [tile(CONV_W, 0), tile(MEM_W, 0), tile(D, 0), tile(D, 0), tile(D, 0)], vmem_mib=40, exchange=exchange)


def ffn_fwd_bwd(h, x1, tgt, wgT, wuT, wd, pk, tm):
    T, D = x1.shape
    F = wd.shape[0]

    def body(h_ref, x1_ref, t_ref, wg_ref, wu_ref, wd_ref, pk_ref,
             dx1_ref, dx2_ref, act_ref, dg_ref, du_ref, loss_ref, dgf_ref):
        @pl.when(pl.program_id(0) == 0)
        def _():
            loss_ref[...] = jnp.zeros_like(loss_ref)
            dgf_ref[...] = jnp.zeros_like(dgf_ref)

        hv = h_ref[...]
        gate = _nt(hv, wg_ref[...])
        up = _nt(hv, wu_ref[...])
        sg = jax.nn.sigmoid(gate)
        sl = gate * sg
        act = _c(sl * up)
        act_ref[...] = act
        x1v = x1_ref[...]
        diff = (x1v + _nn(act, wd_ref[...])) - t_ref[...]
        loss_ref[...] += 0.5 * jnp.sum(jnp.sum(diff * diff, axis=-1, keepdims=True) / D, axis=0, keepdims=True)
        dx2 = diff / D
        dx2b = _c(dx2)
        dx2_ref[...] = dx2b
        d_act = _nt(dx2b, wd_ref[...])
        d_up = _c(d_act * sl)
        d_gate = _c(d_act * up * (sg * (1.0 + gate * (1.0 - sg))))
        du_ref[...] = d_up
        dg_ref[...] = d_gate
        dh = _nn(d_gate, wg_ref[...]) + _nn(d_up, wu_ref[...])
        dv, dgf = _norm_bwd(dh, x1v, _rstd(x1v), _small(pk_ref, "norm_ffn"))
        dx1_ref[...] = dx2 + dv
        dgf_ref[...] += dgf

    tile = lambda w: pl.BlockSpec((tm, w), lambda i: (i, 0))
    return _run("ffn_fwd_bwd", body, (T // tm,), [h, x1, tgt, wgT, wuT, wd, pk],
                [tile(D), tile(D), tile(D), VM, VM, VM, VM],
                [SDS((T, D), f32), SDS((T, D), MXU), SDS((T, F), MXU), SDS((T, F), MXU), SDS((T, F), MXU),
                 SDS((8, 128), f32), SDS((1, D), f32)],
                [tile(D), tile(D), tile(F), tile(F), tile(F), pl.BlockSpec((8, 128), lambda i: (0, 0)),
                 pl.BlockSpec((1, D), lambda i: (0, 0))], vmem_mib=56)


def matmul_tn(a, b, name, tmo, tk):
    T, M = a.shape
    N = b.shape[1]

    def body(a_ref, b_ref, o_ref):
        @pl.when(pl.program_id(1) == 0)
        def _():
            o_ref[...] = jnp.zeros_like(o_ref)

        o_ref[...] += _tn(a_ref[...], b_ref[...])

    return _run(name, body, (M // tmo, T // tk), [a, b],
                [pl.BlockSpec((tk, tmo), lambda m, k: (k, m)), pl.BlockSpec((tk, N), lambda m, k: (k, 0))],
                [SDS((M, N), f32)], [pl.BlockSpec((tmo, N), lambda m, k: (m, 0))], vmem_mib=48)[0]


def out_proj_bwd(dx1, merged, attn_out, conv_out, mem_out, pk, wout, tm):
    T, D = dx1.shape

    def body(dx1_ref, mg_ref, ao_ref, co_ref, mo_ref, pk_ref, w_ref,
             dao_ref, dco_ref, dmo_ref, dw_ref, dgain_ref):
        @pl.when(pl.program_id(0) == 0)
        def _():
            dw_ref[...] = jnp.zeros_like(dw_ref)
            dgain_ref[...] = jnp.zeros_like(dgain_ref)

        dxb = _c(dx1_ref[...])
        dw_ref[...] += _tn(mg_ref[...], dxb)
        dmg = _nt(dxb, w_ref[...])
        ao, co, mo = ao_ref[...], co_ref[...], mo_ref[...]
        da, ga = _norm_bwd(dmg[:, :ATT_W], ao, _rstd(ao), _small(pk_ref, "out_norm_attn"))
        dc, gc = _norm_bwd(dmg[:, ATT_W:ATT_W + CONV_W], co, _rstd(co), _small(pk_ref, "out_norm_conv"))
        dm, gm = _norm_bwd(dmg[:, ATT_W + CONV_W:], mo, _rstd(mo), _small(pk_ref, "out_norm_mem"))
        dao_ref[...] = da
        dco_ref[...] = dc
        dmo_ref[...] = dm
        dgain_ref[...] += jnp.concatenate([ga, gc, gm], axis=1)

    tile = lambda w: pl.BlockSpec((tm, w), lambda i: (i, 0))
    return _run("out_proj_bwd", body, (T // tm,), [dx1, merged, attn_out, conv_out, mem_out, pk, wout],
                [tile(D), tile(D), tile(ATT_W), tile(CONV_W), tile(MEM_W), VM, VM],
                [SDS((T, ATT_W), f32), SDS((T, CONV_W), f32), SDS((T, MEM_W), f32), SDS((D, D), f32), SDS((1, D), f32)],
                [tile(ATT_W), tile(CONV_W), tile(MEM_W), pl.BlockSpec((D, D), lambda i: (0, 0)),
                 pl.BlockSpec((1, D), lambda i: (0, 0))], vmem_mib=40)


def attn_bwd(qkv, d_attn, attn_out, sink_rows, BL, S, exchange):
    NB = S // BLK
    T = BL * S

    def body(q_ref, kc_ref, kp_ref, vc_ref, vp_ref, do_ref, ao_ref, sk_ref, tab_ref,
             dq_ref, dk_ref, dv_ref, dsk_ref, pend_k, pend_v):
        b, j = pl.program_id(0), pl.program_id(1)

        @pl.when((b == 0) & (j == 0))
        def _():
            dsk_ref[...] = jnp.zeros_like(dsk_ref)

        @pl.when(j == 0)
        def _():
            pend_k[...] = jnp.zeros_like(pend_k)
            pend_v[...] = jnp.zeros_like(pend_v)

        @pl.when(j < NB)
        def _():
            q, do, ao = q_ref[...], do_ref[...], ao_ref[...]
            k2 = jnp.concatenate([kp_ref[...], kc_ref[...]], axis=0)
            v2 = jnp.concatenate([vp_ref[...], vc_ref[...]], axis=0)
            lane = lax.broadcasted_iota(jnp.int32, (8, 128), 1)
            ones_w = jnp.ones((2 * BLK, 2 * BLK), MXU)
            dsk = jnp.zeros((8, 128), f32)
            dks, dvs = [], []
            for g in range(N_KV):
                kn, vh = k2[:, g * HD:(g + 1) * HD], v2[:, g * HD:(g + 1) * HD]
                qs = _stack_heads(q, g)
                s = _nt(qs, kn) * (HD ** -0.5) + tab_ref[g]
                e, es = _exp_scores(s, _sink_column(g, sk_ref))
                eb = _c(e)
                inv_w = 1.0 / (_nn(eb, ones_w) + es)
                inv_n = inv_w[:, :HD]
                dos = _stack_heads(do, g)
                delta = _rowsum_mxu(dos * _stack_heads(ao, g), 2 * BLK)
                dp = _nt(_c(dos), vh)
                ds = _c(e * inv_w * (dp - delta) * (HD ** -0.5))
                t = es * inv_n[:, 0:1] * delta[:, 0:1]
                for hh in range(GQA):
                    dsk = dsk + jnp.where(lane == g * GQA + hh, -jnp.sum(t[hh * BLK:(hh + 1) * BLK]), 0.0)
                dvs.append(_tn(eb, _c(dos * inv_n)))
                dks.append(_tn(ds, qs))
                dqs = _nn(ds, kn)
                for hh in range(GQA):
                    dq_ref[:, pl.ds((g * GQA + hh) * HD, HD)] = dqs[hh * BLK:(hh + 1) * BLK]
            dk2 = jnp.concatenate(dks, axis=1)
            dv2 = jnp.concatenate(dvs, axis=1)
            dk_ref[...] = pend_k[...] + dk2[:BLK]
            dv_ref[...] = pend_v[...] + dv2[:BLK]
            pend_k[...] = dk2[BLK:]
            pend_v[...] = dv2[BLK:]
            dsk_ref[...] += dsk

        @pl.when(j == NB)
        def _():
            dk_ref[...] = pend_k[...]
            dv_ref[...] = pend_v[...]

    cur = lambda col: (lambda b, j: (b * NB + jnp.minimum(j, NB - 1), col))
    prev = lambda col: (lambda b, j: (b * NB + jnp.maximum(j - 1, 0), col))
    small = lambda shape: pl.BlockSpec(shape, lambda b, j: (0, 0))
    return _run("attn_bwd", body, (BL, NB + 1), [qkv, qkv, qkv, qkv, qkv, d_attn, attn_out, sink_rows, _swa_bias_table()],
                [pl.BlockSpec((BLK, ATT_W), cur(0)),
                 pl.BlockSpec((BLK, KV_W), cur(4)), pl.BlockSpec((BLK, KV_W), prev(4)),
                 pl.BlockSpec((BLK, KV_W), cur(5)), pl.BlockSpec((BLK, KV_W), prev(5)),
                 pl.BlockSpec((BLK, ATT_W), cur(0)), pl.BlockSpec((BLK, ATT_W), cur(0)), small((8, 128)),
                 pl.BlockSpec((None, N_KV, GQA * BLK, 2 * BLK), lambda b, j: (jnp.minimum(j, 1), 0, 0, 0))],
                [SDS((T, ATT_W), f32), SDS((T, KV_W), f32), SDS((T, KV_W), f32), SDS((8, 128), f32)],
                [pl.BlockSpec((BLK, ATT_W), cur(0)), pl.BlockSpec((BLK, KV_W), prev(0)),
                 pl.BlockSpec((BLK, KV_W), prev(0)), small((8, 128))],
                scratch=[pltpu.VMEM((BLK, KV_W), f32)] * 2, vmem_mib=56, exchange=exchange)


def mem_conv_bwd(d_mem_out, mem_out, d_conv_out, proj, qkv, km, vm, conv_w8, pk, S, tm, exchange):
    T = d_mem_out.shape[0]
    NM = km.shape[0] // (T // S)

    def body(dmo_ref, mo_ref, dco_ref, ch_ref, cb_ref, cc_ref, chh_ref, cch_ref, qm_ref, km_ref, vm_ref, cw_ref,
             pk_ref, dqm_ref, dkm_ref, dvm_ref, dcb_ref, dcv_ref, dcw_ref, dcbias_ref):
        i = pl.program_id(0)
        first = (i * tm) % S == 0

        @pl.when(i == 0)
        def _():
            dcw_ref[...] = jnp.zeros_like(dcw_ref)
            dcbias_ref[...] = jnp.zeros_like(dcbias_ref)

        @pl.when(first)
        def _():
            dkm_ref[...] = jnp.zeros_like(dkm_ref)
            dvm_ref[...] = jnp.zeros_like(dvm_ref)

        qm, kmv, vmv, dmo, mo = qm_ref[...], km_ref[...], vm_ref[...], dmo_ref[...], mo_ref[...]
        ones_w = jnp.ones((NM, NM), MXU)
        for h in range(N_MEMH):
            qh, kh, vh, e = _mem_head(qm, kmv, vmv, h)
            eb = _c(e)
            doh = dmo[:, h * HD:(h + 1) * HD]
            delta = _rowsum_mxu(doh * mo[:, h * HD:(h + 1) * HD], NM)
            dp = _nt(_c(doh), vh)
            inv_w = 1.0 / _nn(eb, ones_w)
            ds = _c(e * inv_w * (dp - delta) * (HD ** -0.5))
            dvm_ref[:, pl.ds(h * HD, HD)] += _tn(eb, _c(doh * inv_w[:, :HD]))
            dkm_ref[:, pl.ds(h * HD, HD)] += _tn(ds, qh)
            dqm_ref[:, pl.ds(h * HD, HD)] = _nn(ds, kh)

        u = cc_ref[...] * ch_ref[...]
        uh = jnp.where(first, 0.0, cch_ref[...] * chh_ref[...])
        u1, u2 = _conv_taps(u, uh)
        conv = cw_ref[0:1, :] * u2 + cw_ref[1:2, :] * u1 + cw_ref[2:3, :] * u + _small(pk_ref, "conv_b")
        dy = dco_ref[...]
        dcb_ref[...] = dy * conv
        dcv = dy * cb_ref[...]
        dcv_ref[...] = dcv
        dcbias_ref[...] += jnp.sum(dcv, axis=0, keepdims=True)
        dcw_ref[0:1, :] += jnp.sum(dcv * u2, axis=0, keepdims=True)
        dcw_ref[1:2, :] += jnp.sum(dcv * u1, axis=0, keepdims=True)
        dcw_ref[2:3, :] += jnp.sum(dcv * u, axis=0, keepdims=True)

    tile = lambda w, col: pl.BlockSpec((tm, w), lambda i: (i, col))
    halo = lambda col: pl.BlockSpec((8, CONV_W), lambda i: (jnp.maximum(i * (tm // 8) - 1, 0), col))
    seq = pl.BlockSpec((NM, MEM_W), lambda i: ((i * tm) // S, 0))
    const = lambda shape: pl.BlockSpec(shape, lambda i: (0, 0))
    return _run("mem_conv_bwd", body, (T // tm,),
                [d_mem_out, mem_out, d_conv_out, proj, proj, proj, proj, proj, qkv, km, vm, conv_w8, pk],
                [tile(MEM_W, 0), tile(MEM_W, 0), tile(CONV_W, 0), tile(CONV_W, 3), tile(CONV_W, 4), tile(CONV_W, 5),
                 halo(3), halo(5), tile(MEM_W, 3), seq, seq, VM, VM],
                [SDS((T, MEM_W), f32), SDS(km.shape, f32), SDS(km.shape, f32),
                 SDS((T, CONV_W), f32), SDS((T, CONV_W), f32), SDS((8, CONV_W), f32), SDS((1, CONV_W), f32)],
                [tile(MEM_W, 0), seq, seq, tile(CONV_W, 0), tile(CONV_W, 0), const((8, CONV_W)), const((1, CONV_W))],
                vmem_mib=48, exchange=exchange)


def in_proj_bwd(dqn, dkn, dv, dcb, dcv, dqmn, proj, conv_w8, xn, x2d, dx1, pk, winT, S, tm, stages, ws, ms, vs):
    T, D = x2d.shape
    P = winT.shape[0]
    last_blk = T // 8 - 1
    n = len(stages)
    nsteps = T // tm
    tile_w = ws[0].shape[1] // (nsteps // 2)
    turn = [e * 2 // n for e in range(n)]

    def body(dq_ref, dk_ref, dv_ref, dcb_ref, dcv_ref, dcvn_ref, dqm_ref, qa_ref, ka_ref, ch_ref, cc_ref, qma_ref,
             cw_ref, xn_ref, x_ref, dx1_ref, pk_ref, w_ref, *rest):
        st, aw, am, av = (rest[k * n:(k + 1) * n] for k in range(4))
        dx_ref, dw_ref, dg_ref, dqg_ref, dkg_ref, dmqg_ref = rest[4 * n:4 * n + 6]
        aouts = rest[4 * n + 6:]
        i = pl.program_id(0)

        for parity in range(2):
            @pl.when(i % 2 == parity)
            def _(parity=parity):
                for e in range(n):
                    if turn[e] == parity:
                        g = jnp.concatenate([_sum_chips(st[e].at[0]), _sum_chips(st[e].at[1])], axis=0)
                        d, mm, vv = _adamw_math(aw[e][...], g, am[e][...], av[e][...])
                        for k, val in enumerate((g, d, mm, vv)):
                            aouts[4 * e + k][...] = val

        @pl.when(i == 0)
        def _():
            dw_ref[...] = jnp.zeros_like(dw_ref)
            dg_ref[...] = jnp.zeros_like(dg_ref)
            dqg_ref[...] = jnp.zeros_like(dqg_ref)
            dkg_ref[...] = jnp.zeros_like(dkg_ref)
            dmqg_ref[...] = jnp.zeros_like(dmqg_ref)

        dqa, gq = _heads_norm_bwd(dq_ref[...], qa_ref[...], _small(pk_ref, "q_norm"))
        dka, gk = _heads_norm_bwd(dk_ref[...], ka_ref[...], _small(pk_ref, "k_norm"))
        dqma, gmq = _heads_norm_bwd(dqm_ref[...], qma_ref[...], _small(pk_ref, "mem_q_norm"))
        dqg_ref[...] += gq
        dkg_ref[...] += gk
        dmqg_ref[...] += gmq

        last = ((i + 1) * tm) % S == 0
        dcv = dcv_ref[...]
        nxt = jnp.where(last, 0.0, dcvn_ref[...])
        row = lax.broadcasted_iota(jnp.int32, dcv.shape, 0)
        n1 = jnp.where(row == tm - 1, nxt[0:1, :], pltpu.roll(dcv, tm - 1, 0))
        n2 = jnp.where(row == tm - 2, nxt[0:1, :], jnp.where(row == tm - 1, nxt[1:2, :], pltpu.roll(dcv, tm - 2, 0)))
        du = cw_ref[2:3, :] * dcv + cw_ref[1:2, :] * n1 + cw_ref[0:1, :] * n2
        d_proj = jnp.concatenate([_c(dqa), _c(dka), _c(dv_ref[...]), _c(du * cc_ref[...]),
                                  _c(dcb_ref[...]), _c(du * ch_ref[...]), _c(dqma)], axis=1)
        dw_ref[...] += _tn(d_proj, xn_ref[...])
        xv = x_ref[...]
        dv_, dg = _norm_bwd(_nn(d_proj, w_ref[...]), xv, _rstd(xv), _small(pk_ref, "norm_mix"))
        dx_ref[...] = dx1_ref[...] + dv_
        dg_ref[...] += dg

    tile = lambda w, col=0: pl.BlockSpec((tm, w), lambda i: (i, col))
    nhalo = pl.BlockSpec((8, CONV_W), lambda i: (jnp.minimum((i + 1) * (tm // 8), last_blk), 0))
    const = lambda shape: pl.BlockSpec(shape, lambda i: (0, 0))
    st_specs = [pl.BlockSpec((2, 4, s.shape[2], tile_w), lambda i: (0, 0, 0, i // 2)) for s in stages]
    w_specs = [pl.BlockSpec((w.shape[0], tile_w), lambda i: (0, i // 2)) for w in ws]
    res = _run("in_proj_bwd", body, (nsteps,),
               [dqn, dkn, dv, dcb, dcv, dcv, dqmn, proj, proj, proj, proj, proj, conv_w8, xn, x2d, dx1, pk, winT]
               + list(stages) + list(ws) + list(ms) + list(vs),
               [tile(ATT_W), tile(KV_W), tile(KV_W), tile(CONV_W), tile(CONV_W), nhalo, tile(MEM_W),
                tile(ATT_W, 0), tile(KV_W, 4), tile(CONV_W, 3), tile(CONV_W, 5), tile(MEM_W, 6), VM,
                tile(D), tile(D), tile(D), VM, VM] + st_specs + w_specs * 3,
               [SDS((T, D), f32), SDS((P, D), f32), SDS((1, D), f32), SDS((1, HD), f32), SDS((1, HD), f32),
                SDS((1, HD), f32)] + [SDS(w.shape, f32) for w in ws for _ in range(4)],
               [tile(D), pl.BlockSpec((P, D), lambda i: (0, 0)), const((1, D)), const((1, HD)), const((1, HD)),
                const((1, HD))] + [s for s in w_specs for _ in range(4)],
               vmem_mib=56)
    return res[:6], [res[6 + 4 * e:10 + 4 * e] for e in range(n)]


def mem_kv_bwd(dkm, dvm, kv, memn, mem2d, pk, wmkv):
    def body(dkm_ref, dvm_ref, kv_ref, mn_ref, m_ref, pk_ref, w_ref, dw_ref, dg_ref, dkg_ref):
        dkk, dkg = _heads_norm_bwd(dkm_ref[...], kv_ref[:, :MEM_W], _small(pk_ref, "mem_k_norm"))
        dkg_ref[...] = dkg
        dkv = _c(jnp.concatenate([dkk, dvm_ref[...]], axis=1))
        dw_ref[...] = _tn(mn_ref[...], dkv)
        mv = m_ref[...]
        dg_ref[...] = jnp.sum(_nt(dkv, w_ref[...]) * mv * _rstd(mv), axis=0, keepdims=True)

    return _run("mem_kv_bwd", body, (), [dkm, dvm, kv, memn, mem2d, pk, wmkv], [VM] * 7,
                [SDS(wmkv.shape, f32), SDS((1, mem2d.shape[1]), f32), SDS((1, HD), f32)], [VM] * 3, vmem_mib=40)


def _halves_view(g):
    return g.reshape(4, 2, g.shape[0] // 8, g.shape[1])


def kernel(x, mem, norm_mix, w_in, q_norm, k_norm, attn_sinks, conv_w, conv_b, norm_mem, w_mem_kv, mem_q_norm, mem_k_norm, out_norm_attn, out_norm_conv, out_norm_mem, w_out, norm_ffn, w_gate, w_up, w_down, loss_target, m_norm_mix, m_w_in, m_q_norm, m_k_norm, m_attn_sinks, m_conv_w, m_conv_b, m_norm_mem, m_w_mem_kv, m_mem_q_norm, m_mem_k_norm, m_out_norm_attn, m_out_norm_conv, m_out_norm_mem, m_w_out, m_norm_ffn, m_w_gate, m_w_up, m_w_down, v_norm_mix, v_w_in, v_q_norm, v_k_norm, v_attn_sinks, v_conv_w, v_conv_b, v_norm_mem, v_w_mem_kv, v_mem_q_norm, v_mem_k_norm, v_out_norm_attn, v_out_norm_conv, v_out_norm_mem, v_w_out, v_norm_ffn, v_w_gate, v_w_up, v_w_down):
    BL, S, D = x.shape
    T = BL * S
    TM = 256
    TM_BIG = min(512, S)
    w_small = dict(norm_mix=norm_mix, norm_mem=norm_mem, norm_ffn=norm_ffn, out_norm_attn=out_norm_attn,
                   out_norm_conv=out_norm_conv, out_norm_mem=out_norm_mem, conv_w=conv_w, conv_b=conv_b, q_norm=q_norm,
                   k_norm=k_norm, mem_q_norm=mem_q_norm, mem_k_norm=mem_k_norm, attn_sinks=attn_sinks)
    m_small = dict(norm_mix=m_norm_mix, norm_mem=m_norm_mem, norm_ffn=m_norm_ffn, out_norm_attn=m_out_norm_attn,
                   out_norm_conv=m_out_norm_conv, out_norm_mem=m_out_norm_mem, conv_w=m_conv_w, conv_b=m_conv_b,
                   q_norm=m_q_norm, k_norm=m_k_norm, mem_q_norm=m_mem_q_norm, mem_k_norm=m_mem_k_norm,
                   attn_sinks=m_attn_sinks)
    v_small = dict(norm_mix=v_norm_mix, norm_mem=v_norm_mem, norm_ffn=v_norm_ffn, out_norm_attn=v_out_norm_attn,
                   out_norm_conv=v_out_norm_conv, out_norm_mem=v_out_norm_mem, conv_w=v_conv_w, conv_b=v_conv_b,
                   q_norm=v_q_norm, k_norm=v_k_norm, mem_q_norm=v_mem_q_norm, mem_k_norm=v_mem_k_norm,
                   attn_sinks=v_attn_sinks)
    pk = _pack_small(w_small)

    rowblocks = lambda a, b, c, d, e, f: [a[0].T, b[0].T, c[0].T, d[0], e[0], f[0]]
    w_rb = rowblocks(w_in, w_gate, w_up, w_down, w_out, w_mem_kv)
    m_rb = rowblocks(m_w_in, m_w_gate, m_w_up, m_w_down, m_w_out, m_w_mem_kv)
    v_rb = rowblocks(v_w_in, v_w_gate, v_w_up, v_w_down, v_w_out, v_w_mem_kv)
    cw_pad = jnp.zeros((8, 128), f32).at[:3, :HD].set(conv_w[0])
    (wgT_s, wuT_s, wd_s, wout_s, wmkv_s), (winT, cw_all) = prep_weights(
        "gather_w_in", w_rb[1:], _together([gather_two_legs_cast(w_rb[:1]), gather_exchange([cw_pad], [False])]))
    conv_w_full = jnp.transpose(cw_all.reshape(4, 8, 128)[:, :3, :HD], (1, 0, 2)).reshape(3, CONV_W)
    conv_w8 = jnp.zeros((8, CONV_W), f32).at[:3].set(conv_w_full)
    sink_rows = jnp.broadcast_to(attn_sinks.reshape(N_Q, 1), (N_Q, 128))

    x2d = x.reshape(T, D)
    mem2d = mem.reshape(-1, D)
    (xn, proj, qkv), near1 = in_proj_fwd(x2d, pk, winT, TM_BIG, gather_near_exchange([wgT_s, wout_s, wmkv_s], relay_early=1))
    (attn_out,), (wgT, wout, wmkv, *near2) = attn_fwd(
        qkv, sink_rows, BL, S, _together([gather_far_exchange(near1, relay_early=2), gather_near_exchange([wuT_s, wd_s], relay_early=2)]))
    memn, kv, km, vm = mem_kv_fwd(mem2d, pk, wmkv)
    (conv_out, mem_out, merged, x1, h), (wuT, wd) = mixer_tail_fwd(
        x2d, attn_out, proj, qkv, km, vm, conv_w8, pk, wout, S, TM_BIG, gather_far_exchange(near2, relay_early=2))

    dx1, dx2b, act, d_gate, d_up, loss8, d_norm_ffn = ffn_fwd_bwd(h, x1, loss_target.reshape(T, D), wgT, wuT, wd, pk, TM)
    F = wd.shape[0]
    g_wd = matmul_tn(act, dx2b, "dw_down", F // 2, min(T, 1024))
    g_wgT = matmul_tn(d_gate, h, "dw_gate", F // 2, min(T, 1024))
    g_wuT = matmul_tn(d_up, h, "dw_up", F // 2, min(T, 1024))

    d_attn, d_conv_out, d_mem_out, g_wout, d_gains = out_proj_bwd(dx1, merged, attn_out, conv_out, mem_out, pk, wout, TM_BIG)
    dqmn, dkm, dvm, dcb, dcv, d_cw8, d_cbias = mem_conv_bwd(
        d_mem_out, mem_out, d_conv_out, proj, qkv, km, vm, conv_w8, pk, S, min(1024, S), None)
    (dqn, dkn, dv, d_sink8), (st_wout, st_wgT, st_wuT, st_wd) = attn_bwd(
        qkv, d_attn, attn_out, sink_rows, BL, S,
        reduce_scatter_exchange([_halves_view(g) for g in (g_wout, g_wgT, g_wuT, g_wd)], BL * (S // BLK + 1),
                                load_step=[0, 1, 4, 7], send_step=[1, 4, 7, 10], relay_step=[6, 16, 25, 33]))
    (g_x, g_winT, d_norm_mix, d_qg, d_kg, d_mqg), late_res = in_proj_bwd(
        dqn, dkn, dv, dcb, dcv, dqmn, proj, conv_w8, xn, x2d, dx1, pk, winT, S, TM,
        [st_wgT, st_wuT, st_wd, st_wout], w_rb[1:5], m_rb[1:5], v_rb[1:5])
    g_wmkv, d_norm_mem, d_mkg = mem_kv_bwd(dkm, dvm, kv, memn, mem2d, pk, wmkv)

    tot, tail_stage = tail_reduce(d_norm_mix, d_norm_mem, d_norm_ffn, d_gains, d_cw8, d_cbias, d_qg, d_kg, d_mqg, d_mkg,
                                  d_sink8, loss8, [_halves_view(g) for g in (g_winT, g_wmkv)])
    loss = tot[5, 384]
    tail_res, _ = adamw_big("adamw_tail", tail_stage, [w_rb[0], w_rb[5]], [m_rb[0], m_rb[5]], [v_rb[0], v_rb[5]], 4)
    res = {"w_in": [a.T[None] for a in tail_res[0]], "w_gate": [a.T[None] for a in late_res[0]],
           "w_up": [a.T[None] for a in late_res[1]], "w_down": [a[None] for a in late_res[2]],
           "w_out": [a[None] for a in late_res[3]], "w_mem_kv": [a[None] for a in tail_res[1]]}
    res.update(adamw_small(tot, pk, _pack_small(m_small), _pack_small(v_small), {k: w_small[k].shape for k in SMALL}))

    order = ["norm_mix", "w_in", "q_norm", "k_norm", "attn_sinks", "conv_w", "conv_b", "norm_mem", "w_mem_kv",
             "mem_q_norm", "mem_k_norm", "out_norm_attn", "out_norm_conv", "out_norm_mem", "w_out", "norm_ffn",
             "w_gate", "w_up", "w_down"]
    return (loss, g_x.reshape(BL, S, D), *[res[n][0] for n in order], *[res[n][1] for n in order],
            *[res[n][2] for n in order], *[res[n][3] for n in order])
```

```python
import collections
import functools

import jax
import jax.numpy as jnp
import numpy as np
from jax import lax
from jax.experimental import pallas as pl
from jax.experimental.pallas import tpu as pltpu

f32 = jnp.float32
MXU = jnp.bfloat16
WIRE = jnp.bfloat16
EPS = 1e-6
NEG = -1e30
HD = 64
BLK = 128
N_Q, N_KV, N_MEMH = 8, 2, 4
GQA = N_Q // N_KV
ATT_W, KV_W, CONV_W, MEM_W = 512, 128, 256, 256
VMEM_MIB = 1024 * 1024
ADAM_LR, ADAM_B1, ADAM_B2, ADAM_EPS, ADAM_WD, ADAM_STEP = 0.001, 0.9, 0.999, 1e-08, 0.01, 10

MESH = pl.DeviceIdType.MESH
VM = pl.BlockSpec(memory_space=pltpu.VMEM)
ANY = pl.BlockSpec(memory_space=pl.ANY)
SDS = jax.ShapeDtypeStruct
DMA = pltpu.SemaphoreType.DMA


def _c(v):
    return v.astype(MXU)


def _nn(a, b):
    return lax.dot_general(a, b, (((1,), (0,)), ((), ())), preferred_element_type=f32)


def _nt(a, b):
    return lax.dot_general(a, b, (((1,), (1,)), ((), ())), preferred_element_type=f32)


def _tn(a, b):
    return lax.dot_general(a, b, (((0,), (0,)), ((), ())), preferred_element_type=f32)


def _rstd(v):
    return lax.rsqrt(jnp.mean(v * v, axis=-1, keepdims=True) + EPS)


def _norm_bwd(dy, v, r, g):
    dyg = dy * g
    dv = r * dyg - v * (r * r * r) * jnp.mean(dyg * v, axis=-1, keepdims=True)
    return dv, jnp.sum(dy * v * r, axis=0, keepdims=True)


def _split3(v):
    hi = _c(v)
    r1 = v - hi.astype(f32)
    mid = _c(r1)
    return hi, mid, _c(r1 - mid.astype(f32))


def _rowsum_mxu(v, width):
    ones = jnp.ones((v.shape[1], width), MXU)
    return sum(_nn(a, ones) for a in _split3(v))


def _seg_sums(v):
    r = lax.broadcasted_iota(jnp.int32, (2 * HD, 2 * HD), 0) // HD
    c = lax.broadcasted_iota(jnp.int32, (2 * HD, 2 * HD), 1) // HD
    bd = (r == c).astype(MXU)
    outs = []
    for b in range(v.shape[1] // (2 * HD)):
        outs.append(sum(_nn(a, bd) for a in _split3(v[:, b * 2 * HD:(b + 1) * 2 * HD])))
    return outs[0] if len(outs) == 1 else jnp.concatenate(outs, axis=1)


def _lanes(g, width):
    return jnp.concatenate([g] * (width // HD), axis=1)


def _heads_rstd(v):
    return lax.rsqrt(_seg_sums(v * v) * (1.0 / HD) + EPS)


def _heads_norm_bwd(dy, v, g):
    r = _heads_rstd(v)
    gl = _lanes(g, v.shape[1])
    dyg = dy * gl
    dv = r * dyg - v * (r * r * r) * (_seg_sums(dyg * v) * (1.0 / HD))
    dgl = jnp.sum(dy * v * r, axis=0, keepdims=True)
    return dv, sum(dgl[:, s * HD:(s + 1) * HD] for s in range(v.shape[1] // HD))


def _exp_scores(s, extra=None):
    m = jnp.max(s, axis=-1, keepdims=True)
    if extra is None:
        return jnp.exp(s - m), None
    m = jnp.maximum(m, extra)
    return jnp.exp(s - m), jnp.exp(extra - m)


def _place():
    return lax.axis_index("x"), lax.axis_index("y"), lax.axis_index("c")


SMALL_AT = {"norm_mix": (0, 0, 1024), "norm_mem": (1, 0, 1024), "norm_ffn": (2, 0, 1024),
            "out_norm_attn": (3, 0, ATT_W), "out_norm_conv": (3, ATT_W, CONV_W), "out_norm_mem": (3, ATT_W + CONV_W, MEM_W),
            "conv_b": (4, 3 * CONV_W, CONV_W), "q_norm": (5, 0, HD), "k_norm": (5, HD, HD), "mem_q_norm": (5, 2 * HD, HD),
            "mem_k_norm": (5, 3 * HD, HD), "attn_sinks": (5, 256, N_Q)}
SMALL = ("norm_mix", "norm_mem", "norm_ffn", "out_norm_attn", "out_norm_conv", "out_norm_mem", "conv_w", "conv_b",
         "q_norm", "k_norm", "mem_q_norm", "mem_k_norm", "attn_sinks")


def _small(pk_ref, name):
    r, c0, w = SMALL_AT[name]
    return pk_ref[r:r + 1, c0:c0 + w]


def _pack_small(d):
    z = lambda n: jnp.zeros((1, n), f32)
    row3 = jnp.concatenate([d["out_norm_attn"], d["out_norm_conv"], d["out_norm_mem"]], axis=1)
    row4 = jnp.concatenate([d["conv_w"].reshape(1, 3 * HD), z(3 * CONV_W - 3 * HD), d["conv_b"]], axis=1)
    row5 = jnp.concatenate([d["q_norm"], d["k_norm"], d["mem_q_norm"], d["mem_k_norm"], d["attn_sinks"],
                            z(1024 - 4 * HD - N_Q)], axis=1)
    return jnp.concatenate([d["norm_mix"], d["norm_mem"], d["norm_ffn"], row3, row4, row5, z(1024), z(1024)], axis=0)


def _other_chips(x, y):
    return [(1 - x, y), (x, 1 - y), (1 - x, 1 - y)]


Exchange = collections.namedtuple("Exchange", "ins outs sems start finish relays aliases", defaults=((), {}))


def _together(exchanges):
    def bounds(key):
        at, out = 0, []
        for ex in exchanges:
            out.append((at, at + len(getattr(ex, key))))
            at += len(getattr(ex, key))
        return out

    bi, bo, bs = bounds("ins"), bounds("outs"), bounds("sems")

    def of(i, fn):
        return lambda xa, xo, xs: fn(xa[bi[i][0]:bi[i][1]], xo[bo[i][0]:bo[i][1]], xs[bs[i][0]:bs[i][1]])

    def every(name):
        fns = [of(i, getattr(ex, name)) for i, ex in enumerate(exchanges)]

        def run(xa, xo, xs):
            for fn in fns:
                fn(xa, xo, xs)
        return run

    aliases = {}
    for i, ex in enumerate(exchanges):
        aliases.update({bi[i][0] + a: bo[i][0] + o for a, o in ex.aliases.items()})
    return Exchange([a for ex in exchanges for a in ex.ins], [o for ex in exchanges for o in ex.outs],
                    [s for ex in exchanges for s in ex.sems], every("start"), every("finish"),
                    [(sbe, of(i, fn)) for i, ex in enumerate(exchanges) for sbe, fn in ex.relays], aliases)


def _run(name, body, grid, ins, in_specs, out_shape, out_specs, scratch=(), vmem_mib=32, exchange=None):
    ins, in_specs, out_shape, out_specs, scratch = list(ins), list(in_specs), list(out_shape), list(out_specs), list(scratch)
    ni, no, ns = len(ins), len(out_shape), len(scratch)
    ex = exchange
    if ex is not None:
        nxi, nxo = len(ex.ins), len(ex.outs)

    def call_body(*refs):
        if ex is None:
            body(*refs)
            return
        a, xa = refs[:ni], refs[ni:ni + nxi]
        o, xo = refs[ni + nxi:ni + nxi + no], refs[ni + nxi + no:ni + nxi + no + nxo]
        s, xs = refs[ni + nxi + no + nxo:ni + nxi + no + nxo + ns], refs[ni + nxi + no + nxo + ns:]
        if grid:
            first = functools.reduce(jnp.logical_and, [pl.program_id(d) == 0 for d in range(len(grid))])
            last = functools.reduce(jnp.logical_and, [pl.program_id(d) == grid[d] - 1 for d in range(len(grid))])
            pl.when(first)(lambda: ex.start(xa, xo, xs))
            body(*a, *o, *s)
            nsteps = functools.reduce(lambda p, q: p * q, grid)
            for before_end, fn in ex.relays:
                at = np.unravel_index(max(nsteps - 1 - before_end, 0), grid)
                here = functools.reduce(jnp.logical_and, [pl.program_id(d) == int(at[d]) for d in range(len(grid))])
                pl.when(here)(functools.partial(fn, xa, xo, xs))
            pl.when(last)(lambda: ex.finish(xa, xo, xs))
        else:
            ex.start(xa, xo, xs)
            if body is not None:
                body(*a, *o, *s)
            for _, fn in ex.relays:
                fn(xa, xo, xs)
            ex.finish(xa, xo, xs)

    kw = dict(grid=grid) if grid else {}
    if ex is not None:
        if ex.aliases:
            kw["input_output_aliases"] = {ni + i: no + o for i, o in ex.aliases.items()}
        ins, in_specs = ins + list(ex.ins), in_specs + [ANY] * nxi
        out_shape, out_specs = out_shape + list(ex.outs), out_specs + [ANY] * nxo
        scratch = scratch + list(ex.sems)
    res = pl.pallas_call(
        call_body, name=name, out_shape=out_shape, in_specs=in_specs, out_specs=out_specs, scratch_shapes=scratch,
        compiler_params=pltpu.CompilerParams(dimension_semantics=("arbitrary",) * len(grid) if grid else None,
                                             vmem_limit_bytes=vmem_mib * VMEM_MIB), **kw)(*ins)
    res = list(res)
    return (res[:no], res[no:]) if ex is not None else res


def _remote(src, dst, ssem, rsem, dev):
    return pltpu.make_async_remote_copy(src_ref=src, dst_ref=dst, send_sem=ssem, recv_sem=rsem,
                                        device_id=dev, device_id_type=MESH)


def gather_exchange(shards, split, relay_early=0):
    n = len(shards)

    def rows(ref, e, kk, half=None):
        R = shards[e].shape[0]
        if half is None:
            return ref.at[pl.ds(pl.multiple_of(kk * R, 8), R)]
        return ref.at[pl.ds(pl.multiple_of(kk * R + half * (R // 2), 8), R // 2)]

    def ici(src, dst, sm, e, j, chip_j, x, y, c):
        k = 2 * x + y
        if split[e]:
            s = src[e].at[pl.ds(pl.multiple_of(c * (shards[e].shape[0] // 2), 8), shards[e].shape[0] // 2)]
            return _remote(s, rows(dst[e], e, k, c), sm[0].at[6 * e + j], sm[1].at[6 * e + j], (*chip_j, c))
        return _remote(src[e], rows(dst[e], e, k), sm[0].at[6 * e + j], sm[1].at[6 * e + j], (*chip_j, c))

    def landed(dst, e, chip_j, c):
        kj = 2 * chip_j[0] + chip_j[1]
        return rows(dst[e], e, kj, c) if split[e] else rows(dst[e], e, kj)

    def forward(dst, sm, e, j, chip_j, x, y, c, sender_c):
        kj = 2 * chip_j[0] + chip_j[1]
        r = rows(dst[e], e, kj, sender_c)
        return _remote(r, r, sm[0].at[6 * e + 3 + j], sm[1].at[6 * e + 3 + j], (x, y, 1 - c))

    def local(src, dst, sm, e, x, y):
        return pltpu.make_async_copy(src[e], rows(dst[e], e, 2 * x + y), sm[2].at[e])

    def start(src, dst, sm):
        x, y, c = _place()
        for e in range(n):
            local(src, dst, sm, e, x, y).start()
            for j, chip_j in enumerate(_other_chips(x, y)):
                ici(src, dst, sm, e, j, chip_j, x, y, c).start()

    def relay(src, dst, sm):
        x, y, c = _place()
        for e in range(n):
            for j, chip_j in enumerate(_other_chips(x, y)):
                r = landed(dst, e, chip_j, c)
                _remote(r, r, sm[0].at[6 * e + j], sm[1].at[6 * e + j], (*chip_j, c)).wait_recv()
                if split[e]:
                    forward(dst, sm, e, j, chip_j, x, y, c, c).start()

    def finish(src, dst, sm):
        x, y, c = _place()
        chips = _other_chips(x, y)
        for e in range(n):
            for j, chip_j in enumerate(chips):
                if split[e]:
                    forward(dst, sm, e, j, chip_j, x, y, c, 1 - c).wait_recv()
        for e in range(n):
            for j, chip_j in enumerate(chips):
                ici(src, dst, sm, e, j, chip_j, x, y, c).wait_send()
                if split[e]:
                    forward(dst, sm, e, j, chip_j, x, y, c, c).wait_send()
            local(src, dst, sm, e, x, y).wait()

    outs = [SDS((4 * s.shape[0], s.shape[1]), s.dtype) for s in shards]
    return Exchange(list(shards), outs, [DMA((6 * n,)), DMA((6 * n,)), DMA((n,))], start, finish, [(relay_early, relay)])


def _block_rows(ref, R, kk, half, quarter=None):
    hr = R // 2
    if quarter is None:
        return ref.at[pl.ds(pl.multiple_of(kk * R + half * hr, 8), hr)]
    return ref.at[pl.ds(pl.multiple_of(kk * R + half * hr + quarter * (hr // 2), 8), hr // 2)]


def gather_near_exchange(shards, relay_early=0):
    n = len(shards)
    R = [s.shape[0] for s in shards]

    def ici(src, dst, sm, e, j, chip_j, x, y, c):
        half = src[e].at[pl.ds(pl.multiple_of(c * (R[e] // 2), 8), R[e] // 2)]
        return _remote(half, _block_rows(dst[e], R[e], 2 * x + y, c), sm[0].at[4 * e + j], sm[1].at[4 * e + j], (*chip_j, c))

    def forward(dst, sm, e, j, chip_j, x, y, c, sender_c):
        r = _block_rows(dst[e], R[e], 2 * chip_j[0] + chip_j[1], sender_c)
        return _remote(r, r, sm[0].at[4 * e + 2 + j], sm[1].at[4 * e + 2 + j], (x, y, 1 - c))

    def local(src, dst, sm, e, x, y):
        return pltpu.make_async_copy(src[e], dst[e].at[pl.ds(pl.multiple_of((2 * x + y) * R[e], 8), R[e])], sm[2].at[e])

    def start(src, dst, sm):
        x, y, c = _place()
        for e in range(n):
            local(src, dst, sm, e, x, y).start()
            for j, chip_j in enumerate(_other_chips(x, y)[:2]):
                ici(src, dst, sm, e, j, chip_j, x, y, c).start()

    def relay(src, dst, sm):
        x, y, c = _place()
        for e in range(n):
            for j, chip_j in enumerate(_other_chips(x, y)[:2]):
                r = _block_rows(dst[e], R[e], 2 * chip_j[0] + chip_j[1], c)
                _remote(r, r, sm[0].at[4 * e + j], sm[1].at[4 * e + j], (*chip_j, c)).wait_recv()
                forward(dst, sm, e, j, chip_j, x, y, c, c).start()

    def finish(src, dst, sm):
        x, y, c = _place()
        near = _other_chips(x, y)[:2]
        for e in range(n):
            for j, chip_j in enumerate(near):
                forward(dst, sm, e, j, chip_j, x, y, c, 1 - c).wait_recv()
        for e in range(n):
            for j, chip_j in enumerate(near):
                ici(src, dst, sm, e, j, chip_j, x, y, c).wait_send()
                forward(dst, sm, e, j, chip_j, x, y, c, c).wait_send()
            local(src, dst, sm, e, x, y).wait()

    outs = [SDS((4 * s.shape[0], s.shape[1]), s.dtype) for s in shards]
    return Exchange(list(shards), outs, [DMA((4 * n,)), DMA((4 * n,)), DMA((n,))], start, finish, [(relay_early, relay)])


def gather_far_exchange(bufs, relay_early=0):
    n = len(bufs)
    R = [b.shape[0] // 4 for b in bufs]

    def send(src, dst, sm, e, j, x, y, c):
        to, of = _other_chips(x, y)[j], _other_chips(x, y)[1 - j]
        kk = 2 * of[0] + of[1]
        return _remote(_block_rows(src[e], R[e], kk, c, j), _block_rows(dst[e], R[e], kk, c, j),
                       sm[0].at[4 * e + j], sm[1].at[4 * e + j], (*to, c))

    def landed(dst, e, j, x, y, half):
        return _block_rows(dst[e], R[e], 2 * (1 - x) + (1 - y), half, j)

    def forward(dst, sm, e, j, x, y, c, sender_c):
        r = landed(dst, e, j, x, y, sender_c)
        return _remote(r, r, sm[0].at[4 * e + 2 + j], sm[1].at[4 * e + 2 + j], (x, y, 1 - c))

    def start(src, dst, sm):
        x, y, c = _place()
        for e in range(n):
            for j in range(2):
                send(src, dst, sm, e, j, x, y, c).start()

    def relay(src, dst, sm):
        x, y, c = _place()
        for e in range(n):
            for j in range(2):
                r = landed(dst, e, j, x, y, c)
                _remote(r, r, sm[0].at[4 * e + j], sm[1].at[4 * e + j], (*_other_chips(x, y)[j], c)).wait_recv()
                forward(dst, sm, e, j, x, y, c, c).start()

    def finish(src, dst, sm):
        x, y, c = _place()
        for e in range(n):
            for j in range(2):
                forward(dst, sm, e, j, x, y, c, 1 - c).wait_recv()
        for e in range(n):
            for j in range(2):
                send(src, dst, sm, e, j, x, y, c).wait_send()
                forward(dst, sm, e, j, x, y, c, c).wait_send()

    outs = [SDS(b.shape, b.dtype) for b in bufs]
    return Exchange(list(bufs), outs, [DMA((4 * n,)), DMA((4 * n,))], start, finish, [(relay_early, relay)],
                    {i: i for i in range(n)})


def gather_two_legs(shards):
    near = gather_near_exchange(shards)
    far = gather_far_exchange(near.outs)

    def finish(src, dst, sm):
        near.relays[0][1](src, dst, sm[:3])
        near.finish(src, dst, sm[:3])
        far.start(dst, dst, sm[3:])
        far.relays[0][1](dst, dst, sm[3:])
        far.finish(dst, dst, sm[3:])

    return Exchange(near.ins, near.outs, list(near.sems) + list(far.sems),
                    lambda src, dst, sm: near.start(src, dst, sm[:3]), finish)


def scatter_exchange(parts, relay_before_end=None, want_issue=False):
    n = len(parts)
    by_entry = relay_before_end is not None
    relay_before_end = relay_before_end or [0] * n

    def ici(p, st, sm, e, j, chip_j, x, y, c):
        k, kj = 2 * x + y, 2 * chip_j[0] + chip_j[1]
        return _remote(p[e].at[kj], st[e].at[c, k], sm[0].at[8 * e + j], sm[1].at[8 * e + j], (*chip_j, c))

    def own(p, st, sm, e, x, y, c):
        k = 2 * x + y
        return _remote(p[e].at[k], st[e].at[c, k], sm[0].at[8 * e + 3], sm[1].at[8 * e + 3], (x, y, 1 - c))

    def forward(st, sm, e, j, chip_j, x, y, c, sender_c):
        kj = 2 * chip_j[0] + chip_j[1]
        r = st[e].at[sender_c, kj]
        return _remote(r, r, sm[0].at[8 * e + 4 + j], sm[1].at[8 * e + 4 + j], (x, y, 1 - c))

    def local(p, st, sm, e, x, y, c):
        k = 2 * x + y
        return pltpu.make_async_copy(p[e].at[k], st[e].at[c, k], sm[2].at[e])

    def issue(e, p, st, sm):
        x, y, c = _place()
        for j, chip_j in enumerate(_other_chips(x, y)):
            ici(p, st, sm, e, j, chip_j, x, y, c).start()
        local(p, st, sm, e, x, y, c).start()
        own(p, st, sm, e, x, y, c).start()

    def start(p, st, sm, before_slot=None):
        x, y, c = _place()
        if by_entry:
            for e in range(n):
                issue(e, p, st, sm)
            return
        for j, chip_j in enumerate(_other_chips(x, y)):
            if before_slot is not None:
                before_slot(j, 2 * chip_j[0] + chip_j[1])
            for e in range(n):
                ici(p, st, sm, e, j, chip_j, x, y, c).start()
        if before_slot is not None:
            before_slot(3, 2 * x + y)
        for e in range(n):
            local(p, st, sm, e, x, y, c).start()
            own(p, st, sm, e, x, y, c).start()

    def relay(e, p, st, sm):
        x, y, c = _place()
        for j, chip_j in enumerate(_other_chips(x, y)):
            kj = 2 * chip_j[0] + chip_j[1]
            r = st[e].at[c, kj]
            _remote(r, r, sm[0].at[8 * e + j], sm[1].at[8 * e + j], (*chip_j, c)).wait_recv()
            forward(st, sm, e, j, chip_j, x, y, c, c).start()

    def finish(p, st, sm):
        x, y, c = _place()
        k = 2 * x + y
        chips = _other_chips(x, y)
        for e in range(n):
            r = st[e].at[1 - c, k]
            _remote(r, r, sm[0].at[8 * e + 3], sm[1].at[8 * e + 3], (x, y, 1 - c)).wait_recv()
            for j, chip_j in enumerate(chips):
                forward(st, sm, e, j, chip_j, x, y, c, 1 - c).wait_recv()
        for e in range(n):
            own(p, st, sm, e, x, y, c).wait_send()
            for j, chip_j in enumerate(chips):
                ici(p, st, sm, e, j, chip_j, x, y, c).wait_send()
                forward(st, sm, e, j, chip_j, x, y, c, c).wait_send()
            local(p, st, sm, e, x, y, c).wait()

    outs = [SDS((2,) + a.shape, a.dtype) for a in parts]
    ex = Exchange(list(parts), outs, [DMA((8 * n,)), DMA((8 * n,)), DMA((n,))], start, finish,
                  [(relay_before_end[e], functools.partial(relay, e)) for e in range(n)])
    return (ex, issue) if want_issue else ex


def reduce_scatter_exchange(grads, nsteps, load_step, send_step, relay_step):
    n = len(grads)
    hrs = [g.shape[2] for g in grads]
    C = grads[0].shape[3]
    scatter, issue = scatter_exchange([SDS((4,) + g.shape[2:], WIRE) for g in grads], want_issue=True)
    hand_on = [fn for _, fn in scatter.relays]

    def refs(xs):
        return xs[:3], xs[3], xs[4], xs[5], xs[6], xs[7:7 + n], xs[7 + n:]

    def push(e, g, psem, qsem, sib_st):
        x, y, c = _place()
        return _remote(g[e].at[:, 1 - c], sib_st[e], psem.at[e], qsem.at[e], (x, y, 1 - c))

    def fetch(e, g, lsem, own_st):
        _, _, c = _place()
        return pltpu.make_async_copy(g[e].at[:, c], own_st.at[e % 2, :, pl.ds(0, hrs[e])], lsem.at[e])

    def start(g, xo, xs):
        _, _, psem, qsem, _, sib_st, _ = refs(xs)
        for e in range(n):
            push(e, g, psem, qsem, sib_st).start()

    def load(e, g, xo, xs):
        _, lsem, _, _, own_st, _, _ = refs(xs)
        fetch(e, g, lsem, own_st).start()

    def send(e, g, xo, xs):
        sm, lsem, psem, qsem, own_st, sib_st, part = refs(xs)
        fetch(e, g, lsem, own_st).wait()
        push(e, g, psem, qsem, sib_st).wait_recv()
        part[e][...] = (own_st[e % 2, :, 0:hrs[e]] + sib_st[e][...]).astype(WIRE)
        issue(e, part, xo, sm)

    def relay(e, g, xo, xs):
        sm, _, _, _, _, _, part = refs(xs)
        hand_on[e](part, xo, sm)

    def finish(g, xo, xs):
        sm, _, psem, qsem, _, sib_st, part = refs(xs)
        scatter.finish(part, xo, sm)
        for e in range(n):
            push(e, g, psem, qsem, sib_st).wait_send()

    plan = sorted([(min(step[e], nsteps - 1), phase, e) for phase, step in enumerate((load_step, send_step, relay_step))
                   for e in range(n)])
    stage = (load, send, relay)
    relays = [(nsteps - 1 - at, functools.partial(stage[phase], e)) for at, phase, e in plan]
    scratch = (list(scatter.sems) + [DMA((n,)), DMA((n,)), DMA((n,))] + [pltpu.VMEM((2, 4, max(hrs), C), f32)]
               + [pltpu.VMEM((4, hr, C), f32) for hr in hrs] + [pltpu.VMEM((4, hr, C), WIRE) for hr in hrs])
    return Exchange(list(grads), scatter.outs, scratch, start, finish, relays)


def tail_reduce(d_norm_mix, d_norm_mem, d_norm_ffn, d_gains, d_cw8, d_cbias, d_qg, d_kg, d_mqg, d_mkg, d_sink8, loss8, tail):
    n = len(tail)
    scatter = scatter_exchange([SDS((4,) + a.shape[2:], WIRE) for a in tail])

    def half_copy(g, sib, hsem, e, j, slot, x, y, c):
        return _remote(g[e].at[slot, 1 - c], sib[e].at[slot], hsem[0].at[4 * e + j], hsem[1].at[4 * e + j], (x, y, 1 - c))

    def body(nm_ref, nmem_ref, nf_ref, gn_ref, cw_ref, cb_ref, qg_ref, kg_ref, mqg_ref, mkg_ref, sk_ref, ls_ref, *rest):
        g, o_ref, st = rest[:n], rest[n], rest[n + 1:2 * n + 1]
        buf, ssem, rsem = rest[2 * n + 1:2 * n + 4]
        own, sib, part = (rest[2 * n + 4 + i * n:2 * n + 4 + (i + 1) * n] for i in range(3))
        lsem = rest[5 * n + 4]
        hsem, xsem = rest[5 * n + 5:5 * n + 7], rest[5 * n + 7:]
        x, y, c = _place()
        loads = [pltpu.make_async_copy(g[e].at[:, c], own[e], lsem.at[e]) for e in range(n)]
        for ld in loads:
            ld.start()
        for j, slot in enumerate([2 * cx + cy for cx, cy in _other_chips(x, y)] + [2 * x + y]):
            for e in range(n):
                half_copy(g, sib, hsem, e, j, slot, x, y, c).start()
        me = 4 * x + 2 * y + c
        mine = buf.at[me]
        mine[...] = jnp.zeros((8, 1024), f32)
        mine[0:1, :] = nm_ref[...]
        mine[1:2, :] = nmem_ref[...]
        mine[2:3, :] = nf_ref[...]
        mine[3:4, :] = gn_ref[...]
        for j in range(3):
            mine[4:5, pl.ds(j * CONV_W, CONV_W)] = cw_ref[j:j + 1, :]
        mine[4:5, pl.ds(3 * CONV_W, CONV_W)] = cb_ref[...]
        for j, r in enumerate((qg_ref, kg_ref, mqg_ref, mkg_ref)):
            mine[5:6, pl.ds(j * HD, HD)] = r[...]
        mine[5:6, pl.ds(256, 128)] = sk_ref[0:1, :]
        mine[5:6, pl.ds(384, 128)] = ls_ref[0:1, :]

        def peer_of(m):
            return (1 - x if m & 4 else x, 1 - y if m & 2 else y, 1 - c if m & 1 else c)

        for m in range(1, 8):
            _remote(mine, mine, ssem.at[m - 1], rsem.at[m - 1], peer_of(m)).start()
        for ld in loads:
            ld.wait()

        def chip_partial(j, slot):
            for e in range(n):
                half_copy(g, sib, hsem, e, j, slot, x, y, c).wait()
                part[e][slot] = (own[e][slot] + sib[e][slot]).astype(WIRE)

        scatter.start(part, st, xsem, chip_partial)
        for _, hand_on in scatter.relays:
            hand_on(part, st, xsem)
        scatter.finish(part, st, xsem)
        for m in range(1, 8):
            p = peer_of(m)
            got = buf.at[4 * p[0] + 2 * p[1] + p[2]]
            _remote(got, got, ssem.at[m - 1], rsem.at[m - 1], p).wait_recv()
        for m in range(1, 8):
            _remote(mine, mine, ssem.at[m - 1], rsem.at[m - 1], peer_of(m)).wait_send()
        acc = buf[0]
        for d in range(1, 8):
            acc = acc + buf[d]
        o_ref[...] = acc

    ins = [d_norm_mix, d_norm_mem, d_norm_ffn, d_gains, d_cw8, d_cbias, d_qg, d_kg, d_mqg, d_mkg, d_sink8, loss8]
    half_shape = [(4,) + a.shape[2:] for a in tail]
    scratch = ([pltpu.VMEM((8, 8, 1024), f32), DMA((7,)), DMA((7,))]
               + [pltpu.VMEM(s, f32) for s in half_shape] * 2 + [pltpu.VMEM(s, WIRE) for s in half_shape]
               + [DMA((n,)), DMA((4 * n,)), DMA((4 * n,))] + list(scatter.sems))
    res = _run("tail_reduce", body, (), ins + list(tail), [VM] * len(ins) + [ANY] * n,
               [SDS((8, 1024), f32)] + list(scatter.outs), [VM] + [ANY] * n, scratch=scratch, vmem_mib=40)
    return res[0], res[1:]


def _adamw_math(w, g, m, v):
    m = ADAM_B1 * m + (1.0 - ADAM_B1) * g
    v = ADAM_B2 * v + (1.0 - ADAM_B2) * (g * g)
    m_hat = m / (1.0 - ADAM_B1 ** ADAM_STEP)
    v_hat = v / (1.0 - ADAM_B2 ** ADAM_STEP)
    delta = -ADAM_LR * (m_hat / (jnp.sqrt(v_hat) + ADAM_EPS) + ADAM_WD * w)
    return delta, m, v


def _sum_chips(st):
    return ((st[0].astype(f32) + st[1].astype(f32)) + st[2].astype(f32)) + st[3].astype(f32)


def adamw_big(name, stages, ws, ms, vs, nstep, exchange=None):
    n = len(stages)

    def body(*refs):
        st, w, m, v = refs[:n], refs[n:2 * n], refs[2 * n:3 * n], refs[3 * n:4 * n]
        outs = refs[4 * n:]
        for e in range(n):
            g = jnp.concatenate([_sum_chips(st[e].at[0]), _sum_chips(st[e].at[1])], axis=0)
            d, mm, vv = _adamw_math(w[e][...], g, m[e][...], v[e][...])
            outs[4 * e][...] = g
            outs[4 * e + 1][...] = d
            outs[4 * e + 2][...] = mm
            outs[4 * e + 3][...] = vv

    st_specs, w_specs = [], []
    for e in range(n):
        _, _, hr, C = stages[e].shape
        st_specs.append(pl.BlockSpec((2, 4, hr, C // nstep), lambda i: (0, 0, 0, i)))
        w_specs.append(pl.BlockSpec((2 * hr, C // nstep), lambda i: (0, i)))
    out_specs = [s for s in w_specs for _ in range(4)]
    out_shape = [SDS(w.shape, f32) for w in ws for _ in range(4)]
    res = _run(name, body, (nstep,), list(stages) + list(ws) + list(ms) + list(vs), st_specs + w_specs * 3,
               out_shape, out_specs, vmem_mib=16, exchange=exchange)
    res, sent = res if exchange is not None else (res, None)
    return [res[4 * e:4 * e + 4] for e in range(n)], sent


def adamw_small(tot, pk_w, pk_m, pk_v, shapes):
    def body(tot_ref, w_ref, m_ref, v_ref, *outs):
        x, y, _ = _place()
        chip = 2 * x + y
        taps = []
        for j in range(3):
            mine = tot_ref[4:5, j * CONV_W:j * CONV_W + HD]
            for s in range(1, 4):
                mine = jnp.where(chip == s, tot_ref[4:5, j * CONV_W + s * HD:j * CONV_W + (s + 1) * HD], mine)
            taps.append(mine)
        row4 = jnp.concatenate(taps + [jnp.zeros((1, 3 * CONV_W - 3 * HD), f32), tot_ref[4:5, 3 * CONV_W:]], axis=1)
        tot_v = tot_ref[...]
        row = lax.broadcasted_iota(jnp.int32, tot_v.shape, 0)
        g = jnp.where(row == 4, jnp.broadcast_to(row4, tot_v.shape), tot_v)
        d, mm, vv = _adamw_math(w_ref[...], g, m_ref[...], v_ref[...])
        for i, name in enumerate(SMALL):
            for k, val in enumerate((g, d, mm, vv)):
                if name == "conv_w":
                    outs[4 * i + k][...] = jnp.concatenate([val[4:5, j * HD:(j + 1) * HD] for j in range(3)], axis=0)[None]
                else:
                    r, c0, w = SMALL_AT[name]
                    outs[4 * i + k][...] = val[r:r + 1, c0:c0 + w]

    out_shape = [SDS(shapes[k], f32) for k in SMALL for _ in range(4)]
    res = _run("adamw_small", body, (), [tot, pk_w, pk_m, pk_v], [VM] * 4, out_shape, [VM] * len(out_shape))
    return {k: res[4 * i:4 * i + 4] for i, k in enumerate(SMALL)}


def prep_weights(name, shards, exchange=None):
    n = len(shards)

    def body(*refs):
        for e in range(n):
            refs[n + e][...] = _c(refs[e][...])

    return _run(name, body, (), shards, [VM] * n, [SDS(a.shape, MXU) for a in shards], [VM] * n, vmem_mib=16, exchange=exchange)


def mem_kv_fwd(mem2d, pk, wmkv):
    M, D = mem2d.shape

    def body(m_ref, pk_ref, w_ref, mn_ref, kv_ref, km_ref, vm_ref):
        m = m_ref[...]
        mn = _c(m * _rstd(m) * _small(pk_ref, "norm_mem"))
        mn_ref[...] = mn
        kv = _nn(mn, w_ref[...])
        kv_ref[...] = kv
        kk = kv[:, :MEM_W]
        km_ref[...] = _c(kk * _heads_rstd(kk) * _lanes(_small(pk_ref, "mem_k_norm"), MEM_W))
        vm_ref[...] = _c(kv[:, MEM_W:])

    return _run("mem_kv_fwd", body, (), [mem2d, pk, wmkv], [VM] * 3,
                [SDS((M, D), MXU), SDS((M, 2 * MEM_W), f32), SDS((M, MEM_W), MXU), SDS((M, MEM_W), MXU)], [VM] * 4)


QKV_W = ATT_W + 2 * KV_W + MEM_W


def in_proj_fwd(x2d, pk, winT, tm, exchange):
    T, D = x2d.shape
    P = winT.shape[0]

    def body(x_ref, pk_ref, w_ref, xn_ref, proj_ref, qkv_ref):
        xv = x_ref[...]
        xn = _c(xv * _rstd(xv) * _small(pk_ref, "norm_mix"))
        xn_ref[...] = xn
        proj = _nt(xn, w_ref[...])
        proj_ref[...] = proj
        q, k = proj[:, :ATT_W], proj[:, ATT_W:ATT_W + KV_W]
        qm = proj[:, P - MEM_W:]
        qkv_ref[...] = jnp.concatenate(
            [_c(q * _heads_rstd(q) * _lanes(_small(pk_ref, "q_norm"), ATT_W)),
             _c(k * _heads_rstd(k) * _lanes(_small(pk_ref, "k_norm"), KV_W)),
             _c(proj[:, ATT_W + KV_W:ATT_W + 2 * KV_W]),
             _c(qm * _heads_rstd(qm) * _lanes(_small(pk_ref, "mem_q_norm"), MEM_W))], axis=1)

    return _run("in_proj_fwd", body, (T // tm,), [x2d, pk, winT],
                [pl.BlockSpec((tm, D), lambda i: (i, 0)), VM, VM],
                [SDS((T, D), MXU), SDS((T, P), f32), SDS((T, QKV_W), MXU)],
                [pl.BlockSpec((tm, D), lambda i: (i, 0)), pl.BlockSpec((tm, P), lambda i: (i, 0)),
                 pl.BlockSpec((tm, QKV_W), lambda i: (i, 0))],
                vmem_mib=40, exchange=exchange)


def _swa_bias_table():
    r = np.arange(GQA * BLK)[:, None]
    k = np.arange(2 * BLK)[None, :]
    dist = (r % BLK) + BLK - k
    band = (dist >= 0) & (dist < BLK)
    tab = np.empty((2, N_KV, GQA * BLK, 2 * BLK), np.float32)
    for later in range(2):
        valid = band & ((k >= BLK) | (later == 1))
        for g in range(N_KV):
            slope = 2.0 ** -(g * GQA + r // BLK + 1.0)
            tab[later, g] = np.where(valid, -slope * dist, NEG)
    return jnp.asarray(tab)


def _sink_column(g, sk_ref):
    hrow = lax.broadcasted_iota(jnp.int32, (GQA * BLK, 1), 0) // BLK
    sink = jnp.zeros((GQA * BLK, 1), f32)
    for hh in range(GQA):
        sink = jnp.where(hrow == hh, sk_ref[g * GQA + hh:g * GQA + hh + 1, 0:1], sink)
    return sink


def _stack_heads(v, g):
    return jnp.concatenate([v[:, (g * GQA + hh) * HD:(g * GQA + hh + 1) * HD] for hh in range(GQA)], axis=0)


def attn_fwd(qkv, sink_rows, BL, S, exchange, qb=2):
    NS = S // (qb * BLK)
    T = BL * S

    def body(q_ref, kc_ref, kp_ref, vc_ref, vp_ref, sk_ref, tab_ref, o_ref):
        j = pl.program_id(1)
        kall = jnp.concatenate([kp_ref[...], kc_ref[...]], axis=0)
        vall = jnp.concatenate([vp_ref[...], vc_ref[...]], axis=0)
        ones = jnp.ones((2 * BLK, HD), MXU)
        for b in range(qb):
            q = q_ref[pl.ds(b * BLK, BLK), :]
            k2, v2 = kall[b * BLK:(b + 2) * BLK], vall[b * BLK:(b + 2) * BLK]
            later = jnp.minimum(j, 1) if b == 0 else 1
            for g in range(N_KV):
                kn, vh = k2[:, g * HD:(g + 1) * HD], v2[:, g * HD:(g + 1) * HD]
                s = _nt(_stack_heads(q, g), kn) * (HD ** -0.5) + tab_ref[later, g]
                e, es = _exp_scores(s, _sink_column(g, sk_ref))
                eb = _c(e)
                o = _nn(eb, vh) * (1.0 / (_nn(eb, ones) + es))
                for hh in range(GQA):
                    o_ref[pl.ds(b * BLK, BLK), pl.ds((g * GQA + hh) * HD, HD)] = o[hh * BLK:(hh + 1) * BLK]

    cur = lambda col: (lambda b, j: (b * NS + j, col))
    prev = lambda col: (lambda b, j: (qb * (b * NS + j) - jnp.minimum(j, 1), col))
    return _run("attn_fwd", body, (BL, NS), [qkv, qkv, qkv, qkv, qkv, sink_rows, _swa_bias_table()],
                [pl.BlockSpec((qb * BLK, ATT_W), cur(0)),
                 pl.BlockSpec((qb * BLK, KV_W), cur(4)), pl.BlockSpec((BLK, KV_W), prev(4)),
                 pl.BlockSpec((qb * BLK, KV_W), cur(5)), pl.BlockSpec((BLK, KV_W), prev(5)),
                 pl.BlockSpec((8, 128), lambda b, j: (0, 0)), VM],
                [SDS((T, ATT_W), f32)], [pl.BlockSpec((qb * BLK, ATT_W), cur(0))], exchange=exchange)


def _conv_taps(u, uh):
    row = lax.broadcasted_iota(jnp.int32, u.shape, 0)
    u1 = jnp.where(row == 0, uh[7:8, :], pltpu.roll(u, 1, 0))
    u2 = jnp.where(row == 0, uh[6:7, :], jnp.where(row == 1, uh[7:8, :], pltpu.roll(u, 2, 0)))
    return u1, u2


def _mem_head(qm, km, vm, h):
    qh, kh, vh = (a[:, h * HD:(h + 1) * HD] for a in (qm, km, vm))
    e, _ = _exp_scores(_nt(qh, kh) * (HD ** -0.5))
    return qh, kh, vh, e


def mixer_tail_fwd(x2d, attn_out, proj, qkv, km, vm, conv_w8, pk, wout, S, tm, exchange):
    T, D = x2d.shape
    NM = km.shape[0] // (T // S)

    def body(x_ref, ao_ref, ch_ref, cb_ref, cc_ref, chh_ref, cch_ref, qm_ref, km_ref, vm_ref, cw_ref, pk_ref,
             wout_ref, co_ref, mo_ref, mg_ref, x1_ref, h_ref):
        first = (pl.program_id(0) * tm) % S == 0
        u = cc_ref[...] * ch_ref[...]
        uh = jnp.where(first, 0.0, cch_ref[...] * chh_ref[...])
        u1, u2 = _conv_taps(u, uh)
        conv = cw_ref[0:1, :] * u2 + cw_ref[1:2, :] * u1 + cw_ref[2:3, :] * u + _small(pk_ref, "conv_b")
        conv_out = cb_ref[...] * conv
        co_ref[...] = conv_out
        qm, kmv, vmv = qm_ref[...], km_ref[...], vm_ref[...]
        ones = jnp.ones((NM, HD), MXU)
        for h in range(N_MEMH):
            _, _, vh, e = _mem_head(qm, kmv, vmv, h)
            eb = _c(e)
            mo_ref[:, pl.ds(h * HD, HD)] = _nn(eb, vh) * (1.0 / _nn(eb, ones))
        mem_out = mo_ref[...]
        ao = ao_ref[...]
        merged = _c(jnp.concatenate([ao * _rstd(ao) * _small(pk_ref, "out_norm_attn"),
                                     conv_out * _rstd(conv_out) * _small(pk_ref, "out_norm_conv"),
                                     mem_out * _rstd(mem_out) * _small(pk_ref, "out_norm_mem")], axis=1))
        mg_ref[...] = merged
        x1 = x_ref[...] + _nn(merged, wout_ref[...])
        x1_ref[...] = x1
        h_ref[...] = _c(x1 * _rstd(x1) * _small(pk_ref, "norm_ffn"))

    tile = lambda w, col: pl.BlockSpec((tm, w), lambda i: (i, col))
    halo = lambda col: pl.BlockSpec((8, CONV_W), lambda i: (jnp.maximum(i * (tm // 8) - 1, 0), col))
    seq = pl.BlockSpec((NM, MEM_W), lambda i: ((i * tm) // S, 0))
    small = lambda a: pl.BlockSpec(a.shape, lambda i: (0, 0))
    return _run("mixer_tail_fwd", body, (T // tm,),
                [x2d, attn_out, proj, proj, proj, proj, proj, qkv, km, vm, conv_w8, pk, wout],
                [tile(D, 0), tile(ATT_W, 0), tile(CONV_W, 3), tile(CONV_W, 4), tile(CONV_W, 5), halo(3), halo(5),
                 tile(MEM_W, 3), seq, seq, VM, VM, VM],
                [SDS((T, CONV_W), f32), SDS((T, MEM_W), f32), SDS((T, D), MXU), SDS((T, D), f32), SDS((T, D), MXU)],
                [tile(CONV_W, 0), tile(MEM_W, 0), tile(D, 0), tile(D, 0), tile(D, 0)], vmem_mib=40, exchange=exchange)


def ffn_fwd_bwd(h, x1, tgt, wgT, wuT, wd, pk, tm):
    T, D = x1.shape
    F = wd.shape[0]

    def body(h_ref, x1_ref, t_ref, wg_ref, wu_ref, wd_ref, pk_ref,
             dx1_ref, dx2_ref, act_ref, dg_ref, du_ref, loss_ref, dgf_ref):
        @pl.when(pl.program_id(0) == 0)
        def _():
            loss_ref[...] = jnp.zeros_like(loss_ref)
            dgf_ref[...] = jnp.zeros_like(dgf_ref)

        hv = h_ref[...]
        gate = _nt(hv, wg_ref[...])
        up = _nt(hv, wu_ref[...])
        sg = jax.nn.sigmoid(gate)
        sl = gate * sg
        act = _c(sl * up)
        act_ref[...] = act
        x1v = x1_ref[...]
        diff = (x1v + _nn(act, wd_ref[...])) - t_ref[...]
        loss_ref[...] += 0.5 * jnp.sum(jnp.sum(diff * diff, axis=-1, keepdims=True) / D, axis=0, keepdims=True)
        dx2 = diff / D
        dx2b = _c(dx2)
        dx2_ref[...] = dx2b
        d_act = _nt(dx2b, wd_ref[...])
        d_up = _c(d_act * sl)
        d_gate = _c(d_act * up * (sg * (1.0 + gate * (1.0 - sg))))
        du_ref[...] = d_up
        dg_ref[...] = d_gate
        dh = _nn(d_gate, wg_ref[...]) + _nn(d_up, wu_ref[...])
        dv, dgf = _norm_bwd(dh, x1v, _rstd(x1v), _small(pk_ref, "norm_ffn"))
        dx1_ref[...] = dx2 + dv
        dgf_ref[...] += dgf

    tile = lambda w: pl.BlockSpec((tm, w), lambda i: (i, 0))
    return _run("ffn_fwd_bwd", body, (T // tm,), [h, x1, tgt, wgT, wuT, wd, pk],
                [tile(D), tile(D), tile(D), VM, VM, VM, VM],
                [SDS((T, D), f32), SDS((T, D), MXU), SDS((T, F), MXU), SDS((T, F), MXU), SDS((T, F), MXU),
                 SDS((8, 128), f32), SDS((1, D), f32)],
                [tile(D), tile(D), tile(F), tile(F), tile(F), pl.BlockSpec((8, 128), lambda i: (0, 0)),
                 pl.BlockSpec((1, D), lambda i: (0, 0))], vmem_mib=56)


def matmul_tn(a, b, name, tmo, tk):
    T, M = a.shape
    N = b.shape[1]

    def body(a_ref, b_ref, o_ref):
        @pl.when(pl.program_id(1) == 0)
        def _():
            o_ref[...] = jnp.zeros_like(o_ref)

        o_ref[...] += _tn(a_ref[...], b_ref[...])

    return _run(name, body, (M // tmo, T // tk), [a, b],
                [pl.BlockSpec((tk, tmo), lambda m, k: (k, m)), pl.BlockSpec((tk, N), lambda m, k: (k, 0))],
                [SDS((M, N), f32)], [pl.BlockSpec((tmo, N), lambda m, k: (m, 0))], vmem_mib=48)[0]


def out_proj_bwd(dx1, merged, attn_out, conv_out, mem_out, pk, wout, tm):
    T, D = dx1.shape

    def body(dx1_ref, mg_ref, ao_ref, co_ref, mo_ref, pk_ref, w_ref,
             dao_ref, dco_ref, dmo_ref, dw_ref, dgain_ref):
        @pl.when(pl.program_id(0) == 0)
        def _():
            dw_ref[...] = jnp.zeros_like(dw_ref)
            dgain_ref[...] = jnp.zeros_like(dgain_ref)

        dxb = _c(dx1_ref[...])
        dw_ref[...] += _tn(mg_ref[...], dxb)
        dmg = _nt(dxb, w_ref[...])
        ao, co, mo = ao_ref[...], co_ref[...], mo_ref[...]
        da, ga = _norm_bwd(dmg[:, :ATT_W], ao, _rstd(ao), _small(pk_ref, "out_norm_attn"))
        dc, gc = _norm_bwd(dmg[:, ATT_W:ATT_W + CONV_W], co, _rstd(co), _small(pk_ref, "out_norm_conv"))
        dm, gm = _norm_bwd(dmg[:, ATT_W + CONV_W:], mo, _rstd(mo), _small(pk_ref, "out_norm_mem"))
        dao_ref[...] = da
        dco_ref[...] = dc
        dmo_ref[...] = dm
        dgain_ref[...] += jnp.concatenate([ga, gc, gm], axis=1)

    tile = lambda w: pl.BlockSpec((tm, w), lambda i: (i, 0))
    return _run("out_proj_bwd", body, (T // tm,), [dx1, merged, attn_out, conv_out, mem_out, pk, wout],
                [tile(D), tile(D), tile(ATT_W), tile(CONV_W), tile(MEM_W), VM, VM],
                [SDS((T, ATT_W), f32), SDS((T, CONV_W), f32), SDS((T, MEM_W), f32), SDS((D, D), f32), SDS((1, D), f32)],
                [tile(ATT_W), tile(CONV_W), tile(MEM_W), pl.BlockSpec((D, D), lambda i: (0, 0)),
                 pl.BlockSpec((1, D), lambda i: (0, 0))], vmem_mib=40)


def attn_bwd(qkv, d_attn, attn_out, sink_rows, BL, S, exchange):
    NB = S // BLK
    T = BL * S

    def body(q_ref, kc_ref, kp_ref, vc_ref, vp_ref, do_ref, ao_ref, sk_ref, tab_ref,
             dq_ref, dk_ref, dv_ref, dsk_ref, pend_k, pend_v):
        b, j = pl.program_id(0), pl.program_id(1)

        @pl.when((b == 0) & (j == 0))
        def _():
            dsk_ref[...] = jnp.zeros_like(dsk_ref)

        @pl.when(j == 0)
        def _():
            pend_k[...] = jnp.zeros_like(pend_k)
            pend_v[...] = jnp.zeros_like(pend_v)

        @pl.when(j < NB)
        def _():
            q, do, ao = q_ref[...], do_ref[...], ao_ref[...]
            k2 = jnp.concatenate([kp_ref[...], kc_ref[...]], axis=0)
            v2 = jnp.concatenate([vp_ref[...], vc_ref[...]], axis=0)
            lane = lax.broadcasted_iota(jnp.int32, (8, 128), 1)
            ones_w = jnp.ones((2 * BLK, 2 * BLK), MXU)
            dsk = jnp.zeros((8, 128), f32)
            dks, dvs = [], []
            for g in range(N_KV):
                kn, vh = k2[:, g * HD:(g + 1) * HD], v2[:, g * HD:(g + 1) * HD]
                qs = _stack_heads(q, g)
                s = _nt(qs, kn) * (HD ** -0.5) + tab_ref[g]
                e, es = _exp_scores(s, _sink_column(g, sk_ref))
                eb = _c(e)
                inv_w = 1.0 / (_nn(eb, ones_w) + es)
                inv_n = inv_w[:, :HD]
                dos = _stack_heads(do, g)
                delta = _rowsum_mxu(dos * _stack_heads(ao, g), 2 * BLK)
                dp = _nt(_c(dos), vh)
                ds = _c(e * inv_w * (dp - delta) * (HD ** -0.5))
                t = es * inv_n[:, 0:1] * delta[:, 0:1]
                for hh in range(GQA):
                    dsk = dsk + jnp.where(lane == g * GQA + hh, -jnp.sum(t[hh * BLK:(hh + 1) * BLK]), 0.0)
                dvs.append(_tn(eb, _c(dos * inv_n)))
                dks.append(_tn(ds, qs))
                dqs = _nn(ds, kn)
                for hh in range(GQA):
                    dq_ref[:, pl.ds((g * GQA + hh) * HD, HD)] = dqs[hh * BLK:(hh + 1) * BLK]
            dk2 = jnp.concatenate(dks, axis=1)
            dv2 = jnp.concatenate(dvs, axis=1)
            dk_ref[...] = pend_k[...] + dk2[:BLK]
            dv_ref[...] = pend_v[...] + dv2[:BLK]
            pend_k[...] = dk2[BLK:]
            pend_v[...] = dv2[BLK:]
            dsk_ref[...] += dsk

        @pl.when(j == NB)
        def _():
            dk_ref[...] = pend_k[...]
            dv_ref[...] = pend_v[...]

    cur = lambda col: (lambda b, j: (b * NB + jnp.minimum(j, NB - 1), col))
    prev = lambda col: (lambda b, j: (b * NB + jnp.maximum(j - 1, 0), col))
    small = lambda shape: pl.BlockSpec(shape, lambda b, j: (0, 0))
    return _run("attn_bwd", body, (BL, NB + 1), [qkv, qkv, qkv, qkv, qkv, d_attn, attn_out, sink_rows, _swa_bias_table()],
                [pl.BlockSpec((BLK, ATT_W), cur(0)),
                 pl.BlockSpec((BLK, KV_W), cur(4)), pl.BlockSpec((BLK, KV_W), prev(4)),
                 pl.BlockSpec((BLK, KV_W), cur(5)), pl.BlockSpec((BLK, KV_W), prev(5)),
                 pl.BlockSpec((BLK, ATT_W), cur(0)), pl.BlockSpec((BLK, ATT_W), cur(0)), small((8, 128)),
                 pl.BlockSpec((None, N_KV, GQA * BLK, 2 * BLK), lambda b, j: (jnp.minimum(j, 1), 0, 0, 0))],
                [SDS((T, ATT_W), f32), SDS((T, KV_W), f32), SDS((T, KV_W), f32), SDS((8, 128), f32)],
                [pl.BlockSpec((BLK, ATT_W), cur(0)), pl.BlockSpec((BLK, KV_W), prev(0)),
                 pl.BlockSpec((BLK, KV_W), prev(0)), small((8, 128))],
                scratch=[pltpu.VMEM((BLK, KV_W), f32)] * 2, vmem_mib=56, exchange=exchange)


def mem_conv_bwd(d_mem_out, mem_out, d_conv_out, proj, qkv, km, vm, conv_w8, pk, S, tm, exchange):
    T = d_mem_out.shape[0]
    NM = km.shape[0] // (T // S)

    def body(dmo_ref, mo_ref, dco_ref, ch_ref, cb_ref, cc_ref, chh_ref, cch_ref, qm_ref, km_ref, vm_ref, cw_ref,
             pk_ref, dqm_ref, dkm_ref, dvm_ref, dcb_ref, dcv_ref, dcw_ref, dcbias_ref):
        i = pl.program_id(0)
        first = (i * tm) % S == 0

        @pl.when(i == 0)
        def _():
            dcw_ref[...] = jnp.zeros_like(dcw_ref)
            dcbias_ref[...] = jnp.zeros_like(dcbias_ref)

        @pl.when(first)
        def _():
            dkm_ref[...] = jnp.zeros_like(dkm_ref)
            dvm_ref[...] = jnp.zeros_like(dvm_ref)

        qm, kmv, vmv, dmo, mo = qm_ref[...], km_ref[...], vm_ref[...], dmo_ref[...], mo_ref[...]
        ones_w = jnp.ones((NM, NM), MXU)
        for h in range(N_MEMH):
            qh, kh, vh, e = _mem_head(qm, kmv, vmv, h)
            eb = _c(e)
            doh = dmo[:, h * HD:(h + 1) * HD]
            delta = _rowsum_mxu(doh * mo[:, h * HD:(h + 1) * HD], NM)
            dp = _nt(_c(doh), vh)
            inv_w = 1.0 / _nn(eb, ones_w)
            ds = _c(e * inv_w * (dp - delta) * (HD ** -0.5))
            dvm_ref[:, pl.ds(h * HD, HD)] += _tn(eb, _c(doh * inv_w[:, :HD]))
            dkm_ref[:, pl.ds(h * HD, HD)] += _tn(ds, qh)
            dqm_ref[:, pl.ds(h * HD, HD)] = _nn(ds, kh)

        u = cc_ref[...] * ch_ref[...]
        uh = jnp.where(first, 0.0, cch_ref[...] * chh_ref[...])
        u1, u2 = _conv_taps(u, uh)
        conv = cw_ref[0:1, :] * u2 + cw_ref[1:2, :] * u1 + cw_ref[2:3, :] * u + _small(pk_ref, "conv_b")
        dy = dco_ref[...]
        dcb_ref[...] = dy * conv
        dcv = dy * cb_ref[...]
        dcv_ref[...] = dcv
        dcbias_ref[...] += jnp.sum(dcv, axis=0, keepdims=True)
        dcw_ref[0:1, :] += jnp.sum(dcv * u2, axis=0, keepdims=True)
        dcw_ref[1:2, :] += jnp.sum(dcv * u1, axis=0, keepdims=True)
        dcw_ref[2:3, :] += jnp.sum(dcv * u, axis=0, keepdims=True)

    tile = lambda w, col: pl.BlockSpec((tm, w), lambda i: (i, col))
    halo = lambda col: pl.BlockSpec((8, CONV_W), lambda i: (jnp.maximum(i * (tm // 8) - 1, 0), col))
    seq = pl.BlockSpec((NM, MEM_W), lambda i: ((i * tm) // S, 0))
    const = lambda shape: pl.BlockSpec(shape, lambda i: (0, 0))
    return _run("mem_conv_bwd", body, (T // tm,),
                [d_mem_out, mem_out, d_conv_out, proj, proj, proj, proj, proj, qkv, km, vm, conv_w8, pk],
                [tile(MEM_W, 0), tile(MEM_W, 0), tile(CONV_W, 0), tile(CONV_W, 3), tile(CONV_W, 4), tile(CONV_W, 5),
                 halo(3), halo(5), tile(MEM_W, 3), seq, seq, VM, VM],
                [SDS((T, MEM_W), f32), SDS(km.shape, f32), SDS(km.shape, f32),
                 SDS((T, CONV_W), f32), SDS((T, CONV_W), f32), SDS((8, CONV_W), f32), SDS((1, CONV_W), f32)],
                [tile(MEM_W, 0), seq, seq, tile(CONV_W, 0), tile(CONV_W, 0), const((8, CONV_W)), const((1, CONV_W))],
                vmem_mib=48, exchange=exchange)


def in_proj_bwd(dqn, dkn, dv, dcb, dcv, dqmn, proj, conv_w8, xn, x2d, dx1, pk, winT, S, tm, stages, ws, ms, vs):
    T, D = x2d.shape
    P = winT.shape[0]
    last_blk = T // 8 - 1
    n = len(stages)
    nsteps = T // tm
    tile_w = ws[0].shape[1] // (nsteps // 2)
    turn = [e * 2 // n for e in range(n)]

    def body(dq_ref, dk_ref, dv_ref, dcb_ref, dcv_ref, dcvn_ref, dqm_ref, qa_ref, ka_ref, ch_ref, cc_ref, qma_ref,
             cw_ref, xn_ref, x_ref, dx1_ref, pk_ref, w_ref, *rest):
        st, aw, am, av = (rest[k * n:(k + 1) * n] for k in range(4))
        dx_ref, dw_ref, dg_ref, dqg_ref, dkg_ref, dmqg_ref = rest[4 * n:4 * n + 6]
        aouts = rest[4 * n + 6:]
        i = pl.program_id(0)

        for parity in range(2):
            @pl.when(i % 2 == parity)
            def _(parity=parity):
                for e in range(n):
                    if turn[e] == parity:
                        g = jnp.concatenate([_sum_chips(st[e].at[0]), _sum_chips(st[e].at[1])], axis=0)
                        d, mm, vv = _adamw_math(aw[e][...], g, am[e][...], av[e][...])
                        for k, val in enumerate((g, d, mm, vv)):
                            aouts[4 * e + k][...] = val

        @pl.when(i == 0)
        def _():
            dw_ref[...] = jnp.zeros_like(dw_ref)
            dg_ref[...] = jnp.zeros_like(dg_ref)
            dqg_ref[...] = jnp.zeros_like(dqg_ref)
            dkg_ref[...] = jnp.zeros_like(dkg_ref)
            dmqg_ref[...] = jnp.zeros_like(dmqg_ref)

        dqa, gq = _heads_norm_bwd(dq_ref[...], qa_ref[...], _small(pk_ref, "q_norm"))
        dka, gk = _heads_norm_bwd(dk_ref[...], ka_ref[...], _small(pk_ref, "k_norm"))
        dqma, gmq = _heads_norm_bwd(dqm_ref[...], qma_ref[...], _small(pk_ref, "mem_q_norm"))
        dqg_ref[...] += gq
        dkg_ref[...] += gk
        dmqg_ref[...] += gmq

        last = ((i + 1) * tm) % S == 0
        dcv = dcv_ref[...]
        nxt = jnp.where(last, 0.0, dcvn_ref[...])
        row = lax.broadcasted_iota(jnp.int32, dcv.shape, 0)
        n1 = jnp.where(row == tm - 1, nxt[0:1, :], pltpu.roll(dcv, tm - 1, 0))
        n2 = jnp.where(row == tm - 2, nxt[0:1, :], jnp.where(row == tm - 1, nxt[1:2, :], pltpu.roll(dcv, tm - 2, 0)))
        du = cw_ref[2:3, :] * dcv + cw_ref[1:2, :] * n1 + cw_ref[0:1, :] * n2
        d_proj = jnp.concatenate([_c(dqa), _c(dka), _c(dv_ref[...]), _c(du * cc_ref[...]),
                                  _c(dcb_ref[...]), _c(du * ch_ref[...]), _c(dqma)], axis=1)
        dw_ref[...] += _tn(d_proj, xn_ref[...])
        xv = x_ref[...]
        dv_, dg = _norm_bwd(_nn(d_proj, w_ref[...]), xv, _rstd(xv), _small(pk_ref, "norm_mix"))
        dx_ref[...] = dx1_ref[...] + dv_
        dg_ref[...] += dg

    tile = lambda w, col=0: pl.BlockSpec((tm, w), lambda i: (i, col))
    nhalo = pl.BlockSpec((8, CONV_W), lambda i: (jnp.minimum((i + 1) * (tm // 8), last_blk), 0))
    const = lambda shape: pl.BlockSpec(shape, lambda i: (0, 0))
    st_specs = [pl.BlockSpec((2, 4, s.shape[2], tile_w), lambda i: (0, 0, 0, i // 2)) for s in stages]
    w_specs = [pl.BlockSpec((w.shape[0], tile_w), lambda i: (0, i // 2)) for w in ws]
    res = _run("in_proj_bwd", body, (nsteps,),
               [dqn, dkn, dv, dcb, dcv, dcv, dqmn, proj, proj, proj, proj, proj, conv_w8, xn, x2d, dx1, pk, winT]
               + list(stages) + list(ws) + list(ms) + list(vs),
               [tile(ATT_W), tile(KV_W), tile(KV_W), tile(CONV_W), tile(CONV_W), nhalo, tile(MEM_W),
                tile(ATT_W, 0), tile(KV_W, 4), tile(CONV_W, 3), tile(CONV_W, 5), tile(MEM_W, 6), VM,
                tile(D), tile(D), tile(D), VM, VM] + st_specs + w_specs * 3,
               [SDS((T, D), f32), SDS((P, D), f32), SDS((1, D), f32), SDS((1, HD), f32), SDS((1, HD), f32),
                SDS((1, HD), f32)] + [SDS(w.shape, f32) for w in ws for _ in range(4)],
               [tile(D), pl.BlockSpec((P, D), lambda i: (0, 0)), const((1, D)), const((1, HD)), const((1, HD)),
                const((1, HD))] + [s for s in w_specs for _ in range(4)],
               vmem_mib=56)
    return res[:6], [res[6 + 4 * e:10 + 4 * e] for e in range(n)]


def mem_kv_bwd(dkm, dvm, kv, memn, mem2d, pk, wmkv):
    def body(dkm_ref, dvm_ref, kv_ref, mn_ref, m_ref, pk_ref, w_ref, dw_ref, dg_ref, dkg_ref):
        dkk, dkg = _heads_norm_bwd(dkm_ref[...], kv_ref[:, :MEM_W], _small(pk_ref, "mem_k_norm"))
        dkg_ref[...] = dkg
        dkv = _c(jnp.concatenate([dkk, dvm_ref[...]], axis=1))
        dw_ref[...] = _tn(mn_ref[...], dkv)
        mv = m_ref[...]
        dg_ref[...] = jnp.sum(_nt(dkv, w_ref[...]) * mv * _rstd(mv), axis=0, keepdims=True)

    return _run("mem_kv_bwd", body, (), [dkm, dvm, kv, memn, mem2d, pk, wmkv], [VM] * 7,
                [SDS(wmkv.shape, f32), SDS((1, mem2d.shape[1]), f32), SDS((1, HD), f32)], [VM] * 3, vmem_mib=40)


def _halves_view(g):
    return g.reshape(4, 2, g.shape[0] // 8, g.shape[1])


def kernel(x, mem, norm_mix, w_in, q_norm, k_norm, attn_sinks, conv_w, conv_b, norm_mem, w_mem_kv, mem_q_norm, mem_k_norm, out_norm_attn, out_norm_conv, out_norm_mem, w_out, norm_ffn, w_gate, w_up, w_down, loss_target, m_norm_mix, m_w_in, m_q_norm, m_k_norm, m_attn_sinks, m_conv_w, m_conv_b, m_norm_mem, m_w_mem_kv, m_mem_q_norm, m_mem_k_norm, m_out_norm_attn, m_out_norm_conv, m_out_norm_mem, m_w_out, m_norm_ffn, m_w_gate, m_w_up, m_w_down, v_norm_mix, v_w_in, v_q_norm, v_k_norm, v_attn_sinks, v_conv_w, v_conv_b, v_norm_mem, v_w_mem_kv, v_mem_q_norm, v_mem_k_norm, v_out_norm_attn, v_out_norm_conv, v_out_norm_mem, v_w_out, v_norm_ffn, v_w_gate, v_w_up, v_w_down):
    BL, S, D = x.shape
    T = BL * S
    TM = 256
    TM_BIG = min(512, S)
    w_small = dict(norm_mix=norm_mix, norm_mem=norm_mem, norm_ffn=norm_ffn, out_norm_attn=out_norm_attn,
                   out_norm_conv=out_norm_conv, out_norm_mem=out_norm_mem, conv_w=conv_w, conv_b=conv_b, q_norm=q_norm,
                   k_norm=k_norm, mem_q_norm=mem_q_norm, mem_k_norm=mem_k_norm, attn_sinks=attn_sinks)
    m_small = dict(norm_mix=m_norm_mix, norm_mem=m_norm_mem, norm_ffn=m_norm_ffn, out_norm_attn=m_out_norm_attn,
                   out_norm_conv=m_out_norm_conv, out_norm_mem=m_out_norm_mem, conv_w=m_conv_w, conv_b=m_conv_b,
                   q_norm=m_q_norm, k_norm=m_k_norm, mem_q_norm=m_mem_q_norm, mem_k_norm=m_mem_k_norm,
                   attn_sinks=m_attn_sinks)
    v_small = dict(norm_mix=v_norm_mix, norm_mem=v_norm_mem, norm_ffn=v_norm_ffn, out_norm_attn=v_out_norm_attn,
                   out_norm_conv=v_out_norm_conv, out_norm_mem=v_out_norm_mem, conv_w=v_conv_w, conv_b=v_conv_b,
                   q_norm=v_q_norm, k_norm=v_k_norm, mem_q_norm=v_mem_q_norm, mem_k_norm=v_mem_k_norm,
                   attn_sinks=v_attn_sinks)
    pk = _pack_small(w_small)

    rowblocks = lambda a, b, c, d, e, f: [a[0].T, b[0].T, c[0].T, d[0], e[0], f[0]]
    w_rb = rowblocks(w_in, w_gate, w_up, w_down, w_out, w_mem_kv)
    m_rb = rowblocks(m_w_in, m_w_gate, m_w_up, m_w_down, m_w_out, m_w_mem_kv)
    v_rb = rowblocks(v_w_in, v_w_gate, v_w_up, v_w_down, v_w_out, v_w_mem_kv)
    (winT_s,) = prep_weights("prep_w_in", w_rb[:1])
    cw_pad = jnp.zeros((8, 128), f32).at[:3, :HD].set(conv_w[0])
    (wgT_s, wuT_s, wd_s, wout_s, wmkv_s), (winT, cw_all) = prep_weights(
        "gather_w_in", w_rb[1:], _together([gather_two_legs([winT_s]), gather_exchange([cw_pad], [False])]))
    conv_w_full = jnp.transpose(cw_all.reshape(4, 8, 128)[:, :3, :HD], (1, 0, 2)).reshape(3, CONV_W)
    conv_w8 = jnp.zeros((8, CONV_W), f32).at[:3].set(conv_w_full)
    sink_rows = jnp.broadcast_to(attn_sinks.reshape(N_Q, 1), (N_Q, 128))

    x2d = x.reshape(T, D)
    mem2d = mem.reshape(-1, D)
    (xn, proj, qkv), near1 = in_proj_fwd(x2d, pk, winT, TM_BIG, gather_near_exchange([wgT_s, wout_s, wmkv_s], relay_early=1))
    (attn_out,), (wgT, wout, wmkv, *near2) = attn_fwd(
        qkv, sink_rows, BL, S, _together([gather_far_exchange(near1, relay_early=2), gather_near_exchange([wuT_s, wd_s], relay_early=2)]))
    memn, kv, km, vm = mem_kv_fwd(mem2d, pk, wmkv)
    (conv_out, mem_out, merged, x1, h), (wuT, wd) = mixer_tail_fwd(
        x2d, attn_out, proj, qkv, km, vm, conv_w8, pk, wout, S, TM_BIG, gather_far_exchange(near2, relay_early=2))

    dx1, dx2b, act, d_gate, d_up, loss8, d_norm_ffn = ffn_fwd_bwd(h, x1, loss_target.reshape(T, D), wgT, wuT, wd, pk, TM)
    F = wd.shape[0]
    g_wd = matmul_tn(act, dx2b, "dw_down", F // 2, min(T, 1024))
    g_wgT = matmul_tn(d_gate, h, "dw_gate", F // 2, min(T, 1024))
    g_wuT = matmul_tn(d_up, h, "dw_up", F // 2, min(T, 1024))

    d_attn, d_conv_out, d_mem_out, g_wout, d_gains = out_proj_bwd(dx1, merged, attn_out, conv_out, mem_out, pk, wout, min(1024, S))
    dqmn, dkm, dvm, dcb, dcv, d_cw8, d_cbias = mem_conv_bwd(
        d_mem_out, mem_out, d_conv_out, proj, qkv, km, vm, conv_w8, pk, S, min(1024, S), None)
    (dqn, dkn, dv, d_sink8), (st_wout, st_wgT, st_wuT, st_wd) = attn_bwd(
        qkv, d_attn, attn_out, sink_rows, BL, S,
        reduce_scatter_exchange([_halves_view(g) for g in (g_wout, g_wgT, g_wuT, g_wd)], BL * (S // BLK + 1),
                                load_step=[0, 1, 4, 7], send_step=[1, 4, 7, 10], relay_step=[6, 16, 25, 33]))
    (g_x, g_winT, d_norm_mix, d_qg, d_kg, d_mqg), late_res = in_proj_bwd(
        dqn, dkn, dv, dcb, dcv, dqmn, proj, conv_w8, xn, x2d, dx1, pk, winT, S, TM,
        [st_wgT, st_wuT, st_wd, st_wout], w_rb[1:5], m_rb[1:5], v_rb[1:5])
    g_wmkv, d_norm_mem, d_mkg = mem_kv_bwd(dkm, dvm, kv, memn, mem2d, pk, wmkv)

    tot, tail_stage = tail_reduce(d_norm_mix, d_norm_mem, d_norm_ffn, d_gains, d_cw8, d_cbias, d_qg, d_kg, d_mqg, d_mkg,
                                  d_sink8, loss8, [_halves_view(g) for g in (g_winT, g_wmkv)])
    loss = tot[5, 384]
    tail_res, _ = adamw_big("adamw_tail", tail_stage, [w_rb[0], w_rb[5]], [m_rb[0], m_rb[5]], [v_rb[0], v_rb[5]], 4)
    res = {"w_in": [a.T[None] for a in tail_res[0]], "w_gate": [a.T[None] for a in late_res[0]],
           "w_up": [a.T[None] for a in late_res[1]], "w_down": [a[None] for a in late_res[2]],
           "w_out": [a[None] for a in late_res[3]], "w_mem_kv": [a[None] for a in tail_res[1]]}
    res.update(adamw_small(tot, pk, _pack_small(m_small), _pack_small(v_small), {k: w_small[k].shape for k in SMALL}))

    order = ["norm_mix", "w_in", "q_norm", "k_norm", "attn_sinks", "conv_w", "conv_b", "norm_mem", "w_mem_kv",
             "mem_q_norm", "mem_k_norm", "out_norm_attn", "out_norm_conv", "out_norm_mem", "w_out", "norm_ffn",
             "w_gate", "w_up", "w_down"]
    return (loss, g_x.reshape(BL, S, D), *[res[n][0] for n in order], *[res[n][1] for n in order],
            *[res[n][2] for n in order], *[res[n][3] for n in order])
```

```python
import collections
import functools

import jax
import jax.numpy as jnp
import numpy as np
from jax import lax
from jax.experimental import pallas as pl
from jax.experimental.pallas import tpu as pltpu

f32 = jnp.float32
MXU = jnp.bfloat16
WIRE = jnp.bfloat16
EPS = 1e-6
NEG = -1e30
HD = 64
BLK = 128
N_Q, N_KV, N_MEMH = 8, 2, 4
GQA = N_Q // N_KV
ATT_W, KV_W, CONV_W, MEM_W = 512, 128, 256, 256
VMEM_MIB = 1024 * 1024
ADAM_LR, ADAM_B1, ADAM_B2, ADAM_EPS, ADAM_WD, ADAM_STEP = 0.001, 0.9, 0.999, 1e-08, 0.01, 10

MESH = pl.DeviceIdType.MESH
VM = pl.BlockSpec(memory_space=pltpu.VMEM)
ANY = pl.BlockSpec(memory_space=pl.ANY)
SDS = jax.ShapeDtypeStruct
DMA = pltpu.SemaphoreType.DMA


def _c(v):
    return v.astype(MXU)


def _nn(a, b):
    return lax.dot_general(a, b, (((1,), (0,)), ((), ())), preferred_element_type=f32)


def _nt(a, b):
    return lax.dot_general(a, b, (((1,), (1,)), ((), ())), preferred_element_type=f32)


def _tn(a, b):
    return lax.dot_general(a, b, (((0,), (0,)), ((), ())), preferred_element_type=f32)


def _rstd(v):
    return lax.rsqrt(jnp.mean(v * v, axis=-1, keepdims=True) + EPS)


def _norm_bwd(dy, v, r, g):
    dyg = dy * g
    dv = r * dyg - v * (r * r * r) * jnp.mean(dyg * v, axis=-1, keepdims=True)
    return dv, jnp.sum(dy * v * r, axis=0, keepdims=True)


def _split3(v):
    hi = _c(v)
    r1 = v - hi.astype(f32)
    mid = _c(r1)
    return hi, mid, _c(r1 - mid.astype(f32))


def _rowsum_mxu(v, width):
    ones = jnp.ones((v.shape[1], width), MXU)
    return sum(_nn(a, ones) for a in _split3(v))


def _seg_sums(v):
    r = lax.broadcasted_iota(jnp.int32, (2 * HD, 2 * HD), 0) // HD
    c = lax.broadcasted_iota(jnp.int32, (2 * HD, 2 * HD), 1) // HD
    bd = (r == c).astype(MXU)
    outs = []
    for b in range(v.shape[1] // (2 * HD)):
        outs.append(sum(_nn(a, bd) for a in _split3(v[:, b * 2 * HD:(b + 1) * 2 * HD])))
    return outs[0] if len(outs) == 1 else jnp.concatenate(outs, axis=1)


def _lanes(g, width):
    return jnp.concatenate([g] * (width // HD), axis=1)


def _heads_rstd(v):
    return lax.rsqrt(_seg_sums(v * v) * (1.0 / HD) + EPS)


def _heads_norm_bwd(dy, v, g):
    r = _heads_rstd(v)
    gl = _lanes(g, v.shape[1])
    dyg = dy * gl
    dv = r * dyg - v * (r * r * r) * (_seg_sums(dyg * v) * (1.0 / HD))
    dgl = jnp.sum(dy * v * r, axis=0, keepdims=True)
    return dv, sum(dgl[:, s * HD:(s + 1) * HD] for s in range(v.shape[1] // HD))


def _exp_scores(s, extra=None):
    m = jnp.max(s, axis=-1, keepdims=True)
    if extra is None:
        return jnp.exp(s - m), None
    m = jnp.maximum(m, extra)
    return jnp.exp(s - m), jnp.exp(extra - m)


def _place():
    return lax.axis_index("x"), lax.axis_index("y"), lax.axis_index("c")


SMALL_AT = {"norm_mix": (0, 0, 1024), "norm_mem": (1, 0, 1024), "norm_ffn": (2, 0, 1024),
            "out_norm_attn": (3, 0, ATT_W), "out_norm_conv": (3, ATT_W, CONV_W), "out_norm_mem": (3, ATT_W + CONV_W, MEM_W),
            "conv_b": (4, 3 * CONV_W, CONV_W), "q_norm": (5, 0, HD), "k_norm": (5, HD, HD), "mem_q_norm": (5, 2 * HD, HD),
            "mem_k_norm": (5, 3 * HD, HD), "attn_sinks": (5, 256, N_Q)}
SMALL = ("norm_mix", "norm_mem", "norm_ffn", "out_norm_attn", "out_norm_conv", "out_norm_mem", "conv_w", "conv_b",
         "q_norm", "k_norm", "mem_q_norm", "mem_k_norm", "attn_sinks")


def _small(pk_ref, name):
    r, c0, w = SMALL_AT[name]
    return pk_ref[r:r + 1, c0:c0 + w]


def _pack_small(d):
    z = lambda n: jnp.zeros((1, n), f32)
    row3 = jnp.concatenate([d["out_norm_attn"], d["out_norm_conv"], d["out_norm_mem"]], axis=1)
    row4 = jnp.concatenate([d["conv_w"].reshape(1, 3 * HD), z(3 * CONV_W - 3 * HD), d["conv_b"]], axis=1)
    row5 = jnp.concatenate([d["q_norm"], d["k_norm"], d["mem_q_norm"], d["mem_k_norm"], d["attn_sinks"],
                            z(1024 - 4 * HD - N_Q)], axis=1)
    return jnp.concatenate([d["norm_mix"], d["norm_mem"], d["norm_ffn"], row3, row4, row5, z(1024), z(1024)], axis=0)


def _other_chips(x, y):
    return [(1 - x, y), (x, 1 - y), (1 - x, 1 - y)]


Exchange = collections.namedtuple("Exchange", "ins outs sems start finish relays aliases", defaults=((), {}))


def _together(exchanges):
    def bounds(key):
        at, out = 0, []
        for ex in exchanges:
            out.append((at, at + len(getattr(ex, key))))
            at += len(getattr(ex, key))
        return out

    bi, bo, bs = bounds("ins"), bounds("outs"), bounds("sems")

    def of(i, fn):
        return lambda xa, xo, xs: fn(xa[bi[i][0]:bi[i][1]], xo[bo[i][0]:bo[i][1]], xs[bs[i][0]:bs[i][1]])

    def every(name):
        fns = [of(i, getattr(ex, name)) for i, ex in enumerate(exchanges)]

        def run(xa, xo, xs):
            for fn in fns:
                fn(xa, xo, xs)
        return run

    aliases = {}
    for i, ex in enumerate(exchanges):
        aliases.update({bi[i][0] + a: bo[i][0] + o for a, o in ex.aliases.items()})
    return Exchange([a for ex in exchanges for a in ex.ins], [o for ex in exchanges for o in ex.outs],
                    [s for ex in exchanges for s in ex.sems], every("start"), every("finish"),
                    [(sbe, of(i, fn)) for i, ex in enumerate(exchanges) for sbe, fn in ex.relays], aliases)


def _run(name, body, grid, ins, in_specs, out_shape, out_specs, scratch=(), vmem_mib=32, exchange=None):
    ins, in_specs, out_shape, out_specs, scratch = list(ins), list(in_specs), list(out_shape), list(out_specs), list(scratch)
    ni, no, ns = len(ins), len(out_shape), len(scratch)
    ex = exchange
    if ex is not None:
        nxi, nxo = len(ex.ins), len(ex.outs)

    def call_body(*refs):
        if ex is None:
            body(*refs)
            return
        a, xa = refs[:ni], refs[ni:ni + nxi]
        o, xo = refs[ni + nxi:ni + nxi + no], refs[ni + nxi + no:ni + nxi + no + nxo]
        s, xs = refs[ni + nxi + no + nxo:ni + nxi + no + nxo + ns], refs[ni + nxi + no + nxo + ns:]
        if grid:
            first = functools.reduce(jnp.logical_and, [pl.program_id(d) == 0 for d in range(len(grid))])
            last = functools.reduce(jnp.logical_and, [pl.program_id(d) == grid[d] - 1 for d in range(len(grid))])
            pl.when(first)(lambda: ex.start(xa, xo, xs))
            body(*a, *o, *s)
            nsteps = functools.reduce(lambda p, q: p * q, grid)
            for before_end, fn in ex.relays:
                at = np.unravel_index(max(nsteps - 1 - before_end, 0), grid)
                here = functools.reduce(jnp.logical_and, [pl.program_id(d) == int(at[d]) for d in range(len(grid))])
                pl.when(here)(functools.partial(fn, xa, xo, xs))
            pl.when(last)(lambda: ex.finish(xa, xo, xs))
        else:
            ex.start(xa, xo, xs)
            if body is not None:
                body(*a, *o, *s)
            for _, fn in ex.relays:
                fn(xa, xo, xs)
            ex.finish(xa, xo, xs)

    kw = dict(grid=grid) if grid else {}
    if ex is not None:
        if ex.aliases:
            kw["input_output_aliases"] = {ni + i: no + o for i, o in ex.aliases.items()}
        ins, in_specs = ins + list(ex.ins), in_specs + [ANY] * nxi
        out_shape, out_specs = out_shape + list(ex.outs), out_specs + [ANY] * nxo
        scratch = scratch + list(ex.sems)
    res = pl.pallas_call(
        call_body, name=name, out_shape=out_shape, in_specs=in_specs, out_specs=out_specs, scratch_shapes=scratch,
        compiler_params=pltpu.CompilerParams(dimension_semantics=("arbitrary",) * len(grid) if grid else None,
                                             vmem_limit_bytes=vmem_mib * VMEM_MIB), **kw)(*ins)
    res = list(res)
    return (res[:no], res[no:]) if ex is not None else res


def _remote(src, dst, ssem, rsem, dev):
    return pltpu.make_async_remote_copy(src_ref=src, dst_ref=dst, send_sem=ssem, recv_sem=rsem,
                                        device_id=dev, device_id_type=MESH)


def gather_exchange(shards, split, relay_early=0):
    n = len(shards)

    def rows(ref, e, kk, half=None):
        R = shards[e].shape[0]
        if half is None:
            return ref.at[pl.ds(pl.multiple_of(kk * R, 8), R)]
        return ref.at[pl.ds(pl.multiple_of(kk * R + half * (R // 2), 8), R // 2)]

    def ici(src, dst, sm, e, j, chip_j, x, y, c):
        k = 2 * x + y
        if split[e]:
            s = src[e].at[pl.ds(pl.multiple_of(c * (shards[e].shape[0] // 2), 8), shards[e].shape[0] // 2)]
            return _remote(s, rows(dst[e], e, k, c), sm[0].at[6 * e + j], sm[1].at[6 * e + j], (*chip_j, c))
        return _remote(src[e], rows(dst[e], e, k), sm[0].at[6 * e + j], sm[1].at[6 * e + j], (*chip_j, c))

    def landed(dst, e, chip_j, c):
        kj = 2 * chip_j[0] + chip_j[1]
        return rows(dst[e], e, kj, c) if split[e] else rows(dst[e], e, kj)

    def forward(dst, sm, e, j, chip_j, x, y, c, sender_c):
        kj = 2 * chip_j[0] + chip_j[1]
        r = rows(dst[e], e, kj, sender_c)
        return _remote(r, r, sm[0].at[6 * e + 3 + j], sm[1].at[6 * e + 3 + j], (x, y, 1 - c))

    def local(src, dst, sm, e, x, y):
        return pltpu.make_async_copy(src[e], rows(dst[e], e, 2 * x + y), sm[2].at[e])

    def start(src, dst, sm):
        x, y, c = _place()
        for e in range(n):
            local(src, dst, sm, e, x, y).start()
            for j, chip_j in enumerate(_other_chips(x, y)):
                ici(src, dst, sm, e, j, chip_j, x, y, c).start()

    def relay(src, dst, sm):
        x, y, c = _place()
        for e in range(n):
            for j, chip_j in enumerate(_other_chips(x, y)):
                r = landed(dst, e, chip_j, c)
                _remote(r, r, sm[0].at[6 * e + j], sm[1].at[6 * e + j], (*chip_j, c)).wait_recv()
                if split[e]:
                    forward(dst, sm, e, j, chip_j, x, y, c, c).start()

    def finish(src, dst, sm):
        x, y, c = _place()
        chips = _other_chips(x, y)
        for e in range(n):
            for j, chip_j in enumerate(chips):
                if split[e]:
                    forward(dst, sm, e, j, chip_j, x, y, c, 1 - c).wait_recv()
        for e in range(n):
            for j, chip_j in enumerate(chips):
                ici(src, dst, sm, e, j, chip_j, x, y, c).wait_send()
                if split[e]:
                    forward(dst, sm, e, j, chip_j, x, y, c, c).wait_send()
            local(src, dst, sm, e, x, y).wait()

    outs = [SDS((4 * s.shape[0], s.shape[1]), s.dtype) for s in shards]
    return Exchange(list(shards), outs, [DMA((6 * n,)), DMA((6 * n,)), DMA((n,))], start, finish, [(relay_early, relay)])


def _block_rows(ref, R, kk, half, quarter=None):
    hr = R // 2
    if quarter is None:
        return ref.at[pl.ds(pl.multiple_of(kk * R + half * hr, 8), hr)]
    return ref.at[pl.ds(pl.multiple_of(kk * R + half * hr + quarter * (hr // 2), 8), hr // 2)]


def gather_near_exchange(shards, relay_early=0):
    n = len(shards)
    R = [s.shape[0] for s in shards]

    def ici(src, dst, sm, e, j, chip_j, x, y, c):
        half = src[e].at[pl.ds(pl.multiple_of(c * (R[e] // 2), 8), R[e] // 2)]
        return _remote(half, _block_rows(dst[e], R[e], 2 * x + y, c), sm[0].at[4 * e + j], sm[1].at[4 * e + j], (*chip_j, c))

    def forward(dst, sm, e, j, chip_j, x, y, c, sender_c):
        r = _block_rows(dst[e], R[e], 2 * chip_j[0] + chip_j[1], sender_c)
        return _remote(r, r, sm[0].at[4 * e + 2 + j], sm[1].at[4 * e + 2 + j], (x, y, 1 - c))

    def local(src, dst, sm, e, x, y):
        return pltpu.make_async_copy(src[e], dst[e].at[pl.ds(pl.multiple_of((2 * x + y) * R[e], 8), R[e])], sm[2].at[e])

    def start(src, dst, sm):
        x, y, c = _place()
        for e in range(n):
            local(src, dst, sm, e, x, y).start()
            for j, chip_j in enumerate(_other_chips(x, y)[:2]):
                ici(src, dst, sm, e, j, chip_j, x, y, c).start()

    def relay(src, dst, sm):
        x, y, c = _place()
        for e in range(n):
            for j, chip_j in enumerate(_other_chips(x, y)[:2]):
                r = _block_rows(dst[e], R[e], 2 * chip_j[0] + chip_j[1], c)
                _remote(r, r, sm[0].at[4 * e + j], sm[1].at[4 * e + j], (*chip_j, c)).wait_recv()
                forward(dst, sm, e, j, chip_j, x, y, c, c).start()

    def finish(src, dst, sm):
        x, y, c = _place()
        near = _other_chips(x, y)[:2]
        for e in range(n):
            for j, chip_j in enumerate(near):
                forward(dst, sm, e, j, chip_j, x, y, c, 1 - c).wait_recv()
        for e in range(n):
            for j, chip_j in enumerate(near):
                ici(src, dst, sm, e, j, chip_j, x, y, c).wait_send()
                forward(dst, sm, e, j, chip_j, x, y, c, c).wait_send()
            local(src, dst, sm, e, x, y).wait()

    outs = [SDS((4 * s.shape[0], s.shape[1]), s.dtype) for s in shards]
    return Exchange(list(shards), outs, [DMA((4 * n,)), DMA((4 * n,)), DMA((n,))], start, finish, [(relay_early, relay)])


def gather_far_exchange(bufs, relay_early=0):
    n = len(bufs)
    R = [b.shape[0] // 4 for b in bufs]

    def send(src, dst, sm, e, j, x, y, c):
        to, of = _other_chips(x, y)[j], _other_chips(x, y)[1 - j]
        kk = 2 * of[0] + of[1]
        return _remote(_block_rows(src[e], R[e], kk, c, j), _block_rows(dst[e], R[e], kk, c, j),
                       sm[0].at[4 * e + j], sm[1].at[4 * e + j], (*to, c))

    def landed(dst, e, j, x, y, half):
        return _block_rows(dst[e], R[e], 2 * (1 - x) + (1 - y), half, j)

    def forward(dst, sm, e, j, x, y, c, sender_c):
        r = landed(dst, e, j, x, y, sender_c)
        return _remote(r, r, sm[0].at[4 * e + 2 + j], sm[1].at[4 * e + 2 + j], (x, y, 1 - c))

    def start(src, dst, sm):
        x, y, c = _place()
        for e in range(n):
            for j in range(2):
                send(src, dst, sm, e, j, x, y, c).start()

    def relay(src, dst, sm):
        x, y, c = _place()
        for e in range(n):
            for j in range(2):
                r = landed(dst, e, j, x, y, c)
                _remote(r, r, sm[0].at[4 * e + j], sm[1].at[4 * e + j], (*_other_chips(x, y)[j], c)).wait_recv()
                forward(dst, sm, e, j, x, y, c, c).start()

    def finish(src, dst, sm):
        x, y, c = _place()
        for e in range(n):
            for j in range(2):
                forward(dst, sm, e, j, x, y, c, 1 - c).wait_recv()
        for e in range(n):
            for j in range(2):
                send(src, dst, sm, e, j, x, y, c).wait_send()
                forward(dst, sm, e, j, x, y, c, c).wait_send()

    outs = [SDS(b.shape, b.dtype) for b in bufs]
    return Exchange(list(bufs), outs, [DMA((4 * n,)), DMA((4 * n,))], start, finish, [(relay_early, relay)],
                    {i: i for i in range(n)})


def gather_two_legs(shards):
    near = gather_near_exchange(shards)
    far = gather_far_exchange(near.outs)

    def finish(src, dst, sm):
        near.relays[0][1](src, dst, sm[:3])
        near.finish(src, dst, sm[:3])
        far.start(dst, dst, sm[3:])
        far.relays[0][1](dst, dst, sm[3:])
        far.finish(dst, dst, sm[3:])

    return Exchange(near.ins, near.outs, list(near.sems) + list(far.sems),
                    lambda src, dst, sm: near.start(src, dst, sm[:3]), finish)


def scatter_exchange(parts, relay_before_end=None, want_issue=False):
    n = len(parts)
    by_entry = relay_before_end is not None
    relay_before_end = relay_before_end or [0] * n

    def ici(p, st, sm, e, j, chip_j, x, y, c):
        k, kj = 2 * x + y, 2 * chip_j[0] + chip_j[1]
        return _remote(p[e].at[kj], st[e].at[c, k], sm[0].at[8 * e + j], sm[1].at[8 * e + j], (*chip_j, c))

    def own(p, st, sm, e, x, y, c):
        k = 2 * x + y
        return _remote(p[e].at[k], st[e].at[c, k], sm[0].at[8 * e + 3], sm[1].at[8 * e + 3], (x, y, 1 - c))

    def forward(st, sm, e, j, chip_j, x, y, c, sender_c):
        kj = 2 * chip_j[0] + chip_j[1]
        r = st[e].at[sender_c, kj]
        return _remote(r, r, sm[0].at[8 * e + 4 + j], sm[1].at[8 * e + 4 + j], (x, y, 1 - c))

    def local(p, st, sm, e, x, y, c):
        k = 2 * x + y
        return pltpu.make_async_copy(p[e].at[k], st[e].at[c, k], sm[2].at[e])

    def issue(e, p, st, sm):
        x, y, c = _place()
        for j, chip_j in enumerate(_other_chips(x, y)):
            ici(p, st, sm, e, j, chip_j, x, y, c).start()
        local(p, st, sm, e, x, y, c).start()
        own(p, st, sm, e, x, y, c).start()

    def start(p, st, sm, before_slot=None):
        x, y, c = _place()
        if by_entry:
            for e in range(n):
                issue(e, p, st, sm)
            return
        for j, chip_j in enumerate(_other_chips(x, y)):
            if before_slot is not None:
                before_slot(j, 2 * chip_j[0] + chip_j[1])
            for e in range(n):
                ici(p, st, sm, e, j, chip_j, x, y, c).start()
        if before_slot is not None:
            before_slot(3, 2 * x + y)
        for e in range(n):
            local(p, st, sm, e, x, y, c).start()
            own(p, st, sm, e, x, y, c).start()

    def relay(e, p, st, sm):
        x, y, c = _place()
        for j, chip_j in enumerate(_other_chips(x, y)):
            kj = 2 * chip_j[0] + chip_j[1]
            r = st[e].at[c, kj]
            _remote(r, r, sm[0].at[8 * e + j], sm[1].at[8 * e + j], (*chip_j, c)).wait_recv()
            forward(st, sm, e, j, chip_j, x, y, c, c).start()

    def finish(p, st, sm):
        x, y, c = _place()
        k = 2 * x + y
        chips = _other_chips(x, y)
        for e in range(n):
            r = st[e].at[1 - c, k]
            _remote(r, r, sm[0].at[8 * e + 3], sm[1].at[8 * e + 3], (x, y, 1 - c)).wait_recv()
            for j, chip_j in enumerate(chips):
                forward(st, sm, e, j, chip_j, x, y, c, 1 - c).wait_recv()
        for e in range(n):
            own(p, st, sm, e, x, y, c).wait_send()
            for j, chip_j in enumerate(chips):
                ici(p, st, sm, e, j, chip_j, x, y, c).wait_send()
                forward(st, sm, e, j, chip_j, x, y, c, c).wait_send()
            local(p, st, sm, e, x, y, c).wait()

    outs = [SDS((2,) + a.shape, a.dtype) for a in parts]
    ex = Exchange(list(parts), outs, [DMA((8 * n,)), DMA((8 * n,)), DMA((n,))], start, finish,
                  [(relay_before_end[e], functools.partial(relay, e)) for e in range(n)])
    return (ex, issue) if want_issue else ex


def reduce_scatter_exchange(grads, nsteps, load_step, send_step, relay_step):
    n = len(grads)
    hrs = [g.shape[2] for g in grads]
    C = grads[0].shape[3]
    scatter, issue = scatter_exchange([SDS((4,) + g.shape[2:], WIRE) for g in grads], want_issue=True)
    hand_on = [fn for _, fn in scatter.relays]

    def refs(xs):
        return xs[:3], xs[3], xs[4], xs[5], xs[6], xs[7:7 + n], xs[7 + n:]

    def push(e, g, psem, qsem, sib_st):
        x, y, c = _place()
        return _remote(g[e].at[:, 1 - c], sib_st[e], psem.at[e], qsem.at[e], (x, y, 1 - c))

    def fetch(e, g, lsem, own_st):
        _, _, c = _place()
        return pltpu.make_async_copy(g[e].at[:, c], own_st.at[e % 2, :, pl.ds(0, hrs[e])], lsem.at[e])

    def start(g, xo, xs):
        _, _, psem, qsem, _, sib_st, _ = refs(xs)
        for e in range(n):
            push(e, g, psem, qsem, sib_st).start()

    def load(e, g, xo, xs):
        _, lsem, _, _, own_st, _, _ = refs(xs)
        fetch(e, g, lsem, own_st).start()

    def send(e, g, xo, xs):
        sm, lsem, psem, qsem, own_st, sib_st, part = refs(xs)
        fetch(e, g, lsem, own_st).wait()
        push(e, g, psem, qsem, sib_st).wait_recv()
        part[e][...] = (own_st[e % 2, :, 0:hrs[e]] + sib_st[e][...]).astype(WIRE)
        issue(e, part, xo, sm)

    def relay(e, g, xo, xs):
        sm, _, _, _, _, _, part = refs(xs)
        hand_on[e](part, xo, sm)

    def finish(g, xo, xs):
        sm, _, psem, qsem, _, sib_st, part = refs(xs)
        scatter.finish(part, xo, sm)
        for e in range(n):
            push(e, g, psem, qsem, sib_st).wait_send()

    plan = sorted([(min(step[e], nsteps - 1), phase, e) for phase, step in enumerate((load_step, send_step, relay_step))
                   for e in range(n)])
    stage = (load, send, relay)
    relays = [(nsteps - 1 - at, functools.partial(stage[phase], e)) for at, phase, e in plan]
    scratch = (list(scatter.sems) + [DMA((n,)), DMA((n,)), DMA((n,))] + [pltpu.VMEM((2, 4, max(hrs), C), f32)]
               + [pltpu.VMEM((4, hr, C), f32) for hr in hrs] + [pltpu.VMEM((4, hr, C), WIRE) for hr in hrs])
    return Exchange(list(grads), scatter.outs, scratch, start, finish, relays)


def tail_reduce(d_norm_mix, d_norm_mem, d_norm_ffn, d_gains, d_cw8, d_cbias, d_qg, d_kg, d_mqg, d_mkg, d_sink8, loss8, tail):
    n = len(tail)
    scatter = scatter_exchange([SDS((4,) + a.shape[2:], WIRE) for a in tail])

    def half_copy(g, sib, hsem, e, j, slot, x, y, c):
        return _remote(g[e].at[slot, 1 - c], sib[e].at[slot], hsem[0].at[4 * e + j], hsem[1].at[4 * e + j], (x, y, 1 - c))

    def body(nm_ref, nmem_ref, nf_ref, gn_ref, cw_ref, cb_ref, qg_ref, kg_ref, mqg_ref, mkg_ref, sk_ref, ls_ref, *rest):
        g, o_ref, st = rest[:n], rest[n], rest[n + 1:2 * n + 1]
        buf, ssem, rsem = rest[2 * n + 1:2 * n + 4]
        own, sib, part = (rest[2 * n + 4 + i * n:2 * n + 4 + (i + 1) * n] for i in range(3))
        lsem = rest[5 * n + 4]
        hsem, xsem = rest[5 * n + 5:5 * n + 7], rest[5 * n + 7:]
        x, y, c = _place()
        loads = [pltpu.make_async_copy(g[e].at[:, c], own[e], lsem.at[e]) for e in range(n)]
        for ld in loads:
            ld.start()
        for j, slot in enumerate([2 * cx + cy for cx, cy in _other_chips(x, y)] + [2 * x + y]):
            for e in range(n):
                half_copy(g, sib, hsem, e, j, slot, x, y, c).start()
        me = 4 * x + 2 * y + c
        mine = buf.at[me]
        mine[...] = jnp.zeros((8, 1024), f32)
        mine[0:1, :] = nm_ref[...]
        mine[1:2, :] = nmem_ref[...]
        mine[2:3, :] = nf_ref[...]
        mine[3:4, :] = gn_ref[...]
        for j in range(3):
            mine[4:5, pl.ds(j * CONV_W, CONV_W)] = cw_ref[j:j + 1, :]
        mine[4:5, pl.ds(3 * CONV_W, CONV_W)] = cb_ref[...]
        for j, r in enumerate((qg_ref, kg_ref, mqg_ref, mkg_ref)):
            mine[5:6, pl.ds(j * HD, HD)] = r[...]
        mine[5:6, pl.ds(256, 128)] = sk_ref[0:1, :]
        mine[5:6, pl.ds(384, 128)] = ls_ref[0:1, :]

        def peer_of(m):
            return (1 - x if m & 4 else x, 1 - y if m & 2 else y, 1 - c if m & 1 else c)

        for m in range(1, 8):
            _remote(mine, mine, ssem.at[m - 1], rsem.at[m - 1], peer_of(m)).start()
        for ld in loads:
            ld.wait()

        def chip_partial(j, slot):
            for e in range(n):
                half_copy(g, sib, hsem, e, j, slot, x, y, c).wait()
                part[e][slot] = (own[e][slot] + sib[e][slot]).astype(WIRE)

        scatter.start(part, st, xsem, chip_partial)
        for _, hand_on in scatter.relays:
            hand_on(part, st, xsem)
        scatter.finish(part, st, xsem)
        for m in range(1, 8):
            p = peer_of(m)
            got = buf.at[4 * p[0] + 2 * p[1] + p[2]]
            _remote(got, got, ssem.at[m - 1], rsem.at[m - 1], p).wait_recv()
        for m in range(1, 8):
            _remote(mine, mine, ssem.at[m - 1], rsem.at[m - 1], peer_of(m)).wait_send()
        acc = buf[0]
        for d in range(1, 8):
            acc = acc + buf[d]
        o_ref[...] = acc

    ins = [d_norm_mix, d_norm_mem, d_norm_ffn, d_gains, d_cw8, d_cbias, d_qg, d_kg, d_mqg, d_mkg, d_sink8, loss8]
    half_shape = [(4,) + a.shape[2:] for a in tail]
    scratch = ([pltpu.VMEM((8, 8, 1024), f32), DMA((7,)), DMA((7,))]
               + [pltpu.VMEM(s, f32) for s in half_shape] * 2 + [pltpu.VMEM(s, WIRE) for s in half_shape]
               + [DMA((n,)), DMA((4 * n,)), DMA((4 * n,))] + list(scatter.sems))
    res = _run("tail_reduce", body, (), ins + list(tail), [VM] * len(ins) + [ANY] * n,
               [SDS((8, 1024), f32)] + list(scatter.outs), [VM] + [ANY] * n, scratch=scratch, vmem_mib=40)
    return res[0], res[1:]


def _adamw_math(w, g, m, v):
    m = ADAM_B1 * m + (1.0 - ADAM_B1) * g
    v = ADAM_B2 * v + (1.0 - ADAM_B2) * (g * g)
    m_hat = m / (1.0 - ADAM_B1 ** ADAM_STEP)
    v_hat = v / (1.0 - ADAM_B2 ** ADAM_STEP)
    delta = -ADAM_LR * (m_hat / (jnp.sqrt(v_hat) + ADAM_EPS) + ADAM_WD * w)
    return delta, m, v


def _sum_chips(st):
    return ((st[0].astype(f32) + st[1].astype(f32)) + st[2].astype(f32)) + st[3].astype(f32)


def adamw_big(name, stages, ws, ms, vs, nstep, exchange=None):
    n = len(stages)

    def body(*refs):
        st, w, m, v = refs[:n], refs[n:2 * n], refs[2 * n:3 * n], refs[3 * n:4 * n]
        outs = refs[4 * n:]
        for e in range(n):
            g = jnp.concatenate([_sum_chips(st[e].at[0]), _sum_chips(st[e].at[1])], axis=0)
            d, mm, vv = _adamw_math(w[e][...], g, m[e][...], v[e][...])
            outs[4 * e][...] = g
            outs[4 * e + 1][...] = d
            outs[4 * e + 2][...] = mm
            outs[4 * e + 3][...] = vv

    st_specs, w_specs = [], []
    for e in range(n):
        _, _, hr, C = stages[e].shape
        st_specs.append(pl.BlockSpec((2, 4, hr, C // nstep), lambda i: (0, 0, 0, i)))
        w_specs.append(pl.BlockSpec((2 * hr, C // nstep), lambda i: (0, i)))
    out_specs = [s for s in w_specs for _ in range(4)]
    out_shape = [SDS(w.shape, f32) for w in ws for _ in range(4)]
    res = _run(name, body, (nstep,), list(stages) + list(ws) + list(ms) + list(vs), st_specs + w_specs * 3,
               out_shape, out_specs, vmem_mib=16, exchange=exchange)
    res, sent = res if exchange is not None else (res, None)
    return [res[4 * e:4 * e + 4] for e in range(n)], sent


def adamw_small(tot, pk_w, pk_m, pk_v, shapes):
    def body(tot_ref, w_ref, m_ref, v_ref, *outs):
        x, y, _ = _place()
        chip = 2 * x + y
        taps = []
        for j in range(3):
            mine = tot_ref[4:5, j * CONV_W:j * CONV_W + HD]
            for s in range(1, 4):
                mine = jnp.where(chip == s, tot_ref[4:5, j * CONV_W + s * HD:j * CONV_W + (s + 1) * HD], mine)
            taps.append(mine)
        row4 = jnp.concatenate(taps + [jnp.zeros((1, 3 * CONV_W - 3 * HD), f32), tot_ref[4:5, 3 * CONV_W:]], axis=1)
        tot_v = tot_ref[...]
        row = lax.broadcasted_iota(jnp.int32, tot_v.shape, 0)
        g = jnp.where(row == 4, jnp.broadcast_to(row4, tot_v.shape), tot_v)
        d, mm, vv = _adamw_math(w_ref[...], g, m_ref[...], v_ref[...])
        for i, name in enumerate(SMALL):
            for k, val in enumerate((g, d, mm, vv)):
                if name == "conv_w":
                    outs[4 * i + k][...] = jnp.concatenate([val[4:5, j * HD:(j + 1) * HD] for j in range(3)], axis=0)[None]
                else:
                    r, c0, w = SMALL_AT[name]
                    outs[4 * i + k][...] = val[r:r + 1, c0:c0 + w]

    out_shape = [SDS(shapes[k], f32) for k in SMALL for _ in range(4)]
    res = _run("adamw_small", body, (), [tot, pk_w, pk_m, pk_v], [VM] * 4, out_shape, [VM] * len(out_shape))
    return {k: res[4 * i:4 * i + 4] for i, k in enumerate(SMALL)}


def prep_weights(name, shards, exchange=None):
    n = len(shards)

    def body(*refs):
        for e in range(n):
            refs[n + e][...] = _c(refs[e][...])

    return _run(name, body, (), shards, [VM] * n, [SDS(a.shape, MXU) for a in shards], [VM] * n, vmem_mib=16, exchange=exchange)


def mem_kv_fwd(mem2d, pk, wmkv):
    M, D = mem2d.shape

    def body(m_ref, pk_ref, w_ref, mn_ref, kv_ref, km_ref, vm_ref):
        m = m_ref[...]
        mn = _c(m * _rstd(m) * _small(pk_ref, "norm_mem"))
        mn_ref[...] = mn
        kv = _nn(mn, w_ref[...])
        kv_ref[...] = kv
        kk = kv[:, :MEM_W]
        km_ref[...] = _c(kk * _heads_rstd(kk) * _lanes(_small(pk_ref, "mem_k_norm"), MEM_W))
        vm_ref[...] = _c(kv[:, MEM_W:])

    return _run("mem_kv_fwd", body, (), [mem2d, pk, wmkv], [VM] * 3,
                [SDS((M, D), MXU), SDS((M, 2 * MEM_W), f32), SDS((M, MEM_W), MXU), SDS((M, MEM_W), MXU)], [VM] * 4)


QKV_W = ATT_W + 2 * KV_W + MEM_W


def in_proj_fwd(x2d, pk, winT, tm, exchange):
    T, D = x2d.shape
    P = winT.shape[0]

    def body(x_ref, pk_ref, w_ref, xn_ref, proj_ref, qkv_ref):
        xv = x_ref[...]
        xn = _c(xv * _rstd(xv) * _small(pk_ref, "norm_mix"))
        xn_ref[...] = xn
        proj = _nt(xn, w_ref[...])
        proj_ref[...] = proj
        q, k = proj[:, :ATT_W], proj[:, ATT_W:ATT_W + KV_W]
        qm = proj[:, P - MEM_W:]
        qkv_ref[...] = jnp.concatenate(
            [_c(q * _heads_rstd(q) * _lanes(_small(pk_ref, "q_norm"), ATT_W)),
             _c(k * _heads_rstd(k) * _lanes(_small(pk_ref, "k_norm"), KV_W)),
             _c(proj[:, ATT_W + KV_W:ATT_W + 2 * KV_W]),
             _c(qm * _heads_rstd(qm) * _lanes(_small(pk_ref, "mem_q_norm"), MEM_W))], axis=1)

    return _run("in_proj_fwd", body, (T // tm,), [x2d, pk, winT],
                [pl.BlockSpec((tm, D), lambda i: (i, 0)), VM, VM],
                [SDS((T, D), MXU), SDS((T, P), f32), SDS((T, QKV_W), MXU)],
                [pl.BlockSpec((tm, D), lambda i: (i, 0)), pl.BlockSpec((tm, P), lambda i: (i, 0)),
                 pl.BlockSpec((tm, QKV_W), lambda i: (i, 0))],
                vmem_mib=40, exchange=exchange)


def _swa_bias_table():
    r = np.arange(GQA * BLK)[:, None]
    k = np.arange(2 * BLK)[None, :]
    dist = (r % BLK) + BLK - k
    band = (dist >= 0) & (dist < BLK)
    tab = np.empty((2, N_KV, GQA * BLK, 2 * BLK), np.float32)
    for later in range(2):
        valid = band & ((k >= BLK) | (later == 1))
        for g in range(N_KV):
            slope = 2.0 ** -(g * GQA + r // BLK + 1.0)
            tab[later, g] = np.where(valid, -slope * dist, NEG)
    return jnp.asarray(tab)


def _sink_column(g, sk_ref):
    hrow = lax.broadcasted_iota(jnp.int32, (GQA * BLK, 1), 0) // BLK
    sink = jnp.zeros((GQA * BLK, 1), f32)
    for hh in range(GQA):
        sink = jnp.where(hrow == hh, sk_ref[g * GQA + hh:g * GQA + hh + 1, 0:1], sink)
    return sink


def _stack_heads(v, g):
    return jnp.concatenate([v[:, (g * GQA + hh) * HD:(g * GQA + hh + 1) * HD] for hh in range(GQA)], axis=0)


def attn_fwd(qkv, sink_rows, BL, S, exchange, qb=2):
    NS = S // (qb * BLK)
    T = BL * S

    def body(q_ref, kc_ref, kp_ref, vc_ref, vp_ref, sk_ref, tab_ref, o_ref):
        j = pl.program_id(1)
        kall = jnp.concatenate([kp_ref[...], kc_ref[...]], axis=0)
        vall = jnp.concatenate([vp_ref[...], vc_ref[...]], axis=0)
        ones = jnp.ones((2 * BLK, HD), MXU)
        for b in range(qb):
            q = q_ref[pl.ds(b * BLK, BLK), :]
            k2, v2 = kall[b * BLK:(b + 2) * BLK], vall[b * BLK:(b + 2) * BLK]
            later = jnp.minimum(j, 1) if b == 0 else 1
            for g in range(N_KV):
                kn, vh = k2[:, g * HD:(g + 1) * HD], v2[:, g * HD:(g + 1) * HD]
                s = _nt(_stack_heads(q, g), kn) * (HD ** -0.5) + tab_ref[later, g]
                e, es = _exp_scores(s, _sink_column(g, sk_ref))
                eb = _c(e)
                o = _nn(eb, vh) * (1.0 / (_nn(eb, ones) + es))
                for hh in range(GQA):
                    o_ref[pl.ds(b * BLK, BLK), pl.ds((g * GQA + hh) * HD, HD)] = o[hh * BLK:(hh + 1) * BLK]

    cur = lambda col: (lambda b, j: (b * NS + j, col))
    prev = lambda col: (lambda b, j: (qb * (b * NS + j) - jnp.minimum(j, 1), col))
    return _run("attn_fwd", body, (BL, NS), [qkv, qkv, qkv, qkv, qkv, sink_rows, _swa_bias_table()],
                [pl.BlockSpec((qb * BLK, ATT_W), cur(0)),
                 pl.BlockSpec((qb * BLK, KV_W), cur(4)), pl.BlockSpec((BLK, KV_W), prev(4)),
                 pl.BlockSpec((qb * BLK, KV_W), cur(5)), pl.BlockSpec((BLK, KV_W), prev(5)),
                 pl.BlockSpec((8, 128), lambda b, j: (0, 0)), VM],
                [SDS((T, ATT_W), f32)], [pl.BlockSpec((qb * BLK, ATT_W), cur(0))], exchange=exchange)


def _conv_taps(u, uh):
    row = lax.broadcasted_iota(jnp.int32, u.shape, 0)
    u1 = jnp.where(row == 0, uh[7:8, :], pltpu.roll(u, 1, 0))
    u2 = jnp.where(row == 0, uh[6:7, :], jnp.where(row == 1, uh[7:8, :], pltpu.roll(u, 2, 0)))
    return u1, u2


def _mem_head(qm, km, vm, h):
    qh, kh, vh = (a[:, h * HD:(h + 1) * HD] for a in (qm, km, vm))
    e, _ = _exp_scores(_nt(qh, kh) * (HD ** -0.5))
    return qh, kh, vh, e


def mixer_tail_fwd(x2d, attn_out, proj, qkv, km, vm, conv_w8, pk, wout, S, tm, exchange):
    T, D = x2d.shape
    NM = km.shape[0] // (T // S)

    def body(x_ref, ao_ref, ch_ref, cb_ref, cc_ref, chh_ref, cch_ref, qm_ref, km_ref, vm_ref, cw_ref, pk_ref,
             wout_ref, co_ref, mo_ref, mg_ref, x1_ref, h_ref):
        first = (pl.program_id(0) * tm) % S == 0
        u = cc_ref[...] * ch_ref[...]
        uh = jnp.where(first, 0.0, cch_ref[...] * chh_ref[...])
        u1, u2 = _conv_taps(u, uh)
        conv = cw_ref[0:1, :] * u2 + cw_ref[1:2, :] * u1 + cw_ref[2:3, :] * u + _small(pk_ref, "conv_b")
        conv_out = cb_ref[...] * conv
        co_ref[...] = conv_out
        qm, kmv, vmv = qm_ref[...], km_ref[...], vm_ref[...]
        ones = jnp.ones((NM, HD), MXU)
        for h in range(N_MEMH):
            _, _, vh, e = _mem_head(qm, kmv, vmv, h)
            eb = _c(e)
            mo_ref[:, pl.ds(h * HD, HD)] = _nn(eb, vh) * (1.0 / _nn(eb, ones))
        mem_out = mo_ref[...]
        ao = ao_ref[...]
        merged = _c(jnp.concatenate([ao * _rstd(ao) * _small(pk_ref, "out_norm_attn"),
                                     conv_out * _rstd(conv_out) * _small(pk_ref, "out_norm_conv"),
                                     mem_out * _rstd(mem_out) * _small(pk_ref, "out_norm_mem")], axis=1))
        mg_ref[...] = merged
        x1 = x_ref[...] + _nn(merged, wout_ref[...])
        x1_ref[...] = x1
        h_ref[...] = _c(x1 * _rstd(x1) * _small(pk_ref, "norm_ffn"))

    tile = lambda w, col: pl.BlockSpec((tm, w), lambda i: (i, col))
    halo = lambda col: pl.BlockSpec((8, CONV_W), lambda i: (jnp.maximum(i * (tm // 8) - 1, 0), col))
    seq = pl.BlockSpec((NM, MEM_W), lambda i: ((i * tm) // S, 0))
    small = lambda a: pl.BlockSpec(a.shape, lambda i: (0, 0))
    return _run("mixer_tail_fwd", body, (T // tm,),
                [x2d, attn_out, proj, proj, proj, proj, proj, qkv, km, vm, conv_w8, pk, wout],
                [tile(D, 0), tile(ATT_W, 0), tile(CONV_W, 3), tile(CONV_W, 4), tile(CONV_W, 5), halo(3), halo(5),
                 tile(MEM_W, 3), seq, seq, VM, VM, VM],
                [SDS((T, CONV_W), f32), SDS((T, MEM_W), f32), SDS((T, D), MXU), SDS((T, D), f32), SDS((T, D), MXU)],
                [tile(CONV_W, 0), tile(MEM_W, 0), tile(D, 0), tile(D, 0), tile(D, 0)], vmem_mib=40, exchange=exchange)


def ffn_fwd_bwd(h, x1, tgt, wgT, wuT, wd, pk, tm):
    T, D = x1.shape
    F = wd.shape[0]

    def body(h_ref, x1_ref, t_ref, wg_ref, wu_ref, wd_ref, pk_ref,
             dx1_ref, dx2_ref, act_ref, dg_ref, du_ref, loss_ref, dgf_ref):
        @pl.when(pl.program_id(0) == 0)
        def _():
            loss_ref[...] = jnp.zeros_like(loss_ref)
            dgf_ref[...] = jnp.zeros_like(dgf_ref)

        hv = h_ref[...]
        gate = _nt(hv, wg_ref[...])
        up = _nt(hv, wu_ref[...])
        sg = jax.nn.sigmoid(gate)
        sl = gate * sg
        act = _c(sl * up)
        act_ref[...] = act
        x1v = x1_ref[...]
        diff = (x1v + _nn(act, wd_ref[...])) - t_ref[...]
        loss_ref[...] += 0.5 * jnp.sum(jnp.sum(diff * diff, axis=-1, keepdims=True) / D, axis=0, keepdims=True)
        dx2 = diff / D
        dx2b = _c(dx2)
        dx2_ref[...] = dx2b
        d_act = _nt(dx2b, wd_ref[...])
        d_up = _c(d_act * sl)
        d_gate = _c(d_act * up * (sg * (1.0 + gate * (1.0 - sg))))
        du_ref[...] = d_up
        dg_ref[...] = d_gate
        dh = _nn(d_gate, wg_ref[...]) + _nn(d_up, wu_ref[...])
        dv, dgf = _norm_bwd(dh, x1v, _rstd(x1v), _small(pk_ref, "norm_ffn"))
        dx1_ref[...] = dx2 + dv
        dgf_ref[...] += dgf

    tile = lambda w: pl.BlockSpec((tm, w), lambda i: (i, 0))
    return _run("ffn_fwd_bwd", body, (T // tm,), [h, x1, tgt, wgT, wuT, wd, pk],
                [tile(D), tile(D), tile(D), VM, VM, VM, VM],
                [SDS((T, D), f32), SDS((T, D), MXU), SDS((T, F), MXU), SDS((T, F), MXU), SDS((T, F), MXU),
                 SDS((8, 128), f32), SDS((1, D), f32)],
                [tile(D), tile(D), tile(F), tile(F), tile(F), pl.BlockSpec((8, 128), lambda i: (0, 0)),
                 pl.BlockSpec((1, D), lambda i: (0, 0))], vmem_mib=56)


def matmul_tn(a, b, name, tmo, tk):
    T, M = a.shape
    N = b.shape[1]

    def body(a_ref, b_ref, o_ref):
        @pl.when(pl.program_id(1) == 0)
        def _():
            o_ref[...] = jnp.zeros_like(o_ref)

        o_ref[...] += _tn(a_ref[...], b_ref[...])

    return _run(name, body, (M // tmo, T // tk), [a, b],
                [pl.BlockSpec((tk, tmo), lambda m, k: (k, m)), pl.BlockSpec((tk, N), lambda m, k: (k, 0))],
                [SDS((M, N), f32)], [pl.BlockSpec((tmo, N), lambda m, k: (m, 0))], vmem_mib=48)[0]


def out_proj_bwd(dx1, merged, attn_out, conv_out, mem_out, pk, wout, tm):
    T, D = dx1.shape

    def body(dx1_ref, mg_ref, ao_ref, co_ref, mo_ref, pk_ref, w_ref,
             dao_ref, dco_ref, dmo_ref, dw_ref, dgain_ref):
        @pl.when(pl.program_id(0) == 0)
        def _():
            dw_ref[...] = jnp.zeros_like(dw_ref)
            dgain_ref[...] = jnp.zeros_like(dgain_ref)

        dxb = _c(dx1_ref[...])
        dw_ref[...] += _tn(mg_ref[...], dxb)
        dmg = _nt(dxb, w_ref[...])
        ao, co, mo = ao_ref[...], co_ref[...], mo_ref[...]
        da, ga = _norm_bwd(dmg[:, :ATT_W], ao, _rstd(ao), _small(pk_ref, "out_norm_attn"))
        dc, gc = _norm_bwd(dmg[:, ATT_W:ATT_W + CONV_W], co, _rstd(co), _small(pk_ref, "out_norm_conv"))
        dm, gm = _norm_bwd(dmg[:, ATT_W + CONV_W:], mo, _rstd(mo), _small(pk_ref, "out_norm_mem"))
        dao_ref[...] = da
        dco_ref[...] = dc
        dmo_ref[...] = dm
        dgain_ref[...] += jnp.concatenate([ga, gc, gm], axis=1)

    tile = lambda w: pl.BlockSpec((tm, w), lambda i: (i, 0))
    return _run("out_proj_bwd", body, (T // tm,), [dx1, merged, attn_out, conv_out, mem_out, pk, wout],
                [tile(D), tile(D), tile(ATT_W), tile(CONV_W), tile(MEM_W), VM, VM],
                [SDS((T, ATT_W), f32), SDS((T, CONV_W), f32), SDS((T, MEM_W), f32), SDS((D, D), f32), SDS((1, D), f32)],
                [tile(ATT_W), tile(CONV_W), tile(MEM_W), pl.BlockSpec((D, D), lambda i: (0, 0)),
                 pl.BlockSpec((1, D), lambda i: (0, 0))], vmem_mib=40)


def attn_bwd(qkv, d_attn, attn_out, sink_rows, BL, S, exchange):
    NB = S // BLK
    T = BL * S

    def body(q_ref, kc_ref, kp_ref, vc_ref, vp_ref, do_ref, ao_ref, sk_ref, tab_ref,
             dq_ref, dk_ref, dv_ref, dsk_ref, pend_k, pend_v):
        b, j = pl.program_id(0), pl.program_id(1)

        @pl.when((b == 0) & (j == 0))
        def _():
            dsk_ref[...] = jnp.zeros_like(dsk_ref)

        @pl.when(j == 0)
        def _():
            pend_k[...] = jnp.zeros_like(pend_k)
            pend_v[...] = jnp.zeros_like(pend_v)

        @pl.when(j < NB)
        def _():
            q, do, ao = q_ref[...], do_ref[...], ao_ref[...]
            k2 = jnp.concatenate([kp_ref[...], kc_ref[...]], axis=0)
            v2 = jnp.concatenate([vp_ref[...], vc_ref[...]], axis=0)
            lane = lax.broadcasted_iota(jnp.int32, (8, 128), 1)
            ones_w = jnp.ones((2 * BLK, 2 * BLK), MXU)
            dsk = jnp.zeros((8, 128), f32)
            dks, dvs = [], []
            for g in range(N_KV):
                kn, vh = k2[:, g * HD:(g + 1) * HD], v2[:, g * HD:(g + 1) * HD]
                qs = _stack_heads(q, g)
                s = _nt(qs, kn) * (HD ** -0.5) + tab_ref[g]
                e, es = _exp_scores(s, _sink_column(g, sk_ref))
                eb = _c(e)
                inv_w = 1.0 / (_nn(eb, ones_w) + es)
                inv_n = inv_w[:, :HD]
                dos = _stack_heads(do, g)
                delta = _rowsum_mxu(dos * _stack_heads(ao, g), 2 * BLK)
                dp = _nt(_c(dos), vh)
                ds = _c(e * inv_w * (dp - delta) * (HD ** -0.5))
                t = es * inv_n[:, 0:1] * delta[:, 0:1]
                for hh in range(GQA):
                    dsk = dsk + jnp.where(lane == g * GQA + hh, -jnp.sum(t[hh * BLK:(hh + 1) * BLK]), 0.0)
                dvs.append(_tn(eb, _c(dos * inv_n)))
                dks.append(_tn(ds, qs))
                dqs = _nn(ds, kn)
                for hh in range(GQA):
                    dq_ref[:, pl.ds((g * GQA + hh) * HD, HD)] = dqs[hh * BLK:(hh + 1) * BLK]
            dk2 = jnp.concatenate(dks, axis=1)
            dv2 = jnp.concatenate(dvs, axis=1)
            dk_ref[...] = pend_k[...] + dk2[:BLK]
            dv_ref[...] = pend_v[...] + dv2[:BLK]
            pend_k[...] = dk2[BLK:]
            pend_v[...] = dv2[BLK:]
            dsk_ref[...] += dsk

        @pl.when(j == NB)
        def _():
            dk_ref[...] = pend_k[...]
            dv_ref[...] = pend_v[...]

    cur = lambda col: (lambda b, j: (b * NB + jnp.minimum(j, NB - 1), col))
    prev = lambda col: (lambda b, j: (b * NB + jnp.maximum(j - 1, 0), col))
    small = lambda shape: pl.BlockSpec(shape, lambda b, j: (0, 0))
    return _run("attn_bwd", body, (BL, NB + 1), [qkv, qkv, qkv, qkv, qkv, d_attn, attn_out, sink_rows, _swa_bias_table()],
                [pl.BlockSpec((BLK, ATT_W), cur(0)),
                 pl.BlockSpec((BLK, KV_W), cur(4)), pl.BlockSpec((BLK, KV_W), prev(4)),
                 pl.BlockSpec((BLK, KV_W), cur(5)), pl.BlockSpec((BLK, KV_W), prev(5)),
                 pl.BlockSpec((BLK, ATT_W), cur(0)), pl.BlockSpec((BLK, ATT_W), cur(0)), small((8, 128)),
                 pl.BlockSpec((None, N_KV, GQA * BLK, 2 * BLK), lambda b, j: (jnp.minimum(j, 1), 0, 0, 0))],
                [SDS((T, ATT_W), f32), SDS((T, KV_W), f32), SDS((T, KV_W), f32), SDS((8, 128), f32)],
                [pl.BlockSpec((BLK, ATT_W), cur(0)), pl.BlockSpec((BLK, KV_W), prev(0)),
                 pl.BlockSpec((BLK, KV_W), prev(0)), small((8, 128))],
                scratch=[pltpu.VMEM((BLK, KV_W), f32)] * 2, vmem_mib=56, exchange=exchange)


def mem_conv_bwd(d_mem_out, mem_out, d_conv_out, proj, qkv, km, vm, conv_w8, pk, S, tm, exchange):
    T = d_mem_out.shape[0]
    NM = km.shape[0] // (T // S)

    def body(dmo_ref, mo_ref, dco_ref, ch_ref, cb_ref, cc_ref, chh_ref, cch_ref, qm_ref, km_ref, vm_ref, cw_ref,
             pk_ref, dqm_ref, dkm_ref, dvm_ref, dcb_ref, dcv_ref, dcw_ref, dcbias_ref):
        i = pl.program_id(0)
        first = (i * tm) % S == 0

        @pl.when(i == 0)
        def _():
            dcw_ref[...] = jnp.zeros_like(dcw_ref)
            dcbias_ref[...] = jnp.zeros_like(dcbias_ref)

        @pl.when(first)
        def _():
            dkm_ref[...] = jnp.zeros_like(dkm_ref)
            dvm_ref[...] = jnp.zeros_like(dvm_ref)

        qm, kmv, vmv, dmo, mo = qm_ref[...], km_ref[...], vm_ref[...], dmo_ref[...], mo_ref[...]
        ones_w = jnp.ones((NM, NM), MXU)
        for h in range(N_MEMH):
            qh, kh, vh, e = _mem_head(qm, kmv, vmv, h)
            eb = _c(e)
            doh = dmo[:, h * HD:(h + 1) * HD]
            delta = _rowsum_mxu(doh * mo[:, h * HD:(h + 1) * HD], NM)
            dp = _nt(_c(doh), vh)
            inv_w = 1.0 / _nn(eb, ones_w)
            ds = _c(e * inv_w * (dp - delta) * (HD ** -0.5))
            dvm_ref[:, pl.ds(h * HD, HD)] += _tn(eb, _c(doh * inv_w[:, :HD]))
            dkm_ref[:, pl.ds(h * HD, HD)] += _tn(ds, qh)
            dqm_ref[:, pl.ds(h * HD, HD)] = _nn(ds, kh)

        u = cc_ref[...] * ch_ref[...]
        uh = jnp.where(first, 0.0, cch_ref[...] * chh_ref[...])
        u1, u2 = _conv_taps(u, uh)
        conv = cw_ref[0:1, :] * u2 + cw_ref[1:2, :] * u1 + cw_ref[2:3, :] * u + _small(pk_ref, "conv_b")
        dy = dco_ref[...]
        dcb_ref[...] = dy * conv
        dcv = dy * cb_ref[...]
        dcv_ref[...] = dcv
        dcbias_ref[...] += jnp.sum(dcv, axis=0, keepdims=True)
        dcw_ref[0:1, :] += jnp.sum(dcv * u2, axis=0, keepdims=True)
        dcw_ref[1:2, :] += jnp.sum(dcv * u1, axis=0, keepdims=True)
        dcw_ref[2:3, :] += jnp.sum(dcv * u, axis=0, keepdims=True)

    tile = lambda w, col: pl.BlockSpec((tm, w), lambda i: (i, col))
    halo = lambda col: pl.BlockSpec((8, CONV_W), lambda i: (jnp.maximum(i * (tm // 8) - 1, 0), col))
    seq = pl.BlockSpec((NM, MEM_W), lambda i: ((i * tm) // S, 0))
    const = lambda shape: pl.BlockSpec(shape, lambda i: (0, 0))
    return _run("mem_conv_bwd", body, (T // tm,),
                [d_mem_out, mem_out, d_conv_out, proj, proj, proj, proj, proj, qkv, km, vm, conv_w8, pk],
                [tile(MEM_W, 0), tile(MEM_W, 0), tile(CONV_W, 0), tile(CONV_W, 3), tile(CONV_W, 4), tile(CONV_W, 5),
                 halo(3), halo(5), tile(MEM_W, 3), seq, seq, VM, VM],
                [SDS((T, MEM_W), f32), SDS(km.shape, f32), SDS(km.shape, f32),
                 SDS((T, CONV_W), f32), SDS((T, CONV_W), f32), SDS((8, CONV_W), f32), SDS((1, CONV_W), f32)],
                [tile(MEM_W, 0), seq, seq, tile(CONV_W, 0), tile(CONV_W, 0), const((8, CONV_W)), const((1, CONV_W))],
                vmem_mib=48, exchange=exchange)


def in_proj_bwd(dqn, dkn, dv, dcb, dcv, dqmn, proj, conv_w8, xn, x2d, dx1, pk, winT, S, tm, stages, ws, ms, vs):
    T, D = x2d.shape
    P = winT.shape[0]
    last_blk = T // 8 - 1
    n = len(stages)
    nsteps = T // tm
    tile_w = ws[0].shape[1] // (nsteps // 2)
    turn = [e * 2 // n for e in range(n)]

    def body(dq_ref, dk_ref, dv_ref, dcb_ref, dcv_ref, dcvn_ref, dqm_ref, qa_ref, ka_ref, ch_ref, cc_ref, qma_ref,
             cw_ref, xn_ref, x_ref, dx1_ref, pk_ref, w_ref, *rest):
        st, aw, am, av = (rest[k * n:(k + 1) * n] for k in range(4))
        dx_ref, dw_ref, dg_ref, dqg_ref, dkg_ref, dmqg_ref = rest[4 * n:4 * n + 6]
        aouts = rest[4 * n + 6:]
        i = pl.program_id(0)

        for parity in range(2):
            @pl.when(i % 2 == parity)
            def _(parity=parity):
                for e in range(n):
                    if turn[e] == parity:
                        g = jnp.concatenate([_sum_chips(st[e].at[0]), _sum_chips(st[e].at[1])], axis=0)
                        d, mm, vv = _adamw_math(aw[e][...], g, am[e][...], av[e][...])
                        for k, val in enumerate((g, d, mm, vv)):
                            aouts[4 * e + k][...] = val

        @pl.when(i == 0)
        def _():
            dw_ref[...] = jnp.zeros_like(dw_ref)
            dg_ref[...] = jnp.zeros_like(dg_ref)
            dqg_ref[...] = jnp.zeros_like(dqg_ref)
            dkg_ref[...] = jnp.zeros_like(dkg_ref)
            dmqg_ref[...] = jnp.zeros_like(dmqg_ref)

        dqa, gq = _heads_norm_bwd(dq_ref[...], qa_ref[...], _small(pk_ref, "q_norm"))
        dka, gk = _heads_norm_bwd(dk_ref[...], ka_ref[...], _small(pk_ref, "k_norm"))
        dqma, gmq = _heads_norm_bwd(dqm_ref[...], qma_ref[...], _small(pk_ref, "mem_q_norm"))
        dqg_ref[...] += gq
        dkg_ref[...] += gk
        dmqg_ref[...] += gmq

        last = ((i + 1) * tm) % S == 0
        dcv = dcv_ref[...]
        nxt = jnp.where(last, 0.0, dcvn_ref[...])
        row = lax.broadcasted_iota(jnp.int32, dcv.shape, 0)
        n1 = jnp.where(row == tm - 1, nxt[0:1, :], pltpu.roll(dcv, tm - 1, 0))
        n2 = jnp.where(row == tm - 2, nxt[0:1, :], jnp.where(row == tm - 1, nxt[1:2, :], pltpu.roll(dcv, tm - 2, 0)))
        du = cw_ref[2:3, :] * dcv + cw_ref[1:2, :] * n1 + cw_ref[0:1, :] * n2
        d_proj = jnp.concatenate([_c(dqa), _c(dka), _c(dv_ref[...]), _c(du * cc_ref[...]),
                                  _c(dcb_ref[...]), _c(du * ch_ref[...]), _c(dqma)], axis=1)
        dw_ref[...] += _tn(d_proj, xn_ref[...])
        xv = x_ref[...]
        dv_, dg = _norm_bwd(_nn(d_proj, w_ref[...]), xv, _rstd(xv), _small(pk_ref, "norm_mix"))
        dx_ref[...] = dx1_ref[...] + dv_
        dg_ref[...] += dg

    tile = lambda w, col=0: pl.BlockSpec((tm, w), lambda i: (i, col))
    nhalo = pl.BlockSpec((8, CONV_W), lambda i: (jnp.minimum((i + 1) * (tm // 8), last_blk), 0))
    const = lambda shape: pl.BlockSpec(shape, lambda i: (0, 0))
    st_specs = [pl.BlockSpec((2, 4, s.shape[2], tile_w), lambda i: (0, 0, 0, i // 2)) for s in stages]
    w_specs = [pl.BlockSpec((w.shape[0], tile_w), lambda i: (0, i // 2)) for w in ws]
    res = _run("in_proj_bwd", body, (nsteps,),
               [dqn, dkn, dv, dcb, dcv, dcv, dqmn, proj, proj, proj, proj, proj, conv_w8, xn, x2d, dx1, pk, winT]
               + list(stages) + list(ws) + list(ms) + list(vs),
               [tile(ATT_W), tile(KV_W), tile(KV_W), tile(CONV_W), tile(CONV_W), nhalo, tile(MEM_W),
                tile(ATT_W, 0), tile(KV_W, 4), tile(CONV_W, 3), tile(CONV_W, 5), tile(MEM_W, 6), VM,
                tile(D), tile(D), tile(D), VM, VM] + st_specs + w_specs * 3,
               [SDS((T, D), f32), SDS((P, D), f32), SDS((1, D), f32), SDS((1, HD), f32), SDS((1, HD), f32),
                SDS((1, HD), f32)] + [SDS(w.shape, f32) for w in ws for _ in range(4)],
               [tile(D), pl.BlockSpec((P, D), lambda i: (0, 0)), const((1, D)), const((1, HD)), const((1, HD)),
                const((1, HD))] + [s for s in w_specs for _ in range(4)],
               vmem_mib=56)
    return res[:6], [res[6 + 4 * e:10 + 4 * e] for e in range(n)]


def mem_kv_bwd(dkm, dvm, kv, memn, mem2d, pk, wmkv):
    def body(dkm_ref, dvm_ref, kv_ref, mn_ref, m_ref, pk_ref, w_ref, dw_ref, dg_ref, dkg_ref):
        dkk, dkg = _heads_norm_bwd(dkm_ref[...], kv_ref[:, :MEM_W], _small(pk_ref, "mem_k_norm"))
        dkg_ref[...] = dkg
        dkv = _c(jnp.concatenate([dkk, dvm_ref[...]], axis=1))
        dw_ref[...] = _tn(mn_ref[...], dkv)
        mv = m_ref[...]
        dg_ref[...] = jnp.sum(_nt(dkv, w_ref[...]) * mv * _rstd(mv), axis=0, keepdims=True)

    return _run("mem_kv_bwd", body, (), [dkm, dvm, kv, memn, mem2d, pk, wmkv], [VM] * 7,
                [SDS(wmkv.shape, f32), SDS((1, mem2d.shape[1]), f32), SDS((1, HD), f32)], [VM] * 3, vmem_mib=40)


def _halves_view(g):
    return g.reshape(4, 2, g.shape[0] // 8, g.shape[1])


def kernel(x, mem, norm_mix, w_in, q_norm, k_norm, attn_sinks, conv_w, conv_b, norm_mem, w_mem_kv, mem_q_norm, mem_k_norm, out_norm_attn, out_norm_conv, out_norm_mem, w_out, norm_ffn, w_gate, w_up, w_down, loss_target, m_norm_mix, m_w_in, m_q_norm, m_k_norm, m_attn_sinks, m_conv_w, m_conv_b, m_norm_mem, m_w_mem_kv, m_mem_q_norm, m_mem_k_norm, m_out_norm_attn, m_out_norm_conv, m_out_norm_mem, m_w_out, m_norm_ffn, m_w_gate, m_w_up, m_w_down, v_norm_mix, v_w_in, v_q_norm, v_k_norm, v_attn_sinks, v_conv_w, v_conv_b, v_norm_mem, v_w_mem_kv, v_mem_q_norm, v_mem_k_norm, v_out_norm_attn, v_out_norm_conv, v_out_norm_mem, v_w_out, v_norm_ffn, v_w_gate, v_w_up, v_w_down):
    BL, S, D = x.shape
    T = BL * S
    TM = 256
    TM_BIG = min(512, S)
    w_small = dict(norm_mix=norm_mix, norm_mem=norm_mem, norm_ffn=norm_ffn, out_norm_attn=out_norm_attn,
                   out_norm_conv=out_norm_conv, out_norm_mem=out_norm_mem, conv_w=conv_w, conv_b=conv_b, q_norm=q_norm,
                   k_norm=k_norm, mem_q_norm=mem_q_norm, mem_k_norm=mem_k_norm, attn_sinks=attn_sinks)
    m_small = dict(norm_mix=m_norm_mix, norm_mem=m_norm_mem, norm_ffn=m_norm_ffn, out_norm_attn=m_out_norm_attn,
                   out_norm_conv=m_out_norm_conv, out_norm_mem=m_out_norm_mem, conv_w=m_conv_w, conv_b=m_conv_b,
                   q_norm=m_q_norm, k_norm=m_k_norm, mem_q_norm=m_mem_q_norm, mem_k_norm=m_mem_k_norm,
                   attn_sinks=m_attn_sinks)
    v_small = dict(norm_mix=v_norm_mix, norm_mem=v_norm_mem, norm_ffn=v_norm_ffn, out_norm_attn=v_out_norm_attn,
                   out_norm_conv=v_out_norm_conv, out_norm_mem=v_out_norm_mem, conv_w=v_conv_w, conv_b=v_conv_b,
                   q_norm=v_q_norm, k_norm=v_k_norm, mem_q_norm=v_mem_q_norm, mem_k_norm=v_mem_k_norm,
                   attn_sinks=v_attn_sinks)
    pk = _pack_small(w_small)

    rowblocks = lambda a, b, c, d, e, f: [a[0].T, b[0].T, c[0].T, d[0], e[0], f[0]]
    w_rb = rowblocks(w_in, w_gate, w_up, w_down, w_out, w_mem_kv)
    m_rb = rowblocks(m_w_in, m_w_gate, m_w_up, m_w_down, m_w_out, m_w_mem_kv)
    v_rb = rowblocks(v_w_in, v_w_gate, v_w_up, v_w_down, v_w_out, v_w_mem_kv)
    (winT_s,) = prep_weights("prep_w_in", w_rb[:1])
    cw_pad = jnp.zeros((8, 128), f32).at[:3, :HD].set(conv_w[0])
    (wgT_s, wuT_s, wd_s, wout_s, wmkv_s), (winT, cw_all) = prep_weights(
        "gather_w_in", w_rb[1:], _together([gather_two_legs([winT_s]), gather_exchange([cw_pad], [False])]))
    conv_w_full = jnp.transpose(cw_all.reshape(4, 8, 128)[:, :3, :HD], (1, 0, 2)).reshape(3, CONV_W)
    conv_w8 = jnp.zeros((8, CONV_W), f32).at[:3].set(conv_w_full)
    sink_rows = jnp.broadcast_to(attn_sinks.reshape(N_Q, 1), (N_Q, 128))

    x2d = x.reshape(T, D)
    mem2d = mem.reshape(-1, D)
    (xn, proj, qkv), near1 = in_proj_fwd(x2d, pk, winT, TM_BIG, gather_near_exchange([wgT_s, wout_s, wmkv_s], relay_early=1))
    (attn_out,), (wgT, wout, wmkv, *near2) = attn_fwd(
        qkv, sink_rows, BL, S, _together([gather_far_exchange(near1, relay_early=2), gather_near_exchange([wuT_s, wd_s], relay_early=2)]))
    memn, kv, km, vm = mem_kv_fwd(mem2d, pk, wmkv)
    (conv_out, mem_out, merged, x1, h), (wuT, wd) = mixer_tail_fwd(
        x2d, attn_out, proj, qkv, km, vm, conv_w8, pk, wout, S, TM_BIG, gather_far_exchange(near2, relay_early=2))

    dx1, dx2b, act, d_gate, d_up, loss8, d_norm_ffn = ffn_fwd_bwd(h, x1, loss_target.reshape(T, D), wgT, wuT, wd, pk, TM)
    F = wd.shape[0]
    g_wd = matmul_tn(act, dx2b, "dw_down", F // 2, min(T, 2048))
    g_wgT = matmul_tn(d_gate, h, "dw_gate", F // 2, min(T, 2048))
    g_wuT = matmul_tn(d_up, h, "dw_up", F // 2, min(T, 2048))

    d_attn, d_conv_out, d_mem_out, g_wout, d_gains = out_proj_bwd(dx1, merged, attn_out, conv_out, mem_out, pk, wout, TM_BIG)
    dqmn, dkm, dvm, dcb, dcv, d_cw8, d_cbias = mem_conv_bwd(
        d_mem_out, mem_out, d_conv_out, proj, qkv, km, vm, conv_w8, pk, S, min(1024, S), None)
    (dqn, dkn, dv, d_sink8), (st_wout, st_wgT, st_wuT, st_wd) = attn_bwd(
        qkv, d_attn, attn_out, sink_rows, BL, S,
        reduce_scatter_exchange([_halves_view(g) for g in (g_wout, g_wgT, g_wuT, g_wd)], BL * (S // BLK + 1),
                                load_step=[0, 1, 4, 7], send_step=[1, 4, 7, 10], relay_step=[6, 16, 25, 33]))
    (g_x, g_winT, d_norm_mix, d_qg, d_kg, d_mqg), late_res = in_proj_bwd(
        dqn, dkn, dv, dcb, dcv, dqmn, proj, conv_w8, xn, x2d, dx1, pk, winT, S, TM,
        [st_wgT, st_wuT, st_wd, st_wout], w_rb[1:5], m_rb[1:5], v_rb[1:5])
    g_wmkv, d_norm_mem, d_mkg = mem_kv_bwd(dkm, dvm, kv, memn, mem2d, pk, wmkv)

    tot, tail_stage = tail_reduce(d_norm_mix, d_norm_mem, d_norm_ffn, d_gains, d_cw8, d_cbias, d_qg, d_kg, d_mqg, d_mkg,
                                  d_sink8, loss8, [_halves_view(g) for g in (g_winT, g_wmkv)])
    loss = tot[5, 384]
    tail_res, _ = adamw_big("adamw_tail", tail_stage, [w_rb[0], w_rb[5]], [m_rb[0], m_rb[5]], [v_rb[0], v_rb[5]], 4)
    res = {"w_in": [a.T[None] for a in tail_res[0]], "w_gate": [a.T[None] for a in late_res[0]],
           "w_up": [a.T[None] for a in late_res[1]], "w_down": [a[None] for a in late_res[2]],
           "w_out": [a[None] for a in late_res[3]], "w_mem_kv": [a[None] for a in tail_res[1]]}
    res.update(adamw_small(tot, pk, _pack_small(m_small), _pack_small(v_small), {k: w_small[k].shape for k in SMALL}))

    order = ["norm_mix", "w_in", "q_norm", "k_norm", "attn_sinks", "conv_w", "conv_b", "norm_mem", "w_mem_kv",
             "mem_q_norm", "mem_k_norm", "out_norm_attn", "out_norm_conv", "out_norm_mem", "w_out", "norm_ffn",
             "w_gate", "w_up", "w_down"]
    return (loss, g_x.reshape(BL, S, D), *[res[n][0] for n in order], *[res[n][1] for n in order],
            *[res[n][2] for n in order], *[res[n][3] for n in order])
```
